```python
import jax, jax.numpy as jnp
from jax import lax
import numpy as np

D_MODEL = 1024
BATCH = 8
SEQ = 2048
DEPTH = 1

CHUNK = 64
RET_HEADS = 8
RET_DQK = 64
RET_DV = 128
SB_HEADS = 8
SB_DH = 64
SB_BLOCK = 128
D_FF = 4 * D_MODEL
ROPE_BASE = 10000.0
EPS = 1e-6

RET_QK = RET_HEADS * RET_DQK
RET_V = RET_HEADS * RET_DV
SB_W = SB_HEADS * SB_DH
IN_SPLITS = (RET_QK, RET_QK, RET_V, RET_V, SB_W, SB_W, SB_W, D_MODEL, D_MODEL)
D_IN = 2 * RET_QK + 2 * RET_V + 3 * SB_W + 2 * D_MODEL

kernel_name = "hybrid_retention_stickbreaking_block"


def rmsnorm(x, g):
    xf = x.astype(jnp.float32)
    y = xf * lax.rsqrt(jnp.mean(xf * xf, axis=-1, keepdims=True) + EPS)
    return y * g


def modulate(h, shift, scale):
    return h * (1.0 + scale[:, None, :]) + shift[:, None, :]


def rotary(x, pos):
    d = x.shape[-1]
    inv_freq = ROPE_BASE ** (-jnp.arange(0, d, 2, dtype=jnp.float32) / d)
    ang = pos.astype(jnp.float32)[..., None] * inv_freq
    cos = jnp.cos(ang)[:, :, None, :]
    sin = jnp.sin(ang)[:, :, None, :]
    x1, x2 = x[..., : d // 2], x[..., d // 2:]
    return jnp.concatenate([x1 * cos - x2 * sin, x1 * sin + x2 * cos], axis=-1)


def retention(q, k, v, pos):
    B, S = q.shape[0], q.shape[1]
    nc = S // CHUNK
    q = rotary(q.astype(jnp.float32), pos)
    k = rotary(k.astype(jnp.float32), pos) * (RET_DQK ** -0.5)
    v = v.astype(jnp.float32)
    log_gamma = jnp.log1p(-(2.0 ** (-5.0 - jnp.arange(RET_HEADS, dtype=jnp.float32))))
    qc = q.reshape(B, nc, CHUNK, RET_HEADS, RET_DQK)
    kc = k.reshape(B, nc, CHUNK, RET_HEADS, RET_DQK)
    vc = v.reshape(B, nc, CHUNK, RET_HEADS, RET_DV)
    idx = jnp.arange(CHUNK, dtype=jnp.float32)
    intra_decay = jnp.exp(jnp.abs(idx[:, None] - idx[None, :])[None] * log_gamma[:, None, None])
    scores = jnp.einsum('bnihd,bnjhd->bnhij', qc, kc) * intra_decay
    intra = jnp.einsum('bnhij,bnjhe->bnihe', scores, vc)
    k_to_end = jnp.exp((CHUNK - 1.0 - idx)[:, None] * log_gamma[None, :])
    kv = jnp.einsum('bnjhd,jh,bnjhe->nbhde', kc, k_to_end, vc)
    chunk_decay = jnp.exp(CHUNK * log_gamma)[None, :, None, None]

    def step(state, kv_n):
        return state * chunk_decay + kv_n, state

    _, s_prev = lax.scan(step, jnp.zeros(kv.shape[1:], jnp.float32), kv)
    q_from_start = jnp.exp((idx + 1.0)[:, None] * log_gamma[None, :])
    cross = jnp.einsum('bnihd,ih,nbhde->bnihe', qc, q_from_start, s_prev)
    return (intra + cross).reshape(B, S, RET_HEADS, RET_DV)


def stick_breaking(q, k, v):
    S = q.shape[1]
    scale = SB_DH ** -0.5
    q = q.astype(jnp.float32)
    k = k.astype(jnp.float32)
    v = v.astype(jnp.float32)
    outs = []
    for blk in range(S // SB_BLOCK):
        t0 = blk * SB_BLOCK
        t1 = t0 + SB_BLOCK
        z = jnp.einsum('bthd,bshd->bhts', q[:, t0:t1], k[:, :t1]) * scale
        t_idx = t0 + jnp.arange(SB_BLOCK)
        s_idx = jnp.arange(t1)
        valid = s_idx[None, :] < t_idx[:, None]
        log_1m = jnp.where(valid, -jax.nn.softplus(z), 0.0)
        log_stick = lax.cumsum(log_1m, axis=3, reverse=True) - log_1m
        a = jnp.where(valid, jnp.exp(jax.nn.log_sigmoid(z) + log_stick), 0.0)
        outs.append(jnp.einsum('bhts,bshd->bthd', a, v[:, :t1]))
    return jnp.concatenate(outs, axis=1)


def _fwd_setup_inputs(seed: int = 0) -> dict:
    key = jax.random.key(seed)
    ks = jax.random.split(key, 18)
    nrm = jax.random.normal
    f32 = jnp.float32
    x = nrm(ks[0], (BATCH, SEQ, D_MODEL), f32)
    c = nrm(ks[1], (BATCH, D_MODEL), f32)
    offset = jax.random.randint(ks[2], (BATCH, 1), 0, 256, dtype=jnp.int32) * CHUNK
    positions = (offset + jnp.arange(SEQ, dtype=jnp.int32)[None, :]).astype(jnp.int32)
    ada_w = nrm(ks[3], (DEPTH, D_MODEL, 6 * D_MODEL), f32) * D_MODEL ** -0.5
    ada_b = 0.01 * nrm(ks[4], (DEPTH, 6 * D_MODEL), f32)
    pre_mix_g = 1.0 + 0.02 * nrm(ks[5], (DEPTH, D_MODEL), f32)
    post_mix_g = 1.0 + 0.02 * nrm(ks[6], (DEPTH, D_MODEL), f32)
    pre_ffn_g = 1.0 + 0.02 * nrm(ks[7], (DEPTH, D_MODEL), f32)
    post_ffn_g = 1.0 + 0.02 * nrm(ks[8], (DEPTH, D_MODEL), f32)
    w_in = nrm(ks[9], (DEPTH, D_MODEL, D_IN), f32) * D_MODEL ** -0.5
    ret_gn_g = 1.0 + 0.02 * nrm(ks[10], (DEPTH, RET_V), f32)
    w_ret_branch = nrm(ks[11], (DEPTH, RET_V, D_MODEL), f32) * RET_V ** -0.5
    w_sb_branch = nrm(ks[12], (DEPTH, SB_W, D_MODEL), f32) * SB_W ** -0.5
    w_out = nrm(ks[13], (DEPTH, D_MODEL, D_MODEL), f32) * D_MODEL ** -0.5
    w_ff1 = nrm(ks[14], (DEPTH, D_MODEL, D_FF), f32) * D_MODEL ** -0.5
    w_ff2 = nrm(ks[15], (DEPTH, D_FF, D_MODEL), f32) * D_FF ** -0.5
    return {"x": x, "c": c, "positions": positions, "ada_w": ada_w, "ada_b": ada_b,
            "pre_mix_g": pre_mix_g, "post_mix_g": post_mix_g, "pre_ffn_g": pre_ffn_g,
            "post_ffn_g": post_ffn_g, "w_in": w_in, "ret_gn_g": ret_gn_g,
            "w_ret_branch": w_ret_branch, "w_sb_branch": w_sb_branch, "w_out": w_out,
            "w_ff1": w_ff1, "w_ff2": w_ff2}


def _fwd_reference(x, c, positions, ada_w, ada_b, pre_mix_g, post_mix_g, pre_ffn_g, post_ffn_g,
              w_in, ret_gn_g, w_ret_branch, w_sb_branch, w_out, w_ff1, w_ff2):
    out_dtype = x.dtype
    B, S, _ = x.shape
    split_pts = [int(p) for p in np.cumsum(IN_SPLITS)[:-1]]
    h_res = x.astype(jnp.float32)
    for l in range(DEPTH):
        mod = jax.nn.silu(c.astype(jnp.float32)) @ ada_w[l] + ada_b[l]
        sh1, sc1, gt1, sh2, sc2, gt2 = jnp.split(mod, 6, axis=-1)

        h = modulate(rmsnorm(h_res, pre_mix_g[l]), sh1, sc1)
        proj = h @ w_in[l]
        q_r, k_r, v_r, g_r, q_s, k_s, v_s, a_r, a_s = jnp.split(proj, split_pts, axis=-1)

        ret = retention(q_r.reshape(B, S, RET_HEADS, RET_DQK),
                        k_r.reshape(B, S, RET_HEADS, RET_DQK),
                        v_r.reshape(B, S, RET_HEADS, RET_DV), positions)
        mu = jnp.mean(ret, axis=-1, keepdims=True)
        var = jnp.mean(jnp.square(ret - mu), axis=-1, keepdims=True)
        ret = (ret - mu) * lax.rsqrt(var + EPS) * ret_gn_g[l].reshape(RET_HEADS, RET_DV)
        ret = jax.nn.silu(g_r) * ret.reshape(B, S, RET_V)

        sb = stick_breaking(q_s.reshape(B, S, SB_HEADS, SB_DH),
                            k_s.reshape(B, S, SB_HEADS, SB_DH),
                            v_s.reshape(B, S, SB_HEADS, SB_DH)).reshape(B, S, SB_W)

        mixed = (jax.nn.sigmoid(a_r) * (ret @ w_ret_branch[l])
                 + jax.nn.sigmoid(a_s) * (sb @ w_sb_branch[l]))
        y = mixed @ w_out[l]
        h_res = h_res + gt1[:, None, :] * rmsnorm(y, post_mix_g[l])

        h2 = modulate(rmsnorm(h_res, pre_ffn_g[l]), sh2, sc2)
        f = jnp.square(jax.nn.relu(h2 @ w_ff1[l])) @ w_ff2[l]
        h_res = h_res + gt2[:, None, :] * rmsnorm(f, post_ffn_g[l])
    return h_res.astype(out_dtype)


import jax as _jax
import jax.numpy as _jnp

TWIN_FORMAT = 'train_step'
FWD_PARAMS = ['x', 'c', 'positions', 'ada_w', 'ada_b', 'pre_mix_g', 'post_mix_g', 'pre_ffn_g', 'post_ffn_g', 'w_in', 'ret_gn_g', 'w_ret_branch', 'w_sb_branch', 'w_out', 'w_ff1', 'w_ff2']
TWIN_WEIGHTS = ['ada_w', 'ada_b', 'pre_mix_g', 'post_mix_g', 'pre_ffn_g', 'post_ffn_g', 'w_in', 'ret_gn_g', 'w_ret_branch', 'w_sb_branch', 'w_out', 'w_ff1', 'w_ff2']
TWIN_DIFF_INPUT = 'x'
TWIN_INPUTS = ['x', 'c', 'positions', 'ada_w', 'ada_b', 'pre_mix_g', 'post_mix_g', 'pre_ffn_g', 'post_ffn_g', 'w_in', 'ret_gn_g', 'w_ret_branch', 'w_sb_branch', 'w_out', 'w_ff1', 'w_ff2', 'loss_target', 'm_ada_w', 'm_ada_b', 'm_pre_mix_g', 'm_post_mix_g', 'm_pre_ffn_g', 'm_post_ffn_g', 'm_w_in', 'm_ret_gn_g', 'm_w_ret_branch', 'm_w_sb_branch', 'm_w_out', 'm_w_ff1', 'm_w_ff2', 'v_ada_w', 'v_ada_b', 'v_pre_mix_g', 'v_post_mix_g', 'v_pre_ffn_g', 'v_post_ffn_g', 'v_w_in', 'v_ret_gn_g', 'v_w_ret_branch', 'v_w_sb_branch', 'v_w_out', 'v_w_ff1', 'v_w_ff2']
TWIN_OUTPUTS = ['loss', 'grad_x', 'grad_ada_w', 'grad_ada_b', 'grad_pre_mix_g', 'grad_post_mix_g', 'grad_pre_ffn_g', 'grad_post_ffn_g', 'grad_w_in', 'grad_ret_gn_g', 'grad_w_ret_branch', 'grad_w_sb_branch', 'grad_w_out', 'grad_w_ff1', 'grad_w_ff2', 'delta_ada_w', 'delta_ada_b', 'delta_pre_mix_g', 'delta_post_mix_g', 'delta_pre_ffn_g', 'delta_post_ffn_g', 'delta_w_in', 'delta_ret_gn_g', 'delta_w_ret_branch', 'delta_w_sb_branch', 'delta_w_out', 'delta_w_ff1', 'delta_w_ff2', 'new_m_ada_w', 'new_m_ada_b', 'new_m_pre_mix_g', 'new_m_post_mix_g', 'new_m_pre_ffn_g', 'new_m_post_ffn_g', 'new_m_w_in', 'new_m_ret_gn_g', 'new_m_w_ret_branch', 'new_m_w_sb_branch', 'new_m_w_out', 'new_m_w_ff1', 'new_m_w_ff2', 'new_v_ada_w', 'new_v_ada_b', 'new_v_pre_mix_g', 'new_v_post_mix_g', 'new_v_pre_ffn_g', 'new_v_post_ffn_g', 'new_v_w_in', 'new_v_ret_gn_g', 'new_v_w_ret_branch', 'new_v_w_sb_branch', 'new_v_w_out', 'new_v_w_ff1', 'new_v_w_ff2']
TWIN_LEAF_KINDS = {'loss': 'loss', 'grad_x': 'grad_x', 'grad_ada_w': 'grad_w', 'grad_ada_b': 'grad_w', 'grad_pre_mix_g': 'grad_w', 'grad_post_mix_g': 'grad_w', 'grad_pre_ffn_g': 'grad_w', 'grad_post_ffn_g': 'grad_w', 'grad_w_in': 'grad_w', 'grad_ret_gn_g': 'grad_w', 'grad_w_ret_branch': 'grad_w', 'grad_w_sb_branch': 'grad_w', 'grad_w_out': 'grad_w', 'grad_w_ff1': 'grad_w', 'grad_w_ff2': 'grad_w', 'delta_ada_w': 'delta_w', 'delta_ada_b': 'delta_w', 'delta_pre_mix_g': 'delta_w', 'delta_post_mix_g': 'delta_w', 'delta_pre_ffn_g': 'delta_w', 'delta_post_ffn_g': 'delta_w', 'delta_w_in': 'delta_w', 'delta_ret_gn_g': 'delta_w', 'delta_w_ret_branch': 'delta_w', 'delta_w_sb_branch': 'delta_w', 'delta_w_out': 'delta_w', 'delta_w_ff1': 'delta_w', 'delta_w_ff2': 'delta_w', 'new_m_ada_w': 'new_m', 'new_m_ada_b': 'new_m', 'new_m_pre_mix_g': 'new_m', 'new_m_post_mix_g': 'new_m', 'new_m_pre_ffn_g': 'new_m', 'new_m_post_ffn_g': 'new_m', 'new_m_w_in': 'new_m', 'new_m_ret_gn_g': 'new_m', 'new_m_w_ret_branch': 'new_m', 'new_m_w_sb_branch': 'new_m', 'new_m_w_out': 'new_m', 'new_m_w_ff1': 'new_m', 'new_m_w_ff2': 'new_m', 'new_v_ada_w': 'new_v', 'new_v_ada_b': 'new_v', 'new_v_pre_mix_g': 'new_v', 'new_v_post_mix_g': 'new_v', 'new_v_pre_ffn_g': 'new_v', 'new_v_post_ffn_g': 'new_v', 'new_v_w_in': 'new_v', 'new_v_ret_gn_g': 'new_v', 'new_v_w_ret_branch': 'new_v', 'new_v_w_sb_branch': 'new_v', 'new_v_w_out': 'new_v', 'new_v_w_ff1': 'new_v', 'new_v_w_ff2': 'new_v'}


def _forward(args):
    return _fwd_reference(*[args[k] for k in FWD_PARAMS])


def _output_shape():
    out = _jax.eval_shape(lambda: _forward(_fwd_setup_inputs(0)))
    return out.shape, out.dtype

N_MICROBATCH = 1
ADAM_LR = 0.001
ADAM_B1 = 0.9
ADAM_B2 = 0.999
ADAM_EPS = 1e-08
ADAM_WD = 0.01
ADAM_STEP = 10
PER_EXAMPLE_BATCH_AXIS = {'x': 0, 'c': 0, 'positions': 0, 'loss_target': 0}
SHARED_INPUTS = []
_WEIGHT_DTYPES = {'ada_w': _jnp.float32, 'ada_b': _jnp.float32, 'pre_mix_g': _jnp.float32, 'post_mix_g': _jnp.float32, 'pre_ffn_g': _jnp.float32, 'post_ffn_g': _jnp.float32, 'w_in': _jnp.float32, 'ret_gn_g': _jnp.float32, 'w_ret_branch': _jnp.float32, 'w_sb_branch': _jnp.float32, 'w_out': _jnp.float32, 'w_ff1': _jnp.float32, 'w_ff2': _jnp.float32}
MOMENT_SCALE = {'ada_w': 2.103186e+00, 'ada_b': 3.826716e+00, 'pre_mix_g': 2.851627e-01, 'post_mix_g': 7.923355e+00, 'pre_ffn_g': 4.429431e-01, 'post_ffn_g': 8.190813e+00, 'w_in': 4.843591e-01, 'ret_gn_g': 3.159073e-01, 'w_ret_branch': 2.763161e-01, 'w_sb_branch': 1.122424e+00, 'w_out': 1.167774e+00, 'w_ff1': 5.452032e-01, 'w_ff2': 1.565118e+00}


def _to_microbatches(a, axis):
    t = _jnp.moveaxis(a, axis, 0)
    t = t.reshape((N_MICROBATCH, t.shape[0] // N_MICROBATCH) + t.shape[1:])
    return _jnp.moveaxis(t, 1, axis + 1)


def setup_inputs(seed: int = 0) -> dict:
    inp = _fwd_setup_inputs(seed)
    key = _jax.random.fold_in(_jax.random.key(seed), 7919)
    shape, _ = _output_shape()
    out = dict(inp)
    out["loss_target"] = _jax.random.normal(_jax.random.fold_in(key, 0), shape, _jnp.float32)
    for i, name in enumerate(TWIN_WEIGHTS):
        w = inp[name].astype(_jnp.float32)
        if MOMENT_SCALE is None:
            s = _jnp.sqrt(_jnp.mean(_jnp.square(w)) + 1e-30)
        else:
            s = MOMENT_SCALE[name]
        km, kv = _jax.random.split(_jax.random.fold_in(key, i + 1))
        out[name] = w
        out["m_" + name] = s * _jax.random.normal(km, w.shape, _jnp.float32)
        out["v_" + name] = (s * s) * _jax.random.uniform(kv, w.shape, _jnp.float32, 0.5, 1.5)
    if N_MICROBATCH > 1:
        for name, axis in PER_EXAMPLE_BATCH_AXIS.items():
            out[name] = _to_microbatches(out[name], axis)
    return {'x': out['x'], 'c': out['c'], 'positions': out['positions'], 'ada_w': out['ada_w'], 'ada_b': out['ada_b'], 'pre_mix_g': out['pre_mix_g'], 'post_mix_g': out['post_mix_g'], 'pre_ffn_g': out['pre_ffn_g'], 'post_ffn_g': out['post_ffn_g'], 'w_in': out['w_in'], 'ret_gn_g': out['ret_gn_g'], 'w_ret_branch': out['w_ret_branch'], 'w_sb_branch': out['w_sb_branch'], 'w_out': out['w_out'], 'w_ff1': out['w_ff1'], 'w_ff2': out['w_ff2'], 'loss_target': out['loss_target'], 'm_ada_w': out['m_ada_w'], 'm_ada_b': out['m_ada_b'], 'm_pre_mix_g': out['m_pre_mix_g'], 'm_post_mix_g': out['m_post_mix_g'], 'm_pre_ffn_g': out['m_pre_ffn_g'], 'm_post_ffn_g': out['m_post_ffn_g'], 'm_w_in': out['m_w_in'], 'm_ret_gn_g': out['m_ret_gn_g'], 'm_w_ret_branch': out['m_w_ret_branch'], 'm_w_sb_branch': out['m_w_sb_branch'], 'm_w_out': out['m_w_out'], 'm_w_ff1': out['m_w_ff1'], 'm_w_ff2': out['m_w_ff2'], 'v_ada_w': out['v_ada_w'], 'v_ada_b': out['v_ada_b'], 'v_pre_mix_g': out['v_pre_mix_g'], 'v_post_mix_g': out['v_post_mix_g'], 'v_pre_ffn_g': out['v_pre_ffn_g'], 'v_post_ffn_g': out['v_post_ffn_g'], 'v_w_in': out['v_w_in'], 'v_ret_gn_g': out['v_ret_gn_g'], 'v_w_ret_branch': out['v_w_ret_branch'], 'v_w_sb_branch': out['v_w_sb_branch'], 'v_w_out': out['v_w_out'], 'v_w_ff1': out['v_w_ff1'], 'v_w_ff2': out['v_w_ff2']}


def _loss(weights, diff, rest, loss_target):
    with _jax.named_scope("forward"):
        args = {**rest, TWIN_DIFF_INPUT: diff, **{k: w.astype(_WEIGHT_DTYPES[k]) for k, w in weights.items()}}
        y = _forward(args)
    with _jax.named_scope("loss_head"):
        err = _jnp.square(y.astype(_jnp.float32) - loss_target)
        return 0.5 * _jnp.sum(_jnp.mean(err, axis=-1)) if err.ndim else 0.5 * err


def _adamw(w, g, m, v):
    m = ADAM_B1 * m + (1.0 - ADAM_B1) * g
    v = ADAM_B2 * v + (1.0 - ADAM_B2) * _jnp.square(g)
    m_hat = m / (1.0 - ADAM_B1 ** ADAM_STEP)
    v_hat = v / (1.0 - ADAM_B2 ** ADAM_STEP)
    delta = -ADAM_LR * (m_hat / (_jnp.sqrt(v_hat) + ADAM_EPS) + ADAM_WD * w)
    return delta, m, v


def reference(x, c, positions, ada_w, ada_b, pre_mix_g, post_mix_g, pre_ffn_g, post_ffn_g, w_in, ret_gn_g, w_ret_branch, w_sb_branch, w_out, w_ff1, w_ff2, loss_target, m_ada_w, m_ada_b, m_pre_mix_g, m_post_mix_g, m_pre_ffn_g, m_post_ffn_g, m_w_in, m_ret_gn_g, m_w_ret_branch, m_w_sb_branch, m_w_out, m_w_ff1, m_w_ff2, v_ada_w, v_ada_b, v_pre_mix_g, v_post_mix_g, v_pre_ffn_g, v_post_ffn_g, v_w_in, v_ret_gn_g, v_w_ret_branch, v_w_sb_branch, v_w_out, v_w_ff1, v_w_ff2):
    given = dict(x=x, c=c, positions=positions, ada_w=ada_w, ada_b=ada_b, pre_mix_g=pre_mix_g, post_mix_g=post_mix_g, pre_ffn_g=pre_ffn_g, post_ffn_g=post_ffn_g, w_in=w_in, ret_gn_g=ret_gn_g, w_ret_branch=w_ret_branch, w_sb_branch=w_sb_branch, w_out=w_out, w_ff1=w_ff1, w_ff2=w_ff2, loss_target=loss_target, m_ada_w=m_ada_w, m_ada_b=m_ada_b, m_pre_mix_g=m_pre_mix_g, m_post_mix_g=m_post_mix_g, m_pre_ffn_g=m_pre_ffn_g, m_post_ffn_g=m_post_ffn_g, m_w_in=m_w_in, m_ret_gn_g=m_ret_gn_g, m_w_ret_branch=m_w_ret_branch, m_w_sb_branch=m_w_sb_branch, m_w_out=m_w_out, m_w_ff1=m_w_ff1, m_w_ff2=m_w_ff2, v_ada_w=v_ada_w, v_ada_b=v_ada_b, v_pre_mix_g=v_pre_mix_g, v_post_mix_g=v_post_mix_g, v_pre_ffn_g=v_pre_ffn_g, v_post_ffn_g=v_post_ffn_g, v_w_in=v_w_in, v_ret_gn_g=v_ret_gn_g, v_w_ret_branch=v_w_ret_branch, v_w_sb_branch=v_w_sb_branch, v_w_out=v_w_out, v_w_ff1=v_w_ff1, v_w_ff2=v_w_ff2)
    weights = {n: given[n] for n in TWIN_WEIGHTS}
    shared = {n: given[n] for n in SHARED_INPUTS}
    per_example = {n: given[n] for n in ['x', 'c', 'positions']}
    grad_fn = _jax.value_and_grad(_loss, argnums=(0, 1))

    def one_microbatch(ex, loss_target):
        ex = dict(ex)
        diff = ex.pop(TWIN_DIFF_INPUT)
        return grad_fn(weights, diff, {**shared, **ex}, loss_target)

    if N_MICROBATCH == 1:
        loss, (grad_w, grad_x) = one_microbatch(per_example, given["loss_target"])
    else:
        def body(carry, xs):
            loss_sum, grad_sum = carry
            l_k, (gw_k, gx_k) = one_microbatch(xs[0], xs[1])
            with _jax.named_scope("update"):
                return (loss_sum + l_k, _jax.tree.map(_jnp.add, grad_sum, gw_k)), gx_k

        init = (_jnp.zeros((), _jnp.float32), _jax.tree.map(_jnp.zeros_like, weights))
        (loss, grad_w), grad_x = _jax.lax.scan(body, init, (per_example, given["loss_target"]))
    with _jax.named_scope("update"):
        delta_w, new_m, new_v = {}, {}, {}
        for n in TWIN_WEIGHTS:
            delta_w[n], new_m[n], new_v[n] = _adamw(weights[n], grad_w[n], given["m_" + n], given["v_" + n])
    return (loss, grad_x, *[grad_w[n] for n in TWIN_WEIGHTS], *[delta_w[n] for n in TWIN_WEIGHTS],
            *[new_m[n] for n in TWIN_WEIGHTS], *[new_v[n] for n in TWIN_WEIGHTS])
```

```python
import functools

import numpy as np
import jax
import jax.numpy as jnp
from jax import lax
from jax.experimental import pallas as pl
from jax.experimental.pallas import tpu as pltpu

F32 = jnp.float32
BF16 = jnp.bfloat16
N_DEV = 8
AXES = ("x", "y", "c")

EPS = 1e-6
CHUNK = 64
CHUNK_SHIFT = 6
HEADS = 8
RET_DQK = 64
RET_DV = 128
SB_DH = 64
RET_QK = HEADS * RET_DQK
RET_V = HEADS * RET_DV
SB_W = HEADS * SB_DH
ROPE_BASE = 10000.0
LANES = 128

ADAM_LR = 0.001
ADAM_B1 = 0.9
ADAM_B2 = 0.999
ADAM_EPS = 1e-08
ADAM_WD = 0.01
ADAM_STEP = 10

VMEM_LIMIT = 56 * 1024 * 1024

_NN = (((1,), (0,)), ((), ()))
_NT = (((1,), (1,)), ((), ()))
_TN = (((0,), (0,)), ((), ()))


def _dot(a, b, dims=_NN):
    if a.dtype != BF16:
        a = a.astype(BF16)
    if b.dtype != BF16:
        b = b.astype(BF16)
    return lax.dot_general(a, b, dims, preferred_element_type=F32)


def _dot_split(a, b):
    hi = a.astype(BF16)
    lo = (a - hi.astype(F32)).astype(BF16)
    return (lax.dot_general(hi, b, _NN, preferred_element_type=F32)
            + lax.dot_general(lo, b, _NN, preferred_element_type=F32))


def _sigmoid(x):
    return 1.0 / (1.0 + jnp.exp(-x))


def _rms(x, d):
    r = lax.rsqrt(jnp.sum(x * x, axis=1, keepdims=True) * (1.0 / d) + EPS)
    return x * r, r


def _rms_bwd(dn, n, r, d):
    return r * (dn - n * (jnp.sum(dn * n, axis=1, keepdims=True) * (1.0 / d)))


def _colsum(v):
    return jnp.sum(v, axis=0, keepdims=True)


def _accum(ref, val, first):
    @pl.when(first)
    def _():
        ref[...] = val

    @pl.when(jnp.logical_not(first))
    def _():
        ref[...] += val


def _call(name, body, grid, ins, outs, scratch=()):
    res = pl.pallas_call(
        functools.partial(body),
        name=name,
        grid=grid,
        in_specs=[pl.BlockSpec(bs, im) for _, bs, im in ins],
        out_specs=[pl.BlockSpec(bs, im) for _, _, bs, im in outs],
        out_shape=[jax.ShapeDtypeStruct(s, d) for s, d, _, _ in outs],
        scratch_shapes=list(scratch),
        compiler_params=pltpu.CompilerParams(
            dimension_semantics=("arbitrary",) * len(grid), vmem_limit_bytes=VMEM_LIMIT),
    )(*[a for a, _, _ in ins])
    return res


def _exchange(name, arrays, scatter):
    n = len(arrays)
    n_peer = N_DEV - 1

    def body(*refs):
        ins, outs = refs[:n], refs[n:2 * n]
        send_sems, recv_sems, local_sems = refs[2 * n:]
        x, y, c = (lax.axis_index(a) for a in AXES)
        me = 4 * x + 2 * y + c
        copies = []
        for i in range(n):
            own = pltpu.make_async_copy(ins[i].at[me] if scatter else ins[i], outs[i].at[me],
                                        local_sems.at[i])
            own.start()
            copies.append(own)
        for k in range(1, N_DEV):
            px = 1 - x if k & 4 else x
            py = 1 - y if k & 2 else y
            pc = 1 - c if k & 1 else c
            peer = 4 * px + 2 * py + pc
            for i in range(n):
                cp = pltpu.make_async_remote_copy(
                    src_ref=ins[i].at[peer] if scatter else ins[i],
                    dst_ref=outs[i].at[me],
                    send_sem=send_sems.at[i * n_peer + k - 1],
                    recv_sem=recv_sems.at[i * n_peer + k - 1],
                    device_id=(px, py, pc),
                    device_id_type=pl.DeviceIdType.MESH,
                )
                cp.start()
                copies.append(cp)
        for cp in copies:
            cp.wait()

    shapes = [a.shape[1:] if scatter else a.shape for a in arrays]
    any_spec = pl.BlockSpec(memory_space=pl.ANY)
    return pl.pallas_call(
        functools.partial(body),
        name=name,
        in_specs=[any_spec] * n,
        out_specs=[any_spec] * n,
        out_shape=[jax.ShapeDtypeStruct((N_DEV,) + tuple(s), a.dtype) for s, a in zip(shapes, arrays)],
        scratch_shapes=[pltpu.SemaphoreType.DMA((n * n_peer,)), pltpu.SemaphoreType.DMA((n * n_peer,)),
                        pltpu.SemaphoreType.DMA((n,))],
    )(*arrays)


def _matmul(name, a, b, kind, tm, tn, out_dtype, blocked_out=False):
    if kind == "tn":
        kdim, m = a.shape
    else:
        m, kdim = a.shape
    n = b.shape[0] if kind == "nt" else b.shape[1]
    tm, tn = min(tm, m), min(tn, n)
    dims = {"nn": _NN, "nt": _NT, "tn": _TN}[kind]

    def body(a_ref, b_ref, o_ref):
        o_ref[...] = _dot(a_ref[...], b_ref[...], dims).astype(o_ref.dtype)

    a_spec = (a, (kdim, tm), lambda j, i: (0, i)) if kind == "tn" else (a, (tm, kdim), lambda j, i: (i, 0))
    b_spec = (b, (tn, kdim), lambda j, i: (j, 0)) if kind == "nt" else (b, (kdim, tn), lambda j, i: (0, j))
    if blocked_out:
        out = ((n // tn, m, tn), out_dtype, (None, tm, tn), lambda j, i: (j, i, 0))
    else:
        out = ((m, n), out_dtype, (tm, tn), lambda j, i: (i, j))
    return _call(name, body, (n // tn, m // tm), [a_spec, b_spec], [out])[0]


def _ada_fwd(cs_all, ada_w, ada_b_cols):
    def body(c_ref, w_ref, b_ref, o_ref):
        o_ref[...] = lax.dot_general(c_ref[...], w_ref[...], _NN, preferred_element_type=F32,
                                     precision=lax.Precision.HIGHEST) + b_ref[...]

    r, d = cs_all.shape
    nc = ada_w.shape[1]
    return _call("ada_fwd", body, (1,),
                 [(cs_all, (r, d), lambda i: (0, 0)), (ada_w, (d, nc), lambda i: (0, 0)),
                  (ada_b_cols, (1, nc), lambda i: (0, 0))],
                 [((r, nc), F32, (r, nc), lambda i: (0, 0))])[0]


def _silu_rows(c_all):
    def body(c_ref, o_ref):
        v = c_ref[...]
        o_ref[...] = v * _sigmoid(v)

    return _call("silu_c", body, (1,), [(c_all, c_all.shape, lambda i: (0, 0))],
                 [(c_all.shape, F32, c_all.shape, lambda i: (0, 0))])[0]


def _pre_norm(x, g, mod, tm):
    s, d = x.shape

    def body(x_ref, g_ref, mod_ref, h_ref):
        n, _ = _rms(x_ref[...], d)
        sh, sc = mod_ref[:, 0:d], mod_ref[:, d:2 * d]
        h_ref[...] = (n * g_ref[...] * (1.0 + sc) + sh).astype(BF16)

    return _call("pre_norm", body, (s // tm,),
                 [(x, (tm, d), lambda i: (i, 0)), (g, (1, d), lambda i: (0, 0)),
                  (mod, (1, 6 * d), lambda i: (0, 0))],
                 [((s, d), BF16, (tm, d), lambda i: (i, 0))])[0]


def _prep(proj, pos_col, inv_freq, tm):
    s = proj.shape[0]
    sb_off = (2 * RET_QK + 2 * RET_V) // (3 * SB_W)

    def body(qk_ref, v_ref, sb_ref, pos_ref, f_ref, qk_out, v_out, sb_out, cos_out, sin_out):
        ang = pos_ref[...] * f_ref[...]
        lane = lax.broadcasted_iota(jnp.int32, (1, LANES), 1)
        first = jnp.bitwise_and(lane, RET_DQK - 1) < (RET_DQK // 2)
        cos = jnp.cos(ang)
        sin = jnp.where(first, -1.0, 1.0) * jnp.sin(ang)
        cos_out[...] = cos
        sin_out[...] = sin
        for g in range(2 * RET_QK // LANES):
            v = qk_ref[:, g * LANES:(g + 1) * LANES]
            sw = jnp.where(first, pltpu.roll(v, LANES - RET_DQK // 2, 1), pltpu.roll(v, RET_DQK // 2, 1))
            r = v * cos + sw * sin
            if g >= RET_QK // LANES:
                r = r * (RET_DQK ** -0.5)
            qk_out[:, g * LANES:(g + 1) * LANES] = r.astype(BF16)
        v_out[...] = v_ref[...].astype(BF16)
        sb_out[:, 0:SB_W] = (sb_ref[:, 0:SB_W] * (SB_DH ** -0.5)).astype(BF16)
        sb_out[:, SB_W:3 * SB_W] = sb_ref[:, SB_W:3 * SB_W].astype(BF16)

    return _call("prep", body, (s // tm,),
                 [(proj, (tm, 2 * RET_QK), lambda i: (i, 0)),
                  (proj, (tm, RET_V), lambda i: (i, 2 * RET_QK // RET_V)),
                  (proj, (tm, 3 * SB_W), lambda i: (i, sb_off)),
                  (pos_col, (tm, 1), lambda i: (i, 0)),
                  (inv_freq, (1, LANES), lambda i: (0, 0))],
                 [((s, 2 * RET_QK), BF16, (tm, 2 * RET_QK), lambda i: (i, 0)),
                  ((s, RET_V), BF16, (tm, RET_V), lambda i: (i, 0)),
                  ((s, 3 * SB_W), BF16, (tm, 3 * SB_W), lambda i: (i, 0)),
                  ((s, LANES), F32, (tm, LANES), lambda i: (i, 0)),
                  ((s, LANES), F32, (tm, LANES), lambda i: (i, 0))])


def _head_mask(hh):
    lane = lax.broadcasted_iota(jnp.int32, (1, LANES), 1)
    return (lane >= RET_DQK) if hh else (lane < RET_DQK)


def _masked(v, m):
    return jnp.where(m, v, jnp.zeros_like(v))


def _ret_decay(lg, i, j, t):
    row = lax.broadcasted_iota(jnp.int32, (t, t), 0) + i * t
    col = lax.broadcasted_iota(jnp.int32, (t, t), 1) + j * t
    allowed = jnp.right_shift(col, CHUNK_SHIFT) <= jnp.right_shift(row, CHUNK_SHIFT)
    dist = jnp.abs(row - col).astype(F32)
    return jnp.where(allowed, jnp.exp(lg * dist), 0.0)


def _ret_fwd(qk_rot, v_bf, proj, gn_g, log_gamma, t):
    s = qk_rot.shape[0]
    gate_off = (2 * RET_QK + RET_V) // (2 * RET_DV)
    n_pair = HEADS // 2

    def body(lg_ref, q_ref, k_ref, v_ref, g_ref, w_ref, ret_ref, rg_ref):
        hp, i = pl.program_id(0), pl.program_id(1)
        qb = q_ref[...]
        for hh in range(2):
            lg = lg_ref[2 * hp + hh]
            cols = slice(hh * RET_DV, (hh + 1) * RET_DV)
            qm = _masked(qb, _head_mask(hh))

            def step(j, o, lg=lg, cols=cols, qm=qm):
                rows = pl.ds(pl.multiple_of(j * t, t), t)
                sc = _dot(qm, k_ref[rows, :], _NT)
                p = (sc * _ret_decay(lg, i, j, t)).astype(BF16)
                return o + _dot(p, v_ref[rows, cols])

            o = lax.fori_loop(0, i + 1, step, jnp.zeros((t, RET_DV), F32))
            ret_ref[:, cols] = o
            mu = jnp.sum(o, axis=1, keepdims=True) * (1.0 / RET_DV)
            xc = o - mu
            var = jnp.sum(xc * xc, axis=1, keepdims=True) * (1.0 / RET_DV)
            nrm = xc * lax.rsqrt(var + EPS) * w_ref[:, cols]
            g = g_ref[:, cols]
            rg_ref[:, cols] = (g * _sigmoid(g) * nrm).astype(BF16)

    pw = 2 * RET_DV
    return pl.pallas_call(
        functools.partial(body),
        name="ret_fwd",
        grid_spec=pltpu.PrefetchScalarGridSpec(
            num_scalar_prefetch=1,
            grid=(n_pair, s // t),
            in_specs=[pl.BlockSpec((t, LANES), lambda hp, i, lg: (i, hp)),
                      pl.BlockSpec((s, LANES), lambda hp, i, lg: (0, n_pair + hp)),
                      pl.BlockSpec((s, pw), lambda hp, i, lg: (0, hp)),
                      pl.BlockSpec((t, pw), lambda hp, i, lg: (i, gate_off + hp)),
                      pl.BlockSpec((1, pw), lambda hp, i, lg: (0, hp))],
            out_specs=[pl.BlockSpec((t, pw), lambda hp, i, lg: (i, hp)),
                       pl.BlockSpec((t, pw), lambda hp, i, lg: (i, hp))],
        ),
        out_shape=[jax.ShapeDtypeStruct((s, RET_V), F32), jax.ShapeDtypeStruct((s, RET_V), BF16)],
        compiler_params=pltpu.CompilerParams(
            dimension_semantics=("arbitrary", "arbitrary"), vmem_limit_bytes=VMEM_LIMIT),
    )(log_gamma, qk_rot, qk_rot, v_bf, proj, gn_g)


def _sb_tile(qm, kj, i, j, t, upper, c):
    z = _dot(qm, kj, _NT)
    row = lax.broadcasted_iota(jnp.int32, (t, t), 0) + i * t
    col = lax.broadcasted_iota(jnp.int32, (t, t), 1) + j * t
    valid = col < row
    sp = jnp.maximum(z, 0.0) + jnp.log(1.0 + jnp.exp(-jnp.abs(z)))
    log_1m = jnp.where(valid, -sp, 0.0)
    log_b = z - sp
    stick = _dot_split(log_1m, upper) + c
    a = jnp.where(valid, jnp.exp(log_b + stick), 0.0)
    return a, log_1m, log_b, valid


def _sb_fwd(qkv, t):
    s = qkv.shape[0]
    n_pair = HEADS // 2
    assert s // t <= LANES

    def body(q_ref, k_ref, v_ref, o_ref, carry_ref):
        i = pl.program_id(1)
        r = lax.broadcasted_iota(jnp.int32, (t, t), 0)
        cc = lax.broadcasted_iota(jnp.int32, (t, t), 1)
        upper = (r > cc).astype(BF16)
        lane = lax.broadcasted_iota(jnp.int32, (1, LANES), 1)
        qb = q_ref[...]
        acc = jnp.zeros((t, LANES), F32)
        for hh in range(2):
            m = _head_mask(hh)
            qm = _masked(qb, m)

            cols = slice(hh * LANES, (hh + 1) * LANES)
            carry_ref[:, cols] = jnp.zeros((t, LANES), F32)

            def step(n, carry, m=m, qm=qm, cols=cols):
                c, o = carry
                j = i - n
                rows = pl.ds(pl.multiple_of(j * t, t), t)
                a, log_1m, _, _ = _sb_tile(qm, k_ref[rows, :], i, j, t, upper, c)
                o = o + _dot(a, _masked(v_ref[rows, :], m))
                carry_ref[:, cols] = jnp.where(lane == j, c, carry_ref[:, cols])
                return c + jnp.sum(log_1m, axis=1, keepdims=True), o

            _, acc = lax.fori_loop(0, i + 1, step, (jnp.zeros((t, 1), F32), acc))
        o_ref[...] = acc

    return _call("sb_fwd", body, (n_pair, s // t),
                 [(qkv, (t, LANES), lambda hp, i: (i, hp)),
                  (qkv, (s, LANES), lambda hp, i: (0, n_pair + hp)),
                  (qkv, (s, LANES), lambda hp, i: (0, 2 * n_pair + hp))],
                 [((s, SB_W), F32, (t, LANES), lambda hp, i: (i, hp)),
                  ((s, HEADS * LANES), F32, (t, 2 * LANES), lambda hp, i: (i, hp))])


def _merge(retg, sb, w_ret, w_sb, proj, tm, tn):
    s, d = retg.shape[0], w_ret.shape[1]
    ar_off = (2 * RET_QK + 2 * RET_V + 3 * SB_W) // tn
    as_off = ar_off + d // tn

    def body(rg_ref, sb_ref, wr_ref, ws_ref, ar_ref, as_ref, mix_ref, r_ref, s_ref):
        rr = _dot(rg_ref[...], wr_ref[...])
        ss = _dot(sb_ref[...], ws_ref[...])
        mix_ref[...] = (_sigmoid(ar_ref[...]) * rr + _sigmoid(as_ref[...]) * ss).astype(BF16)
        r_ref[...] = rr.astype(BF16)
        s_ref[...] = ss.astype(BF16)

    tile = (tm, tn)
    return _call("merge", body, (d // tn, s // tm),
                 [(retg, (tm, RET_V), lambda j, i: (i, 0)), (sb, (tm, SB_W), lambda j, i: (i, 0)),
                  (w_ret, (RET_V, tn), lambda j, i: (0, j)), (w_sb, (SB_W, tn), lambda j, i: (0, j)),
                  (proj, tile, lambda j, i: (i, ar_off + j)), (proj, tile, lambda j, i: (i, as_off + j))],
                 [((s, d), BF16, tile, lambda j, i: (i, j))] * 3)


def _out_proj(mixed, w_out, x, mod, gp1, g2, tm):
    s, d = x.shape

    def body(a_ref, w_ref, x_ref, mod_ref, gp_ref, g2_ref, y_ref, hres_ref, h2_ref):
        y = _dot(a_ref[...], w_ref[...])
        y_ref[...] = y
        ny, _ = _rms(y, d)
        hres = x_ref[...] + mod_ref[:, 2 * d:3 * d] * (ny * gp_ref[...])
        hres_ref[...] = hres
        n2, _ = _rms(hres, d)
        h2_ref[...] = (n2 * g2_ref[...] * (1.0 + mod_ref[:, 4 * d:5 * d]) + mod_ref[:, 3 * d:4 * d]).astype(BF16)

    row = lambda i: (i, 0)
    fix = lambda i: (0, 0)
    return _call("out_proj", body, (s // tm,),
                 [(mixed, (tm, d), row), (w_out, (d, d), fix), (x, (tm, d), row),
                  (mod, (1, 6 * d), fix), (gp1, (1, d), fix), (g2, (1, d), fix)],
                 [((s, d), F32, (tm, d), row), ((s, d), F32, (tm, d), row), ((s, d), BF16, (tm, d), row)])


def _ff1(h2, w_ff1, tm, tn):
    s, f = h2.shape[0], w_ff1.shape[1]
    tm = min(tm, s)

    def body(a_ref, w_ref, u_ref, act_ref):
        u = _dot(a_ref[...], w_ref[...])
        r = jnp.maximum(u, 0.0)
        u_ref[...] = u.astype(BF16)
        act_ref[...] = (r * r).astype(BF16)

    d = h2.shape[1]
    return _call("ff1", body, (f // tn, s // tm),
                 [(h2, (tm, d), lambda j, i: (i, 0)), (w_ff1, (d, tn), lambda j, i: (0, j))],
                 [((s, f), BF16, (tm, tn), lambda j, i: (i, j))] * 2)


def _ff2_loss(act, w_ff2, hres, target, mod, gp2, tm):
    s, d = hres.shape
    f = act.shape[1]

    def body(a_ref, w_ref, h_ref, t_ref, mod_ref, gp_ref, dout_ref, df_ref, loss_ref, dgt_ref, dgp_ref):
        first = pl.program_id(0) == 0
        ff = _dot(a_ref[...], w_ref[...])
        nf, rf = _rms(ff, d)
        gt, gp = mod_ref[:, 5 * d:6 * d], gp_ref[...]
        out = h_ref[...] + gt * (nf * gp)
        err = out - t_ref[...]
        sq = jnp.sum(err * err, axis=1, keepdims=True)
        _accum(loss_ref, jnp.sum(sq, axis=0, keepdims=True), first)
        dout = err * (1.0 / d)
        dout_ref[...] = dout
        _accum(dgt_ref, _colsum(dout * (nf * gp)), first)
        _accum(dgp_ref, _colsum(dout * gt * nf), first)
        df_ref[...] = _rms_bwd(dout * gt * gp, nf, rf, d).astype(BF16)

    row = lambda i: (i, 0)
    fix = lambda i: (0, 0)
    return _call("ff2_loss", body, (s // tm,),
                 [(act, (tm, f), row), (w_ff2, (f, d), fix), (hres, (tm, d), row), (target, (tm, d), row),
                  (mod, (1, 6 * d), fix), (gp2, (1, d), fix)],
                 [((s, d), F32, (tm, d), row), ((s, d), BF16, (tm, d), row), ((1, 1), F32, (1, 1), fix),
                  ((1, d), F32, (1, d), fix), ((1, d), F32, (1, d), fix)])


def _ff2_bwd(df, w_ff2, u, tm, tn):
    s, d = df.shape
    f = w_ff2.shape[0]
    tm = min(tm, s)

    def body(a_ref, w_ref, u_ref, du_ref):
        da = _dot(a_ref[...], w_ref[...], _NT)
        du_ref[...] = (da * (2.0 * jnp.maximum(u_ref[...].astype(F32), 0.0))).astype(BF16)

    return _call("ff2_bwd", body, (f // tn, s // tm),
                 [(df, (tm, d), lambda j, i: (i, 0)), (w_ff2, (tn, d), lambda j, i: (j, 0)),
                  (u, (tm, tn), lambda j, i: (i, j))],
                 [((s, f), BF16, (tm, tn), lambda j, i: (i, j))])[0]


def _ff1_bwd(du, w_ff1, hres, dout, y, mod, g2, gp1, tm):
    s, d = hres.shape
    f = du.shape[1]

    def body(a_ref, w_ref, h_ref, do_ref, y_ref, mod_ref, g2_ref, gp_ref,
             dh_ref, dy_ref, dsh_ref, dsc_ref, dg2_ref, dgt_ref, dgp_ref):
        first = pl.program_id(0) == 0
        dh2 = _dot(a_ref[...], w_ref[...], _NT)
        n2, r2 = _rms(h_ref[...], d)
        g2, sc2 = g2_ref[...], mod_ref[:, 4 * d:5 * d]
        _accum(dsh_ref, _colsum(dh2), first)
        _accum(dsc_ref, _colsum(dh2 * n2 * g2), first)
        _accum(dg2_ref, _colsum(dh2 * n2 * (1.0 + sc2)), first)
        dhres = do_ref[...] + _rms_bwd(dh2 * g2 * (1.0 + sc2), n2, r2, d)
        dh_ref[...] = dhres
        ny, ry = _rms(y_ref[...], d)
        gt, gp = mod_ref[:, 2 * d:3 * d], gp_ref[...]
        _accum(dgt_ref, _colsum(dhres * (ny * gp)), first)
        _accum(dgp_ref, _colsum(dhres * gt * ny), first)
        dy_ref[...] = _rms_bwd(dhres * gt * gp, ny, ry, d).astype(BF16)

    row = lambda i: (i, 0)
    fix = lambda i: (0, 0)
    vec = ((1, d), F32, (1, d), fix)
    return _call("ff1_bwd", body, (s // tm,),
                 [(du, (tm, f), row), (w_ff1, (d, f), fix), (hres, (tm, d), row), (dout, (tm, d), row),
                  (y, (tm, d), row), (mod, (1, 6 * d), fix), (g2, (1, d), fix), (gp1, (1, d), fix)],
                 [((s, d), F32, (tm, d), row), ((s, d), BF16, (tm, d), row), vec, vec, vec, vec, vec])


def _out_bwd(dy, w_out, proj, r_bf, s_bf, tm, tn):
    s, d = dy.shape
    ar_off = (2 * RET_QK + 2 * RET_V + 3 * SB_W) // tn
    as_off = ar_off + d // tn

    def body(a_ref, w_ref, ar_ref, as_ref, r_ref, s_ref, dr_ref, ds_ref, dar_ref, das_ref):
        dm = _dot(a_ref[...], w_ref[...], _NT)
        sr, ss = _sigmoid(ar_ref[...]), _sigmoid(as_ref[...])
        dr_ref[...] = (dm * sr).astype(BF16)
        ds_ref[...] = (dm * ss).astype(BF16)
        dar_ref[...] = (dm * r_ref[...].astype(F32) * sr * (1.0 - sr)).astype(BF16)
        das_ref[...] = (dm * s_ref[...].astype(F32) * ss * (1.0 - ss)).astype(BF16)

    tile = (tm, tn)
    here = lambda j, i: (i, j)
    return _call("out_bwd", body, (d // tn, s // tm),
                 [(dy, (tm, d), lambda j, i: (i, 0)), (w_out, (tn, d), lambda j, i: (j, 0)),
                  (proj, tile, lambda j, i: (i, ar_off + j)), (proj, tile, lambda j, i: (i, as_off + j)),
                  (r_bf, tile, here), (s_bf, tile, here)],
                 [((s, d), BF16, tile, here)] * 4)


def _gn_bwd(dretg, ret, proj, gn_g, tm):
    s = ret.shape[0]
    gate_off = (2 * RET_QK + RET_V) // RET_V

    def body(d_ref, r_ref, g_ref, w_ref, dg_ref, dret_ref, dw_ref):
        first = pl.program_id(0) == 0
        for h in range(HEADS):
            cols = slice(h * RET_DV, (h + 1) * RET_DV)
            o, g, w, dr = r_ref[:, cols], g_ref[:, cols], w_ref[:, cols], d_ref[:, cols]
            mu = jnp.sum(o, axis=1, keepdims=True) * (1.0 / RET_DV)
            xc = o - mu
            rstd = lax.rsqrt(jnp.sum(xc * xc, axis=1, keepdims=True) * (1.0 / RET_DV) + EPS)
            n = xc * rstd
            sg = _sigmoid(g)
            silu = g * sg
            dg_ref[:, cols] = (dr * n * w * (sg * (1.0 + g * (1.0 - sg)))).astype(BF16)
            _accum(dw_ref.at[:, cols], _colsum(dr * silu * n), first)
            dn = dr * silu * w
            m1 = jnp.sum(dn, axis=1, keepdims=True) * (1.0 / RET_DV)
            m2 = jnp.sum(dn * n, axis=1, keepdims=True) * (1.0 / RET_DV)
            dret_ref[:, cols] = (rstd * (dn - m1 - n * m2)).astype(BF16)

    row = lambda i: (i, 0)
    fix = lambda i: (0, 0)
    return _call("gn_bwd", body, (s // tm,),
                 [(dretg, (tm, RET_V), row), (ret, (tm, RET_V), row),
                  (proj, (tm, RET_V), lambda i: (i, gate_off)), (gn_g, (1, RET_V), fix)],
                 [((s, RET_V), BF16, (tm, RET_V), row), ((s, RET_V), BF16, (tm, RET_V), row),
                  ((1, RET_V), F32, (1, RET_V), fix)])


def _ret_bwd(qk_rot, v_bf, dret, log_gamma, t):
    s = qk_rot.shape[0]
    n_pair = HEADS // 2
    pw = 2 * RET_DV

    def body(lg_ref, q_ref, k_ref, v_ref, do_ref, dq_ref, dk_ref, dv_ref):
        hp, i = pl.program_id(0), pl.program_id(1)

        @pl.when(i == 0)
        def _():
            dk_ref[...] = jnp.zeros_like(dk_ref)
            dv_ref[...] = jnp.zeros_like(dv_ref)

        qb = q_ref[...]
        dq = jnp.zeros((t, LANES), F32)
        for hh in range(2):
            lg = lg_ref[2 * hp + hh]
            cols = slice(hh * RET_DV, (hh + 1) * RET_DV)
            m = _head_mask(hh)
            qm = _masked(qb, m)
            dob = do_ref[:, cols]

            def step(j, dq, lg=lg, cols=cols, m=m, qm=qm, dob=dob):
                rows = pl.ds(pl.multiple_of(j * t, t), t)
                kj = k_ref[rows, :]
                dec = _ret_decay(lg, i, j, t)
                p = (_dot(qm, kj, _NT) * dec).astype(BF16)
                ds = (_dot(dob, v_ref[rows, cols], _NT) * dec).astype(BF16)
                dv_ref[rows, cols] += _dot(p, dob, _TN)
                dk_ref[rows, :] += _dot(ds, qm, _TN)
                return dq + _dot(ds, _masked(kj, m))

            dq = lax.fori_loop(0, i + 1, step, dq)
        dq_ref[...] = dq

    return pl.pallas_call(
        functools.partial(body),
        name="ret_bwd",
        grid_spec=pltpu.PrefetchScalarGridSpec(
            num_scalar_prefetch=1,
            grid=(n_pair, s // t),
            in_specs=[pl.BlockSpec((t, LANES), lambda hp, i, lg: (i, hp)),
                      pl.BlockSpec((s, LANES), lambda hp, i, lg: (0, n_pair + hp)),
                      pl.BlockSpec((s, pw), lambda hp, i, lg: (0, hp)),
                      pl.BlockSpec((t, pw), lambda hp, i, lg: (i, hp))],
            out_specs=[pl.BlockSpec((t, LANES), lambda hp, i, lg: (i, hp)),
                       pl.BlockSpec((s, LANES), lambda hp, i, lg: (0, hp)),
                       pl.BlockSpec((s, pw), lambda hp, i, lg: (0, hp))],
        ),
        out_shape=[jax.ShapeDtypeStruct((s, RET_QK), F32), jax.ShapeDtypeStruct((s, RET_QK), F32),
                   jax.ShapeDtypeStruct((s, RET_V), F32)],
        compiler_params=pltpu.CompilerParams(
            dimension_semantics=("arbitrary", "arbitrary"), vmem_limit_bytes=VMEM_LIMIT),
    )(log_gamma, qk_rot, qk_rot, v_bf, dret)


def _sb_bwd(qkv, carries, do, t):
    s = qkv.shape[0]
    n_pair = HEADS // 2

    def body(q_ref, k_ref, v_ref, c_ref, do_ref, dq_ref, dk_ref, dv_ref):
        i = pl.program_id(1)

        @pl.when(i == 0)
        def _():
            dk_ref[...] = jnp.zeros_like(dk_ref)
            dv_ref[...] = jnp.zeros_like(dv_ref)

        r = lax.broadcasted_iota(jnp.int32, (t, t), 0)
        cc = lax.broadcasted_iota(jnp.int32, (t, t), 1)
        upper = (r > cc).astype(BF16)
        lower = (r < cc).astype(BF16)
        lane = lax.broadcasted_iota(jnp.int32, (1, LANES), 1)
        qb = q_ref[...]
        dob = do_ref[...].astype(BF16)
        dq = jnp.zeros((t, LANES), F32)
        for hh in range(2):
            m = _head_mask(hh)
            qm = _masked(qb, m)
            dom = _masked(dob, m)
            cm = c_ref[:, hh * LANES:(hh + 1) * LANES]

            def step(j, carry, m=m, qm=qm, dom=dom, cm=cm):
                c_e, dq = carry
                rows = pl.ds(pl.multiple_of(j * t, t), t)
                kj, vj = k_ref[rows, :], v_ref[rows, :]
                c_stick = jnp.sum(jnp.where(lane == j, cm, 0.0), axis=1, keepdims=True)
                a, _, log_b, valid = _sb_tile(qm, kj, i, j, t, upper, c_stick)
                e = a * _dot(dom, vj, _NT)
                dv_ref[rows, :] += _dot(a, dom, _TN)
                prefix = _dot_split(e, lower) + c_e
                beta = jnp.exp(log_b)
                dz = jnp.where(valid, e * (1.0 - beta) - prefix * beta, 0.0).astype(BF16)
                dk_ref[rows, :] += _dot(dz, qm, _TN)
                return c_e + jnp.sum(e, axis=1, keepdims=True), dq + _dot(dz, _masked(kj, m))

            _, dq = lax.fori_loop(0, i + 1, step, (jnp.zeros((t, 1), F32), dq))
        dq_ref[...] = dq

    blk = lambda hp, i: (i, hp)
    return _call("sb_bwd", body, (n_pair, s // t),
                 [(qkv, (t, LANES), blk),
                  (qkv, (s, LANES), lambda hp, i: (0, n_pair + hp)),
                  (qkv, (s, LANES), lambda hp, i: (0, 2 * n_pair + hp)),
                  (carries, (t, 2 * LANES), blk), (do, (t, LANES), blk)],
                 [((s, SB_W), F32, (t, LANES), blk),
                  ((s, SB_W), F32, (s, LANES), lambda hp, i: (0, hp)),
                  ((s, SB_W), F32, (s, LANES), lambda hp, i: (0, hp))])


def _assemble_dproj(dq_r, dk_r, dv_r, dg_r, dq_s, dk_s, dv_s, da_r, da_s, cos, sin, tm):
    s, d = da_r.shape
    width = 2 * RET_QK + 2 * RET_V + 3 * SB_W + 2 * d

    def body(dq_ref, dk_ref, dv_ref, dg_ref, dqs_ref, dks_ref, dvs_ref, dar_ref, das_ref, cos_ref, sin_ref, o_ref):
        lane = lax.broadcasted_iota(jnp.int32, (1, LANES), 1)
        first = jnp.bitwise_and(lane, RET_DQK - 1) < (RET_DQK // 2)
        cos, sin = cos_ref[...], sin_ref[...]
        for src, base, scale in ((dq_ref, 0, 1.0), (dk_ref, RET_QK, RET_DQK ** -0.5)):
            for g in range(RET_QK // LANES):
                v = src[:, g * LANES:(g + 1) * LANES]
                sw = jnp.where(first, pltpu.roll(v, LANES - RET_DQK // 2, 1), pltpu.roll(v, RET_DQK // 2, 1))
                o_ref[:, base + g * LANES:base + (g + 1) * LANES] = ((v * cos - sw * sin) * scale).astype(BF16)
        off = 2 * RET_QK
        o_ref[:, off:off + RET_V] = dv_ref[...].astype(BF16)
        off += RET_V
        o_ref[:, off:off + RET_V] = dg_ref[...]
        off += RET_V
        o_ref[:, off:off + SB_W] = (dqs_ref[...] * (SB_DH ** -0.5)).astype(BF16)
        off += SB_W
        o_ref[:, off:off + SB_W] = dks_ref[...].astype(BF16)
        off += SB_W
        o_ref[:, off:off + SB_W] = dvs_ref[...].astype(BF16)
        off += SB_W
        o_ref[:, off:off + d] = dar_ref[...]
        off += d
        o_ref[:, off:off + d] = das_ref[...]

    row = lambda i: (i, 0)
    ins = [(a, (tm, a.shape[1]), row) for a in (dq_r, dk_r, dv_r, dg_r, dq_s, dk_s, dv_s, da_r, da_s, cos, sin)]
    return _call("assemble_dproj", body, (s // tm,), ins, [((s, width), BF16, (tm, width), row)])[0]


def _in_bwd(dproj, w_in, x, dhres, mod, g1, tm):
    s, d = x.shape
    width = dproj.shape[1]

    def body(a_ref, w_ref, x_ref, dh_ref, mod_ref, g_ref, dx_ref, dsh_ref, dsc_ref, dg_ref):
        first = pl.program_id(0) == 0
        dh = _dot(a_ref[...], w_ref[...], _NT)
        n1, r1 = _rms(x_ref[...], d)
        g1, sc1 = g_ref[...], mod_ref[:, d:2 * d]
        _accum(dsh_ref, _colsum(dh), first)
        _accum(dsc_ref, _colsum(dh * n1 * g1), first)
        _accum(dg_ref, _colsum(dh * n1 * (1.0 + sc1)), first)
        dx_ref[...] = dh_ref[...] + _rms_bwd(dh * g1 * (1.0 + sc1), n1, r1, d)

    row = lambda i: (i, 0)
    fix = lambda i: (0, 0)
    vec = ((1, d), F32, (1, d), fix)
    return _call("in_bwd", body, (s // tm,),
                 [(dproj, (tm, width), row), (w_in, (d, width), fix), (x, (tm, d), row), (dhres, (tm, d), row),
                  (mod, (1, 6 * d), fix), (g1, (1, d), fix)],
                 [((s, d), F32, (tm, d), row), vec, vec, vec])


def _adamw(w, g, m, v):
    m = ADAM_B1 * m + (1.0 - ADAM_B1) * g
    v = ADAM_B2 * v + (1.0 - ADAM_B2) * (g * g)
    m_hat = m / (1.0 - ADAM_B1 ** ADAM_STEP)
    v_hat = v / (1.0 - ADAM_B2 ** ADAM_STEP)
    delta = -ADAM_LR * (m_hat / (jnp.sqrt(v_hat) + ADAM_EPS) + ADAM_WD * w)
    return delta, m, v


def _adam_reduce(name, parts, w, m, v, tr):
    rws, cls = w.shape
    tr = min(tr, rws)

    def body(p_ref, w_ref, m_ref, v_ref, g_out, d_out, m_out, v_out):
        g = p_ref[0].astype(F32)
        for k in range(1, N_DEV):
            g = g + p_ref[k].astype(F32)
        delta, mn, vn = _adamw(w_ref[...], g, m_ref[...], v_ref[...])
        g_out[...] = g
        d_out[...] = delta
        m_out[...] = mn
        v_out[...] = vn

    row = lambda i: (i, 0)
    blk = (tr, cls)
    return _call(name, body, (rws // tr,),
                 [(parts, (N_DEV, tr, cls), lambda i: (0, i, 0)), (w, blk, row), (m, blk, row), (v, blk, row)],
                 [((rws, cls), F32, blk, row)] * 4)


def _ada_bwd_adam(cs_t, dmod_cols, w, m, v):
    d, nc = w.shape

    def body(c_ref, dm_ref, w_ref, m_ref, v_ref, g_out, d_out, m_out, v_out):
        g = c_ref[0] * dm_ref[0:1, :]
        for r in range(1, N_DEV):
            g = g + c_ref[r] * dm_ref[r:r + 1, :]
        delta, mn, vn = _adamw(w_ref[...], g, m_ref[...], v_ref[...])
        g_out[...] = g
        d_out[...] = delta
        m_out[...] = mn
        v_out[...] = vn

    fix = lambda i: (0, 0)
    blk = (d, nc)
    return _call("ada_bwd_adam", body, (1,),
                 [(cs_t, (N_DEV, d, 1), lambda i: (0, 0, 0)), (dmod_cols, (N_DEV, nc), fix), (w, blk, fix), (m, blk, fix), (v, blk, fix)],
                 [((d, nc), F32, blk, fix)] * 4)


def _small_adam(parts, w, m, v):
    n = w.shape[1]

    def body(p_ref, w_ref, m_ref, v_ref, g_out, d_out, m_out, v_out):
        g = p_ref[0:1, :]
        for k in range(1, N_DEV):
            g = g + p_ref[k:k + 1, :]
        delta, mn, vn = _adamw(w_ref[...], g, m_ref[...], v_ref[...])
        g_out[...] = g
        d_out[...] = delta
        m_out[...] = mn
        v_out[...] = vn

    fix = lambda i: (0, 0)
    return _call("small_adam", body, (1,),
                 [(parts, (N_DEV, n), fix), (w, (1, n), fix), (m, (1, n), fix), (v, (1, n), fix)],
                 [((1, n), F32, (1, n), fix)] * 4)


def kernel(x, c, positions, ada_w, ada_b, pre_mix_g, post_mix_g, pre_ffn_g, post_ffn_g, w_in, ret_gn_g, w_ret_branch, w_sb_branch, w_out, w_ff1, w_ff2, loss_target, m_ada_w, m_ada_b, m_pre_mix_g, m_post_mix_g, m_pre_ffn_g, m_post_ffn_g, m_w_in, m_ret_gn_g, m_w_ret_branch, m_w_sb_branch, m_w_out, m_w_ff1, m_w_ff2, v_ada_w, v_ada_b, v_pre_mix_g, v_post_mix_g, v_pre_ffn_g, v_post_ffn_g, v_w_in, v_ret_gn_g, v_w_ret_branch, v_w_sb_branch, v_w_out, v_w_ff1, v_w_ff2):
    _, s, d = x.shape
    d_ff = w_ff1.shape[2] * N_DEV
    d_in = w_in.shape[2] * N_DEV
    me = 4 * lax.axis_index("x") + 2 * lax.axis_index("y") + lax.axis_index("c")
    x2, tgt = x[0], loss_target[0]

    big = [w_in[0], w_ret_branch[0], w_sb_branch[0], w_out[0], w_ff1[0], w_ff2[0]]
    gathered = _exchange("gather_weights", [c] + [w.astype(BF16) for w in big], scatter=False)
    c_all = gathered[0].reshape(N_DEV, d)
    g_in, g_ret, g_sb, g_out, g_ff1, g_ff2 = gathered[1:]
    wf_in = jnp.moveaxis(g_in, 0, 1).reshape(d, d_in)
    wf_ret = g_ret.reshape(RET_V, d)
    wf_sb = jnp.moveaxis(g_sb, 0, 1).reshape(SB_W, d)
    wf_out = g_out.reshape(d, d)
    wf_ff1 = jnp.moveaxis(g_ff1, 0, 1).reshape(d, d_ff)
    wf_ff2 = g_ff2.reshape(d_ff, d)

    n_ada = ada_w.shape[2]
    cs_all = _silu_rows(c_all)
    ada_b_cols = lax.dynamic_slice(ada_b, (0, me * n_ada), (1, n_ada))
    mod_cols = _ada_fwd(cs_all, ada_w[0], ada_b_cols)
    mod_all = _exchange("gather_mod", [mod_cols], scatter=False)[0]
    mod = lax.dynamic_index_in_dim(mod_all, me, axis=1, keepdims=False).reshape(1, 6 * d)

    tm = min(256, s)
    h = _pre_norm(x2, pre_mix_g, mod, tm)
    proj = _matmul("in_proj", h, wf_in, "nn", 1024, 512, F32)
    pos_col = positions.reshape(s, 1).astype(F32)
    freqs = ROPE_BASE ** (-jnp.arange(0, RET_DQK, 2, dtype=F32) / RET_DQK)
    inv_freq = jnp.tile(freqs, LANES // (RET_DQK // 2)).reshape(1, LANES)
    qk_rot, v_bf, qkv_sb, cos_t, sin_t = _prep(proj, pos_col, inv_freq, tm)
    log_gamma = jnp.asarray(np.log1p(-(2.0 ** (-5.0 - np.arange(HEADS)))), F32)
    t_ret = min(256, s)
    t_sb = min(128, s)
    ret, retg = _ret_fwd(qk_rot, v_bf, proj, ret_gn_g, log_gamma, t_ret)
    sb, sb_carry = _sb_fwd(qkv_sb, t_sb)
    mixed, r_bf, s_bf = _merge(retg, sb, wf_ret, wf_sb, proj, tm, min(512, d))
    y, hres, h2 = _out_proj(mixed, wf_out, x2, mod, post_mix_g, pre_ffn_g, tm)
    u, act = _ff1(h2, wf_ff1, 1024, 512)
    dout, df, loss_sum, d_gt2, d_gp2 = _ff2_loss(act, wf_ff2, hres, tgt, mod, post_ffn_g, tm)

    du = _ff2_bwd(df, wf_ff2, u, 1024, 512)
    gw_ff2 = _matmul("grad_w_ff2", act, df, "tn", 512, 512, BF16).reshape(N_DEV, d_ff // N_DEV, d)
    gw_ff1 = _matmul("grad_w_ff1", h2, du, "tn", 512, d_ff // N_DEV, BF16, blocked_out=True)
    dhres, dy, d_sh2, d_sc2, d_g2, d_gt1, d_gp1 = _ff1_bwd(du, wf_ff1, hres, dout, y, mod, pre_ffn_g, post_mix_g, tm)
    d_r, d_s, da_r, da_s = _out_bwd(dy, wf_out, proj, r_bf, s_bf, tm, min(512, d))
    gw_out = _matmul("grad_w_out", mixed, dy, "tn", 512, 512, BF16).reshape(N_DEV, d // N_DEV, d)
    dretg = _matmul("ret_branch_bwd", d_r, wf_ret, "nt", 1024, 512, BF16)
    dsb = _matmul("sb_branch_bwd", d_s, wf_sb, "nt", 1024, 512, F32)
    gw_ret = _matmul("grad_w_ret", retg, d_r, "tn", 512, 512, BF16).reshape(N_DEV, RET_V // N_DEV, d)
    gw_sb = _matmul("grad_w_sb", sb, d_s, "tn", 512, d // N_DEV, BF16, blocked_out=True)
    dg_r, dret, d_gn = _gn_bwd(dretg, ret, proj, ret_gn_g, tm)
    dq_r, dk_r, dv_r = _ret_bwd(qk_rot, v_bf, dret, log_gamma, t_ret)
    dq_s, dk_s, dv_s = _sb_bwd(qkv_sb, sb_carry, dsb, t_sb)
    dproj = _assemble_dproj(dq_r, dk_r, dv_r, dg_r, dq_s, dk_s, dv_s, da_r, da_s, cos_t, sin_t, tm)
    gw_in_full = _matmul("grad_w_in", h, dproj, "tn", 512, 512, BF16)
    gw_in = jnp.moveaxis(gw_in_full.reshape(d, N_DEV, d_in // N_DEV), 1, 0)
    grad_x, d_sh1, d_sc1, d_g1 = _in_bwd(dproj, wf_in, x2, dhres, mod, pre_mix_g, tm)

    small = jnp.concatenate([d_sh1, d_sc1, d_gt1, d_sh2, d_sc2, d_gt2, d_g1, d_gp1, d_g2, d_gp2, d_gn], axis=1)
    small_all = _exchange("gather_small", [small], scatter=False)[0].reshape(N_DEV, small.shape[1])
    parts = _exchange("scatter_grads", [gw_in, gw_ret, gw_sb, gw_out, gw_ff1, gw_ff2], scatter=True)

    res = {}
    names = ["w_in", "w_ret_branch", "w_sb_branch", "w_out", "w_ff1", "w_ff2"]
    ws = [w_in, w_ret_branch, w_sb_branch, w_out, w_ff1, w_ff2]
    ms = [m_w_in, m_w_ret_branch, m_w_sb_branch, m_w_out, m_w_ff1, m_w_ff2]
    vs = [v_w_in, v_w_ret_branch, v_w_sb_branch, v_w_out, v_w_ff1, v_w_ff2]
    for nm, p, w, m, v in zip(names, parts, ws, ms, vs):
        res[nm] = [o[None] for o in _adam_reduce("adam_" + nm, p, w[0], m[0], v[0], 256)]
    dmod_cols = lax.dynamic_slice(small_all, (0, me * n_ada), (N_DEV, n_ada))
    res["ada_w"] = [o[None] for o in _ada_bwd_adam(cs_all.reshape(N_DEV, d, 1), dmod_cols, ada_w[0], m_ada_w[0], v_ada_w[0])]
    vec_names = ["ada_b", "pre_mix_g", "post_mix_g", "pre_ffn_g", "post_ffn_g", "ret_gn_g"]
    cat = lambda xs: jnp.concatenate(xs, axis=1)
    packed = _small_adam(small_all,
                         cat([ada_b, pre_mix_g, post_mix_g, pre_ffn_g, post_ffn_g, ret_gn_g]),
                         cat([m_ada_b, m_pre_mix_g, m_post_mix_g, m_pre_ffn_g, m_post_ffn_g, m_ret_gn_g]),
                         cat([v_ada_b, v_pre_mix_g, v_post_mix_g, v_pre_ffn_g, v_post_ffn_g, v_ret_gn_g]))
    off = 0
    for nm, width in zip(vec_names, [6 * d, d, d, d, d, RET_V]):
        res[nm] = [p[:, off:off + width] for p in packed]
        off += width

    loss = (0.5 / d) * lax.psum(loss_sum[0, 0], AXES)
    order = ["ada_w", "ada_b", "pre_mix_g", "post_mix_g", "pre_ffn_g", "post_ffn_g", "w_in", "ret_gn_g",
             "w_ret_branch", "w_sb_branch", "w_out", "w_ff1", "w_ff2"]
    outs = [loss, grad_x[None]]
    for k in range(4):
        outs += [res[nm][k] for nm in order]
    return tuple(outs)
```

```python
import functools

import numpy as np
import jax
import jax.numpy as jnp
from jax import lax
from jax.experimental import pallas as pl
from jax.experimental.pallas import tpu as pltpu

F32 = jnp.float32
BF16 = jnp.bfloat16
N_DEV = 8
AXES = ("x", "y", "c")

EPS = 1e-6
CHUNK = 64
CHUNK_SHIFT = 6
HEADS = 8
RET_DQK = 64
RET_DV = 128
SB_DH = 64
RET_QK = HEADS * RET_DQK
RET_V = HEADS * RET_DV
SB_W = HEADS * SB_DH
ROPE_BASE = 10000.0
LANES = 128

ADAM_LR = 0.001
ADAM_B1 = 0.9
ADAM_B2 = 0.999
ADAM_EPS = 1e-08
ADAM_WD = 0.01
ADAM_STEP = 10

VMEM_LIMIT = 56 * 1024 * 1024

_NN = (((1,), (0,)), ((), ()))
_NT = (((1,), (1,)), ((), ()))
_TN = (((0,), (0,)), ((), ()))


def _dot(a, b, dims=_NN):
    if a.dtype != BF16:
        a = a.astype(BF16)
    if b.dtype != BF16:
        b = b.astype(BF16)
    return lax.dot_general(a, b, dims, preferred_element_type=F32)


def _dot_split(a, b):
    hi = a.astype(BF16)
    lo = (a - hi.astype(F32)).astype(BF16)
    return (lax.dot_general(hi, b, _NN, preferred_element_type=F32)
            + lax.dot_general(lo, b, _NN, preferred_element_type=F32))


def _sigmoid(x):
    return 1.0 / (1.0 + jnp.exp(-x))


def _rms(x, d):
    r = lax.rsqrt(jnp.sum(x * x, axis=1, keepdims=True) * (1.0 / d) + EPS)
    return x * r, r


def _rms_bwd(dn, n, r, d):
    return r * (dn - n * (jnp.sum(dn * n, axis=1, keepdims=True) * (1.0 / d)))


def _colsum(v):
    return jnp.sum(v, axis=0, keepdims=True)


def _accum(ref, val, first):
    @pl.when(first)
    def _():
        ref[...] = val

    @pl.when(jnp.logical_not(first))
    def _():
        ref[...] += val


def _exchange_copies(ins, outs, send_sems, recv_sems, local_sems, scatter):
    n = len(ins)
    n_peer = N_DEV - 1
    flags = _per_array(scatter, n)
    x, y, c = (lax.axis_index(a) for a in AXES)
    me = 4 * x + 2 * y + c
    copies = [pltpu.make_async_copy(ins[i].at[me] if flags[i] else ins[i], outs[i].at[me], local_sems.at[i])
              for i in range(n)]
    for k in range(1, N_DEV):
        px = 1 - x if k & 4 else x
        py = 1 - y if k & 2 else y
        pc = 1 - c if k & 1 else c
        peer = 4 * px + 2 * py + pc
        for i in range(n):
            copies.append(pltpu.make_async_remote_copy(
                src_ref=ins[i].at[peer] if flags[i] else ins[i],
                dst_ref=outs[i].at[me],
                send_sem=send_sems.at[i * n_peer + k - 1],
                recv_sem=recv_sems.at[i * n_peer + k - 1],
                device_id=(px, py, pc),
                device_id_type=pl.DeviceIdType.MESH,
            ))
    return copies


def _per_array(scatter, n):
    return list(scatter) if isinstance(scatter, (list, tuple)) else [scatter] * n


def _exchange_shapes(arrays, scatter):
    flags = _per_array(scatter, len(arrays))
    return [jax.ShapeDtypeStruct((N_DEV,) + tuple(a.shape[1:] if f else a.shape), a.dtype)
            for a, f in zip(arrays, flags)]


def _exchange_sems(n):
    return [pltpu.SemaphoreType.DMA((n * (N_DEV - 1),)), pltpu.SemaphoreType.DMA((n * (N_DEV - 1),)),
            pltpu.SemaphoreType.DMA((n,))]


def _call(name, body, grid, ins, outs, scratch=(), riders=None):
    any_spec = pl.BlockSpec(memory_space=pl.ANY)
    in_specs = [pl.BlockSpec(memory_space=im) if bs is None else pl.BlockSpec(bs, im) for _, bs, im in ins]
    out_specs = [pl.BlockSpec(bs, im) for _, _, bs, im in outs]
    out_shape = [jax.ShapeDtypeStruct(s, d) for s, d, _, _ in outs]
    operands = [a for a, _, _ in ins]
    scratch = list(scratch)
    kernel = functools.partial(body)
    if riders is not None:
        arrays, scatter = riders
        nr, n_in, n_out, n_scr = len(arrays), len(ins), len(outs), len(scratch)

        def kernel(*refs):
            own_in, ride_in = refs[:n_in], refs[n_in:n_in + nr]
            own_out = refs[n_in + nr:n_in + nr + n_out]
            ride_out = refs[n_in + nr + n_out:n_in + 2 * nr + n_out]
            own_scr = refs[n_in + 2 * nr + n_out:n_in + 2 * nr + n_out + n_scr]
            sems = refs[n_in + 2 * nr + n_out + n_scr:]
            ids = [pl.program_id(a) for a in range(len(grid))]
            first = functools.reduce(jnp.logical_and, [i == 0 for i in ids])
            last = functools.reduce(jnp.logical_and, [i == g - 1 for i, g in zip(ids, grid)])

            @pl.when(first)
            def _():
                for cp in _exchange_copies(ride_in, ride_out, *sems, scatter):
                    cp.start()

            body(*own_in, *own_out, *own_scr)

            @pl.when(last)
            def _():
                for cp in _exchange_copies(ride_in, ride_out, *sems, scatter):
                    cp.wait()

        in_specs += [any_spec] * nr
        out_specs += [any_spec] * nr
        out_shape += _exchange_shapes(arrays, scatter)
        operands += list(arrays)
        scratch += _exchange_sems(nr)
    return pl.pallas_call(
        kernel,
        name=name,
        grid=grid,
        in_specs=in_specs,
        out_specs=out_specs,
        out_shape=out_shape,
        scratch_shapes=scratch,
        compiler_params=pltpu.CompilerParams(
            dimension_semantics=("arbitrary",) * len(grid), vmem_limit_bytes=VMEM_LIMIT),
    )(*operands)


def _exchange(name, arrays, scatter):
    n = len(arrays)

    def body(*refs):
        copies = _exchange_copies(refs[:n], refs[n:2 * n], *refs[2 * n:], scatter)
        for cp in copies:
            cp.start()
        for cp in copies:
            cp.wait()

    any_spec = pl.BlockSpec(memory_space=pl.ANY)
    return pl.pallas_call(
        functools.partial(body),
        name=name,
        in_specs=[any_spec] * n,
        out_specs=[any_spec] * n,
        out_shape=_exchange_shapes(arrays, scatter),
        scratch_shapes=_exchange_sems(n),
    )(*arrays)


def _matmul(name, a, b, kind, tm, tn, out_dtype, blocked_out=False):
    if kind == "tn":
        kdim, m = a.shape
    else:
        m, kdim = a.shape
    n = b.shape[0] if kind == "nt" else b.shape[1]
    tm, tn = min(tm, m), min(tn, n)
    dims = {"nn": _NN, "nt": _NT, "tn": _TN}[kind]

    def body(a_ref, b_ref, o_ref):
        o_ref[...] = _dot(a_ref[...], b_ref[...], dims).astype(o_ref.dtype)

    a_spec = (a, (kdim, tm), lambda j, i: (0, i)) if kind == "tn" else (a, (tm, kdim), lambda j, i: (i, 0))
    b_spec = (b, (tn, kdim), lambda j, i: (j, 0)) if kind == "nt" else (b, (kdim, tn), lambda j, i: (0, j))
    if blocked_out:
        out = ((n // tn, m, tn), out_dtype, (None, tm, tn), lambda j, i: (j, i, 0))
    else:
        out = ((m, n), out_dtype, (tm, tn), lambda j, i: (i, j))
    return _call(name, body, (n // tn, m // tm), [a_spec, b_spec], [out])[0]


def _ada_fwd(cs_all, ada_w, ada_b_cols):
    def body(c_ref, w_ref, b_ref, o_ref):
        o_ref[...] = lax.dot_general(c_ref[...], w_ref[...], _NN, preferred_element_type=F32,
                                     precision=lax.Precision.HIGHEST) + b_ref[...]

    r, d = cs_all.shape
    nc = ada_w.shape[1]
    return _call("ada_fwd", body, (1,),
                 [(cs_all, (r, d), lambda i: (0, 0)), (ada_w, (d, nc), lambda i: (0, 0)),
                  (ada_b_cols, (1, nc), lambda i: (0, 0))],
                 [((r, nc), F32, (r, nc), lambda i: (0, 0))])[0]


def _silu_rows(c_all):
    def body(c_ref, o_ref):
        v = c_ref[...]
        o_ref[...] = v * _sigmoid(v)

    return _call("silu_c", body, (1,), [(c_all, c_all.shape, lambda i: (0, 0))],
                 [(c_all.shape, F32, c_all.shape, lambda i: (0, 0))])[0]


def _pre_norm(x, g, mod, tm):
    s, d = x.shape

    def body(x_ref, g_ref, mod_ref, h_ref):
        n, _ = _rms(x_ref[...], d)
        sh, sc = mod_ref[:, 0:d], mod_ref[:, d:2 * d]
        h_ref[...] = (n * g_ref[...] * (1.0 + sc) + sh).astype(BF16)

    return _call("pre_norm", body, (s // tm,),
                 [(x, (tm, d), lambda i: (i, 0)), (g, (1, d), lambda i: (0, 0)),
                  (mod, (1, 6 * d), lambda i: (0, 0))],
                 [((s, d), BF16, (tm, d), lambda i: (i, 0))])[0]


def _prep(proj, pos_col, inv_freq, tm):
    s = proj.shape[0]
    sb_off = (2 * RET_QK + 2 * RET_V) // (3 * SB_W)

    def body(qk_ref, v_ref, sb_ref, pos_ref, f_ref, qk_out, v_out, sb_out, cos_out, sin_out):
        ang = pos_ref[...] * f_ref[...]
        lane = lax.broadcasted_iota(jnp.int32, (1, LANES), 1)
        first = jnp.bitwise_and(lane, RET_DQK - 1) < (RET_DQK // 2)
        cos = jnp.cos(ang)
        sin = jnp.where(first, -1.0, 1.0) * jnp.sin(ang)
        cos_out[...] = cos
        sin_out[...] = sin
        for g in range(2 * RET_QK // LANES):
            v = qk_ref[:, g * LANES:(g + 1) * LANES]
            sw = jnp.where(first, pltpu.roll(v, LANES - RET_DQK // 2, 1), pltpu.roll(v, RET_DQK // 2, 1))
            r = v * cos + sw * sin
            if g >= RET_QK // LANES:
                r = r * (RET_DQK ** -0.5)
            qk_out[:, g * LANES:(g + 1) * LANES] = r.astype(BF16)
        v_out[...] = v_ref[...].astype(BF16)
        sb_out[:, 0:SB_W] = (sb_ref[:, 0:SB_W] * (SB_DH ** -0.5)).astype(BF16)
        sb_out[:, SB_W:3 * SB_W] = sb_ref[:, SB_W:3 * SB_W].astype(BF16)

    return _call("prep", body, (s // tm,),
                 [(proj, (tm, 2 * RET_QK), lambda i: (i, 0)),
                  (proj, (tm, RET_V), lambda i: (i, 2 * RET_QK // RET_V)),
                  (proj, (tm, 3 * SB_W), lambda i: (i, sb_off)),
                  (pos_col, (tm, 1), lambda i: (i, 0)),
                  (inv_freq, (1, LANES), lambda i: (0, 0))],
                 [((s, 2 * RET_QK), BF16, (tm, 2 * RET_QK), lambda i: (i, 0)),
                  ((s, RET_V), BF16, (tm, RET_V), lambda i: (i, 0)),
                  ((s, 3 * SB_W), BF16, (tm, 3 * SB_W), lambda i: (i, 0)),
                  ((s, LANES), F32, (tm, LANES), lambda i: (i, 0)),
                  ((s, LANES), F32, (tm, LANES), lambda i: (i, 0))])


def _head_mask(hh):
    lane = lax.broadcasted_iota(jnp.int32, (1, LANES), 1)
    return (lane >= RET_DQK) if hh else (lane < RET_DQK)


def _masked(v, m):
    return jnp.where(m, v, jnp.zeros_like(v))


GROUP = 4


def _stack_heads(v):
    return jnp.concatenate([_masked(v, _head_mask(0)), _masked(v, _head_mask(1))], axis=0)


def _side_by_side(v, t):
    return jnp.concatenate([v[:t], v[t:]], axis=1)


def _split_bf16(v):
    hi = v.astype(BF16)
    lo = (v - hi.astype(F32)).astype(BF16)
    return jnp.concatenate([hi, lo], axis=1)


def _ret_decay(lg_rows, i, j, t):
    row = jnp.bitwise_and(lax.broadcasted_iota(jnp.int32, (2 * t, t), 0), t - 1) + i * t
    col = lax.broadcasted_iota(jnp.int32, (2 * t, t), 1) + j * t
    allowed = jnp.right_shift(col, CHUNK_SHIFT) <= jnp.right_shift(row, CHUNK_SHIFT)
    dist = jnp.abs(row - col).astype(F32)
    return jnp.where(allowed, jnp.exp(lg_rows * dist), 0.0)


def _lg_rows(lg_ref, hp, t):
    first = lax.broadcasted_iota(jnp.int32, (2 * t, 1), 0) < t
    return jnp.where(first, lg_ref[2 * hp], lg_ref[2 * hp + 1])


def _ret_fwd(qk_rot, v_bf, proj, gn_g, log_gamma, t, riders=None):
    s = qk_rot.shape[0]
    gate_off = (2 * RET_QK + RET_V) // (2 * RET_DV)
    n_pair = HEADS // 2
    assert (s // t) % GROUP == 0 and t & (t - 1) == 0

    def body(lg_ref, q_ref, k_ref, v_ref, g_ref, w_ref, ret_ref, rg_ref):
        hp, i = pl.program_id(0), pl.program_id(1)
        qs = _stack_heads(q_ref[...])
        lg_rows = _lg_rows(lg_ref, hp, t)

        def step(g, carry):
            o0, o1 = carry
            js = [g * GROUP + sub for sub in range(GROUP)]
            rows = [pl.ds(pl.multiple_of(j * t, t), t) for j in js]
            ss = [_dot(qs, k_ref[rw, :], _NT) for rw in rows]
            ps = [(sc * _ret_decay(lg_rows, i, j, t)).astype(BF16) for sc, j in zip(ss, js)]
            for p, rw in zip(ps, rows):
                o0 = o0 + _dot(p[:t], v_ref[rw, 0:RET_DV])
                o1 = o1 + _dot(p[t:], v_ref[rw, RET_DV:2 * RET_DV])
            return o0, o1

        zero = jnp.zeros((t, RET_DV), F32)
        outs = lax.fori_loop(0, (i + GROUP) // GROUP, step, (zero, zero))
        for hh, o in enumerate(outs):
            cols = slice(hh * RET_DV, (hh + 1) * RET_DV)
            ret_ref[:, cols] = o
            mu = jnp.sum(o, axis=1, keepdims=True) * (1.0 / RET_DV)
            xc = o - mu
            var = jnp.sum(xc * xc, axis=1, keepdims=True) * (1.0 / RET_DV)
            nrm = xc * lax.rsqrt(var + EPS) * w_ref[:, cols]
            g = g_ref[:, cols]
            rg_ref[:, cols] = (g * _sigmoid(g) * nrm).astype(BF16)

    pw = 2 * RET_DV
    blk = lambda hp, i: (i, hp)
    return _call("ret_fwd", body, (n_pair, s // t),
                 [(log_gamma, None, pltpu.SMEM),
                  (qk_rot, (t, LANES), blk),
                  (qk_rot, (s, LANES), lambda hp, i: (0, n_pair + hp)),
                  (v_bf, (s, pw), lambda hp, i: (0, hp)),
                  (proj, (t, pw), lambda hp, i: (i, gate_off + hp)),
                  (gn_g, (1, pw), lambda hp, i: (0, hp))],
                 [((s, RET_V), F32, (t, pw), blk), ((s, RET_V), BF16, (t, pw), blk)], riders=riders)


def _sb_scores(qs, k_ref, i, js, t, tri2):
    row = jnp.bitwise_and(lax.broadcasted_iota(jnp.int32, (2 * t, t), 0), t - 1) + i * t
    col = lax.broadcasted_iota(jnp.int32, (2 * t, t), 1)
    zs = [_dot(qs, k_ref[pl.ds(pl.multiple_of(j * t, t), t), :], _NT) for j in js]
    valids = [(col + j * t) < row for j in js]
    sps = [jnp.maximum(z, 0.0) + jnp.log(1.0 + jnp.exp(-jnp.abs(z))) for z in zs]
    log_1ms = [jnp.where(v, -sp, 0.0) for v, sp in zip(valids, sps)]
    log_bs = [z - sp for z, sp in zip(zs, sps)]
    sticks = [lax.dot_general(_split_bf16(l), tri2, _NN, preferred_element_type=F32) for l in log_1ms]
    sums = [jnp.sum(l, axis=1, keepdims=True) for l in log_1ms]
    return log_1ms, log_bs, sticks, valids, sums


def _sb_fwd(qkv, t, riders=None):
    s = qkv.shape[0]
    n_pair = HEADS // 2
    assert s // t <= LANES and (s // t) % GROUP == 0 and t & (t - 1) == 0

    def body(q_ref, k_ref, v_ref, o_ref, carry_ref):
        i = pl.program_id(1)
        r = lax.broadcasted_iota(jnp.int32, (2 * t, t), 0)
        cc = lax.broadcasted_iota(jnp.int32, (2 * t, t), 1)
        upper2 = (jnp.bitwise_and(r, t - 1) > cc).astype(BF16)
        lane = lax.broadcasted_iota(jnp.int32, (1, LANES), 1)
        qs = _stack_heads(q_ref[...])
        carry_ref[...] = jnp.zeros_like(carry_ref)
        n_groups = (i + GROUP) // GROUP

        def step(n, carry):
            c, o = carry
            g = n_groups - 1 - n
            js = [g * GROUP + sub for sub in range(GROUP)]
            _, log_bs, sticks, valids, sums = _sb_scores(qs, k_ref, i, js, t, upper2)
            cs = [None] * GROUP
            for sub in reversed(range(GROUP)):
                cs[sub] = c
                c = c + sums[sub]
            for sub, j in enumerate(js):
                a = jnp.where(valids[sub], jnp.exp(log_bs[sub] + sticks[sub] + cs[sub]), 0.0)
                vj = v_ref[pl.ds(pl.multiple_of(j * t, t), t), :]
                o = o + _dot(_side_by_side(a.astype(BF16), t), _stack_heads(vj))
            for hh in range(2):
                cols = slice(hh * LANES, (hh + 1) * LANES)
                cm = carry_ref[:, cols]
                for sub, j in enumerate(js):
                    cm = jnp.where(lane == j, cs[sub][hh * t:(hh + 1) * t], cm)
                carry_ref[:, cols] = cm
            return c, o

        _, acc = lax.fori_loop(0, n_groups, step, (jnp.zeros((2 * t, 1), F32), jnp.zeros((t, LANES), F32)))
        o_ref[...] = acc

    return _call("sb_fwd", body, (n_pair, s // t),
                 [(qkv, (t, LANES), lambda hp, i: (i, hp)),
                  (qkv, (s, LANES), lambda hp, i: (0, n_pair + hp)),
                  (qkv, (s, LANES), lambda hp, i: (0, 2 * n_pair + hp))],
                 [((s, SB_W), F32, (t, LANES), lambda hp, i: (i, hp)),
                  ((s, HEADS * LANES), F32, (t, 2 * LANES), lambda hp, i: (i, hp))], riders=riders)


def _merge(retg, sb, w_ret, w_sb, proj, tm, tn):
    s, d = retg.shape[0], w_ret.shape[1]
    ar_off = (2 * RET_QK + 2 * RET_V + 3 * SB_W) // tn
    as_off = ar_off + d // tn

    def body(rg_ref, sb_ref, wr_ref, ws_ref, ar_ref, as_ref, mix_ref, r_ref, s_ref):
        rr = _dot(rg_ref[...], wr_ref[...])
        ss = _dot(sb_ref[...], ws_ref[...])
        mix_ref[...] = (_sigmoid(ar_ref[...]) * rr + _sigmoid(as_ref[...]) * ss).astype(BF16)
        r_ref[...] = rr.astype(BF16)
        s_ref[...] = ss.astype(BF16)

    tile = (tm, tn)
    return _call("merge", body, (d // tn, s // tm),
                 [(retg, (tm, RET_V), lambda j, i: (i, 0)), (sb, (tm, SB_W), lambda j, i: (i, 0)),
                  (w_ret, (RET_V, tn), lambda j, i: (0, j)), (w_sb, (SB_W, tn), lambda j, i: (0, j)),
                  (proj, tile, lambda j, i: (i, ar_off + j)), (proj, tile, lambda j, i: (i, as_off + j))],
                 [((s, d), BF16, tile, lambda j, i: (i, j))] * 3)


def _out_proj(mixed, w_out, x, mod, gp1, g2, tm):
    s, d = x.shape

    def body(a_ref, w_ref, x_ref, mod_ref, gp_ref, g2_ref, y_ref, hres_ref, h2_ref):
        y = _dot(a_ref[...], w_ref[...])
        y_ref[...] = y
        ny, _ = _rms(y, d)
        hres = x_ref[...] + mod_ref[:, 2 * d:3 * d] * (ny * gp_ref[...])
        hres_ref[...] = hres
        n2, _ = _rms(hres, d)
        h2_ref[...] = (n2 * g2_ref[...] * (1.0 + mod_ref[:, 4 * d:5 * d]) + mod_ref[:, 3 * d:4 * d]).astype(BF16)

    row = lambda i: (i, 0)
    fix = lambda i: (0, 0)
    return _call("out_proj", body, (s // tm,),
                 [(mixed, (tm, d), row), (w_out, (d, d), fix), (x, (tm, d), row),
                  (mod, (1, 6 * d), fix), (gp1, (1, d), fix), (g2, (1, d), fix)],
                 [((s, d), F32, (tm, d), row), ((s, d), F32, (tm, d), row), ((s, d), BF16, (tm, d), row)])


def _ff1(h2, w_ff1, tm, tn):
    s, f = h2.shape[0], w_ff1.shape[1]
    tm = min(tm, s)

    def body(a_ref, w_ref, u_ref, act_ref):
        u = _dot(a_ref[...], w_ref[...])
        r = jnp.maximum(u, 0.0)
        u_ref[...] = u.astype(BF16)
        act_ref[...] = (r * r).astype(BF16)

    d = h2.shape[1]
    return _call("ff1", body, (f // tn, s // tm),
                 [(h2, (tm, d), lambda j, i: (i, 0)), (w_ff1, (d, tn), lambda j, i: (0, j))],
                 [((s, f), BF16, (tm, tn), lambda j, i: (i, j))] * 2)


def _ff2_loss(act, w_ff2, hres, target, mod, gp2, tm):
    s, d = hres.shape
    f = act.shape[1]

    def body(a_ref, w_ref, h_ref, t_ref, mod_ref, gp_ref, dout_ref, df_ref, loss_ref, dgt_ref, dgp_ref):
        first = pl.program_id(0) == 0
        ff = _dot(a_ref[...], w_ref[...])
        nf, rf = _rms(ff, d)
        gt, gp = mod_ref[:, 5 * d:6 * d], gp_ref[...]
        out = h_ref[...] + gt * (nf * gp)
        err = out - t_ref[...]
        sq = jnp.sum(err * err, axis=1, keepdims=True)
        _accum(loss_ref, jnp.sum(sq, axis=0, keepdims=True), first)
        dout = err * (1.0 / d)
        dout_ref[...] = dout
        _accum(dgt_ref, _colsum(dout * (nf * gp)), first)
        _accum(dgp_ref, _colsum(dout * gt * nf), first)
        df_ref[...] = _rms_bwd(dout * gt * gp, nf, rf, d).astype(BF16)

    row = lambda i: (i, 0)
    fix = lambda i: (0, 0)
    return _call("ff2_loss", body, (s // tm,),
                 [(act, (tm, f), row), (w_ff2, (f, d), fix), (hres, (tm, d), row), (target, (tm, d), row),
                  (mod, (1, 6 * d), fix), (gp2, (1, d), fix)],
                 [((s, d), F32, (tm, d), row), ((s, d), BF16, (tm, d), row), ((1, 1), F32, (1, 1), fix),
                  ((1, d), F32, (1, d), fix), ((1, d), F32, (1, d), fix)])


def _ff2_bwd(df, w_ff2, u, tm, tn):
    s, d = df.shape
    f = w_ff2.shape[0]
    tm = min(tm, s)

    def body(a_ref, w_ref, u_ref, du_ref):
        da = _dot(a_ref[...], w_ref[...], _NT)
        du_ref[...] = (da * (2.0 * jnp.maximum(u_ref[...].astype(F32), 0.0))).astype(BF16)

    return _call("ff2_bwd", body, (f // tn, s // tm),
                 [(df, (tm, d), lambda j, i: (i, 0)), (w_ff2, (tn, d), lambda j, i: (j, 0)),
                  (u, (tm, tn), lambda j, i: (i, j))],
                 [((s, f), BF16, (tm, tn), lambda j, i: (i, j))])[0]


def _ff1_bwd(du, w_ff1, hres, dout, y, mod, g2, gp1, tm):
    s, d = hres.shape
    f = du.shape[1]

    def body(a_ref, w_ref, h_ref, do_ref, y_ref, mod_ref, g2_ref, gp_ref,
             dh_ref, dy_ref, dsh_ref, dsc_ref, dg2_ref, dgt_ref, dgp_ref):
        first = pl.program_id(0) == 0
        dh2 = _dot(a_ref[...], w_ref[...], _NT)
        n2, r2 = _rms(h_ref[...], d)
        g2, sc2 = g2_ref[...], mod_ref[:, 4 * d:5 * d]
        _accum(dsh_ref, _colsum(dh2), first)
        _accum(dsc_ref, _colsum(dh2 * n2 * g2), first)
        _accum(dg2_ref, _colsum(dh2 * n2 * (1.0 + sc2)), first)
        dhres = do_ref[...] + _rms_bwd(dh2 * g2 * (1.0 + sc2), n2, r2, d)
        dh_ref[...] = dhres
        ny, ry = _rms(y_ref[...], d)
        gt, gp = mod_ref[:, 2 * d:3 * d], gp_ref[...]
        _accum(dgt_ref, _colsum(dhres * (ny * gp)), first)
        _accum(dgp_ref, _colsum(dhres * gt * ny), first)
        dy_ref[...] = _rms_bwd(dhres * gt * gp, ny, ry, d).astype(BF16)

    row = lambda i: (i, 0)
    fix = lambda i: (0, 0)
    vec = ((1, d), F32, (1, d), fix)
    return _call("ff1_bwd", body, (s // tm,),
                 [(du, (tm, f), row), (w_ff1, (d, f), fix), (hres, (tm, d), row), (dout, (tm, d), row),
                  (y, (tm, d), row), (mod, (1, 6 * d), fix), (g2, (1, d), fix), (gp1, (1, d), fix)],
                 [((s, d), F32, (tm, d), row), ((s, d), BF16, (tm, d), row), vec, vec, vec, vec, vec])


def _out_bwd(dy, w_out, proj, r_bf, s_bf, tm, tn):
    s, d = dy.shape
    ar_off = (2 * RET_QK + 2 * RET_V + 3 * SB_W) // tn
    as_off = ar_off + d // tn

    def body(a_ref, w_ref, ar_ref, as_ref, r_ref, s_ref, dr_ref, ds_ref, dar_ref, das_ref):
        dm = _dot(a_ref[...], w_ref[...], _NT)
        sr, ss = _sigmoid(ar_ref[...]), _sigmoid(as_ref[...])
        dr_ref[...] = (dm * sr).astype(BF16)
        ds_ref[...] = (dm * ss).astype(BF16)
        dar_ref[...] = (dm * r_ref[...].astype(F32) * sr * (1.0 - sr)).astype(BF16)
        das_ref[...] = (dm * s_ref[...].astype(F32) * ss * (1.0 - ss)).astype(BF16)

    tile = (tm, tn)
    here = lambda j, i: (i, j)
    return _call("out_bwd", body, (d // tn, s // tm),
                 [(dy, (tm, d), lambda j, i: (i, 0)), (w_out, (tn, d), lambda j, i: (j, 0)),
                  (proj, tile, lambda j, i: (i, ar_off + j)), (proj, tile, lambda j, i: (i, as_off + j)),
                  (r_bf, tile, here), (s_bf, tile, here)],
                 [((s, d), BF16, tile, here)] * 4)


def _gn_bwd(dretg, ret, proj, gn_g, tm):
    s = ret.shape[0]
    gate_off = (2 * RET_QK + RET_V) // RET_V

    def body(d_ref, r_ref, g_ref, w_ref, dg_ref, dret_ref, dw_ref):
        first = pl.program_id(0) == 0
        for h in range(HEADS):
            cols = slice(h * RET_DV, (h + 1) * RET_DV)
            o, g, w, dr = r_ref[:, cols], g_ref[:, cols], w_ref[:, cols], d_ref[:, cols]
            mu = jnp.sum(o, axis=1, keepdims=True) * (1.0 / RET_DV)
            xc = o - mu
            rstd = lax.rsqrt(jnp.sum(xc * xc, axis=1, keepdims=True) * (1.0 / RET_DV) + EPS)
            n = xc * rstd
            sg = _sigmoid(g)
            silu = g * sg
            dg_ref[:, cols] = (dr * n * w * (sg * (1.0 + g * (1.0 - sg)))).astype(BF16)
            _accum(dw_ref.at[:, cols], _colsum(dr * silu * n), first)
            dn = dr * silu * w
            m1 = jnp.sum(dn, axis=1, keepdims=True) * (1.0 / RET_DV)
            m2 = jnp.sum(dn * n, axis=1, keepdims=True) * (1.0 / RET_DV)
            dret_ref[:, cols] = (rstd * (dn - m1 - n * m2)).astype(BF16)

    row = lambda i: (i, 0)
    fix = lambda i: (0, 0)
    return _call("gn_bwd", body, (s // tm,),
                 [(dretg, (tm, RET_V), row), (ret, (tm, RET_V), row),
                  (proj, (tm, RET_V), lambda i: (i, gate_off)), (gn_g, (1, RET_V), fix)],
                 [((s, RET_V), BF16, (tm, RET_V), row), ((s, RET_V), BF16, (tm, RET_V), row),
                  ((1, RET_V), F32, (1, RET_V), fix)])


def _ret_bwd(qk_rot, v_bf, dret, log_gamma, t, riders=None):
    s = qk_rot.shape[0]
    n_pair = HEADS // 2
    pw = 2 * RET_DV

    def body(lg_ref, q_ref, k_ref, v_ref, do_ref, dq_ref, dk_ref, dv_ref):
        hp, i = pl.program_id(0), pl.program_id(1)

        @pl.when(i == 0)
        def _():
            dk_ref[...] = jnp.zeros_like(dk_ref)
            dv_ref[...] = jnp.zeros_like(dv_ref)

        qs = _stack_heads(q_ref[...])
        lg_rows = _lg_rows(lg_ref, hp, t)
        do0, do1 = do_ref[:, 0:RET_DV], do_ref[:, RET_DV:pw]

        def step(g, dq):
            js = [g * GROUP + sub for sub in range(GROUP)]
            rows = [pl.ds(pl.multiple_of(j * t, t), t) for j in js]
            ss = [_dot(qs, k_ref[rw, :], _NT) for rw in rows]
            dps = [jnp.concatenate([_dot(do0, v_ref[rw, 0:RET_DV], _NT), _dot(do1, v_ref[rw, RET_DV:pw], _NT)], axis=0)
                   for rw in rows]
            decs = [_ret_decay(lg_rows, i, j, t) for j in js]
            ps = [(sc * dec).astype(BF16) for sc, dec in zip(ss, decs)]
            dss = [(dp * dec).astype(BF16) for dp, dec in zip(dps, decs)]
            for p, ds, rw in zip(ps, dss, rows):
                dv_ref[rw, 0:RET_DV] += _dot(p[:t], do0, _TN)
                dv_ref[rw, RET_DV:pw] += _dot(p[t:], do1, _TN)
                dk_ref[rw, :] += _dot(ds, qs, _TN)
                dq = dq + _dot(_side_by_side(ds, t), _stack_heads(k_ref[rw, :]))
            return dq

        dq_ref[...] = lax.fori_loop(0, (i + GROUP) // GROUP, step, jnp.zeros((t, LANES), F32))

    blk = lambda hp, i: (i, hp)
    return _call("ret_bwd", body, (n_pair, s // t),
                 [(log_gamma, None, pltpu.SMEM),
                  (qk_rot, (t, LANES), blk),
                  (qk_rot, (s, LANES), lambda hp, i: (0, n_pair + hp)),
                  (v_bf, (s, pw), lambda hp, i: (0, hp)),
                  (dret, (t, pw), blk)],
                 [((s, RET_QK), F32, (t, LANES), blk),
                  ((s, RET_QK), F32, (s, LANES), lambda hp, i: (0, hp)),
                  ((s, RET_V), F32, (s, pw), lambda hp, i: (0, hp))], riders=riders)


def _sb_bwd(qkv, carries, do, t, riders=None):
    s = qkv.shape[0]
    n_pair = HEADS // 2

    def body(q_ref, k_ref, v_ref, c_ref, do_ref, dq_ref, dk_ref, dv_ref):
        i = pl.program_id(1)

        @pl.when(i == 0)
        def _():
            dk_ref[...] = jnp.zeros_like(dk_ref)
            dv_ref[...] = jnp.zeros_like(dv_ref)

        r = jnp.bitwise_and(lax.broadcasted_iota(jnp.int32, (2 * t, t), 0), t - 1)
        cc = lax.broadcasted_iota(jnp.int32, (2 * t, t), 1)
        upper2 = (r > cc).astype(BF16)
        lower2 = (r < cc).astype(BF16)
        lane = lax.broadcasted_iota(jnp.int32, (1, LANES), 1)
        qs = _stack_heads(q_ref[...])
        dos = _stack_heads(do_ref[...].astype(BF16))
        cms = jnp.concatenate([c_ref[:, 0:LANES], c_ref[:, LANES:2 * LANES]], axis=0)
        n_groups = (i + GROUP) // GROUP

        def step(g, carry):
            c_e, dq = carry
            js = [g * GROUP + sub for sub in range(GROUP)]
            rows = [pl.ds(pl.multiple_of(j * t, t), t) for j in js]
            _, log_bs, sticks, valids, _ = _sb_scores(qs, k_ref, i, js, t, upper2)
            das = [_dot(dos, v_ref[rw, :], _NT) for rw in rows]
            c_sticks = [jnp.sum(jnp.where(lane == j, cms, 0.0), axis=1, keepdims=True) for j in js]
            avals = [jnp.where(v, jnp.exp(lb + st + cst), 0.0)
                     for v, lb, st, cst in zip(valids, log_bs, sticks, c_sticks)]
            es = [a * da for a, da in zip(avals, das)]
            prefixes = [lax.dot_general(_split_bf16(e), lower2, _NN, preferred_element_type=F32) for e in es]
            betas = [jnp.exp(lb) for lb in log_bs]
            for sub in range(GROUP):
                dv_ref[rows[sub], :] += _dot(avals[sub], dos, _TN)
            for sub in range(GROUP):
                dz = jnp.where(valids[sub], es[sub] * (1.0 - betas[sub]) - (prefixes[sub] + c_e) * betas[sub],
                               0.0).astype(BF16)
                dk_ref[rows[sub], :] += _dot(dz, qs, _TN)
                dq = dq + _dot(_side_by_side(dz, t), _stack_heads(k_ref[rows[sub], :]))
                c_e = c_e + jnp.sum(es[sub], axis=1, keepdims=True)
            return c_e, dq

        _, dq = lax.fori_loop(0, n_groups, step, (jnp.zeros((2 * t, 1), F32), jnp.zeros((t, LANES), F32)))
        dq_ref[...] = dq

    blk = lambda hp, i: (i, hp)
    return _call("sb_bwd", body, (n_pair, s // t),
                 [(qkv, (t, LANES), blk),
                  (qkv, (s, LANES), lambda hp, i: (0, n_pair + hp)),
                  (qkv, (s, LANES), lambda hp, i: (0, 2 * n_pair + hp)),
                  (carries, (t, 2 * LANES), blk), (do, (t, LANES), blk)],
                 [((s, SB_W), F32, (t, LANES), blk),
                  ((s, SB_W), F32, (s, LANES), lambda hp, i: (0, hp)),
                  ((s, SB_W), F32, (s, LANES), lambda hp, i: (0, hp))], riders=riders)


def _assemble_dproj(dq_r, dk_r, dv_r, dg_r, dq_s, dk_s, dv_s, da_r, da_s, cos, sin, tm):
    s, d = da_r.shape
    width = 2 * RET_QK + 2 * RET_V + 3 * SB_W + 2 * d

    def body(dq_ref, dk_ref, dv_ref, dg_ref, dqs_ref, dks_ref, dvs_ref, dar_ref, das_ref, cos_ref, sin_ref, o_ref):
        lane = lax.broadcasted_iota(jnp.int32, (1, LANES), 1)
        first = jnp.bitwise_and(lane, RET_DQK - 1) < (RET_DQK // 2)
        cos, sin = cos_ref[...], sin_ref[...]
        for src, base, scale in ((dq_ref, 0, 1.0), (dk_ref, RET_QK, RET_DQK ** -0.5)):
            for g in range(RET_QK // LANES):
                v = src[:, g * LANES:(g + 1) * LANES]
                sw = jnp.where(first, pltpu.roll(v, LANES - RET_DQK // 2, 1), pltpu.roll(v, RET_DQK // 2, 1))
                o_ref[:, base + g * LANES:base + (g + 1) * LANES] = ((v * cos - sw * sin) * scale).astype(BF16)
        off = 2 * RET_QK
        o_ref[:, off:off + RET_V] = dv_ref[...].astype(BF16)
        off += RET_V
        o_ref[:, off:off + RET_V] = dg_ref[...]
        off += RET_V
        o_ref[:, off:off + SB_W] = (dqs_ref[...] * (SB_DH ** -0.5)).astype(BF16)
        off += SB_W
        o_ref[:, off:off + SB_W] = dks_ref[...].astype(BF16)
        off += SB_W
        o_ref[:, off:off + SB_W] = dvs_ref[...].astype(BF16)
        off += SB_W
        o_ref[:, off:off + d] = dar_ref[...]
        off += d
        o_ref[:, off:off + d] = das_ref[...]

    row = lambda i: (i, 0)
    ins = [(a, (tm, a.shape[1]), row) for a in (dq_r, dk_r, dv_r, dg_r, dq_s, dk_s, dv_s, da_r, da_s, cos, sin)]
    return _call("assemble_dproj", body, (s // tm,), ins, [((s, width), BF16, (tm, width), row)])[0]


def _in_bwd(dproj, w_in, x, dhres, mod, g1, tm):
    s, d = x.shape
    width = dproj.shape[1]

    def body(a_ref, w_ref, x_ref, dh_ref, mod_ref, g_ref, dx_ref, dsh_ref, dsc_ref, dg_ref):
        first = pl.program_id(0) == 0
        dh = _dot(a_ref[...], w_ref[...], _NT)
        n1, r1 = _rms(x_ref[...], d)
        g1, sc1 = g_ref[...], mod_ref[:, d:2 * d]
        _accum(dsh_ref, _colsum(dh), first)
        _accum(dsc_ref, _colsum(dh * n1 * g1), first)
        _accum(dg_ref, _colsum(dh * n1 * (1.0 + sc1)), first)
        dx_ref[...] = dh_ref[...] + _rms_bwd(dh * g1 * (1.0 + sc1), n1, r1, d)

    row = lambda i: (i, 0)
    fix = lambda i: (0, 0)
    vec = ((1, d), F32, (1, d), fix)
    return _call("in_bwd", body, (s // tm,),
                 [(dproj, (tm, width), row), (w_in, (d, width), fix), (x, (tm, d), row), (dhres, (tm, d), row),
                  (mod, (1, 6 * d), fix), (g1, (1, d), fix)],
                 [((s, d), F32, (tm, d), row), vec, vec, vec])


def _adamw(w, g, m, v):
    m = ADAM_B1 * m + (1.0 - ADAM_B1) * g
    v = ADAM_B2 * v + (1.0 - ADAM_B2) * (g * g)
    m_hat = m / (1.0 - ADAM_B1 ** ADAM_STEP)
    v_hat = v / (1.0 - ADAM_B2 ** ADAM_STEP)
    delta = -ADAM_LR * (m_hat / (jnp.sqrt(v_hat) + ADAM_EPS) + ADAM_WD * w)
    return delta, m, v


def _adam_reduce(name, parts, w, m, v, tr):
    rws, cls = w.shape
    tr = min(tr, rws)

    def body(p_ref, w_ref, m_ref, v_ref, g_out, d_out, m_out, v_out):
        g = p_ref[0].astype(F32)
        for k in range(1, N_DEV):
            g = g + p_ref[k].astype(F32)
        delta, mn, vn = _adamw(w_ref[...], g, m_ref[...], v_ref[...])
        g_out[...] = g
        d_out[...] = delta
        m_out[...] = mn
        v_out[...] = vn

    row = lambda i: (i, 0)
    blk = (tr, cls)
    return _call(name, body, (rws // tr,),
                 [(parts, (N_DEV, tr, cls), lambda i: (0, i, 0)), (w, blk, row), (m, blk, row), (v, blk, row)],
                 [((rws, cls), F32, blk, row)] * 4)


def _ada_bwd_adam(cs_t, dmod_cols, w, m, v):
    d, nc = w.shape

    def body(c_ref, dm_ref, w_ref, m_ref, v_ref, g_out, d_out, m_out, v_out):
        g = c_ref[0] * dm_ref[0:1, :]
        for r in range(1, N_DEV):
            g = g + c_ref[r] * dm_ref[r:r + 1, :]
        delta, mn, vn = _adamw(w_ref[...], g, m_ref[...], v_ref[...])
        g_out[...] = g
        d_out[...] = delta
        m_out[...] = mn
        v_out[...] = vn

    fix = lambda i: (0, 0)
    blk = (d, nc)
    return _call("ada_bwd_adam", body, (1,),
                 [(cs_t, (N_DEV, d, 1), lambda i: (0, 0, 0)), (dmod_cols, (N_DEV, nc), fix), (w, blk, fix), (m, blk, fix), (v, blk, fix)],
                 [((d, nc), F32, blk, fix)] * 4)


def _small_adam(parts, w, m, v):
    n = w.shape[1]

    def body(p_ref, w_ref, m_ref, v_ref, g_out, d_out, m_out, v_out):
        g = p_ref[0:1, :]
        for k in range(1, N_DEV):
            g = g + p_ref[k:k + 1, :]
        delta, mn, vn = _adamw(w_ref[...], g, m_ref[...], v_ref[...])
        g_out[...] = g
        d_out[...] = delta
        m_out[...] = mn
        v_out[...] = vn

    fix = lambda i: (0, 0)
    return _call("small_adam", body, (1,),
                 [(parts, (N_DEV, n), fix), (w, (1, n), fix), (m, (1, n), fix), (v, (1, n), fix)],
                 [((1, n), F32, (1, n), fix)] * 4)


def kernel(x, c, positions, ada_w, ada_b, pre_mix_g, post_mix_g, pre_ffn_g, post_ffn_g, w_in, ret_gn_g, w_ret_branch, w_sb_branch, w_out, w_ff1, w_ff2, loss_target, m_ada_w, m_ada_b, m_pre_mix_g, m_post_mix_g, m_pre_ffn_g, m_post_ffn_g, m_w_in, m_ret_gn_g, m_w_ret_branch, m_w_sb_branch, m_w_out, m_w_ff1, m_w_ff2, v_ada_w, v_ada_b, v_pre_mix_g, v_post_mix_g, v_pre_ffn_g, v_post_ffn_g, v_w_in, v_ret_gn_g, v_w_ret_branch, v_w_sb_branch, v_w_out, v_w_ff1, v_w_ff2):
    _, s, d = x.shape
    d_ff = w_ff1.shape[2] * N_DEV
    d_in = w_in.shape[2] * N_DEV
    me = 4 * lax.axis_index("x") + 2 * lax.axis_index("y") + lax.axis_index("c")
    x2, tgt = x[0], loss_target[0]

    c_all, g_in = _exchange("gather_in", [c, w_in[0].astype(BF16)], scatter=False)
    c_all = c_all.reshape(N_DEV, d)
    wf_in = jnp.moveaxis(g_in, 0, 1).reshape(d, d_in)

    n_ada = ada_w.shape[2]
    cs_all = _silu_rows(c_all)
    ada_b_cols = lax.dynamic_slice(ada_b, (0, me * n_ada), (1, n_ada))
    mod_cols = _ada_fwd(cs_all, ada_w[0], ada_b_cols)
    mod_all = _exchange("gather_mod", [mod_cols], scatter=False)[0]
    mod = lax.dynamic_index_in_dim(mod_all, me, axis=1, keepdims=False).reshape(1, 6 * d)

    tm = min(256, s)
    h = _pre_norm(x2, pre_mix_g, mod, tm)
    proj = _matmul("in_proj", h, wf_in, "nn", 1024, 512, F32)
    pos_col = positions.reshape(s, 1).astype(F32)
    freqs = ROPE_BASE ** (-jnp.arange(0, RET_DQK, 2, dtype=F32) / RET_DQK)
    inv_freq = jnp.tile(freqs, LANES // (RET_DQK // 2)).reshape(1, LANES)
    qk_rot, v_bf, qkv_sb, cos_t, sin_t = _prep(proj, pos_col, inv_freq, tm)
    log_gamma = jnp.asarray(np.log1p(-(2.0 ** (-5.0 - np.arange(HEADS)))), F32)
    t_ret = min(128, s)
    t_sb = min(128, s)
    bf = lambda w: w[0].astype(BF16)
    ret, retg, g_ret, g_sb, g_out = _ret_fwd(qk_rot, v_bf, proj, ret_gn_g, log_gamma, t_ret,
                                             riders=([bf(w_ret_branch), bf(w_sb_branch), bf(w_out)], False))
    sb, sb_carry, g_ff1, g_ff2 = _sb_fwd(qkv_sb, t_sb, riders=([bf(w_ff1), bf(w_ff2)], False))
    wf_ret = g_ret.reshape(RET_V, d)
    wf_sb = jnp.moveaxis(g_sb, 0, 1).reshape(SB_W, d)
    wf_out = g_out.reshape(d, d)
    wf_ff1 = jnp.moveaxis(g_ff1, 0, 1).reshape(d, d_ff)
    wf_ff2 = g_ff2.reshape(d_ff, d)
    mixed, r_bf, s_bf = _merge(retg, sb, wf_ret, wf_sb, proj, tm, min(512, d))
    y, hres, h2 = _out_proj(mixed, wf_out, x2, mod, post_mix_g, pre_ffn_g, tm)
    u, act = _ff1(h2, wf_ff1, 1024, 512)
    dout, df, loss_sum, d_gt2, d_gp2 = _ff2_loss(act, wf_ff2, hres, tgt, mod, post_ffn_g, tm)

    du = _ff2_bwd(df, wf_ff2, u, 1024, 512)
    gw_ff2 = _matmul("grad_w_ff2", act, df, "tn", 512, 512, BF16).reshape(N_DEV, d_ff // N_DEV, d)
    gw_ff1 = _matmul("grad_w_ff1", h2, du, "tn", 512, d_ff // N_DEV, BF16, blocked_out=True)
    dhres, dy, d_sh2, d_sc2, d_g2, d_gt1, d_gp1 = _ff1_bwd(du, wf_ff1, hres, dout, y, mod, pre_ffn_g, post_mix_g, tm)
    d_r, d_s, da_r, da_s = _out_bwd(dy, wf_out, proj, r_bf, s_bf, tm, min(512, d))
    gw_out = _matmul("grad_w_out", mixed, dy, "tn", 512, 512, BF16).reshape(N_DEV, d // N_DEV, d)
    dretg = _matmul("ret_branch_bwd", d_r, wf_ret, "nt", 1024, 512, BF16)
    dsb = _matmul("sb_branch_bwd", d_s, wf_sb, "nt", 1024, 512, F32)
    gw_ret = _matmul("grad_w_ret", retg, d_r, "tn", 512, 512, BF16).reshape(N_DEV, RET_V // N_DEV, d)
    gw_sb = _matmul("grad_w_sb", sb, d_s, "tn", 512, d // N_DEV, BF16, blocked_out=True)
    dg_r, dret, d_gn = _gn_bwd(dretg, ret, proj, ret_gn_g, tm)
    dq_r, dk_r, dv_r, p_out, p_ret, p_sb = _ret_bwd(qk_rot, v_bf, dret, log_gamma, t_ret,
                                                    riders=([gw_out, gw_ret, gw_sb], True))
    dq_s, dk_s, dv_s, p_ff1, p_ff2 = _sb_bwd(qkv_sb, sb_carry, dsb, t_sb, riders=([gw_ff1, gw_ff2], True))
    dproj = _assemble_dproj(dq_r, dk_r, dv_r, dg_r, dq_s, dk_s, dv_s, da_r, da_s, cos_t, sin_t, tm)
    gw_in_full = _matmul("grad_w_in", h, dproj, "tn", 512, 512, BF16)
    gw_in = jnp.moveaxis(gw_in_full.reshape(d, N_DEV, d_in // N_DEV), 1, 0)
    grad_x, d_sh1, d_sc1, d_g1 = _in_bwd(dproj, wf_in, x2, dhres, mod, pre_mix_g, tm)

    small = jnp.concatenate([d_sh1, d_sc1, d_gt1, d_sh2, d_sc2, d_gt2, d_g1, d_gp1, d_g2, d_gp2, d_gn], axis=1)
    small_all, p_in = _exchange("exchange_last", [small, gw_in], scatter=[False, True])
    small_all = small_all.reshape(N_DEV, small.shape[1])
    parts = [p_in, p_ret, p_sb, p_out, p_ff1, p_ff2]

    res = {}
    names = ["w_in", "w_ret_branch", "w_sb_branch", "w_out", "w_ff1", "w_ff2"]
    ws = [w_in, w_ret_branch, w_sb_branch, w_out, w_ff1, w_ff2]
    ms = [m_w_in, m_w_ret_branch, m_w_sb_branch, m_w_out, m_w_ff1, m_w_ff2]
    vs = [v_w_in, v_w_ret_branch, v_w_sb_branch, v_w_out, v_w_ff1, v_w_ff2]
    for nm, p, w, m, v in zip(names, parts, ws, ms, vs):
        res[nm] = [o[None] for o in _adam_reduce("adam_" + nm, p, w[0], m[0], v[0], 256)]
    dmod_cols = lax.dynamic_slice(small_all, (0, me * n_ada), (N_DEV, n_ada))
    res["ada_w"] = [o[None] for o in _ada_bwd_adam(cs_all.reshape(N_DEV, d, 1), dmod_cols, ada_w[0], m_ada_w[0], v_ada_w[0])]
    vec_names = ["ada_b", "pre_mix_g", "post_mix_g", "pre_ffn_g", "post_ffn_g", "ret_gn_g"]
    cat = lambda xs: jnp.concatenate(xs, axis=1)
    packed = _small_adam(small_all,
                         cat([ada_b, pre_mix_g, post_mix_g, pre_ffn_g, post_ffn_g, ret_gn_g]),
                         cat([m_ada_b, m_pre_mix_g, m_post_mix_g, m_pre_ffn_g, m_post_ffn_g, m_ret_gn_g]),
                         cat([v_ada_b, v_pre_mix_g, v_post_mix_g, v_pre_ffn_g, v_post_ffn_g, v_ret_gn_g]))
    off = 0
    for nm, width in zip(vec_names, [6 * d, d, d, d, d, RET_V]):
        res[nm] = [p[:, off:off + width] for p in packed]
        off += width

    loss = (0.5 / d) * lax.psum(loss_sum[0, 0], AXES)
    order = ["ada_w", "ada_b", "pre_mix_g", "post_mix_g", "pre_ffn_g", "post_ffn_g", "w_in", "ret_gn_g",
             "w_ret_branch", "w_sb_branch", "w_out", "w_ff1", "w_ff2"]
    outs = [loss, grad_x[None]]
    for k in range(4):
        outs += [res[nm][k] for nm in order]
    return tuple(outs)
```

```python
import functools

import numpy as np
import jax
import jax.numpy as jnp
from jax import lax
from jax.experimental import pallas as pl
from jax.experimental.pallas import tpu as pltpu

F32 = jnp.float32
BF16 = jnp.bfloat16
N_DEV = 8
AXES = ("x", "y", "c")

EPS = 1e-6
CHUNK = 64
CHUNK_SHIFT = 6
HEADS = 8
RET_DQK = 64
RET_DV = 128
SB_DH = 64
RET_QK = HEADS * RET_DQK
RET_V = HEADS * RET_DV
SB_W = HEADS * SB_DH
ROPE_BASE = 10000.0
LANES = 128

ADAM_LR = 0.001
ADAM_B1 = 0.9
ADAM_B2 = 0.999
ADAM_EPS = 1e-08
ADAM_WD = 0.01
ADAM_STEP = 10

VMEM_LIMIT = 56 * 1024 * 1024

_NN = (((1,), (0,)), ((), ()))
_NT = (((1,), (1,)), ((), ()))
_TN = (((0,), (0,)), ((), ()))


def _dot(a, b, dims=_NN):
    if a.dtype != BF16:
        a = a.astype(BF16)
    if b.dtype != BF16:
        b = b.astype(BF16)
    return lax.dot_general(a, b, dims, preferred_element_type=F32)


def _dot_split(a, b):
    hi = a.astype(BF16)
    lo = (a - hi.astype(F32)).astype(BF16)
    return (lax.dot_general(hi, b, _NN, preferred_element_type=F32)
            + lax.dot_general(lo, b, _NN, preferred_element_type=F32))


def _sigmoid(x):
    return 1.0 / (1.0 + jnp.exp(-x))


def _rms(x, d):
    r = lax.rsqrt(jnp.sum(x * x, axis=1, keepdims=True) * (1.0 / d) + EPS)
    return x * r, r


def _rms_bwd(dn, n, r, d):
    return r * (dn - n * (jnp.sum(dn * n, axis=1, keepdims=True) * (1.0 / d)))


def _colsum(v):
    return jnp.sum(v, axis=0, keepdims=True)


def _accum(ref, val, first):
    @pl.when(first)
    def _():
        ref[...] = val

    @pl.when(jnp.logical_not(first))
    def _():
        ref[...] += val


KIND_SLOTS = {"gather": N_DEV, "gather_chip": N_DEV, "forward": N_DEV, "pair": N_DEV // 2, "chip_scatter": N_DEV // 2}
SEMS_PER_ARRAY = N_DEV - 1


def _exchange_copies(ins, outs, send_sems, recv_sems, local_sems, kinds):
    x, y, c = (lax.axis_index(a) for a in AXES)
    me, chip, sibling = 4 * x + 2 * y + c, 2 * x + y, (x, y, 1 - c)
    mesh_id = pl.DeviceIdType.MESH
    other_chips = []
    for k in range(1, N_DEV // 2):
        px = 1 - x if k & 2 else x
        py = 1 - y if k & 1 else y
        other_chips.append((px, py))
    copies = []
    for i, kind in enumerate(kinds):
        def remote(src, dst, k, to, i=i):
            return pltpu.make_async_remote_copy(
                src_ref=src, dst_ref=dst, send_sem=send_sems.at[i * SEMS_PER_ARRAY + k],
                recv_sem=recv_sems.at[i * SEMS_PER_ARRAY + k], device_id=to, device_id_type=mesh_id)

        if kind == "gather":
            copies.append(pltpu.make_async_copy(ins[i], outs[i].at[me], local_sems.at[i]))
            for k in range(1, N_DEV):
                to = (1 - x if k & 4 else x, 1 - y if k & 2 else y, 1 - c if k & 1 else c)
                copies.append(remote(ins[i], outs[i].at[me], k - 1, to))
        elif kind == "gather_chip":
            copies.append(pltpu.make_async_copy(ins[i], outs[i].at[me], local_sems.at[i]))
            copies.append(remote(ins[i], outs[i].at[me], 0, sibling))
            for k, (px, py) in enumerate(other_chips):
                copies.append(remote(ins[i], outs[i].at[me], 1 + k, (px, py, c)))
        elif kind == "forward":
            for k, (px, py) in enumerate(other_chips):
                slot = 4 * px + 2 * py + c
                copies.append(remote(outs[i].at[slot], outs[i].at[slot], k, sibling))
        elif kind == "pair":
            for k in range(N_DEV // 2):
                copies.append(remote(ins[i].at[2 * k + 1 - c], outs[i].at[k], k, sibling))
        elif kind == "chip_scatter":
            copies.append(pltpu.make_async_copy(ins[i].at[chip], outs[i].at[chip], local_sems.at[i]))
            for k, (px, py) in enumerate(other_chips):
                copies.append(remote(ins[i].at[2 * px + py], outs[i].at[chip], k, (px, py, c)))
        else:
            raise ValueError(kind)
    return copies


def _exchange_shapes(arrays, kinds):
    shapes = []
    for a, kind in zip(arrays, kinds):
        tail = a.shape if kind in ("gather", "gather_chip") else a.shape[1:]
        shapes.append(jax.ShapeDtypeStruct((KIND_SLOTS[kind],) + tuple(tail), a.dtype))
    return shapes


def _exchange_sems(n):
    return [pltpu.SemaphoreType.DMA((n * SEMS_PER_ARRAY,)), pltpu.SemaphoreType.DMA((n * SEMS_PER_ARRAY,)),
            pltpu.SemaphoreType.DMA((n,))]


def _call(name, body, grid, ins, outs, scratch=(), riders=None, prefetch=None):
    any_spec = pl.BlockSpec(memory_space=pl.ANY)
    in_specs = [pl.BlockSpec(memory_space=im) if bs is None else pl.BlockSpec(bs, im) for _, bs, im in ins]
    out_specs = [pl.BlockSpec(bs, im) for _, _, bs, im in outs]
    out_shape = [jax.ShapeDtypeStruct(s, d) for s, d, _, _ in outs]
    operands = [a for a, _, _ in ins]
    scratch = list(scratch)
    aliases = {}
    n_pre = 0 if prefetch is None else 1
    kernel = functools.partial(body) if prefetch is None else (lambda _, *refs: body(*refs))
    if riders is not None:
        arrays, kinds = riders
        nr, n_in, n_out, n_scr = len(arrays), len(ins), len(outs), len(scratch)

        def kernel(*refs):
            refs = refs[n_pre:]
            own_in, ride_in = refs[:n_in], refs[n_in:n_in + nr]
            own_out = refs[n_in + nr:n_in + nr + n_out]
            ride_out = refs[n_in + nr + n_out:n_in + 2 * nr + n_out]
            own_scr = refs[n_in + 2 * nr + n_out:n_in + 2 * nr + n_out + n_scr]
            sems = refs[n_in + 2 * nr + n_out + n_scr:]
            ids = [pl.program_id(a) for a in range(len(grid))]
            first = functools.reduce(jnp.logical_and, [i == 0 for i in ids])
            last = functools.reduce(jnp.logical_and, [i == g - 1 for i, g in zip(ids, grid)])

            @pl.when(first)
            def _():
                for cp in _exchange_copies(ride_in, ride_out, *sems, kinds):
                    cp.start()

            body(*own_in, *own_out, *own_scr)

            @pl.when(last)
            def _():
                for cp in _exchange_copies(ride_in, ride_out, *sems, kinds):
                    cp.wait()

        in_specs += [any_spec] * nr
        out_specs += [any_spec] * nr
        out_shape += _exchange_shapes(arrays, kinds)
        operands += list(arrays)
        scratch += _exchange_sems(nr)
        aliases = {n_pre + n_in + r: n_out + r for r, kind in enumerate(kinds) if kind == "forward"}
    params = pltpu.CompilerParams(dimension_semantics=("arbitrary",) * len(grid), vmem_limit_bytes=VMEM_LIMIT)
    if prefetch is None:
        return pl.pallas_call(kernel, name=name, grid=grid, in_specs=in_specs, out_specs=out_specs,
                              out_shape=out_shape, scratch_shapes=scratch, input_output_aliases=aliases,
                              compiler_params=params)(*operands)
    grid_spec = pltpu.PrefetchScalarGridSpec(num_scalar_prefetch=1, grid=grid, in_specs=in_specs,
                                             out_specs=out_specs, scratch_shapes=scratch)
    return pl.pallas_call(kernel, name=name, grid_spec=grid_spec, out_shape=out_shape,
                          input_output_aliases=aliases, compiler_params=params)(prefetch, *operands)


def _exchange(name, arrays, kinds):
    n = len(arrays)

    def body(*refs):
        copies = _exchange_copies(refs[:n], refs[n:2 * n], *refs[2 * n:], kinds)
        for cp in copies:
            cp.start()
        for cp in copies:
            cp.wait()

    any_spec = pl.BlockSpec(memory_space=pl.ANY)
    return pl.pallas_call(
        functools.partial(body),
        name=name,
        in_specs=[any_spec] * n,
        out_specs=[any_spec] * n,
        out_shape=_exchange_shapes(arrays, kinds),
        scratch_shapes=_exchange_sems(n),
        input_output_aliases={i: i for i, kind in enumerate(kinds) if kind == "forward"},
    )(*arrays)


def _pair_sum(name, mine, theirs, my_core, tr):
    _, rws, cls = mine.shape
    tr = min(tr, rws)

    def body(a_ref, b_ref, o_ref):
        o_ref[...] = (a_ref[...].astype(F32) + b_ref[...].astype(F32)).astype(o_ref.dtype)

    return _call(name, body, (N_DEV // 2, rws // tr),
                 [(mine, (None, tr, cls), lambda k, r, core: (2 * k + core[0], r, 0)),
                  (theirs, (None, tr, cls), lambda k, r, core: (k, r, 0))],
                 [((N_DEV // 2, rws, cls), mine.dtype, (None, tr, cls), lambda k, r, core: (k, r, 0))],
                 prefetch=my_core)[0]


def _matmul(name, a, b, kind, tm, tn, out_dtype, blocked_out=False):
    if kind == "tn":
        kdim, m = a.shape
    else:
        m, kdim = a.shape
    n = b.shape[0] if kind == "nt" else b.shape[1]
    tm, tn = min(tm, m), min(tn, n)
    dims = {"nn": _NN, "nt": _NT, "tn": _TN}[kind]

    def body(a_ref, b_ref, o_ref):
        o_ref[...] = _dot(a_ref[...], b_ref[...], dims).astype(o_ref.dtype)

    a_spec = (a, (kdim, tm), lambda j, i: (0, i)) if kind == "tn" else (a, (tm, kdim), lambda j, i: (i, 0))
    b_spec = (b, (tn, kdim), lambda j, i: (j, 0)) if kind == "nt" else (b, (kdim, tn), lambda j, i: (0, j))
    if blocked_out:
        out = ((n // tn, m, tn), out_dtype, (None, tm, tn), lambda j, i: (j, i, 0))
    else:
        out = ((m, n), out_dtype, (tm, tn), lambda j, i: (i, j))
    return _call(name, body, (n // tn, m // tm), [a_spec, b_spec], [out])[0]


def _ada_fwd(cs_all, ada_w, ada_b_cols):
    def body(c_ref, w_ref, b_ref, o_ref):
        o_ref[...] = lax.dot_general(c_ref[...], w_ref[...], _NN, preferred_element_type=F32,
                                     precision=lax.Precision.HIGHEST) + b_ref[...]

    r, d = cs_all.shape
    nc = ada_w.shape[1]
    return _call("ada_fwd", body, (1,),
                 [(cs_all, (r, d), lambda i: (0, 0)), (ada_w, (d, nc), lambda i: (0, 0)),
                  (ada_b_cols, (1, nc), lambda i: (0, 0))],
                 [((r, nc), F32, (r, nc), lambda i: (0, 0))])[0]


def _silu_rows(c_all):
    def body(c_ref, o_ref):
        v = c_ref[...]
        o_ref[...] = v * _sigmoid(v)

    return _call("silu_c", body, (1,), [(c_all, c_all.shape, lambda i: (0, 0))],
                 [(c_all.shape, F32, c_all.shape, lambda i: (0, 0))])[0]


def _pre_norm(x, g, mod, tm, riders=None):
    s, d = x.shape

    def body(x_ref, g_ref, mod_ref, h_ref):
        n, _ = _rms(x_ref[...], d)
        sh, sc = mod_ref[:, 0:d], mod_ref[:, d:2 * d]
        h_ref[...] = (n * g_ref[...] * (1.0 + sc) + sh).astype(BF16)

    return _call("pre_norm", body, (s // tm,),
                 [(x, (tm, d), lambda i: (i, 0)), (g, (1, d), lambda i: (0, 0)),
                  (mod, (1, 6 * d), lambda i: (0, 0))],
                 [((s, d), BF16, (tm, d), lambda i: (i, 0))], riders=riders)


def _prep(proj, pos_col, inv_freq, tm):
    s = proj.shape[0]
    sb_off = (2 * RET_QK + 2 * RET_V) // (3 * SB_W)

    def body(qk_ref, v_ref, sb_ref, pos_ref, f_ref, qk_out, v_out, sb_out, cos_out, sin_out):
        ang = pos_ref[...] * f_ref[...]
        lane = lax.broadcasted_iota(jnp.int32, (1, LANES), 1)
        first = jnp.bitwise_and(lane, RET_DQK - 1) < (RET_DQK // 2)
        cos = jnp.cos(ang)
        sin = jnp.where(first, -1.0, 1.0) * jnp.sin(ang)
        cos_out[...] = cos
        sin_out[...] = sin
        for g in range(2 * RET_QK // LANES):
            v = qk_ref[:, g * LANES:(g + 1) * LANES]
            sw = jnp.where(first, pltpu.roll(v, LANES - RET_DQK // 2, 1), pltpu.roll(v, RET_DQK // 2, 1))
            r = v * cos + sw * sin
            if g >= RET_QK // LANES:
                r = r * (RET_DQK ** -0.5)
            qk_out[:, g * LANES:(g + 1) * LANES] = r.astype(BF16)
        v_out[...] = v_ref[...].astype(BF16)
        sb_out[:, 0:SB_W] = (sb_ref[:, 0:SB_W] * (SB_DH ** -0.5)).astype(BF16)
        sb_out[:, SB_W:3 * SB_W] = sb_ref[:, SB_W:3 * SB_W].astype(BF16)

    return _call("prep", body, (s // tm,),
                 [(proj, (tm, 2 * RET_QK), lambda i: (i, 0)),
                  (proj, (tm, RET_V), lambda i: (i, 2 * RET_QK // RET_V)),
                  (proj, (tm, 3 * SB_W), lambda i: (i, sb_off)),
                  (pos_col, (tm, 1), lambda i: (i, 0)),
                  (inv_freq, (1, LANES), lambda i: (0, 0))],
                 [((s, 2 * RET_QK), BF16, (tm, 2 * RET_QK), lambda i: (i, 0)),
                  ((s, RET_V), BF16, (tm, RET_V), lambda i: (i, 0)),
                  ((s, 3 * SB_W), BF16, (tm, 3 * SB_W), lambda i: (i, 0)),
                  ((s, LANES), F32, (tm, LANES), lambda i: (i, 0)),
                  ((s, LANES), F32, (tm, LANES), lambda i: (i, 0))])


def _head_mask(hh):
    lane = lax.broadcasted_iota(jnp.int32, (1, LANES), 1)
    return (lane >= RET_DQK) if hh else (lane < RET_DQK)


def _masked(v, m):
    return jnp.where(m, v, jnp.zeros_like(v))


GROUP = 4


def _stack_heads(v):
    return jnp.concatenate([_masked(v, _head_mask(0)), _masked(v, _head_mask(1))], axis=0)


def _side_by_side(v, t):
    return jnp.concatenate([v[:t], v[t:]], axis=1)


def _split_bf16(v):
    hi = v.astype(BF16)
    lo = (v - hi.astype(F32)).astype(BF16)
    return jnp.concatenate([hi, lo], axis=1)


def _ret_decay(lg_rows, i, j, t):
    row = jnp.bitwise_and(lax.broadcasted_iota(jnp.int32, (2 * t, t), 0), t - 1) + i * t
    col = lax.broadcasted_iota(jnp.int32, (2 * t, t), 1) + j * t
    allowed = jnp.right_shift(col, CHUNK_SHIFT) <= jnp.right_shift(row, CHUNK_SHIFT)
    dist = jnp.abs(row - col).astype(F32)
    return jnp.where(allowed, jnp.exp(lg_rows * dist), 0.0)


def _lg_rows(lg_ref, hp, t):
    first = lax.broadcasted_iota(jnp.int32, (2 * t, 1), 0) < t
    return jnp.where(first, lg_ref[2 * hp], lg_ref[2 * hp + 1])


def _ret_fwd(qk_rot, v_bf, proj, gn_g, log_gamma, t, riders=None):
    s = qk_rot.shape[0]
    gate_off = (2 * RET_QK + RET_V) // (2 * RET_DV)
    n_pair = HEADS // 2
    assert (s // t) % GROUP == 0 and t & (t - 1) == 0

    def body(lg_ref, q_ref, k_ref, v_ref, g_ref, w_ref, ret_ref, rg_ref):
        hp, i = pl.program_id(0), pl.program_id(1)
        qs = _stack_heads(q_ref[...])
        lg_rows = _lg_rows(lg_ref, hp, t)

        def step(g, carry):
            o0, o1 = carry
            js = [g * GROUP + sub for sub in range(GROUP)]
            rows = [pl.ds(pl.multiple_of(j * t, t), t) for j in js]
            ss = [_dot(qs, k_ref[rw, :], _NT) for rw in rows]
            ps = [(sc * _ret_decay(lg_rows, i, j, t)).astype(BF16) for sc, j in zip(ss, js)]
            for p, rw in zip(ps, rows):
                o0 = o0 + _dot(p[:t], v_ref[rw, 0:RET_DV])
                o1 = o1 + _dot(p[t:], v_ref[rw, RET_DV:2 * RET_DV])
            return o0, o1

        zero = jnp.zeros((t, RET_DV), F32)
        outs = lax.fori_loop(0, (i + GROUP) // GROUP, step, (zero, zero))
        for hh, o in enumerate(outs):
            cols = slice(hh * RET_DV, (hh + 1) * RET_DV)
            ret_ref[:, cols] = o
            mu = jnp.sum(o, axis=1, keepdims=True) * (1.0 / RET_DV)
            xc = o - mu
            var = jnp.sum(xc * xc, axis=1, keepdims=True) * (1.0 / RET_DV)
            nrm = xc * lax.rsqrt(var + EPS) * w_ref[:, cols]
            g = g_ref[:, cols]
            rg_ref[:, cols] = (g * _sigmoid(g) * nrm).astype(BF16)

    pw = 2 * RET_DV
    blk = lambda hp, i: (i, hp)
    return _call("ret_fwd", body, (n_pair, s // t),
                 [(log_gamma, None, pltpu.SMEM),
                  (qk_rot, (t, LANES), blk),
                  (qk_rot, (s, LANES), lambda hp, i: (0, n_pair + hp)),
                  (v_bf, (s, pw), lambda hp, i: (0, hp)),
                  (proj, (t, pw), lambda hp, i: (i, gate_off + hp)),
                  (gn_g, (1, pw), lambda hp, i: (0, hp))],
                 [((s, RET_V), F32, (t, pw), blk), ((s, RET_V), BF16, (t, pw), blk)], riders=riders)


def _sb_scores(qs, k_ref, i, js, t, tri2):
    row = jnp.bitwise_and(lax.broadcasted_iota(jnp.int32, (2 * t, t), 0), t - 1) + i * t
    col = lax.broadcasted_iota(jnp.int32, (2 * t, t), 1)
    zs = [_dot(qs, k_ref[pl.ds(pl.multiple_of(j * t, t), t), :], _NT) for j in js]
    valids = [(col + j * t) < row for j in js]
    sps = [jnp.maximum(z, 0.0) + jnp.log(1.0 + jnp.exp(-jnp.abs(z))) for z in zs]
    log_1ms = [jnp.where(v, -sp, 0.0) for v, sp in zip(valids, sps)]
    log_bs = [z - sp for z, sp in zip(zs, sps)]
    sticks = [lax.dot_general(_split_bf16(l), tri2, _NN, preferred_element_type=F32) for l in log_1ms]
    sums = [jnp.sum(l, axis=1, keepdims=True) for l in log_1ms]
    return log_1ms, log_bs, sticks, valids, sums


def _sb_fwd(qkv, t, riders=None):
    s = qkv.shape[0]
    n_pair = HEADS // 2
    assert s // t <= LANES and (s // t) % GROUP == 0 and t & (t - 1) == 0

    def body(q_ref, k_ref, v_ref, o_ref, carry_ref):
        i = pl.program_id(1)
        r = lax.broadcasted_iota(jnp.int32, (2 * t, t), 0)
        cc = lax.broadcasted_iota(jnp.int32, (2 * t, t), 1)
        upper2 = (jnp.bitwise_and(r, t - 1) > cc).astype(BF16)
        lane = lax.broadcasted_iota(jnp.int32, (1, LANES), 1)
        qs = _stack_heads(q_ref[...])
        carry_ref[...] = jnp.zeros_like(carry_ref)
        n_groups = (i + GROUP) // GROUP

        def step(n, carry):
            c, o = carry
            g = n_groups - 1 - n
            js = [g * GROUP + sub for sub in range(GROUP)]
            _, log_bs, sticks, valids, sums = _sb_scores(qs, k_ref, i, js, t, upper2)
            cs = [None] * GROUP
            for sub in reversed(range(GROUP)):
                cs[sub] = c
                c = c + sums[sub]
            for sub, j in enumerate(js):
                a = jnp.where(valids[sub], jnp.exp(log_bs[sub] + sticks[sub] + cs[sub]), 0.0)
                vj = v_ref[pl.ds(pl.multiple_of(j * t, t), t), :]
                o = o + _dot(_side_by_side(a.astype(BF16), t), _stack_heads(vj))
            for hh in range(2):
                cols = slice(hh * LANES, (hh + 1) * LANES)
                cm = carry_ref[:, cols]
                for sub, j in enumerate(js):
                    cm = jnp.where(lane == j, cs[sub][hh * t:(hh + 1) * t], cm)
                carry_ref[:, cols] = cm
            return c, o

        _, acc = lax.fori_loop(0, n_groups, step, (jnp.zeros((2 * t, 1), F32), jnp.zeros((t, LANES), F32)))
        o_ref[...] = acc

    return _call("sb_fwd", body, (n_pair, s // t),
                 [(qkv, (t, LANES), lambda hp, i: (i, hp)),
                  (qkv, (s, LANES), lambda hp, i: (0, n_pair + hp)),
                  (qkv, (s, LANES), lambda hp, i: (0, 2 * n_pair + hp))],
                 [((s, SB_W), F32, (t, LANES), lambda hp, i: (i, hp)),
                  ((s, HEADS * LANES), F32, (t, 2 * LANES), lambda hp, i: (i, hp))], riders=riders)


def _merge(retg, sb, w_ret, w_sb, proj, tm, tn):
    s, d = retg.shape[0], w_ret.shape[1]
    ar_off = (2 * RET_QK + 2 * RET_V + 3 * SB_W) // tn
    as_off = ar_off + d // tn

    def body(rg_ref, sb_ref, wr_ref, ws_ref, ar_ref, as_ref, mix_ref, r_ref, s_ref):
        rr = _dot(rg_ref[...], wr_ref[...])
        ss = _dot(sb_ref[...], ws_ref[...])
        mix_ref[...] = (_sigmoid(ar_ref[...]) * rr + _sigmoid(as_ref[...]) * ss).astype(BF16)
        r_ref[...] = rr.astype(BF16)
        s_ref[...] = ss.astype(BF16)

    tile = (tm, tn)
    return _call("merge", body, (d // tn, s // tm),
                 [(retg, (tm, RET_V), lambda j, i: (i, 0)), (sb, (tm, SB_W), lambda j, i: (i, 0)),
                  (w_ret, (RET_V, tn), lambda j, i: (0, j)), (w_sb, (SB_W, tn), lambda j, i: (0, j)),
                  (proj, tile, lambda j, i: (i, ar_off + j)), (proj, tile, lambda j, i: (i, as_off + j))],
                 [((s, d), BF16, tile, lambda j, i: (i, j))] * 3)


def _out_proj(mixed, w_out, x, mod, gp1, g2, tm):
    s, d = x.shape

    def body(a_ref, w_ref, x_ref, mod_ref, gp_ref, g2_ref, y_ref, hres_ref, h2_ref):
        y = _dot(a_ref[...], w_ref[...])
        y_ref[...] = y
        ny, _ = _rms(y, d)
        hres = x_ref[...] + mod_ref[:, 2 * d:3 * d] * (ny * gp_ref[...])
        hres_ref[...] = hres
        n2, _ = _rms(hres, d)
        h2_ref[...] = (n2 * g2_ref[...] * (1.0 + mod_ref[:, 4 * d:5 * d]) + mod_ref[:, 3 * d:4 * d]).astype(BF16)

    row = lambda i: (i, 0)
    fix = lambda i: (0, 0)
    return _call("out_proj", body, (s // tm,),
                 [(mixed, (tm, d), row), (w_out, (d, d), fix), (x, (tm, d), row),
                  (mod, (1, 6 * d), fix), (gp1, (1, d), fix), (g2, (1, d), fix)],
                 [((s, d), F32, (tm, d), row), ((s, d), F32, (tm, d), row), ((s, d), BF16, (tm, d), row)])


def _ff1(h2, w_ff1, tm, tn):
    s, f = h2.shape[0], w_ff1.shape[1]
    tm = min(tm, s)

    def body(a_ref, w_ref, u_ref, act_ref):
        u = _dot(a_ref[...], w_ref[...])
        r = jnp.maximum(u, 0.0)
        u_ref[...] = u.astype(BF16)
        act_ref[...] = (r * r).astype(BF16)

    d = h2.shape[1]
    return _call("ff1", body, (f // tn, s // tm),
                 [(h2, (tm, d), lambda j, i: (i, 0)), (w_ff1, (d, tn), lambda j, i: (0, j))],
                 [((s, f), BF16, (tm, tn), lambda j, i: (i, j))] * 2)


def _ff2_loss(act, w_ff2, hres, target, mod, gp2, tm):
    s, d = hres.shape
    f = act.shape[1]

    def body(a_ref, w_ref, h_ref, t_ref, mod_ref, gp_ref, dout_ref, df_ref, loss_ref, dgt_ref, dgp_ref):
        first = pl.program_id(0) == 0
        ff = _dot(a_ref[...], w_ref[...])
        nf, rf = _rms(ff, d)
        gt, gp = mod_ref[:, 5 * d:6 * d], gp_ref[...]
        out = h_ref[...] + gt * (nf * gp)
        err = out - t_ref[...]
        sq = jnp.sum(err * err, axis=1, keepdims=True)
        _accum(loss_ref, jnp.sum(sq, axis=0, keepdims=True), first)
        dout = err * (1.0 / d)
        dout_ref[...] = dout
        _accum(dgt_ref, _colsum(dout * (nf * gp)), first)
        _accum(dgp_ref, _colsum(dout * gt * nf), first)
        df_ref[...] = _rms_bwd(dout * gt * gp, nf, rf, d).astype(BF16)

    row = lambda i: (i, 0)
    fix = lambda i: (0, 0)
    return _call("ff2_loss", body, (s // tm,),
                 [(act, (tm, f), row), (w_ff2, (f, d), fix), (hres, (tm, d), row), (target, (tm, d), row),
                  (mod, (1, 6 * d), fix), (gp2, (1, d), fix)],
                 [((s, d), F32, (tm, d), row), ((s, d), BF16, (tm, d), row), ((1, 1), F32, (1, 1), fix),
                  ((1, d), F32, (1, d), fix), ((1, d), F32, (1, d), fix)])


def _ff2_bwd(df, w_ff2, u, tm, tn):
    s, d = df.shape
    f = w_ff2.shape[0]
    tm = min(tm, s)

    def body(a_ref, w_ref, u_ref, du_ref):
        da = _dot(a_ref[...], w_ref[...], _NT)
        du_ref[...] = (da * (2.0 * jnp.maximum(u_ref[...].astype(F32), 0.0))).astype(BF16)

    return _call("ff2_bwd", body, (f // tn, s // tm),
                 [(df, (tm, d), lambda j, i: (i, 0)), (w_ff2, (tn, d), lambda j, i: (j, 0)),
                  (u, (tm, tn), lambda j, i: (i, j))],
                 [((s, f), BF16, (tm, tn), lambda j, i: (i, j))])[0]


def _ff1_bwd(du, w_ff1, hres, dout, y, mod, g2, gp1, tm, riders=None):
    s, d = hres.shape
    f = du.shape[1]

    def body(a_ref, w_ref, h_ref, do_ref, y_ref, mod_ref, g2_ref, gp_ref,
             dh_ref, dy_ref, dsh_ref, dsc_ref, dg2_ref, dgt_ref, dgp_ref):
        first = pl.program_id(0) == 0
        dh2 = _dot(a_ref[...], w_ref[...], _NT)
        n2, r2 = _rms(h_ref[...], d)
        g2, sc2 = g2_ref[...], mod_ref[:, 4 * d:5 * d]
        _accum(dsh_ref, _colsum(dh2), first)
        _accum(dsc_ref, _colsum(dh2 * n2 * g2), first)
        _accum(dg2_ref, _colsum(dh2 * n2 * (1.0 + sc2)), first)
        dhres = do_ref[...] + _rms_bwd(dh2 * g2 * (1.0 + sc2), n2, r2, d)
        dh_ref[...] = dhres
        ny, ry = _rms(y_ref[...], d)
        gt, gp = mod_ref[:, 2 * d:3 * d], gp_ref[...]
        _accum(dgt_ref, _colsum(dhres * (ny * gp)), first)
        _accum(dgp_ref, _colsum(dhres * gt * ny), first)
        dy_ref[...] = _rms_bwd(dhres * gt * gp, ny, ry, d).astype(BF16)

    row = lambda i: (i, 0)
    fix = lambda i: (0, 0)
    vec = ((1, d), F32, (1, d), fix)
    return _call("ff1_bwd", body, (s // tm,),
                 [(du, (tm, f), row), (w_ff1, (d, f), fix), (hres, (tm, d), row), (dout, (tm, d), row),
                  (y, (tm, d), row), (mod, (1, 6 * d), fix), (g2, (1, d), fix), (gp1, (1, d), fix)],
                 [((s, d), F32, (tm, d), row), ((s, d), BF16, (tm, d), row), vec, vec, vec, vec, vec], riders=riders)


def _out_bwd(dy, w_out, proj, r_bf, s_bf, tm, tn):
    s, d = dy.shape
    ar_off = (2 * RET_QK + 2 * RET_V + 3 * SB_W) // tn
    as_off = ar_off + d // tn

    def body(a_ref, w_ref, ar_ref, as_ref, r_ref, s_ref, dr_ref, ds_ref, dar_ref, das_ref):
        dm = _dot(a_ref[...], w_ref[...], _NT)
        sr, ss = _sigmoid(ar_ref[...]), _sigmoid(as_ref[...])
        dr_ref[...] = (dm * sr).astype(BF16)
        ds_ref[...] = (dm * ss).astype(BF16)
        dar_ref[...] = (dm * r_ref[...].astype(F32) * sr * (1.0 - sr)).astype(BF16)
        das_ref[...] = (dm * s_ref[...].astype(F32) * ss * (1.0 - ss)).astype(BF16)

    tile = (tm, tn)
    here = lambda j, i: (i, j)
    return _call("out_bwd", body, (d // tn, s // tm),
                 [(dy, (tm, d), lambda j, i: (i, 0)), (w_out, (tn, d), lambda j, i: (j, 0)),
                  (proj, tile, lambda j, i: (i, ar_off + j)), (proj, tile, lambda j, i: (i, as_off + j)),
                  (r_bf, tile, here), (s_bf, tile, here)],
                 [((s, d), BF16, tile, here)] * 4)


def _gn_bwd(dretg, ret, proj, gn_g, tm, riders=None):
    s = ret.shape[0]
    gate_off = (2 * RET_QK + RET_V) // RET_V

    def body(d_ref, r_ref, g_ref, w_ref, dg_ref, dret_ref, dw_ref):
        first = pl.program_id(0) == 0
        for h in range(HEADS):
            cols = slice(h * RET_DV, (h + 1) * RET_DV)
            o, g, w, dr = r_ref[:, cols], g_ref[:, cols], w_ref[:, cols], d_ref[:, cols]
            mu = jnp.sum(o, axis=1, keepdims=True) * (1.0 / RET_DV)
            xc = o - mu
            rstd = lax.rsqrt(jnp.sum(xc * xc, axis=1, keepdims=True) * (1.0 / RET_DV) + EPS)
            n = xc * rstd
            sg = _sigmoid(g)
            silu = g * sg
            dg_ref[:, cols] = (dr * n * w * (sg * (1.0 + g * (1.0 - sg)))).astype(BF16)
            _accum(dw_ref.at[:, cols], _colsum(dr * silu * n), first)
            dn = dr * silu * w
            m1 = jnp.sum(dn, axis=1, keepdims=True) * (1.0 / RET_DV)
            m2 = jnp.sum(dn * n, axis=1, keepdims=True) * (1.0 / RET_DV)
            dret_ref[:, cols] = (rstd * (dn - m1 - n * m2)).astype(BF16)

    row = lambda i: (i, 0)
    fix = lambda i: (0, 0)
    return _call("gn_bwd", body, (s // tm,),
                 [(dretg, (tm, RET_V), row), (ret, (tm, RET_V), row),
                  (proj, (tm, RET_V), lambda i: (i, gate_off)), (gn_g, (1, RET_V), fix)],
                 [((s, RET_V), BF16, (tm, RET_V), row), ((s, RET_V), BF16, (tm, RET_V), row),
                  ((1, RET_V), F32, (1, RET_V), fix)], riders=riders)


def _ret_bwd(qk_rot, v_bf, dret, log_gamma, t, riders=None):
    s = qk_rot.shape[0]
    n_pair = HEADS // 2
    pw = 2 * RET_DV

    def body(lg_ref, q_ref, k_ref, v_ref, do_ref, dq_ref, dk_ref, dv_ref):
        hp, i = pl.program_id(0), pl.program_id(1)

        @pl.when(i == 0)
        def _():
            dk_ref[...] = jnp.zeros_like(dk_ref)
            dv_ref[...] = jnp.zeros_like(dv_ref)

        qs = _stack_heads(q_ref[...])
        lg_rows = _lg_rows(lg_ref, hp, t)
        do0, do1 = do_ref[:, 0:RET_DV], do_ref[:, RET_DV:pw]

        def step(g, dq):
            js = [g * GROUP + sub for sub in range(GROUP)]
            rows = [pl.ds(pl.multiple_of(j * t, t), t) for j in js]
            ss = [_dot(qs, k_ref[rw, :], _NT) for rw in rows]
            dps = [jnp.concatenate([_dot(do0, v_ref[rw, 0:RET_DV], _NT), _dot(do1, v_ref[rw, RET_DV:pw], _NT)], axis=0)
                   for rw in rows]
            decs = [_ret_decay(lg_rows, i, j, t) for j in js]
            ps = [(sc * dec).astype(BF16) for sc, dec in zip(ss, decs)]
            dss = [(dp * dec).astype(BF16) for dp, dec in zip(dps, decs)]
            for p, ds, rw in zip(ps, dss, rows):
                dv_ref[rw, 0:RET_DV] += _dot(p[:t], do0, _TN)
                dv_ref[rw, RET_DV:pw] += _dot(p[t:], do1, _TN)
                dk_ref[rw, :] += _dot(ds, qs, _TN)
                dq = dq + _dot(_side_by_side(ds, t), _stack_heads(k_ref[rw, :]))
            return dq

        dq_ref[...] = lax.fori_loop(0, (i + GROUP) // GROUP, step, jnp.zeros((t, LANES), F32))

    blk = lambda hp, i: (i, hp)
    return _call("ret_bwd", body, (n_pair, s // t),
                 [(log_gamma, None, pltpu.SMEM),
                  (qk_rot, (t, LANES), blk),
                  (qk_rot, (s, LANES), lambda hp, i: (0, n_pair + hp)),
                  (v_bf, (s, pw), lambda hp, i: (0, hp)),
                  (dret, (t, pw), blk)],
                 [((s, RET_QK), F32, (t, LANES), blk),
                  ((s, RET_QK), F32, (s, LANES), lambda hp, i: (0, hp)),
                  ((s, RET_V), F32, (s, pw), lambda hp, i: (0, hp))], riders=riders)


def _sb_bwd(qkv, carries, do, t, riders=None):
    s = qkv.shape[0]
    n_pair = HEADS // 2

    def body(q_ref, k_ref, v_ref, c_ref, do_ref, dq_ref, dk_ref, dv_ref):
        i = pl.program_id(1)

        @pl.when(i == 0)
        def _():
            dk_ref[...] = jnp.zeros_like(dk_ref)
            dv_ref[...] = jnp.zeros_like(dv_ref)

        r = jnp.bitwise_and(lax.broadcasted_iota(jnp.int32, (2 * t, t), 0), t - 1)
        cc = lax.broadcasted_iota(jnp.int32, (2 * t, t), 1)
        upper2 = (r > cc).astype(BF16)
        lower2 = (r < cc).astype(BF16)
        lane = lax.broadcasted_iota(jnp.int32, (1, LANES), 1)
        qs = _stack_heads(q_ref[...])
        dos = _stack_heads(do_ref[...].astype(BF16))
        cms = jnp.concatenate([c_ref[:, 0:LANES], c_ref[:, LANES:2 * LANES]], axis=0)
        n_groups = (i + GROUP) // GROUP

        def step(g, carry):
            c_e, dq = carry
            js = [g * GROUP + sub for sub in range(GROUP)]
            rows = [pl.ds(pl.multiple_of(j * t, t), t) for j in js]
            _, log_bs, sticks, valids, _ = _sb_scores(qs, k_ref, i, js, t, upper2)
            das = [_dot(dos, v_ref[rw, :], _NT) for rw in rows]
            c_sticks = [jnp.sum(jnp.where(lane == j, cms, 0.0), axis=1, keepdims=True) for j in js]
            avals = [jnp.where(v, jnp.exp(lb + st + cst), 0.0)
                     for v, lb, st, cst in zip(valids, log_bs, sticks, c_sticks)]
            es = [a * da for a, da in zip(avals, das)]
            prefixes = [lax.dot_general(_split_bf16(e), lower2, _NN, preferred_element_type=F32) for e in es]
            betas = [jnp.exp(lb) for lb in log_bs]
            for sub in range(GROUP):
                dv_ref[rows[sub], :] += _dot(avals[sub], dos, _TN)
            for sub in range(GROUP):
                dz = jnp.where(valids[sub], es[sub] * (1.0 - betas[sub]) - (prefixes[sub] + c_e) * betas[sub],
                               0.0).astype(BF16)
                dk_ref[rows[sub], :] += _dot(dz, qs, _TN)
                dq = dq + _dot(_side_by_side(dz, t), _stack_heads(k_ref[rows[sub], :]))
                c_e = c_e + jnp.sum(es[sub], axis=1, keepdims=True)
            return c_e, dq

        _, dq = lax.fori_loop(0, n_groups, step, (jnp.zeros((2 * t, 1), F32), jnp.zeros((t, LANES), F32)))
        dq_ref[...] = dq

    blk = lambda hp, i: (i, hp)
    return _call("sb_bwd", body, (n_pair, s // t),
                 [(qkv, (t, LANES), blk),
                  (qkv, (s, LANES), lambda hp, i: (0, n_pair + hp)),
                  (qkv, (s, LANES), lambda hp, i: (0, 2 * n_pair + hp)),
                  (carries, (t, 2 * LANES), blk), (do, (t, LANES), blk)],
                 [((s, SB_W), F32, (t, LANES), blk),
                  ((s, SB_W), F32, (s, LANES), lambda hp, i: (0, hp)),
                  ((s, SB_W), F32, (s, LANES), lambda hp, i: (0, hp))], riders=riders)


def _assemble_dproj(dq_r, dk_r, dv_r, dg_r, dq_s, dk_s, dv_s, da_r, da_s, cos, sin, tm):
    s, d = da_r.shape
    width = 2 * RET_QK + 2 * RET_V + 3 * SB_W + 2 * d

    def body(dq_ref, dk_ref, dv_ref, dg_ref, dqs_ref, dks_ref, dvs_ref, dar_ref, das_ref, cos_ref, sin_ref, o_ref):
        lane = lax.broadcasted_iota(jnp.int32, (1, LANES), 1)
        first = jnp.bitwise_and(lane, RET_DQK - 1) < (RET_DQK // 2)
        cos, sin = cos_ref[...], sin_ref[...]
        for src, base, scale in ((dq_ref, 0, 1.0), (dk_ref, RET_QK, RET_DQK ** -0.5)):
            for g in range(RET_QK // LANES):
                v = src[:, g * LANES:(g + 1) * LANES]
                sw = jnp.where(first, pltpu.roll(v, LANES - RET_DQK // 2, 1), pltpu.roll(v, RET_DQK // 2, 1))
                o_ref[:, base + g * LANES:base + (g + 1) * LANES] = ((v * cos - sw * sin) * scale).astype(BF16)
        off = 2 * RET_QK
        o_ref[:, off:off + RET_V] = dv_ref[...].astype(BF16)
        off += RET_V
        o_ref[:, off:off + RET_V] = dg_ref[...]
        off += RET_V
        o_ref[:, off:off + SB_W] = (dqs_ref[...] * (SB_DH ** -0.5)).astype(BF16)
        off += SB_W
        o_ref[:, off:off + SB_W] = dks_ref[...].astype(BF16)
        off += SB_W
        o_ref[:, off:off + SB_W] = dvs_ref[...].astype(BF16)
        off += SB_W
        o_ref[:, off:off + d] = dar_ref[...]
        off += d
        o_ref[:, off:off + d] = das_ref[...]

    row = lambda i: (i, 0)
    ins = [(a, (tm, a.shape[1]), row) for a in (dq_r, dk_r, dv_r, dg_r, dq_s, dk_s, dv_s, da_r, da_s, cos, sin)]
    return _call("assemble_dproj", body, (s // tm,), ins, [((s, width), BF16, (tm, width), row)])[0]


def _in_bwd(dproj, w_in, x, dhres, mod, g1, tm, riders=None):
    s, d = x.shape
    width = dproj.shape[1]

    def body(a_ref, w_ref, x_ref, dh_ref, mod_ref, g_ref, dx_ref, dsh_ref, dsc_ref, dg_ref):
        first = pl.program_id(0) == 0
        dh = _dot(a_ref[...], w_ref[...], _NT)
        n1, r1 = _rms(x_ref[...], d)
        g1, sc1 = g_ref[...], mod_ref[:, d:2 * d]
        _accum(dsh_ref, _colsum(dh), first)
        _accum(dsc_ref, _colsum(dh * n1 * g1), first)
        _accum(dg_ref, _colsum(dh * n1 * (1.0 + sc1)), first)
        dx_ref[...] = dh_ref[...] + _rms_bwd(dh * g1 * (1.0 + sc1), n1, r1, d)

    row = lambda i: (i, 0)
    fix = lambda i: (0, 0)
    vec = ((1, d), F32, (1, d), fix)
    return _call("in_bwd", body, (s // tm,),
                 [(dproj, (tm, width), row), (w_in, (d, width), fix), (x, (tm, d), row), (dhres, (tm, d), row),
                  (mod, (1, 6 * d), fix), (g1, (1, d), fix)],
                 [((s, d), F32, (tm, d), row), vec, vec, vec], riders=riders)


def _adamw(w, g, m, v):
    m = ADAM_B1 * m + (1.0 - ADAM_B1) * g
    v = ADAM_B2 * v + (1.0 - ADAM_B2) * (g * g)
    m_hat = m / (1.0 - ADAM_B1 ** ADAM_STEP)
    v_hat = v / (1.0 - ADAM_B2 ** ADAM_STEP)
    delta = -ADAM_LR * (m_hat / (jnp.sqrt(v_hat) + ADAM_EPS) + ADAM_WD * w)
    return delta, m, v


def _adam_reduce(name, parts, w, m, v, tr):
    rws, cls = w.shape
    tr = min(tr, rws)
    n_parts = parts.shape[0]

    def body(p_ref, w_ref, m_ref, v_ref, g_out, d_out, m_out, v_out):
        g = p_ref[0].astype(F32)
        for k in range(1, n_parts):
            g = g + p_ref[k].astype(F32)
        delta, mn, vn = _adamw(w_ref[...], g, m_ref[...], v_ref[...])
        g_out[...] = g
        d_out[...] = delta
        m_out[...] = mn
        v_out[...] = vn

    row = lambda i: (i, 0)
    blk = (tr, cls)
    return _call(name, body, (rws // tr,),
                 [(parts, (n_parts, tr, cls), lambda i: (0, i, 0)), (w, blk, row), (m, blk, row), (v, blk, row)],
                 [((rws, cls), F32, blk, row)] * 4)


def _ada_bwd_adam(cs_t, dmod_cols, w, m, v):
    d, nc = w.shape

    def body(c_ref, dm_ref, w_ref, m_ref, v_ref, g_out, d_out, m_out, v_out):
        g = c_ref[0] * dm_ref[0:1, :]
        for r in range(1, N_DEV):
            g = g + c_ref[r] * dm_ref[r:r + 1, :]
        delta, mn, vn = _adamw(w_ref[...], g, m_ref[...], v_ref[...])
        g_out[...] = g
        d_out[...] = delta
        m_out[...] = mn
        v_out[...] = vn

    fix = lambda i: (0, 0)
    blk = (d, nc)
    return _call("ada_bwd_adam", body, (1,),
                 [(cs_t, (N_DEV, d, 1), lambda i: (0, 0, 0)), (dmod_cols, (N_DEV, nc), fix), (w, blk, fix), (m, blk, fix), (v, blk, fix)],
                 [((d, nc), F32, blk, fix)] * 4)


def _small_adam(parts, w, m, v):
    n = w.shape[1]

    def body(p_ref, w_ref, m_ref, v_ref, g_out, d_out, m_out, v_out):
        g = p_ref[0:1, :]
        for k in range(1, N_DEV):
            g = g + p_ref[k:k + 1, :]
        delta, mn, vn = _adamw(w_ref[...], g, m_ref[...], v_ref[...])
        g_out[...] = g
        d_out[...] = delta
        m_out[...] = mn
        v_out[...] = vn

    fix = lambda i: (0, 0)
    return _call("small_adam", body, (1,),
                 [(parts, (N_DEV, n), fix), (w, (1, n), fix), (m, (1, n), fix), (v, (1, n), fix)],
                 [((1, n), F32, (1, n), fix)] * 4)


def kernel(x, c, positions, ada_w, ada_b, pre_mix_g, post_mix_g, pre_ffn_g, post_ffn_g, w_in, ret_gn_g, w_ret_branch, w_sb_branch, w_out, w_ff1, w_ff2, loss_target, m_ada_w, m_ada_b, m_pre_mix_g, m_post_mix_g, m_pre_ffn_g, m_post_ffn_g, m_w_in, m_ret_gn_g, m_w_ret_branch, m_w_sb_branch, m_w_out, m_w_ff1, m_w_ff2, v_ada_w, v_ada_b, v_pre_mix_g, v_post_mix_g, v_pre_ffn_g, v_post_ffn_g, v_w_in, v_ret_gn_g, v_w_ret_branch, v_w_sb_branch, v_w_out, v_w_ff1, v_w_ff2):
    _, s, d = x.shape
    d_ff = w_ff1.shape[2] * N_DEV
    d_in = w_in.shape[2] * N_DEV
    me = 4 * lax.axis_index("x") + 2 * lax.axis_index("y") + lax.axis_index("c")
    x2, tgt = x[0], loss_target[0]

    core = lax.axis_index("c").astype(jnp.int32).reshape(1)
    bf = lambda w: w[0].astype(BF16)

    c_all, g_in = _exchange("gather_in", [c, bf(w_in)], ["gather", "gather_chip"])
    c_all = c_all.reshape(N_DEV, d)

    n_ada = ada_w.shape[2]
    cs_all = _silu_rows(c_all)
    ada_b_cols = lax.dynamic_slice(ada_b, (0, me * n_ada), (1, n_ada))
    mod_cols = _ada_fwd(cs_all, ada_w[0], ada_b_cols)
    mod_all = _exchange("gather_mod", [mod_cols], ["gather"])[0]
    mod = lax.dynamic_index_in_dim(mod_all, me, axis=1, keepdims=False).reshape(1, 6 * d)

    tm = min(256, s)
    h, g_in = _pre_norm(x2, pre_mix_g, mod, tm, riders=([g_in], ["forward"]))
    wf_in = jnp.moveaxis(g_in, 0, 1).reshape(d, d_in)
    proj = _matmul("in_proj", h, wf_in, "nn", 1024, 512, F32)
    pos_col = positions.reshape(s, 1).astype(F32)
    freqs = ROPE_BASE ** (-jnp.arange(0, RET_DQK, 2, dtype=F32) / RET_DQK)
    inv_freq = jnp.tile(freqs, LANES // (RET_DQK // 2)).reshape(1, LANES)
    qk_rot, v_bf, qkv_sb, cos_t, sin_t = _prep(proj, pos_col, inv_freq, tm)
    log_gamma = jnp.asarray(np.log1p(-(2.0 ** (-5.0 - np.arange(HEADS)))), F32)
    t_ret = min(128, s)
    t_sb = min(128, s)
    later = [bf(w_ret_branch), bf(w_sb_branch), bf(w_out), bf(w_ff1), bf(w_ff2)]
    sb, sb_carry, *later = _sb_fwd(qkv_sb, t_sb, riders=(later, ["gather_chip"] * 5))
    ret, retg, g_ret, g_sb, g_out, g_ff1, g_ff2 = _ret_fwd(qk_rot, v_bf, proj, ret_gn_g, log_gamma, t_ret,
                                                           riders=(later, ["forward"] * 5))
    wf_ret = g_ret.reshape(RET_V, d)
    wf_sb = jnp.moveaxis(g_sb, 0, 1).reshape(SB_W, d)
    wf_out = g_out.reshape(d, d)
    wf_ff1 = jnp.moveaxis(g_ff1, 0, 1).reshape(d, d_ff)
    wf_ff2 = g_ff2.reshape(d_ff, d)
    mixed, r_bf, s_bf = _merge(retg, sb, wf_ret, wf_sb, proj, tm, min(512, d))
    y, hres, h2 = _out_proj(mixed, wf_out, x2, mod, post_mix_g, pre_ffn_g, tm)
    u, act = _ff1(h2, wf_ff1, 1024, 512)
    dout, df, loss_sum, d_gt2, d_gp2 = _ff2_loss(act, wf_ff2, hres, tgt, mod, post_ffn_g, tm)

    du = _ff2_bwd(df, wf_ff2, u, 1024, 512)
    gw_ff2 = _matmul("grad_w_ff2", act, df, "tn", 512, 512, BF16).reshape(N_DEV, d_ff // N_DEV, d)
    gw_ff1 = _matmul("grad_w_ff1", h2, du, "tn", 512, d_ff // N_DEV, BF16, blocked_out=True)
    dhres, dy, d_sh2, d_sc2, d_g2, d_gt1, d_gp1, t_ff1, t_ff2 = _ff1_bwd(
        du, wf_ff1, hres, dout, y, mod, pre_ffn_g, post_mix_g, tm, riders=([gw_ff1, gw_ff2], ["pair"] * 2))
    s_ff1 = _pair_sum("pair_sum_ff1", gw_ff1, t_ff1, core, 256)
    s_ff2 = _pair_sum("pair_sum_ff2", gw_ff2, t_ff2, core, 256)
    d_r, d_s, da_r, da_s = _out_bwd(dy, wf_out, proj, r_bf, s_bf, tm, min(512, d))
    gw_out = _matmul("grad_w_out", mixed, dy, "tn", 512, 512, BF16).reshape(N_DEV, d // N_DEV, d)
    dretg = _matmul("ret_branch_bwd", d_r, wf_ret, "nt", 1024, 512, BF16)
    dsb = _matmul("sb_branch_bwd", d_s, wf_sb, "nt", 1024, 512, F32)
    gw_ret = _matmul("grad_w_ret", retg, d_r, "tn", 512, 512, BF16).reshape(N_DEV, RET_V // N_DEV, d)
    gw_sb = _matmul("grad_w_sb", sb, d_s, "tn", 512, d // N_DEV, BF16, blocked_out=True)
    dq_s, dk_s, dv_s, p_ff1, p_ff2 = _sb_bwd(qkv_sb, sb_carry, dsb, t_sb,
                                             riders=([s_ff1, s_ff2], ["chip_scatter"] * 2))
    dg_r, dret, d_gn, t_out, t_ret_w, t_sb_w = _gn_bwd(dretg, ret, proj, ret_gn_g, tm,
                                                       riders=([gw_out, gw_ret, gw_sb], ["pair"] * 3))
    s_out = _pair_sum("pair_sum_out", gw_out, t_out, core, 256)
    s_ret = _pair_sum("pair_sum_ret", gw_ret, t_ret_w, core, 256)
    s_sb = _pair_sum("pair_sum_sb", gw_sb, t_sb_w, core, 256)
    dq_r, dk_r, dv_r, p_out, p_ret, p_sb = _ret_bwd(qk_rot, v_bf, dret, log_gamma, t_ret,
                                                    riders=([s_out, s_ret, s_sb], ["chip_scatter"] * 3))
    dproj = _assemble_dproj(dq_r, dk_r, dv_r, dg_r, dq_s, dk_s, dv_s, da_r, da_s, cos_t, sin_t, tm)
    gw_in_full = _matmul("grad_w_in", h, dproj, "tn", 512, 512, BF16)
    gw_in = jnp.moveaxis(gw_in_full.reshape(d, N_DEV, d_in // N_DEV), 1, 0)
    t_in = _exchange("pair_in", [gw_in], ["pair"])[0]
    s_in = _pair_sum("pair_sum_in", gw_in, t_in, core, 256)
    grad_x, d_sh1, d_sc1, d_g1, p_in = _in_bwd(dproj, wf_in, x2, dhres, mod, pre_mix_g, tm,
                                               riders=([s_in], ["chip_scatter"]))
    small = jnp.concatenate([d_sh1, d_sc1, d_gt1, d_sh2, d_sc2, d_gt2, d_g1, d_gp1, d_g2, d_gp2, d_gn], axis=1)
    small_all = _exchange("gather_small", [small], ["gather"])[0].reshape(N_DEV, small.shape[1])
    parts = [p_in, p_ret, p_sb, p_out, p_ff1, p_ff2]

    res = {}
    names = ["w_in", "w_ret_branch", "w_sb_branch", "w_out", "w_ff1", "w_ff2"]
    ws = [w_in, w_ret_branch, w_sb_branch, w_out, w_ff1, w_ff2]
    ms = [m_w_in, m_w_ret_branch, m_w_sb_branch, m_w_out, m_w_ff1, m_w_ff2]
    vs = [v_w_in, v_w_ret_branch, v_w_sb_branch, v_w_out, v_w_ff1, v_w_ff2]
    for nm, p, w, m, v in zip(names, parts, ws, ms, vs):
        res[nm] = [o[None] for o in _adam_reduce("adam_" + nm, p, w[0], m[0], v[0], 256)]
    dmod_cols = lax.dynamic_slice(small_all, (0, me * n_ada), (N_DEV, n_ada))
    res["ada_w"] = [o[None] for o in _ada_bwd_adam(cs_all.reshape(N_DEV, d, 1), dmod_cols, ada_w[0], m_ada_w[0], v_ada_w[0])]
    vec_names = ["ada_b", "pre_mix_g", "post_mix_g", "pre_ffn_g", "post_ffn_g", "ret_gn_g"]
    cat = lambda xs: jnp.concatenate(xs, axis=1)
    packed = _small_adam(small_all,
                         cat([ada_b, pre_mix_g, post_mix_g, pre_ffn_g, post_ffn_g, ret_gn_g]),
                         cat([m_ada_b, m_pre_mix_g, m_post_mix_g, m_pre_ffn_g, m_post_ffn_g, m_ret_gn_g]),
                         cat([v_ada_b, v_pre_mix_g, v_post_mix_g, v_pre_ffn_g, v_post_ffn_g, v_ret_gn_g]))
    off = 0
    for nm, width in zip(vec_names, [6 * d, d, d, d, d, RET_V]):
        res[nm] = [p[:, off:off + width] for p in packed]
        off += width

    loss = (0.5 / d) * lax.psum(loss_sum[0, 0], AXES)
    order = ["ada_w", "ada_b", "pre_mix_g", "post_mix_g", "pre_ffn_g", "post_ffn_g", "w_in", "ret_gn_g",
             "w_ret_branch", "w_sb_branch", "w_out", "w_ff1", "w_ff2"]
    outs = [loss, grad_x[None]]
    for k in range(4):
        outs += [res[nm][k] for nm in order]
    return tuple(outs)
```

```python
import functools

import numpy as np
import jax
import jax.numpy as jnp
from jax import lax
from jax.experimental import pallas as pl
from jax.experimental.pallas import tpu as pltpu

F32 = jnp.float32
BF16 = jnp.bfloat16
N_DEV = 8
AXES = ("x", "y", "c")

EPS = 1e-6
CHUNK = 64
CHUNK_SHIFT = 6
HEADS = 8
RET_DQK = 64
RET_DV = 128
SB_DH = 64
RET_QK = HEADS * RET_DQK
RET_V = HEADS * RET_DV
SB_W = HEADS * SB_DH
ROPE_BASE = 10000.0
LANES = 128

ADAM_LR = 0.001
ADAM_B1 = 0.9
ADAM_B2 = 0.999
ADAM_EPS = 1e-08
ADAM_WD = 0.01
ADAM_STEP = 10

VMEM_LIMIT = 56 * 1024 * 1024

_NN = (((1,), (0,)), ((), ()))
_NT = (((1,), (1,)), ((), ()))
_TN = (((0,), (0,)), ((), ()))


def _dot(a, b, dims=_NN):
    if a.dtype != BF16:
        a = a.astype(BF16)
    if b.dtype != BF16:
        b = b.astype(BF16)
    return lax.dot_general(a, b, dims, preferred_element_type=F32)


def _dot_split(a, b):
    hi = a.astype(BF16)
    lo = (a - hi.astype(F32)).astype(BF16)
    return (lax.dot_general(hi, b, _NN, preferred_element_type=F32)
            + lax.dot_general(lo, b, _NN, preferred_element_type=F32))


def _sigmoid(x):
    return 1.0 / (1.0 + jnp.exp(-x))


def _rms(x, d):
    r = lax.rsqrt(jnp.sum(x * x, axis=1, keepdims=True) * (1.0 / d) + EPS)
    return x * r, r


def _rms_bwd(dn, n, r, d):
    return r * (dn - n * (jnp.sum(dn * n, axis=1, keepdims=True) * (1.0 / d)))


def _colsum(v):
    return jnp.sum(v, axis=0, keepdims=True)


def _accum(ref, val, first):
    @pl.when(first)
    def _():
        ref[...] = val

    @pl.when(jnp.logical_not(first))
    def _():
        ref[...] += val


KIND_SLOTS = {"gather": N_DEV, "gather_chip": N_DEV, "forward": N_DEV, "pair": N_DEV // 2, "chip_scatter": N_DEV // 2}
SEMS_PER_ARRAY = N_DEV - 1


def _exchange_copies(ins, outs, send_sems, recv_sems, local_sems, kinds):
    x, y, c = (lax.axis_index(a) for a in AXES)
    me, chip, sibling = 4 * x + 2 * y + c, 2 * x + y, (x, y, 1 - c)
    mesh_id = pl.DeviceIdType.MESH
    other_chips = []
    for k in range(1, N_DEV // 2):
        px = 1 - x if k & 2 else x
        py = 1 - y if k & 1 else y
        other_chips.append((px, py))
    copies = []
    for i, kind in enumerate(kinds):
        def remote(src, dst, k, to, i=i):
            return pltpu.make_async_remote_copy(
                src_ref=src, dst_ref=dst, send_sem=send_sems.at[i * SEMS_PER_ARRAY + k],
                recv_sem=recv_sems.at[i * SEMS_PER_ARRAY + k], device_id=to, device_id_type=mesh_id)

        if kind == "gather":
            copies.append(pltpu.make_async_copy(ins[i], outs[i].at[me], local_sems.at[i]))
            for k in range(1, N_DEV):
                to = (1 - x if k & 4 else x, 1 - y if k & 2 else y, 1 - c if k & 1 else c)
                copies.append(remote(ins[i], outs[i].at[me], k - 1, to))
        elif kind == "gather_chip":
            copies.append(pltpu.make_async_copy(ins[i], outs[i].at[me], local_sems.at[i]))
            copies.append(remote(ins[i], outs[i].at[me], 0, sibling))
            for k, (px, py) in enumerate(other_chips):
                copies.append(remote(ins[i], outs[i].at[me], 1 + k, (px, py, c)))
        elif kind == "forward":
            for k, (px, py) in enumerate(other_chips):
                slot = 4 * px + 2 * py + c
                copies.append(remote(outs[i].at[slot], outs[i].at[slot], k, sibling))
        elif kind == "pair":
            for k in range(N_DEV // 2):
                copies.append(remote(ins[i].at[2 * k + 1 - c], outs[i].at[k], k, sibling))
        elif kind == "chip_scatter":
            copies.append(pltpu.make_async_copy(ins[i].at[chip], outs[i].at[chip], local_sems.at[i]))
            for k, (px, py) in enumerate(other_chips):
                copies.append(remote(ins[i].at[2 * px + py], outs[i].at[chip], k, (px, py, c)))
        else:
            raise ValueError(kind)
    return copies


def _exchange_shapes(arrays, kinds):
    shapes = []
    for a, kind in zip(arrays, kinds):
        tail = a.shape if kind in ("gather", "gather_chip") else a.shape[1:]
        shapes.append(jax.ShapeDtypeStruct((KIND_SLOTS[kind],) + tuple(tail), a.dtype))
    return shapes


def _exchange_sems(n):
    return [pltpu.SemaphoreType.DMA((n * SEMS_PER_ARRAY,)), pltpu.SemaphoreType.DMA((n * SEMS_PER_ARRAY,)),
            pltpu.SemaphoreType.DMA((n,))]


def _call(name, body, grid, ins, outs, scratch=(), riders=None, prefetch=None):
    any_spec = pl.BlockSpec(memory_space=pl.ANY)
    in_specs = [pl.BlockSpec(memory_space=im) if bs is None else pl.BlockSpec(bs, im) for _, bs, im in ins]
    out_specs = [pl.BlockSpec(bs, im) for _, _, bs, im in outs]
    out_shape = [jax.ShapeDtypeStruct(s, d) for s, d, _, _ in outs]
    operands = [a for a, _, _ in ins]
    scratch = list(scratch)
    aliases = {}
    n_pre = 0 if prefetch is None else 1
    kernel = functools.partial(body) if prefetch is None else (lambda _, *refs: body(*refs))
    if riders is not None:
        arrays, kinds = riders
        nr, n_in, n_out, n_scr = len(arrays), len(ins), len(outs), len(scratch)

        def kernel(*refs):
            refs = refs[n_pre:]
            own_in, ride_in = refs[:n_in], refs[n_in:n_in + nr]
            own_out = refs[n_in + nr:n_in + nr + n_out]
            ride_out = refs[n_in + nr + n_out:n_in + 2 * nr + n_out]
            own_scr = refs[n_in + 2 * nr + n_out:n_in + 2 * nr + n_out + n_scr]
            sems = refs[n_in + 2 * nr + n_out + n_scr:]
            ids = [pl.program_id(a) for a in range(len(grid))]
            first = functools.reduce(jnp.logical_and, [i == 0 for i in ids])
            last = functools.reduce(jnp.logical_and, [i == g - 1 for i, g in zip(ids, grid)])

            @pl.when(first)
            def _():
                for cp in _exchange_copies(ride_in, ride_out, *sems, kinds):
                    cp.start()

            body(*own_in, *own_out, *own_scr)

            @pl.when(last)
            def _():
                for cp in _exchange_copies(ride_in, ride_out, *sems, kinds):
                    cp.wait()

        in_specs += [any_spec] * nr
        out_specs += [any_spec] * nr
        out_shape += _exchange_shapes(arrays, kinds)
        operands += list(arrays)
        scratch += _exchange_sems(nr)
        aliases = {n_pre + n_in + r: n_out + r for r, kind in enumerate(kinds) if kind == "forward"}
    params = pltpu.CompilerParams(dimension_semantics=("arbitrary",) * len(grid), vmem_limit_bytes=VMEM_LIMIT)
    if prefetch is None:
        return pl.pallas_call(kernel, name=name, grid=grid, in_specs=in_specs, out_specs=out_specs,
                              out_shape=out_shape, scratch_shapes=scratch, input_output_aliases=aliases,
                              compiler_params=params)(*operands)
    grid_spec = pltpu.PrefetchScalarGridSpec(num_scalar_prefetch=1, grid=grid, in_specs=in_specs,
                                             out_specs=out_specs, scratch_shapes=scratch)
    return pl.pallas_call(kernel, name=name, grid_spec=grid_spec, out_shape=out_shape,
                          input_output_aliases=aliases, compiler_params=params)(prefetch, *operands)


def _exchange(name, arrays, kinds):
    n = len(arrays)

    def body(*refs):
        copies = _exchange_copies(refs[:n], refs[n:2 * n], *refs[2 * n:], kinds)
        for cp in copies:
            cp.start()
        for cp in copies:
            cp.wait()

    any_spec = pl.BlockSpec(memory_space=pl.ANY)
    return pl.pallas_call(
        functools.partial(body),
        name=name,
        in_specs=[any_spec] * n,
        out_specs=[any_spec] * n,
        out_shape=_exchange_shapes(arrays, kinds),
        scratch_shapes=_exchange_sems(n),
        input_output_aliases={i: i for i, kind in enumerate(kinds) if kind == "forward"},
    )(*arrays)


def _pair_sum(name, mine, theirs, my_core, tr):
    _, rws, cls = mine.shape
    tr = min(tr, rws)

    def body(a_ref, b_ref, o_ref):
        o_ref[...] = (a_ref[...].astype(F32) + b_ref[...].astype(F32)).astype(o_ref.dtype)

    return _call(name, body, (N_DEV // 2, rws // tr),
                 [(mine, (None, tr, cls), lambda k, r, core: (2 * k + core[0], r, 0)),
                  (theirs, (None, tr, cls), lambda k, r, core: (k, r, 0))],
                 [((N_DEV // 2, rws, cls), mine.dtype, (None, tr, cls), lambda k, r, core: (k, r, 0))],
                 prefetch=my_core)[0]


def _matmul(name, a, b, kind, tm, tn, out_dtype, blocked_out=False):
    if kind == "tn":
        kdim, m = a.shape
    else:
        m, kdim = a.shape
    n = b.shape[0] if kind == "nt" else b.shape[1]
    tm, tn = min(tm, m), min(tn, n)
    dims = {"nn": _NN, "nt": _NT, "tn": _TN}[kind]

    def body(a_ref, b_ref, o_ref):
        o_ref[...] = _dot(a_ref[...], b_ref[...], dims).astype(o_ref.dtype)

    a_spec = (a, (kdim, tm), lambda j, i: (0, i)) if kind == "tn" else (a, (tm, kdim), lambda j, i: (i, 0))
    b_spec = (b, (tn, kdim), lambda j, i: (j, 0)) if kind == "nt" else (b, (kdim, tn), lambda j, i: (0, j))
    if blocked_out:
        out = ((n // tn, m, tn), out_dtype, (None, tm, tn), lambda j, i: (j, i, 0))
    else:
        out = ((m, n), out_dtype, (tm, tn), lambda j, i: (i, j))
    return _call(name, body, (n // tn, m // tm), [a_spec, b_spec], [out])[0]


def _ada_fwd(cs_all, ada_w, ada_b_cols):
    def body(c_ref, w_ref, b_ref, o_ref):
        o_ref[...] = lax.dot_general(c_ref[...], w_ref[...], _NN, preferred_element_type=F32,
                                     precision=lax.Precision.HIGHEST) + b_ref[...]

    r, d = cs_all.shape
    nc = ada_w.shape[1]
    return _call("ada_fwd", body, (1,),
                 [(cs_all, (r, d), lambda i: (0, 0)), (ada_w, (d, nc), lambda i: (0, 0)),
                  (ada_b_cols, (1, nc), lambda i: (0, 0))],
                 [((r, nc), F32, (r, nc), lambda i: (0, 0))])[0]


def _silu_rows(c_all):
    def body(c_ref, o_ref):
        v = c_ref[...]
        o_ref[...] = v * _sigmoid(v)

    return _call("silu_c", body, (1,), [(c_all, c_all.shape, lambda i: (0, 0))],
                 [(c_all.shape, F32, c_all.shape, lambda i: (0, 0))])[0]


def _pre_norm(x, g, mod, tm, riders=None):
    s, d = x.shape

    def body(x_ref, g_ref, mod_ref, h_ref):
        n, _ = _rms(x_ref[...], d)
        sh, sc = mod_ref[:, 0:d], mod_ref[:, d:2 * d]
        h_ref[...] = (n * g_ref[...] * (1.0 + sc) + sh).astype(BF16)

    return _call("pre_norm", body, (s // tm,),
                 [(x, (tm, d), lambda i: (i, 0)), (g, (1, d), lambda i: (0, 0)),
                  (mod, (1, 6 * d), lambda i: (0, 0))],
                 [((s, d), BF16, (tm, d), lambda i: (i, 0))], riders=riders)


def _prep(proj, pos_col, inv_freq, tm):
    s = proj.shape[0]
    sb_off = (2 * RET_QK + 2 * RET_V) // (3 * SB_W)

    def body(qk_ref, v_ref, sb_ref, pos_ref, f_ref, qk_out, v_out, sb_out, cos_out, sin_out):
        ang = pos_ref[...] * f_ref[...]
        lane = lax.broadcasted_iota(jnp.int32, (1, LANES), 1)
        first = jnp.bitwise_and(lane, RET_DQK - 1) < (RET_DQK // 2)
        cos = jnp.cos(ang)
        sin = jnp.where(first, -1.0, 1.0) * jnp.sin(ang)
        cos_out[...] = cos
        sin_out[...] = sin
        for g in range(2 * RET_QK // LANES):
            v = qk_ref[:, g * LANES:(g + 1) * LANES]
            sw = jnp.where(first, pltpu.roll(v, LANES - RET_DQK // 2, 1), pltpu.roll(v, RET_DQK // 2, 1))
            r = v * cos + sw * sin
            if g >= RET_QK // LANES:
                r = r * (RET_DQK ** -0.5)
            qk_out[:, g * LANES:(g + 1) * LANES] = r.astype(BF16)
        v_out[...] = v_ref[...].astype(BF16)
        sb_out[:, 0:SB_W] = (sb_ref[:, 0:SB_W] * (SB_DH ** -0.5)).astype(BF16)
        sb_out[:, SB_W:3 * SB_W] = sb_ref[:, SB_W:3 * SB_W].astype(BF16)

    return _call("prep", body, (s // tm,),
                 [(proj, (tm, 2 * RET_QK), lambda i: (i, 0)),
                  (proj, (tm, RET_V), lambda i: (i, 2 * RET_QK // RET_V)),
                  (proj, (tm, 3 * SB_W), lambda i: (i, sb_off)),
                  (pos_col, (tm, 1), lambda i: (i, 0)),
                  (inv_freq, (1, LANES), lambda i: (0, 0))],
                 [((s, 2 * RET_QK), BF16, (tm, 2 * RET_QK), lambda i: (i, 0)),
                  ((s, RET_V), BF16, (tm, RET_V), lambda i: (i, 0)),
                  ((s, 3 * SB_W), BF16, (tm, 3 * SB_W), lambda i: (i, 0)),
                  ((s, LANES), F32, (tm, LANES), lambda i: (i, 0)),
                  ((s, LANES), F32, (tm, LANES), lambda i: (i, 0))])


def _head_mask(hh):
    lane = lax.broadcasted_iota(jnp.int32, (1, LANES), 1)
    return (lane >= RET_DQK) if hh else (lane < RET_DQK)


def _masked(v, m):
    return jnp.where(m, v, jnp.zeros_like(v))


GROUP = 4


def _stack_heads(v):
    return jnp.concatenate([_masked(v, _head_mask(0)), _masked(v, _head_mask(1))], axis=0)


def _side_by_side(v, t):
    return jnp.concatenate([v[:t], v[t:]], axis=1)


def _split_bf16(v):
    hi = v.astype(BF16)
    lo = (v - hi.astype(F32)).astype(BF16)
    return jnp.concatenate([hi, lo], axis=1)


def _tile_pos(i, j, tq, tk):
    row = jnp.bitwise_and(lax.broadcasted_iota(jnp.int32, (2 * tq, tk), 0), tq - 1) + i * tq
    col = lax.broadcasted_iota(jnp.int32, (2 * tq, tk), 1) + j * tk
    return row, col


def _n_groups(i, tq, tk):
    return ((i + 1) * (tq // tk) + GROUP - 1) // GROUP


def _key_rows(j, tk):
    return pl.ds(pl.multiple_of(j * tk, tk), tk)


def _ret_decay(lg_rows, i, j, tq, tk):
    row, col = _tile_pos(i, j, tq, tk)
    allowed = jnp.right_shift(col, CHUNK_SHIFT) <= jnp.right_shift(row, CHUNK_SHIFT)
    dist = jnp.abs(row - col).astype(F32)
    return jnp.where(allowed, jnp.exp(lg_rows * dist), 0.0)


def _lg_rows(lg_ref, hp, tq):
    first = lax.broadcasted_iota(jnp.int32, (2 * tq, 1), 0) < tq
    return jnp.where(first, lg_ref[2 * hp], lg_ref[2 * hp + 1])


def _check_tiles(s, tq, tk):
    assert tq % tk == 0 and tq & (tq - 1) == 0 and tk & (tk - 1) == 0
    assert s % tq == 0 and (s // tk) % GROUP == 0 and s // tk <= LANES


def _ret_fwd(qk_rot, v_bf, proj, gn_g, log_gamma, tq, tk, riders=None):
    s = qk_rot.shape[0]
    gate_off = (2 * RET_QK + RET_V) // (2 * RET_DV)
    n_pair = HEADS // 2
    _check_tiles(s, tq, tk)

    def body(lg_ref, q_ref, k_ref, v_ref, g_ref, w_ref, ret_ref, rg_ref):
        hp, i = pl.program_id(0), pl.program_id(1)
        qs = _stack_heads(q_ref[...])
        lg_rows = _lg_rows(lg_ref, hp, tq)

        def step(g, carry):
            o0, o1 = carry
            js = [g * GROUP + sub for sub in range(GROUP)]
            rows = [_key_rows(j, tk) for j in js]
            ss = [_dot(qs, k_ref[rw, :], _NT) for rw in rows]
            ps = [(sc * _ret_decay(lg_rows, i, j, tq, tk)).astype(BF16) for sc, j in zip(ss, js)]
            for p, rw in zip(ps, rows):
                o0 = o0 + _dot(p[:tq], v_ref[rw, 0:RET_DV])
                o1 = o1 + _dot(p[tq:], v_ref[rw, RET_DV:2 * RET_DV])
            return o0, o1

        zero = jnp.zeros((tq, RET_DV), F32)
        outs = lax.fori_loop(0, _n_groups(i, tq, tk), step, (zero, zero))
        for hh, o in enumerate(outs):
            cols = slice(hh * RET_DV, (hh + 1) * RET_DV)
            ret_ref[:, cols] = o
            mu = jnp.sum(o, axis=1, keepdims=True) * (1.0 / RET_DV)
            xc = o - mu
            var = jnp.sum(xc * xc, axis=1, keepdims=True) * (1.0 / RET_DV)
            nrm = xc * lax.rsqrt(var + EPS) * w_ref[:, cols]
            g = g_ref[:, cols]
            rg_ref[:, cols] = (g * _sigmoid(g) * nrm).astype(BF16)

    pw = 2 * RET_DV
    blk = lambda hp, i: (i, hp)
    return _call("ret_fwd", body, (n_pair, s // tq),
                 [(log_gamma, None, pltpu.SMEM),
                  (qk_rot, (tq, LANES), blk),
                  (qk_rot, (s, LANES), lambda hp, i: (0, n_pair + hp)),
                  (v_bf, (s, pw), lambda hp, i: (0, hp)),
                  (proj, (tq, pw), lambda hp, i: (i, gate_off + hp)),
                  (gn_g, (1, pw), lambda hp, i: (0, hp))],
                 [((s, RET_V), F32, (tq, pw), blk), ((s, RET_V), BF16, (tq, pw), blk)], riders=riders)


def _tri2(tk, strict_upper):
    r = jnp.bitwise_and(lax.broadcasted_iota(jnp.int32, (2 * tk, tk), 0), tk - 1)
    cc = lax.broadcasted_iota(jnp.int32, (2 * tk, tk), 1)
    return ((r > cc) if strict_upper else (r < cc)).astype(BF16)


def _sb_scores(qs, k_ref, i, js, tq, tk, tri2):
    zs = [_dot(qs, k_ref[_key_rows(j, tk), :], _NT) for j in js]
    valids = []
    for j in js:
        row, col = _tile_pos(i, j, tq, tk)
        valids.append(col < row)
    sps = [jnp.maximum(z, 0.0) + jnp.log(1.0 + jnp.exp(-jnp.abs(z))) for z in zs]
    log_1ms = [jnp.where(v, -sp, 0.0) for v, sp in zip(valids, sps)]
    log_bs = [z - sp for z, sp in zip(zs, sps)]
    sticks = [lax.dot_general(_split_bf16(l), tri2, _NN, preferred_element_type=F32) for l in log_1ms]
    sums = [jnp.sum(l, axis=1, keepdims=True) for l in log_1ms]
    return log_1ms, log_bs, sticks, valids, sums


def _sb_fwd(qkv, tq, tk, riders=None):
    s = qkv.shape[0]
    n_pair = HEADS // 2
    _check_tiles(s, tq, tk)

    def body(q_ref, k_ref, v_ref, o_ref, carry_ref):
        i = pl.program_id(1)
        upper2 = _tri2(tk, True)
        lane = lax.broadcasted_iota(jnp.int32, (1, LANES), 1)
        qs = _stack_heads(q_ref[...])
        carry_ref[...] = jnp.zeros_like(carry_ref)
        n_groups = _n_groups(i, tq, tk)

        def step(n, carry):
            c, o = carry
            g = n_groups - 1 - n
            js = [g * GROUP + sub for sub in range(GROUP)]
            _, log_bs, sticks, valids, sums = _sb_scores(qs, k_ref, i, js, tq, tk, upper2)
            cs = [None] * GROUP
            for sub in reversed(range(GROUP)):
                cs[sub] = c
                c = c + sums[sub]
            for sub, j in enumerate(js):
                a = jnp.where(valids[sub], jnp.exp(log_bs[sub] + sticks[sub] + cs[sub]), 0.0)
                o = o + _dot(_side_by_side(a.astype(BF16), tq), _stack_heads(v_ref[_key_rows(j, tk), :]))
            for hh in range(2):
                cols = slice(hh * LANES, (hh + 1) * LANES)
                cm = carry_ref[:, cols]
                for sub, j in enumerate(js):
                    cm = jnp.where(lane == j, cs[sub][hh * tq:(hh + 1) * tq], cm)
                carry_ref[:, cols] = cm
            return c, o

        _, acc = lax.fori_loop(0, n_groups, step, (jnp.zeros((2 * tq, 1), F32), jnp.zeros((tq, LANES), F32)))
        o_ref[...] = acc

    return _call("sb_fwd", body, (n_pair, s // tq),
                 [(qkv, (tq, LANES), lambda hp, i: (i, hp)),
                  (qkv, (s, LANES), lambda hp, i: (0, n_pair + hp)),
                  (qkv, (s, LANES), lambda hp, i: (0, 2 * n_pair + hp))],
                 [((s, SB_W), F32, (tq, LANES), lambda hp, i: (i, hp)),
                  ((s, HEADS * LANES), F32, (tq, 2 * LANES), lambda hp, i: (i, hp))], riders=riders)


def _merge(retg, sb, w_ret, w_sb, proj, tm, tn):
    s, d = retg.shape[0], w_ret.shape[1]
    ar_off = (2 * RET_QK + 2 * RET_V + 3 * SB_W) // tn
    as_off = ar_off + d // tn

    def body(rg_ref, sb_ref, wr_ref, ws_ref, ar_ref, as_ref, mix_ref, r_ref, s_ref):
        rr = _dot(rg_ref[...], wr_ref[...])
        ss = _dot(sb_ref[...], ws_ref[...])
        mix_ref[...] = (_sigmoid(ar_ref[...]) * rr + _sigmoid(as_ref[...]) * ss).astype(BF16)
        r_ref[...] = rr.astype(BF16)
        s_ref[...] = ss.astype(BF16)

    tile = (tm, tn)
    return _call("merge", body, (d // tn, s // tm),
                 [(retg, (tm, RET_V), lambda j, i: (i, 0)), (sb, (tm, SB_W), lambda j, i: (i, 0)),
                  (w_ret, (RET_V, tn), lambda j, i: (0, j)), (w_sb, (SB_W, tn), lambda j, i: (0, j)),
                  (proj, tile, lambda j, i: (i, ar_off + j)), (proj, tile, lambda j, i: (i, as_off + j))],
                 [((s, d), BF16, tile, lambda j, i: (i, j))] * 3)


def _out_proj(mixed, w_out, x, mod, gp1, g2, tm):
    s, d = x.shape

    def body(a_ref, w_ref, x_ref, mod_ref, gp_ref, g2_ref, y_ref, hres_ref, h2_ref):
        y = _dot(a_ref[...], w_ref[...])
        y_ref[...] = y
        ny, _ = _rms(y, d)
        hres = x_ref[...] + mod_ref[:, 2 * d:3 * d] * (ny * gp_ref[...])
        hres_ref[...] = hres
        n2, _ = _rms(hres, d)
        h2_ref[...] = (n2 * g2_ref[...] * (1.0 + mod_ref[:, 4 * d:5 * d]) + mod_ref[:, 3 * d:4 * d]).astype(BF16)

    row = lambda i: (i, 0)
    fix = lambda i: (0, 0)
    return _call("out_proj", body, (s // tm,),
                 [(mixed, (tm, d), row), (w_out, (d, d), fix), (x, (tm, d), row),
                  (mod, (1, 6 * d), fix), (gp1, (1, d), fix), (g2, (1, d), fix)],
                 [((s, d), F32, (tm, d), row), ((s, d), F32, (tm, d), row), ((s, d), BF16, (tm, d), row)])


def _ff1(h2, w_ff1, tm, tn):
    s, f = h2.shape[0], w_ff1.shape[1]
    tm = min(tm, s)

    def body(a_ref, w_ref, u_ref, act_ref):
        u = _dot(a_ref[...], w_ref[...])
        r = jnp.maximum(u, 0.0)
        u_ref[...] = u.astype(BF16)
        act_ref[...] = (r * r).astype(BF16)

    d = h2.shape[1]
    return _call("ff1", body, (f // tn, s // tm),
                 [(h2, (tm, d), lambda j, i: (i, 0)), (w_ff1, (d, tn), lambda j, i: (0, j))],
                 [((s, f), BF16, (tm, tn), lambda j, i: (i, j))] * 2)


def _ff2_loss(act, w_ff2, hres, target, mod, gp2, tm):
    s, d = hres.shape
    f = act.shape[1]

    def body(a_ref, w_ref, h_ref, t_ref, mod_ref, gp_ref, dout_ref, df_ref, loss_ref, dgt_ref, dgp_ref):
        first = pl.program_id(0) == 0
        ff = _dot(a_ref[...], w_ref[...])
        nf, rf = _rms(ff, d)
        gt, gp = mod_ref[:, 5 * d:6 * d], gp_ref[...]
        out = h_ref[...] + gt * (nf * gp)
        err = out - t_ref[...]
        sq = jnp.sum(err * err, axis=1, keepdims=True)
        _accum(loss_ref, jnp.sum(sq, axis=0, keepdims=True), first)
        dout = err * (1.0 / d)
        dout_ref[...] = dout
        _accum(dgt_ref, _colsum(dout * (nf * gp)), first)
        _accum(dgp_ref, _colsum(dout * gt * nf), first)
        df_ref[...] = _rms_bwd(dout * gt * gp, nf, rf, d).astype(BF16)

    row = lambda i: (i, 0)
    fix = lambda i: (0, 0)
    return _call("ff2_loss", body, (s // tm,),
                 [(act, (tm, f), row), (w_ff2, (f, d), fix), (hres, (tm, d), row), (target, (tm, d), row),
                  (mod, (1, 6 * d), fix), (gp2, (1, d), fix)],
                 [((s, d), F32, (tm, d), row), ((s, d), BF16, (tm, d), row), ((1, 1), F32, (1, 1), fix),
                  ((1, d), F32, (1, d), fix), ((1, d), F32, (1, d), fix)])


def _ff2_bwd(df, w_ff2, u, tm, tn):
    s, d = df.shape
    f = w_ff2.shape[0]
    tm = min(tm, s)

    def body(a_ref, w_ref, u_ref, du_ref):
        da = _dot(a_ref[...], w_ref[...], _NT)
        du_ref[...] = (da * (2.0 * jnp.maximum(u_ref[...].astype(F32), 0.0))).astype(BF16)

    return _call("ff2_bwd", body, (f // tn, s // tm),
                 [(df, (tm, d), lambda j, i: (i, 0)), (w_ff2, (tn, d), lambda j, i: (j, 0)),
                  (u, (tm, tn), lambda j, i: (i, j))],
                 [((s, f), BF16, (tm, tn), lambda j, i: (i, j))])[0]


def _ff1_bwd(du, w_ff1, hres, dout, y, mod, g2, gp1, tm, riders=None):
    s, d = hres.shape
    f = du.shape[1]

    def body(a_ref, w_ref, h_ref, do_ref, y_ref, mod_ref, g2_ref, gp_ref,
             dh_ref, dy_ref, dsh_ref, dsc_ref, dg2_ref, dgt_ref, dgp_ref):
        first = pl.program_id(0) == 0
        dh2 = _dot(a_ref[...], w_ref[...], _NT)
        n2, r2 = _rms(h_ref[...], d)
        g2, sc2 = g2_ref[...], mod_ref[:, 4 * d:5 * d]
        _accum(dsh_ref, _colsum(dh2), first)
        _accum(dsc_ref, _colsum(dh2 * n2 * g2), first)
        _accum(dg2_ref, _colsum(dh2 * n2 * (1.0 + sc2)), first)
        dhres = do_ref[...] + _rms_bwd(dh2 * g2 * (1.0 + sc2), n2, r2, d)
        dh_ref[...] = dhres
        ny, ry = _rms(y_ref[...], d)
        gt, gp = mod_ref[:, 2 * d:3 * d], gp_ref[...]
        _accum(dgt_ref, _colsum(dhres * (ny * gp)), first)
        _accum(dgp_ref, _colsum(dhres * gt * ny), first)
        dy_ref[...] = _rms_bwd(dhres * gt * gp, ny, ry, d).astype(BF16)

    row = lambda i: (i, 0)
    fix = lambda i: (0, 0)
    vec = ((1, d), F32, (1, d), fix)
    return _call("ff1_bwd", body, (s // tm,),
                 [(du, (tm, f), row), (w_ff1, (d, f), fix), (hres, (tm, d), row), (dout, (tm, d), row),
                  (y, (tm, d), row), (mod, (1, 6 * d), fix), (g2, (1, d), fix), (gp1, (1, d), fix)],
                 [((s, d), F32, (tm, d), row), ((s, d), BF16, (tm, d), row), vec, vec, vec, vec, vec], riders=riders)


def _out_bwd(dy, w_out, proj, r_bf, s_bf, tm, tn):
    s, d = dy.shape
    ar_off = (2 * RET_QK + 2 * RET_V + 3 * SB_W) // tn
    as_off = ar_off + d // tn

    def body(a_ref, w_ref, ar_ref, as_ref, r_ref, s_ref, dr_ref, ds_ref, dar_ref, das_ref):
        dm = _dot(a_ref[...], w_ref[...], _NT)
        sr, ss = _sigmoid(ar_ref[...]), _sigmoid(as_ref[...])
        dr_ref[...] = (dm * sr).astype(BF16)
        ds_ref[...] = (dm * ss).astype(BF16)
        dar_ref[...] = (dm * r_ref[...].astype(F32) * sr * (1.0 - sr)).astype(BF16)
        das_ref[...] = (dm * s_ref[...].astype(F32) * ss * (1.0 - ss)).astype(BF16)

    tile = (tm, tn)
    here = lambda j, i: (i, j)
    return _call("out_bwd", body, (d // tn, s // tm),
                 [(dy, (tm, d), lambda j, i: (i, 0)), (w_out, (tn, d), lambda j, i: (j, 0)),
                  (proj, tile, lambda j, i: (i, ar_off + j)), (proj, tile, lambda j, i: (i, as_off + j)),
                  (r_bf, tile, here), (s_bf, tile, here)],
                 [((s, d), BF16, tile, here)] * 4)


def _gn_bwd(dretg, ret, proj, gn_g, tm, riders=None):
    s = ret.shape[0]
    gate_off = (2 * RET_QK + RET_V) // RET_V

    def body(d_ref, r_ref, g_ref, w_ref, dg_ref, dret_ref, dw_ref):
        first = pl.program_id(0) == 0
        for h in range(HEADS):
            cols = slice(h * RET_DV, (h + 1) * RET_DV)
            o, g, w, dr = r_ref[:, cols], g_ref[:, cols], w_ref[:, cols], d_ref[:, cols]
            mu = jnp.sum(o, axis=1, keepdims=True) * (1.0 / RET_DV)
            xc = o - mu
            rstd = lax.rsqrt(jnp.sum(xc * xc, axis=1, keepdims=True) * (1.0 / RET_DV) + EPS)
            n = xc * rstd
            sg = _sigmoid(g)
            silu = g * sg
            dg_ref[:, cols] = (dr * n * w * (sg * (1.0 + g * (1.0 - sg)))).astype(BF16)
            _accum(dw_ref.at[:, cols], _colsum(dr * silu * n), first)
            dn = dr * silu * w
            m1 = jnp.sum(dn, axis=1, keepdims=True) * (1.0 / RET_DV)
            m2 = jnp.sum(dn * n, axis=1, keepdims=True) * (1.0 / RET_DV)
            dret_ref[:, cols] = (rstd * (dn - m1 - n * m2)).astype(BF16)

    row = lambda i: (i, 0)
    fix = lambda i: (0, 0)
    return _call("gn_bwd", body, (s // tm,),
                 [(dretg, (tm, RET_V), row), (ret, (tm, RET_V), row),
                  (proj, (tm, RET_V), lambda i: (i, gate_off)), (gn_g, (1, RET_V), fix)],
                 [((s, RET_V), BF16, (tm, RET_V), row), ((s, RET_V), BF16, (tm, RET_V), row),
                  ((1, RET_V), F32, (1, RET_V), fix)], riders=riders)


def _ret_bwd(qk_rot, v_bf, dret, log_gamma, tq, tk, riders=None):
    s = qk_rot.shape[0]
    n_pair = HEADS // 2
    pw = 2 * RET_DV
    _check_tiles(s, tq, tk)

    def body(lg_ref, q_ref, k_ref, v_ref, do_ref, dq_ref, dk_ref, dv_ref):
        hp, i = pl.program_id(0), pl.program_id(1)

        @pl.when(i == 0)
        def _():
            dk_ref[...] = jnp.zeros_like(dk_ref)
            dv_ref[...] = jnp.zeros_like(dv_ref)

        qs = _stack_heads(q_ref[...])
        lg_rows = _lg_rows(lg_ref, hp, tq)
        do0, do1 = do_ref[:, 0:RET_DV], do_ref[:, RET_DV:pw]

        def step(g, dq):
            js = [g * GROUP + sub for sub in range(GROUP)]
            rows = [_key_rows(j, tk) for j in js]
            ss = [_dot(qs, k_ref[rw, :], _NT) for rw in rows]
            dps = [jnp.concatenate([_dot(do0, v_ref[rw, 0:RET_DV], _NT), _dot(do1, v_ref[rw, RET_DV:pw], _NT)], axis=0)
                   for rw in rows]
            decs = [_ret_decay(lg_rows, i, j, tq, tk) for j in js]
            ps = [(sc * dec).astype(BF16) for sc, dec in zip(ss, decs)]
            dss = [(dp * dec).astype(BF16) for dp, dec in zip(dps, decs)]
            for p, ds, rw in zip(ps, dss, rows):
                dv_ref[rw, 0:RET_DV] += _dot(p[:tq], do0, _TN)
                dv_ref[rw, RET_DV:pw] += _dot(p[tq:], do1, _TN)
                dk_ref[rw, :] += _dot(ds, qs, _TN)
                dq = dq + _dot(_side_by_side(ds, tq), _stack_heads(k_ref[rw, :]))
            return dq

        dq_ref[...] = lax.fori_loop(0, _n_groups(i, tq, tk), step, jnp.zeros((tq, LANES), F32))

    blk = lambda hp, i: (i, hp)
    return _call("ret_bwd", body, (n_pair, s // tq),
                 [(log_gamma, None, pltpu.SMEM),
                  (qk_rot, (tq, LANES), blk),
                  (qk_rot, (s, LANES), lambda hp, i: (0, n_pair + hp)),
                  (v_bf, (s, pw), lambda hp, i: (0, hp)),
                  (dret, (tq, pw), blk)],
                 [((s, RET_QK), F32, (tq, LANES), blk),
                  ((s, RET_QK), F32, (s, LANES), lambda hp, i: (0, hp)),
                  ((s, RET_V), F32, (s, pw), lambda hp, i: (0, hp))], riders=riders)


def _sb_bwd(qkv, carries, do, tq, tk, riders=None):
    s = qkv.shape[0]
    n_pair = HEADS // 2
    _check_tiles(s, tq, tk)

    def body(q_ref, k_ref, v_ref, c_ref, do_ref, dq_ref, dk_ref, dv_ref):
        i = pl.program_id(1)

        @pl.when(i == 0)
        def _():
            dk_ref[...] = jnp.zeros_like(dk_ref)
            dv_ref[...] = jnp.zeros_like(dv_ref)

        upper2 = _tri2(tk, True)
        lower2 = _tri2(tk, False)
        lane = lax.broadcasted_iota(jnp.int32, (1, LANES), 1)
        qs = _stack_heads(q_ref[...])
        dos = _stack_heads(do_ref[...].astype(BF16))
        cms = jnp.concatenate([c_ref[:, 0:LANES], c_ref[:, LANES:2 * LANES]], axis=0)

        def step(g, carry):
            c_e, dq = carry
            js = [g * GROUP + sub for sub in range(GROUP)]
            rows = [_key_rows(j, tk) for j in js]
            _, log_bs, sticks, valids, _ = _sb_scores(qs, k_ref, i, js, tq, tk, upper2)
            das = [_dot(dos, v_ref[rw, :], _NT) for rw in rows]
            c_sticks = [jnp.sum(jnp.where(lane == j, cms, 0.0), axis=1, keepdims=True) for j in js]
            avals = [jnp.where(v, jnp.exp(lb + st + cst), 0.0)
                     for v, lb, st, cst in zip(valids, log_bs, sticks, c_sticks)]
            es = [a * da for a, da in zip(avals, das)]
            prefixes = [lax.dot_general(_split_bf16(e), lower2, _NN, preferred_element_type=F32) for e in es]
            betas = [jnp.exp(lb) for lb in log_bs]
            for sub in range(GROUP):
                dv_ref[rows[sub], :] += _dot(avals[sub], dos, _TN)
            for sub in range(GROUP):
                dz = jnp.where(valids[sub], es[sub] * (1.0 - betas[sub]) - (prefixes[sub] + c_e) * betas[sub],
                               0.0).astype(BF16)
                dk_ref[rows[sub], :] += _dot(dz, qs, _TN)
                dq = dq + _dot(_side_by_side(dz, tq), _stack_heads(k_ref[rows[sub], :]))
                c_e = c_e + jnp.sum(es[sub], axis=1, keepdims=True)
            return c_e, dq

        _, dq = lax.fori_loop(0, _n_groups(i, tq, tk), step,
                              (jnp.zeros((2 * tq, 1), F32), jnp.zeros((tq, LANES), F32)))
        dq_ref[...] = dq

    blk = lambda hp, i: (i, hp)
    return _call("sb_bwd", body, (n_pair, s // tq),
                 [(qkv, (tq, LANES), blk),
                  (qkv, (s, LANES), lambda hp, i: (0, n_pair + hp)),
                  (qkv, (s, LANES), lambda hp, i: (0, 2 * n_pair + hp)),
                  (carries, (tq, 2 * LANES), blk), (do, (tq, LANES), blk)],
                 [((s, SB_W), F32, (tq, LANES), blk),
                  ((s, SB_W), F32, (s, LANES), lambda hp, i: (0, hp)),
                  ((s, SB_W), F32, (s, LANES), lambda hp, i: (0, hp))], riders=riders)


def _assemble_dproj(dq_r, dk_r, dv_r, dg_r, dq_s, dk_s, dv_s, da_r, da_s, cos, sin, tm):
    s, d = da_r.shape
    width = 2 * RET_QK + 2 * RET_V + 3 * SB_W + 2 * d

    def body(dq_ref, dk_ref, dv_ref, dg_ref, dqs_ref, dks_ref, dvs_ref, dar_ref, das_ref, cos_ref, sin_ref, o_ref):
        lane = lax.broadcasted_iota(jnp.int32, (1, LANES), 1)
        first = jnp.bitwise_and(lane, RET_DQK - 1) < (RET_DQK // 2)
        cos, sin = cos_ref[...], sin_ref[...]
        for src, base, scale in ((dq_ref, 0, 1.0), (dk_ref, RET_QK, RET_DQK ** -0.5)):
            for g in range(RET_QK // LANES):
                v = src[:, g * LANES:(g + 1) * LANES]
                sw = jnp.where(first, pltpu.roll(v, LANES - RET_DQK // 2, 1), pltpu.roll(v, RET_DQK // 2, 1))
                o_ref[:, base + g * LANES:base + (g + 1) * LANES] = ((v * cos - sw * sin) * scale).astype(BF16)
        off = 2 * RET_QK
        o_ref[:, off:off + RET_V] = dv_ref[...].astype(BF16)
        off += RET_V
        o_ref[:, off:off + RET_V] = dg_ref[...]
        off += RET_V
        o_ref[:, off:off + SB_W] = (dqs_ref[...] * (SB_DH ** -0.5)).astype(BF16)
        off += SB_W
        o_ref[:, off:off + SB_W] = dks_ref[...].astype(BF16)
        off += SB_W
        o_ref[:, off:off + SB_W] = dvs_ref[...].astype(BF16)
        off += SB_W
        o_ref[:, off:off + d] = dar_ref[...]
        off += d
        o_ref[:, off:off + d] = das_ref[...]

    row = lambda i: (i, 0)
    ins = [(a, (tm, a.shape[1]), row) for a in (dq_r, dk_r, dv_r, dg_r, dq_s, dk_s, dv_s, da_r, da_s, cos, sin)]
    return _call("assemble_dproj", body, (s // tm,), ins, [((s, width), BF16, (tm, width), row)])[0]


def _in_bwd(dproj, w_in, x, dhres, mod, g1, tm, riders=None):
    s, d = x.shape
    width = dproj.shape[1]

    def body(a_ref, w_ref, x_ref, dh_ref, mod_ref, g_ref, dx_ref, dsh_ref, dsc_ref, dg_ref):
        first = pl.program_id(0) == 0
        dh = _dot(a_ref[...], w_ref[...], _NT)
        n1, r1 = _rms(x_ref[...], d)
        g1, sc1 = g_ref[...], mod_ref[:, d:2 * d]
        _accum(dsh_ref, _colsum(dh), first)
        _accum(dsc_ref, _colsum(dh * n1 * g1), first)
        _accum(dg_ref, _colsum(dh * n1 * (1.0 + sc1)), first)
        dx_ref[...] = dh_ref[...] + _rms_bwd(dh * g1 * (1.0 + sc1), n1, r1, d)

    row = lambda i: (i, 0)
    fix = lambda i: (0, 0)
    vec = ((1, d), F32, (1, d), fix)
    return _call("in_bwd", body, (s // tm,),
                 [(dproj, (tm, width), row), (w_in, (d, width), fix), (x, (tm, d), row), (dhres, (tm, d), row),
                  (mod, (1, 6 * d), fix), (g1, (1, d), fix)],
                 [((s, d), F32, (tm, d), row), vec, vec, vec], riders=riders)


def _adamw(w, g, m, v):
    m = ADAM_B1 * m + (1.0 - ADAM_B1) * g
    v = ADAM_B2 * v + (1.0 - ADAM_B2) * (g * g)
    m_hat = m / (1.0 - ADAM_B1 ** ADAM_STEP)
    v_hat = v / (1.0 - ADAM_B2 ** ADAM_STEP)
    delta = -ADAM_LR * (m_hat / (jnp.sqrt(v_hat) + ADAM_EPS) + ADAM_WD * w)
    return delta, m, v


def _adam_reduce(name, parts, w, m, v, tr):
    rws, cls = w.shape
    tr = min(tr, rws)
    n_parts = parts.shape[0]

    def body(p_ref, w_ref, m_ref, v_ref, g_out, d_out, m_out, v_out):
        g = p_ref[0].astype(F32)
        for k in range(1, n_parts):
            g = g + p_ref[k].astype(F32)
        delta, mn, vn = _adamw(w_ref[...], g, m_ref[...], v_ref[...])
        g_out[...] = g
        d_out[...] = delta
        m_out[...] = mn
        v_out[...] = vn

    row = lambda i: (i, 0)
    blk = (tr, cls)
    return _call(name, body, (rws // tr,),
                 [(parts, (n_parts, tr, cls), lambda i: (0, i, 0)), (w, blk, row), (m, blk, row), (v, blk, row)],
                 [((rws, cls), F32, blk, row)] * 4)


def _ada_bwd_adam(cs_t, dmod_cols, w, m, v):
    d, nc = w.shape

    def body(c_ref, dm_ref, w_ref, m_ref, v_ref, g_out, d_out, m_out, v_out):
        g = c_ref[0] * dm_ref[0:1, :]
        for r in range(1, N_DEV):
            g = g + c_ref[r] * dm_ref[r:r + 1, :]
        delta, mn, vn = _adamw(w_ref[...], g, m_ref[...], v_ref[...])
        g_out[...] = g
        d_out[...] = delta
        m_out[...] = mn
        v_out[...] = vn

    fix = lambda i: (0, 0)
    blk = (d, nc)
    return _call("ada_bwd_adam", body, (1,),
                 [(cs_t, (N_DEV, d, 1), lambda i: (0, 0, 0)), (dmod_cols, (N_DEV, nc), fix), (w, blk, fix), (m, blk, fix), (v, blk, fix)],
                 [((d, nc), F32, blk, fix)] * 4)


def _small_adam(parts, w, m, v):
    n = w.shape[1]

    def body(p_ref, w_ref, m_ref, v_ref, g_out, d_out, m_out, v_out):
        g = p_ref[0:1, :]
        for k in range(1, N_DEV):
            g = g + p_ref[k:k + 1, :]
        delta, mn, vn = _adamw(w_ref[...], g, m_ref[...], v_ref[...])
        g_out[...] = g
        d_out[...] = delta
        m_out[...] = mn
        v_out[...] = vn

    fix = lambda i: (0, 0)
    return _call("small_adam", body, (1,),
                 [(parts, (N_DEV, n), fix), (w, (1, n), fix), (m, (1, n), fix), (v, (1, n), fix)],
                 [((1, n), F32, (1, n), fix)] * 4)


def kernel(x, c, positions, ada_w, ada_b, pre_mix_g, post_mix_g, pre_ffn_g, post_ffn_g, w_in, ret_gn_g, w_ret_branch, w_sb_branch, w_out, w_ff1, w_ff2, loss_target, m_ada_w, m_ada_b, m_pre_mix_g, m_post_mix_g, m_pre_ffn_g, m_post_ffn_g, m_w_in, m_ret_gn_g, m_w_ret_branch, m_w_sb_branch, m_w_out, m_w_ff1, m_w_ff2, v_ada_w, v_ada_b, v_pre_mix_g, v_post_mix_g, v_pre_ffn_g, v_post_ffn_g, v_w_in, v_ret_gn_g, v_w_ret_branch, v_w_sb_branch, v_w_out, v_w_ff1, v_w_ff2):
    _, s, d = x.shape
    d_ff = w_ff1.shape[2] * N_DEV
    d_in = w_in.shape[2] * N_DEV
    me = 4 * lax.axis_index("x") + 2 * lax.axis_index("y") + lax.axis_index("c")
    x2, tgt = x[0], loss_target[0]

    core = lax.axis_index("c").astype(jnp.int32).reshape(1)
    bf = lambda w: w[0].astype(BF16)

    c_all, g_in = _exchange("gather_in", [c, bf(w_in)], ["gather", "gather_chip"])
    c_all = c_all.reshape(N_DEV, d)

    n_ada = ada_w.shape[2]
    cs_all = _silu_rows(c_all)
    ada_b_cols = lax.dynamic_slice(ada_b, (0, me * n_ada), (1, n_ada))
    mod_cols = _ada_fwd(cs_all, ada_w[0], ada_b_cols)
    mod_all = _exchange("gather_mod", [mod_cols], ["gather"])[0]
    mod = lax.dynamic_index_in_dim(mod_all, me, axis=1, keepdims=False).reshape(1, 6 * d)

    tm = min(256, s)
    h, g_in = _pre_norm(x2, pre_mix_g, mod, tm, riders=([g_in], ["forward"]))
    wf_in = jnp.moveaxis(g_in, 0, 1).reshape(d, d_in)
    proj = _matmul("in_proj", h, wf_in, "nn", s, 512, F32)
    pos_col = positions.reshape(s, 1).astype(F32)
    freqs = ROPE_BASE ** (-jnp.arange(0, RET_DQK, 2, dtype=F32) / RET_DQK)
    inv_freq = jnp.tile(freqs, LANES // (RET_DQK // 2)).reshape(1, LANES)
    qk_rot, v_bf, qkv_sb, cos_t, sin_t = _prep(proj, pos_col, inv_freq, tm)
    log_gamma = jnp.asarray(np.log1p(-(2.0 ** (-5.0 - np.arange(HEADS)))), F32)
    tq, tk = min(256, s), min(128, s)
    later = [bf(w_ret_branch), bf(w_sb_branch), bf(w_out), bf(w_ff1), bf(w_ff2)]
    sb, sb_carry, *later = _sb_fwd(qkv_sb, tq, tk, riders=(later, ["gather_chip"] * 5))
    ret, retg, g_ret, g_sb, g_out, g_ff1, g_ff2 = _ret_fwd(qk_rot, v_bf, proj, ret_gn_g, log_gamma, tq, tk,
                                                           riders=(later, ["forward"] * 5))
    wf_ret = g_ret.reshape(RET_V, d)
    wf_sb = jnp.moveaxis(g_sb, 0, 1).reshape(SB_W, d)
    wf_out = g_out.reshape(d, d)
    wf_ff1 = jnp.moveaxis(g_ff1, 0, 1).reshape(d, d_ff)
    wf_ff2 = g_ff2.reshape(d_ff, d)
    mixed, r_bf, s_bf = _merge(retg, sb, wf_ret, wf_sb, proj, tm, min(512, d))
    y, hres, h2 = _out_proj(mixed, wf_out, x2, mod, post_mix_g, pre_ffn_g, tm)
    u, act = _ff1(h2, wf_ff1, s, 512)
    dout, df, loss_sum, d_gt2, d_gp2 = _ff2_loss(act, wf_ff2, hres, tgt, mod, post_ffn_g, tm)

    du = _ff2_bwd(df, wf_ff2, u, s, 512)
    gw_ff2 = _matmul("grad_w_ff2", act, df, "tn", 512, d, BF16).reshape(N_DEV, d_ff // N_DEV, d)
    gw_ff1 = _matmul("grad_w_ff1", h2, du, "tn", d, d_ff // N_DEV, BF16, blocked_out=True)
    dhres, dy, d_sh2, d_sc2, d_g2, d_gt1, d_gp1, t_ff1, t_ff2 = _ff1_bwd(
        du, wf_ff1, hres, dout, y, mod, pre_ffn_g, post_mix_g, tm, riders=([gw_ff1, gw_ff2], ["pair"] * 2))
    s_ff1 = _pair_sum("pair_sum_ff1", gw_ff1, t_ff1, core, 256)
    s_ff2 = _pair_sum("pair_sum_ff2", gw_ff2, t_ff2, core, 256)
    d_r, d_s, da_r, da_s = _out_bwd(dy, wf_out, proj, r_bf, s_bf, tm, min(512, d))
    gw_out = _matmul("grad_w_out", mixed, dy, "tn", 512, d, BF16).reshape(N_DEV, d // N_DEV, d)
    dretg = _matmul("ret_branch_bwd", d_r, wf_ret, "nt", s, 512, BF16)
    dsb = _matmul("sb_branch_bwd", d_s, wf_sb, "nt", s, 512, F32)
    gw_ret = _matmul("grad_w_ret", retg, d_r, "tn", 512, d, BF16).reshape(N_DEV, RET_V // N_DEV, d)
    gw_sb = _matmul("grad_w_sb", sb, d_s, "tn", 512, d // N_DEV, BF16, blocked_out=True)
    dq_s, dk_s, dv_s, p_ff1, p_ff2 = _sb_bwd(qkv_sb, sb_carry, dsb, tq, tk,
                                             riders=([s_ff1, s_ff2], ["chip_scatter"] * 2))
    dg_r, dret, d_gn, t_out, t_ret_w, t_sb_w = _gn_bwd(dretg, ret, proj, ret_gn_g, tm,
                                                       riders=([gw_out, gw_ret, gw_sb], ["pair"] * 3))
    s_out = _pair_sum("pair_sum_out", gw_out, t_out, core, 256)
    s_ret = _pair_sum("pair_sum_ret", gw_ret, t_ret_w, core, 256)
    s_sb = _pair_sum("pair_sum_sb", gw_sb, t_sb_w, core, 256)
    dq_r, dk_r, dv_r, p_out, p_ret, p_sb = _ret_bwd(qk_rot, v_bf, dret, log_gamma, tq, tk,
                                                    riders=([s_out, s_ret, s_sb], ["chip_scatter"] * 3))
    dproj = _assemble_dproj(dq_r, dk_r, dv_r, dg_r, dq_s, dk_s, dv_s, da_r, da_s, cos_t, sin_t, tm)
    gw_in_full = _matmul("grad_w_in", h, dproj, "tn", d, 512, BF16)
    gw_in = jnp.moveaxis(gw_in_full.reshape(d, N_DEV, d_in // N_DEV), 1, 0)
    t_in = _exchange("pair_in", [gw_in], ["pair"])[0]
    s_in = _pair_sum("pair_sum_in", gw_in, t_in, core, 256)
    grad_x, d_sh1, d_sc1, d_g1, p_in = _in_bwd(dproj, wf_in, x2, dhres, mod, pre_mix_g, tm,
                                               riders=([s_in], ["chip_scatter"]))
    small = jnp.concatenate([d_sh1, d_sc1, d_gt1, d_sh2, d_sc2, d_gt2, d_g1, d_gp1, d_g2, d_gp2, d_gn], axis=1)
    small_all = _exchange("gather_small", [small], ["gather"])[0].reshape(N_DEV, small.shape[1])
    parts = [p_in, p_ret, p_sb, p_out, p_ff1, p_ff2]

    res = {}
    names = ["w_in", "w_ret_branch", "w_sb_branch", "w_out", "w_ff1", "w_ff2"]
    ws = [w_in, w_ret_branch, w_sb_branch, w_out, w_ff1, w_ff2]
    ms = [m_w_in, m_w_ret_branch, m_w_sb_branch, m_w_out, m_w_ff1, m_w_ff2]
    vs = [v_w_in, v_w_ret_branch, v_w_sb_branch, v_w_out, v_w_ff1, v_w_ff2]
    for nm, p, w, m, v in zip(names, parts, ws, ms, vs):
        res[nm] = [o[None] for o in _adam_reduce("adam_" + nm, p, w[0], m[0], v[0], 256)]
    dmod_cols = lax.dynamic_slice(small_all, (0, me * n_ada), (N_DEV, n_ada))
    res["ada_w"] = [o[None] for o in _ada_bwd_adam(cs_all.reshape(N_DEV, d, 1), dmod_cols, ada_w[0], m_ada_w[0], v_ada_w[0])]
    vec_names = ["ada_b", "pre_mix_g", "post_mix_g", "pre_ffn_g", "post_ffn_g", "ret_gn_g"]
    cat = lambda xs: jnp.concatenate(xs, axis=1)
    packed = _small_adam(small_all,
                         cat([ada_b, pre_mix_g, post_mix_g, pre_ffn_g, post_ffn_g, ret_gn_g]),
                         cat([m_ada_b, m_pre_mix_g, m_post_mix_g, m_pre_ffn_g, m_post_ffn_g, m_ret_gn_g]),
                         cat([v_ada_b, v_pre_mix_g, v_post_mix_g, v_pre_ffn_g, v_post_ffn_g, v_ret_gn_g]))
    off = 0
    for nm, width in zip(vec_names, [6 * d, d, d, d, d, RET_V]):
        res[nm] = [p[:, off:off + width] for p in packed]
        off += width

    loss = (0.5 / d) * lax.psum(loss_sum[0, 0], AXES)
    order = ["ada_w", "ada_b", "pre_mix_g", "post_mix_g", "pre_ffn_g", "post_ffn_g", "w_in", "ret_gn_g",
             "w_ret_branch", "w_sb_branch", "w_out", "w_ff1", "w_ff2"]
    outs = [loss, grad_x[None]]
    for k in range(4):
        outs += [res[nm][k] for nm in order]
    return tuple(outs)
```

```python
import functools

import numpy as np
import jax
import jax.numpy as jnp
from jax import lax
from jax.experimental import pallas as pl
from jax.experimental.pallas import tpu as pltpu

F32 = jnp.float32
BF16 = jnp.bfloat16
N_DEV = 8
AXES = ("x", "y", "c")

EPS = 1e-6
CHUNK = 64
CHUNK_SHIFT = 6
HEADS = 8
RET_DQK = 64
RET_DV = 128
SB_DH = 64
RET_QK = HEADS * RET_DQK
RET_V = HEADS * RET_DV
SB_W = HEADS * SB_DH
ROPE_BASE = 10000.0
LANES = 128

ADAM_LR = 0.001
ADAM_B1 = 0.9
ADAM_B2 = 0.999
ADAM_EPS = 1e-08
ADAM_WD = 0.01
ADAM_STEP = 10

VMEM_LIMIT = 56 * 1024 * 1024

_NN = (((1,), (0,)), ((), ()))
_NT = (((1,), (1,)), ((), ()))
_TN = (((0,), (0,)), ((), ()))


def _dot(a, b, dims=_NN):
    if a.dtype != BF16:
        a = a.astype(BF16)
    if b.dtype != BF16:
        b = b.astype(BF16)
    return lax.dot_general(a, b, dims, preferred_element_type=F32)


def _dot_split(a, b):
    hi = a.astype(BF16)
    lo = (a - hi.astype(F32)).astype(BF16)
    return (lax.dot_general(hi, b, _NN, preferred_element_type=F32)
            + lax.dot_general(lo, b, _NN, preferred_element_type=F32))


def _sigmoid(x):
    return 1.0 / (1.0 + jnp.exp(-x))


def _rms(x, d):
    r = lax.rsqrt(jnp.sum(x * x, axis=1, keepdims=True) * (1.0 / d) + EPS)
    return x * r, r


def _rms_bwd(dn, n, r, d):
    return r * (dn - n * (jnp.sum(dn * n, axis=1, keepdims=True) * (1.0 / d)))


def _colsum(v):
    return jnp.sum(v, axis=0, keepdims=True)


def _accum(ref, val, first):
    @pl.when(first)
    def _():
        ref[...] = val

    @pl.when(jnp.logical_not(first))
    def _():
        ref[...] += val


KIND_SLOTS = {"gather": N_DEV, "gather_chip": N_DEV, "forward": N_DEV, "pair": N_DEV // 2, "chip_scatter": N_DEV // 2}
SEMS_PER_ARRAY = N_DEV - 1


def _exchange_copies(ins, outs, send_sems, recv_sems, local_sems, kinds):
    x, y, c = (lax.axis_index(a) for a in AXES)
    me, chip, sibling = 4 * x + 2 * y + c, 2 * x + y, (x, y, 1 - c)
    mesh_id = pl.DeviceIdType.MESH
    other_chips = []
    for k in range(1, N_DEV // 2):
        px = 1 - x if k & 2 else x
        py = 1 - y if k & 1 else y
        other_chips.append((px, py))
    copies = []
    for i, kind in enumerate(kinds):
        def remote(src, dst, k, to, i=i):
            return pltpu.make_async_remote_copy(
                src_ref=src, dst_ref=dst, send_sem=send_sems.at[i * SEMS_PER_ARRAY + k],
                recv_sem=recv_sems.at[i * SEMS_PER_ARRAY + k], device_id=to, device_id_type=mesh_id)

        if kind == "gather":
            copies.append(pltpu.make_async_copy(ins[i], outs[i].at[me], local_sems.at[i]))
            for k in range(1, N_DEV):
                to = (1 - x if k & 4 else x, 1 - y if k & 2 else y, 1 - c if k & 1 else c)
                copies.append(remote(ins[i], outs[i].at[me], k - 1, to))
        elif kind == "gather_chip":
            copies.append(pltpu.make_async_copy(ins[i], outs[i].at[me], local_sems.at[i]))
            copies.append(remote(ins[i], outs[i].at[me], 0, sibling))
            for k, (px, py) in enumerate(other_chips):
                copies.append(remote(ins[i], outs[i].at[me], 1 + k, (px, py, c)))
        elif kind == "forward":
            for k, (px, py) in enumerate(other_chips):
                slot = 4 * px + 2 * py + c
                copies.append(remote(outs[i].at[slot], outs[i].at[slot], k, sibling))
        elif kind == "pair":
            for k in range(N_DEV // 2):
                copies.append(remote(ins[i].at[2 * k + 1 - c], outs[i].at[k], k, sibling))
        elif kind == "chip_scatter":
            copies.append(pltpu.make_async_copy(ins[i].at[chip], outs[i].at[chip], local_sems.at[i]))
            for k, (px, py) in enumerate(other_chips):
                copies.append(remote(ins[i].at[2 * px + py], outs[i].at[chip], k, (px, py, c)))
        else:
            raise ValueError(kind)
    return copies


def _exchange_shapes(arrays, kinds):
    shapes = []
    for a, kind in zip(arrays, kinds):
        tail = a.shape if kind in ("gather", "gather_chip") else a.shape[1:]
        shapes.append(jax.ShapeDtypeStruct((KIND_SLOTS[kind],) + tuple(tail), a.dtype))
    return shapes


def _exchange_sems(n):
    return [pltpu.SemaphoreType.DMA((n * SEMS_PER_ARRAY,)), pltpu.SemaphoreType.DMA((n * SEMS_PER_ARRAY,)),
            pltpu.SemaphoreType.DMA((n,))]


def _call(name, body, grid, ins, outs, scratch=(), riders=None, prefetch=None):
    any_spec = pl.BlockSpec(memory_space=pl.ANY)
    in_specs = [pl.BlockSpec(memory_space=im) if bs is None else pl.BlockSpec(bs, im) for _, bs, im in ins]
    out_specs = [pl.BlockSpec(bs, im) for _, _, bs, im in outs]
    out_shape = [jax.ShapeDtypeStruct(s, d) for s, d, _, _ in outs]
    operands = [a for a, _, _ in ins]
    scratch = list(scratch)
    aliases = {}
    n_pre = 0 if prefetch is None else 1
    kernel = functools.partial(body) if prefetch is None else (lambda _, *refs: body(*refs))
    if riders is not None:
        arrays, kinds = riders
        nr, n_in, n_out, n_scr = len(arrays), len(ins), len(outs), len(scratch)

        def kernel(*refs):
            refs = refs[n_pre:]
            own_in, ride_in = refs[:n_in], refs[n_in:n_in + nr]
            own_out = refs[n_in + nr:n_in + nr + n_out]
            ride_out = refs[n_in + nr + n_out:n_in + 2 * nr + n_out]
            own_scr = refs[n_in + 2 * nr + n_out:n_in + 2 * nr + n_out + n_scr]
            sems = refs[n_in + 2 * nr + n_out + n_scr:]
            ids = [pl.program_id(a) for a in range(len(grid))]
            first = functools.reduce(jnp.logical_and, [i == 0 for i in ids])
            last = functools.reduce(jnp.logical_and, [i == g - 1 for i, g in zip(ids, grid)])

            @pl.when(first)
            def _():
                for cp in _exchange_copies(ride_in, ride_out, *sems, kinds):
                    cp.start()

            body(*own_in, *own_out, *own_scr)

            @pl.when(last)
            def _():
                for cp in _exchange_copies(ride_in, ride_out, *sems, kinds):
                    cp.wait()

        in_specs += [any_spec] * nr
        out_specs += [any_spec] * nr
        out_shape += _exchange_shapes(arrays, kinds)
        operands += list(arrays)
        scratch += _exchange_sems(nr)
        aliases = {n_pre + n_in + r: n_out + r for r, kind in enumerate(kinds) if kind == "forward"}
    params = pltpu.CompilerParams(dimension_semantics=("arbitrary",) * len(grid), vmem_limit_bytes=VMEM_LIMIT)
    if prefetch is None:
        return pl.pallas_call(kernel, name=name, grid=grid, in_specs=in_specs, out_specs=out_specs,
                              out_shape=out_shape, scratch_shapes=scratch, input_output_aliases=aliases,
                              compiler_params=params)(*operands)
    grid_spec = pltpu.PrefetchScalarGridSpec(num_scalar_prefetch=1, grid=grid, in_specs=in_specs,
                                             out_specs=out_specs, scratch_shapes=scratch)
    return pl.pallas_call(kernel, name=name, grid_spec=grid_spec, out_shape=out_shape,
                          input_output_aliases=aliases, compiler_params=params)(prefetch, *operands)


def _exchange(name, arrays, kinds):
    n = len(arrays)

    def body(*refs):
        copies = _exchange_copies(refs[:n], refs[n:2 * n], *refs[2 * n:], kinds)
        for cp in copies:
            cp.start()
        for cp in copies:
            cp.wait()

    any_spec = pl.BlockSpec(memory_space=pl.ANY)
    return pl.pallas_call(
        functools.partial(body),
        name=name,
        in_specs=[any_spec] * n,
        out_specs=[any_spec] * n,
        out_shape=_exchange_shapes(arrays, kinds),
        scratch_shapes=_exchange_sems(n),
        input_output_aliases={i: i for i, kind in enumerate(kinds) if kind == "forward"},
    )(*arrays)


def _pair_sum(name, mine, theirs, my_core, tr):
    _, rws, cls = mine.shape
    tr = min(tr, rws)

    def body(a_ref, b_ref, o_ref):
        o_ref[...] = (a_ref[...].astype(F32) + b_ref[...].astype(F32)).astype(o_ref.dtype)

    return _call(name, body, (N_DEV // 2, rws // tr),
                 [(mine, (None, tr, cls), lambda k, r, core: (2 * k + core[0], r, 0)),
                  (theirs, (None, tr, cls), lambda k, r, core: (k, r, 0))],
                 [((N_DEV // 2, rws, cls), mine.dtype, (None, tr, cls), lambda k, r, core: (k, r, 0))],
                 prefetch=my_core)[0]


def _matmul(name, a, b, kind, tm, tn, out_dtype, blocked_out=False):
    if kind == "tn":
        kdim, m = a.shape
    else:
        m, kdim = a.shape
    n = b.shape[0] if kind == "nt" else b.shape[1]
    tm, tn = min(tm, m), min(tn, n)
    dims = {"nn": _NN, "nt": _NT, "tn": _TN}[kind]

    def body(a_ref, b_ref, o_ref):
        o_ref[...] = _dot(a_ref[...], b_ref[...], dims).astype(o_ref.dtype)

    a_spec = (a, (kdim, tm), lambda j, i: (0, i)) if kind == "tn" else (a, (tm, kdim), lambda j, i: (i, 0))
    b_spec = (b, (tn, kdim), lambda j, i: (j, 0)) if kind == "nt" else (b, (kdim, tn), lambda j, i: (0, j))
    if blocked_out:
        out = ((n // tn, m, tn), out_dtype, (None, tm, tn), lambda j, i: (j, i, 0))
    else:
        out = ((m, n), out_dtype, (tm, tn), lambda j, i: (i, j))
    return _call(name, body, (n // tn, m // tm), [a_spec, b_spec], [out])[0]


def _ada_fwd(cs_all, ada_w, ada_b_cols):
    def body(c_ref, w_ref, b_ref, o_ref):
        o_ref[...] = lax.dot_general(c_ref[...], w_ref[...], _NN, preferred_element_type=F32,
                                     precision=lax.Precision.HIGHEST) + b_ref[...]

    r, d = cs_all.shape
    nc = ada_w.shape[1]
    return _call("ada_fwd", body, (1,),
                 [(cs_all, (r, d), lambda i: (0, 0)), (ada_w, (d, nc), lambda i: (0, 0)),
                  (ada_b_cols, (1, nc), lambda i: (0, 0))],
                 [((r, nc), F32, (r, nc), lambda i: (0, 0))])[0]


def _silu_rows(c_all):
    def body(c_ref, o_ref):
        v = c_ref[...]
        o_ref[...] = v * _sigmoid(v)

    return _call("silu_c", body, (1,), [(c_all, c_all.shape, lambda i: (0, 0))],
                 [(c_all.shape, F32, c_all.shape, lambda i: (0, 0))])[0]


def _pre_norm(x, g, mod, tm, riders=None):
    s, d = x.shape

    def body(x_ref, g_ref, mod_ref, h_ref):
        n, _ = _rms(x_ref[...], d)
        sh, sc = mod_ref[:, 0:d], mod_ref[:, d:2 * d]
        h_ref[...] = (n * g_ref[...] * (1.0 + sc) + sh).astype(BF16)

    return _call("pre_norm", body, (s // tm,),
                 [(x, (tm, d), lambda i: (i, 0)), (g, (1, d), lambda i: (0, 0)),
                  (mod, (1, 6 * d), lambda i: (0, 0))],
                 [((s, d), BF16, (tm, d), lambda i: (i, 0))], riders=riders)


LOG2E = 1.4426950408889634
LN2 = 0.6931471805599453


def _decay_scale(lg_ref, idx, g, sign):
    return jnp.exp((sign * idx) * lg_ref[:, g * LANES:(g + 1) * LANES])


def _prep(proj, pos_col, idx_col, inv_freq, lg_lanes, tm):
    s = proj.shape[0]
    sb_off = (2 * RET_QK + 2 * RET_V) // (3 * SB_W)
    n_q = RET_QK // LANES

    def body(qk_ref, v_ref, sb_ref, pos_ref, idx_ref, f_ref, lg_ref, qk_out, v_out, sb_out, cos_out, sin_out):
        ang = pos_ref[...] * f_ref[...]
        lane = lax.broadcasted_iota(jnp.int32, (1, LANES), 1)
        first = jnp.bitwise_and(lane, RET_DQK - 1) < (RET_DQK // 2)
        cos = jnp.cos(ang)
        sin = jnp.where(first, -1.0, 1.0) * jnp.sin(ang)
        cos_out[...] = cos
        sin_out[...] = sin
        idx = idx_ref[...]
        for g in range(2 * n_q):
            v = qk_ref[:, g * LANES:(g + 1) * LANES]
            sw = jnp.where(first, pltpu.roll(v, LANES - RET_DQK // 2, 1), pltpu.roll(v, RET_DQK // 2, 1))
            r = v * cos + sw * sin
            if g < n_q:
                r = r * _decay_scale(lg_ref, idx, g, 1.0)
            else:
                r = r * (_decay_scale(lg_ref, idx, g - n_q, -1.0) * (RET_DQK ** -0.5))
            qk_out[:, g * LANES:(g + 1) * LANES] = r.astype(BF16)
        v_out[...] = v_ref[...].astype(BF16)
        sb_out[:, 0:SB_W] = (sb_ref[:, 0:SB_W] * (SB_DH ** -0.5 * LOG2E)).astype(BF16)
        sb_out[:, SB_W:3 * SB_W] = sb_ref[:, SB_W:3 * SB_W].astype(BF16)

    return _call("prep", body, (s // tm,),
                 [(proj, (tm, 2 * RET_QK), lambda i: (i, 0)),
                  (proj, (tm, RET_V), lambda i: (i, 2 * RET_QK // RET_V)),
                  (proj, (tm, 3 * SB_W), lambda i: (i, sb_off)),
                  (pos_col, (tm, 1), lambda i: (i, 0)),
                  (idx_col, (tm, 1), lambda i: (i, 0)),
                  (inv_freq, (1, LANES), lambda i: (0, 0)),
                  (lg_lanes, (1, RET_QK), lambda i: (0, 0))],
                 [((s, 2 * RET_QK), BF16, (tm, 2 * RET_QK), lambda i: (i, 0)),
                  ((s, RET_V), BF16, (tm, RET_V), lambda i: (i, 0)),
                  ((s, 3 * SB_W), BF16, (tm, 3 * SB_W), lambda i: (i, 0)),
                  ((s, LANES), F32, (tm, LANES), lambda i: (i, 0)),
                  ((s, LANES), F32, (tm, LANES), lambda i: (i, 0))])


def _head_mask(hh):
    lane = lax.broadcasted_iota(jnp.int32, (1, LANES), 1)
    return (lane >= RET_DQK) if hh else (lane < RET_DQK)


def _masked(v, m):
    return jnp.where(m, v, jnp.zeros_like(v))


SB_GROUP = 4
RET_GROUP = 4


def _stack_heads(v):
    return jnp.concatenate([_masked(v, _head_mask(0)), _masked(v, _head_mask(1))], axis=0)


def _side_by_side(v, t):
    return jnp.concatenate([v[:t], v[t:]], axis=1)


def _split_bf16(v):
    hi = v.astype(BF16)
    lo = (v - hi.astype(F32)).astype(BF16)
    return jnp.concatenate([hi, lo], axis=1)


def _tile_pos(i, j, tq, tk):
    row = jnp.bitwise_and(lax.broadcasted_iota(jnp.int32, (2 * tq, tk), 0), tq - 1) + i * tq
    col = lax.broadcasted_iota(jnp.int32, (2 * tq, tk), 1) + j * tk
    return row, col


def _n_groups(i, tq, tk, grp):
    return ((i + 1) * (tq // tk) + grp - 1) // grp


def _n_full(i, tq, tk, grp):
    return (i * (tq // tk)) // grp


def _key_rows(j, tk):
    return pl.ds(pl.multiple_of(j * tk, tk), tk)


def _ret_weight(lg_rows, i, j, tq, tk):
    row, col = _tile_pos(i, j, tq, tk)
    same = jnp.right_shift(col, CHUNK_SHIFT) == jnp.right_shift(row, CHUNK_SHIFT)
    later = jnp.where(same, jnp.exp((2.0 * lg_rows) * (col - row).astype(F32)), 0.0)
    return jnp.where(col <= row, 1.0, later)


def _lg_rows(lg_ref, hp, tq):
    first = lax.broadcasted_iota(jnp.int32, (2 * tq, 1), 0) < tq
    return jnp.where(first, lg_ref[2 * hp], lg_ref[2 * hp + 1])


def _check_tiles(s, tq, tk, grp):
    assert tq % tk == 0 and tq & (tq - 1) == 0 and tk & (tk - 1) == 0
    assert s % tq == 0 and (s // tk) % grp == 0 and s // tk <= LANES


def _ret_fwd(qk_rot, v_bf, proj, gn_g, log_gamma, tq, tk, riders=None):
    s = qk_rot.shape[0]
    gate_off = (2 * RET_QK + RET_V) // (2 * RET_DV)
    n_pair = HEADS // 2
    _check_tiles(s, tq, tk, RET_GROUP)

    def body(lg_ref, q_ref, k_ref, v_ref, g_ref, w_ref, ret_ref, rg_ref):
        hp, i = pl.program_id(0), pl.program_id(1)
        qs = _stack_heads(q_ref[...])
        lg_rows = _lg_rows(lg_ref, hp, tq)

        def make_step(near_diagonal):
            def step(g, carry):
                o0, o1 = carry
                js = [g * RET_GROUP + sub for sub in range(RET_GROUP)]
                rows = [_key_rows(j, tk) for j in js]
                ss = [_dot(qs, k_ref[rw, :], _NT) for rw in rows]
                if near_diagonal:
                    ss = [sc * _ret_weight(lg_rows, i, j, tq, tk) for sc, j in zip(ss, js)]
                for sc, rw in zip(ss, rows):
                    p = sc.astype(BF16)
                    o0 = o0 + _dot(p[:tq], v_ref[rw, 0:RET_DV])
                    o1 = o1 + _dot(p[tq:], v_ref[rw, RET_DV:2 * RET_DV])
                return o0, o1
            return step

        zero = jnp.zeros((tq, RET_DV), F32)
        n_full = _n_full(i, tq, tk, RET_GROUP)
        outs = lax.fori_loop(0, n_full, make_step(False), (zero, zero))
        outs = lax.fori_loop(n_full, _n_groups(i, tq, tk, RET_GROUP), make_step(True), outs)
        for hh, o in enumerate(outs):
            cols = slice(hh * RET_DV, (hh + 1) * RET_DV)
            ret_ref[:, cols] = o
            mu = jnp.sum(o, axis=1, keepdims=True) * (1.0 / RET_DV)
            xc = o - mu
            var = jnp.sum(xc * xc, axis=1, keepdims=True) * (1.0 / RET_DV)
            nrm = xc * lax.rsqrt(var + EPS) * w_ref[:, cols]
            g = g_ref[:, cols]
            rg_ref[:, cols] = (g * _sigmoid(g) * nrm).astype(BF16)

    pw = 2 * RET_DV
    blk = lambda hp, i: (i, hp)
    return _call("ret_fwd", body, (n_pair, s // tq),
                 [(log_gamma, None, pltpu.SMEM),
                  (qk_rot, (tq, LANES), blk),
                  (qk_rot, (s, LANES), lambda hp, i: (0, n_pair + hp)),
                  (v_bf, (s, pw), lambda hp, i: (0, hp)),
                  (proj, (tq, pw), lambda hp, i: (i, gate_off + hp)),
                  (gn_g, (1, pw), lambda hp, i: (0, hp))],
                 [((s, RET_V), F32, (tq, pw), blk), ((s, RET_V), BF16, (tq, pw), blk)], riders=riders)


def _tri2(tk, strict_upper):
    r = jnp.bitwise_and(lax.broadcasted_iota(jnp.int32, (2 * tk, tk), 0), tk - 1)
    cc = lax.broadcasted_iota(jnp.int32, (2 * tk, tk), 1)
    return ((r > cc) if strict_upper else (r < cc)).astype(BF16)


def _sb_scores(qs, k_ref, i, js, tq, tk, tri2, near_diagonal):
    zs = [_dot(qs, k_ref[_key_rows(j, tk), :], _NT) for j in js]
    log1ps = [jnp.log2(1.0 + jnp.exp2(-jnp.abs(z))) for z in zs]
    log_1ms = [-jnp.maximum(z, 0.0) - t for z, t in zip(zs, log1ps)]
    log_bs = [jnp.minimum(z, 0.0) - t for z, t in zip(zs, log1ps)]
    valids = [None] * len(js)
    if near_diagonal:
        valids = []
        for j in js:
            row, col = _tile_pos(i, j, tq, tk)
            valids.append(col < row)
        log_1ms = [jnp.where(v, l, 0.0) for v, l in zip(valids, log_1ms)]
    sticks = [lax.dot_general(_split_bf16(l), tri2, _NN, preferred_element_type=F32) for l in log_1ms]
    sums = [jnp.sum(l, axis=1, keepdims=True) for l in log_1ms]
    return log_1ms, log_bs, sticks, valids, sums


def _sb_weights(log_b, stick, c, valid):
    a = jnp.exp2(log_b + stick + c)
    return a if valid is None else jnp.where(valid, a, 0.0)


def _sb_fwd(qkv, tq, tk, riders=None):
    s = qkv.shape[0]
    n_pair = HEADS // 2
    _check_tiles(s, tq, tk, SB_GROUP)

    def body(q_ref, k_ref, v_ref, o_ref, carry_ref):
        i = pl.program_id(1)
        upper2 = _tri2(tk, True)
        lane = lax.broadcasted_iota(jnp.int32, (1, LANES), 1)
        qs = _stack_heads(q_ref[...])
        carry_ref[...] = jnp.zeros_like(carry_ref)
        n_full, n_groups = _n_full(i, tq, tk, SB_GROUP), _n_groups(i, tq, tk, SB_GROUP)

        def make_step(near_diagonal, last):
            def step(n, carry):
                c, o = carry
                g = last - 1 - n
                js = [g * SB_GROUP + sub for sub in range(SB_GROUP)]
                _, log_bs, sticks, valids, sums = _sb_scores(qs, k_ref, i, js, tq, tk, upper2, near_diagonal)
                cs = [None] * SB_GROUP
                for sub in reversed(range(SB_GROUP)):
                    cs[sub] = c
                    c = c + sums[sub]
                for sub, j in enumerate(js):
                    a = _sb_weights(log_bs[sub], sticks[sub], cs[sub], valids[sub])
                    o = o + _dot(_side_by_side(a.astype(BF16), tq), _stack_heads(v_ref[_key_rows(j, tk), :]))
                for hh in range(2):
                    cols = slice(hh * LANES, (hh + 1) * LANES)
                    cm = carry_ref[:, cols]
                    for sub, j in enumerate(js):
                        cm = jnp.where(lane == j, cs[sub][hh * tq:(hh + 1) * tq], cm)
                    carry_ref[:, cols] = cm
                return c, o
            return step

        carry = (jnp.zeros((2 * tq, 1), F32), jnp.zeros((tq, LANES), F32))
        carry = lax.fori_loop(0, n_groups - n_full, make_step(True, n_groups), carry)
        _, acc = lax.fori_loop(0, n_full, make_step(False, n_full), carry)
        o_ref[...] = acc

    return _call("sb_fwd", body, (n_pair, s // tq),
                 [(qkv, (tq, LANES), lambda hp, i: (i, hp)),
                  (qkv, (s, LANES), lambda hp, i: (0, n_pair + hp)),
                  (qkv, (s, LANES), lambda hp, i: (0, 2 * n_pair + hp))],
                 [((s, SB_W), F32, (tq, LANES), lambda hp, i: (i, hp)),
                  ((s, HEADS * LANES), F32, (tq, 2 * LANES), lambda hp, i: (i, hp))], riders=riders)


def _merge(retg, sb, w_ret, w_sb, proj, tm, tn):
    s, d = retg.shape[0], w_ret.shape[1]
    ar_off = (2 * RET_QK + 2 * RET_V + 3 * SB_W) // tn
    as_off = ar_off + d // tn

    def body(rg_ref, sb_ref, wr_ref, ws_ref, ar_ref, as_ref, mix_ref, r_ref, s_ref):
        rr = _dot(rg_ref[...], wr_ref[...])
        ss = _dot(sb_ref[...], ws_ref[...])
        mix_ref[...] = (_sigmoid(ar_ref[...]) * rr + _sigmoid(as_ref[...]) * ss).astype(BF16)
        r_ref[...] = rr.astype(BF16)
        s_ref[...] = ss.astype(BF16)

    tile = (tm, tn)
    return _call("merge", body, (d // tn, s // tm),
                 [(retg, (tm, RET_V), lambda j, i: (i, 0)), (sb, (tm, SB_W), lambda j, i: (i, 0)),
                  (w_ret, (RET_V, tn), lambda j, i: (0, j)), (w_sb, (SB_W, tn), lambda j, i: (0, j)),
                  (proj, tile, lambda j, i: (i, ar_off + j)), (proj, tile, lambda j, i: (i, as_off + j))],
                 [((s, d), BF16, tile, lambda j, i: (i, j))] * 3)


def _out_proj(mixed, w_out, x, mod, gp1, g2, tm):
    s, d = x.shape

    def body(a_ref, w_ref, x_ref, mod_ref, gp_ref, g2_ref, y_ref, hres_ref, h2_ref):
        y = _dot(a_ref[...], w_ref[...])
        y_ref[...] = y
        ny, _ = _rms(y, d)
        hres = x_ref[...] + mod_ref[:, 2 * d:3 * d] * (ny * gp_ref[...])
        hres_ref[...] = hres
        n2, _ = _rms(hres, d)
        h2_ref[...] = (n2 * g2_ref[...] * (1.0 + mod_ref[:, 4 * d:5 * d]) + mod_ref[:, 3 * d:4 * d]).astype(BF16)

    row = lambda i: (i, 0)
    fix = lambda i: (0, 0)
    return _call("out_proj", body, (s // tm,),
                 [(mixed, (tm, d), row), (w_out, (d, d), fix), (x, (tm, d), row),
                  (mod, (1, 6 * d), fix), (gp1, (1, d), fix), (g2, (1, d), fix)],
                 [((s, d), F32, (tm, d), row), ((s, d), F32, (tm, d), row), ((s, d), BF16, (tm, d), row)])


def _ff1(h2, w_ff1, tm, tn):
    s, f = h2.shape[0], w_ff1.shape[1]
    tm = min(tm, s)

    def body(a_ref, w_ref, u_ref, act_ref):
        u = _dot(a_ref[...], w_ref[...])
        r = jnp.maximum(u, 0.0)
        u_ref[...] = u.astype(BF16)
        act_ref[...] = (r * r).astype(BF16)

    d = h2.shape[1]
    return _call("ff1", body, (f // tn, s // tm),
                 [(h2, (tm, d), lambda j, i: (i, 0)), (w_ff1, (d, tn), lambda j, i: (0, j))],
                 [((s, f), BF16, (tm, tn), lambda j, i: (i, j))] * 2)


def _ff2_loss(act, w_ff2, hres, target, mod, gp2, tm):
    s, d = hres.shape
    f = act.shape[1]

    def body(a_ref, w_ref, h_ref, t_ref, mod_ref, gp_ref, dout_ref, df_ref, loss_ref, dgt_ref, dgp_ref):
        first = pl.program_id(0) == 0
        ff = _dot(a_ref[...], w_ref[...])
        nf, rf = _rms(ff, d)
        gt, gp = mod_ref[:, 5 * d:6 * d], gp_ref[...]
        out = h_ref[...] + gt * (nf * gp)
        err = out - t_ref[...]
        sq = jnp.sum(err * err, axis=1, keepdims=True)
        _accum(loss_ref, jnp.sum(sq, axis=0, keepdims=True), first)
        dout = err * (1.0 / d)
        dout_ref[...] = dout
        _accum(dgt_ref, _colsum(dout * (nf * gp)), first)
        _accum(dgp_ref, _colsum(dout * gt * nf), first)
        df_ref[...] = _rms_bwd(dout * gt * gp, nf, rf, d).astype(BF16)

    row = lambda i: (i, 0)
    fix = lambda i: (0, 0)
    return _call("ff2_loss", body, (s // tm,),
                 [(act, (tm, f), row), (w_ff2, (f, d), fix), (hres, (tm, d), row), (target, (tm, d), row),
                  (mod, (1, 6 * d), fix), (gp2, (1, d), fix)],
                 [((s, d), F32, (tm, d), row), ((s, d), BF16, (tm, d), row), ((1, 1), F32, (1, 1), fix),
                  ((1, d), F32, (1, d), fix), ((1, d), F32, (1, d), fix)])


def _ff2_bwd(df, w_ff2, u, tm, tn):
    s, d = df.shape
    f = w_ff2.shape[0]
    tm = min(tm, s)

    def body(a_ref, w_ref, u_ref, du_ref):
        da = _dot(a_ref[...], w_ref[...], _NT)
        du_ref[...] = (da * (2.0 * jnp.maximum(u_ref[...].astype(F32), 0.0))).astype(BF16)

    return _call("ff2_bwd", body, (f // tn, s // tm),
                 [(df, (tm, d), lambda j, i: (i, 0)), (w_ff2, (tn, d), lambda j, i: (j, 0)),
                  (u, (tm, tn), lambda j, i: (i, j))],
                 [((s, f), BF16, (tm, tn), lambda j, i: (i, j))])[0]


def _ff1_bwd(du, w_ff1, hres, dout, y, mod, g2, gp1, tm, riders=None):
    s, d = hres.shape
    f = du.shape[1]

    def body(a_ref, w_ref, h_ref, do_ref, y_ref, mod_ref, g2_ref, gp_ref,
             dh_ref, dy_ref, dsh_ref, dsc_ref, dg2_ref, dgt_ref, dgp_ref):
        first = pl.program_id(0) == 0
        dh2 = _dot(a_ref[...], w_ref[...], _NT)
        n2, r2 = _rms(h_ref[...], d)
        g2, sc2 = g2_ref[...], mod_ref[:, 4 * d:5 * d]
        _accum(dsh_ref, _colsum(dh2), first)
        _accum(dsc_ref, _colsum(dh2 * n2 * g2), first)
        _accum(dg2_ref, _colsum(dh2 * n2 * (1.0 + sc2)), first)
        dhres = do_ref[...] + _rms_bwd(dh2 * g2 * (1.0 + sc2), n2, r2, d)
        dh_ref[...] = dhres
        ny, ry = _rms(y_ref[...], d)
        gt, gp = mod_ref[:, 2 * d:3 * d], gp_ref[...]
        _accum(dgt_ref, _colsum(dhres * (ny * gp)), first)
        _accum(dgp_ref, _colsum(dhres * gt * ny), first)
        dy_ref[...] = _rms_bwd(dhres * gt * gp, ny, ry, d).astype(BF16)

    row = lambda i: (i, 0)
    fix = lambda i: (0, 0)
    vec = ((1, d), F32, (1, d), fix)
    return _call("ff1_bwd", body, (s // tm,),
                 [(du, (tm, f), row), (w_ff1, (d, f), fix), (hres, (tm, d), row), (dout, (tm, d), row),
                  (y, (tm, d), row), (mod, (1, 6 * d), fix), (g2, (1, d), fix), (gp1, (1, d), fix)],
                 [((s, d), F32, (tm, d), row), ((s, d), BF16, (tm, d), row), vec, vec, vec, vec, vec], riders=riders)


def _out_bwd(dy, w_out, proj, r_bf, s_bf, tm, tn):
    s, d = dy.shape
    ar_off = (2 * RET_QK + 2 * RET_V + 3 * SB_W) // tn
    as_off = ar_off + d // tn

    def body(a_ref, w_ref, ar_ref, as_ref, r_ref, s_ref, dr_ref, ds_ref, dar_ref, das_ref):
        dm = _dot(a_ref[...], w_ref[...], _NT)
        sr, ss = _sigmoid(ar_ref[...]), _sigmoid(as_ref[...])
        dr_ref[...] = (dm * sr).astype(BF16)
        ds_ref[...] = (dm * ss).astype(BF16)
        dar_ref[...] = (dm * r_ref[...].astype(F32) * sr * (1.0 - sr)).astype(BF16)
        das_ref[...] = (dm * s_ref[...].astype(F32) * ss * (1.0 - ss)).astype(BF16)

    tile = (tm, tn)
    here = lambda j, i: (i, j)
    return _call("out_bwd", body, (d // tn, s // tm),
                 [(dy, (tm, d), lambda j, i: (i, 0)), (w_out, (tn, d), lambda j, i: (j, 0)),
                  (proj, tile, lambda j, i: (i, ar_off + j)), (proj, tile, lambda j, i: (i, as_off + j)),
                  (r_bf, tile, here), (s_bf, tile, here)],
                 [((s, d), BF16, tile, here)] * 4)


def _gn_bwd(dretg, ret, proj, gn_g, tm, riders=None):
    s = ret.shape[0]
    gate_off = (2 * RET_QK + RET_V) // RET_V

    def body(d_ref, r_ref, g_ref, w_ref, dg_ref, dret_ref, dw_ref):
        first = pl.program_id(0) == 0
        for h in range(HEADS):
            cols = slice(h * RET_DV, (h + 1) * RET_DV)
            o, g, w, dr = r_ref[:, cols], g_ref[:, cols], w_ref[:, cols], d_ref[:, cols]
            mu = jnp.sum(o, axis=1, keepdims=True) * (1.0 / RET_DV)
            xc = o - mu
            rstd = lax.rsqrt(jnp.sum(xc * xc, axis=1, keepdims=True) * (1.0 / RET_DV) + EPS)
            n = xc * rstd
            sg = _sigmoid(g)
            silu = g * sg
            dg_ref[:, cols] = (dr * n * w * (sg * (1.0 + g * (1.0 - sg)))).astype(BF16)
            _accum(dw_ref.at[:, cols], _colsum(dr * silu * n), first)
            dn = dr * silu * w
            m1 = jnp.sum(dn, axis=1, keepdims=True) * (1.0 / RET_DV)
            m2 = jnp.sum(dn * n, axis=1, keepdims=True) * (1.0 / RET_DV)
            dret_ref[:, cols] = (rstd * (dn - m1 - n * m2)).astype(BF16)

    row = lambda i: (i, 0)
    fix = lambda i: (0, 0)
    return _call("gn_bwd", body, (s // tm,),
                 [(dretg, (tm, RET_V), row), (ret, (tm, RET_V), row),
                  (proj, (tm, RET_V), lambda i: (i, gate_off)), (gn_g, (1, RET_V), fix)],
                 [((s, RET_V), BF16, (tm, RET_V), row), ((s, RET_V), BF16, (tm, RET_V), row),
                  ((1, RET_V), F32, (1, RET_V), fix)], riders=riders)


def _ret_bwd(qk_rot, v_bf, dret, log_gamma, tq, tk, riders=None):
    s = qk_rot.shape[0]
    n_pair = HEADS // 2
    pw = 2 * RET_DV
    _check_tiles(s, tq, tk, RET_GROUP)

    def body(lg_ref, q_ref, k_ref, v_ref, do_ref, dq_ref, dk_ref, dv_ref):
        hp, i = pl.program_id(0), pl.program_id(1)

        @pl.when(i == 0)
        def _():
            dk_ref[...] = jnp.zeros_like(dk_ref)
            dv_ref[...] = jnp.zeros_like(dv_ref)

        qs = _stack_heads(q_ref[...])
        lg_rows = _lg_rows(lg_ref, hp, tq)
        do0, do1 = do_ref[:, 0:RET_DV], do_ref[:, RET_DV:pw]

        def make_step(near_diagonal):
            def step(g, dq):
                js = [g * RET_GROUP + sub for sub in range(RET_GROUP)]
                rows = [_key_rows(j, tk) for j in js]
                ss = [_dot(qs, k_ref[rw, :], _NT) for rw in rows]
                dps = [jnp.concatenate([_dot(do0, v_ref[rw, 0:RET_DV], _NT), _dot(do1, v_ref[rw, RET_DV:pw], _NT)],
                                       axis=0) for rw in rows]
                if near_diagonal:
                    ws = [_ret_weight(lg_rows, i, j, tq, tk) for j in js]
                    ss = [sc * w for sc, w in zip(ss, ws)]
                    dps = [dp * w for dp, w in zip(dps, ws)]
                for sc, dp, rw in zip(ss, dps, rows):
                    p, ds = sc.astype(BF16), dp.astype(BF16)
                    dv_ref[rw, 0:RET_DV] += _dot(p[:tq], do0, _TN)
                    dv_ref[rw, RET_DV:pw] += _dot(p[tq:], do1, _TN)
                    dk_ref[rw, :] += _dot(ds, qs, _TN)
                    dq = dq + _dot(_side_by_side(ds, tq), _stack_heads(k_ref[rw, :]))
                return dq
            return step

        n_full = _n_full(i, tq, tk, RET_GROUP)
        dq = lax.fori_loop(0, n_full, make_step(False), jnp.zeros((tq, LANES), F32))
        dq_ref[...] = lax.fori_loop(n_full, _n_groups(i, tq, tk, RET_GROUP), make_step(True), dq)

    blk = lambda hp, i: (i, hp)
    return _call("ret_bwd", body, (n_pair, s // tq),
                 [(log_gamma, None, pltpu.SMEM),
                  (qk_rot, (tq, LANES), blk),
                  (qk_rot, (s, LANES), lambda hp, i: (0, n_pair + hp)),
                  (v_bf, (s, pw), lambda hp, i: (0, hp)),
                  (dret, (tq, pw), blk)],
                 [((s, RET_QK), F32, (tq, LANES), blk),
                  ((s, RET_QK), F32, (s, LANES), lambda hp, i: (0, hp)),
                  ((s, RET_V), F32, (s, pw), lambda hp, i: (0, hp))], riders=riders)


def _sb_bwd(qkv, carries, do, tq, tk, riders=None):
    s = qkv.shape[0]
    n_pair = HEADS // 2
    _check_tiles(s, tq, tk, SB_GROUP)

    def body(q_ref, k_ref, v_ref, c_ref, do_ref, dq_ref, dk_ref, dv_ref):
        i = pl.program_id(1)

        @pl.when(i == 0)
        def _():
            dk_ref[...] = jnp.zeros_like(dk_ref)
            dv_ref[...] = jnp.zeros_like(dv_ref)

        upper2 = _tri2(tk, True)
        lower2 = _tri2(tk, False)
        lane = lax.broadcasted_iota(jnp.int32, (1, LANES), 1)
        qs = _stack_heads(q_ref[...])
        dos = _stack_heads(do_ref[...].astype(BF16))
        cms = jnp.concatenate([c_ref[:, 0:LANES], c_ref[:, LANES:2 * LANES]], axis=0)

        def make_step(near_diagonal):
            def step(g, carry):
                c_e, dq = carry
                js = [g * SB_GROUP + sub for sub in range(SB_GROUP)]
                rows = [_key_rows(j, tk) for j in js]
                _, log_bs, sticks, valids, _ = _sb_scores(qs, k_ref, i, js, tq, tk, upper2, near_diagonal)
                das = [_dot(dos, v_ref[rw, :], _NT) for rw in rows]
                c_sticks = [jnp.sum(jnp.where(lane == j, cms, 0.0), axis=1, keepdims=True) for j in js]
                avals = [_sb_weights(lb, st, cst, v) for lb, st, cst, v in zip(log_bs, sticks, c_sticks, valids)]
                es = [a * da for a, da in zip(avals, das)]
                prefixes = [lax.dot_general(_split_bf16(e), lower2, _NN, preferred_element_type=F32) for e in es]
                betas = [jnp.exp2(lb) for lb in log_bs]
                for sub in range(SB_GROUP):
                    dv_ref[rows[sub], :] += _dot(avals[sub], dos, _TN)
                for sub in range(SB_GROUP):
                    dz = es[sub] * (1.0 - betas[sub]) - (prefixes[sub] + c_e) * betas[sub]
                    if near_diagonal:
                        dz = jnp.where(valids[sub], dz, 0.0)
                    dz = dz.astype(BF16)
                    dk_ref[rows[sub], :] += _dot(dz, qs, _TN)
                    dq = dq + _dot(_side_by_side(dz, tq), _stack_heads(k_ref[rows[sub], :]))
                    c_e = c_e + jnp.sum(es[sub], axis=1, keepdims=True)
                return c_e, dq
            return step

        n_full = _n_full(i, tq, tk, SB_GROUP)
        carry = (jnp.zeros((2 * tq, 1), F32), jnp.zeros((tq, LANES), F32))
        carry = lax.fori_loop(0, n_full, make_step(False), carry)
        _, dq = lax.fori_loop(n_full, _n_groups(i, tq, tk, SB_GROUP), make_step(True), carry)
        dq_ref[...] = dq

    blk = lambda hp, i: (i, hp)
    return _call("sb_bwd", body, (n_pair, s // tq),
                 [(qkv, (tq, LANES), blk),
                  (qkv, (s, LANES), lambda hp, i: (0, n_pair + hp)),
                  (qkv, (s, LANES), lambda hp, i: (0, 2 * n_pair + hp)),
                  (carries, (tq, 2 * LANES), blk), (do, (tq, LANES), blk)],
                 [((s, SB_W), F32, (tq, LANES), blk),
                  ((s, SB_W), F32, (s, LANES), lambda hp, i: (0, hp)),
                  ((s, SB_W), F32, (s, LANES), lambda hp, i: (0, hp))], riders=riders)


def _assemble_dproj(dq_r, dk_r, dv_r, dg_r, dq_s, dk_s, dv_s, da_r, da_s, cos, sin, idx_col, lg_lanes, tm):
    s, d = da_r.shape
    width = 2 * RET_QK + 2 * RET_V + 3 * SB_W + 2 * d

    def body(dq_ref, dk_ref, dv_ref, dg_ref, dqs_ref, dks_ref, dvs_ref, dar_ref, das_ref, cos_ref, sin_ref,
             idx_ref, lg_ref, o_ref):
        lane = lax.broadcasted_iota(jnp.int32, (1, LANES), 1)
        first = jnp.bitwise_and(lane, RET_DQK - 1) < (RET_DQK // 2)
        cos, sin = cos_ref[...], sin_ref[...]
        idx = idx_ref[...]
        for src, base, sign, scale in ((dq_ref, 0, 1.0, 1.0), (dk_ref, RET_QK, -1.0, RET_DQK ** -0.5)):
            for g in range(RET_QK // LANES):
                v = src[:, g * LANES:(g + 1) * LANES] * (_decay_scale(lg_ref, idx, g, sign) * scale)
                sw = jnp.where(first, pltpu.roll(v, LANES - RET_DQK // 2, 1), pltpu.roll(v, RET_DQK // 2, 1))
                o_ref[:, base + g * LANES:base + (g + 1) * LANES] = (v * cos - sw * sin).astype(BF16)
        off = 2 * RET_QK
        o_ref[:, off:off + RET_V] = dv_ref[...].astype(BF16)
        off += RET_V
        o_ref[:, off:off + RET_V] = dg_ref[...]
        off += RET_V
        o_ref[:, off:off + SB_W] = (dqs_ref[...] * (SB_DH ** -0.5)).astype(BF16)
        off += SB_W
        o_ref[:, off:off + SB_W] = (dks_ref[...] * LN2).astype(BF16)
        off += SB_W
        o_ref[:, off:off + SB_W] = dvs_ref[...].astype(BF16)
        off += SB_W
        o_ref[:, off:off + d] = dar_ref[...]
        off += d
        o_ref[:, off:off + d] = das_ref[...]

    row = lambda i: (i, 0)
    ins = [(a, (tm, a.shape[1]), row) for a in (dq_r, dk_r, dv_r, dg_r, dq_s, dk_s, dv_s, da_r, da_s, cos, sin, idx_col)]
    ins.append((lg_lanes, (1, RET_QK), lambda i: (0, 0)))
    return _call("assemble_dproj", body, (s // tm,), ins, [((s, width), BF16, (tm, width), row)])[0]


def _in_bwd(dproj, w_in, x, dhres, mod, g1, tm, riders=None):
    s, d = x.shape
    width = dproj.shape[1]

    def body(a_ref, w_ref, x_ref, dh_ref, mod_ref, g_ref, dx_ref, dsh_ref, dsc_ref, dg_ref):
        first = pl.program_id(0) == 0
        dh = _dot(a_ref[...], w_ref[...], _NT)
        n1, r1 = _rms(x_ref[...], d)
        g1, sc1 = g_ref[...], mod_ref[:, d:2 * d]
        _accum(dsh_ref, _colsum(dh), first)
        _accum(dsc_ref, _colsum(dh * n1 * g1), first)
        _accum(dg_ref, _colsum(dh * n1 * (1.0 + sc1)), first)
        dx_ref[...] = dh_ref[...] + _rms_bwd(dh * g1 * (1.0 + sc1), n1, r1, d)

    row = lambda i: (i, 0)
    fix = lambda i: (0, 0)
    vec = ((1, d), F32, (1, d), fix)
    return _call("in_bwd", body, (s // tm,),
                 [(dproj, (tm, width), row), (w_in, (d, width), fix), (x, (tm, d), row), (dhres, (tm, d), row),
                  (mod, (1, 6 * d), fix), (g1, (1, d), fix)],
                 [((s, d), F32, (tm, d), row), vec, vec, vec], riders=riders)


def _adamw(w, g, m, v):
    m = ADAM_B1 * m + (1.0 - ADAM_B1) * g
    v = ADAM_B2 * v + (1.0 - ADAM_B2) * (g * g)
    m_hat = m / (1.0 - ADAM_B1 ** ADAM_STEP)
    v_hat = v / (1.0 - ADAM_B2 ** ADAM_STEP)
    delta = -ADAM_LR * (m_hat / (jnp.sqrt(v_hat) + ADAM_EPS) + ADAM_WD * w)
    return delta, m, v


def _adam_reduce(name, parts, w, m, v, tr):
    rws, cls = w.shape
    tr = min(tr, rws)
    n_parts = parts.shape[0]

    def body(p_ref, w_ref, m_ref, v_ref, g_out, d_out, m_out, v_out):
        g = p_ref[0].astype(F32)
        for k in range(1, n_parts):
            g = g + p_ref[k].astype(F32)
        delta, mn, vn = _adamw(w_ref[...], g, m_ref[...], v_ref[...])
        g_out[...] = g
        d_out[...] = delta
        m_out[...] = mn
        v_out[...] = vn

    row = lambda i: (i, 0)
    blk = (tr, cls)
    return _call(name, body, (rws // tr,),
                 [(parts, (n_parts, tr, cls), lambda i: (0, i, 0)), (w, blk, row), (m, blk, row), (v, blk, row)],
                 [((rws, cls), F32, blk, row)] * 4)


def _ada_bwd_adam(cs_t, dmod_cols, w, m, v):
    d, nc = w.shape

    def body(c_ref, dm_ref, w_ref, m_ref, v_ref, g_out, d_out, m_out, v_out):
        g = c_ref[0] * dm_ref[0:1, :]
        for r in range(1, N_DEV):
            g = g + c_ref[r] * dm_ref[r:r + 1, :]
        delta, mn, vn = _adamw(w_ref[...], g, m_ref[...], v_ref[...])
        g_out[...] = g
        d_out[...] = delta
        m_out[...] = mn
        v_out[...] = vn

    fix = lambda i: (0, 0)
    blk = (d, nc)
    return _call("ada_bwd_adam", body, (1,),
                 [(cs_t, (N_DEV, d, 1), lambda i: (0, 0, 0)), (dmod_cols, (N_DEV, nc), fix), (w, blk, fix), (m, blk, fix), (v, blk, fix)],
                 [((d, nc), F32, blk, fix)] * 4)


def _small_adam(parts, w, m, v):
    n = w.shape[1]

    def body(p_ref, w_ref, m_ref, v_ref, g_out, d_out, m_out, v_out):
        g = p_ref[0:1, :]
        for k in range(1, N_DEV):
            g = g + p_ref[k:k + 1, :]
        delta, mn, vn = _adamw(w_ref[...], g, m_ref[...], v_ref[...])
        g_out[...] = g
        d_out[...] = delta
        m_out[...] = mn
        v_out[...] = vn

    fix = lambda i: (0, 0)
    return _call("small_adam", body, (1,),
                 [(parts, (N_DEV, n), fix), (w, (1, n), fix), (m, (1, n), fix), (v, (1, n), fix)],
                 [((1, n), F32, (1, n), fix)] * 4)


def kernel(x, c, positions, ada_w, ada_b, pre_mix_g, post_mix_g, pre_ffn_g, post_ffn_g, w_in, ret_gn_g, w_ret_branch, w_sb_branch, w_out, w_ff1, w_ff2, loss_target, m_ada_w, m_ada_b, m_pre_mix_g, m_post_mix_g, m_pre_ffn_g, m_post_ffn_g, m_w_in, m_ret_gn_g, m_w_ret_branch, m_w_sb_branch, m_w_out, m_w_ff1, m_w_ff2, v_ada_w, v_ada_b, v_pre_mix_g, v_post_mix_g, v_pre_ffn_g, v_post_ffn_g, v_w_in, v_ret_gn_g, v_w_ret_branch, v_w_sb_branch, v_w_out, v_w_ff1, v_w_ff2):
    _, s, d = x.shape
    d_ff = w_ff1.shape[2] * N_DEV
    d_in = w_in.shape[2] * N_DEV
    me = 4 * lax.axis_index("x") + 2 * lax.axis_index("y") + lax.axis_index("c")
    x2, tgt = x[0], loss_target[0]

    core = lax.axis_index("c").astype(jnp.int32).reshape(1)
    bf = lambda w: w[0].astype(BF16)

    c_all, g_in = _exchange("gather_in", [c, bf(w_in)], ["gather", "gather_chip"])
    c_all = c_all.reshape(N_DEV, d)

    n_ada = ada_w.shape[2]
    cs_all = _silu_rows(c_all)
    ada_b_cols = lax.dynamic_slice(ada_b, (0, me * n_ada), (1, n_ada))
    mod_cols = _ada_fwd(cs_all, ada_w[0], ada_b_cols)
    mod_all = _exchange("gather_mod", [mod_cols], ["gather"])[0]
    mod = lax.dynamic_index_in_dim(mod_all, me, axis=1, keepdims=False).reshape(1, 6 * d)

    tm = min(256, s)
    h, g_in = _pre_norm(x2, pre_mix_g, mod, tm, riders=([g_in], ["forward"]))
    wf_in = jnp.moveaxis(g_in, 0, 1).reshape(d, d_in)
    proj = _matmul("in_proj", h, wf_in, "nn", s, 512, F32)
    pos_col = positions.reshape(s, 1).astype(F32)
    freqs = ROPE_BASE ** (-jnp.arange(0, RET_DQK, 2, dtype=F32) / RET_DQK)
    inv_freq = jnp.tile(freqs, LANES // (RET_DQK // 2)).reshape(1, LANES)
    log_gamma_np = np.log1p(-(2.0 ** (-5.0 - np.arange(HEADS))))
    log_gamma = jnp.asarray(log_gamma_np, F32)
    lg_lanes = jnp.asarray(np.repeat(log_gamma_np, RET_DQK).reshape(1, RET_QK), F32)
    idx_col = (jnp.arange(s, dtype=F32) - (s // 2)).reshape(s, 1)
    qk_rot, v_bf, qkv_sb, cos_t, sin_t = _prep(proj, pos_col, idx_col, inv_freq, lg_lanes, tm)
    tq, tk = min(256, s), min(128, s)
    later = [bf(w_ret_branch), bf(w_sb_branch), bf(w_out), bf(w_ff1), bf(w_ff2)]
    sb, sb_carry, *later = _sb_fwd(qkv_sb, tq, tk, riders=(later, ["gather_chip"] * 5))
    ret, retg, g_ret, g_sb, g_out, g_ff1, g_ff2 = _ret_fwd(qk_rot, v_bf, proj, ret_gn_g, log_gamma, tq, tk,
                                                           riders=(later, ["forward"] * 5))
    wf_ret = g_ret.reshape(RET_V, d)
    wf_sb = jnp.moveaxis(g_sb, 0, 1).reshape(SB_W, d)
    wf_out = g_out.reshape(d, d)
    wf_ff1 = jnp.moveaxis(g_ff1, 0, 1).reshape(d, d_ff)
    wf_ff2 = g_ff2.reshape(d_ff, d)
    mixed, r_bf, s_bf = _merge(retg, sb, wf_ret, wf_sb, proj, tm, min(512, d))
    y, hres, h2 = _out_proj(mixed, wf_out, x2, mod, post_mix_g, pre_ffn_g, tm)
    u, act = _ff1(h2, wf_ff1, s, 512)
    dout, df, loss_sum, d_gt2, d_gp2 = _ff2_loss(act, wf_ff2, hres, tgt, mod, post_ffn_g, tm)

    du = _ff2_bwd(df, wf_ff2, u, s, 512)
    gw_ff2 = _matmul("grad_w_ff2", act, df, "tn", 512, d, BF16).reshape(N_DEV, d_ff // N_DEV, d)
    gw_ff1 = _matmul("grad_w_ff1", h2, du, "tn", d, d_ff // N_DEV, BF16, blocked_out=True)
    dhres, dy, d_sh2, d_sc2, d_g2, d_gt1, d_gp1, t_ff1, t_ff2 = _ff1_bwd(
        du, wf_ff1, hres, dout, y, mod, pre_ffn_g, post_mix_g, tm, riders=([gw_ff1, gw_ff2], ["pair"] * 2))
    s_ff1 = _pair_sum("pair_sum_ff1", gw_ff1, t_ff1, core, 256)
    s_ff2 = _pair_sum("pair_sum_ff2", gw_ff2, t_ff2, core, 256)
    d_r, d_s, da_r, da_s = _out_bwd(dy, wf_out, proj, r_bf, s_bf, tm, min(512, d))
    gw_out = _matmul("grad_w_out", mixed, dy, "tn", 512, d, BF16).reshape(N_DEV, d // N_DEV, d)
    dretg = _matmul("ret_branch_bwd", d_r, wf_ret, "nt", s, 512, BF16)
    dsb = _matmul("sb_branch_bwd", d_s, wf_sb, "nt", s, 512, F32)
    gw_ret = _matmul("grad_w_ret", retg, d_r, "tn", 512, d, BF16).reshape(N_DEV, RET_V // N_DEV, d)
    gw_sb = _matmul("grad_w_sb", sb, d_s, "tn", 512, d // N_DEV, BF16, blocked_out=True)
    dq_s, dk_s, dv_s, p_ff1, p_ff2 = _sb_bwd(qkv_sb, sb_carry, dsb, tq, tk,
                                             riders=([s_ff1, s_ff2], ["chip_scatter"] * 2))
    dg_r, dret, d_gn, t_out, t_ret_w, t_sb_w = _gn_bwd(dretg, ret, proj, ret_gn_g, tm,
                                                       riders=([gw_out, gw_ret, gw_sb], ["pair"] * 3))
    s_out = _pair_sum("pair_sum_out", gw_out, t_out, core, 256)
    s_ret = _pair_sum("pair_sum_ret", gw_ret, t_ret_w, core, 256)
    s_sb = _pair_sum("pair_sum_sb", gw_sb, t_sb_w, core, 256)
    dq_r, dk_r, dv_r, p_out, p_ret, p_sb = _ret_bwd(qk_rot, v_bf, dret, log_gamma, tq, tk,
                                                    riders=([s_out, s_ret, s_sb], ["chip_scatter"] * 3))
    dproj = _assemble_dproj(dq_r, dk_r, dv_r, dg_r, dq_s, dk_s, dv_s, da_r, da_s, cos_t, sin_t, idx_col, lg_lanes, tm)
    gw_in_full = _matmul("grad_w_in", h, dproj, "tn", d, 512, BF16)
    gw_in = jnp.moveaxis(gw_in_full.reshape(d, N_DEV, d_in // N_DEV), 1, 0)
    t_in = _exchange("pair_in", [gw_in], ["pair"])[0]
    s_in = _pair_sum("pair_sum_in", gw_in, t_in, core, 256)
    grad_x, d_sh1, d_sc1, d_g1, p_in = _in_bwd(dproj, wf_in, x2, dhres, mod, pre_mix_g, tm,
                                               riders=([s_in], ["chip_scatter"]))
    small = jnp.concatenate([d_sh1, d_sc1, d_gt1, d_sh2, d_sc2, d_gt2, d_g1, d_gp1, d_g2, d_gp2, d_gn], axis=1)
    small_all = _exchange("gather_small", [small], ["gather"])[0].reshape(N_DEV, small.shape[1])
    parts = [p_in, p_ret, p_sb, p_out, p_ff1, p_ff2]

    res = {}
    names = ["w_in", "w_ret_branch", "w_sb_branch", "w_out", "w_ff1", "w_ff2"]
    ws = [w_in, w_ret_branch, w_sb_branch, w_out, w_ff1, w_ff2]
    ms = [m_w_in, m_w_ret_branch, m_w_sb_branch, m_w_out, m_w_ff1, m_w_ff2]
    vs = [v_w_in, v_w_ret_branch, v_w_sb_branch, v_w_out, v_w_ff1, v_w_ff2]
    for nm, p, w, m, v in zip(names, parts, ws, ms, vs):
        res[nm] = [o[None] for o in _adam_reduce("adam_" + nm, p, w[0], m[0], v[0], 256)]
    dmod_cols = lax.dynamic_slice(small_all, (0, me * n_ada), (N_DEV, n_ada))
    res["ada_w"] = [o[None] for o in _ada_bwd_adam(cs_all.reshape(N_DEV, d, 1), dmod_cols, ada_w[0], m_ada_w[0], v_ada_w[0])]
    vec_names = ["ada_b", "pre_mix_g", "post_mix_g", "pre_ffn_g", "post_ffn_g", "ret_gn_g"]
    cat = lambda xs: jnp.concatenate(xs, axis=1)
    packed = _small_adam(small_all,
                         cat([ada_b, pre_mix_g, post_mix_g, pre_ffn_g, post_ffn_g, ret_gn_g]),
                         cat([m_ada_b, m_pre_mix_g, m_post_mix_g, m_pre_ffn_g, m_post_ffn_g, m_ret_gn_g]),
                         cat([v_ada_b, v_pre_mix_g, v_post_mix_g, v_pre_ffn_g, v_post_ffn_g, v_ret_gn_g]))
    off = 0
    for nm, width in zip(vec_names, [6 * d, d, d, d, d, RET_V]):
        res[nm] = [p[:, off:off + width] for p in packed]
        off += width

    loss = (0.5 / d) * lax.psum(loss_sum[0, 0], AXES)
    order = ["ada_w", "ada_b", "pre_mix_g", "post_mix_g", "pre_ffn_g", "post_ffn_g", "w_in", "ret_gn_g",
             "w_ret_branch", "w_sb_branch", "w_out", "w_ff1", "w_ff2"]
    outs = [loss, grad_x[None]]
    for k in range(4):
        outs += [res[nm][k] for nm in order]
    return tuple(outs)
```

```python
import functools

import numpy as np
import jax
import jax.numpy as jnp
from jax import lax
from jax.experimental import pallas as pl
from jax.experimental.pallas import tpu as pltpu

F32 = jnp.float32
BF16 = jnp.bfloat16
N_DEV = 8
AXES = ("x", "y", "c")

EPS = 1e-6
CHUNK = 64
CHUNK_SHIFT = 6
HEADS = 8
RET_DQK = 64
RET_DV = 128
SB_DH = 64
RET_QK = HEADS * RET_DQK
RET_V = HEADS * RET_DV
SB_W = HEADS * SB_DH
ROPE_BASE = 10000.0
LANES = 128

ADAM_LR = 0.001
ADAM_B1 = 0.9
ADAM_B2 = 0.999
ADAM_EPS = 1e-08
ADAM_WD = 0.01
ADAM_STEP = 10

VMEM_LIMIT = 56 * 1024 * 1024

_NN = (((1,), (0,)), ((), ()))
_NT = (((1,), (1,)), ((), ()))
_TN = (((0,), (0,)), ((), ()))


def _dot(a, b, dims=_NN):
    if a.dtype != BF16:
        a = a.astype(BF16)
    if b.dtype != BF16:
        b = b.astype(BF16)
    return lax.dot_general(a, b, dims, preferred_element_type=F32)


def _dot_split(a, b):
    hi = a.astype(BF16)
    lo = (a - hi.astype(F32)).astype(BF16)
    return (lax.dot_general(hi, b, _NN, preferred_element_type=F32)
            + lax.dot_general(lo, b, _NN, preferred_element_type=F32))


def _sigmoid(x):
    return 1.0 / (1.0 + jnp.exp(-x))


def _rms(x, d):
    r = lax.rsqrt(jnp.sum(x * x, axis=1, keepdims=True) * (1.0 / d) + EPS)
    return x * r, r


def _rms_bwd(dn, n, r, d):
    return r * (dn - n * (jnp.sum(dn * n, axis=1, keepdims=True) * (1.0 / d)))


def _colsum(v):
    return jnp.sum(v, axis=0, keepdims=True)


def _accum(ref, val, first):
    @pl.when(first)
    def _():
        ref[...] = val

    @pl.when(jnp.logical_not(first))
    def _():
        ref[...] += val


KIND_SLOTS = {"gather": N_DEV, "scatter": N_DEV, "gather_chip": N_DEV, "forward": N_DEV, "pair": N_DEV // 2,
              "chip_scatter": N_DEV // 2}
SEMS_PER_ARRAY = N_DEV - 1


def _exchange_copies(ins, outs, send_sems, recv_sems, local_sems, kinds):
    x, y, c = (lax.axis_index(a) for a in AXES)
    me, chip, sibling = 4 * x + 2 * y + c, 2 * x + y, (x, y, 1 - c)
    mesh_id = pl.DeviceIdType.MESH
    other_chips = []
    for k in range(1, N_DEV // 2):
        px = 1 - x if k & 2 else x
        py = 1 - y if k & 1 else y
        other_chips.append((px, py))
    copies = []
    for i, kind in enumerate(kinds):
        def remote(src, dst, k, to, i=i):
            return pltpu.make_async_remote_copy(
                src_ref=src, dst_ref=dst, send_sem=send_sems.at[i * SEMS_PER_ARRAY + k],
                recv_sem=recv_sems.at[i * SEMS_PER_ARRAY + k], device_id=to, device_id_type=mesh_id)

        if kind in ("gather", "scatter"):
            pick = (lambda ref, d: ref.at[d]) if kind == "scatter" else (lambda ref, d: ref)
            copies.append(pltpu.make_async_copy(pick(ins[i], me), outs[i].at[me], local_sems.at[i]))
            for k in range(1, N_DEV):
                to = (1 - x if k & 4 else x, 1 - y if k & 2 else y, 1 - c if k & 1 else c)
                copies.append(remote(pick(ins[i], 4 * to[0] + 2 * to[1] + to[2]), outs[i].at[me], k - 1, to))
        elif kind == "gather_chip":
            copies.append(pltpu.make_async_copy(ins[i], outs[i].at[me], local_sems.at[i]))
            copies.append(remote(ins[i], outs[i].at[me], 0, sibling))
            for k, (px, py) in enumerate(other_chips):
                copies.append(remote(ins[i], outs[i].at[me], 1 + k, (px, py, c)))
        elif kind == "forward":
            for k, (px, py) in enumerate(other_chips):
                slot = 4 * px + 2 * py + c
                copies.append(remote(outs[i].at[slot], outs[i].at[slot], k, sibling))
        elif kind == "pair":
            for k in range(N_DEV // 2):
                copies.append(remote(ins[i].at[2 * k + 1 - c], outs[i].at[k], k, sibling))
        elif kind == "chip_scatter":
            copies.append(pltpu.make_async_copy(ins[i].at[chip], outs[i].at[chip], local_sems.at[i]))
            for k, (px, py) in enumerate(other_chips):
                copies.append(remote(ins[i].at[2 * px + py], outs[i].at[chip], k, (px, py, c)))
        else:
            raise ValueError(kind)
    return copies


def _exchange_shapes(arrays, kinds):
    shapes = []
    for a, kind in zip(arrays, kinds):
        tail = a.shape if kind in ("gather", "gather_chip") else a.shape[1:]
        shapes.append(jax.ShapeDtypeStruct((KIND_SLOTS[kind],) + tuple(tail), a.dtype))
    return shapes


def _exchange_sems(n):
    return [pltpu.SemaphoreType.DMA((n * SEMS_PER_ARRAY,)), pltpu.SemaphoreType.DMA((n * SEMS_PER_ARRAY,)),
            pltpu.SemaphoreType.DMA((n,))]


def _call(name, body, grid, ins, outs, scratch=(), riders=None, prefetch=None):
    any_spec = pl.BlockSpec(memory_space=pl.ANY)
    in_specs = [pl.BlockSpec(memory_space=im) if bs is None else pl.BlockSpec(bs, im) for _, bs, im in ins]
    out_specs = [pl.BlockSpec(bs, im) for _, _, bs, im in outs]
    out_shape = [jax.ShapeDtypeStruct(s, d) for s, d, _, _ in outs]
    operands = [a for a, _, _ in ins]
    scratch = list(scratch)
    aliases = {}
    n_pre = 0 if prefetch is None else 1
    kernel = functools.partial(body) if prefetch is None else (lambda _, *refs: body(*refs))
    if riders is not None:
        arrays, kinds = riders
        nr, n_in, n_out, n_scr = len(arrays), len(ins), len(outs), len(scratch)

        def kernel(*refs):
            refs = refs[n_pre:]
            own_in, ride_in = refs[:n_in], refs[n_in:n_in + nr]
            own_out = refs[n_in + nr:n_in + nr + n_out]
            ride_out = refs[n_in + nr + n_out:n_in + 2 * nr + n_out]
            own_scr = refs[n_in + 2 * nr + n_out:n_in + 2 * nr + n_out + n_scr]
            sems = refs[n_in + 2 * nr + n_out + n_scr:]
            ids = [pl.program_id(a) for a in range(len(grid))]
            first = functools.reduce(jnp.logical_and, [i == 0 for i in ids])
            last = functools.reduce(jnp.logical_and, [i == g - 1 for i, g in zip(ids, grid)])

            @pl.when(first)
            def _():
                for cp in _exchange_copies(ride_in, ride_out, *sems, kinds):
                    cp.start()

            body(*own_in, *own_out, *own_scr)

            @pl.when(last)
            def _():
                for cp in _exchange_copies(ride_in, ride_out, *sems, kinds):
                    cp.wait()

        in_specs += [any_spec] * nr
        out_specs += [any_spec] * nr
        out_shape += _exchange_shapes(arrays, kinds)
        operands += list(arrays)
        scratch += _exchange_sems(nr)
        aliases = {n_pre + n_in + r: n_out + r for r, kind in enumerate(kinds) if kind == "forward"}
    params = pltpu.CompilerParams(dimension_semantics=("arbitrary",) * len(grid), vmem_limit_bytes=VMEM_LIMIT)
    if prefetch is None:
        return pl.pallas_call(kernel, name=name, grid=grid, in_specs=in_specs, out_specs=out_specs,
                              out_shape=out_shape, scratch_shapes=scratch, input_output_aliases=aliases,
                              compiler_params=params)(*operands)
    grid_spec = pltpu.PrefetchScalarGridSpec(num_scalar_prefetch=1, grid=grid, in_specs=in_specs,
                                             out_specs=out_specs, scratch_shapes=scratch)
    return pl.pallas_call(kernel, name=name, grid_spec=grid_spec, out_shape=out_shape,
                          input_output_aliases=aliases, compiler_params=params)(prefetch, *operands)


def _exchange(name, arrays, kinds):
    n = len(arrays)

    def body(*refs):
        copies = _exchange_copies(refs[:n], refs[n:2 * n], *refs[2 * n:], kinds)
        for cp in copies:
            cp.start()
        for cp in copies:
            cp.wait()

    any_spec = pl.BlockSpec(memory_space=pl.ANY)
    return pl.pallas_call(
        functools.partial(body),
        name=name,
        in_specs=[any_spec] * n,
        out_specs=[any_spec] * n,
        out_shape=_exchange_shapes(arrays, kinds),
        scratch_shapes=_exchange_sems(n),
        input_output_aliases={i: i for i, kind in enumerate(kinds) if kind == "forward"},
    )(*arrays)


def _pair_sum(name, mine, theirs, my_core, tr):
    _, rws, cls = mine.shape
    tr = min(tr, rws)

    def body(a_ref, b_ref, o_ref):
        o_ref[...] = (a_ref[...].astype(F32) + b_ref[...].astype(F32)).astype(o_ref.dtype)

    return _call(name, body, (N_DEV // 2, rws // tr),
                 [(mine, (None, tr, cls), lambda k, r, core: (2 * k + core[0], r, 0)),
                  (theirs, (None, tr, cls), lambda k, r, core: (k, r, 0))],
                 [((N_DEV // 2, rws, cls), mine.dtype, (None, tr, cls), lambda k, r, core: (k, r, 0))],
                 prefetch=my_core)[0]


def _matmul(name, a, b, kind, tm, tn, out_dtype, blocked_out=False):
    if kind == "tn":
        kdim, m = a.shape
    else:
        m, kdim = a.shape
    n = b.shape[0] if kind == "nt" else b.shape[1]
    tm, tn = min(tm, m), min(tn, n)
    dims = {"nn": _NN, "nt": _NT, "tn": _TN}[kind]

    def body(a_ref, b_ref, o_ref):
        o_ref[...] = _dot(a_ref[...], b_ref[...], dims).astype(o_ref.dtype)

    a_spec = (a, (kdim, tm), lambda j, i: (0, i)) if kind == "tn" else (a, (tm, kdim), lambda j, i: (i, 0))
    b_spec = (b, (tn, kdim), lambda j, i: (j, 0)) if kind == "nt" else (b, (kdim, tn), lambda j, i: (0, j))
    if blocked_out:
        out = ((n // tn, m, tn), out_dtype, (None, tm, tn), lambda j, i: (j, i, 0))
    else:
        out = ((m, n), out_dtype, (tm, tn), lambda j, i: (i, j))
    return _call(name, body, (n // tn, m // tm), [a_spec, b_spec], [out])[0]


def _ada_fwd(cs_all, ada_w, ada_b_cols):
    def body(c_ref, w_ref, b_ref, o_ref):
        o_ref[...] = lax.dot_general(c_ref[...], w_ref[...], _NN, preferred_element_type=F32,
                                     precision=lax.Precision.HIGHEST) + b_ref[...]

    r, d = cs_all.shape
    nc = ada_w.shape[1]
    return _call("ada_fwd", body, (1,),
                 [(cs_all, (r, d), lambda i: (0, 0)), (ada_w, (d, nc), lambda i: (0, 0)),
                  (ada_b_cols, (1, nc), lambda i: (0, 0))],
                 [((r, nc), F32, (r, nc), lambda i: (0, 0))])[0]


def _silu_rows(c_all):
    def body(c_ref, o_ref):
        v = c_ref[...]
        o_ref[...] = v * _sigmoid(v)

    return _call("silu_c", body, (1,), [(c_all, c_all.shape, lambda i: (0, 0))],
                 [(c_all.shape, F32, c_all.shape, lambda i: (0, 0))])[0]


def _pre_norm(x, g, mod, tm, riders=None):
    s, d = x.shape

    def body(x_ref, g_ref, mod_ref, h_ref):
        n, _ = _rms(x_ref[...], d)
        sh, sc = mod_ref[:, 0:d], mod_ref[:, d:2 * d]
        h_ref[...] = (n * g_ref[...] * (1.0 + sc) + sh).astype(BF16)

    return _call("pre_norm", body, (s // tm,),
                 [(x, (tm, d), lambda i: (i, 0)), (g, (1, d), lambda i: (0, 0)),
                  (mod, (1, 6 * d), lambda i: (0, 0))],
                 [((s, d), BF16, (tm, d), lambda i: (i, 0))], riders=riders)


LOG2E = 1.4426950408889634
LN2 = 0.6931471805599453


def _decay_scale(lg_ref, idx, g, sign):
    return jnp.exp((sign * idx) * lg_ref[:, g * LANES:(g + 1) * LANES])


def _prep(proj, pos_col, idx_col, inv_freq, lg_lanes, tm):
    s = proj.shape[0]
    sb_off = (2 * RET_QK + 2 * RET_V) // (3 * SB_W)
    n_q = RET_QK // LANES

    def body(qk_ref, v_ref, sb_ref, pos_ref, idx_ref, f_ref, lg_ref, qk_out, v_out, sb_out, cos_out, sin_out):
        ang = pos_ref[...] * f_ref[...]
        lane = lax.broadcasted_iota(jnp.int32, (1, LANES), 1)
        first = jnp.bitwise_and(lane, RET_DQK - 1) < (RET_DQK // 2)
        cos = jnp.cos(ang)
        sin = jnp.where(first, -1.0, 1.0) * jnp.sin(ang)
        cos_out[...] = cos
        sin_out[...] = sin
        idx = idx_ref[...]
        for g in range(2 * n_q):
            v = qk_ref[:, g * LANES:(g + 1) * LANES].astype(F32)
            sw = jnp.where(first, pltpu.roll(v, LANES - RET_DQK // 2, 1), pltpu.roll(v, RET_DQK // 2, 1))
            r = v * cos + sw * sin
            if g < n_q:
                r = r * _decay_scale(lg_ref, idx, g, 1.0)
            else:
                r = r * (_decay_scale(lg_ref, idx, g - n_q, -1.0) * (RET_DQK ** -0.5))
            qk_out[:, g * LANES:(g + 1) * LANES] = r.astype(BF16)
        v_out[...] = v_ref[...].astype(BF16)
        sb_out[:, 0:SB_W] = (sb_ref[:, 0:SB_W].astype(F32) * (SB_DH ** -0.5 * LOG2E)).astype(BF16)
        sb_out[:, SB_W:3 * SB_W] = sb_ref[:, SB_W:3 * SB_W].astype(BF16)

    return _call("prep", body, (s // tm,),
                 [(proj, (tm, 2 * RET_QK), lambda i: (i, 0)),
                  (proj, (tm, RET_V), lambda i: (i, 2 * RET_QK // RET_V)),
                  (proj, (tm, 3 * SB_W), lambda i: (i, sb_off)),
                  (pos_col, (tm, 1), lambda i: (i, 0)),
                  (idx_col, (tm, 1), lambda i: (i, 0)),
                  (inv_freq, (1, LANES), lambda i: (0, 0)),
                  (lg_lanes, (1, RET_QK), lambda i: (0, 0))],
                 [((s, 2 * RET_QK), BF16, (tm, 2 * RET_QK), lambda i: (i, 0)),
                  ((s, RET_V), BF16, (tm, RET_V), lambda i: (i, 0)),
                  ((s, 3 * SB_W), BF16, (tm, 3 * SB_W), lambda i: (i, 0)),
                  ((s, LANES), F32, (tm, LANES), lambda i: (i, 0)),
                  ((s, LANES), F32, (tm, LANES), lambda i: (i, 0))])


def _head_mask(hh):
    lane = lax.broadcasted_iota(jnp.int32, (1, LANES), 1)
    return (lane >= RET_DQK) if hh else (lane < RET_DQK)


def _masked(v, m):
    return jnp.where(m, v, jnp.zeros_like(v))


SB_GROUP = 4
RET_GROUP = 4


def _stack_heads(v):
    return jnp.concatenate([_masked(v, _head_mask(0)), _masked(v, _head_mask(1))], axis=0)


def _side_by_side(v, t):
    return jnp.concatenate([v[:t], v[t:]], axis=1)


def _split_bf16(v):
    hi = v.astype(BF16)
    lo = (v - hi.astype(F32)).astype(BF16)
    return jnp.concatenate([hi, lo], axis=1)


def _tile_pos(i, j, tq, tk):
    row = jnp.bitwise_and(lax.broadcasted_iota(jnp.int32, (2 * tq, tk), 0), tq - 1) + i * tq
    col = lax.broadcasted_iota(jnp.int32, (2 * tq, tk), 1) + j * tk
    return row, col


def _n_groups(i, tq, tk, grp):
    return ((i + 1) * (tq // tk) + grp - 1) // grp


def _n_full(i, tq, tk, grp):
    return (i * (tq // tk)) // grp


def _key_rows(j, tk):
    return pl.ds(pl.multiple_of(j * tk, tk), tk)


def _ret_weight(lg_rows, i, j, tq, tk):
    row, col = _tile_pos(i, j, tq, tk)
    same = jnp.right_shift(col, CHUNK_SHIFT) == jnp.right_shift(row, CHUNK_SHIFT)
    later = jnp.where(same, jnp.exp((2.0 * lg_rows) * (col - row).astype(F32)), 0.0)
    return jnp.where(col <= row, 1.0, later)


def _lg_rows(lg_ref, hp, tq):
    first = lax.broadcasted_iota(jnp.int32, (2 * tq, 1), 0) < tq
    return jnp.where(first, lg_ref[2 * hp], lg_ref[2 * hp + 1])


def _check_tiles(s, tq, tk, grp):
    assert tq % tk == 0 and tq & (tq - 1) == 0 and tk & (tk - 1) == 0
    assert s % tq == 0 and (s // tk) % grp == 0 and s // tk <= LANES


def _ret_fwd(qk_rot, v_bf, proj, gn_g, log_gamma, tq, tk, riders=None):
    s = qk_rot.shape[0]
    gate_off = (2 * RET_QK + RET_V) // (2 * RET_DV)
    n_pair = HEADS // 2
    _check_tiles(s, tq, tk, RET_GROUP)

    def body(lg_ref, q_ref, k_ref, v_ref, g_ref, w_ref, ret_ref, rg_ref):
        hp, i = pl.program_id(0), pl.program_id(1)
        qs = _stack_heads(q_ref[...])
        lg_rows = _lg_rows(lg_ref, hp, tq)

        def make_step(near_diagonal):
            def step(g, carry):
                o0, o1 = carry
                js = [g * RET_GROUP + sub for sub in range(RET_GROUP)]
                rows = [_key_rows(j, tk) for j in js]
                ss = [_dot(qs, k_ref[rw, :], _NT) for rw in rows]
                if near_diagonal:
                    ss = [sc * _ret_weight(lg_rows, i, j, tq, tk) for sc, j in zip(ss, js)]
                for sc, rw in zip(ss, rows):
                    p = sc.astype(BF16)
                    o0 = o0 + _dot(p[:tq], v_ref[rw, 0:RET_DV])
                    o1 = o1 + _dot(p[tq:], v_ref[rw, RET_DV:2 * RET_DV])
                return o0, o1
            return step

        zero = jnp.zeros((tq, RET_DV), F32)
        n_full = _n_full(i, tq, tk, RET_GROUP)
        outs = lax.fori_loop(0, n_full, make_step(False), (zero, zero))
        outs = lax.fori_loop(n_full, _n_groups(i, tq, tk, RET_GROUP), make_step(True), outs)
        for hh, o in enumerate(outs):
            cols = slice(hh * RET_DV, (hh + 1) * RET_DV)
            ret_ref[:, cols] = o
            mu = jnp.sum(o, axis=1, keepdims=True) * (1.0 / RET_DV)
            xc = o - mu
            var = jnp.sum(xc * xc, axis=1, keepdims=True) * (1.0 / RET_DV)
            nrm = xc * lax.rsqrt(var + EPS) * w_ref[:, cols]
            g = g_ref[:, cols].astype(F32)
            rg_ref[:, cols] = (g * _sigmoid(g) * nrm).astype(BF16)

    pw = 2 * RET_DV
    blk = lambda hp, i: (i, hp)
    return _call("ret_fwd", body, (n_pair, s // tq),
                 [(log_gamma, None, pltpu.SMEM),
                  (qk_rot, (tq, LANES), blk),
                  (qk_rot, (s, LANES), lambda hp, i: (0, n_pair + hp)),
                  (v_bf, (s, pw), lambda hp, i: (0, hp)),
                  (proj, (tq, pw), lambda hp, i: (i, gate_off + hp)),
                  (gn_g, (1, pw), lambda hp, i: (0, hp))],
                 [((s, RET_V), F32, (tq, pw), blk), ((s, RET_V), BF16, (tq, pw), blk)], riders=riders)


def _tri2(tk, strict_upper):
    r = jnp.bitwise_and(lax.broadcasted_iota(jnp.int32, (2 * tk, tk), 0), tk - 1)
    cc = lax.broadcasted_iota(jnp.int32, (2 * tk, tk), 1)
    return ((r > cc) if strict_upper else (r < cc)).astype(BF16)


def _sb_scores(qs, k_ref, i, js, tq, tk, tri2, near_diagonal):
    zs = [_dot(qs, k_ref[_key_rows(j, tk), :], _NT) for j in js]
    log1ps = [jnp.log2(1.0 + jnp.exp2(-jnp.abs(z))) for z in zs]
    log_1ms = [-jnp.maximum(z, 0.0) - t for z, t in zip(zs, log1ps)]
    log_bs = [jnp.minimum(z, 0.0) - t for z, t in zip(zs, log1ps)]
    valids = [None] * len(js)
    if near_diagonal:
        valids = []
        for j in js:
            row, col = _tile_pos(i, j, tq, tk)
            valids.append(col < row)
        log_1ms = [jnp.where(v, l, 0.0) for v, l in zip(valids, log_1ms)]
    sticks = [lax.dot_general(_split_bf16(l), tri2, _NN, preferred_element_type=F32) for l in log_1ms]
    sums = [jnp.sum(l, axis=1, keepdims=True) for l in log_1ms]
    return log_1ms, log_bs, sticks, valids, sums


def _sb_weights(log_b, stick, c, valid):
    a = jnp.exp2(log_b + stick + c)
    return a if valid is None else jnp.where(valid, a, 0.0)


def _sb_fwd(qkv, tq, tk, riders=None):
    s = qkv.shape[0]
    n_pair = HEADS // 2
    _check_tiles(s, tq, tk, SB_GROUP)

    def body(q_ref, k_ref, v_ref, o_ref, carry_ref):
        i = pl.program_id(1)
        upper2 = _tri2(tk, True)
        lane = lax.broadcasted_iota(jnp.int32, (1, LANES), 1)
        qs = _stack_heads(q_ref[...])
        carry_ref[...] = jnp.zeros_like(carry_ref)
        n_full, n_groups = _n_full(i, tq, tk, SB_GROUP), _n_groups(i, tq, tk, SB_GROUP)

        def make_step(near_diagonal, last):
            def step(n, carry):
                c, o = carry
                g = last - 1 - n
                js = [g * SB_GROUP + sub for sub in range(SB_GROUP)]
                _, log_bs, sticks, valids, sums = _sb_scores(qs, k_ref, i, js, tq, tk, upper2, near_diagonal)
                cs = [None] * SB_GROUP
                for sub in reversed(range(SB_GROUP)):
                    cs[sub] = c
                    c = c + sums[sub]
                for sub, j in enumerate(js):
                    a = _sb_weights(log_bs[sub], sticks[sub], cs[sub], valids[sub])
                    o = o + _dot(_side_by_side(a.astype(BF16), tq), _stack_heads(v_ref[_key_rows(j, tk), :]))
                for hh in range(2):
                    cols = slice(hh * LANES, (hh + 1) * LANES)
                    cm = carry_ref[:, cols]
                    for sub, j in enumerate(js):
                        cm = jnp.where(lane == j, cs[sub][hh * tq:(hh + 1) * tq], cm)
                    carry_ref[:, cols] = cm
                return c, o
            return step

        carry = (jnp.zeros((2 * tq, 1), F32), jnp.zeros((tq, LANES), F32))
        carry = lax.fori_loop(0, n_groups - n_full, make_step(True, n_groups), carry)
        _, acc = lax.fori_loop(0, n_full, make_step(False, n_full), carry)
        o_ref[...] = acc

    return _call("sb_fwd", body, (n_pair, s // tq),
                 [(qkv, (tq, LANES), lambda hp, i: (i, hp)),
                  (qkv, (s, LANES), lambda hp, i: (0, n_pair + hp)),
                  (qkv, (s, LANES), lambda hp, i: (0, 2 * n_pair + hp))],
                 [((s, SB_W), F32, (tq, LANES), lambda hp, i: (i, hp)),
                  ((s, HEADS * LANES), F32, (tq, 2 * LANES), lambda hp, i: (i, hp))], riders=riders)


def _merge(retg, sb, w_ret, w_sb, proj, tm, tn, riders=None):
    s, d = retg.shape[0], w_ret.shape[1]
    ar_off = (2 * RET_QK + 2 * RET_V + 3 * SB_W) // tn
    as_off = ar_off + d // tn

    def body(rg_ref, sb_ref, wr_ref, ws_ref, ar_ref, as_ref, mix_ref, r_ref, s_ref):
        rr = _dot(rg_ref[...], wr_ref[...])
        ss = _dot(sb_ref[...], ws_ref[...])
        mix_ref[...] = (_sigmoid(ar_ref[...].astype(F32)) * rr + _sigmoid(as_ref[...].astype(F32)) * ss).astype(BF16)
        r_ref[...] = rr.astype(BF16)
        s_ref[...] = ss.astype(BF16)

    tile = (tm, tn)
    return _call("merge", body, (d // tn, s // tm),
                 [(retg, (tm, RET_V), lambda j, i: (i, 0)), (sb, (tm, SB_W), lambda j, i: (i, 0)),
                  (w_ret, (RET_V, tn), lambda j, i: (0, j)), (w_sb, (SB_W, tn), lambda j, i: (0, j)),
                  (proj, tile, lambda j, i: (i, ar_off + j)), (proj, tile, lambda j, i: (i, as_off + j))],
                 [((s, d), BF16, tile, lambda j, i: (i, j))] * 3, riders=riders)


def _out_proj(mixed, w_out, x, mod, gp1, g2, tm):
    s, d = x.shape

    def body(a_ref, w_ref, x_ref, mod_ref, gp_ref, g2_ref, y_ref, hres_ref, h2_ref):
        y = _dot(a_ref[...], w_ref[...])
        y_ref[...] = y
        ny, _ = _rms(y, d)
        hres = x_ref[...] + mod_ref[:, 2 * d:3 * d] * (ny * gp_ref[...])
        hres_ref[...] = hres
        n2, _ = _rms(hres, d)
        h2_ref[...] = (n2 * g2_ref[...] * (1.0 + mod_ref[:, 4 * d:5 * d]) + mod_ref[:, 3 * d:4 * d]).astype(BF16)

    row = lambda i: (i, 0)
    fix = lambda i: (0, 0)
    return _call("out_proj", body, (s // tm,),
                 [(mixed, (tm, d), row), (w_out, (d, d), fix), (x, (tm, d), row),
                  (mod, (1, 6 * d), fix), (gp1, (1, d), fix), (g2, (1, d), fix)],
                 [((s, d), F32, (tm, d), row), ((s, d), F32, (tm, d), row), ((s, d), BF16, (tm, d), row)])


def _ff1(h2, w_ff1, tm, tn):
    s, f = h2.shape[0], w_ff1.shape[1]
    tm = min(tm, s)

    def body(a_ref, w_ref, u_ref, act_ref):
        u = _dot(a_ref[...], w_ref[...])
        r = jnp.maximum(u, 0.0)
        u_ref[...] = u.astype(BF16)
        act_ref[...] = (r * r).astype(BF16)

    d = h2.shape[1]
    return _call("ff1", body, (f // tn, s // tm),
                 [(h2, (tm, d), lambda j, i: (i, 0)), (w_ff1, (d, tn), lambda j, i: (0, j))],
                 [((s, f), BF16, (tm, tn), lambda j, i: (i, j))] * 2)


def _ff2_loss(act, w_ff2, hres, target, mod, gp2, tm):
    s, d = hres.shape
    f = act.shape[1]

    def body(a_ref, w_ref, h_ref, t_ref, mod_ref, gp_ref, dout_ref, df_ref, loss_ref, dgt_ref, dgp_ref):
        first = pl.program_id(0) == 0
        ff = _dot(a_ref[...], w_ref[...])
        nf, rf = _rms(ff, d)
        gt, gp = mod_ref[:, 5 * d:6 * d], gp_ref[...]
        out = h_ref[...] + gt * (nf * gp)
        err = out - t_ref[...]
        sq = jnp.sum(err * err, axis=1, keepdims=True)
        _accum(loss_ref, jnp.sum(sq, axis=0, keepdims=True), first)
        dout = err * (1.0 / d)
        dout_ref[...] = dout
        _accum(dgt_ref, _colsum(dout * (nf * gp)), first)
        _accum(dgp_ref, _colsum(dout * gt * nf), first)
        df_ref[...] = _rms_bwd(dout * gt * gp, nf, rf, d).astype(BF16)

    row = lambda i: (i, 0)
    fix = lambda i: (0, 0)
    return _call("ff2_loss", body, (s // tm,),
                 [(act, (tm, f), row), (w_ff2, (f, d), fix), (hres, (tm, d), row), (target, (tm, d), row),
                  (mod, (1, 6 * d), fix), (gp2, (1, d), fix)],
                 [((s, d), F32, (tm, d), row), ((s, d), BF16, (tm, d), row), ((1, 1), F32, (1, 1), fix),
                  ((1, d), F32, (1, d), fix), ((1, d), F32, (1, d), fix)])


def _ff2_bwd(df, w_ff2, u, tm, tn):
    s, d = df.shape
    f = w_ff2.shape[0]
    tm = min(tm, s)

    def body(a_ref, w_ref, u_ref, du_ref):
        da = _dot(a_ref[...], w_ref[...], _NT)
        du_ref[...] = (da * (2.0 * jnp.maximum(u_ref[...].astype(F32), 0.0))).astype(BF16)

    return _call("ff2_bwd", body, (f // tn, s // tm),
                 [(df, (tm, d), lambda j, i: (i, 0)), (w_ff2, (tn, d), lambda j, i: (j, 0)),
                  (u, (tm, tn), lambda j, i: (i, j))],
                 [((s, f), BF16, (tm, tn), lambda j, i: (i, j))])[0]


def _ff1_bwd(du, w_ff1, hres, dout, y, mod, g2, gp1, tm, riders=None):
    s, d = hres.shape
    f = du.shape[1]

    def body(a_ref, w_ref, h_ref, do_ref, y_ref, mod_ref, g2_ref, gp_ref,
             dh_ref, dy_ref, dsh_ref, dsc_ref, dg2_ref, dgt_ref, dgp_ref):
        first = pl.program_id(0) == 0
        dh2 = _dot(a_ref[...], w_ref[...], _NT)
        n2, r2 = _rms(h_ref[...], d)
        g2, sc2 = g2_ref[...], mod_ref[:, 4 * d:5 * d]
        _accum(dsh_ref, _colsum(dh2), first)
        _accum(dsc_ref, _colsum(dh2 * n2 * g2), first)
        _accum(dg2_ref, _colsum(dh2 * n2 * (1.0 + sc2)), first)
        dhres = do_ref[...] + _rms_bwd(dh2 * g2 * (1.0 + sc2), n2, r2, d)
        dh_ref[...] = dhres
        ny, ry = _rms(y_ref[...], d)
        gt, gp = mod_ref[:, 2 * d:3 * d], gp_ref[...]
        _accum(dgt_ref, _colsum(dhres * (ny * gp)), first)
        _accum(dgp_ref, _colsum(dhres * gt * ny), first)
        dy_ref[...] = _rms_bwd(dhres * gt * gp, ny, ry, d).astype(BF16)

    row = lambda i: (i, 0)
    fix = lambda i: (0, 0)
    vec = ((1, d), F32, (1, d), fix)
    return _call("ff1_bwd", body, (s // tm,),
                 [(du, (tm, f), row), (w_ff1, (d, f), fix), (hres, (tm, d), row), (dout, (tm, d), row),
                  (y, (tm, d), row), (mod, (1, 6 * d), fix), (g2, (1, d), fix), (gp1, (1, d), fix)],
                 [((s, d), F32, (tm, d), row), ((s, d), BF16, (tm, d), row), vec, vec, vec, vec, vec], riders=riders)


def _out_bwd(dy, w_out, proj, r_bf, s_bf, tm, tn):
    s, d = dy.shape
    ar_off = (2 * RET_QK + 2 * RET_V + 3 * SB_W) // tn
    as_off = ar_off + d // tn

    def body(a_ref, w_ref, ar_ref, as_ref, r_ref, s_ref, dr_ref, ds_ref, dar_ref, das_ref):
        dm = _dot(a_ref[...], w_ref[...], _NT)
        sr, ss = _sigmoid(ar_ref[...].astype(F32)), _sigmoid(as_ref[...].astype(F32))
        dr_ref[...] = (dm * sr).astype(BF16)
        ds_ref[...] = (dm * ss).astype(BF16)
        dar_ref[...] = (dm * r_ref[...].astype(F32) * sr * (1.0 - sr)).astype(BF16)
        das_ref[...] = (dm * s_ref[...].astype(F32) * ss * (1.0 - ss)).astype(BF16)

    tile = (tm, tn)
    here = lambda j, i: (i, j)
    return _call("out_bwd", body, (d // tn, s // tm),
                 [(dy, (tm, d), lambda j, i: (i, 0)), (w_out, (tn, d), lambda j, i: (j, 0)),
                  (proj, tile, lambda j, i: (i, ar_off + j)), (proj, tile, lambda j, i: (i, as_off + j)),
                  (r_bf, tile, here), (s_bf, tile, here)],
                 [((s, d), BF16, tile, here)] * 4)


def _gn_bwd(dretg, ret, proj, gn_g, tm, riders=None):
    s = ret.shape[0]
    gate_off = (2 * RET_QK + RET_V) // RET_V

    def body(d_ref, r_ref, g_ref, w_ref, dg_ref, dret_ref, dw_ref):
        first = pl.program_id(0) == 0
        for h in range(HEADS):
            cols = slice(h * RET_DV, (h + 1) * RET_DV)
            o, g, w, dr = r_ref[:, cols], g_ref[:, cols].astype(F32), w_ref[:, cols], d_ref[:, cols].astype(F32)
            mu = jnp.sum(o, axis=1, keepdims=True) * (1.0 / RET_DV)
            xc = o - mu
            rstd = lax.rsqrt(jnp.sum(xc * xc, axis=1, keepdims=True) * (1.0 / RET_DV) + EPS)
            n = xc * rstd
            sg = _sigmoid(g)
            silu = g * sg
            dg_ref[:, cols] = (dr * n * w * (sg * (1.0 + g * (1.0 - sg)))).astype(BF16)
            _accum(dw_ref.at[:, cols], _colsum(dr * silu * n), first)
            dn = dr * silu * w
            m1 = jnp.sum(dn, axis=1, keepdims=True) * (1.0 / RET_DV)
            m2 = jnp.sum(dn * n, axis=1, keepdims=True) * (1.0 / RET_DV)
            dret_ref[:, cols] = (rstd * (dn - m1 - n * m2)).astype(BF16)

    row = lambda i: (i, 0)
    fix = lambda i: (0, 0)
    return _call("gn_bwd", body, (s // tm,),
                 [(dretg, (tm, RET_V), row), (ret, (tm, RET_V), row),
                  (proj, (tm, RET_V), lambda i: (i, gate_off)), (gn_g, (1, RET_V), fix)],
                 [((s, RET_V), BF16, (tm, RET_V), row), ((s, RET_V), BF16, (tm, RET_V), row),
                  ((1, RET_V), F32, (1, RET_V), fix)], riders=riders)


def _ret_bwd(qk_rot, v_bf, dret, log_gamma, tq, tk, riders=None):
    s = qk_rot.shape[0]
    n_pair = HEADS // 2
    pw = 2 * RET_DV
    _check_tiles(s, tq, tk, RET_GROUP)

    def body(lg_ref, q_ref, k_ref, v_ref, do_ref, dq_ref, dk_ref, dv_ref):
        hp, i = pl.program_id(0), pl.program_id(1)

        @pl.when(i == 0)
        def _():
            dk_ref[...] = jnp.zeros_like(dk_ref)
            dv_ref[...] = jnp.zeros_like(dv_ref)

        qs = _stack_heads(q_ref[...])
        lg_rows = _lg_rows(lg_ref, hp, tq)
        do0, do1 = do_ref[:, 0:RET_DV], do_ref[:, RET_DV:pw]

        def make_step(near_diagonal):
            def step(g, dq):
                js = [g * RET_GROUP + sub for sub in range(RET_GROUP)]
                rows = [_key_rows(j, tk) for j in js]
                ss = [_dot(qs, k_ref[rw, :], _NT) for rw in rows]
                dps = [jnp.concatenate([_dot(do0, v_ref[rw, 0:RET_DV], _NT), _dot(do1, v_ref[rw, RET_DV:pw], _NT)],
                                       axis=0) for rw in rows]
                if near_diagonal:
                    ws = [_ret_weight(lg_rows, i, j, tq, tk) for j in js]
                    ss = [sc * w for sc, w in zip(ss, ws)]
                    dps = [dp * w for dp, w in zip(dps, ws)]
                for sc, dp, rw in zip(ss, dps, rows):
                    p, ds = sc.astype(BF16), dp.astype(BF16)
                    dv_ref[rw, 0:RET_DV] += _dot(p[:tq], do0, _TN)
                    dv_ref[rw, RET_DV:pw] += _dot(p[tq:], do1, _TN)
                    dk_ref[rw, :] += _dot(ds, qs, _TN)
                    dq = dq + _dot(_side_by_side(ds, tq), _stack_heads(k_ref[rw, :]))
                return dq
            return step

        n_full = _n_full(i, tq, tk, RET_GROUP)
        dq = lax.fori_loop(0, n_full, make_step(False), jnp.zeros((tq, LANES), F32))
        dq_ref[...] = lax.fori_loop(n_full, _n_groups(i, tq, tk, RET_GROUP), make_step(True), dq)

    blk = lambda hp, i: (i, hp)
    return _call("ret_bwd", body, (n_pair, s // tq),
                 [(log_gamma, None, pltpu.SMEM),
                  (qk_rot, (tq, LANES), blk),
                  (qk_rot, (s, LANES), lambda hp, i: (0, n_pair + hp)),
                  (v_bf, (s, pw), lambda hp, i: (0, hp)),
                  (dret, (tq, pw), blk)],
                 [((s, RET_QK), F32, (tq, LANES), blk),
                  ((s, RET_QK), F32, (s, LANES), lambda hp, i: (0, hp)),
                  ((s, RET_V), F32, (s, pw), lambda hp, i: (0, hp))], riders=riders)


def _sb_bwd(qkv, carries, do, tq, tk, riders=None):
    s = qkv.shape[0]
    n_pair = HEADS // 2
    _check_tiles(s, tq, tk, SB_GROUP)

    def body(q_ref, k_ref, v_ref, c_ref, do_ref, dq_ref, dk_ref, dv_ref):
        i = pl.program_id(1)

        @pl.when(i == 0)
        def _():
            dk_ref[...] = jnp.zeros_like(dk_ref)
            dv_ref[...] = jnp.zeros_like(dv_ref)

        upper2 = _tri2(tk, True)
        lower2 = _tri2(tk, False)
        lane = lax.broadcasted_iota(jnp.int32, (1, LANES), 1)
        qs = _stack_heads(q_ref[...])
        dos = _stack_heads(do_ref[...].astype(BF16))
        cms = jnp.concatenate([c_ref[:, 0:LANES], c_ref[:, LANES:2 * LANES]], axis=0)

        def make_step(near_diagonal):
            def step(g, carry):
                c_e, dq = carry
                js = [g * SB_GROUP + sub for sub in range(SB_GROUP)]
                rows = [_key_rows(j, tk) for j in js]
                _, log_bs, sticks, valids, _ = _sb_scores(qs, k_ref, i, js, tq, tk, upper2, near_diagonal)
                das = [_dot(dos, v_ref[rw, :], _NT) for rw in rows]
                c_sticks = [jnp.sum(jnp.where(lane == j, cms, 0.0), axis=1, keepdims=True) for j in js]
                avals = [_sb_weights(lb, st, cst, v) for lb, st, cst, v in zip(log_bs, sticks, c_sticks, valids)]
                es = [a * da for a, da in zip(avals, das)]
                prefixes = [lax.dot_general(_split_bf16(e), lower2, _NN, preferred_element_type=F32) for e in es]
                betas = [jnp.exp2(lb) for lb in log_bs]
                for sub in range(SB_GROUP):
                    dv_ref[rows[sub], :] += _dot(avals[sub], dos, _TN)
                for sub in range(SB_GROUP):
                    dz = es[sub] * (1.0 - betas[sub]) - (prefixes[sub] + c_e) * betas[sub]
                    if near_diagonal:
                        dz = jnp.where(valids[sub], dz, 0.0)
                    dz = dz.astype(BF16)
                    dk_ref[rows[sub], :] += _dot(dz, qs, _TN)
                    dq = dq + _dot(_side_by_side(dz, tq), _stack_heads(k_ref[rows[sub], :]))
                    c_e = c_e + jnp.sum(es[sub], axis=1, keepdims=True)
                return c_e, dq
            return step

        n_full = _n_full(i, tq, tk, SB_GROUP)
        carry = (jnp.zeros((2 * tq, 1), F32), jnp.zeros((tq, LANES), F32))
        carry = lax.fori_loop(0, n_full, make_step(False), carry)
        _, dq = lax.fori_loop(n_full, _n_groups(i, tq, tk, SB_GROUP), make_step(True), carry)
        dq_ref[...] = dq

    blk = lambda hp, i: (i, hp)
    return _call("sb_bwd", body, (n_pair, s // tq),
                 [(qkv, (tq, LANES), blk),
                  (qkv, (s, LANES), lambda hp, i: (0, n_pair + hp)),
                  (qkv, (s, LANES), lambda hp, i: (0, 2 * n_pair + hp)),
                  (carries, (tq, 2 * LANES), blk), (do, (tq, LANES), blk)],
                 [((s, SB_W), F32, (tq, LANES), blk),
                  ((s, SB_W), F32, (s, LANES), lambda hp, i: (0, hp)),
                  ((s, SB_W), F32, (s, LANES), lambda hp, i: (0, hp))], riders=riders)


def _assemble_dproj(dq_r, dk_r, dv_r, dg_r, dq_s, dk_s, dv_s, da_r, da_s, cos, sin, idx_col, lg_lanes, tm):
    s, d = da_r.shape
    width = 2 * RET_QK + 2 * RET_V + 3 * SB_W + 2 * d

    def body(dq_ref, dk_ref, dv_ref, dg_ref, dqs_ref, dks_ref, dvs_ref, dar_ref, das_ref, cos_ref, sin_ref,
             idx_ref, lg_ref, o_ref):
        lane = lax.broadcasted_iota(jnp.int32, (1, LANES), 1)
        first = jnp.bitwise_and(lane, RET_DQK - 1) < (RET_DQK // 2)
        cos, sin = cos_ref[...], sin_ref[...]
        idx = idx_ref[...]
        for src, base, sign, scale in ((dq_ref, 0, 1.0, 1.0), (dk_ref, RET_QK, -1.0, RET_DQK ** -0.5)):
            for g in range(RET_QK // LANES):
                v = src[:, g * LANES:(g + 1) * LANES] * (_decay_scale(lg_ref, idx, g, sign) * scale)
                sw = jnp.where(first, pltpu.roll(v, LANES - RET_DQK // 2, 1), pltpu.roll(v, RET_DQK // 2, 1))
                o_ref[:, base + g * LANES:base + (g + 1) * LANES] = (v * cos - sw * sin).astype(BF16)
        off = 2 * RET_QK
        o_ref[:, off:off + RET_V] = dv_ref[...].astype(BF16)
        off += RET_V
        o_ref[:, off:off + RET_V] = dg_ref[...]
        off += RET_V
        o_ref[:, off:off + SB_W] = (dqs_ref[...] * (SB_DH ** -0.5)).astype(BF16)
        off += SB_W
        o_ref[:, off:off + SB_W] = (dks_ref[...] * LN2).astype(BF16)
        off += SB_W
        o_ref[:, off:off + SB_W] = dvs_ref[...].astype(BF16)
        off += SB_W
        o_ref[:, off:off + d] = dar_ref[...]
        off += d
        o_ref[:, off:off + d] = das_ref[...]

    row = lambda i: (i, 0)
    ins = [(a, (tm, a.shape[1]), row) for a in (dq_r, dk_r, dv_r, dg_r, dq_s, dk_s, dv_s, da_r, da_s, cos, sin, idx_col)]
    ins.append((lg_lanes, (1, RET_QK), lambda i: (0, 0)))
    return _call("assemble_dproj", body, (s // tm,), ins, [((s, width), BF16, (tm, width), row)])[0]


def _in_bwd(dproj, w_in, x, dhres, mod, g1, tm, riders=None):
    s, d = x.shape
    width = dproj.shape[1]

    def body(a_ref, w_ref, x_ref, dh_ref, mod_ref, g_ref, dx_ref, dsh_ref, dsc_ref, dg_ref):
        first = pl.program_id(0) == 0
        dh = _dot(a_ref[...], w_ref[...], _NT)
        n1, r1 = _rms(x_ref[...], d)
        g1, sc1 = g_ref[...], mod_ref[:, d:2 * d]
        _accum(dsh_ref, _colsum(dh), first)
        _accum(dsc_ref, _colsum(dh * n1 * g1), first)
        _accum(dg_ref, _colsum(dh * n1 * (1.0 + sc1)), first)
        dx_ref[...] = dh_ref[...] + _rms_bwd(dh * g1 * (1.0 + sc1), n1, r1, d)

    row = lambda i: (i, 0)
    fix = lambda i: (0, 0)
    vec = ((1, d), F32, (1, d), fix)
    return _call("in_bwd", body, (s // tm,),
                 [(dproj, (tm, width), row), (w_in, (d, width), fix), (x, (tm, d), row), (dhres, (tm, d), row),
                  (mod, (1, 6 * d), fix), (g1, (1, d), fix)],
                 [((s, d), F32, (tm, d), row), vec, vec, vec], riders=riders)


def _adamw(w, g, m, v):
    m = ADAM_B1 * m + (1.0 - ADAM_B1) * g
    v = ADAM_B2 * v + (1.0 - ADAM_B2) * (g * g)
    m_hat = m / (1.0 - ADAM_B1 ** ADAM_STEP)
    v_hat = v / (1.0 - ADAM_B2 ** ADAM_STEP)
    delta = -ADAM_LR * (m_hat / (jnp.sqrt(v_hat) + ADAM_EPS) + ADAM_WD * w)
    return delta, m, v


def _adam_reduce(name, parts, w, m, v, tr):
    rws, cls = w.shape
    tr = min(tr, rws)
    n_parts = parts.shape[0]

    def body(p_ref, w_ref, m_ref, v_ref, g_out, d_out, m_out, v_out):
        g = p_ref[0].astype(F32)
        for k in range(1, n_parts):
            g = g + p_ref[k].astype(F32)
        delta, mn, vn = _adamw(w_ref[...], g, m_ref[...], v_ref[...])
        g_out[...] = g
        d_out[...] = delta
        m_out[...] = mn
        v_out[...] = vn

    row = lambda i: (i, 0)
    blk = (tr, cls)
    return _call(name, body, (rws // tr,),
                 [(parts, (n_parts, tr, cls), lambda i: (0, i, 0)), (w, blk, row), (m, blk, row), (v, blk, row)],
                 [((rws, cls), F32, blk, row)] * 4)


def _ada_bwd_adam(cs_t, dmod_cols, w, m, v):
    d, nc = w.shape

    def body(c_ref, dm_ref, w_ref, m_ref, v_ref, g_out, d_out, m_out, v_out):
        g = c_ref[0] * dm_ref[0:1, :]
        for r in range(1, N_DEV):
            g = g + c_ref[r] * dm_ref[r:r + 1, :]
        delta, mn, vn = _adamw(w_ref[...], g, m_ref[...], v_ref[...])
        g_out[...] = g
        d_out[...] = delta
        m_out[...] = mn
        v_out[...] = vn

    fix = lambda i: (0, 0)
    blk = (d, nc)
    return _call("ada_bwd_adam", body, (1,),
                 [(cs_t, (N_DEV, d, 1), lambda i: (0, 0, 0)), (dmod_cols, (N_DEV, nc), fix), (w, blk, fix), (m, blk, fix), (v, blk, fix)],
                 [((d, nc), F32, blk, fix)] * 4)


def _small_adam(parts, w, m, v):
    n = w.shape[1]

    def body(p_ref, w_ref, m_ref, v_ref, g_out, d_out, m_out, v_out):
        g = p_ref[0:1, :]
        for k in range(1, N_DEV):
            g = g + p_ref[k:k + 1, :]
        delta, mn, vn = _adamw(w_ref[...], g, m_ref[...], v_ref[...])
        g_out[...] = g
        d_out[...] = delta
        m_out[...] = mn
        v_out[...] = vn

    fix = lambda i: (0, 0)
    return _call("small_adam", body, (1,),
                 [(parts, (N_DEV, n), fix), (w, (1, n), fix), (m, (1, n), fix), (v, (1, n), fix)],
                 [((1, n), F32, (1, n), fix)] * 4)


def kernel(x, c, positions, ada_w, ada_b, pre_mix_g, post_mix_g, pre_ffn_g, post_ffn_g, w_in, ret_gn_g, w_ret_branch, w_sb_branch, w_out, w_ff1, w_ff2, loss_target, m_ada_w, m_ada_b, m_pre_mix_g, m_post_mix_g, m_pre_ffn_g, m_post_ffn_g, m_w_in, m_ret_gn_g, m_w_ret_branch, m_w_sb_branch, m_w_out, m_w_ff1, m_w_ff2, v_ada_w, v_ada_b, v_pre_mix_g, v_post_mix_g, v_pre_ffn_g, v_post_ffn_g, v_w_in, v_ret_gn_g, v_w_ret_branch, v_w_sb_branch, v_w_out, v_w_ff1, v_w_ff2):
    _, s, d = x.shape
    d_ff = w_ff1.shape[2] * N_DEV
    d_in = w_in.shape[2] * N_DEV
    me = 4 * lax.axis_index("x") + 2 * lax.axis_index("y") + lax.axis_index("c")
    x2, tgt = x[0], loss_target[0]

    core = lax.axis_index("c").astype(jnp.int32).reshape(1)
    bf = lambda w: w[0].astype(BF16)

    c_all, g_in = _exchange("gather_in", [c, bf(w_in)], ["gather", "gather_chip"])
    c_all = c_all.reshape(N_DEV, d)

    n_ada = ada_w.shape[2]
    cs_all = _silu_rows(c_all)
    ada_b_cols = lax.dynamic_slice(ada_b, (0, me * n_ada), (1, n_ada))
    mod_cols = _ada_fwd(cs_all, ada_w[0], ada_b_cols)
    mod_all = _exchange("gather_mod", [mod_cols], ["gather"])[0]
    mod = lax.dynamic_index_in_dim(mod_all, me, axis=1, keepdims=False).reshape(1, 6 * d)

    tm = min(256, s)
    h, g_in = _pre_norm(x2, pre_mix_g, mod, tm, riders=([g_in], ["forward"]))
    wf_in = jnp.moveaxis(g_in, 0, 1).reshape(d, d_in)
    proj = _matmul("in_proj", h, wf_in, "nn", s, 512, BF16)
    pos_col = positions.reshape(s, 1).astype(F32)
    freqs = ROPE_BASE ** (-jnp.arange(0, RET_DQK, 2, dtype=F32) / RET_DQK)
    inv_freq = jnp.tile(freqs, LANES // (RET_DQK // 2)).reshape(1, LANES)
    log_gamma_np = np.log1p(-(2.0 ** (-5.0 - np.arange(HEADS))))
    log_gamma = jnp.asarray(log_gamma_np, F32)
    lg_lanes = jnp.asarray(np.repeat(log_gamma_np, RET_DQK).reshape(1, RET_QK), F32)
    idx_col = (jnp.arange(s, dtype=F32) - (s // 2)).reshape(s, 1)
    qk_rot, v_bf, qkv_sb, cos_t, sin_t = _prep(proj, pos_col, idx_col, inv_freq, lg_lanes, tm)
    tq, tk = min(256, s), min(128, s)
    later = [bf(w_ret_branch), bf(w_sb_branch), bf(w_out), bf(w_ff1)]
    sb, sb_carry, *later = _sb_fwd(qkv_sb, tq, tk, riders=(later, ["gather_chip"] * 4))
    ret, retg, g_ret, g_sb, g_out, g_ff1, g_ff2 = _ret_fwd(
        qk_rot, v_bf, proj, ret_gn_g, log_gamma, tq, tk,
        riders=(later + [bf(w_ff2)], ["forward"] * 4 + ["gather_chip"]))
    wf_ret = g_ret.reshape(RET_V, d)
    wf_sb = jnp.moveaxis(g_sb, 0, 1).reshape(SB_W, d)
    wf_out = g_out.reshape(d, d)
    wf_ff1 = jnp.moveaxis(g_ff1, 0, 1).reshape(d, d_ff)
    mixed, r_bf, s_bf, g_ff2 = _merge(retg, sb, wf_ret, wf_sb, proj, tm, min(512, d), riders=([g_ff2], ["forward"]))
    wf_ff2 = g_ff2.reshape(d_ff, d)
    y, hres, h2 = _out_proj(mixed, wf_out, x2, mod, post_mix_g, pre_ffn_g, tm)
    u, act = _ff1(h2, wf_ff1, s, 512)
    dout, df, loss_sum, d_gt2, d_gp2 = _ff2_loss(act, wf_ff2, hres, tgt, mod, post_ffn_g, tm)

    du = _ff2_bwd(df, wf_ff2, u, s, 512)
    gw_ff2 = _matmul("grad_w_ff2", act, df, "tn", 512, d, BF16).reshape(N_DEV, d_ff // N_DEV, d)
    gw_ff1 = _matmul("grad_w_ff1", h2, du, "tn", d, d_ff // N_DEV, BF16, blocked_out=True)
    dhres, dy, d_sh2, d_sc2, d_g2, d_gt1, d_gp1, t_ff1, t_ff2 = _ff1_bwd(
        du, wf_ff1, hres, dout, y, mod, pre_ffn_g, post_mix_g, tm, riders=([gw_ff1, gw_ff2], ["pair"] * 2))
    s_ff1 = _pair_sum("pair_sum_ff1", gw_ff1, t_ff1, core, 256)
    s_ff2 = _pair_sum("pair_sum_ff2", gw_ff2, t_ff2, core, 256)
    d_r, d_s, da_r, da_s = _out_bwd(dy, wf_out, proj, r_bf, s_bf, tm, min(512, d))
    gw_out = _matmul("grad_w_out", mixed, dy, "tn", 512, d, BF16).reshape(N_DEV, d // N_DEV, d)
    dretg = _matmul("ret_branch_bwd", d_r, wf_ret, "nt", s, 512, BF16)
    dsb = _matmul("sb_branch_bwd", d_s, wf_sb, "nt", s, 512, F32)
    gw_ret = _matmul("grad_w_ret", retg, d_r, "tn", 512, d, BF16).reshape(N_DEV, RET_V // N_DEV, d)
    gw_sb = _matmul("grad_w_sb", sb, d_s, "tn", 512, d // N_DEV, BF16, blocked_out=True)
    dq_s, dk_s, dv_s, p_ff1, p_ff2 = _sb_bwd(qkv_sb, sb_carry, dsb, tq, tk,
                                             riders=([s_ff1, s_ff2], ["chip_scatter"] * 2))
    dg_r, dret, d_gn = _gn_bwd(dretg, ret, proj, ret_gn_g, tm)
    dq_r, dk_r, dv_r, p_out, p_ret, p_sb = _ret_bwd(qk_rot, v_bf, dret, log_gamma, tq, tk,
                                                    riders=([gw_out, gw_ret, gw_sb], ["scatter"] * 3))
    dproj = _assemble_dproj(dq_r, dk_r, dv_r, dg_r, dq_s, dk_s, dv_s, da_r, da_s, cos_t, sin_t, idx_col, lg_lanes, tm)
    gw_in_full = _matmul("grad_w_in", h, dproj, "tn", d, 512, BF16)
    gw_in = jnp.moveaxis(gw_in_full.reshape(d, N_DEV, d_in // N_DEV), 1, 0)
    t_in = _exchange("pair_in", [gw_in], ["pair"])[0]
    s_in = _pair_sum("pair_sum_in", gw_in, t_in, core, 256)
    grad_x, d_sh1, d_sc1, d_g1, p_in = _in_bwd(dproj, wf_in, x2, dhres, mod, pre_mix_g, tm,
                                               riders=([s_in], ["chip_scatter"]))
    loss_lanes = jnp.pad(loss_sum, ((0, 0), (0, LANES - 1)))
    small = jnp.concatenate([d_sh1, d_sc1, d_gt1, d_sh2, d_sc2, d_gt2, d_g1, d_gp1, d_g2, d_gp2, d_gn, loss_lanes], axis=1)
    small_all = _exchange("gather_small", [small], ["gather"])[0].reshape(N_DEV, small.shape[1])
    parts = [p_in, p_ret, p_sb, p_out, p_ff1, p_ff2]

    res = {}
    names = ["w_in", "w_ret_branch", "w_sb_branch", "w_out", "w_ff1", "w_ff2"]
    ws = [w_in, w_ret_branch, w_sb_branch, w_out, w_ff1, w_ff2]
    ms = [m_w_in, m_w_ret_branch, m_w_sb_branch, m_w_out, m_w_ff1, m_w_ff2]
    vs = [v_w_in, v_w_ret_branch, v_w_sb_branch, v_w_out, v_w_ff1, v_w_ff2]
    for nm, p, w, m, v in zip(names, parts, ws, ms, vs):
        res[nm] = [o[None] for o in _adam_reduce("adam_" + nm, p, w[0], m[0], v[0], 256)]
    dmod_cols = lax.dynamic_slice(small_all, (0, me * n_ada), (N_DEV, n_ada))
    res["ada_w"] = [o[None] for o in _ada_bwd_adam(cs_all.reshape(N_DEV, d, 1), dmod_cols, ada_w[0], m_ada_w[0], v_ada_w[0])]
    vec_names = ["ada_b", "pre_mix_g", "post_mix_g", "pre_ffn_g", "post_ffn_g", "ret_gn_g"]
    cat = lambda xs: jnp.concatenate(xs + [jnp.zeros((1, LANES), F32)], axis=1)
    packed = _small_adam(small_all,
                         cat([ada_b, pre_mix_g, post_mix_g, pre_ffn_g, post_ffn_g, ret_gn_g]),
                         cat([m_ada_b, m_pre_mix_g, m_post_mix_g, m_pre_ffn_g, m_post_ffn_g, m_ret_gn_g]),
                         cat([v_ada_b, v_pre_mix_g, v_post_mix_g, v_pre_ffn_g, v_post_ffn_g, v_ret_gn_g]))
    off = 0
    for nm, width in zip(vec_names, [6 * d, d, d, d, d, RET_V]):
        res[nm] = [p[:, off:off + width] for p in packed]
        off += width

    loss = (0.5 / d) * packed[0][0, off]
    order = ["ada_w", "ada_b", "pre_mix_g", "post_mix_g", "pre_ffn_g", "post_ffn_g", "w_in", "ret_gn_g",
             "w_ret_branch", "w_sb_branch", "w_out", "w_ff1", "w_ff2"]
    outs = [loss, grad_x[None]]
    for k in range(4):
        outs += [res[nm][k] for nm in order]
    return tuple(outs)
```

```python
import functools

import numpy as np
import jax
import jax.numpy as jnp
from jax import lax
from jax.experimental import pallas as pl
from jax.experimental.pallas import tpu as pltpu

F32 = jnp.float32
BF16 = jnp.bfloat16
N_DEV = 8
AXES = ("x", "y", "c")

EPS = 1e-6
CHUNK = 64
CHUNK_SHIFT = 6
HEADS = 8
RET_DQK = 64
RET_DV = 128
SB_DH = 64
RET_QK = HEADS * RET_DQK
RET_V = HEADS * RET_DV
SB_W = HEADS * SB_DH
ROPE_BASE = 10000.0
LANES = 128

ADAM_LR = 0.001
ADAM_B1 = 0.9
ADAM_B2 = 0.999
ADAM_EPS = 1e-08
ADAM_WD = 0.01
ADAM_STEP = 10

VMEM_LIMIT = 56 * 1024 * 1024

_NN = (((1,), (0,)), ((), ()))
_NT = (((1,), (1,)), ((), ()))
_TN = (((0,), (0,)), ((), ()))


def _dot(a, b, dims=_NN):
    if a.dtype != BF16:
        a = a.astype(BF16)
    if b.dtype != BF16:
        b = b.astype(BF16)
    return lax.dot_general(a, b, dims, preferred_element_type=F32)


def _dot_split(a, b):
    hi = a.astype(BF16)
    lo = (a - hi.astype(F32)).astype(BF16)
    return (lax.dot_general(hi, b, _NN, preferred_element_type=F32)
            + lax.dot_general(lo, b, _NN, preferred_element_type=F32))


def _sigmoid(x):
    return 1.0 / (1.0 + jnp.exp(-x))


def _rms(x, d):
    r = lax.rsqrt(jnp.sum(x * x, axis=1, keepdims=True) * (1.0 / d) + EPS)
    return x * r, r


def _rms_bwd(dn, n, r, d):
    return r * (dn - n * (jnp.sum(dn * n, axis=1, keepdims=True) * (1.0 / d)))


def _colsum(v):
    return jnp.sum(v, axis=0, keepdims=True)


def _accum(ref, val, first):
    @pl.when(first)
    def _():
        ref[...] = val

    @pl.when(jnp.logical_not(first))
    def _():
        ref[...] += val


KIND_SLOTS = {"gather": N_DEV, "scatter": N_DEV, "gather_chip": N_DEV, "forward": N_DEV, "pair": N_DEV // 2,
              "chip_scatter": N_DEV // 2}
SEMS_PER_ARRAY = N_DEV - 1


def _exchange_copies(ins, outs, send_sems, recv_sems, local_sems, kinds):
    x, y, c = (lax.axis_index(a) for a in AXES)
    me, chip, sibling = 4 * x + 2 * y + c, 2 * x + y, (x, y, 1 - c)
    mesh_id = pl.DeviceIdType.MESH
    other_chips = []
    for k in range(1, N_DEV // 2):
        px = 1 - x if k & 2 else x
        py = 1 - y if k & 1 else y
        other_chips.append((px, py))
    copies = []
    for i, kind in enumerate(kinds):
        def remote(src, dst, k, to, i=i):
            return pltpu.make_async_remote_copy(
                src_ref=src, dst_ref=dst, send_sem=send_sems.at[i * SEMS_PER_ARRAY + k],
                recv_sem=recv_sems.at[i * SEMS_PER_ARRAY + k], device_id=to, device_id_type=mesh_id)

        if kind in ("gather", "scatter"):
            pick = (lambda ref, d: ref.at[d]) if kind == "scatter" else (lambda ref, d: ref)
            copies.append(pltpu.make_async_copy(pick(ins[i], me), outs[i].at[me], local_sems.at[i]))
            for k in range(1, N_DEV):
                to = (1 - x if k & 4 else x, 1 - y if k & 2 else y, 1 - c if k & 1 else c)
                copies.append(remote(pick(ins[i], 4 * to[0] + 2 * to[1] + to[2]), outs[i].at[me], k - 1, to))
        elif kind == "gather_chip":
            copies.append(pltpu.make_async_copy(ins[i], outs[i].at[me], local_sems.at[i]))
            copies.append(remote(ins[i], outs[i].at[me], 0, sibling))
            for k, (px, py) in enumerate(other_chips):
                copies.append(remote(ins[i], outs[i].at[me], 1 + k, (px, py, c)))
        elif kind == "forward":
            for k, (px, py) in enumerate(other_chips):
                slot = 4 * px + 2 * py + c
                copies.append(remote(outs[i].at[slot], outs[i].at[slot], k, sibling))
        elif kind == "pair":
            for k in range(N_DEV // 2):
                copies.append(remote(ins[i].at[2 * k + 1 - c], outs[i].at[k], k, sibling))
        elif kind == "chip_scatter":
            copies.append(pltpu.make_async_copy(ins[i].at[chip], outs[i].at[chip], local_sems.at[i]))
            for k, (px, py) in enumerate(other_chips):
                copies.append(remote(ins[i].at[2 * px + py], outs[i].at[chip], k, (px, py, c)))
        else:
            raise ValueError(kind)
    return copies


def _exchange_shapes(arrays, kinds):
    shapes = []
    for a, kind in zip(arrays, kinds):
        tail = a.shape if kind in ("gather", "gather_chip") else a.shape[1:]
        shapes.append(jax.ShapeDtypeStruct((KIND_SLOTS[kind],) + tuple(tail), a.dtype))
    return shapes


def _exchange_sems(n):
    return [pltpu.SemaphoreType.DMA((n * SEMS_PER_ARRAY,)), pltpu.SemaphoreType.DMA((n * SEMS_PER_ARRAY,)),
            pltpu.SemaphoreType.DMA((n,))]


def _call(name, body, grid, ins, outs, scratch=(), riders=None, prefetch=None):
    any_spec = pl.BlockSpec(memory_space=pl.ANY)
    in_specs = [pl.BlockSpec(memory_space=im) if bs is None else pl.BlockSpec(bs, im) for _, bs, im in ins]
    out_specs = [pl.BlockSpec(bs, im) for _, _, bs, im in outs]
    out_shape = [jax.ShapeDtypeStruct(s, d) for s, d, _, _ in outs]
    operands = [a for a, _, _ in ins]
    scratch = list(scratch)
    aliases = {}
    n_pre = 0 if prefetch is None else 1
    kernel = functools.partial(body) if prefetch is None else (lambda _, *refs: body(*refs))
    if riders is not None:
        arrays, kinds = riders
        nr, n_in, n_out, n_scr = len(arrays), len(ins), len(outs), len(scratch)

        def kernel(*refs):
            refs = refs[n_pre:]
            own_in, ride_in = refs[:n_in], refs[n_in:n_in + nr]
            own_out = refs[n_in + nr:n_in + nr + n_out]
            ride_out = refs[n_in + nr + n_out:n_in + 2 * nr + n_out]
            own_scr = refs[n_in + 2 * nr + n_out:n_in + 2 * nr + n_out + n_scr]
            sems = refs[n_in + 2 * nr + n_out + n_scr:]
            ids = [pl.program_id(a) for a in range(len(grid))]
            first = functools.reduce(jnp.logical_and, [i == 0 for i in ids])
            last = functools.reduce(jnp.logical_and, [i == g - 1 for i, g in zip(ids, grid)])

            @pl.when(first)
            def _():
                for cp in _exchange_copies(ride_in, ride_out, *sems, kinds):
                    cp.start()

            body(*own_in, *own_out, *own_scr)

            @pl.when(last)
            def _():
                for cp in _exchange_copies(ride_in, ride_out, *sems, kinds):
                    cp.wait()

        in_specs += [any_spec] * nr
        out_specs += [any_spec] * nr
        out_shape += _exchange_shapes(arrays, kinds)
        operands += list(arrays)
        scratch += _exchange_sems(nr)
        aliases = {n_pre + n_in + r: n_out + r for r, kind in enumerate(kinds) if kind == "forward"}
    params = pltpu.CompilerParams(dimension_semantics=("arbitrary",) * len(grid), vmem_limit_bytes=VMEM_LIMIT)
    if prefetch is None:
        return pl.pallas_call(kernel, name=name, grid=grid, in_specs=in_specs, out_specs=out_specs,
                              out_shape=out_shape, scratch_shapes=scratch, input_output_aliases=aliases,
                              compiler_params=params)(*operands)
    grid_spec = pltpu.PrefetchScalarGridSpec(num_scalar_prefetch=1, grid=grid, in_specs=in_specs,
                                             out_specs=out_specs, scratch_shapes=scratch)
    return pl.pallas_call(kernel, name=name, grid_spec=grid_spec, out_shape=out_shape,
                          input_output_aliases=aliases, compiler_params=params)(prefetch, *operands)


def _exchange(name, arrays, kinds):
    n = len(arrays)

    def body(*refs):
        copies = _exchange_copies(refs[:n], refs[n:2 * n], *refs[2 * n:], kinds)
        for cp in copies:
            cp.start()
        for cp in copies:
            cp.wait()

    any_spec = pl.BlockSpec(memory_space=pl.ANY)
    return pl.pallas_call(
        functools.partial(body),
        name=name,
        in_specs=[any_spec] * n,
        out_specs=[any_spec] * n,
        out_shape=_exchange_shapes(arrays, kinds),
        scratch_shapes=_exchange_sems(n),
        input_output_aliases={i: i for i, kind in enumerate(kinds) if kind == "forward"},
    )(*arrays)


def _pair_sum(name, mine, theirs, my_core, tr):
    _, rws, cls = mine.shape
    tr = min(tr, rws)

    def body(a_ref, b_ref, o_ref):
        o_ref[...] = (a_ref[...].astype(F32) + b_ref[...].astype(F32)).astype(o_ref.dtype)

    return _call(name, body, (N_DEV // 2, rws // tr),
                 [(mine, (None, tr, cls), lambda k, r, core: (2 * k + core[0], r, 0)),
                  (theirs, (None, tr, cls), lambda k, r, core: (k, r, 0))],
                 [((N_DEV // 2, rws, cls), mine.dtype, (None, tr, cls), lambda k, r, core: (k, r, 0))],
                 prefetch=my_core)[0]


def _matmul(name, a, b, kind, tm, tn, out_dtype, blocked_out=False):
    if kind == "tn":
        kdim, m = a.shape
    else:
        m, kdim = a.shape
    n = b.shape[0] if kind == "nt" else b.shape[1]
    tm, tn = min(tm, m), min(tn, n)
    dims = {"nn": _NN, "nt": _NT, "tn": _TN}[kind]

    def body(a_ref, b_ref, o_ref):
        o_ref[...] = _dot(a_ref[...], b_ref[...], dims).astype(o_ref.dtype)

    a_spec = (a, (kdim, tm), lambda j, i: (0, i)) if kind == "tn" else (a, (tm, kdim), lambda j, i: (i, 0))
    b_spec = (b, (tn, kdim), lambda j, i: (j, 0)) if kind == "nt" else (b, (kdim, tn), lambda j, i: (0, j))
    if blocked_out:
        out = ((n // tn, m, tn), out_dtype, (None, tm, tn), lambda j, i: (j, i, 0))
    else:
        out = ((m, n), out_dtype, (tm, tn), lambda j, i: (i, j))
    return _call(name, body, (n // tn, m // tm), [a_spec, b_spec], [out])[0]


def _ada_fwd(cs_all, ada_w, ada_b_cols):
    def body(c_ref, w_ref, b_ref, o_ref):
        o_ref[...] = lax.dot_general(c_ref[...], w_ref[...], _NN, preferred_element_type=F32,
                                     precision=lax.Precision.HIGHEST) + b_ref[...]

    r, d = cs_all.shape
    nc = ada_w.shape[1]
    return _call("ada_fwd", body, (1,),
                 [(cs_all, (r, d), lambda i: (0, 0)), (ada_w, (d, nc), lambda i: (0, 0)),
                  (ada_b_cols, (1, nc), lambda i: (0, 0))],
                 [((r, nc), F32, (r, nc), lambda i: (0, 0))])[0]


def _silu_rows(c_all):
    def body(c_ref, o_ref):
        v = c_ref[...]
        o_ref[...] = v * _sigmoid(v)

    return _call("silu_c", body, (1,), [(c_all, c_all.shape, lambda i: (0, 0))],
                 [(c_all.shape, F32, c_all.shape, lambda i: (0, 0))])[0]


def _pre_norm(x, g, mod, tm, riders=None):
    s, d = x.shape

    def body(x_ref, g_ref, mod_ref, h_ref):
        n, _ = _rms(x_ref[...], d)
        sh, sc = mod_ref[:, 0:d], mod_ref[:, d:2 * d]
        h_ref[...] = (n * g_ref[...] * (1.0 + sc) + sh).astype(BF16)

    return _call("pre_norm", body, (s // tm,),
                 [(x, (tm, d), lambda i: (i, 0)), (g, (1, d), lambda i: (0, 0)),
                  (mod, (1, 6 * d), lambda i: (0, 0))],
                 [((s, d), BF16, (tm, d), lambda i: (i, 0))], riders=riders)


LOG2E = 1.4426950408889634
LN2 = 0.6931471805599453


def _decay_scale(lg_ref, idx, g, sign):
    return jnp.exp((sign * idx) * lg_ref[:, g * LANES:(g + 1) * LANES])


def _prep(proj, pos_col, idx_col, inv_freq, lg_lanes, tm):
    s = proj.shape[0]
    sb_off = (2 * RET_QK + 2 * RET_V) // (3 * SB_W)
    n_q = RET_QK // LANES

    def body(qk_ref, v_ref, sb_ref, pos_ref, idx_ref, f_ref, lg_ref, qk_out, v_out, sb_out, cos_out, sin_out):
        ang = pos_ref[...] * f_ref[...]
        lane = lax.broadcasted_iota(jnp.int32, (1, LANES), 1)
        first = jnp.bitwise_and(lane, RET_DQK - 1) < (RET_DQK // 2)
        cos = jnp.cos(ang)
        sin = jnp.where(first, -1.0, 1.0) * jnp.sin(ang)
        cos_out[...] = cos
        sin_out[...] = sin
        idx = idx_ref[...]
        for g in range(2 * n_q):
            v = qk_ref[:, g * LANES:(g + 1) * LANES].astype(F32)
            sw = jnp.where(first, pltpu.roll(v, LANES - RET_DQK // 2, 1), pltpu.roll(v, RET_DQK // 2, 1))
            r = v * cos + sw * sin
            if g < n_q:
                r = r * _decay_scale(lg_ref, idx, g, 1.0)
            else:
                r = r * (_decay_scale(lg_ref, idx, g - n_q, -1.0) * (RET_DQK ** -0.5))
            qk_out[:, g * LANES:(g + 1) * LANES] = r.astype(BF16)
        v_out[...] = v_ref[...].astype(BF16)
        sb_out[:, 0:SB_W] = (sb_ref[:, 0:SB_W].astype(F32) * (SB_DH ** -0.5 * LOG2E)).astype(BF16)
        sb_out[:, SB_W:3 * SB_W] = sb_ref[:, SB_W:3 * SB_W].astype(BF16)

    return _call("prep", body, (s // tm,),
                 [(proj, (tm, 2 * RET_QK), lambda i: (i, 0)),
                  (proj, (tm, RET_V), lambda i: (i, 2 * RET_QK // RET_V)),
                  (proj, (tm, 3 * SB_W), lambda i: (i, sb_off)),
                  (pos_col, (tm, 1), lambda i: (i, 0)),
                  (idx_col, (tm, 1), lambda i: (i, 0)),
                  (inv_freq, (1, LANES), lambda i: (0, 0)),
                  (lg_lanes, (1, RET_QK), lambda i: (0, 0))],
                 [((s, 2 * RET_QK), BF16, (tm, 2 * RET_QK), lambda i: (i, 0)),
                  ((s, RET_V), BF16, (tm, RET_V), lambda i: (i, 0)),
                  ((s, 3 * SB_W), BF16, (tm, 3 * SB_W), lambda i: (i, 0)),
                  ((s, LANES), F32, (tm, LANES), lambda i: (i, 0)),
                  ((s, LANES), F32, (tm, LANES), lambda i: (i, 0))])


def _head_mask(hh):
    lane = lax.broadcasted_iota(jnp.int32, (1, LANES), 1)
    return (lane >= RET_DQK) if hh else (lane < RET_DQK)


def _masked(v, m):
    return jnp.where(m, v, jnp.zeros_like(v))


SB_GROUP = 4
RET_GROUP = 4


def _stack_heads(v):
    return jnp.concatenate([_masked(v, _head_mask(0)), _masked(v, _head_mask(1))], axis=0)


def _side_by_side(v, t):
    return jnp.concatenate([v[:t], v[t:]], axis=1)


def _split_bf16(v):
    hi = v.astype(BF16)
    lo = (v - hi.astype(F32)).astype(BF16)
    return jnp.concatenate([hi, lo], axis=1)


def _tile_pos(i, j, tq, tk):
    row = jnp.bitwise_and(lax.broadcasted_iota(jnp.int32, (2 * tq, tk), 0), tq - 1) + i * tq
    col = lax.broadcasted_iota(jnp.int32, (2 * tq, tk), 1) + j * tk
    return row, col


def _n_groups(i, tq, tk, grp):
    return ((i + 1) * (tq // tk) + grp - 1) // grp


def _n_full(i, tq, tk, grp):
    return (i * (tq // tk)) // grp


def _key_rows(j, tk):
    return pl.ds(pl.multiple_of(j * tk, tk), tk)


def _ret_weight(lg_rows, i, j, tq, tk):
    row, col = _tile_pos(i, j, tq, tk)
    same = jnp.right_shift(col, CHUNK_SHIFT) == jnp.right_shift(row, CHUNK_SHIFT)
    later = jnp.where(same, jnp.exp((2.0 * lg_rows) * (col - row).astype(F32)), 0.0)
    return jnp.where(col <= row, 1.0, later)


def _lg_rows(lg_ref, hp, tq):
    first = lax.broadcasted_iota(jnp.int32, (2 * tq, 1), 0) < tq
    return jnp.where(first, lg_ref[2 * hp], lg_ref[2 * hp + 1])


def _check_tiles(s, tq, tk, grp):
    assert tq % tk == 0 and tq & (tq - 1) == 0 and tk & (tk - 1) == 0
    assert s % tq == 0 and (s // tk) % grp == 0 and s // tk <= LANES


def _ret_fwd(qk_rot, v_bf, proj, gn_g, log_gamma, tq, tk, riders=None):
    s = qk_rot.shape[0]
    gate_off = (2 * RET_QK + RET_V) // (2 * RET_DV)
    n_pair = HEADS // 2
    _check_tiles(s, tq, tk, RET_GROUP)

    def body(lg_ref, q_ref, k_ref, v_ref, g_ref, w_ref, ret_ref, rg_ref):
        hp, i = pl.program_id(0), pl.program_id(1)
        qs = _stack_heads(q_ref[...])
        lg_rows = _lg_rows(lg_ref, hp, tq)

        def make_step(near_diagonal):
            def step(g, carry):
                o0, o1 = carry
                js = [g * RET_GROUP + sub for sub in range(RET_GROUP)]
                rows = [_key_rows(j, tk) for j in js]
                ss = [_dot(qs, k_ref[rw, :], _NT) for rw in rows]
                if near_diagonal:
                    ss = [sc * _ret_weight(lg_rows, i, j, tq, tk) for sc, j in zip(ss, js)]
                for sc, rw in zip(ss, rows):
                    p = sc.astype(BF16)
                    o0 = o0 + _dot(p[:tq], v_ref[rw, 0:RET_DV])
                    o1 = o1 + _dot(p[tq:], v_ref[rw, RET_DV:2 * RET_DV])
                return o0, o1
            return step

        zero = jnp.zeros((tq, RET_DV), F32)
        n_full = _n_full(i, tq, tk, RET_GROUP)
        outs = lax.fori_loop(0, n_full, make_step(False), (zero, zero))
        outs = lax.fori_loop(n_full, _n_groups(i, tq, tk, RET_GROUP), make_step(True), outs)
        for hh, o in enumerate(outs):
            cols = slice(hh * RET_DV, (hh + 1) * RET_DV)
            ret_ref[:, cols] = o
            mu = jnp.sum(o, axis=1, keepdims=True) * (1.0 / RET_DV)
            xc = o - mu
            var = jnp.sum(xc * xc, axis=1, keepdims=True) * (1.0 / RET_DV)
            nrm = xc * lax.rsqrt(var + EPS) * w_ref[:, cols]
            g = g_ref[:, cols].astype(F32)
            rg_ref[:, cols] = (g * _sigmoid(g) * nrm).astype(BF16)

    pw = 2 * RET_DV
    blk = lambda hp, i: (i, hp)
    return _call("ret_fwd", body, (n_pair, s // tq),
                 [(log_gamma, None, pltpu.SMEM),
                  (qk_rot, (tq, LANES), blk),
                  (qk_rot, (s, LANES), lambda hp, i: (0, n_pair + hp)),
                  (v_bf, (s, pw), lambda hp, i: (0, hp)),
                  (proj, (tq, pw), lambda hp, i: (i, gate_off + hp)),
                  (gn_g, (1, pw), lambda hp, i: (0, hp))],
                 [((s, RET_V), F32, (tq, pw), blk), ((s, RET_V), BF16, (tq, pw), blk)], riders=riders)


def _tri2(tk, strict_upper):
    r = jnp.bitwise_and(lax.broadcasted_iota(jnp.int32, (2 * tk, tk), 0), tk - 1)
    cc = lax.broadcasted_iota(jnp.int32, (2 * tk, tk), 1)
    return ((r > cc) if strict_upper else (r < cc)).astype(BF16)


def _sb_scores(qs, k_ref, i, js, tq, tk, tri2, near_diagonal):
    zs = [_dot(qs, k_ref[_key_rows(j, tk), :], _NT) for j in js]
    log1ps = [jnp.log2(1.0 + jnp.exp2(-jnp.abs(z))) for z in zs]
    log_1ms = [-jnp.maximum(z, 0.0) - t for z, t in zip(zs, log1ps)]
    log_bs = [jnp.minimum(z, 0.0) - t for z, t in zip(zs, log1ps)]
    valids = [None] * len(js)
    if near_diagonal:
        valids = []
        for j in js:
            row, col = _tile_pos(i, j, tq, tk)
            valids.append(col < row)
        log_1ms = [jnp.where(v, l, 0.0) for v, l in zip(valids, log_1ms)]
    sticks = [lax.dot_general(_split_bf16(l), tri2, _NN, preferred_element_type=F32) for l in log_1ms]
    sums = [jnp.sum(l, axis=1, keepdims=True) for l in log_1ms]
    return log_1ms, log_bs, sticks, valids, sums


def _sb_weights(log_b, stick, c, valid):
    a = jnp.exp2(log_b + stick + c)
    return a if valid is None else jnp.where(valid, a, 0.0)


def _sb_fwd(qkv, tq, tk, riders=None):
    s = qkv.shape[0]
    n_pair = HEADS // 2
    _check_tiles(s, tq, tk, SB_GROUP)

    def body(q_ref, k_ref, v_ref, o_ref, carry_ref):
        i = pl.program_id(1)
        upper2 = _tri2(tk, True)
        lane = lax.broadcasted_iota(jnp.int32, (1, LANES), 1)
        qs = _stack_heads(q_ref[...])
        carry_ref[...] = jnp.zeros_like(carry_ref)
        n_full, n_groups = _n_full(i, tq, tk, SB_GROUP), _n_groups(i, tq, tk, SB_GROUP)

        def make_step(near_diagonal, last):
            def step(n, carry):
                c, o = carry
                g = last - 1 - n
                js = [g * SB_GROUP + sub for sub in range(SB_GROUP)]
                _, log_bs, sticks, valids, sums = _sb_scores(qs, k_ref, i, js, tq, tk, upper2, near_diagonal)
                cs = [None] * SB_GROUP
                for sub in reversed(range(SB_GROUP)):
                    cs[sub] = c
                    c = c + sums[sub]
                for sub, j in enumerate(js):
                    a = _sb_weights(log_bs[sub], sticks[sub], cs[sub], valids[sub])
                    o = o + _dot(_side_by_side(a.astype(BF16), tq), _stack_heads(v_ref[_key_rows(j, tk), :]))
                for hh in range(2):
                    cols = slice(hh * LANES, (hh + 1) * LANES)
                    cm = carry_ref[:, cols]
                    for sub, j in enumerate(js):
                        cm = jnp.where(lane == j, cs[sub][hh * tq:(hh + 1) * tq], cm)
                    carry_ref[:, cols] = cm
                return c, o
            return step

        carry = (jnp.zeros((2 * tq, 1), F32), jnp.zeros((tq, LANES), F32))
        carry = lax.fori_loop(0, n_groups - n_full, make_step(True, n_groups), carry)
        _, acc = lax.fori_loop(0, n_full, make_step(False, n_full), carry)
        o_ref[...] = acc

    return _call("sb_fwd", body, (n_pair, s // tq),
                 [(qkv, (tq, LANES), lambda hp, i: (i, hp)),
                  (qkv, (s, LANES), lambda hp, i: (0, n_pair + hp)),
                  (qkv, (s, LANES), lambda hp, i: (0, 2 * n_pair + hp))],
                 [((s, SB_W), F32, (tq, LANES), lambda hp, i: (i, hp)),
                  ((s, HEADS * LANES), F32, (tq, 2 * LANES), lambda hp, i: (i, hp))], riders=riders)


def _merge(retg, sb, w_ret, w_sb, proj, tm, tn, riders=None):
    s, d = retg.shape[0], w_ret.shape[1]
    ar_off = (2 * RET_QK + 2 * RET_V + 3 * SB_W) // tn
    as_off = ar_off + d // tn

    def body(rg_ref, sb_ref, wr_ref, ws_ref, ar_ref, as_ref, mix_ref, r_ref, s_ref):
        rr = _dot(rg_ref[...], wr_ref[...])
        ss = _dot(sb_ref[...], ws_ref[...])
        mix_ref[...] = (_sigmoid(ar_ref[...].astype(F32)) * rr + _sigmoid(as_ref[...].astype(F32)) * ss).astype(BF16)
        r_ref[...] = rr.astype(BF16)
        s_ref[...] = ss.astype(BF16)

    tile = (tm, tn)
    return _call("merge", body, (d // tn, s // tm),
                 [(retg, (tm, RET_V), lambda j, i: (i, 0)), (sb, (tm, SB_W), lambda j, i: (i, 0)),
                  (w_ret, (RET_V, tn), lambda j, i: (0, j)), (w_sb, (SB_W, tn), lambda j, i: (0, j)),
                  (proj, tile, lambda j, i: (i, ar_off + j)), (proj, tile, lambda j, i: (i, as_off + j))],
                 [((s, d), BF16, tile, lambda j, i: (i, j))] * 3, riders=riders)


def _out_proj(mixed, w_out, x, mod, gp1, g2, tm):
    s, d = x.shape

    def body(a_ref, w_ref, x_ref, mod_ref, gp_ref, g2_ref, y_ref, hres_ref, h2_ref):
        y = _dot(a_ref[...], w_ref[...])
        y_ref[...] = y
        ny, _ = _rms(y, d)
        hres = x_ref[...] + mod_ref[:, 2 * d:3 * d] * (ny * gp_ref[...])
        hres_ref[...] = hres
        n2, _ = _rms(hres, d)
        h2_ref[...] = (n2 * g2_ref[...] * (1.0 + mod_ref[:, 4 * d:5 * d]) + mod_ref[:, 3 * d:4 * d]).astype(BF16)

    row = lambda i: (i, 0)
    fix = lambda i: (0, 0)
    return _call("out_proj", body, (s // tm,),
                 [(mixed, (tm, d), row), (w_out, (d, d), fix), (x, (tm, d), row),
                  (mod, (1, 6 * d), fix), (gp1, (1, d), fix), (g2, (1, d), fix)],
                 [((s, d), F32, (tm, d), row), ((s, d), F32, (tm, d), row), ((s, d), BF16, (tm, d), row)])


def _ff1(h2, w_ff1, tm, tn):
    s, f = h2.shape[0], w_ff1.shape[1]
    tm = min(tm, s)

    def body(a_ref, w_ref, u_ref, act_ref):
        u = _dot(a_ref[...], w_ref[...])
        r = jnp.maximum(u, 0.0)
        u_ref[...] = u.astype(BF16)
        act_ref[...] = (r * r).astype(BF16)

    d = h2.shape[1]
    return _call("ff1", body, (f // tn, s // tm),
                 [(h2, (tm, d), lambda j, i: (i, 0)), (w_ff1, (d, tn), lambda j, i: (0, j))],
                 [((s, f), BF16, (tm, tn), lambda j, i: (i, j))] * 2)


def _ff2_loss(act, w_ff2, hres, target, mod, gp2, tm):
    s, d = hres.shape
    f = act.shape[1]

    def body(a_ref, w_ref, h_ref, t_ref, mod_ref, gp_ref, dout_ref, df_ref, loss_ref, dgt_ref, dgp_ref):
        first = pl.program_id(0) == 0
        ff = _dot(a_ref[...], w_ref[...])
        nf, rf = _rms(ff, d)
        gt, gp = mod_ref[:, 5 * d:6 * d], gp_ref[...]
        out = h_ref[...] + gt * (nf * gp)
        err = out - t_ref[...]
        sq = jnp.sum(err * err, axis=1, keepdims=True)
        _accum(loss_ref, jnp.sum(sq, axis=0, keepdims=True), first)
        dout = err * (1.0 / d)
        dout_ref[...] = dout
        _accum(dgt_ref, _colsum(dout * (nf * gp)), first)
        _accum(dgp_ref, _colsum(dout * gt * nf), first)
        df_ref[...] = _rms_bwd(dout * gt * gp, nf, rf, d).astype(BF16)

    row = lambda i: (i, 0)
    fix = lambda i: (0, 0)
    return _call("ff2_loss", body, (s // tm,),
                 [(act, (tm, f), row), (w_ff2, (f, d), fix), (hres, (tm, d), row), (target, (tm, d), row),
                  (mod, (1, 6 * d), fix), (gp2, (1, d), fix)],
                 [((s, d), F32, (tm, d), row), ((s, d), BF16, (tm, d), row), ((1, 1), F32, (1, 1), fix),
                  ((1, d), F32, (1, d), fix), ((1, d), F32, (1, d), fix)])


def _ff2_bwd(df, w_ff2, u, tm, tn):
    s, d = df.shape
    f = w_ff2.shape[0]
    tm = min(tm, s)

    def body(a_ref, w_ref, u_ref, du_ref):
        da = _dot(a_ref[...], w_ref[...], _NT)
        du_ref[...] = (da * (2.0 * jnp.maximum(u_ref[...].astype(F32), 0.0))).astype(BF16)

    return _call("ff2_bwd", body, (f // tn, s // tm),
                 [(df, (tm, d), lambda j, i: (i, 0)), (w_ff2, (tn, d), lambda j, i: (j, 0)),
                  (u, (tm, tn), lambda j, i: (i, j))],
                 [((s, f), BF16, (tm, tn), lambda j, i: (i, j))])[0]


def _ff1_bwd(du, w_ff1, hres, dout, y, mod, g2, gp1, tm, riders=None):
    s, d = hres.shape
    f = du.shape[1]

    def body(a_ref, w_ref, h_ref, do_ref, y_ref, mod_ref, g2_ref, gp_ref,
             dh_ref, dy_ref, dsh_ref, dsc_ref, dg2_ref, dgt_ref, dgp_ref):
        first = pl.program_id(0) == 0
        dh2 = _dot(a_ref[...], w_ref[...], _NT)
        n2, r2 = _rms(h_ref[...], d)
        g2, sc2 = g2_ref[...], mod_ref[:, 4 * d:5 * d]
        _accum(dsh_ref, _colsum(dh2), first)
        _accum(dsc_ref, _colsum(dh2 * n2 * g2), first)
        _accum(dg2_ref, _colsum(dh2 * n2 * (1.0 + sc2)), first)
        dhres = do_ref[...] + _rms_bwd(dh2 * g2 * (1.0 + sc2), n2, r2, d)
        dh_ref[...] = dhres
        ny, ry = _rms(y_ref[...], d)
        gt, gp = mod_ref[:, 2 * d:3 * d], gp_ref[...]
        _accum(dgt_ref, _colsum(dhres * (ny * gp)), first)
        _accum(dgp_ref, _colsum(dhres * gt * ny), first)
        dy_ref[...] = _rms_bwd(dhres * gt * gp, ny, ry, d).astype(BF16)

    row = lambda i: (i, 0)
    fix = lambda i: (0, 0)
    vec = ((1, d), F32, (1, d), fix)
    return _call("ff1_bwd", body, (s // tm,),
                 [(du, (tm, f), row), (w_ff1, (d, f), fix), (hres, (tm, d), row), (dout, (tm, d), row),
                  (y, (tm, d), row), (mod, (1, 6 * d), fix), (g2, (1, d), fix), (gp1, (1, d), fix)],
                 [((s, d), F32, (tm, d), row), ((s, d), BF16, (tm, d), row), vec, vec, vec, vec, vec], riders=riders)


def _out_bwd(dy, w_out, proj, r_bf, s_bf, tm, tn):
    s, d = dy.shape
    ar_off = (2 * RET_QK + 2 * RET_V + 3 * SB_W) // tn
    as_off = ar_off + d // tn

    def body(a_ref, w_ref, ar_ref, as_ref, r_ref, s_ref, dr_ref, ds_ref, dar_ref, das_ref):
        dm = _dot(a_ref[...], w_ref[...], _NT)
        sr, ss = _sigmoid(ar_ref[...].astype(F32)), _sigmoid(as_ref[...].astype(F32))
        dr_ref[...] = (dm * sr).astype(BF16)
        ds_ref[...] = (dm * ss).astype(BF16)
        dar_ref[...] = (dm * r_ref[...].astype(F32) * sr * (1.0 - sr)).astype(BF16)
        das_ref[...] = (dm * s_ref[...].astype(F32) * ss * (1.0 - ss)).astype(BF16)

    tile = (tm, tn)
    here = lambda j, i: (i, j)
    return _call("out_bwd", body, (d // tn, s // tm),
                 [(dy, (tm, d), lambda j, i: (i, 0)), (w_out, (tn, d), lambda j, i: (j, 0)),
                  (proj, tile, lambda j, i: (i, ar_off + j)), (proj, tile, lambda j, i: (i, as_off + j)),
                  (r_bf, tile, here), (s_bf, tile, here)],
                 [((s, d), BF16, tile, here)] * 4)


def _gn_bwd(dretg, ret, proj, gn_g, tm, riders=None):
    s = ret.shape[0]
    gate_off = (2 * RET_QK + RET_V) // RET_V

    def body(d_ref, r_ref, g_ref, w_ref, dg_ref, dret_ref, dw_ref):
        first = pl.program_id(0) == 0
        for h in range(HEADS):
            cols = slice(h * RET_DV, (h + 1) * RET_DV)
            o, g, w, dr = r_ref[:, cols], g_ref[:, cols].astype(F32), w_ref[:, cols], d_ref[:, cols].astype(F32)
            mu = jnp.sum(o, axis=1, keepdims=True) * (1.0 / RET_DV)
            xc = o - mu
            rstd = lax.rsqrt(jnp.sum(xc * xc, axis=1, keepdims=True) * (1.0 / RET_DV) + EPS)
            n = xc * rstd
            sg = _sigmoid(g)
            silu = g * sg
            dg_ref[:, cols] = (dr * n * w * (sg * (1.0 + g * (1.0 - sg)))).astype(BF16)
            _accum(dw_ref.at[:, cols], _colsum(dr * silu * n), first)
            dn = dr * silu * w
            m1 = jnp.sum(dn, axis=1, keepdims=True) * (1.0 / RET_DV)
            m2 = jnp.sum(dn * n, axis=1, keepdims=True) * (1.0 / RET_DV)
            dret_ref[:, cols] = (rstd * (dn - m1 - n * m2)).astype(BF16)

    row = lambda i: (i, 0)
    fix = lambda i: (0, 0)
    return _call("gn_bwd", body, (s // tm,),
                 [(dretg, (tm, RET_V), row), (ret, (tm, RET_V), row),
                  (proj, (tm, RET_V), lambda i: (i, gate_off)), (gn_g, (1, RET_V), fix)],
                 [((s, RET_V), BF16, (tm, RET_V), row), ((s, RET_V), BF16, (tm, RET_V), row),
                  ((1, RET_V), F32, (1, RET_V), fix)], riders=riders)


def _ret_bwd(qk_rot, v_bf, dret, log_gamma, tq, tk, riders=None):
    s = qk_rot.shape[0]
    n_pair = HEADS // 2
    pw = 2 * RET_DV
    _check_tiles(s, tq, tk, RET_GROUP)

    def body(lg_ref, q_ref, k_ref, v_ref, do_ref, dq_ref, dk_ref, dv_ref):
        hp, i = pl.program_id(0), pl.program_id(1)

        @pl.when(i == 0)
        def _():
            dk_ref[...] = jnp.zeros_like(dk_ref)
            dv_ref[...] = jnp.zeros_like(dv_ref)

        qs = _stack_heads(q_ref[...])
        lg_rows = _lg_rows(lg_ref, hp, tq)
        do0, do1 = do_ref[:, 0:RET_DV], do_ref[:, RET_DV:pw]

        def make_step(near_diagonal):
            def step(g, dq):
                js = [g * RET_GROUP + sub for sub in range(RET_GROUP)]
                rows = [_key_rows(j, tk) for j in js]
                ss = [_dot(qs, k_ref[rw, :], _NT) for rw in rows]
                dps = [jnp.concatenate([_dot(do0, v_ref[rw, 0:RET_DV], _NT), _dot(do1, v_ref[rw, RET_DV:pw], _NT)],
                                       axis=0) for rw in rows]
                if near_diagonal:
                    ws = [_ret_weight(lg_rows, i, j, tq, tk) for j in js]
                    ss = [sc * w for sc, w in zip(ss, ws)]
                    dps = [dp * w for dp, w in zip(dps, ws)]
                for sc, dp, rw in zip(ss, dps, rows):
                    p, ds = sc.astype(BF16), dp.astype(BF16)
                    dv_ref[rw, 0:RET_DV] += _dot(p[:tq], do0, _TN)
                    dv_ref[rw, RET_DV:pw] += _dot(p[tq:], do1, _TN)
                    dk_ref[rw, :] += _dot(ds, qs, _TN)
                    dq = dq + _dot(_side_by_side(ds, tq), _stack_heads(k_ref[rw, :]))
                return dq
            return step

        n_full = _n_full(i, tq, tk, RET_GROUP)
        dq = lax.fori_loop(0, n_full, make_step(False), jnp.zeros((tq, LANES), F32))
        dq_ref[...] = lax.fori_loop(n_full, _n_groups(i, tq, tk, RET_GROUP), make_step(True), dq)

    blk = lambda hp, i: (i, hp)
    return _call("ret_bwd", body, (n_pair, s // tq),
                 [(log_gamma, None, pltpu.SMEM),
                  (qk_rot, (tq, LANES), blk),
                  (qk_rot, (s, LANES), lambda hp, i: (0, n_pair + hp)),
                  (v_bf, (s, pw), lambda hp, i: (0, hp)),
                  (dret, (tq, pw), blk)],
                 [((s, RET_QK), F32, (tq, LANES), blk),
                  ((s, RET_QK), F32, (s, LANES), lambda hp, i: (0, hp)),
                  ((s, RET_V), F32, (s, pw), lambda hp, i: (0, hp))], riders=riders)


def _sb_bwd(qkv, carries, do, tq, tk, riders=None):
    s = qkv.shape[0]
    n_pair = HEADS // 2
    _check_tiles(s, tq, tk, SB_GROUP)

    def body(q_ref, k_ref, v_ref, c_ref, do_ref, dq_ref, dk_ref, dv_ref):
        i = pl.program_id(1)

        @pl.when(i == 0)
        def _():
            dk_ref[...] = jnp.zeros_like(dk_ref)
            dv_ref[...] = jnp.zeros_like(dv_ref)

        upper2 = _tri2(tk, True)
        lower2 = _tri2(tk, False)
        lane = lax.broadcasted_iota(jnp.int32, (1, LANES), 1)
        qs = _stack_heads(q_ref[...])
        dos = _stack_heads(do_ref[...].astype(BF16))
        cms = jnp.concatenate([c_ref[:, 0:LANES], c_ref[:, LANES:2 * LANES]], axis=0)

        def make_step(near_diagonal):
            def step(g, carry):
                c_e, dq = carry
                js = [g * SB_GROUP + sub for sub in range(SB_GROUP)]
                rows = [_key_rows(j, tk) for j in js]
                _, log_bs, sticks, valids, _ = _sb_scores(qs, k_ref, i, js, tq, tk, upper2, near_diagonal)
                das = [_dot(dos, v_ref[rw, :], _NT) for rw in rows]
                c_sticks = [jnp.sum(jnp.where(lane == j, cms, 0.0), axis=1, keepdims=True) for j in js]
                avals = [_sb_weights(lb, st, cst, v) for lb, st, cst, v in zip(log_bs, sticks, c_sticks, valids)]
                es = [a * da for a, da in zip(avals, das)]
                prefixes = [lax.dot_general(_split_bf16(e), lower2, _NN, preferred_element_type=F32) for e in es]
                betas = [jnp.exp2(lb) for lb in log_bs]
                for sub in range(SB_GROUP):
                    dv_ref[rows[sub], :] += _dot(avals[sub], dos, _TN)
                for sub in range(SB_GROUP):
                    dz = es[sub] * (1.0 - betas[sub]) - (prefixes[sub] + c_e) * betas[sub]
                    if near_diagonal:
                        dz = jnp.where(valids[sub], dz, 0.0)
                    dz = dz.astype(BF16)
                    dk_ref[rows[sub], :] += _dot(dz, qs, _TN)
                    dq = dq + _dot(_side_by_side(dz, tq), _stack_heads(k_ref[rows[sub], :]))
                    c_e = c_e + jnp.sum(es[sub], axis=1, keepdims=True)
                return c_e, dq
            return step

        n_full = _n_full(i, tq, tk, SB_GROUP)
        carry = (jnp.zeros((2 * tq, 1), F32), jnp.zeros((tq, LANES), F32))
        carry = lax.fori_loop(0, n_full, make_step(False), carry)
        _, dq = lax.fori_loop(n_full, _n_groups(i, tq, tk, SB_GROUP), make_step(True), carry)
        dq_ref[...] = dq

    blk = lambda hp, i: (i, hp)
    return _call("sb_bwd", body, (n_pair, s // tq),
                 [(qkv, (tq, LANES), blk),
                  (qkv, (s, LANES), lambda hp, i: (0, n_pair + hp)),
                  (qkv, (s, LANES), lambda hp, i: (0, 2 * n_pair + hp)),
                  (carries, (tq, 2 * LANES), blk), (do, (tq, LANES), blk)],
                 [((s, SB_W), F32, (tq, LANES), blk),
                  ((s, SB_W), F32, (s, LANES), lambda hp, i: (0, hp)),
                  ((s, SB_W), F32, (s, LANES), lambda hp, i: (0, hp))], riders=riders)


def _assemble_dproj(dq_r, dk_r, dv_r, dg_r, dq_s, dk_s, dv_s, da_r, da_s, cos, sin, idx_col, lg_lanes, tm):
    s, d = da_r.shape
    width = 2 * RET_QK + 2 * RET_V + 3 * SB_W + 2 * d

    def body(dq_ref, dk_ref, dv_ref, dg_ref, dqs_ref, dks_ref, dvs_ref, dar_ref, das_ref, cos_ref, sin_ref,
             idx_ref, lg_ref, o_ref):
        lane = lax.broadcasted_iota(jnp.int32, (1, LANES), 1)
        first = jnp.bitwise_and(lane, RET_DQK - 1) < (RET_DQK // 2)
        cos, sin = cos_ref[...], sin_ref[...]
        idx = idx_ref[...]
        for src, base, sign, scale in ((dq_ref, 0, 1.0, 1.0), (dk_ref, RET_QK, -1.0, RET_DQK ** -0.5)):
            for g in range(RET_QK // LANES):
                v = src[:, g * LANES:(g + 1) * LANES] * (_decay_scale(lg_ref, idx, g, sign) * scale)
                sw = jnp.where(first, pltpu.roll(v, LANES - RET_DQK // 2, 1), pltpu.roll(v, RET_DQK // 2, 1))
                o_ref[:, base + g * LANES:base + (g + 1) * LANES] = (v * cos - sw * sin).astype(BF16)
        off = 2 * RET_QK
        o_ref[:, off:off + RET_V] = dv_ref[...].astype(BF16)
        off += RET_V
        o_ref[:, off:off + RET_V] = dg_ref[...]
        off += RET_V
        o_ref[:, off:off + SB_W] = (dqs_ref[...] * (SB_DH ** -0.5)).astype(BF16)
        off += SB_W
        o_ref[:, off:off + SB_W] = (dks_ref[...] * LN2).astype(BF16)
        off += SB_W
        o_ref[:, off:off + SB_W] = dvs_ref[...].astype(BF16)
        off += SB_W
        o_ref[:, off:off + d] = dar_ref[...]
        off += d
        o_ref[:, off:off + d] = das_ref[...]

    row = lambda i: (i, 0)
    ins = [(a, (tm, a.shape[1]), row) for a in (dq_r, dk_r, dv_r, dg_r, dq_s, dk_s, dv_s, da_r, da_s, cos, sin, idx_col)]
    ins.append((lg_lanes, (1, RET_QK), lambda i: (0, 0)))
    return _call("assemble_dproj", body, (s // tm,), ins, [((s, width), BF16, (tm, width), row)])[0]


def _in_bwd(dproj, w_in_t, x, dhres, mod, g1, tm, riders=None):
    s, d = x.shape
    width = dproj.shape[1]

    def body(a_ref, w_ref, x_ref, dh_ref, mod_ref, g_ref, dx_ref, dsh_ref, dsc_ref, dg_ref):
        first = pl.program_id(0) == 0
        dh = _dot(a_ref[...], w_ref[...])
        n1, r1 = _rms(x_ref[...], d)
        g1, sc1 = g_ref[...], mod_ref[:, d:2 * d]
        _accum(dsh_ref, _colsum(dh), first)
        _accum(dsc_ref, _colsum(dh * n1 * g1), first)
        _accum(dg_ref, _colsum(dh * n1 * (1.0 + sc1)), first)
        dx_ref[...] = dh_ref[...] + _rms_bwd(dh * g1 * (1.0 + sc1), n1, r1, d)

    row = lambda i: (i, 0)
    fix = lambda i: (0, 0)
    vec = ((1, d), F32, (1, d), fix)
    return _call("in_bwd", body, (s // tm,),
                 [(dproj, (tm, width), row), (w_in_t, (width, d), fix), (x, (tm, d), row), (dhres, (tm, d), row),
                  (mod, (1, 6 * d), fix), (g1, (1, d), fix)],
                 [((s, d), F32, (tm, d), row), vec, vec, vec], riders=riders)


def _adamw(w, g, m, v):
    m = ADAM_B1 * m + (1.0 - ADAM_B1) * g
    v = ADAM_B2 * v + (1.0 - ADAM_B2) * (g * g)
    m_hat = m / (1.0 - ADAM_B1 ** ADAM_STEP)
    v_hat = v / (1.0 - ADAM_B2 ** ADAM_STEP)
    delta = -ADAM_LR * (m_hat / (jnp.sqrt(v_hat) + ADAM_EPS) + ADAM_WD * w)
    return delta, m, v


def _adam_reduce(name, parts, w, m, v, tr):
    rws, cls = w.shape
    tr = min(tr, rws)
    n_parts = parts.shape[0]

    def body(p_ref, w_ref, m_ref, v_ref, g_out, d_out, m_out, v_out):
        g = p_ref[0].astype(F32)
        for k in range(1, n_parts):
            g = g + p_ref[k].astype(F32)
        delta, mn, vn = _adamw(w_ref[...], g, m_ref[...], v_ref[...])
        g_out[...] = g
        d_out[...] = delta
        m_out[...] = mn
        v_out[...] = vn

    row = lambda i: (i, 0)
    blk = (tr, cls)
    return _call(name, body, (rws // tr,),
                 [(parts, (n_parts, tr, cls), lambda i: (0, i, 0)), (w, blk, row), (m, blk, row), (v, blk, row)],
                 [((rws, cls), F32, blk, row)] * 4)


def _ada_bwd_adam(cs_t, dmod_cols, w, m, v):
    d, nc = w.shape

    def body(c_ref, dm_ref, w_ref, m_ref, v_ref, g_out, d_out, m_out, v_out):
        g = c_ref[0] * dm_ref[0:1, :]
        for r in range(1, N_DEV):
            g = g + c_ref[r] * dm_ref[r:r + 1, :]
        delta, mn, vn = _adamw(w_ref[...], g, m_ref[...], v_ref[...])
        g_out[...] = g
        d_out[...] = delta
        m_out[...] = mn
        v_out[...] = vn

    fix = lambda i: (0, 0)
    blk = (d, nc)
    return _call("ada_bwd_adam", body, (1,),
                 [(cs_t, (N_DEV, d, 1), lambda i: (0, 0, 0)), (dmod_cols, (N_DEV, nc), fix), (w, blk, fix), (m, blk, fix), (v, blk, fix)],
                 [((d, nc), F32, blk, fix)] * 4)


def _small_adam(parts, w, m, v):
    n = w.shape[1]

    def body(p_ref, w_ref, m_ref, v_ref, g_out, d_out, m_out, v_out):
        g = p_ref[0:1, :]
        for k in range(1, N_DEV):
            g = g + p_ref[k:k + 1, :]
        delta, mn, vn = _adamw(w_ref[...], g, m_ref[...], v_ref[...])
        g_out[...] = g
        d_out[...] = delta
        m_out[...] = mn
        v_out[...] = vn

    fix = lambda i: (0, 0)
    return _call("small_adam", body, (1,),
                 [(parts, (N_DEV, n), fix), (w, (1, n), fix), (m, (1, n), fix), (v, (1, n), fix)],
                 [((1, n), F32, (1, n), fix)] * 4)


def kernel(x, c, positions, ada_w, ada_b, pre_mix_g, post_mix_g, pre_ffn_g, post_ffn_g, w_in, ret_gn_g, w_ret_branch, w_sb_branch, w_out, w_ff1, w_ff2, loss_target, m_ada_w, m_ada_b, m_pre_mix_g, m_post_mix_g, m_pre_ffn_g, m_post_ffn_g, m_w_in, m_ret_gn_g, m_w_ret_branch, m_w_sb_branch, m_w_out, m_w_ff1, m_w_ff2, v_ada_w, v_ada_b, v_pre_mix_g, v_post_mix_g, v_pre_ffn_g, v_post_ffn_g, v_w_in, v_ret_gn_g, v_w_ret_branch, v_w_sb_branch, v_w_out, v_w_ff1, v_w_ff2):
    _, s, d = x.shape
    d_ff = w_ff1.shape[2] * N_DEV
    d_in = w_in.shape[2] * N_DEV
    me = 4 * lax.axis_index("x") + 2 * lax.axis_index("y") + lax.axis_index("c")
    x2, tgt = x[0], loss_target[0]

    core = lax.axis_index("c").astype(jnp.int32).reshape(1)
    bf = lambda w: w[0].astype(BF16)

    w_in_t, m_in_t, v_in_t = (jnp.swapaxes(a[0], 0, 1) for a in (w_in, m_w_in, v_w_in))

    c_all, g_in = _exchange("gather_in", [c, w_in_t.astype(BF16)], ["gather", "gather_chip"])
    c_all = c_all.reshape(N_DEV, d)

    n_ada = ada_w.shape[2]
    cs_all = _silu_rows(c_all)
    ada_b_cols = lax.dynamic_slice(ada_b, (0, me * n_ada), (1, n_ada))
    mod_cols = _ada_fwd(cs_all, ada_w[0], ada_b_cols)
    mod_all = _exchange("gather_mod", [mod_cols], ["gather"])[0]
    mod = lax.dynamic_index_in_dim(mod_all, me, axis=1, keepdims=False).reshape(1, 6 * d)

    tm = min(256, s)
    h, g_in = _pre_norm(x2, pre_mix_g, mod, tm, riders=([g_in], ["forward"]))
    wt_in = g_in.reshape(d_in, d)
    proj = _matmul("in_proj", h, wt_in, "nt", s, 512, BF16)
    pos_col = positions.reshape(s, 1).astype(F32)
    freqs = ROPE_BASE ** (-jnp.arange(0, RET_DQK, 2, dtype=F32) / RET_DQK)
    inv_freq = jnp.tile(freqs, LANES // (RET_DQK // 2)).reshape(1, LANES)
    log_gamma_np = np.log1p(-(2.0 ** (-5.0 - np.arange(HEADS))))
    log_gamma = jnp.asarray(log_gamma_np, F32)
    lg_lanes = jnp.asarray(np.repeat(log_gamma_np, RET_DQK).reshape(1, RET_QK), F32)
    idx_col = (jnp.arange(s, dtype=F32) - (s // 2)).reshape(s, 1)
    qk_rot, v_bf, qkv_sb, cos_t, sin_t = _prep(proj, pos_col, idx_col, inv_freq, lg_lanes, tm)
    tq, tk = min(256, s), min(128, s)
    later = [bf(w_ret_branch), bf(w_sb_branch), bf(w_out), bf(w_ff1)]
    sb, sb_carry, *later = _sb_fwd(qkv_sb, tq, tk, riders=(later, ["gather_chip"] * 4))
    ret, retg, g_ret, g_sb, g_out, g_ff1, g_ff2 = _ret_fwd(
        qk_rot, v_bf, proj, ret_gn_g, log_gamma, tq, tk,
        riders=(later + [bf(w_ff2)], ["forward"] * 4 + ["gather_chip"]))
    wf_ret = g_ret.reshape(RET_V, d)
    wf_sb = jnp.moveaxis(g_sb, 0, 1).reshape(SB_W, d)
    wf_out = g_out.reshape(d, d)
    wf_ff1 = jnp.moveaxis(g_ff1, 0, 1).reshape(d, d_ff)
    mixed, r_bf, s_bf, g_ff2 = _merge(retg, sb, wf_ret, wf_sb, proj, tm, min(512, d), riders=([g_ff2], ["forward"]))
    wf_ff2 = g_ff2.reshape(d_ff, d)
    y, hres, h2 = _out_proj(mixed, wf_out, x2, mod, post_mix_g, pre_ffn_g, tm)
    u, act = _ff1(h2, wf_ff1, s, 512)
    dout, df, loss_sum, d_gt2, d_gp2 = _ff2_loss(act, wf_ff2, hres, tgt, mod, post_ffn_g, tm)

    du = _ff2_bwd(df, wf_ff2, u, s, 512)
    gw_ff2 = _matmul("grad_w_ff2", act, df, "tn", 512, d, BF16).reshape(N_DEV, d_ff // N_DEV, d)
    gw_ff1 = _matmul("grad_w_ff1", h2, du, "tn", d, d_ff // N_DEV, BF16, blocked_out=True)
    dhres, dy, d_sh2, d_sc2, d_g2, d_gt1, d_gp1, t_ff1, t_ff2 = _ff1_bwd(
        du, wf_ff1, hres, dout, y, mod, pre_ffn_g, post_mix_g, tm, riders=([gw_ff1, gw_ff2], ["pair"] * 2))
    s_ff1 = _pair_sum("pair_sum_ff1", gw_ff1, t_ff1, core, 256)
    s_ff2 = _pair_sum("pair_sum_ff2", gw_ff2, t_ff2, core, 256)
    d_r, d_s, da_r, da_s = _out_bwd(dy, wf_out, proj, r_bf, s_bf, tm, min(512, d))
    gw_out = _matmul("grad_w_out", mixed, dy, "tn", 512, d, BF16).reshape(N_DEV, d // N_DEV, d)
    dretg = _matmul("ret_branch_bwd", d_r, wf_ret, "nt", s, 512, BF16)
    dsb = _matmul("sb_branch_bwd", d_s, wf_sb, "nt", s, 512, F32)
    gw_ret = _matmul("grad_w_ret", retg, d_r, "tn", 512, d, BF16).reshape(N_DEV, RET_V // N_DEV, d)
    gw_sb = _matmul("grad_w_sb", sb, d_s, "tn", 512, d // N_DEV, BF16, blocked_out=True)
    dq_s, dk_s, dv_s, p_ff1, p_ff2 = _sb_bwd(qkv_sb, sb_carry, dsb, tq, tk,
                                             riders=([s_ff1, s_ff2], ["chip_scatter"] * 2))
    dg_r, dret, d_gn = _gn_bwd(dretg, ret, proj, ret_gn_g, tm)
    dq_r, dk_r, dv_r, p_out, p_ret, p_sb = _ret_bwd(qk_rot, v_bf, dret, log_gamma, tq, tk,
                                                    riders=([gw_out, gw_ret, gw_sb], ["scatter"] * 3))
    dproj = _assemble_dproj(dq_r, dk_r, dv_r, dg_r, dq_s, dk_s, dv_s, da_r, da_s, cos_t, sin_t, idx_col, lg_lanes, tm)
    gw_in = _matmul("grad_w_in", dproj, h, "tn", 512, d, BF16).reshape(N_DEV, d_in // N_DEV, d)
    t_in = _exchange("pair_in", [gw_in], ["pair"])[0]
    tr_in = d_in // N_DEV // 4
    s_in = _pair_sum("pair_sum_in", gw_in, t_in, core, tr_in)
    grad_x, d_sh1, d_sc1, d_g1, p_in = _in_bwd(dproj, wt_in, x2, dhres, mod, pre_mix_g, tm,
                                               riders=([s_in], ["chip_scatter"]))
    loss_lanes = jnp.pad(loss_sum, ((0, 0), (0, LANES - 1)))
    small = jnp.concatenate([d_sh1, d_sc1, d_gt1, d_sh2, d_sc2, d_gt2, d_g1, d_gp1, d_g2, d_gp2, d_gn, loss_lanes], axis=1)
    small_all = _exchange("gather_small", [small], ["gather"])[0].reshape(N_DEV, small.shape[1])
    parts = [p_in, p_ret, p_sb, p_out, p_ff1, p_ff2]

    res = {}
    names = ["w_ret_branch", "w_sb_branch", "w_out", "w_ff1", "w_ff2"]
    ws = [w_ret_branch, w_sb_branch, w_out, w_ff1, w_ff2]
    ms = [m_w_ret_branch, m_w_sb_branch, m_w_out, m_w_ff1, m_w_ff2]
    vs = [v_w_ret_branch, v_w_sb_branch, v_w_out, v_w_ff1, v_w_ff2]
    for nm, p, w, m, v in zip(names, parts[1:], ws, ms, vs):
        res[nm] = [o[None] for o in _adam_reduce("adam_" + nm, p, w[0], m[0], v[0], 256)]
    res["w_in"] = [jnp.swapaxes(o, 0, 1)[None]
                   for o in _adam_reduce("adam_w_in", parts[0], w_in_t, m_in_t, v_in_t, tr_in)]
    dmod_cols = lax.dynamic_slice(small_all, (0, me * n_ada), (N_DEV, n_ada))
    res["ada_w"] = [o[None] for o in _ada_bwd_adam(cs_all.reshape(N_DEV, d, 1), dmod_cols, ada_w[0], m_ada_w[0], v_ada_w[0])]
    vec_names = ["ada_b", "pre_mix_g", "post_mix_g", "pre_ffn_g", "post_ffn_g", "ret_gn_g"]
    cat = lambda xs: jnp.concatenate(xs + [jnp.zeros((1, LANES), F32)], axis=1)
    packed = _small_adam(small_all,
                         cat([ada_b, pre_mix_g, post_mix_g, pre_ffn_g, post_ffn_g, ret_gn_g]),
                         cat([m_ada_b, m_pre_mix_g, m_post_mix_g, m_pre_ffn_g, m_post_ffn_g, m_ret_gn_g]),
                         cat([v_ada_b, v_pre_mix_g, v_post_mix_g, v_pre_ffn_g, v_post_ffn_g, v_ret_gn_g]))
    off = 0
    for nm, width in zip(vec_names, [6 * d, d, d, d, d, RET_V]):
        res[nm] = [p[:, off:off + width] for p in packed]
        off += width

    loss = (0.5 / d) * packed[0][0, off]
    order = ["ada_w", "ada_b", "pre_mix_g", "post_mix_g", "pre_ffn_g", "post_ffn_g", "w_in", "ret_gn_g",
             "w_ret_branch", "w_sb_branch", "w_out", "w_ff1", "w_ff2"]
    outs = [loss, grad_x[None]]
    for k in range(4):
        outs += [res[nm][k] for nm in order]
    return tuple(outs)
```

```python
import functools

import numpy as np
import jax
import jax.numpy as jnp
from jax import lax
from jax.experimental import pallas as pl
from jax.experimental.pallas import tpu as pltpu

F32 = jnp.float32
BF16 = jnp.bfloat16
N_DEV = 8
AXES = ("x", "y", "c")

EPS = 1e-6
CHUNK = 64
CHUNK_SHIFT = 6
HEADS = 8
RET_DQK = 64
RET_DV = 128
SB_DH = 64
RET_QK = HEADS * RET_DQK
RET_V = HEADS * RET_DV
SB_W = HEADS * SB_DH
ROPE_BASE = 10000.0
LANES = 128

ADAM_LR = 0.001
ADAM_B1 = 0.9
ADAM_B2 = 0.999
ADAM_EPS = 1e-08
ADAM_WD = 0.01
ADAM_STEP = 10

VMEM_LIMIT = 56 * 1024 * 1024

_NN = (((1,), (0,)), ((), ()))
_NT = (((1,), (1,)), ((), ()))
_TN = (((0,), (0,)), ((), ()))


def _dot(a, b, dims=_NN):
    if a.dtype != BF16:
        a = a.astype(BF16)
    if b.dtype != BF16:
        b = b.astype(BF16)
    return lax.dot_general(a, b, dims, preferred_element_type=F32)


def _dot_split(a, b):
    hi = a.astype(BF16)
    lo = (a - hi.astype(F32)).astype(BF16)
    return (lax.dot_general(hi, b, _NN, preferred_element_type=F32)
            + lax.dot_general(lo, b, _NN, preferred_element_type=F32))


def _sigmoid(x):
    return 1.0 / (1.0 + jnp.exp(-x))


def _rms(x, d):
    r = lax.rsqrt(jnp.sum(x * x, axis=1, keepdims=True) * (1.0 / d) + EPS)
    return x * r, r


def _rms_bwd(dn, n, r, d):
    return r * (dn - n * (jnp.sum(dn * n, axis=1, keepdims=True) * (1.0 / d)))


def _colsum(v):
    return jnp.sum(v, axis=0, keepdims=True)


def _accum(ref, val, first):
    @pl.when(first)
    def _():
        ref[...] = val

    @pl.when(jnp.logical_not(first))
    def _():
        ref[...] += val


KIND_SLOTS = {"gather": N_DEV, "scatter": N_DEV, "gather_chip": N_DEV, "forward": N_DEV, "pair": N_DEV // 2,
              "chip_scatter": N_DEV // 2}
SEMS_PER_ARRAY = N_DEV - 1


def _exchange_copies(ins, outs, send_sems, recv_sems, local_sems, kinds):
    x, y, c = (lax.axis_index(a) for a in AXES)
    me, chip, sibling = 4 * x + 2 * y + c, 2 * x + y, (x, y, 1 - c)
    mesh_id = pl.DeviceIdType.MESH
    other_chips = []
    for k in range(1, N_DEV // 2):
        px = 1 - x if k & 2 else x
        py = 1 - y if k & 1 else y
        other_chips.append((px, py))
    copies = []
    for i, kind in enumerate(kinds):
        def remote(src, dst, k, to, i=i):
            return pltpu.make_async_remote_copy(
                src_ref=src, dst_ref=dst, send_sem=send_sems.at[i * SEMS_PER_ARRAY + k],
                recv_sem=recv_sems.at[i * SEMS_PER_ARRAY + k], device_id=to, device_id_type=mesh_id)

        if kind in ("gather", "scatter"):
            pick = (lambda ref, d: ref.at[d]) if kind == "scatter" else (lambda ref, d: ref)
            copies.append(pltpu.make_async_copy(pick(ins[i], me), outs[i].at[me], local_sems.at[i]))
            for k in range(1, N_DEV):
                to = (1 - x if k & 4 else x, 1 - y if k & 2 else y, 1 - c if k & 1 else c)
                copies.append(remote(pick(ins[i], 4 * to[0] + 2 * to[1] + to[2]), outs[i].at[me], k - 1, to))
        elif kind == "gather_chip":
            copies.append(pltpu.make_async_copy(ins[i], outs[i].at[me], local_sems.at[i]))
            copies.append(remote(ins[i], outs[i].at[me], 0, sibling))
            for k, (px, py) in enumerate(other_chips):
                copies.append(remote(ins[i], outs[i].at[me], 1 + k, (px, py, c)))
        elif kind == "forward":
            for k, (px, py) in enumerate(other_chips):
                slot = 4 * px + 2 * py + c
                copies.append(remote(outs[i].at[slot], outs[i].at[slot], k, sibling))
        elif kind == "pair":
            for k in range(N_DEV // 2):
                copies.append(remote(ins[i].at[2 * k + 1 - c], outs[i].at[k], k, sibling))
        elif kind == "chip_scatter":
            copies.append(pltpu.make_async_copy(ins[i].at[chip], outs[i].at[chip], local_sems.at[i]))
            for k, (px, py) in enumerate(other_chips):
                copies.append(remote(ins[i].at[2 * px + py], outs[i].at[chip], k, (px, py, c)))
        else:
            raise ValueError(kind)
    return copies


def _exchange_shapes(arrays, kinds):
    shapes = []
    for a, kind in zip(arrays, kinds):
        tail = a.shape if kind in ("gather", "gather_chip") else a.shape[1:]
        shapes.append(jax.ShapeDtypeStruct((KIND_SLOTS[kind],) + tuple(tail), a.dtype))
    return shapes


def _exchange_sems(n):
    return [pltpu.SemaphoreType.DMA((n * SEMS_PER_ARRAY,)), pltpu.SemaphoreType.DMA((n * SEMS_PER_ARRAY,)),
            pltpu.SemaphoreType.DMA((n,))]


def _call(name, body, grid, ins, outs, scratch=(), riders=None, prefetch=None):
    any_spec = pl.BlockSpec(memory_space=pl.ANY)
    in_specs = [pl.BlockSpec(memory_space=im) if bs is None else pl.BlockSpec(bs, im) for _, bs, im in ins]
    out_specs = [pl.BlockSpec(bs, im) for _, _, bs, im in outs]
    out_shape = [jax.ShapeDtypeStruct(s, d) for s, d, _, _ in outs]
    operands = [a for a, _, _ in ins]
    scratch = list(scratch)
    aliases = {}
    n_pre = 0 if prefetch is None else 1
    kernel = functools.partial(body) if prefetch is None else (lambda _, *refs: body(*refs))
    if riders is not None:
        arrays, kinds = riders
        nr, n_in, n_out, n_scr = len(arrays), len(ins), len(outs), len(scratch)

        def kernel(*refs):
            refs = refs[n_pre:]
            own_in, ride_in = refs[:n_in], refs[n_in:n_in + nr]
            own_out = refs[n_in + nr:n_in + nr + n_out]
            ride_out = refs[n_in + nr + n_out:n_in + 2 * nr + n_out]
            own_scr = refs[n_in + 2 * nr + n_out:n_in + 2 * nr + n_out + n_scr]
            sems = refs[n_in + 2 * nr + n_out + n_scr:]
            ids = [pl.program_id(a) for a in range(len(grid))]
            first = functools.reduce(jnp.logical_and, [i == 0 for i in ids])
            last = functools.reduce(jnp.logical_and, [i == g - 1 for i, g in zip(ids, grid)])

            @pl.when(first)
            def _():
                for cp in _exchange_copies(ride_in, ride_out, *sems, kinds):
                    cp.start()

            body(*own_in, *own_out, *own_scr)

            @pl.when(last)
            def _():
                for cp in _exchange_copies(ride_in, ride_out, *sems, kinds):
                    cp.wait()

        in_specs += [any_spec] * nr
        out_specs += [any_spec] * nr
        out_shape += _exchange_shapes(arrays, kinds)
        operands += list(arrays)
        scratch += _exchange_sems(nr)
        aliases = {n_pre + n_in + r: n_out + r for r, kind in enumerate(kinds) if kind == "forward"}
    params = pltpu.CompilerParams(dimension_semantics=("arbitrary",) * len(grid), vmem_limit_bytes=VMEM_LIMIT)
    if prefetch is None:
        return pl.pallas_call(kernel, name=name, grid=grid, in_specs=in_specs, out_specs=out_specs,
                              out_shape=out_shape, scratch_shapes=scratch, input_output_aliases=aliases,
                              compiler_params=params)(*operands)
    grid_spec = pltpu.PrefetchScalarGridSpec(num_scalar_prefetch=1, grid=grid, in_specs=in_specs,
                                             out_specs=out_specs, scratch_shapes=scratch)
    return pl.pallas_call(kernel, name=name, grid_spec=grid_spec, out_shape=out_shape,
                          input_output_aliases=aliases, compiler_params=params)(prefetch, *operands)


def _exchange(name, arrays, kinds):
    n = len(arrays)

    def body(*refs):
        copies = _exchange_copies(refs[:n], refs[n:2 * n], *refs[2 * n:], kinds)
        for cp in copies:
            cp.start()
        for cp in copies:
            cp.wait()

    any_spec = pl.BlockSpec(memory_space=pl.ANY)
    return pl.pallas_call(
        functools.partial(body),
        name=name,
        in_specs=[any_spec] * n,
        out_specs=[any_spec] * n,
        out_shape=_exchange_shapes(arrays, kinds),
        scratch_shapes=_exchange_sems(n),
        input_output_aliases={i: i for i, kind in enumerate(kinds) if kind == "forward"},
    )(*arrays)


def _pair_sum(name, mine, theirs, my_core, tr):
    _, rws, cls = mine.shape
    tr = min(tr, rws)

    def body(a_ref, b_ref, o_ref):
        o_ref[...] = (a_ref[...].astype(F32) + b_ref[...].astype(F32)).astype(o_ref.dtype)

    return _call(name, body, (N_DEV // 2, rws // tr),
                 [(mine, (None, tr, cls), lambda k, r, core: (2 * k + core[0], r, 0)),
                  (theirs, (None, tr, cls), lambda k, r, core: (k, r, 0))],
                 [((N_DEV // 2, rws, cls), mine.dtype, (None, tr, cls), lambda k, r, core: (k, r, 0))],
                 prefetch=my_core)[0]


def _matmul(name, a, b, kind, tm, tn, out_dtype, blocked_out=False):
    if kind == "tn":
        kdim, m = a.shape
    else:
        m, kdim = a.shape
    n = b.shape[0] if kind == "nt" else b.shape[1]
    tm, tn = min(tm, m), min(tn, n)
    dims = {"nn": _NN, "nt": _NT, "tn": _TN}[kind]

    def body(a_ref, b_ref, o_ref):
        o_ref[...] = _dot(a_ref[...], b_ref[...], dims).astype(o_ref.dtype)

    a_spec = (a, (kdim, tm), lambda j, i: (0, i)) if kind == "tn" else (a, (tm, kdim), lambda j, i: (i, 0))
    b_spec = (b, (tn, kdim), lambda j, i: (j, 0)) if kind == "nt" else (b, (kdim, tn), lambda j, i: (0, j))
    if blocked_out:
        out = ((n // tn, m, tn), out_dtype, (None, tm, tn), lambda j, i: (j, i, 0))
    else:
        out = ((m, n), out_dtype, (tm, tn), lambda j, i: (i, j))
    return _call(name, body, (n // tn, m // tm), [a_spec, b_spec], [out])[0]


def _ada_fwd(cs_all, ada_w, ada_b_cols):
    def body(c_ref, w_ref, b_ref, o_ref):
        o_ref[...] = lax.dot_general(c_ref[...], w_ref[...], _NN, preferred_element_type=F32,
                                     precision=lax.Precision.HIGHEST) + b_ref[...]

    r, d = cs_all.shape
    nc = ada_w.shape[1]
    return _call("ada_fwd", body, (1,),
                 [(cs_all, (r, d), lambda i: (0, 0)), (ada_w, (d, nc), lambda i: (0, 0)),
                  (ada_b_cols, (1, nc), lambda i: (0, 0))],
                 [((r, nc), F32, (r, nc), lambda i: (0, 0))])[0]


def _silu_rows(c_all):
    def body(c_ref, o_ref):
        v = c_ref[...]
        o_ref[...] = v * _sigmoid(v)

    return _call("silu_c", body, (1,), [(c_all, c_all.shape, lambda i: (0, 0))],
                 [(c_all.shape, F32, c_all.shape, lambda i: (0, 0))])[0]


def _pre_norm(x, g, mod, tm, riders=None):
    s, d = x.shape

    def body(x_ref, g_ref, mod_ref, h_ref):
        n, _ = _rms(x_ref[...], d)
        sh, sc = mod_ref[:, 0:d], mod_ref[:, d:2 * d]
        h_ref[...] = (n * g_ref[...] * (1.0 + sc) + sh).astype(BF16)

    return _call("pre_norm", body, (s // tm,),
                 [(x, (tm, d), lambda i: (i, 0)), (g, (1, d), lambda i: (0, 0)),
                  (mod, (1, 6 * d), lambda i: (0, 0))],
                 [((s, d), BF16, (tm, d), lambda i: (i, 0))], riders=riders)


LOG2E = 1.4426950408889634
LN2 = 0.6931471805599453


def _decay_scale(lg_ref, idx, g, sign):
    return jnp.exp((sign * idx) * lg_ref[:, g * LANES:(g + 1) * LANES])


def _prep(proj, pos_col, idx_col, inv_freq, lg_lanes, tm):
    s = proj.shape[0]
    sb_off = (2 * RET_QK + 2 * RET_V) // (3 * SB_W)
    n_q = RET_QK // LANES

    def body(qk_ref, v_ref, sb_ref, pos_ref, idx_ref, f_ref, lg_ref, qk_out, v_out, sb_out, cos_out, sin_out):
        ang = pos_ref[...] * f_ref[...]
        lane = lax.broadcasted_iota(jnp.int32, (1, LANES), 1)
        first = jnp.bitwise_and(lane, RET_DQK - 1) < (RET_DQK // 2)
        cos = jnp.cos(ang)
        sin = jnp.where(first, -1.0, 1.0) * jnp.sin(ang)
        cos_out[...] = cos
        sin_out[...] = sin
        idx = idx_ref[...]
        for g in range(2 * n_q):
            v = qk_ref[:, g * LANES:(g + 1) * LANES].astype(F32)
            sw = jnp.where(first, pltpu.roll(v, LANES - RET_DQK // 2, 1), pltpu.roll(v, RET_DQK // 2, 1))
            r = v * cos + sw * sin
            if g < n_q:
                r = r * _decay_scale(lg_ref, idx, g, 1.0)
            else:
                r = r * (_decay_scale(lg_ref, idx, g - n_q, -1.0) * (RET_DQK ** -0.5))
            qk_out[:, g * LANES:(g + 1) * LANES] = r.astype(BF16)
        v_out[...] = v_ref[...].astype(BF16)
        sb_out[:, 0:SB_W] = (sb_ref[:, 0:SB_W].astype(F32) * (SB_DH ** -0.5 * LOG2E)).astype(BF16)
        sb_out[:, SB_W:3 * SB_W] = sb_ref[:, SB_W:3 * SB_W].astype(BF16)

    return _call("prep", body, (s // tm,),
                 [(proj, (tm, 2 * RET_QK), lambda i: (i, 0)),
                  (proj, (tm, RET_V), lambda i: (i, 2 * RET_QK // RET_V)),
                  (proj, (tm, 3 * SB_W), lambda i: (i, sb_off)),
                  (pos_col, (tm, 1), lambda i: (i, 0)),
                  (idx_col, (tm, 1), lambda i: (i, 0)),
                  (inv_freq, (1, LANES), lambda i: (0, 0)),
                  (lg_lanes, (1, RET_QK), lambda i: (0, 0))],
                 [((s, 2 * RET_QK), BF16, (tm, 2 * RET_QK), lambda i: (i, 0)),
                  ((s, RET_V), BF16, (tm, RET_V), lambda i: (i, 0)),
                  ((s, 3 * SB_W), BF16, (tm, 3 * SB_W), lambda i: (i, 0)),
                  ((s, LANES), F32, (tm, LANES), lambda i: (i, 0)),
                  ((s, LANES), F32, (tm, LANES), lambda i: (i, 0))])


def _head_mask(hh):
    lane = lax.broadcasted_iota(jnp.int32, (1, LANES), 1)
    return (lane >= RET_DQK) if hh else (lane < RET_DQK)


def _masked(v, m):
    return jnp.where(m, v, jnp.zeros_like(v))


SB_GROUP = 4
RET_GROUP = 4


def _stack_heads(v):
    return jnp.concatenate([_masked(v, _head_mask(0)), _masked(v, _head_mask(1))], axis=0)


def _side_by_side(v, t):
    return jnp.concatenate([v[:t], v[t:]], axis=1)


def _split_bf16(v):
    hi = v.astype(BF16)
    lo = (v - hi.astype(F32)).astype(BF16)
    return jnp.concatenate([hi, lo], axis=1)


def _tile_pos(i, j, tq, tk):
    row = jnp.bitwise_and(lax.broadcasted_iota(jnp.int32, (2 * tq, tk), 0), tq - 1) + i * tq
    col = lax.broadcasted_iota(jnp.int32, (2 * tq, tk), 1) + j * tk
    return row, col


def _n_groups(i, tq, tk, grp):
    return ((i + 1) * (tq // tk) + grp - 1) // grp


def _n_full(i, tq, tk, grp):
    return (i * (tq // tk)) // grp


def _key_rows(j, tk):
    return pl.ds(pl.multiple_of(j * tk, tk), tk)


def _ret_weight(lg_rows, i, j, tq, tk):
    row, col = _tile_pos(i, j, tq, tk)
    same = jnp.right_shift(col, CHUNK_SHIFT) == jnp.right_shift(row, CHUNK_SHIFT)
    later = jnp.where(same, jnp.exp((2.0 * lg_rows) * (col - row).astype(F32)), 0.0)
    return jnp.where(col <= row, 1.0, later)


def _lg_rows(lg_ref, hp, tq):
    first = lax.broadcasted_iota(jnp.int32, (2 * tq, 1), 0) < tq
    return jnp.where(first, lg_ref[2 * hp], lg_ref[2 * hp + 1])


def _check_tiles(s, tq, tk, grp):
    assert tq % tk == 0 and tq & (tq - 1) == 0 and tk & (tk - 1) == 0
    assert s % tq == 0 and (s // tk) % grp == 0 and s // tk <= LANES


def _ret_fwd(qk_rot, v_bf, proj, gn_g, log_gamma, tq, tk, riders=None):
    s = qk_rot.shape[0]
    gate_off = (2 * RET_QK + RET_V) // (2 * RET_DV)
    n_pair = HEADS // 2
    _check_tiles(s, tq, tk, RET_GROUP)

    def body(lg_ref, q_ref, k_ref, v_ref, g_ref, w_ref, ret_ref, rg_ref):
        hp, i = pl.program_id(0), pl.program_id(1)
        qs = _stack_heads(q_ref[...])
        lg_rows = _lg_rows(lg_ref, hp, tq)

        def make_step(near_diagonal):
            def step(g, carry):
                o0, o1 = carry
                js = [g * RET_GROUP + sub for sub in range(RET_GROUP)]
                rows = [_key_rows(j, tk) for j in js]
                ss = [_dot(qs, k_ref[rw, :], _NT) for rw in rows]
                if near_diagonal:
                    ss = [sc * _ret_weight(lg_rows, i, j, tq, tk) for sc, j in zip(ss, js)]
                for sc, rw in zip(ss, rows):
                    p = sc.astype(BF16)
                    o0 = o0 + _dot(p[:tq], v_ref[rw, 0:RET_DV])
                    o1 = o1 + _dot(p[tq:], v_ref[rw, RET_DV:2 * RET_DV])
                return o0, o1
            return step

        zero = jnp.zeros((tq, RET_DV), F32)
        n_full = _n_full(i, tq, tk, RET_GROUP)
        outs = lax.fori_loop(0, n_full, make_step(False), (zero, zero))
        outs = lax.fori_loop(n_full, _n_groups(i, tq, tk, RET_GROUP), make_step(True), outs)
        for hh, o in enumerate(outs):
            cols = slice(hh * RET_DV, (hh + 1) * RET_DV)
            ret_ref[:, cols] = o
            mu = jnp.sum(o, axis=1, keepdims=True) * (1.0 / RET_DV)
            xc = o - mu
            var = jnp.sum(xc * xc, axis=1, keepdims=True) * (1.0 / RET_DV)
            nrm = xc * lax.rsqrt(var + EPS) * w_ref[:, cols]
            g = g_ref[:, cols].astype(F32)
            rg_ref[:, cols] = (g * _sigmoid(g) * nrm).astype(BF16)

    pw = 2 * RET_DV
    blk = lambda hp, i: (i, hp)
    return _call("ret_fwd", body, (n_pair, s // tq),
                 [(log_gamma, None, pltpu.SMEM),
                  (qk_rot, (tq, LANES), blk),
                  (qk_rot, (s, LANES), lambda hp, i: (0, n_pair + hp)),
                  (v_bf, (s, pw), lambda hp, i: (0, hp)),
                  (proj, (tq, pw), lambda hp, i: (i, gate_off + hp)),
                  (gn_g, (1, pw), lambda hp, i: (0, hp))],
                 [((s, RET_V), F32, (tq, pw), blk), ((s, RET_V), BF16, (tq, pw), blk)], riders=riders)


def _tri2(tk, strict_upper):
    r = jnp.bitwise_and(lax.broadcasted_iota(jnp.int32, (2 * tk, tk), 0), tk - 1)
    cc = lax.broadcasted_iota(jnp.int32, (2 * tk, tk), 1)
    return ((r > cc) if strict_upper else (r < cc)).astype(BF16)


def _sb_valid(i, j, tq, tk):
    row, col = _tile_pos(i, j, tq, tk)
    return col < row


def _sb_fwd(qkv, tq, tk, riders=None):
    s = qkv.shape[0]
    n_pair = HEADS // 2
    _check_tiles(s, tq, tk, SB_GROUP)

    def body(q_ref, k_ref, v_ref, o_ref, a_ref):
        i = pl.program_id(1)
        upper2 = _tri2(tk, True)
        qs = _stack_heads(q_ref[...])
        n_full, n_groups = _n_full(i, tq, tk, SB_GROUP), _n_groups(i, tq, tk, SB_GROUP)

        def make_step(near_diagonal, last):
            def step(n, carry):
                c, o = carry
                g = last - 1 - n
                js = [g * SB_GROUP + sub for sub in range(SB_GROUP)]
                zs = [_dot(qs, k_ref[_key_rows(j, tk), :], _NT) for j in js]
                log1ps = [jnp.log2(1.0 + jnp.exp2(-jnp.abs(z))) for z in zs]
                log_1ms = [-jnp.maximum(z, 0.0) - t for z, t in zip(zs, log1ps)]
                log_bs = [jnp.minimum(z, 0.0) - t for z, t in zip(zs, log1ps)]
                if near_diagonal:
                    valids = [_sb_valid(i, j, tq, tk) for j in js]
                    log_1ms = [jnp.where(v, l, 0.0) for v, l in zip(valids, log_1ms)]
                sticks = [lax.dot_general(_split_bf16(l), upper2, _NN, preferred_element_type=F32) for l in log_1ms]
                sums = [jnp.sum(l, axis=1, keepdims=True) for l in log_1ms]
                cs = [None] * SB_GROUP
                for sub in reversed(range(SB_GROUP)):
                    cs[sub] = c
                    c = c + sums[sub]
                for sub, j in enumerate(js):
                    a = jnp.exp2(log_bs[sub] + sticks[sub] + cs[sub])
                    if near_diagonal:
                        a = jnp.where(valids[sub], a, 0.0)
                    a = a.astype(BF16)
                    a_ref[j] = a
                    o = o + _dot(_side_by_side(a, tq), _stack_heads(v_ref[_key_rows(j, tk), :]))
                return c, o
            return step

        carry = (jnp.zeros((2 * tq, 1), F32), jnp.zeros((tq, LANES), F32))
        carry = lax.fori_loop(0, n_groups - n_full, make_step(True, n_groups), carry)
        _, acc = lax.fori_loop(0, n_full, make_step(False, n_full), carry)
        o_ref[...] = acc

    n_kb = s // tk
    return _call("sb_fwd", body, (n_pair, s // tq),
                 [(qkv, (tq, LANES), lambda hp, i: (i, hp)),
                  (qkv, (s, LANES), lambda hp, i: (0, n_pair + hp)),
                  (qkv, (s, LANES), lambda hp, i: (0, 2 * n_pair + hp))],
                 [((s, SB_W), F32, (tq, LANES), lambda hp, i: (i, hp)),
                  ((n_pair, s // tq, n_kb, 2 * tq, tk), BF16, (None, None, n_kb, 2 * tq, tk),
                   lambda hp, i: (hp, i, 0, 0, 0))], riders=riders)


def _merge(retg, sb, w_ret, w_sb, proj, tm, tn, riders=None):
    s, d = retg.shape[0], w_ret.shape[1]
    ar_off = (2 * RET_QK + 2 * RET_V + 3 * SB_W) // tn
    as_off = ar_off + d // tn

    def body(rg_ref, sb_ref, wr_ref, ws_ref, ar_ref, as_ref, mix_ref, r_ref, s_ref):
        rr = _dot(rg_ref[...], wr_ref[...])
        ss = _dot(sb_ref[...], ws_ref[...])
        mix_ref[...] = (_sigmoid(ar_ref[...].astype(F32)) * rr + _sigmoid(as_ref[...].astype(F32)) * ss).astype(BF16)
        r_ref[...] = rr.astype(BF16)
        s_ref[...] = ss.astype(BF16)

    tile = (tm, tn)
    return _call("merge", body, (d // tn, s // tm),
                 [(retg, (tm, RET_V), lambda j, i: (i, 0)), (sb, (tm, SB_W), lambda j, i: (i, 0)),
                  (w_ret, (RET_V, tn), lambda j, i: (0, j)), (w_sb, (SB_W, tn), lambda j, i: (0, j)),
                  (proj, tile, lambda j, i: (i, ar_off + j)), (proj, tile, lambda j, i: (i, as_off + j))],
                 [((s, d), BF16, tile, lambda j, i: (i, j))] * 3, riders=riders)


def _out_proj(mixed, w_out, x, mod, gp1, g2, tm):
    s, d = x.shape

    def body(a_ref, w_ref, x_ref, mod_ref, gp_ref, g2_ref, y_ref, hres_ref, h2_ref):
        y = _dot(a_ref[...], w_ref[...])
        y_ref[...] = y
        ny, _ = _rms(y, d)
        hres = x_ref[...] + mod_ref[:, 2 * d:3 * d] * (ny * gp_ref[...])
        hres_ref[...] = hres
        n2, _ = _rms(hres, d)
        h2_ref[...] = (n2 * g2_ref[...] * (1.0 + mod_ref[:, 4 * d:5 * d]) + mod_ref[:, 3 * d:4 * d]).astype(BF16)

    row = lambda i: (i, 0)
    fix = lambda i: (0, 0)
    return _call("out_proj", body, (s // tm,),
                 [(mixed, (tm, d), row), (w_out, (d, d), fix), (x, (tm, d), row),
                  (mod, (1, 6 * d), fix), (gp1, (1, d), fix), (g2, (1, d), fix)],
                 [((s, d), F32, (tm, d), row), ((s, d), F32, (tm, d), row), ((s, d), BF16, (tm, d), row)])


def _ff1(h2, w_ff1, tm, tn):
    s, f = h2.shape[0], w_ff1.shape[1]
    tm = min(tm, s)

    def body(a_ref, w_ref, u_ref, act_ref):
        u = _dot(a_ref[...], w_ref[...])
        r = jnp.maximum(u, 0.0)
        u_ref[...] = u.astype(BF16)
        act_ref[...] = (r * r).astype(BF16)

    d = h2.shape[1]
    return _call("ff1", body, (f // tn, s // tm),
                 [(h2, (tm, d), lambda j, i: (i, 0)), (w_ff1, (d, tn), lambda j, i: (0, j))],
                 [((s, f), BF16, (tm, tn), lambda j, i: (i, j))] * 2)


def _ff2_loss(act, w_ff2, hres, target, mod, gp2, tm):
    s, d = hres.shape
    f = act.shape[1]

    def body(a_ref, w_ref, h_ref, t_ref, mod_ref, gp_ref, dout_ref, df_ref, loss_ref, dgt_ref, dgp_ref):
        first = pl.program_id(0) == 0
        ff = _dot(a_ref[...], w_ref[...])
        nf, rf = _rms(ff, d)
        gt, gp = mod_ref[:, 5 * d:6 * d], gp_ref[...]
        out = h_ref[...] + gt * (nf * gp)
        err = out - t_ref[...]
        sq = jnp.sum(err * err, axis=1, keepdims=True)
        _accum(loss_ref, jnp.sum(sq, axis=0, keepdims=True), first)
        dout = err * (1.0 / d)
        dout_ref[...] = dout
        _accum(dgt_ref, _colsum(dout * (nf * gp)), first)
        _accum(dgp_ref, _colsum(dout * gt * nf), first)
        df_ref[...] = _rms_bwd(dout * gt * gp, nf, rf, d).astype(BF16)

    row = lambda i: (i, 0)
    fix = lambda i: (0, 0)
    return _call("ff2_loss", body, (s // tm,),
                 [(act, (tm, f), row), (w_ff2, (f, d), fix), (hres, (tm, d), row), (target, (tm, d), row),
                  (mod, (1, 6 * d), fix), (gp2, (1, d), fix)],
                 [((s, d), F32, (tm, d), row), ((s, d), BF16, (tm, d), row), ((1, 1), F32, (1, 1), fix),
                  ((1, d), F32, (1, d), fix), ((1, d), F32, (1, d), fix)])


def _ff2_bwd(df, w_ff2, u, tm, tn):
    s, d = df.shape
    f = w_ff2.shape[0]
    tm = min(tm, s)

    def body(a_ref, w_ref, u_ref, du_ref):
        da = _dot(a_ref[...], w_ref[...], _NT)
        du_ref[...] = (da * (2.0 * jnp.maximum(u_ref[...].astype(F32), 0.0))).astype(BF16)

    return _call("ff2_bwd", body, (f // tn, s // tm),
                 [(df, (tm, d), lambda j, i: (i, 0)), (w_ff2, (tn, d), lambda j, i: (j, 0)),
                  (u, (tm, tn), lambda j, i: (i, j))],
                 [((s, f), BF16, (tm, tn), lambda j, i: (i, j))])[0]


def _ff1_bwd(du, w_ff1, hres, dout, y, mod, g2, gp1, tm, riders=None):
    s, d = hres.shape
    f = du.shape[1]

    def body(a_ref, w_ref, h_ref, do_ref, y_ref, mod_ref, g2_ref, gp_ref,
             dh_ref, dy_ref, dsh_ref, dsc_ref, dg2_ref, dgt_ref, dgp_ref):
        first = pl.program_id(0) == 0
        dh2 = _dot(a_ref[...], w_ref[...], _NT)
        n2, r2 = _rms(h_ref[...], d)
        g2, sc2 = g2_ref[...], mod_ref[:, 4 * d:5 * d]
        _accum(dsh_ref, _colsum(dh2), first)
        _accum(dsc_ref, _colsum(dh2 * n2 * g2), first)
        _accum(dg2_ref, _colsum(dh2 * n2 * (1.0 + sc2)), first)
        dhres = do_ref[...] + _rms_bwd(dh2 * g2 * (1.0 + sc2), n2, r2, d)
        dh_ref[...] = dhres
        ny, ry = _rms(y_ref[...], d)
        gt, gp = mod_ref[:, 2 * d:3 * d], gp_ref[...]
        _accum(dgt_ref, _colsum(dhres * (ny * gp)), first)
        _accum(dgp_ref, _colsum(dhres * gt * ny), first)
        dy_ref[...] = _rms_bwd(dhres * gt * gp, ny, ry, d).astype(BF16)

    row = lambda i: (i, 0)
    fix = lambda i: (0, 0)
    vec = ((1, d), F32, (1, d), fix)
    return _call("ff1_bwd", body, (s // tm,),
                 [(du, (tm, f), row), (w_ff1, (d, f), fix), (hres, (tm, d), row), (dout, (tm, d), row),
                  (y, (tm, d), row), (mod, (1, 6 * d), fix), (g2, (1, d), fix), (gp1, (1, d), fix)],
                 [((s, d), F32, (tm, d), row), ((s, d), BF16, (tm, d), row), vec, vec, vec, vec, vec], riders=riders)


def _out_bwd(dy, w_out, proj, r_bf, s_bf, tm, tn):
    s, d = dy.shape
    ar_off = (2 * RET_QK + 2 * RET_V + 3 * SB_W) // tn
    as_off = ar_off + d // tn

    def body(a_ref, w_ref, ar_ref, as_ref, r_ref, s_ref, dr_ref, ds_ref, dar_ref, das_ref):
        dm = _dot(a_ref[...], w_ref[...], _NT)
        sr, ss = _sigmoid(ar_ref[...].astype(F32)), _sigmoid(as_ref[...].astype(F32))
        dr_ref[...] = (dm * sr).astype(BF16)
        ds_ref[...] = (dm * ss).astype(BF16)
        dar_ref[...] = (dm * r_ref[...].astype(F32) * sr * (1.0 - sr)).astype(BF16)
        das_ref[...] = (dm * s_ref[...].astype(F32) * ss * (1.0 - ss)).astype(BF16)

    tile = (tm, tn)
    here = lambda j, i: (i, j)
    return _call("out_bwd", body, (d // tn, s // tm),
                 [(dy, (tm, d), lambda j, i: (i, 0)), (w_out, (tn, d), lambda j, i: (j, 0)),
                  (proj, tile, lambda j, i: (i, ar_off + j)), (proj, tile, lambda j, i: (i, as_off + j)),
                  (r_bf, tile, here), (s_bf, tile, here)],
                 [((s, d), BF16, tile, here)] * 4)


def _gn_bwd(dretg, ret, proj, gn_g, tm, riders=None):
    s = ret.shape[0]
    gate_off = (2 * RET_QK + RET_V) // RET_V

    def body(d_ref, r_ref, g_ref, w_ref, dg_ref, dret_ref, dw_ref):
        first = pl.program_id(0) == 0
        for h in range(HEADS):
            cols = slice(h * RET_DV, (h + 1) * RET_DV)
            o, g, w, dr = r_ref[:, cols], g_ref[:, cols].astype(F32), w_ref[:, cols], d_ref[:, cols].astype(F32)
            mu = jnp.sum(o, axis=1, keepdims=True) * (1.0 / RET_DV)
            xc = o - mu
            rstd = lax.rsqrt(jnp.sum(xc * xc, axis=1, keepdims=True) * (1.0 / RET_DV) + EPS)
            n = xc * rstd
            sg = _sigmoid(g)
            silu = g * sg
            dg_ref[:, cols] = (dr * n * w * (sg * (1.0 + g * (1.0 - sg)))).astype(BF16)
            _accum(dw_ref.at[:, cols], _colsum(dr * silu * n), first)
            dn = dr * silu * w
            m1 = jnp.sum(dn, axis=1, keepdims=True) * (1.0 / RET_DV)
            m2 = jnp.sum(dn * n, axis=1, keepdims=True) * (1.0 / RET_DV)
            dret_ref[:, cols] = (rstd * (dn - m1 - n * m2)).astype(BF16)

    row = lambda i: (i, 0)
    fix = lambda i: (0, 0)
    return _call("gn_bwd", body, (s // tm,),
                 [(dretg, (tm, RET_V), row), (ret, (tm, RET_V), row),
                  (proj, (tm, RET_V), lambda i: (i, gate_off)), (gn_g, (1, RET_V), fix)],
                 [((s, RET_V), BF16, (tm, RET_V), row), ((s, RET_V), BF16, (tm, RET_V), row),
                  ((1, RET_V), F32, (1, RET_V), fix)], riders=riders)


def _ret_bwd(qk_rot, v_bf, dret, log_gamma, tq, tk, riders=None):
    s = qk_rot.shape[0]
    n_pair = HEADS // 2
    pw = 2 * RET_DV
    _check_tiles(s, tq, tk, RET_GROUP)

    def body(lg_ref, q_ref, k_ref, v_ref, do_ref, dq_ref, dk_ref, dv_ref):
        hp, i = pl.program_id(0), pl.program_id(1)

        @pl.when(i == 0)
        def _():
            dk_ref[...] = jnp.zeros_like(dk_ref)
            dv_ref[...] = jnp.zeros_like(dv_ref)

        qs = _stack_heads(q_ref[...])
        lg_rows = _lg_rows(lg_ref, hp, tq)
        do0, do1 = do_ref[:, 0:RET_DV], do_ref[:, RET_DV:pw]

        def make_step(near_diagonal):
            def step(g, dq):
                js = [g * RET_GROUP + sub for sub in range(RET_GROUP)]
                rows = [_key_rows(j, tk) for j in js]
                ss = [_dot(qs, k_ref[rw, :], _NT) for rw in rows]
                dps = [jnp.concatenate([_dot(do0, v_ref[rw, 0:RET_DV], _NT), _dot(do1, v_ref[rw, RET_DV:pw], _NT)],
                                       axis=0) for rw in rows]
                if near_diagonal:
                    ws = [_ret_weight(lg_rows, i, j, tq, tk) for j in js]
                    ss = [sc * w for sc, w in zip(ss, ws)]
                    dps = [dp * w for dp, w in zip(dps, ws)]
                for sc, dp, rw in zip(ss, dps, rows):
                    p, ds = sc.astype(BF16), dp.astype(BF16)
                    dv_ref[rw, 0:RET_DV] += _dot(p[:tq], do0, _TN)
                    dv_ref[rw, RET_DV:pw] += _dot(p[tq:], do1, _TN)
                    dk_ref[rw, :] += _dot(ds, qs, _TN)
                    dq = dq + _dot(_side_by_side(ds, tq), _stack_heads(k_ref[rw, :]))
                return dq
            return step

        n_full = _n_full(i, tq, tk, RET_GROUP)
        dq = lax.fori_loop(0, n_full, make_step(False), jnp.zeros((tq, LANES), F32))
        dq_ref[...] = lax.fori_loop(n_full, _n_groups(i, tq, tk, RET_GROUP), make_step(True), dq)

    blk = lambda hp, i: (i, hp)
    return _call("ret_bwd", body, (n_pair, s // tq),
                 [(log_gamma, None, pltpu.SMEM),
                  (qk_rot, (tq, LANES), blk),
                  (qk_rot, (s, LANES), lambda hp, i: (0, n_pair + hp)),
                  (v_bf, (s, pw), lambda hp, i: (0, hp)),
                  (dret, (tq, pw), blk)],
                 [((s, RET_QK), F32, (tq, LANES), blk),
                  ((s, RET_QK), F32, (s, LANES), lambda hp, i: (0, hp)),
                  ((s, RET_V), F32, (s, pw), lambda hp, i: (0, hp))], riders=riders)


def _sb_bwd(qkv, weights, do, tq, tk, riders=None):
    s = qkv.shape[0]
    n_pair = HEADS // 2
    _check_tiles(s, tq, tk, SB_GROUP)

    def body(q_ref, k_ref, v_ref, a_ref, do_ref, dq_ref, dk_ref, dv_ref):
        i = pl.program_id(1)

        @pl.when(i == 0)
        def _():
            dk_ref[...] = jnp.zeros_like(dk_ref)
            dv_ref[...] = jnp.zeros_like(dv_ref)

        lower2 = _tri2(tk, False)
        qs = _stack_heads(q_ref[...])
        dos = _stack_heads(do_ref[...].astype(BF16))

        def make_step(near_diagonal):
            def step(g, carry):
                c_e, dq = carry
                js = [g * SB_GROUP + sub for sub in range(SB_GROUP)]
                rows = [_key_rows(j, tk) for j in js]
                zs = [_dot(qs, k_ref[rw, :], _NT) for rw in rows]
                das = [_dot(dos, v_ref[rw, :], _NT) for rw in rows]
                avals = [a_ref[j] for j in js]
                for a, rw in zip(avals, rows):
                    dv_ref[rw, :] += _dot(a, dos, _TN)
                es = [a.astype(F32) * da for a, da in zip(avals, das)]
                prefixes = [lax.dot_general(_split_bf16(e), lower2, _NN, preferred_element_type=F32) for e in es]
                betas = [1.0 / (1.0 + jnp.exp2(-z)) for z in zs]
                for sub in range(SB_GROUP):
                    dz = es[sub] - (es[sub] + prefixes[sub] + c_e) * betas[sub]
                    if near_diagonal:
                        dz = jnp.where(_sb_valid(i, js[sub], tq, tk), dz, 0.0)
                    dz = dz.astype(BF16)
                    dk_ref[rows[sub], :] += _dot(dz, qs, _TN)
                    dq = dq + _dot(_side_by_side(dz, tq), _stack_heads(k_ref[rows[sub], :]))
                    c_e = c_e + jnp.sum(es[sub], axis=1, keepdims=True)
                return c_e, dq
            return step

        n_full = _n_full(i, tq, tk, SB_GROUP)
        carry = (jnp.zeros((2 * tq, 1), F32), jnp.zeros((tq, LANES), F32))
        carry = lax.fori_loop(0, n_full, make_step(False), carry)
        _, dq = lax.fori_loop(n_full, _n_groups(i, tq, tk, SB_GROUP), make_step(True), carry)
        dq_ref[...] = dq

    blk = lambda hp, i: (i, hp)
    n_kb = s // tk
    return _call("sb_bwd", body, (n_pair, s // tq),
                 [(qkv, (tq, LANES), blk),
                  (qkv, (s, LANES), lambda hp, i: (0, n_pair + hp)),
                  (qkv, (s, LANES), lambda hp, i: (0, 2 * n_pair + hp)),
                  (weights, (None, None, n_kb, 2 * tq, tk), lambda hp, i: (hp, i, 0, 0, 0)),
                  (do, (tq, LANES), blk)],
                 [((s, SB_W), F32, (tq, LANES), blk),
                  ((s, SB_W), F32, (s, LANES), lambda hp, i: (0, hp)),
                  ((s, SB_W), F32, (s, LANES), lambda hp, i: (0, hp))], riders=riders)


def _assemble_dproj(dq_r, dk_r, dv_r, dg_r, dq_s, dk_s, dv_s, da_r, da_s, cos, sin, idx_col, lg_lanes, tm):
    s, d = da_r.shape
    width = 2 * RET_QK + 2 * RET_V + 3 * SB_W + 2 * d

    def body(dq_ref, dk_ref, dv_ref, dg_ref, dqs_ref, dks_ref, dvs_ref, dar_ref, das_ref, cos_ref, sin_ref,
             idx_ref, lg_ref, o_ref):
        lane = lax.broadcasted_iota(jnp.int32, (1, LANES), 1)
        first = jnp.bitwise_and(lane, RET_DQK - 1) < (RET_DQK // 2)
        cos, sin = cos_ref[...], sin_ref[...]
        idx = idx_ref[...]
        for src, base, sign, scale in ((dq_ref, 0, 1.0, 1.0), (dk_ref, RET_QK, -1.0, RET_DQK ** -0.5)):
            for g in range(RET_QK // LANES):
                v = src[:, g * LANES:(g + 1) * LANES] * (_decay_scale(lg_ref, idx, g, sign) * scale)
                sw = jnp.where(first, pltpu.roll(v, LANES - RET_DQK // 2, 1), pltpu.roll(v, RET_DQK // 2, 1))
                o_ref[:, base + g * LANES:base + (g + 1) * LANES] = (v * cos - sw * sin).astype(BF16)
        off = 2 * RET_QK
        o_ref[:, off:off + RET_V] = dv_ref[...].astype(BF16)
        off += RET_V
        o_ref[:, off:off + RET_V] = dg_ref[...]
        off += RET_V
        o_ref[:, off:off + SB_W] = (dqs_ref[...] * (SB_DH ** -0.5)).astype(BF16)
        off += SB_W
        o_ref[:, off:off + SB_W] = (dks_ref[...] * LN2).astype(BF16)
        off += SB_W
        o_ref[:, off:off + SB_W] = dvs_ref[...].astype(BF16)
        off += SB_W
        o_ref[:, off:off + d] = dar_ref[...]
        off += d
        o_ref[:, off:off + d] = das_ref[...]

    row = lambda i: (i, 0)
    ins = [(a, (tm, a.shape[1]), row) for a in (dq_r, dk_r, dv_r, dg_r, dq_s, dk_s, dv_s, da_r, da_s, cos, sin, idx_col)]
    ins.append((lg_lanes, (1, RET_QK), lambda i: (0, 0)))
    return _call("assemble_dproj", body, (s // tm,), ins, [((s, width), BF16, (tm, width), row)])[0]


def _in_bwd(dproj, w_in_t, x, dhres, mod, g1, tm, riders=None):
    s, d = x.shape
    width = dproj.shape[1]

    def body(a_ref, w_ref, x_ref, dh_ref, mod_ref, g_ref, dx_ref, dsh_ref, dsc_ref, dg_ref):
        first = pl.program_id(0) == 0
        dh = _dot(a_ref[...], w_ref[...])
        n1, r1 = _rms(x_ref[...], d)
        g1, sc1 = g_ref[...], mod_ref[:, d:2 * d]
        _accum(dsh_ref, _colsum(dh), first)
        _accum(dsc_ref, _colsum(dh * n1 * g1), first)
        _accum(dg_ref, _colsum(dh * n1 * (1.0 + sc1)), first)
        dx_ref[...] = dh_ref[...] + _rms_bwd(dh * g1 * (1.0 + sc1), n1, r1, d)

    row = lambda i: (i, 0)
    fix = lambda i: (0, 0)
    vec = ((1, d), F32, (1, d), fix)
    return _call("in_bwd", body, (s // tm,),
                 [(dproj, (tm, width), row), (w_in_t, (width, d), fix), (x, (tm, d), row), (dhres, (tm, d), row),
                  (mod, (1, 6 * d), fix), (g1, (1, d), fix)],
                 [((s, d), F32, (tm, d), row), vec, vec, vec], riders=riders)


def _adamw(w, g, m, v):
    m = ADAM_B1 * m + (1.0 - ADAM_B1) * g
    v = ADAM_B2 * v + (1.0 - ADAM_B2) * (g * g)
    m_hat = m / (1.0 - ADAM_B1 ** ADAM_STEP)
    v_hat = v / (1.0 - ADAM_B2 ** ADAM_STEP)
    delta = -ADAM_LR * (m_hat / (jnp.sqrt(v_hat) + ADAM_EPS) + ADAM_WD * w)
    return delta, m, v


def _adam_reduce(name, parts, w, m, v, tr):
    rws, cls = w.shape
    tr = min(tr, rws)
    n_parts = parts.shape[0]

    def body(p_ref, w_ref, m_ref, v_ref, g_out, d_out, m_out, v_out):
        g = p_ref[0].astype(F32)
        for k in range(1, n_parts):
            g = g + p_ref[k].astype(F32)
        delta, mn, vn = _adamw(w_ref[...], g, m_ref[...], v_ref[...])
        g_out[...] = g
        d_out[...] = delta
        m_out[...] = mn
        v_out[...] = vn

    row = lambda i: (i, 0)
    blk = (tr, cls)
    return _call(name, body, (rws // tr,),
                 [(parts, (n_parts, tr, cls), lambda i: (0, i, 0)), (w, blk, row), (m, blk, row), (v, blk, row)],
                 [((rws, cls), F32, blk, row)] * 4)


def _ada_bwd_adam(cs_t, dmod_cols, w, m, v):
    d, nc = w.shape

    def body(c_ref, dm_ref, w_ref, m_ref, v_ref, g_out, d_out, m_out, v_out):
        g = c_ref[0] * dm_ref[0:1, :]
        for r in range(1, N_DEV):
            g = g + c_ref[r] * dm_ref[r:r + 1, :]
        delta, mn, vn = _adamw(w_ref[...], g, m_ref[...], v_ref[...])
        g_out[...] = g
        d_out[...] = delta
        m_out[...] = mn
        v_out[...] = vn

    fix = lambda i: (0, 0)
    blk = (d, nc)
    return _call("ada_bwd_adam", body, (1,),
                 [(cs_t, (N_DEV, d, 1), lambda i: (0, 0, 0)), (dmod_cols, (N_DEV, nc), fix), (w, blk, fix), (m, blk, fix), (v, blk, fix)],
                 [((d, nc), F32, blk, fix)] * 4)


def _small_adam(parts, w, m, v):
    n = w.shape[1]

    def body(p_ref, w_ref, m_ref, v_ref, g_out, d_out, m_out, v_out):
        g = p_ref[0:1, :]
        for k in range(1, N_DEV):
            g = g + p_ref[k:k + 1, :]
        delta, mn, vn = _adamw(w_ref[...], g, m_ref[...], v_ref[...])
        g_out[...] = g
        d_out[...] = delta
        m_out[...] = mn
        v_out[...] = vn

    fix = lambda i: (0, 0)
    return _call("small_adam", body, (1,),
                 [(parts, (N_DEV, n), fix), (w, (1, n), fix), (m, (1, n), fix), (v, (1, n), fix)],
                 [((1, n), F32, (1, n), fix)] * 4)


def kernel(x, c, positions, ada_w, ada_b, pre_mix_g, post_mix_g, pre_ffn_g, post_ffn_g, w_in, ret_gn_g, w_ret_branch, w_sb_branch, w_out, w_ff1, w_ff2, loss_target, m_ada_w, m_ada_b, m_pre_mix_g, m_post_mix_g, m_pre_ffn_g, m_post_ffn_g, m_w_in, m_ret_gn_g, m_w_ret_branch, m_w_sb_branch, m_w_out, m_w_ff1, m_w_ff2, v_ada_w, v_ada_b, v_pre_mix_g, v_post_mix_g, v_pre_ffn_g, v_post_ffn_g, v_w_in, v_ret_gn_g, v_w_ret_branch, v_w_sb_branch, v_w_out, v_w_ff1, v_w_ff2):
    _, s, d = x.shape
    d_ff = w_ff1.shape[2] * N_DEV
    d_in = w_in.shape[2] * N_DEV
    me = 4 * lax.axis_index("x") + 2 * lax.axis_index("y") + lax.axis_index("c")
    x2, tgt = x[0], loss_target[0]

    core = lax.axis_index("c").astype(jnp.int32).reshape(1)
    bf = lambda w: w[0].astype(BF16)

    w_in_t, m_in_t, v_in_t = (jnp.swapaxes(a[0], 0, 1) for a in (w_in, m_w_in, v_w_in))

    c_all, g_in = _exchange("gather_in", [c, w_in_t.astype(BF16)], ["gather", "gather_chip"])
    c_all = c_all.reshape(N_DEV, d)

    n_ada = ada_w.shape[2]
    cs_all = _silu_rows(c_all)
    ada_b_cols = lax.dynamic_slice(ada_b, (0, me * n_ada), (1, n_ada))
    mod_cols = _ada_fwd(cs_all, ada_w[0], ada_b_cols)
    mod_all = _exchange("gather_mod", [mod_cols], ["gather"])[0]
    mod = lax.dynamic_index_in_dim(mod_all, me, axis=1, keepdims=False).reshape(1, 6 * d)

    tm = min(256, s)
    h, g_in = _pre_norm(x2, pre_mix_g, mod, tm, riders=([g_in], ["forward"]))
    wt_in = g_in.reshape(d_in, d)
    proj = _matmul("in_proj", h, wt_in, "nt", s, 512, BF16)
    pos_col = positions.reshape(s, 1).astype(F32)
    freqs = ROPE_BASE ** (-jnp.arange(0, RET_DQK, 2, dtype=F32) / RET_DQK)
    inv_freq = jnp.tile(freqs, LANES // (RET_DQK // 2)).reshape(1, LANES)
    log_gamma_np = np.log1p(-(2.0 ** (-5.0 - np.arange(HEADS))))
    log_gamma = jnp.asarray(log_gamma_np, F32)
    lg_lanes = jnp.asarray(np.repeat(log_gamma_np, RET_DQK).reshape(1, RET_QK), F32)
    idx_col = (jnp.arange(s, dtype=F32) - (s // 2)).reshape(s, 1)
    qk_rot, v_bf, qkv_sb, cos_t, sin_t = _prep(proj, pos_col, idx_col, inv_freq, lg_lanes, tm)
    tq, tk = min(256, s), min(128, s)
    later = [bf(w_ret_branch), bf(w_sb_branch), bf(w_out), bf(w_ff1)]
    sb, sb_weights, *later = _sb_fwd(qkv_sb, tq, tk, riders=(later, ["gather_chip"] * 4))
    ret, retg, g_ret, g_sb, g_out, g_ff1, g_ff2 = _ret_fwd(
        qk_rot, v_bf, proj, ret_gn_g, log_gamma, tq, tk,
        riders=(later + [bf(w_ff2)], ["forward"] * 4 + ["gather_chip"]))
    wf_ret = g_ret.reshape(RET_V, d)
    wf_sb = jnp.moveaxis(g_sb, 0, 1).reshape(SB_W, d)
    wf_out = g_out.reshape(d, d)
    wf_ff1 = jnp.moveaxis(g_ff1, 0, 1).reshape(d, d_ff)
    mixed, r_bf, s_bf, g_ff2 = _merge(retg, sb, wf_ret, wf_sb, proj, tm, min(512, d), riders=([g_ff2], ["forward"]))
    wf_ff2 = g_ff2.reshape(d_ff, d)
    y, hres, h2 = _out_proj(mixed, wf_out, x2, mod, post_mix_g, pre_ffn_g, tm)
    u, act = _ff1(h2, wf_ff1, s, 512)
    dout, df, loss_sum, d_gt2, d_gp2 = _ff2_loss(act, wf_ff2, hres, tgt, mod, post_ffn_g, tm)

    du = _ff2_bwd(df, wf_ff2, u, s, 512)
    gw_ff2 = _matmul("grad_w_ff2", act, df, "tn", 512, d, BF16).reshape(N_DEV, d_ff // N_DEV, d)
    gw_ff1 = _matmul("grad_w_ff1", h2, du, "tn", d, d_ff // N_DEV, BF16, blocked_out=True)
    dhres, dy, d_sh2, d_sc2, d_g2, d_gt1, d_gp1, t_ff1, t_ff2 = _ff1_bwd(
        du, wf_ff1, hres, dout, y, mod, pre_ffn_g, post_mix_g, tm, riders=([gw_ff1, gw_ff2], ["pair"] * 2))
    s_ff1 = _pair_sum("pair_sum_ff1", gw_ff1, t_ff1, core, 256)
    s_ff2 = _pair_sum("pair_sum_ff2", gw_ff2, t_ff2, core, 256)
    d_r, d_s, da_r, da_s = _out_bwd(dy, wf_out, proj, r_bf, s_bf, tm, min(512, d))
    gw_out = _matmul("grad_w_out", mixed, dy, "tn", 512, d, BF16).reshape(N_DEV, d // N_DEV, d)
    dretg = _matmul("ret_branch_bwd", d_r, wf_ret, "nt", s, 512, BF16)
    dsb = _matmul("sb_branch_bwd", d_s, wf_sb, "nt", s, 512, F32)
    gw_ret = _matmul("grad_w_ret", retg, d_r, "tn", 512, d, BF16).reshape(N_DEV, RET_V // N_DEV, d)
    gw_sb = _matmul("grad_w_sb", sb, d_s, "tn", 512, d // N_DEV, BF16, blocked_out=True)
    dq_s, dk_s, dv_s, p_ff1, p_ff2 = _sb_bwd(qkv_sb, sb_weights, dsb, tq, tk,
                                             riders=([s_ff1, s_ff2], ["chip_scatter"] * 2))
    dg_r, dret, d_gn = _gn_bwd(dretg, ret, proj, ret_gn_g, tm)
    dq_r, dk_r, dv_r, p_out, p_ret, p_sb = _ret_bwd(qk_rot, v_bf, dret, log_gamma, tq, tk,
                                                    riders=([gw_out, gw_ret, gw_sb], ["scatter"] * 3))
    dproj = _assemble_dproj(dq_r, dk_r, dv_r, dg_r, dq_s, dk_s, dv_s, da_r, da_s, cos_t, sin_t, idx_col, lg_lanes, tm)
    gw_in = _matmul("grad_w_in", dproj, h, "tn", 512, d, BF16).reshape(N_DEV, d_in // N_DEV, d)
    t_in = _exchange("pair_in", [gw_in], ["pair"])[0]
    tr_in = d_in // N_DEV // 4
    s_in = _pair_sum("pair_sum_in", gw_in, t_in, core, tr_in)
    grad_x, d_sh1, d_sc1, d_g1, p_in = _in_bwd(dproj, wt_in, x2, dhres, mod, pre_mix_g, tm,
                                               riders=([s_in], ["chip_scatter"]))
    loss_lanes = jnp.pad(loss_sum, ((0, 0), (0, LANES - 1)))
    small = jnp.concatenate([d_sh1, d_sc1, d_gt1, d_sh2, d_sc2, d_gt2, d_g1, d_gp1, d_g2, d_gp2, d_gn, loss_lanes], axis=1)
    small_all = _exchange("gather_small", [small], ["gather"])[0].reshape(N_DEV, small.shape[1])
    parts = [p_in, p_ret, p_sb, p_out, p_ff1, p_ff2]

    res = {}
    names = ["w_ret_branch", "w_sb_branch", "w_out", "w_ff1", "w_ff2"]
    ws = [w_ret_branch, w_sb_branch, w_out, w_ff1, w_ff2]
    ms = [m_w_ret_branch, m_w_sb_branch, m_w_out, m_w_ff1, m_w_ff2]
    vs = [v_w_ret_branch, v_w_sb_branch, v_w_out, v_w_ff1, v_w_ff2]
    for nm, p, w, m, v in zip(names, parts[1:], ws, ms, vs):
        res[nm] = [o[None] for o in _adam_reduce("adam_" + nm, p, w[0], m[0], v[0], 256)]
    res["w_in"] = [jnp.swapaxes(o, 0, 1)[None]
                   for o in _adam_reduce("adam_w_in", parts[0], w_in_t, m_in_t, v_in_t, tr_in)]
    dmod_cols = lax.dynamic_slice(small_all, (0, me * n_ada), (N_DEV, n_ada))
    res["ada_w"] = [o[None] for o in _ada_bwd_adam(cs_all.reshape(N_DEV, d, 1), dmod_cols, ada_w[0], m_ada_w[0], v_ada_w[0])]
    vec_names = ["ada_b", "pre_mix_g", "post_mix_g", "pre_ffn_g", "post_ffn_g", "ret_gn_g"]
    cat = lambda xs: jnp.concatenate(xs + [jnp.zeros((1, LANES), F32)], axis=1)
    packed = _small_adam(small_all,
                         cat([ada_b, pre_mix_g, post_mix_g, pre_ffn_g, post_ffn_g, ret_gn_g]),
                         cat([m_ada_b, m_pre_mix_g, m_post_mix_g, m_pre_ffn_g, m_post_ffn_g, m_ret_gn_g]),
                         cat([v_ada_b, v_pre_mix_g, v_post_mix_g, v_pre_ffn_g, v_post_ffn_g, v_ret_gn_g]))
    off = 0
    for nm, width in zip(vec_names, [6 * d, d, d, d, d, RET_V]):
        res[nm] = [p[:, off:off + width] for p in packed]
        off += width

    loss = (0.5 / d) * packed[0][0, off]
    order = ["ada_w", "ada_b", "pre_mix_g", "post_mix_g", "pre_ffn_g", "post_ffn_g", "w_in", "ret_gn_g",
             "w_ret_branch", "w_sb_branch", "w_out", "w_ff1", "w_ff2"]
    outs = [loss, grad_x[None]]
    for k in range(4):
        outs += [res[nm][k] for nm in order]
    return tuple(outs)
```

```python
import functools

import numpy as np
import jax
import jax.numpy as jnp
from jax import lax
from jax.experimental import pallas as pl
from jax.experimental.pallas import tpu as pltpu

F32 = jnp.float32
BF16 = jnp.bfloat16
N_DEV = 8
AXES = ("x", "y", "c")

EPS = 1e-6
CHUNK = 64
CHUNK_SHIFT = 6
HEADS = 8
RET_DQK = 64
RET_DV = 128
SB_DH = 64
RET_QK = HEADS * RET_DQK
RET_V = HEADS * RET_DV
SB_W = HEADS * SB_DH
ROPE_BASE = 10000.0
LANES = 128

ADAM_LR = 0.001
ADAM_B1 = 0.9
ADAM_B2 = 0.999
ADAM_EPS = 1e-08
ADAM_WD = 0.01
ADAM_STEP = 10

VMEM_LIMIT = 56 * 1024 * 1024

_NN = (((1,), (0,)), ((), ()))
_NT = (((1,), (1,)), ((), ()))
_TN = (((0,), (0,)), ((), ()))


def _dot(a, b, dims=_NN):
    if a.dtype != BF16:
        a = a.astype(BF16)
    if b.dtype != BF16:
        b = b.astype(BF16)
    return lax.dot_general(a, b, dims, preferred_element_type=F32)


def _dot_split(a, b):
    hi = a.astype(BF16)
    lo = (a - hi.astype(F32)).astype(BF16)
    return (lax.dot_general(hi, b, _NN, preferred_element_type=F32)
            + lax.dot_general(lo, b, _NN, preferred_element_type=F32))


def _sigmoid(x):
    return 1.0 / (1.0 + jnp.exp(-x))


def _rms(x, d):
    r = lax.rsqrt(jnp.sum(x * x, axis=1, keepdims=True) * (1.0 / d) + EPS)
    return x * r, r


def _rms_bwd(dn, n, r, d):
    return r * (dn - n * (jnp.sum(dn * n, axis=1, keepdims=True) * (1.0 / d)))


def _colsum(v):
    return jnp.sum(v, axis=0, keepdims=True)


def _accum(ref, val, first):
    @pl.when(first)
    def _():
        ref[...] = val

    @pl.when(jnp.logical_not(first))
    def _():
        ref[...] += val


KIND_SLOTS = {"gather": N_DEV, "scatter": N_DEV, "gather_chip": N_DEV, "forward": N_DEV, "pair": N_DEV // 2,
              "chip_scatter": N_DEV // 2}
SEMS_PER_ARRAY = N_DEV - 1


def _exchange_copies(ins, outs, send_sems, recv_sems, local_sems, kinds):
    x, y, c = (lax.axis_index(a) for a in AXES)
    me, chip, sibling = 4 * x + 2 * y + c, 2 * x + y, (x, y, 1 - c)
    mesh_id = pl.DeviceIdType.MESH
    other_chips = []
    for k in range(1, N_DEV // 2):
        px = 1 - x if k & 2 else x
        py = 1 - y if k & 1 else y
        other_chips.append((px, py))
    copies = []
    for i, kind in enumerate(kinds):
        def remote(src, dst, k, to, i=i):
            return pltpu.make_async_remote_copy(
                src_ref=src, dst_ref=dst, send_sem=send_sems.at[i * SEMS_PER_ARRAY + k],
                recv_sem=recv_sems.at[i * SEMS_PER_ARRAY + k], device_id=to, device_id_type=mesh_id)

        if kind in ("gather", "scatter"):
            pick = (lambda ref, d: ref.at[d]) if kind == "scatter" else (lambda ref, d: ref)
            copies.append(pltpu.make_async_copy(pick(ins[i], me), outs[i].at[me], local_sems.at[i]))
            for k in range(1, N_DEV):
                to = (1 - x if k & 4 else x, 1 - y if k & 2 else y, 1 - c if k & 1 else c)
                copies.append(remote(pick(ins[i], 4 * to[0] + 2 * to[1] + to[2]), outs[i].at[me], k - 1, to))
        elif kind == "gather_chip":
            copies.append(pltpu.make_async_copy(ins[i], outs[i].at[me], local_sems.at[i]))
            copies.append(remote(ins[i], outs[i].at[me], 0, sibling))
            for k, (px, py) in enumerate(other_chips):
                copies.append(remote(ins[i], outs[i].at[me], 1 + k, (px, py, c)))
        elif kind == "forward":
            for k, (px, py) in enumerate(other_chips):
                slot = 4 * px + 2 * py + c
                copies.append(remote(outs[i].at[slot], outs[i].at[slot], k, sibling))
        elif kind == "pair":
            for k in range(N_DEV // 2):
                copies.append(remote(ins[i].at[2 * k + 1 - c], outs[i].at[k], k, sibling))
        elif kind == "chip_scatter":
            copies.append(pltpu.make_async_copy(ins[i].at[chip], outs[i].at[chip], local_sems.at[i]))
            for k, (px, py) in enumerate(other_chips):
                copies.append(remote(ins[i].at[2 * px + py], outs[i].at[chip], k, (px, py, c)))
        else:
            raise ValueError(kind)
    return copies


def _exchange_shapes(arrays, kinds):
    shapes = []
    for a, kind in zip(arrays, kinds):
        tail = a.shape if kind in ("gather", "gather_chip") else a.shape[1:]
        shapes.append(jax.ShapeDtypeStruct((KIND_SLOTS[kind],) + tuple(tail), a.dtype))
    return shapes


def _exchange_sems(n):
    return [pltpu.SemaphoreType.DMA((n * SEMS_PER_ARRAY,)), pltpu.SemaphoreType.DMA((n * SEMS_PER_ARRAY,)),
            pltpu.SemaphoreType.DMA((n,))]


def _call(name, body, grid, ins, outs, scratch=(), riders=None, prefetch=None):
    any_spec = pl.BlockSpec(memory_space=pl.ANY)
    in_specs = [pl.BlockSpec(memory_space=im) if bs is None else pl.BlockSpec(bs, im) for _, bs, im in ins]
    out_specs = [pl.BlockSpec(bs, im) for _, _, bs, im in outs]
    out_shape = [jax.ShapeDtypeStruct(s, d) for s, d, _, _ in outs]
    operands = [a for a, _, _ in ins]
    scratch = list(scratch)
    aliases = {}
    n_pre = 0 if prefetch is None else 1
    kernel = functools.partial(body) if prefetch is None else (lambda _, *refs: body(*refs))
    if riders is not None:
        arrays, kinds = riders
        nr, n_in, n_out, n_scr = len(arrays), len(ins), len(outs), len(scratch)

        def kernel(*refs):
            refs = refs[n_pre:]
            own_in, ride_in = refs[:n_in], refs[n_in:n_in + nr]
            own_out = refs[n_in + nr:n_in + nr + n_out]
            ride_out = refs[n_in + nr + n_out:n_in + 2 * nr + n_out]
            own_scr = refs[n_in + 2 * nr + n_out:n_in + 2 * nr + n_out + n_scr]
            sems = refs[n_in + 2 * nr + n_out + n_scr:]
            ids = [pl.program_id(a) for a in range(len(grid))]
            first = functools.reduce(jnp.logical_and, [i == 0 for i in ids])
            last = functools.reduce(jnp.logical_and, [i == g - 1 for i, g in zip(ids, grid)])

            @pl.when(first)
            def _():
                for cp in _exchange_copies(ride_in, ride_out, *sems, kinds):
                    cp.start()

            body(*own_in, *own_out, *own_scr)

            @pl.when(last)
            def _():
                for cp in _exchange_copies(ride_in, ride_out, *sems, kinds):
                    cp.wait()

        in_specs += [any_spec] * nr
        out_specs += [any_spec] * nr
        out_shape += _exchange_shapes(arrays, kinds)
        operands += list(arrays)
        scratch += _exchange_sems(nr)
        aliases = {n_pre + n_in + r: n_out + r for r, kind in enumerate(kinds) if kind == "forward"}
    params = pltpu.CompilerParams(dimension_semantics=("arbitrary",) * len(grid), vmem_limit_bytes=VMEM_LIMIT)
    if prefetch is None:
        return pl.pallas_call(kernel, name=name, grid=grid, in_specs=in_specs, out_specs=out_specs,
                              out_shape=out_shape, scratch_shapes=scratch, input_output_aliases=aliases,
                              compiler_params=params)(*operands)
    grid_spec = pltpu.PrefetchScalarGridSpec(num_scalar_prefetch=1, grid=grid, in_specs=in_specs,
                                             out_specs=out_specs, scratch_shapes=scratch)
    return pl.pallas_call(kernel, name=name, grid_spec=grid_spec, out_shape=out_shape,
                          input_output_aliases=aliases, compiler_params=params)(prefetch, *operands)


def _exchange(name, arrays, kinds):
    n = len(arrays)

    def body(*refs):
        copies = _exchange_copies(refs[:n], refs[n:2 * n], *refs[2 * n:], kinds)
        for cp in copies:
            cp.start()
        for cp in copies:
            cp.wait()

    any_spec = pl.BlockSpec(memory_space=pl.ANY)
    return pl.pallas_call(
        functools.partial(body),
        name=name,
        in_specs=[any_spec] * n,
        out_specs=[any_spec] * n,
        out_shape=_exchange_shapes(arrays, kinds),
        scratch_shapes=_exchange_sems(n),
        input_output_aliases={i: i for i, kind in enumerate(kinds) if kind == "forward"},
    )(*arrays)


def _pair_sum(name, mine, theirs, my_core, tr):
    _, rws, cls = mine.shape
    tr = min(tr, rws)

    def body(a_ref, b_ref, o_ref):
        o_ref[...] = (a_ref[...].astype(F32) + b_ref[...].astype(F32)).astype(o_ref.dtype)

    return _call(name, body, (N_DEV // 2, rws // tr),
                 [(mine, (None, tr, cls), lambda k, r, core: (2 * k + core[0], r, 0)),
                  (theirs, (None, tr, cls), lambda k, r, core: (k, r, 0))],
                 [((N_DEV // 2, rws, cls), mine.dtype, (None, tr, cls), lambda k, r, core: (k, r, 0))],
                 prefetch=my_core)[0]


def _matmul(name, a, b, kind, tm, tn, out_dtype, blocked_out=False, riders=None):
    if kind == "tn":
        kdim, m = a.shape
    else:
        m, kdim = a.shape
    n = b.shape[0] if kind == "nt" else b.shape[1]
    tm, tn = min(tm, m), min(tn, n)
    dims = {"nn": _NN, "nt": _NT, "tn": _TN}[kind]

    def body(a_ref, b_ref, o_ref):
        o_ref[...] = _dot(a_ref[...], b_ref[...], dims).astype(o_ref.dtype)

    a_spec = (a, (kdim, tm), lambda j, i: (0, i)) if kind == "tn" else (a, (tm, kdim), lambda j, i: (i, 0))
    b_spec = (b, (tn, kdim), lambda j, i: (j, 0)) if kind == "nt" else (b, (kdim, tn), lambda j, i: (0, j))
    if blocked_out:
        out = ((n // tn, m, tn), out_dtype, (None, tm, tn), lambda j, i: (j, i, 0))
    else:
        out = ((m, n), out_dtype, (tm, tn), lambda j, i: (i, j))
    res = _call(name, body, (n // tn, m // tm), [a_spec, b_spec], [out], riders=riders)
    return res[0] if riders is None else res


def _ada_fwd(cs_all, ada_w, ada_b_cols):
    def body(c_ref, w_ref, b_ref, o_ref):
        o_ref[...] = lax.dot_general(c_ref[...], w_ref[...], _NN, preferred_element_type=F32,
                                     precision=lax.Precision.HIGHEST) + b_ref[...]

    r, d = cs_all.shape
    nc = ada_w.shape[1]
    return _call("ada_fwd", body, (1,),
                 [(cs_all, (r, d), lambda i: (0, 0)), (ada_w, (d, nc), lambda i: (0, 0)),
                  (ada_b_cols, (1, nc), lambda i: (0, 0))],
                 [((r, nc), F32, (r, nc), lambda i: (0, 0))])[0]


def _silu_rows(c_all):
    def body(c_ref, o_ref):
        v = c_ref[...]
        o_ref[...] = v * _sigmoid(v)

    return _call("silu_c", body, (1,), [(c_all, c_all.shape, lambda i: (0, 0))],
                 [(c_all.shape, F32, c_all.shape, lambda i: (0, 0))])[0]


def _pre_norm(x, g, mod, tm, riders=None):
    s, d = x.shape

    def body(x_ref, g_ref, mod_ref, h_ref):
        n, _ = _rms(x_ref[...], d)
        sh, sc = mod_ref[:, 0:d], mod_ref[:, d:2 * d]
        h_ref[...] = (n * g_ref[...] * (1.0 + sc) + sh).astype(BF16)

    return _call("pre_norm", body, (s // tm,),
                 [(x, (tm, d), lambda i: (i, 0)), (g, (1, d), lambda i: (0, 0)),
                  (mod, (1, 6 * d), lambda i: (0, 0))],
                 [((s, d), BF16, (tm, d), lambda i: (i, 0))], riders=riders)


LOG2E = 1.4426950408889634
LN2 = 0.6931471805599453


def _decay_scale(lg_ref, idx, g, sign):
    return jnp.exp((sign * idx) * lg_ref[:, g * LANES:(g + 1) * LANES])


def _prep(proj, pos_col, idx_col, inv_freq, lg_lanes, tm):
    s = proj.shape[0]
    sb_off = (2 * RET_QK + 2 * RET_V) // (3 * SB_W)
    n_q = RET_QK // LANES

    def body(qk_ref, v_ref, sb_ref, pos_ref, idx_ref, f_ref, lg_ref, qk_out, v_out, sb_out, cos_out, sin_out):
        ang = pos_ref[...] * f_ref[...]
        lane = lax.broadcasted_iota(jnp.int32, (1, LANES), 1)
        first = jnp.bitwise_and(lane, RET_DQK - 1) < (RET_DQK // 2)
        cos = jnp.cos(ang)
        sin = jnp.where(first, -1.0, 1.0) * jnp.sin(ang)
        cos_out[...] = cos
        sin_out[...] = sin
        idx = idx_ref[...]
        for g in range(2 * n_q):
            v = qk_ref[:, g * LANES:(g + 1) * LANES].astype(F32)
            sw = jnp.where(first, pltpu.roll(v, LANES - RET_DQK // 2, 1), pltpu.roll(v, RET_DQK // 2, 1))
            r = v * cos + sw * sin
            if g < n_q:
                r = r * _decay_scale(lg_ref, idx, g, 1.0)
            else:
                r = r * (_decay_scale(lg_ref, idx, g - n_q, -1.0) * (RET_DQK ** -0.5))
            qk_out[:, g * LANES:(g + 1) * LANES] = r.astype(BF16)
        v_out[...] = v_ref[...].astype(BF16)
        sb_out[:, 0:SB_W] = (sb_ref[:, 0:SB_W].astype(F32) * (SB_DH ** -0.5 * LOG2E)).astype(BF16)
        sb_out[:, SB_W:3 * SB_W] = sb_ref[:, SB_W:3 * SB_W].astype(BF16)

    return _call("prep", body, (s // tm,),
                 [(proj, (tm, 2 * RET_QK), lambda i: (i, 0)),
                  (proj, (tm, RET_V), lambda i: (i, 2 * RET_QK // RET_V)),
                  (proj, (tm, 3 * SB_W), lambda i: (i, sb_off)),
                  (pos_col, (tm, 1), lambda i: (i, 0)),
                  (idx_col, (tm, 1), lambda i: (i, 0)),
                  (inv_freq, (1, LANES), lambda i: (0, 0)),
                  (lg_lanes, (1, RET_QK), lambda i: (0, 0))],
                 [((s, 2 * RET_QK), BF16, (tm, 2 * RET_QK), lambda i: (i, 0)),
                  ((s, RET_V), BF16, (tm, RET_V), lambda i: (i, 0)),
                  ((s, 3 * SB_W), BF16, (tm, 3 * SB_W), lambda i: (i, 0)),
                  ((s, LANES), F32, (tm, LANES), lambda i: (i, 0)),
                  ((s, LANES), F32, (tm, LANES), lambda i: (i, 0))])


def _head_mask(hh):
    lane = lax.broadcasted_iota(jnp.int32, (1, LANES), 1)
    return (lane >= RET_DQK) if hh else (lane < RET_DQK)


def _masked(v, m):
    return jnp.where(m, v, jnp.zeros_like(v))


SB_GROUP = 4
RET_GROUP = 4


def _stack_heads(v):
    return jnp.concatenate([_masked(v, _head_mask(0)), _masked(v, _head_mask(1))], axis=0)


def _side_by_side(v, t):
    return jnp.concatenate([v[:t], v[t:]], axis=1)


def _split_bf16(v):
    hi = v.astype(BF16)
    lo = (v - hi.astype(F32)).astype(BF16)
    return jnp.concatenate([hi, lo], axis=1)


def _tile_pos(i, j, tq, tk):
    row = jnp.bitwise_and(lax.broadcasted_iota(jnp.int32, (2 * tq, tk), 0), tq - 1) + i * tq
    col = lax.broadcasted_iota(jnp.int32, (2 * tq, tk), 1) + j * tk
    return row, col


def _n_groups(i, tq, tk, grp):
    return ((i + 1) * (tq // tk) + grp - 1) // grp


def _n_full(i, tq, tk, grp):
    return (i * (tq // tk)) // grp


def _key_rows(j, tk):
    return pl.ds(pl.multiple_of(j * tk, tk), tk)


def _ret_weight(lg_rows, i, j, tq, tk):
    row, col = _tile_pos(i, j, tq, tk)
    same = jnp.right_shift(col, CHUNK_SHIFT) == jnp.right_shift(row, CHUNK_SHIFT)
    later = jnp.where(same, jnp.exp((2.0 * lg_rows) * (col - row).astype(F32)), 0.0)
    return jnp.where(col <= row, 1.0, later)


def _lg_rows(lg_ref, hp, tq):
    first = lax.broadcasted_iota(jnp.int32, (2 * tq, 1), 0) < tq
    return jnp.where(first, lg_ref[2 * hp], lg_ref[2 * hp + 1])


def _check_tiles(s, tq, tk, grp):
    assert tq % tk == 0 and tq & (tq - 1) == 0 and tk & (tk - 1) == 0
    assert s % tq == 0 and (s // tk) % grp == 0 and s // tk <= LANES


def _pair_mask():
    r = lax.broadcasted_iota(jnp.int32, (LANES, 2 * RET_DV), 0) >= RET_DQK
    c = lax.broadcasted_iota(jnp.int32, (LANES, 2 * RET_DV), 1) >= RET_DV
    return (r == c).astype(F32)


def _ret_block(lg_ref, hp, i, t, qb, kb):
    w = _ret_weight(_lg_rows(lg_ref, hp, t), i, i, t, t)
    return _dot(_stack_heads(qb), kb, _NT), w


def _ret_fwd(qk_rot, v_bf, proj, gn_g, log_gamma, t, riders=None):
    s = qk_rot.shape[0]
    gate_off = (2 * RET_QK + RET_V) // (2 * RET_DV)
    n_pair = HEADS // 2
    pw = 2 * RET_DV
    assert s % t == 0 and t % CHUNK == 0 and t & (t - 1) == 0

    def body(lg_ref, q_ref, k_ref, v_ref, g_ref, w_ref, ret_ref, rg_ref, state_ref):
        hp, i = pl.program_id(0), pl.program_id(1)

        @pl.when(i == 0)
        def _():
            state_ref[...] = jnp.zeros_like(state_ref)

        qb, kb, vb = q_ref[...], k_ref[...], v_ref[...]
        z, w = _ret_block(lg_ref, hp, i, t, qb, kb)
        p = (z * w).astype(BF16)
        intra = jnp.concatenate([_dot(p[:t], vb[:, 0:RET_DV]), _dot(p[t:], vb[:, RET_DV:pw])], axis=1)
        out = intra + _dot(qb, state_ref[...])
        state_ref[...] += _pair_mask() * _dot(kb, vb, _TN)
        ret_ref[...] = out
        for hh in range(2):
            cols = slice(hh * RET_DV, (hh + 1) * RET_DV)
            o = out[:, cols]
            mu = jnp.sum(o, axis=1, keepdims=True) * (1.0 / RET_DV)
            xc = o - mu
            var = jnp.sum(xc * xc, axis=1, keepdims=True) * (1.0 / RET_DV)
            nrm = xc * lax.rsqrt(var + EPS) * w_ref[:, cols]
            g = g_ref[:, cols].astype(F32)
            rg_ref[:, cols] = (g * _sigmoid(g) * nrm).astype(BF16)

    blk = lambda hp, i: (i, hp)
    return _call("ret_fwd", body, (n_pair, s // t),
                 [(log_gamma, None, pltpu.SMEM),
                  (qk_rot, (t, LANES), blk),
                  (qk_rot, (t, LANES), lambda hp, i: (i, n_pair + hp)),
                  (v_bf, (t, pw), blk),
                  (proj, (t, pw), lambda hp, i: (i, gate_off + hp)),
                  (gn_g, (1, pw), lambda hp, i: (0, hp))],
                 [((s, RET_V), F32, (t, pw), blk), ((s, RET_V), BF16, (t, pw), blk)],
                 scratch=[pltpu.VMEM((LANES, pw), F32)], riders=riders)


def _tri2(tk, strict_upper):
    r = jnp.bitwise_and(lax.broadcasted_iota(jnp.int32, (2 * tk, tk), 0), tk - 1)
    cc = lax.broadcasted_iota(jnp.int32, (2 * tk, tk), 1)
    return ((r > cc) if strict_upper else (r < cc)).astype(BF16)


def _sb_valid(i, j, tq, tk):
    row, col = _tile_pos(i, j, tq, tk)
    return col < row


def _sb_fwd(qkv, tq, tk, riders=None):
    s = qkv.shape[0]
    n_pair = HEADS // 2
    _check_tiles(s, tq, tk, SB_GROUP)

    def body(q_ref, k_ref, v_ref, o_ref, a_ref):
        i = pl.program_id(1)
        upper2 = _tri2(tk, True)
        qs = _stack_heads(q_ref[...])
        n_full, n_groups = _n_full(i, tq, tk, SB_GROUP), _n_groups(i, tq, tk, SB_GROUP)

        def make_step(near_diagonal, last):
            def step(n, carry):
                c, o = carry
                g = last - 1 - n
                js = [g * SB_GROUP + sub for sub in range(SB_GROUP)]
                zs = [_dot(qs, k_ref[_key_rows(j, tk), :], _NT) for j in js]
                log1ps = [jnp.log2(1.0 + jnp.exp2(-jnp.abs(z))) for z in zs]
                log_1ms = [-jnp.maximum(z, 0.0) - t for z, t in zip(zs, log1ps)]
                log_bs = [jnp.minimum(z, 0.0) - t for z, t in zip(zs, log1ps)]
                if near_diagonal:
                    valids = [_sb_valid(i, j, tq, tk) for j in js]
                    log_1ms = [jnp.where(v, l, 0.0) for v, l in zip(valids, log_1ms)]
                sticks = [lax.dot_general(_split_bf16(l), upper2, _NN, preferred_element_type=F32) for l in log_1ms]
                sums = [jnp.sum(l, axis=1, keepdims=True) for l in log_1ms]
                cs = [None] * SB_GROUP
                for sub in reversed(range(SB_GROUP)):
                    cs[sub] = c
                    c = c + sums[sub]
                for sub, j in enumerate(js):
                    a = jnp.exp2(log_bs[sub] + sticks[sub] + cs[sub])
                    if near_diagonal:
                        a = jnp.where(valids[sub], a, 0.0)
                    a = a.astype(BF16)
                    a_ref[j] = a
                    o = o + _dot(_side_by_side(a, tq), _stack_heads(v_ref[_key_rows(j, tk), :]))
                return c, o
            return step

        carry = (jnp.zeros((2 * tq, 1), F32), jnp.zeros((tq, LANES), F32))
        carry = lax.fori_loop(0, n_groups - n_full, make_step(True, n_groups), carry)
        _, acc = lax.fori_loop(0, n_full, make_step(False, n_full), carry)
        o_ref[...] = acc

    n_kb = s // tk
    return _call("sb_fwd", body, (n_pair, s // tq),
                 [(qkv, (tq, LANES), lambda hp, i: (i, hp)),
                  (qkv, (s, LANES), lambda hp, i: (0, n_pair + hp)),
                  (qkv, (s, LANES), lambda hp, i: (0, 2 * n_pair + hp))],
                 [((s, SB_W), F32, (tq, LANES), lambda hp, i: (i, hp)),
                  ((n_pair, s // tq, n_kb, 2 * tq, tk), BF16, (None, None, n_kb, 2 * tq, tk),
                   lambda hp, i: (hp, i, 0, 0, 0))], riders=riders)


def _merge(retg, sb, w_ret, w_sb, proj, tm, tn, riders=None):
    s, d = retg.shape[0], w_ret.shape[1]
    ar_off = (2 * RET_QK + 2 * RET_V + 3 * SB_W) // tn
    as_off = ar_off + d // tn

    def body(rg_ref, sb_ref, wr_ref, ws_ref, ar_ref, as_ref, mix_ref, r_ref, s_ref):
        rr = _dot(rg_ref[...], wr_ref[...])
        ss = _dot(sb_ref[...], ws_ref[...])
        mix_ref[...] = (_sigmoid(ar_ref[...].astype(F32)) * rr + _sigmoid(as_ref[...].astype(F32)) * ss).astype(BF16)
        r_ref[...] = rr.astype(BF16)
        s_ref[...] = ss.astype(BF16)

    tile = (tm, tn)
    return _call("merge", body, (d // tn, s // tm),
                 [(retg, (tm, RET_V), lambda j, i: (i, 0)), (sb, (tm, SB_W), lambda j, i: (i, 0)),
                  (w_ret, (RET_V, tn), lambda j, i: (0, j)), (w_sb, (SB_W, tn), lambda j, i: (0, j)),
                  (proj, tile, lambda j, i: (i, ar_off + j)), (proj, tile, lambda j, i: (i, as_off + j))],
                 [((s, d), BF16, tile, lambda j, i: (i, j))] * 3, riders=riders)


def _out_proj(mixed, w_out, x, mod, gp1, g2, tm):
    s, d = x.shape

    def body(a_ref, w_ref, x_ref, mod_ref, gp_ref, g2_ref, y_ref, hres_ref, h2_ref):
        y = _dot(a_ref[...], w_ref[...])
        y_ref[...] = y
        ny, _ = _rms(y, d)
        hres = x_ref[...] + mod_ref[:, 2 * d:3 * d] * (ny * gp_ref[...])
        hres_ref[...] = hres
        n2, _ = _rms(hres, d)
        h2_ref[...] = (n2 * g2_ref[...] * (1.0 + mod_ref[:, 4 * d:5 * d]) + mod_ref[:, 3 * d:4 * d]).astype(BF16)

    row = lambda i: (i, 0)
    fix = lambda i: (0, 0)
    return _call("out_proj", body, (s // tm,),
                 [(mixed, (tm, d), row), (w_out, (d, d), fix), (x, (tm, d), row),
                  (mod, (1, 6 * d), fix), (gp1, (1, d), fix), (g2, (1, d), fix)],
                 [((s, d), F32, (tm, d), row), ((s, d), F32, (tm, d), row), ((s, d), BF16, (tm, d), row)])


def _ff1(h2, w_ff1, tm, tn):
    s, f = h2.shape[0], w_ff1.shape[1]
    tm = min(tm, s)

    def body(a_ref, w_ref, u_ref, act_ref):
        u = _dot(a_ref[...], w_ref[...])
        r = jnp.maximum(u, 0.0)
        u_ref[...] = u.astype(BF16)
        act_ref[...] = (r * r).astype(BF16)

    d = h2.shape[1]
    return _call("ff1", body, (f // tn, s // tm),
                 [(h2, (tm, d), lambda j, i: (i, 0)), (w_ff1, (d, tn), lambda j, i: (0, j))],
                 [((s, f), BF16, (tm, tn), lambda j, i: (i, j))] * 2)


def _ff2_loss(act, w_ff2, hres, target, mod, gp2, tm):
    s, d = hres.shape
    f = act.shape[1]

    def body(a_ref, w_ref, h_ref, t_ref, mod_ref, gp_ref, dout_ref, df_ref, loss_ref, dgt_ref, dgp_ref):
        first = pl.program_id(0) == 0
        ff = _dot(a_ref[...], w_ref[...])
        nf, rf = _rms(ff, d)
        gt, gp = mod_ref[:, 5 * d:6 * d], gp_ref[...]
        out = h_ref[...] + gt * (nf * gp)
        err = out - t_ref[...]
        sq = jnp.sum(err * err, axis=1, keepdims=True)
        _accum(loss_ref, jnp.sum(sq, axis=0, keepdims=True), first)
        dout = err * (1.0 / d)
        dout_ref[...] = dout
        _accum(dgt_ref, _colsum(dout * (nf * gp)), first)
        _accum(dgp_ref, _colsum(dout * gt * nf), first)
        df_ref[...] = _rms_bwd(dout * gt * gp, nf, rf, d).astype(BF16)

    row = lambda i: (i, 0)
    fix = lambda i: (0, 0)
    return _call("ff2_loss", body, (s // tm,),
                 [(act, (tm, f), row), (w_ff2, (f, d), fix), (hres, (tm, d), row), (target, (tm, d), row),
                  (mod, (1, 6 * d), fix), (gp2, (1, d), fix)],
                 [((s, d), F32, (tm, d), row), ((s, d), BF16, (tm, d), row), ((1, 1), F32, (1, 1), fix),
                  ((1, d), F32, (1, d), fix), ((1, d), F32, (1, d), fix)])


def _ff2_bwd(df, w_ff2, u, tm, tn):
    s, d = df.shape
    f = w_ff2.shape[0]
    tm = min(tm, s)

    def body(a_ref, w_ref, u_ref, du_ref):
        da = _dot(a_ref[...], w_ref[...], _NT)
        du_ref[...] = (da * (2.0 * jnp.maximum(u_ref[...].astype(F32), 0.0))).astype(BF16)

    return _call("ff2_bwd", body, (f // tn, s // tm),
                 [(df, (tm, d), lambda j, i: (i, 0)), (w_ff2, (tn, d), lambda j, i: (j, 0)),
                  (u, (tm, tn), lambda j, i: (i, j))],
                 [((s, f), BF16, (tm, tn), lambda j, i: (i, j))])[0]


def _ff1_bwd(du, w_ff1, hres, dout, y, mod, g2, gp1, tm, riders=None):
    s, d = hres.shape
    f = du.shape[1]

    def body(a_ref, w_ref, h_ref, do_ref, y_ref, mod_ref, g2_ref, gp_ref,
             dh_ref, dy_ref, dsh_ref, dsc_ref, dg2_ref, dgt_ref, dgp_ref):
        first = pl.program_id(0) == 0
        dh2 = _dot(a_ref[...], w_ref[...], _NT)
        n2, r2 = _rms(h_ref[...], d)
        g2, sc2 = g2_ref[...], mod_ref[:, 4 * d:5 * d]
        _accum(dsh_ref, _colsum(dh2), first)
        _accum(dsc_ref, _colsum(dh2 * n2 * g2), first)
        _accum(dg2_ref, _colsum(dh2 * n2 * (1.0 + sc2)), first)
        dhres = do_ref[...] + _rms_bwd(dh2 * g2 * (1.0 + sc2), n2, r2, d)
        dh_ref[...] = dhres
        ny, ry = _rms(y_ref[...], d)
        gt, gp = mod_ref[:, 2 * d:3 * d], gp_ref[...]
        _accum(dgt_ref, _colsum(dhres * (ny * gp)), first)
        _accum(dgp_ref, _colsum(dhres * gt * ny), first)
        dy_ref[...] = _rms_bwd(dhres * gt * gp, ny, ry, d).astype(BF16)

    row = lambda i: (i, 0)
    fix = lambda i: (0, 0)
    vec = ((1, d), F32, (1, d), fix)
    return _call("ff1_bwd", body, (s // tm,),
                 [(du, (tm, f), row), (w_ff1, (d, f), fix), (hres, (tm, d), row), (dout, (tm, d), row),
                  (y, (tm, d), row), (mod, (1, 6 * d), fix), (g2, (1, d), fix), (gp1, (1, d), fix)],
                 [((s, d), F32, (tm, d), row), ((s, d), BF16, (tm, d), row), vec, vec, vec, vec, vec], riders=riders)


def _out_bwd(dy, w_out, proj, r_bf, s_bf, tm, tn):
    s, d = dy.shape
    ar_off = (2 * RET_QK + 2 * RET_V + 3 * SB_W) // tn
    as_off = ar_off + d // tn

    def body(a_ref, w_ref, ar_ref, as_ref, r_ref, s_ref, dr_ref, ds_ref, dar_ref, das_ref):
        dm = _dot(a_ref[...], w_ref[...], _NT)
        sr, ss = _sigmoid(ar_ref[...].astype(F32)), _sigmoid(as_ref[...].astype(F32))
        dr_ref[...] = (dm * sr).astype(BF16)
        ds_ref[...] = (dm * ss).astype(BF16)
        dar_ref[...] = (dm * r_ref[...].astype(F32) * sr * (1.0 - sr)).astype(BF16)
        das_ref[...] = (dm * s_ref[...].astype(F32) * ss * (1.0 - ss)).astype(BF16)

    tile = (tm, tn)
    here = lambda j, i: (i, j)
    return _call("out_bwd", body, (d // tn, s // tm),
                 [(dy, (tm, d), lambda j, i: (i, 0)), (w_out, (tn, d), lambda j, i: (j, 0)),
                  (proj, tile, lambda j, i: (i, ar_off + j)), (proj, tile, lambda j, i: (i, as_off + j)),
                  (r_bf, tile, here), (s_bf, tile, here)],
                 [((s, d), BF16, tile, here)] * 4)


def _gn_bwd(dretg, ret, proj, gn_g, tm, riders=None):
    s = ret.shape[0]
    gate_off = (2 * RET_QK + RET_V) // RET_V

    def body(d_ref, r_ref, g_ref, w_ref, dg_ref, dret_ref, dw_ref):
        first = pl.program_id(0) == 0
        for h in range(HEADS):
            cols = slice(h * RET_DV, (h + 1) * RET_DV)
            o, g, w, dr = r_ref[:, cols], g_ref[:, cols].astype(F32), w_ref[:, cols], d_ref[:, cols].astype(F32)
            mu = jnp.sum(o, axis=1, keepdims=True) * (1.0 / RET_DV)
            xc = o - mu
            rstd = lax.rsqrt(jnp.sum(xc * xc, axis=1, keepdims=True) * (1.0 / RET_DV) + EPS)
            n = xc * rstd
            sg = _sigmoid(g)
            silu = g * sg
            dg_ref[:, cols] = (dr * n * w * (sg * (1.0 + g * (1.0 - sg)))).astype(BF16)
            _accum(dw_ref.at[:, cols], _colsum(dr * silu * n), first)
            dn = dr * silu * w
            m1 = jnp.sum(dn, axis=1, keepdims=True) * (1.0 / RET_DV)
            m2 = jnp.sum(dn * n, axis=1, keepdims=True) * (1.0 / RET_DV)
            dret_ref[:, cols] = (rstd * (dn - m1 - n * m2)).astype(BF16)

    row = lambda i: (i, 0)
    fix = lambda i: (0, 0)
    return _call("gn_bwd", body, (s // tm,),
                 [(dretg, (tm, RET_V), row), (ret, (tm, RET_V), row),
                  (proj, (tm, RET_V), lambda i: (i, gate_off)), (gn_g, (1, RET_V), fix)],
                 [((s, RET_V), BF16, (tm, RET_V), row), ((s, RET_V), BF16, (tm, RET_V), row),
                  ((1, RET_V), F32, (1, RET_V), fix)], riders=riders)


def _ret_bwd(qk_rot, v_bf, dret, log_gamma, t, riders=None):
    s = qk_rot.shape[0]
    n_pair = HEADS // 2
    pw = 2 * RET_DV
    n_blk = s // t

    def d_scores(lg_ref, hp, i, qb, kb, vb, dob):
        z, w = _ret_block(lg_ref, hp, i, t, qb, kb)
        dp = jnp.concatenate([_dot(dob[:, 0:RET_DV], vb[:, 0:RET_DV], _NT),
                              _dot(dob[:, RET_DV:pw], vb[:, RET_DV:pw], _NT)], axis=0)
        return (z * w).astype(BF16), (dp * w).astype(BF16)

    def up_body(lg_ref, q_ref, k_ref, v_ref, do_ref, dq_ref, state_ref):
        hp, i = pl.program_id(0), pl.program_id(1)

        @pl.when(i == 0)
        def _():
            state_ref[...] = jnp.zeros_like(state_ref)

        qb, kb, vb, dob = q_ref[...], k_ref[...], v_ref[...], do_ref[...]
        _, ds = d_scores(lg_ref, hp, i, qb, kb, vb, dob)
        dq_ref[...] = _dot(_side_by_side(ds, t), _stack_heads(kb)) + _dot(dob, state_ref[...], _NT)
        state_ref[...] += _pair_mask() * _dot(kb, vb, _TN)

    def down_body(lg_ref, q_ref, k_ref, v_ref, do_ref, dk_ref, dv_ref, state_ref):
        hp, i = pl.program_id(0), n_blk - 1 - pl.program_id(1)

        @pl.when(pl.program_id(1) == 0)
        def _():
            state_ref[...] = jnp.zeros_like(state_ref)

        qb, kb, vb, dob = q_ref[...], k_ref[...], v_ref[...], do_ref[...]
        p, ds = d_scores(lg_ref, hp, i, qb, kb, vb, dob)
        later = state_ref[...]
        dv_ref[...] = jnp.concatenate([_dot(p[:t], dob[:, 0:RET_DV], _TN), _dot(p[t:], dob[:, RET_DV:pw], _TN)],
                                      axis=1) + _dot(kb, later)
        dk_ref[...] = _dot(ds, _stack_heads(qb), _TN) + _dot(vb, later, _NT)
        state_ref[...] += _pair_mask() * _dot(qb, dob, _TN)

    def ins(order):
        return [(log_gamma, None, pltpu.SMEM),
                (qk_rot, (t, LANES), lambda hp, i: (order(i), hp)),
                (qk_rot, (t, LANES), lambda hp, i: (order(i), n_pair + hp)),
                (v_bf, (t, pw), lambda hp, i: (order(i), hp)),
                (dret, (t, pw), lambda hp, i: (order(i), hp))]

    up = lambda i: i
    down = lambda i: n_blk - 1 - i
    scratch = [pltpu.VMEM((LANES, pw), F32)]
    dq = _call("ret_bwd_q", up_body, (n_pair, n_blk), ins(up),
               [((s, RET_QK), F32, (t, LANES), lambda hp, i: (i, hp))], scratch=scratch)[0]
    dk, dv, *rest = _call("ret_bwd_kv", down_body, (n_pair, n_blk), ins(down),
                          [((s, RET_QK), F32, (t, LANES), lambda hp, i: (down(i), hp)),
                           ((s, RET_V), F32, (t, pw), lambda hp, i: (down(i), hp))],
                          scratch=scratch, riders=riders)
    return [dq, dk, dv] + rest


def _sb_bwd(qkv, weights, do, tq, tk, riders=None):
    s = qkv.shape[0]
    n_pair = HEADS // 2
    _check_tiles(s, tq, tk, SB_GROUP)

    def body(q_ref, k_ref, v_ref, a_ref, do_ref, dq_ref, dk_ref, dv_ref):
        i = pl.program_id(1)

        @pl.when(i == 0)
        def _():
            dk_ref[...] = jnp.zeros_like(dk_ref)
            dv_ref[...] = jnp.zeros_like(dv_ref)

        lower2 = _tri2(tk, False)
        qs = _stack_heads(q_ref[...])
        dos = _stack_heads(do_ref[...].astype(BF16))

        def make_step(near_diagonal):
            def step(g, carry):
                c_e, dq = carry
                js = [g * SB_GROUP + sub for sub in range(SB_GROUP)]
                rows = [_key_rows(j, tk) for j in js]
                zs = [_dot(qs, k_ref[rw, :], _NT) for rw in rows]
                das = [_dot(dos, v_ref[rw, :], _NT) for rw in rows]
                avals = [a_ref[j] for j in js]
                for a, rw in zip(avals, rows):
                    dv_ref[rw, :] += _dot(a, dos, _TN)
                es = [a.astype(F32) * da for a, da in zip(avals, das)]
                prefixes = [lax.dot_general(_split_bf16(e), lower2, _NN, preferred_element_type=F32) for e in es]
                betas = [1.0 / (1.0 + jnp.exp2(-z)) for z in zs]
                for sub in range(SB_GROUP):
                    dz = es[sub] - (es[sub] + prefixes[sub] + c_e) * betas[sub]
                    if near_diagonal:
                        dz = jnp.where(_sb_valid(i, js[sub], tq, tk), dz, 0.0)
                    dz = dz.astype(BF16)
                    dk_ref[rows[sub], :] += _dot(dz, qs, _TN)
                    dq = dq + _dot(_side_by_side(dz, tq), _stack_heads(k_ref[rows[sub], :]))
                    c_e = c_e + jnp.sum(es[sub], axis=1, keepdims=True)
                return c_e, dq
            return step

        n_full = _n_full(i, tq, tk, SB_GROUP)
        carry = (jnp.zeros((2 * tq, 1), F32), jnp.zeros((tq, LANES), F32))
        carry = lax.fori_loop(0, n_full, make_step(False), carry)
        _, dq = lax.fori_loop(n_full, _n_groups(i, tq, tk, SB_GROUP), make_step(True), carry)
        dq_ref[...] = dq

    blk = lambda hp, i: (i, hp)
    n_kb = s // tk
    return _call("sb_bwd", body, (n_pair, s // tq),
                 [(qkv, (tq, LANES), blk),
                  (qkv, (s, LANES), lambda hp, i: (0, n_pair + hp)),
                  (qkv, (s, LANES), lambda hp, i: (0, 2 * n_pair + hp)),
                  (weights, (None, None, n_kb, 2 * tq, tk), lambda hp, i: (hp, i, 0, 0, 0)),
                  (do, (tq, LANES), blk)],
                 [((s, SB_W), F32, (tq, LANES), blk),
                  ((s, SB_W), F32, (s, LANES), lambda hp, i: (0, hp)),
                  ((s, SB_W), F32, (s, LANES), lambda hp, i: (0, hp))], riders=riders)


def _assemble_dproj(dq_r, dk_r, dv_r, dg_r, dq_s, dk_s, dv_s, da_r, da_s, cos, sin, idx_col, lg_lanes, tm):
    s, d = da_r.shape
    width = 2 * RET_QK + 2 * RET_V + 3 * SB_W + 2 * d

    def body(dq_ref, dk_ref, dv_ref, dg_ref, dqs_ref, dks_ref, dvs_ref, dar_ref, das_ref, cos_ref, sin_ref,
             idx_ref, lg_ref, o_ref):
        lane = lax.broadcasted_iota(jnp.int32, (1, LANES), 1)
        first = jnp.bitwise_and(lane, RET_DQK - 1) < (RET_DQK // 2)
        cos, sin = cos_ref[...], sin_ref[...]
        idx = idx_ref[...]
        for src, base, sign, scale in ((dq_ref, 0, 1.0, 1.0), (dk_ref, RET_QK, -1.0, RET_DQK ** -0.5)):
            for g in range(RET_QK // LANES):
                v = src[:, g * LANES:(g + 1) * LANES] * (_decay_scale(lg_ref, idx, g, sign) * scale)
                sw = jnp.where(first, pltpu.roll(v, LANES - RET_DQK // 2, 1), pltpu.roll(v, RET_DQK // 2, 1))
                o_ref[:, base + g * LANES:base + (g + 1) * LANES] = (v * cos - sw * sin).astype(BF16)
        off = 2 * RET_QK
        o_ref[:, off:off + RET_V] = dv_ref[...].astype(BF16)
        off += RET_V
        o_ref[:, off:off + RET_V] = dg_ref[...]
        off += RET_V
        o_ref[:, off:off + SB_W] = (dqs_ref[...] * (SB_DH ** -0.5)).astype(BF16)
        off += SB_W
        o_ref[:, off:off + SB_W] = (dks_ref[...] * LN2).astype(BF16)
        off += SB_W
        o_ref[:, off:off + SB_W] = dvs_ref[...].astype(BF16)
        off += SB_W
        o_ref[:, off:off + d] = dar_ref[...]
        off += d
        o_ref[:, off:off + d] = das_ref[...]

    row = lambda i: (i, 0)
    ins = [(a, (tm, a.shape[1]), row) for a in (dq_r, dk_r, dv_r, dg_r, dq_s, dk_s, dv_s, da_r, da_s, cos, sin, idx_col)]
    ins.append((lg_lanes, (1, RET_QK), lambda i: (0, 0)))
    return _call("assemble_dproj", body, (s // tm,), ins, [((s, width), BF16, (tm, width), row)])[0]


def _in_bwd(dproj, w_in_t, x, dhres, mod, g1, tm, riders=None):
    s, d = x.shape
    width = dproj.shape[1]

    def body(a_ref, w_ref, x_ref, dh_ref, mod_ref, g_ref, dx_ref, dsh_ref, dsc_ref, dg_ref):
        first = pl.program_id(0) == 0
        dh = _dot(a_ref[...], w_ref[...])
        n1, r1 = _rms(x_ref[...], d)
        g1, sc1 = g_ref[...], mod_ref[:, d:2 * d]
        _accum(dsh_ref, _colsum(dh), first)
        _accum(dsc_ref, _colsum(dh * n1 * g1), first)
        _accum(dg_ref, _colsum(dh * n1 * (1.0 + sc1)), first)
        dx_ref[...] = dh_ref[...] + _rms_bwd(dh * g1 * (1.0 + sc1), n1, r1, d)

    row = lambda i: (i, 0)
    fix = lambda i: (0, 0)
    vec = ((1, d), F32, (1, d), fix)
    return _call("in_bwd", body, (s // tm,),
                 [(dproj, (tm, width), row), (w_in_t, (width, d), fix), (x, (tm, d), row), (dhres, (tm, d), row),
                  (mod, (1, 6 * d), fix), (g1, (1, d), fix)],
                 [((s, d), F32, (tm, d), row), vec, vec, vec], riders=riders)


def _adamw(w, g, m, v):
    m = ADAM_B1 * m + (1.0 - ADAM_B1) * g
    v = ADAM_B2 * v + (1.0 - ADAM_B2) * (g * g)
    m_hat = m / (1.0 - ADAM_B1 ** ADAM_STEP)
    v_hat = v / (1.0 - ADAM_B2 ** ADAM_STEP)
    delta = -ADAM_LR * (m_hat / (jnp.sqrt(v_hat) + ADAM_EPS) + ADAM_WD * w)
    return delta, m, v


def _adam_reduce(name, parts, w, m, v, tr):
    rws, cls = w.shape
    tr = min(tr, rws)
    n_parts = parts.shape[0]

    def body(p_ref, w_ref, m_ref, v_ref, g_out, d_out, m_out, v_out):
        g = p_ref[0].astype(F32)
        for k in range(1, n_parts):
            g = g + p_ref[k].astype(F32)
        delta, mn, vn = _adamw(w_ref[...], g, m_ref[...], v_ref[...])
        g_out[...] = g
        d_out[...] = delta
        m_out[...] = mn
        v_out[...] = vn

    row = lambda i: (i, 0)
    blk = (tr, cls)
    return _call(name, body, (rws // tr,),
                 [(parts, (n_parts, tr, cls), lambda i: (0, i, 0)), (w, blk, row), (m, blk, row), (v, blk, row)],
                 [((rws, cls), F32, blk, row)] * 4)


def _ada_bwd_adam(cs_t, dmod_cols, w, m, v):
    d, nc = w.shape

    def body(c_ref, dm_ref, w_ref, m_ref, v_ref, g_out, d_out, m_out, v_out):
        g = c_ref[0] * dm_ref[0:1, :]
        for r in range(1, N_DEV):
            g = g + c_ref[r] * dm_ref[r:r + 1, :]
        delta, mn, vn = _adamw(w_ref[...], g, m_ref[...], v_ref[...])
        g_out[...] = g
        d_out[...] = delta
        m_out[...] = mn
        v_out[...] = vn

    fix = lambda i: (0, 0)
    blk = (d, nc)
    return _call("ada_bwd_adam", body, (1,),
                 [(cs_t, (N_DEV, d, 1), lambda i: (0, 0, 0)), (dmod_cols, (N_DEV, nc), fix), (w, blk, fix), (m, blk, fix), (v, blk, fix)],
                 [((d, nc), F32, blk, fix)] * 4)


def _small_adam(parts, w, m, v):
    n = w.shape[1]

    def body(p_ref, w_ref, m_ref, v_ref, g_out, d_out, m_out, v_out):
        g = p_ref[0:1, :]
        for k in range(1, N_DEV):
            g = g + p_ref[k:k + 1, :]
        delta, mn, vn = _adamw(w_ref[...], g, m_ref[...], v_ref[...])
        g_out[...] = g
        d_out[...] = delta
        m_out[...] = mn
        v_out[...] = vn

    fix = lambda i: (0, 0)
    return _call("small_adam", body, (1,),
                 [(parts, (N_DEV, n), fix), (w, (1, n), fix), (m, (1, n), fix), (v, (1, n), fix)],
                 [((1, n), F32, (1, n), fix)] * 4)


def kernel(x, c, positions, ada_w, ada_b, pre_mix_g, post_mix_g, pre_ffn_g, post_ffn_g, w_in, ret_gn_g, w_ret_branch, w_sb_branch, w_out, w_ff1, w_ff2, loss_target, m_ada_w, m_ada_b, m_pre_mix_g, m_post_mix_g, m_pre_ffn_g, m_post_ffn_g, m_w_in, m_ret_gn_g, m_w_ret_branch, m_w_sb_branch, m_w_out, m_w_ff1, m_w_ff2, v_ada_w, v_ada_b, v_pre_mix_g, v_post_mix_g, v_pre_ffn_g, v_post_ffn_g, v_w_in, v_ret_gn_g, v_w_ret_branch, v_w_sb_branch, v_w_out, v_w_ff1, v_w_ff2):
    _, s, d = x.shape
    d_ff = w_ff1.shape[2] * N_DEV
    d_in = w_in.shape[2] * N_DEV
    me = 4 * lax.axis_index("x") + 2 * lax.axis_index("y") + lax.axis_index("c")
    x2, tgt = x[0], loss_target[0]

    core = lax.axis_index("c").astype(jnp.int32).reshape(1)
    bf = lambda w: w[0].astype(BF16)

    w_in_t, m_in_t, v_in_t = (jnp.swapaxes(a[0], 0, 1) for a in (w_in, m_w_in, v_w_in))

    c_all, g_in = _exchange("gather_in", [c, w_in_t.astype(BF16)], ["gather", "gather_chip"])
    c_all = c_all.reshape(N_DEV, d)

    n_ada = ada_w.shape[2]
    cs_all = _silu_rows(c_all)
    ada_b_cols = lax.dynamic_slice(ada_b, (0, me * n_ada), (1, n_ada))
    mod_cols = _ada_fwd(cs_all, ada_w[0], ada_b_cols)
    mod_all = _exchange("gather_mod", [mod_cols], ["gather"])[0]
    mod = lax.dynamic_index_in_dim(mod_all, me, axis=1, keepdims=False).reshape(1, 6 * d)

    tm = min(256, s)
    h, g_in = _pre_norm(x2, pre_mix_g, mod, tm, riders=([g_in], ["forward"]))
    wt_in = g_in.reshape(d_in, d)
    proj, g_ff2 = _matmul("in_proj", h, wt_in, "nt", s, 512, BF16, riders=([bf(w_ff2)], ["gather_chip"]))
    pos_col = positions.reshape(s, 1).astype(F32)
    freqs = ROPE_BASE ** (-jnp.arange(0, RET_DQK, 2, dtype=F32) / RET_DQK)
    inv_freq = jnp.tile(freqs, LANES // (RET_DQK // 2)).reshape(1, LANES)
    log_gamma_np = np.log1p(-(2.0 ** (-5.0 - np.arange(HEADS))))
    log_gamma = jnp.asarray(log_gamma_np, F32)
    lg_lanes = jnp.asarray(np.repeat(log_gamma_np, RET_DQK).reshape(1, RET_QK), F32)
    idx_col = (jnp.arange(s, dtype=F32) - (s // 2)).reshape(s, 1)
    qk_rot, v_bf, qkv_sb, cos_t, sin_t = _prep(proj, pos_col, idx_col, inv_freq, lg_lanes, tm)
    tq, tk = min(256, s), min(128, s)
    later = [bf(w_ret_branch), bf(w_sb_branch), bf(w_out), bf(w_ff1)]
    sb, sb_weights, *later = _sb_fwd(qkv_sb, tq, tk, riders=(later, ["gather_chip"] * 4))
    ret, retg, g_ret, g_sb, g_out, g_ff1, g_ff2 = _ret_fwd(qk_rot, v_bf, proj, ret_gn_g, log_gamma, tq,
                                                           riders=(later + [g_ff2], ["forward"] * 5))
    wf_ret = g_ret.reshape(RET_V, d)
    wf_sb = jnp.moveaxis(g_sb, 0, 1).reshape(SB_W, d)
    wf_out = g_out.reshape(d, d)
    wf_ff1 = jnp.moveaxis(g_ff1, 0, 1).reshape(d, d_ff)
    wf_ff2 = g_ff2.reshape(d_ff, d)
    mixed, r_bf, s_bf = _merge(retg, sb, wf_ret, wf_sb, proj, tm, min(512, d))
    y, hres, h2 = _out_proj(mixed, wf_out, x2, mod, post_mix_g, pre_ffn_g, tm)
    u, act = _ff1(h2, wf_ff1, s, 512)
    dout, df, loss_sum, d_gt2, d_gp2 = _ff2_loss(act, wf_ff2, hres, tgt, mod, post_ffn_g, tm)

    du = _ff2_bwd(df, wf_ff2, u, s, 512)
    gw_ff2 = _matmul("grad_w_ff2", act, df, "tn", 512, d, BF16).reshape(N_DEV, d_ff // N_DEV, d)
    gw_ff1 = _matmul("grad_w_ff1", h2, du, "tn", d, d_ff // N_DEV, BF16, blocked_out=True)
    dhres, dy, d_sh2, d_sc2, d_g2, d_gt1, d_gp1, t_ff1, t_ff2 = _ff1_bwd(
        du, wf_ff1, hres, dout, y, mod, pre_ffn_g, post_mix_g, tm, riders=([gw_ff1, gw_ff2], ["pair"] * 2))
    s_ff1 = _pair_sum("pair_sum_ff1", gw_ff1, t_ff1, core, 256)
    s_ff2 = _pair_sum("pair_sum_ff2", gw_ff2, t_ff2, core, 256)
    d_r, d_s, da_r, da_s = _out_bwd(dy, wf_out, proj, r_bf, s_bf, tm, min(512, d))
    gw_out = _matmul("grad_w_out", mixed, dy, "tn", 512, d, BF16).reshape(N_DEV, d // N_DEV, d)
    dretg = _matmul("ret_branch_bwd", d_r, wf_ret, "nt", s, 512, BF16)
    dsb = _matmul("sb_branch_bwd", d_s, wf_sb, "nt", s, 512, F32)
    gw_ret = _matmul("grad_w_ret", retg, d_r, "tn", 512, d, BF16).reshape(N_DEV, RET_V // N_DEV, d)
    gw_sb = _matmul("grad_w_sb", sb, d_s, "tn", 512, d // N_DEV, BF16, blocked_out=True)
    dq_s, dk_s, dv_s, p_ff1, p_ff2 = _sb_bwd(qkv_sb, sb_weights, dsb, tq, tk,
                                             riders=([s_ff1, s_ff2], ["chip_scatter"] * 2))
    dg_r, dret, d_gn = _gn_bwd(dretg, ret, proj, ret_gn_g, tm)
    dq_r, dk_r, dv_r, p_out, p_ret, p_sb = _ret_bwd(qk_rot, v_bf, dret, log_gamma, tq,
                                                    riders=([gw_out, gw_ret, gw_sb], ["scatter"] * 3))
    dproj = _assemble_dproj(dq_r, dk_r, dv_r, dg_r, dq_s, dk_s, dv_s, da_r, da_s, cos_t, sin_t, idx_col, lg_lanes, tm)
    gw_in = _matmul("grad_w_in", dproj, h, "tn", 512, d, BF16).reshape(N_DEV, d_in // N_DEV, d)
    t_in = _exchange("pair_in", [gw_in], ["pair"])[0]
    tr_in = d_in // N_DEV // 4
    s_in = _pair_sum("pair_sum_in", gw_in, t_in, core, tr_in)
    grad_x, d_sh1, d_sc1, d_g1, p_in = _in_bwd(dproj, wt_in, x2, dhres, mod, pre_mix_g, tm,
                                               riders=([s_in], ["chip_scatter"]))
    loss_lanes = jnp.pad(loss_sum, ((0, 0), (0, LANES - 1)))
    small = jnp.concatenate([d_sh1, d_sc1, d_gt1, d_sh2, d_sc2, d_gt2, d_g1, d_gp1, d_g2, d_gp2, d_gn, loss_lanes], axis=1)
    small_all = _exchange("gather_small", [small], ["gather"])[0].reshape(N_DEV, small.shape[1])
    parts = [p_in, p_ret, p_sb, p_out, p_ff1, p_ff2]

    res = {}
    names = ["w_ret_branch", "w_sb_branch", "w_out", "w_ff1", "w_ff2"]
    ws = [w_ret_branch, w_sb_branch, w_out, w_ff1, w_ff2]
    ms = [m_w_ret_branch, m_w_sb_branch, m_w_out, m_w_ff1, m_w_ff2]
    vs = [v_w_ret_branch, v_w_sb_branch, v_w_out, v_w_ff1, v_w_ff2]
    for nm, p, w, m, v in zip(names, parts[1:], ws, ms, vs):
        res[nm] = [o[None] for o in _adam_reduce("adam_" + nm, p, w[0], m[0], v[0], 256)]
    res["w_in"] = [jnp.swapaxes(o, 0, 1)[None]
                   for o in _adam_reduce("adam_w_in", parts[0], w_in_t, m_in_t, v_in_t, tr_in)]
    dmod_cols = lax.dynamic_slice(small_all, (0, me * n_ada), (N_DEV, n_ada))
    res["ada_w"] = [o[None] for o in _ada_bwd_adam(cs_all.reshape(N_DEV, d, 1), dmod_cols, ada_w[0], m_ada_w[0], v_ada_w[0])]
    vec_names = ["ada_b", "pre_mix_g", "post_mix_g", "pre_ffn_g", "post_ffn_g", "ret_gn_g"]
    cat = lambda xs: jnp.concatenate(xs + [jnp.zeros((1, LANES), F32)], axis=1)
    packed = _small_adam(small_all,
                         cat([ada_b, pre_mix_g, post_mix_g, pre_ffn_g, post_ffn_g, ret_gn_g]),
                         cat([m_ada_b, m_pre_mix_g, m_post_mix_g, m_pre_ffn_g, m_post_ffn_g, m_ret_gn_g]),
                         cat([v_ada_b, v_pre_mix_g, v_post_mix_g, v_pre_ffn_g, v_post_ffn_g, v_ret_gn_g]))
    off = 0
    for nm, width in zip(vec_names, [6 * d, d, d, d, d, RET_V]):
        res[nm] = [p[:, off:off + width] for p in packed]
        off += width

    loss = (0.5 / d) * packed[0][0, off]
    order = ["ada_w", "ada_b", "pre_mix_g", "post_mix_g", "pre_ffn_g", "post_ffn_g", "w_in", "ret_gn_g",
             "w_ret_branch", "w_sb_branch", "w_out", "w_ff1", "w_ff2"]
    outs = [loss, grad_x[None]]
    for k in range(4):
        outs += [res[nm][k] for nm in order]
    return tuple(outs)
```

```python
import functools

import numpy as np
import jax
import jax.numpy as jnp
from jax import lax
from jax.experimental import pallas as pl
from jax.experimental.pallas import tpu as pltpu

F32 = jnp.float32
BF16 = jnp.bfloat16
N_DEV = 8
AXES = ("x", "y", "c")

EPS = 1e-6
CHUNK = 64
CHUNK_SHIFT = 6
HEADS = 8
RET_DQK = 64
RET_DV = 128
SB_DH = 64
RET_QK = HEADS * RET_DQK
RET_V = HEADS * RET_DV
SB_W = HEADS * SB_DH
ROPE_BASE = 10000.0
LANES = 128

ADAM_LR = 0.001
ADAM_B1 = 0.9
ADAM_B2 = 0.999
ADAM_EPS = 1e-08
ADAM_WD = 0.01
ADAM_STEP = 10

VMEM_LIMIT = 56 * 1024 * 1024

_NN = (((1,), (0,)), ((), ()))
_NT = (((1,), (1,)), ((), ()))
_TN = (((0,), (0,)), ((), ()))


def _dot(a, b, dims=_NN):
    if a.dtype != BF16:
        a = a.astype(BF16)
    if b.dtype != BF16:
        b = b.astype(BF16)
    return lax.dot_general(a, b, dims, preferred_element_type=F32)


def _dot_split(a, b):
    hi = a.astype(BF16)
    lo = (a - hi.astype(F32)).astype(BF16)
    return (lax.dot_general(hi, b, _NN, preferred_element_type=F32)
            + lax.dot_general(lo, b, _NN, preferred_element_type=F32))


def _sigmoid(x):
    return 1.0 / (1.0 + jnp.exp(-x))


def _rms(x, d):
    r = lax.rsqrt(jnp.sum(x * x, axis=1, keepdims=True) * (1.0 / d) + EPS)
    return x * r, r


def _rms_bwd(dn, n, r, d):
    return r * (dn - n * (jnp.sum(dn * n, axis=1, keepdims=True) * (1.0 / d)))


def _colsum(v):
    return jnp.sum(v, axis=0, keepdims=True)


def _accum(ref, val, first):
    @pl.when(first)
    def _():
        ref[...] = val

    @pl.when(jnp.logical_not(first))
    def _():
        ref[...] += val


ROW_SPLIT = 2


def _zero_at_start(refs):
    @pl.when(pl.program_id(0) == 0)
    def _():
        for r in refs:
            r[...] = jnp.zeros_like(r)


def _pieces(tm):
    step = tm // ROW_SPLIT
    return [slice(k * step, (k + 1) * step) for k in range(ROW_SPLIT)]


KIND_SLOTS = {"gather": N_DEV, "scatter": N_DEV, "gather_chip": N_DEV, "forward": N_DEV, "pair": N_DEV // 2,
              "chip_scatter": N_DEV // 2}
SEMS_PER_ARRAY = N_DEV - 1


def _exchange_copies(ins, outs, send_sems, recv_sems, local_sems, kinds):
    x, y, c = (lax.axis_index(a) for a in AXES)
    me, chip, sibling = 4 * x + 2 * y + c, 2 * x + y, (x, y, 1 - c)
    mesh_id = pl.DeviceIdType.MESH
    other_chips = []
    for k in range(1, N_DEV // 2):
        px = 1 - x if k & 2 else x
        py = 1 - y if k & 1 else y
        other_chips.append((px, py))
    copies = []
    for i, kind in enumerate(kinds):
        def remote(src, dst, k, to, i=i):
            return pltpu.make_async_remote_copy(
                src_ref=src, dst_ref=dst, send_sem=send_sems.at[i * SEMS_PER_ARRAY + k],
                recv_sem=recv_sems.at[i * SEMS_PER_ARRAY + k], device_id=to, device_id_type=mesh_id)

        if kind in ("gather", "scatter"):
            pick = (lambda ref, d: ref.at[d]) if kind == "scatter" else (lambda ref, d: ref)
            copies.append(pltpu.make_async_copy(pick(ins[i], me), outs[i].at[me], local_sems.at[i]))
            for k in range(1, N_DEV):
                to = (1 - x if k & 4 else x, 1 - y if k & 2 else y, 1 - c if k & 1 else c)
                copies.append(remote(pick(ins[i], 4 * to[0] + 2 * to[1] + to[2]), outs[i].at[me], k - 1, to))
        elif kind == "gather_chip":
            copies.append(pltpu.make_async_copy(ins[i], outs[i].at[me], local_sems.at[i]))
            copies.append(remote(ins[i], outs[i].at[me], 0, sibling))
            for k, (px, py) in enumerate(other_chips):
                copies.append(remote(ins[i], outs[i].at[me], 1 + k, (px, py, c)))
        elif kind == "forward":
            for k, (px, py) in enumerate(other_chips):
                slot = 4 * px + 2 * py + c
                copies.append(remote(outs[i].at[slot], outs[i].at[slot], k, sibling))
        elif kind == "pair":
            for k in range(N_DEV // 2):
                copies.append(remote(ins[i].at[2 * k + 1 - c], outs[i].at[k], k, sibling))
        elif kind == "chip_scatter":
            copies.append(pltpu.make_async_copy(ins[i].at[chip], outs[i].at[chip], local_sems.at[i]))
            for k, (px, py) in enumerate(other_chips):
                copies.append(remote(ins[i].at[2 * px + py], outs[i].at[chip], k, (px, py, c)))
        else:
            raise ValueError(kind)
    return copies


def _exchange_shapes(arrays, kinds):
    shapes = []
    for a, kind in zip(arrays, kinds):
        tail = a.shape if kind in ("gather", "gather_chip") else a.shape[1:]
        shapes.append(jax.ShapeDtypeStruct((KIND_SLOTS[kind],) + tuple(tail), a.dtype))
    return shapes


def _exchange_sems(n):
    return [pltpu.SemaphoreType.DMA((n * SEMS_PER_ARRAY,)), pltpu.SemaphoreType.DMA((n * SEMS_PER_ARRAY,)),
            pltpu.SemaphoreType.DMA((n,))]


def _call(name, body, grid, ins, outs, scratch=(), riders=None, prefetch=None):
    any_spec = pl.BlockSpec(memory_space=pl.ANY)
    in_specs = [pl.BlockSpec(memory_space=im) if bs is None else pl.BlockSpec(bs, im) for _, bs, im in ins]
    out_specs = [pl.BlockSpec(bs, im) for _, _, bs, im in outs]
    out_shape = [jax.ShapeDtypeStruct(s, d) for s, d, _, _ in outs]
    operands = [a for a, _, _ in ins]
    scratch = list(scratch)
    aliases = {}
    n_pre = 0 if prefetch is None else 1
    kernel = functools.partial(body) if prefetch is None else (lambda _, *refs: body(*refs))
    if riders is not None:
        arrays, kinds = riders
        nr, n_in, n_out, n_scr = len(arrays), len(ins), len(outs), len(scratch)

        def kernel(*refs):
            refs = refs[n_pre:]
            own_in, ride_in = refs[:n_in], refs[n_in:n_in + nr]
            own_out = refs[n_in + nr:n_in + nr + n_out]
            ride_out = refs[n_in + nr + n_out:n_in + 2 * nr + n_out]
            own_scr = refs[n_in + 2 * nr + n_out:n_in + 2 * nr + n_out + n_scr]
            sems = refs[n_in + 2 * nr + n_out + n_scr:]
            ids = [pl.program_id(a) for a in range(len(grid))]
            first = functools.reduce(jnp.logical_and, [i == 0 for i in ids])
            last = functools.reduce(jnp.logical_and, [i == g - 1 for i, g in zip(ids, grid)])

            @pl.when(first)
            def _():
                for cp in _exchange_copies(ride_in, ride_out, *sems, kinds):
                    cp.start()

            body(*own_in, *own_out, *own_scr)

            @pl.when(last)
            def _():
                for cp in _exchange_copies(ride_in, ride_out, *sems, kinds):
                    cp.wait()

        in_specs += [any_spec] * nr
        out_specs += [any_spec] * nr
        out_shape += _exchange_shapes(arrays, kinds)
        operands += list(arrays)
        scratch += _exchange_sems(nr)
        aliases = {n_pre + n_in + r: n_out + r for r, kind in enumerate(kinds) if kind == "forward"}
    params = pltpu.CompilerParams(dimension_semantics=("arbitrary",) * len(grid), vmem_limit_bytes=VMEM_LIMIT)
    if prefetch is None:
        return pl.pallas_call(kernel, name=name, grid=grid, in_specs=in_specs, out_specs=out_specs,
                              out_shape=out_shape, scratch_shapes=scratch, input_output_aliases=aliases,
                              compiler_params=params)(*operands)
    grid_spec = pltpu.PrefetchScalarGridSpec(num_scalar_prefetch=1, grid=grid, in_specs=in_specs,
                                             out_specs=out_specs, scratch_shapes=scratch)
    return pl.pallas_call(kernel, name=name, grid_spec=grid_spec, out_shape=out_shape,
                          input_output_aliases=aliases, compiler_params=params)(prefetch, *operands)


def _exchange(name, arrays, kinds):
    n = len(arrays)

    def body(*refs):
        copies = _exchange_copies(refs[:n], refs[n:2 * n], *refs[2 * n:], kinds)
        for cp in copies:
            cp.start()
        for cp in copies:
            cp.wait()

    any_spec = pl.BlockSpec(memory_space=pl.ANY)
    return pl.pallas_call(
        functools.partial(body),
        name=name,
        in_specs=[any_spec] * n,
        out_specs=[any_spec] * n,
        out_shape=_exchange_shapes(arrays, kinds),
        scratch_shapes=_exchange_sems(n),
        input_output_aliases={i: i for i, kind in enumerate(kinds) if kind == "forward"},
    )(*arrays)


def _pair_sum(name, mine, theirs, my_core, tr):
    _, rws, cls = mine.shape
    tr = min(tr, rws)

    def body(a_ref, b_ref, o_ref):
        o_ref[...] = (a_ref[...].astype(F32) + b_ref[...].astype(F32)).astype(o_ref.dtype)

    return _call(name, body, (N_DEV // 2, rws // tr),
                 [(mine, (None, tr, cls), lambda k, r, core: (2 * k + core[0], r, 0)),
                  (theirs, (None, tr, cls), lambda k, r, core: (k, r, 0))],
                 [((N_DEV // 2, rws, cls), mine.dtype, (None, tr, cls), lambda k, r, core: (k, r, 0))],
                 prefetch=my_core)[0]


def _matmul(name, a, b, kind, tm, tn, out_dtype, blocked_out=False, riders=None):
    if kind == "tn":
        kdim, m = a.shape
    else:
        m, kdim = a.shape
    n = b.shape[0] if kind == "nt" else b.shape[1]
    tm, tn = min(tm, m), min(tn, n)
    dims = {"nn": _NN, "nt": _NT, "tn": _TN}[kind]

    def body(a_ref, b_ref, o_ref):
        o_ref[...] = _dot(a_ref[...], b_ref[...], dims).astype(o_ref.dtype)

    a_spec = (a, (kdim, tm), lambda j, i: (0, i)) if kind == "tn" else (a, (tm, kdim), lambda j, i: (i, 0))
    b_spec = (b, (tn, kdim), lambda j, i: (j, 0)) if kind == "nt" else (b, (kdim, tn), lambda j, i: (0, j))
    if blocked_out:
        out = ((n // tn, m, tn), out_dtype, (None, tm, tn), lambda j, i: (j, i, 0))
    else:
        out = ((m, n), out_dtype, (tm, tn), lambda j, i: (i, j))
    res = _call(name, body, (n // tn, m // tm), [a_spec, b_spec], [out], riders=riders)
    return res[0] if riders is None else res


def _ada_fwd(cs_all, ada_w, ada_b_cols):
    def body(c_ref, w_ref, b_ref, o_ref):
        o_ref[...] = lax.dot_general(c_ref[...], w_ref[...], _NN, preferred_element_type=F32,
                                     precision=lax.Precision.HIGHEST) + b_ref[...]

    r, d = cs_all.shape
    nc = ada_w.shape[1]
    return _call("ada_fwd", body, (1,),
                 [(cs_all, (r, d), lambda i: (0, 0)), (ada_w, (d, nc), lambda i: (0, 0)),
                  (ada_b_cols, (1, nc), lambda i: (0, 0))],
                 [((r, nc), F32, (r, nc), lambda i: (0, 0))])[0]


def _silu_rows(c_all):
    def body(c_ref, o_ref):
        v = c_ref[...]
        o_ref[...] = v * _sigmoid(v)

    return _call("silu_c", body, (1,), [(c_all, c_all.shape, lambda i: (0, 0))],
                 [(c_all.shape, F32, c_all.shape, lambda i: (0, 0))])[0]


def _pre_norm(x, g, mod, tm, riders=None):
    s, d = x.shape

    def body(x_ref, g_ref, mod_ref, h_ref):
        n, _ = _rms(x_ref[...], d)
        sh, sc = mod_ref[:, 0:d], mod_ref[:, d:2 * d]
        h_ref[...] = (n * g_ref[...] * (1.0 + sc) + sh).astype(BF16)

    return _call("pre_norm", body, (s // tm,),
                 [(x, (tm, d), lambda i: (i, 0)), (g, (1, d), lambda i: (0, 0)),
                  (mod, (1, 6 * d), lambda i: (0, 0))],
                 [((s, d), BF16, (tm, d), lambda i: (i, 0))], riders=riders)


LOG2E = 1.4426950408889634
LN2 = 0.6931471805599453


def _decay_scale(lg_ref, idx, g, sign):
    return jnp.exp((sign * idx) * lg_ref[:, g * LANES:(g + 1) * LANES])


def _prep(proj, pos_col, idx_col, inv_freq, lg_lanes, tm):
    s = proj.shape[0]
    sb_off = (2 * RET_QK + 2 * RET_V) // (3 * SB_W)
    n_q = RET_QK // LANES

    def body(qk_ref, v_ref, sb_ref, pos_ref, idx_ref, f_ref, lg_ref, qk_out, v_out, sb_out, cos_out, sin_out):
        ang = pos_ref[...] * f_ref[...]
        lane = lax.broadcasted_iota(jnp.int32, (1, LANES), 1)
        first = jnp.bitwise_and(lane, RET_DQK - 1) < (RET_DQK // 2)
        cos = jnp.cos(ang)
        sin = jnp.where(first, -1.0, 1.0) * jnp.sin(ang)
        cos_out[...] = cos
        sin_out[...] = sin
        idx = idx_ref[...]
        for g in range(2 * n_q):
            v = qk_ref[:, g * LANES:(g + 1) * LANES].astype(F32)
            sw = jnp.where(first, pltpu.roll(v, LANES - RET_DQK // 2, 1), pltpu.roll(v, RET_DQK // 2, 1))
            r = v * cos + sw * sin
            if g < n_q:
                r = r * _decay_scale(lg_ref, idx, g, 1.0)
            else:
                r = r * (_decay_scale(lg_ref, idx, g - n_q, -1.0) * (RET_DQK ** -0.5))
            qk_out[:, g * LANES:(g + 1) * LANES] = r.astype(BF16)
        v_out[...] = v_ref[...].astype(BF16)
        sb_out[:, 0:SB_W] = (sb_ref[:, 0:SB_W].astype(F32) * (SB_DH ** -0.5 * LOG2E)).astype(BF16)
        sb_out[:, SB_W:3 * SB_W] = sb_ref[:, SB_W:3 * SB_W].astype(BF16)

    return _call("prep", body, (s // tm,),
                 [(proj, (tm, 2 * RET_QK), lambda i: (i, 0)),
                  (proj, (tm, RET_V), lambda i: (i, 2 * RET_QK // RET_V)),
                  (proj, (tm, 3 * SB_W), lambda i: (i, sb_off)),
                  (pos_col, (tm, 1), lambda i: (i, 0)),
                  (idx_col, (tm, 1), lambda i: (i, 0)),
                  (inv_freq, (1, LANES), lambda i: (0, 0)),
                  (lg_lanes, (1, RET_QK), lambda i: (0, 0))],
                 [((s, 2 * RET_QK), BF16, (tm, 2 * RET_QK), lambda i: (i, 0)),
                  ((s, RET_V), BF16, (tm, RET_V), lambda i: (i, 0)),
                  ((s, 3 * SB_W), BF16, (tm, 3 * SB_W), lambda i: (i, 0)),
                  ((s, LANES), F32, (tm, LANES), lambda i: (i, 0)),
                  ((s, LANES), F32, (tm, LANES), lambda i: (i, 0))])


def _head_mask(hh):
    lane = lax.broadcasted_iota(jnp.int32, (1, LANES), 1)
    return (lane >= RET_DQK) if hh else (lane < RET_DQK)


def _masked(v, m):
    return jnp.where(m, v, jnp.zeros_like(v))


SB_GROUP = 4
RET_GROUP = 4


def _stack_heads(v):
    return jnp.concatenate([_masked(v, _head_mask(0)), _masked(v, _head_mask(1))], axis=0)


def _side_by_side(v, t):
    return jnp.concatenate([v[:t], v[t:]], axis=1)


def _split_bf16(v):
    hi = v.astype(BF16)
    lo = (v - hi.astype(F32)).astype(BF16)
    return jnp.concatenate([hi, lo], axis=1)


def _tile_pos(i, j, tq, tk):
    row = jnp.bitwise_and(lax.broadcasted_iota(jnp.int32, (2 * tq, tk), 0), tq - 1) + i * tq
    col = lax.broadcasted_iota(jnp.int32, (2 * tq, tk), 1) + j * tk
    return row, col


def _n_groups(i, tq, tk, grp):
    return ((i + 1) * (tq // tk) + grp - 1) // grp


def _n_full(i, tq, tk, grp):
    return (i * (tq // tk)) // grp


def _key_rows(j, tk):
    return pl.ds(pl.multiple_of(j * tk, tk), tk)


def _ret_weight(lg_rows, i, j, tq, tk):
    row, col = _tile_pos(i, j, tq, tk)
    same = jnp.right_shift(col, CHUNK_SHIFT) == jnp.right_shift(row, CHUNK_SHIFT)
    later = jnp.where(same, jnp.exp((2.0 * lg_rows) * (col - row).astype(F32)), 0.0)
    return jnp.where(col <= row, 1.0, later)


def _lg_rows(lg_ref, hp, tq):
    first = lax.broadcasted_iota(jnp.int32, (2 * tq, 1), 0) < tq
    return jnp.where(first, lg_ref[2 * hp], lg_ref[2 * hp + 1])


def _check_tiles(s, tq, tk, grp):
    assert tq % tk == 0 and tq & (tq - 1) == 0 and tk & (tk - 1) == 0
    assert s % tq == 0 and (s // tk) % grp == 0 and s // tk <= LANES


def _pair_mask():
    r = lax.broadcasted_iota(jnp.int32, (LANES, 2 * RET_DV), 0) >= RET_DQK
    c = lax.broadcasted_iota(jnp.int32, (LANES, 2 * RET_DV), 1) >= RET_DV
    return (r == c).astype(F32)


def _ret_block(lg_ref, hp, i, t, qb, kb):
    w = _ret_weight(_lg_rows(lg_ref, hp, t), i, i, t, t)
    return _dot(_stack_heads(qb), kb, _NT), w


def _ret_fwd(qk_rot, v_bf, proj, gn_g, log_gamma, t, riders=None):
    s = qk_rot.shape[0]
    gate_off = (2 * RET_QK + RET_V) // (2 * RET_DV)
    n_pair = HEADS // 2
    pw = 2 * RET_DV
    assert s % t == 0 and t % CHUNK == 0 and t & (t - 1) == 0

    def body(lg_ref, q_ref, k_ref, v_ref, g_ref, w_ref, ret_ref, rg_ref, state_ref):
        hp, i = pl.program_id(0), pl.program_id(1)

        @pl.when(i == 0)
        def _():
            state_ref[...] = jnp.zeros_like(state_ref)

        qb, kb, vb = q_ref[...], k_ref[...], v_ref[...]
        z, w = _ret_block(lg_ref, hp, i, t, qb, kb)
        p = (z * w).astype(BF16)
        intra = jnp.concatenate([_dot(p[:t], vb[:, 0:RET_DV]), _dot(p[t:], vb[:, RET_DV:pw])], axis=1)
        out = intra + _dot(qb, state_ref[...])
        state_ref[...] += _pair_mask() * _dot(kb, vb, _TN)
        ret_ref[...] = out
        for hh in range(2):
            cols = slice(hh * RET_DV, (hh + 1) * RET_DV)
            o = out[:, cols]
            mu = jnp.sum(o, axis=1, keepdims=True) * (1.0 / RET_DV)
            xc = o - mu
            var = jnp.sum(xc * xc, axis=1, keepdims=True) * (1.0 / RET_DV)
            nrm = xc * lax.rsqrt(var + EPS) * w_ref[:, cols]
            g = g_ref[:, cols].astype(F32)
            rg_ref[:, cols] = (g * _sigmoid(g) * nrm).astype(BF16)

    blk = lambda hp, i: (i, hp)
    return _call("ret_fwd", body, (n_pair, s // t),
                 [(log_gamma, None, pltpu.SMEM),
                  (qk_rot, (t, LANES), blk),
                  (qk_rot, (t, LANES), lambda hp, i: (i, n_pair + hp)),
                  (v_bf, (t, pw), blk),
                  (proj, (t, pw), lambda hp, i: (i, gate_off + hp)),
                  (gn_g, (1, pw), lambda hp, i: (0, hp))],
                 [((s, RET_V), F32, (t, pw), blk), ((s, RET_V), BF16, (t, pw), blk)],
                 scratch=[pltpu.VMEM((LANES, pw), F32)], riders=riders)


def _tri2(tk, strict_upper):
    r = jnp.bitwise_and(lax.broadcasted_iota(jnp.int32, (2 * tk, tk), 0), tk - 1)
    cc = lax.broadcasted_iota(jnp.int32, (2 * tk, tk), 1)
    return ((r > cc) if strict_upper else (r < cc)).astype(BF16)


def _sb_valid(i, j, tq, tk):
    row, col = _tile_pos(i, j, tq, tk)
    return col < row


def _sb_fwd(qkv, tq, tk, riders=None):
    s = qkv.shape[0]
    n_pair = HEADS // 2
    _check_tiles(s, tq, tk, SB_GROUP)

    def body(q_ref, k_ref, v_ref, o_ref, a_ref):
        i = pl.program_id(1)
        upper2 = _tri2(tk, True)
        qs = _stack_heads(q_ref[...])
        n_full, n_groups = _n_full(i, tq, tk, SB_GROUP), _n_groups(i, tq, tk, SB_GROUP)

        def make_step(near_diagonal, last):
            def step(n, carry):
                c, o = carry
                g = last - 1 - n
                js = [g * SB_GROUP + sub for sub in range(SB_GROUP)]
                zs = [_dot(qs, k_ref[_key_rows(j, tk), :], _NT) for j in js]
                log1ps = [jnp.log2(1.0 + jnp.exp2(-jnp.abs(z))) for z in zs]
                log_1ms = [-jnp.maximum(z, 0.0) - t for z, t in zip(zs, log1ps)]
                log_bs = [jnp.minimum(z, 0.0) - t for z, t in zip(zs, log1ps)]
                if near_diagonal:
                    valids = [_sb_valid(i, j, tq, tk) for j in js]
                    log_1ms = [jnp.where(v, l, 0.0) for v, l in zip(valids, log_1ms)]
                sticks = [lax.dot_general(_split_bf16(l), upper2, _NN, preferred_element_type=F32) for l in log_1ms]
                sums = [jnp.sum(l, axis=1, keepdims=True) for l in log_1ms]
                cs = [None] * SB_GROUP
                for sub in reversed(range(SB_GROUP)):
                    cs[sub] = c
                    c = c + sums[sub]
                for sub, j in enumerate(js):
                    a = jnp.exp2(log_bs[sub] + sticks[sub] + cs[sub])
                    if near_diagonal:
                        a = jnp.where(valids[sub], a, 0.0)
                    a = a.astype(BF16)
                    a_ref[j] = a
                    o = o + _dot(_side_by_side(a, tq), _stack_heads(v_ref[_key_rows(j, tk), :]))
                return c, o
            return step

        carry = (jnp.zeros((2 * tq, 1), F32), jnp.zeros((tq, LANES), F32))
        carry = lax.fori_loop(0, n_groups - n_full, make_step(True, n_groups), carry)
        _, acc = lax.fori_loop(0, n_full, make_step(False, n_full), carry)
        o_ref[...] = acc

    n_kb = s // tk
    return _call("sb_fwd", body, (n_pair, s // tq),
                 [(qkv, (tq, LANES), lambda hp, i: (i, hp)),
                  (qkv, (s, LANES), lambda hp, i: (0, n_pair + hp)),
                  (qkv, (s, LANES), lambda hp, i: (0, 2 * n_pair + hp))],
                 [((s, SB_W), F32, (tq, LANES), lambda hp, i: (i, hp)),
                  ((n_pair, s // tq, n_kb, 2 * tq, tk), BF16, (None, None, n_kb, 2 * tq, tk),
                   lambda hp, i: (hp, i, 0, 0, 0))], riders=riders)


def _merge(retg, sb, w_ret, w_sb_t, proj, tm, tn, riders=None):
    s, d = retg.shape[0], w_ret.shape[1]
    ar_off = (2 * RET_QK + 2 * RET_V + 3 * SB_W) // tn
    as_off = ar_off + d // tn

    def body(rg_ref, sb_ref, wr_ref, ws_ref, ar_ref, as_ref, mix_ref, r_ref, s_ref):
        rr = _dot(rg_ref[...], wr_ref[...])
        ss = _dot(sb_ref[...], ws_ref[...], _NT)
        mix_ref[...] = (_sigmoid(ar_ref[...].astype(F32)) * rr + _sigmoid(as_ref[...].astype(F32)) * ss).astype(BF16)
        r_ref[...] = rr.astype(BF16)
        s_ref[...] = ss.astype(BF16)

    tile = (tm, tn)
    return _call("merge", body, (d // tn, s // tm),
                 [(retg, (tm, RET_V), lambda j, i: (i, 0)), (sb, (tm, SB_W), lambda j, i: (i, 0)),
                  (w_ret, (RET_V, tn), lambda j, i: (0, j)), (w_sb_t, (tn, SB_W), lambda j, i: (j, 0)),
                  (proj, tile, lambda j, i: (i, ar_off + j)), (proj, tile, lambda j, i: (i, as_off + j))],
                 [((s, d), BF16, tile, lambda j, i: (i, j))] * 3, riders=riders)


def _out_proj(mixed, w_out, x, mod, gp1, g2, tm):
    s, d = x.shape

    def body(a_ref, w_ref, x_ref, mod_ref, gp_ref, g2_ref, y_ref, hres_ref, h2_ref):
        for rows in _pieces(tm):
            y = _dot(a_ref[rows, :], w_ref[...])
            y_ref[rows, :] = y
            ny, _ = _rms(y, d)
            hres = x_ref[rows, :] + mod_ref[:, 2 * d:3 * d] * (ny * gp_ref[...])
            hres_ref[rows, :] = hres
            n2, _ = _rms(hres, d)
            h2_ref[rows, :] = (n2 * g2_ref[...] * (1.0 + mod_ref[:, 4 * d:5 * d]) + mod_ref[:, 3 * d:4 * d]).astype(BF16)

    row = lambda i: (i, 0)
    fix = lambda i: (0, 0)
    return _call("out_proj", body, (s // tm,),
                 [(mixed, (tm, d), row), (w_out, (d, d), fix), (x, (tm, d), row),
                  (mod, (1, 6 * d), fix), (gp1, (1, d), fix), (g2, (1, d), fix)],
                 [((s, d), F32, (tm, d), row), ((s, d), F32, (tm, d), row), ((s, d), BF16, (tm, d), row)])


def _ff1(h2, w_ff1_t, tm, tn):
    s, f = h2.shape[0], w_ff1_t.shape[0]
    tm = min(tm, s)

    def body(a_ref, w_ref, u_ref, act_ref):
        u = _dot(a_ref[...], w_ref[...], _NT)
        r = jnp.maximum(u, 0.0)
        u_ref[...] = u.astype(BF16)
        act_ref[...] = (r * r).astype(BF16)

    d = h2.shape[1]
    return _call("ff1", body, (f // tn, s // tm),
                 [(h2, (tm, d), lambda j, i: (i, 0)), (w_ff1_t, (tn, d), lambda j, i: (j, 0))],
                 [((s, f), BF16, (tm, tn), lambda j, i: (i, j))] * 2)


def _ff2_loss(act, w_ff2, hres, target, mod, gp2, tm):
    s, d = hres.shape
    f = act.shape[1]

    def body(a_ref, w_ref, h_ref, t_ref, mod_ref, gp_ref, dout_ref, df_ref, loss_ref, dgt_ref, dgp_ref):
        _zero_at_start([loss_ref, dgt_ref, dgp_ref])
        gt, gp = mod_ref[:, 5 * d:6 * d], gp_ref[...]
        for rows in _pieces(tm):
            ff = _dot(a_ref[rows, :], w_ref[...])
            nf, rf = _rms(ff, d)
            out = h_ref[rows, :] + gt * (nf * gp)
            err = out - t_ref[rows, :]
            sq = jnp.sum(err * err, axis=1, keepdims=True)
            loss_ref[...] += jnp.sum(sq, axis=0, keepdims=True)
            dout = err * (1.0 / d)
            dout_ref[rows, :] = dout
            dgt_ref[...] += _colsum(dout * (nf * gp))
            dgp_ref[...] += _colsum(dout * gt * nf)
            df_ref[rows, :] = _rms_bwd(dout * gt * gp, nf, rf, d).astype(BF16)

    row = lambda i: (i, 0)
    fix = lambda i: (0, 0)
    return _call("ff2_loss", body, (s // tm,),
                 [(act, (tm, f), row), (w_ff2, (f, d), fix), (hres, (tm, d), row), (target, (tm, d), row),
                  (mod, (1, 6 * d), fix), (gp2, (1, d), fix)],
                 [((s, d), F32, (tm, d), row), ((s, d), BF16, (tm, d), row), ((1, 1), F32, (1, 1), fix),
                  ((1, d), F32, (1, d), fix), ((1, d), F32, (1, d), fix)])


def _ff2_bwd(df, w_ff2, u, tm, tn):
    s, d = df.shape
    f = w_ff2.shape[0]
    tm = min(tm, s)

    def body(a_ref, w_ref, u_ref, du_ref):
        da = _dot(a_ref[...], w_ref[...], _NT)
        du_ref[...] = (da * (2.0 * jnp.maximum(u_ref[...].astype(F32), 0.0))).astype(BF16)

    return _call("ff2_bwd", body, (f // tn, s // tm),
                 [(df, (tm, d), lambda j, i: (i, 0)), (w_ff2, (tn, d), lambda j, i: (j, 0)),
                  (u, (tm, tn), lambda j, i: (i, j))],
                 [((s, f), BF16, (tm, tn), lambda j, i: (i, j))])[0]


def _ff1_bwd(du, w_ff1_t, hres, dout, y, mod, g2, gp1, tm, riders=None):
    s, d = hres.shape
    f = du.shape[1]

    def body(a_ref, w_ref, h_ref, do_ref, y_ref, mod_ref, g2_ref, gp_ref,
             dh_ref, dy_ref, dsh_ref, dsc_ref, dg2_ref, dgt_ref, dgp_ref):
        _zero_at_start([dsh_ref, dsc_ref, dg2_ref, dgt_ref, dgp_ref])
        g2, sc2 = g2_ref[...], mod_ref[:, 4 * d:5 * d]
        gt, gp = mod_ref[:, 2 * d:3 * d], gp_ref[...]
        for rows in _pieces(tm):
            dh2 = _dot(a_ref[rows, :], w_ref[...])
            n2, r2 = _rms(h_ref[rows, :], d)
            dsh_ref[...] += _colsum(dh2)
            dsc_ref[...] += _colsum(dh2 * n2 * g2)
            dg2_ref[...] += _colsum(dh2 * n2 * (1.0 + sc2))
            dhres = do_ref[rows, :] + _rms_bwd(dh2 * g2 * (1.0 + sc2), n2, r2, d)
            dh_ref[rows, :] = dhres
            ny, ry = _rms(y_ref[rows, :], d)
            dgt_ref[...] += _colsum(dhres * (ny * gp))
            dgp_ref[...] += _colsum(dhres * gt * ny)
            dy_ref[rows, :] = _rms_bwd(dhres * gt * gp, ny, ry, d).astype(BF16)

    row = lambda i: (i, 0)
    fix = lambda i: (0, 0)
    vec = ((1, d), F32, (1, d), fix)
    return _call("ff1_bwd", body, (s // tm,),
                 [(du, (tm, f), row), (w_ff1_t, (f, d), fix), (hres, (tm, d), row), (dout, (tm, d), row),
                  (y, (tm, d), row), (mod, (1, 6 * d), fix), (g2, (1, d), fix), (gp1, (1, d), fix)],
                 [((s, d), F32, (tm, d), row), ((s, d), BF16, (tm, d), row), vec, vec, vec, vec, vec], riders=riders)


def _out_bwd(dy, w_out, proj, r_bf, s_bf, tm, tn):
    s, d = dy.shape
    ar_off = (2 * RET_QK + 2 * RET_V + 3 * SB_W) // tn
    as_off = ar_off + d // tn

    def body(a_ref, w_ref, ar_ref, as_ref, r_ref, s_ref, dr_ref, ds_ref, dar_ref, das_ref):
        dm = _dot(a_ref[...], w_ref[...], _NT)
        sr, ss = _sigmoid(ar_ref[...].astype(F32)), _sigmoid(as_ref[...].astype(F32))
        dr_ref[...] = (dm * sr).astype(BF16)
        ds_ref[...] = (dm * ss).astype(BF16)
        dar_ref[...] = (dm * r_ref[...].astype(F32) * sr * (1.0 - sr)).astype(BF16)
        das_ref[...] = (dm * s_ref[...].astype(F32) * ss * (1.0 - ss)).astype(BF16)

    tile = (tm, tn)
    here = lambda j, i: (i, j)
    return _call("out_bwd", body, (d // tn, s // tm),
                 [(dy, (tm, d), lambda j, i: (i, 0)), (w_out, (tn, d), lambda j, i: (j, 0)),
                  (proj, tile, lambda j, i: (i, ar_off + j)), (proj, tile, lambda j, i: (i, as_off + j)),
                  (r_bf, tile, here), (s_bf, tile, here)],
                 [((s, d), BF16, tile, here)] * 4)


def _gn_bwd(dretg, ret, proj, gn_g, tm, riders=None):
    s = ret.shape[0]
    gate_off = (2 * RET_QK + RET_V) // RET_V

    def body(d_ref, r_ref, g_ref, w_ref, dg_ref, dret_ref, dw_ref):
        first = pl.program_id(0) == 0
        for h in range(HEADS):
            cols = slice(h * RET_DV, (h + 1) * RET_DV)
            o, g, w, dr = r_ref[:, cols], g_ref[:, cols].astype(F32), w_ref[:, cols], d_ref[:, cols].astype(F32)
            mu = jnp.sum(o, axis=1, keepdims=True) * (1.0 / RET_DV)
            xc = o - mu
            rstd = lax.rsqrt(jnp.sum(xc * xc, axis=1, keepdims=True) * (1.0 / RET_DV) + EPS)
            n = xc * rstd
            sg = _sigmoid(g)
            silu = g * sg
            dg_ref[:, cols] = (dr * n * w * (sg * (1.0 + g * (1.0 - sg)))).astype(BF16)
            _accum(dw_ref.at[:, cols], _colsum(dr * silu * n), first)
            dn = dr * silu * w
            m1 = jnp.sum(dn, axis=1, keepdims=True) * (1.0 / RET_DV)
            m2 = jnp.sum(dn * n, axis=1, keepdims=True) * (1.0 / RET_DV)
            dret_ref[:, cols] = (rstd * (dn - m1 - n * m2)).astype(BF16)

    row = lambda i: (i, 0)
    fix = lambda i: (0, 0)
    return _call("gn_bwd", body, (s // tm,),
                 [(dretg, (tm, RET_V), row), (ret, (tm, RET_V), row),
                  (proj, (tm, RET_V), lambda i: (i, gate_off)), (gn_g, (1, RET_V), fix)],
                 [((s, RET_V), BF16, (tm, RET_V), row), ((s, RET_V), BF16, (tm, RET_V), row),
                  ((1, RET_V), F32, (1, RET_V), fix)], riders=riders)


def _ret_bwd(qk_rot, v_bf, dret, log_gamma, t, riders=None):
    s = qk_rot.shape[0]
    n_pair = HEADS // 2
    pw = 2 * RET_DV
    n_blk = s // t

    def d_scores(lg_ref, hp, i, qb, kb, vb, dob):
        z, w = _ret_block(lg_ref, hp, i, t, qb, kb)
        dp = jnp.concatenate([_dot(dob[:, 0:RET_DV], vb[:, 0:RET_DV], _NT),
                              _dot(dob[:, RET_DV:pw], vb[:, RET_DV:pw], _NT)], axis=0)
        return (z * w).astype(BF16), (dp * w).astype(BF16)

    def up_body(lg_ref, q_ref, k_ref, v_ref, do_ref, dq_ref, state_ref):
        hp, i = pl.program_id(0), pl.program_id(1)

        @pl.when(i == 0)
        def _():
            state_ref[...] = jnp.zeros_like(state_ref)

        qb, kb, vb, dob = q_ref[...], k_ref[...], v_ref[...], do_ref[...]
        _, ds = d_scores(lg_ref, hp, i, qb, kb, vb, dob)
        dq_ref[...] = _dot(_side_by_side(ds, t), _stack_heads(kb)) + _dot(dob, state_ref[...], _NT)
        state_ref[...] += _pair_mask() * _dot(kb, vb, _TN)

    def down_body(lg_ref, q_ref, k_ref, v_ref, do_ref, dk_ref, dv_ref, state_ref):
        hp, i = pl.program_id(0), n_blk - 1 - pl.program_id(1)

        @pl.when(pl.program_id(1) == 0)
        def _():
            state_ref[...] = jnp.zeros_like(state_ref)

        qb, kb, vb, dob = q_ref[...], k_ref[...], v_ref[...], do_ref[...]
        p, ds = d_scores(lg_ref, hp, i, qb, kb, vb, dob)
        later = state_ref[...]
        dv_ref[...] = jnp.concatenate([_dot(p[:t], dob[:, 0:RET_DV], _TN), _dot(p[t:], dob[:, RET_DV:pw], _TN)],
                                      axis=1) + _dot(kb, later)
        dk_ref[...] = _dot(ds, _stack_heads(qb), _TN) + _dot(vb, later, _NT)
        state_ref[...] += _pair_mask() * _dot(qb, dob, _TN)

    def ins(order):
        return [(log_gamma, None, pltpu.SMEM),
                (qk_rot, (t, LANES), lambda hp, i: (order(i), hp)),
                (qk_rot, (t, LANES), lambda hp, i: (order(i), n_pair + hp)),
                (v_bf, (t, pw), lambda hp, i: (order(i), hp)),
                (dret, (t, pw), lambda hp, i: (order(i), hp))]

    up = lambda i: i
    down = lambda i: n_blk - 1 - i
    scratch = [pltpu.VMEM((LANES, pw), F32)]
    dq = _call("ret_bwd_q", up_body, (n_pair, n_blk), ins(up),
               [((s, RET_QK), F32, (t, LANES), lambda hp, i: (i, hp))], scratch=scratch)[0]
    dk, dv, *rest = _call("ret_bwd_kv", down_body, (n_pair, n_blk), ins(down),
                          [((s, RET_QK), F32, (t, LANES), lambda hp, i: (down(i), hp)),
                           ((s, RET_V), F32, (t, pw), lambda hp, i: (down(i), hp))],
                          scratch=scratch, riders=riders)
    return [dq, dk, dv] + rest


def _sb_bwd(qkv, weights, do, tq, tk, riders=None):
    s = qkv.shape[0]
    n_pair = HEADS // 2
    _check_tiles(s, tq, tk, SB_GROUP)

    def body(q_ref, k_ref, v_ref, a_ref, do_ref, dq_ref, dk_ref, dv_ref):
        i = pl.program_id(1)

        @pl.when(i == 0)
        def _():
            dk_ref[...] = jnp.zeros_like(dk_ref)
            dv_ref[...] = jnp.zeros_like(dv_ref)

        lower2 = _tri2(tk, False)
        qs = _stack_heads(q_ref[...])
        dos = _stack_heads(do_ref[...].astype(BF16))

        def make_step(near_diagonal):
            def step(g, carry):
                c_e, dq = carry
                js = [g * SB_GROUP + sub for sub in range(SB_GROUP)]
                rows = [_key_rows(j, tk) for j in js]
                zs = [_dot(qs, k_ref[rw, :], _NT) for rw in rows]
                das = [_dot(dos, v_ref[rw, :], _NT) for rw in rows]
                avals = [a_ref[j] for j in js]
                for a, rw in zip(avals, rows):
                    dv_ref[rw, :] += _dot(a, dos, _TN)
                es = [a.astype(F32) * da for a, da in zip(avals, das)]
                prefixes = [lax.dot_general(_split_bf16(e), lower2, _NN, preferred_element_type=F32) for e in es]
                betas = [1.0 / (1.0 + jnp.exp2(-z)) for z in zs]
                for sub in range(SB_GROUP):
                    dz = es[sub] - (es[sub] + prefixes[sub] + c_e) * betas[sub]
                    if near_diagonal:
                        dz = jnp.where(_sb_valid(i, js[sub], tq, tk), dz, 0.0)
                    dz = dz.astype(BF16)
                    dk_ref[rows[sub], :] += _dot(dz, qs, _TN)
                    dq = dq + _dot(_side_by_side(dz, tq), _stack_heads(k_ref[rows[sub], :]))
                    c_e = c_e + jnp.sum(es[sub], axis=1, keepdims=True)
                return c_e, dq
            return step

        n_full = _n_full(i, tq, tk, SB_GROUP)
        carry = (jnp.zeros((2 * tq, 1), F32), jnp.zeros((tq, LANES), F32))
        carry = lax.fori_loop(0, n_full, make_step(False), carry)
        _, dq = lax.fori_loop(n_full, _n_groups(i, tq, tk, SB_GROUP), make_step(True), carry)
        dq_ref[...] = dq

    blk = lambda hp, i: (i, hp)
    n_kb = s // tk
    return _call("sb_bwd", body, (n_pair, s // tq),
                 [(qkv, (tq, LANES), blk),
                  (qkv, (s, LANES), lambda hp, i: (0, n_pair + hp)),
                  (qkv, (s, LANES), lambda hp, i: (0, 2 * n_pair + hp)),
                  (weights, (None, None, n_kb, 2 * tq, tk), lambda hp, i: (hp, i, 0, 0, 0)),
                  (do, (tq, LANES), blk)],
                 [((s, SB_W), F32, (tq, LANES), blk),
                  ((s, SB_W), F32, (s, LANES), lambda hp, i: (0, hp)),
                  ((s, SB_W), F32, (s, LANES), lambda hp, i: (0, hp))], riders=riders)


def _assemble_dproj(dq_r, dk_r, dv_r, dg_r, dq_s, dk_s, dv_s, da_r, da_s, cos, sin, idx_col, lg_lanes, tm):
    s, d = da_r.shape
    width = 2 * RET_QK + 2 * RET_V + 3 * SB_W + 2 * d

    def body(dq_ref, dk_ref, dv_ref, dg_ref, dqs_ref, dks_ref, dvs_ref, dar_ref, das_ref, cos_ref, sin_ref,
             idx_ref, lg_ref, o_ref):
        lane = lax.broadcasted_iota(jnp.int32, (1, LANES), 1)
        first = jnp.bitwise_and(lane, RET_DQK - 1) < (RET_DQK // 2)
        cos, sin = cos_ref[...], sin_ref[...]
        idx = idx_ref[...]
        for src, base, sign, scale in ((dq_ref, 0, 1.0, 1.0), (dk_ref, RET_QK, -1.0, RET_DQK ** -0.5)):
            for g in range(RET_QK // LANES):
                v = src[:, g * LANES:(g + 1) * LANES] * (_decay_scale(lg_ref, idx, g, sign) * scale)
                sw = jnp.where(first, pltpu.roll(v, LANES - RET_DQK // 2, 1), pltpu.roll(v, RET_DQK // 2, 1))
                o_ref[:, base + g * LANES:base + (g + 1) * LANES] = (v * cos - sw * sin).astype(BF16)
        off = 2 * RET_QK
        o_ref[:, off:off + RET_V] = dv_ref[...].astype(BF16)
        off += RET_V
        o_ref[:, off:off + RET_V] = dg_ref[...]
        off += RET_V
        o_ref[:, off:off + SB_W] = (dqs_ref[...] * (SB_DH ** -0.5)).astype(BF16)
        off += SB_W
        o_ref[:, off:off + SB_W] = (dks_ref[...] * LN2).astype(BF16)
        off += SB_W
        o_ref[:, off:off + SB_W] = dvs_ref[...].astype(BF16)
        off += SB_W
        o_ref[:, off:off + d] = dar_ref[...]
        off += d
        o_ref[:, off:off + d] = das_ref[...]

    row = lambda i: (i, 0)
    ins = [(a, (tm, a.shape[1]), row) for a in (dq_r, dk_r, dv_r, dg_r, dq_s, dk_s, dv_s, da_r, da_s, cos, sin, idx_col)]
    ins.append((lg_lanes, (1, RET_QK), lambda i: (0, 0)))
    return _call("assemble_dproj", body, (s // tm,), ins, [((s, width), BF16, (tm, width), row)])[0]


def _in_bwd(dproj, w_in_t, x, dhres, mod, g1, tm, riders=None):
    s, d = x.shape
    width = dproj.shape[1]

    def body(a_ref, w_ref, x_ref, dh_ref, mod_ref, g_ref, dx_ref, dsh_ref, dsc_ref, dg_ref):
        _zero_at_start([dsh_ref, dsc_ref, dg_ref])
        g1, sc1 = g_ref[...], mod_ref[:, d:2 * d]
        for rows in _pieces(tm):
            dh = _dot(a_ref[rows, :], w_ref[...])
            n1, r1 = _rms(x_ref[rows, :], d)
            dsh_ref[...] += _colsum(dh)
            dsc_ref[...] += _colsum(dh * n1 * g1)
            dg_ref[...] += _colsum(dh * n1 * (1.0 + sc1))
            dx_ref[rows, :] = dh_ref[rows, :] + _rms_bwd(dh * g1 * (1.0 + sc1), n1, r1, d)

    row = lambda i: (i, 0)
    fix = lambda i: (0, 0)
    vec = ((1, d), F32, (1, d), fix)
    return _call("in_bwd", body, (s // tm,),
                 [(dproj, (tm, width), row), (w_in_t, (width, d), fix), (x, (tm, d), row), (dhres, (tm, d), row),
                  (mod, (1, 6 * d), fix), (g1, (1, d), fix)],
                 [((s, d), F32, (tm, d), row), vec, vec, vec], riders=riders)


def _adamw(w, g, m, v):
    m = ADAM_B1 * m + (1.0 - ADAM_B1) * g
    v = ADAM_B2 * v + (1.0 - ADAM_B2) * (g * g)
    m_hat = m / (1.0 - ADAM_B1 ** ADAM_STEP)
    v_hat = v / (1.0 - ADAM_B2 ** ADAM_STEP)
    delta = -ADAM_LR * (m_hat / (jnp.sqrt(v_hat) + ADAM_EPS) + ADAM_WD * w)
    return delta, m, v


def _adam_reduce(name, parts, w, m, v, tr):
    rws, cls = w.shape
    tr = min(tr, rws)
    n_parts = parts.shape[0]

    def body(p_ref, w_ref, m_ref, v_ref, g_out, d_out, m_out, v_out):
        g = p_ref[0].astype(F32)
        for k in range(1, n_parts):
            g = g + p_ref[k].astype(F32)
        delta, mn, vn = _adamw(w_ref[...], g, m_ref[...], v_ref[...])
        g_out[...] = g
        d_out[...] = delta
        m_out[...] = mn
        v_out[...] = vn

    row = lambda i: (i, 0)
    blk = (tr, cls)
    return _call(name, body, (rws // tr,),
                 [(parts, (n_parts, tr, cls), lambda i: (0, i, 0)), (w, blk, row), (m, blk, row), (v, blk, row)],
                 [((rws, cls), F32, blk, row)] * 4)


def _ada_bwd_adam(cs_t, dmod_cols, w, m, v):
    d, nc = w.shape

    def body(c_ref, dm_ref, w_ref, m_ref, v_ref, g_out, d_out, m_out, v_out):
        g = c_ref[0] * dm_ref[0:1, :]
        for r in range(1, N_DEV):
            g = g + c_ref[r] * dm_ref[r:r + 1, :]
        delta, mn, vn = _adamw(w_ref[...], g, m_ref[...], v_ref[...])
        g_out[...] = g
        d_out[...] = delta
        m_out[...] = mn
        v_out[...] = vn

    fix = lambda i: (0, 0)
    blk = (d, nc)
    return _call("ada_bwd_adam", body, (1,),
                 [(cs_t, (N_DEV, d, 1), lambda i: (0, 0, 0)), (dmod_cols, (N_DEV, nc), fix), (w, blk, fix), (m, blk, fix), (v, blk, fix)],
                 [((d, nc), F32, blk, fix)] * 4)


def _small_adam(parts, w, m, v):
    n = w.shape[1]

    def body(p_ref, w_ref, m_ref, v_ref, g_out, d_out, m_out, v_out):
        g = p_ref[0:1, :]
        for k in range(1, N_DEV):
            g = g + p_ref[k:k + 1, :]
        delta, mn, vn = _adamw(w_ref[...], g, m_ref[...], v_ref[...])
        g_out[...] = g
        d_out[...] = delta
        m_out[...] = mn
        v_out[...] = vn

    fix = lambda i: (0, 0)
    return _call("small_adam", body, (1,),
                 [(parts, (N_DEV, n), fix), (w, (1, n), fix), (m, (1, n), fix), (v, (1, n), fix)],
                 [((1, n), F32, (1, n), fix)] * 4)


def kernel(x, c, positions, ada_w, ada_b, pre_mix_g, post_mix_g, pre_ffn_g, post_ffn_g, w_in, ret_gn_g, w_ret_branch, w_sb_branch, w_out, w_ff1, w_ff2, loss_target, m_ada_w, m_ada_b, m_pre_mix_g, m_post_mix_g, m_pre_ffn_g, m_post_ffn_g, m_w_in, m_ret_gn_g, m_w_ret_branch, m_w_sb_branch, m_w_out, m_w_ff1, m_w_ff2, v_ada_w, v_ada_b, v_pre_mix_g, v_post_mix_g, v_pre_ffn_g, v_post_ffn_g, v_w_in, v_ret_gn_g, v_w_ret_branch, v_w_sb_branch, v_w_out, v_w_ff1, v_w_ff2):
    _, s, d = x.shape
    d_ff = w_ff1.shape[2] * N_DEV
    d_in = w_in.shape[2] * N_DEV
    me = 4 * lax.axis_index("x") + 2 * lax.axis_index("y") + lax.axis_index("c")
    x2, tgt = x[0], loss_target[0]

    core = lax.axis_index("c").astype(jnp.int32).reshape(1)
    bf = lambda w: w[0].astype(BF16)

    w_in_t, m_in_t, v_in_t = (jnp.swapaxes(a[0], 0, 1) for a in (w_in, m_w_in, v_w_in))

    c_all, g_in = _exchange("gather_in", [c, w_in_t.astype(BF16)], ["gather", "gather_chip"])
    c_all = c_all.reshape(N_DEV, d)

    n_ada = ada_w.shape[2]
    cs_all = _silu_rows(c_all)
    ada_b_cols = lax.dynamic_slice(ada_b, (0, me * n_ada), (1, n_ada))
    mod_cols = _ada_fwd(cs_all, ada_w[0], ada_b_cols)
    mod_all = _exchange("gather_mod", [mod_cols], ["gather"])[0]
    mod = lax.dynamic_index_in_dim(mod_all, me, axis=1, keepdims=False).reshape(1, 6 * d)

    tm = min(256, s)
    h, g_in = _pre_norm(x2, pre_mix_g, mod, tm, riders=([g_in], ["forward"]))
    wt_in = g_in.reshape(d_in, d)
    proj = _matmul("in_proj", h, wt_in, "nt", s, 512, BF16)
    pos_col = positions.reshape(s, 1).astype(F32)
    freqs = ROPE_BASE ** (-jnp.arange(0, RET_DQK, 2, dtype=F32) / RET_DQK)
    inv_freq = jnp.tile(freqs, LANES // (RET_DQK // 2)).reshape(1, LANES)
    log_gamma_np = np.log1p(-(2.0 ** (-5.0 - np.arange(HEADS))))
    log_gamma = jnp.asarray(log_gamma_np, F32)
    lg_lanes = jnp.asarray(np.repeat(log_gamma_np, RET_DQK).reshape(1, RET_QK), F32)
    idx_col = (jnp.arange(s, dtype=F32) - (s // 2)).reshape(s, 1)
    qk_rot, v_bf, qkv_sb, cos_t, sin_t = _prep(proj, pos_col, idx_col, inv_freq, lg_lanes, tm)
    tq, tk = min(256, s), min(128, s)
    bf_t = lambda w: jnp.swapaxes(w[0], 0, 1).astype(BF16)
    later = [bf(w_ret_branch), bf_t(w_sb_branch), bf(w_out), bf_t(w_ff1)]
    sb, sb_weights, *later = _sb_fwd(qkv_sb, tq, tk, riders=(later, ["gather_chip"] * 4))
    ret, retg, g_ret, g_sb, g_out, g_ff1, g_ff2 = _ret_fwd(qk_rot, v_bf, proj, ret_gn_g, log_gamma, tq,
                                                           riders=(later + [bf(w_ff2)], ["forward"] * 4 + ["gather_chip"]))
    wf_ret = g_ret.reshape(RET_V, d)
    wt_sb = g_sb.reshape(d, SB_W)
    wf_out = g_out.reshape(d, d)
    wt_ff1 = g_ff1.reshape(d_ff, d)
    mixed, r_bf, s_bf, g_ff2 = _merge(retg, sb, wf_ret, wt_sb, proj, tm, min(512, d), riders=([g_ff2], ["forward"]))
    wf_ff2 = g_ff2.reshape(d_ff, d)
    y, hres, h2 = _out_proj(mixed, wf_out, x2, mod, post_mix_g, pre_ffn_g, tm)
    u, act = _ff1(h2, wt_ff1, s, 512)
    dout, df, loss_sum, d_gt2, d_gp2 = _ff2_loss(act, wf_ff2, hres, tgt, mod, post_ffn_g, tm)

    du = _ff2_bwd(df, wf_ff2, u, s, 512)
    gw_ff2 = _matmul("grad_w_ff2", act, df, "tn", 512, d, BF16).reshape(N_DEV, d_ff // N_DEV, d)
    gw_ff1 = _matmul("grad_w_ff1", h2, du, "tn", d, d_ff // N_DEV, BF16, blocked_out=True)
    dhres, dy, d_sh2, d_sc2, d_g2, d_gt1, d_gp1, t_ff1, t_ff2 = _ff1_bwd(
        du, wt_ff1, hres, dout, y, mod, pre_ffn_g, post_mix_g, tm, riders=([gw_ff1, gw_ff2], ["pair"] * 2))
    s_ff1 = _pair_sum("pair_sum_ff1", gw_ff1, t_ff1, core, 256)
    s_ff2 = _pair_sum("pair_sum_ff2", gw_ff2, t_ff2, core, 256)
    d_r, d_s, da_r, da_s = _out_bwd(dy, wf_out, proj, r_bf, s_bf, tm, min(512, d))
    gw_out = _matmul("grad_w_out", mixed, dy, "tn", 512, d, BF16).reshape(N_DEV, d // N_DEV, d)
    dretg = _matmul("ret_branch_bwd", d_r, wf_ret, "nt", s, 512, BF16)
    dsb = _matmul("sb_branch_bwd", d_s, wt_sb, "nn", s, 512, F32)
    gw_ret = _matmul("grad_w_ret", retg, d_r, "tn", 512, d, BF16).reshape(N_DEV, RET_V // N_DEV, d)
    gw_sb = _matmul("grad_w_sb", sb, d_s, "tn", 512, d // N_DEV, BF16, blocked_out=True)
    dq_s, dk_s, dv_s, p_ff1, p_ff2, p_ret = _sb_bwd(qkv_sb, sb_weights, dsb, tq, tk,
                                                    riders=([s_ff1, s_ff2, gw_ret], ["chip_scatter"] * 2 + ["scatter"]))
    dg_r, dret, d_gn = _gn_bwd(dretg, ret, proj, ret_gn_g, tm)
    dq_r, dk_r, dv_r, p_out, p_sb = _ret_bwd(qk_rot, v_bf, dret, log_gamma, tq,
                                             riders=([gw_out, gw_sb], ["scatter"] * 2))
    dproj = _assemble_dproj(dq_r, dk_r, dv_r, dg_r, dq_s, dk_s, dv_s, da_r, da_s, cos_t, sin_t, idx_col, lg_lanes, tm)
    gw_in = _matmul("grad_w_in", dproj, h, "tn", 512, d, BF16).reshape(N_DEV, d_in // N_DEV, d)
    t_in = _exchange("pair_in", [gw_in], ["pair"])[0]
    tr_in = d_in // N_DEV // 4
    s_in = _pair_sum("pair_sum_in", gw_in, t_in, core, tr_in)
    grad_x, d_sh1, d_sc1, d_g1, p_in = _in_bwd(dproj, wt_in, x2, dhres, mod, pre_mix_g, tm,
                                               riders=([s_in], ["chip_scatter"]))
    loss_lanes = jnp.pad(loss_sum, ((0, 0), (0, LANES - 1)))
    small = jnp.concatenate([d_sh1, d_sc1, d_gt1, d_sh2, d_sc2, d_gt2, d_g1, d_gp1, d_g2, d_gp2, d_gn, loss_lanes], axis=1)
    small_all = _exchange("gather_small", [small], ["gather"])[0].reshape(N_DEV, small.shape[1])
    parts = [p_in, p_ret, p_sb, p_out, p_ff1, p_ff2]

    res = {}
    names = ["w_ret_branch", "w_sb_branch", "w_out", "w_ff1", "w_ff2"]
    ws = [w_ret_branch, w_sb_branch, w_out, w_ff1, w_ff2]
    ms = [m_w_ret_branch, m_w_sb_branch, m_w_out, m_w_ff1, m_w_ff2]
    vs = [v_w_ret_branch, v_w_sb_branch, v_w_out, v_w_ff1, v_w_ff2]
    for nm, p, w, m, v in zip(names, parts[1:], ws, ms, vs):
        res[nm] = [o[None] for o in _adam_reduce("adam_" + nm, p, w[0], m[0], v[0], 256)]
    res["w_in"] = [jnp.swapaxes(o, 0, 1)[None]
                   for o in _adam_reduce("adam_w_in", parts[0], w_in_t, m_in_t, v_in_t, tr_in)]
    dmod_cols = lax.dynamic_slice(small_all, (0, me * n_ada), (N_DEV, n_ada))
    res["ada_w"] = [o[None] for o in _ada_bwd_adam(cs_all.reshape(N_DEV, d, 1), dmod_cols, ada_w[0], m_ada_w[0], v_ada_w[0])]
    vec_names = ["ada_b", "pre_mix_g", "post_mix_g", "pre_ffn_g", "post_ffn_g", "ret_gn_g"]
    cat = lambda xs: jnp.concatenate(xs + [jnp.zeros((1, LANES), F32)], axis=1)
    packed = _small_adam(small_all,
                         cat([ada_b, pre_mix_g, post_mix_g, pre_ffn_g, post_ffn_g, ret_gn_g]),
                         cat([m_ada_b, m_pre_mix_g, m_post_mix_g, m_pre_ffn_g, m_post_ffn_g, m_ret_gn_g]),
                         cat([v_ada_b, v_pre_mix_g, v_post_mix_g, v_pre_ffn_g, v_post_ffn_g, v_ret_gn_g]))
    off = 0
    for nm, width in zip(vec_names, [6 * d, d, d, d, d, RET_V]):
        res[nm] = [p[:, off:off + width] for p in packed]
        off += width

    loss = (0.5 / d) * packed[0][0, off]
    order = ["ada_w", "ada_b", "pre_mix_g", "post_mix_g", "pre_ffn_g", "post_ffn_g", "w_in", "ret_gn_g",
             "w_ret_branch", "w_sb_branch", "w_out", "w_ff1", "w_ff2"]
    outs = [loss, grad_x[None]]
    for k in range(4):
        outs += [res[nm][k] for nm in order]
    return tuple(outs)
```

```python
import functools

import numpy as np
import jax
import jax.numpy as jnp
from jax import lax
from jax.experimental import pallas as pl
from jax.experimental.pallas import tpu as pltpu

F32 = jnp.float32
BF16 = jnp.bfloat16
N_DEV = 8
AXES = ("x", "y", "c")

EPS = 1e-6
CHUNK = 64
CHUNK_SHIFT = 6
HEADS = 8
RET_DQK = 64
RET_DV = 128
SB_DH = 64
RET_QK = HEADS * RET_DQK
RET_V = HEADS * RET_DV
SB_W = HEADS * SB_DH
ROPE_BASE = 10000.0
LANES = 128

ADAM_LR = 0.001
ADAM_B1 = 0.9
ADAM_B2 = 0.999
ADAM_EPS = 1e-08
ADAM_WD = 0.01
ADAM_STEP = 10

VMEM_LIMIT = 56 * 1024 * 1024

_NN = (((1,), (0,)), ((), ()))
_NT = (((1,), (1,)), ((), ()))
_TN = (((0,), (0,)), ((), ()))


def _dot(a, b, dims=_NN):
    if a.dtype != BF16:
        a = a.astype(BF16)
    if b.dtype != BF16:
        b = b.astype(BF16)
    return lax.dot_general(a, b, dims, preferred_element_type=F32)


def _dot_split(a, b):
    hi = a.astype(BF16)
    lo = (a - hi.astype(F32)).astype(BF16)
    return (lax.dot_general(hi, b, _NN, preferred_element_type=F32)
            + lax.dot_general(lo, b, _NN, preferred_element_type=F32))


def _sigmoid(x):
    return 1.0 / (1.0 + jnp.exp(-x))


def _rms(x, d):
    r = lax.rsqrt(jnp.sum(x * x, axis=1, keepdims=True) * (1.0 / d) + EPS)
    return x * r, r


def _rms_bwd(dn, n, r, d):
    return r * (dn - n * (jnp.sum(dn * n, axis=1, keepdims=True) * (1.0 / d)))


def _colsum(v):
    return jnp.sum(v, axis=0, keepdims=True)


def _accum(ref, val, first):
    @pl.when(first)
    def _():
        ref[...] = val

    @pl.when(jnp.logical_not(first))
    def _():
        ref[...] += val


ROW_SPLIT = 2


def _zero_at_start(refs):
    @pl.when(pl.program_id(0) == 0)
    def _():
        for r in refs:
            r[...] = jnp.zeros_like(r)


def _pieces(tm):
    step = tm // ROW_SPLIT
    return [slice(k * step, (k + 1) * step) for k in range(ROW_SPLIT)]


KIND_SLOTS = {"gather": N_DEV, "scatter": N_DEV, "gather_chip": N_DEV, "forward": N_DEV, "pair": N_DEV // 2,
              "chip_scatter": N_DEV // 2}
SEMS_PER_ARRAY = N_DEV - 1


def _exchange_copies(ins, outs, send_sems, recv_sems, local_sems, kinds):
    x, y, c = (lax.axis_index(a) for a in AXES)
    me, chip, sibling = 4 * x + 2 * y + c, 2 * x + y, (x, y, 1 - c)
    mesh_id = pl.DeviceIdType.MESH
    other_chips = []
    for k in range(1, N_DEV // 2):
        px = 1 - x if k & 2 else x
        py = 1 - y if k & 1 else y
        other_chips.append((px, py))
    copies = []
    for i, kind in enumerate(kinds):
        def remote(src, dst, k, to, i=i):
            return pltpu.make_async_remote_copy(
                src_ref=src, dst_ref=dst, send_sem=send_sems.at[i * SEMS_PER_ARRAY + k],
                recv_sem=recv_sems.at[i * SEMS_PER_ARRAY + k], device_id=to, device_id_type=mesh_id)

        if kind in ("gather", "scatter"):
            pick = (lambda ref, d: ref.at[d]) if kind == "scatter" else (lambda ref, d: ref)
            copies.append(pltpu.make_async_copy(pick(ins[i], me), outs[i].at[me], local_sems.at[i]))
            for k in range(1, N_DEV):
                to = (1 - x if k & 4 else x, 1 - y if k & 2 else y, 1 - c if k & 1 else c)
                copies.append(remote(pick(ins[i], 4 * to[0] + 2 * to[1] + to[2]), outs[i].at[me], k - 1, to))
        elif kind == "gather_chip":
            copies.append(pltpu.make_async_copy(ins[i], outs[i].at[me], local_sems.at[i]))
            copies.append(remote(ins[i], outs[i].at[me], 0, sibling))
            for k, (px, py) in enumerate(other_chips):
                copies.append(remote(ins[i], outs[i].at[me], 1 + k, (px, py, c)))
        elif kind == "forward":
            for k, (px, py) in enumerate(other_chips):
                slot = 4 * px + 2 * py + c
                copies.append(remote(outs[i].at[slot], outs[i].at[slot], k, sibling))
        elif kind == "pair":
            for k in range(N_DEV // 2):
                copies.append(remote(ins[i].at[2 * k + 1 - c], outs[i].at[k], k, sibling))
        elif kind == "chip_scatter":
            copies.append(pltpu.make_async_copy(ins[i].at[chip], outs[i].at[chip], local_sems.at[i]))
            for k, (px, py) in enumerate(other_chips):
                copies.append(remote(ins[i].at[2 * px + py], outs[i].at[chip], k, (px, py, c)))
        else:
            raise ValueError(kind)
    return copies


def _exchange_shapes(arrays, kinds):
    shapes = []
    for a, kind in zip(arrays, kinds):
        tail = a.shape if kind in ("gather", "gather_chip") else a.shape[1:]
        shapes.append(jax.ShapeDtypeStruct((KIND_SLOTS[kind],) + tuple(tail), a.dtype))
    return shapes


def _exchange_sems(n):
    return [pltpu.SemaphoreType.DMA((n * SEMS_PER_ARRAY,)), pltpu.SemaphoreType.DMA((n * SEMS_PER_ARRAY,)),
            pltpu.SemaphoreType.DMA((n,))]


def _call(name, body, grid, ins, outs, scratch=(), riders=None, prefetch=None):
    any_spec = pl.BlockSpec(memory_space=pl.ANY)
    in_specs = [pl.BlockSpec(memory_space=im) if bs is None else pl.BlockSpec(bs, im) for _, bs, im in ins]
    out_specs = [pl.BlockSpec(bs, im) for _, _, bs, im in outs]
    out_shape = [jax.ShapeDtypeStruct(s, d) for s, d, _, _ in outs]
    operands = [a for a, _, _ in ins]
    scratch = list(scratch)
    aliases = {}
    n_pre = 0 if prefetch is None else 1
    kernel = functools.partial(body) if prefetch is None else (lambda _, *refs: body(*refs))
    if riders is not None:
        arrays, kinds = riders
        nr, n_in, n_out, n_scr = len(arrays), len(ins), len(outs), len(scratch)

        def kernel(*refs):
            refs = refs[n_pre:]
            own_in, ride_in = refs[:n_in], refs[n_in:n_in + nr]
            own_out = refs[n_in + nr:n_in + nr + n_out]
            ride_out = refs[n_in + nr + n_out:n_in + 2 * nr + n_out]
            own_scr = refs[n_in + 2 * nr + n_out:n_in + 2 * nr + n_out + n_scr]
            sems = refs[n_in + 2 * nr + n_out + n_scr:]
            ids = [pl.program_id(a) for a in range(len(grid))]
            first = functools.reduce(jnp.logical_and, [i == 0 for i in ids])
            last = functools.reduce(jnp.logical_and, [i == g - 1 for i, g in zip(ids, grid)])

            @pl.when(first)
            def _():
                for cp in _exchange_copies(ride_in, ride_out, *sems, kinds):
                    cp.start()

            body(*own_in, *own_out, *own_scr)

            @pl.when(last)
            def _():
                for cp in _exchange_copies(ride_in, ride_out, *sems, kinds):
                    cp.wait()

        in_specs += [any_spec] * nr
        out_specs += [any_spec] * nr
        out_shape += _exchange_shapes(arrays, kinds)
        operands += list(arrays)
        scratch += _exchange_sems(nr)
        aliases = {n_pre + n_in + r: n_out + r for r, kind in enumerate(kinds) if kind == "forward"}
    params = pltpu.CompilerParams(dimension_semantics=("arbitrary",) * len(grid), vmem_limit_bytes=VMEM_LIMIT)
    if prefetch is None:
        return pl.pallas_call(kernel, name=name, grid=grid, in_specs=in_specs, out_specs=out_specs,
                              out_shape=out_shape, scratch_shapes=scratch, input_output_aliases=aliases,
                              compiler_params=params)(*operands)
    grid_spec = pltpu.PrefetchScalarGridSpec(num_scalar_prefetch=1, grid=grid, in_specs=in_specs,
                                             out_specs=out_specs, scratch_shapes=scratch)
    return pl.pallas_call(kernel, name=name, grid_spec=grid_spec, out_shape=out_shape,
                          input_output_aliases=aliases, compiler_params=params)(prefetch, *operands)


def _exchange(name, arrays, kinds):
    n = len(arrays)

    def body(*refs):
        copies = _exchange_copies(refs[:n], refs[n:2 * n], *refs[2 * n:], kinds)
        for cp in copies:
            cp.start()
        for cp in copies:
            cp.wait()

    any_spec = pl.BlockSpec(memory_space=pl.ANY)
    return pl.pallas_call(
        functools.partial(body),
        name=name,
        in_specs=[any_spec] * n,
        out_specs=[any_spec] * n,
        out_shape=_exchange_shapes(arrays, kinds),
        scratch_shapes=_exchange_sems(n),
        input_output_aliases={i: i for i, kind in enumerate(kinds) if kind == "forward"},
    )(*arrays)


def _pair_sum(name, mine, theirs, my_core, tr):
    _, rws, cls = mine.shape
    tr = min(tr, rws)

    def body(a_ref, b_ref, o_ref):
        o_ref[...] = (a_ref[...].astype(F32) + b_ref[...].astype(F32)).astype(o_ref.dtype)

    return _call(name, body, (N_DEV // 2, rws // tr),
                 [(mine, (None, tr, cls), lambda k, r, core: (2 * k + core[0], r, 0)),
                  (theirs, (None, tr, cls), lambda k, r, core: (k, r, 0))],
                 [((N_DEV // 2, rws, cls), mine.dtype, (None, tr, cls), lambda k, r, core: (k, r, 0))],
                 prefetch=my_core)[0]


def _matmul(name, a, b, kind, tm, tn, out_dtype, blocked_out=False, riders=None):
    if kind == "tn":
        kdim, m = a.shape
    else:
        m, kdim = a.shape
    n = b.shape[0] if kind == "nt" else b.shape[1]
    tm, tn = min(tm, m), min(tn, n)
    dims = {"nn": _NN, "nt": _NT, "tn": _TN}[kind]

    def body(a_ref, b_ref, o_ref):
        o_ref[...] = _dot(a_ref[...], b_ref[...], dims).astype(o_ref.dtype)

    a_spec = (a, (kdim, tm), lambda j, i: (0, i)) if kind == "tn" else (a, (tm, kdim), lambda j, i: (i, 0))
    b_spec = (b, (tn, kdim), lambda j, i: (j, 0)) if kind == "nt" else (b, (kdim, tn), lambda j, i: (0, j))
    if blocked_out:
        out = ((n // tn, m, tn), out_dtype, (None, tm, tn), lambda j, i: (j, i, 0))
    else:
        out = ((m, n), out_dtype, (tm, tn), lambda j, i: (i, j))
    res = _call(name, body, (n // tn, m // tm), [a_spec, b_spec], [out], riders=riders)
    return res[0] if riders is None else res


def _ada_fwd(cs_all, ada_w, ada_b_cols):
    def body(c_ref, w_ref, b_ref, o_ref):
        o_ref[...] = lax.dot_general(c_ref[...], w_ref[...], _NN, preferred_element_type=F32,
                                     precision=lax.Precision.HIGHEST) + b_ref[...]

    r, d = cs_all.shape
    nc = ada_w.shape[1]
    return _call("ada_fwd", body, (1,),
                 [(cs_all, (r, d), lambda i: (0, 0)), (ada_w, (d, nc), lambda i: (0, 0)),
                  (ada_b_cols, (1, nc), lambda i: (0, 0))],
                 [((r, nc), F32, (r, nc), lambda i: (0, 0))])[0]


def _silu_rows(c_all):
    def body(c_ref, o_ref):
        v = c_ref[...]
        o_ref[...] = v * _sigmoid(v)

    return _call("silu_c", body, (1,), [(c_all, c_all.shape, lambda i: (0, 0))],
                 [(c_all.shape, F32, c_all.shape, lambda i: (0, 0))])[0]


def _pre_norm(x, g, mod, tm, riders=None):
    s, d = x.shape

    def body(x_ref, g_ref, mod_ref, h_ref):
        n, _ = _rms(x_ref[...], d)
        sh, sc = mod_ref[:, 0:d], mod_ref[:, d:2 * d]
        h_ref[...] = (n * g_ref[...] * (1.0 + sc) + sh).astype(BF16)

    return _call("pre_norm", body, (s // tm,),
                 [(x, (tm, d), lambda i: (i, 0)), (g, (1, d), lambda i: (0, 0)),
                  (mod, (1, 6 * d), lambda i: (0, 0))],
                 [((s, d), BF16, (tm, d), lambda i: (i, 0))], riders=riders)


LOG2E = 1.4426950408889634
LN2 = 0.6931471805599453


def _decay_scale(lg_ref, idx, g, sign):
    return jnp.exp((sign * idx) * lg_ref[:, g * LANES:(g + 1) * LANES])


def _prep(proj, pos_col, idx_col, inv_freq, lg_lanes, tm):
    s = proj.shape[0]
    sb_off = (2 * RET_QK + 2 * RET_V) // (3 * SB_W)
    n_q = RET_QK // LANES

    def body(qk_ref, v_ref, sb_ref, pos_ref, idx_ref, f_ref, lg_ref, qk_out, v_out, sb_out, cos_out, sin_out):
        ang = pos_ref[...] * f_ref[...]
        lane = lax.broadcasted_iota(jnp.int32, (1, LANES), 1)
        first = jnp.bitwise_and(lane, RET_DQK - 1) < (RET_DQK // 2)
        cos = jnp.cos(ang)
        sin = jnp.where(first, -1.0, 1.0) * jnp.sin(ang)
        cos_out[...] = cos
        sin_out[...] = sin
        idx = idx_ref[...]
        for g in range(2 * n_q):
            v = qk_ref[:, g * LANES:(g + 1) * LANES].astype(F32)
            sw = jnp.where(first, pltpu.roll(v, LANES - RET_DQK // 2, 1), pltpu.roll(v, RET_DQK // 2, 1))
            r = v * cos + sw * sin
            if g < n_q:
                r = r * _decay_scale(lg_ref, idx, g, 1.0)
            else:
                r = r * (_decay_scale(lg_ref, idx, g - n_q, -1.0) * (RET_DQK ** -0.5))
            qk_out[:, g * LANES:(g + 1) * LANES] = r.astype(BF16)
        v_out[...] = v_ref[...].astype(BF16)
        sb_out[:, 0:SB_W] = (sb_ref[:, 0:SB_W].astype(F32) * (SB_DH ** -0.5 * LOG2E)).astype(BF16)
        sb_out[:, SB_W:3 * SB_W] = sb_ref[:, SB_W:3 * SB_W].astype(BF16)

    return _call("prep", body, (s // tm,),
                 [(proj, (tm, 2 * RET_QK), lambda i: (i, 0)),
                  (proj, (tm, RET_V), lambda i: (i, 2 * RET_QK // RET_V)),
                  (proj, (tm, 3 * SB_W), lambda i: (i, sb_off)),
                  (pos_col, (tm, 1), lambda i: (i, 0)),
                  (idx_col, (tm, 1), lambda i: (i, 0)),
                  (inv_freq, (1, LANES), lambda i: (0, 0)),
                  (lg_lanes, (1, RET_QK), lambda i: (0, 0))],
                 [((s, 2 * RET_QK), BF16, (tm, 2 * RET_QK), lambda i: (i, 0)),
                  ((s, RET_V), BF16, (tm, RET_V), lambda i: (i, 0)),
                  ((s, 3 * SB_W), BF16, (tm, 3 * SB_W), lambda i: (i, 0)),
                  ((s, LANES), F32, (tm, LANES), lambda i: (i, 0)),
                  ((s, LANES), F32, (tm, LANES), lambda i: (i, 0))])


def _head_mask(hh):
    lane = lax.broadcasted_iota(jnp.int32, (1, LANES), 1)
    return (lane >= RET_DQK) if hh else (lane < RET_DQK)


def _masked(v, m):
    return jnp.where(m, v, jnp.zeros_like(v))


SB_GROUP = 4
RET_GROUP = 4


def _stack_heads(v):
    return jnp.concatenate([_masked(v, _head_mask(0)), _masked(v, _head_mask(1))], axis=0)


def _side_by_side(v, t):
    return jnp.concatenate([v[:t], v[t:]], axis=1)


def _split_bf16(v):
    hi = v.astype(BF16)
    lo = (v - hi.astype(F32)).astype(BF16)
    return jnp.concatenate([hi, lo], axis=1)


def _tile_pos(i, j, tq, tk):
    row = jnp.bitwise_and(lax.broadcasted_iota(jnp.int32, (2 * tq, tk), 0), tq - 1) + i * tq
    col = lax.broadcasted_iota(jnp.int32, (2 * tq, tk), 1) + j * tk
    return row, col


def _n_groups(i, tq, tk, grp):
    return ((i + 1) * (tq // tk) + grp - 1) // grp


def _n_full(i, tq, tk, grp):
    return (i * (tq // tk)) // grp


def _key_rows(j, tk):
    return pl.ds(pl.multiple_of(j * tk, tk), tk)


def _ret_weight(lg_rows, i, j, tq, tk):
    row, col = _tile_pos(i, j, tq, tk)
    same = jnp.right_shift(col, CHUNK_SHIFT) == jnp.right_shift(row, CHUNK_SHIFT)
    later = jnp.where(same, jnp.exp((2.0 * lg_rows) * (col - row).astype(F32)), 0.0)
    return jnp.where(col <= row, 1.0, later)


def _lg_rows(lg_ref, hp, tq):
    first = lax.broadcasted_iota(jnp.int32, (2 * tq, 1), 0) < tq
    return jnp.where(first, lg_ref[2 * hp], lg_ref[2 * hp + 1])


def _check_tiles(s, tq, tk, grp):
    assert tq % tk == 0 and tq & (tq - 1) == 0 and tk & (tk - 1) == 0
    assert s % tq == 0 and (s // tk) % grp == 0 and s // tk <= LANES


def _pair_mask():
    r = lax.broadcasted_iota(jnp.int32, (LANES, 2 * RET_DV), 0) >= RET_DQK
    c = lax.broadcasted_iota(jnp.int32, (LANES, 2 * RET_DV), 1) >= RET_DV
    return (r == c).astype(F32)


def _ret_block(lg_ref, hp, i, t, qb, kb):
    w = _ret_weight(_lg_rows(lg_ref, hp, t), i, i, t, t)
    return _dot(_stack_heads(qb), kb, _NT), w


RET_PAIRS = 2


def _lanes(ref, p, width):
    return ref[:, p * width:(p + 1) * width]


def _ret_fwd(qk_rot, v_bf, proj, gn_g, log_gamma, t, riders=None):
    s = qk_rot.shape[0]
    n_pair = HEADS // 2
    pw = 2 * RET_DV
    wq, wv = RET_PAIRS * LANES, RET_PAIRS * pw
    gate_off = (2 * RET_QK + RET_V) // wv
    assert s % t == 0 and t % CHUNK == 0 and t & (t - 1) == 0 and n_pair % RET_PAIRS == 0

    def body(lg_ref, q_ref, k_ref, v_ref, g_ref, w_ref, ret_ref, rg_ref, state_ref):
        hg, i = pl.program_id(0), pl.program_id(1)

        @pl.when(i == 0)
        def _():
            state_ref[...] = jnp.zeros_like(state_ref)

        pairs = range(RET_PAIRS)
        qbs = [_lanes(q_ref, p, LANES) for p in pairs]
        kbs = [_lanes(k_ref, p, LANES) for p in pairs]
        vbs = [_lanes(v_ref, p, pw) for p in pairs]
        zws = [_ret_block(lg_ref, hg * RET_PAIRS + p, i, t, qbs[p], kbs[p]) for p in pairs]
        ps = [(z * w).astype(BF16) for z, w in zws]
        outs = [jnp.concatenate([_dot(ps[p][:t], vbs[p][:, 0:RET_DV]), _dot(ps[p][t:], vbs[p][:, RET_DV:pw])], axis=1)
                + _dot(qbs[p], state_ref[p]) for p in pairs]
        for p in pairs:
            state_ref[p] += _pair_mask() * _dot(kbs[p], vbs[p], _TN)
        for p in pairs:
            for hh in range(2):
                cols = slice(p * pw + hh * RET_DV, p * pw + (hh + 1) * RET_DV)
                o = outs[p][:, hh * RET_DV:(hh + 1) * RET_DV]
                ret_ref[:, cols] = o
                mu = jnp.sum(o, axis=1, keepdims=True) * (1.0 / RET_DV)
                xc = o - mu
                var = jnp.sum(xc * xc, axis=1, keepdims=True) * (1.0 / RET_DV)
                nrm = xc * lax.rsqrt(var + EPS) * w_ref[:, cols]
                g = g_ref[:, cols].astype(F32)
                rg_ref[:, cols] = (g * _sigmoid(g) * nrm).astype(BF16)

    blk = lambda hg, i: (i, hg)
    return _call("ret_fwd", body, (n_pair // RET_PAIRS, s // t),
                 [(log_gamma, None, pltpu.SMEM),
                  (qk_rot, (t, wq), blk),
                  (qk_rot, (t, wq), lambda hg, i: (i, n_pair // RET_PAIRS + hg)),
                  (v_bf, (t, wv), blk),
                  (proj, (t, wv), lambda hg, i: (i, gate_off + hg)),
                  (gn_g, (1, wv), lambda hg, i: (0, hg))],
                 [((s, RET_V), F32, (t, wv), blk), ((s, RET_V), BF16, (t, wv), blk)],
                 scratch=[pltpu.VMEM((RET_PAIRS, LANES, pw), F32)], riders=riders)


def _tri2(tk, strict_upper):
    r = jnp.bitwise_and(lax.broadcasted_iota(jnp.int32, (2 * tk, tk), 0), tk - 1)
    cc = lax.broadcasted_iota(jnp.int32, (2 * tk, tk), 1)
    return ((r > cc) if strict_upper else (r < cc)).astype(BF16)


def _sb_valid(i, j, tq, tk):
    row, col = _tile_pos(i, j, tq, tk)
    return col < row


def _sb_fwd(qkv, tq, tk, riders=None):
    s = qkv.shape[0]
    n_pair = HEADS // 2
    _check_tiles(s, tq, tk, SB_GROUP)

    def body(q_ref, k_ref, v_ref, o_ref, a_ref):
        i = pl.program_id(1)
        upper2 = _tri2(tk, True)
        qs = _stack_heads(q_ref[...])
        n_full, n_groups = _n_full(i, tq, tk, SB_GROUP), _n_groups(i, tq, tk, SB_GROUP)

        def make_step(near_diagonal, last):
            def step(n, carry):
                c, o = carry
                g = last - 1 - n
                js = [g * SB_GROUP + sub for sub in range(SB_GROUP)]
                zs = [_dot(qs, k_ref[_key_rows(j, tk), :], _NT) for j in js]
                log1ps = [jnp.log2(1.0 + jnp.exp2(-jnp.abs(z))) for z in zs]
                log_1ms = [-jnp.maximum(z, 0.0) - t for z, t in zip(zs, log1ps)]
                log_bs = [jnp.minimum(z, 0.0) - t for z, t in zip(zs, log1ps)]
                if near_diagonal:
                    valids = [_sb_valid(i, j, tq, tk) for j in js]
                    log_1ms = [jnp.where(v, l, 0.0) for v, l in zip(valids, log_1ms)]
                sticks = [lax.dot_general(_split_bf16(l), upper2, _NN, preferred_element_type=F32) for l in log_1ms]
                sums = [jnp.sum(l, axis=1, keepdims=True) for l in log_1ms]
                cs = [None] * SB_GROUP
                for sub in reversed(range(SB_GROUP)):
                    cs[sub] = c
                    c = c + sums[sub]
                for sub, j in enumerate(js):
                    a = jnp.exp2(log_bs[sub] + sticks[sub] + cs[sub])
                    if near_diagonal:
                        a = jnp.where(valids[sub], a, 0.0)
                    a = a.astype(BF16)
                    a_ref[j] = a
                    o = o + _dot(_side_by_side(a, tq), _stack_heads(v_ref[_key_rows(j, tk), :]))
                return c, o
            return step

        carry = (jnp.zeros((2 * tq, 1), F32), jnp.zeros((tq, LANES), F32))
        carry = lax.fori_loop(0, n_groups - n_full, make_step(True, n_groups), carry)
        _, acc = lax.fori_loop(0, n_full, make_step(False, n_full), carry)
        o_ref[...] = acc

    n_kb = s // tk
    return _call("sb_fwd", body, (n_pair, s // tq),
                 [(qkv, (tq, LANES), lambda hp, i: (i, hp)),
                  (qkv, (s, LANES), lambda hp, i: (0, n_pair + hp)),
                  (qkv, (s, LANES), lambda hp, i: (0, 2 * n_pair + hp))],
                 [((s, SB_W), F32, (tq, LANES), lambda hp, i: (i, hp)),
                  ((n_pair, s // tq, n_kb, 2 * tq, tk), BF16, (None, None, n_kb, 2 * tq, tk),
                   lambda hp, i: (hp, i, 0, 0, 0))], riders=riders)


def _merge(retg, sb, w_ret, w_sb_t, proj, tm, tn, riders=None):
    s, d = retg.shape[0], w_ret.shape[1]
    ar_off = (2 * RET_QK + 2 * RET_V + 3 * SB_W) // tn
    as_off = ar_off + d // tn

    def body(rg_ref, sb_ref, wr_ref, ws_ref, ar_ref, as_ref, mix_ref, r_ref, s_ref):
        rr = _dot(rg_ref[...], wr_ref[...])
        ss = _dot(sb_ref[...], ws_ref[...], _NT)
        mix_ref[...] = (_sigmoid(ar_ref[...].astype(F32)) * rr + _sigmoid(as_ref[...].astype(F32)) * ss).astype(BF16)
        r_ref[...] = rr.astype(BF16)
        s_ref[...] = ss.astype(BF16)

    tile = (tm, tn)
    return _call("merge", body, (d // tn, s // tm),
                 [(retg, (tm, RET_V), lambda j, i: (i, 0)), (sb, (tm, SB_W), lambda j, i: (i, 0)),
                  (w_ret, (RET_V, tn), lambda j, i: (0, j)), (w_sb_t, (tn, SB_W), lambda j, i: (j, 0)),
                  (proj, tile, lambda j, i: (i, ar_off + j)), (proj, tile, lambda j, i: (i, as_off + j))],
                 [((s, d), BF16, tile, lambda j, i: (i, j))] * 3, riders=riders)


def _out_proj(mixed, w_out, x, mod, gp1, g2, tm):
    s, d = x.shape

    def body(a_ref, w_ref, x_ref, mod_ref, gp_ref, g2_ref, y_ref, hres_ref, h2_ref):
        for rows in _pieces(tm):
            y = _dot(a_ref[rows, :], w_ref[...])
            y_ref[rows, :] = y
            ny, _ = _rms(y, d)
            hres = x_ref[rows, :] + mod_ref[:, 2 * d:3 * d] * (ny * gp_ref[...])
            hres_ref[rows, :] = hres
            n2, _ = _rms(hres, d)
            h2_ref[rows, :] = (n2 * g2_ref[...] * (1.0 + mod_ref[:, 4 * d:5 * d]) + mod_ref[:, 3 * d:4 * d]).astype(BF16)

    row = lambda i: (i, 0)
    fix = lambda i: (0, 0)
    return _call("out_proj", body, (s // tm,),
                 [(mixed, (tm, d), row), (w_out, (d, d), fix), (x, (tm, d), row),
                  (mod, (1, 6 * d), fix), (gp1, (1, d), fix), (g2, (1, d), fix)],
                 [((s, d), F32, (tm, d), row), ((s, d), F32, (tm, d), row), ((s, d), BF16, (tm, d), row)])


def _ff1(h2, w_ff1_t, tm, tn):
    s, f = h2.shape[0], w_ff1_t.shape[0]
    tm = min(tm, s)

    def body(a_ref, w_ref, u_ref, act_ref):
        u = _dot(a_ref[...], w_ref[...], _NT)
        r = jnp.maximum(u, 0.0)
        u_ref[...] = u.astype(BF16)
        act_ref[...] = (r * r).astype(BF16)

    d = h2.shape[1]
    return _call("ff1", body, (f // tn, s // tm),
                 [(h2, (tm, d), lambda j, i: (i, 0)), (w_ff1_t, (tn, d), lambda j, i: (j, 0))],
                 [((s, f), BF16, (tm, tn), lambda j, i: (i, j))] * 2)


def _ff2_loss(act, w_ff2, hres, target, mod, gp2, tm):
    s, d = hres.shape
    f = act.shape[1]

    def body(a_ref, w_ref, h_ref, t_ref, mod_ref, gp_ref, dout_ref, df_ref, loss_ref, dgt_ref, dgp_ref):
        _zero_at_start([loss_ref, dgt_ref, dgp_ref])
        gt, gp = mod_ref[:, 5 * d:6 * d], gp_ref[...]
        for rows in _pieces(tm):
            ff = _dot(a_ref[rows, :], w_ref[...])
            nf, rf = _rms(ff, d)
            out = h_ref[rows, :] + gt * (nf * gp)
            err = out - t_ref[rows, :]
            sq = jnp.sum(err * err, axis=1, keepdims=True)
            loss_ref[...] += jnp.sum(sq, axis=0, keepdims=True)
            dout = err * (1.0 / d)
            dout_ref[rows, :] = dout
            dgt_ref[...] += _colsum(dout * (nf * gp))
            dgp_ref[...] += _colsum(dout * gt * nf)
            df_ref[rows, :] = _rms_bwd(dout * gt * gp, nf, rf, d).astype(BF16)

    row = lambda i: (i, 0)
    fix = lambda i: (0, 0)
    return _call("ff2_loss", body, (s // tm,),
                 [(act, (tm, f), row), (w_ff2, (f, d), fix), (hres, (tm, d), row), (target, (tm, d), row),
                  (mod, (1, 6 * d), fix), (gp2, (1, d), fix)],
                 [((s, d), F32, (tm, d), row), ((s, d), BF16, (tm, d), row), ((1, 1), F32, (1, 1), fix),
                  ((1, d), F32, (1, d), fix), ((1, d), F32, (1, d), fix)])


def _ff2_bwd(df, w_ff2, u, tm, tn):
    s, d = df.shape
    f = w_ff2.shape[0]
    tm = min(tm, s)

    def body(a_ref, w_ref, u_ref, du_ref):
        da = _dot(a_ref[...], w_ref[...], _NT)
        du_ref[...] = (da * (2.0 * jnp.maximum(u_ref[...].astype(F32), 0.0))).astype(BF16)

    return _call("ff2_bwd", body, (f // tn, s // tm),
                 [(df, (tm, d), lambda j, i: (i, 0)), (w_ff2, (tn, d), lambda j, i: (j, 0)),
                  (u, (tm, tn), lambda j, i: (i, j))],
                 [((s, f), BF16, (tm, tn), lambda j, i: (i, j))])[0]


def _ff1_bwd(du, w_ff1_t, hres, dout, y, mod, g2, gp1, tm, riders=None):
    s, d = hres.shape
    f = du.shape[1]

    def body(a_ref, w_ref, h_ref, do_ref, y_ref, mod_ref, g2_ref, gp_ref,
             dh_ref, dy_ref, dsh_ref, dsc_ref, dg2_ref, dgt_ref, dgp_ref):
        _zero_at_start([dsh_ref, dsc_ref, dg2_ref, dgt_ref, dgp_ref])
        g2, sc2 = g2_ref[...], mod_ref[:, 4 * d:5 * d]
        gt, gp = mod_ref[:, 2 * d:3 * d], gp_ref[...]
        for rows in _pieces(tm):
            dh2 = _dot(a_ref[rows, :], w_ref[...])
            n2, r2 = _rms(h_ref[rows, :], d)
            dsh_ref[...] += _colsum(dh2)
            dsc_ref[...] += _colsum(dh2 * n2 * g2)
            dg2_ref[...] += _colsum(dh2 * n2 * (1.0 + sc2))
            dhres = do_ref[rows, :] + _rms_bwd(dh2 * g2 * (1.0 + sc2), n2, r2, d)
            dh_ref[rows, :] = dhres
            ny, ry = _rms(y_ref[rows, :], d)
            dgt_ref[...] += _colsum(dhres * (ny * gp))
            dgp_ref[...] += _colsum(dhres * gt * ny)
            dy_ref[rows, :] = _rms_bwd(dhres * gt * gp, ny, ry, d).astype(BF16)

    row = lambda i: (i, 0)
    fix = lambda i: (0, 0)
    vec = ((1, d), F32, (1, d), fix)
    return _call("ff1_bwd", body, (s // tm,),
                 [(du, (tm, f), row), (w_ff1_t, (f, d), fix), (hres, (tm, d), row), (dout, (tm, d), row),
                  (y, (tm, d), row), (mod, (1, 6 * d), fix), (g2, (1, d), fix), (gp1, (1, d), fix)],
                 [((s, d), F32, (tm, d), row), ((s, d), BF16, (tm, d), row), vec, vec, vec, vec, vec], riders=riders)


def _out_bwd(dy, w_out, proj, r_bf, s_bf, tm, tn):
    s, d = dy.shape
    ar_off = (2 * RET_QK + 2 * RET_V + 3 * SB_W) // tn
    as_off = ar_off + d // tn

    def body(a_ref, w_ref, ar_ref, as_ref, r_ref, s_ref, dr_ref, ds_ref, dar_ref, das_ref):
        dm = _dot(a_ref[...], w_ref[...], _NT)
        sr, ss = _sigmoid(ar_ref[...].astype(F32)), _sigmoid(as_ref[...].astype(F32))
        dr_ref[...] = (dm * sr).astype(BF16)
        ds_ref[...] = (dm * ss).astype(BF16)
        dar_ref[...] = (dm * r_ref[...].astype(F32) * sr * (1.0 - sr)).astype(BF16)
        das_ref[...] = (dm * s_ref[...].astype(F32) * ss * (1.0 - ss)).astype(BF16)

    tile = (tm, tn)
    here = lambda j, i: (i, j)
    return _call("out_bwd", body, (d // tn, s // tm),
                 [(dy, (tm, d), lambda j, i: (i, 0)), (w_out, (tn, d), lambda j, i: (j, 0)),
                  (proj, tile, lambda j, i: (i, ar_off + j)), (proj, tile, lambda j, i: (i, as_off + j)),
                  (r_bf, tile, here), (s_bf, tile, here)],
                 [((s, d), BF16, tile, here)] * 4)


def _gn_bwd(dretg, ret, proj, gn_g, tm, riders=None):
    s = ret.shape[0]
    gate_off = (2 * RET_QK + RET_V) // RET_V

    def body(d_ref, r_ref, g_ref, w_ref, dg_ref, dret_ref, dw_ref):
        first = pl.program_id(0) == 0
        for h in range(HEADS):
            cols = slice(h * RET_DV, (h + 1) * RET_DV)
            o, g, w, dr = r_ref[:, cols], g_ref[:, cols].astype(F32), w_ref[:, cols], d_ref[:, cols].astype(F32)
            mu = jnp.sum(o, axis=1, keepdims=True) * (1.0 / RET_DV)
            xc = o - mu
            rstd = lax.rsqrt(jnp.sum(xc * xc, axis=1, keepdims=True) * (1.0 / RET_DV) + EPS)
            n = xc * rstd
            sg = _sigmoid(g)
            silu = g * sg
            dg_ref[:, cols] = (dr * n * w * (sg * (1.0 + g * (1.0 - sg)))).astype(BF16)
            _accum(dw_ref.at[:, cols], _colsum(dr * silu * n), first)
            dn = dr * silu * w
            m1 = jnp.sum(dn, axis=1, keepdims=True) * (1.0 / RET_DV)
            m2 = jnp.sum(dn * n, axis=1, keepdims=True) * (1.0 / RET_DV)
            dret_ref[:, cols] = (rstd * (dn - m1 - n * m2)).astype(BF16)

    row = lambda i: (i, 0)
    fix = lambda i: (0, 0)
    return _call("gn_bwd", body, (s // tm,),
                 [(dretg, (tm, RET_V), row), (ret, (tm, RET_V), row),
                  (proj, (tm, RET_V), lambda i: (i, gate_off)), (gn_g, (1, RET_V), fix)],
                 [((s, RET_V), BF16, (tm, RET_V), row), ((s, RET_V), BF16, (tm, RET_V), row),
                  ((1, RET_V), F32, (1, RET_V), fix)], riders=riders)


def _ret_bwd(qk_rot, v_bf, dret, log_gamma, t, riders=None):
    s = qk_rot.shape[0]
    n_pair = HEADS // 2
    pw = 2 * RET_DV
    wq, wv = RET_PAIRS * LANES, RET_PAIRS * pw
    n_blk = s // t
    pairs = range(RET_PAIRS)

    def load(q_ref, k_ref, v_ref, do_ref):
        return ([_lanes(q_ref, p, LANES) for p in pairs], [_lanes(k_ref, p, LANES) for p in pairs],
                [_lanes(v_ref, p, pw) for p in pairs], [_lanes(do_ref, p, pw) for p in pairs])

    def d_scores(lg_ref, hp, i, qb, kb, vb, dob):
        z, w = _ret_block(lg_ref, hp, i, t, qb, kb)
        dp = jnp.concatenate([_dot(dob[:, 0:RET_DV], vb[:, 0:RET_DV], _NT),
                              _dot(dob[:, RET_DV:pw], vb[:, RET_DV:pw], _NT)], axis=0)
        return (z * w).astype(BF16), (dp * w).astype(BF16)

    def up_body(lg_ref, q_ref, k_ref, v_ref, do_ref, dq_ref, state_ref):
        hg, i = pl.program_id(0), pl.program_id(1)

        @pl.when(i == 0)
        def _():
            state_ref[...] = jnp.zeros_like(state_ref)

        qbs, kbs, vbs, dobs = load(q_ref, k_ref, v_ref, do_ref)
        dss = [d_scores(lg_ref, hg * RET_PAIRS + p, i, qbs[p], kbs[p], vbs[p], dobs[p])[1] for p in pairs]
        for p in pairs:
            dq_ref[:, p * LANES:(p + 1) * LANES] = (_dot(_side_by_side(dss[p], t), _stack_heads(kbs[p]))
                                                    + _dot(dobs[p], state_ref[p], _NT))
        for p in pairs:
            state_ref[p] += _pair_mask() * _dot(kbs[p], vbs[p], _TN)

    def down_body(lg_ref, q_ref, k_ref, v_ref, do_ref, dk_ref, dv_ref, state_ref):
        hg, i = pl.program_id(0), n_blk - 1 - pl.program_id(1)

        @pl.when(pl.program_id(1) == 0)
        def _():
            state_ref[...] = jnp.zeros_like(state_ref)

        qbs, kbs, vbs, dobs = load(q_ref, k_ref, v_ref, do_ref)
        both = [d_scores(lg_ref, hg * RET_PAIRS + p, i, qbs[p], kbs[p], vbs[p], dobs[p]) for p in pairs]
        for p in pairs:
            pp, ds = both[p]
            later = state_ref[p]
            dv_ref[:, p * pw:(p + 1) * pw] = jnp.concatenate(
                [_dot(pp[:t], dobs[p][:, 0:RET_DV], _TN), _dot(pp[t:], dobs[p][:, RET_DV:pw], _TN)],
                axis=1) + _dot(kbs[p], later)
            dk_ref[:, p * LANES:(p + 1) * LANES] = _dot(ds, _stack_heads(qbs[p]), _TN) + _dot(vbs[p], later, _NT)
        for p in pairs:
            state_ref[p] += _pair_mask() * _dot(qbs[p], dobs[p], _TN)

    n_grp = n_pair // RET_PAIRS

    def ins(order):
        return [(log_gamma, None, pltpu.SMEM),
                (qk_rot, (t, wq), lambda hg, i: (order(i), hg)),
                (qk_rot, (t, wq), lambda hg, i: (order(i), n_grp + hg)),
                (v_bf, (t, wv), lambda hg, i: (order(i), hg)),
                (dret, (t, wv), lambda hg, i: (order(i), hg))]

    up = lambda i: i
    down = lambda i: n_blk - 1 - i
    scratch = [pltpu.VMEM((RET_PAIRS, LANES, pw), F32)]
    dq = _call("ret_bwd_q", up_body, (n_grp, n_blk), ins(up),
               [((s, RET_QK), F32, (t, wq), lambda hg, i: (i, hg))], scratch=scratch)[0]
    dk, dv, *rest = _call("ret_bwd_kv", down_body, (n_grp, n_blk), ins(down),
                          [((s, RET_QK), F32, (t, wq), lambda hg, i: (down(i), hg)),
                           ((s, RET_V), F32, (t, wv), lambda hg, i: (down(i), hg))],
                          scratch=scratch, riders=riders)
    return [dq, dk, dv] + rest


def _sb_bwd(qkv, weights, do, tq, tk, riders=None):
    s = qkv.shape[0]
    n_pair = HEADS // 2
    _check_tiles(s, tq, tk, SB_GROUP)

    def body(q_ref, k_ref, v_ref, a_ref, do_ref, dq_ref, dk_ref, dv_ref):
        i = pl.program_id(1)

        @pl.when(i == 0)
        def _():
            dk_ref[...] = jnp.zeros_like(dk_ref)
            dv_ref[...] = jnp.zeros_like(dv_ref)

        lower2 = _tri2(tk, False)
        qs = _stack_heads(q_ref[...])
        dos = _stack_heads(do_ref[...].astype(BF16))

        def make_step(near_diagonal):
            def step(g, carry):
                c_e, dq = carry
                js = [g * SB_GROUP + sub for sub in range(SB_GROUP)]
                rows = [_key_rows(j, tk) for j in js]
                zs = [_dot(qs, k_ref[rw, :], _NT) for rw in rows]
                das = [_dot(dos, v_ref[rw, :], _NT) for rw in rows]
                avals = [a_ref[j] for j in js]
                for a, rw in zip(avals, rows):
                    dv_ref[rw, :] += _dot(a, dos, _TN)
                es = [a.astype(F32) * da for a, da in zip(avals, das)]
                prefixes = [lax.dot_general(_split_bf16(e), lower2, _NN, preferred_element_type=F32) for e in es]
                betas = [1.0 / (1.0 + jnp.exp2(-z)) for z in zs]
                for sub in range(SB_GROUP):
                    dz = es[sub] - (es[sub] + prefixes[sub] + c_e) * betas[sub]
                    if near_diagonal:
                        dz = jnp.where(_sb_valid(i, js[sub], tq, tk), dz, 0.0)
                    dz = dz.astype(BF16)
                    dk_ref[rows[sub], :] += _dot(dz, qs, _TN)
                    dq = dq + _dot(_side_by_side(dz, tq), _stack_heads(k_ref[rows[sub], :]))
                    c_e = c_e + jnp.sum(es[sub], axis=1, keepdims=True)
                return c_e, dq
            return step

        n_full = _n_full(i, tq, tk, SB_GROUP)
        carry = (jnp.zeros((2 * tq, 1), F32), jnp.zeros((tq, LANES), F32))
        carry = lax.fori_loop(0, n_full, make_step(False), carry)
        _, dq = lax.fori_loop(n_full, _n_groups(i, tq, tk, SB_GROUP), make_step(True), carry)
        dq_ref[...] = dq

    blk = lambda hp, i: (i, hp)
    n_kb = s // tk
    return _call("sb_bwd", body, (n_pair, s // tq),
                 [(qkv, (tq, LANES), blk),
                  (qkv, (s, LANES), lambda hp, i: (0, n_pair + hp)),
                  (qkv, (s, LANES), lambda hp, i: (0, 2 * n_pair + hp)),
                  (weights, (None, None, n_kb, 2 * tq, tk), lambda hp, i: (hp, i, 0, 0, 0)),
                  (do, (tq, LANES), blk)],
                 [((s, SB_W), F32, (tq, LANES), blk),
                  ((s, SB_W), F32, (s, LANES), lambda hp, i: (0, hp)),
                  ((s, SB_W), F32, (s, LANES), lambda hp, i: (0, hp))], riders=riders)


def _assemble_dproj(dq_r, dk_r, dv_r, dg_r, dq_s, dk_s, dv_s, da_r, da_s, cos, sin, idx_col, lg_lanes, tm):
    s, d = da_r.shape
    width = 2 * RET_QK + 2 * RET_V + 3 * SB_W + 2 * d

    def body(dq_ref, dk_ref, dv_ref, dg_ref, dqs_ref, dks_ref, dvs_ref, dar_ref, das_ref, cos_ref, sin_ref,
             idx_ref, lg_ref, o_ref):
        lane = lax.broadcasted_iota(jnp.int32, (1, LANES), 1)
        first = jnp.bitwise_and(lane, RET_DQK - 1) < (RET_DQK // 2)
        cos, sin = cos_ref[...], sin_ref[...]
        idx = idx_ref[...]
        for src, base, sign, scale in ((dq_ref, 0, 1.0, 1.0), (dk_ref, RET_QK, -1.0, RET_DQK ** -0.5)):
            for g in range(RET_QK // LANES):
                v = src[:, g * LANES:(g + 1) * LANES] * (_decay_scale(lg_ref, idx, g, sign) * scale)
                sw = jnp.where(first, pltpu.roll(v, LANES - RET_DQK // 2, 1), pltpu.roll(v, RET_DQK // 2, 1))
                o_ref[:, base + g * LANES:base + (g + 1) * LANES] = (v * cos - sw * sin).astype(BF16)
        off = 2 * RET_QK
        o_ref[:, off:off + RET_V] = dv_ref[...].astype(BF16)
        off += RET_V
        o_ref[:, off:off + RET_V] = dg_ref[...]
        off += RET_V
        o_ref[:, off:off + SB_W] = (dqs_ref[...] * (SB_DH ** -0.5)).astype(BF16)
        off += SB_W
        o_ref[:, off:off + SB_W] = (dks_ref[...] * LN2).astype(BF16)
        off += SB_W
        o_ref[:, off:off + SB_W] = dvs_ref[...].astype(BF16)
        off += SB_W
        o_ref[:, off:off + d] = dar_ref[...]
        off += d
        o_ref[:, off:off + d] = das_ref[...]

    row = lambda i: (i, 0)
    ins = [(a, (tm, a.shape[1]), row) for a in (dq_r, dk_r, dv_r, dg_r, dq_s, dk_s, dv_s, da_r, da_s, cos, sin, idx_col)]
    ins.append((lg_lanes, (1, RET_QK), lambda i: (0, 0)))
    return _call("assemble_dproj", body, (s // tm,), ins, [((s, width), BF16, (tm, width), row)])[0]


def _in_bwd(dproj, w_in_t, x, dhres, mod, g1, tm, riders=None):
    s, d = x.shape
    width = dproj.shape[1]

    def body(a_ref, w_ref, x_ref, dh_ref, mod_ref, g_ref, dx_ref, dsh_ref, dsc_ref, dg_ref):
        _zero_at_start([dsh_ref, dsc_ref, dg_ref])
        g1, sc1 = g_ref[...], mod_ref[:, d:2 * d]
        for rows in _pieces(tm):
            dh = _dot(a_ref[rows, :], w_ref[...])
            n1, r1 = _rms(x_ref[rows, :], d)
            dsh_ref[...] += _colsum(dh)
            dsc_ref[...] += _colsum(dh * n1 * g1)
            dg_ref[...] += _colsum(dh * n1 * (1.0 + sc1))
            dx_ref[rows, :] = dh_ref[rows, :] + _rms_bwd(dh * g1 * (1.0 + sc1), n1, r1, d)

    row = lambda i: (i, 0)
    fix = lambda i: (0, 0)
    vec = ((1, d), F32, (1, d), fix)
    return _call("in_bwd", body, (s // tm,),
                 [(dproj, (tm, width), row), (w_in_t, (width, d), fix), (x, (tm, d), row), (dhres, (tm, d), row),
                  (mod, (1, 6 * d), fix), (g1, (1, d), fix)],
                 [((s, d), F32, (tm, d), row), vec, vec, vec], riders=riders)


def _adamw(w, g, m, v):
    m = ADAM_B1 * m + (1.0 - ADAM_B1) * g
    v = ADAM_B2 * v + (1.0 - ADAM_B2) * (g * g)
    m_hat = m / (1.0 - ADAM_B1 ** ADAM_STEP)
    v_hat = v / (1.0 - ADAM_B2 ** ADAM_STEP)
    delta = -ADAM_LR * (m_hat / (jnp.sqrt(v_hat) + ADAM_EPS) + ADAM_WD * w)
    return delta, m, v


def _adam_reduce(name, parts, w, m, v, tr):
    rws, cls = w.shape
    tr = min(tr, rws)
    n_parts = parts.shape[0]

    def body(p_ref, w_ref, m_ref, v_ref, g_out, d_out, m_out, v_out):
        g = p_ref[0].astype(F32)
        for k in range(1, n_parts):
            g = g + p_ref[k].astype(F32)
        delta, mn, vn = _adamw(w_ref[...], g, m_ref[...], v_ref[...])
        g_out[...] = g
        d_out[...] = delta
        m_out[...] = mn
        v_out[...] = vn

    row = lambda i: (i, 0)
    blk = (tr, cls)
    return _call(name, body, (rws // tr,),
                 [(parts, (n_parts, tr, cls), lambda i: (0, i, 0)), (w, blk, row), (m, blk, row), (v, blk, row)],
                 [((rws, cls), F32, blk, row)] * 4)


def _ada_bwd_adam(cs_t, dmod_cols, w, m, v):
    d, nc = w.shape

    def body(c_ref, dm_ref, w_ref, m_ref, v_ref, g_out, d_out, m_out, v_out):
        g = c_ref[0] * dm_ref[0:1, :]
        for r in range(1, N_DEV):
            g = g + c_ref[r] * dm_ref[r:r + 1, :]
        delta, mn, vn = _adamw(w_ref[...], g, m_ref[...], v_ref[...])
        g_out[...] = g
        d_out[...] = delta
        m_out[...] = mn
        v_out[...] = vn

    fix = lambda i: (0, 0)
    blk = (d, nc)
    return _call("ada_bwd_adam", body, (1,),
                 [(cs_t, (N_DEV, d, 1), lambda i: (0, 0, 0)), (dmod_cols, (N_DEV, nc), fix), (w, blk, fix), (m, blk, fix), (v, blk, fix)],
                 [((d, nc), F32, blk, fix)] * 4)


def _small_adam(parts, w, m, v):
    n = w.shape[1]

    def body(p_ref, w_ref, m_ref, v_ref, g_out, d_out, m_out, v_out):
        g = p_ref[0:1, :]
        for k in range(1, N_DEV):
            g = g + p_ref[k:k + 1, :]
        delta, mn, vn = _adamw(w_ref[...], g, m_ref[...], v_ref[...])
        g_out[...] = g
        d_out[...] = delta
        m_out[...] = mn
        v_out[...] = vn

    fix = lambda i: (0, 0)
    return _call("small_adam", body, (1,),
                 [(parts, (N_DEV, n), fix), (w, (1, n), fix), (m, (1, n), fix), (v, (1, n), fix)],
                 [((1, n), F32, (1, n), fix)] * 4)


def kernel(x, c, positions, ada_w, ada_b, pre_mix_g, post_mix_g, pre_ffn_g, post_ffn_g, w_in, ret_gn_g, w_ret_branch, w_sb_branch, w_out, w_ff1, w_ff2, loss_target, m_ada_w, m_ada_b, m_pre_mix_g, m_post_mix_g, m_pre_ffn_g, m_post_ffn_g, m_w_in, m_ret_gn_g, m_w_ret_branch, m_w_sb_branch, m_w_out, m_w_ff1, m_w_ff2, v_ada_w, v_ada_b, v_pre_mix_g, v_post_mix_g, v_pre_ffn_g, v_post_ffn_g, v_w_in, v_ret_gn_g, v_w_ret_branch, v_w_sb_branch, v_w_out, v_w_ff1, v_w_ff2):
    _, s, d = x.shape
    d_ff = w_ff1.shape[2] * N_DEV
    d_in = w_in.shape[2] * N_DEV
    me = 4 * lax.axis_index("x") + 2 * lax.axis_index("y") + lax.axis_index("c")
    x2, tgt = x[0], loss_target[0]

    core = lax.axis_index("c").astype(jnp.int32).reshape(1)
    bf = lambda w: w[0].astype(BF16)

    w_in_t, m_in_t, v_in_t = (jnp.swapaxes(a[0], 0, 1) for a in (w_in, m_w_in, v_w_in))

    c_all, g_in = _exchange("gather_in", [c, w_in_t.astype(BF16)], ["gather", "gather_chip"])
    c_all = c_all.reshape(N_DEV, d)

    n_ada = ada_w.shape[2]
    cs_all = _silu_rows(c_all)
    ada_b_cols = lax.dynamic_slice(ada_b, (0, me * n_ada), (1, n_ada))
    mod_cols = _ada_fwd(cs_all, ada_w[0], ada_b_cols)
    mod_all = _exchange("gather_mod", [mod_cols], ["gather"])[0]
    mod = lax.dynamic_index_in_dim(mod_all, me, axis=1, keepdims=False).reshape(1, 6 * d)

    tm = min(256, s)
    h, g_in = _pre_norm(x2, pre_mix_g, mod, tm, riders=([g_in], ["forward"]))
    wt_in = g_in.reshape(d_in, d)
    bf_t = lambda w: jnp.swapaxes(w[0], 0, 1).astype(BF16)
    proj, g_ff1 = _matmul("in_proj", h, wt_in, "nt", s, 512, BF16, riders=([bf_t(w_ff1)], ["gather_chip"]))
    pos_col = positions.reshape(s, 1).astype(F32)
    freqs = ROPE_BASE ** (-jnp.arange(0, RET_DQK, 2, dtype=F32) / RET_DQK)
    inv_freq = jnp.tile(freqs, LANES // (RET_DQK // 2)).reshape(1, LANES)
    log_gamma_np = np.log1p(-(2.0 ** (-5.0 - np.arange(HEADS))))
    log_gamma = jnp.asarray(log_gamma_np, F32)
    lg_lanes = jnp.asarray(np.repeat(log_gamma_np, RET_DQK).reshape(1, RET_QK), F32)
    idx_col = (jnp.arange(s, dtype=F32) - (s // 2)).reshape(s, 1)
    qk_rot, v_bf, qkv_sb, cos_t, sin_t = _prep(proj, pos_col, idx_col, inv_freq, lg_lanes, tm)
    tq, tk = min(256, s), min(128, s)
    later = [bf(w_ret_branch), bf_t(w_sb_branch), bf(w_out), bf(w_ff2)]
    sb, sb_weights, *later = _sb_fwd(qkv_sb, tq, tk, riders=(later, ["gather_chip"] * 4))
    ret, retg, g_ret, g_sb, g_out, g_ff2, g_ff1 = _ret_fwd(qk_rot, v_bf, proj, ret_gn_g, log_gamma, tq,
                                                           riders=(later + [g_ff1], ["forward"] * 5))
    wf_ret = g_ret.reshape(RET_V, d)
    wt_sb = g_sb.reshape(d, SB_W)
    wf_out = g_out.reshape(d, d)
    wt_ff1 = g_ff1.reshape(d_ff, d)
    wf_ff2 = g_ff2.reshape(d_ff, d)
    mixed, r_bf, s_bf = _merge(retg, sb, wf_ret, wt_sb, proj, tm, min(512, d))
    y, hres, h2 = _out_proj(mixed, wf_out, x2, mod, post_mix_g, pre_ffn_g, tm)
    u, act = _ff1(h2, wt_ff1, s, 512)
    dout, df, loss_sum, d_gt2, d_gp2 = _ff2_loss(act, wf_ff2, hres, tgt, mod, post_ffn_g, tm)

    du = _ff2_bwd(df, wf_ff2, u, s, 512)
    gw_ff2 = _matmul("grad_w_ff2", act, df, "tn", 512, d, BF16).reshape(N_DEV, d_ff // N_DEV, d)
    gw_ff1 = _matmul("grad_w_ff1", h2, du, "tn", d, d_ff // N_DEV, BF16, blocked_out=True)
    dhres, dy, d_sh2, d_sc2, d_g2, d_gt1, d_gp1, t_ff1, t_ff2 = _ff1_bwd(
        du, wt_ff1, hres, dout, y, mod, pre_ffn_g, post_mix_g, tm, riders=([gw_ff1, gw_ff2], ["pair"] * 2))
    s_ff1 = _pair_sum("pair_sum_ff1", gw_ff1, t_ff1, core, 256)
    s_ff2 = _pair_sum("pair_sum_ff2", gw_ff2, t_ff2, core, 256)
    d_r, d_s, da_r, da_s = _out_bwd(dy, wf_out, proj, r_bf, s_bf, tm, min(512, d))
    gw_out = _matmul("grad_w_out", mixed, dy, "tn", 512, d, BF16).reshape(N_DEV, d // N_DEV, d)
    dretg = _matmul("ret_branch_bwd", d_r, wf_ret, "nt", s, 512, BF16)
    dsb = _matmul("sb_branch_bwd", d_s, wt_sb, "nn", s, 512, F32)
    gw_ret = _matmul("grad_w_ret", retg, d_r, "tn", 512, d, BF16).reshape(N_DEV, RET_V // N_DEV, d)
    gw_sb = _matmul("grad_w_sb", sb, d_s, "tn", 512, d // N_DEV, BF16, blocked_out=True)
    dq_s, dk_s, dv_s, p_ff1, p_ff2, p_ret = _sb_bwd(qkv_sb, sb_weights, dsb, tq, tk,
                                                    riders=([s_ff1, s_ff2, gw_ret], ["chip_scatter"] * 2 + ["scatter"]))
    dg_r, dret, d_gn = _gn_bwd(dretg, ret, proj, ret_gn_g, tm)
    dq_r, dk_r, dv_r, p_out, p_sb = _ret_bwd(qk_rot, v_bf, dret, log_gamma, tq,
                                             riders=([gw_out, gw_sb], ["scatter"] * 2))
    dproj = _assemble_dproj(dq_r, dk_r, dv_r, dg_r, dq_s, dk_s, dv_s, da_r, da_s, cos_t, sin_t, idx_col, lg_lanes, tm)
    gw_in = _matmul("grad_w_in", dproj, h, "tn", 512, d, BF16).reshape(N_DEV, d_in // N_DEV, d)
    t_in = _exchange("pair_in", [gw_in], ["pair"])[0]
    tr_in = d_in // N_DEV // 4
    s_in = _pair_sum("pair_sum_in", gw_in, t_in, core, tr_in)
    grad_x, d_sh1, d_sc1, d_g1, p_in = _in_bwd(dproj, wt_in, x2, dhres, mod, pre_mix_g, tm,
                                               riders=([s_in], ["chip_scatter"]))
    loss_lanes = jnp.pad(loss_sum, ((0, 0), (0, LANES - 1)))
    small = jnp.concatenate([d_sh1, d_sc1, d_gt1, d_sh2, d_sc2, d_gt2, d_g1, d_gp1, d_g2, d_gp2, d_gn, loss_lanes], axis=1)
    small_all = _exchange("gather_small", [small], ["gather"])[0].reshape(N_DEV, small.shape[1])
    parts = [p_in, p_ret, p_sb, p_out, p_ff1, p_ff2]

    res = {}
    names = ["w_ret_branch", "w_sb_branch", "w_out", "w_ff1", "w_ff2"]
    ws = [w_ret_branch, w_sb_branch, w_out, w_ff1, w_ff2]
    ms = [m_w_ret_branch, m_w_sb_branch, m_w_out, m_w_ff1, m_w_ff2]
    vs = [v_w_ret_branch, v_w_sb_branch, v_w_out, v_w_ff1, v_w_ff2]
    for nm, p, w, m, v in zip(names, parts[1:], ws, ms, vs):
        res[nm] = [o[None] for o in _adam_reduce("adam_" + nm, p, w[0], m[0], v[0], 256)]
    res["w_in"] = [jnp.swapaxes(o, 0, 1)[None]
                   for o in _adam_reduce("adam_w_in", parts[0], w_in_t, m_in_t, v_in_t, tr_in)]
    dmod_cols = lax.dynamic_slice(small_all, (0, me * n_ada), (N_DEV, n_ada))
    res["ada_w"] = [o[None] for o in _ada_bwd_adam(cs_all.reshape(N_DEV, d, 1), dmod_cols, ada_w[0], m_ada_w[0], v_ada_w[0])]
    vec_names = ["ada_b", "pre_mix_g", "post_mix_g", "pre_ffn_g", "post_ffn_g", "ret_gn_g"]
    cat = lambda xs: jnp.concatenate(xs + [jnp.zeros((1, LANES), F32)], axis=1)
    packed = _small_adam(small_all,
                         cat([ada_b, pre_mix_g, post_mix_g, pre_ffn_g, post_ffn_g, ret_gn_g]),
                         cat([m_ada_b, m_pre_mix_g, m_post_mix_g, m_pre_ffn_g, m_post_ffn_g, m_ret_gn_g]),
                         cat([v_ada_b, v_pre_mix_g, v_post_mix_g, v_pre_ffn_g, v_post_ffn_g, v_ret_gn_g]))
    off = 0
    for nm, width in zip(vec_names, [6 * d, d, d, d, d, RET_V]):
        res[nm] = [p[:, off:off + width] for p in packed]
        off += width

    loss = (0.5 / d) * packed[0][0, off]
    order = ["ada_w", "ada_b", "pre_mix_g", "post_mix_g", "pre_ffn_g", "post_ffn_g", "w_in", "ret_gn_g",
             "w_ret_branch", "w_sb_branch", "w_out", "w_ff1", "w_ff2"]
    outs = [loss, grad_x[None]]
    for k in range(4):
        outs += [res[nm][k] for nm in order]
    return tuple(outs)
```

```python
import functools

import numpy as np
import jax
import jax.numpy as jnp
from jax import lax
from jax.experimental import pallas as pl
from jax.experimental.pallas import tpu as pltpu

F32 = jnp.float32
BF16 = jnp.bfloat16
N_DEV = 8
AXES = ("x", "y", "c")

EPS = 1e-6
CHUNK = 64
CHUNK_SHIFT = 6
HEADS = 8
RET_DQK = 64
RET_DV = 128
SB_DH = 64
RET_QK = HEADS * RET_DQK
RET_V = HEADS * RET_DV
SB_W = HEADS * SB_DH
ROPE_BASE = 10000.0
LANES = 128

ADAM_LR = 0.001
ADAM_B1 = 0.9
ADAM_B2 = 0.999
ADAM_EPS = 1e-08
ADAM_WD = 0.01
ADAM_STEP = 10

VMEM_LIMIT = 56 * 1024 * 1024

_NN = (((1,), (0,)), ((), ()))
_NT = (((1,), (1,)), ((), ()))
_TN = (((0,), (0,)), ((), ()))


def _dot(a, b, dims=_NN):
    if a.dtype != BF16:
        a = a.astype(BF16)
    if b.dtype != BF16:
        b = b.astype(BF16)
    return lax.dot_general(a, b, dims, preferred_element_type=F32)


def _dot_split(a, b):
    hi = a.astype(BF16)
    lo = (a - hi.astype(F32)).astype(BF16)
    return (lax.dot_general(hi, b, _NN, preferred_element_type=F32)
            + lax.dot_general(lo, b, _NN, preferred_element_type=F32))


def _sigmoid(x):
    return 1.0 / (1.0 + jnp.exp(-x))


def _rms(x, d):
    r = lax.rsqrt(jnp.sum(x * x, axis=1, keepdims=True) * (1.0 / d) + EPS)
    return x * r, r


def _rms_bwd(dn, n, r, d):
    return r * (dn - n * (jnp.sum(dn * n, axis=1, keepdims=True) * (1.0 / d)))


def _colsum(v):
    return jnp.sum(v, axis=0, keepdims=True)


def _accum(ref, val, first):
    @pl.when(first)
    def _():
        ref[...] = val

    @pl.when(jnp.logical_not(first))
    def _():
        ref[...] += val


ROW_SPLIT = 2


def _zero_at_start(refs):
    @pl.when(pl.program_id(0) == 0)
    def _():
        for r in refs:
            r[...] = jnp.zeros_like(r)


def _pieces(tm):
    step = tm // ROW_SPLIT
    return [slice(k * step, (k + 1) * step) for k in range(ROW_SPLIT)]


KIND_SLOTS = {"gather": N_DEV, "scatter": N_DEV, "gather_chip": N_DEV, "forward": N_DEV, "pair": N_DEV // 2,
              "chip_scatter": N_DEV // 2}
SEMS_PER_ARRAY = N_DEV - 1


def _exchange_copies(ins, outs, send_sems, recv_sems, local_sems, kinds):
    x, y, c = (lax.axis_index(a) for a in AXES)
    me, chip, sibling = 4 * x + 2 * y + c, 2 * x + y, (x, y, 1 - c)
    mesh_id = pl.DeviceIdType.MESH
    other_chips = []
    for k in range(1, N_DEV // 2):
        px = 1 - x if k & 2 else x
        py = 1 - y if k & 1 else y
        other_chips.append((px, py))
    copies = []
    for i, kind in enumerate(kinds):
        def remote(src, dst, k, to, i=i):
            return pltpu.make_async_remote_copy(
                src_ref=src, dst_ref=dst, send_sem=send_sems.at[i * SEMS_PER_ARRAY + k],
                recv_sem=recv_sems.at[i * SEMS_PER_ARRAY + k], device_id=to, device_id_type=mesh_id)

        if kind in ("gather", "scatter"):
            pick = (lambda ref, d: ref.at[d]) if kind == "scatter" else (lambda ref, d: ref)
            copies.append(pltpu.make_async_copy(pick(ins[i], me), outs[i].at[me], local_sems.at[i]))
            for k in range(1, N_DEV):
                to = (1 - x if k & 4 else x, 1 - y if k & 2 else y, 1 - c if k & 1 else c)
                copies.append(remote(pick(ins[i], 4 * to[0] + 2 * to[1] + to[2]), outs[i].at[me], k - 1, to))
        elif kind == "gather_chip":
            copies.append(pltpu.make_async_copy(ins[i], outs[i].at[me], local_sems.at[i]))
            copies.append(remote(ins[i], outs[i].at[me], 0, sibling))
            for k, (px, py) in enumerate(other_chips):
                copies.append(remote(ins[i], outs[i].at[me], 1 + k, (px, py, c)))
        elif kind == "forward":
            for k, (px, py) in enumerate(other_chips):
                slot = 4 * px + 2 * py + c
                copies.append(remote(outs[i].at[slot], outs[i].at[slot], k, sibling))
        elif kind == "pair":
            for k in range(N_DEV // 2):
                copies.append(remote(ins[i].at[2 * k + 1 - c], outs[i].at[k], k, sibling))
        elif kind == "chip_scatter":
            copies.append(pltpu.make_async_copy(ins[i].at[chip], outs[i].at[chip], local_sems.at[i]))
            for k, (px, py) in enumerate(other_chips):
                copies.append(remote(ins[i].at[2 * px + py], outs[i].at[chip], k, (px, py, c)))
        else:
            raise ValueError(kind)
    return copies


def _exchange_shapes(arrays, kinds):
    shapes = []
    for a, kind in zip(arrays, kinds):
        tail = a.shape if kind in ("gather", "gather_chip") else a.shape[1:]
        shapes.append(jax.ShapeDtypeStruct((KIND_SLOTS[kind],) + tuple(tail), a.dtype))
    return shapes


def _exchange_sems(n):
    return [pltpu.SemaphoreType.DMA((n * SEMS_PER_ARRAY,)), pltpu.SemaphoreType.DMA((n * SEMS_PER_ARRAY,)),
            pltpu.SemaphoreType.DMA((n,))]


def _call(name, body, grid, ins, outs, scratch=(), riders=None, prefetch=None):
    any_spec = pl.BlockSpec(memory_space=pl.ANY)
    in_specs = [pl.BlockSpec(memory_space=im) if bs is None else pl.BlockSpec(bs, im) for _, bs, im in ins]
    out_specs = [pl.BlockSpec(bs, im) for _, _, bs, im in outs]
    out_shape = [jax.ShapeDtypeStruct(s, d) for s, d, _, _ in outs]
    operands = [a for a, _, _ in ins]
    scratch = list(scratch)
    aliases = {}
    n_pre = 0 if prefetch is None else 1
    kernel = functools.partial(body) if prefetch is None else (lambda _, *refs: body(*refs))
    if riders is not None:
        arrays, kinds = riders
        nr, n_in, n_out, n_scr = len(arrays), len(ins), len(outs), len(scratch)

        def kernel(*refs):
            refs = refs[n_pre:]
            own_in, ride_in = refs[:n_in], refs[n_in:n_in + nr]
            own_out = refs[n_in + nr:n_in + nr + n_out]
            ride_out = refs[n_in + nr + n_out:n_in + 2 * nr + n_out]
            own_scr = refs[n_in + 2 * nr + n_out:n_in + 2 * nr + n_out + n_scr]
            sems = refs[n_in + 2 * nr + n_out + n_scr:]
            ids = [pl.program_id(a) for a in range(len(grid))]
            first = functools.reduce(jnp.logical_and, [i == 0 for i in ids])
            last = functools.reduce(jnp.logical_and, [i == g - 1 for i, g in zip(ids, grid)])

            @pl.when(first)
            def _():
                for cp in _exchange_copies(ride_in, ride_out, *sems, kinds):
                    cp.start()

            body(*own_in, *own_out, *own_scr)

            @pl.when(last)
            def _():
                for cp in _exchange_copies(ride_in, ride_out, *sems, kinds):
                    cp.wait()

        in_specs += [any_spec] * nr
        out_specs += [any_spec] * nr
        out_shape += _exchange_shapes(arrays, kinds)
        operands += list(arrays)
        scratch += _exchange_sems(nr)
        aliases = {n_pre + n_in + r: n_out + r for r, kind in enumerate(kinds) if kind == "forward"}
    params = pltpu.CompilerParams(dimension_semantics=("arbitrary",) * len(grid), vmem_limit_bytes=VMEM_LIMIT)
    if prefetch is None:
        return pl.pallas_call(kernel, name=name, grid=grid, in_specs=in_specs, out_specs=out_specs,
                              out_shape=out_shape, scratch_shapes=scratch, input_output_aliases=aliases,
                              compiler_params=params)(*operands)
    grid_spec = pltpu.PrefetchScalarGridSpec(num_scalar_prefetch=1, grid=grid, in_specs=in_specs,
                                             out_specs=out_specs, scratch_shapes=scratch)
    return pl.pallas_call(kernel, name=name, grid_spec=grid_spec, out_shape=out_shape,
                          input_output_aliases=aliases, compiler_params=params)(prefetch, *operands)


def _exchange(name, arrays, kinds):
    n = len(arrays)

    def body(*refs):
        copies = _exchange_copies(refs[:n], refs[n:2 * n], *refs[2 * n:], kinds)
        for cp in copies:
            cp.start()
        for cp in copies:
            cp.wait()

    any_spec = pl.BlockSpec(memory_space=pl.ANY)
    return pl.pallas_call(
        functools.partial(body),
        name=name,
        in_specs=[any_spec] * n,
        out_specs=[any_spec] * n,
        out_shape=_exchange_shapes(arrays, kinds),
        scratch_shapes=_exchange_sems(n),
        input_output_aliases={i: i for i, kind in enumerate(kinds) if kind == "forward"},
    )(*arrays)


def _pair_sum(name, mine, theirs, my_core, tr):
    _, rws, cls = mine.shape
    tr = min(tr, rws)

    def body(a_ref, b_ref, o_ref):
        o_ref[...] = (a_ref[...].astype(F32) + b_ref[...].astype(F32)).astype(o_ref.dtype)

    return _call(name, body, (N_DEV // 2, rws // tr),
                 [(mine, (None, tr, cls), lambda k, r, core: (2 * k + core[0], r, 0)),
                  (theirs, (None, tr, cls), lambda k, r, core: (k, r, 0))],
                 [((N_DEV // 2, rws, cls), mine.dtype, (None, tr, cls), lambda k, r, core: (k, r, 0))],
                 prefetch=my_core)[0]


def _matmul(name, a, b, kind, tm, tn, out_dtype, blocked_out=False, riders=None):
    if kind == "tn":
        kdim, m = a.shape
    else:
        m, kdim = a.shape
    n = b.shape[0] if kind == "nt" else b.shape[1]
    tm, tn = min(tm, m), min(tn, n)
    dims = {"nn": _NN, "nt": _NT, "tn": _TN}[kind]

    def body(a_ref, b_ref, o_ref):
        o_ref[...] = _dot(a_ref[...], b_ref[...], dims).astype(o_ref.dtype)

    a_spec = (a, (kdim, tm), lambda j, i: (0, i)) if kind == "tn" else (a, (tm, kdim), lambda j, i: (i, 0))
    b_spec = (b, (tn, kdim), lambda j, i: (j, 0)) if kind == "nt" else (b, (kdim, tn), lambda j, i: (0, j))
    if blocked_out:
        out = ((n // tn, m, tn), out_dtype, (None, tm, tn), lambda j, i: (j, i, 0))
    else:
        out = ((m, n), out_dtype, (tm, tn), lambda j, i: (i, j))
    res = _call(name, body, (n // tn, m // tm), [a_spec, b_spec], [out], riders=riders)
    return res[0] if riders is None else res


def _ada_fwd(cs_all, ada_w, ada_b_cols):
    def body(c_ref, w_ref, b_ref, o_ref):
        o_ref[...] = lax.dot_general(c_ref[...], w_ref[...], _NN, preferred_element_type=F32,
                                     precision=lax.Precision.HIGHEST) + b_ref[...]

    r, d = cs_all.shape
    nc = ada_w.shape[1]
    return _call("ada_fwd", body, (1,),
                 [(cs_all, (r, d), lambda i: (0, 0)), (ada_w, (d, nc), lambda i: (0, 0)),
                  (ada_b_cols, (1, nc), lambda i: (0, 0))],
                 [((r, nc), F32, (r, nc), lambda i: (0, 0))])[0]


def _silu_rows(c_all):
    def body(c_ref, o_ref):
        v = c_ref[...]
        o_ref[...] = v * _sigmoid(v)

    return _call("silu_c", body, (1,), [(c_all, c_all.shape, lambda i: (0, 0))],
                 [(c_all.shape, F32, c_all.shape, lambda i: (0, 0))])[0]


def _pre_norm(x, g, mod, tm, riders=None):
    s, d = x.shape

    def body(x_ref, g_ref, mod_ref, h_ref):
        n, _ = _rms(x_ref[...], d)
        sh, sc = mod_ref[:, 0:d], mod_ref[:, d:2 * d]
        h_ref[...] = (n * g_ref[...] * (1.0 + sc) + sh).astype(BF16)

    return _call("pre_norm", body, (s // tm,),
                 [(x, (tm, d), lambda i: (i, 0)), (g, (1, d), lambda i: (0, 0)),
                  (mod, (1, 6 * d), lambda i: (0, 0))],
                 [((s, d), BF16, (tm, d), lambda i: (i, 0))], riders=riders)


LOG2E = 1.4426950408889634
LN2 = 0.6931471805599453


def _decay_scale(lg_ref, idx, g, sign):
    return jnp.exp((sign * idx) * lg_ref[:, g * LANES:(g + 1) * LANES])


def _prep(proj, pos_col, idx_col, inv_freq, lg_lanes, tm):
    s = proj.shape[0]
    sb_off = (2 * RET_QK + 2 * RET_V) // (3 * SB_W)
    n_q = RET_QK // LANES

    def body(qk_ref, v_ref, sb_ref, pos_ref, idx_ref, f_ref, lg_ref, qk_out, v_out, sb_out, cos_out, sin_out):
        ang = pos_ref[...] * f_ref[...]
        lane = lax.broadcasted_iota(jnp.int32, (1, LANES), 1)
        first = jnp.bitwise_and(lane, RET_DQK - 1) < (RET_DQK // 2)
        cos = jnp.cos(ang)
        sin = jnp.where(first, -1.0, 1.0) * jnp.sin(ang)
        cos_out[...] = cos
        sin_out[...] = sin
        idx = idx_ref[...]
        for g in range(2 * n_q):
            v = qk_ref[:, g * LANES:(g + 1) * LANES].astype(F32)
            sw = jnp.where(first, pltpu.roll(v, LANES - RET_DQK // 2, 1), pltpu.roll(v, RET_DQK // 2, 1))
            r = v * cos + sw * sin
            if g < n_q:
                r = r * _decay_scale(lg_ref, idx, g, 1.0)
            else:
                r = r * (_decay_scale(lg_ref, idx, g - n_q, -1.0) * (RET_DQK ** -0.5))
            qk_out[:, g * LANES:(g + 1) * LANES] = r.astype(BF16)
        v_out[...] = v_ref[...].astype(BF16)
        sb_out[:, 0:SB_W] = (sb_ref[:, 0:SB_W].astype(F32) * (SB_DH ** -0.5 * LOG2E)).astype(BF16)
        sb_out[:, SB_W:3 * SB_W] = sb_ref[:, SB_W:3 * SB_W].astype(BF16)

    return _call("prep", body, (s // tm,),
                 [(proj, (tm, 2 * RET_QK), lambda i: (i, 0)),
                  (proj, (tm, RET_V), lambda i: (i, 2 * RET_QK // RET_V)),
                  (proj, (tm, 3 * SB_W), lambda i: (i, sb_off)),
                  (pos_col, (tm, 1), lambda i: (i, 0)),
                  (idx_col, (tm, 1), lambda i: (i, 0)),
                  (inv_freq, (1, LANES), lambda i: (0, 0)),
                  (lg_lanes, (1, RET_QK), lambda i: (0, 0))],
                 [((s, 2 * RET_QK), BF16, (tm, 2 * RET_QK), lambda i: (i, 0)),
                  ((s, RET_V), BF16, (tm, RET_V), lambda i: (i, 0)),
                  ((s, 3 * SB_W), BF16, (tm, 3 * SB_W), lambda i: (i, 0)),
                  ((s, LANES), F32, (tm, LANES), lambda i: (i, 0)),
                  ((s, LANES), F32, (tm, LANES), lambda i: (i, 0))])


def _head_mask(hh):
    lane = lax.broadcasted_iota(jnp.int32, (1, LANES), 1)
    return (lane >= RET_DQK) if hh else (lane < RET_DQK)


def _masked(v, m):
    return jnp.where(m, v, jnp.zeros_like(v))


SB_GROUP = 4
SB_TQ = 256
RET_GROUP = 4


def _stack_heads(v):
    return jnp.concatenate([_masked(v, _head_mask(0)), _masked(v, _head_mask(1))], axis=0)


def _side_by_side(v, t):
    return jnp.concatenate([v[:t], v[t:]], axis=1)


def _tile_pos(i, j, tq, tk):
    row = jnp.bitwise_and(lax.broadcasted_iota(jnp.int32, (2 * tq, tk), 0), tq - 1) + i * tq
    col = lax.broadcasted_iota(jnp.int32, (2 * tq, tk), 1) + j * tk
    return row, col


def _n_groups(i, tq, tk, grp):
    return ((i + 1) * (tq // tk) + grp - 1) // grp


def _n_full(i, tq, tk, grp):
    return (i * (tq // tk)) // grp


def _key_rows(j, tk):
    return pl.ds(pl.multiple_of(j * tk, tk), tk)


def _ret_weight(lg_rows, i, j, tq, tk):
    row, col = _tile_pos(i, j, tq, tk)
    same = jnp.right_shift(col, CHUNK_SHIFT) == jnp.right_shift(row, CHUNK_SHIFT)
    later = jnp.where(same, jnp.exp((2.0 * lg_rows) * (col - row).astype(F32)), 0.0)
    return jnp.where(col <= row, 1.0, later)


def _lg_rows(lg_ref, hp, tq):
    first = lax.broadcasted_iota(jnp.int32, (2 * tq, 1), 0) < tq
    return jnp.where(first, lg_ref[2 * hp], lg_ref[2 * hp + 1])


def _check_tiles(s, tq, tk, grp):
    assert tq % tk == 0 and tq & (tq - 1) == 0 and tk & (tk - 1) == 0
    assert s % tq == 0 and (s // tk) % grp == 0 and s // tk <= LANES


def _pair_mask():
    r = lax.broadcasted_iota(jnp.int32, (LANES, 2 * RET_DV), 0) >= RET_DQK
    c = lax.broadcasted_iota(jnp.int32, (LANES, 2 * RET_DV), 1) >= RET_DV
    return (r == c).astype(F32)


def _ret_block(lg_ref, hp, i, t, qb, kb):
    w = _ret_weight(_lg_rows(lg_ref, hp, t), i, i, t, t)
    return _dot(_stack_heads(qb), kb, _NT), w


RET_PAIRS = 2


def _lanes(ref, p, width):
    return ref[:, p * width:(p + 1) * width]


def _ret_fwd(qk_rot, v_bf, proj, gn_g, log_gamma, t, riders=None):
    s = qk_rot.shape[0]
    n_pair = HEADS // 2
    pw = 2 * RET_DV
    wq, wv = RET_PAIRS * LANES, RET_PAIRS * pw
    gate_off = (2 * RET_QK + RET_V) // wv
    assert s % t == 0 and t % CHUNK == 0 and t & (t - 1) == 0 and n_pair % RET_PAIRS == 0

    def body(lg_ref, q_ref, k_ref, v_ref, g_ref, w_ref, ret_ref, rg_ref, state_ref):
        hg, i = pl.program_id(0), pl.program_id(1)

        @pl.when(i == 0)
        def _():
            state_ref[...] = jnp.zeros_like(state_ref)

        pairs = range(RET_PAIRS)
        qbs = [_lanes(q_ref, p, LANES) for p in pairs]
        kbs = [_lanes(k_ref, p, LANES) for p in pairs]
        vbs = [_lanes(v_ref, p, pw) for p in pairs]
        zws = [_ret_block(lg_ref, hg * RET_PAIRS + p, i, t, qbs[p], kbs[p]) for p in pairs]
        ps = [(z * w).astype(BF16) for z, w in zws]
        outs = [jnp.concatenate([_dot(ps[p][:t], vbs[p][:, 0:RET_DV]), _dot(ps[p][t:], vbs[p][:, RET_DV:pw])], axis=1)
                + _dot(qbs[p], state_ref[p]) for p in pairs]
        for p in pairs:
            state_ref[p] += _pair_mask() * _dot(kbs[p], vbs[p], _TN)
        for p in pairs:
            for hh in range(2):
                cols = slice(p * pw + hh * RET_DV, p * pw + (hh + 1) * RET_DV)
                o = outs[p][:, hh * RET_DV:(hh + 1) * RET_DV]
                ret_ref[:, cols] = o
                mu = jnp.sum(o, axis=1, keepdims=True) * (1.0 / RET_DV)
                xc = o - mu
                var = jnp.sum(xc * xc, axis=1, keepdims=True) * (1.0 / RET_DV)
                nrm = xc * lax.rsqrt(var + EPS) * w_ref[:, cols]
                g = g_ref[:, cols].astype(F32)
                rg_ref[:, cols] = (g * _sigmoid(g) * nrm).astype(BF16)

    blk = lambda hg, i: (i, hg)
    return _call("ret_fwd", body, (n_pair // RET_PAIRS, s // t),
                 [(log_gamma, None, pltpu.SMEM),
                  (qk_rot, (t, wq), blk),
                  (qk_rot, (t, wq), lambda hg, i: (i, n_pair // RET_PAIRS + hg)),
                  (v_bf, (t, wv), blk),
                  (proj, (t, wv), lambda hg, i: (i, gate_off + hg)),
                  (gn_g, (1, wv), lambda hg, i: (0, hg))],
                 [((s, RET_V), F32, (t, wv), blk), ((s, RET_V), BF16, (t, wv), blk)],
                 scratch=[pltpu.VMEM((RET_PAIRS, LANES, pw), F32)], riders=riders)


def _tri(tk, strict_upper):
    r = lax.broadcasted_iota(jnp.int32, (tk, tk), 0)
    cc = lax.broadcasted_iota(jnp.int32, (tk, tk), 1)
    return ((r > cc) if strict_upper else (r < cc)).astype(BF16)


def _sb_valid(i, j, tq, tk):
    row, col = _tile_pos(i, j, tq, tk)
    return col < row


def _sb_fwd(qkv, tq, tk, riders=None):
    s = qkv.shape[0]
    n_pair = HEADS // 2
    _check_tiles(s, tq, tk, SB_GROUP)

    def body(q_ref, k_ref, v_ref, o_ref, a_ref):
        i = pl.program_id(1)
        upper = _tri(tk, True)
        qs = _stack_heads(q_ref[...])
        n_full, n_groups = _n_full(i, tq, tk, SB_GROUP), _n_groups(i, tq, tk, SB_GROUP)

        def make_step(near_diagonal, last):
            def step(n, carry):
                c, o = carry
                g = last - 1 - n
                js = [g * SB_GROUP + sub for sub in range(SB_GROUP)]
                zs = [_dot(qs, k_ref[_key_rows(j, tk), :], _NT) for j in js]
                log1ps = [jnp.log2(1.0 + jnp.exp2(-jnp.abs(z))) for z in zs]
                log_1ms = [-jnp.maximum(z, 0.0) - t for z, t in zip(zs, log1ps)]
                log_bs = [jnp.minimum(z, 0.0) - t for z, t in zip(zs, log1ps)]
                if near_diagonal:
                    valids = [_sb_valid(i, j, tq, tk) for j in js]
                    log_1ms = [jnp.where(v, l, 0.0) for v, l in zip(valids, log_1ms)]
                sticks = [_dot(l, upper) for l in log_1ms]
                sums = [jnp.sum(l, axis=1, keepdims=True) for l in log_1ms]
                cs = [None] * SB_GROUP
                for sub in reversed(range(SB_GROUP)):
                    cs[sub] = c
                    c = c + sums[sub]
                for sub, j in enumerate(js):
                    a = jnp.exp2(log_bs[sub] + sticks[sub] + cs[sub])
                    if near_diagonal:
                        a = jnp.where(valids[sub], a, 0.0)
                    a = a.astype(BF16)
                    a_ref[j] = a
                    o = o + _dot(_side_by_side(a, tq), _stack_heads(v_ref[_key_rows(j, tk), :]))
                return c, o
            return step

        carry = (jnp.zeros((2 * tq, 1), F32), jnp.zeros((tq, LANES), F32))
        carry = lax.fori_loop(0, n_groups - n_full, make_step(True, n_groups), carry)
        _, acc = lax.fori_loop(0, n_full, make_step(False, n_full), carry)
        o_ref[...] = acc

    n_kb = s // tk
    return _call("sb_fwd", body, (n_pair, s // tq),
                 [(qkv, (tq, LANES), lambda hp, i: (i, hp)),
                  (qkv, (s, LANES), lambda hp, i: (0, n_pair + hp)),
                  (qkv, (s, LANES), lambda hp, i: (0, 2 * n_pair + hp))],
                 [((s, SB_W), F32, (tq, LANES), lambda hp, i: (i, hp)),
                  ((n_pair, s // tq, n_kb, 2 * tq, tk), BF16, (None, None, n_kb, 2 * tq, tk),
                   lambda hp, i: (hp, i, 0, 0, 0))], riders=riders)


def _merge(retg, sb, w_ret, w_sb_t, proj, tm, tn, riders=None):
    s, d = retg.shape[0], w_ret.shape[1]
    ar_off = (2 * RET_QK + 2 * RET_V + 3 * SB_W) // tn
    as_off = ar_off + d // tn

    def body(rg_ref, sb_ref, wr_ref, ws_ref, ar_ref, as_ref, mix_ref, r_ref, s_ref):
        rr = _dot(rg_ref[...], wr_ref[...])
        ss = _dot(sb_ref[...], ws_ref[...], _NT)
        mix_ref[...] = (_sigmoid(ar_ref[...].astype(F32)) * rr + _sigmoid(as_ref[...].astype(F32)) * ss).astype(BF16)
        r_ref[...] = rr.astype(BF16)
        s_ref[...] = ss.astype(BF16)

    tile = (tm, tn)
    return _call("merge", body, (d // tn, s // tm),
                 [(retg, (tm, RET_V), lambda j, i: (i, 0)), (sb, (tm, SB_W), lambda j, i: (i, 0)),
                  (w_ret, (RET_V, tn), lambda j, i: (0, j)), (w_sb_t, (tn, SB_W), lambda j, i: (j, 0)),
                  (proj, tile, lambda j, i: (i, ar_off + j)), (proj, tile, lambda j, i: (i, as_off + j))],
                 [((s, d), BF16, tile, lambda j, i: (i, j))] * 3, riders=riders)


def _out_proj(mixed, w_out, x, mod, gp1, g2, tm):
    s, d = x.shape

    def body(a_ref, w_ref, x_ref, mod_ref, gp_ref, g2_ref, y_ref, hres_ref, h2_ref):
        for rows in _pieces(tm):
            y = _dot(a_ref[rows, :], w_ref[...])
            y_ref[rows, :] = y
            ny, _ = _rms(y, d)
            hres = x_ref[rows, :] + mod_ref[:, 2 * d:3 * d] * (ny * gp_ref[...])
            hres_ref[rows, :] = hres
            n2, _ = _rms(hres, d)
            h2_ref[rows, :] = (n2 * g2_ref[...] * (1.0 + mod_ref[:, 4 * d:5 * d]) + mod_ref[:, 3 * d:4 * d]).astype(BF16)

    row = lambda i: (i, 0)
    fix = lambda i: (0, 0)
    return _call("out_proj", body, (s // tm,),
                 [(mixed, (tm, d), row), (w_out, (d, d), fix), (x, (tm, d), row),
                  (mod, (1, 6 * d), fix), (gp1, (1, d), fix), (g2, (1, d), fix)],
                 [((s, d), F32, (tm, d), row), ((s, d), F32, (tm, d), row), ((s, d), BF16, (tm, d), row)])


def _ff1(h2, w_ff1_t, tm, tn):
    s, f = h2.shape[0], w_ff1_t.shape[0]
    tm = min(tm, s)

    def body(a_ref, w_ref, u_ref, act_ref):
        u = _dot(a_ref[...], w_ref[...], _NT)
        r = jnp.maximum(u, 0.0)
        u_ref[...] = u.astype(BF16)
        act_ref[...] = (r * r).astype(BF16)

    d = h2.shape[1]
    return _call("ff1", body, (f // tn, s // tm),
                 [(h2, (tm, d), lambda j, i: (i, 0)), (w_ff1_t, (tn, d), lambda j, i: (j, 0))],
                 [((s, f), BF16, (tm, tn), lambda j, i: (i, j))] * 2)


def _ff2_loss(act, w_ff2, hres, target, mod, gp2, tm):
    s, d = hres.shape
    f = act.shape[1]

    def body(a_ref, w_ref, h_ref, t_ref, mod_ref, gp_ref, dout_ref, df_ref, loss_ref, dgt_ref, dgp_ref):
        _zero_at_start([loss_ref, dgt_ref, dgp_ref])
        gt, gp = mod_ref[:, 5 * d:6 * d], gp_ref[...]
        for rows in _pieces(tm):
            ff = _dot(a_ref[rows, :], w_ref[...])
            nf, rf = _rms(ff, d)
            out = h_ref[rows, :] + gt * (nf * gp)
            err = out - t_ref[rows, :]
            sq = jnp.sum(err * err, axis=1, keepdims=True)
            loss_ref[...] += jnp.sum(sq, axis=0, keepdims=True)
            dout = err * (1.0 / d)
            dout_ref[rows, :] = dout
            dgt_ref[...] += _colsum(dout * (nf * gp))
            dgp_ref[...] += _colsum(dout * gt * nf)
            df_ref[rows, :] = _rms_bwd(dout * gt * gp, nf, rf, d).astype(BF16)

    row = lambda i: (i, 0)
    fix = lambda i: (0, 0)
    return _call("ff2_loss", body, (s // tm,),
                 [(act, (tm, f), row), (w_ff2, (f, d), fix), (hres, (tm, d), row), (target, (tm, d), row),
                  (mod, (1, 6 * d), fix), (gp2, (1, d), fix)],
                 [((s, d), F32, (tm, d), row), ((s, d), BF16, (tm, d), row), ((1, 1), F32, (1, 1), fix),
                  ((1, d), F32, (1, d), fix), ((1, d), F32, (1, d), fix)])


def _ff2_bwd(df, w_ff2, u, tm, tn):
    s, d = df.shape
    f = w_ff2.shape[0]
    tm = min(tm, s)

    def body(a_ref, w_ref, u_ref, du_ref):
        da = _dot(a_ref[...], w_ref[...], _NT)
        du_ref[...] = (da * (2.0 * jnp.maximum(u_ref[...].astype(F32), 0.0))).astype(BF16)

    return _call("ff2_bwd", body, (f // tn, s // tm),
                 [(df, (tm, d), lambda j, i: (i, 0)), (w_ff2, (tn, d), lambda j, i: (j, 0)),
                  (u, (tm, tn), lambda j, i: (i, j))],
                 [((s, f), BF16, (tm, tn), lambda j, i: (i, j))])[0]


def _ff1_bwd(du, w_ff1_t, hres, dout, y, mod, g2, gp1, tm, riders=None):
    s, d = hres.shape
    f = du.shape[1]

    def body(a_ref, w_ref, h_ref, do_ref, y_ref, mod_ref, g2_ref, gp_ref,
             dh_ref, dy_ref, dsh_ref, dsc_ref, dg2_ref, dgt_ref, dgp_ref):
        _zero_at_start([dsh_ref, dsc_ref, dg2_ref, dgt_ref, dgp_ref])
        g2, sc2 = g2_ref[...], mod_ref[:, 4 * d:5 * d]
        gt, gp = mod_ref[:, 2 * d:3 * d], gp_ref[...]
        for rows in _pieces(tm):
            dh2 = _dot(a_ref[rows, :], w_ref[...])
            n2, r2 = _rms(h_ref[rows, :], d)
            dsh_ref[...] += _colsum(dh2)
            dsc_ref[...] += _colsum(dh2 * n2 * g2)
            dg2_ref[...] += _colsum(dh2 * n2 * (1.0 + sc2))
            dhres = do_ref[rows, :] + _rms_bwd(dh2 * g2 * (1.0 + sc2), n2, r2, d)
            dh_ref[rows, :] = dhres
            ny, ry = _rms(y_ref[rows, :], d)
            dgt_ref[...] += _colsum(dhres * (ny * gp))
            dgp_ref[...] += _colsum(dhres * gt * ny)
            dy_ref[rows, :] = _rms_bwd(dhres * gt * gp, ny, ry, d).astype(BF16)

    row = lambda i: (i, 0)
    fix = lambda i: (0, 0)
    vec = ((1, d), F32, (1, d), fix)
    return _call("ff1_bwd", body, (s // tm,),
                 [(du, (tm, f), row), (w_ff1_t, (f, d), fix), (hres, (tm, d), row), (dout, (tm, d), row),
                  (y, (tm, d), row), (mod, (1, 6 * d), fix), (g2, (1, d), fix), (gp1, (1, d), fix)],
                 [((s, d), F32, (tm, d), row), ((s, d), BF16, (tm, d), row), vec, vec, vec, vec, vec], riders=riders)


def _out_bwd(dy, w_out, proj, r_bf, s_bf, tm, tn):
    s, d = dy.shape
    ar_off = (2 * RET_QK + 2 * RET_V + 3 * SB_W) // tn
    as_off = ar_off + d // tn

    def body(a_ref, w_ref, ar_ref, as_ref, r_ref, s_ref, dr_ref, ds_ref, dar_ref, das_ref):
        dm = _dot(a_ref[...], w_ref[...], _NT)
        sr, ss = _sigmoid(ar_ref[...].astype(F32)), _sigmoid(as_ref[...].astype(F32))
        dr_ref[...] = (dm * sr).astype(BF16)
        ds_ref[...] = (dm * ss).astype(BF16)
        dar_ref[...] = (dm * r_ref[...].astype(F32) * sr * (1.0 - sr)).astype(BF16)
        das_ref[...] = (dm * s_ref[...].astype(F32) * ss * (1.0 - ss)).astype(BF16)

    tile = (tm, tn)
    here = lambda j, i: (i, j)
    return _call("out_bwd", body, (d // tn, s // tm),
                 [(dy, (tm, d), lambda j, i: (i, 0)), (w_out, (tn, d), lambda j, i: (j, 0)),
                  (proj, tile, lambda j, i: (i, ar_off + j)), (proj, tile, lambda j, i: (i, as_off + j)),
                  (r_bf, tile, here), (s_bf, tile, here)],
                 [((s, d), BF16, tile, here)] * 4)


def _gn_bwd(dretg, ret, proj, gn_g, tm, riders=None):
    s = ret.shape[0]
    gate_off = (2 * RET_QK + RET_V) // RET_V

    def body(d_ref, r_ref, g_ref, w_ref, dg_ref, dret_ref, dw_ref):
        first = pl.program_id(0) == 0
        for h in range(HEADS):
            cols = slice(h * RET_DV, (h + 1) * RET_DV)
            o, g, w, dr = r_ref[:, cols], g_ref[:, cols].astype(F32), w_ref[:, cols], d_ref[:, cols].astype(F32)
            mu = jnp.sum(o, axis=1, keepdims=True) * (1.0 / RET_DV)
            xc = o - mu
            rstd = lax.rsqrt(jnp.sum(xc * xc, axis=1, keepdims=True) * (1.0 / RET_DV) + EPS)
            n = xc * rstd
            sg = _sigmoid(g)
            silu = g * sg
            dg_ref[:, cols] = (dr * n * w * (sg * (1.0 + g * (1.0 - sg)))).astype(BF16)
            _accum(dw_ref.at[:, cols], _colsum(dr * silu * n), first)
            dn = dr * silu * w
            m1 = jnp.sum(dn, axis=1, keepdims=True) * (1.0 / RET_DV)
            m2 = jnp.sum(dn * n, axis=1, keepdims=True) * (1.0 / RET_DV)
            dret_ref[:, cols] = (rstd * (dn - m1 - n * m2)).astype(BF16)

    row = lambda i: (i, 0)
    fix = lambda i: (0, 0)
    return _call("gn_bwd", body, (s // tm,),
                 [(dretg, (tm, RET_V), row), (ret, (tm, RET_V), row),
                  (proj, (tm, RET_V), lambda i: (i, gate_off)), (gn_g, (1, RET_V), fix)],
                 [((s, RET_V), BF16, (tm, RET_V), row), ((s, RET_V), BF16, (tm, RET_V), row),
                  ((1, RET_V), F32, (1, RET_V), fix)], riders=riders)


def _ret_bwd(qk_rot, v_bf, dret, log_gamma, t, riders=None):
    s = qk_rot.shape[0]
    n_pair = HEADS // 2
    pw = 2 * RET_DV
    wq, wv = RET_PAIRS * LANES, RET_PAIRS * pw
    n_blk = s // t
    pairs = range(RET_PAIRS)

    def load(q_ref, k_ref, v_ref, do_ref):
        return ([_lanes(q_ref, p, LANES) for p in pairs], [_lanes(k_ref, p, LANES) for p in pairs],
                [_lanes(v_ref, p, pw) for p in pairs], [_lanes(do_ref, p, pw) for p in pairs])

    def d_scores(lg_ref, hp, i, qb, kb, vb, dob):
        z, w = _ret_block(lg_ref, hp, i, t, qb, kb)
        dp = jnp.concatenate([_dot(dob[:, 0:RET_DV], vb[:, 0:RET_DV], _NT),
                              _dot(dob[:, RET_DV:pw], vb[:, RET_DV:pw], _NT)], axis=0)
        return (z * w).astype(BF16), (dp * w).astype(BF16)

    def up_body(lg_ref, q_ref, k_ref, v_ref, do_ref, dq_ref, state_ref):
        hg, i = pl.program_id(0), pl.program_id(1)

        @pl.when(i == 0)
        def _():
            state_ref[...] = jnp.zeros_like(state_ref)

        qbs, kbs, vbs, dobs = load(q_ref, k_ref, v_ref, do_ref)
        dss = [d_scores(lg_ref, hg * RET_PAIRS + p, i, qbs[p], kbs[p], vbs[p], dobs[p])[1] for p in pairs]
        for p in pairs:
            dq_ref[:, p * LANES:(p + 1) * LANES] = (_dot(_side_by_side(dss[p], t), _stack_heads(kbs[p]))
                                                    + _dot(dobs[p], state_ref[p], _NT))
        for p in pairs:
            state_ref[p] += _pair_mask() * _dot(kbs[p], vbs[p], _TN)

    def down_body(lg_ref, q_ref, k_ref, v_ref, do_ref, dk_ref, dv_ref, state_ref):
        hg, i = pl.program_id(0), n_blk - 1 - pl.program_id(1)

        @pl.when(pl.program_id(1) == 0)
        def _():
            state_ref[...] = jnp.zeros_like(state_ref)

        qbs, kbs, vbs, dobs = load(q_ref, k_ref, v_ref, do_ref)
        both = [d_scores(lg_ref, hg * RET_PAIRS + p, i, qbs[p], kbs[p], vbs[p], dobs[p]) for p in pairs]
        for p in pairs:
            pp, ds = both[p]
            later = state_ref[p]
            dv_ref[:, p * pw:(p + 1) * pw] = jnp.concatenate(
                [_dot(pp[:t], dobs[p][:, 0:RET_DV], _TN), _dot(pp[t:], dobs[p][:, RET_DV:pw], _TN)],
                axis=1) + _dot(kbs[p], later)
            dk_ref[:, p * LANES:(p + 1) * LANES] = _dot(ds, _stack_heads(qbs[p]), _TN) + _dot(vbs[p], later, _NT)
        for p in pairs:
            state_ref[p] += _pair_mask() * _dot(qbs[p], dobs[p], _TN)

    n_grp = n_pair // RET_PAIRS

    def ins(order):
        return [(log_gamma, None, pltpu.SMEM),
                (qk_rot, (t, wq), lambda hg, i: (order(i), hg)),
                (qk_rot, (t, wq), lambda hg, i: (order(i), n_grp + hg)),
                (v_bf, (t, wv), lambda hg, i: (order(i), hg)),
                (dret, (t, wv), lambda hg, i: (order(i), hg))]

    up = lambda i: i
    down = lambda i: n_blk - 1 - i
    scratch = [pltpu.VMEM((RET_PAIRS, LANES, pw), F32)]
    dq = _call("ret_bwd_q", up_body, (n_grp, n_blk), ins(up),
               [((s, RET_QK), F32, (t, wq), lambda hg, i: (i, hg))], scratch=scratch)[0]
    dk, dv, *rest = _call("ret_bwd_kv", down_body, (n_grp, n_blk), ins(down),
                          [((s, RET_QK), F32, (t, wq), lambda hg, i: (down(i), hg)),
                           ((s, RET_V), F32, (t, wv), lambda hg, i: (down(i), hg))],
                          scratch=scratch, riders=riders)
    return [dq, dk, dv] + rest


def _sb_bwd(qkv, weights, do, tq, tk, riders=None):
    s = qkv.shape[0]
    n_pair = HEADS // 2
    _check_tiles(s, tq, tk, SB_GROUP)

    def body(q_ref, k_ref, v_ref, a_ref, do_ref, dq_ref, dk_ref, dv_ref):
        i = pl.program_id(1)

        @pl.when(i == 0)
        def _():
            dk_ref[...] = jnp.zeros_like(dk_ref)
            dv_ref[...] = jnp.zeros_like(dv_ref)

        lower = _tri(tk, False)
        qs = _stack_heads(q_ref[...])
        dos = _stack_heads(do_ref[...].astype(BF16))

        def make_step(near_diagonal):
            def step(g, carry):
                c_e, dq = carry
                js = [g * SB_GROUP + sub for sub in range(SB_GROUP)]
                rows = [_key_rows(j, tk) for j in js]
                zs = [_dot(qs, k_ref[rw, :], _NT) for rw in rows]
                das = [_dot(dos, v_ref[rw, :], _NT) for rw in rows]
                avals = [a_ref[j] for j in js]
                for a, rw in zip(avals, rows):
                    dv_ref[rw, :] += _dot(a, dos, _TN)
                es = [a.astype(F32) * da for a, da in zip(avals, das)]
                prefixes = [_dot(e, lower) for e in es]
                betas = [1.0 / (1.0 + jnp.exp2(-z)) for z in zs]
                for sub in range(SB_GROUP):
                    dz = es[sub] - (es[sub] + prefixes[sub] + c_e) * betas[sub]
                    if near_diagonal:
                        dz = jnp.where(_sb_valid(i, js[sub], tq, tk), dz, 0.0)
                    dz = dz.astype(BF16)
                    dk_ref[rows[sub], :] += _dot(dz, qs, _TN)
                    dq = dq + _dot(_side_by_side(dz, tq), _stack_heads(k_ref[rows[sub], :]))
                    c_e = c_e + jnp.sum(es[sub], axis=1, keepdims=True)
                return c_e, dq
            return step

        n_full = _n_full(i, tq, tk, SB_GROUP)
        carry = (jnp.zeros((2 * tq, 1), F32), jnp.zeros((tq, LANES), F32))
        carry = lax.fori_loop(0, n_full, make_step(False), carry)
        _, dq = lax.fori_loop(n_full, _n_groups(i, tq, tk, SB_GROUP), make_step(True), carry)
        dq_ref[...] = dq

    blk = lambda hp, i: (i, hp)
    n_kb = s // tk
    return _call("sb_bwd", body, (n_pair, s // tq),
                 [(qkv, (tq, LANES), blk),
                  (qkv, (s, LANES), lambda hp, i: (0, n_pair + hp)),
                  (qkv, (s, LANES), lambda hp, i: (0, 2 * n_pair + hp)),
                  (weights, (None, None, n_kb, 2 * tq, tk), lambda hp, i: (hp, i, 0, 0, 0)),
                  (do, (tq, LANES), blk)],
                 [((s, SB_W), F32, (tq, LANES), blk),
                  ((s, SB_W), F32, (s, LANES), lambda hp, i: (0, hp)),
                  ((s, SB_W), F32, (s, LANES), lambda hp, i: (0, hp))], riders=riders)


def _assemble_dproj(dq_r, dk_r, dv_r, dg_r, dq_s, dk_s, dv_s, da_r, da_s, cos, sin, idx_col, lg_lanes, tm):
    s, d = da_r.shape
    width = 2 * RET_QK + 2 * RET_V + 3 * SB_W + 2 * d

    def body(dq_ref, dk_ref, dv_ref, dg_ref, dqs_ref, dks_ref, dvs_ref, dar_ref, das_ref, cos_ref, sin_ref,
             idx_ref, lg_ref, o_ref):
        lane = lax.broadcasted_iota(jnp.int32, (1, LANES), 1)
        first = jnp.bitwise_and(lane, RET_DQK - 1) < (RET_DQK // 2)
        cos, sin = cos_ref[...], sin_ref[...]
        idx = idx_ref[...]
        for src, base, sign, scale in ((dq_ref, 0, 1.0, 1.0), (dk_ref, RET_QK, -1.0, RET_DQK ** -0.5)):
            for g in range(RET_QK // LANES):
                v = src[:, g * LANES:(g + 1) * LANES] * (_decay_scale(lg_ref, idx, g, sign) * scale)
                sw = jnp.where(first, pltpu.roll(v, LANES - RET_DQK // 2, 1), pltpu.roll(v, RET_DQK // 2, 1))
                o_ref[:, base + g * LANES:base + (g + 1) * LANES] = (v * cos - sw * sin).astype(BF16)
        off = 2 * RET_QK
        o_ref[:, off:off + RET_V] = dv_ref[...].astype(BF16)
        off += RET_V
        o_ref[:, off:off + RET_V] = dg_ref[...]
        off += RET_V
        o_ref[:, off:off + SB_W] = (dqs_ref[...] * (SB_DH ** -0.5)).astype(BF16)
        off += SB_W
        o_ref[:, off:off + SB_W] = (dks_ref[...] * LN2).astype(BF16)
        off += SB_W
        o_ref[:, off:off + SB_W] = dvs_ref[...].astype(BF16)
        off += SB_W
        o_ref[:, off:off + d] = dar_ref[...]
        off += d
        o_ref[:, off:off + d] = das_ref[...]

    row = lambda i: (i, 0)
    ins = [(a, (tm, a.shape[1]), row) for a in (dq_r, dk_r, dv_r, dg_r, dq_s, dk_s, dv_s, da_r, da_s, cos, sin, idx_col)]
    ins.append((lg_lanes, (1, RET_QK), lambda i: (0, 0)))
    return _call("assemble_dproj", body, (s // tm,), ins, [((s, width), BF16, (tm, width), row)])[0]


def _in_bwd(dproj, w_in_t, x, dhres, mod, g1, tm, riders=None):
    s, d = x.shape
    width = dproj.shape[1]

    def body(a_ref, w_ref, x_ref, dh_ref, mod_ref, g_ref, dx_ref, dsh_ref, dsc_ref, dg_ref):
        _zero_at_start([dsh_ref, dsc_ref, dg_ref])
        g1, sc1 = g_ref[...], mod_ref[:, d:2 * d]
        for rows in _pieces(tm):
            dh = _dot(a_ref[rows, :], w_ref[...])
            n1, r1 = _rms(x_ref[rows, :], d)
            dsh_ref[...] += _colsum(dh)
            dsc_ref[...] += _colsum(dh * n1 * g1)
            dg_ref[...] += _colsum(dh * n1 * (1.0 + sc1))
            dx_ref[rows, :] = dh_ref[rows, :] + _rms_bwd(dh * g1 * (1.0 + sc1), n1, r1, d)

    row = lambda i: (i, 0)
    fix = lambda i: (0, 0)
    vec = ((1, d), F32, (1, d), fix)
    return _call("in_bwd", body, (s // tm,),
                 [(dproj, (tm, width), row), (w_in_t, (width, d), fix), (x, (tm, d), row), (dhres, (tm, d), row),
                  (mod, (1, 6 * d), fix), (g1, (1, d), fix)],
                 [((s, d), F32, (tm, d), row), vec, vec, vec], riders=riders)


def _adamw(w, g, m, v):
    m = ADAM_B1 * m + (1.0 - ADAM_B1) * g
    v = ADAM_B2 * v + (1.0 - ADAM_B2) * (g * g)
    m_hat = m / (1.0 - ADAM_B1 ** ADAM_STEP)
    v_hat = v / (1.0 - ADAM_B2 ** ADAM_STEP)
    delta = -ADAM_LR * (m_hat / (jnp.sqrt(v_hat) + ADAM_EPS) + ADAM_WD * w)
    return delta, m, v


def _adam_reduce(name, parts, w, m, v, tr):
    rws, cls = w.shape
    tr = min(tr, rws)
    n_parts = parts.shape[0]

    def body(p_ref, w_ref, m_ref, v_ref, g_out, d_out, m_out, v_out):
        g = p_ref[0].astype(F32)
        for k in range(1, n_parts):
            g = g + p_ref[k].astype(F32)
        delta, mn, vn = _adamw(w_ref[...], g, m_ref[...], v_ref[...])
        g_out[...] = g
        d_out[...] = delta
        m_out[...] = mn
        v_out[...] = vn

    row = lambda i: (i, 0)
    blk = (tr, cls)
    return _call(name, body, (rws // tr,),
                 [(parts, (n_parts, tr, cls), lambda i: (0, i, 0)), (w, blk, row), (m, blk, row), (v, blk, row)],
                 [((rws, cls), F32, blk, row)] * 4)


def _ada_bwd_adam(cs_t, dmod_cols, w, m, v):
    d, nc = w.shape

    def body(c_ref, dm_ref, w_ref, m_ref, v_ref, g_out, d_out, m_out, v_out):
        g = c_ref[0] * dm_ref[0:1, :]
        for r in range(1, N_DEV):
            g = g + c_ref[r] * dm_ref[r:r + 1, :]
        delta, mn, vn = _adamw(w_ref[...], g, m_ref[...], v_ref[...])
        g_out[...] = g
        d_out[...] = delta
        m_out[...] = mn
        v_out[...] = vn

    fix = lambda i: (0, 0)
    blk = (d, nc)
    return _call("ada_bwd_adam", body, (1,),
                 [(cs_t, (N_DEV, d, 1), lambda i: (0, 0, 0)), (dmod_cols, (N_DEV, nc), fix), (w, blk, fix), (m, blk, fix), (v, blk, fix)],
                 [((d, nc), F32, blk, fix)] * 4)


def _small_adam(parts, w, m, v):
    n = w.shape[1]

    def body(p_ref, w_ref, m_ref, v_ref, g_out, d_out, m_out, v_out):
        g = p_ref[0:1, :]
        for k in range(1, N_DEV):
            g = g + p_ref[k:k + 1, :]
        delta, mn, vn = _adamw(w_ref[...], g, m_ref[...], v_ref[...])
        g_out[...] = g
        d_out[...] = delta
        m_out[...] = mn
        v_out[...] = vn

    fix = lambda i: (0, 0)
    return _call("small_adam", body, (1,),
                 [(parts, (N_DEV, n), fix), (w, (1, n), fix), (m, (1, n), fix), (v, (1, n), fix)],
                 [((1, n), F32, (1, n), fix)] * 4)


def kernel(x, c, positions, ada_w, ada_b, pre_mix_g, post_mix_g, pre_ffn_g, post_ffn_g, w_in, ret_gn_g, w_ret_branch, w_sb_branch, w_out, w_ff1, w_ff2, loss_target, m_ada_w, m_ada_b, m_pre_mix_g, m_post_mix_g, m_pre_ffn_g, m_post_ffn_g, m_w_in, m_ret_gn_g, m_w_ret_branch, m_w_sb_branch, m_w_out, m_w_ff1, m_w_ff2, v_ada_w, v_ada_b, v_pre_mix_g, v_post_mix_g, v_pre_ffn_g, v_post_ffn_g, v_w_in, v_ret_gn_g, v_w_ret_branch, v_w_sb_branch, v_w_out, v_w_ff1, v_w_ff2):
    _, s, d = x.shape
    d_ff = w_ff1.shape[2] * N_DEV
    d_in = w_in.shape[2] * N_DEV
    me = 4 * lax.axis_index("x") + 2 * lax.axis_index("y") + lax.axis_index("c")
    x2, tgt = x[0], loss_target[0]

    core = lax.axis_index("c").astype(jnp.int32).reshape(1)
    bf = lambda w: w[0].astype(BF16)

    w_in_t, m_in_t, v_in_t = (jnp.swapaxes(a[0], 0, 1) for a in (w_in, m_w_in, v_w_in))

    c_all, g_in = _exchange("gather_in", [c, w_in_t.astype(BF16)], ["gather", "gather_chip"])
    c_all = c_all.reshape(N_DEV, d)

    n_ada = ada_w.shape[2]
    cs_all = _silu_rows(c_all)
    ada_b_cols = lax.dynamic_slice(ada_b, (0, me * n_ada), (1, n_ada))
    mod_cols = _ada_fwd(cs_all, ada_w[0], ada_b_cols)
    mod_all = _exchange("gather_mod", [mod_cols], ["gather"])[0]
    mod = lax.dynamic_index_in_dim(mod_all, me, axis=1, keepdims=False).reshape(1, 6 * d)

    tm = min(256, s)
    h, g_in = _pre_norm(x2, pre_mix_g, mod, tm, riders=([g_in], ["forward"]))
    wt_in = g_in.reshape(d_in, d)
    bf_t = lambda w: jnp.swapaxes(w[0], 0, 1).astype(BF16)
    proj = _matmul("in_proj", h, wt_in, "nt", s, 512, BF16)
    pos_col = positions.reshape(s, 1).astype(F32)
    freqs = ROPE_BASE ** (-jnp.arange(0, RET_DQK, 2, dtype=F32) / RET_DQK)
    inv_freq = jnp.tile(freqs, LANES // (RET_DQK // 2)).reshape(1, LANES)
    log_gamma_np = np.log1p(-(2.0 ** (-5.0 - np.arange(HEADS))))
    log_gamma = jnp.asarray(log_gamma_np, F32)
    lg_lanes = jnp.asarray(np.repeat(log_gamma_np, RET_DQK).reshape(1, RET_QK), F32)
    idx_col = (jnp.arange(s, dtype=F32) - (s // 2)).reshape(s, 1)
    qk_rot, v_bf, qkv_sb, cos_t, sin_t = _prep(proj, pos_col, idx_col, inv_freq, lg_lanes, tm)
    tq, tk = min(256, s), min(128, s)
    tq_sb = min(SB_TQ, s)
    later = [bf(w_ret_branch), bf_t(w_sb_branch), bf(w_out), bf(w_ff2), bf_t(w_ff1)]
    sb, sb_weights, *later = _sb_fwd(qkv_sb, tq_sb, tk, riders=(later, ["gather_chip"] * 5))
    ret, retg, g_ret, g_sb, g_out, g_ff2, g_ff1 = _ret_fwd(qk_rot, v_bf, proj, ret_gn_g, log_gamma, tq,
                                                           riders=(later, ["forward"] * 5))
    wf_ret = g_ret.reshape(RET_V, d)
    wt_sb = g_sb.reshape(d, SB_W)
    wf_out = g_out.reshape(d, d)
    wt_ff1 = g_ff1.reshape(d_ff, d)
    wf_ff2 = g_ff2.reshape(d_ff, d)
    mixed, r_bf, s_bf = _merge(retg, sb, wf_ret, wt_sb, proj, tm, min(512, d))
    y, hres, h2 = _out_proj(mixed, wf_out, x2, mod, post_mix_g, pre_ffn_g, tm)
    u, act = _ff1(h2, wt_ff1, s, 512)
    dout, df, loss_sum, d_gt2, d_gp2 = _ff2_loss(act, wf_ff2, hres, tgt, mod, post_ffn_g, tm)

    du = _ff2_bwd(df, wf_ff2, u, s, 512)
    gw_ff2 = _matmul("grad_w_ff2", act, df, "tn", 512, d, BF16).reshape(N_DEV, d_ff // N_DEV, d)
    gw_ff1 = _matmul("grad_w_ff1", h2, du, "tn", d, d_ff // N_DEV, BF16, blocked_out=True)
    dhres, dy, d_sh2, d_sc2, d_g2, d_gt1, d_gp1, t_ff1, t_ff2 = _ff1_bwd(
        du, wt_ff1, hres, dout, y, mod, pre_ffn_g, post_mix_g, tm, riders=([gw_ff1, gw_ff2], ["pair"] * 2))
    s_ff1 = _pair_sum("pair_sum_ff1", gw_ff1, t_ff1, core, 256)
    s_ff2 = _pair_sum("pair_sum_ff2", gw_ff2, t_ff2, core, 256)
    d_r, d_s, da_r, da_s = _out_bwd(dy, wf_out, proj, r_bf, s_bf, tm, min(512, d))
    gw_out = _matmul("grad_w_out", mixed, dy, "tn", 512, d, BF16).reshape(N_DEV, d // N_DEV, d)
    dretg = _matmul("ret_branch_bwd", d_r, wf_ret, "nt", s, 512, BF16)
    dsb = _matmul("sb_branch_bwd", d_s, wt_sb, "nn", s, 512, F32)
    gw_ret = _matmul("grad_w_ret", retg, d_r, "tn", 512, d, BF16).reshape(N_DEV, RET_V // N_DEV, d)
    gw_sb = _matmul("grad_w_sb", sb, d_s, "tn", 512, d // N_DEV, BF16, blocked_out=True)
    dq_s, dk_s, dv_s, p_ff1, p_ff2, p_ret = _sb_bwd(qkv_sb, sb_weights, dsb, tq_sb, tk,
                                                    riders=([s_ff1, s_ff2, gw_ret], ["chip_scatter"] * 2 + ["scatter"]))
    dg_r, dret, d_gn = _gn_bwd(dretg, ret, proj, ret_gn_g, tm)
    dq_r, dk_r, dv_r, p_out, p_sb = _ret_bwd(qk_rot, v_bf, dret, log_gamma, tq,
                                             riders=([gw_out, gw_sb], ["scatter"] * 2))
    dproj = _assemble_dproj(dq_r, dk_r, dv_r, dg_r, dq_s, dk_s, dv_s, da_r, da_s, cos_t, sin_t, idx_col, lg_lanes, tm)
    gw_in = _matmul("grad_w_in", dproj, h, "tn", 512, d, BF16).reshape(N_DEV, d_in // N_DEV, d)
    t_in = _exchange("pair_in", [gw_in], ["pair"])[0]
    tr_in = d_in // N_DEV // 4
    s_in = _pair_sum("pair_sum_in", gw_in, t_in, core, tr_in)
    grad_x, d_sh1, d_sc1, d_g1, p_in = _in_bwd(dproj, wt_in, x2, dhres, mod, pre_mix_g, tm,
                                               riders=([s_in], ["chip_scatter"]))
    loss_lanes = jnp.pad(loss_sum, ((0, 0), (0, LANES - 1)))
    small = jnp.concatenate([d_sh1, d_sc1, d_gt1, d_sh2, d_sc2, d_gt2, d_g1, d_gp1, d_g2, d_gp2, d_gn, loss_lanes], axis=1)
    small_all = _exchange("gather_small", [small], ["gather"])[0].reshape(N_DEV, small.shape[1])
    parts = [p_in, p_ret, p_sb, p_out, p_ff1, p_ff2]

    res = {}
    names = ["w_ret_branch", "w_sb_branch", "w_out", "w_ff1", "w_ff2"]
    ws = [w_ret_branch, w_sb_branch, w_out, w_ff1, w_ff2]
    ms = [m_w_ret_branch, m_w_sb_branch, m_w_out, m_w_ff1, m_w_ff2]
    vs = [v_w_ret_branch, v_w_sb_branch, v_w_out, v_w_ff1, v_w_ff2]
    for nm, p, w, m, v in zip(names, parts[1:], ws, ms, vs):
        res[nm] = [o[None] for o in _adam_reduce("adam_" + nm, p, w[0], m[0], v[0], 256)]
    res["w_in"] = [jnp.swapaxes(o, 0, 1)[None]
                   for o in _adam_reduce("adam_w_in", parts[0], w_in_t, m_in_t, v_in_t, tr_in)]
    dmod_cols = lax.dynamic_slice(small_all, (0, me * n_ada), (N_DEV, n_ada))
    res["ada_w"] = [o[None] for o in _ada_bwd_adam(cs_all.reshape(N_DEV, d, 1), dmod_cols, ada_w[0], m_ada_w[0], v_ada_w[0])]
    vec_names = ["ada_b", "pre_mix_g", "post_mix_g", "pre_ffn_g", "post_ffn_g", "ret_gn_g"]
    cat = lambda xs: jnp.concatenate(xs + [jnp.zeros((1, LANES), F32)], axis=1)
    packed = _small_adam(small_all,
                         cat([ada_b, pre_mix_g, post_mix_g, pre_ffn_g, post_ffn_g, ret_gn_g]),
                         cat([m_ada_b, m_pre_mix_g, m_post_mix_g, m_pre_ffn_g, m_post_ffn_g, m_ret_gn_g]),
                         cat([v_ada_b, v_pre_mix_g, v_post_mix_g, v_pre_ffn_g, v_post_ffn_g, v_ret_gn_g]))
    off = 0
    for nm, width in zip(vec_names, [6 * d, d, d, d, d, RET_V]):
        res[nm] = [p[:, off:off + width] for p in packed]
        off += width

    loss = (0.5 / d) * packed[0][0, off]
    order = ["ada_w", "ada_b", "pre_mix_g", "post_mix_g", "pre_ffn_g", "post_ffn_g", "w_in", "ret_gn_g",
             "w_ret_branch", "w_sb_branch", "w_out", "w_ff1", "w_ff2"]
    outs = [loss, grad_x[None]]
    for k in range(4):
        outs += [res[nm][k] for nm in order]
    return tuple(outs)
```

```python
import functools

import numpy as np
import jax
import jax.numpy as jnp
from jax import lax
from jax.experimental import pallas as pl
from jax.experimental.pallas import tpu as pltpu

F32 = jnp.float32
BF16 = jnp.bfloat16
N_DEV = 8
AXES = ("x", "y", "c")

EPS = 1e-6
CHUNK = 64
CHUNK_SHIFT = 6
HEADS = 8
RET_DQK = 64
RET_DV = 128
SB_DH = 64
RET_QK = HEADS * RET_DQK
RET_V = HEADS * RET_DV
SB_W = HEADS * SB_DH
ROPE_BASE = 10000.0
LANES = 128

ADAM_LR = 0.001
ADAM_B1 = 0.9
ADAM_B2 = 0.999
ADAM_EPS = 1e-08
ADAM_WD = 0.01
ADAM_STEP = 10

VMEM_LIMIT = 56 * 1024 * 1024

_NN = (((1,), (0,)), ((), ()))
_NT = (((1,), (1,)), ((), ()))
_TN = (((0,), (0,)), ((), ()))


def _dot(a, b, dims=_NN):
    if a.dtype != BF16:
        a = a.astype(BF16)
    if b.dtype != BF16:
        b = b.astype(BF16)
    return lax.dot_general(a, b, dims, preferred_element_type=F32)


def _dot_split(a, b):
    hi = a.astype(BF16)
    lo = (a - hi.astype(F32)).astype(BF16)
    return (lax.dot_general(hi, b, _NN, preferred_element_type=F32)
            + lax.dot_general(lo, b, _NN, preferred_element_type=F32))


def _sigmoid(x):
    return 1.0 / (1.0 + jnp.exp(-x))


def _rms(x, d):
    r = lax.rsqrt(jnp.sum(x * x, axis=1, keepdims=True) * (1.0 / d) + EPS)
    return x * r, r


def _rms_bwd(dn, n, r, d):
    return r * (dn - n * (jnp.sum(dn * n, axis=1, keepdims=True) * (1.0 / d)))


def _colsum(v):
    return jnp.sum(v, axis=0, keepdims=True)


def _accum(ref, val, first):
    @pl.when(first)
    def _():
        ref[...] = val

    @pl.when(jnp.logical_not(first))
    def _():
        ref[...] += val


ROW_SPLIT = 2


def _zero_at_start(refs):
    @pl.when(pl.program_id(0) == 0)
    def _():
        for r in refs:
            r[...] = jnp.zeros_like(r)


def _pieces(tm):
    step = tm // ROW_SPLIT
    return [slice(k * step, (k + 1) * step) for k in range(ROW_SPLIT)]


KIND_SLOTS = {"gather": N_DEV, "scatter": N_DEV, "gather_chip": N_DEV, "forward": N_DEV, "pair": N_DEV // 2,
              "chip_scatter": N_DEV // 2}
SEMS_PER_ARRAY = N_DEV - 1


def _exchange_copies(ins, outs, send_sems, recv_sems, local_sems, kinds):
    x, y, c = (lax.axis_index(a) for a in AXES)
    me, chip, sibling = 4 * x + 2 * y + c, 2 * x + y, (x, y, 1 - c)
    mesh_id = pl.DeviceIdType.MESH
    other_chips = []
    for k in range(1, N_DEV // 2):
        px = 1 - x if k & 2 else x
        py = 1 - y if k & 1 else y
        other_chips.append((px, py))
    copies = []
    for i, kind in enumerate(kinds):
        def remote(src, dst, k, to, i=i):
            return pltpu.make_async_remote_copy(
                src_ref=src, dst_ref=dst, send_sem=send_sems.at[i * SEMS_PER_ARRAY + k],
                recv_sem=recv_sems.at[i * SEMS_PER_ARRAY + k], device_id=to, device_id_type=mesh_id)

        if kind in ("gather", "scatter"):
            pick = (lambda ref, d: ref.at[d]) if kind == "scatter" else (lambda ref, d: ref)
            copies.append(pltpu.make_async_copy(pick(ins[i], me), outs[i].at[me], local_sems.at[i]))
            for k in range(1, N_DEV):
                to = (1 - x if k & 4 else x, 1 - y if k & 2 else y, 1 - c if k & 1 else c)
                copies.append(remote(pick(ins[i], 4 * to[0] + 2 * to[1] + to[2]), outs[i].at[me], k - 1, to))
        elif kind == "gather_chip":
            copies.append(pltpu.make_async_copy(ins[i], outs[i].at[me], local_sems.at[i]))
            copies.append(remote(ins[i], outs[i].at[me], 0, sibling))
            for k, (px, py) in enumerate(other_chips):
                copies.append(remote(ins[i], outs[i].at[me], 1 + k, (px, py, c)))
        elif kind == "forward":
            for k, (px, py) in enumerate(other_chips):
                slot = 4 * px + 2 * py + c
                copies.append(remote(outs[i].at[slot], outs[i].at[slot], k, sibling))
        elif kind == "pair":
            for k in range(N_DEV // 2):
                copies.append(remote(ins[i].at[2 * k + 1 - c], outs[i].at[k], k, sibling))
        elif kind == "chip_scatter":
            copies.append(pltpu.make_async_copy(ins[i].at[chip], outs[i].at[chip], local_sems.at[i]))
            for k, (px, py) in enumerate(other_chips):
                copies.append(remote(ins[i].at[2 * px + py], outs[i].at[chip], k, (px, py, c)))
        else:
            raise ValueError(kind)
    return copies


def _exchange_shapes(arrays, kinds):
    shapes = []
    for a, kind in zip(arrays, kinds):
        tail = a.shape if kind in ("gather", "gather_chip") else a.shape[1:]
        shapes.append(jax.ShapeDtypeStruct((KIND_SLOTS[kind],) + tuple(tail), a.dtype))
    return shapes


def _exchange_sems(n):
    return [pltpu.SemaphoreType.DMA((n * SEMS_PER_ARRAY,)), pltpu.SemaphoreType.DMA((n * SEMS_PER_ARRAY,)),
            pltpu.SemaphoreType.DMA((n,))]


def _call(name, body, grid, ins, outs, scratch=(), riders=None, prefetch=None):
    any_spec = pl.BlockSpec(memory_space=pl.ANY)
    in_specs = [pl.BlockSpec(memory_space=im) if bs is None else pl.BlockSpec(bs, im) for _, bs, im in ins]
    out_specs = [pl.BlockSpec(bs, im) for _, _, bs, im in outs]
    out_shape = [jax.ShapeDtypeStruct(s, d) for s, d, _, _ in outs]
    operands = [a for a, _, _ in ins]
    scratch = list(scratch)
    aliases = {}
    n_pre = 0 if prefetch is None else 1
    kernel = functools.partial(body) if prefetch is None else (lambda _, *refs: body(*refs))
    if riders is not None:
        arrays, kinds = riders
        nr, n_in, n_out, n_scr = len(arrays), len(ins), len(outs), len(scratch)

        def kernel(*refs):
            refs = refs[n_pre:]
            own_in, ride_in = refs[:n_in], refs[n_in:n_in + nr]
            own_out = refs[n_in + nr:n_in + nr + n_out]
            ride_out = refs[n_in + nr + n_out:n_in + 2 * nr + n_out]
            own_scr = refs[n_in + 2 * nr + n_out:n_in + 2 * nr + n_out + n_scr]
            sems = refs[n_in + 2 * nr + n_out + n_scr:]
            ids = [pl.program_id(a) for a in range(len(grid))]
            first = functools.reduce(jnp.logical_and, [i == 0 for i in ids])
            last = functools.reduce(jnp.logical_and, [i == g - 1 for i, g in zip(ids, grid)])

            @pl.when(first)
            def _():
                for cp in _exchange_copies(ride_in, ride_out, *sems, kinds):
                    cp.start()

            body(*own_in, *own_out, *own_scr)

            @pl.when(last)
            def _():
                for cp in _exchange_copies(ride_in, ride_out, *sems, kinds):
                    cp.wait()

        in_specs += [any_spec] * nr
        out_specs += [any_spec] * nr
        out_shape += _exchange_shapes(arrays, kinds)
        operands += list(arrays)
        scratch += _exchange_sems(nr)
        aliases = {n_pre + n_in + r: n_out + r for r, kind in enumerate(kinds) if kind == "forward"}
    params = pltpu.CompilerParams(dimension_semantics=("arbitrary",) * len(grid), vmem_limit_bytes=VMEM_LIMIT)
    if prefetch is None:
        return pl.pallas_call(kernel, name=name, grid=grid, in_specs=in_specs, out_specs=out_specs,
                              out_shape=out_shape, scratch_shapes=scratch, input_output_aliases=aliases,
                              compiler_params=params)(*operands)
    grid_spec = pltpu.PrefetchScalarGridSpec(num_scalar_prefetch=1, grid=grid, in_specs=in_specs,
                                             out_specs=out_specs, scratch_shapes=scratch)
    return pl.pallas_call(kernel, name=name, grid_spec=grid_spec, out_shape=out_shape,
                          input_output_aliases=aliases, compiler_params=params)(prefetch, *operands)


def _exchange(name, arrays, kinds):
    n = len(arrays)

    def body(*refs):
        copies = _exchange_copies(refs[:n], refs[n:2 * n], *refs[2 * n:], kinds)
        for cp in copies:
            cp.start()
        for cp in copies:
            cp.wait()

    any_spec = pl.BlockSpec(memory_space=pl.ANY)
    return pl.pallas_call(
        functools.partial(body),
        name=name,
        in_specs=[any_spec] * n,
        out_specs=[any_spec] * n,
        out_shape=_exchange_shapes(arrays, kinds),
        scratch_shapes=_exchange_sems(n),
        input_output_aliases={i: i for i, kind in enumerate(kinds) if kind == "forward"},
    )(*arrays)


def _pair_sum(name, mine, theirs, my_core, tr):
    _, rws, cls = mine.shape
    tr = min(tr, rws)

    def body(a_ref, b_ref, o_ref):
        o_ref[...] = (a_ref[...].astype(F32) + b_ref[...].astype(F32)).astype(o_ref.dtype)

    return _call(name, body, (N_DEV // 2, rws // tr),
                 [(mine, (None, tr, cls), lambda k, r, core: (2 * k + core[0], r, 0)),
                  (theirs, (None, tr, cls), lambda k, r, core: (k, r, 0))],
                 [((N_DEV // 2, rws, cls), mine.dtype, (None, tr, cls), lambda k, r, core: (k, r, 0))],
                 prefetch=my_core)[0]


def _matmul(name, a, b, kind, tm, tn, out_dtype, blocked_out=False, riders=None):
    if kind == "tn":
        kdim, m = a.shape
    else:
        m, kdim = a.shape
    n = b.shape[0] if kind == "nt" else b.shape[1]
    tm, tn = min(tm, m), min(tn, n)
    dims = {"nn": _NN, "nt": _NT, "tn": _TN}[kind]

    def body(a_ref, b_ref, o_ref):
        o_ref[...] = _dot(a_ref[...], b_ref[...], dims).astype(o_ref.dtype)

    a_spec = (a, (kdim, tm), lambda j, i: (0, i)) if kind == "tn" else (a, (tm, kdim), lambda j, i: (i, 0))
    b_spec = (b, (tn, kdim), lambda j, i: (j, 0)) if kind == "nt" else (b, (kdim, tn), lambda j, i: (0, j))
    if blocked_out:
        out = ((n // tn, m, tn), out_dtype, (None, tm, tn), lambda j, i: (j, i, 0))
    else:
        out = ((m, n), out_dtype, (tm, tn), lambda j, i: (i, j))
    res = _call(name, body, (n // tn, m // tm), [a_spec, b_spec], [out], riders=riders)
    return res[0] if riders is None else res


def _ada_fwd(cs_all, ada_w, ada_b_cols):
    def body(c_ref, w_ref, b_ref, o_ref):
        o_ref[...] = lax.dot_general(c_ref[...], w_ref[...], _NN, preferred_element_type=F32,
                                     precision=lax.Precision.HIGHEST) + b_ref[...]

    r, d = cs_all.shape
    nc = ada_w.shape[1]
    return _call("ada_fwd", body, (1,),
                 [(cs_all, (r, d), lambda i: (0, 0)), (ada_w, (d, nc), lambda i: (0, 0)),
                  (ada_b_cols, (1, nc), lambda i: (0, 0))],
                 [((r, nc), F32, (r, nc), lambda i: (0, 0))])[0]


def _silu_rows(c_all):
    def body(c_ref, o_ref):
        v = c_ref[...]
        o_ref[...] = v * _sigmoid(v)

    return _call("silu_c", body, (1,), [(c_all, c_all.shape, lambda i: (0, 0))],
                 [(c_all.shape, F32, c_all.shape, lambda i: (0, 0))])[0]


def _pre_norm(x, g, mod, tm, riders=None):
    s, d = x.shape

    def body(x_ref, g_ref, mod_ref, h_ref):
        n, _ = _rms(x_ref[...], d)
        sh, sc = mod_ref[:, 0:d], mod_ref[:, d:2 * d]
        h_ref[...] = (n * g_ref[...] * (1.0 + sc) + sh).astype(BF16)

    return _call("pre_norm", body, (s // tm,),
                 [(x, (tm, d), lambda i: (i, 0)), (g, (1, d), lambda i: (0, 0)),
                  (mod, (1, 6 * d), lambda i: (0, 0))],
                 [((s, d), BF16, (tm, d), lambda i: (i, 0))], riders=riders)


LOG2E = 1.4426950408889634
LN2 = 0.6931471805599453


def _decay_scale(lg_ref, idx, g, sign):
    return jnp.exp((sign * idx) * lg_ref[:, g * LANES:(g + 1) * LANES])


def _prep(proj, pos_col, idx_col, inv_freq, lg_lanes, tm):
    s = proj.shape[0]
    sb_off = (2 * RET_QK + 2 * RET_V) // (3 * SB_W)
    n_q = RET_QK // LANES

    def body(qk_ref, v_ref, sb_ref, pos_ref, idx_ref, f_ref, lg_ref, qk_out, v_out, sb_out, cos_out, sin_out):
        ang = pos_ref[...] * f_ref[...]
        lane = lax.broadcasted_iota(jnp.int32, (1, LANES), 1)
        first = jnp.bitwise_and(lane, RET_DQK - 1) < (RET_DQK // 2)
        cos = jnp.cos(ang)
        sin = jnp.where(first, -1.0, 1.0) * jnp.sin(ang)
        cos_out[...] = cos
        sin_out[...] = sin
        idx = idx_ref[...]
        for g in range(2 * n_q):
            v = qk_ref[:, g * LANES:(g + 1) * LANES].astype(F32)
            sw = jnp.where(first, pltpu.roll(v, LANES - RET_DQK // 2, 1), pltpu.roll(v, RET_DQK // 2, 1))
            r = v * cos + sw * sin
            if g < n_q:
                r = r * _decay_scale(lg_ref, idx, g, 1.0)
            else:
                r = r * (_decay_scale(lg_ref, idx, g - n_q, -1.0) * (RET_DQK ** -0.5))
            qk_out[:, g * LANES:(g + 1) * LANES] = r.astype(BF16)
        v_out[...] = v_ref[...].astype(BF16)
        sb_out[:, 0:SB_W] = (sb_ref[:, 0:SB_W].astype(F32) * (SB_DH ** -0.5 * LOG2E)).astype(BF16)
        sb_out[:, SB_W:3 * SB_W] = sb_ref[:, SB_W:3 * SB_W].astype(BF16)

    return _call("prep", body, (s // tm,),
                 [(proj, (tm, 2 * RET_QK), lambda i: (i, 0)),
                  (proj, (tm, RET_V), lambda i: (i, 2 * RET_QK // RET_V)),
                  (proj, (tm, 3 * SB_W), lambda i: (i, sb_off)),
                  (pos_col, (tm, 1), lambda i: (i, 0)),
                  (idx_col, (tm, 1), lambda i: (i, 0)),
                  (inv_freq, (1, LANES), lambda i: (0, 0)),
                  (lg_lanes, (1, RET_QK), lambda i: (0, 0))],
                 [((s, 2 * RET_QK), BF16, (tm, 2 * RET_QK), lambda i: (i, 0)),
                  ((s, RET_V), BF16, (tm, RET_V), lambda i: (i, 0)),
                  ((s, 3 * SB_W), BF16, (tm, 3 * SB_W), lambda i: (i, 0)),
                  ((s, LANES), F32, (tm, LANES), lambda i: (i, 0)),
                  ((s, LANES), F32, (tm, LANES), lambda i: (i, 0))])


def _head_mask(hh):
    lane = lax.broadcasted_iota(jnp.int32, (1, LANES), 1)
    return (lane >= RET_DQK) if hh else (lane < RET_DQK)


def _masked(v, m):
    return jnp.where(m, v, jnp.zeros_like(v))


SB_GROUP = 4
SB_TQ = 256
RET_GROUP = 4


def _stack_heads(v):
    return jnp.concatenate([_masked(v, _head_mask(0)), _masked(v, _head_mask(1))], axis=0)


def _side_by_side(v, t):
    return jnp.concatenate([v[:t], v[t:]], axis=1)


def _tile_pos(i, j, tq, tk):
    row = jnp.bitwise_and(lax.broadcasted_iota(jnp.int32, (2 * tq, tk), 0), tq - 1) + i * tq
    col = lax.broadcasted_iota(jnp.int32, (2 * tq, tk), 1) + j * tk
    return row, col


def _n_groups(i, tq, tk, grp):
    return ((i + 1) * (tq // tk) + grp - 1) // grp


def _n_full(i, tq, tk, grp):
    return (i * (tq // tk)) // grp


def _key_rows(j, tk):
    return pl.ds(pl.multiple_of(j * tk, tk), tk)


def _ret_weight(lg_rows, i, j, tq, tk):
    row, col = _tile_pos(i, j, tq, tk)
    same = jnp.right_shift(col, CHUNK_SHIFT) == jnp.right_shift(row, CHUNK_SHIFT)
    later = jnp.where(same, jnp.exp((2.0 * lg_rows) * (col - row).astype(F32)), 0.0)
    return jnp.where(col <= row, 1.0, later)


def _lg_rows(lg_ref, hp, tq):
    first = lax.broadcasted_iota(jnp.int32, (2 * tq, 1), 0) < tq
    return jnp.where(first, lg_ref[2 * hp], lg_ref[2 * hp + 1])


def _check_tiles(s, tq, tk, grp):
    assert tq % tk == 0 and tq & (tq - 1) == 0 and tk & (tk - 1) == 0
    assert s % tq == 0 and (s // tk) % grp == 0 and s // tk <= LANES


def _pair_mask():
    r = lax.broadcasted_iota(jnp.int32, (LANES, 2 * RET_DV), 0) >= RET_DQK
    c = lax.broadcasted_iota(jnp.int32, (LANES, 2 * RET_DV), 1) >= RET_DV
    return (r == c).astype(F32)


def _ret_block(lg_ref, hp, i, t, qb, kb):
    w = _ret_weight(_lg_rows(lg_ref, hp, t), i, i, t, t)
    return _dot(_stack_heads(qb), kb, _NT), w


RET_PAIRS = 2


def _lanes(ref, p, width):
    return ref[:, p * width:(p + 1) * width]


def _ret_fwd(qk_rot, v_bf, proj, gn_g, log_gamma, t, riders=None):
    s = qk_rot.shape[0]
    n_pair = HEADS // 2
    pw = 2 * RET_DV
    wq, wv = RET_PAIRS * LANES, RET_PAIRS * pw
    gate_off = (2 * RET_QK + RET_V) // wv
    assert s % t == 0 and t % CHUNK == 0 and t & (t - 1) == 0 and n_pair % RET_PAIRS == 0

    def body(lg_ref, q_ref, k_ref, v_ref, g_ref, w_ref, ret_ref, rg_ref, state_ref):
        hg, i = pl.program_id(0), pl.program_id(1)

        @pl.when(i == 0)
        def _():
            state_ref[...] = jnp.zeros_like(state_ref)

        pairs = range(RET_PAIRS)
        qbs = [_lanes(q_ref, p, LANES) for p in pairs]
        kbs = [_lanes(k_ref, p, LANES) for p in pairs]
        vbs = [_lanes(v_ref, p, pw) for p in pairs]
        zws = [_ret_block(lg_ref, hg * RET_PAIRS + p, i, t, qbs[p], kbs[p]) for p in pairs]
        ps = [(z * w).astype(BF16) for z, w in zws]
        outs = [jnp.concatenate([_dot(ps[p][:t], vbs[p][:, 0:RET_DV]), _dot(ps[p][t:], vbs[p][:, RET_DV:pw])], axis=1)
                + _dot(qbs[p], state_ref[p]) for p in pairs]
        for p in pairs:
            state_ref[p] += _pair_mask() * _dot(kbs[p], vbs[p], _TN)
        for p in pairs:
            for hh in range(2):
                cols = slice(p * pw + hh * RET_DV, p * pw + (hh + 1) * RET_DV)
                o = outs[p][:, hh * RET_DV:(hh + 1) * RET_DV]
                ret_ref[:, cols] = o
                mu = jnp.sum(o, axis=1, keepdims=True) * (1.0 / RET_DV)
                xc = o - mu
                var = jnp.sum(xc * xc, axis=1, keepdims=True) * (1.0 / RET_DV)
                nrm = xc * lax.rsqrt(var + EPS) * w_ref[:, cols]
                g = g_ref[:, cols].astype(F32)
                rg_ref[:, cols] = (g * _sigmoid(g) * nrm).astype(BF16)

    blk = lambda hg, i: (i, hg)
    return _call("ret_fwd", body, (n_pair // RET_PAIRS, s // t),
                 [(log_gamma, None, pltpu.SMEM),
                  (qk_rot, (t, wq), blk),
                  (qk_rot, (t, wq), lambda hg, i: (i, n_pair // RET_PAIRS + hg)),
                  (v_bf, (t, wv), blk),
                  (proj, (t, wv), lambda hg, i: (i, gate_off + hg)),
                  (gn_g, (1, wv), lambda hg, i: (0, hg))],
                 [((s, RET_V), F32, (t, wv), blk), ((s, RET_V), BF16, (t, wv), blk)],
                 scratch=[pltpu.VMEM((RET_PAIRS, LANES, pw), F32)], riders=riders)


def _tri(tk, strict_upper):
    r = lax.broadcasted_iota(jnp.int32, (tk, tk), 0)
    cc = lax.broadcasted_iota(jnp.int32, (tk, tk), 1)
    return ((r > cc) if strict_upper else (r < cc)).astype(BF16)


def _sb_valid(i, j, tq, tk):
    row, col = _tile_pos(i, j, tq, tk)
    return col < row


def _sb_fwd(qkv, tq, tk, riders=None):
    s = qkv.shape[0]
    n_pair = HEADS // 2
    _check_tiles(s, tq, tk, SB_GROUP)

    def body(q_ref, k_ref, v_ref, o_ref, a_ref):
        i = pl.program_id(1)
        upper = _tri(tk, True)
        qs = _stack_heads(q_ref[...])
        n_full, n_groups = _n_full(i, tq, tk, SB_GROUP), _n_groups(i, tq, tk, SB_GROUP)

        def make_step(near_diagonal, last):
            def step(n, carry):
                c, o = carry
                g = last - 1 - n
                js = [g * SB_GROUP + sub for sub in range(SB_GROUP)]
                zs = [_dot(qs, k_ref[_key_rows(j, tk), :], _NT) for j in js]
                log1ps = [jnp.log2(1.0 + jnp.exp2(-jnp.abs(z))) for z in zs]
                log_1ms = [-jnp.maximum(z, 0.0) - t for z, t in zip(zs, log1ps)]
                log_bs = [jnp.minimum(z, 0.0) - t for z, t in zip(zs, log1ps)]
                if near_diagonal:
                    valids = [_sb_valid(i, j, tq, tk) for j in js]
                    log_1ms = [jnp.where(v, l, 0.0) for v, l in zip(valids, log_1ms)]
                sticks = [_dot(l, upper) for l in log_1ms]
                sums = [jnp.sum(l, axis=1, keepdims=True) for l in log_1ms]
                cs = [None] * SB_GROUP
                for sub in reversed(range(SB_GROUP)):
                    cs[sub] = c
                    c = c + sums[sub]
                for sub, j in enumerate(js):
                    a = jnp.exp2(log_bs[sub] + sticks[sub] + cs[sub])
                    if near_diagonal:
                        a = jnp.where(valids[sub], a, 0.0)
                    a = a.astype(BF16)
                    a_ref[j] = a
                    o = o + _dot(_side_by_side(a, tq), _stack_heads(v_ref[_key_rows(j, tk), :]))
                return c, o
            return step

        carry = (jnp.zeros((2 * tq, 1), F32), jnp.zeros((tq, LANES), F32))
        carry = lax.fori_loop(0, n_groups - n_full, make_step(True, n_groups), carry)
        _, acc = lax.fori_loop(0, n_full, make_step(False, n_full), carry)
        o_ref[...] = acc

    n_kb = s // tk
    return _call("sb_fwd", body, (n_pair, s // tq),
                 [(qkv, (tq, LANES), lambda hp, i: (i, hp)),
                  (qkv, (s, LANES), lambda hp, i: (0, n_pair + hp)),
                  (qkv, (s, LANES), lambda hp, i: (0, 2 * n_pair + hp))],
                 [((s, SB_W), F32, (tq, LANES), lambda hp, i: (i, hp)),
                  ((n_pair, s // tq, n_kb, 2 * tq, tk), BF16, (None, None, n_kb, 2 * tq, tk),
                   lambda hp, i: (hp, i, 0, 0, 0))], riders=riders)


def _merge(retg, sb, w_ret, w_sb_t, proj, tm, tn, riders=None):
    s, d = retg.shape[0], w_ret.shape[1]
    ar_off = (2 * RET_QK + 2 * RET_V + 3 * SB_W) // tn
    as_off = ar_off + d // tn

    def body(rg_ref, sb_ref, wr_ref, ws_ref, ar_ref, as_ref, mix_ref, r_ref, s_ref):
        rr = _dot(rg_ref[...], wr_ref[...])
        ss = _dot(sb_ref[...], ws_ref[...], _NT)
        mix_ref[...] = (_sigmoid(ar_ref[...].astype(F32)) * rr + _sigmoid(as_ref[...].astype(F32)) * ss).astype(BF16)
        r_ref[...] = rr.astype(BF16)
        s_ref[...] = ss.astype(BF16)

    tile = (tm, tn)
    return _call("merge", body, (d // tn, s // tm),
                 [(retg, (tm, RET_V), lambda j, i: (i, 0)), (sb, (tm, SB_W), lambda j, i: (i, 0)),
                  (w_ret, (RET_V, tn), lambda j, i: (0, j)), (w_sb_t, (tn, SB_W), lambda j, i: (j, 0)),
                  (proj, tile, lambda j, i: (i, ar_off + j)), (proj, tile, lambda j, i: (i, as_off + j))],
                 [((s, d), BF16, tile, lambda j, i: (i, j))] * 3, riders=riders)


def _out_proj(mixed, w_out, x, mod, gp1, g2, tm):
    s, d = x.shape

    def body(a_ref, w_ref, x_ref, mod_ref, gp_ref, g2_ref, y_ref, hres_ref, h2_ref):
        for rows in _pieces(tm):
            y = _dot(a_ref[rows, :], w_ref[...])
            y_ref[rows, :] = y
            ny, _ = _rms(y, d)
            hres = x_ref[rows, :] + mod_ref[:, 2 * d:3 * d] * (ny * gp_ref[...])
            hres_ref[rows, :] = hres
            n2, _ = _rms(hres, d)
            h2_ref[rows, :] = (n2 * g2_ref[...] * (1.0 + mod_ref[:, 4 * d:5 * d]) + mod_ref[:, 3 * d:4 * d]).astype(BF16)

    row = lambda i: (i, 0)
    fix = lambda i: (0, 0)
    return _call("out_proj", body, (s // tm,),
                 [(mixed, (tm, d), row), (w_out, (d, d), fix), (x, (tm, d), row),
                  (mod, (1, 6 * d), fix), (gp1, (1, d), fix), (g2, (1, d), fix)],
                 [((s, d), F32, (tm, d), row), ((s, d), F32, (tm, d), row), ((s, d), BF16, (tm, d), row)])


def _ff1(h2, w_ff1_t, tm, tn):
    s, f = h2.shape[0], w_ff1_t.shape[0]
    tm = min(tm, s)

    def body(a_ref, w_ref, u_ref, act_ref):
        u = _dot(a_ref[...], w_ref[...], _NT)
        r = jnp.maximum(u, 0.0)
        u_ref[...] = u.astype(BF16)
        act_ref[...] = (r * r).astype(BF16)

    d = h2.shape[1]
    return _call("ff1", body, (f // tn, s // tm),
                 [(h2, (tm, d), lambda j, i: (i, 0)), (w_ff1_t, (tn, d), lambda j, i: (j, 0))],
                 [((s, f), BF16, (tm, tn), lambda j, i: (i, j))] * 2)


def _ff2_loss(act, w_ff2, hres, target, mod, gp2, tm):
    s, d = hres.shape
    f = act.shape[1]

    def body(a_ref, w_ref, h_ref, t_ref, mod_ref, gp_ref, dout_ref, df_ref, loss_ref, dgt_ref, dgp_ref):
        _zero_at_start([loss_ref, dgt_ref, dgp_ref])
        gt, gp = mod_ref[:, 5 * d:6 * d], gp_ref[...]
        for rows in _pieces(tm):
            ff = _dot(a_ref[rows, :], w_ref[...])
            nf, rf = _rms(ff, d)
            out = h_ref[rows, :] + gt * (nf * gp)
            err = out - t_ref[rows, :]
            sq = jnp.sum(err * err, axis=1, keepdims=True)
            loss_ref[...] += jnp.sum(sq, axis=0, keepdims=True)
            dout = err * (1.0 / d)
            dout_ref[rows, :] = dout
            dgt_ref[...] += _colsum(dout * (nf * gp))
            dgp_ref[...] += _colsum(dout * gt * nf)
            df_ref[rows, :] = _rms_bwd(dout * gt * gp, nf, rf, d).astype(BF16)

    row = lambda i: (i, 0)
    fix = lambda i: (0, 0)
    return _call("ff2_loss", body, (s // tm,),
                 [(act, (tm, f), row), (w_ff2, (f, d), fix), (hres, (tm, d), row), (target, (tm, d), row),
                  (mod, (1, 6 * d), fix), (gp2, (1, d), fix)],
                 [((s, d), F32, (tm, d), row), ((s, d), BF16, (tm, d), row), ((1, 1), F32, (1, 1), fix),
                  ((1, d), F32, (1, d), fix), ((1, d), F32, (1, d), fix)])


def _ff2_bwd(df, w_ff2, u, tm, tn):
    s, d = df.shape
    f = w_ff2.shape[0]
    tm = min(tm, s)

    def body(a_ref, w_ref, u_ref, du_ref):
        da = _dot(a_ref[...], w_ref[...], _NT)
        du_ref[...] = (da * (2.0 * jnp.maximum(u_ref[...].astype(F32), 0.0))).astype(BF16)

    return _call("ff2_bwd", body, (f // tn, s // tm),
                 [(df, (tm, d), lambda j, i: (i, 0)), (w_ff2, (tn, d), lambda j, i: (j, 0)),
                  (u, (tm, tn), lambda j, i: (i, j))],
                 [((s, f), BF16, (tm, tn), lambda j, i: (i, j))])[0]


def _ff1_bwd(du, w_ff1_t, hres, dout, y, mod, g2, gp1, tm, riders=None):
    s, d = hres.shape
    f = du.shape[1]

    def body(a_ref, w_ref, h_ref, do_ref, y_ref, mod_ref, g2_ref, gp_ref,
             dh_ref, dy_ref, dsh_ref, dsc_ref, dg2_ref, dgt_ref, dgp_ref):
        _zero_at_start([dsh_ref, dsc_ref, dg2_ref, dgt_ref, dgp_ref])
        g2, sc2 = g2_ref[...], mod_ref[:, 4 * d:5 * d]
        gt, gp = mod_ref[:, 2 * d:3 * d], gp_ref[...]
        for rows in _pieces(tm):
            dh2 = _dot(a_ref[rows, :], w_ref[...])
            n2, r2 = _rms(h_ref[rows, :], d)
            dsh_ref[...] += _colsum(dh2)
            dsc_ref[...] += _colsum(dh2 * n2 * g2)
            dg2_ref[...] += _colsum(dh2 * n2 * (1.0 + sc2))
            dhres = do_ref[rows, :] + _rms_bwd(dh2 * g2 * (1.0 + sc2), n2, r2, d)
            dh_ref[rows, :] = dhres
            ny, ry = _rms(y_ref[rows, :], d)
            dgt_ref[...] += _colsum(dhres * (ny * gp))
            dgp_ref[...] += _colsum(dhres * gt * ny)
            dy_ref[rows, :] = _rms_bwd(dhres * gt * gp, ny, ry, d).astype(BF16)

    row = lambda i: (i, 0)
    fix = lambda i: (0, 0)
    vec = ((1, d), F32, (1, d), fix)
    return _call("ff1_bwd", body, (s // tm,),
                 [(du, (tm, f), row), (w_ff1_t, (f, d), fix), (hres, (tm, d), row), (dout, (tm, d), row),
                  (y, (tm, d), row), (mod, (1, 6 * d), fix), (g2, (1, d), fix), (gp1, (1, d), fix)],
                 [((s, d), F32, (tm, d), row), ((s, d), BF16, (tm, d), row), vec, vec, vec, vec, vec], riders=riders)


def _out_bwd(dy, w_out, proj, r_bf, s_bf, tm, tn, riders=None):
    s, d = dy.shape
    ar_off = (2 * RET_QK + 2 * RET_V + 3 * SB_W) // tn
    as_off = ar_off + d // tn

    def body(a_ref, w_ref, ar_ref, as_ref, r_ref, s_ref, dr_ref, ds_ref, dar_ref, das_ref):
        dm = _dot(a_ref[...], w_ref[...], _NT)
        sr, ss = _sigmoid(ar_ref[...].astype(F32)), _sigmoid(as_ref[...].astype(F32))
        dr_ref[...] = (dm * sr).astype(BF16)
        ds_ref[...] = (dm * ss).astype(BF16)
        dar_ref[...] = (dm * r_ref[...].astype(F32) * sr * (1.0 - sr)).astype(BF16)
        das_ref[...] = (dm * s_ref[...].astype(F32) * ss * (1.0 - ss)).astype(BF16)

    tile = (tm, tn)
    here = lambda j, i: (i, j)
    return _call("out_bwd", body, (d // tn, s // tm),
                 [(dy, (tm, d), lambda j, i: (i, 0)), (w_out, (tn, d), lambda j, i: (j, 0)),
                  (proj, tile, lambda j, i: (i, ar_off + j)), (proj, tile, lambda j, i: (i, as_off + j)),
                  (r_bf, tile, here), (s_bf, tile, here)],
                 [((s, d), BF16, tile, here)] * 4, riders=riders)


def _gn_bwd(dretg, ret, proj, gn_g, tm, riders=None):
    s = ret.shape[0]
    gate_off = (2 * RET_QK + RET_V) // RET_V

    def body(d_ref, r_ref, g_ref, w_ref, dg_ref, dret_ref, dw_ref):
        first = pl.program_id(0) == 0
        for h in range(HEADS):
            cols = slice(h * RET_DV, (h + 1) * RET_DV)
            o, g, w, dr = r_ref[:, cols], g_ref[:, cols].astype(F32), w_ref[:, cols], d_ref[:, cols].astype(F32)
            mu = jnp.sum(o, axis=1, keepdims=True) * (1.0 / RET_DV)
            xc = o - mu
            rstd = lax.rsqrt(jnp.sum(xc * xc, axis=1, keepdims=True) * (1.0 / RET_DV) + EPS)
            n = xc * rstd
            sg = _sigmoid(g)
            silu = g * sg
            dg_ref[:, cols] = (dr * n * w * (sg * (1.0 + g * (1.0 - sg)))).astype(BF16)
            _accum(dw_ref.at[:, cols], _colsum(dr * silu * n), first)
            dn = dr * silu * w
            m1 = jnp.sum(dn, axis=1, keepdims=True) * (1.0 / RET_DV)
            m2 = jnp.sum(dn * n, axis=1, keepdims=True) * (1.0 / RET_DV)
            dret_ref[:, cols] = (rstd * (dn - m1 - n * m2)).astype(BF16)

    row = lambda i: (i, 0)
    fix = lambda i: (0, 0)
    return _call("gn_bwd", body, (s // tm,),
                 [(dretg, (tm, RET_V), row), (ret, (tm, RET_V), row),
                  (proj, (tm, RET_V), lambda i: (i, gate_off)), (gn_g, (1, RET_V), fix)],
                 [((s, RET_V), BF16, (tm, RET_V), row), ((s, RET_V), BF16, (tm, RET_V), row),
                  ((1, RET_V), F32, (1, RET_V), fix)], riders=riders)


def _ret_bwd(qk_rot, v_bf, dret, log_gamma, t, riders=None):
    s = qk_rot.shape[0]
    n_pair = HEADS // 2
    pw = 2 * RET_DV
    wq, wv = RET_PAIRS * LANES, RET_PAIRS * pw
    n_blk = s // t
    pairs = range(RET_PAIRS)

    def load(q_ref, k_ref, v_ref, do_ref):
        return ([_lanes(q_ref, p, LANES) for p in pairs], [_lanes(k_ref, p, LANES) for p in pairs],
                [_lanes(v_ref, p, pw) for p in pairs], [_lanes(do_ref, p, pw) for p in pairs])

    def d_scores(lg_ref, hp, i, qb, kb, vb, dob):
        z, w = _ret_block(lg_ref, hp, i, t, qb, kb)
        dp = jnp.concatenate([_dot(dob[:, 0:RET_DV], vb[:, 0:RET_DV], _NT),
                              _dot(dob[:, RET_DV:pw], vb[:, RET_DV:pw], _NT)], axis=0)
        return (z * w).astype(BF16), (dp * w).astype(BF16)

    def up_body(lg_ref, q_ref, k_ref, v_ref, do_ref, dq_ref, state_ref):
        hg, i = pl.program_id(0), pl.program_id(1)

        @pl.when(i == 0)
        def _():
            state_ref[...] = jnp.zeros_like(state_ref)

        qbs, kbs, vbs, dobs = load(q_ref, k_ref, v_ref, do_ref)
        dss = [d_scores(lg_ref, hg * RET_PAIRS + p, i, qbs[p], kbs[p], vbs[p], dobs[p])[1] for p in pairs]
        for p in pairs:
            dq_ref[:, p * LANES:(p + 1) * LANES] = (_dot(_side_by_side(dss[p], t), _stack_heads(kbs[p]))
                                                    + _dot(dobs[p], state_ref[p], _NT))
        for p in pairs:
            state_ref[p] += _pair_mask() * _dot(kbs[p], vbs[p], _TN)

    def down_body(lg_ref, q_ref, k_ref, v_ref, do_ref, dk_ref, dv_ref, state_ref):
        hg, i = pl.program_id(0), n_blk - 1 - pl.program_id(1)

        @pl.when(pl.program_id(1) == 0)
        def _():
            state_ref[...] = jnp.zeros_like(state_ref)

        qbs, kbs, vbs, dobs = load(q_ref, k_ref, v_ref, do_ref)
        both = [d_scores(lg_ref, hg * RET_PAIRS + p, i, qbs[p], kbs[p], vbs[p], dobs[p]) for p in pairs]
        for p in pairs:
            pp, ds = both[p]
            later = state_ref[p]
            dv_ref[:, p * pw:(p + 1) * pw] = jnp.concatenate(
                [_dot(pp[:t], dobs[p][:, 0:RET_DV], _TN), _dot(pp[t:], dobs[p][:, RET_DV:pw], _TN)],
                axis=1) + _dot(kbs[p], later)
            dk_ref[:, p * LANES:(p + 1) * LANES] = _dot(ds, _stack_heads(qbs[p]), _TN) + _dot(vbs[p], later, _NT)
        for p in pairs:
            state_ref[p] += _pair_mask() * _dot(qbs[p], dobs[p], _TN)

    n_grp = n_pair // RET_PAIRS

    def ins(order):
        return [(log_gamma, None, pltpu.SMEM),
                (qk_rot, (t, wq), lambda hg, i: (order(i), hg)),
                (qk_rot, (t, wq), lambda hg, i: (order(i), n_grp + hg)),
                (v_bf, (t, wv), lambda hg, i: (order(i), hg)),
                (dret, (t, wv), lambda hg, i: (order(i), hg))]

    up = lambda i: i
    down = lambda i: n_blk - 1 - i
    scratch = [pltpu.VMEM((RET_PAIRS, LANES, pw), F32)]
    dq = _call("ret_bwd_q", up_body, (n_grp, n_blk), ins(up),
               [((s, RET_QK), F32, (t, wq), lambda hg, i: (i, hg))], scratch=scratch)[0]
    dk, dv, *rest = _call("ret_bwd_kv", down_body, (n_grp, n_blk), ins(down),
                          [((s, RET_QK), F32, (t, wq), lambda hg, i: (down(i), hg)),
                           ((s, RET_V), F32, (t, wv), lambda hg, i: (down(i), hg))],
                          scratch=scratch, riders=riders)
    return [dq, dk, dv] + rest


def _sb_bwd(qkv, weights, do, tq, tk, riders=None):
    s = qkv.shape[0]
    n_pair = HEADS // 2
    _check_tiles(s, tq, tk, SB_GROUP)

    def body(q_ref, k_ref, v_ref, a_ref, do_ref, dq_ref, dk_ref, dv_ref):
        i = pl.program_id(1)

        @pl.when(i == 0)
        def _():
            dk_ref[...] = jnp.zeros_like(dk_ref)
            dv_ref[...] = jnp.zeros_like(dv_ref)

        lower = _tri(tk, False)
        qs = _stack_heads(q_ref[...])
        dos = _stack_heads(do_ref[...].astype(BF16))

        def make_step(near_diagonal):
            def step(g, carry):
                c_e, dq = carry
                js = [g * SB_GROUP + sub for sub in range(SB_GROUP)]
                rows = [_key_rows(j, tk) for j in js]
                zs = [_dot(qs, k_ref[rw, :], _NT) for rw in rows]
                das = [_dot(dos, v_ref[rw, :], _NT) for rw in rows]
                avals = [a_ref[j] for j in js]
                for a, rw in zip(avals, rows):
                    dv_ref[rw, :] += _dot(a, dos, _TN)
                es = [a.astype(F32) * da for a, da in zip(avals, das)]
                prefixes = [_dot(e, lower) for e in es]
                betas = [1.0 / (1.0 + jnp.exp2(-z)) for z in zs]
                for sub in range(SB_GROUP):
                    dz = es[sub] - (es[sub] + prefixes[sub] + c_e) * betas[sub]
                    if near_diagonal:
                        dz = jnp.where(_sb_valid(i, js[sub], tq, tk), dz, 0.0)
                    dz = dz.astype(BF16)
                    dk_ref[rows[sub], :] += _dot(dz, qs, _TN)
                    dq = dq + _dot(_side_by_side(dz, tq), _stack_heads(k_ref[rows[sub], :]))
                    c_e = c_e + jnp.sum(es[sub], axis=1, keepdims=True)
                return c_e, dq
            return step

        n_full = _n_full(i, tq, tk, SB_GROUP)
        carry = (jnp.zeros((2 * tq, 1), F32), jnp.zeros((tq, LANES), F32))
        carry = lax.fori_loop(0, n_full, make_step(False), carry)
        _, dq = lax.fori_loop(n_full, _n_groups(i, tq, tk, SB_GROUP), make_step(True), carry)
        dq_ref[...] = dq

    blk = lambda hp, i: (i, hp)
    n_kb = s // tk
    return _call("sb_bwd", body, (n_pair, s // tq),
                 [(qkv, (tq, LANES), blk),
                  (qkv, (s, LANES), lambda hp, i: (0, n_pair + hp)),
                  (qkv, (s, LANES), lambda hp, i: (0, 2 * n_pair + hp)),
                  (weights, (None, None, n_kb, 2 * tq, tk), lambda hp, i: (hp, i, 0, 0, 0)),
                  (do, (tq, LANES), blk)],
                 [((s, SB_W), F32, (tq, LANES), blk),
                  ((s, SB_W), F32, (s, LANES), lambda hp, i: (0, hp)),
                  ((s, SB_W), F32, (s, LANES), lambda hp, i: (0, hp))], riders=riders)


def _assemble_dproj(dq_r, dk_r, dv_r, dg_r, dq_s, dk_s, dv_s, da_r, da_s, cos, sin, idx_col, lg_lanes, tm):
    s, d = da_r.shape
    width = 2 * RET_QK + 2 * RET_V + 3 * SB_W + 2 * d

    def body(dq_ref, dk_ref, dv_ref, dg_ref, dqs_ref, dks_ref, dvs_ref, dar_ref, das_ref, cos_ref, sin_ref,
             idx_ref, lg_ref, o_ref):
        lane = lax.broadcasted_iota(jnp.int32, (1, LANES), 1)
        first = jnp.bitwise_and(lane, RET_DQK - 1) < (RET_DQK // 2)
        cos, sin = cos_ref[...], sin_ref[...]
        idx = idx_ref[...]
        for src, base, sign, scale in ((dq_ref, 0, 1.0, 1.0), (dk_ref, RET_QK, -1.0, RET_DQK ** -0.5)):
            for g in range(RET_QK // LANES):
                v = src[:, g * LANES:(g + 1) * LANES] * (_decay_scale(lg_ref, idx, g, sign) * scale)
                sw = jnp.where(first, pltpu.roll(v, LANES - RET_DQK // 2, 1), pltpu.roll(v, RET_DQK // 2, 1))
                o_ref[:, base + g * LANES:base + (g + 1) * LANES] = (v * cos - sw * sin).astype(BF16)
        off = 2 * RET_QK
        o_ref[:, off:off + RET_V] = dv_ref[...].astype(BF16)
        off += RET_V
        o_ref[:, off:off + RET_V] = dg_ref[...]
        off += RET_V
        o_ref[:, off:off + SB_W] = (dqs_ref[...] * (SB_DH ** -0.5)).astype(BF16)
        off += SB_W
        o_ref[:, off:off + SB_W] = (dks_ref[...] * LN2).astype(BF16)
        off += SB_W
        o_ref[:, off:off + SB_W] = dvs_ref[...].astype(BF16)
        off += SB_W
        o_ref[:, off:off + d] = dar_ref[...]
        off += d
        o_ref[:, off:off + d] = das_ref[...]

    row = lambda i: (i, 0)
    ins = [(a, (tm, a.shape[1]), row) for a in (dq_r, dk_r, dv_r, dg_r, dq_s, dk_s, dv_s, da_r, da_s, cos, sin, idx_col)]
    ins.append((lg_lanes, (1, RET_QK), lambda i: (0, 0)))
    return _call("assemble_dproj", body, (s // tm,), ins, [((s, width), BF16, (tm, width), row)])[0]


def _in_bwd(dproj, w_in_t, x, dhres, mod, g1, tm, riders=None):
    s, d = x.shape
    width = dproj.shape[1]

    def body(a_ref, w_ref, x_ref, dh_ref, mod_ref, g_ref, dx_ref, dsh_ref, dsc_ref, dg_ref):
        _zero_at_start([dsh_ref, dsc_ref, dg_ref])
        g1, sc1 = g_ref[...], mod_ref[:, d:2 * d]
        for rows in _pieces(tm):
            dh = _dot(a_ref[rows, :], w_ref[...])
            n1, r1 = _rms(x_ref[rows, :], d)
            dsh_ref[...] += _colsum(dh)
            dsc_ref[...] += _colsum(dh * n1 * g1)
            dg_ref[...] += _colsum(dh * n1 * (1.0 + sc1))
            dx_ref[rows, :] = dh_ref[rows, :] + _rms_bwd(dh * g1 * (1.0 + sc1), n1, r1, d)

    row = lambda i: (i, 0)
    fix = lambda i: (0, 0)
    vec = ((1, d), F32, (1, d), fix)
    return _call("in_bwd", body, (s // tm,),
                 [(dproj, (tm, width), row), (w_in_t, (width, d), fix), (x, (tm, d), row), (dhres, (tm, d), row),
                  (mod, (1, 6 * d), fix), (g1, (1, d), fix)],
                 [((s, d), F32, (tm, d), row), vec, vec, vec], riders=riders)


def _adamw(w, g, m, v):
    m = ADAM_B1 * m + (1.0 - ADAM_B1) * g
    v = ADAM_B2 * v + (1.0 - ADAM_B2) * (g * g)
    m_hat = m / (1.0 - ADAM_B1 ** ADAM_STEP)
    v_hat = v / (1.0 - ADAM_B2 ** ADAM_STEP)
    delta = -ADAM_LR * (m_hat / (jnp.sqrt(v_hat) + ADAM_EPS) + ADAM_WD * w)
    return delta, m, v


def _adam_reduce(name, parts, w, m, v, tr):
    rws, cls = w.shape
    tr = min(tr, rws)
    n_parts = parts.shape[0]

    def body(p_ref, w_ref, m_ref, v_ref, g_out, d_out, m_out, v_out):
        g = p_ref[0].astype(F32)
        for k in range(1, n_parts):
            g = g + p_ref[k].astype(F32)
        delta, mn, vn = _adamw(w_ref[...], g, m_ref[...], v_ref[...])
        g_out[...] = g
        d_out[...] = delta
        m_out[...] = mn
        v_out[...] = vn

    row = lambda i: (i, 0)
    blk = (tr, cls)
    return _call(name, body, (rws // tr,),
                 [(parts, (n_parts, tr, cls), lambda i: (0, i, 0)), (w, blk, row), (m, blk, row), (v, blk, row)],
                 [((rws, cls), F32, blk, row)] * 4)


def _ada_bwd_adam(cs_t, dmod_cols, w, m, v):
    d, nc = w.shape

    def body(c_ref, dm_ref, w_ref, m_ref, v_ref, g_out, d_out, m_out, v_out):
        g = c_ref[0] * dm_ref[0:1, :]
        for r in range(1, N_DEV):
            g = g + c_ref[r] * dm_ref[r:r + 1, :]
        delta, mn, vn = _adamw(w_ref[...], g, m_ref[...], v_ref[...])
        g_out[...] = g
        d_out[...] = delta
        m_out[...] = mn
        v_out[...] = vn

    fix = lambda i: (0, 0)
    blk = (d, nc)
    return _call("ada_bwd_adam", body, (1,),
                 [(cs_t, (N_DEV, d, 1), lambda i: (0, 0, 0)), (dmod_cols, (N_DEV, nc), fix), (w, blk, fix), (m, blk, fix), (v, blk, fix)],
                 [((d, nc), F32, blk, fix)] * 4)


def _small_adam(parts, w, m, v):
    n = w.shape[1]

    def body(p_ref, w_ref, m_ref, v_ref, g_out, d_out, m_out, v_out):
        g = p_ref[0:1, :]
        for k in range(1, N_DEV):
            g = g + p_ref[k:k + 1, :]
        delta, mn, vn = _adamw(w_ref[...], g, m_ref[...], v_ref[...])
        g_out[...] = g
        d_out[...] = delta
        m_out[...] = mn
        v_out[...] = vn

    fix = lambda i: (0, 0)
    return _call("small_adam", body, (1,),
                 [(parts, (N_DEV, n), fix), (w, (1, n), fix), (m, (1, n), fix), (v, (1, n), fix)],
                 [((1, n), F32, (1, n), fix)] * 4)


def kernel(x, c, positions, ada_w, ada_b, pre_mix_g, post_mix_g, pre_ffn_g, post_ffn_g, w_in, ret_gn_g, w_ret_branch, w_sb_branch, w_out, w_ff1, w_ff2, loss_target, m_ada_w, m_ada_b, m_pre_mix_g, m_post_mix_g, m_pre_ffn_g, m_post_ffn_g, m_w_in, m_ret_gn_g, m_w_ret_branch, m_w_sb_branch, m_w_out, m_w_ff1, m_w_ff2, v_ada_w, v_ada_b, v_pre_mix_g, v_post_mix_g, v_pre_ffn_g, v_post_ffn_g, v_w_in, v_ret_gn_g, v_w_ret_branch, v_w_sb_branch, v_w_out, v_w_ff1, v_w_ff2):
    _, s, d = x.shape
    d_ff = w_ff1.shape[2] * N_DEV
    d_in = w_in.shape[2] * N_DEV
    me = 4 * lax.axis_index("x") + 2 * lax.axis_index("y") + lax.axis_index("c")
    x2, tgt = x[0], loss_target[0]

    core = lax.axis_index("c").astype(jnp.int32).reshape(1)
    bf = lambda w: w[0].astype(BF16)

    w_in_t, m_in_t, v_in_t = (jnp.swapaxes(a[0], 0, 1) for a in (w_in, m_w_in, v_w_in))

    c_all, g_in = _exchange("gather_in", [c, w_in_t.astype(BF16)], ["gather", "gather_chip"])
    c_all = c_all.reshape(N_DEV, d)

    n_ada = ada_w.shape[2]
    cs_all = _silu_rows(c_all)
    ada_b_cols = lax.dynamic_slice(ada_b, (0, me * n_ada), (1, n_ada))
    mod_cols = _ada_fwd(cs_all, ada_w[0], ada_b_cols)
    mod_all = _exchange("gather_mod", [mod_cols], ["gather"])[0]
    mod = lax.dynamic_index_in_dim(mod_all, me, axis=1, keepdims=False).reshape(1, 6 * d)

    tm = min(256, s)
    h, g_in = _pre_norm(x2, pre_mix_g, mod, tm, riders=([g_in], ["forward"]))
    wt_in = g_in.reshape(d_in, d)
    bf_t = lambda w: jnp.swapaxes(w[0], 0, 1).astype(BF16)
    small_w = [bf(w_ret_branch), bf_t(w_sb_branch), bf(w_out)]
    proj, *small_w = _matmul("in_proj", h, wt_in, "nt", s, 512, BF16, riders=(small_w, ["gather_chip"] * 3))
    pos_col = positions.reshape(s, 1).astype(F32)
    freqs = ROPE_BASE ** (-jnp.arange(0, RET_DQK, 2, dtype=F32) / RET_DQK)
    inv_freq = jnp.tile(freqs, LANES // (RET_DQK // 2)).reshape(1, LANES)
    log_gamma_np = np.log1p(-(2.0 ** (-5.0 - np.arange(HEADS))))
    log_gamma = jnp.asarray(log_gamma_np, F32)
    lg_lanes = jnp.asarray(np.repeat(log_gamma_np, RET_DQK).reshape(1, RET_QK), F32)
    idx_col = (jnp.arange(s, dtype=F32) - (s // 2)).reshape(s, 1)
    qk_rot, v_bf, qkv_sb, cos_t, sin_t = _prep(proj, pos_col, idx_col, inv_freq, lg_lanes, tm)
    tq, tk = min(256, s), min(128, s)
    tq_sb = min(SB_TQ, s)
    sb, sb_weights, *big_w = _sb_fwd(qkv_sb, tq_sb, tk, riders=([bf(w_ff2), bf_t(w_ff1)], ["gather_chip"] * 2))
    ret, retg, g_ret, g_sb, g_out, g_ff2, g_ff1 = _ret_fwd(qk_rot, v_bf, proj, ret_gn_g, log_gamma, tq,
                                                           riders=(small_w + big_w, ["forward"] * 5))
    wf_ret = g_ret.reshape(RET_V, d)
    wt_sb = g_sb.reshape(d, SB_W)
    wf_out = g_out.reshape(d, d)
    wt_ff1 = g_ff1.reshape(d_ff, d)
    wf_ff2 = g_ff2.reshape(d_ff, d)
    mixed, r_bf, s_bf = _merge(retg, sb, wf_ret, wt_sb, proj, tm, min(512, d))
    y, hres, h2 = _out_proj(mixed, wf_out, x2, mod, post_mix_g, pre_ffn_g, tm)
    u, act = _ff1(h2, wt_ff1, s, 512)
    dout, df, loss_sum, d_gt2, d_gp2 = _ff2_loss(act, wf_ff2, hres, tgt, mod, post_ffn_g, tm)

    du = _ff2_bwd(df, wf_ff2, u, s, 512)
    gw_ff2 = _matmul("grad_w_ff2", act, df, "tn", 512, d, BF16).reshape(N_DEV, d_ff // N_DEV, d)
    gw_ff1 = _matmul("grad_w_ff1", h2, du, "tn", d, d_ff // N_DEV, BF16, blocked_out=True)
    dhres, dy, d_sh2, d_sc2, d_g2, d_gt1, d_gp1, t_ff1, t_ff2 = _ff1_bwd(
        du, wt_ff1, hres, dout, y, mod, pre_ffn_g, post_mix_g, tm, riders=([gw_ff1, gw_ff2], ["pair"] * 2))
    s_ff1 = _pair_sum("pair_sum_ff1", gw_ff1, t_ff1, core, 256)
    s_ff2 = _pair_sum("pair_sum_ff2", gw_ff2, t_ff2, core, 256)
    gw_out = _matmul("grad_w_out", mixed, dy, "tn", 512, d, BF16).reshape(N_DEV, d // N_DEV, d)
    d_r, d_s, da_r, da_s, p_out = _out_bwd(dy, wf_out, proj, r_bf, s_bf, tm, min(512, d), riders=([gw_out], ["scatter"]))
    dretg = _matmul("ret_branch_bwd", d_r, wf_ret, "nt", s, 512, BF16)
    dsb = _matmul("sb_branch_bwd", d_s, wt_sb, "nn", s, 512, F32)
    gw_ret = _matmul("grad_w_ret", retg, d_r, "tn", 512, d, BF16).reshape(N_DEV, RET_V // N_DEV, d)
    gw_sb = _matmul("grad_w_sb", sb, d_s, "tn", 512, d // N_DEV, BF16, blocked_out=True)
    dq_s, dk_s, dv_s, p_ff1, p_ff2, p_ret = _sb_bwd(qkv_sb, sb_weights, dsb, tq_sb, tk,
                                                    riders=([s_ff1, s_ff2, gw_ret], ["chip_scatter"] * 2 + ["scatter"]))
    dg_r, dret, d_gn = _gn_bwd(dretg, ret, proj, ret_gn_g, tm)
    dq_r, dk_r, dv_r, p_sb = _ret_bwd(qk_rot, v_bf, dret, log_gamma, tq, riders=([gw_sb], ["scatter"]))
    dproj = _assemble_dproj(dq_r, dk_r, dv_r, dg_r, dq_s, dk_s, dv_s, da_r, da_s, cos_t, sin_t, idx_col, lg_lanes, tm)
    gw_in = _matmul("grad_w_in", dproj, h, "tn", 512, d, BF16).reshape(N_DEV, d_in // N_DEV, d)
    t_in = _exchange("pair_in", [gw_in], ["pair"])[0]
    tr_in = d_in // N_DEV // 4
    s_in = _pair_sum("pair_sum_in", gw_in, t_in, core, tr_in)
    grad_x, d_sh1, d_sc1, d_g1, p_in = _in_bwd(dproj, wt_in, x2, dhres, mod, pre_mix_g, tm,
                                               riders=([s_in], ["chip_scatter"]))
    loss_lanes = jnp.pad(loss_sum, ((0, 0), (0, LANES - 1)))
    small = jnp.concatenate([d_sh1, d_sc1, d_gt1, d_sh2, d_sc2, d_gt2, d_g1, d_gp1, d_g2, d_gp2, d_gn, loss_lanes], axis=1)
    small_all = _exchange("gather_small", [small], ["gather"])[0].reshape(N_DEV, small.shape[1])
    parts = [p_in, p_ret, p_sb, p_out, p_ff1, p_ff2]

    res = {}
    names = ["w_ret_branch", "w_sb_branch", "w_out", "w_ff1", "w_ff2"]
    ws = [w_ret_branch, w_sb_branch, w_out, w_ff1, w_ff2]
    ms = [m_w_ret_branch, m_w_sb_branch, m_w_out, m_w_ff1, m_w_ff2]
    vs = [v_w_ret_branch, v_w_sb_branch, v_w_out, v_w_ff1, v_w_ff2]
    for nm, p, w, m, v in zip(names, parts[1:], ws, ms, vs):
        res[nm] = [o[None] for o in _adam_reduce("adam_" + nm, p, w[0], m[0], v[0], 256)]
    res["w_in"] = [jnp.swapaxes(o, 0, 1)[None]
                   for o in _adam_reduce("adam_w_in", parts[0], w_in_t, m_in_t, v_in_t, tr_in)]
    dmod_cols = lax.dynamic_slice(small_all, (0, me * n_ada), (N_DEV, n_ada))
    res["ada_w"] = [o[None] for o in _ada_bwd_adam(cs_all.reshape(N_DEV, d, 1), dmod_cols, ada_w[0], m_ada_w[0], v_ada_w[0])]
    vec_names = ["ada_b", "pre_mix_g", "post_mix_g", "pre_ffn_g", "post_ffn_g", "ret_gn_g"]
    cat = lambda xs: jnp.concatenate(xs + [jnp.zeros((1, LANES), F32)], axis=1)
    packed = _small_adam(small_all,
                         cat([ada_b, pre_mix_g, post_mix_g, pre_ffn_g, post_ffn_g, ret_gn_g]),
                         cat([m_ada_b, m_pre_mix_g, m_post_mix_g, m_pre_ffn_g, m_post_ffn_g, m_ret_gn_g]),
                         cat([v_ada_b, v_pre_mix_g, v_post_mix_g, v_pre_ffn_g, v_post_ffn_g, v_ret_gn_g]))
    off = 0
    for nm, width in zip(vec_names, [6 * d, d, d, d, d, RET_V]):
        res[nm] = [p[:, off:off + width] for p in packed]
        off += width

    loss = (0.5 / d) * packed[0][0, off]
    order = ["ada_w", "ada_b", "pre_mix_g", "post_mix_g", "pre_ffn_g", "post_ffn_g", "w_in", "ret_gn_g",
             "w_ret_branch", "w_sb_branch", "w_out", "w_ff1", "w_ff2"]
    outs = [loss, grad_x[None]]
    for k in range(4):
        outs += [res[nm][k] for nm in order]
    return tuple(outs)
```

```python
import functools

import numpy as np
import jax
import jax.numpy as jnp
from jax import lax
from jax.experimental import pallas as pl
from jax.experimental.pallas import tpu as pltpu

F32 = jnp.float32
BF16 = jnp.bfloat16
N_DEV = 8
AXES = ("x", "y", "c")

EPS = 1e-6
CHUNK = 64
CHUNK_SHIFT = 6
HEADS = 8
RET_DQK = 64
RET_DV = 128
SB_DH = 64
RET_QK = HEADS * RET_DQK
RET_V = HEADS * RET_DV
SB_W = HEADS * SB_DH
ROPE_BASE = 10000.0
LANES = 128

ADAM_LR = 0.001
ADAM_B1 = 0.9
ADAM_B2 = 0.999
ADAM_EPS = 1e-08
ADAM_WD = 0.01
ADAM_STEP = 10

VMEM_LIMIT = 56 * 1024 * 1024

_NN = (((1,), (0,)), ((), ()))
_NT = (((1,), (1,)), ((), ()))
_TN = (((0,), (0,)), ((), ()))


def _dot(a, b, dims=_NN):
    if a.dtype != BF16:
        a = a.astype(BF16)
    if b.dtype != BF16:
        b = b.astype(BF16)
    return lax.dot_general(a, b, dims, preferred_element_type=F32)


def _dot_split(a, b):
    hi = a.astype(BF16)
    lo = (a - hi.astype(F32)).astype(BF16)
    return (lax.dot_general(hi, b, _NN, preferred_element_type=F32)
            + lax.dot_general(lo, b, _NN, preferred_element_type=F32))


def _sigmoid(x):
    return 1.0 / (1.0 + jnp.exp(-x))


def _rms(x, d):
    r = lax.rsqrt(jnp.sum(x * x, axis=1, keepdims=True) * (1.0 / d) + EPS)
    return x * r, r


def _rms_bwd(dn, n, r, d):
    return r * (dn - n * (jnp.sum(dn * n, axis=1, keepdims=True) * (1.0 / d)))


def _colsum(v):
    return jnp.sum(v, axis=0, keepdims=True)


def _accum(ref, val, first):
    @pl.when(first)
    def _():
        ref[...] = val

    @pl.when(jnp.logical_not(first))
    def _():
        ref[...] += val


ROW_SPLIT = 2


def _zero_at_start(refs):
    @pl.when(pl.program_id(0) == 0)
    def _():
        for r in refs:
            r[...] = jnp.zeros_like(r)


def _pieces(tm):
    step = tm // ROW_SPLIT
    return [slice(k * step, (k + 1) * step) for k in range(ROW_SPLIT)]


KIND_SLOTS = {"gather": N_DEV, "scatter": N_DEV, "gather_chip": N_DEV, "forward": N_DEV, "pair": N_DEV // 2,
              "chip_scatter": N_DEV // 2}
SEMS_PER_ARRAY = N_DEV - 1


def _exchange_copies(ins, outs, send_sems, recv_sems, local_sems, kinds):
    x, y, c = (lax.axis_index(a) for a in AXES)
    me, chip, sibling = 4 * x + 2 * y + c, 2 * x + y, (x, y, 1 - c)
    mesh_id = pl.DeviceIdType.MESH
    other_chips = []
    for k in range(1, N_DEV // 2):
        px = 1 - x if k & 2 else x
        py = 1 - y if k & 1 else y
        other_chips.append((px, py))
    copies = []
    for i, kind in enumerate(kinds):
        def remote(src, dst, k, to, i=i):
            return pltpu.make_async_remote_copy(
                src_ref=src, dst_ref=dst, send_sem=send_sems.at[i * SEMS_PER_ARRAY + k],
                recv_sem=recv_sems.at[i * SEMS_PER_ARRAY + k], device_id=to, device_id_type=mesh_id)

        if kind in ("gather", "scatter"):
            pick = (lambda ref, d: ref.at[d]) if kind == "scatter" else (lambda ref, d: ref)
            copies.append(pltpu.make_async_copy(pick(ins[i], me), outs[i].at[me], local_sems.at[i]))
            for k in range(1, N_DEV):
                to = (1 - x if k & 4 else x, 1 - y if k & 2 else y, 1 - c if k & 1 else c)
                copies.append(remote(pick(ins[i], 4 * to[0] + 2 * to[1] + to[2]), outs[i].at[me], k - 1, to))
        elif kind == "gather_chip":
            copies.append(pltpu.make_async_copy(ins[i], outs[i].at[me], local_sems.at[i]))
            copies.append(remote(ins[i], outs[i].at[me], 0, sibling))
            for k, (px, py) in enumerate(other_chips):
                copies.append(remote(ins[i], outs[i].at[me], 1 + k, (px, py, c)))
        elif kind == "forward":
            for k, (px, py) in enumerate(other_chips):
                slot = 4 * px + 2 * py + c
                copies.append(remote(outs[i].at[slot], outs[i].at[slot], k, sibling))
        elif kind == "pair":
            for k in range(N_DEV // 2):
                copies.append(remote(ins[i].at[2 * k + 1 - c], outs[i].at[k], k, sibling))
        elif kind == "chip_scatter":
            copies.append(pltpu.make_async_copy(ins[i].at[chip], outs[i].at[chip], local_sems.at[i]))
            for k, (px, py) in enumerate(other_chips):
                copies.append(remote(ins[i].at[2 * px + py], outs[i].at[chip], k, (px, py, c)))
        else:
            raise ValueError(kind)
    return copies


def _exchange_shapes(arrays, kinds):
    shapes = []
    for a, kind in zip(arrays, kinds):
        tail = a.shape if kind in ("gather", "gather_chip") else a.shape[1:]
        shapes.append(jax.ShapeDtypeStruct((KIND_SLOTS[kind],) + tuple(tail), a.dtype))
    return shapes


def _exchange_sems(n):
    return [pltpu.SemaphoreType.DMA((n * SEMS_PER_ARRAY,)), pltpu.SemaphoreType.DMA((n * SEMS_PER_ARRAY,)),
            pltpu.SemaphoreType.DMA((n,))]


def _call(name, body, grid, ins, outs, scratch=(), riders=None, prefetch=None):
    any_spec = pl.BlockSpec(memory_space=pl.ANY)
    in_specs = [pl.BlockSpec(memory_space=im) if bs is None else pl.BlockSpec(bs, im) for _, bs, im in ins]
    out_specs = [pl.BlockSpec(bs, im) for _, _, bs, im in outs]
    out_shape = [jax.ShapeDtypeStruct(s, d) for s, d, _, _ in outs]
    operands = [a for a, _, _ in ins]
    scratch = list(scratch)
    aliases = {}
    n_pre = 0 if prefetch is None else 1
    kernel = functools.partial(body) if prefetch is None else (lambda _, *refs: body(*refs))
    if riders is not None:
        arrays, kinds = riders
        nr, n_in, n_out, n_scr = len(arrays), len(ins), len(outs), len(scratch)

        def kernel(*refs):
            refs = refs[n_pre:]
            own_in, ride_in = refs[:n_in], refs[n_in:n_in + nr]
            own_out = refs[n_in + nr:n_in + nr + n_out]
            ride_out = refs[n_in + nr + n_out:n_in + 2 * nr + n_out]
            own_scr = refs[n_in + 2 * nr + n_out:n_in + 2 * nr + n_out + n_scr]
            sems = refs[n_in + 2 * nr + n_out + n_scr:]
            ids = [pl.program_id(a) for a in range(len(grid))]
            first = functools.reduce(jnp.logical_and, [i == 0 for i in ids])
            last = functools.reduce(jnp.logical_and, [i == g - 1 for i, g in zip(ids, grid)])

            @pl.when(first)
            def _():
                for cp in _exchange_copies(ride_in, ride_out, *sems, kinds):
                    cp.start()

            body(*own_in, *own_out, *own_scr)

            @pl.when(last)
            def _():
                for cp in _exchange_copies(ride_in, ride_out, *sems, kinds):
                    cp.wait()

        in_specs += [any_spec] * nr
        out_specs += [any_spec] * nr
        out_shape += _exchange_shapes(arrays, kinds)
        operands += list(arrays)
        scratch += _exchange_sems(nr)
        aliases = {n_pre + n_in + r: n_out + r for r, kind in enumerate(kinds) if kind == "forward"}
    params = pltpu.CompilerParams(dimension_semantics=("arbitrary",) * len(grid), vmem_limit_bytes=VMEM_LIMIT)
    if prefetch is None:
        return pl.pallas_call(kernel, name=name, grid=grid, in_specs=in_specs, out_specs=out_specs,
                              out_shape=out_shape, scratch_shapes=scratch, input_output_aliases=aliases,
                              compiler_params=params)(*operands)
    grid_spec = pltpu.PrefetchScalarGridSpec(num_scalar_prefetch=1, grid=grid, in_specs=in_specs,
                                             out_specs=out_specs, scratch_shapes=scratch)
    return pl.pallas_call(kernel, name=name, grid_spec=grid_spec, out_shape=out_shape,
                          input_output_aliases=aliases, compiler_params=params)(prefetch, *operands)


def _exchange(name, arrays, kinds):
    n = len(arrays)

    def body(*refs):
        copies = _exchange_copies(refs[:n], refs[n:2 * n], *refs[2 * n:], kinds)
        for cp in copies:
            cp.start()
        for cp in copies:
            cp.wait()

    any_spec = pl.BlockSpec(memory_space=pl.ANY)
    return pl.pallas_call(
        functools.partial(body),
        name=name,
        in_specs=[any_spec] * n,
        out_specs=[any_spec] * n,
        out_shape=_exchange_shapes(arrays, kinds),
        scratch_shapes=_exchange_sems(n),
        input_output_aliases={i: i for i, kind in enumerate(kinds) if kind == "forward"},
    )(*arrays)


def _pair_sum(name, mine, theirs, my_core, tr):
    _, rws, cls = mine.shape
    tr = min(tr, rws)

    def body(a_ref, b_ref, o_ref):
        o_ref[...] = (a_ref[...].astype(F32) + b_ref[...].astype(F32)).astype(o_ref.dtype)

    return _call(name, body, (N_DEV // 2, rws // tr),
                 [(mine, (None, tr, cls), lambda k, r, core: (2 * k + core[0], r, 0)),
                  (theirs, (None, tr, cls), lambda k, r, core: (k, r, 0))],
                 [((N_DEV // 2, rws, cls), mine.dtype, (None, tr, cls), lambda k, r, core: (k, r, 0))],
                 prefetch=my_core)[0]


def _matmul(name, a, b, kind, tm, tn, out_dtype, blocked_out=False, riders=None):
    if kind == "tn":
        kdim, m = a.shape
    else:
        m, kdim = a.shape
    n = b.shape[0] if kind == "nt" else b.shape[1]
    tm, tn = min(tm, m), min(tn, n)
    dims = {"nn": _NN, "nt": _NT, "tn": _TN}[kind]

    def body(a_ref, b_ref, o_ref):
        o_ref[...] = _dot(a_ref[...], b_ref[...], dims).astype(o_ref.dtype)

    a_spec = (a, (kdim, tm), lambda j, i: (0, i)) if kind == "tn" else (a, (tm, kdim), lambda j, i: (i, 0))
    b_spec = (b, (tn, kdim), lambda j, i: (j, 0)) if kind == "nt" else (b, (kdim, tn), lambda j, i: (0, j))
    if blocked_out:
        out = ((n // tn, m, tn), out_dtype, (None, tm, tn), lambda j, i: (j, i, 0))
    else:
        out = ((m, n), out_dtype, (tm, tn), lambda j, i: (i, j))
    res = _call(name, body, (n // tn, m // tm), [a_spec, b_spec], [out], riders=riders)
    return res[0] if riders is None else res


def _ada_fwd(cs_all, ada_w, ada_b_cols):
    def body(c_ref, w_ref, b_ref, o_ref):
        o_ref[...] = lax.dot_general(c_ref[...], w_ref[...], _NN, preferred_element_type=F32,
                                     precision=lax.Precision.HIGHEST) + b_ref[...]

    r, d = cs_all.shape
    nc = ada_w.shape[1]
    return _call("ada_fwd", body, (1,),
                 [(cs_all, (r, d), lambda i: (0, 0)), (ada_w, (d, nc), lambda i: (0, 0)),
                  (ada_b_cols, (1, nc), lambda i: (0, 0))],
                 [((r, nc), F32, (r, nc), lambda i: (0, 0))])[0]


def _silu_rows(c_all):
    def body(c_ref, o_ref):
        v = c_ref[...]
        o_ref[...] = v * _sigmoid(v)

    return _call("silu_c", body, (1,), [(c_all, c_all.shape, lambda i: (0, 0))],
                 [(c_all.shape, F32, c_all.shape, lambda i: (0, 0))])[0]


def _pre_norm(x, g, mod, tm, riders=None):
    s, d = x.shape

    def body(x_ref, g_ref, mod_ref, h_ref):
        n, _ = _rms(x_ref[...], d)
        sh, sc = mod_ref[:, 0:d], mod_ref[:, d:2 * d]
        h_ref[...] = (n * g_ref[...] * (1.0 + sc) + sh).astype(BF16)

    return _call("pre_norm", body, (s // tm,),
                 [(x, (tm, d), lambda i: (i, 0)), (g, (1, d), lambda i: (0, 0)),
                  (mod, (1, 6 * d), lambda i: (0, 0))],
                 [((s, d), BF16, (tm, d), lambda i: (i, 0))], riders=riders)


LOG2E = 1.4426950408889634
LN2 = 0.6931471805599453


def _decay_scale(lg_ref, idx, g, sign):
    return jnp.exp((sign * idx) * lg_ref[:, g * LANES:(g + 1) * LANES])


def _prep(proj, pos_col, idx_col, inv_freq, lg_lanes, tm):
    s = proj.shape[0]
    sb_off = (2 * RET_QK + 2 * RET_V) // (3 * SB_W)
    n_q = RET_QK // LANES

    def body(qk_ref, v_ref, sb_ref, pos_ref, idx_ref, f_ref, lg_ref, qk_out, v_out, sb_out, cos_out, sin_out):
        ang = pos_ref[...] * f_ref[...]
        lane = lax.broadcasted_iota(jnp.int32, (1, LANES), 1)
        first = jnp.bitwise_and(lane, RET_DQK - 1) < (RET_DQK // 2)
        cos = jnp.cos(ang)
        sin = jnp.where(first, -1.0, 1.0) * jnp.sin(ang)
        cos_out[...] = cos
        sin_out[...] = sin
        idx = idx_ref[...]
        for g in range(2 * n_q):
            v = qk_ref[:, g * LANES:(g + 1) * LANES].astype(F32)
            sw = jnp.where(first, pltpu.roll(v, LANES - RET_DQK // 2, 1), pltpu.roll(v, RET_DQK // 2, 1))
            r = v * cos + sw * sin
            if g < n_q:
                r = r * _decay_scale(lg_ref, idx, g, 1.0)
            else:
                r = r * (_decay_scale(lg_ref, idx, g - n_q, -1.0) * (RET_DQK ** -0.5))
            qk_out[:, g * LANES:(g + 1) * LANES] = r.astype(BF16)
        v_out[...] = v_ref[...].astype(BF16)
        sb_out[:, 0:SB_W] = (sb_ref[:, 0:SB_W].astype(F32) * (SB_DH ** -0.5 * LOG2E)).astype(BF16)
        sb_out[:, SB_W:3 * SB_W] = sb_ref[:, SB_W:3 * SB_W].astype(BF16)

    return _call("prep", body, (s // tm,),
                 [(proj, (tm, 2 * RET_QK), lambda i: (i, 0)),
                  (proj, (tm, RET_V), lambda i: (i, 2 * RET_QK // RET_V)),
                  (proj, (tm, 3 * SB_W), lambda i: (i, sb_off)),
                  (pos_col, (tm, 1), lambda i: (i, 0)),
                  (idx_col, (tm, 1), lambda i: (i, 0)),
                  (inv_freq, (1, LANES), lambda i: (0, 0)),
                  (lg_lanes, (1, RET_QK), lambda i: (0, 0))],
                 [((s, 2 * RET_QK), BF16, (tm, 2 * RET_QK), lambda i: (i, 0)),
                  ((s, RET_V), BF16, (tm, RET_V), lambda i: (i, 0)),
                  ((s, 3 * SB_W), BF16, (tm, 3 * SB_W), lambda i: (i, 0)),
                  ((s, LANES), F32, (tm, LANES), lambda i: (i, 0)),
                  ((s, LANES), F32, (tm, LANES), lambda i: (i, 0))])


def _head_mask(hh):
    lane = lax.broadcasted_iota(jnp.int32, (1, LANES), 1)
    return (lane >= RET_DQK) if hh else (lane < RET_DQK)


def _masked(v, m):
    return jnp.where(m, v, jnp.zeros_like(v))


SB_GROUP = 4
SB_TQ = 256
RET_GROUP = 4


def _stack_heads(v):
    return jnp.concatenate([_masked(v, _head_mask(0)), _masked(v, _head_mask(1))], axis=0)


def _side_by_side(v, t):
    return jnp.concatenate([v[:t], v[t:]], axis=1)


def _tile_pos(i, j, tq, tk):
    row = jnp.bitwise_and(lax.broadcasted_iota(jnp.int32, (2 * tq, tk), 0), tq - 1) + i * tq
    col = lax.broadcasted_iota(jnp.int32, (2 * tq, tk), 1) + j * tk
    return row, col


def _n_groups(i, tq, tk, grp):
    return ((i + 1) * (tq // tk) + grp - 1) // grp


def _n_full(i, tq, tk, grp):
    return (i * (tq // tk)) // grp


def _key_rows(j, tk):
    return pl.ds(pl.multiple_of(j * tk, tk), tk)


def _ret_weight(lg_rows, i, j, tq, tk):
    row, col = _tile_pos(i, j, tq, tk)
    same = jnp.right_shift(col, CHUNK_SHIFT) == jnp.right_shift(row, CHUNK_SHIFT)
    later = jnp.where(same, jnp.exp((2.0 * lg_rows) * (col - row).astype(F32)), 0.0)
    return jnp.where(col <= row, 1.0, later)


def _lg_rows(lg_ref, hp, tq):
    first = lax.broadcasted_iota(jnp.int32, (2 * tq, 1), 0) < tq
    return jnp.where(first, lg_ref[2 * hp], lg_ref[2 * hp + 1])


def _check_tiles(s, tq, tk, grp):
    assert tq % tk == 0 and tq & (tq - 1) == 0 and tk & (tk - 1) == 0
    assert s % tq == 0 and (s // tk) % grp == 0 and s // tk <= LANES


def _pair_mask():
    r = lax.broadcasted_iota(jnp.int32, (LANES, 2 * RET_DV), 0) >= RET_DQK
    c = lax.broadcasted_iota(jnp.int32, (LANES, 2 * RET_DV), 1) >= RET_DV
    return (r == c).astype(F32)


def _ret_block(lg_ref, hp, i, t, qb, kb):
    w = _ret_weight(_lg_rows(lg_ref, hp, t), i, i, t, t)
    return _dot(_stack_heads(qb), kb, _NT), w


RET_PAIRS = 2


def _lanes(ref, p, width):
    return ref[:, p * width:(p + 1) * width]


def _ret_fwd(qk_rot, v_bf, proj, gn_g, log_gamma, t, riders=None):
    s = qk_rot.shape[0]
    n_pair = HEADS // 2
    pw = 2 * RET_DV
    wq, wv = RET_PAIRS * LANES, RET_PAIRS * pw
    gate_off = (2 * RET_QK + RET_V) // wv
    assert s % t == 0 and t % CHUNK == 0 and t & (t - 1) == 0 and n_pair % RET_PAIRS == 0

    def body(lg_ref, q_ref, k_ref, v_ref, g_ref, w_ref, ret_ref, rg_ref, state_ref):
        hg, i = pl.program_id(0), pl.program_id(1)

        @pl.when(i == 0)
        def _():
            state_ref[...] = jnp.zeros_like(state_ref)

        pairs = range(RET_PAIRS)
        qbs = [_lanes(q_ref, p, LANES) for p in pairs]
        kbs = [_lanes(k_ref, p, LANES) for p in pairs]
        vbs = [_lanes(v_ref, p, pw) for p in pairs]
        zws = [_ret_block(lg_ref, hg * RET_PAIRS + p, i, t, qbs[p], kbs[p]) for p in pairs]
        ps = [(z * w).astype(BF16) for z, w in zws]
        outs = [jnp.concatenate([_dot(ps[p][:t], vbs[p][:, 0:RET_DV]), _dot(ps[p][t:], vbs[p][:, RET_DV:pw])], axis=1)
                + _dot(qbs[p], state_ref[p]) for p in pairs]
        for p in pairs:
            state_ref[p] += _pair_mask() * _dot(kbs[p], vbs[p], _TN)
        for p in pairs:
            for hh in range(2):
                cols = slice(p * pw + hh * RET_DV, p * pw + (hh + 1) * RET_DV)
                o = outs[p][:, hh * RET_DV:(hh + 1) * RET_DV]
                ret_ref[:, cols] = o
                mu = jnp.sum(o, axis=1, keepdims=True) * (1.0 / RET_DV)
                xc = o - mu
                var = jnp.sum(xc * xc, axis=1, keepdims=True) * (1.0 / RET_DV)
                nrm = xc * lax.rsqrt(var + EPS) * w_ref[:, cols]
                g = g_ref[:, cols].astype(F32)
                rg_ref[:, cols] = (g * _sigmoid(g) * nrm).astype(BF16)

    blk = lambda hg, i: (i, hg)
    return _call("ret_fwd", body, (n_pair // RET_PAIRS, s // t),
                 [(log_gamma, None, pltpu.SMEM),
                  (qk_rot, (t, wq), blk),
                  (qk_rot, (t, wq), lambda hg, i: (i, n_pair // RET_PAIRS + hg)),
                  (v_bf, (t, wv), blk),
                  (proj, (t, wv), lambda hg, i: (i, gate_off + hg)),
                  (gn_g, (1, wv), lambda hg, i: (0, hg))],
                 [((s, RET_V), F32, (t, wv), blk), ((s, RET_V), BF16, (t, wv), blk)],
                 scratch=[pltpu.VMEM((RET_PAIRS, LANES, pw), F32)], riders=riders)


def _tri(tk, strict_upper):
    r = lax.broadcasted_iota(jnp.int32, (tk, tk), 0)
    cc = lax.broadcasted_iota(jnp.int32, (tk, tk), 1)
    return ((r > cc) if strict_upper else (r < cc)).astype(BF16)


def _diagonal_step(i, tq, tk, make, carry):
    assert tq == 2 * tk and SB_GROUP == 4
    half = lax.rem(i, 2) == 0
    return lax.cond(half, lambda cr: make(SB_GROUP // 2)(0, cr), lambda cr: make(SB_GROUP)(0, cr), carry)


def _sb_valid(i, j, tq, tk):
    row, col = _tile_pos(i, j, tq, tk)
    return col < row


def _sb_fwd(qkv, tq, tk, riders=None):
    s = qkv.shape[0]
    n_pair = HEADS // 2
    _check_tiles(s, tq, tk, SB_GROUP)

    def body(q_ref, k_ref, v_ref, o_ref, a_ref):
        i = pl.program_id(1)
        upper = _tri(tk, True)
        qs = _stack_heads(q_ref[...])
        n_full, n_groups = _n_full(i, tq, tk, SB_GROUP), _n_groups(i, tq, tk, SB_GROUP)

        def make_step(near_diagonal, last, n_sub=SB_GROUP):
            def step(n, carry):
                c, o = carry
                g = last - 1 - n
                js = [g * SB_GROUP + sub for sub in range(n_sub)]
                zs = [_dot(qs, k_ref[_key_rows(j, tk), :], _NT) for j in js]
                log1ps = [jnp.log2(1.0 + jnp.exp2(-jnp.abs(z))) for z in zs]
                log_1ms = [-jnp.maximum(z, 0.0) - t for z, t in zip(zs, log1ps)]
                log_bs = [jnp.minimum(z, 0.0) - t for z, t in zip(zs, log1ps)]
                if near_diagonal:
                    valids = [_sb_valid(i, j, tq, tk) for j in js]
                    log_1ms = [jnp.where(v, l, 0.0) for v, l in zip(valids, log_1ms)]
                sticks = [_dot(l, upper) for l in log_1ms]
                sums = [jnp.sum(l, axis=1, keepdims=True) for l in log_1ms]
                cs = [None] * n_sub
                for sub in reversed(range(n_sub)):
                    cs[sub] = c
                    c = c + sums[sub]
                for sub, j in enumerate(js):
                    a = jnp.exp2(log_bs[sub] + sticks[sub] + cs[sub])
                    if near_diagonal:
                        a = jnp.where(valids[sub], a, 0.0)
                    a = a.astype(BF16)
                    a_ref[j] = a
                    o = o + _dot(_side_by_side(a, tq), _stack_heads(v_ref[_key_rows(j, tk), :]))
                return c, o
            return step

        carry = (jnp.zeros((2 * tq, 1), F32), jnp.zeros((tq, LANES), F32))
        carry = _diagonal_step(i, tq, tk, lambda n_sub: make_step(True, n_groups, n_sub), carry)
        _, acc = lax.fori_loop(0, n_full, make_step(False, n_full), carry)
        o_ref[...] = acc

    n_kb = s // tk
    return _call("sb_fwd", body, (n_pair, s // tq),
                 [(qkv, (tq, LANES), lambda hp, i: (i, hp)),
                  (qkv, (s, LANES), lambda hp, i: (0, n_pair + hp)),
                  (qkv, (s, LANES), lambda hp, i: (0, 2 * n_pair + hp))],
                 [((s, SB_W), F32, (tq, LANES), lambda hp, i: (i, hp)),
                  ((n_pair, s // tq, n_kb, 2 * tq, tk), BF16, (None, None, n_kb, 2 * tq, tk),
                   lambda hp, i: (hp, i, 0, 0, 0))], riders=riders)


def _merge(retg, sb, w_ret, w_sb_t, proj, tm, tn, riders=None):
    s, d = retg.shape[0], w_ret.shape[1]
    ar_off = (2 * RET_QK + 2 * RET_V + 3 * SB_W) // tn
    as_off = ar_off + d // tn

    def body(rg_ref, sb_ref, wr_ref, ws_ref, ar_ref, as_ref, mix_ref, r_ref, s_ref):
        rr = _dot(rg_ref[...], wr_ref[...])
        ss = _dot(sb_ref[...], ws_ref[...], _NT)
        mix_ref[...] = (_sigmoid(ar_ref[...].astype(F32)) * rr + _sigmoid(as_ref[...].astype(F32)) * ss).astype(BF16)
        r_ref[...] = rr.astype(BF16)
        s_ref[...] = ss.astype(BF16)

    tile = (tm, tn)
    return _call("merge", body, (d // tn, s // tm),
                 [(retg, (tm, RET_V), lambda j, i: (i, 0)), (sb, (tm, SB_W), lambda j, i: (i, 0)),
                  (w_ret, (RET_V, tn), lambda j, i: (0, j)), (w_sb_t, (tn, SB_W), lambda j, i: (j, 0)),
                  (proj, tile, lambda j, i: (i, ar_off + j)), (proj, tile, lambda j, i: (i, as_off + j))],
                 [((s, d), BF16, tile, lambda j, i: (i, j))] * 3, riders=riders)


def _out_proj(mixed, w_out, x, mod, gp1, g2, tm):
    s, d = x.shape

    def body(a_ref, w_ref, x_ref, mod_ref, gp_ref, g2_ref, y_ref, hres_ref, h2_ref):
        for rows in _pieces(tm):
            y = _dot(a_ref[rows, :], w_ref[...])
            y_ref[rows, :] = y
            ny, _ = _rms(y, d)
            hres = x_ref[rows, :] + mod_ref[:, 2 * d:3 * d] * (ny * gp_ref[...])
            hres_ref[rows, :] = hres
            n2, _ = _rms(hres, d)
            h2_ref[rows, :] = (n2 * g2_ref[...] * (1.0 + mod_ref[:, 4 * d:5 * d]) + mod_ref[:, 3 * d:4 * d]).astype(BF16)

    row = lambda i: (i, 0)
    fix = lambda i: (0, 0)
    return _call("out_proj", body, (s // tm,),
                 [(mixed, (tm, d), row), (w_out, (d, d), fix), (x, (tm, d), row),
                  (mod, (1, 6 * d), fix), (gp1, (1, d), fix), (g2, (1, d), fix)],
                 [((s, d), F32, (tm, d), row), ((s, d), F32, (tm, d), row), ((s, d), BF16, (tm, d), row)])


def _ff1(h2, w_ff1_t, tm, tn):
    s, f = h2.shape[0], w_ff1_t.shape[0]
    tm = min(tm, s)

    def body(a_ref, w_ref, u_ref, act_ref):
        u = _dot(a_ref[...], w_ref[...], _NT)
        r = jnp.maximum(u, 0.0)
        u_ref[...] = u.astype(BF16)
        act_ref[...] = (r * r).astype(BF16)

    d = h2.shape[1]
    return _call("ff1", body, (f // tn, s // tm),
                 [(h2, (tm, d), lambda j, i: (i, 0)), (w_ff1_t, (tn, d), lambda j, i: (j, 0))],
                 [((s, f), BF16, (tm, tn), lambda j, i: (i, j))] * 2)


def _ff2_loss(act, w_ff2, hres, target, mod, gp2, tm):
    s, d = hres.shape
    f = act.shape[1]

    def body(a_ref, w_ref, h_ref, t_ref, mod_ref, gp_ref, dout_ref, df_ref, loss_ref, dgt_ref, dgp_ref):
        _zero_at_start([loss_ref, dgt_ref, dgp_ref])
        gt, gp = mod_ref[:, 5 * d:6 * d], gp_ref[...]
        for rows in _pieces(tm):
            ff = _dot(a_ref[rows, :], w_ref[...])
            nf, rf = _rms(ff, d)
            out = h_ref[rows, :] + gt * (nf * gp)
            err = out - t_ref[rows, :]
            sq = jnp.sum(err * err, axis=1, keepdims=True)
            loss_ref[...] += jnp.sum(sq, axis=0, keepdims=True)
            dout = err * (1.0 / d)
            dout_ref[rows, :] = dout
            dgt_ref[...] += _colsum(dout * (nf * gp))
            dgp_ref[...] += _colsum(dout * gt * nf)
            df_ref[rows, :] = _rms_bwd(dout * gt * gp, nf, rf, d).astype(BF16)

    row = lambda i: (i, 0)
    fix = lambda i: (0, 0)
    return _call("ff2_loss", body, (s // tm,),
                 [(act, (tm, f), row), (w_ff2, (f, d), fix), (hres, (tm, d), row), (target, (tm, d), row),
                  (mod, (1, 6 * d), fix), (gp2, (1, d), fix)],
                 [((s, d), F32, (tm, d), row), ((s, d), BF16, (tm, d), row), ((1, 1), F32, (1, 1), fix),
                  ((1, d), F32, (1, d), fix), ((1, d), F32, (1, d), fix)])


def _ff2_bwd(df, w_ff2, u, tm, tn):
    s, d = df.shape
    f = w_ff2.shape[0]
    tm = min(tm, s)

    def body(a_ref, w_ref, u_ref, du_ref):
        da = _dot(a_ref[...], w_ref[...], _NT)
        du_ref[...] = (da * (2.0 * jnp.maximum(u_ref[...].astype(F32), 0.0))).astype(BF16)

    return _call("ff2_bwd", body, (f // tn, s // tm),
                 [(df, (tm, d), lambda j, i: (i, 0)), (w_ff2, (tn, d), lambda j, i: (j, 0)),
                  (u, (tm, tn), lambda j, i: (i, j))],
                 [((s, f), BF16, (tm, tn), lambda j, i: (i, j))])[0]


def _ff1_bwd(du, w_ff1_t, hres, dout, y, mod, g2, gp1, tm, riders=None):
    s, d = hres.shape
    f = du.shape[1]

    def body(a_ref, w_ref, h_ref, do_ref, y_ref, mod_ref, g2_ref, gp_ref,
             dh_ref, dy_ref, dsh_ref, dsc_ref, dg2_ref, dgt_ref, dgp_ref):
        _zero_at_start([dsh_ref, dsc_ref, dg2_ref, dgt_ref, dgp_ref])
        g2, sc2 = g2_ref[...], mod_ref[:, 4 * d:5 * d]
        gt, gp = mod_ref[:, 2 * d:3 * d], gp_ref[...]
        for rows in _pieces(tm):
            dh2 = _dot(a_ref[rows, :], w_ref[...])
            n2, r2 = _rms(h_ref[rows, :], d)
            dsh_ref[...] += _colsum(dh2)
            dsc_ref[...] += _colsum(dh2 * n2 * g2)
            dg2_ref[...] += _colsum(dh2 * n2 * (1.0 + sc2))
            dhres = do_ref[rows, :] + _rms_bwd(dh2 * g2 * (1.0 + sc2), n2, r2, d)
            dh_ref[rows, :] = dhres
            ny, ry = _rms(y_ref[rows, :], d)
            dgt_ref[...] += _colsum(dhres * (ny * gp))
            dgp_ref[...] += _colsum(dhres * gt * ny)
            dy_ref[rows, :] = _rms_bwd(dhres * gt * gp, ny, ry, d).astype(BF16)

    row = lambda i: (i, 0)
    fix = lambda i: (0, 0)
    vec = ((1, d), F32, (1, d), fix)
    return _call("ff1_bwd", body, (s // tm,),
                 [(du, (tm, f), row), (w_ff1_t, (f, d), fix), (hres, (tm, d), row), (dout, (tm, d), row),
                  (y, (tm, d), row), (mod, (1, 6 * d), fix), (g2, (1, d), fix), (gp1, (1, d), fix)],
                 [((s, d), F32, (tm, d), row), ((s, d), BF16, (tm, d), row), vec, vec, vec, vec, vec], riders=riders)


def _out_bwd(dy, w_out, proj, r_bf, s_bf, tm, tn, riders=None):
    s, d = dy.shape
    ar_off = (2 * RET_QK + 2 * RET_V + 3 * SB_W) // tn
    as_off = ar_off + d // tn

    def body(a_ref, w_ref, ar_ref, as_ref, r_ref, s_ref, dr_ref, ds_ref, dar_ref, das_ref):
        dm = _dot(a_ref[...], w_ref[...], _NT)
        sr, ss = _sigmoid(ar_ref[...].astype(F32)), _sigmoid(as_ref[...].astype(F32))
        dr_ref[...] = (dm * sr).astype(BF16)
        ds_ref[...] = (dm * ss).astype(BF16)
        dar_ref[...] = (dm * r_ref[...].astype(F32) * sr * (1.0 - sr)).astype(BF16)
        das_ref[...] = (dm * s_ref[...].astype(F32) * ss * (1.0 - ss)).astype(BF16)

    tile = (tm, tn)
    here = lambda j, i: (i, j)
    return _call("out_bwd", body, (d // tn, s // tm),
                 [(dy, (tm, d), lambda j, i: (i, 0)), (w_out, (tn, d), lambda j, i: (j, 0)),
                  (proj, tile, lambda j, i: (i, ar_off + j)), (proj, tile, lambda j, i: (i, as_off + j)),
                  (r_bf, tile, here), (s_bf, tile, here)],
                 [((s, d), BF16, tile, here)] * 4, riders=riders)


def _gn_bwd(dretg, ret, proj, gn_g, tm, riders=None):
    s = ret.shape[0]
    gate_off = (2 * RET_QK + RET_V) // RET_V

    def body(d_ref, r_ref, g_ref, w_ref, dg_ref, dret_ref, dw_ref):
        first = pl.program_id(0) == 0
        for h in range(HEADS):
            cols = slice(h * RET_DV, (h + 1) * RET_DV)
            o, g, w, dr = r_ref[:, cols], g_ref[:, cols].astype(F32), w_ref[:, cols], d_ref[:, cols].astype(F32)
            mu = jnp.sum(o, axis=1, keepdims=True) * (1.0 / RET_DV)
            xc = o - mu
            rstd = lax.rsqrt(jnp.sum(xc * xc, axis=1, keepdims=True) * (1.0 / RET_DV) + EPS)
            n = xc * rstd
            sg = _sigmoid(g)
            silu = g * sg
            dg_ref[:, cols] = (dr * n * w * (sg * (1.0 + g * (1.0 - sg)))).astype(BF16)
            _accum(dw_ref.at[:, cols], _colsum(dr * silu * n), first)
            dn = dr * silu * w
            m1 = jnp.sum(dn, axis=1, keepdims=True) * (1.0 / RET_DV)
            m2 = jnp.sum(dn * n, axis=1, keepdims=True) * (1.0 / RET_DV)
            dret_ref[:, cols] = (rstd * (dn - m1 - n * m2)).astype(BF16)

    row = lambda i: (i, 0)
    fix = lambda i: (0, 0)
    return _call("gn_bwd", body, (s // tm,),
                 [(dretg, (tm, RET_V), row), (ret, (tm, RET_V), row),
                  (proj, (tm, RET_V), lambda i: (i, gate_off)), (gn_g, (1, RET_V), fix)],
                 [((s, RET_V), BF16, (tm, RET_V), row), ((s, RET_V), BF16, (tm, RET_V), row),
                  ((1, RET_V), F32, (1, RET_V), fix)], riders=riders)


def _ret_bwd(qk_rot, v_bf, dret, log_gamma, t, riders=None):
    s = qk_rot.shape[0]
    n_pair = HEADS // 2
    pw = 2 * RET_DV
    wq, wv = RET_PAIRS * LANES, RET_PAIRS * pw
    n_blk = s // t
    pairs = range(RET_PAIRS)

    def load(q_ref, k_ref, v_ref, do_ref):
        return ([_lanes(q_ref, p, LANES) for p in pairs], [_lanes(k_ref, p, LANES) for p in pairs],
                [_lanes(v_ref, p, pw) for p in pairs], [_lanes(do_ref, p, pw) for p in pairs])

    def d_scores(lg_ref, hp, i, qb, kb, vb, dob):
        z, w = _ret_block(lg_ref, hp, i, t, qb, kb)
        dp = jnp.concatenate([_dot(dob[:, 0:RET_DV], vb[:, 0:RET_DV], _NT),
                              _dot(dob[:, RET_DV:pw], vb[:, RET_DV:pw], _NT)], axis=0)
        return (z * w).astype(BF16), (dp * w).astype(BF16)

    def up_body(lg_ref, q_ref, k_ref, v_ref, do_ref, dq_ref, state_ref):
        hg, i = pl.program_id(0), pl.program_id(1)

        @pl.when(i == 0)
        def _():
            state_ref[...] = jnp.zeros_like(state_ref)

        qbs, kbs, vbs, dobs = load(q_ref, k_ref, v_ref, do_ref)
        dss = [d_scores(lg_ref, hg * RET_PAIRS + p, i, qbs[p], kbs[p], vbs[p], dobs[p])[1] for p in pairs]
        for p in pairs:
            dq_ref[:, p * LANES:(p + 1) * LANES] = (_dot(_side_by_side(dss[p], t), _stack_heads(kbs[p]))
                                                    + _dot(dobs[p], state_ref[p], _NT))
        for p in pairs:
            state_ref[p] += _pair_mask() * _dot(kbs[p], vbs[p], _TN)

    def down_body(lg_ref, q_ref, k_ref, v_ref, do_ref, dk_ref, dv_ref, state_ref):
        hg, i = pl.program_id(0), n_blk - 1 - pl.program_id(1)

        @pl.when(pl.program_id(1) == 0)
        def _():
            state_ref[...] = jnp.zeros_like(state_ref)

        qbs, kbs, vbs, dobs = load(q_ref, k_ref, v_ref, do_ref)
        both = [d_scores(lg_ref, hg * RET_PAIRS + p, i, qbs[p], kbs[p], vbs[p], dobs[p]) for p in pairs]
        for p in pairs:
            pp, ds = both[p]
            later = state_ref[p]
            dv_ref[:, p * pw:(p + 1) * pw] = jnp.concatenate(
                [_dot(pp[:t], dobs[p][:, 0:RET_DV], _TN), _dot(pp[t:], dobs[p][:, RET_DV:pw], _TN)],
                axis=1) + _dot(kbs[p], later)
            dk_ref[:, p * LANES:(p + 1) * LANES] = _dot(ds, _stack_heads(qbs[p]), _TN) + _dot(vbs[p], later, _NT)
        for p in pairs:
            state_ref[p] += _pair_mask() * _dot(qbs[p], dobs[p], _TN)

    n_grp = n_pair // RET_PAIRS

    def ins(order):
        return [(log_gamma, None, pltpu.SMEM),
                (qk_rot, (t, wq), lambda hg, i: (order(i), hg)),
                (qk_rot, (t, wq), lambda hg, i: (order(i), n_grp + hg)),
                (v_bf, (t, wv), lambda hg, i: (order(i), hg)),
                (dret, (t, wv), lambda hg, i: (order(i), hg))]

    up = lambda i: i
    down = lambda i: n_blk - 1 - i
    scratch = [pltpu.VMEM((RET_PAIRS, LANES, pw), F32)]
    dq = _call("ret_bwd_q", up_body, (n_grp, n_blk), ins(up),
               [((s, RET_QK), F32, (t, wq), lambda hg, i: (i, hg))], scratch=scratch)[0]
    dk, dv, *rest = _call("ret_bwd_kv", down_body, (n_grp, n_blk), ins(down),
                          [((s, RET_QK), F32, (t, wq), lambda hg, i: (down(i), hg)),
                           ((s, RET_V), F32, (t, wv), lambda hg, i: (down(i), hg))],
                          scratch=scratch, riders=riders)
    return [dq, dk, dv] + rest


def _sb_bwd(qkv, weights, do, tq, tk, riders=None):
    s = qkv.shape[0]
    n_pair = HEADS // 2
    _check_tiles(s, tq, tk, SB_GROUP)

    def body(q_ref, k_ref, v_ref, a_ref, do_ref, dq_ref, dk_ref, dv_ref):
        i = pl.program_id(1)

        @pl.when(i == 0)
        def _():
            dk_ref[...] = jnp.zeros_like(dk_ref)
            dv_ref[...] = jnp.zeros_like(dv_ref)

        lower = _tri(tk, False)
        qs = _stack_heads(q_ref[...])
        dos = _stack_heads(do_ref[...].astype(BF16))

        def make_step(near_diagonal, n_sub=SB_GROUP):
            def step(g, carry):
                c_e, dq = carry
                js = [g * SB_GROUP + sub for sub in range(n_sub)]
                rows = [_key_rows(j, tk) for j in js]
                zs = [_dot(qs, k_ref[rw, :], _NT) for rw in rows]
                das = [_dot(dos, v_ref[rw, :], _NT) for rw in rows]
                avals = [a_ref[j] for j in js]
                for a, rw in zip(avals, rows):
                    dv_ref[rw, :] += _dot(a, dos, _TN)
                es = [a.astype(F32) * da for a, da in zip(avals, das)]
                prefixes = [_dot(e, lower) for e in es]
                betas = [1.0 / (1.0 + jnp.exp2(-z)) for z in zs]
                for sub in range(n_sub):
                    dz = es[sub] - (es[sub] + prefixes[sub] + c_e) * betas[sub]
                    if near_diagonal:
                        dz = jnp.where(_sb_valid(i, js[sub], tq, tk), dz, 0.0)
                    dz = dz.astype(BF16)
                    dk_ref[rows[sub], :] += _dot(dz, qs, _TN)
                    dq = dq + _dot(_side_by_side(dz, tq), _stack_heads(k_ref[rows[sub], :]))
                    c_e = c_e + jnp.sum(es[sub], axis=1, keepdims=True)
                return c_e, dq
            return step

        n_full = _n_full(i, tq, tk, SB_GROUP)
        carry = (jnp.zeros((2 * tq, 1), F32), jnp.zeros((tq, LANES), F32))
        carry = lax.fori_loop(0, n_full, make_step(False), carry)
        _, dq = _diagonal_step(i, tq, tk, lambda n_sub: (lambda n, cr: make_step(True, n_sub)(n_full, cr)), carry)
        dq_ref[...] = dq

    blk = lambda hp, i: (i, hp)
    n_kb = s // tk
    return _call("sb_bwd", body, (n_pair, s // tq),
                 [(qkv, (tq, LANES), blk),
                  (qkv, (s, LANES), lambda hp, i: (0, n_pair + hp)),
                  (qkv, (s, LANES), lambda hp, i: (0, 2 * n_pair + hp)),
                  (weights, (None, None, n_kb, 2 * tq, tk), lambda hp, i: (hp, i, 0, 0, 0)),
                  (do, (tq, LANES), blk)],
                 [((s, SB_W), F32, (tq, LANES), blk),
                  ((s, SB_W), F32, (s, LANES), lambda hp, i: (0, hp)),
                  ((s, SB_W), F32, (s, LANES), lambda hp, i: (0, hp))], riders=riders)


def _assemble_dproj(dq_r, dk_r, dv_r, dg_r, dq_s, dk_s, dv_s, da_r, da_s, cos, sin, idx_col, lg_lanes, tm):
    s, d = da_r.shape
    width = 2 * RET_QK + 2 * RET_V + 3 * SB_W + 2 * d

    def body(dq_ref, dk_ref, dv_ref, dg_ref, dqs_ref, dks_ref, dvs_ref, dar_ref, das_ref, cos_ref, sin_ref,
             idx_ref, lg_ref, o_ref):
        lane = lax.broadcasted_iota(jnp.int32, (1, LANES), 1)
        first = jnp.bitwise_and(lane, RET_DQK - 1) < (RET_DQK // 2)
        cos, sin = cos_ref[...], sin_ref[...]
        idx = idx_ref[...]
        for src, base, sign, scale in ((dq_ref, 0, 1.0, 1.0), (dk_ref, RET_QK, -1.0, RET_DQK ** -0.5)):
            for g in range(RET_QK // LANES):
                v = src[:, g * LANES:(g + 1) * LANES] * (_decay_scale(lg_ref, idx, g, sign) * scale)
                sw = jnp.where(first, pltpu.roll(v, LANES - RET_DQK // 2, 1), pltpu.roll(v, RET_DQK // 2, 1))
                o_ref[:, base + g * LANES:base + (g + 1) * LANES] = (v * cos - sw * sin).astype(BF16)
        off = 2 * RET_QK
        o_ref[:, off:off + RET_V] = dv_ref[...].astype(BF16)
        off += RET_V
        o_ref[:, off:off + RET_V] = dg_ref[...]
        off += RET_V
        o_ref[:, off:off + SB_W] = (dqs_ref[...] * (SB_DH ** -0.5)).astype(BF16)
        off += SB_W
        o_ref[:, off:off + SB_W] = (dks_ref[...] * LN2).astype(BF16)
        off += SB_W
        o_ref[:, off:off + SB_W] = dvs_ref[...].astype(BF16)
        off += SB_W
        o_ref[:, off:off + d] = dar_ref[...]
        off += d
        o_ref[:, off:off + d] = das_ref[...]

    row = lambda i: (i, 0)
    ins = [(a, (tm, a.shape[1]), row) for a in (dq_r, dk_r, dv_r, dg_r, dq_s, dk_s, dv_s, da_r, da_s, cos, sin, idx_col)]
    ins.append((lg_lanes, (1, RET_QK), lambda i: (0, 0)))
    return _call("assemble_dproj", body, (s // tm,), ins, [((s, width), BF16, (tm, width), row)])[0]


def _in_bwd(dproj, w_in_t, x, dhres, mod, g1, tm, riders=None):
    s, d = x.shape
    width = dproj.shape[1]

    def body(a_ref, w_ref, x_ref, dh_ref, mod_ref, g_ref, dx_ref, dsh_ref, dsc_ref, dg_ref):
        _zero_at_start([dsh_ref, dsc_ref, dg_ref])
        g1, sc1 = g_ref[...], mod_ref[:, d:2 * d]
        for rows in _pieces(tm):
            dh = _dot(a_ref[rows, :], w_ref[...])
            n1, r1 = _rms(x_ref[rows, :], d)
            dsh_ref[...] += _colsum(dh)
            dsc_ref[...] += _colsum(dh * n1 * g1)
            dg_ref[...] += _colsum(dh * n1 * (1.0 + sc1))
            dx_ref[rows, :] = dh_ref[rows, :] + _rms_bwd(dh * g1 * (1.0 + sc1), n1, r1, d)

    row = lambda i: (i, 0)
    fix = lambda i: (0, 0)
    vec = ((1, d), F32, (1, d), fix)
    return _call("in_bwd", body, (s // tm,),
                 [(dproj, (tm, width), row), (w_in_t, (width, d), fix), (x, (tm, d), row), (dhres, (tm, d), row),
                  (mod, (1, 6 * d), fix), (g1, (1, d), fix)],
                 [((s, d), F32, (tm, d), row), vec, vec, vec], riders=riders)


def _adamw(w, g, m, v):
    m = ADAM_B1 * m + (1.0 - ADAM_B1) * g
    v = ADAM_B2 * v + (1.0 - ADAM_B2) * (g * g)
    m_hat = m / (1.0 - ADAM_B1 ** ADAM_STEP)
    v_hat = v / (1.0 - ADAM_B2 ** ADAM_STEP)
    delta = -ADAM_LR * (m_hat / (jnp.sqrt(v_hat) + ADAM_EPS) + ADAM_WD * w)
    return delta, m, v


def _adam_reduce(name, parts, w, m, v, tr):
    rws, cls = w.shape
    tr = min(tr, rws)
    n_parts = parts.shape[0]

    def body(p_ref, w_ref, m_ref, v_ref, g_out, d_out, m_out, v_out):
        g = p_ref[0].astype(F32)
        for k in range(1, n_parts):
            g = g + p_ref[k].astype(F32)
        delta, mn, vn = _adamw(w_ref[...], g, m_ref[...], v_ref[...])
        g_out[...] = g
        d_out[...] = delta
        m_out[...] = mn
        v_out[...] = vn

    row = lambda i: (i, 0)
    blk = (tr, cls)
    return _call(name, body, (rws // tr,),
                 [(parts, (n_parts, tr, cls), lambda i: (0, i, 0)), (w, blk, row), (m, blk, row), (v, blk, row)],
                 [((rws, cls), F32, blk, row)] * 4)


def _ada_bwd_adam(cs_t, dmod_cols, w, m, v):
    d, nc = w.shape

    def body(c_ref, dm_ref, w_ref, m_ref, v_ref, g_out, d_out, m_out, v_out):
        g = c_ref[0] * dm_ref[0:1, :]
        for r in range(1, N_DEV):
            g = g + c_ref[r] * dm_ref[r:r + 1, :]
        delta, mn, vn = _adamw(w_ref[...], g, m_ref[...], v_ref[...])
        g_out[...] = g
        d_out[...] = delta
        m_out[...] = mn
        v_out[...] = vn

    fix = lambda i: (0, 0)
    blk = (d, nc)
    return _call("ada_bwd_adam", body, (1,),
                 [(cs_t, (N_DEV, d, 1), lambda i: (0, 0, 0)), (dmod_cols, (N_DEV, nc), fix), (w, blk, fix), (m, blk, fix), (v, blk, fix)],
                 [((d, nc), F32, blk, fix)] * 4)


def _small_adam(parts, w, m, v):
    n = w.shape[1]

    def body(p_ref, w_ref, m_ref, v_ref, g_out, d_out, m_out, v_out):
        g = p_ref[0:1, :]
        for k in range(1, N_DEV):
            g = g + p_ref[k:k + 1, :]
        delta, mn, vn = _adamw(w_ref[...], g, m_ref[...], v_ref[...])
        g_out[...] = g
        d_out[...] = delta
        m_out[...] = mn
        v_out[...] = vn

    fix = lambda i: (0, 0)
    return _call("small_adam", body, (1,),
                 [(parts, (N_DEV, n), fix), (w, (1, n), fix), (m, (1, n), fix), (v, (1, n), fix)],
                 [((1, n), F32, (1, n), fix)] * 4)


def kernel(x, c, positions, ada_w, ada_b, pre_mix_g, post_mix_g, pre_ffn_g, post_ffn_g, w_in, ret_gn_g, w_ret_branch, w_sb_branch, w_out, w_ff1, w_ff2, loss_target, m_ada_w, m_ada_b, m_pre_mix_g, m_post_mix_g, m_pre_ffn_g, m_post_ffn_g, m_w_in, m_ret_gn_g, m_w_ret_branch, m_w_sb_branch, m_w_out, m_w_ff1, m_w_ff2, v_ada_w, v_ada_b, v_pre_mix_g, v_post_mix_g, v_pre_ffn_g, v_post_ffn_g, v_w_in, v_ret_gn_g, v_w_ret_branch, v_w_sb_branch, v_w_out, v_w_ff1, v_w_ff2):
    _, s, d = x.shape
    d_ff = w_ff1.shape[2] * N_DEV
    d_in = w_in.shape[2] * N_DEV
    me = 4 * lax.axis_index("x") + 2 * lax.axis_index("y") + lax.axis_index("c")
    x2, tgt = x[0], loss_target[0]

    core = lax.axis_index("c").astype(jnp.int32).reshape(1)
    bf = lambda w: w[0].astype(BF16)

    w_in_t, m_in_t, v_in_t = (jnp.swapaxes(a[0], 0, 1) for a in (w_in, m_w_in, v_w_in))

    c_all, g_in = _exchange("gather_in", [c, w_in_t.astype(BF16)], ["gather", "gather_chip"])
    c_all = c_all.reshape(N_DEV, d)

    n_ada = ada_w.shape[2]
    cs_all = _silu_rows(c_all)
    ada_b_cols = lax.dynamic_slice(ada_b, (0, me * n_ada), (1, n_ada))
    mod_cols = _ada_fwd(cs_all, ada_w[0], ada_b_cols)
    mod_all = _exchange("gather_mod", [mod_cols], ["gather"])[0]
    mod = lax.dynamic_index_in_dim(mod_all, me, axis=1, keepdims=False).reshape(1, 6 * d)

    tm = min(256, s)
    h, g_in = _pre_norm(x2, pre_mix_g, mod, tm, riders=([g_in], ["forward"]))
    wt_in = g_in.reshape(d_in, d)
    bf_t = lambda w: jnp.swapaxes(w[0], 0, 1).astype(BF16)
    small_w = [bf(w_ret_branch), bf_t(w_sb_branch), bf(w_out)]
    proj, *small_w = _matmul("in_proj", h, wt_in, "nt", s, 512, BF16, riders=(small_w, ["gather_chip"] * 3))
    pos_col = positions.reshape(s, 1).astype(F32)
    freqs = ROPE_BASE ** (-jnp.arange(0, RET_DQK, 2, dtype=F32) / RET_DQK)
    inv_freq = jnp.tile(freqs, LANES // (RET_DQK // 2)).reshape(1, LANES)
    log_gamma_np = np.log1p(-(2.0 ** (-5.0 - np.arange(HEADS))))
    log_gamma = jnp.asarray(log_gamma_np, F32)
    lg_lanes = jnp.asarray(np.repeat(log_gamma_np, RET_DQK).reshape(1, RET_QK), F32)
    idx_col = (jnp.arange(s, dtype=F32) - (s // 2)).reshape(s, 1)
    qk_rot, v_bf, qkv_sb, cos_t, sin_t = _prep(proj, pos_col, idx_col, inv_freq, lg_lanes, tm)
    tq, tk = min(256, s), min(128, s)
    tq_sb = min(SB_TQ, s)
    sb, sb_weights, *big_w = _sb_fwd(qkv_sb, tq_sb, tk, riders=([bf(w_ff2), bf_t(w_ff1)], ["gather_chip"] * 2))
    ret, retg, g_ret, g_sb, g_out, g_ff2, g_ff1 = _ret_fwd(qk_rot, v_bf, proj, ret_gn_g, log_gamma, tq,
                                                           riders=(small_w + big_w, ["forward"] * 5))
    wf_ret = g_ret.reshape(RET_V, d)
    wt_sb = g_sb.reshape(d, SB_W)
    wf_out = g_out.reshape(d, d)
    wt_ff1 = g_ff1.reshape(d_ff, d)
    wf_ff2 = g_ff2.reshape(d_ff, d)
    mixed, r_bf, s_bf = _merge(retg, sb, wf_ret, wt_sb, proj, tm, min(512, d))
    y, hres, h2 = _out_proj(mixed, wf_out, x2, mod, post_mix_g, pre_ffn_g, tm)
    u, act = _ff1(h2, wt_ff1, s, 512)
    dout, df, loss_sum, d_gt2, d_gp2 = _ff2_loss(act, wf_ff2, hres, tgt, mod, post_ffn_g, tm)

    du = _ff2_bwd(df, wf_ff2, u, s, 512)
    gw_ff2 = _matmul("grad_w_ff2", act, df, "tn", 512, d, BF16).reshape(N_DEV, d_ff // N_DEV, d)
    gw_ff1 = _matmul("grad_w_ff1", h2, du, "tn", d, d_ff // N_DEV, BF16, blocked_out=True)
    dhres, dy, d_sh2, d_sc2, d_g2, d_gt1, d_gp1, t_ff1, t_ff2 = _ff1_bwd(
        du, wt_ff1, hres, dout, y, mod, pre_ffn_g, post_mix_g, tm, riders=([gw_ff1, gw_ff2], ["pair"] * 2))
    s_ff1 = _pair_sum("pair_sum_ff1", gw_ff1, t_ff1, core, 256)
    s_ff2 = _pair_sum("pair_sum_ff2", gw_ff2, t_ff2, core, 256)
    gw_out = _matmul("grad_w_out", mixed, dy, "tn", 512, d, BF16).reshape(N_DEV, d // N_DEV, d)
    d_r, d_s, da_r, da_s, p_out = _out_bwd(dy, wf_out, proj, r_bf, s_bf, tm, min(512, d), riders=([gw_out], ["scatter"]))
    dretg = _matmul("ret_branch_bwd", d_r, wf_ret, "nt", s, 512, BF16)
    dsb = _matmul("sb_branch_bwd", d_s, wt_sb, "nn", s, 512, F32)
    gw_ret = _matmul("grad_w_ret", retg, d_r, "tn", 512, d, BF16).reshape(N_DEV, RET_V // N_DEV, d)
    gw_sb = _matmul("grad_w_sb", sb, d_s, "tn", 512, d // N_DEV, BF16, blocked_out=True)
    dq_s, dk_s, dv_s, p_ff1, p_ff2, p_ret = _sb_bwd(qkv_sb, sb_weights, dsb, tq_sb, tk,
                                                    riders=([s_ff1, s_ff2, gw_ret], ["chip_scatter"] * 2 + ["scatter"]))
    dg_r, dret, d_gn = _gn_bwd(dretg, ret, proj, ret_gn_g, tm)
    dq_r, dk_r, dv_r, p_sb = _ret_bwd(qk_rot, v_bf, dret, log_gamma, tq, riders=([gw_sb], ["scatter"]))
    dproj = _assemble_dproj(dq_r, dk_r, dv_r, dg_r, dq_s, dk_s, dv_s, da_r, da_s, cos_t, sin_t, idx_col, lg_lanes, tm)
    gw_in = _matmul("grad_w_in", dproj, h, "tn", 512, d, BF16).reshape(N_DEV, d_in // N_DEV, d)
    t_in = _exchange("pair_in", [gw_in], ["pair"])[0]
    tr_in = d_in // N_DEV // 4
    s_in = _pair_sum("pair_sum_in", gw_in, t_in, core, tr_in)
    grad_x, d_sh1, d_sc1, d_g1, p_in = _in_bwd(dproj, wt_in, x2, dhres, mod, pre_mix_g, tm,
                                               riders=([s_in], ["chip_scatter"]))
    loss_lanes = jnp.pad(loss_sum, ((0, 0), (0, LANES - 1)))
    small = jnp.concatenate([d_sh1, d_sc1, d_gt1, d_sh2, d_sc2, d_gt2, d_g1, d_gp1, d_g2, d_gp2, d_gn, loss_lanes], axis=1)
    small_all = _exchange("gather_small", [small], ["gather"])[0].reshape(N_DEV, small.shape[1])
    parts = [p_in, p_ret, p_sb, p_out, p_ff1, p_ff2]

    res = {}
    names = ["w_ret_branch", "w_sb_branch", "w_out", "w_ff1", "w_ff2"]
    ws = [w_ret_branch, w_sb_branch, w_out, w_ff1, w_ff2]
    ms = [m_w_ret_branch, m_w_sb_branch, m_w_out, m_w_ff1, m_w_ff2]
    vs = [v_w_ret_branch, v_w_sb_branch, v_w_out, v_w_ff1, v_w_ff2]
    for nm, p, w, m, v in zip(names, parts[1:], ws, ms, vs):
        res[nm] = [o[None] for o in _adam_reduce("adam_" + nm, p, w[0], m[0], v[0], 256)]
    res["w_in"] = [jnp.swapaxes(o, 0, 1)[None]
                   for o in _adam_reduce("adam_w_in", parts[0], w_in_t, m_in_t, v_in_t, tr_in)]
    dmod_cols = lax.dynamic_slice(small_all, (0, me * n_ada), (N_DEV, n_ada))
    res["ada_w"] = [o[None] for o in _ada_bwd_adam(cs_all.reshape(N_DEV, d, 1), dmod_cols, ada_w[0], m_ada_w[0], v_ada_w[0])]
    vec_names = ["ada_b", "pre_mix_g", "post_mix_g", "pre_ffn_g", "post_ffn_g", "ret_gn_g"]
    cat = lambda xs: jnp.concatenate(xs + [jnp.zeros((1, LANES), F32)], axis=1)
    packed = _small_adam(small_all,
                         cat([ada_b, pre_mix_g, post_mix_g, pre_ffn_g, post_ffn_g, ret_gn_g]),
                         cat([m_ada_b, m_pre_mix_g, m_post_mix_g, m_pre_ffn_g, m_post_ffn_g, m_ret_gn_g]),
                         cat([v_ada_b, v_pre_mix_g, v_post_mix_g, v_pre_ffn_g, v_post_ffn_g, v_ret_gn_g]))
    off = 0
    for nm, width in zip(vec_names, [6 * d, d, d, d, d, RET_V]):
        res[nm] = [p[:, off:off + width] for p in packed]
        off += width

    loss = (0.5 / d) * packed[0][0, off]
    order = ["ada_w", "ada_b", "pre_mix_g", "post_mix_g", "pre_ffn_g", "post_ffn_g", "w_in", "ret_gn_g",
             "w_ret_branch", "w_sb_branch", "w_out", "w_ff1", "w_ff2"]
    outs = [loss, grad_x[None]]
    for k in range(4):
        outs += [res[nm][k] for nm in order]
    return tuple(outs)
```

```python
import functools

import numpy as np
import jax
import jax.numpy as jnp
from jax import lax
from jax.experimental import pallas as pl
from jax.experimental.pallas import tpu as pltpu

F32 = jnp.float32
BF16 = jnp.bfloat16
N_DEV = 8
AXES = ("x", "y", "c")

EPS = 1e-6
CHUNK = 64
CHUNK_SHIFT = 6
HEADS = 8
RET_DQK = 64
RET_DV = 128
SB_DH = 64
RET_QK = HEADS * RET_DQK
RET_V = HEADS * RET_DV
SB_W = HEADS * SB_DH
ROPE_BASE = 10000.0
LANES = 128

ADAM_LR = 0.001
ADAM_B1 = 0.9
ADAM_B2 = 0.999
ADAM_EPS = 1e-08
ADAM_WD = 0.01
ADAM_STEP = 10

VMEM_LIMIT = 56 * 1024 * 1024

_NN = (((1,), (0,)), ((), ()))
_NT = (((1,), (1,)), ((), ()))
_TN = (((0,), (0,)), ((), ()))


def _dot(a, b, dims=_NN):
    if a.dtype != BF16:
        a = a.astype(BF16)
    if b.dtype != BF16:
        b = b.astype(BF16)
    return lax.dot_general(a, b, dims, preferred_element_type=F32)


def _dot_split(a, b):
    hi = a.astype(BF16)
    lo = (a - hi.astype(F32)).astype(BF16)
    return (lax.dot_general(hi, b, _NN, preferred_element_type=F32)
            + lax.dot_general(lo, b, _NN, preferred_element_type=F32))


def _sigmoid(x):
    return 1.0 / (1.0 + jnp.exp(-x))


def _rms(x, d):
    r = lax.rsqrt(jnp.sum(x * x, axis=1, keepdims=True) * (1.0 / d) + EPS)
    return x * r, r


def _rms_bwd(dn, n, r, d):
    return r * (dn - n * (jnp.sum(dn * n, axis=1, keepdims=True) * (1.0 / d)))


def _colsum(v):
    return jnp.sum(v, axis=0, keepdims=True)


def _accum(ref, val, first):
    @pl.when(first)
    def _():
        ref[...] = val

    @pl.when(jnp.logical_not(first))
    def _():
        ref[...] += val


ROW_SPLIT = 2


def _zero_at_start(refs):
    @pl.when(pl.program_id(0) == 0)
    def _():
        for r in refs:
            r[...] = jnp.zeros_like(r)


def _pieces(tm):
    step = tm // ROW_SPLIT
    return [slice(k * step, (k + 1) * step) for k in range(ROW_SPLIT)]


KIND_SLOTS = {"gather": N_DEV, "scatter": N_DEV, "gather_chip": N_DEV, "forward": N_DEV, "pair": N_DEV // 2,
              "chip_scatter": N_DEV // 2}
SEMS_PER_ARRAY = N_DEV - 1


def _exchange_copies(ins, outs, send_sems, recv_sems, local_sems, kinds):
    x, y, c = (lax.axis_index(a) for a in AXES)
    me, chip, sibling = 4 * x + 2 * y + c, 2 * x + y, (x, y, 1 - c)
    mesh_id = pl.DeviceIdType.MESH
    other_chips = []
    for k in range(1, N_DEV // 2):
        px = 1 - x if k & 2 else x
        py = 1 - y if k & 1 else y
        other_chips.append((px, py))
    copies = []
    for i, kind in enumerate(kinds):
        def remote(src, dst, k, to, i=i):
            return pltpu.make_async_remote_copy(
                src_ref=src, dst_ref=dst, send_sem=send_sems.at[i * SEMS_PER_ARRAY + k],
                recv_sem=recv_sems.at[i * SEMS_PER_ARRAY + k], device_id=to, device_id_type=mesh_id)

        if kind in ("gather", "scatter"):
            pick = (lambda ref, d: ref.at[d]) if kind == "scatter" else (lambda ref, d: ref)
            copies.append(pltpu.make_async_copy(pick(ins[i], me), outs[i].at[me], local_sems.at[i]))
            for k in range(1, N_DEV):
                to = (1 - x if k & 4 else x, 1 - y if k & 2 else y, 1 - c if k & 1 else c)
                copies.append(remote(pick(ins[i], 4 * to[0] + 2 * to[1] + to[2]), outs[i].at[me], k - 1, to))
        elif kind == "gather_chip":
            copies.append(pltpu.make_async_copy(ins[i], outs[i].at[me], local_sems.at[i]))
            copies.append(remote(ins[i], outs[i].at[me], 0, sibling))
            for k, (px, py) in enumerate(other_chips):
                copies.append(remote(ins[i], outs[i].at[me], 1 + k, (px, py, c)))
        elif kind == "forward":
            for k, (px, py) in enumerate(other_chips):
                slot = 4 * px + 2 * py + c
                copies.append(remote(outs[i].at[slot], outs[i].at[slot], k, sibling))
        elif kind == "pair":
            for k in range(N_DEV // 2):
                copies.append(remote(ins[i].at[2 * k + 1 - c], outs[i].at[k], k, sibling))
        elif kind == "chip_scatter":
            copies.append(pltpu.make_async_copy(ins[i].at[chip], outs[i].at[chip], local_sems.at[i]))
            for k, (px, py) in enumerate(other_chips):
                copies.append(remote(ins[i].at[2 * px + py], outs[i].at[chip], k, (px, py, c)))
        else:
            raise ValueError(kind)
    return copies


def _exchange_shapes(arrays, kinds):
    shapes = []
    for a, kind in zip(arrays, kinds):
        tail = a.shape if kind in ("gather", "gather_chip") else a.shape[1:]
        shapes.append(jax.ShapeDtypeStruct((KIND_SLOTS[kind],) + tuple(tail), a.dtype))
    return shapes


def _exchange_sems(n):
    return [pltpu.SemaphoreType.DMA((n * SEMS_PER_ARRAY,)), pltpu.SemaphoreType.DMA((n * SEMS_PER_ARRAY,)),
            pltpu.SemaphoreType.DMA((n,))]


def _call(name, body, grid, ins, outs, scratch=(), riders=None, prefetch=None):
    any_spec = pl.BlockSpec(memory_space=pl.ANY)
    in_specs = [pl.BlockSpec(memory_space=im) if bs is None else pl.BlockSpec(bs, im) for _, bs, im in ins]
    out_specs = [pl.BlockSpec(bs, im) for _, _, bs, im in outs]
    out_shape = [jax.ShapeDtypeStruct(s, d) for s, d, _, _ in outs]
    operands = [a for a, _, _ in ins]
    scratch = list(scratch)
    aliases = {}
    n_pre = 0 if prefetch is None else 1
    kernel = functools.partial(body) if prefetch is None else (lambda _, *refs: body(*refs))
    if riders is not None:
        arrays, kinds = riders
        nr, n_in, n_out, n_scr = len(arrays), len(ins), len(outs), len(scratch)

        def kernel(*refs):
            refs = refs[n_pre:]
            own_in, ride_in = refs[:n_in], refs[n_in:n_in + nr]
            own_out = refs[n_in + nr:n_in + nr + n_out]
            ride_out = refs[n_in + nr + n_out:n_in + 2 * nr + n_out]
            own_scr = refs[n_in + 2 * nr + n_out:n_in + 2 * nr + n_out + n_scr]
            sems = refs[n_in + 2 * nr + n_out + n_scr:]
            ids = [pl.program_id(a) for a in range(len(grid))]
            first = functools.reduce(jnp.logical_and, [i == 0 for i in ids])
            last = functools.reduce(jnp.logical_and, [i == g - 1 for i, g in zip(ids, grid)])

            @pl.when(first)
            def _():
                for cp in _exchange_copies(ride_in, ride_out, *sems, kinds):
                    cp.start()

            body(*own_in, *own_out, *own_scr)

            @pl.when(last)
            def _():
                for cp in _exchange_copies(ride_in, ride_out, *sems, kinds):
                    cp.wait()

        in_specs += [any_spec] * nr
        out_specs += [any_spec] * nr
        out_shape += _exchange_shapes(arrays, kinds)
        operands += list(arrays)
        scratch += _exchange_sems(nr)
        aliases = {n_pre + n_in + r: n_out + r for r, kind in enumerate(kinds) if kind == "forward"}
    params = pltpu.CompilerParams(dimension_semantics=("arbitrary",) * len(grid), vmem_limit_bytes=VMEM_LIMIT)
    if prefetch is None:
        return pl.pallas_call(kernel, name=name, grid=grid, in_specs=in_specs, out_specs=out_specs,
                              out_shape=out_shape, scratch_shapes=scratch, input_output_aliases=aliases,
                              compiler_params=params)(*operands)
    grid_spec = pltpu.PrefetchScalarGridSpec(num_scalar_prefetch=1, grid=grid, in_specs=in_specs,
                                             out_specs=out_specs, scratch_shapes=scratch)
    return pl.pallas_call(kernel, name=name, grid_spec=grid_spec, out_shape=out_shape,
                          input_output_aliases=aliases, compiler_params=params)(prefetch, *operands)


def _exchange(name, arrays, kinds):
    n = len(arrays)

    def body(*refs):
        copies = _exchange_copies(refs[:n], refs[n:2 * n], *refs[2 * n:], kinds)
        for cp in copies:
            cp.start()
        for cp in copies:
            cp.wait()

    any_spec = pl.BlockSpec(memory_space=pl.ANY)
    return pl.pallas_call(
        functools.partial(body),
        name=name,
        in_specs=[any_spec] * n,
        out_specs=[any_spec] * n,
        out_shape=_exchange_shapes(arrays, kinds),
        scratch_shapes=_exchange_sems(n),
        input_output_aliases={i: i for i, kind in enumerate(kinds) if kind == "forward"},
    )(*arrays)


def _pair_sum(name, mine, theirs, my_core, tr):
    _, rws, cls = mine.shape
    tr = min(tr, rws)

    def body(a_ref, b_ref, o_ref):
        o_ref[...] = (a_ref[...].astype(F32) + b_ref[...].astype(F32)).astype(o_ref.dtype)

    return _call(name, body, (N_DEV // 2, rws // tr),
                 [(mine, (None, tr, cls), lambda k, r, core: (2 * k + core[0], r, 0)),
                  (theirs, (None, tr, cls), lambda k, r, core: (k, r, 0))],
                 [((N_DEV // 2, rws, cls), mine.dtype, (None, tr, cls), lambda k, r, core: (k, r, 0))],
                 prefetch=my_core)[0]


def _matmul(name, a, b, kind, tm, tn, out_dtype, blocked_out=False, riders=None):
    if kind == "tn":
        kdim, m = a.shape
    else:
        m, kdim = a.shape
    n = b.shape[0] if kind == "nt" else b.shape[1]
    tm, tn = min(tm, m), min(tn, n)
    dims = {"nn": _NN, "nt": _NT, "tn": _TN}[kind]

    def body(a_ref, b_ref, o_ref):
        o_ref[...] = _dot(a_ref[...], b_ref[...], dims).astype(o_ref.dtype)

    a_spec = (a, (kdim, tm), lambda j, i: (0, i)) if kind == "tn" else (a, (tm, kdim), lambda j, i: (i, 0))
    b_spec = (b, (tn, kdim), lambda j, i: (j, 0)) if kind == "nt" else (b, (kdim, tn), lambda j, i: (0, j))
    if blocked_out:
        out = ((n // tn, m, tn), out_dtype, (None, tm, tn), lambda j, i: (j, i, 0))
    else:
        out = ((m, n), out_dtype, (tm, tn), lambda j, i: (i, j))
    res = _call(name, body, (n // tn, m // tm), [a_spec, b_spec], [out], riders=riders)
    return res[0] if riders is None else res


def _ada_fwd(cs_all, ada_w, ada_b_cols):
    def body(c_ref, w_ref, b_ref, o_ref):
        o_ref[...] = lax.dot_general(c_ref[...], w_ref[...], _NN, preferred_element_type=F32,
                                     precision=lax.Precision.HIGHEST) + b_ref[...]

    r, d = cs_all.shape
    nc = ada_w.shape[1]
    return _call("ada_fwd", body, (1,),
                 [(cs_all, (r, d), lambda i: (0, 0)), (ada_w, (d, nc), lambda i: (0, 0)),
                  (ada_b_cols, (1, nc), lambda i: (0, 0))],
                 [((r, nc), F32, (r, nc), lambda i: (0, 0))])[0]


def _silu_rows(c_all):
    def body(c_ref, o_ref):
        v = c_ref[...]
        o_ref[...] = v * _sigmoid(v)

    return _call("silu_c", body, (1,), [(c_all, c_all.shape, lambda i: (0, 0))],
                 [(c_all.shape, F32, c_all.shape, lambda i: (0, 0))])[0]


def _pre_norm(x, g, mod, tm, riders=None):
    s, d = x.shape

    def body(x_ref, g_ref, mod_ref, h_ref):
        n, _ = _rms(x_ref[...], d)
        sh, sc = mod_ref[:, 0:d], mod_ref[:, d:2 * d]
        h_ref[...] = (n * g_ref[...] * (1.0 + sc) + sh).astype(BF16)

    return _call("pre_norm", body, (s // tm,),
                 [(x, (tm, d), lambda i: (i, 0)), (g, (1, d), lambda i: (0, 0)),
                  (mod, (1, 6 * d), lambda i: (0, 0))],
                 [((s, d), BF16, (tm, d), lambda i: (i, 0))], riders=riders)


LOG2E = 1.4426950408889634
LN2 = 0.6931471805599453


def _decay_scale(lg_ref, idx, g, sign):
    return jnp.exp((sign * idx) * lg_ref[:, g * LANES:(g + 1) * LANES])


def _prep(proj, pos_col, idx_col, inv_freq, lg_lanes, tm):
    s = proj.shape[0]
    sb_off = (2 * RET_QK + 2 * RET_V) // (3 * SB_W)
    n_q = RET_QK // LANES

    def body(qk_ref, v_ref, sb_ref, pos_ref, idx_ref, f_ref, lg_ref, qk_out, v_out, sb_out, cos_out, sin_out):
        ang = pos_ref[...] * f_ref[...]
        lane = lax.broadcasted_iota(jnp.int32, (1, LANES), 1)
        first = jnp.bitwise_and(lane, RET_DQK - 1) < (RET_DQK // 2)
        cos = jnp.cos(ang)
        sin = jnp.where(first, -1.0, 1.0) * jnp.sin(ang)
        cos_out[...] = cos
        sin_out[...] = sin
        idx = idx_ref[...]
        for g in range(2 * n_q):
            v = qk_ref[:, g * LANES:(g + 1) * LANES].astype(F32)
            sw = jnp.where(first, pltpu.roll(v, LANES - RET_DQK // 2, 1), pltpu.roll(v, RET_DQK // 2, 1))
            r = v * cos + sw * sin
            if g < n_q:
                r = r * _decay_scale(lg_ref, idx, g, 1.0)
            else:
                r = r * (_decay_scale(lg_ref, idx, g - n_q, -1.0) * (RET_DQK ** -0.5))
            qk_out[:, g * LANES:(g + 1) * LANES] = r.astype(BF16)
        v_out[...] = v_ref[...].astype(BF16)
        sb_out[:, 0:SB_W] = (sb_ref[:, 0:SB_W].astype(F32) * (SB_DH ** -0.5 * LOG2E)).astype(BF16)
        sb_out[:, SB_W:3 * SB_W] = sb_ref[:, SB_W:3 * SB_W].astype(BF16)

    return _call("prep", body, (s // tm,),
                 [(proj, (tm, 2 * RET_QK), lambda i: (i, 0)),
                  (proj, (tm, RET_V), lambda i: (i, 2 * RET_QK // RET_V)),
                  (proj, (tm, 3 * SB_W), lambda i: (i, sb_off)),
                  (pos_col, (tm, 1), lambda i: (i, 0)),
                  (idx_col, (tm, 1), lambda i: (i, 0)),
                  (inv_freq, (1, LANES), lambda i: (0, 0)),
                  (lg_lanes, (1, RET_QK), lambda i: (0, 0))],
                 [((s, 2 * RET_QK), BF16, (tm, 2 * RET_QK), lambda i: (i, 0)),
                  ((s, RET_V), BF16, (tm, RET_V), lambda i: (i, 0)),
                  ((s, 3 * SB_W), BF16, (tm, 3 * SB_W), lambda i: (i, 0)),
                  ((s, LANES), F32, (tm, LANES), lambda i: (i, 0)),
                  ((s, LANES), F32, (tm, LANES), lambda i: (i, 0))])


def _head_mask(hh):
    lane = lax.broadcasted_iota(jnp.int32, (1, LANES), 1)
    return (lane >= RET_DQK) if hh else (lane < RET_DQK)


def _masked(v, m):
    return jnp.where(m, v, jnp.zeros_like(v))


SB_GROUP = 4
SB_TQ = 256
RET_GROUP = 4


def _stack_heads(v):
    return jnp.concatenate([_masked(v, _head_mask(0)), _masked(v, _head_mask(1))], axis=0)


def _side_by_side(v, t):
    return jnp.concatenate([v[:t], v[t:]], axis=1)


def _tile_pos(i, j, tq, tk):
    row = jnp.bitwise_and(lax.broadcasted_iota(jnp.int32, (2 * tq, tk), 0), tq - 1) + i * tq
    col = lax.broadcasted_iota(jnp.int32, (2 * tq, tk), 1) + j * tk
    return row, col


def _n_groups(i, tq, tk, grp):
    return ((i + 1) * (tq // tk) + grp - 1) // grp


def _n_full(i, tq, tk, grp):
    return (i * (tq // tk)) // grp


def _key_rows(j, tk):
    return pl.ds(pl.multiple_of(j * tk, tk), tk)


def _ret_weight(lg_rows, i, j, tq, tk):
    row, col = _tile_pos(i, j, tq, tk)
    same = jnp.right_shift(col, CHUNK_SHIFT) == jnp.right_shift(row, CHUNK_SHIFT)
    later = jnp.where(same, jnp.exp((2.0 * lg_rows) * (col - row).astype(F32)), 0.0)
    return jnp.where(col <= row, 1.0, later)


def _lg_rows(lg_ref, hp, tq):
    first = lax.broadcasted_iota(jnp.int32, (2 * tq, 1), 0) < tq
    return jnp.where(first, lg_ref[2 * hp], lg_ref[2 * hp + 1])


def _check_tiles(s, tq, tk, grp):
    assert tq % tk == 0 and tq & (tq - 1) == 0 and tk & (tk - 1) == 0
    assert s % tq == 0 and (s // tk) % grp == 0 and s // tk <= LANES


def _pair_mask():
    r = lax.broadcasted_iota(jnp.int32, (LANES, 2 * RET_DV), 0) >= RET_DQK
    c = lax.broadcasted_iota(jnp.int32, (LANES, 2 * RET_DV), 1) >= RET_DV
    return (r == c).astype(F32)


def _ret_block(lg_ref, hp, i, t, qb, kb):
    w = _ret_weight(_lg_rows(lg_ref, hp, t), i, i, t, t)
    return _dot(_stack_heads(qb), kb, _NT), w


RET_PAIRS = 2


def _lanes(ref, p, width):
    return ref[:, p * width:(p + 1) * width]


def _ret_fwd(qk_rot, v_bf, proj, gn_g, log_gamma, t, riders=None):
    s = qk_rot.shape[0]
    n_pair = HEADS // 2
    pw = 2 * RET_DV
    wq, wv = RET_PAIRS * LANES, RET_PAIRS * pw
    gate_off = (2 * RET_QK + RET_V) // wv
    assert s % t == 0 and t % CHUNK == 0 and t & (t - 1) == 0 and n_pair % RET_PAIRS == 0

    def body(lg_ref, q_ref, k_ref, v_ref, g_ref, w_ref, ret_ref, rg_ref, state_ref):
        hg, i = pl.program_id(0), pl.program_id(1)

        @pl.when(i == 0)
        def _():
            state_ref[...] = jnp.zeros_like(state_ref)

        pairs = range(RET_PAIRS)
        qbs = [_lanes(q_ref, p, LANES) for p in pairs]
        kbs = [_lanes(k_ref, p, LANES) for p in pairs]
        vbs = [_lanes(v_ref, p, pw) for p in pairs]
        zws = [_ret_block(lg_ref, hg * RET_PAIRS + p, i, t, qbs[p], kbs[p]) for p in pairs]
        ps = [(z * w).astype(BF16) for z, w in zws]
        outs = [jnp.concatenate([_dot(ps[p][:t], vbs[p][:, 0:RET_DV]), _dot(ps[p][t:], vbs[p][:, RET_DV:pw])], axis=1)
                + _dot(qbs[p], state_ref[p]) for p in pairs]
        for p in pairs:
            state_ref[p] += _pair_mask() * _dot(kbs[p], vbs[p], _TN)
        for p in pairs:
            for hh in range(2):
                cols = slice(p * pw + hh * RET_DV, p * pw + (hh + 1) * RET_DV)
                o = outs[p][:, hh * RET_DV:(hh + 1) * RET_DV]
                ret_ref[:, cols] = o
                mu = jnp.sum(o, axis=1, keepdims=True) * (1.0 / RET_DV)
                xc = o - mu
                var = jnp.sum(xc * xc, axis=1, keepdims=True) * (1.0 / RET_DV)
                nrm = xc * lax.rsqrt(var + EPS) * w_ref[:, cols]
                g = g_ref[:, cols].astype(F32)
                rg_ref[:, cols] = (g * _sigmoid(g) * nrm).astype(BF16)

    blk = lambda hg, i: (i, hg)
    return _call("ret_fwd", body, (n_pair // RET_PAIRS, s // t),
                 [(log_gamma, None, pltpu.SMEM),
                  (qk_rot, (t, wq), blk),
                  (qk_rot, (t, wq), lambda hg, i: (i, n_pair // RET_PAIRS + hg)),
                  (v_bf, (t, wv), blk),
                  (proj, (t, wv), lambda hg, i: (i, gate_off + hg)),
                  (gn_g, (1, wv), lambda hg, i: (0, hg))],
                 [((s, RET_V), F32, (t, wv), blk), ((s, RET_V), BF16, (t, wv), blk)],
                 scratch=[pltpu.VMEM((RET_PAIRS, LANES, pw), F32)], riders=riders)


def _tri(tk, strict_upper):
    r = lax.broadcasted_iota(jnp.int32, (tk, tk), 0)
    cc = lax.broadcasted_iota(jnp.int32, (tk, tk), 1)
    return ((r > cc) if strict_upper else (r < cc)).astype(BF16)


def _diagonal_step(i, tq, tk, make, carry):
    assert tq == 2 * tk and SB_GROUP == 4
    half = lax.rem(i, 2) == 0
    return lax.cond(half, lambda cr: make(SB_GROUP // 2)(0, cr), lambda cr: make(SB_GROUP)(0, cr), carry)


def _sb_valid(i, j, tq, tk):
    row, col = _tile_pos(i, j, tq, tk)
    return col < row


def _sb_fwd(qkv, tq, tk, riders=None):
    s = qkv.shape[0]
    n_pair = HEADS // 2
    _check_tiles(s, tq, tk, SB_GROUP)

    def body(q_ref, k_ref, v_ref, o_ref, a_ref):
        i = pl.program_id(1)
        upper = _tri(tk, True)
        qs = _stack_heads(q_ref[...])
        n_full, n_groups = _n_full(i, tq, tk, SB_GROUP), _n_groups(i, tq, tk, SB_GROUP)

        def make_step(near_diagonal, last, n_sub=SB_GROUP):
            def step(n, carry):
                c, o = carry
                g = last - 1 - n
                js = [g * SB_GROUP + sub for sub in range(n_sub)]
                zs = [_dot(qs, k_ref[_key_rows(j, tk), :], _NT) for j in js]
                log1ps = [jnp.log2(1.0 + jnp.exp2(-jnp.abs(z))) for z in zs]
                log_1ms = [-jnp.maximum(z, 0.0) - t for z, t in zip(zs, log1ps)]
                log_bs = [jnp.minimum(z, 0.0) - t for z, t in zip(zs, log1ps)]
                if near_diagonal:
                    valids = [_sb_valid(i, j, tq, tk) for j in js]
                    log_1ms = [jnp.where(v, l, 0.0) for v, l in zip(valids, log_1ms)]
                sticks = [_dot(l, upper) for l in log_1ms]
                sums = [jnp.sum(l, axis=1, keepdims=True) for l in log_1ms]
                cs = [None] * n_sub
                for sub in reversed(range(n_sub)):
                    cs[sub] = c
                    c = c + sums[sub]
                for sub, j in enumerate(js):
                    a = jnp.exp2(log_bs[sub] + sticks[sub] + cs[sub])
                    if near_diagonal:
                        a = jnp.where(valids[sub], a, 0.0)
                    a = a.astype(BF16)
                    a_ref[j] = a
                    o = o + _dot(_side_by_side(a, tq), _stack_heads(v_ref[_key_rows(j, tk), :]))
                return c, o
            return step

        carry = (jnp.zeros((2 * tq, 1), F32), jnp.zeros((tq, LANES), F32))
        carry = _diagonal_step(i, tq, tk, lambda n_sub: make_step(True, n_groups, n_sub), carry)
        _, acc = lax.fori_loop(0, n_full, make_step(False, n_full), carry)
        o_ref[...] = acc.astype(BF16)

    n_kb = s // tk
    return _call("sb_fwd", body, (n_pair, s // tq),
                 [(qkv, (tq, LANES), lambda hp, i: (i, hp)),
                  (qkv, (s, LANES), lambda hp, i: (0, n_pair + hp)),
                  (qkv, (s, LANES), lambda hp, i: (0, 2 * n_pair + hp))],
                 [((s, SB_W), BF16, (tq, LANES), lambda hp, i: (i, hp)),
                  ((n_pair, s // tq, n_kb, 2 * tq, tk), BF16, (None, None, n_kb, 2 * tq, tk),
                   lambda hp, i: (hp, i, 0, 0, 0))], riders=riders)


def _merge(retg, sb, w_ret, w_sb_t, proj, tm, tn, riders=None):
    s, d = retg.shape[0], w_ret.shape[1]
    ar_off = (2 * RET_QK + 2 * RET_V + 3 * SB_W) // tn
    as_off = ar_off + d // tn

    def body(rg_ref, sb_ref, wr_ref, ws_ref, ar_ref, as_ref, mix_ref, r_ref, s_ref):
        rr = _dot(rg_ref[...], wr_ref[...])
        ss = _dot(sb_ref[...], ws_ref[...], _NT)
        mix_ref[...] = (_sigmoid(ar_ref[...].astype(F32)) * rr + _sigmoid(as_ref[...].astype(F32)) * ss).astype(BF16)
        r_ref[...] = rr.astype(BF16)
        s_ref[...] = ss.astype(BF16)

    tile = (tm, tn)
    return _call("merge", body, (d // tn, s // tm),
                 [(retg, (tm, RET_V), lambda j, i: (i, 0)), (sb, (tm, SB_W), lambda j, i: (i, 0)),
                  (w_ret, (RET_V, tn), lambda j, i: (0, j)), (w_sb_t, (tn, SB_W), lambda j, i: (j, 0)),
                  (proj, tile, lambda j, i: (i, ar_off + j)), (proj, tile, lambda j, i: (i, as_off + j))],
                 [((s, d), BF16, tile, lambda j, i: (i, j))] * 3, riders=riders)


def _out_proj(mixed, w_out, x, mod, gp1, g2, tm):
    s, d = x.shape

    def body(a_ref, w_ref, x_ref, mod_ref, gp_ref, g2_ref, y_ref, hres_ref, h2_ref):
        for rows in _pieces(tm):
            y = _dot(a_ref[rows, :], w_ref[...])
            y_ref[rows, :] = y
            ny, _ = _rms(y, d)
            hres = x_ref[rows, :] + mod_ref[:, 2 * d:3 * d] * (ny * gp_ref[...])
            hres_ref[rows, :] = hres
            n2, _ = _rms(hres, d)
            h2_ref[rows, :] = (n2 * g2_ref[...] * (1.0 + mod_ref[:, 4 * d:5 * d]) + mod_ref[:, 3 * d:4 * d]).astype(BF16)

    row = lambda i: (i, 0)
    fix = lambda i: (0, 0)
    return _call("out_proj", body, (s // tm,),
                 [(mixed, (tm, d), row), (w_out, (d, d), fix), (x, (tm, d), row),
                  (mod, (1, 6 * d), fix), (gp1, (1, d), fix), (g2, (1, d), fix)],
                 [((s, d), F32, (tm, d), row), ((s, d), F32, (tm, d), row), ((s, d), BF16, (tm, d), row)])


def _ff1(h2, w_ff1_t, tm, tn):
    s, f = h2.shape[0], w_ff1_t.shape[0]
    tm = min(tm, s)

    def body(a_ref, w_ref, u_ref, act_ref):
        u = _dot(a_ref[...], w_ref[...], _NT)
        r = jnp.maximum(u, 0.0)
        u_ref[...] = u.astype(BF16)
        act_ref[...] = (r * r).astype(BF16)

    d = h2.shape[1]
    return _call("ff1", body, (f // tn, s // tm),
                 [(h2, (tm, d), lambda j, i: (i, 0)), (w_ff1_t, (tn, d), lambda j, i: (j, 0))],
                 [((s, f), BF16, (tm, tn), lambda j, i: (i, j))] * 2)


def _ff2_loss(act, w_ff2, hres, target, mod, gp2, tm):
    s, d = hres.shape
    f = act.shape[1]

    def body(a_ref, w_ref, h_ref, t_ref, mod_ref, gp_ref, dout_ref, df_ref, loss_ref, dgt_ref, dgp_ref):
        _zero_at_start([loss_ref, dgt_ref, dgp_ref])
        gt, gp = mod_ref[:, 5 * d:6 * d], gp_ref[...]
        for rows in _pieces(tm):
            ff = _dot(a_ref[rows, :], w_ref[...])
            nf, rf = _rms(ff, d)
            out = h_ref[rows, :] + gt * (nf * gp)
            err = out - t_ref[rows, :]
            sq = jnp.sum(err * err, axis=1, keepdims=True)
            loss_ref[...] += jnp.sum(sq, axis=0, keepdims=True)
            dout = err * (1.0 / d)
            dout_ref[rows, :] = dout
            dgt_ref[...] += _colsum(dout * (nf * gp))
            dgp_ref[...] += _colsum(dout * gt * nf)
            df_ref[rows, :] = _rms_bwd(dout * gt * gp, nf, rf, d).astype(BF16)

    row = lambda i: (i, 0)
    fix = lambda i: (0, 0)
    return _call("ff2_loss", body, (s // tm,),
                 [(act, (tm, f), row), (w_ff2, (f, d), fix), (hres, (tm, d), row), (target, (tm, d), row),
                  (mod, (1, 6 * d), fix), (gp2, (1, d), fix)],
                 [((s, d), F32, (tm, d), row), ((s, d), BF16, (tm, d), row), ((1, 1), F32, (1, 1), fix),
                  ((1, d), F32, (1, d), fix), ((1, d), F32, (1, d), fix)])


def _ff2_bwd(df, w_ff2, u, tm, tn):
    s, d = df.shape
    f = w_ff2.shape[0]
    tm = min(tm, s)

    def body(a_ref, w_ref, u_ref, du_ref):
        da = _dot(a_ref[...], w_ref[...], _NT)
        du_ref[...] = (da * (2.0 * jnp.maximum(u_ref[...].astype(F32), 0.0))).astype(BF16)

    return _call("ff2_bwd", body, (f // tn, s // tm),
                 [(df, (tm, d), lambda j, i: (i, 0)), (w_ff2, (tn, d), lambda j, i: (j, 0)),
                  (u, (tm, tn), lambda j, i: (i, j))],
                 [((s, f), BF16, (tm, tn), lambda j, i: (i, j))])[0]


def _ff1_bwd(du, w_ff1_t, hres, dout, y, mod, g2, gp1, tm, riders=None):
    s, d = hres.shape
    f = du.shape[1]

    def body(a_ref, w_ref, h_ref, do_ref, y_ref, mod_ref, g2_ref, gp_ref,
             dh_ref, dy_ref, dsh_ref, dsc_ref, dg2_ref, dgt_ref, dgp_ref):
        _zero_at_start([dsh_ref, dsc_ref, dg2_ref, dgt_ref, dgp_ref])
        g2, sc2 = g2_ref[...], mod_ref[:, 4 * d:5 * d]
        gt, gp = mod_ref[:, 2 * d:3 * d], gp_ref[...]
        for rows in _pieces(tm):
            dh2 = _dot(a_ref[rows, :], w_ref[...])
            n2, r2 = _rms(h_ref[rows, :], d)
            dsh_ref[...] += _colsum(dh2)
            dsc_ref[...] += _colsum(dh2 * n2 * g2)
            dg2_ref[...] += _colsum(dh2 * n2 * (1.0 + sc2))
            dhres = do_ref[rows, :] + _rms_bwd(dh2 * g2 * (1.0 + sc2), n2, r2, d)
            dh_ref[rows, :] = dhres
            ny, ry = _rms(y_ref[rows, :], d)
            dgt_ref[...] += _colsum(dhres * (ny * gp))
            dgp_ref[...] += _colsum(dhres * gt * ny)
            dy_ref[rows, :] = _rms_bwd(dhres * gt * gp, ny, ry, d).astype(BF16)

    row = lambda i: (i, 0)
    fix = lambda i: (0, 0)
    vec = ((1, d), F32, (1, d), fix)
    return _call("ff1_bwd", body, (s // tm,),
                 [(du, (tm, f), row), (w_ff1_t, (f, d), fix), (hres, (tm, d), row), (dout, (tm, d), row),
                  (y, (tm, d), row), (mod, (1, 6 * d), fix), (g2, (1, d), fix), (gp1, (1, d), fix)],
                 [((s, d), F32, (tm, d), row), ((s, d), BF16, (tm, d), row), vec, vec, vec, vec, vec], riders=riders)


def _out_bwd(dy, w_out, proj, r_bf, s_bf, tm, tn, riders=None):
    s, d = dy.shape
    ar_off = (2 * RET_QK + 2 * RET_V + 3 * SB_W) // tn
    as_off = ar_off + d // tn

    def body(a_ref, w_ref, ar_ref, as_ref, r_ref, s_ref, dr_ref, ds_ref, dar_ref, das_ref):
        dm = _dot(a_ref[...], w_ref[...], _NT)
        sr, ss = _sigmoid(ar_ref[...].astype(F32)), _sigmoid(as_ref[...].astype(F32))
        dr_ref[...] = (dm * sr).astype(BF16)
        ds_ref[...] = (dm * ss).astype(BF16)
        dar_ref[...] = (dm * r_ref[...].astype(F32) * sr * (1.0 - sr)).astype(BF16)
        das_ref[...] = (dm * s_ref[...].astype(F32) * ss * (1.0 - ss)).astype(BF16)

    tile = (tm, tn)
    here = lambda j, i: (i, j)
    return _call("out_bwd", body, (d // tn, s // tm),
                 [(dy, (tm, d), lambda j, i: (i, 0)), (w_out, (tn, d), lambda j, i: (j, 0)),
                  (proj, tile, lambda j, i: (i, ar_off + j)), (proj, tile, lambda j, i: (i, as_off + j)),
                  (r_bf, tile, here), (s_bf, tile, here)],
                 [((s, d), BF16, tile, here)] * 4, riders=riders)


def _gn_bwd(dretg, ret, proj, gn_g, tm, riders=None):
    s = ret.shape[0]
    gate_off = (2 * RET_QK + RET_V) // RET_V

    def body(d_ref, r_ref, g_ref, w_ref, dg_ref, dret_ref, dw_ref):
        first = pl.program_id(0) == 0
        for h in range(HEADS):
            cols = slice(h * RET_DV, (h + 1) * RET_DV)
            o, g, w, dr = r_ref[:, cols], g_ref[:, cols].astype(F32), w_ref[:, cols], d_ref[:, cols].astype(F32)
            mu = jnp.sum(o, axis=1, keepdims=True) * (1.0 / RET_DV)
            xc = o - mu
            rstd = lax.rsqrt(jnp.sum(xc * xc, axis=1, keepdims=True) * (1.0 / RET_DV) + EPS)
            n = xc * rstd
            sg = _sigmoid(g)
            silu = g * sg
            dg_ref[:, cols] = (dr * n * w * (sg * (1.0 + g * (1.0 - sg)))).astype(BF16)
            _accum(dw_ref.at[:, cols], _colsum(dr * silu * n), first)
            dn = dr * silu * w
            m1 = jnp.sum(dn, axis=1, keepdims=True) * (1.0 / RET_DV)
            m2 = jnp.sum(dn * n, axis=1, keepdims=True) * (1.0 / RET_DV)
            dret_ref[:, cols] = (rstd * (dn - m1 - n * m2)).astype(BF16)

    row = lambda i: (i, 0)
    fix = lambda i: (0, 0)
    return _call("gn_bwd", body, (s // tm,),
                 [(dretg, (tm, RET_V), row), (ret, (tm, RET_V), row),
                  (proj, (tm, RET_V), lambda i: (i, gate_off)), (gn_g, (1, RET_V), fix)],
                 [((s, RET_V), BF16, (tm, RET_V), row), ((s, RET_V), BF16, (tm, RET_V), row),
                  ((1, RET_V), F32, (1, RET_V), fix)], riders=riders)


def _ret_bwd(qk_rot, v_bf, dret, log_gamma, t, riders=None):
    s = qk_rot.shape[0]
    n_pair = HEADS // 2
    pw = 2 * RET_DV
    wq, wv = RET_PAIRS * LANES, RET_PAIRS * pw
    n_blk = s // t
    pairs = range(RET_PAIRS)

    def load(q_ref, k_ref, v_ref, do_ref):
        return ([_lanes(q_ref, p, LANES) for p in pairs], [_lanes(k_ref, p, LANES) for p in pairs],
                [_lanes(v_ref, p, pw) for p in pairs], [_lanes(do_ref, p, pw) for p in pairs])

    def d_scores(lg_ref, hp, i, qb, kb, vb, dob):
        z, w = _ret_block(lg_ref, hp, i, t, qb, kb)
        dp = jnp.concatenate([_dot(dob[:, 0:RET_DV], vb[:, 0:RET_DV], _NT),
                              _dot(dob[:, RET_DV:pw], vb[:, RET_DV:pw], _NT)], axis=0)
        return (z * w).astype(BF16), (dp * w).astype(BF16)

    def up_body(lg_ref, q_ref, k_ref, v_ref, do_ref, dq_ref, state_ref):
        hg, i = pl.program_id(0), pl.program_id(1)

        @pl.when(i == 0)
        def _():
            state_ref[...] = jnp.zeros_like(state_ref)

        qbs, kbs, vbs, dobs = load(q_ref, k_ref, v_ref, do_ref)
        dss = [d_scores(lg_ref, hg * RET_PAIRS + p, i, qbs[p], kbs[p], vbs[p], dobs[p])[1] for p in pairs]
        for p in pairs:
            dq_ref[:, p * LANES:(p + 1) * LANES] = (_dot(_side_by_side(dss[p], t), _stack_heads(kbs[p]))
                                                    + _dot(dobs[p], state_ref[p], _NT)).astype(BF16)
        for p in pairs:
            state_ref[p] += _pair_mask() * _dot(kbs[p], vbs[p], _TN)

    def down_body(lg_ref, q_ref, k_ref, v_ref, do_ref, dk_ref, dv_ref, state_ref):
        hg, i = pl.program_id(0), n_blk - 1 - pl.program_id(1)

        @pl.when(pl.program_id(1) == 0)
        def _():
            state_ref[...] = jnp.zeros_like(state_ref)

        qbs, kbs, vbs, dobs = load(q_ref, k_ref, v_ref, do_ref)
        both = [d_scores(lg_ref, hg * RET_PAIRS + p, i, qbs[p], kbs[p], vbs[p], dobs[p]) for p in pairs]
        for p in pairs:
            pp, ds = both[p]
            later = state_ref[p]
            dv_ref[:, p * pw:(p + 1) * pw] = (jnp.concatenate(
                [_dot(pp[:t], dobs[p][:, 0:RET_DV], _TN), _dot(pp[t:], dobs[p][:, RET_DV:pw], _TN)],
                axis=1) + _dot(kbs[p], later)).astype(BF16)
            dk_ref[:, p * LANES:(p + 1) * LANES] = (_dot(ds, _stack_heads(qbs[p]), _TN)
                                                    + _dot(vbs[p], later, _NT)).astype(BF16)
        for p in pairs:
            state_ref[p] += _pair_mask() * _dot(qbs[p], dobs[p], _TN)

    n_grp = n_pair // RET_PAIRS

    def ins(order):
        return [(log_gamma, None, pltpu.SMEM),
                (qk_rot, (t, wq), lambda hg, i: (order(i), hg)),
                (qk_rot, (t, wq), lambda hg, i: (order(i), n_grp + hg)),
                (v_bf, (t, wv), lambda hg, i: (order(i), hg)),
                (dret, (t, wv), lambda hg, i: (order(i), hg))]

    up = lambda i: i
    down = lambda i: n_blk - 1 - i
    scratch = [pltpu.VMEM((RET_PAIRS, LANES, pw), F32)]
    dq = _call("ret_bwd_q", up_body, (n_grp, n_blk), ins(up),
               [((s, RET_QK), BF16, (t, wq), lambda hg, i: (i, hg))], scratch=scratch)[0]
    dk, dv, *rest = _call("ret_bwd_kv", down_body, (n_grp, n_blk), ins(down),
                          [((s, RET_QK), BF16, (t, wq), lambda hg, i: (down(i), hg)),
                           ((s, RET_V), BF16, (t, wv), lambda hg, i: (down(i), hg))],
                          scratch=scratch, riders=riders)
    return [dq, dk, dv] + rest


def _sb_bwd(qkv, weights, do, tq, tk, riders=None):
    s = qkv.shape[0]
    n_pair = HEADS // 2
    _check_tiles(s, tq, tk, SB_GROUP)

    def body(q_ref, k_ref, v_ref, a_ref, do_ref, dq_ref, dk_ref, dv_ref):
        i = pl.program_id(1)

        @pl.when(i == 0)
        def _():
            dk_ref[...] = jnp.zeros_like(dk_ref)
            dv_ref[...] = jnp.zeros_like(dv_ref)

        lower = _tri(tk, False)
        qs = _stack_heads(q_ref[...])
        dos = _stack_heads(do_ref[...].astype(BF16))

        def make_step(near_diagonal, n_sub=SB_GROUP):
            def step(g, carry):
                c_e, dq = carry
                js = [g * SB_GROUP + sub for sub in range(n_sub)]
                rows = [_key_rows(j, tk) for j in js]
                zs = [_dot(qs, k_ref[rw, :], _NT) for rw in rows]
                das = [_dot(dos, v_ref[rw, :], _NT) for rw in rows]
                avals = [a_ref[j] for j in js]
                for a, rw in zip(avals, rows):
                    dv_ref[rw, :] += _dot(a, dos, _TN)
                es = [a.astype(F32) * da for a, da in zip(avals, das)]
                prefixes = [_dot(e, lower) for e in es]
                betas = [1.0 / (1.0 + jnp.exp2(-z)) for z in zs]
                for sub in range(n_sub):
                    dz = es[sub] - (es[sub] + prefixes[sub] + c_e) * betas[sub]
                    if near_diagonal:
                        dz = jnp.where(_sb_valid(i, js[sub], tq, tk), dz, 0.0)
                    dz = dz.astype(BF16)
                    dk_ref[rows[sub], :] += _dot(dz, qs, _TN)
                    dq = dq + _dot(_side_by_side(dz, tq), _stack_heads(k_ref[rows[sub], :]))
                    c_e = c_e + jnp.sum(es[sub], axis=1, keepdims=True)
                return c_e, dq
            return step

        n_full = _n_full(i, tq, tk, SB_GROUP)
        carry = (jnp.zeros((2 * tq, 1), F32), jnp.zeros((tq, LANES), F32))
        carry = lax.fori_loop(0, n_full, make_step(False), carry)
        _, dq = _diagonal_step(i, tq, tk, lambda n_sub: (lambda n, cr: make_step(True, n_sub)(n_full, cr)), carry)
        dq_ref[...] = dq

    blk = lambda hp, i: (i, hp)
    n_kb = s // tk
    return _call("sb_bwd", body, (n_pair, s // tq),
                 [(qkv, (tq, LANES), blk),
                  (qkv, (s, LANES), lambda hp, i: (0, n_pair + hp)),
                  (qkv, (s, LANES), lambda hp, i: (0, 2 * n_pair + hp)),
                  (weights, (None, None, n_kb, 2 * tq, tk), lambda hp, i: (hp, i, 0, 0, 0)),
                  (do, (tq, LANES), blk)],
                 [((s, SB_W), F32, (tq, LANES), blk),
                  ((s, SB_W), F32, (s, LANES), lambda hp, i: (0, hp)),
                  ((s, SB_W), F32, (s, LANES), lambda hp, i: (0, hp))], riders=riders)


def _assemble_dproj(dq_r, dk_r, dv_r, dg_r, dq_s, dk_s, dv_s, da_r, da_s, cos, sin, idx_col, lg_lanes, tm, riders=None):
    s, d = da_r.shape
    width = 2 * RET_QK + 2 * RET_V + 3 * SB_W + 2 * d

    def body(dq_ref, dk_ref, dv_ref, dg_ref, dqs_ref, dks_ref, dvs_ref, dar_ref, das_ref, cos_ref, sin_ref,
             idx_ref, lg_ref, o_ref):
        lane = lax.broadcasted_iota(jnp.int32, (1, LANES), 1)
        first = jnp.bitwise_and(lane, RET_DQK - 1) < (RET_DQK // 2)
        cos, sin = cos_ref[...], sin_ref[...]
        idx = idx_ref[...]
        for src, base, sign, scale in ((dq_ref, 0, 1.0, 1.0), (dk_ref, RET_QK, -1.0, RET_DQK ** -0.5)):
            for g in range(RET_QK // LANES):
                v = src[:, g * LANES:(g + 1) * LANES].astype(F32) * (_decay_scale(lg_ref, idx, g, sign) * scale)
                sw = jnp.where(first, pltpu.roll(v, LANES - RET_DQK // 2, 1), pltpu.roll(v, RET_DQK // 2, 1))
                o_ref[:, base + g * LANES:base + (g + 1) * LANES] = (v * cos - sw * sin).astype(BF16)
        off = 2 * RET_QK
        o_ref[:, off:off + RET_V] = dv_ref[...].astype(BF16)
        off += RET_V
        o_ref[:, off:off + RET_V] = dg_ref[...]
        off += RET_V
        o_ref[:, off:off + SB_W] = (dqs_ref[...] * (SB_DH ** -0.5)).astype(BF16)
        off += SB_W
        o_ref[:, off:off + SB_W] = (dks_ref[...] * LN2).astype(BF16)
        off += SB_W
        o_ref[:, off:off + SB_W] = dvs_ref[...].astype(BF16)
        off += SB_W
        o_ref[:, off:off + d] = dar_ref[...]
        off += d
        o_ref[:, off:off + d] = das_ref[...]

    row = lambda i: (i, 0)
    ins = [(a, (tm, a.shape[1]), row) for a in (dq_r, dk_r, dv_r, dg_r, dq_s, dk_s, dv_s, da_r, da_s, cos, sin, idx_col)]
    ins.append((lg_lanes, (1, RET_QK), lambda i: (0, 0)))
    return _call("assemble_dproj", body, (s // tm,), ins, [((s, width), BF16, (tm, width), row)], riders=riders)


def _in_bwd(dproj, w_in_t, x, dhres, mod, g1, tm, riders=None):
    s, d = x.shape
    width = dproj.shape[1]

    def body(a_ref, w_ref, x_ref, dh_ref, mod_ref, g_ref, dx_ref, dsh_ref, dsc_ref, dg_ref):
        _zero_at_start([dsh_ref, dsc_ref, dg_ref])
        g1, sc1 = g_ref[...], mod_ref[:, d:2 * d]
        for rows in _pieces(tm):
            dh = _dot(a_ref[rows, :], w_ref[...])
            n1, r1 = _rms(x_ref[rows, :], d)
            dsh_ref[...] += _colsum(dh)
            dsc_ref[...] += _colsum(dh * n1 * g1)
            dg_ref[...] += _colsum(dh * n1 * (1.0 + sc1))
            dx_ref[rows, :] = dh_ref[rows, :] + _rms_bwd(dh * g1 * (1.0 + sc1), n1, r1, d)

    row = lambda i: (i, 0)
    fix = lambda i: (0, 0)
    vec = ((1, d), F32, (1, d), fix)
    return _call("in_bwd", body, (s // tm,),
                 [(dproj, (tm, width), row), (w_in_t, (width, d), fix), (x, (tm, d), row), (dhres, (tm, d), row),
                  (mod, (1, 6 * d), fix), (g1, (1, d), fix)],
                 [((s, d), F32, (tm, d), row), vec, vec, vec], riders=riders)


def _adamw(w, g, m, v):
    m = ADAM_B1 * m + (1.0 - ADAM_B1) * g
    v = ADAM_B2 * v + (1.0 - ADAM_B2) * (g * g)
    m_hat = m / (1.0 - ADAM_B1 ** ADAM_STEP)
    v_hat = v / (1.0 - ADAM_B2 ** ADAM_STEP)
    delta = -ADAM_LR * (m_hat / (jnp.sqrt(v_hat) + ADAM_EPS) + ADAM_WD * w)
    return delta, m, v


def _adam_reduce(name, parts, w, m, v, tr):
    rws, cls = w.shape
    tr = min(tr, rws)
    n_parts = parts.shape[0]

    def body(p_ref, w_ref, m_ref, v_ref, g_out, d_out, m_out, v_out):
        g = p_ref[0].astype(F32)
        for k in range(1, n_parts):
            g = g + p_ref[k].astype(F32)
        delta, mn, vn = _adamw(w_ref[...], g, m_ref[...], v_ref[...])
        g_out[...] = g
        d_out[...] = delta
        m_out[...] = mn
        v_out[...] = vn

    row = lambda i: (i, 0)
    blk = (tr, cls)
    return _call(name, body, (rws // tr,),
                 [(parts, (n_parts, tr, cls), lambda i: (0, i, 0)), (w, blk, row), (m, blk, row), (v, blk, row)],
                 [((rws, cls), F32, blk, row)] * 4)


def _ada_bwd_adam(cs_t, dmod_cols, w, m, v):
    d, nc = w.shape

    def body(c_ref, dm_ref, w_ref, m_ref, v_ref, g_out, d_out, m_out, v_out):
        g = c_ref[0] * dm_ref[0:1, :]
        for r in range(1, N_DEV):
            g = g + c_ref[r] * dm_ref[r:r + 1, :]
        delta, mn, vn = _adamw(w_ref[...], g, m_ref[...], v_ref[...])
        g_out[...] = g
        d_out[...] = delta
        m_out[...] = mn
        v_out[...] = vn

    fix = lambda i: (0, 0)
    blk = (d, nc)
    return _call("ada_bwd_adam", body, (1,),
                 [(cs_t, (N_DEV, d, 1), lambda i: (0, 0, 0)), (dmod_cols, (N_DEV, nc), fix), (w, blk, fix), (m, blk, fix), (v, blk, fix)],
                 [((d, nc), F32, blk, fix)] * 4)


def _small_adam(parts, w, m, v):
    n = w.shape[1]

    def body(p_ref, w_ref, m_ref, v_ref, g_out, d_out, m_out, v_out):
        g = p_ref[0:1, :]
        for k in range(1, N_DEV):
            g = g + p_ref[k:k + 1, :]
        delta, mn, vn = _adamw(w_ref[...], g, m_ref[...], v_ref[...])
        g_out[...] = g
        d_out[...] = delta
        m_out[...] = mn
        v_out[...] = vn

    fix = lambda i: (0, 0)
    return _call("small_adam", body, (1,),
                 [(parts, (N_DEV, n), fix), (w, (1, n), fix), (m, (1, n), fix), (v, (1, n), fix)],
                 [((1, n), F32, (1, n), fix)] * 4)


def kernel(x, c, positions, ada_w, ada_b, pre_mix_g, post_mix_g, pre_ffn_g, post_ffn_g, w_in, ret_gn_g, w_ret_branch, w_sb_branch, w_out, w_ff1, w_ff2, loss_target, m_ada_w, m_ada_b, m_pre_mix_g, m_post_mix_g, m_pre_ffn_g, m_post_ffn_g, m_w_in, m_ret_gn_g, m_w_ret_branch, m_w_sb_branch, m_w_out, m_w_ff1, m_w_ff2, v_ada_w, v_ada_b, v_pre_mix_g, v_post_mix_g, v_pre_ffn_g, v_post_ffn_g, v_w_in, v_ret_gn_g, v_w_ret_branch, v_w_sb_branch, v_w_out, v_w_ff1, v_w_ff2):
    _, s, d = x.shape
    d_ff = w_ff1.shape[2] * N_DEV
    d_in = w_in.shape[2] * N_DEV
    me = 4 * lax.axis_index("x") + 2 * lax.axis_index("y") + lax.axis_index("c")
    x2, tgt = x[0], loss_target[0]

    core = lax.axis_index("c").astype(jnp.int32).reshape(1)
    bf = lambda w: w[0].astype(BF16)

    w_in_t, m_in_t, v_in_t = (jnp.swapaxes(a[0], 0, 1) for a in (w_in, m_w_in, v_w_in))

    c_all, g_in = _exchange("gather_in", [c, w_in_t.astype(BF16)], ["gather", "gather_chip"])
    c_all = c_all.reshape(N_DEV, d)

    n_ada = ada_w.shape[2]
    cs_all = _silu_rows(c_all)
    ada_b_cols = lax.dynamic_slice(ada_b, (0, me * n_ada), (1, n_ada))
    mod_cols = _ada_fwd(cs_all, ada_w[0], ada_b_cols)
    mod_all = _exchange("gather_mod", [mod_cols], ["gather"])[0]
    mod = lax.dynamic_index_in_dim(mod_all, me, axis=1, keepdims=False).reshape(1, 6 * d)

    tm = min(256, s)
    h, g_in = _pre_norm(x2, pre_mix_g, mod, tm, riders=([g_in], ["forward"]))
    wt_in = g_in.reshape(d_in, d)
    bf_t = lambda w: jnp.swapaxes(w[0], 0, 1).astype(BF16)
    small_w = [bf(w_ret_branch), bf_t(w_sb_branch), bf(w_out)]
    proj, *small_w = _matmul("in_proj", h, wt_in, "nt", s, 512, BF16, riders=(small_w, ["gather_chip"] * 3))
    pos_col = positions.reshape(s, 1).astype(F32)
    freqs = ROPE_BASE ** (-jnp.arange(0, RET_DQK, 2, dtype=F32) / RET_DQK)
    inv_freq = jnp.tile(freqs, LANES // (RET_DQK // 2)).reshape(1, LANES)
    log_gamma_np = np.log1p(-(2.0 ** (-5.0 - np.arange(HEADS))))
    log_gamma = jnp.asarray(log_gamma_np, F32)
    lg_lanes = jnp.asarray(np.repeat(log_gamma_np, RET_DQK).reshape(1, RET_QK), F32)
    idx_col = (jnp.arange(s, dtype=F32) - (s // 2)).reshape(s, 1)
    qk_rot, v_bf, qkv_sb, cos_t, sin_t = _prep(proj, pos_col, idx_col, inv_freq, lg_lanes, tm)
    tq, tk = min(256, s), min(128, s)
    tq_sb = min(SB_TQ, s)
    sb, sb_weights, *big_w = _sb_fwd(qkv_sb, tq_sb, tk, riders=([bf(w_ff2), bf_t(w_ff1)], ["gather_chip"] * 2))
    ret, retg, g_ret, g_sb, g_out, g_ff2, g_ff1 = _ret_fwd(qk_rot, v_bf, proj, ret_gn_g, log_gamma, tq,
                                                           riders=(small_w + big_w, ["forward"] * 5))
    wf_ret = g_ret.reshape(RET_V, d)
    wt_sb = g_sb.reshape(d, SB_W)
    wf_out = g_out.reshape(d, d)
    wt_ff1 = g_ff1.reshape(d_ff, d)
    wf_ff2 = g_ff2.reshape(d_ff, d)
    mixed, r_bf, s_bf = _merge(retg, sb, wf_ret, wt_sb, proj, tm, min(512, d))
    y, hres, h2 = _out_proj(mixed, wf_out, x2, mod, post_mix_g, pre_ffn_g, tm)
    u, act = _ff1(h2, wt_ff1, s, 512)
    dout, df, loss_sum, d_gt2, d_gp2 = _ff2_loss(act, wf_ff2, hres, tgt, mod, post_ffn_g, tm)

    du = _ff2_bwd(df, wf_ff2, u, s, 512)
    gw_ff2 = _matmul("grad_w_ff2", act, df, "tn", 512, d, BF16).reshape(N_DEV, d_ff // N_DEV, d)
    gw_ff1 = _matmul("grad_w_ff1", h2, du, "tn", d, d_ff // N_DEV, BF16, blocked_out=True)
    dhres, dy, d_sh2, d_sc2, d_g2, d_gt1, d_gp1, t_ff1, t_ff2 = _ff1_bwd(
        du, wt_ff1, hres, dout, y, mod, pre_ffn_g, post_mix_g, tm, riders=([gw_ff1, gw_ff2], ["pair"] * 2))
    s_ff1 = _pair_sum("pair_sum_ff1", gw_ff1, t_ff1, core, 256)
    s_ff2 = _pair_sum("pair_sum_ff2", gw_ff2, t_ff2, core, 256)
    gw_out = _matmul("grad_w_out", mixed, dy, "tn", 512, d, BF16).reshape(N_DEV, d // N_DEV, d)
    d_r, d_s, da_r, da_s, p_out = _out_bwd(dy, wf_out, proj, r_bf, s_bf, tm, min(512, d), riders=([gw_out], ["scatter"]))
    dretg = _matmul("ret_branch_bwd", d_r, wf_ret, "nt", s, 512, BF16)
    dsb = _matmul("sb_branch_bwd", d_s, wt_sb, "nn", s, 512, F32)
    gw_ret = _matmul("grad_w_ret", retg, d_r, "tn", 512, d, BF16).reshape(N_DEV, RET_V // N_DEV, d)
    gw_sb = _matmul("grad_w_sb", sb, d_s, "tn", 512, d // N_DEV, BF16, blocked_out=True)
    dq_s, dk_s, dv_s, p_ff1, p_ff2 = _sb_bwd(qkv_sb, sb_weights, dsb, tq_sb, tk,
                                             riders=([s_ff1, s_ff2], ["chip_scatter"] * 2))
    dg_r, dret, d_gn = _gn_bwd(dretg, ret, proj, ret_gn_g, tm)
    dq_r, dk_r, dv_r, p_sb = _ret_bwd(qk_rot, v_bf, dret, log_gamma, tq, riders=([gw_sb], ["scatter"]))
    dproj, p_ret = _assemble_dproj(dq_r, dk_r, dv_r, dg_r, dq_s, dk_s, dv_s, da_r, da_s, cos_t, sin_t, idx_col, lg_lanes,
                                   tm, riders=([gw_ret], ["scatter"]))
    gw_in = _matmul("grad_w_in", dproj, h, "tn", 512, d, BF16).reshape(N_DEV, d_in // N_DEV, d)
    t_in = _exchange("pair_in", [gw_in], ["pair"])[0]
    tr_in = d_in // N_DEV // 4
    s_in = _pair_sum("pair_sum_in", gw_in, t_in, core, tr_in)
    grad_x, d_sh1, d_sc1, d_g1, p_in = _in_bwd(dproj, wt_in, x2, dhres, mod, pre_mix_g, tm,
                                               riders=([s_in], ["chip_scatter"]))
    loss_lanes = jnp.pad(loss_sum, ((0, 0), (0, LANES - 1)))
    small = jnp.concatenate([d_sh1, d_sc1, d_gt1, d_sh2, d_sc2, d_gt2, d_g1, d_gp1, d_g2, d_gp2, d_gn, loss_lanes], axis=1)
    small_all = _exchange("gather_small", [small], ["gather"])[0].reshape(N_DEV, small.shape[1])
    parts = [p_in, p_ret, p_sb, p_out, p_ff1, p_ff2]

    res = {}
    names = ["w_ret_branch", "w_sb_branch", "w_out", "w_ff1", "w_ff2"]
    ws = [w_ret_branch, w_sb_branch, w_out, w_ff1, w_ff2]
    ms = [m_w_ret_branch, m_w_sb_branch, m_w_out, m_w_ff1, m_w_ff2]
    vs = [v_w_ret_branch, v_w_sb_branch, v_w_out, v_w_ff1, v_w_ff2]
    for nm, p, w, m, v in zip(names, parts[1:], ws, ms, vs):
        res[nm] = [o[None] for o in _adam_reduce("adam_" + nm, p, w[0], m[0], v[0], 256)]
    res["w_in"] = [jnp.swapaxes(o, 0, 1)[None]
                   for o in _adam_reduce("adam_w_in", parts[0], w_in_t, m_in_t, v_in_t, tr_in)]
    dmod_cols = lax.dynamic_slice(small_all, (0, me * n_ada), (N_DEV, n_ada))
    res["ada_w"] = [o[None] for o in _ada_bwd_adam(cs_all.reshape(N_DEV, d, 1), dmod_cols, ada_w[0], m_ada_w[0], v_ada_w[0])]
    vec_names = ["ada_b", "pre_mix_g", "post_mix_g", "pre_ffn_g", "post_ffn_g", "ret_gn_g"]
    cat = lambda xs: jnp.concatenate(xs + [jnp.zeros((1, LANES), F32)], axis=1)
    packed = _small_adam(small_all,
                         cat([ada_b, pre_mix_g, post_mix_g, pre_ffn_g, post_ffn_g, ret_gn_g]),
                         cat([m_ada_b, m_pre_mix_g, m_post_mix_g, m_pre_ffn_g, m_post_ffn_g, m_ret_gn_g]),
                         cat([v_ada_b, v_pre_mix_g, v_post_mix_g, v_pre_ffn_g, v_post_ffn_g, v_ret_gn_g]))
    off = 0
    for nm, width in zip(vec_names, [6 * d, d, d, d, d, RET_V]):
        res[nm] = [p[:, off:off + width] for p in packed]
        off += width

    loss = (0.5 / d) * packed[0][0, off]
    order = ["ada_w", "ada_b", "pre_mix_g", "post_mix_g", "pre_ffn_g", "post_ffn_g", "w_in", "ret_gn_g",
             "w_ret_branch", "w_sb_branch", "w_out", "w_ff1", "w_ff2"]
    outs = [loss, grad_x[None]]
    for k in range(4):
        outs += [res[nm][k] for nm in order]
    return tuple(outs)
```

```python
import functools

import numpy as np
import jax
import jax.numpy as jnp
from jax import lax
from jax.experimental import pallas as pl
from jax.experimental.pallas import tpu as pltpu

F32 = jnp.float32
BF16 = jnp.bfloat16
N_DEV = 8
AXES = ("x", "y", "c")

EPS = 1e-6
CHUNK = 64
CHUNK_SHIFT = 6
HEADS = 8
RET_DQK = 64
RET_DV = 128
SB_DH = 64
RET_QK = HEADS * RET_DQK
RET_V = HEADS * RET_DV
SB_W = HEADS * SB_DH
ROPE_BASE = 10000.0
LANES = 128

ADAM_LR = 0.001
ADAM_B1 = 0.9
ADAM_B2 = 0.999
ADAM_EPS = 1e-08
ADAM_WD = 0.01
ADAM_STEP = 10

VMEM_LIMIT = 56 * 1024 * 1024

_NN = (((1,), (0,)), ((), ()))
_NT = (((1,), (1,)), ((), ()))
_TN = (((0,), (0,)), ((), ()))


def _dot(a, b, dims=_NN):
    if a.dtype != BF16:
        a = a.astype(BF16)
    if b.dtype != BF16:
        b = b.astype(BF16)
    return lax.dot_general(a, b, dims, preferred_element_type=F32)


def _sigmoid(x):
    return 1.0 / (1.0 + jnp.exp(-x))


def _rms(x, d):
    r = lax.rsqrt(jnp.sum(x * x, axis=1, keepdims=True) * (1.0 / d) + EPS)
    return x * r, r


def _rms_bwd(dn, n, r, d):
    return r * (dn - n * (jnp.sum(dn * n, axis=1, keepdims=True) * (1.0 / d)))


def _colsum(v):
    return jnp.sum(v, axis=0, keepdims=True)


def _accum(ref, val, first):
    @pl.when(first)
    def _():
        ref[...] = val

    @pl.when(jnp.logical_not(first))
    def _():
        ref[...] += val


ROW_SPLIT = 2


def _zero_at_start(refs):
    @pl.when(pl.program_id(0) == 0)
    def _():
        for r in refs:
            r[...] = jnp.zeros_like(r)


def _pieces(tm):
    step = tm // ROW_SPLIT
    return [slice(k * step, (k + 1) * step) for k in range(ROW_SPLIT)]


KIND_SLOTS = {"gather": N_DEV, "scatter": N_DEV, "gather_chip": N_DEV, "forward": N_DEV, "pair": N_DEV // 2,
              "chip_scatter": N_DEV // 2}
SEMS_PER_ARRAY = N_DEV - 1


def _exchange_copies(ins, outs, send_sems, recv_sems, local_sems, kinds):
    x, y, c = (lax.axis_index(a) for a in AXES)
    me, chip, sibling = 4 * x + 2 * y + c, 2 * x + y, (x, y, 1 - c)
    mesh_id = pl.DeviceIdType.MESH
    other_chips = []
    for k in range(1, N_DEV // 2):
        px = 1 - x if k & 2 else x
        py = 1 - y if k & 1 else y
        other_chips.append((px, py))
    copies = []
    for i, kind in enumerate(kinds):
        def remote(src, dst, k, to, i=i):
            return pltpu.make_async_remote_copy(
                src_ref=src, dst_ref=dst, send_sem=send_sems.at[i * SEMS_PER_ARRAY + k],
                recv_sem=recv_sems.at[i * SEMS_PER_ARRAY + k], device_id=to, device_id_type=mesh_id)

        if kind in ("gather", "scatter"):
            pick = (lambda ref, d: ref.at[d]) if kind == "scatter" else (lambda ref, d: ref)
            copies.append(pltpu.make_async_copy(pick(ins[i], me), outs[i].at[me], local_sems.at[i]))
            for k in range(1, N_DEV):
                to = (1 - x if k & 4 else x, 1 - y if k & 2 else y, 1 - c if k & 1 else c)
                copies.append(remote(pick(ins[i], 4 * to[0] + 2 * to[1] + to[2]), outs[i].at[me], k - 1, to))
        elif kind == "gather_chip":
            copies.append(pltpu.make_async_copy(ins[i], outs[i].at[me], local_sems.at[i]))
            copies.append(remote(ins[i], outs[i].at[me], 0, sibling))
            for k, (px, py) in enumerate(other_chips):
                copies.append(remote(ins[i], outs[i].at[me], 1 + k, (px, py, c)))
        elif kind == "forward":
            for k, (px, py) in enumerate(other_chips):
                slot = 4 * px + 2 * py + c
                copies.append(remote(outs[i].at[slot], outs[i].at[slot], k, sibling))
        elif kind == "pair":
            for k in range(N_DEV // 2):
                copies.append(remote(ins[i].at[2 * k + 1 - c], outs[i].at[k], k, sibling))
        elif kind == "chip_scatter":
            copies.append(pltpu.make_async_copy(ins[i].at[chip], outs[i].at[chip], local_sems.at[i]))
            for k, (px, py) in enumerate(other_chips):
                copies.append(remote(ins[i].at[2 * px + py], outs[i].at[chip], k, (px, py, c)))
        else:
            raise ValueError(kind)
    return copies


def _exchange_shapes(arrays, kinds):
    shapes = []
    for a, kind in zip(arrays, kinds):
        tail = a.shape if kind in ("gather", "gather_chip") else a.shape[1:]
        shapes.append(jax.ShapeDtypeStruct((KIND_SLOTS[kind],) + tuple(tail), a.dtype))
    return shapes


def _exchange_sems(n):
    return [pltpu.SemaphoreType.DMA((n * SEMS_PER_ARRAY,)), pltpu.SemaphoreType.DMA((n * SEMS_PER_ARRAY,)),
            pltpu.SemaphoreType.DMA((n,))]


def _call(name, body, grid, ins, outs, scratch=(), riders=None, prefetch=None):
    any_spec = pl.BlockSpec(memory_space=pl.ANY)
    in_specs = [pl.BlockSpec(memory_space=im) if bs is None else pl.BlockSpec(bs, im) for _, bs, im in ins]
    out_specs = [pl.BlockSpec(bs, im) for _, _, bs, im in outs]
    out_shape = [jax.ShapeDtypeStruct(s, d) for s, d, _, _ in outs]
    operands = [a for a, _, _ in ins]
    scratch = list(scratch)
    aliases = {}
    n_pre = 0 if prefetch is None else 1
    kernel = functools.partial(body) if prefetch is None else (lambda _, *refs: body(*refs))
    if riders is not None:
        arrays, kinds = riders
        nr, n_in, n_out, n_scr = len(arrays), len(ins), len(outs), len(scratch)

        def kernel(*refs):
            refs = refs[n_pre:]
            own_in, ride_in = refs[:n_in], refs[n_in:n_in + nr]
            own_out = refs[n_in + nr:n_in + nr + n_out]
            ride_out = refs[n_in + nr + n_out:n_in + 2 * nr + n_out]
            own_scr = refs[n_in + 2 * nr + n_out:n_in + 2 * nr + n_out + n_scr]
            sems = refs[n_in + 2 * nr + n_out + n_scr:]
            ids = [pl.program_id(a) for a in range(len(grid))]
            first = functools.reduce(jnp.logical_and, [i == 0 for i in ids])
            last = functools.reduce(jnp.logical_and, [i == g - 1 for i, g in zip(ids, grid)])

            @pl.when(first)
            def _():
                for cp in _exchange_copies(ride_in, ride_out, *sems, kinds):
                    cp.start()

            body(*own_in, *own_out, *own_scr)

            @pl.when(last)
            def _():
                for cp in _exchange_copies(ride_in, ride_out, *sems, kinds):
                    cp.wait()

        in_specs += [any_spec] * nr
        out_specs += [any_spec] * nr
        out_shape += _exchange_shapes(arrays, kinds)
        operands += list(arrays)
        scratch += _exchange_sems(nr)
        aliases = {n_pre + n_in + r: n_out + r for r, kind in enumerate(kinds) if kind == "forward"}
    params = pltpu.CompilerParams(dimension_semantics=("arbitrary",) * len(grid), vmem_limit_bytes=VMEM_LIMIT)
    if prefetch is None:
        return pl.pallas_call(kernel, name=name, grid=grid, in_specs=in_specs, out_specs=out_specs,
                              out_shape=out_shape, scratch_shapes=scratch, input_output_aliases=aliases,
                              compiler_params=params)(*operands)
    grid_spec = pltpu.PrefetchScalarGridSpec(num_scalar_prefetch=1, grid=grid, in_specs=in_specs,
                                             out_specs=out_specs, scratch_shapes=scratch)
    return pl.pallas_call(kernel, name=name, grid_spec=grid_spec, out_shape=out_shape,
                          input_output_aliases=aliases, compiler_params=params)(prefetch, *operands)


def _exchange(name, arrays, kinds):
    n = len(arrays)

    def body(*refs):
        copies = _exchange_copies(refs[:n], refs[n:2 * n], *refs[2 * n:], kinds)
        for cp in copies:
            cp.start()
        for cp in copies:
            cp.wait()

    any_spec = pl.BlockSpec(memory_space=pl.ANY)
    return pl.pallas_call(
        functools.partial(body),
        name=name,
        in_specs=[any_spec] * n,
        out_specs=[any_spec] * n,
        out_shape=_exchange_shapes(arrays, kinds),
        scratch_shapes=_exchange_sems(n),
        input_output_aliases={i: i for i, kind in enumerate(kinds) if kind == "forward"},
    )(*arrays)


def _pair_sum(name, mine, theirs, my_core, tr):
    _, rws, cls = mine.shape
    tr = min(tr, rws)

    def body(a_ref, b_ref, o_ref):
        o_ref[...] = (a_ref[...].astype(F32) + b_ref[...].astype(F32)).astype(o_ref.dtype)

    return _call(name, body, (N_DEV // 2, rws // tr),
                 [(mine, (None, tr, cls), lambda k, r, core: (2 * k + core[0], r, 0)),
                  (theirs, (None, tr, cls), lambda k, r, core: (k, r, 0))],
                 [((N_DEV // 2, rws, cls), mine.dtype, (None, tr, cls), lambda k, r, core: (k, r, 0))],
                 prefetch=my_core)[0]


def _matmul(name, a, b, kind, tm, tn, out_dtype, blocked_out=False, riders=None):
    if kind == "tn":
        kdim, m = a.shape
    else:
        m, kdim = a.shape
    n = b.shape[0] if kind == "nt" else b.shape[1]
    tm, tn = min(tm, m), min(tn, n)
    dims = {"nn": _NN, "nt": _NT, "tn": _TN}[kind]

    def body(a_ref, b_ref, o_ref):
        o_ref[...] = _dot(a_ref[...], b_ref[...], dims).astype(o_ref.dtype)

    a_spec = (a, (kdim, tm), lambda j, i: (0, i)) if kind == "tn" else (a, (tm, kdim), lambda j, i: (i, 0))
    b_spec = (b, (tn, kdim), lambda j, i: (j, 0)) if kind == "nt" else (b, (kdim, tn), lambda j, i: (0, j))
    if blocked_out:
        out = ((n // tn, m, tn), out_dtype, (None, tm, tn), lambda j, i: (j, i, 0))
    else:
        out = ((m, n), out_dtype, (tm, tn), lambda j, i: (i, j))
    res = _call(name, body, (n // tn, m // tm), [a_spec, b_spec], [out], riders=riders)
    return res[0] if riders is None else res


def _ada_fwd(cs_all, ada_w, ada_b_cols):
    def body(c_ref, w_ref, b_ref, o_ref):
        o_ref[...] = lax.dot_general(c_ref[...], w_ref[...], _NN, preferred_element_type=F32,
                                     precision=lax.Precision.HIGHEST) + b_ref[...]

    r, d = cs_all.shape
    nc = ada_w.shape[1]
    return _call("ada_fwd", body, (1,),
                 [(cs_all, (r, d), lambda i: (0, 0)), (ada_w, (d, nc), lambda i: (0, 0)),
                  (ada_b_cols, (1, nc), lambda i: (0, 0))],
                 [((r, nc), F32, (r, nc), lambda i: (0, 0))])[0]


def _silu_rows(c_all):
    def body(c_ref, o_ref):
        v = c_ref[...]
        o_ref[...] = v * _sigmoid(v)

    return _call("silu_c", body, (1,), [(c_all, c_all.shape, lambda i: (0, 0))],
                 [(c_all.shape, F32, c_all.shape, lambda i: (0, 0))])[0]


def _pre_norm(x, g, mod, tm, riders=None):
    s, d = x.shape

    def body(x_ref, g_ref, mod_ref, h_ref):
        n, _ = _rms(x_ref[...], d)
        sh, sc = mod_ref[:, 0:d], mod_ref[:, d:2 * d]
        h_ref[...] = (n * g_ref[...] * (1.0 + sc) + sh).astype(BF16)

    return _call("pre_norm", body, (s // tm,),
                 [(x, (tm, d), lambda i: (i, 0)), (g, (1, d), lambda i: (0, 0)),
                  (mod, (1, 6 * d), lambda i: (0, 0))],
                 [((s, d), BF16, (tm, d), lambda i: (i, 0))], riders=riders)


LOG2E = 1.4426950408889634
LN2 = 0.6931471805599453


def _decay_scale(lg_ref, idx, g, sign):
    return jnp.exp((sign * idx) * lg_ref[:, g * LANES:(g + 1) * LANES])


def _prep(proj, pos_col, idx_col, inv_freq, lg_lanes, tm):
    s = proj.shape[0]
    sb_off = (2 * RET_QK + 2 * RET_V) // (3 * SB_W)
    n_q = RET_QK // LANES

    def body(qk_ref, v_ref, sb_ref, pos_ref, idx_ref, f_ref, lg_ref, qk_out, v_out, sb_out, cos_out, sin_out):
        ang = pos_ref[...] * f_ref[...]
        lane = lax.broadcasted_iota(jnp.int32, (1, LANES), 1)
        first = jnp.bitwise_and(lane, RET_DQK - 1) < (RET_DQK // 2)
        cos = jnp.cos(ang)
        sin = jnp.where(first, -1.0, 1.0) * jnp.sin(ang)
        cos_out[...] = cos
        sin_out[...] = sin
        idx = idx_ref[...]
        for g in range(2 * n_q):
            v = qk_ref[:, g * LANES:(g + 1) * LANES].astype(F32)
            sw = jnp.where(first, pltpu.roll(v, LANES - RET_DQK // 2, 1), pltpu.roll(v, RET_DQK // 2, 1))
            r = v * cos + sw * sin
            if g < n_q:
                r = r * _decay_scale(lg_ref, idx, g, 1.0)
            else:
                r = r * (_decay_scale(lg_ref, idx, g - n_q, -1.0) * (RET_DQK ** -0.5))
            qk_out[:, g * LANES:(g + 1) * LANES] = r.astype(BF16)
        v_out[...] = v_ref[...].astype(BF16)
        sb_out[:, 0:SB_W] = (sb_ref[:, 0:SB_W].astype(F32) * (SB_DH ** -0.5 * LOG2E)).astype(BF16)
        sb_out[:, SB_W:3 * SB_W] = sb_ref[:, SB_W:3 * SB_W].astype(BF16)

    return _call("prep", body, (s // tm,),
                 [(proj, (tm, 2 * RET_QK), lambda i: (i, 0)),
                  (proj, (tm, RET_V), lambda i: (i, 2 * RET_QK // RET_V)),
                  (proj, (tm, 3 * SB_W), lambda i: (i, sb_off)),
                  (pos_col, (tm, 1), lambda i: (i, 0)),
                  (idx_col, (tm, 1), lambda i: (i, 0)),
                  (inv_freq, (1, LANES), lambda i: (0, 0)),
                  (lg_lanes, (1, RET_QK), lambda i: (0, 0))],
                 [((s, 2 * RET_QK), BF16, (tm, 2 * RET_QK), lambda i: (i, 0)),
                  ((s, RET_V), BF16, (tm, RET_V), lambda i: (i, 0)),
                  ((s, 3 * SB_W), BF16, (tm, 3 * SB_W), lambda i: (i, 0)),
                  ((s, LANES), F32, (tm, LANES), lambda i: (i, 0)),
                  ((s, LANES), F32, (tm, LANES), lambda i: (i, 0))])


def _head_mask(hh):
    lane = lax.broadcasted_iota(jnp.int32, (1, LANES), 1)
    return (lane >= RET_DQK) if hh else (lane < RET_DQK)


def _masked(v, m):
    return jnp.where(m, v, jnp.zeros_like(v))


SB_GROUP = 4
SB_TQ = 256


def _stack_heads(v):
    return jnp.concatenate([_masked(v, _head_mask(0)), _masked(v, _head_mask(1))], axis=0)


def _side_by_side(v, t):
    return jnp.concatenate([v[:t], v[t:]], axis=1)


def _tile_pos(i, j, tq, tk):
    row = jnp.bitwise_and(lax.broadcasted_iota(jnp.int32, (2 * tq, tk), 0), tq - 1) + i * tq
    col = lax.broadcasted_iota(jnp.int32, (2 * tq, tk), 1) + j * tk
    return row, col


def _n_groups(i, tq, tk, grp):
    return ((i + 1) * (tq // tk) + grp - 1) // grp


def _n_full(i, tq, tk, grp):
    return (i * (tq // tk)) // grp


def _key_rows(j, tk):
    return pl.ds(pl.multiple_of(j * tk, tk), tk)


def _ret_weight(lg_rows, i, j, tq, tk):
    row, col = _tile_pos(i, j, tq, tk)
    same = jnp.right_shift(col, CHUNK_SHIFT) == jnp.right_shift(row, CHUNK_SHIFT)
    later = jnp.where(same, jnp.exp((2.0 * lg_rows) * (col - row).astype(F32)), 0.0)
    return jnp.where(col <= row, 1.0, later)


def _lg_rows(lg_ref, hp, tq):
    first = lax.broadcasted_iota(jnp.int32, (2 * tq, 1), 0) < tq
    return jnp.where(first, lg_ref[2 * hp], lg_ref[2 * hp + 1])


def _check_tiles(s, tq, tk, grp):
    assert tq % tk == 0 and tq & (tq - 1) == 0 and tk & (tk - 1) == 0
    assert s % tq == 0 and (s // tk) % grp == 0 and s // tk <= LANES


def _pair_mask():
    r = lax.broadcasted_iota(jnp.int32, (LANES, 2 * RET_DV), 0) >= RET_DQK
    c = lax.broadcasted_iota(jnp.int32, (LANES, 2 * RET_DV), 1) >= RET_DV
    return (r == c).astype(F32)


def _ret_block(lg_ref, hp, i, t, qb, kb):
    w = _ret_weight(_lg_rows(lg_ref, hp, t), i, i, t, t)
    return _dot(_stack_heads(qb), kb, _NT), w


RET_PAIRS = 2


def _lanes(ref, p, width):
    return ref[:, p * width:(p + 1) * width]


def _ret_fwd(qk_rot, v_bf, proj, gn_g, log_gamma, t, riders=None):
    s = qk_rot.shape[0]
    n_pair = HEADS // 2
    pw = 2 * RET_DV
    wq, wv = RET_PAIRS * LANES, RET_PAIRS * pw
    gate_off = (2 * RET_QK + RET_V) // wv
    assert s % t == 0 and t % CHUNK == 0 and t & (t - 1) == 0 and n_pair % RET_PAIRS == 0

    def body(lg_ref, q_ref, k_ref, v_ref, g_ref, w_ref, ret_ref, rg_ref, state_ref):
        hg, i = pl.program_id(0), pl.program_id(1)

        @pl.when(i == 0)
        def _():
            state_ref[...] = jnp.zeros_like(state_ref)

        pairs = range(RET_PAIRS)
        qbs = [_lanes(q_ref, p, LANES) for p in pairs]
        kbs = [_lanes(k_ref, p, LANES) for p in pairs]
        vbs = [_lanes(v_ref, p, pw) for p in pairs]
        zws = [_ret_block(lg_ref, hg * RET_PAIRS + p, i, t, qbs[p], kbs[p]) for p in pairs]
        ps = [(z * w).astype(BF16) for z, w in zws]
        outs = [jnp.concatenate([_dot(ps[p][:t], vbs[p][:, 0:RET_DV]), _dot(ps[p][t:], vbs[p][:, RET_DV:pw])], axis=1)
                + _dot(qbs[p], state_ref[p]) for p in pairs]
        for p in pairs:
            state_ref[p] += _pair_mask() * _dot(kbs[p], vbs[p], _TN)
        for p in pairs:
            for hh in range(2):
                cols = slice(p * pw + hh * RET_DV, p * pw + (hh + 1) * RET_DV)
                o = outs[p][:, hh * RET_DV:(hh + 1) * RET_DV]
                ret_ref[:, cols] = o
                mu = jnp.sum(o, axis=1, keepdims=True) * (1.0 / RET_DV)
                xc = o - mu
                var = jnp.sum(xc * xc, axis=1, keepdims=True) * (1.0 / RET_DV)
                nrm = xc * lax.rsqrt(var + EPS) * w_ref[:, cols]
                g = g_ref[:, cols].astype(F32)
                rg_ref[:, cols] = (g * _sigmoid(g) * nrm).astype(BF16)

    blk = lambda hg, i: (i, hg)
    return _call("ret_fwd", body, (n_pair // RET_PAIRS, s // t),
                 [(log_gamma, None, pltpu.SMEM),
                  (qk_rot, (t, wq), blk),
                  (qk_rot, (t, wq), lambda hg, i: (i, n_pair // RET_PAIRS + hg)),
                  (v_bf, (t, wv), blk),
                  (proj, (t, wv), lambda hg, i: (i, gate_off + hg)),
                  (gn_g, (1, wv), lambda hg, i: (0, hg))],
                 [((s, RET_V), F32, (t, wv), blk), ((s, RET_V), BF16, (t, wv), blk)],
                 scratch=[pltpu.VMEM((RET_PAIRS, LANES, pw), F32)], riders=riders)


def _tri(tk, strict_upper):
    r = lax.broadcasted_iota(jnp.int32, (tk, tk), 0)
    cc = lax.broadcasted_iota(jnp.int32, (tk, tk), 1)
    return ((r > cc) if strict_upper else (r < cc)).astype(BF16)


def _diagonal_step(i, tq, tk, make, carry):
    if (tq // tk) % SB_GROUP == 0:
        return make(SB_GROUP)(0, carry)
    assert 2 * (tq // tk) == SB_GROUP
    half = lax.rem(i, 2) == 0
    return lax.cond(half, lambda cr: make(SB_GROUP // 2)(0, cr), lambda cr: make(SB_GROUP)(0, cr), carry)


def _sb_valid(i, j, tq, tk):
    row, col = _tile_pos(i, j, tq, tk)
    return col < row


def _sb_fwd(qkv, tq, tk, riders=None):
    s = qkv.shape[0]
    n_pair = HEADS // 2
    _check_tiles(s, tq, tk, SB_GROUP)

    def body(q_ref, k_ref, v_ref, o_ref, a_ref):
        i = pl.program_id(1)
        upper = _tri(tk, True)
        qs = _stack_heads(q_ref[...])
        n_full, n_groups = _n_full(i, tq, tk, SB_GROUP), _n_groups(i, tq, tk, SB_GROUP)

        def make_step(near_diagonal, last, n_sub=SB_GROUP):
            def step(n, carry):
                c, o = carry
                g = last - 1 - n
                js = [g * SB_GROUP + sub for sub in range(n_sub)]
                zs = [_dot(qs, k_ref[_key_rows(j, tk), :], _NT) for j in js]
                log1ps = [jnp.log2(1.0 + jnp.exp2(-jnp.abs(z))) for z in zs]
                log_1ms = [-jnp.maximum(z, 0.0) - t for z, t in zip(zs, log1ps)]
                log_bs = [jnp.minimum(z, 0.0) - t for z, t in zip(zs, log1ps)]
                if near_diagonal:
                    valids = [_sb_valid(i, j, tq, tk) for j in js]
                    log_1ms = [jnp.where(v, l, 0.0) for v, l in zip(valids, log_1ms)]
                sticks = [_dot(l, upper) for l in log_1ms]
                sums = [jnp.sum(l, axis=1, keepdims=True) for l in log_1ms]
                cs = [None] * n_sub
                for sub in reversed(range(n_sub)):
                    cs[sub] = c
                    c = c + sums[sub]
                for sub, j in enumerate(js):
                    a = jnp.exp2(log_bs[sub] + sticks[sub] + cs[sub])
                    if near_diagonal:
                        a = jnp.where(valids[sub], a, 0.0)
                    a = a.astype(BF16)
                    a_ref[j] = a
                    o = o + _dot(_side_by_side(a, tq), _stack_heads(v_ref[_key_rows(j, tk), :]))
                return c, o
            return step

        carry = (jnp.zeros((2 * tq, 1), F32), jnp.zeros((tq, LANES), F32))
        carry = _diagonal_step(i, tq, tk, lambda n_sub: make_step(True, n_groups, n_sub), carry)
        _, acc = lax.fori_loop(0, n_full, make_step(False, n_full), carry)
        o_ref[...] = acc.astype(BF16)

    n_kb = s // tk
    return _call("sb_fwd", body, (n_pair, s // tq),
                 [(qkv, (tq, LANES), lambda hp, i: (i, hp)),
                  (qkv, (s, LANES), lambda hp, i: (0, n_pair + hp)),
                  (qkv, (s, LANES), lambda hp, i: (0, 2 * n_pair + hp))],
                 [((s, SB_W), BF16, (tq, LANES), lambda hp, i: (i, hp)),
                  ((n_pair, s // tq, n_kb, 2 * tq, tk), BF16, (None, None, n_kb, 2 * tq, tk),
                   lambda hp, i: (hp, i, 0, 0, 0))], riders=riders)


def _merge(retg, sb, w_ret, w_sb_t, proj, tm, tn, riders=None):
    s, d = retg.shape[0], w_ret.shape[1]
    ar_off = (2 * RET_QK + 2 * RET_V + 3 * SB_W) // tn
    as_off = ar_off + d // tn

    def body(rg_ref, sb_ref, wr_ref, ws_ref, ar_ref, as_ref, mix_ref, r_ref, s_ref):
        rr = _dot(rg_ref[...], wr_ref[...])
        ss = _dot(sb_ref[...], ws_ref[...], _NT)
        mix_ref[...] = (_sigmoid(ar_ref[...].astype(F32)) * rr + _sigmoid(as_ref[...].astype(F32)) * ss).astype(BF16)
        r_ref[...] = rr.astype(BF16)
        s_ref[...] = ss.astype(BF16)

    tile = (tm, tn)
    return _call("merge", body, (d // tn, s // tm),
                 [(retg, (tm, RET_V), lambda j, i: (i, 0)), (sb, (tm, SB_W), lambda j, i: (i, 0)),
                  (w_ret, (RET_V, tn), lambda j, i: (0, j)), (w_sb_t, (tn, SB_W), lambda j, i: (j, 0)),
                  (proj, tile, lambda j, i: (i, ar_off + j)), (proj, tile, lambda j, i: (i, as_off + j))],
                 [((s, d), BF16, tile, lambda j, i: (i, j))] * 3, riders=riders)


def _out_proj(mixed, w_out, x, mod, gp1, g2, tm):
    s, d = x.shape

    def body(a_ref, w_ref, x_ref, mod_ref, gp_ref, g2_ref, y_ref, hres_ref, h2_ref):
        for rows in _pieces(tm):
            y = _dot(a_ref[rows, :], w_ref[...])
            y_ref[rows, :] = y
            ny, _ = _rms(y, d)
            hres = x_ref[rows, :] + mod_ref[:, 2 * d:3 * d] * (ny * gp_ref[...])
            hres_ref[rows, :] = hres
            n2, _ = _rms(hres, d)
            h2_ref[rows, :] = (n2 * g2_ref[...] * (1.0 + mod_ref[:, 4 * d:5 * d]) + mod_ref[:, 3 * d:4 * d]).astype(BF16)

    row = lambda i: (i, 0)
    fix = lambda i: (0, 0)
    return _call("out_proj", body, (s // tm,),
                 [(mixed, (tm, d), row), (w_out, (d, d), fix), (x, (tm, d), row),
                  (mod, (1, 6 * d), fix), (gp1, (1, d), fix), (g2, (1, d), fix)],
                 [((s, d), F32, (tm, d), row), ((s, d), F32, (tm, d), row), ((s, d), BF16, (tm, d), row)])


def _ff1(h2, w_ff1_t, tm, tn):
    s, f = h2.shape[0], w_ff1_t.shape[0]
    tm = min(tm, s)

    def body(a_ref, w_ref, u_ref, act_ref):
        u = _dot(a_ref[...], w_ref[...], _NT)
        r = jnp.maximum(u, 0.0)
        u_ref[...] = u.astype(BF16)
        act_ref[...] = (r * r).astype(BF16)

    d = h2.shape[1]
    return _call("ff1", body, (f // tn, s // tm),
                 [(h2, (tm, d), lambda j, i: (i, 0)), (w_ff1_t, (tn, d), lambda j, i: (j, 0))],
                 [((s, f), BF16, (tm, tn), lambda j, i: (i, j))] * 2)


def _ff2_loss(act, w_ff2, hres, target, mod, gp2, tm):
    s, d = hres.shape
    f = act.shape[1]

    def body(a_ref, w_ref, h_ref, t_ref, mod_ref, gp_ref, dout_ref, df_ref, loss_ref, dgt_ref, dgp_ref):
        _zero_at_start([loss_ref, dgt_ref, dgp_ref])
        gt, gp = mod_ref[:, 5 * d:6 * d], gp_ref[...]
        for rows in _pieces(tm):
            ff = _dot(a_ref[rows, :], w_ref[...])
            nf, rf = _rms(ff, d)
            out = h_ref[rows, :] + gt * (nf * gp)
            err = out - t_ref[rows, :]
            sq = jnp.sum(err * err, axis=1, keepdims=True)
            loss_ref[...] += jnp.sum(sq, axis=0, keepdims=True)
            dout = err * (1.0 / d)
            dout_ref[rows, :] = dout
            dgt_ref[...] += _colsum(dout * (nf * gp))
            dgp_ref[...] += _colsum(dout * gt * nf)
            df_ref[rows, :] = _rms_bwd(dout * gt * gp, nf, rf, d).astype(BF16)

    row = lambda i: (i, 0)
    fix = lambda i: (0, 0)
    return _call("ff2_loss", body, (s // tm,),
                 [(act, (tm, f), row), (w_ff2, (f, d), fix), (hres, (tm, d), row), (target, (tm, d), row),
                  (mod, (1, 6 * d), fix), (gp2, (1, d), fix)],
                 [((s, d), F32, (tm, d), row), ((s, d), BF16, (tm, d), row), ((1, 1), F32, (1, 1), fix),
                  ((1, d), F32, (1, d), fix), ((1, d), F32, (1, d), fix)])


def _ff2_bwd(df, w_ff2, u, tm, tn):
    s, d = df.shape
    f = w_ff2.shape[0]
    tm = min(tm, s)

    def body(a_ref, w_ref, u_ref, du_ref):
        da = _dot(a_ref[...], w_ref[...], _NT)
        du_ref[...] = (da * (2.0 * jnp.maximum(u_ref[...].astype(F32), 0.0))).astype(BF16)

    return _call("ff2_bwd", body, (f // tn, s // tm),
                 [(df, (tm, d), lambda j, i: (i, 0)), (w_ff2, (tn, d), lambda j, i: (j, 0)),
                  (u, (tm, tn), lambda j, i: (i, j))],
                 [((s, f), BF16, (tm, tn), lambda j, i: (i, j))])[0]


def _ff1_bwd(du, w_ff1_t, hres, dout, y, mod, g2, gp1, tm, riders=None):
    s, d = hres.shape
    f = du.shape[1]

    def body(a_ref, w_ref, h_ref, do_ref, y_ref, mod_ref, g2_ref, gp_ref,
             dh_ref, dy_ref, dsh_ref, dsc_ref, dg2_ref, dgt_ref, dgp_ref):
        _zero_at_start([dsh_ref, dsc_ref, dg2_ref, dgt_ref, dgp_ref])
        g2, sc2 = g2_ref[...], mod_ref[:, 4 * d:5 * d]
        gt, gp = mod_ref[:, 2 * d:3 * d], gp_ref[...]
        for rows in _pieces(tm):
            dh2 = _dot(a_ref[rows, :], w_ref[...])
            n2, r2 = _rms(h_ref[rows, :], d)
            dsh_ref[...] += _colsum(dh2)
            dsc_ref[...] += _colsum(dh2 * n2 * g2)
            dg2_ref[...] += _colsum(dh2 * n2 * (1.0 + sc2))
            dhres = do_ref[rows, :] + _rms_bwd(dh2 * g2 * (1.0 + sc2), n2, r2, d)
            dh_ref[rows, :] = dhres
            ny, ry = _rms(y_ref[rows, :], d)
            dgt_ref[...] += _colsum(dhres * (ny * gp))
            dgp_ref[...] += _colsum(dhres * gt * ny)
            dy_ref[rows, :] = _rms_bwd(dhres * gt * gp, ny, ry, d).astype(BF16)

    row = lambda i: (i, 0)
    fix = lambda i: (0, 0)
    vec = ((1, d), F32, (1, d), fix)
    return _call("ff1_bwd", body, (s // tm,),
                 [(du, (tm, f), row), (w_ff1_t, (f, d), fix), (hres, (tm, d), row), (dout, (tm, d), row),
                  (y, (tm, d), row), (mod, (1, 6 * d), fix), (g2, (1, d), fix), (gp1, (1, d), fix)],
                 [((s, d), F32, (tm, d), row), ((s, d), BF16, (tm, d), row), vec, vec, vec, vec, vec], riders=riders)


def _out_bwd(dy, w_out, proj, r_bf, s_bf, tm, tn, riders=None):
    s, d = dy.shape
    ar_off = (2 * RET_QK + 2 * RET_V + 3 * SB_W) // tn
    as_off = ar_off + d // tn

    def body(a_ref, w_ref, ar_ref, as_ref, r_ref, s_ref, dr_ref, ds_ref, dar_ref, das_ref):
        dm = _dot(a_ref[...], w_ref[...], _NT)
        sr, ss = _sigmoid(ar_ref[...].astype(F32)), _sigmoid(as_ref[...].astype(F32))
        dr_ref[...] = (dm * sr).astype(BF16)
        ds_ref[...] = (dm * ss).astype(BF16)
        dar_ref[...] = (dm * r_ref[...].astype(F32) * sr * (1.0 - sr)).astype(BF16)
        das_ref[...] = (dm * s_ref[...].astype(F32) * ss * (1.0 - ss)).astype(BF16)

    tile = (tm, tn)
    here = lambda j, i: (i, j)
    return _call("out_bwd", body, (d // tn, s // tm),
                 [(dy, (tm, d), lambda j, i: (i, 0)), (w_out, (tn, d), lambda j, i: (j, 0)),
                  (proj, tile, lambda j, i: (i, ar_off + j)), (proj, tile, lambda j, i: (i, as_off + j)),
                  (r_bf, tile, here), (s_bf, tile, here)],
                 [((s, d), BF16, tile, here)] * 4, riders=riders)


def _gn_bwd(dretg, ret, proj, gn_g, tm, riders=None):
    s = ret.shape[0]
    gate_off = (2 * RET_QK + RET_V) // RET_V

    def body(d_ref, r_ref, g_ref, w_ref, dg_ref, dret_ref, dw_ref):
        first = pl.program_id(0) == 0
        for h in range(HEADS):
            cols = slice(h * RET_DV, (h + 1) * RET_DV)
            o, g, w, dr = r_ref[:, cols], g_ref[:, cols].astype(F32), w_ref[:, cols], d_ref[:, cols].astype(F32)
            mu = jnp.sum(o, axis=1, keepdims=True) * (1.0 / RET_DV)
            xc = o - mu
            rstd = lax.rsqrt(jnp.sum(xc * xc, axis=1, keepdims=True) * (1.0 / RET_DV) + EPS)
            n = xc * rstd
            sg = _sigmoid(g)
            silu = g * sg
            dg_ref[:, cols] = (dr * n * w * (sg * (1.0 + g * (1.0 - sg)))).astype(BF16)
            _accum(dw_ref.at[:, cols], _colsum(dr * silu * n), first)
            dn = dr * silu * w
            m1 = jnp.sum(dn, axis=1, keepdims=True) * (1.0 / RET_DV)
            m2 = jnp.sum(dn * n, axis=1, keepdims=True) * (1.0 / RET_DV)
            dret_ref[:, cols] = (rstd * (dn - m1 - n * m2)).astype(BF16)

    row = lambda i: (i, 0)
    fix = lambda i: (0, 0)
    return _call("gn_bwd", body, (s // tm,),
                 [(dretg, (tm, RET_V), row), (ret, (tm, RET_V), row),
                  (proj, (tm, RET_V), lambda i: (i, gate_off)), (gn_g, (1, RET_V), fix)],
                 [((s, RET_V), BF16, (tm, RET_V), row), ((s, RET_V), BF16, (tm, RET_V), row),
                  ((1, RET_V), F32, (1, RET_V), fix)], riders=riders)


def _ret_bwd(qk_rot, v_bf, dret, log_gamma, t, riders=None):
    s = qk_rot.shape[0]
    n_pair = HEADS // 2
    pw = 2 * RET_DV
    wq, wv = RET_PAIRS * LANES, RET_PAIRS * pw
    n_blk = s // t
    pairs = range(RET_PAIRS)

    def load(q_ref, k_ref, v_ref, do_ref):
        return ([_lanes(q_ref, p, LANES) for p in pairs], [_lanes(k_ref, p, LANES) for p in pairs],
                [_lanes(v_ref, p, pw) for p in pairs], [_lanes(do_ref, p, pw) for p in pairs])

    def d_scores(lg_ref, hp, i, qb, kb, vb, dob):
        z, w = _ret_block(lg_ref, hp, i, t, qb, kb)
        dp = jnp.concatenate([_dot(dob[:, 0:RET_DV], vb[:, 0:RET_DV], _NT),
                              _dot(dob[:, RET_DV:pw], vb[:, RET_DV:pw], _NT)], axis=0)
        return (z * w).astype(BF16), (dp * w).astype(BF16)

    def up_body(lg_ref, q_ref, k_ref, v_ref, do_ref, dq_ref, state_ref):
        hg, i = pl.program_id(0), pl.program_id(1)

        @pl.when(i == 0)
        def _():
            state_ref[...] = jnp.zeros_like(state_ref)

        qbs, kbs, vbs, dobs = load(q_ref, k_ref, v_ref, do_ref)
        dss = [d_scores(lg_ref, hg * RET_PAIRS + p, i, qbs[p], kbs[p], vbs[p], dobs[p])[1] for p in pairs]
        for p in pairs:
            dq_ref[:, p * LANES:(p + 1) * LANES] = (_dot(_side_by_side(dss[p], t), _stack_heads(kbs[p]))
                                                    + _dot(dobs[p], state_ref[p], _NT)).astype(BF16)
        for p in pairs:
            state_ref[p] += _pair_mask() * _dot(kbs[p], vbs[p], _TN)

    def down_body(lg_ref, q_ref, k_ref, v_ref, do_ref, dk_ref, dv_ref, state_ref):
        hg, i = pl.program_id(0), n_blk - 1 - pl.program_id(1)

        @pl.when(pl.program_id(1) == 0)
        def _():
            state_ref[...] = jnp.zeros_like(state_ref)

        qbs, kbs, vbs, dobs = load(q_ref, k_ref, v_ref, do_ref)
        both = [d_scores(lg_ref, hg * RET_PAIRS + p, i, qbs[p], kbs[p], vbs[p], dobs[p]) for p in pairs]
        for p in pairs:
            pp, ds = both[p]
            later = state_ref[p]
            dv_ref[:, p * pw:(p + 1) * pw] = (jnp.concatenate(
                [_dot(pp[:t], dobs[p][:, 0:RET_DV], _TN), _dot(pp[t:], dobs[p][:, RET_DV:pw], _TN)],
                axis=1) + _dot(kbs[p], later)).astype(BF16)
            dk_ref[:, p * LANES:(p + 1) * LANES] = (_dot(ds, _stack_heads(qbs[p]), _TN)
                                                    + _dot(vbs[p], later, _NT)).astype(BF16)
        for p in pairs:
            state_ref[p] += _pair_mask() * _dot(qbs[p], dobs[p], _TN)

    n_grp = n_pair // RET_PAIRS

    def ins(order):
        return [(log_gamma, None, pltpu.SMEM),
                (qk_rot, (t, wq), lambda hg, i: (order(i), hg)),
                (qk_rot, (t, wq), lambda hg, i: (order(i), n_grp + hg)),
                (v_bf, (t, wv), lambda hg, i: (order(i), hg)),
                (dret, (t, wv), lambda hg, i: (order(i), hg))]

    up = lambda i: i
    down = lambda i: n_blk - 1 - i
    scratch = [pltpu.VMEM((RET_PAIRS, LANES, pw), F32)]
    dq = _call("ret_bwd_q", up_body, (n_grp, n_blk), ins(up),
               [((s, RET_QK), BF16, (t, wq), lambda hg, i: (i, hg))], scratch=scratch)[0]
    dk, dv, *rest = _call("ret_bwd_kv", down_body, (n_grp, n_blk), ins(down),
                          [((s, RET_QK), BF16, (t, wq), lambda hg, i: (down(i), hg)),
                           ((s, RET_V), BF16, (t, wv), lambda hg, i: (down(i), hg))],
                          scratch=scratch, riders=riders)
    return [dq, dk, dv] + rest


def _sb_bwd(qkv, weights, do, tq, tk, riders=None):
    s = qkv.shape[0]
    n_pair = HEADS // 2
    _check_tiles(s, tq, tk, SB_GROUP)

    def body(q_ref, k_ref, v_ref, a_ref, do_ref, dq_ref, dk_ref, dv_ref):
        i = pl.program_id(1)

        @pl.when(i == 0)
        def _():
            dk_ref[...] = jnp.zeros_like(dk_ref)
            dv_ref[...] = jnp.zeros_like(dv_ref)

        lower = _tri(tk, False)
        qs = _stack_heads(q_ref[...])
        dos = _stack_heads(do_ref[...].astype(BF16))

        def make_step(near_diagonal, n_sub=SB_GROUP):
            def step(g, carry):
                c_e, dq = carry
                js = [g * SB_GROUP + sub for sub in range(n_sub)]
                rows = [_key_rows(j, tk) for j in js]
                zs = [_dot(qs, k_ref[rw, :], _NT) for rw in rows]
                das = [_dot(dos, v_ref[rw, :], _NT) for rw in rows]
                avals = [a_ref[j] for j in js]
                for a, rw in zip(avals, rows):
                    dv_ref[rw, :] += _dot(a, dos, _TN)
                es = [a.astype(F32) * da for a, da in zip(avals, das)]
                prefixes = [_dot(e, lower) for e in es]
                betas = [1.0 / (1.0 + jnp.exp2(-z)) for z in zs]
                for sub in range(n_sub):
                    dz = es[sub] - (es[sub] + prefixes[sub] + c_e) * betas[sub]
                    if near_diagonal:
                        dz = jnp.where(_sb_valid(i, js[sub], tq, tk), dz, 0.0)
                    dz = dz.astype(BF16)
                    dk_ref[rows[sub], :] += _dot(dz, qs, _TN)
                    dq = dq + _dot(_side_by_side(dz, tq), _stack_heads(k_ref[rows[sub], :]))
                    c_e = c_e + jnp.sum(es[sub], axis=1, keepdims=True)
                return c_e, dq
            return step

        n_full = _n_full(i, tq, tk, SB_GROUP)
        carry = (jnp.zeros((2 * tq, 1), F32), jnp.zeros((tq, LANES), F32))
        carry = lax.fori_loop(0, n_full, make_step(False), carry)
        _, dq = _diagonal_step(i, tq, tk, lambda n_sub: (lambda n, cr: make_step(True, n_sub)(n_full, cr)), carry)
        dq_ref[...] = dq

    blk = lambda hp, i: (i, hp)
    n_kb = s // tk
    return _call("sb_bwd", body, (n_pair, s // tq),
                 [(qkv, (tq, LANES), blk),
                  (qkv, (s, LANES), lambda hp, i: (0, n_pair + hp)),
                  (qkv, (s, LANES), lambda hp, i: (0, 2 * n_pair + hp)),
                  (weights, (None, None, n_kb, 2 * tq, tk), lambda hp, i: (hp, i, 0, 0, 0)),
                  (do, (tq, LANES), blk)],
                 [((s, SB_W), F32, (tq, LANES), blk),
                  ((s, SB_W), F32, (s, LANES), lambda hp, i: (0, hp)),
                  ((s, SB_W), F32, (s, LANES), lambda hp, i: (0, hp))], riders=riders)


def _assemble_dproj(dq_r, dk_r, dv_r, dg_r, dq_s, dk_s, dv_s, da_r, da_s, cos, sin, idx_col, lg_lanes, tm, riders=None):
    s, d = da_r.shape
    width = 2 * RET_QK + 2 * RET_V + 3 * SB_W + 2 * d

    def body(dq_ref, dk_ref, dv_ref, dg_ref, dqs_ref, dks_ref, dvs_ref, dar_ref, das_ref, cos_ref, sin_ref,
             idx_ref, lg_ref, o_ref):
        lane = lax.broadcasted_iota(jnp.int32, (1, LANES), 1)
        first = jnp.bitwise_and(lane, RET_DQK - 1) < (RET_DQK // 2)
        cos, sin = cos_ref[...], sin_ref[...]
        idx = idx_ref[...]
        for src, base, sign, scale in ((dq_ref, 0, 1.0, 1.0), (dk_ref, RET_QK, -1.0, RET_DQK ** -0.5)):
            for g in range(RET_QK // LANES):
                v = src[:, g * LANES:(g + 1) * LANES].astype(F32) * (_decay_scale(lg_ref, idx, g, sign) * scale)
                sw = jnp.where(first, pltpu.roll(v, LANES - RET_DQK // 2, 1), pltpu.roll(v, RET_DQK // 2, 1))
                o_ref[:, base + g * LANES:base + (g + 1) * LANES] = (v * cos - sw * sin).astype(BF16)
        off = 2 * RET_QK
        o_ref[:, off:off + RET_V] = dv_ref[...].astype(BF16)
        off += RET_V
        o_ref[:, off:off + RET_V] = dg_ref[...]
        off += RET_V
        o_ref[:, off:off + SB_W] = (dqs_ref[...] * (SB_DH ** -0.5)).astype(BF16)
        off += SB_W
        o_ref[:, off:off + SB_W] = (dks_ref[...] * LN2).astype(BF16)
        off += SB_W
        o_ref[:, off:off + SB_W] = dvs_ref[...].astype(BF16)
        off += SB_W
        o_ref[:, off:off + d] = dar_ref[...]
        off += d
        o_ref[:, off:off + d] = das_ref[...]

    row = lambda i: (i, 0)
    ins = [(a, (tm, a.shape[1]), row) for a in (dq_r, dk_r, dv_r, dg_r, dq_s, dk_s, dv_s, da_r, da_s, cos, sin, idx_col)]
    ins.append((lg_lanes, (1, RET_QK), lambda i: (0, 0)))
    return _call("assemble_dproj", body, (s // tm,), ins, [((s, width), BF16, (tm, width), row)], riders=riders)


def _in_bwd(dproj, w_in_t, x, dhres, mod, g1, tm, riders=None):
    s, d = x.shape
    width = dproj.shape[1]

    def body(a_ref, w_ref, x_ref, dh_ref, mod_ref, g_ref, dx_ref, dsh_ref, dsc_ref, dg_ref):
        _zero_at_start([dsh_ref, dsc_ref, dg_ref])
        g1, sc1 = g_ref[...], mod_ref[:, d:2 * d]
        for rows in _pieces(tm):
            dh = _dot(a_ref[rows, :], w_ref[...])
            n1, r1 = _rms(x_ref[rows, :], d)
            dsh_ref[...] += _colsum(dh)
            dsc_ref[...] += _colsum(dh * n1 * g1)
            dg_ref[...] += _colsum(dh * n1 * (1.0 + sc1))
            dx_ref[rows, :] = dh_ref[rows, :] + _rms_bwd(dh * g1 * (1.0 + sc1), n1, r1, d)

    row = lambda i: (i, 0)
    fix = lambda i: (0, 0)
    vec = ((1, d), F32, (1, d), fix)
    return _call("in_bwd", body, (s // tm,),
                 [(dproj, (tm, width), row), (w_in_t, (width, d), fix), (x, (tm, d), row), (dhres, (tm, d), row),
                  (mod, (1, 6 * d), fix), (g1, (1, d), fix)],
                 [((s, d), F32, (tm, d), row), vec, vec, vec], riders=riders)


def _adamw(w, g, m, v):
    m = ADAM_B1 * m + (1.0 - ADAM_B1) * g
    v = ADAM_B2 * v + (1.0 - ADAM_B2) * (g * g)
    m_hat = m / (1.0 - ADAM_B1 ** ADAM_STEP)
    v_hat = v / (1.0 - ADAM_B2 ** ADAM_STEP)
    delta = -ADAM_LR * (m_hat / (jnp.sqrt(v_hat) + ADAM_EPS) + ADAM_WD * w)
    return delta, m, v


def _adam_reduce(name, parts, w, m, v, tr):
    rws, cls = w.shape
    tr = min(tr, rws)
    n_parts = parts.shape[0]

    def body(p_ref, w_ref, m_ref, v_ref, g_out, d_out, m_out, v_out):
        g = p_ref[0].astype(F32)
        for k in range(1, n_parts):
            g = g + p_ref[k].astype(F32)
        delta, mn, vn = _adamw(w_ref[...], g, m_ref[...], v_ref[...])
        g_out[...] = g
        d_out[...] = delta
        m_out[...] = mn
        v_out[...] = vn

    row = lambda i: (i, 0)
    blk = (tr, cls)
    return _call(name, body, (rws // tr,),
                 [(parts, (n_parts, tr, cls), lambda i: (0, i, 0)), (w, blk, row), (m, blk, row), (v, blk, row)],
                 [((rws, cls), F32, blk, row)] * 4)


def _ada_bwd_adam(cs_t, dmod_cols, w, m, v):
    d, nc = w.shape

    def body(c_ref, dm_ref, w_ref, m_ref, v_ref, g_out, d_out, m_out, v_out):
        g = c_ref[0] * dm_ref[0:1, :]
        for r in range(1, N_DEV):
            g = g + c_ref[r] * dm_ref[r:r + 1, :]
        delta, mn, vn = _adamw(w_ref[...], g, m_ref[...], v_ref[...])
        g_out[...] = g
        d_out[...] = delta
        m_out[...] = mn
        v_out[...] = vn

    fix = lambda i: (0, 0)
    blk = (d, nc)
    return _call("ada_bwd_adam", body, (1,),
                 [(cs_t, (N_DEV, d, 1), lambda i: (0, 0, 0)), (dmod_cols, (N_DEV, nc), fix), (w, blk, fix), (m, blk, fix), (v, blk, fix)],
                 [((d, nc), F32, blk, fix)] * 4)


def _small_adam(parts, w, m, v):
    n = w.shape[1]

    def body(p_ref, w_ref, m_ref, v_ref, g_out, d_out, m_out, v_out):
        g = p_ref[0:1, :]
        for k in range(1, N_DEV):
            g = g + p_ref[k:k + 1, :]
        delta, mn, vn = _adamw(w_ref[...], g, m_ref[...], v_ref[...])
        g_out[...] = g
        d_out[...] = delta
        m_out[...] = mn
        v_out[...] = vn

    fix = lambda i: (0, 0)
    return _call("small_adam", body, (1,),
                 [(parts, (N_DEV, n), fix), (w, (1, n), fix), (m, (1, n), fix), (v, (1, n), fix)],
                 [((1, n), F32, (1, n), fix)] * 4)


def kernel(x, c, positions, ada_w, ada_b, pre_mix_g, post_mix_g, pre_ffn_g, post_ffn_g, w_in, ret_gn_g, w_ret_branch, w_sb_branch, w_out, w_ff1, w_ff2, loss_target, m_ada_w, m_ada_b, m_pre_mix_g, m_post_mix_g, m_pre_ffn_g, m_post_ffn_g, m_w_in, m_ret_gn_g, m_w_ret_branch, m_w_sb_branch, m_w_out, m_w_ff1, m_w_ff2, v_ada_w, v_ada_b, v_pre_mix_g, v_post_mix_g, v_pre_ffn_g, v_post_ffn_g, v_w_in, v_ret_gn_g, v_w_ret_branch, v_w_sb_branch, v_w_out, v_w_ff1, v_w_ff2):
    _, s, d = x.shape
    d_ff = w_ff1.shape[2] * N_DEV
    d_in = w_in.shape[2] * N_DEV
    me = 4 * lax.axis_index("x") + 2 * lax.axis_index("y") + lax.axis_index("c")
    x2, tgt = x[0], loss_target[0]

    core = lax.axis_index("c").astype(jnp.int32).reshape(1)
    bf = lambda w: w[0].astype(BF16)

    w_in_t, m_in_t, v_in_t = (jnp.swapaxes(a[0], 0, 1) for a in (w_in, m_w_in, v_w_in))

    c_all, g_in = _exchange("gather_in", [c, w_in_t.astype(BF16)], ["gather", "gather_chip"])
    c_all = c_all.reshape(N_DEV, d)

    n_ada = ada_w.shape[2]
    cs_all = _silu_rows(c_all)
    ada_b_cols = lax.dynamic_slice(ada_b, (0, me * n_ada), (1, n_ada))
    mod_cols = _ada_fwd(cs_all, ada_w[0], ada_b_cols)
    mod_all = _exchange("gather_mod", [mod_cols], ["gather"])[0]
    mod = lax.dynamic_index_in_dim(mod_all, me, axis=1, keepdims=False).reshape(1, 6 * d)

    tm = min(256, s)
    h, g_in = _pre_norm(x2, pre_mix_g, mod, 2 * tm, riders=([g_in], ["forward"]))
    wt_in = g_in.reshape(d_in, d)
    bf_t = lambda w: jnp.swapaxes(w[0], 0, 1).astype(BF16)
    small_w = [bf(w_ret_branch), bf_t(w_sb_branch), bf(w_out)]
    proj, *small_w = _matmul("in_proj", h, wt_in, "nt", s, 512, BF16, riders=(small_w, ["gather_chip"] * 3))
    pos_col = positions.reshape(s, 1).astype(F32)
    freqs = ROPE_BASE ** (-jnp.arange(0, RET_DQK, 2, dtype=F32) / RET_DQK)
    inv_freq = jnp.tile(freqs, LANES // (RET_DQK // 2)).reshape(1, LANES)
    log_gamma_np = np.log1p(-(2.0 ** (-5.0 - np.arange(HEADS))))
    log_gamma = jnp.asarray(log_gamma_np, F32)
    lg_lanes = jnp.asarray(np.repeat(log_gamma_np, RET_DQK).reshape(1, RET_QK), F32)
    idx_col = (jnp.arange(s, dtype=F32) - (s // 2)).reshape(s, 1)
    qk_rot, v_bf, qkv_sb, cos_t, sin_t = _prep(proj, pos_col, idx_col, inv_freq, lg_lanes, 2 * tm)
    tq, tk = min(256, s), min(128, s)
    tq_sb = min(SB_TQ, s)
    sb, sb_weights, *big_w = _sb_fwd(qkv_sb, tq_sb, tk, riders=([bf(w_ff2), bf_t(w_ff1)], ["gather_chip"] * 2))
    ret, retg, g_ret, g_sb, g_out, g_ff2, g_ff1 = _ret_fwd(qk_rot, v_bf, proj, ret_gn_g, log_gamma, tq,
                                                           riders=(small_w + big_w, ["forward"] * 5))
    wf_ret = g_ret.reshape(RET_V, d)
    wt_sb = g_sb.reshape(d, SB_W)
    wf_out = g_out.reshape(d, d)
    wt_ff1 = g_ff1.reshape(d_ff, d)
    wf_ff2 = g_ff2.reshape(d_ff, d)
    mixed, r_bf, s_bf = _merge(retg, sb, wf_ret, wt_sb, proj, tm, min(512, d))
    y, hres, h2 = _out_proj(mixed, wf_out, x2, mod, post_mix_g, pre_ffn_g, tm)
    u, act = _ff1(h2, wt_ff1, s, 512)
    dout, df, loss_sum, d_gt2, d_gp2 = _ff2_loss(act, wf_ff2, hres, tgt, mod, post_ffn_g, tm)

    du = _ff2_bwd(df, wf_ff2, u, s, 512)
    gw_ff2 = _matmul("grad_w_ff2", act, df, "tn", 512, d, BF16).reshape(N_DEV, d_ff // N_DEV, d)
    gw_ff1 = _matmul("grad_w_ff1", h2, du, "tn", d, d_ff // N_DEV, BF16, blocked_out=True)
    dhres, dy, d_sh2, d_sc2, d_g2, d_gt1, d_gp1, t_ff1, t_ff2 = _ff1_bwd(
        du, wt_ff1, hres, dout, y, mod, pre_ffn_g, post_mix_g, tm, riders=([gw_ff1, gw_ff2], ["pair"] * 2))
    s_ff1 = _pair_sum("pair_sum_ff1", gw_ff1, t_ff1, core, d)
    s_ff2 = _pair_sum("pair_sum_ff2", gw_ff2, t_ff2, core, d)
    gw_out = _matmul("grad_w_out", mixed, dy, "tn", 512, d, BF16).reshape(N_DEV, d // N_DEV, d)
    d_r, d_s, da_r, da_s, p_out = _out_bwd(dy, wf_out, proj, r_bf, s_bf, tm, min(512, d), riders=([gw_out], ["scatter"]))
    dretg = _matmul("ret_branch_bwd", d_r, wf_ret, "nt", s, 512, BF16)
    dsb = _matmul("sb_branch_bwd", d_s, wt_sb, "nn", s, 512, F32)
    gw_ret = _matmul("grad_w_ret", retg, d_r, "tn", 512, d, BF16).reshape(N_DEV, RET_V // N_DEV, d)
    gw_sb = _matmul("grad_w_sb", sb, d_s, "tn", 512, d // N_DEV, BF16, blocked_out=True)
    dq_s, dk_s, dv_s, p_ff1, p_ff2 = _sb_bwd(qkv_sb, sb_weights, dsb, tq_sb, tk,
                                             riders=([s_ff1, s_ff2], ["chip_scatter"] * 2))
    dg_r, dret, d_gn = _gn_bwd(dretg, ret, proj, ret_gn_g, 2 * tm)
    dq_r, dk_r, dv_r, p_sb = _ret_bwd(qk_rot, v_bf, dret, log_gamma, tq, riders=([gw_sb], ["scatter"]))
    dproj, p_ret = _assemble_dproj(dq_r, dk_r, dv_r, dg_r, dq_s, dk_s, dv_s, da_r, da_s, cos_t, sin_t, idx_col, lg_lanes,
                                   tm, riders=([gw_ret], ["scatter"]))
    gw_in = _matmul("grad_w_in", dproj, h, "tn", 512, d, BF16).reshape(N_DEV, d_in // N_DEV, d)
    t_in = _exchange("pair_in", [gw_in], ["pair"])[0]
    tr_in = d_in // N_DEV // 4
    s_in = _pair_sum("pair_sum_in", gw_in, t_in, core, d_in // N_DEV)
    grad_x, d_sh1, d_sc1, d_g1, p_in = _in_bwd(dproj, wt_in, x2, dhres, mod, pre_mix_g, tm,
                                               riders=([s_in], ["chip_scatter"]))
    loss_lanes = jnp.pad(loss_sum, ((0, 0), (0, LANES - 1)))
    small = jnp.concatenate([d_sh1, d_sc1, d_gt1, d_sh2, d_sc2, d_gt2, d_g1, d_gp1, d_g2, d_gp2, d_gn, loss_lanes], axis=1)
    small_all = _exchange("gather_small", [small], ["gather"])[0].reshape(N_DEV, small.shape[1])
    parts = [p_in, p_ret, p_sb, p_out, p_ff1, p_ff2]

    res = {}
    names = ["w_ret_branch", "w_sb_branch", "w_out", "w_ff1", "w_ff2"]
    ws = [w_ret_branch, w_sb_branch, w_out, w_ff1, w_ff2]
    ms = [m_w_ret_branch, m_w_sb_branch, m_w_out, m_w_ff1, m_w_ff2]
    vs = [v_w_ret_branch, v_w_sb_branch, v_w_out, v_w_ff1, v_w_ff2]
    for nm, p, w, m, v in zip(names, parts[1:], ws, ms, vs):
        res[nm] = [o[None] for o in _adam_reduce("adam_" + nm, p, w[0], m[0], v[0], 256)]
    res["w_in"] = [jnp.swapaxes(o, 0, 1)[None]
                   for o in _adam_reduce("adam_w_in", parts[0], w_in_t, m_in_t, v_in_t, tr_in)]
    dmod_cols = lax.dynamic_slice(small_all, (0, me * n_ada), (N_DEV, n_ada))
    res["ada_w"] = [o[None] for o in _ada_bwd_adam(cs_all.reshape(N_DEV, d, 1), dmod_cols, ada_w[0], m_ada_w[0], v_ada_w[0])]
    vec_names = ["ada_b", "pre_mix_g", "post_mix_g", "pre_ffn_g", "post_ffn_g", "ret_gn_g"]
    cat = lambda xs: jnp.concatenate(xs + [jnp.zeros((1, LANES), F32)], axis=1)
    packed = _small_adam(small_all,
                         cat([ada_b, pre_mix_g, post_mix_g, pre_ffn_g, post_ffn_g, ret_gn_g]),
                         cat([m_ada_b, m_pre_mix_g, m_post_mix_g, m_pre_ffn_g, m_post_ffn_g, m_ret_gn_g]),
                         cat([v_ada_b, v_pre_mix_g, v_post_mix_g, v_pre_ffn_g, v_post_ffn_g, v_ret_gn_g]))
    off = 0
    for nm, width in zip(vec_names, [6 * d, d, d, d, d, RET_V]):
        res[nm] = [p[:, off:off + width] for p in packed]
        off += width

    loss = (0.5 / d) * packed[0][0, off]
    order = ["ada_w", "ada_b", "pre_mix_g", "post_mix_g", "pre_ffn_g", "post_ffn_g", "w_in", "ret_gn_g",
             "w_ret_branch", "w_sb_branch", "w_out", "w_ff1", "w_ff2"]
    outs = [loss, grad_x[None]]
    for k in range(4):
        outs += [res[nm][k] for nm in order]
    return tuple(outs)
```

```python
import functools

import numpy as np
import jax
import jax.numpy as jnp
from jax import lax
from jax.experimental import pallas as pl
from jax.experimental.pallas import tpu as pltpu

F32 = jnp.float32
BF16 = jnp.bfloat16
N_DEV = 8
AXES = ("x", "y", "c")

EPS = 1e-6
CHUNK = 64
CHUNK_SHIFT = 6
HEADS = 8
RET_DQK = 64
RET_DV = 128
SB_DH = 64
RET_QK = HEADS * RET_DQK
RET_V = HEADS * RET_DV
SB_W = HEADS * SB_DH
ROPE_BASE = 10000.0
LANES = 128

ADAM_LR = 0.001
ADAM_B1 = 0.9
ADAM_B2 = 0.999
ADAM_EPS = 1e-08
ADAM_WD = 0.01
ADAM_STEP = 10

VMEM_LIMIT = 56 * 1024 * 1024

_NN = (((1,), (0,)), ((), ()))
_NT = (((1,), (1,)), ((), ()))
_TN = (((0,), (0,)), ((), ()))


def _dot(a, b, dims=_NN):
    if a.dtype != BF16:
        a = a.astype(BF16)
    if b.dtype != BF16:
        b = b.astype(BF16)
    return lax.dot_general(a, b, dims, preferred_element_type=F32)


def _sigmoid(x):
    return 1.0 / (1.0 + jnp.exp(-x))


def _rms(x, d):
    r = lax.rsqrt(jnp.sum(x * x, axis=1, keepdims=True) * (1.0 / d) + EPS)
    return x * r, r


def _rms_bwd(dn, n, r, d):
    return r * (dn - n * (jnp.sum(dn * n, axis=1, keepdims=True) * (1.0 / d)))


def _colsum(v):
    return jnp.sum(v, axis=0, keepdims=True)


def _accum(ref, val, first):
    @pl.when(first)
    def _():
        ref[...] = val

    @pl.when(jnp.logical_not(first))
    def _():
        ref[...] += val


ROW_SPLIT = 2


def _zero_at_start(refs):
    @pl.when(pl.program_id(0) == 0)
    def _():
        for r in refs:
            r[...] = jnp.zeros_like(r)


def _pieces(tm):
    step = tm // ROW_SPLIT
    return [slice(k * step, (k + 1) * step) for k in range(ROW_SPLIT)]


KIND_SLOTS = {"gather": N_DEV, "scatter": N_DEV, "gather_chip": N_DEV, "forward": N_DEV, "pair": N_DEV // 2,
              "chip_scatter": N_DEV // 2}
SEMS_PER_ARRAY = N_DEV - 1


def _exchange_copies(ins, outs, send_sems, recv_sems, local_sems, kinds):
    x, y, c = (lax.axis_index(a) for a in AXES)
    me, chip, sibling = 4 * x + 2 * y + c, 2 * x + y, (x, y, 1 - c)
    mesh_id = pl.DeviceIdType.MESH
    other_chips = []
    for k in range(1, N_DEV // 2):
        px = 1 - x if k & 2 else x
        py = 1 - y if k & 1 else y
        other_chips.append((px, py))
    copies = []
    for i, kind in enumerate(kinds):
        def remote(src, dst, k, to, i=i):
            return pltpu.make_async_remote_copy(
                src_ref=src, dst_ref=dst, send_sem=send_sems.at[i * SEMS_PER_ARRAY + k],
                recv_sem=recv_sems.at[i * SEMS_PER_ARRAY + k], device_id=to, device_id_type=mesh_id)

        if kind in ("gather", "scatter"):
            pick = (lambda ref, d: ref.at[d]) if kind == "scatter" else (lambda ref, d: ref)
            copies.append(pltpu.make_async_copy(pick(ins[i], me), outs[i].at[me], local_sems.at[i]))
            for k in range(1, N_DEV):
                to = (1 - x if k & 4 else x, 1 - y if k & 2 else y, 1 - c if k & 1 else c)
                copies.append(remote(pick(ins[i], 4 * to[0] + 2 * to[1] + to[2]), outs[i].at[me], k - 1, to))
        elif kind == "gather_chip":
            copies.append(pltpu.make_async_copy(ins[i], outs[i].at[me], local_sems.at[i]))
            copies.append(remote(ins[i], outs[i].at[me], 0, sibling))
            for k, (px, py) in enumerate(other_chips):
                copies.append(remote(ins[i], outs[i].at[me], 1 + k, (px, py, c)))
        elif kind == "forward":
            for k, (px, py) in enumerate(other_chips):
                slot = 4 * px + 2 * py + c
                copies.append(remote(outs[i].at[slot], outs[i].at[slot], k, sibling))
        elif kind == "pair":
            for k in range(N_DEV // 2):
                copies.append(remote(ins[i].at[2 * k + 1 - c], outs[i].at[k], k, sibling))
        elif kind == "chip_scatter":
            copies.append(pltpu.make_async_copy(ins[i].at[chip], outs[i].at[chip], local_sems.at[i]))
            for k, (px, py) in enumerate(other_chips):
                copies.append(remote(ins[i].at[2 * px + py], outs[i].at[chip], k, (px, py, c)))
        else:
            raise ValueError(kind)
    return copies


def _exchange_shapes(arrays, kinds):
    shapes = []
    for a, kind in zip(arrays, kinds):
        tail = a.shape if kind in ("gather", "gather_chip") else a.shape[1:]
        shapes.append(jax.ShapeDtypeStruct((KIND_SLOTS[kind],) + tuple(tail), a.dtype))
    return shapes


def _exchange_sems(n):
    return [pltpu.SemaphoreType.DMA((n * SEMS_PER_ARRAY,)), pltpu.SemaphoreType.DMA((n * SEMS_PER_ARRAY,)),
            pltpu.SemaphoreType.DMA((n,))]


def _call(name, body, grid, ins, outs, scratch=(), riders=None, prefetch=None):
    any_spec = pl.BlockSpec(memory_space=pl.ANY)
    in_specs = [pl.BlockSpec(memory_space=im) if bs is None else pl.BlockSpec(bs, im) for _, bs, im in ins]
    out_specs = [pl.BlockSpec(bs, im) for _, _, bs, im in outs]
    out_shape = [jax.ShapeDtypeStruct(s, d) for s, d, _, _ in outs]
    operands = [a for a, _, _ in ins]
    scratch = list(scratch)
    aliases = {}
    n_pre = 0 if prefetch is None else 1
    kernel = functools.partial(body) if prefetch is None else (lambda _, *refs: body(*refs))
    if riders is not None:
        arrays, kinds = riders
        nr, n_in, n_out, n_scr = len(arrays), len(ins), len(outs), len(scratch)

        def kernel(*refs):
            refs = refs[n_pre:]
            own_in, ride_in = refs[:n_in], refs[n_in:n_in + nr]
            own_out = refs[n_in + nr:n_in + nr + n_out]
            ride_out = refs[n_in + nr + n_out:n_in + 2 * nr + n_out]
            own_scr = refs[n_in + 2 * nr + n_out:n_in + 2 * nr + n_out + n_scr]
            sems = refs[n_in + 2 * nr + n_out + n_scr:]
            ids = [pl.program_id(a) for a in range(len(grid))]
            first = functools.reduce(jnp.logical_and, [i == 0 for i in ids])
            last = functools.reduce(jnp.logical_and, [i == g - 1 for i, g in zip(ids, grid)])

            @pl.when(first)
            def _():
                for cp in _exchange_copies(ride_in, ride_out, *sems, kinds):
                    cp.start()

            body(*own_in, *own_out, *own_scr)

            @pl.when(last)
            def _():
                for cp in _exchange_copies(ride_in, ride_out, *sems, kinds):
                    cp.wait()

        in_specs += [any_spec] * nr
        out_specs += [any_spec] * nr
        out_shape += _exchange_shapes(arrays, kinds)
        operands += list(arrays)
        scratch += _exchange_sems(nr)
        aliases = {n_pre + n_in + r: n_out + r for r, kind in enumerate(kinds) if kind == "forward"}
    params = pltpu.CompilerParams(dimension_semantics=("arbitrary",) * len(grid), vmem_limit_bytes=VMEM_LIMIT)
    if prefetch is None:
        return pl.pallas_call(kernel, name=name, grid=grid, in_specs=in_specs, out_specs=out_specs,
                              out_shape=out_shape, scratch_shapes=scratch, input_output_aliases=aliases,
                              compiler_params=params)(*operands)
    grid_spec = pltpu.PrefetchScalarGridSpec(num_scalar_prefetch=1, grid=grid, in_specs=in_specs,
                                             out_specs=out_specs, scratch_shapes=scratch)
    return pl.pallas_call(kernel, name=name, grid_spec=grid_spec, out_shape=out_shape,
                          input_output_aliases=aliases, compiler_params=params)(prefetch, *operands)


def _exchange(name, arrays, kinds):
    n = len(arrays)

    def body(*refs):
        copies = _exchange_copies(refs[:n], refs[n:2 * n], *refs[2 * n:], kinds)
        for cp in copies:
            cp.start()
        for cp in copies:
            cp.wait()

    any_spec = pl.BlockSpec(memory_space=pl.ANY)
    return pl.pallas_call(
        functools.partial(body),
        name=name,
        in_specs=[any_spec] * n,
        out_specs=[any_spec] * n,
        out_shape=_exchange_shapes(arrays, kinds),
        scratch_shapes=_exchange_sems(n),
        input_output_aliases={i: i for i, kind in enumerate(kinds) if kind == "forward"},
    )(*arrays)


def _pair_sum(name, mine, theirs, my_core, tr):
    _, rws, cls = mine.shape
    tr = min(tr, rws)

    def body(a_ref, b_ref, o_ref):
        o_ref[...] = (a_ref[...].astype(F32) + b_ref[...].astype(F32)).astype(o_ref.dtype)

    return _call(name, body, (N_DEV // 2, rws // tr),
                 [(mine, (None, tr, cls), lambda k, r, core: (2 * k + core[0], r, 0)),
                  (theirs, (None, tr, cls), lambda k, r, core: (k, r, 0))],
                 [((N_DEV // 2, rws, cls), mine.dtype, (None, tr, cls), lambda k, r, core: (k, r, 0))],
                 prefetch=my_core)[0]


def _matmul(name, a, b, kind, tm, tn, out_dtype, blocked_out=False, riders=None):
    if kind == "tn":
        kdim, m = a.shape
    else:
        m, kdim = a.shape
    n = b.shape[0] if kind == "nt" else b.shape[1]
    tm, tn = min(tm, m), min(tn, n)
    dims = {"nn": _NN, "nt": _NT, "tn": _TN}[kind]

    def body(a_ref, b_ref, o_ref):
        o_ref[...] = _dot(a_ref[...], b_ref[...], dims).astype(o_ref.dtype)

    a_spec = (a, (kdim, tm), lambda j, i: (0, i)) if kind == "tn" else (a, (tm, kdim), lambda j, i: (i, 0))
    b_spec = (b, (tn, kdim), lambda j, i: (j, 0)) if kind == "nt" else (b, (kdim, tn), lambda j, i: (0, j))
    if blocked_out:
        out = ((n // tn, m, tn), out_dtype, (None, tm, tn), lambda j, i: (j, i, 0))
    else:
        out = ((m, n), out_dtype, (tm, tn), lambda j, i: (i, j))
    res = _call(name, body, (n // tn, m // tm), [a_spec, b_spec], [out], riders=riders)
    return res[0] if riders is None else res


def _ada_fwd(cs_all, ada_w, ada_b_cols):
    def body(c_ref, w_ref, b_ref, o_ref):
        o_ref[...] = lax.dot_general(c_ref[...], w_ref[...], _NN, preferred_element_type=F32,
                                     precision=lax.Precision.HIGHEST) + b_ref[...]

    r, d = cs_all.shape
    nc = ada_w.shape[1]
    return _call("ada_fwd", body, (1,),
                 [(cs_all, (r, d), lambda i: (0, 0)), (ada_w, (d, nc), lambda i: (0, 0)),
                  (ada_b_cols, (1, nc), lambda i: (0, 0))],
                 [((r, nc), F32, (r, nc), lambda i: (0, 0))])[0]


def _silu_rows(c_all):
    def body(c_ref, o_ref):
        v = c_ref[...]
        o_ref[...] = v * _sigmoid(v)

    return _call("silu_c", body, (1,), [(c_all, c_all.shape, lambda i: (0, 0))],
                 [(c_all.shape, F32, c_all.shape, lambda i: (0, 0))])[0]


def _pre_norm(x, g, mod, tm, riders=None):
    s, d = x.shape

    def body(x_ref, g_ref, mod_ref, h_ref):
        n, _ = _rms(x_ref[...], d)
        sh, sc = mod_ref[:, 0:d], mod_ref[:, d:2 * d]
        h_ref[...] = (n * g_ref[...] * (1.0 + sc) + sh).astype(BF16)

    return _call("pre_norm", body, (s // tm,),
                 [(x, (tm, d), lambda i: (i, 0)), (g, (1, d), lambda i: (0, 0)),
                  (mod, (1, 6 * d), lambda i: (0, 0))],
                 [((s, d), BF16, (tm, d), lambda i: (i, 0))], riders=riders)


LOG2E = 1.4426950408889634
LN2 = 0.6931471805599453


def _decay_scale(lg_ref, idx, g, sign):
    return jnp.exp((sign * idx) * lg_ref[:, g * LANES:(g + 1) * LANES])


def _prep(proj, pos_col, idx_col, inv_freq, lg_lanes, tm):
    s = proj.shape[0]
    sb_off = (2 * RET_QK + 2 * RET_V) // (3 * SB_W)
    n_q = RET_QK // LANES

    def body(qk_ref, v_ref, sb_ref, pos_ref, idx_ref, f_ref, lg_ref, qk_out, v_out, sb_out, cos_out, sin_out):
        ang = pos_ref[...] * f_ref[...]
        lane = lax.broadcasted_iota(jnp.int32, (1, LANES), 1)
        first = jnp.bitwise_and(lane, RET_DQK - 1) < (RET_DQK // 2)
        cos = jnp.cos(ang)
        sin = jnp.where(first, -1.0, 1.0) * jnp.sin(ang)
        cos_out[...] = cos
        sin_out[...] = sin
        idx = idx_ref[...]
        for g in range(2 * n_q):
            v = qk_ref[:, g * LANES:(g + 1) * LANES].astype(F32)
            sw = jnp.where(first, pltpu.roll(v, LANES - RET_DQK // 2, 1), pltpu.roll(v, RET_DQK // 2, 1))
            r = v * cos + sw * sin
            if g < n_q:
                r = r * _decay_scale(lg_ref, idx, g, 1.0)
            else:
                r = r * (_decay_scale(lg_ref, idx, g - n_q, -1.0) * (RET_DQK ** -0.5))
            qk_out[:, g * LANES:(g + 1) * LANES] = r.astype(BF16)
        v_out[...] = v_ref[...].astype(BF16)
        sb_out[:, 0:SB_W] = (sb_ref[:, 0:SB_W].astype(F32) * (SB_DH ** -0.5 * LOG2E)).astype(BF16)
        sb_out[:, SB_W:3 * SB_W] = sb_ref[:, SB_W:3 * SB_W].astype(BF16)

    return _call("prep", body, (s // tm,),
                 [(proj, (tm, 2 * RET_QK), lambda i: (i, 0)),
                  (proj, (tm, RET_V), lambda i: (i, 2 * RET_QK // RET_V)),
                  (proj, (tm, 3 * SB_W), lambda i: (i, sb_off)),
                  (pos_col, (tm, 1), lambda i: (i, 0)),
                  (idx_col, (tm, 1), lambda i: (i, 0)),
                  (inv_freq, (1, LANES), lambda i: (0, 0)),
                  (lg_lanes, (1, RET_QK), lambda i: (0, 0))],
                 [((s, 2 * RET_QK), BF16, (tm, 2 * RET_QK), lambda i: (i, 0)),
                  ((s, RET_V), BF16, (tm, RET_V), lambda i: (i, 0)),
                  ((s, 3 * SB_W), BF16, (tm, 3 * SB_W), lambda i: (i, 0)),
                  ((s, LANES), F32, (tm, LANES), lambda i: (i, 0)),
                  ((s, LANES), F32, (tm, LANES), lambda i: (i, 0))])


def _head_mask(hh):
    lane = lax.broadcasted_iota(jnp.int32, (1, LANES), 1)
    return (lane >= RET_DQK) if hh else (lane < RET_DQK)


def _masked(v, m):
    return jnp.where(m, v, jnp.zeros_like(v))


SB_GROUP = 4
SB_TQ = 256


def _stack_heads(v):
    return jnp.concatenate([_masked(v, _head_mask(0)), _masked(v, _head_mask(1))], axis=0)


def _side_by_side(v, t):
    return jnp.concatenate([v[:t], v[t:]], axis=1)


def _tile_pos(i, j, tq, tk):
    row = jnp.bitwise_and(lax.broadcasted_iota(jnp.int32, (2 * tq, tk), 0), tq - 1) + i * tq
    col = lax.broadcasted_iota(jnp.int32, (2 * tq, tk), 1) + j * tk
    return row, col


def _n_groups(i, tq, tk, grp):
    return ((i + 1) * (tq // tk) + grp - 1) // grp


def _n_full(i, tq, tk, grp):
    return (i * (tq // tk)) // grp


def _key_rows(j, tk):
    return pl.ds(pl.multiple_of(j * tk, tk), tk)


def _ret_weight(lg_rows, i, j, tq, tk):
    row, col = _tile_pos(i, j, tq, tk)
    same = jnp.right_shift(col, CHUNK_SHIFT) == jnp.right_shift(row, CHUNK_SHIFT)
    later = jnp.where(same, jnp.exp((2.0 * lg_rows) * (col - row).astype(F32)), 0.0)
    return jnp.where(col <= row, 1.0, later)


def _lg_rows(lg_ref, hp, tq):
    first = lax.broadcasted_iota(jnp.int32, (2 * tq, 1), 0) < tq
    return jnp.where(first, lg_ref[2 * hp], lg_ref[2 * hp + 1])


def _check_tiles(s, tq, tk, grp):
    assert tq % tk == 0 and tq & (tq - 1) == 0 and tk & (tk - 1) == 0
    assert s % tq == 0 and (s // tk) % grp == 0 and s // tk <= LANES


def _pair_mask():
    r = lax.broadcasted_iota(jnp.int32, (LANES, 2 * RET_DV), 0) >= RET_DQK
    c = lax.broadcasted_iota(jnp.int32, (LANES, 2 * RET_DV), 1) >= RET_DV
    return (r == c).astype(F32)


def _ret_block(lg_ref, hp, i, t, qb, kb):
    w = _ret_weight(_lg_rows(lg_ref, hp, t), i, i, t, t)
    return _dot(_stack_heads(qb), kb, _NT), w


RET_PAIRS = 2


def _lanes(ref, p, width):
    return ref[:, p * width:(p + 1) * width]


def _ret_fwd(qk_rot, v_bf, proj, gn_g, log_gamma, t, riders=None):
    s = qk_rot.shape[0]
    n_pair = HEADS // 2
    pw = 2 * RET_DV
    wq, wv = RET_PAIRS * LANES, RET_PAIRS * pw
    gate_off = (2 * RET_QK + RET_V) // wv
    assert s % t == 0 and t % CHUNK == 0 and t & (t - 1) == 0 and n_pair % RET_PAIRS == 0

    def body(lg_ref, q_ref, k_ref, v_ref, g_ref, w_ref, ret_ref, rg_ref, state_ref):
        hg, i = pl.program_id(0), pl.program_id(1)

        @pl.when(i == 0)
        def _():
            state_ref[...] = jnp.zeros_like(state_ref)

        pairs = range(RET_PAIRS)
        qbs = [_lanes(q_ref, p, LANES) for p in pairs]
        kbs = [_lanes(k_ref, p, LANES) for p in pairs]
        vbs = [_lanes(v_ref, p, pw) for p in pairs]
        zws = [_ret_block(lg_ref, hg * RET_PAIRS + p, i, t, qbs[p], kbs[p]) for p in pairs]
        ps = [(z * w).astype(BF16) for z, w in zws]
        outs = [jnp.concatenate([_dot(ps[p][:t], vbs[p][:, 0:RET_DV]), _dot(ps[p][t:], vbs[p][:, RET_DV:pw])], axis=1)
                + _dot(qbs[p], state_ref[p]) for p in pairs]
        for p in pairs:
            state_ref[p] += _pair_mask() * _dot(kbs[p], vbs[p], _TN)
        for p in pairs:
            for hh in range(2):
                cols = slice(p * pw + hh * RET_DV, p * pw + (hh + 1) * RET_DV)
                o = outs[p][:, hh * RET_DV:(hh + 1) * RET_DV]
                ret_ref[:, cols] = o
                mu = jnp.sum(o, axis=1, keepdims=True) * (1.0 / RET_DV)
                xc = o - mu
                var = jnp.sum(xc * xc, axis=1, keepdims=True) * (1.0 / RET_DV)
                nrm = xc * lax.rsqrt(var + EPS) * w_ref[:, cols]
                g = g_ref[:, cols].astype(F32)
                rg_ref[:, cols] = (g * _sigmoid(g) * nrm).astype(BF16)

    blk = lambda hg, i: (i, hg)
    return _call("ret_fwd", body, (n_pair // RET_PAIRS, s // t),
                 [(log_gamma, None, pltpu.SMEM),
                  (qk_rot, (t, wq), blk),
                  (qk_rot, (t, wq), lambda hg, i: (i, n_pair // RET_PAIRS + hg)),
                  (v_bf, (t, wv), blk),
                  (proj, (t, wv), lambda hg, i: (i, gate_off + hg)),
                  (gn_g, (1, wv), lambda hg, i: (0, hg))],
                 [((s, RET_V), F32, (t, wv), blk), ((s, RET_V), BF16, (t, wv), blk)],
                 scratch=[pltpu.VMEM((RET_PAIRS, LANES, pw), F32)], riders=riders)


def _tri(tk, strict_upper):
    r = lax.broadcasted_iota(jnp.int32, (tk, tk), 0)
    cc = lax.broadcasted_iota(jnp.int32, (tk, tk), 1)
    return ((r > cc) if strict_upper else (r < cc)).astype(BF16)


def _diagonal_step(i, tq, tk, make, carry):
    if (tq // tk) % SB_GROUP == 0:
        return make(SB_GROUP)(0, carry)
    assert 2 * (tq // tk) == SB_GROUP
    half = lax.rem(i, 2) == 0
    return lax.cond(half, lambda cr: make(SB_GROUP // 2)(0, cr), lambda cr: make(SB_GROUP)(0, cr), carry)


def _sb_valid(i, j, tq, tk):
    row, col = _tile_pos(i, j, tq, tk)
    return col < row


def _sb_fwd(qkv, tq, tk, riders=None):
    s = qkv.shape[0]
    n_pair = HEADS // 2
    _check_tiles(s, tq, tk, SB_GROUP)

    def body(q_ref, k_ref, v_ref, o_ref, a_ref):
        i = pl.program_id(1)
        upper = _tri(tk, True)
        qs = _stack_heads(q_ref[...])
        n_full, n_groups = _n_full(i, tq, tk, SB_GROUP), _n_groups(i, tq, tk, SB_GROUP)

        def make_step(near_diagonal, last, n_sub=SB_GROUP):
            def step(n, carry):
                c, o = carry
                g = last - 1 - n
                js = [g * SB_GROUP + sub for sub in range(n_sub)]
                zs = [_dot(qs, k_ref[_key_rows(j, tk), :], _NT) for j in js]
                log1ps = [jnp.log2(1.0 + jnp.exp2(-jnp.abs(z))) for z in zs]
                log_1ms = [-jnp.maximum(z, 0.0) - t for z, t in zip(zs, log1ps)]
                log_bs = [jnp.minimum(z, 0.0) - t for z, t in zip(zs, log1ps)]
                if near_diagonal:
                    valids = [_sb_valid(i, j, tq, tk) for j in js]
                    log_1ms = [jnp.where(v, l, 0.0) for v, l in zip(valids, log_1ms)]
                sticks = [_dot(l, upper) for l in log_1ms]
                sums = [jnp.sum(l, axis=1, keepdims=True) for l in log_1ms]
                cs = [None] * n_sub
                for sub in reversed(range(n_sub)):
                    cs[sub] = c
                    c = c + sums[sub]
                for sub, j in enumerate(js):
                    a = jnp.exp2(log_bs[sub] + sticks[sub] + cs[sub])
                    if near_diagonal:
                        a = jnp.where(valids[sub], a, 0.0)
                    a = a.astype(BF16)
                    a_ref[j] = a
                    o = o + _dot(_side_by_side(a, tq), _stack_heads(v_ref[_key_rows(j, tk), :]))
                return c, o
            return step

        carry = (jnp.zeros((2 * tq, 1), F32), jnp.zeros((tq, LANES), F32))
        carry = _diagonal_step(i, tq, tk, lambda n_sub: make_step(True, n_groups, n_sub), carry)
        _, acc = lax.fori_loop(0, n_full, make_step(False, n_full), carry)
        o_ref[...] = acc.astype(BF16)

    n_kb = s // tk
    return _call("sb_fwd", body, (n_pair, s // tq),
                 [(qkv, (tq, LANES), lambda hp, i: (i, hp)),
                  (qkv, (s, LANES), lambda hp, i: (0, n_pair + hp)),
                  (qkv, (s, LANES), lambda hp, i: (0, 2 * n_pair + hp))],
                 [((s, SB_W), BF16, (tq, LANES), lambda hp, i: (i, hp)),
                  ((n_pair, s // tq, n_kb, 2 * tq, tk), BF16, (None, None, n_kb, 2 * tq, tk),
                   lambda hp, i: (hp, i, 0, 0, 0))], riders=riders)


def _merge(retg, sb, w_ret, w_sb_t, proj, tm, tn, riders=None):
    s, d = retg.shape[0], w_ret.shape[1]
    ar_off = (2 * RET_QK + 2 * RET_V + 3 * SB_W) // tn
    as_off = ar_off + d // tn

    def body(rg_ref, sb_ref, wr_ref, ws_ref, ar_ref, as_ref, mix_ref, r_ref, s_ref):
        rr = _dot(rg_ref[...], wr_ref[...])
        ss = _dot(sb_ref[...], ws_ref[...], _NT)
        mix_ref[...] = (_sigmoid(ar_ref[...].astype(F32)) * rr + _sigmoid(as_ref[...].astype(F32)) * ss).astype(BF16)
        r_ref[...] = rr.astype(BF16)
        s_ref[...] = ss.astype(BF16)

    tile = (tm, tn)
    return _call("merge", body, (d // tn, s // tm),
                 [(retg, (tm, RET_V), lambda j, i: (i, 0)), (sb, (tm, SB_W), lambda j, i: (i, 0)),
                  (w_ret, (RET_V, tn), lambda j, i: (0, j)), (w_sb_t, (tn, SB_W), lambda j, i: (j, 0)),
                  (proj, tile, lambda j, i: (i, ar_off + j)), (proj, tile, lambda j, i: (i, as_off + j))],
                 [((s, d), BF16, tile, lambda j, i: (i, j))] * 3, riders=riders)


def _out_proj(mixed, w_out, x, mod, gp1, g2, tm):
    s, d = x.shape

    def body(a_ref, w_ref, x_ref, mod_ref, gp_ref, g2_ref, y_ref, hres_ref, h2_ref):
        for rows in _pieces(tm):
            y = _dot(a_ref[rows, :], w_ref[...])
            y_ref[rows, :] = y
            ny, _ = _rms(y, d)
            hres = x_ref[rows, :] + mod_ref[:, 2 * d:3 * d] * (ny * gp_ref[...])
            hres_ref[rows, :] = hres
            n2, _ = _rms(hres, d)
            h2_ref[rows, :] = (n2 * g2_ref[...] * (1.0 + mod_ref[:, 4 * d:5 * d]) + mod_ref[:, 3 * d:4 * d]).astype(BF16)

    row = lambda i: (i, 0)
    fix = lambda i: (0, 0)
    return _call("out_proj", body, (s // tm,),
                 [(mixed, (tm, d), row), (w_out, (d, d), fix), (x, (tm, d), row),
                  (mod, (1, 6 * d), fix), (gp1, (1, d), fix), (g2, (1, d), fix)],
                 [((s, d), F32, (tm, d), row), ((s, d), F32, (tm, d), row), ((s, d), BF16, (tm, d), row)])


def _ff1(h2, w_ff1_t, tm, tn):
    s, f = h2.shape[0], w_ff1_t.shape[0]
    tm = min(tm, s)

    def body(a_ref, w_ref, u_ref, act_ref):
        u = _dot(a_ref[...], w_ref[...], _NT)
        r = jnp.maximum(u, 0.0)
        u_ref[...] = u.astype(BF16)
        act_ref[...] = (r * r).astype(BF16)

    d = h2.shape[1]
    return _call("ff1", body, (f // tn, s // tm),
                 [(h2, (tm, d), lambda j, i: (i, 0)), (w_ff1_t, (tn, d), lambda j, i: (j, 0))],
                 [((s, f), BF16, (tm, tn), lambda j, i: (i, j))] * 2)


def _ff2_loss(act, w_ff2, hres, target, mod, gp2, tm):
    s, d = hres.shape
    f = act.shape[1]

    def body(a_ref, w_ref, h_ref, t_ref, mod_ref, gp_ref, dout_ref, df_ref, loss_ref, dgt_ref, dgp_ref):
        _zero_at_start([loss_ref, dgt_ref, dgp_ref])
        gt, gp = mod_ref[:, 5 * d:6 * d], gp_ref[...]
        for rows in _pieces(tm):
            ff = _dot(a_ref[rows, :], w_ref[...])
            nf, rf = _rms(ff, d)
            out = h_ref[rows, :] + gt * (nf * gp)
            err = out - t_ref[rows, :]
            sq = jnp.sum(err * err, axis=1, keepdims=True)
            loss_ref[...] += jnp.sum(sq, axis=0, keepdims=True)
            dout = err * (1.0 / d)
            dout_ref[rows, :] = dout
            dgt_ref[...] += _colsum(dout * (nf * gp))
            dgp_ref[...] += _colsum(dout * gt * nf)
            df_ref[rows, :] = _rms_bwd(dout * gt * gp, nf, rf, d).astype(BF16)

    row = lambda i: (i, 0)
    fix = lambda i: (0, 0)
    return _call("ff2_loss", body, (s // tm,),
                 [(act, (tm, f), row), (w_ff2, (f, d), fix), (hres, (tm, d), row), (target, (tm, d), row),
                  (mod, (1, 6 * d), fix), (gp2, (1, d), fix)],
                 [((s, d), F32, (tm, d), row), ((s, d), BF16, (tm, d), row), ((1, 1), F32, (1, 1), fix),
                  ((1, d), F32, (1, d), fix), ((1, d), F32, (1, d), fix)])


def _ff2_bwd(df, w_ff2, u, tm, tn):
    s, d = df.shape
    f = w_ff2.shape[0]
    tm = min(tm, s)

    def body(a_ref, w_ref, u_ref, du_ref):
        da = _dot(a_ref[...], w_ref[...], _NT)
        du_ref[...] = (da * (2.0 * jnp.maximum(u_ref[...].astype(F32), 0.0))).astype(BF16)

    return _call("ff2_bwd", body, (f // tn, s // tm),
                 [(df, (tm, d), lambda j, i: (i, 0)), (w_ff2, (tn, d), lambda j, i: (j, 0)),
                  (u, (tm, tn), lambda j, i: (i, j))],
                 [((s, f), BF16, (tm, tn), lambda j, i: (i, j))])[0]


def _ff1_bwd(du, w_ff1_t, hres, dout, y, mod, g2, gp1, tm, riders=None):
    s, d = hres.shape
    f = du.shape[1]

    def body(a_ref, w_ref, h_ref, do_ref, y_ref, mod_ref, g2_ref, gp_ref,
             dh_ref, dy_ref, dsh_ref, dsc_ref, dg2_ref, dgt_ref, dgp_ref):
        _zero_at_start([dsh_ref, dsc_ref, dg2_ref, dgt_ref, dgp_ref])
        g2, sc2 = g2_ref[...], mod_ref[:, 4 * d:5 * d]
        gt, gp = mod_ref[:, 2 * d:3 * d], gp_ref[...]
        for rows in _pieces(tm):
            dh2 = _dot(a_ref[rows, :], w_ref[...])
            n2, r2 = _rms(h_ref[rows, :], d)
            dsh_ref[...] += _colsum(dh2)
            dsc_ref[...] += _colsum(dh2 * n2 * g2)
            dg2_ref[...] += _colsum(dh2 * n2 * (1.0 + sc2))
            dhres = do_ref[rows, :] + _rms_bwd(dh2 * g2 * (1.0 + sc2), n2, r2, d)
            dh_ref[rows, :] = dhres
            ny, ry = _rms(y_ref[rows, :], d)
            dgt_ref[...] += _colsum(dhres * (ny * gp))
            dgp_ref[...] += _colsum(dhres * gt * ny)
            dy_ref[rows, :] = _rms_bwd(dhres * gt * gp, ny, ry, d).astype(BF16)

    row = lambda i: (i, 0)
    fix = lambda i: (0, 0)
    vec = ((1, d), F32, (1, d), fix)
    return _call("ff1_bwd", body, (s // tm,),
                 [(du, (tm, f), row), (w_ff1_t, (f, d), fix), (hres, (tm, d), row), (dout, (tm, d), row),
                  (y, (tm, d), row), (mod, (1, 6 * d), fix), (g2, (1, d), fix), (gp1, (1, d), fix)],
                 [((s, d), F32, (tm, d), row), ((s, d), BF16, (tm, d), row), vec, vec, vec, vec, vec], riders=riders)


def _out_bwd(dy, w_out, proj, r_bf, s_bf, tm, tn, riders=None):
    s, d = dy.shape
    ar_off = (2 * RET_QK + 2 * RET_V + 3 * SB_W) // tn
    as_off = ar_off + d // tn

    def body(a_ref, w_ref, ar_ref, as_ref, r_ref, s_ref, dr_ref, ds_ref, dar_ref, das_ref):
        dm = _dot(a_ref[...], w_ref[...], _NT)
        sr, ss = _sigmoid(ar_ref[...].astype(F32)), _sigmoid(as_ref[...].astype(F32))
        dr_ref[...] = (dm * sr).astype(BF16)
        ds_ref[...] = (dm * ss).astype(BF16)
        dar_ref[...] = (dm * r_ref[...].astype(F32) * sr * (1.0 - sr)).astype(BF16)
        das_ref[...] = (dm * s_ref[...].astype(F32) * ss * (1.0 - ss)).astype(BF16)

    tile = (tm, tn)
    here = lambda j, i: (i, j)
    return _call("out_bwd", body, (d // tn, s // tm),
                 [(dy, (tm, d), lambda j, i: (i, 0)), (w_out, (tn, d), lambda j, i: (j, 0)),
                  (proj, tile, lambda j, i: (i, ar_off + j)), (proj, tile, lambda j, i: (i, as_off + j)),
                  (r_bf, tile, here), (s_bf, tile, here)],
                 [((s, d), BF16, tile, here)] * 4, riders=riders)


def _gn_bwd(dretg, ret, proj, gn_g, tm, riders=None):
    s = ret.shape[0]
    gate_off = (2 * RET_QK + RET_V) // RET_V

    def body(d_ref, r_ref, g_ref, w_ref, dg_ref, dret_ref, dw_ref):
        first = pl.program_id(0) == 0
        for h in range(HEADS):
            cols = slice(h * RET_DV, (h + 1) * RET_DV)
            o, g, w, dr = r_ref[:, cols], g_ref[:, cols].astype(F32), w_ref[:, cols], d_ref[:, cols].astype(F32)
            mu = jnp.sum(o, axis=1, keepdims=True) * (1.0 / RET_DV)
            xc = o - mu
            rstd = lax.rsqrt(jnp.sum(xc * xc, axis=1, keepdims=True) * (1.0 / RET_DV) + EPS)
            n = xc * rstd
            sg = _sigmoid(g)
            silu = g * sg
            dg_ref[:, cols] = (dr * n * w * (sg * (1.0 + g * (1.0 - sg)))).astype(BF16)
            _accum(dw_ref.at[:, cols], _colsum(dr * silu * n), first)
            dn = dr * silu * w
            m1 = jnp.sum(dn, axis=1, keepdims=True) * (1.0 / RET_DV)
            m2 = jnp.sum(dn * n, axis=1, keepdims=True) * (1.0 / RET_DV)
            dret_ref[:, cols] = (rstd * (dn - m1 - n * m2)).astype(BF16)

    row = lambda i: (i, 0)
    fix = lambda i: (0, 0)
    return _call("gn_bwd", body, (s // tm,),
                 [(dretg, (tm, RET_V), row), (ret, (tm, RET_V), row),
                  (proj, (tm, RET_V), lambda i: (i, gate_off)), (gn_g, (1, RET_V), fix)],
                 [((s, RET_V), BF16, (tm, RET_V), row), ((s, RET_V), BF16, (tm, RET_V), row),
                  ((1, RET_V), F32, (1, RET_V), fix)], riders=riders)


def _ret_bwd(qk_rot, v_bf, dret, log_gamma, t, riders=None):
    s = qk_rot.shape[0]
    n_pair = HEADS // 2
    pw = 2 * RET_DV
    wq, wv = RET_PAIRS * LANES, RET_PAIRS * pw
    n_blk = s // t
    pairs = range(RET_PAIRS)

    def load(q_ref, k_ref, v_ref, do_ref):
        return ([_lanes(q_ref, p, LANES) for p in pairs], [_lanes(k_ref, p, LANES) for p in pairs],
                [_lanes(v_ref, p, pw) for p in pairs], [_lanes(do_ref, p, pw) for p in pairs])

    def d_scores(lg_ref, hp, i, qb, kb, vb, dob):
        z, w = _ret_block(lg_ref, hp, i, t, qb, kb)
        dp = jnp.concatenate([_dot(dob[:, 0:RET_DV], vb[:, 0:RET_DV], _NT),
                              _dot(dob[:, RET_DV:pw], vb[:, RET_DV:pw], _NT)], axis=0)
        return (z * w).astype(BF16), (dp * w).astype(BF16)

    def up_body(lg_ref, q_ref, k_ref, v_ref, do_ref, dq_ref, state_ref):
        hg, i = pl.program_id(0), pl.program_id(1)

        @pl.when(i == 0)
        def _():
            state_ref[...] = jnp.zeros_like(state_ref)

        qbs, kbs, vbs, dobs = load(q_ref, k_ref, v_ref, do_ref)
        dss = [d_scores(lg_ref, hg * RET_PAIRS + p, i, qbs[p], kbs[p], vbs[p], dobs[p])[1] for p in pairs]
        for p in pairs:
            dq_ref[:, p * LANES:(p + 1) * LANES] = (_dot(_side_by_side(dss[p], t), _stack_heads(kbs[p]))
                                                    + _dot(dobs[p], state_ref[p], _NT)).astype(BF16)
        for p in pairs:
            state_ref[p] += _pair_mask() * _dot(kbs[p], vbs[p], _TN)

    def down_body(lg_ref, q_ref, k_ref, v_ref, do_ref, dk_ref, dv_ref, state_ref):
        hg, i = pl.program_id(0), n_blk - 1 - pl.program_id(1)

        @pl.when(pl.program_id(1) == 0)
        def _():
            state_ref[...] = jnp.zeros_like(state_ref)

        qbs, kbs, vbs, dobs = load(q_ref, k_ref, v_ref, do_ref)
        both = [d_scores(lg_ref, hg * RET_PAIRS + p, i, qbs[p], kbs[p], vbs[p], dobs[p]) for p in pairs]
        for p in pairs:
            pp, ds = both[p]
            later = state_ref[p]
            dv_ref[:, p * pw:(p + 1) * pw] = (jnp.concatenate(
                [_dot(pp[:t], dobs[p][:, 0:RET_DV], _TN), _dot(pp[t:], dobs[p][:, RET_DV:pw], _TN)],
                axis=1) + _dot(kbs[p], later)).astype(BF16)
            dk_ref[:, p * LANES:(p + 1) * LANES] = (_dot(ds, _stack_heads(qbs[p]), _TN)
                                                    + _dot(vbs[p], later, _NT)).astype(BF16)
        for p in pairs:
            state_ref[p] += _pair_mask() * _dot(qbs[p], dobs[p], _TN)

    n_grp = n_pair // RET_PAIRS

    def ins(order):
        return [(log_gamma, None, pltpu.SMEM),
                (qk_rot, (t, wq), lambda hg, i: (order(i), hg)),
                (qk_rot, (t, wq), lambda hg, i: (order(i), n_grp + hg)),
                (v_bf, (t, wv), lambda hg, i: (order(i), hg)),
                (dret, (t, wv), lambda hg, i: (order(i), hg))]

    up = lambda i: i
    down = lambda i: n_blk - 1 - i
    scratch = [pltpu.VMEM((RET_PAIRS, LANES, pw), F32)]
    dq = _call("ret_bwd_q", up_body, (n_grp, n_blk), ins(up),
               [((s, RET_QK), BF16, (t, wq), lambda hg, i: (i, hg))], scratch=scratch)[0]
    dk, dv, *rest = _call("ret_bwd_kv", down_body, (n_grp, n_blk), ins(down),
                          [((s, RET_QK), BF16, (t, wq), lambda hg, i: (down(i), hg)),
                           ((s, RET_V), BF16, (t, wv), lambda hg, i: (down(i), hg))],
                          scratch=scratch, riders=riders)
    return [dq, dk, dv] + rest


def _sb_bwd(qkv, weights, do, tq, tk, riders=None):
    s = qkv.shape[0]
    n_pair = HEADS // 2
    _check_tiles(s, tq, tk, SB_GROUP)

    def body(q_ref, k_ref, v_ref, a_ref, do_ref, dq_ref, dk_ref, dv_ref):
        i = pl.program_id(1)

        @pl.when(i == 0)
        def _():
            dk_ref[...] = jnp.zeros_like(dk_ref)
            dv_ref[...] = jnp.zeros_like(dv_ref)

        lower = _tri(tk, False)
        qs = _stack_heads(q_ref[...])
        dos = _stack_heads(do_ref[...].astype(BF16))

        def make_step(near_diagonal, n_sub=SB_GROUP):
            def step(g, carry):
                c_e, dq = carry
                js = [g * SB_GROUP + sub for sub in range(n_sub)]
                rows = [_key_rows(j, tk) for j in js]
                zs = [_dot(qs, k_ref[rw, :], _NT) for rw in rows]
                das = [_dot(dos, v_ref[rw, :], _NT) for rw in rows]
                avals = [a_ref[j] for j in js]
                for a, rw in zip(avals, rows):
                    dv_ref[rw, :] += _dot(a, dos, _TN)
                es = [a.astype(F32) * da for a, da in zip(avals, das)]
                prefixes = [_dot(e, lower) for e in es]
                betas = [1.0 / (1.0 + jnp.exp2(-z)) for z in zs]
                for sub in range(n_sub):
                    dz = es[sub] - (es[sub] + prefixes[sub] + c_e) * betas[sub]
                    if near_diagonal:
                        dz = jnp.where(_sb_valid(i, js[sub], tq, tk), dz, 0.0)
                    dz = dz.astype(BF16)
                    dk_ref[rows[sub], :] += _dot(dz, qs, _TN)
                    dq = dq + _dot(_side_by_side(dz, tq), _stack_heads(k_ref[rows[sub], :]))
                    c_e = c_e + jnp.sum(es[sub], axis=1, keepdims=True)
                return c_e, dq
            return step

        n_full = _n_full(i, tq, tk, SB_GROUP)
        carry = (jnp.zeros((2 * tq, 1), F32), jnp.zeros((tq, LANES), F32))
        carry = lax.fori_loop(0, n_full, make_step(False), carry)
        _, dq = _diagonal_step(i, tq, tk, lambda n_sub: (lambda n, cr: make_step(True, n_sub)(n_full, cr)), carry)
        dq_ref[...] = dq

    blk = lambda hp, i: (i, hp)
    n_kb = s // tk
    return _call("sb_bwd", body, (n_pair, s // tq),
                 [(qkv, (tq, LANES), blk),
                  (qkv, (s, LANES), lambda hp, i: (0, n_pair + hp)),
                  (qkv, (s, LANES), lambda hp, i: (0, 2 * n_pair + hp)),
                  (weights, (None, None, n_kb, 2 * tq, tk), lambda hp, i: (hp, i, 0, 0, 0)),
                  (do, (tq, LANES), blk)],
                 [((s, SB_W), F32, (tq, LANES), blk),
                  ((s, SB_W), F32, (s, LANES), lambda hp, i: (0, hp)),
                  ((s, SB_W), F32, (s, LANES), lambda hp, i: (0, hp))], riders=riders)


def _assemble_dproj(dq_r, dk_r, dv_r, dg_r, dq_s, dk_s, dv_s, da_r, da_s, cos, sin, idx_col, lg_lanes, tm, riders=None):
    s, d = da_r.shape
    width = 2 * RET_QK + 2 * RET_V + 3 * SB_W + 2 * d

    def body(dq_ref, dk_ref, dv_ref, dg_ref, dqs_ref, dks_ref, dvs_ref, dar_ref, das_ref, cos_ref, sin_ref,
             idx_ref, lg_ref, o_ref):
        lane = lax.broadcasted_iota(jnp.int32, (1, LANES), 1)
        first = jnp.bitwise_and(lane, RET_DQK - 1) < (RET_DQK // 2)
        cos, sin = cos_ref[...], sin_ref[...]
        idx = idx_ref[...]
        for src, base, sign, scale in ((dq_ref, 0, 1.0, 1.0), (dk_ref, RET_QK, -1.0, RET_DQK ** -0.5)):
            for g in range(RET_QK // LANES):
                v = src[:, g * LANES:(g + 1) * LANES].astype(F32) * (_decay_scale(lg_ref, idx, g, sign) * scale)
                sw = jnp.where(first, pltpu.roll(v, LANES - RET_DQK // 2, 1), pltpu.roll(v, RET_DQK // 2, 1))
                o_ref[:, base + g * LANES:base + (g + 1) * LANES] = (v * cos - sw * sin).astype(BF16)
        off = 2 * RET_QK
        o_ref[:, off:off + RET_V] = dv_ref[...].astype(BF16)
        off += RET_V
        o_ref[:, off:off + RET_V] = dg_ref[...]
        off += RET_V
        o_ref[:, off:off + SB_W] = (dqs_ref[...] * (SB_DH ** -0.5)).astype(BF16)
        off += SB_W
        o_ref[:, off:off + SB_W] = (dks_ref[...] * LN2).astype(BF16)
        off += SB_W
        o_ref[:, off:off + SB_W] = dvs_ref[...].astype(BF16)
        off += SB_W
        o_ref[:, off:off + d] = dar_ref[...]
        off += d
        o_ref[:, off:off + d] = das_ref[...]

    row = lambda i: (i, 0)
    ins = [(a, (tm, a.shape[1]), row) for a in (dq_r, dk_r, dv_r, dg_r, dq_s, dk_s, dv_s, da_r, da_s, cos, sin, idx_col)]
    ins.append((lg_lanes, (1, RET_QK), lambda i: (0, 0)))
    return _call("assemble_dproj", body, (s // tm,), ins, [((s, width), BF16, (tm, width), row)], riders=riders)


def _in_bwd(dproj, w_in_t, x, dhres, mod, g1, tm, riders=None):
    s, d = x.shape
    width = dproj.shape[1]

    def body(a_ref, w_ref, x_ref, dh_ref, mod_ref, g_ref, dx_ref, dsh_ref, dsc_ref, dg_ref):
        _zero_at_start([dsh_ref, dsc_ref, dg_ref])
        g1, sc1 = g_ref[...], mod_ref[:, d:2 * d]
        for rows in _pieces(tm):
            dh = _dot(a_ref[rows, :], w_ref[...])
            n1, r1 = _rms(x_ref[rows, :], d)
            dsh_ref[...] += _colsum(dh)
            dsc_ref[...] += _colsum(dh * n1 * g1)
            dg_ref[...] += _colsum(dh * n1 * (1.0 + sc1))
            dx_ref[rows, :] = dh_ref[rows, :] + _rms_bwd(dh * g1 * (1.0 + sc1), n1, r1, d)

    row = lambda i: (i, 0)
    fix = lambda i: (0, 0)
    vec = ((1, d), F32, (1, d), fix)
    return _call("in_bwd", body, (s // tm,),
                 [(dproj, (tm, width), row), (w_in_t, (width, d), fix), (x, (tm, d), row), (dhres, (tm, d), row),
                  (mod, (1, 6 * d), fix), (g1, (1, d), fix)],
                 [((s, d), F32, (tm, d), row), vec, vec, vec], riders=riders)


def _adamw(w, g, m, v):
    m = ADAM_B1 * m + (1.0 - ADAM_B1) * g
    v = ADAM_B2 * v + (1.0 - ADAM_B2) * (g * g)
    m_hat = m / (1.0 - ADAM_B1 ** ADAM_STEP)
    v_hat = v / (1.0 - ADAM_B2 ** ADAM_STEP)
    delta = -ADAM_LR * (m_hat / (jnp.sqrt(v_hat) + ADAM_EPS) + ADAM_WD * w)
    return delta, m, v


def _adam_reduce(name, parts, w, m, v, tr):
    rws, cls = w.shape
    tr = min(tr, rws)
    n_parts = parts.shape[0]

    def body(p_ref, w_ref, m_ref, v_ref, g_out, d_out, m_out, v_out):
        g = p_ref[0].astype(F32)
        for k in range(1, n_parts):
            g = g + p_ref[k].astype(F32)
        delta, mn, vn = _adamw(w_ref[...], g, m_ref[...], v_ref[...])
        g_out[...] = g
        d_out[...] = delta
        m_out[...] = mn
        v_out[...] = vn

    row = lambda i: (i, 0)
    blk = (tr, cls)
    return _call(name, body, (rws // tr,),
                 [(parts, (n_parts, tr, cls), lambda i: (0, i, 0)), (w, blk, row), (m, blk, row), (v, blk, row)],
                 [((rws, cls), F32, blk, row)] * 4)


def _ada_bwd_adam(cs_t, dmod_cols, w, m, v):
    d, nc = w.shape

    def body(c_ref, dm_ref, w_ref, m_ref, v_ref, g_out, d_out, m_out, v_out):
        g = c_ref[0] * dm_ref[0:1, :]
        for r in range(1, N_DEV):
            g = g + c_ref[r] * dm_ref[r:r + 1, :]
        delta, mn, vn = _adamw(w_ref[...], g, m_ref[...], v_ref[...])
        g_out[...] = g
        d_out[...] = delta
        m_out[...] = mn
        v_out[...] = vn

    fix = lambda i: (0, 0)
    blk = (d, nc)
    return _call("ada_bwd_adam", body, (1,),
                 [(cs_t, (N_DEV, d, 1), lambda i: (0, 0, 0)), (dmod_cols, (N_DEV, nc), fix), (w, blk, fix), (m, blk, fix), (v, blk, fix)],
                 [((d, nc), F32, blk, fix)] * 4)


def _small_adam(parts, ws, ms, vs):
    n = len(ws)
    widths = [w.shape[1] for w in ws]
    total = parts.shape[1]
    assert sum(widths) + LANES == total

    def body(p_ref, *refs):
        w_refs, m_refs, v_refs = refs[:n], refs[n:2 * n], refs[2 * n:3 * n]
        outs = refs[3 * n:]
        g = p_ref[0:1, :]
        for k in range(1, N_DEV):
            g = g + p_ref[k:k + 1, :]
        off = 0
        for i, width in enumerate(widths):
            gi = g[:, off:off + width]
            delta, mn, vn = _adamw(w_refs[i][...], gi, m_refs[i][...], v_refs[i][...])
            for o_ref, val in zip(outs[4 * i:4 * i + 4], (gi, delta, mn, vn)):
                o_ref[...] = val
            off += width
        outs[4 * n][...] = g[:, off:off + LANES]

    fix = lambda i: (0, 0)
    vec = lambda a: (a, (1, a.shape[1]), fix)
    out_specs = [((1, width), F32, (1, width), fix) for width in widths for _ in range(4)]
    out_specs.append(((1, LANES), F32, (1, LANES), fix))
    res = _call("small_adam", body, (1,),
                [(parts, (N_DEV, total), fix)] + [vec(a) for a in list(ws) + list(ms) + list(vs)], out_specs)
    return [res[4 * i:4 * i + 4] for i in range(n)], res[4 * n]


def kernel(x, c, positions, ada_w, ada_b, pre_mix_g, post_mix_g, pre_ffn_g, post_ffn_g, w_in, ret_gn_g, w_ret_branch, w_sb_branch, w_out, w_ff1, w_ff2, loss_target, m_ada_w, m_ada_b, m_pre_mix_g, m_post_mix_g, m_pre_ffn_g, m_post_ffn_g, m_w_in, m_ret_gn_g, m_w_ret_branch, m_w_sb_branch, m_w_out, m_w_ff1, m_w_ff2, v_ada_w, v_ada_b, v_pre_mix_g, v_post_mix_g, v_pre_ffn_g, v_post_ffn_g, v_w_in, v_ret_gn_g, v_w_ret_branch, v_w_sb_branch, v_w_out, v_w_ff1, v_w_ff2):
    _, s, d = x.shape
    d_ff = w_ff1.shape[2] * N_DEV
    d_in = w_in.shape[2] * N_DEV
    me = 4 * lax.axis_index("x") + 2 * lax.axis_index("y") + lax.axis_index("c")
    x2, tgt = x[0], loss_target[0]

    core = lax.axis_index("c").astype(jnp.int32).reshape(1)
    bf = lambda w: w[0].astype(BF16)

    w_in_t, m_in_t, v_in_t = (jnp.swapaxes(a[0], 0, 1) for a in (w_in, m_w_in, v_w_in))

    c_all, g_in = _exchange("gather_in", [c, w_in_t.astype(BF16)], ["gather", "gather_chip"])
    c_all = c_all.reshape(N_DEV, d)

    n_ada = ada_w.shape[2]
    cs_all = _silu_rows(c_all)
    ada_b_cols = lax.dynamic_slice(ada_b, (0, me * n_ada), (1, n_ada))
    mod_cols = _ada_fwd(cs_all, ada_w[0], ada_b_cols)
    mod_all = _exchange("gather_mod", [mod_cols], ["gather"])[0]
    mod = lax.dynamic_index_in_dim(mod_all, me, axis=1, keepdims=False).reshape(1, 6 * d)

    tm = min(256, s)
    h, g_in = _pre_norm(x2, pre_mix_g, mod, 2 * tm, riders=([g_in], ["forward"]))
    wt_in = g_in.reshape(d_in, d)
    bf_t = lambda w: jnp.swapaxes(w[0], 0, 1).astype(BF16)
    small_w = [bf(w_ret_branch), bf_t(w_sb_branch), bf(w_out)]
    proj, *small_w = _matmul("in_proj", h, wt_in, "nt", s, 512, BF16, riders=(small_w, ["gather_chip"] * 3))
    pos_col = positions.reshape(s, 1).astype(F32)
    freqs = ROPE_BASE ** (-jnp.arange(0, RET_DQK, 2, dtype=F32) / RET_DQK)
    inv_freq = jnp.tile(freqs, LANES // (RET_DQK // 2)).reshape(1, LANES)
    log_gamma_np = np.log1p(-(2.0 ** (-5.0 - np.arange(HEADS))))
    log_gamma = jnp.asarray(log_gamma_np, F32)
    lg_lanes = jnp.asarray(np.repeat(log_gamma_np, RET_DQK).reshape(1, RET_QK), F32)
    idx_col = (jnp.arange(s, dtype=F32) - (s // 2)).reshape(s, 1)
    qk_rot, v_bf, qkv_sb, cos_t, sin_t = _prep(proj, pos_col, idx_col, inv_freq, lg_lanes, 2 * tm)
    tq, tk = min(256, s), min(128, s)
    tq_sb = min(SB_TQ, s)
    sb, sb_weights, *big_w = _sb_fwd(qkv_sb, tq_sb, tk, riders=([bf(w_ff2), bf_t(w_ff1)], ["gather_chip"] * 2))
    ret, retg, g_ret, g_sb, g_out, g_ff2, g_ff1 = _ret_fwd(qk_rot, v_bf, proj, ret_gn_g, log_gamma, tq,
                                                           riders=(small_w + big_w, ["forward"] * 5))
    wf_ret = g_ret.reshape(RET_V, d)
    wt_sb = g_sb.reshape(d, SB_W)
    wf_out = g_out.reshape(d, d)
    wt_ff1 = g_ff1.reshape(d_ff, d)
    wf_ff2 = g_ff2.reshape(d_ff, d)
    mixed, r_bf, s_bf = _merge(retg, sb, wf_ret, wt_sb, proj, 2 * tm, min(512, d))
    y, hres, h2 = _out_proj(mixed, wf_out, x2, mod, post_mix_g, pre_ffn_g, tm)
    u, act = _ff1(h2, wt_ff1, s, 512)
    dout, df, loss_sum, d_gt2, d_gp2 = _ff2_loss(act, wf_ff2, hres, tgt, mod, post_ffn_g, tm)

    du = _ff2_bwd(df, wf_ff2, u, s, 512)
    gw_ff2 = _matmul("grad_w_ff2", act, df, "tn", 512, d, BF16).reshape(N_DEV, d_ff // N_DEV, d)
    gw_ff1 = _matmul("grad_w_ff1", h2, du, "tn", d, d_ff // N_DEV, BF16, blocked_out=True)
    dhres, dy, d_sh2, d_sc2, d_g2, d_gt1, d_gp1, t_ff1, t_ff2 = _ff1_bwd(
        du, wt_ff1, hres, dout, y, mod, pre_ffn_g, post_mix_g, tm, riders=([gw_ff1, gw_ff2], ["pair"] * 2))
    s_ff1 = _pair_sum("pair_sum_ff1", gw_ff1, t_ff1, core, d)
    s_ff2 = _pair_sum("pair_sum_ff2", gw_ff2, t_ff2, core, d)
    gw_out = _matmul("grad_w_out", mixed, dy, "tn", 512, d, BF16).reshape(N_DEV, d // N_DEV, d)
    d_r, d_s, da_r, da_s, p_out = _out_bwd(dy, wf_out, proj, r_bf, s_bf, 2 * tm, min(512, d), riders=([gw_out], ["scatter"]))
    dretg = _matmul("ret_branch_bwd", d_r, wf_ret, "nt", s, 512, BF16)
    dsb = _matmul("sb_branch_bwd", d_s, wt_sb, "nn", s, 512, F32)
    gw_ret = _matmul("grad_w_ret", retg, d_r, "tn", 512, d, BF16).reshape(N_DEV, RET_V // N_DEV, d)
    gw_sb = _matmul("grad_w_sb", sb, d_s, "tn", 512, d // N_DEV, BF16, blocked_out=True)
    dq_s, dk_s, dv_s, p_ff1, p_ff2 = _sb_bwd(qkv_sb, sb_weights, dsb, tq_sb, tk,
                                             riders=([s_ff1, s_ff2], ["chip_scatter"] * 2))
    dg_r, dret, d_gn = _gn_bwd(dretg, ret, proj, ret_gn_g, 2 * tm)
    dq_r, dk_r, dv_r, p_sb = _ret_bwd(qk_rot, v_bf, dret, log_gamma, tq, riders=([gw_sb], ["scatter"]))
    dproj, p_ret = _assemble_dproj(dq_r, dk_r, dv_r, dg_r, dq_s, dk_s, dv_s, da_r, da_s, cos_t, sin_t, idx_col, lg_lanes,
                                   tm, riders=([gw_ret], ["scatter"]))
    gw_in = _matmul("grad_w_in", dproj, h, "tn", 512, d, BF16).reshape(N_DEV, d_in // N_DEV, d)
    t_in = _exchange("pair_in", [gw_in], ["pair"])[0]
    s_in = _pair_sum("pair_sum_in", gw_in, t_in, core, d_in // N_DEV)
    grad_x, d_sh1, d_sc1, d_g1, p_in = _in_bwd(dproj, wt_in, x2, dhres, mod, pre_mix_g, tm,
                                               riders=([s_in], ["chip_scatter"]))
    loss_lanes = jnp.pad(loss_sum, ((0, 0), (0, LANES - 1)))
    small = jnp.concatenate([d_sh1, d_sc1, d_gt1, d_sh2, d_sc2, d_gt2, d_g1, d_gp1, d_g2, d_gp2, d_gn, loss_lanes], axis=1)
    small_all = _exchange("gather_small", [small], ["gather"])[0].reshape(N_DEV, small.shape[1])
    parts = [p_in, p_ret, p_sb, p_out, p_ff1, p_ff2]

    res = {}
    names = ["w_ret_branch", "w_sb_branch", "w_out", "w_ff1", "w_ff2"]
    ws = [w_ret_branch, w_sb_branch, w_out, w_ff1, w_ff2]
    ms = [m_w_ret_branch, m_w_sb_branch, m_w_out, m_w_ff1, m_w_ff2]
    vs = [v_w_ret_branch, v_w_sb_branch, v_w_out, v_w_ff1, v_w_ff2]
    for nm, p, w, m, v in zip(names, parts[1:], ws, ms, vs):
        res[nm] = [o[None] for o in _adam_reduce("adam_" + nm, p, w[0], m[0], v[0], 512)]
    res["w_in"] = [jnp.swapaxes(o, 0, 1)[None]
                   for o in _adam_reduce("adam_w_in", parts[0], w_in_t, m_in_t, v_in_t, d_in // N_DEV // 2)]
    dmod_cols = lax.dynamic_slice(small_all, (0, me * n_ada), (N_DEV, n_ada))
    res["ada_w"] = [o[None] for o in _ada_bwd_adam(cs_all.reshape(N_DEV, d, 1), dmod_cols, ada_w[0], m_ada_w[0], v_ada_w[0])]
    vec_names = ["ada_b", "pre_mix_g", "post_mix_g", "pre_ffn_g", "post_ffn_g", "ret_gn_g"]
    vec_res, loss_lanes = _small_adam(small_all,
                                      [ada_b, pre_mix_g, post_mix_g, pre_ffn_g, post_ffn_g, ret_gn_g],
                                      [m_ada_b, m_pre_mix_g, m_post_mix_g, m_pre_ffn_g, m_post_ffn_g, m_ret_gn_g],
                                      [v_ada_b, v_pre_mix_g, v_post_mix_g, v_pre_ffn_g, v_post_ffn_g, v_ret_gn_g])
    res.update(zip(vec_names, vec_res))
    loss = (0.5 / d) * loss_lanes[0, 0]
    order = ["ada_w", "ada_b", "pre_mix_g", "post_mix_g", "pre_ffn_g", "post_ffn_g", "w_in", "ret_gn_g",
             "w_ret_branch", "w_sb_branch", "w_out", "w_ff1", "w_ff2"]
    outs = [loss, grad_x[None]]
    for k in range(4):
        outs += [res[nm][k] for nm in order]
    return tuple(outs)
```

```python
import functools

import numpy as np
import jax
import jax.numpy as jnp
from jax import lax
from jax.experimental import pallas as pl
from jax.experimental.pallas import tpu as pltpu

F32 = jnp.float32
BF16 = jnp.bfloat16
N_DEV = 8
AXES = ("x", "y", "c")

EPS = 1e-6
CHUNK = 64
CHUNK_SHIFT = 6
HEADS = 8
RET_DQK = 64
RET_DV = 128
SB_DH = 64
RET_QK = HEADS * RET_DQK
RET_V = HEADS * RET_DV
SB_W = HEADS * SB_DH
ROPE_BASE = 10000.0
LANES = 128

ADAM_LR = 0.001
ADAM_B1 = 0.9
ADAM_B2 = 0.999
ADAM_EPS = 1e-08
ADAM_WD = 0.01
ADAM_STEP = 10

VMEM_LIMIT = 56 * 1024 * 1024

_NN = (((1,), (0,)), ((), ()))
_NT = (((1,), (1,)), ((), ()))
_TN = (((0,), (0,)), ((), ()))


def _dot(a, b, dims=_NN):
    if a.dtype != BF16:
        a = a.astype(BF16)
    if b.dtype != BF16:
        b = b.astype(BF16)
    return lax.dot_general(a, b, dims, preferred_element_type=F32)


def _sigmoid(x):
    return 1.0 / (1.0 + jnp.exp(-x))


def _rms(x, d):
    r = lax.rsqrt(jnp.sum(x * x, axis=1, keepdims=True) * (1.0 / d) + EPS)
    return x * r, r


def _rms_bwd(dn, n, r, d):
    return r * (dn - n * (jnp.sum(dn * n, axis=1, keepdims=True) * (1.0 / d)))


def _colsum(v):
    return jnp.sum(v, axis=0, keepdims=True)


def _accum(ref, val, first):
    @pl.when(first)
    def _():
        ref[...] = val

    @pl.when(jnp.logical_not(first))
    def _():
        ref[...] += val


ROW_SPLIT = 2


def _zero_at_start(refs):
    @pl.when(pl.program_id(0) == 0)
    def _():
        for r in refs:
            r[...] = jnp.zeros_like(r)


def _pieces(tm):
    step = tm // ROW_SPLIT
    return [slice(k * step, (k + 1) * step) for k in range(ROW_SPLIT)]


KIND_SLOTS = {"gather": N_DEV, "scatter": N_DEV, "gather_chip": N_DEV, "forward": N_DEV, "pair": N_DEV // 2,
              "chip_scatter": N_DEV // 2}
SEMS_PER_ARRAY = N_DEV - 1


def _exchange_copies(ins, outs, send_sems, recv_sems, local_sems, kinds):
    x, y, c = (lax.axis_index(a) for a in AXES)
    me, chip, sibling = 4 * x + 2 * y + c, 2 * x + y, (x, y, 1 - c)
    mesh_id = pl.DeviceIdType.MESH
    other_chips = []
    for k in range(1, N_DEV // 2):
        px = 1 - x if k & 2 else x
        py = 1 - y if k & 1 else y
        other_chips.append((px, py))
    copies = []
    for i, kind in enumerate(kinds):
        def remote(src, dst, k, to, i=i):
            return pltpu.make_async_remote_copy(
                src_ref=src, dst_ref=dst, send_sem=send_sems.at[i * SEMS_PER_ARRAY + k],
                recv_sem=recv_sems.at[i * SEMS_PER_ARRAY + k], device_id=to, device_id_type=mesh_id)

        if kind in ("gather", "scatter"):
            pick = (lambda ref, d: ref.at[d]) if kind == "scatter" else (lambda ref, d: ref)
            copies.append(pltpu.make_async_copy(pick(ins[i], me), outs[i].at[me], local_sems.at[i]))
            for k in range(1, N_DEV):
                to = (1 - x if k & 4 else x, 1 - y if k & 2 else y, 1 - c if k & 1 else c)
                copies.append(remote(pick(ins[i], 4 * to[0] + 2 * to[1] + to[2]), outs[i].at[me], k - 1, to))
        elif kind == "gather_chip":
            copies.append(pltpu.make_async_copy(ins[i], outs[i].at[me], local_sems.at[i]))
            copies.append(remote(ins[i], outs[i].at[me], 0, sibling))
            for k, (px, py) in enumerate(other_chips):
                copies.append(remote(ins[i], outs[i].at[me], 1 + k, (px, py, c)))
        elif kind == "forward":
            for k, (px, py) in enumerate(other_chips):
                slot = 4 * px + 2 * py + c
                copies.append(remote(outs[i].at[slot], outs[i].at[slot], k, sibling))
        elif kind == "pair":
            for k in range(N_DEV // 2):
                copies.append(remote(ins[i].at[2 * k + 1 - c], outs[i].at[k], k, sibling))
        elif kind == "chip_scatter":
            copies.append(pltpu.make_async_copy(ins[i].at[chip], outs[i].at[chip], local_sems.at[i]))
            for k, (px, py) in enumerate(other_chips):
                copies.append(remote(ins[i].at[2 * px + py], outs[i].at[chip], k, (px, py, c)))
        else:
            raise ValueError(kind)
    return copies


def _exchange_shapes(arrays, kinds):
    shapes = []
    for a, kind in zip(arrays, kinds):
        tail = a.shape if kind in ("gather", "gather_chip") else a.shape[1:]
        shapes.append(jax.ShapeDtypeStruct((KIND_SLOTS[kind],) + tuple(tail), a.dtype))
    return shapes


def _exchange_sems(n):
    return [pltpu.SemaphoreType.DMA((n * SEMS_PER_ARRAY,)), pltpu.SemaphoreType.DMA((n * SEMS_PER_ARRAY,)),
            pltpu.SemaphoreType.DMA((n,))]


def _call(name, body, grid, ins, outs, scratch=(), riders=None, prefetch=None):
    any_spec = pl.BlockSpec(memory_space=pl.ANY)
    in_specs = [pl.BlockSpec(memory_space=im) if bs is None else pl.BlockSpec(bs, im) for _, bs, im in ins]
    out_specs = [pl.BlockSpec(bs, im) for _, _, bs, im in outs]
    out_shape = [jax.ShapeDtypeStruct(s, d) for s, d, _, _ in outs]
    operands = [a for a, _, _ in ins]
    scratch = list(scratch)
    aliases = {}
    n_pre = 0 if prefetch is None else 1
    kernel = functools.partial(body) if prefetch is None else (lambda _, *refs: body(*refs))
    if riders is not None:
        arrays, kinds = riders
        nr, n_in, n_out, n_scr = len(arrays), len(ins), len(outs), len(scratch)

        def kernel(*refs):
            refs = refs[n_pre:]
            own_in, ride_in = refs[:n_in], refs[n_in:n_in + nr]
            own_out = refs[n_in + nr:n_in + nr + n_out]
            ride_out = refs[n_in + nr + n_out:n_in + 2 * nr + n_out]
            own_scr = refs[n_in + 2 * nr + n_out:n_in + 2 * nr + n_out + n_scr]
            sems = refs[n_in + 2 * nr + n_out + n_scr:]
            ids = [pl.program_id(a) for a in range(len(grid))]
            first = functools.reduce(jnp.logical_and, [i == 0 for i in ids])
            last = functools.reduce(jnp.logical_and, [i == g - 1 for i, g in zip(ids, grid)])

            @pl.when(first)
            def _():
                for cp in _exchange_copies(ride_in, ride_out, *sems, kinds):
                    cp.start()

            body(*own_in, *own_out, *own_scr)

            @pl.when(last)
            def _():
                for cp in _exchange_copies(ride_in, ride_out, *sems, kinds):
                    cp.wait()

        in_specs += [any_spec] * nr
        out_specs += [any_spec] * nr
        out_shape += _exchange_shapes(arrays, kinds)
        operands += list(arrays)
        scratch += _exchange_sems(nr)
        aliases = {n_pre + n_in + r: n_out + r for r, kind in enumerate(kinds) if kind == "forward"}
    params = pltpu.CompilerParams(dimension_semantics=("arbitrary",) * len(grid), vmem_limit_bytes=VMEM_LIMIT)
    if prefetch is None:
        return pl.pallas_call(kernel, name=name, grid=grid, in_specs=in_specs, out_specs=out_specs,
                              out_shape=out_shape, scratch_shapes=scratch, input_output_aliases=aliases,
                              compiler_params=params)(*operands)
    grid_spec = pltpu.PrefetchScalarGridSpec(num_scalar_prefetch=1, grid=grid, in_specs=in_specs,
                                             out_specs=out_specs, scratch_shapes=scratch)
    return pl.pallas_call(kernel, name=name, grid_spec=grid_spec, out_shape=out_shape,
                          input_output_aliases=aliases, compiler_params=params)(prefetch, *operands)


def _exchange(name, arrays, kinds):
    n = len(arrays)

    def body(*refs):
        copies = _exchange_copies(refs[:n], refs[n:2 * n], *refs[2 * n:], kinds)
        for cp in copies:
            cp.start()
        for cp in copies:
            cp.wait()

    any_spec = pl.BlockSpec(memory_space=pl.ANY)
    return pl.pallas_call(
        functools.partial(body),
        name=name,
        in_specs=[any_spec] * n,
        out_specs=[any_spec] * n,
        out_shape=_exchange_shapes(arrays, kinds),
        scratch_shapes=_exchange_sems(n),
        input_output_aliases={i: i for i, kind in enumerate(kinds) if kind == "forward"},
    )(*arrays)


def _pair_sum(name, mine, theirs, my_core, tr):
    _, rws, cls = mine.shape
    tr = min(tr, rws)

    def body(a_ref, b_ref, o_ref):
        o_ref[...] = (a_ref[...].astype(F32) + b_ref[...].astype(F32)).astype(o_ref.dtype)

    return _call(name, body, (N_DEV // 2, rws // tr),
                 [(mine, (None, tr, cls), lambda k, r, core: (2 * k + core[0], r, 0)),
                  (theirs, (None, tr, cls), lambda k, r, core: (k, r, 0))],
                 [((N_DEV // 2, rws, cls), mine.dtype, (None, tr, cls), lambda k, r, core: (k, r, 0))],
                 prefetch=my_core)[0]


def _matmul(name, a, b, kind, tm, tn, out_dtype, blocked_out=False, riders=None):
    if kind == "tn":
        kdim, m = a.shape
    else:
        m, kdim = a.shape
    n = b.shape[0] if kind == "nt" else b.shape[1]
    tm, tn = min(tm, m), min(tn, n)
    dims = {"nn": _NN, "nt": _NT, "tn": _TN}[kind]

    def body(a_ref, b_ref, o_ref):
        o_ref[...] = _dot(a_ref[...], b_ref[...], dims).astype(o_ref.dtype)

    a_spec = (a, (kdim, tm), lambda j, i: (0, i)) if kind == "tn" else (a, (tm, kdim), lambda j, i: (i, 0))
    b_spec = (b, (tn, kdim), lambda j, i: (j, 0)) if kind == "nt" else (b, (kdim, tn), lambda j, i: (0, j))
    if blocked_out:
        out = ((n // tn, m, tn), out_dtype, (None, tm, tn), lambda j, i: (j, i, 0))
    else:
        out = ((m, n), out_dtype, (tm, tn), lambda j, i: (i, j))
    res = _call(name, body, (n // tn, m // tm), [a_spec, b_spec], [out], riders=riders)
    return res[0] if riders is None else res


def _ada_fwd(cs_all, ada_w, ada_b_cols):
    def body(c_ref, w_ref, b_ref, o_ref):
        o_ref[...] = lax.dot_general(c_ref[...], w_ref[...], _NN, preferred_element_type=F32,
                                     precision=lax.Precision.HIGHEST) + b_ref[...]

    r, d = cs_all.shape
    nc = ada_w.shape[1]
    return _call("ada_fwd", body, (1,),
                 [(cs_all, (r, d), lambda i: (0, 0)), (ada_w, (d, nc), lambda i: (0, 0)),
                  (ada_b_cols, (1, nc), lambda i: (0, 0))],
                 [((r, nc), F32, (r, nc), lambda i: (0, 0))])[0]


def _silu_rows(c_all):
    def body(c_ref, o_ref):
        v = c_ref[...]
        o_ref[...] = v * _sigmoid(v)

    return _call("silu_c", body, (1,), [(c_all, c_all.shape, lambda i: (0, 0))],
                 [(c_all.shape, F32, c_all.shape, lambda i: (0, 0))])[0]


def _pre_norm(x, g, mod, tm, riders=None):
    s, d = x.shape

    def body(x_ref, g_ref, mod_ref, h_ref):
        n, _ = _rms(x_ref[...], d)
        sh, sc = mod_ref[:, 0:d], mod_ref[:, d:2 * d]
        h_ref[...] = (n * g_ref[...] * (1.0 + sc) + sh).astype(BF16)

    return _call("pre_norm", body, (s // tm,),
                 [(x, (tm, d), lambda i: (i, 0)), (g, (1, d), lambda i: (0, 0)),
                  (mod, (1, 6 * d), lambda i: (0, 0))],
                 [((s, d), BF16, (tm, d), lambda i: (i, 0))], riders=riders)


LOG2E = 1.4426950408889634
LN2 = 0.6931471805599453


def _decay_scale(lg_ref, idx, g, sign):
    return jnp.exp((sign * idx) * lg_ref[:, g * LANES:(g + 1) * LANES])


def _prep(proj, pos_col, idx_col, inv_freq, lg_lanes, tm):
    s = proj.shape[0]
    sb_off = (2 * RET_QK + 2 * RET_V) // (3 * SB_W)
    n_q = RET_QK // LANES

    def body(qk_ref, v_ref, sb_ref, pos_ref, idx_ref, f_ref, lg_ref, qk_out, v_out, sb_out, cos_out, sin_out):
        ang = pos_ref[...] * f_ref[...]
        lane = lax.broadcasted_iota(jnp.int32, (1, LANES), 1)
        first = jnp.bitwise_and(lane, RET_DQK - 1) < (RET_DQK // 2)
        cos = jnp.cos(ang)
        sin = jnp.where(first, -1.0, 1.0) * jnp.sin(ang)
        cos_out[...] = cos
        sin_out[...] = sin
        idx = idx_ref[...]
        for g in range(2 * n_q):
            v = qk_ref[:, g * LANES:(g + 1) * LANES].astype(F32)
            sw = jnp.where(first, pltpu.roll(v, LANES - RET_DQK // 2, 1), pltpu.roll(v, RET_DQK // 2, 1))
            r = v * cos + sw * sin
            if g < n_q:
                r = r * _decay_scale(lg_ref, idx, g, 1.0)
            else:
                r = r * (_decay_scale(lg_ref, idx, g - n_q, -1.0) * (RET_DQK ** -0.5))
            qk_out[:, g * LANES:(g + 1) * LANES] = r.astype(BF16)
        v_out[...] = v_ref[...].astype(BF16)
        sb_out[:, 0:SB_W] = (sb_ref[:, 0:SB_W].astype(F32) * (SB_DH ** -0.5 * LOG2E)).astype(BF16)
        sb_out[:, SB_W:3 * SB_W] = sb_ref[:, SB_W:3 * SB_W].astype(BF16)

    return _call("prep", body, (s // tm,),
                 [(proj, (tm, 2 * RET_QK), lambda i: (i, 0)),
                  (proj, (tm, RET_V), lambda i: (i, 2 * RET_QK // RET_V)),
                  (proj, (tm, 3 * SB_W), lambda i: (i, sb_off)),
                  (pos_col, (tm, 1), lambda i: (i, 0)),
                  (idx_col, (tm, 1), lambda i: (i, 0)),
                  (inv_freq, (1, LANES), lambda i: (0, 0)),
                  (lg_lanes, (1, RET_QK), lambda i: (0, 0))],
                 [((s, 2 * RET_QK), BF16, (tm, 2 * RET_QK), lambda i: (i, 0)),
                  ((s, RET_V), BF16, (tm, RET_V), lambda i: (i, 0)),
                  ((s, 3 * SB_W), BF16, (tm, 3 * SB_W), lambda i: (i, 0)),
                  ((s, LANES), F32, (tm, LANES), lambda i: (i, 0)),
                  ((s, LANES), F32, (tm, LANES), lambda i: (i, 0))])


def _head_mask(hh):
    lane = lax.broadcasted_iota(jnp.int32, (1, LANES), 1)
    return (lane >= RET_DQK) if hh else (lane < RET_DQK)


def _masked(v, m):
    return jnp.where(m, v, jnp.zeros_like(v))


SB_GROUP = 4
SB_TQ = 256


def _stack_heads(v):
    return jnp.concatenate([_masked(v, _head_mask(0)), _masked(v, _head_mask(1))], axis=0)


def _side_by_side(v, t):
    return jnp.concatenate([v[:t], v[t:]], axis=1)


def _tile_pos(i, j, tq, tk):
    row = jnp.bitwise_and(lax.broadcasted_iota(jnp.int32, (2 * tq, tk), 0), tq - 1) + i * tq
    col = lax.broadcasted_iota(jnp.int32, (2 * tq, tk), 1) + j * tk
    return row, col


def _n_groups(i, tq, tk, grp):
    return ((i + 1) * (tq // tk) + grp - 1) // grp


def _n_full(i, tq, tk, grp):
    return (i * (tq // tk)) // grp


def _key_rows(j, tk):
    return pl.ds(pl.multiple_of(j * tk, tk), tk)


def _ret_weight(lg_rows, i, j, tq, tk):
    row, col = _tile_pos(i, j, tq, tk)
    same = jnp.right_shift(col, CHUNK_SHIFT) == jnp.right_shift(row, CHUNK_SHIFT)
    later = jnp.where(same, jnp.exp((2.0 * lg_rows) * (col - row).astype(F32)), 0.0)
    return jnp.where(col <= row, 1.0, later)


def _lg_rows(lg_ref, hp, tq):
    first = lax.broadcasted_iota(jnp.int32, (2 * tq, 1), 0) < tq
    return jnp.where(first, lg_ref[2 * hp], lg_ref[2 * hp + 1])


def _check_tiles(s, tq, tk, grp):
    assert tq % tk == 0 and tq & (tq - 1) == 0 and tk & (tk - 1) == 0
    assert s % tq == 0 and (s // tk) % grp == 0 and s // tk <= LANES


def _pair_mask():
    r = lax.broadcasted_iota(jnp.int32, (LANES, 2 * RET_DV), 0) >= RET_DQK
    c = lax.broadcasted_iota(jnp.int32, (LANES, 2 * RET_DV), 1) >= RET_DV
    return (r == c).astype(F32)


def _ret_block(lg_ref, hp, i, t, qb, kb):
    w = _ret_weight(_lg_rows(lg_ref, hp, t), i, i, t, t)
    return _dot(_stack_heads(qb), kb, _NT), w


RET_PAIRS = 2


def _lanes(ref, p, width):
    return ref[:, p * width:(p + 1) * width]


def _ret_fwd(qk_rot, v_bf, proj, gn_g, log_gamma, t, riders=None):
    s = qk_rot.shape[0]
    n_pair = HEADS // 2
    pw = 2 * RET_DV
    wq, wv = RET_PAIRS * LANES, RET_PAIRS * pw
    gate_off = (2 * RET_QK + RET_V) // wv
    assert s % t == 0 and t % CHUNK == 0 and t & (t - 1) == 0 and n_pair % RET_PAIRS == 0

    def body(lg_ref, q_ref, k_ref, v_ref, g_ref, w_ref, ret_ref, rg_ref, state_ref):
        hg, i = pl.program_id(0), pl.program_id(1)

        @pl.when(i == 0)
        def _():
            state_ref[...] = jnp.zeros_like(state_ref)

        pairs = range(RET_PAIRS)
        qbs = [_lanes(q_ref, p, LANES) for p in pairs]
        kbs = [_lanes(k_ref, p, LANES) for p in pairs]
        vbs = [_lanes(v_ref, p, pw) for p in pairs]
        zws = [_ret_block(lg_ref, hg * RET_PAIRS + p, i, t, qbs[p], kbs[p]) for p in pairs]
        ps = [(z * w).astype(BF16) for z, w in zws]
        outs = [jnp.concatenate([_dot(ps[p][:t], vbs[p][:, 0:RET_DV]), _dot(ps[p][t:], vbs[p][:, RET_DV:pw])], axis=1)
                + _dot(qbs[p], state_ref[p]) for p in pairs]
        for p in pairs:
            state_ref[p] += _pair_mask() * _dot(kbs[p], vbs[p], _TN)
        for p in pairs:
            for hh in range(2):
                cols = slice(p * pw + hh * RET_DV, p * pw + (hh + 1) * RET_DV)
                o = outs[p][:, hh * RET_DV:(hh + 1) * RET_DV]
                ret_ref[:, cols] = o
                mu = jnp.sum(o, axis=1, keepdims=True) * (1.0 / RET_DV)
                xc = o - mu
                var = jnp.sum(xc * xc, axis=1, keepdims=True) * (1.0 / RET_DV)
                nrm = xc * lax.rsqrt(var + EPS) * w_ref[:, cols]
                g = g_ref[:, cols].astype(F32)
                rg_ref[:, cols] = (g * _sigmoid(g) * nrm).astype(BF16)

    blk = lambda hg, i: (i, hg)
    return _call("ret_fwd", body, (n_pair // RET_PAIRS, s // t),
                 [(log_gamma, None, pltpu.SMEM),
                  (qk_rot, (t, wq), blk),
                  (qk_rot, (t, wq), lambda hg, i: (i, n_pair // RET_PAIRS + hg)),
                  (v_bf, (t, wv), blk),
                  (proj, (t, wv), lambda hg, i: (i, gate_off + hg)),
                  (gn_g, (1, wv), lambda hg, i: (0, hg))],
                 [((s, RET_V), F32, (t, wv), blk), ((s, RET_V), BF16, (t, wv), blk)],
                 scratch=[pltpu.VMEM((RET_PAIRS, LANES, pw), F32)], riders=riders)


def _tri(tk, strict_upper):
    r = lax.broadcasted_iota(jnp.int32, (tk, tk), 0)
    cc = lax.broadcasted_iota(jnp.int32, (tk, tk), 1)
    return ((r > cc) if strict_upper else (r < cc)).astype(BF16)


def _diagonal_step(i, tq, tk, make, carry):
    if (tq // tk) % SB_GROUP == 0:
        return make(SB_GROUP)(0, carry)
    assert 2 * (tq // tk) == SB_GROUP
    half = lax.rem(i, 2) == 0
    return lax.cond(half, lambda cr: make(SB_GROUP // 2)(0, cr), lambda cr: make(SB_GROUP)(0, cr), carry)


def _sb_valid(i, j, tq, tk):
    row, col = _tile_pos(i, j, tq, tk)
    return col < row


def _sb_fwd(qkv, tq, tk, riders=None):
    s = qkv.shape[0]
    n_pair = HEADS // 2
    _check_tiles(s, tq, tk, SB_GROUP)

    def body(q_ref, k_ref, v_ref, o_ref, a_ref):
        i = pl.program_id(1)
        upper = _tri(tk, True)
        qs = _stack_heads(q_ref[...])
        n_full, n_groups = _n_full(i, tq, tk, SB_GROUP), _n_groups(i, tq, tk, SB_GROUP)

        def make_step(near_diagonal, last, n_sub=SB_GROUP):
            def step(n, carry):
                c, o = carry
                g = last - 1 - n
                js = [g * SB_GROUP + sub for sub in range(n_sub)]
                zs = [_dot(qs, k_ref[_key_rows(j, tk), :], _NT) for j in js]
                log1ps = [jnp.log2(1.0 + jnp.exp2(-jnp.abs(z))) for z in zs]
                log_1ms = [-jnp.maximum(z, 0.0) - t for z, t in zip(zs, log1ps)]
                log_bs = [jnp.minimum(z, 0.0) - t for z, t in zip(zs, log1ps)]
                if near_diagonal:
                    valids = [_sb_valid(i, j, tq, tk) for j in js]
                    log_1ms = [jnp.where(v, l, 0.0) for v, l in zip(valids, log_1ms)]
                sticks = [_dot(l, upper) for l in log_1ms]
                sums = [jnp.sum(l, axis=1, keepdims=True) for l in log_1ms]
                cs = [None] * n_sub
                for sub in reversed(range(n_sub)):
                    cs[sub] = c
                    c = c + sums[sub]
                for sub, j in enumerate(js):
                    a = jnp.exp2(log_bs[sub] + sticks[sub] + cs[sub])
                    if near_diagonal:
                        a = jnp.where(valids[sub], a, 0.0)
                    a = a.astype(BF16)
                    a_ref[j] = a
                    o = o + _dot(_side_by_side(a, tq), _stack_heads(v_ref[_key_rows(j, tk), :]))
                return c, o
            return step

        carry = (jnp.zeros((2 * tq, 1), F32), jnp.zeros((tq, LANES), F32))
        carry = _diagonal_step(i, tq, tk, lambda n_sub: make_step(True, n_groups, n_sub), carry)
        _, acc = lax.fori_loop(0, n_full, make_step(False, n_full), carry)
        o_ref[...] = acc.astype(BF16)

    n_kb = s // tk
    return _call("sb_fwd", body, (n_pair, s // tq),
                 [(qkv, (tq, LANES), lambda hp, i: (i, hp)),
                  (qkv, (s, LANES), lambda hp, i: (0, n_pair + hp)),
                  (qkv, (s, LANES), lambda hp, i: (0, 2 * n_pair + hp))],
                 [((s, SB_W), BF16, (tq, LANES), lambda hp, i: (i, hp)),
                  ((n_pair, s // tq, n_kb, 2 * tq, tk), BF16, (None, None, n_kb, 2 * tq, tk),
                   lambda hp, i: (hp, i, 0, 0, 0))], riders=riders)


def _merge(retg, sb, w_ret, w_sb_t, proj, tm, tn, riders=None):
    s, d = retg.shape[0], w_ret.shape[1]
    ar_off = (2 * RET_QK + 2 * RET_V + 3 * SB_W) // tn
    as_off = ar_off + d // tn

    def body(rg_ref, sb_ref, wr_ref, ws_ref, ar_ref, as_ref, mix_ref, r_ref, s_ref):
        rr = _dot(rg_ref[...], wr_ref[...])
        ss = _dot(sb_ref[...], ws_ref[...], _NT)
        mix_ref[...] = (_sigmoid(ar_ref[...].astype(F32)) * rr + _sigmoid(as_ref[...].astype(F32)) * ss).astype(BF16)
        r_ref[...] = rr.astype(BF16)
        s_ref[...] = ss.astype(BF16)

    tile = (tm, tn)
    return _call("merge", body, (d // tn, s // tm),
                 [(retg, (tm, RET_V), lambda j, i: (i, 0)), (sb, (tm, SB_W), lambda j, i: (i, 0)),
                  (w_ret, (RET_V, tn), lambda j, i: (0, j)), (w_sb_t, (tn, SB_W), lambda j, i: (j, 0)),
                  (proj, tile, lambda j, i: (i, ar_off + j)), (proj, tile, lambda j, i: (i, as_off + j))],
                 [((s, d), BF16, tile, lambda j, i: (i, j))] * 3, riders=riders)


def _out_proj(mixed, w_out, x, mod, gp1, g2, tm):
    s, d = x.shape

    def body(a_ref, w_ref, x_ref, mod_ref, gp_ref, g2_ref, y_ref, hres_ref, h2_ref):
        for rows in _pieces(tm):
            y = _dot(a_ref[rows, :], w_ref[...])
            y_ref[rows, :] = y
            ny, _ = _rms(y, d)
            hres = x_ref[rows, :] + mod_ref[:, 2 * d:3 * d] * (ny * gp_ref[...])
            hres_ref[rows, :] = hres
            n2, _ = _rms(hres, d)
            h2_ref[rows, :] = (n2 * g2_ref[...] * (1.0 + mod_ref[:, 4 * d:5 * d]) + mod_ref[:, 3 * d:4 * d]).astype(BF16)

    row = lambda i: (i, 0)
    fix = lambda i: (0, 0)
    return _call("out_proj", body, (s // tm,),
                 [(mixed, (tm, d), row), (w_out, (d, d), fix), (x, (tm, d), row),
                  (mod, (1, 6 * d), fix), (gp1, (1, d), fix), (g2, (1, d), fix)],
                 [((s, d), F32, (tm, d), row), ((s, d), F32, (tm, d), row), ((s, d), BF16, (tm, d), row)])


def _ff1(h2, w_ff1_t, tm, tn):
    s, f = h2.shape[0], w_ff1_t.shape[0]
    tm = min(tm, s)

    def body(a_ref, w_ref, u_ref, act_ref):
        u = _dot(a_ref[...], w_ref[...], _NT)
        r = jnp.maximum(u, 0.0)
        u_ref[...] = u.astype(BF16)
        act_ref[...] = (r * r).astype(BF16)

    d = h2.shape[1]
    return _call("ff1", body, (f // tn, s // tm),
                 [(h2, (tm, d), lambda j, i: (i, 0)), (w_ff1_t, (tn, d), lambda j, i: (j, 0))],
                 [((s, f), BF16, (tm, tn), lambda j, i: (i, j))] * 2)


def _ff2_loss(act, w_ff2, hres, target, mod, gp2, tm):
    s, d = hres.shape
    f = act.shape[1]

    def body(a_ref, w_ref, h_ref, t_ref, mod_ref, gp_ref, dout_ref, df_ref, loss_ref, dgt_ref, dgp_ref):
        _zero_at_start([loss_ref, dgt_ref, dgp_ref])
        gt, gp = mod_ref[:, 5 * d:6 * d], gp_ref[...]
        for rows in _pieces(tm):
            ff = _dot(a_ref[rows, :], w_ref[...])
            nf, rf = _rms(ff, d)
            out = h_ref[rows, :] + gt * (nf * gp)
            err = out - t_ref[rows, :]
            sq = jnp.sum(err * err, axis=1, keepdims=True)
            loss_ref[...] += jnp.sum(sq, axis=0, keepdims=True)
            dout = err * (1.0 / d)
            dout_ref[rows, :] = dout
            dgt_ref[...] += _colsum(dout * (nf * gp))
            dgp_ref[...] += _colsum(dout * gt * nf)
            df_ref[rows, :] = _rms_bwd(dout * gt * gp, nf, rf, d).astype(BF16)

    row = lambda i: (i, 0)
    fix = lambda i: (0, 0)
    return _call("ff2_loss", body, (s // tm,),
                 [(act, (tm, f), row), (w_ff2, (f, d), fix), (hres, (tm, d), row), (target, (tm, d), row),
                  (mod, (1, 6 * d), fix), (gp2, (1, d), fix)],
                 [((s, d), F32, (tm, d), row), ((s, d), BF16, (tm, d), row), ((1, 1), F32, (1, 1), fix),
                  ((1, d), F32, (1, d), fix), ((1, d), F32, (1, d), fix)])


def _ff2_bwd(df, w_ff2, u, tm, tn):
    s, d = df.shape
    f = w_ff2.shape[0]
    tm = min(tm, s)

    def body(a_ref, w_ref, u_ref, du_ref):
        da = _dot(a_ref[...], w_ref[...], _NT)
        du_ref[...] = (da * (2.0 * jnp.maximum(u_ref[...].astype(F32), 0.0))).astype(BF16)

    return _call("ff2_bwd", body, (f // tn, s // tm),
                 [(df, (tm, d), lambda j, i: (i, 0)), (w_ff2, (tn, d), lambda j, i: (j, 0)),
                  (u, (tm, tn), lambda j, i: (i, j))],
                 [((s, f), BF16, (tm, tn), lambda j, i: (i, j))])[0]


def _ff1_bwd(du, w_ff1_t, hres, dout, y, mod, g2, gp1, tm, riders=None):
    s, d = hres.shape
    f = du.shape[1]

    def body(a_ref, w_ref, h_ref, do_ref, y_ref, mod_ref, g2_ref, gp_ref,
             dh_ref, dy_ref, dsh_ref, dsc_ref, dg2_ref, dgt_ref, dgp_ref):
        _zero_at_start([dsh_ref, dsc_ref, dg2_ref, dgt_ref, dgp_ref])
        g2, sc2 = g2_ref[...], mod_ref[:, 4 * d:5 * d]
        gt, gp = mod_ref[:, 2 * d:3 * d], gp_ref[...]
        for rows in _pieces(tm):
            dh2 = _dot(a_ref[rows, :], w_ref[...])
            n2, r2 = _rms(h_ref[rows, :], d)
            dsh_ref[...] += _colsum(dh2)
            dsc_ref[...] += _colsum(dh2 * n2 * g2)
            dg2_ref[...] += _colsum(dh2 * n2 * (1.0 + sc2))
            dhres = do_ref[rows, :] + _rms_bwd(dh2 * g2 * (1.0 + sc2), n2, r2, d)
            dh_ref[rows, :] = dhres
            ny, ry = _rms(y_ref[rows, :], d)
            dgt_ref[...] += _colsum(dhres * (ny * gp))
            dgp_ref[...] += _colsum(dhres * gt * ny)
            dy_ref[rows, :] = _rms_bwd(dhres * gt * gp, ny, ry, d).astype(BF16)

    row = lambda i: (i, 0)
    fix = lambda i: (0, 0)
    vec = ((1, d), F32, (1, d), fix)
    return _call("ff1_bwd", body, (s // tm,),
                 [(du, (tm, f), row), (w_ff1_t, (f, d), fix), (hres, (tm, d), row), (dout, (tm, d), row),
                  (y, (tm, d), row), (mod, (1, 6 * d), fix), (g2, (1, d), fix), (gp1, (1, d), fix)],
                 [((s, d), F32, (tm, d), row), ((s, d), BF16, (tm, d), row), vec, vec, vec, vec, vec], riders=riders)


def _out_bwd(dy, w_out, proj, r_bf, s_bf, tm, tn, riders=None):
    s, d = dy.shape
    ar_off = (2 * RET_QK + 2 * RET_V + 3 * SB_W) // tn
    as_off = ar_off + d // tn

    def body(a_ref, w_ref, ar_ref, as_ref, r_ref, s_ref, dr_ref, ds_ref, dar_ref, das_ref):
        dm = _dot(a_ref[...], w_ref[...], _NT)
        sr, ss = _sigmoid(ar_ref[...].astype(F32)), _sigmoid(as_ref[...].astype(F32))
        dr_ref[...] = (dm * sr).astype(BF16)
        ds_ref[...] = (dm * ss).astype(BF16)
        dar_ref[...] = (dm * r_ref[...].astype(F32) * sr * (1.0 - sr)).astype(BF16)
        das_ref[...] = (dm * s_ref[...].astype(F32) * ss * (1.0 - ss)).astype(BF16)

    tile = (tm, tn)
    here = lambda j, i: (i, j)
    return _call("out_bwd", body, (d // tn, s // tm),
                 [(dy, (tm, d), lambda j, i: (i, 0)), (w_out, (tn, d), lambda j, i: (j, 0)),
                  (proj, tile, lambda j, i: (i, ar_off + j)), (proj, tile, lambda j, i: (i, as_off + j)),
                  (r_bf, tile, here), (s_bf, tile, here)],
                 [((s, d), BF16, tile, here)] * 4, riders=riders)


def _branch_bwd(d_r, d_s, retg, sb, w_ret, w_sb_t):
    s, d = d_r.shape
    half_v, half_s, half_d = RET_V // 2, SB_W // 2, d // 2
    per_dev = d // N_DEV
    n_blk = half_d // per_dev

    def body(dr_ref, ds_ref, rg_ref, sb_ref, wr_ref, ws_ref, dretg_ref, dsb_ref, gwr_ref, gws_ref):
        i = pl.program_id(0)
        dr, ds = dr_ref[...], ds_ref[...]
        dretg_ref[...] = _dot(dr, wr_ref[...], _NT).astype(BF16)
        dsb_ref[...] = _dot(ds, ws_ref[...]).astype(BF16)
        gwr_ref[...] = _dot(rg_ref[...], dr, _TN).astype(BF16)
        cols = pl.ds(pl.multiple_of(i * half_d, half_d), half_d)
        gws = _dot(sb_ref[...], ds_ref[:, cols], _TN).astype(BF16)
        for k in range(n_blk):
            gws_ref[k] = gws[:, k * per_dev:(k + 1) * per_dev]

    fix = lambda i: (0, 0)
    return _call("branch_bwd", body, (2,),
                 [(d_r, (s, d), fix), (d_s, (s, d), fix), (retg, (s, half_v), lambda i: (0, i)), (sb, (s, SB_W), fix),
                  (w_ret, (half_v, d), lambda i: (i, 0)), (w_sb_t, (d, half_s), lambda i: (0, i))],
                 [((s, RET_V), BF16, (s, half_v), lambda i: (0, i)), ((s, SB_W), BF16, (s, half_s), lambda i: (0, i)),
                  ((RET_V, d), BF16, (half_v, d), lambda i: (i, 0)),
                  ((N_DEV, SB_W, per_dev), BF16, (n_blk, SB_W, per_dev), lambda i: (i, 0, 0))])


def _gn_bwd(dretg, ret, proj, gn_g, tm, riders=None):
    s = ret.shape[0]
    gate_off = (2 * RET_QK + RET_V) // RET_V

    def body(d_ref, r_ref, g_ref, w_ref, dg_ref, dret_ref, dw_ref):
        first = pl.program_id(0) == 0
        for h in range(HEADS):
            cols = slice(h * RET_DV, (h + 1) * RET_DV)
            o, g, w, dr = r_ref[:, cols], g_ref[:, cols].astype(F32), w_ref[:, cols], d_ref[:, cols].astype(F32)
            mu = jnp.sum(o, axis=1, keepdims=True) * (1.0 / RET_DV)
            xc = o - mu
            rstd = lax.rsqrt(jnp.sum(xc * xc, axis=1, keepdims=True) * (1.0 / RET_DV) + EPS)
            n = xc * rstd
            sg = _sigmoid(g)
            silu = g * sg
            dg_ref[:, cols] = (dr * n * w * (sg * (1.0 + g * (1.0 - sg)))).astype(BF16)
            _accum(dw_ref.at[:, cols], _colsum(dr * silu * n), first)
            dn = dr * silu * w
            m1 = jnp.sum(dn, axis=1, keepdims=True) * (1.0 / RET_DV)
            m2 = jnp.sum(dn * n, axis=1, keepdims=True) * (1.0 / RET_DV)
            dret_ref[:, cols] = (rstd * (dn - m1 - n * m2)).astype(BF16)

    row = lambda i: (i, 0)
    fix = lambda i: (0, 0)
    return _call("gn_bwd", body, (s // tm,),
                 [(dretg, (tm, RET_V), row), (ret, (tm, RET_V), row),
                  (proj, (tm, RET_V), lambda i: (i, gate_off)), (gn_g, (1, RET_V), fix)],
                 [((s, RET_V), BF16, (tm, RET_V), row), ((s, RET_V), BF16, (tm, RET_V), row),
                  ((1, RET_V), F32, (1, RET_V), fix)], riders=riders)


def _ret_bwd(qk_rot, v_bf, dret, log_gamma, t, riders=None):
    s = qk_rot.shape[0]
    n_pair = HEADS // 2
    pw = 2 * RET_DV
    wq, wv = RET_PAIRS * LANES, RET_PAIRS * pw
    n_blk = s // t
    pairs = range(RET_PAIRS)

    def load(q_ref, k_ref, v_ref, do_ref):
        return ([_lanes(q_ref, p, LANES) for p in pairs], [_lanes(k_ref, p, LANES) for p in pairs],
                [_lanes(v_ref, p, pw) for p in pairs], [_lanes(do_ref, p, pw) for p in pairs])

    def d_scores(lg_ref, hp, i, qb, kb, vb, dob):
        z, w = _ret_block(lg_ref, hp, i, t, qb, kb)
        dp = jnp.concatenate([_dot(dob[:, 0:RET_DV], vb[:, 0:RET_DV], _NT),
                              _dot(dob[:, RET_DV:pw], vb[:, RET_DV:pw], _NT)], axis=0)
        return (z * w).astype(BF16), (dp * w).astype(BF16)

    def up_body(lg_ref, q_ref, k_ref, v_ref, do_ref, dq_ref, state_ref):
        hg, i = pl.program_id(0), pl.program_id(1)

        @pl.when(i == 0)
        def _():
            state_ref[...] = jnp.zeros_like(state_ref)

        qbs, kbs, vbs, dobs = load(q_ref, k_ref, v_ref, do_ref)
        dss = [d_scores(lg_ref, hg * RET_PAIRS + p, i, qbs[p], kbs[p], vbs[p], dobs[p])[1] for p in pairs]
        for p in pairs:
            dq_ref[:, p * LANES:(p + 1) * LANES] = (_dot(_side_by_side(dss[p], t), _stack_heads(kbs[p]))
                                                    + _dot(dobs[p], state_ref[p], _NT)).astype(BF16)
        for p in pairs:
            state_ref[p] += _pair_mask() * _dot(kbs[p], vbs[p], _TN)

    def down_body(lg_ref, q_ref, k_ref, v_ref, do_ref, dk_ref, dv_ref, state_ref):
        hg, i = pl.program_id(0), n_blk - 1 - pl.program_id(1)

        @pl.when(pl.program_id(1) == 0)
        def _():
            state_ref[...] = jnp.zeros_like(state_ref)

        qbs, kbs, vbs, dobs = load(q_ref, k_ref, v_ref, do_ref)
        both = [d_scores(lg_ref, hg * RET_PAIRS + p, i, qbs[p], kbs[p], vbs[p], dobs[p]) for p in pairs]
        for p in pairs:
            pp, ds = both[p]
            later = state_ref[p]
            dv_ref[:, p * pw:(p + 1) * pw] = (jnp.concatenate(
                [_dot(pp[:t], dobs[p][:, 0:RET_DV], _TN), _dot(pp[t:], dobs[p][:, RET_DV:pw], _TN)],
                axis=1) + _dot(kbs[p], later)).astype(BF16)
            dk_ref[:, p * LANES:(p + 1) * LANES] = (_dot(ds, _stack_heads(qbs[p]), _TN)
                                                    + _dot(vbs[p], later, _NT)).astype(BF16)
        for p in pairs:
            state_ref[p] += _pair_mask() * _dot(qbs[p], dobs[p], _TN)

    n_grp = n_pair // RET_PAIRS

    def ins(order):
        return [(log_gamma, None, pltpu.SMEM),
                (qk_rot, (t, wq), lambda hg, i: (order(i), hg)),
                (qk_rot, (t, wq), lambda hg, i: (order(i), n_grp + hg)),
                (v_bf, (t, wv), lambda hg, i: (order(i), hg)),
                (dret, (t, wv), lambda hg, i: (order(i), hg))]

    up = lambda i: i
    down = lambda i: n_blk - 1 - i
    scratch = [pltpu.VMEM((RET_PAIRS, LANES, pw), F32)]
    dq = _call("ret_bwd_q", up_body, (n_grp, n_blk), ins(up),
               [((s, RET_QK), BF16, (t, wq), lambda hg, i: (i, hg))], scratch=scratch)[0]
    dk, dv, *rest = _call("ret_bwd_kv", down_body, (n_grp, n_blk), ins(down),
                          [((s, RET_QK), BF16, (t, wq), lambda hg, i: (down(i), hg)),
                           ((s, RET_V), BF16, (t, wv), lambda hg, i: (down(i), hg))],
                          scratch=scratch, riders=riders)
    return [dq, dk, dv] + rest


def _sb_bwd(qkv, weights, do, tq, tk, riders=None):
    s = qkv.shape[0]
    n_pair = HEADS // 2
    _check_tiles(s, tq, tk, SB_GROUP)

    def body(q_ref, k_ref, v_ref, a_ref, do_ref, dq_ref, dk_ref, dv_ref):
        i = pl.program_id(1)

        @pl.when(i == 0)
        def _():
            dk_ref[...] = jnp.zeros_like(dk_ref)
            dv_ref[...] = jnp.zeros_like(dv_ref)

        lower = _tri(tk, False)
        qs = _stack_heads(q_ref[...])
        dos = _stack_heads(do_ref[...].astype(BF16))

        def make_step(near_diagonal, n_sub=SB_GROUP):
            def step(g, carry):
                c_e, dq = carry
                js = [g * SB_GROUP + sub for sub in range(n_sub)]
                rows = [_key_rows(j, tk) for j in js]
                zs = [_dot(qs, k_ref[rw, :], _NT) for rw in rows]
                das = [_dot(dos, v_ref[rw, :], _NT) for rw in rows]
                avals = [a_ref[j] for j in js]
                for a, rw in zip(avals, rows):
                    dv_ref[rw, :] += _dot(a, dos, _TN)
                es = [a.astype(F32) * da for a, da in zip(avals, das)]
                prefixes = [_dot(e, lower) for e in es]
                betas = [1.0 / (1.0 + jnp.exp2(-z)) for z in zs]
                for sub in range(n_sub):
                    dz = es[sub] - (es[sub] + prefixes[sub] + c_e) * betas[sub]
                    if near_diagonal:
                        dz = jnp.where(_sb_valid(i, js[sub], tq, tk), dz, 0.0)
                    dz = dz.astype(BF16)
                    dk_ref[rows[sub], :] += _dot(dz, qs, _TN)
                    dq = dq + _dot(_side_by_side(dz, tq), _stack_heads(k_ref[rows[sub], :]))
                    c_e = c_e + jnp.sum(es[sub], axis=1, keepdims=True)
                return c_e, dq
            return step

        n_full = _n_full(i, tq, tk, SB_GROUP)
        carry = (jnp.zeros((2 * tq, 1), F32), jnp.zeros((tq, LANES), F32))
        carry = lax.fori_loop(0, n_full, make_step(False), carry)
        _, dq = _diagonal_step(i, tq, tk, lambda n_sub: (lambda n, cr: make_step(True, n_sub)(n_full, cr)), carry)
        dq_ref[...] = dq

    blk = lambda hp, i: (i, hp)
    n_kb = s // tk
    return _call("sb_bwd", body, (n_pair, s // tq),
                 [(qkv, (tq, LANES), blk),
                  (qkv, (s, LANES), lambda hp, i: (0, n_pair + hp)),
                  (qkv, (s, LANES), lambda hp, i: (0, 2 * n_pair + hp)),
                  (weights, (None, None, n_kb, 2 * tq, tk), lambda hp, i: (hp, i, 0, 0, 0)),
                  (do, (tq, LANES), blk)],
                 [((s, SB_W), F32, (tq, LANES), blk),
                  ((s, SB_W), F32, (s, LANES), lambda hp, i: (0, hp)),
                  ((s, SB_W), F32, (s, LANES), lambda hp, i: (0, hp))], riders=riders)


def _assemble_dproj(dq_r, dk_r, dv_r, dg_r, dq_s, dk_s, dv_s, da_r, da_s, cos, sin, idx_col, lg_lanes, tm, riders=None):
    s, d = da_r.shape
    width = 2 * RET_QK + 2 * RET_V + 3 * SB_W + 2 * d

    def body(dq_ref, dk_ref, dv_ref, dg_ref, dqs_ref, dks_ref, dvs_ref, dar_ref, das_ref, cos_ref, sin_ref,
             idx_ref, lg_ref, o_ref):
        lane = lax.broadcasted_iota(jnp.int32, (1, LANES), 1)
        first = jnp.bitwise_and(lane, RET_DQK - 1) < (RET_DQK // 2)
        cos, sin = cos_ref[...], sin_ref[...]
        idx = idx_ref[...]
        for src, base, sign, scale in ((dq_ref, 0, 1.0, 1.0), (dk_ref, RET_QK, -1.0, RET_DQK ** -0.5)):
            for g in range(RET_QK // LANES):
                v = src[:, g * LANES:(g + 1) * LANES].astype(F32) * (_decay_scale(lg_ref, idx, g, sign) * scale)
                sw = jnp.where(first, pltpu.roll(v, LANES - RET_DQK // 2, 1), pltpu.roll(v, RET_DQK // 2, 1))
                o_ref[:, base + g * LANES:base + (g + 1) * LANES] = (v * cos - sw * sin).astype(BF16)
        off = 2 * RET_QK
        o_ref[:, off:off + RET_V] = dv_ref[...].astype(BF16)
        off += RET_V
        o_ref[:, off:off + RET_V] = dg_ref[...]
        off += RET_V
        o_ref[:, off:off + SB_W] = (dqs_ref[...] * (SB_DH ** -0.5)).astype(BF16)
        off += SB_W
        o_ref[:, off:off + SB_W] = (dks_ref[...] * LN2).astype(BF16)
        off += SB_W
        o_ref[:, off:off + SB_W] = dvs_ref[...].astype(BF16)
        off += SB_W
        o_ref[:, off:off + d] = dar_ref[...]
        off += d
        o_ref[:, off:off + d] = das_ref[...]

    row = lambda i: (i, 0)
    ins = [(a, (tm, a.shape[1]), row) for a in (dq_r, dk_r, dv_r, dg_r, dq_s, dk_s, dv_s, da_r, da_s, cos, sin, idx_col)]
    ins.append((lg_lanes, (1, RET_QK), lambda i: (0, 0)))
    return _call("assemble_dproj", body, (s // tm,), ins, [((s, width), BF16, (tm, width), row)], riders=riders)


def _in_bwd(dproj, w_in_t, x, dhres, mod, g1, tm, riders=None):
    s, d = x.shape
    width = dproj.shape[1]

    def body(a_ref, w_ref, x_ref, dh_ref, mod_ref, g_ref, dx_ref, dsh_ref, dsc_ref, dg_ref):
        _zero_at_start([dsh_ref, dsc_ref, dg_ref])
        g1, sc1 = g_ref[...], mod_ref[:, d:2 * d]
        for rows in _pieces(tm):
            dh = _dot(a_ref[rows, :], w_ref[...])
            n1, r1 = _rms(x_ref[rows, :], d)
            dsh_ref[...] += _colsum(dh)
            dsc_ref[...] += _colsum(dh * n1 * g1)
            dg_ref[...] += _colsum(dh * n1 * (1.0 + sc1))
            dx_ref[rows, :] = dh_ref[rows, :] + _rms_bwd(dh * g1 * (1.0 + sc1), n1, r1, d)

    row = lambda i: (i, 0)
    fix = lambda i: (0, 0)
    vec = ((1, d), F32, (1, d), fix)
    return _call("in_bwd", body, (s // tm,),
                 [(dproj, (tm, width), row), (w_in_t, (width, d), fix), (x, (tm, d), row), (dhres, (tm, d), row),
                  (mod, (1, 6 * d), fix), (g1, (1, d), fix)],
                 [((s, d), F32, (tm, d), row), vec, vec, vec], riders=riders)


def _adamw(w, g, m, v):
    m = ADAM_B1 * m + (1.0 - ADAM_B1) * g
    v = ADAM_B2 * v + (1.0 - ADAM_B2) * (g * g)
    m_hat = m / (1.0 - ADAM_B1 ** ADAM_STEP)
    v_hat = v / (1.0 - ADAM_B2 ** ADAM_STEP)
    delta = -ADAM_LR * (m_hat / (jnp.sqrt(v_hat) + ADAM_EPS) + ADAM_WD * w)
    return delta, m, v


def _adam_reduce(name, parts, w, m, v, tr):
    rws, cls = w.shape
    tr = min(tr, rws)
    n_parts = parts.shape[0]

    def body(p_ref, w_ref, m_ref, v_ref, g_out, d_out, m_out, v_out):
        g = p_ref[0].astype(F32)
        for k in range(1, n_parts):
            g = g + p_ref[k].astype(F32)
        delta, mn, vn = _adamw(w_ref[...], g, m_ref[...], v_ref[...])
        g_out[...] = g
        d_out[...] = delta
        m_out[...] = mn
        v_out[...] = vn

    row = lambda i: (i, 0)
    blk = (tr, cls)
    return _call(name, body, (rws // tr,),
                 [(parts, (n_parts, tr, cls), lambda i: (0, i, 0)), (w, blk, row), (m, blk, row), (v, blk, row)],
                 [((rws, cls), F32, blk, row)] * 4)


def _ada_bwd_adam(cs_t, dmod_cols, w, m, v):
    d, nc = w.shape

    def body(c_ref, dm_ref, w_ref, m_ref, v_ref, g_out, d_out, m_out, v_out):
        g = c_ref[0] * dm_ref[0:1, :]
        for r in range(1, N_DEV):
            g = g + c_ref[r] * dm_ref[r:r + 1, :]
        delta, mn, vn = _adamw(w_ref[...], g, m_ref[...], v_ref[...])
        g_out[...] = g
        d_out[...] = delta
        m_out[...] = mn
        v_out[...] = vn

    fix = lambda i: (0, 0)
    blk = (d, nc)
    return _call("ada_bwd_adam", body, (1,),
                 [(cs_t, (N_DEV, d, 1), lambda i: (0, 0, 0)), (dmod_cols, (N_DEV, nc), fix), (w, blk, fix), (m, blk, fix), (v, blk, fix)],
                 [((d, nc), F32, blk, fix)] * 4)


def _small_adam(parts, ws, ms, vs):
    n = len(ws)
    widths = [w.shape[1] for w in ws]
    total = parts.shape[1]
    assert sum(widths) + LANES == total

    def body(p_ref, *refs):
        w_refs, m_refs, v_refs = refs[:n], refs[n:2 * n], refs[2 * n:3 * n]
        outs = refs[3 * n:]
        g = p_ref[0:1, :]
        for k in range(1, N_DEV):
            g = g + p_ref[k:k + 1, :]
        off = 0
        for i, width in enumerate(widths):
            gi = g[:, off:off + width]
            delta, mn, vn = _adamw(w_refs[i][...], gi, m_refs[i][...], v_refs[i][...])
            for o_ref, val in zip(outs[4 * i:4 * i + 4], (gi, delta, mn, vn)):
                o_ref[...] = val
            off += width
        outs[4 * n][...] = g[:, off:off + LANES]

    fix = lambda i: (0, 0)
    vec = lambda a: (a, (1, a.shape[1]), fix)
    out_specs = [((1, width), F32, (1, width), fix) for width in widths for _ in range(4)]
    out_specs.append(((1, LANES), F32, (1, LANES), fix))
    res = _call("small_adam", body, (1,),
                [(parts, (N_DEV, total), fix)] + [vec(a) for a in list(ws) + list(ms) + list(vs)], out_specs)
    return [res[4 * i:4 * i + 4] for i in range(n)], res[4 * n]


def kernel(x, c, positions, ada_w, ada_b, pre_mix_g, post_mix_g, pre_ffn_g, post_ffn_g, w_in, ret_gn_g, w_ret_branch, w_sb_branch, w_out, w_ff1, w_ff2, loss_target, m_ada_w, m_ada_b, m_pre_mix_g, m_post_mix_g, m_pre_ffn_g, m_post_ffn_g, m_w_in, m_ret_gn_g, m_w_ret_branch, m_w_sb_branch, m_w_out, m_w_ff1, m_w_ff2, v_ada_w, v_ada_b, v_pre_mix_g, v_post_mix_g, v_pre_ffn_g, v_post_ffn_g, v_w_in, v_ret_gn_g, v_w_ret_branch, v_w_sb_branch, v_w_out, v_w_ff1, v_w_ff2):
    _, s, d = x.shape
    d_ff = w_ff1.shape[2] * N_DEV
    d_in = w_in.shape[2] * N_DEV
    me = 4 * lax.axis_index("x") + 2 * lax.axis_index("y") + lax.axis_index("c")
    x2, tgt = x[0], loss_target[0]

    core = lax.axis_index("c").astype(jnp.int32).reshape(1)
    bf = lambda w: w[0].astype(BF16)

    w_in_t, m_in_t, v_in_t = (jnp.swapaxes(a[0], 0, 1) for a in (w_in, m_w_in, v_w_in))

    c_all, g_in = _exchange("gather_in", [c, w_in_t.astype(BF16)], ["gather", "gather_chip"])
    c_all = c_all.reshape(N_DEV, d)

    n_ada = ada_w.shape[2]
    cs_all = _silu_rows(c_all)
    ada_b_cols = lax.dynamic_slice(ada_b, (0, me * n_ada), (1, n_ada))
    mod_cols = _ada_fwd(cs_all, ada_w[0], ada_b_cols)
    mod_all = _exchange("gather_mod", [mod_cols], ["gather"])[0]
    mod = lax.dynamic_index_in_dim(mod_all, me, axis=1, keepdims=False).reshape(1, 6 * d)

    tm = min(256, s)
    h, g_in = _pre_norm(x2, pre_mix_g, mod, 2 * tm, riders=([g_in], ["forward"]))
    wt_in = g_in.reshape(d_in, d)
    bf_t = lambda w: jnp.swapaxes(w[0], 0, 1).astype(BF16)
    small_w = [bf(w_ret_branch), bf_t(w_sb_branch), bf(w_out)]
    proj, *small_w = _matmul("in_proj", h, wt_in, "nt", s, 512, BF16, riders=(small_w, ["gather_chip"] * 3))
    pos_col = positions.reshape(s, 1).astype(F32)
    freqs = ROPE_BASE ** (-jnp.arange(0, RET_DQK, 2, dtype=F32) / RET_DQK)
    inv_freq = jnp.tile(freqs, LANES // (RET_DQK // 2)).reshape(1, LANES)
    log_gamma_np = np.log1p(-(2.0 ** (-5.0 - np.arange(HEADS))))
    log_gamma = jnp.asarray(log_gamma_np, F32)
    lg_lanes = jnp.asarray(np.repeat(log_gamma_np, RET_DQK).reshape(1, RET_QK), F32)
    idx_col = (jnp.arange(s, dtype=F32) - (s // 2)).reshape(s, 1)
    qk_rot, v_bf, qkv_sb, cos_t, sin_t = _prep(proj, pos_col, idx_col, inv_freq, lg_lanes, 2 * tm)
    tq, tk = min(256, s), min(128, s)
    tq_sb = min(SB_TQ, s)
    sb, sb_weights, *big_w = _sb_fwd(qkv_sb, tq_sb, tk, riders=([bf(w_ff2), bf_t(w_ff1)], ["gather_chip"] * 2))
    ret, retg, g_ret, g_sb, g_out, g_ff2, g_ff1 = _ret_fwd(qk_rot, v_bf, proj, ret_gn_g, log_gamma, tq,
                                                           riders=(small_w + big_w, ["forward"] * 5))
    wf_ret = g_ret.reshape(RET_V, d)
    wt_sb = g_sb.reshape(d, SB_W)
    wf_out = g_out.reshape(d, d)
    wt_ff1 = g_ff1.reshape(d_ff, d)
    wf_ff2 = g_ff2.reshape(d_ff, d)
    mixed, r_bf, s_bf = _merge(retg, sb, wf_ret, wt_sb, proj, 2 * tm, min(512, d))
    y, hres, h2 = _out_proj(mixed, wf_out, x2, mod, post_mix_g, pre_ffn_g, tm)
    u, act = _ff1(h2, wt_ff1, s, 512)
    dout, df, loss_sum, d_gt2, d_gp2 = _ff2_loss(act, wf_ff2, hres, tgt, mod, post_ffn_g, tm)

    du = _ff2_bwd(df, wf_ff2, u, s, 512)
    gw_ff2 = _matmul("grad_w_ff2", act, df, "tn", 512, d, BF16).reshape(N_DEV, d_ff // N_DEV, d)
    gw_ff1 = _matmul("grad_w_ff1", h2, du, "tn", d, d_ff // N_DEV, BF16, blocked_out=True)
    dhres, dy, d_sh2, d_sc2, d_g2, d_gt1, d_gp1, t_ff1, t_ff2 = _ff1_bwd(
        du, wt_ff1, hres, dout, y, mod, pre_ffn_g, post_mix_g, tm, riders=([gw_ff1, gw_ff2], ["pair"] * 2))
    s_ff1 = _pair_sum("pair_sum_ff1", gw_ff1, t_ff1, core, d)
    s_ff2 = _pair_sum("pair_sum_ff2", gw_ff2, t_ff2, core, d)
    gw_out = _matmul("grad_w_out", mixed, dy, "tn", 512, d, BF16).reshape(N_DEV, d // N_DEV, d)
    d_r, d_s, da_r, da_s, p_out = _out_bwd(dy, wf_out, proj, r_bf, s_bf, 2 * tm, min(512, d), riders=([gw_out], ["scatter"]))
    dretg, dsb, gw_ret, gw_sb = _branch_bwd(d_r, d_s, retg, sb, wf_ret, wt_sb)
    gw_ret = gw_ret.reshape(N_DEV, RET_V // N_DEV, d)
    dq_s, dk_s, dv_s, p_ff1, p_ff2 = _sb_bwd(qkv_sb, sb_weights, dsb, tq_sb, tk,
                                             riders=([s_ff1, s_ff2], ["chip_scatter"] * 2))
    dg_r, dret, d_gn = _gn_bwd(dretg, ret, proj, ret_gn_g, 2 * tm)
    dq_r, dk_r, dv_r, p_sb = _ret_bwd(qk_rot, v_bf, dret, log_gamma, tq, riders=([gw_sb], ["scatter"]))
    dproj, p_ret = _assemble_dproj(dq_r, dk_r, dv_r, dg_r, dq_s, dk_s, dv_s, da_r, da_s, cos_t, sin_t, idx_col, lg_lanes,
                                   tm, riders=([gw_ret], ["scatter"]))
    gw_in = _matmul("grad_w_in", dproj, h, "tn", 512, d, BF16).reshape(N_DEV, d_in // N_DEV, d)
    t_in = _exchange("pair_in", [gw_in], ["pair"])[0]
    s_in = _pair_sum("pair_sum_in", gw_in, t_in, core, d_in // N_DEV)
    grad_x, d_sh1, d_sc1, d_g1, p_in = _in_bwd(dproj, wt_in, x2, dhres, mod, pre_mix_g, tm,
                                               riders=([s_in], ["chip_scatter"]))
    loss_lanes = jnp.pad(loss_sum, ((0, 0), (0, LANES - 1)))
    small = jnp.concatenate([d_sh1, d_sc1, d_gt1, d_sh2, d_sc2, d_gt2, d_g1, d_gp1, d_g2, d_gp2, d_gn, loss_lanes], axis=1)
    small_all = _exchange("gather_small", [small], ["gather"])[0].reshape(N_DEV, small.shape[1])
    parts = [p_in, p_ret, p_sb, p_out, p_ff1, p_ff2]

    res = {}
    names = ["w_ret_branch", "w_sb_branch", "w_out", "w_ff1", "w_ff2"]
    ws = [w_ret_branch, w_sb_branch, w_out, w_ff1, w_ff2]
    ms = [m_w_ret_branch, m_w_sb_branch, m_w_out, m_w_ff1, m_w_ff2]
    vs = [v_w_ret_branch, v_w_sb_branch, v_w_out, v_w_ff1, v_w_ff2]
    for nm, p, w, m, v in zip(names, parts[1:], ws, ms, vs):
        res[nm] = [o[None] for o in _adam_reduce("adam_" + nm, p, w[0], m[0], v[0], 256)]
    res["w_in"] = [jnp.swapaxes(o, 0, 1)[None]
                   for o in _adam_reduce("adam_w_in", parts[0], w_in_t, m_in_t, v_in_t, d_in // N_DEV // 2)]
    dmod_cols = lax.dynamic_slice(small_all, (0, me * n_ada), (N_DEV, n_ada))
    res["ada_w"] = [o[None] for o in _ada_bwd_adam(cs_all.reshape(N_DEV, d, 1), dmod_cols, ada_w[0], m_ada_w[0], v_ada_w[0])]
    vec_names = ["ada_b", "pre_mix_g", "post_mix_g", "pre_ffn_g", "post_ffn_g", "ret_gn_g"]
    vec_res, loss_lanes = _small_adam(small_all,
                                      [ada_b, pre_mix_g, post_mix_g, pre_ffn_g, post_ffn_g, ret_gn_g],
                                      [m_ada_b, m_pre_mix_g, m_post_mix_g, m_pre_ffn_g, m_post_ffn_g, m_ret_gn_g],
                                      [v_ada_b, v_pre_mix_g, v_post_mix_g, v_pre_ffn_g, v_post_ffn_g, v_ret_gn_g])
    res.update(zip(vec_names, vec_res))
    loss = (0.5 / d) * loss_lanes[0, 0]
    order = ["ada_w", "ada_b", "pre_mix_g", "post_mix_g", "pre_ffn_g", "post_ffn_g", "w_in", "ret_gn_g",
             "w_ret_branch", "w_sb_branch", "w_out", "w_ff1", "w_ff2"]
    outs = [loss, grad_x[None]]
    for k in range(4):
        outs += [res[nm][k] for nm in order]
    return tuple(outs)
```

```python
import functools

import numpy as np
import jax
import jax.numpy as jnp
from jax import lax
from jax.experimental import pallas as pl
from jax.experimental.pallas import tpu as pltpu

F32 = jnp.float32
BF16 = jnp.bfloat16
N_DEV = 8
AXES = ("x", "y", "c")

EPS = 1e-6
CHUNK = 64
CHUNK_SHIFT = 6
HEADS = 8
RET_DQK = 64
RET_DV = 128
SB_DH = 64
RET_QK = HEADS * RET_DQK
RET_V = HEADS * RET_DV
SB_W = HEADS * SB_DH
ROPE_BASE = 10000.0
LANES = 128

ADAM_LR = 0.001
ADAM_B1 = 0.9
ADAM_B2 = 0.999
ADAM_EPS = 1e-08
ADAM_WD = 0.01
ADAM_STEP = 10

VMEM_LIMIT = 56 * 1024 * 1024

_NN = (((1,), (0,)), ((), ()))
_NT = (((1,), (1,)), ((), ()))
_TN = (((0,), (0,)), ((), ()))


def _dot(a, b, dims=_NN):
    if a.dtype != BF16:
        a = a.astype(BF16)
    if b.dtype != BF16:
        b = b.astype(BF16)
    return lax.dot_general(a, b, dims, preferred_element_type=F32)


def _sigmoid(x):
    return 1.0 / (1.0 + jnp.exp(-x))


def _rms(x, d):
    r = lax.rsqrt(jnp.sum(x * x, axis=1, keepdims=True) * (1.0 / d) + EPS)
    return x * r, r


def _rms_bwd(dn, n, r, d):
    return r * (dn - n * (jnp.sum(dn * n, axis=1, keepdims=True) * (1.0 / d)))


def _colsum(v):
    return jnp.sum(v, axis=0, keepdims=True)


def _accum(ref, val, first):
    @pl.when(first)
    def _():
        ref[...] = val

    @pl.when(jnp.logical_not(first))
    def _():
        ref[...] += val


ROW_SPLIT = 2


def _zero_at_start(refs):
    @pl.when(pl.program_id(0) == 0)
    def _():
        for r in refs:
            r[...] = jnp.zeros_like(r)


def _pieces(tm):
    step = tm // ROW_SPLIT
    return [slice(k * step, (k + 1) * step) for k in range(ROW_SPLIT)]


KIND_SLOTS = {"gather": N_DEV, "scatter": N_DEV, "gather_chip": N_DEV, "forward": N_DEV, "pair": N_DEV // 2,
              "chip_scatter": N_DEV // 2}
SEMS_PER_ARRAY = N_DEV - 1


def _exchange_copies(ins, outs, send_sems, recv_sems, local_sems, kinds):
    x, y, c = (lax.axis_index(a) for a in AXES)
    me, chip, sibling = 4 * x + 2 * y + c, 2 * x + y, (x, y, 1 - c)
    mesh_id = pl.DeviceIdType.MESH
    other_chips = []
    for k in range(1, N_DEV // 2):
        px = 1 - x if k & 2 else x
        py = 1 - y if k & 1 else y
        other_chips.append((px, py))
    copies = []
    for i, kind in enumerate(kinds):
        def remote(src, dst, k, to, i=i):
            return pltpu.make_async_remote_copy(
                src_ref=src, dst_ref=dst, send_sem=send_sems.at[i * SEMS_PER_ARRAY + k],
                recv_sem=recv_sems.at[i * SEMS_PER_ARRAY + k], device_id=to, device_id_type=mesh_id)

        if kind in ("gather", "scatter"):
            pick = (lambda ref, d: ref.at[d]) if kind == "scatter" else (lambda ref, d: ref)
            copies.append(pltpu.make_async_copy(pick(ins[i], me), outs[i].at[me], local_sems.at[i]))
            for k in range(1, N_DEV):
                to = (1 - x if k & 4 else x, 1 - y if k & 2 else y, 1 - c if k & 1 else c)
                copies.append(remote(pick(ins[i], 4 * to[0] + 2 * to[1] + to[2]), outs[i].at[me], k - 1, to))
        elif kind == "gather_chip":
            copies.append(pltpu.make_async_copy(ins[i], outs[i].at[me], local_sems.at[i]))
            copies.append(remote(ins[i], outs[i].at[me], 0, sibling))
            for k, (px, py) in enumerate(other_chips):
                copies.append(remote(ins[i], outs[i].at[me], 1 + k, (px, py, c)))
        elif kind == "forward":
            for k, (px, py) in enumerate(other_chips):
                slot = 4 * px + 2 * py + c
                copies.append(remote(outs[i].at[slot], outs[i].at[slot], k, sibling))
        elif kind == "pair":
            for k in range(N_DEV // 2):
                copies.append(remote(ins[i].at[2 * k + 1 - c], outs[i].at[k], k, sibling))
        elif kind == "chip_scatter":
            copies.append(pltpu.make_async_copy(ins[i].at[chip], outs[i].at[chip], local_sems.at[i]))
            for k, (px, py) in enumerate(other_chips):
                copies.append(remote(ins[i].at[2 * px + py], outs[i].at[chip], k, (px, py, c)))
        else:
            raise ValueError(kind)
    return copies


def _exchange_shapes(arrays, kinds):
    shapes = []
    for a, kind in zip(arrays, kinds):
        tail = a.shape if kind in ("gather", "gather_chip") else a.shape[1:]
        shapes.append(jax.ShapeDtypeStruct((KIND_SLOTS[kind],) + tuple(tail), a.dtype))
    return shapes


def _exchange_sems(n):
    return [pltpu.SemaphoreType.DMA((n * SEMS_PER_ARRAY,)), pltpu.SemaphoreType.DMA((n * SEMS_PER_ARRAY,)),
            pltpu.SemaphoreType.DMA((n,))]


def _call(name, body, grid, ins, outs, scratch=(), riders=None, prefetch=None):
    any_spec = pl.BlockSpec(memory_space=pl.ANY)
    in_specs = [pl.BlockSpec(memory_space=im) if bs is None else pl.BlockSpec(bs, im) for _, bs, im in ins]
    out_specs = [pl.BlockSpec(bs, im) for _, _, bs, im in outs]
    out_shape = [jax.ShapeDtypeStruct(s, d) for s, d, _, _ in outs]
    operands = [a for a, _, _ in ins]
    scratch = list(scratch)
    aliases = {}
    n_pre = 0 if prefetch is None else 1
    kernel = functools.partial(body) if prefetch is None else (lambda _, *refs: body(*refs))
    if riders is not None:
        arrays, kinds = riders
        nr, n_in, n_out, n_scr = len(arrays), len(ins), len(outs), len(scratch)

        def kernel(*refs):
            refs = refs[n_pre:]
            own_in, ride_in = refs[:n_in], refs[n_in:n_in + nr]
            own_out = refs[n_in + nr:n_in + nr + n_out]
            ride_out = refs[n_in + nr + n_out:n_in + 2 * nr + n_out]
            own_scr = refs[n_in + 2 * nr + n_out:n_in + 2 * nr + n_out + n_scr]
            sems = refs[n_in + 2 * nr + n_out + n_scr:]
            ids = [pl.program_id(a) for a in range(len(grid))]
            first = functools.reduce(jnp.logical_and, [i == 0 for i in ids])
            last = functools.reduce(jnp.logical_and, [i == g - 1 for i, g in zip(ids, grid)])

            @pl.when(first)
            def _():
                for cp in _exchange_copies(ride_in, ride_out, *sems, kinds):
                    cp.start()

            body(*own_in, *own_out, *own_scr)

            @pl.when(last)
            def _():
                for cp in _exchange_copies(ride_in, ride_out, *sems, kinds):
                    cp.wait()

        in_specs += [any_spec] * nr
        out_specs += [any_spec] * nr
        out_shape += _exchange_shapes(arrays, kinds)
        operands += list(arrays)
        scratch += _exchange_sems(nr)
        aliases = {n_pre + n_in + r: n_out + r for r, kind in enumerate(kinds) if kind == "forward"}
    params = pltpu.CompilerParams(dimension_semantics=("arbitrary",) * len(grid), vmem_limit_bytes=VMEM_LIMIT)
    if prefetch is None:
        return pl.pallas_call(kernel, name=name, grid=grid, in_specs=in_specs, out_specs=out_specs,
                              out_shape=out_shape, scratch_shapes=scratch, input_output_aliases=aliases,
                              compiler_params=params)(*operands)
    grid_spec = pltpu.PrefetchScalarGridSpec(num_scalar_prefetch=1, grid=grid, in_specs=in_specs,
                                             out_specs=out_specs, scratch_shapes=scratch)
    return pl.pallas_call(kernel, name=name, grid_spec=grid_spec, out_shape=out_shape,
                          input_output_aliases=aliases, compiler_params=params)(prefetch, *operands)


def _exchange(name, arrays, kinds):
    n = len(arrays)

    def body(*refs):
        copies = _exchange_copies(refs[:n], refs[n:2 * n], *refs[2 * n:], kinds)
        for cp in copies:
            cp.start()
        for cp in copies:
            cp.wait()

    any_spec = pl.BlockSpec(memory_space=pl.ANY)
    return pl.pallas_call(
        functools.partial(body),
        name=name,
        in_specs=[any_spec] * n,
        out_specs=[any_spec] * n,
        out_shape=_exchange_shapes(arrays, kinds),
        scratch_shapes=_exchange_sems(n),
        input_output_aliases={i: i for i, kind in enumerate(kinds) if kind == "forward"},
    )(*arrays)


def _pair_sum(name, mine, theirs, my_core, tr):
    _, rws, cls = mine.shape
    tr = min(tr, rws)

    def body(a_ref, b_ref, o_ref):
        o_ref[...] = (a_ref[...].astype(F32) + b_ref[...].astype(F32)).astype(o_ref.dtype)

    return _call(name, body, (N_DEV // 2, rws // tr),
                 [(mine, (None, tr, cls), lambda k, r, core: (2 * k + core[0], r, 0)),
                  (theirs, (None, tr, cls), lambda k, r, core: (k, r, 0))],
                 [((N_DEV // 2, rws, cls), mine.dtype, (None, tr, cls), lambda k, r, core: (k, r, 0))],
                 prefetch=my_core)[0]


def _matmul(name, a, b, kind, tm, tn, out_dtype, blocked_out=False, riders=None):
    if kind == "tn":
        kdim, m = a.shape
    else:
        m, kdim = a.shape
    n = b.shape[0] if kind == "nt" else b.shape[1]
    tm, tn = min(tm, m), min(tn, n)
    dims = {"nn": _NN, "nt": _NT, "tn": _TN}[kind]

    def body(a_ref, b_ref, o_ref):
        o_ref[...] = _dot(a_ref[...], b_ref[...], dims).astype(o_ref.dtype)

    a_spec = (a, (kdim, tm), lambda j, i: (0, i)) if kind == "tn" else (a, (tm, kdim), lambda j, i: (i, 0))
    b_spec = (b, (tn, kdim), lambda j, i: (j, 0)) if kind == "nt" else (b, (kdim, tn), lambda j, i: (0, j))
    if blocked_out:
        out = ((n // tn, m, tn), out_dtype, (None, tm, tn), lambda j, i: (j, i, 0))
    else:
        out = ((m, n), out_dtype, (tm, tn), lambda j, i: (i, j))
    res = _call(name, body, (n // tn, m // tm), [a_spec, b_spec], [out], riders=riders)
    return res[0] if riders is None else res


def _ada_fwd(cs_all, ada_w, ada_b_cols):
    def body(c_ref, w_ref, b_ref, o_ref):
        o_ref[...] = lax.dot_general(c_ref[...], w_ref[...], _NN, preferred_element_type=F32,
                                     precision=lax.Precision.HIGHEST) + b_ref[...]

    r, d = cs_all.shape
    nc = ada_w.shape[1]
    return _call("ada_fwd", body, (1,),
                 [(cs_all, (r, d), lambda i: (0, 0)), (ada_w, (d, nc), lambda i: (0, 0)),
                  (ada_b_cols, (1, nc), lambda i: (0, 0))],
                 [((r, nc), F32, (r, nc), lambda i: (0, 0))])[0]


def _silu_rows(c_all):
    def body(c_ref, o_ref):
        v = c_ref[...]
        o_ref[...] = v * _sigmoid(v)

    return _call("silu_c", body, (1,), [(c_all, c_all.shape, lambda i: (0, 0))],
                 [(c_all.shape, F32, c_all.shape, lambda i: (0, 0))])[0]


def _pre_norm(x, g, mod, tm, riders=None):
    s, d = x.shape

    def body(x_ref, g_ref, mod_ref, h_ref):
        n, _ = _rms(x_ref[...], d)
        sh, sc = mod_ref[:, 0:d], mod_ref[:, d:2 * d]
        h_ref[...] = (n * g_ref[...] * (1.0 + sc) + sh).astype(BF16)

    return _call("pre_norm", body, (s // tm,),
                 [(x, (tm, d), lambda i: (i, 0)), (g, (1, d), lambda i: (0, 0)),
                  (mod, (1, 6 * d), lambda i: (0, 0))],
                 [((s, d), BF16, (tm, d), lambda i: (i, 0))], riders=riders)


LOG2E = 1.4426950408889634
LN2 = 0.6931471805599453


def _decay_scale(lg_ref, idx, g, sign):
    return jnp.exp((sign * idx) * lg_ref[:, g * LANES:(g + 1) * LANES])


def _prep(proj, pos_col, idx_col, inv_freq, lg_lanes, tm):
    s = proj.shape[0]
    sb_off = (2 * RET_QK + 2 * RET_V) // (3 * SB_W)
    n_q = RET_QK // LANES

    def body(qk_ref, v_ref, sb_ref, pos_ref, idx_ref, f_ref, lg_ref, qk_out, v_out, sb_out, cos_out, sin_out):
        ang = pos_ref[...] * f_ref[...]
        lane = lax.broadcasted_iota(jnp.int32, (1, LANES), 1)
        first = jnp.bitwise_and(lane, RET_DQK - 1) < (RET_DQK // 2)
        cos = jnp.cos(ang)
        sin = jnp.where(first, -1.0, 1.0) * jnp.sin(ang)
        cos_out[...] = cos
        sin_out[...] = sin
        idx = idx_ref[...]
        for g in range(2 * n_q):
            v = qk_ref[:, g * LANES:(g + 1) * LANES].astype(F32)
            sw = jnp.where(first, pltpu.roll(v, LANES - RET_DQK // 2, 1), pltpu.roll(v, RET_DQK // 2, 1))
            r = v * cos + sw * sin
            if g < n_q:
                r = r * _decay_scale(lg_ref, idx, g, 1.0)
            else:
                r = r * (_decay_scale(lg_ref, idx, g - n_q, -1.0) * (RET_DQK ** -0.5))
            qk_out[:, g * LANES:(g + 1) * LANES] = r.astype(BF16)
        v_out[...] = v_ref[...].astype(BF16)
        sb_out[:, 0:SB_W] = (sb_ref[:, 0:SB_W].astype(F32) * (SB_DH ** -0.5 * LOG2E)).astype(BF16)
        sb_out[:, SB_W:3 * SB_W] = sb_ref[:, SB_W:3 * SB_W].astype(BF16)

    return _call("prep", body, (s // tm,),
                 [(proj, (tm, 2 * RET_QK), lambda i: (i, 0)),
                  (proj, (tm, RET_V), lambda i: (i, 2 * RET_QK // RET_V)),
                  (proj, (tm, 3 * SB_W), lambda i: (i, sb_off)),
                  (pos_col, (tm, 1), lambda i: (i, 0)),
                  (idx_col, (tm, 1), lambda i: (i, 0)),
                  (inv_freq, (1, LANES), lambda i: (0, 0)),
                  (lg_lanes, (1, RET_QK), lambda i: (0, 0))],
                 [((s, 2 * RET_QK), BF16, (tm, 2 * RET_QK), lambda i: (i, 0)),
                  ((s, RET_V), BF16, (tm, RET_V), lambda i: (i, 0)),
                  ((s, 3 * SB_W), BF16, (tm, 3 * SB_W), lambda i: (i, 0)),
                  ((s, LANES), F32, (tm, LANES), lambda i: (i, 0)),
                  ((s, LANES), F32, (tm, LANES), lambda i: (i, 0))])


def _head_mask(hh):
    lane = lax.broadcasted_iota(jnp.int32, (1, LANES), 1)
    return (lane >= RET_DQK) if hh else (lane < RET_DQK)


def _masked(v, m):
    return jnp.where(m, v, jnp.zeros_like(v))


SB_GROUP = 4
SB_TQ = 256


def _stack_heads(v):
    return jnp.concatenate([_masked(v, _head_mask(0)), _masked(v, _head_mask(1))], axis=0)


def _side_by_side(v, t):
    return jnp.concatenate([v[:t], v[t:]], axis=1)


def _tile_pos(i, j, tq, tk):
    row = jnp.bitwise_and(lax.broadcasted_iota(jnp.int32, (2 * tq, tk), 0), tq - 1) + i * tq
    col = lax.broadcasted_iota(jnp.int32, (2 * tq, tk), 1) + j * tk
    return row, col


def _n_groups(i, tq, tk, grp):
    return ((i + 1) * (tq // tk) + grp - 1) // grp


def _n_full(i, tq, tk, grp):
    return (i * (tq // tk)) // grp


def _key_rows(j, tk):
    return pl.ds(pl.multiple_of(j * tk, tk), tk)


def _ret_weight(lg_rows, i, j, tq, tk):
    row, col = _tile_pos(i, j, tq, tk)
    same = jnp.right_shift(col, CHUNK_SHIFT) == jnp.right_shift(row, CHUNK_SHIFT)
    later = jnp.where(same, jnp.exp((2.0 * lg_rows) * (col - row).astype(F32)), 0.0)
    return jnp.where(col <= row, 1.0, later)


def _lg_rows(lg_ref, hp, tq):
    first = lax.broadcasted_iota(jnp.int32, (2 * tq, 1), 0) < tq
    return jnp.where(first, lg_ref[2 * hp], lg_ref[2 * hp + 1])


def _check_tiles(s, tq, tk, grp):
    assert tq % tk == 0 and tq & (tq - 1) == 0 and tk & (tk - 1) == 0
    assert s % tq == 0 and (s // tk) % grp == 0 and s // tk <= LANES


def _pair_mask():
    r = lax.broadcasted_iota(jnp.int32, (LANES, 2 * RET_DV), 0) >= RET_DQK
    c = lax.broadcasted_iota(jnp.int32, (LANES, 2 * RET_DV), 1) >= RET_DV
    return (r == c).astype(F32)


def _ret_block(lg_ref, hp, i, t, qb, kb):
    w = _ret_weight(_lg_rows(lg_ref, hp, t), i, i, t, t)
    return _dot(_stack_heads(qb), kb, _NT), w


RET_PAIRS = 2


def _lanes(ref, p, width):
    return ref[:, p * width:(p + 1) * width]


def _ret_fwd(qk_rot, v_bf, proj, gn_g, log_gamma, t, riders=None):
    s = qk_rot.shape[0]
    n_pair = HEADS // 2
    pw = 2 * RET_DV
    wq, wv = RET_PAIRS * LANES, RET_PAIRS * pw
    gate_off = (2 * RET_QK + RET_V) // wv
    assert s % t == 0 and t % CHUNK == 0 and t & (t - 1) == 0 and n_pair % RET_PAIRS == 0

    def body(lg_ref, q_ref, k_ref, v_ref, g_ref, w_ref, ret_ref, rg_ref, state_ref):
        hg, i = pl.program_id(0), pl.program_id(1)

        @pl.when(i == 0)
        def _():
            state_ref[...] = jnp.zeros_like(state_ref)

        pairs = range(RET_PAIRS)
        qbs = [_lanes(q_ref, p, LANES) for p in pairs]
        kbs = [_lanes(k_ref, p, LANES) for p in pairs]
        vbs = [_lanes(v_ref, p, pw) for p in pairs]
        zws = [_ret_block(lg_ref, hg * RET_PAIRS + p, i, t, qbs[p], kbs[p]) for p in pairs]
        ps = [(z * w).astype(BF16) for z, w in zws]
        outs = [jnp.concatenate([_dot(ps[p][:t], vbs[p][:, 0:RET_DV]), _dot(ps[p][t:], vbs[p][:, RET_DV:pw])], axis=1)
                + _dot(qbs[p], state_ref[p]) for p in pairs]
        for p in pairs:
            state_ref[p] += _pair_mask() * _dot(kbs[p], vbs[p], _TN)
        for p in pairs:
            for hh in range(2):
                cols = slice(p * pw + hh * RET_DV, p * pw + (hh + 1) * RET_DV)
                o = outs[p][:, hh * RET_DV:(hh + 1) * RET_DV]
                ret_ref[:, cols] = o
                mu = jnp.sum(o, axis=1, keepdims=True) * (1.0 / RET_DV)
                xc = o - mu
                var = jnp.sum(xc * xc, axis=1, keepdims=True) * (1.0 / RET_DV)
                nrm = xc * lax.rsqrt(var + EPS) * w_ref[:, cols]
                g = g_ref[:, cols].astype(F32)
                rg_ref[:, cols] = (g * _sigmoid(g) * nrm).astype(BF16)

    blk = lambda hg, i: (i, hg)
    return _call("ret_fwd", body, (n_pair // RET_PAIRS, s // t),
                 [(log_gamma, None, pltpu.SMEM),
                  (qk_rot, (t, wq), blk),
                  (qk_rot, (t, wq), lambda hg, i: (i, n_pair // RET_PAIRS + hg)),
                  (v_bf, (t, wv), blk),
                  (proj, (t, wv), lambda hg, i: (i, gate_off + hg)),
                  (gn_g, (1, wv), lambda hg, i: (0, hg))],
                 [((s, RET_V), F32, (t, wv), blk), ((s, RET_V), BF16, (t, wv), blk)],
                 scratch=[pltpu.VMEM((RET_PAIRS, LANES, pw), F32)], riders=riders)


def _tri(tk, strict_upper):
    r = lax.broadcasted_iota(jnp.int32, (tk, tk), 0)
    cc = lax.broadcasted_iota(jnp.int32, (tk, tk), 1)
    return ((r > cc) if strict_upper else (r < cc)).astype(BF16)


def _diagonal_step(i, tq, tk, make, carry):
    if (tq // tk) % SB_GROUP == 0:
        return make(SB_GROUP)(0, carry)
    assert 2 * (tq // tk) == SB_GROUP
    half = lax.rem(i, 2) == 0
    return lax.cond(half, lambda cr: make(SB_GROUP // 2)(0, cr), lambda cr: make(SB_GROUP)(0, cr), carry)


def _sb_valid(i, j, tq, tk):
    row, col = _tile_pos(i, j, tq, tk)
    return col < row


def _sb_fwd(qkv, tq, tk, riders=None):
    s = qkv.shape[0]
    n_pair = HEADS // 2
    _check_tiles(s, tq, tk, SB_GROUP)

    def body(q_ref, k_ref, v_ref, o_ref, a_ref):
        i = pl.program_id(1)
        upper = _tri(tk, True)
        qs = _stack_heads(q_ref[...])
        n_full, n_groups = _n_full(i, tq, tk, SB_GROUP), _n_groups(i, tq, tk, SB_GROUP)

        def make_step(near_diagonal, last, n_sub=SB_GROUP):
            def step(n, carry):
                c, o = carry
                g = last - 1 - n
                js = [g * SB_GROUP + sub for sub in range(n_sub)]
                zs = [_dot(qs, k_ref[_key_rows(j, tk), :], _NT) for j in js]
                log1ps = [jnp.log2(1.0 + jnp.exp2(-jnp.abs(z))) for z in zs]
                log_1ms = [-jnp.maximum(z, 0.0) - t for z, t in zip(zs, log1ps)]
                log_bs = [jnp.minimum(z, 0.0) - t for z, t in zip(zs, log1ps)]
                if near_diagonal:
                    valids = [_sb_valid(i, j, tq, tk) for j in js]
                    log_1ms = [jnp.where(v, l, 0.0) for v, l in zip(valids, log_1ms)]
                sticks = [_dot(l, upper) for l in log_1ms]
                sums = [jnp.sum(l, axis=1, keepdims=True) for l in log_1ms]
                cs = [None] * n_sub
                for sub in reversed(range(n_sub)):
                    cs[sub] = c
                    c = c + sums[sub]
                for sub, j in enumerate(js):
                    a = jnp.exp2(log_bs[sub] + sticks[sub] + cs[sub])
                    if near_diagonal:
                        a = jnp.where(valids[sub], a, 0.0)
                    a = a.astype(BF16)
                    a_ref[j] = a
                    o = o + _dot(_side_by_side(a, tq), _stack_heads(v_ref[_key_rows(j, tk), :]))
                return c, o
            return step

        carry = (jnp.zeros((2 * tq, 1), F32), jnp.zeros((tq, LANES), F32))
        carry = _diagonal_step(i, tq, tk, lambda n_sub: make_step(True, n_groups, n_sub), carry)
        _, acc = lax.fori_loop(0, n_full, make_step(False, n_full), carry)
        o_ref[...] = acc.astype(BF16)

    n_kb = s // tk
    return _call("sb_fwd", body, (n_pair, s // tq),
                 [(qkv, (tq, LANES), lambda hp, i: (i, hp)),
                  (qkv, (s, LANES), lambda hp, i: (0, n_pair + hp)),
                  (qkv, (s, LANES), lambda hp, i: (0, 2 * n_pair + hp))],
                 [((s, SB_W), BF16, (tq, LANES), lambda hp, i: (i, hp)),
                  ((n_pair, s // tq, n_kb, 2 * tq, tk), BF16, (None, None, n_kb, 2 * tq, tk),
                   lambda hp, i: (hp, i, 0, 0, 0))], riders=riders)


def _merge_out(retg, sb, w_ret, w_sb_t, w_out, proj, x, mod, gp1, g2, tm):
    s, d = x.shape
    gw = min(512, d)
    n_g = d // gw
    ar_off = (2 * RET_QK + 2 * RET_V + 3 * SB_W) // gw

    def body(rg_ref, sb_ref, wr_ref, ws_ref, wo_ref, *refs):
        gate_refs, (x_ref, mod_ref, gp_ref, g2_ref, mix_ref, r_ref, s_ref, y_ref, hres_ref, h2_ref) = refs[:2 * n_g], refs[2 * n_g:]
        for rows in _pieces(tm):
            rr = _dot(rg_ref[rows, :], wr_ref[...])
            ss = _dot(sb_ref[rows, :], ws_ref[...], _NT)
            a_r = jnp.concatenate([g[rows, :] for g in gate_refs[:n_g]], axis=1).astype(F32)
            a_s = jnp.concatenate([g[rows, :] for g in gate_refs[n_g:]], axis=1).astype(F32)
            mixed = (_sigmoid(a_r) * rr + _sigmoid(a_s) * ss).astype(BF16)
            mix_ref[rows, :] = mixed
            r_ref[rows, :] = rr.astype(BF16)
            s_ref[rows, :] = ss.astype(BF16)
            y = _dot(mixed, wo_ref[...])
            y_ref[rows, :] = y
            ny, _ = _rms(y, d)
            hres = x_ref[rows, :] + mod_ref[:, 2 * d:3 * d] * (ny * gp_ref[...])
            hres_ref[rows, :] = hres
            n2, _ = _rms(hres, d)
            h2_ref[rows, :] = (n2 * g2_ref[...] * (1.0 + mod_ref[:, 4 * d:5 * d]) + mod_ref[:, 3 * d:4 * d]).astype(BF16)

    row = lambda i: (i, 0)
    fix = lambda i: (0, 0)
    tile_bf = ((s, d), BF16, (tm, d), row)
    tile_f = ((s, d), F32, (tm, d), row)
    return _call("merge_out", body, (s // tm,),
                 [(retg, (tm, RET_V), row), (sb, (tm, SB_W), row), (w_ret, (RET_V, d), fix), (w_sb_t, (d, SB_W), fix),
                  (w_out, (d, d), fix)]
                 + [(proj, (tm, gw), functools.partial(lambda i, k: (i, ar_off + k), k=k)) for k in range(2 * n_g)]
                 + [(x, (tm, d), row), (mod, (1, 6 * d), fix), (gp1, (1, d), fix), (g2, (1, d), fix)],
                 [tile_bf, tile_bf, tile_bf, tile_f, tile_f, tile_bf])


def _ff1(h2, w_ff1_t, tm, tn):
    s, f = h2.shape[0], w_ff1_t.shape[0]
    tm = min(tm, s)

    def body(a_ref, w_ref, u_ref, act_ref):
        u = _dot(a_ref[...], w_ref[...], _NT)
        r = jnp.maximum(u, 0.0)
        u_ref[...] = u.astype(BF16)
        act_ref[...] = (r * r).astype(BF16)

    d = h2.shape[1]
    return _call("ff1", body, (f // tn, s // tm),
                 [(h2, (tm, d), lambda j, i: (i, 0)), (w_ff1_t, (tn, d), lambda j, i: (j, 0))],
                 [((s, f), BF16, (tm, tn), lambda j, i: (i, j))] * 2)


def _ff2_loss(act, w_ff2, hres, target, mod, gp2, tm):
    s, d = hres.shape
    f = act.shape[1]

    def body(a_ref, w_ref, h_ref, t_ref, mod_ref, gp_ref, dout_ref, df_ref, loss_ref, dgt_ref, dgp_ref):
        _zero_at_start([loss_ref, dgt_ref, dgp_ref])
        gt, gp = mod_ref[:, 5 * d:6 * d], gp_ref[...]
        for rows in _pieces(tm):
            ff = _dot(a_ref[rows, :], w_ref[...])
            nf, rf = _rms(ff, d)
            out = h_ref[rows, :] + gt * (nf * gp)
            err = out - t_ref[rows, :]
            sq = jnp.sum(err * err, axis=1, keepdims=True)
            loss_ref[...] += jnp.sum(sq, axis=0, keepdims=True)
            dout = err * (1.0 / d)
            dout_ref[rows, :] = dout
            dgt_ref[...] += _colsum(dout * (nf * gp))
            dgp_ref[...] += _colsum(dout * gt * nf)
            df_ref[rows, :] = _rms_bwd(dout * gt * gp, nf, rf, d).astype(BF16)

    row = lambda i: (i, 0)
    fix = lambda i: (0, 0)
    return _call("ff2_loss", body, (s // tm,),
                 [(act, (tm, f), row), (w_ff2, (f, d), fix), (hres, (tm, d), row), (target, (tm, d), row),
                  (mod, (1, 6 * d), fix), (gp2, (1, d), fix)],
                 [((s, d), F32, (tm, d), row), ((s, d), BF16, (tm, d), row), ((1, 1), F32, (1, 1), fix),
                  ((1, d), F32, (1, d), fix), ((1, d), F32, (1, d), fix)])


def _ffn_bwd(df, w_ff2, u, act, h2, tn):
    s, d = df.shape
    f = w_ff2.shape[0]

    def body(df_ref, w_ref, u_ref, act_ref, h2_ref, du_ref, gw2_ref, gw1_ref):
        dfb = df_ref[...]
        du = (_dot(dfb, w_ref[...], _NT) * (2.0 * jnp.maximum(u_ref[...].astype(F32), 0.0))).astype(BF16)
        du_ref[...] = du
        gw2_ref[...] = _dot(act_ref[...], dfb, _TN).astype(BF16)
        gw1_ref[...] = _dot(h2_ref[...], du, _TN).astype(BF16)

    fix = lambda j: (0, 0)
    col = lambda j: (0, j)
    return _call("ffn_bwd", body, (f // tn,),
                 [(df, (s, d), fix), (w_ff2, (tn, d), lambda j: (j, 0)), (u, (s, tn), col), (act, (s, tn), col),
                  (h2, (s, d), fix)],
                 [((s, f), BF16, (s, tn), col), ((f, d), BF16, (tn, d), lambda j: (j, 0)),
                  ((f // tn, d, tn), BF16, (None, d, tn), lambda j: (j, 0, 0))])


def _ff1_bwd(du, w_ff1_t, hres, dout, y, mod, g2, gp1, tm, riders=None):
    s, d = hres.shape
    f = du.shape[1]

    def body(a_ref, w_ref, h_ref, do_ref, y_ref, mod_ref, g2_ref, gp_ref,
             dh_ref, dy_ref, dsh_ref, dsc_ref, dg2_ref, dgt_ref, dgp_ref):
        _zero_at_start([dsh_ref, dsc_ref, dg2_ref, dgt_ref, dgp_ref])
        g2, sc2 = g2_ref[...], mod_ref[:, 4 * d:5 * d]
        gt, gp = mod_ref[:, 2 * d:3 * d], gp_ref[...]
        for rows in _pieces(tm):
            dh2 = _dot(a_ref[rows, :], w_ref[...])
            n2, r2 = _rms(h_ref[rows, :], d)
            dsh_ref[...] += _colsum(dh2)
            dsc_ref[...] += _colsum(dh2 * n2 * g2)
            dg2_ref[...] += _colsum(dh2 * n2 * (1.0 + sc2))
            dhres = do_ref[rows, :] + _rms_bwd(dh2 * g2 * (1.0 + sc2), n2, r2, d)
            dh_ref[rows, :] = dhres
            ny, ry = _rms(y_ref[rows, :], d)
            dgt_ref[...] += _colsum(dhres * (ny * gp))
            dgp_ref[...] += _colsum(dhres * gt * ny)
            dy_ref[rows, :] = _rms_bwd(dhres * gt * gp, ny, ry, d).astype(BF16)

    row = lambda i: (i, 0)
    fix = lambda i: (0, 0)
    vec = ((1, d), F32, (1, d), fix)
    return _call("ff1_bwd", body, (s // tm,),
                 [(du, (tm, f), row), (w_ff1_t, (f, d), fix), (hres, (tm, d), row), (dout, (tm, d), row),
                  (y, (tm, d), row), (mod, (1, 6 * d), fix), (g2, (1, d), fix), (gp1, (1, d), fix)],
                 [((s, d), F32, (tm, d), row), ((s, d), BF16, (tm, d), row), vec, vec, vec, vec, vec], riders=riders)


def _out_bwd(dy, w_out, proj, r_bf, s_bf, tm, tn, riders=None):
    s, d = dy.shape
    ar_off = (2 * RET_QK + 2 * RET_V + 3 * SB_W) // tn
    as_off = ar_off + d // tn

    def body(a_ref, w_ref, ar_ref, as_ref, r_ref, s_ref, dr_ref, ds_ref, dar_ref, das_ref):
        dm = _dot(a_ref[...], w_ref[...], _NT)
        sr, ss = _sigmoid(ar_ref[...].astype(F32)), _sigmoid(as_ref[...].astype(F32))
        dr_ref[...] = (dm * sr).astype(BF16)
        ds_ref[...] = (dm * ss).astype(BF16)
        dar_ref[...] = (dm * r_ref[...].astype(F32) * sr * (1.0 - sr)).astype(BF16)
        das_ref[...] = (dm * s_ref[...].astype(F32) * ss * (1.0 - ss)).astype(BF16)

    tile = (tm, tn)
    here = lambda j, i: (i, j)
    return _call("out_bwd", body, (d // tn, s // tm),
                 [(dy, (tm, d), lambda j, i: (i, 0)), (w_out, (tn, d), lambda j, i: (j, 0)),
                  (proj, tile, lambda j, i: (i, ar_off + j)), (proj, tile, lambda j, i: (i, as_off + j)),
                  (r_bf, tile, here), (s_bf, tile, here)],
                 [((s, d), BF16, tile, here)] * 4, riders=riders)


def _branch_bwd(d_r, d_s, retg, sb, w_ret, w_sb_t):
    s, d = d_r.shape
    half_v, half_s, half_d = RET_V // 2, SB_W // 2, d // 2
    per_dev = d // N_DEV
    n_blk = half_d // per_dev

    def body(dr_ref, ds_ref, rg_ref, sb_ref, wr_ref, ws_ref, dretg_ref, dsb_ref, gwr_ref, gws_ref):
        i = pl.program_id(0)
        dr, ds = dr_ref[...], ds_ref[...]
        dretg_ref[...] = _dot(dr, wr_ref[...], _NT).astype(BF16)
        dsb_ref[...] = _dot(ds, ws_ref[...]).astype(BF16)
        gwr_ref[...] = _dot(rg_ref[...], dr, _TN).astype(BF16)
        cols = pl.ds(pl.multiple_of(i * half_d, half_d), half_d)
        gws = _dot(sb_ref[...], ds_ref[:, cols], _TN).astype(BF16)
        for k in range(n_blk):
            gws_ref[k] = gws[:, k * per_dev:(k + 1) * per_dev]

    fix = lambda i: (0, 0)
    return _call("branch_bwd", body, (2,),
                 [(d_r, (s, d), fix), (d_s, (s, d), fix), (retg, (s, half_v), lambda i: (0, i)), (sb, (s, SB_W), fix),
                  (w_ret, (half_v, d), lambda i: (i, 0)), (w_sb_t, (d, half_s), lambda i: (0, i))],
                 [((s, RET_V), BF16, (s, half_v), lambda i: (0, i)), ((s, SB_W), BF16, (s, half_s), lambda i: (0, i)),
                  ((RET_V, d), BF16, (half_v, d), lambda i: (i, 0)),
                  ((N_DEV, SB_W, per_dev), BF16, (n_blk, SB_W, per_dev), lambda i: (i, 0, 0))])


def _gn_bwd(dretg, ret, proj, gn_g, tm, riders=None):
    s = ret.shape[0]
    gate_off = (2 * RET_QK + RET_V) // RET_V

    def body(d_ref, r_ref, g_ref, w_ref, dg_ref, dret_ref, dw_ref):
        first = pl.program_id(0) == 0
        for h in range(HEADS):
            cols = slice(h * RET_DV, (h + 1) * RET_DV)
            o, g, w, dr = r_ref[:, cols], g_ref[:, cols].astype(F32), w_ref[:, cols], d_ref[:, cols].astype(F32)
            mu = jnp.sum(o, axis=1, keepdims=True) * (1.0 / RET_DV)
            xc = o - mu
            rstd = lax.rsqrt(jnp.sum(xc * xc, axis=1, keepdims=True) * (1.0 / RET_DV) + EPS)
            n = xc * rstd
            sg = _sigmoid(g)
            silu = g * sg
            dg_ref[:, cols] = (dr * n * w * (sg * (1.0 + g * (1.0 - sg)))).astype(BF16)
            _accum(dw_ref.at[:, cols], _colsum(dr * silu * n), first)
            dn = dr * silu * w
            m1 = jnp.sum(dn, axis=1, keepdims=True) * (1.0 / RET_DV)
            m2 = jnp.sum(dn * n, axis=1, keepdims=True) * (1.0 / RET_DV)
            dret_ref[:, cols] = (rstd * (dn - m1 - n * m2)).astype(BF16)

    row = lambda i: (i, 0)
    fix = lambda i: (0, 0)
    return _call("gn_bwd", body, (s // tm,),
                 [(dretg, (tm, RET_V), row), (ret, (tm, RET_V), row),
                  (proj, (tm, RET_V), lambda i: (i, gate_off)), (gn_g, (1, RET_V), fix)],
                 [((s, RET_V), BF16, (tm, RET_V), row), ((s, RET_V), BF16, (tm, RET_V), row),
                  ((1, RET_V), F32, (1, RET_V), fix)], riders=riders)


def _ret_bwd(qk_rot, v_bf, dret, log_gamma, t, riders=None):
    s = qk_rot.shape[0]
    n_pair = HEADS // 2
    pw = 2 * RET_DV
    wq, wv = RET_PAIRS * LANES, RET_PAIRS * pw
    n_blk = s // t
    pairs = range(RET_PAIRS)

    def load(q_ref, k_ref, v_ref, do_ref):
        return ([_lanes(q_ref, p, LANES) for p in pairs], [_lanes(k_ref, p, LANES) for p in pairs],
                [_lanes(v_ref, p, pw) for p in pairs], [_lanes(do_ref, p, pw) for p in pairs])

    def d_scores(lg_ref, hp, i, qb, kb, vb, dob):
        z, w = _ret_block(lg_ref, hp, i, t, qb, kb)
        dp = jnp.concatenate([_dot(dob[:, 0:RET_DV], vb[:, 0:RET_DV], _NT),
                              _dot(dob[:, RET_DV:pw], vb[:, RET_DV:pw], _NT)], axis=0)
        return (z * w).astype(BF16), (dp * w).astype(BF16)

    def up_body(lg_ref, q_ref, k_ref, v_ref, do_ref, dq_ref, state_ref):
        hg, i = pl.program_id(0), pl.program_id(1)

        @pl.when(i == 0)
        def _():
            state_ref[...] = jnp.zeros_like(state_ref)

        qbs, kbs, vbs, dobs = load(q_ref, k_ref, v_ref, do_ref)
        dss = [d_scores(lg_ref, hg * RET_PAIRS + p, i, qbs[p], kbs[p], vbs[p], dobs[p])[1] for p in pairs]
        for p in pairs:
            dq_ref[:, p * LANES:(p + 1) * LANES] = (_dot(_side_by_side(dss[p], t), _stack_heads(kbs[p]))
                                                    + _dot(dobs[p], state_ref[p], _NT)).astype(BF16)
        for p in pairs:
            state_ref[p] += _pair_mask() * _dot(kbs[p], vbs[p], _TN)

    def down_body(lg_ref, q_ref, k_ref, v_ref, do_ref, dk_ref, dv_ref, state_ref):
        hg, i = pl.program_id(0), n_blk - 1 - pl.program_id(1)

        @pl.when(pl.program_id(1) == 0)
        def _():
            state_ref[...] = jnp.zeros_like(state_ref)

        qbs, kbs, vbs, dobs = load(q_ref, k_ref, v_ref, do_ref)
        both = [d_scores(lg_ref, hg * RET_PAIRS + p, i, qbs[p], kbs[p], vbs[p], dobs[p]) for p in pairs]
        for p in pairs:
            pp, ds = both[p]
            later = state_ref[p]
            dv_ref[:, p * pw:(p + 1) * pw] = (jnp.concatenate(
                [_dot(pp[:t], dobs[p][:, 0:RET_DV], _TN), _dot(pp[t:], dobs[p][:, RET_DV:pw], _TN)],
                axis=1) + _dot(kbs[p], later)).astype(BF16)
            dk_ref[:, p * LANES:(p + 1) * LANES] = (_dot(ds, _stack_heads(qbs[p]), _TN)
                                                    + _dot(vbs[p], later, _NT)).astype(BF16)
        for p in pairs:
            state_ref[p] += _pair_mask() * _dot(qbs[p], dobs[p], _TN)

    n_grp = n_pair // RET_PAIRS

    def ins(order):
        return [(log_gamma, None, pltpu.SMEM),
                (qk_rot, (t, wq), lambda hg, i: (order(i), hg)),
                (qk_rot, (t, wq), lambda hg, i: (order(i), n_grp + hg)),
                (v_bf, (t, wv), lambda hg, i: (order(i), hg)),
                (dret, (t, wv), lambda hg, i: (order(i), hg))]

    up = lambda i: i
    down = lambda i: n_blk - 1 - i
    scratch = [pltpu.VMEM((RET_PAIRS, LANES, pw), F32)]
    dq = _call("ret_bwd_q", up_body, (n_grp, n_blk), ins(up),
               [((s, RET_QK), BF16, (t, wq), lambda hg, i: (i, hg))], scratch=scratch)[0]
    dk, dv, *rest = _call("ret_bwd_kv", down_body, (n_grp, n_blk), ins(down),
                          [((s, RET_QK), BF16, (t, wq), lambda hg, i: (down(i), hg)),
                           ((s, RET_V), BF16, (t, wv), lambda hg, i: (down(i), hg))],
                          scratch=scratch, riders=riders)
    return [dq, dk, dv] + rest


def _sb_bwd(qkv, weights, do, tq, tk, riders=None):
    s = qkv.shape[0]
    n_pair = HEADS // 2
    _check_tiles(s, tq, tk, SB_GROUP)

    def body(q_ref, k_ref, v_ref, a_ref, do_ref, dq_ref, dk_ref, dv_ref):
        i = pl.program_id(1)

        @pl.when(i == 0)
        def _():
            dk_ref[...] = jnp.zeros_like(dk_ref)
            dv_ref[...] = jnp.zeros_like(dv_ref)

        lower = _tri(tk, False)
        qs = _stack_heads(q_ref[...])
        dos = _stack_heads(do_ref[...].astype(BF16))

        def make_step(near_diagonal, n_sub=SB_GROUP):
            def step(g, carry):
                c_e, dq = carry
                js = [g * SB_GROUP + sub for sub in range(n_sub)]
                rows = [_key_rows(j, tk) for j in js]
                zs = [_dot(qs, k_ref[rw, :], _NT) for rw in rows]
                das = [_dot(dos, v_ref[rw, :], _NT) for rw in rows]
                avals = [a_ref[j] for j in js]
                for a, rw in zip(avals, rows):
                    dv_ref[rw, :] += _dot(a, dos, _TN)
                es = [a.astype(F32) * da for a, da in zip(avals, das)]
                prefixes = [_dot(e, lower) for e in es]
                betas = [1.0 / (1.0 + jnp.exp2(-z)) for z in zs]
                for sub in range(n_sub):
                    dz = es[sub] - (es[sub] + prefixes[sub] + c_e) * betas[sub]
                    if near_diagonal:
                        dz = jnp.where(_sb_valid(i, js[sub], tq, tk), dz, 0.0)
                    dz = dz.astype(BF16)
                    dk_ref[rows[sub], :] += _dot(dz, qs, _TN)
                    dq = dq + _dot(_side_by_side(dz, tq), _stack_heads(k_ref[rows[sub], :]))
                    c_e = c_e + jnp.sum(es[sub], axis=1, keepdims=True)
                return c_e, dq
            return step

        n_full = _n_full(i, tq, tk, SB_GROUP)
        carry = (jnp.zeros((2 * tq, 1), F32), jnp.zeros((tq, LANES), F32))
        carry = lax.fori_loop(0, n_full, make_step(False), carry)
        _, dq = _diagonal_step(i, tq, tk, lambda n_sub: (lambda n, cr: make_step(True, n_sub)(n_full, cr)), carry)
        dq_ref[...] = dq

    blk = lambda hp, i: (i, hp)
    n_kb = s // tk
    return _call("sb_bwd", body, (n_pair, s // tq),
                 [(qkv, (tq, LANES), blk),
                  (qkv, (s, LANES), lambda hp, i: (0, n_pair + hp)),
                  (qkv, (s, LANES), lambda hp, i: (0, 2 * n_pair + hp)),
                  (weights, (None, None, n_kb, 2 * tq, tk), lambda hp, i: (hp, i, 0, 0, 0)),
                  (do, (tq, LANES), blk)],
                 [((s, SB_W), F32, (tq, LANES), blk),
                  ((s, SB_W), F32, (s, LANES), lambda hp, i: (0, hp)),
                  ((s, SB_W), F32, (s, LANES), lambda hp, i: (0, hp))], riders=riders)


def _assemble_dproj(dq_r, dk_r, dv_r, dg_r, dq_s, dk_s, dv_s, da_r, da_s, cos, sin, idx_col, lg_lanes, tm, riders=None):
    s, d = da_r.shape
    width = 2 * RET_QK + 2 * RET_V + 3 * SB_W + 2 * d

    def body(dq_ref, dk_ref, dv_ref, dg_ref, dqs_ref, dks_ref, dvs_ref, dar_ref, das_ref, cos_ref, sin_ref,
             idx_ref, lg_ref, o_ref):
        lane = lax.broadcasted_iota(jnp.int32, (1, LANES), 1)
        first = jnp.bitwise_and(lane, RET_DQK - 1) < (RET_DQK // 2)
        cos, sin = cos_ref[...], sin_ref[...]
        idx = idx_ref[...]
        for src, base, sign, scale in ((dq_ref, 0, 1.0, 1.0), (dk_ref, RET_QK, -1.0, RET_DQK ** -0.5)):
            for g in range(RET_QK // LANES):
                v = src[:, g * LANES:(g + 1) * LANES].astype(F32) * (_decay_scale(lg_ref, idx, g, sign) * scale)
                sw = jnp.where(first, pltpu.roll(v, LANES - RET_DQK // 2, 1), pltpu.roll(v, RET_DQK // 2, 1))
                o_ref[:, base + g * LANES:base + (g + 1) * LANES] = (v * cos - sw * sin).astype(BF16)
        off = 2 * RET_QK
        o_ref[:, off:off + RET_V] = dv_ref[...].astype(BF16)
        off += RET_V
        o_ref[:, off:off + RET_V] = dg_ref[...]
        off += RET_V
        o_ref[:, off:off + SB_W] = (dqs_ref[...] * (SB_DH ** -0.5)).astype(BF16)
        off += SB_W
        o_ref[:, off:off + SB_W] = (dks_ref[...] * LN2).astype(BF16)
        off += SB_W
        o_ref[:, off:off + SB_W] = dvs_ref[...].astype(BF16)
        off += SB_W
        o_ref[:, off:off + d] = dar_ref[...]
        off += d
        o_ref[:, off:off + d] = das_ref[...]

    row = lambda i: (i, 0)
    ins = [(a, (tm, a.shape[1]), row) for a in (dq_r, dk_r, dv_r, dg_r, dq_s, dk_s, dv_s, da_r, da_s, cos, sin, idx_col)]
    ins.append((lg_lanes, (1, RET_QK), lambda i: (0, 0)))
    return _call("assemble_dproj", body, (s // tm,), ins, [((s, width), BF16, (tm, width), row)], riders=riders)


def _in_bwd(dproj, w_in_t, x, dhres, mod, g1, tm, riders=None):
    s, d = x.shape
    width = dproj.shape[1]

    def body(a_ref, w_ref, x_ref, dh_ref, mod_ref, g_ref, dx_ref, dsh_ref, dsc_ref, dg_ref):
        _zero_at_start([dsh_ref, dsc_ref, dg_ref])
        g1, sc1 = g_ref[...], mod_ref[:, d:2 * d]
        for rows in _pieces(tm):
            dh = _dot(a_ref[rows, :], w_ref[...])
            n1, r1 = _rms(x_ref[rows, :], d)
            dsh_ref[...] += _colsum(dh)
            dsc_ref[...] += _colsum(dh * n1 * g1)
            dg_ref[...] += _colsum(dh * n1 * (1.0 + sc1))
            dx_ref[rows, :] = dh_ref[rows, :] + _rms_bwd(dh * g1 * (1.0 + sc1), n1, r1, d)

    row = lambda i: (i, 0)
    fix = lambda i: (0, 0)
    vec = ((1, d), F32, (1, d), fix)
    return _call("in_bwd", body, (s // tm,),
                 [(dproj, (tm, width), row), (w_in_t, (width, d), fix), (x, (tm, d), row), (dhres, (tm, d), row),
                  (mod, (1, 6 * d), fix), (g1, (1, d), fix)],
                 [((s, d), F32, (tm, d), row), vec, vec, vec], riders=riders)


def _adamw(w, g, m, v):
    m = ADAM_B1 * m + (1.0 - ADAM_B1) * g
    v = ADAM_B2 * v + (1.0 - ADAM_B2) * (g * g)
    m_hat = m / (1.0 - ADAM_B1 ** ADAM_STEP)
    v_hat = v / (1.0 - ADAM_B2 ** ADAM_STEP)
    delta = -ADAM_LR * (m_hat / (jnp.sqrt(v_hat) + ADAM_EPS) + ADAM_WD * w)
    return delta, m, v


def _adam_reduce(name, parts, w, m, v, tr):
    rws, cls = w.shape
    tr = min(tr, rws)
    n_parts = parts.shape[0]

    def body(p_ref, w_ref, m_ref, v_ref, g_out, d_out, m_out, v_out):
        g = p_ref[0].astype(F32)
        for k in range(1, n_parts):
            g = g + p_ref[k].astype(F32)
        delta, mn, vn = _adamw(w_ref[...], g, m_ref[...], v_ref[...])
        g_out[...] = g
        d_out[...] = delta
        m_out[...] = mn
        v_out[...] = vn

    row = lambda i: (i, 0)
    blk = (tr, cls)
    return _call(name, body, (rws // tr,),
                 [(parts, (n_parts, tr, cls), lambda i: (0, i, 0)), (w, blk, row), (m, blk, row), (v, blk, row)],
                 [((rws, cls), F32, blk, row)] * 4)


def _ada_bwd_adam(cs_t, dmod_cols, w, m, v):
    d, nc = w.shape

    def body(c_ref, dm_ref, w_ref, m_ref, v_ref, g_out, d_out, m_out, v_out):
        g = c_ref[0] * dm_ref[0:1, :]
        for r in range(1, N_DEV):
            g = g + c_ref[r] * dm_ref[r:r + 1, :]
        delta, mn, vn = _adamw(w_ref[...], g, m_ref[...], v_ref[...])
        g_out[...] = g
        d_out[...] = delta
        m_out[...] = mn
        v_out[...] = vn

    fix = lambda i: (0, 0)
    blk = (d, nc)
    return _call("ada_bwd_adam", body, (1,),
                 [(cs_t, (N_DEV, d, 1), lambda i: (0, 0, 0)), (dmod_cols, (N_DEV, nc), fix), (w, blk, fix), (m, blk, fix), (v, blk, fix)],
                 [((d, nc), F32, blk, fix)] * 4)


def _small_adam(parts, ws, ms, vs):
    n = len(ws)
    widths = [w.shape[1] for w in ws]
    total = parts.shape[1]
    assert sum(widths) + LANES == total

    def body(p_ref, *refs):
        w_refs, m_refs, v_refs = refs[:n], refs[n:2 * n], refs[2 * n:3 * n]
        outs = refs[3 * n:]
        g = p_ref[0:1, :]
        for k in range(1, N_DEV):
            g = g + p_ref[k:k + 1, :]
        off = 0
        for i, width in enumerate(widths):
            gi = g[:, off:off + width]
            delta, mn, vn = _adamw(w_refs[i][...], gi, m_refs[i][...], v_refs[i][...])
            for o_ref, val in zip(outs[4 * i:4 * i + 4], (gi, delta, mn, vn)):
                o_ref[...] = val
            off += width
        outs[4 * n][...] = g[:, off:off + LANES]

    fix = lambda i: (0, 0)
    vec = lambda a: (a, (1, a.shape[1]), fix)
    out_specs = [((1, width), F32, (1, width), fix) for width in widths for _ in range(4)]
    out_specs.append(((1, LANES), F32, (1, LANES), fix))
    res = _call("small_adam", body, (1,),
                [(parts, (N_DEV, total), fix)] + [vec(a) for a in list(ws) + list(ms) + list(vs)], out_specs)
    return [res[4 * i:4 * i + 4] for i in range(n)], res[4 * n]


def kernel(x, c, positions, ada_w, ada_b, pre_mix_g, post_mix_g, pre_ffn_g, post_ffn_g, w_in, ret_gn_g, w_ret_branch, w_sb_branch, w_out, w_ff1, w_ff2, loss_target, m_ada_w, m_ada_b, m_pre_mix_g, m_post_mix_g, m_pre_ffn_g, m_post_ffn_g, m_w_in, m_ret_gn_g, m_w_ret_branch, m_w_sb_branch, m_w_out, m_w_ff1, m_w_ff2, v_ada_w, v_ada_b, v_pre_mix_g, v_post_mix_g, v_pre_ffn_g, v_post_ffn_g, v_w_in, v_ret_gn_g, v_w_ret_branch, v_w_sb_branch, v_w_out, v_w_ff1, v_w_ff2):
    _, s, d = x.shape
    d_ff = w_ff1.shape[2] * N_DEV
    d_in = w_in.shape[2] * N_DEV
    me = 4 * lax.axis_index("x") + 2 * lax.axis_index("y") + lax.axis_index("c")
    x2, tgt = x[0], loss_target[0]

    core = lax.axis_index("c").astype(jnp.int32).reshape(1)
    bf = lambda w: w[0].astype(BF16)

    w_in_t, m_in_t, v_in_t = (jnp.swapaxes(a[0], 0, 1) for a in (w_in, m_w_in, v_w_in))

    c_all, g_in = _exchange("gather_in", [c, w_in_t.astype(BF16)], ["gather", "gather_chip"])
    c_all = c_all.reshape(N_DEV, d)

    n_ada = ada_w.shape[2]
    cs_all = _silu_rows(c_all)
    ada_b_cols = lax.dynamic_slice(ada_b, (0, me * n_ada), (1, n_ada))
    mod_cols = _ada_fwd(cs_all, ada_w[0], ada_b_cols)
    mod_all = _exchange("gather_mod", [mod_cols], ["gather"])[0]
    mod = lax.dynamic_index_in_dim(mod_all, me, axis=1, keepdims=False).reshape(1, 6 * d)

    tm = min(256, s)
    h, g_in = _pre_norm(x2, pre_mix_g, mod, 2 * tm, riders=([g_in], ["forward"]))
    wt_in = g_in.reshape(d_in, d)
    bf_t = lambda w: jnp.swapaxes(w[0], 0, 1).astype(BF16)
    small_w = [bf(w_ret_branch), bf_t(w_sb_branch), bf(w_out)]
    proj, *small_w = _matmul("in_proj", h, wt_in, "nt", s, 512, BF16, riders=(small_w, ["gather_chip"] * 3))
    pos_col = positions.reshape(s, 1).astype(F32)
    freqs = ROPE_BASE ** (-jnp.arange(0, RET_DQK, 2, dtype=F32) / RET_DQK)
    inv_freq = jnp.tile(freqs, LANES // (RET_DQK // 2)).reshape(1, LANES)
    log_gamma_np = np.log1p(-(2.0 ** (-5.0 - np.arange(HEADS))))
    log_gamma = jnp.asarray(log_gamma_np, F32)
    lg_lanes = jnp.asarray(np.repeat(log_gamma_np, RET_DQK).reshape(1, RET_QK), F32)
    idx_col = (jnp.arange(s, dtype=F32) - (s // 2)).reshape(s, 1)
    qk_rot, v_bf, qkv_sb, cos_t, sin_t = _prep(proj, pos_col, idx_col, inv_freq, lg_lanes, 2 * tm)
    tq, tk = min(256, s), min(128, s)
    tq_sb = min(SB_TQ, s)
    sb, sb_weights, *big_w = _sb_fwd(qkv_sb, tq_sb, tk, riders=([bf(w_ff2), bf_t(w_ff1)], ["gather_chip"] * 2))
    ret, retg, g_ret, g_sb, g_out, g_ff2, g_ff1 = _ret_fwd(qk_rot, v_bf, proj, ret_gn_g, log_gamma, tq,
                                                           riders=(small_w + big_w, ["forward"] * 5))
    wf_ret = g_ret.reshape(RET_V, d)
    wt_sb = g_sb.reshape(d, SB_W)
    wf_out = g_out.reshape(d, d)
    wt_ff1 = g_ff1.reshape(d_ff, d)
    wf_ff2 = g_ff2.reshape(d_ff, d)
    mixed, r_bf, s_bf, y, hres, h2 = _merge_out(retg, sb, wf_ret, wt_sb, wf_out, proj, x2, mod, post_mix_g, pre_ffn_g, tm)
    u, act = _ff1(h2, wt_ff1, s, 512)
    dout, df, loss_sum, d_gt2, d_gp2 = _ff2_loss(act, wf_ff2, hres, tgt, mod, post_ffn_g, tm)

    du, gw_ff2, gw_ff1 = _ffn_bwd(df, wf_ff2, u, act, h2, d_ff // N_DEV)
    gw_ff2 = gw_ff2.reshape(N_DEV, d_ff // N_DEV, d)
    dhres, dy, d_sh2, d_sc2, d_g2, d_gt1, d_gp1, t_ff1, t_ff2 = _ff1_bwd(
        du, wt_ff1, hres, dout, y, mod, pre_ffn_g, post_mix_g, tm, riders=([gw_ff1, gw_ff2], ["pair"] * 2))
    s_ff1 = _pair_sum("pair_sum_ff1", gw_ff1, t_ff1, core, d)
    s_ff2 = _pair_sum("pair_sum_ff2", gw_ff2, t_ff2, core, d)
    gw_out = _matmul("grad_w_out", mixed, dy, "tn", 512, d, BF16).reshape(N_DEV, d // N_DEV, d)
    d_r, d_s, da_r, da_s, p_out = _out_bwd(dy, wf_out, proj, r_bf, s_bf, 2 * tm, min(512, d), riders=([gw_out], ["scatter"]))
    dretg, dsb, gw_ret, gw_sb = _branch_bwd(d_r, d_s, retg, sb, wf_ret, wt_sb)
    gw_ret = gw_ret.reshape(N_DEV, RET_V // N_DEV, d)
    dq_s, dk_s, dv_s, p_ff1, p_ff2 = _sb_bwd(qkv_sb, sb_weights, dsb, tq_sb, tk,
                                             riders=([s_ff1, s_ff2], ["chip_scatter"] * 2))
    dg_r, dret, d_gn = _gn_bwd(dretg, ret, proj, ret_gn_g, 2 * tm)
    dq_r, dk_r, dv_r, p_sb = _ret_bwd(qk_rot, v_bf, dret, log_gamma, tq, riders=([gw_sb], ["scatter"]))
    dproj, p_ret = _assemble_dproj(dq_r, dk_r, dv_r, dg_r, dq_s, dk_s, dv_s, da_r, da_s, cos_t, sin_t, idx_col, lg_lanes,
                                   tm, riders=([gw_ret], ["scatter"]))
    gw_in = _matmul("grad_w_in", dproj, h, "tn", 512, d, BF16).reshape(N_DEV, d_in // N_DEV, d)
    t_in = _exchange("pair_in", [gw_in], ["pair"])[0]
    s_in = _pair_sum("pair_sum_in", gw_in, t_in, core, d_in // N_DEV)
    grad_x, d_sh1, d_sc1, d_g1, p_in = _in_bwd(dproj, wt_in, x2, dhres, mod, pre_mix_g, tm,
                                               riders=([s_in], ["chip_scatter"]))
    loss_lanes = jnp.pad(loss_sum, ((0, 0), (0, LANES - 1)))
    small = jnp.concatenate([d_sh1, d_sc1, d_gt1, d_sh2, d_sc2, d_gt2, d_g1, d_gp1, d_g2, d_gp2, d_gn, loss_lanes], axis=1)
    small_all = _exchange("gather_small", [small], ["gather"])[0].reshape(N_DEV, small.shape[1])
    parts = [p_in, p_ret, p_sb, p_out, p_ff1, p_ff2]

    res = {}
    names = ["w_ret_branch", "w_sb_branch", "w_out", "w_ff1", "w_ff2"]
    ws = [w_ret_branch, w_sb_branch, w_out, w_ff1, w_ff2]
    ms = [m_w_ret_branch, m_w_sb_branch, m_w_out, m_w_ff1, m_w_ff2]
    vs = [v_w_ret_branch, v_w_sb_branch, v_w_out, v_w_ff1, v_w_ff2]
    for nm, p, w, m, v in zip(names, parts[1:], ws, ms, vs):
        res[nm] = [o[None] for o in _adam_reduce("adam_" + nm, p, w[0], m[0], v[0], 256)]
    res["w_in"] = [jnp.swapaxes(o, 0, 1)[None]
                   for o in _adam_reduce("adam_w_in", parts[0], w_in_t, m_in_t, v_in_t, d_in // N_DEV // 2)]
    dmod_cols = lax.dynamic_slice(small_all, (0, me * n_ada), (N_DEV, n_ada))
    res["ada_w"] = [o[None] for o in _ada_bwd_adam(cs_all.reshape(N_DEV, d, 1), dmod_cols, ada_w[0], m_ada_w[0], v_ada_w[0])]
    vec_names = ["ada_b", "pre_mix_g", "post_mix_g", "pre_ffn_g", "post_ffn_g", "ret_gn_g"]
    vec_res, loss_lanes = _small_adam(small_all,
                                      [ada_b, pre_mix_g, post_mix_g, pre_ffn_g, post_ffn_g, ret_gn_g],
                                      [m_ada_b, m_pre_mix_g, m_post_mix_g, m_pre_ffn_g, m_post_ffn_g, m_ret_gn_g],
                                      [v_ada_b, v_pre_mix_g, v_post_mix_g, v_pre_ffn_g, v_post_ffn_g, v_ret_gn_g])
    res.update(zip(vec_names, vec_res))
    loss = (0.5 / d) * loss_lanes[0, 0]
    order = ["ada_w", "ada_b", "pre_mix_g", "post_mix_g", "pre_ffn_g", "post_ffn_g", "w_in", "ret_gn_g",
             "w_ret_branch", "w_sb_branch", "w_out", "w_ff1", "w_ff2"]
    outs = [loss, grad_x[None]]
    for k in range(4):
        outs += [res[nm][k] for nm in order]
    return tuple(outs)
```

```python
import functools

import numpy as np
import jax
import jax.numpy as jnp
from jax import lax
from jax.experimental import pallas as pl
from jax.experimental.pallas import tpu as pltpu

F32 = jnp.float32
BF16 = jnp.bfloat16
N_DEV = 8
AXES = ("x", "y", "c")

EPS = 1e-6
CHUNK = 64
CHUNK_SHIFT = 6
HEADS = 8
RET_DQK = 64
RET_DV = 128
SB_DH = 64
RET_QK = HEADS * RET_DQK
RET_V = HEADS * RET_DV
SB_W = HEADS * SB_DH
ROPE_BASE = 10000.0
LANES = 128

ADAM_LR = 0.001
ADAM_B1 = 0.9
ADAM_B2 = 0.999
ADAM_EPS = 1e-08
ADAM_WD = 0.01
ADAM_STEP = 10

VMEM_LIMIT = 56 * 1024 * 1024

_NN = (((1,), (0,)), ((), ()))
_NT = (((1,), (1,)), ((), ()))
_TN = (((0,), (0,)), ((), ()))


def _dot(a, b, dims=_NN):
    if a.dtype != BF16:
        a = a.astype(BF16)
    if b.dtype != BF16:
        b = b.astype(BF16)
    return lax.dot_general(a, b, dims, preferred_element_type=F32)


def _sigmoid(x):
    return 1.0 / (1.0 + jnp.exp(-x))


def _rms(x, d):
    r = lax.rsqrt(jnp.sum(x * x, axis=1, keepdims=True) * (1.0 / d) + EPS)
    return x * r, r


def _rms_bwd(dn, n, r, d):
    return r * (dn - n * (jnp.sum(dn * n, axis=1, keepdims=True) * (1.0 / d)))


def _colsum(v):
    return jnp.sum(v, axis=0, keepdims=True)


def _accum(ref, val, first):
    @pl.when(first)
    def _():
        ref[...] = val

    @pl.when(jnp.logical_not(first))
    def _():
        ref[...] += val


ROW_SPLIT = 2


def _zero_at_start(refs):
    @pl.when(pl.program_id(0) == 0)
    def _():
        for r in refs:
            r[...] = jnp.zeros_like(r)


def _pieces(tm):
    step = tm // ROW_SPLIT
    return [slice(k * step, (k + 1) * step) for k in range(ROW_SPLIT)]


KIND_SLOTS = {"gather": N_DEV, "scatter": N_DEV, "gather_chip": N_DEV, "forward": N_DEV, "pair": N_DEV // 2,
              "chip_scatter": N_DEV // 2}
SEMS_PER_ARRAY = N_DEV - 1


def _exchange_copies(ins, outs, send_sems, recv_sems, local_sems, kinds):
    x, y, c = (lax.axis_index(a) for a in AXES)
    me, chip, sibling = 4 * x + 2 * y + c, 2 * x + y, (x, y, 1 - c)
    mesh_id = pl.DeviceIdType.MESH
    other_chips = []
    for k in range(1, N_DEV // 2):
        px = 1 - x if k & 2 else x
        py = 1 - y if k & 1 else y
        other_chips.append((px, py))
    copies = []
    for i, kind in enumerate(kinds):
        def remote(src, dst, k, to, i=i):
            return pltpu.make_async_remote_copy(
                src_ref=src, dst_ref=dst, send_sem=send_sems.at[i * SEMS_PER_ARRAY + k],
                recv_sem=recv_sems.at[i * SEMS_PER_ARRAY + k], device_id=to, device_id_type=mesh_id)

        if kind in ("gather", "scatter"):
            pick = (lambda ref, d: ref.at[d]) if kind == "scatter" else (lambda ref, d: ref)
            copies.append(pltpu.make_async_copy(pick(ins[i], me), outs[i].at[me], local_sems.at[i]))
            for k in range(1, N_DEV):
                to = (1 - x if k & 4 else x, 1 - y if k & 2 else y, 1 - c if k & 1 else c)
                copies.append(remote(pick(ins[i], 4 * to[0] + 2 * to[1] + to[2]), outs[i].at[me], k - 1, to))
        elif kind == "gather_chip":
            copies.append(pltpu.make_async_copy(ins[i], outs[i].at[me], local_sems.at[i]))
            copies.append(remote(ins[i], outs[i].at[me], 0, sibling))
            for k, (px, py) in enumerate(other_chips):
                copies.append(remote(ins[i], outs[i].at[me], 1 + k, (px, py, c)))
        elif kind == "forward":
            for k, (px, py) in enumerate(other_chips):
                slot = 4 * px + 2 * py + c
                copies.append(remote(outs[i].at[slot], outs[i].at[slot], k, sibling))
        elif kind == "pair":
            for k in range(N_DEV // 2):
                copies.append(remote(ins[i].at[2 * k + 1 - c], outs[i].at[k], k, sibling))
        elif kind == "chip_scatter":
            copies.append(pltpu.make_async_copy(ins[i].at[chip], outs[i].at[chip], local_sems.at[i]))
            for k, (px, py) in enumerate(other_chips):
                copies.append(remote(ins[i].at[2 * px + py], outs[i].at[chip], k, (px, py, c)))
        else:
            raise ValueError(kind)
    return copies


def _exchange_shapes(arrays, kinds):
    shapes = []
    for a, kind in zip(arrays, kinds):
        tail = a.shape if kind in ("gather", "gather_chip") else a.shape[1:]
        shapes.append(jax.ShapeDtypeStruct((KIND_SLOTS[kind],) + tuple(tail), a.dtype))
    return shapes


def _exchange_sems(n):
    return [pltpu.SemaphoreType.DMA((n * SEMS_PER_ARRAY,)), pltpu.SemaphoreType.DMA((n * SEMS_PER_ARRAY,)),
            pltpu.SemaphoreType.DMA((n,))]


def _call(name, body, grid, ins, outs, scratch=(), riders=None, prefetch=None):
    any_spec = pl.BlockSpec(memory_space=pl.ANY)
    in_specs = [pl.BlockSpec(memory_space=im) if bs is None else pl.BlockSpec(bs, im) for _, bs, im in ins]
    out_specs = [pl.BlockSpec(bs, im) for _, _, bs, im in outs]
    out_shape = [jax.ShapeDtypeStruct(s, d) for s, d, _, _ in outs]
    operands = [a for a, _, _ in ins]
    scratch = list(scratch)
    aliases = {}
    n_pre = 0 if prefetch is None else 1
    kernel = functools.partial(body) if prefetch is None else (lambda _, *refs: body(*refs))
    if riders is not None:
        arrays, kinds = riders
        nr, n_in, n_out, n_scr = len(arrays), len(ins), len(outs), len(scratch)

        def kernel(*refs):
            refs = refs[n_pre:]
            own_in, ride_in = refs[:n_in], refs[n_in:n_in + nr]
            own_out = refs[n_in + nr:n_in + nr + n_out]
            ride_out = refs[n_in + nr + n_out:n_in + 2 * nr + n_out]
            own_scr = refs[n_in + 2 * nr + n_out:n_in + 2 * nr + n_out + n_scr]
            sems = refs[n_in + 2 * nr + n_out + n_scr:]
            ids = [pl.program_id(a) for a in range(len(grid))]
            first = functools.reduce(jnp.logical_and, [i == 0 for i in ids])
            last = functools.reduce(jnp.logical_and, [i == g - 1 for i, g in zip(ids, grid)])

            @pl.when(first)
            def _():
                for cp in _exchange_copies(ride_in, ride_out, *sems, kinds):
                    cp.start()

            body(*own_in, *own_out, *own_scr)

            @pl.when(last)
            def _():
                for cp in _exchange_copies(ride_in, ride_out, *sems, kinds):
                    cp.wait()

        in_specs += [any_spec] * nr
        out_specs += [any_spec] * nr
        out_shape += _exchange_shapes(arrays, kinds)
        operands += list(arrays)
        scratch += _exchange_sems(nr)
        aliases = {n_pre + n_in + r: n_out + r for r, kind in enumerate(kinds) if kind == "forward"}
    params = pltpu.CompilerParams(dimension_semantics=("arbitrary",) * len(grid), vmem_limit_bytes=VMEM_LIMIT)
    if prefetch is None:
        return pl.pallas_call(kernel, name=name, grid=grid, in_specs=in_specs, out_specs=out_specs,
                              out_shape=out_shape, scratch_shapes=scratch, input_output_aliases=aliases,
                              compiler_params=params)(*operands)
    grid_spec = pltpu.PrefetchScalarGridSpec(num_scalar_prefetch=1, grid=grid, in_specs=in_specs,
                                             out_specs=out_specs, scratch_shapes=scratch)
    return pl.pallas_call(kernel, name=name, grid_spec=grid_spec, out_shape=out_shape,
                          input_output_aliases=aliases, compiler_params=params)(prefetch, *operands)


def _exchange(name, arrays, kinds):
    n = len(arrays)

    def body(*refs):
        copies = _exchange_copies(refs[:n], refs[n:2 * n], *refs[2 * n:], kinds)
        for cp in copies:
            cp.start()
        for cp in copies:
            cp.wait()

    any_spec = pl.BlockSpec(memory_space=pl.ANY)
    return pl.pallas_call(
        functools.partial(body),
        name=name,
        in_specs=[any_spec] * n,
        out_specs=[any_spec] * n,
        out_shape=_exchange_shapes(arrays, kinds),
        scratch_shapes=_exchange_sems(n),
        input_output_aliases={i: i for i, kind in enumerate(kinds) if kind == "forward"},
    )(*arrays)


def _pair_sum(name, pairs, my_core):
    n = len(pairs)

    def body(*refs):
        for k in range(n):
            a_ref, b_ref, o_ref = refs[2 * k], refs[2 * k + 1], refs[2 * n + k]
            o_ref[...] = (a_ref[...].astype(F32) + b_ref[...].astype(F32)).astype(o_ref.dtype)

    ins, outs = [], []
    for mine, theirs in pairs:
        _, rws, cls = mine.shape
        ins += [(mine, (None, rws, cls), lambda k, core: (2 * k + core[0], 0, 0)),
                (theirs, (None, rws, cls), lambda k, core: (k, 0, 0))]
        outs.append(((N_DEV // 2, rws, cls), mine.dtype, (None, rws, cls), lambda k, core: (k, 0, 0)))
    return _call(name, body, (N_DEV // 2,), ins, outs, prefetch=my_core)


def _matmul(name, a, b, kind, tm, tn, out_dtype, blocked_out=False, riders=None):
    if kind == "tn":
        kdim, m = a.shape
    else:
        m, kdim = a.shape
    n = b.shape[0] if kind == "nt" else b.shape[1]
    tm, tn = min(tm, m), min(tn, n)
    dims = {"nn": _NN, "nt": _NT, "tn": _TN}[kind]

    def body(a_ref, b_ref, o_ref):
        o_ref[...] = _dot(a_ref[...], b_ref[...], dims).astype(o_ref.dtype)

    a_spec = (a, (kdim, tm), lambda j, i: (0, i)) if kind == "tn" else (a, (tm, kdim), lambda j, i: (i, 0))
    b_spec = (b, (tn, kdim), lambda j, i: (j, 0)) if kind == "nt" else (b, (kdim, tn), lambda j, i: (0, j))
    if blocked_out:
        out = ((n // tn, m, tn), out_dtype, (None, tm, tn), lambda j, i: (j, i, 0))
    else:
        out = ((m, n), out_dtype, (tm, tn), lambda j, i: (i, j))
    res = _call(name, body, (n // tn, m // tm), [a_spec, b_spec], [out], riders=riders)
    return res[0] if riders is None else res


def _ada_fwd(cs_all, ada_w, ada_b_cols):
    def body(c_ref, w_ref, b_ref, o_ref):
        o_ref[...] = lax.dot_general(c_ref[...], w_ref[...], _NN, preferred_element_type=F32,
                                     precision=lax.Precision.HIGHEST) + b_ref[...]

    r, d = cs_all.shape
    nc = ada_w.shape[1]
    return _call("ada_fwd", body, (1,),
                 [(cs_all, (r, d), lambda i: (0, 0)), (ada_w, (d, nc), lambda i: (0, 0)),
                  (ada_b_cols, (1, nc), lambda i: (0, 0))],
                 [((r, nc), F32, (r, nc), lambda i: (0, 0))])[0]


def _silu_rows(c_all):
    def body(c_ref, o_ref):
        v = c_ref[...]
        o_ref[...] = v * _sigmoid(v)

    return _call("silu_c", body, (1,), [(c_all, c_all.shape, lambda i: (0, 0))],
                 [(c_all.shape, F32, c_all.shape, lambda i: (0, 0))])[0]


def _pre_norm(x, g, mod, tm, riders=None):
    s, d = x.shape

    def body(x_ref, g_ref, mod_ref, h_ref):
        n, _ = _rms(x_ref[...], d)
        sh, sc = mod_ref[:, 0:d], mod_ref[:, d:2 * d]
        h_ref[...] = (n * g_ref[...] * (1.0 + sc) + sh).astype(BF16)

    return _call("pre_norm", body, (s // tm,),
                 [(x, (tm, d), lambda i: (i, 0)), (g, (1, d), lambda i: (0, 0)),
                  (mod, (1, 6 * d), lambda i: (0, 0))],
                 [((s, d), BF16, (tm, d), lambda i: (i, 0))], riders=riders)


LOG2E = 1.4426950408889634
LN2 = 0.6931471805599453


def _decay_scale(lg_ref, idx, g, sign):
    return jnp.exp((sign * idx) * lg_ref[:, g * LANES:(g + 1) * LANES])


def _prep(proj, pos_col, idx_col, inv_freq, lg_lanes, tm):
    s = proj.shape[0]
    sb_off = (2 * RET_QK + 2 * RET_V) // (3 * SB_W)
    n_q = RET_QK // LANES

    def body(qk_ref, v_ref, sb_ref, pos_ref, idx_ref, f_ref, lg_ref, qk_out, v_out, sb_out, cos_out, sin_out):
        ang = pos_ref[...] * f_ref[...]
        lane = lax.broadcasted_iota(jnp.int32, (1, LANES), 1)
        first = jnp.bitwise_and(lane, RET_DQK - 1) < (RET_DQK // 2)
        cos = jnp.cos(ang)
        sin = jnp.where(first, -1.0, 1.0) * jnp.sin(ang)
        cos_out[...] = cos
        sin_out[...] = sin
        idx = idx_ref[...]
        for g in range(2 * n_q):
            v = qk_ref[:, g * LANES:(g + 1) * LANES].astype(F32)
            sw = jnp.where(first, pltpu.roll(v, LANES - RET_DQK // 2, 1), pltpu.roll(v, RET_DQK // 2, 1))
            r = v * cos + sw * sin
            if g < n_q:
                r = r * _decay_scale(lg_ref, idx, g, 1.0)
            else:
                r = r * (_decay_scale(lg_ref, idx, g - n_q, -1.0) * (RET_DQK ** -0.5))
            qk_out[:, g * LANES:(g + 1) * LANES] = r.astype(BF16)
        v_out[...] = v_ref[...].astype(BF16)
        sb_out[:, 0:SB_W] = (sb_ref[:, 0:SB_W].astype(F32) * (SB_DH ** -0.5 * LOG2E)).astype(BF16)
        sb_out[:, SB_W:3 * SB_W] = sb_ref[:, SB_W:3 * SB_W].astype(BF16)

    return _call("prep", body, (s // tm,),
                 [(proj, (tm, 2 * RET_QK), lambda i: (i, 0)),
                  (proj, (tm, RET_V), lambda i: (i, 2 * RET_QK // RET_V)),
                  (proj, (tm, 3 * SB_W), lambda i: (i, sb_off)),
                  (pos_col, (tm, 1), lambda i: (i, 0)),
                  (idx_col, (tm, 1), lambda i: (i, 0)),
                  (inv_freq, (1, LANES), lambda i: (0, 0)),
                  (lg_lanes, (1, RET_QK), lambda i: (0, 0))],
                 [((s, 2 * RET_QK), BF16, (tm, 2 * RET_QK), lambda i: (i, 0)),
                  ((s, RET_V), BF16, (tm, RET_V), lambda i: (i, 0)),
                  ((s, 3 * SB_W), BF16, (tm, 3 * SB_W), lambda i: (i, 0)),
                  ((s, LANES), F32, (tm, LANES), lambda i: (i, 0)),
                  ((s, LANES), F32, (tm, LANES), lambda i: (i, 0))])


def _head_mask(hh):
    lane = lax.broadcasted_iota(jnp.int32, (1, LANES), 1)
    return (lane >= RET_DQK) if hh else (lane < RET_DQK)


def _masked(v, m):
    return jnp.where(m, v, jnp.zeros_like(v))


SB_GROUP = 4
SB_TQ = 256


def _stack_heads(v):
    return jnp.concatenate([_masked(v, _head_mask(0)), _masked(v, _head_mask(1))], axis=0)


def _side_by_side(v, t):
    return jnp.concatenate([v[:t], v[t:]], axis=1)


def _tile_pos(i, j, tq, tk):
    row = jnp.bitwise_and(lax.broadcasted_iota(jnp.int32, (2 * tq, tk), 0), tq - 1) + i * tq
    col = lax.broadcasted_iota(jnp.int32, (2 * tq, tk), 1) + j * tk
    return row, col


def _n_groups(i, tq, tk, grp):
    return ((i + 1) * (tq // tk) + grp - 1) // grp


def _n_full(i, tq, tk, grp):
    return (i * (tq // tk)) // grp


def _key_rows(j, tk):
    return pl.ds(pl.multiple_of(j * tk, tk), tk)


def _ret_weight(lg_rows, i, j, tq, tk):
    row, col = _tile_pos(i, j, tq, tk)
    same = jnp.right_shift(col, CHUNK_SHIFT) == jnp.right_shift(row, CHUNK_SHIFT)
    later = jnp.where(same, jnp.exp((2.0 * lg_rows) * (col - row).astype(F32)), 0.0)
    return jnp.where(col <= row, 1.0, later)


def _lg_rows(lg_ref, hp, tq):
    first = lax.broadcasted_iota(jnp.int32, (2 * tq, 1), 0) < tq
    return jnp.where(first, lg_ref[2 * hp], lg_ref[2 * hp + 1])


def _check_tiles(s, tq, tk, grp):
    assert tq % tk == 0 and tq & (tq - 1) == 0 and tk & (tk - 1) == 0
    assert s % tq == 0 and (s // tk) % grp == 0 and s // tk <= LANES


def _pair_mask():
    r = lax.broadcasted_iota(jnp.int32, (LANES, 2 * RET_DV), 0) >= RET_DQK
    c = lax.broadcasted_iota(jnp.int32, (LANES, 2 * RET_DV), 1) >= RET_DV
    return (r == c).astype(F32)


def _ret_block(lg_ref, hp, i, t, qb, kb):
    w = _ret_weight(_lg_rows(lg_ref, hp, t), i, i, t, t)
    return _dot(_stack_heads(qb), kb, _NT), w


RET_PAIRS = 2


def _lanes(ref, p, width):
    return ref[:, p * width:(p + 1) * width]


def _ret_fwd(qk_rot, v_bf, proj, gn_g, log_gamma, t, riders=None):
    s = qk_rot.shape[0]
    n_pair = HEADS // 2
    pw = 2 * RET_DV
    wq, wv = RET_PAIRS * LANES, RET_PAIRS * pw
    gate_off = (2 * RET_QK + RET_V) // wv
    assert s % t == 0 and t % CHUNK == 0 and t & (t - 1) == 0 and n_pair % RET_PAIRS == 0

    def body(lg_ref, q_ref, k_ref, v_ref, g_ref, w_ref, ret_ref, rg_ref, state_ref):
        hg, i = pl.program_id(0), pl.program_id(1)

        @pl.when(i == 0)
        def _():
            state_ref[...] = jnp.zeros_like(state_ref)

        pairs = range(RET_PAIRS)
        qbs = [_lanes(q_ref, p, LANES) for p in pairs]
        kbs = [_lanes(k_ref, p, LANES) for p in pairs]
        vbs = [_lanes(v_ref, p, pw) for p in pairs]
        zws = [_ret_block(lg_ref, hg * RET_PAIRS + p, i, t, qbs[p], kbs[p]) for p in pairs]
        ps = [(z * w).astype(BF16) for z, w in zws]
        outs = [jnp.concatenate([_dot(ps[p][:t], vbs[p][:, 0:RET_DV]), _dot(ps[p][t:], vbs[p][:, RET_DV:pw])], axis=1)
                + _dot(qbs[p], state_ref[p]) for p in pairs]
        for p in pairs:
            state_ref[p] += _pair_mask() * _dot(kbs[p], vbs[p], _TN)
        for p in pairs:
            for hh in range(2):
                cols = slice(p * pw + hh * RET_DV, p * pw + (hh + 1) * RET_DV)
                o = outs[p][:, hh * RET_DV:(hh + 1) * RET_DV]
                ret_ref[:, cols] = o
                mu = jnp.sum(o, axis=1, keepdims=True) * (1.0 / RET_DV)
                xc = o - mu
                var = jnp.sum(xc * xc, axis=1, keepdims=True) * (1.0 / RET_DV)
                nrm = xc * lax.rsqrt(var + EPS) * w_ref[:, cols]
                g = g_ref[:, cols].astype(F32)
                rg_ref[:, cols] = (g * _sigmoid(g) * nrm).astype(BF16)

    blk = lambda hg, i: (i, hg)
    return _call("ret_fwd", body, (n_pair // RET_PAIRS, s // t),
                 [(log_gamma, None, pltpu.SMEM),
                  (qk_rot, (t, wq), blk),
                  (qk_rot, (t, wq), lambda hg, i: (i, n_pair // RET_PAIRS + hg)),
                  (v_bf, (t, wv), blk),
                  (proj, (t, wv), lambda hg, i: (i, gate_off + hg)),
                  (gn_g, (1, wv), lambda hg, i: (0, hg))],
                 [((s, RET_V), F32, (t, wv), blk), ((s, RET_V), BF16, (t, wv), blk)],
                 scratch=[pltpu.VMEM((RET_PAIRS, LANES, pw), F32)], riders=riders)


def _tri(tk, strict_upper):
    r = lax.broadcasted_iota(jnp.int32, (tk, tk), 0)
    cc = lax.broadcasted_iota(jnp.int32, (tk, tk), 1)
    return ((r > cc) if strict_upper else (r < cc)).astype(BF16)


def _diagonal_step(i, tq, tk, make, carry):
    if (tq // tk) % SB_GROUP == 0:
        return make(SB_GROUP)(0, carry)
    assert 2 * (tq // tk) == SB_GROUP
    half = lax.rem(i, 2) == 0
    return lax.cond(half, lambda cr: make(SB_GROUP // 2)(0, cr), lambda cr: make(SB_GROUP)(0, cr), carry)


def _sb_valid(i, j, tq, tk):
    row, col = _tile_pos(i, j, tq, tk)
    return col < row


def _sb_fwd(qkv, tq, tk, riders=None):
    s = qkv.shape[0]
    n_pair = HEADS // 2
    _check_tiles(s, tq, tk, SB_GROUP)

    def body(q_ref, k_ref, v_ref, o_ref, a_ref):
        i = pl.program_id(1)
        upper = _tri(tk, True)
        qs = _stack_heads(q_ref[...])
        n_full, n_groups = _n_full(i, tq, tk, SB_GROUP), _n_groups(i, tq, tk, SB_GROUP)

        def make_step(near_diagonal, last, n_sub=SB_GROUP):
            def step(n, carry):
                c, o = carry
                g = last - 1 - n
                js = [g * SB_GROUP + sub for sub in range(n_sub)]
                zs = [_dot(qs, k_ref[_key_rows(j, tk), :], _NT) for j in js]
                log1ps = [jnp.log2(1.0 + jnp.exp2(-jnp.abs(z))) for z in zs]
                log_1ms = [-jnp.maximum(z, 0.0) - t for z, t in zip(zs, log1ps)]
                log_bs = [jnp.minimum(z, 0.0) - t for z, t in zip(zs, log1ps)]
                if near_diagonal:
                    valids = [_sb_valid(i, j, tq, tk) for j in js]
                    log_1ms = [jnp.where(v, l, 0.0) for v, l in zip(valids, log_1ms)]
                sticks = [_dot(l, upper) for l in log_1ms]
                sums = [jnp.sum(l, axis=1, keepdims=True) for l in log_1ms]
                cs = [None] * n_sub
                for sub in reversed(range(n_sub)):
                    cs[sub] = c
                    c = c + sums[sub]
                for sub, j in enumerate(js):
                    a = jnp.exp2(log_bs[sub] + sticks[sub] + cs[sub])
                    if near_diagonal:
                        a = jnp.where(valids[sub], a, 0.0)
                    a = a.astype(BF16)
                    a_ref[j] = a
                    o = o + _dot(_side_by_side(a, tq), _stack_heads(v_ref[_key_rows(j, tk), :]))
                return c, o
            return step

        carry = (jnp.zeros((2 * tq, 1), F32), jnp.zeros((tq, LANES), F32))
        carry = _diagonal_step(i, tq, tk, lambda n_sub: make_step(True, n_groups, n_sub), carry)
        _, acc = lax.fori_loop(0, n_full, make_step(False, n_full), carry)
        o_ref[...] = acc.astype(BF16)

    n_kb = s // tk
    return _call("sb_fwd", body, (n_pair, s // tq),
                 [(qkv, (tq, LANES), lambda hp, i: (i, hp)),
                  (qkv, (s, LANES), lambda hp, i: (0, n_pair + hp)),
                  (qkv, (s, LANES), lambda hp, i: (0, 2 * n_pair + hp))],
                 [((s, SB_W), BF16, (tq, LANES), lambda hp, i: (i, hp)),
                  ((n_pair, s // tq, n_kb, 2 * tq, tk), BF16, (None, None, n_kb, 2 * tq, tk),
                   lambda hp, i: (hp, i, 0, 0, 0))], riders=riders)


def _merge_out(retg, sb, w_ret, w_sb_t, w_out, proj, x, mod, gp1, g2, tm):
    s, d = x.shape
    gw = min(512, d)
    n_g = d // gw
    ar_off = (2 * RET_QK + 2 * RET_V + 3 * SB_W) // gw

    def body(rg_ref, sb_ref, wr_ref, ws_ref, wo_ref, *refs):
        gate_refs, (x_ref, mod_ref, gp_ref, g2_ref, mix_ref, r_ref, s_ref, y_ref, hres_ref, h2_ref) = refs[:2 * n_g], refs[2 * n_g:]
        for rows in _pieces(tm):
            rr = _dot(rg_ref[rows, :], wr_ref[...])
            ss = _dot(sb_ref[rows, :], ws_ref[...], _NT)
            a_r = jnp.concatenate([g[rows, :] for g in gate_refs[:n_g]], axis=1).astype(F32)
            a_s = jnp.concatenate([g[rows, :] for g in gate_refs[n_g:]], axis=1).astype(F32)
            mixed = (_sigmoid(a_r) * rr + _sigmoid(a_s) * ss).astype(BF16)
            mix_ref[rows, :] = mixed
            r_ref[rows, :] = rr.astype(BF16)
            s_ref[rows, :] = ss.astype(BF16)
            y = _dot(mixed, wo_ref[...])
            y_ref[rows, :] = y
            ny, _ = _rms(y, d)
            hres = x_ref[rows, :] + mod_ref[:, 2 * d:3 * d] * (ny * gp_ref[...])
            hres_ref[rows, :] = hres
            n2, _ = _rms(hres, d)
            h2_ref[rows, :] = (n2 * g2_ref[...] * (1.0 + mod_ref[:, 4 * d:5 * d]) + mod_ref[:, 3 * d:4 * d]).astype(BF16)

    row = lambda i: (i, 0)
    fix = lambda i: (0, 0)
    tile_bf = ((s, d), BF16, (tm, d), row)
    tile_f = ((s, d), F32, (tm, d), row)
    return _call("merge_out", body, (s // tm,),
                 [(retg, (tm, RET_V), row), (sb, (tm, SB_W), row), (w_ret, (RET_V, d), fix), (w_sb_t, (d, SB_W), fix),
                  (w_out, (d, d), fix)]
                 + [(proj, (tm, gw), functools.partial(lambda i, k: (i, ar_off + k), k=k)) for k in range(2 * n_g)]
                 + [(x, (tm, d), row), (mod, (1, 6 * d), fix), (gp1, (1, d), fix), (g2, (1, d), fix)],
                 [tile_bf, tile_bf, tile_bf, tile_f, tile_f, tile_bf])


def _ff1(h2, w_ff1_t, tm, tn):
    s, f = h2.shape[0], w_ff1_t.shape[0]
    tm = min(tm, s)

    def body(a_ref, w_ref, u_ref, act_ref):
        u = _dot(a_ref[...], w_ref[...], _NT)
        r = jnp.maximum(u, 0.0)
        u_ref[...] = u.astype(BF16)
        act_ref[...] = (r * r).astype(BF16)

    d = h2.shape[1]
    return _call("ff1", body, (f // tn, s // tm),
                 [(h2, (tm, d), lambda j, i: (i, 0)), (w_ff1_t, (tn, d), lambda j, i: (j, 0))],
                 [((s, f), BF16, (tm, tn), lambda j, i: (i, j))] * 2)


def _ff2_loss(act, w_ff2, hres, target, mod, gp2, tm):
    s, d = hres.shape
    f = act.shape[1]

    def body(a_ref, w_ref, h_ref, t_ref, mod_ref, gp_ref, dout_ref, df_ref, loss_ref, dgt_ref, dgp_ref):
        _zero_at_start([loss_ref, dgt_ref, dgp_ref])
        gt, gp = mod_ref[:, 5 * d:6 * d], gp_ref[...]
        for rows in _pieces(tm):
            ff = _dot(a_ref[rows, :], w_ref[...])
            nf, rf = _rms(ff, d)
            out = h_ref[rows, :] + gt * (nf * gp)
            err = out - t_ref[rows, :]
            sq = jnp.sum(err * err, axis=1, keepdims=True)
            loss_ref[...] += jnp.sum(sq, axis=0, keepdims=True)
            dout = err * (1.0 / d)
            dout_ref[rows, :] = dout
            dgt_ref[...] += _colsum(dout * (nf * gp))
            dgp_ref[...] += _colsum(dout * gt * nf)
            df_ref[rows, :] = _rms_bwd(dout * gt * gp, nf, rf, d).astype(BF16)

    row = lambda i: (i, 0)
    fix = lambda i: (0, 0)
    return _call("ff2_loss", body, (s // tm,),
                 [(act, (tm, f), row), (w_ff2, (f, d), fix), (hres, (tm, d), row), (target, (tm, d), row),
                  (mod, (1, 6 * d), fix), (gp2, (1, d), fix)],
                 [((s, d), F32, (tm, d), row), ((s, d), BF16, (tm, d), row), ((1, 1), F32, (1, 1), fix),
                  ((1, d), F32, (1, d), fix), ((1, d), F32, (1, d), fix)])


def _ffn_bwd(df, w_ff2, u, act, h2, tn):
    s, d = df.shape
    f = w_ff2.shape[0]

    def body(df_ref, w_ref, u_ref, act_ref, h2_ref, du_ref, gw2_ref, gw1_ref):
        dfb = df_ref[...]
        du = (_dot(dfb, w_ref[...], _NT) * (2.0 * jnp.maximum(u_ref[...].astype(F32), 0.0))).astype(BF16)
        du_ref[...] = du
        gw2_ref[...] = _dot(act_ref[...], dfb, _TN).astype(BF16)
        gw1_ref[...] = _dot(h2_ref[...], du, _TN).astype(BF16)

    fix = lambda j: (0, 0)
    col = lambda j: (0, j)
    return _call("ffn_bwd", body, (f // tn,),
                 [(df, (s, d), fix), (w_ff2, (tn, d), lambda j: (j, 0)), (u, (s, tn), col), (act, (s, tn), col),
                  (h2, (s, d), fix)],
                 [((s, f), BF16, (s, tn), col), ((f, d), BF16, (tn, d), lambda j: (j, 0)),
                  ((f // tn, d, tn), BF16, (None, d, tn), lambda j: (j, 0, 0))])


def _ff1_bwd(du, w_ff1_t, hres, dout, y, mod, g2, gp1, tm, riders=None):
    s, d = hres.shape
    f = du.shape[1]

    def body(a_ref, w_ref, h_ref, do_ref, y_ref, mod_ref, g2_ref, gp_ref,
             dh_ref, dy_ref, dsh_ref, dsc_ref, dg2_ref, dgt_ref, dgp_ref):
        _zero_at_start([dsh_ref, dsc_ref, dg2_ref, dgt_ref, dgp_ref])
        g2, sc2 = g2_ref[...], mod_ref[:, 4 * d:5 * d]
        gt, gp = mod_ref[:, 2 * d:3 * d], gp_ref[...]
        for rows in _pieces(tm):
            dh2 = _dot(a_ref[rows, :], w_ref[...])
            n2, r2 = _rms(h_ref[rows, :], d)
            dsh_ref[...] += _colsum(dh2)
            dsc_ref[...] += _colsum(dh2 * n2 * g2)
            dg2_ref[...] += _colsum(dh2 * n2 * (1.0 + sc2))
            dhres = do_ref[rows, :] + _rms_bwd(dh2 * g2 * (1.0 + sc2), n2, r2, d)
            dh_ref[rows, :] = dhres
            ny, ry = _rms(y_ref[rows, :], d)
            dgt_ref[...] += _colsum(dhres * (ny * gp))
            dgp_ref[...] += _colsum(dhres * gt * ny)
            dy_ref[rows, :] = _rms_bwd(dhres * gt * gp, ny, ry, d).astype(BF16)

    row = lambda i: (i, 0)
    fix = lambda i: (0, 0)
    vec = ((1, d), F32, (1, d), fix)
    return _call("ff1_bwd", body, (s // tm,),
                 [(du, (tm, f), row), (w_ff1_t, (f, d), fix), (hres, (tm, d), row), (dout, (tm, d), row),
                  (y, (tm, d), row), (mod, (1, 6 * d), fix), (g2, (1, d), fix), (gp1, (1, d), fix)],
                 [((s, d), F32, (tm, d), row), ((s, d), BF16, (tm, d), row), vec, vec, vec, vec, vec], riders=riders)


def _out_bwd(dy, w_out, proj, r_bf, s_bf, tm, tn, riders=None):
    s, d = dy.shape
    ar_off = (2 * RET_QK + 2 * RET_V + 3 * SB_W) // tn
    as_off = ar_off + d // tn

    def body(a_ref, w_ref, ar_ref, as_ref, r_ref, s_ref, dr_ref, ds_ref, dar_ref, das_ref):
        dm = _dot(a_ref[...], w_ref[...], _NT)
        sr, ss = _sigmoid(ar_ref[...].astype(F32)), _sigmoid(as_ref[...].astype(F32))
        dr_ref[...] = (dm * sr).astype(BF16)
        ds_ref[...] = (dm * ss).astype(BF16)
        dar_ref[...] = (dm * r_ref[...].astype(F32) * sr * (1.0 - sr)).astype(BF16)
        das_ref[...] = (dm * s_ref[...].astype(F32) * ss * (1.0 - ss)).astype(BF16)

    tile = (tm, tn)
    here = lambda j, i: (i, j)
    return _call("out_bwd", body, (d // tn, s // tm),
                 [(dy, (tm, d), lambda j, i: (i, 0)), (w_out, (tn, d), lambda j, i: (j, 0)),
                  (proj, tile, lambda j, i: (i, ar_off + j)), (proj, tile, lambda j, i: (i, as_off + j)),
                  (r_bf, tile, here), (s_bf, tile, here)],
                 [((s, d), BF16, tile, here)] * 4, riders=riders)


def _branch_bwd(d_r, d_s, retg, sb, w_ret, w_sb_t):
    s, d = d_r.shape
    half_v, half_s, half_d = RET_V // 2, SB_W // 2, d // 2
    per_dev = d // N_DEV
    n_blk = half_d // per_dev

    def body(dr_ref, ds_ref, rg_ref, sb_ref, wr_ref, ws_ref, dretg_ref, dsb_ref, gwr_ref, gws_ref):
        i = pl.program_id(0)
        dr, ds = dr_ref[...], ds_ref[...]
        dretg_ref[...] = _dot(dr, wr_ref[...], _NT).astype(BF16)
        dsb_ref[...] = _dot(ds, ws_ref[...]).astype(BF16)
        gwr_ref[...] = _dot(rg_ref[...], dr, _TN).astype(BF16)
        cols = pl.ds(pl.multiple_of(i * half_d, half_d), half_d)
        gws = _dot(sb_ref[...], ds_ref[:, cols], _TN).astype(BF16)
        for k in range(n_blk):
            gws_ref[k] = gws[:, k * per_dev:(k + 1) * per_dev]

    fix = lambda i: (0, 0)
    return _call("branch_bwd", body, (2,),
                 [(d_r, (s, d), fix), (d_s, (s, d), fix), (retg, (s, half_v), lambda i: (0, i)), (sb, (s, SB_W), fix),
                  (w_ret, (half_v, d), lambda i: (i, 0)), (w_sb_t, (d, half_s), lambda i: (0, i))],
                 [((s, RET_V), BF16, (s, half_v), lambda i: (0, i)), ((s, SB_W), BF16, (s, half_s), lambda i: (0, i)),
                  ((RET_V, d), BF16, (half_v, d), lambda i: (i, 0)),
                  ((N_DEV, SB_W, per_dev), BF16, (n_blk, SB_W, per_dev), lambda i: (i, 0, 0))])


def _gn_bwd(dretg, ret, proj, gn_g, tm, riders=None):
    s = ret.shape[0]
    gate_off = (2 * RET_QK + RET_V) // RET_V

    def body(d_ref, r_ref, g_ref, w_ref, dg_ref, dret_ref, dw_ref):
        first = pl.program_id(0) == 0
        for h in range(HEADS):
            cols = slice(h * RET_DV, (h + 1) * RET_DV)
            o, g, w, dr = r_ref[:, cols], g_ref[:, cols].astype(F32), w_ref[:, cols], d_ref[:, cols].astype(F32)
            mu = jnp.sum(o, axis=1, keepdims=True) * (1.0 / RET_DV)
            xc = o - mu
            rstd = lax.rsqrt(jnp.sum(xc * xc, axis=1, keepdims=True) * (1.0 / RET_DV) + EPS)
            n = xc * rstd
            sg = _sigmoid(g)
            silu = g * sg
            dg_ref[:, cols] = (dr * n * w * (sg * (1.0 + g * (1.0 - sg)))).astype(BF16)
            _accum(dw_ref.at[:, cols], _colsum(dr * silu * n), first)
            dn = dr * silu * w
            m1 = jnp.sum(dn, axis=1, keepdims=True) * (1.0 / RET_DV)
            m2 = jnp.sum(dn * n, axis=1, keepdims=True) * (1.0 / RET_DV)
            dret_ref[:, cols] = (rstd * (dn - m1 - n * m2)).astype(BF16)

    row = lambda i: (i, 0)
    fix = lambda i: (0, 0)
    return _call("gn_bwd", body, (s // tm,),
                 [(dretg, (tm, RET_V), row), (ret, (tm, RET_V), row),
                  (proj, (tm, RET_V), lambda i: (i, gate_off)), (gn_g, (1, RET_V), fix)],
                 [((s, RET_V), BF16, (tm, RET_V), row), ((s, RET_V), BF16, (tm, RET_V), row),
                  ((1, RET_V), F32, (1, RET_V), fix)], riders=riders)


def _ret_bwd(qk_rot, v_bf, dret, log_gamma, t, riders=None):
    s = qk_rot.shape[0]
    n_pair = HEADS // 2
    pw = 2 * RET_DV
    wq, wv = RET_PAIRS * LANES, RET_PAIRS * pw
    n_blk = s // t
    pairs = range(RET_PAIRS)

    def load(q_ref, k_ref, v_ref, do_ref):
        return ([_lanes(q_ref, p, LANES) for p in pairs], [_lanes(k_ref, p, LANES) for p in pairs],
                [_lanes(v_ref, p, pw) for p in pairs], [_lanes(do_ref, p, pw) for p in pairs])

    def d_scores(lg_ref, hp, i, qb, kb, vb, dob):
        z, w = _ret_block(lg_ref, hp, i, t, qb, kb)
        dp = jnp.concatenate([_dot(dob[:, 0:RET_DV], vb[:, 0:RET_DV], _NT),
                              _dot(dob[:, RET_DV:pw], vb[:, RET_DV:pw], _NT)], axis=0)
        return (z * w).astype(BF16), (dp * w).astype(BF16)

    def up_body(lg_ref, q_ref, k_ref, v_ref, do_ref, dq_ref, state_ref):
        hg, i = pl.program_id(0), pl.program_id(1)

        @pl.when(i == 0)
        def _():
            state_ref[...] = jnp.zeros_like(state_ref)

        qbs, kbs, vbs, dobs = load(q_ref, k_ref, v_ref, do_ref)
        dss = [d_scores(lg_ref, hg * RET_PAIRS + p, i, qbs[p], kbs[p], vbs[p], dobs[p])[1] for p in pairs]
        for p in pairs:
            dq_ref[:, p * LANES:(p + 1) * LANES] = (_dot(_side_by_side(dss[p], t), _stack_heads(kbs[p]))
                                                    + _dot(dobs[p], state_ref[p], _NT)).astype(BF16)
        for p in pairs:
            state_ref[p] += _pair_mask() * _dot(kbs[p], vbs[p], _TN)

    def down_body(lg_ref, q_ref, k_ref, v_ref, do_ref, dk_ref, dv_ref, state_ref):
        hg, i = pl.program_id(0), n_blk - 1 - pl.program_id(1)

        @pl.when(pl.program_id(1) == 0)
        def _():
            state_ref[...] = jnp.zeros_like(state_ref)

        qbs, kbs, vbs, dobs = load(q_ref, k_ref, v_ref, do_ref)
        both = [d_scores(lg_ref, hg * RET_PAIRS + p, i, qbs[p], kbs[p], vbs[p], dobs[p]) for p in pairs]
        for p in pairs:
            pp, ds = both[p]
            later = state_ref[p]
            dv_ref[:, p * pw:(p + 1) * pw] = (jnp.concatenate(
                [_dot(pp[:t], dobs[p][:, 0:RET_DV], _TN), _dot(pp[t:], dobs[p][:, RET_DV:pw], _TN)],
                axis=1) + _dot(kbs[p], later)).astype(BF16)
            dk_ref[:, p * LANES:(p + 1) * LANES] = (_dot(ds, _stack_heads(qbs[p]), _TN)
                                                    + _dot(vbs[p], later, _NT)).astype(BF16)
        for p in pairs:
            state_ref[p] += _pair_mask() * _dot(qbs[p], dobs[p], _TN)

    n_grp = n_pair // RET_PAIRS

    def ins(order):
        return [(log_gamma, None, pltpu.SMEM),
                (qk_rot, (t, wq), lambda hg, i: (order(i), hg)),
                (qk_rot, (t, wq), lambda hg, i: (order(i), n_grp + hg)),
                (v_bf, (t, wv), lambda hg, i: (order(i), hg)),
                (dret, (t, wv), lambda hg, i: (order(i), hg))]

    up = lambda i: i
    down = lambda i: n_blk - 1 - i
    scratch = [pltpu.VMEM((RET_PAIRS, LANES, pw), F32)]
    dq = _call("ret_bwd_q", up_body, (n_grp, n_blk), ins(up),
               [((s, RET_QK), BF16, (t, wq), lambda hg, i: (i, hg))], scratch=scratch)[0]
    dk, dv, *rest = _call("ret_bwd_kv", down_body, (n_grp, n_blk), ins(down),
                          [((s, RET_QK), BF16, (t, wq), lambda hg, i: (down(i), hg)),
                           ((s, RET_V), BF16, (t, wv), lambda hg, i: (down(i), hg))],
                          scratch=scratch, riders=riders)
    return [dq, dk, dv] + rest


def _sb_bwd(qkv, weights, do, tq, tk, riders=None):
    s = qkv.shape[0]
    n_pair = HEADS // 2
    _check_tiles(s, tq, tk, SB_GROUP)

    def body(q_ref, k_ref, v_ref, a_ref, do_ref, dq_ref, dk_ref, dv_ref):
        i = pl.program_id(1)

        @pl.when(i == 0)
        def _():
            dk_ref[...] = jnp.zeros_like(dk_ref)
            dv_ref[...] = jnp.zeros_like(dv_ref)

        lower = _tri(tk, False)
        qs = _stack_heads(q_ref[...])
        dos = _stack_heads(do_ref[...].astype(BF16))

        def make_step(near_diagonal, n_sub=SB_GROUP):
            def step(g, carry):
                c_e, dq = carry
                js = [g * SB_GROUP + sub for sub in range(n_sub)]
                rows = [_key_rows(j, tk) for j in js]
                zs = [_dot(qs, k_ref[rw, :], _NT) for rw in rows]
                das = [_dot(dos, v_ref[rw, :], _NT) for rw in rows]
                avals = [a_ref[j] for j in js]
                for a, rw in zip(avals, rows):
                    dv_ref[rw, :] += _dot(a, dos, _TN)
                es = [a.astype(F32) * da for a, da in zip(avals, das)]
                prefixes = [_dot(e, lower) for e in es]
                betas = [1.0 / (1.0 + jnp.exp2(-z)) for z in zs]
                for sub in range(n_sub):
                    dz = es[sub] - (es[sub] + prefixes[sub] + c_e) * betas[sub]
                    if near_diagonal:
                        dz = jnp.where(_sb_valid(i, js[sub], tq, tk), dz, 0.0)
                    dz = dz.astype(BF16)
                    dk_ref[rows[sub], :] += _dot(dz, qs, _TN)
                    dq = dq + _dot(_side_by_side(dz, tq), _stack_heads(k_ref[rows[sub], :]))
                    c_e = c_e + jnp.sum(es[sub], axis=1, keepdims=True)
                return c_e, dq
            return step

        n_full = _n_full(i, tq, tk, SB_GROUP)
        carry = (jnp.zeros((2 * tq, 1), F32), jnp.zeros((tq, LANES), F32))
        carry = lax.fori_loop(0, n_full, make_step(False), carry)
        _, dq = _diagonal_step(i, tq, tk, lambda n_sub: (lambda n, cr: make_step(True, n_sub)(n_full, cr)), carry)
        dq_ref[...] = dq

    blk = lambda hp, i: (i, hp)
    n_kb = s // tk
    return _call("sb_bwd", body, (n_pair, s // tq),
                 [(qkv, (tq, LANES), blk),
                  (qkv, (s, LANES), lambda hp, i: (0, n_pair + hp)),
                  (qkv, (s, LANES), lambda hp, i: (0, 2 * n_pair + hp)),
                  (weights, (None, None, n_kb, 2 * tq, tk), lambda hp, i: (hp, i, 0, 0, 0)),
                  (do, (tq, LANES), blk)],
                 [((s, SB_W), F32, (tq, LANES), blk),
                  ((s, SB_W), F32, (s, LANES), lambda hp, i: (0, hp)),
                  ((s, SB_W), F32, (s, LANES), lambda hp, i: (0, hp))], riders=riders)


def _assemble_dproj(dq_r, dk_r, dv_r, dg_r, dq_s, dk_s, dv_s, da_r, da_s, cos, sin, idx_col, lg_lanes, tm, riders=None):
    s, d = da_r.shape
    width = 2 * RET_QK + 2 * RET_V + 3 * SB_W + 2 * d

    def body(dq_ref, dk_ref, dv_ref, dg_ref, dqs_ref, dks_ref, dvs_ref, dar_ref, das_ref, cos_ref, sin_ref,
             idx_ref, lg_ref, o_ref):
        lane = lax.broadcasted_iota(jnp.int32, (1, LANES), 1)
        first = jnp.bitwise_and(lane, RET_DQK - 1) < (RET_DQK // 2)
        cos, sin = cos_ref[...], sin_ref[...]
        idx = idx_ref[...]
        for src, base, sign, scale in ((dq_ref, 0, 1.0, 1.0), (dk_ref, RET_QK, -1.0, RET_DQK ** -0.5)):
            for g in range(RET_QK // LANES):
                v = src[:, g * LANES:(g + 1) * LANES].astype(F32) * (_decay_scale(lg_ref, idx, g, sign) * scale)
                sw = jnp.where(first, pltpu.roll(v, LANES - RET_DQK // 2, 1), pltpu.roll(v, RET_DQK // 2, 1))
                o_ref[:, base + g * LANES:base + (g + 1) * LANES] = (v * cos - sw * sin).astype(BF16)
        off = 2 * RET_QK
        o_ref[:, off:off + RET_V] = dv_ref[...].astype(BF16)
        off += RET_V
        o_ref[:, off:off + RET_V] = dg_ref[...]
        off += RET_V
        o_ref[:, off:off + SB_W] = (dqs_ref[...] * (SB_DH ** -0.5)).astype(BF16)
        off += SB_W
        o_ref[:, off:off + SB_W] = (dks_ref[...] * LN2).astype(BF16)
        off += SB_W
        o_ref[:, off:off + SB_W] = dvs_ref[...].astype(BF16)
        off += SB_W
        o_ref[:, off:off + d] = dar_ref[...]
        off += d
        o_ref[:, off:off + d] = das_ref[...]

    row = lambda i: (i, 0)
    ins = [(a, (tm, a.shape[1]), row) for a in (dq_r, dk_r, dv_r, dg_r, dq_s, dk_s, dv_s, da_r, da_s, cos, sin, idx_col)]
    ins.append((lg_lanes, (1, RET_QK), lambda i: (0, 0)))
    return _call("assemble_dproj", body, (s // tm,), ins, [((s, width), BF16, (tm, width), row)], riders=riders)


def _in_bwd(dproj, w_in_t, x, dhres, mod, g1, tm, riders=None):
    s, d = x.shape
    width = dproj.shape[1]

    def body(a_ref, w_ref, x_ref, dh_ref, mod_ref, g_ref, dx_ref, dsh_ref, dsc_ref, dg_ref):
        _zero_at_start([dsh_ref, dsc_ref, dg_ref])
        g1, sc1 = g_ref[...], mod_ref[:, d:2 * d]
        for rows in _pieces(tm):
            dh = _dot(a_ref[rows, :], w_ref[...])
            n1, r1 = _rms(x_ref[rows, :], d)
            dsh_ref[...] += _colsum(dh)
            dsc_ref[...] += _colsum(dh * n1 * g1)
            dg_ref[...] += _colsum(dh * n1 * (1.0 + sc1))
            dx_ref[rows, :] = dh_ref[rows, :] + _rms_bwd(dh * g1 * (1.0 + sc1), n1, r1, d)

    row = lambda i: (i, 0)
    fix = lambda i: (0, 0)
    vec = ((1, d), F32, (1, d), fix)
    return _call("in_bwd", body, (s // tm,),
                 [(dproj, (tm, width), row), (w_in_t, (width, d), fix), (x, (tm, d), row), (dhres, (tm, d), row),
                  (mod, (1, 6 * d), fix), (g1, (1, d), fix)],
                 [((s, d), F32, (tm, d), row), vec, vec, vec], riders=riders)


def _adamw(w, g, m, v):
    m = ADAM_B1 * m + (1.0 - ADAM_B1) * g
    v = ADAM_B2 * v + (1.0 - ADAM_B2) * (g * g)
    m_hat = m / (1.0 - ADAM_B1 ** ADAM_STEP)
    v_hat = v / (1.0 - ADAM_B2 ** ADAM_STEP)
    delta = -ADAM_LR * (m_hat / (jnp.sqrt(v_hat) + ADAM_EPS) + ADAM_WD * w)
    return delta, m, v


def _adam_reduce(name, sets, steps):
    n = len(sets)

    def body(*refs):
        for k in range(n):
            p_ref, w_ref, m_ref, v_ref = refs[4 * k:4 * k + 4]
            outs = refs[4 * n + 4 * k:4 * n + 4 * k + 4]
            g = p_ref[0].astype(F32)
            for j in range(1, p_ref.shape[0]):
                g = g + p_ref[j].astype(F32)
            for o_ref, val in zip(outs, (g,) + _adamw(w_ref[...], g, m_ref[...], v_ref[...])):
                o_ref[...] = val

    ins, outs = [], []
    row = lambda i: (i, 0)
    for parts, w, m, v in sets:
        rws, cls = w.shape
        tr = rws // steps
        assert tr * steps == rws and tr % 16 == 0
        ins += [(parts, (parts.shape[0], tr, cls), lambda i: (0, i, 0)), (w, (tr, cls), row), (m, (tr, cls), row),
                (v, (tr, cls), row)]
        outs += [((rws, cls), F32, (tr, cls), row)] * 4
    res = _call(name, body, (steps,), ins, outs)
    return [res[4 * k:4 * k + 4] for k in range(n)]


def _ada_bwd_adam(cs_t, dmod_cols, w, m, v):
    d, nc = w.shape

    def body(c_ref, dm_ref, w_ref, m_ref, v_ref, g_out, d_out, m_out, v_out):
        g = c_ref[0] * dm_ref[0:1, :]
        for r in range(1, N_DEV):
            g = g + c_ref[r] * dm_ref[r:r + 1, :]
        delta, mn, vn = _adamw(w_ref[...], g, m_ref[...], v_ref[...])
        g_out[...] = g
        d_out[...] = delta
        m_out[...] = mn
        v_out[...] = vn

    fix = lambda i: (0, 0)
    blk = (d, nc)
    return _call("ada_bwd_adam", body, (1,),
                 [(cs_t, (N_DEV, d, 1), lambda i: (0, 0, 0)), (dmod_cols, (N_DEV, nc), fix), (w, blk, fix), (m, blk, fix), (v, blk, fix)],
                 [((d, nc), F32, blk, fix)] * 4)


def _small_adam(parts, ws, ms, vs):
    n = len(ws)
    widths = [w.shape[1] for w in ws]
    total = parts.shape[1]
    assert sum(widths) + LANES == total

    def body(p_ref, *refs):
        w_refs, m_refs, v_refs = refs[:n], refs[n:2 * n], refs[2 * n:3 * n]
        outs = refs[3 * n:]
        g = p_ref[0:1, :]
        for k in range(1, N_DEV):
            g = g + p_ref[k:k + 1, :]
        off = 0
        for i, width in enumerate(widths):
            gi = g[:, off:off + width]
            delta, mn, vn = _adamw(w_refs[i][...], gi, m_refs[i][...], v_refs[i][...])
            for o_ref, val in zip(outs[4 * i:4 * i + 4], (gi, delta, mn, vn)):
                o_ref[...] = val
            off += width
        outs[4 * n][...] = g[:, off:off + LANES]

    fix = lambda i: (0, 0)
    vec = lambda a: (a, (1, a.shape[1]), fix)
    out_specs = [((1, width), F32, (1, width), fix) for width in widths for _ in range(4)]
    out_specs.append(((1, LANES), F32, (1, LANES), fix))
    res = _call("small_adam", body, (1,),
                [(parts, (N_DEV, total), fix)] + [vec(a) for a in list(ws) + list(ms) + list(vs)], out_specs)
    return [res[4 * i:4 * i + 4] for i in range(n)], res[4 * n]


def kernel(x, c, positions, ada_w, ada_b, pre_mix_g, post_mix_g, pre_ffn_g, post_ffn_g, w_in, ret_gn_g, w_ret_branch, w_sb_branch, w_out, w_ff1, w_ff2, loss_target, m_ada_w, m_ada_b, m_pre_mix_g, m_post_mix_g, m_pre_ffn_g, m_post_ffn_g, m_w_in, m_ret_gn_g, m_w_ret_branch, m_w_sb_branch, m_w_out, m_w_ff1, m_w_ff2, v_ada_w, v_ada_b, v_pre_mix_g, v_post_mix_g, v_pre_ffn_g, v_post_ffn_g, v_w_in, v_ret_gn_g, v_w_ret_branch, v_w_sb_branch, v_w_out, v_w_ff1, v_w_ff2):
    _, s, d = x.shape
    d_ff = w_ff1.shape[2] * N_DEV
    d_in = w_in.shape[2] * N_DEV
    me = 4 * lax.axis_index("x") + 2 * lax.axis_index("y") + lax.axis_index("c")
    x2, tgt = x[0], loss_target[0]

    core = lax.axis_index("c").astype(jnp.int32).reshape(1)
    bf = lambda w: w[0].astype(BF16)

    w_in_t, m_in_t, v_in_t = (jnp.swapaxes(a[0], 0, 1) for a in (w_in, m_w_in, v_w_in))

    c_all, g_in = _exchange("gather_in", [c, w_in_t.astype(BF16)], ["gather", "gather_chip"])
    c_all = c_all.reshape(N_DEV, d)

    n_ada = ada_w.shape[2]
    cs_all = _silu_rows(c_all)
    ada_b_cols = lax.dynamic_slice(ada_b, (0, me * n_ada), (1, n_ada))
    mod_cols = _ada_fwd(cs_all, ada_w[0], ada_b_cols)
    mod_all = _exchange("gather_mod", [mod_cols], ["gather"])[0]
    mod = lax.dynamic_index_in_dim(mod_all, me, axis=1, keepdims=False).reshape(1, 6 * d)

    tm = min(256, s)
    h, g_in = _pre_norm(x2, pre_mix_g, mod, 2 * tm, riders=([g_in], ["forward"]))
    wt_in = g_in.reshape(d_in, d)
    bf_t = lambda w: jnp.swapaxes(w[0], 0, 1).astype(BF16)
    small_w = [bf(w_ret_branch), bf_t(w_sb_branch), bf(w_out)]
    proj, *small_w = _matmul("in_proj", h, wt_in, "nt", s, 512, BF16, riders=(small_w, ["gather_chip"] * 3))
    pos_col = positions.reshape(s, 1).astype(F32)
    freqs = ROPE_BASE ** (-jnp.arange(0, RET_DQK, 2, dtype=F32) / RET_DQK)
    inv_freq = jnp.tile(freqs, LANES // (RET_DQK // 2)).reshape(1, LANES)
    log_gamma_np = np.log1p(-(2.0 ** (-5.0 - np.arange(HEADS))))
    log_gamma = jnp.asarray(log_gamma_np, F32)
    lg_lanes = jnp.asarray(np.repeat(log_gamma_np, RET_DQK).reshape(1, RET_QK), F32)
    idx_col = (jnp.arange(s, dtype=F32) - (s // 2)).reshape(s, 1)
    qk_rot, v_bf, qkv_sb, cos_t, sin_t = _prep(proj, pos_col, idx_col, inv_freq, lg_lanes, 2 * tm)
    tq, tk = min(256, s), min(128, s)
    tq_sb = min(SB_TQ, s)
    sb, sb_weights, *big_w = _sb_fwd(qkv_sb, tq_sb, tk, riders=([bf(w_ff2), bf_t(w_ff1)], ["gather_chip"] * 2))
    ret, retg, g_ret, g_sb, g_out, g_ff2, g_ff1 = _ret_fwd(qk_rot, v_bf, proj, ret_gn_g, log_gamma, tq,
                                                           riders=(small_w + big_w, ["forward"] * 5))
    wf_ret = g_ret.reshape(RET_V, d)
    wt_sb = g_sb.reshape(d, SB_W)
    wf_out = g_out.reshape(d, d)
    wt_ff1 = g_ff1.reshape(d_ff, d)
    wf_ff2 = g_ff2.reshape(d_ff, d)
    mixed, r_bf, s_bf, y, hres, h2 = _merge_out(retg, sb, wf_ret, wt_sb, wf_out, proj, x2, mod, post_mix_g, pre_ffn_g, tm)
    u, act = _ff1(h2, wt_ff1, s, 512)
    dout, df, loss_sum, d_gt2, d_gp2 = _ff2_loss(act, wf_ff2, hres, tgt, mod, post_ffn_g, tm)

    du, gw_ff2, gw_ff1 = _ffn_bwd(df, wf_ff2, u, act, h2, d_ff // N_DEV)
    gw_ff2 = gw_ff2.reshape(N_DEV, d_ff // N_DEV, d)
    dhres, dy, d_sh2, d_sc2, d_g2, d_gt1, d_gp1, t_ff1, t_ff2 = _ff1_bwd(
        du, wt_ff1, hres, dout, y, mod, pre_ffn_g, post_mix_g, tm, riders=([gw_ff1, gw_ff2], ["pair"] * 2))
    s_ff1, s_ff2 = _pair_sum("pair_sum_ff", [(gw_ff1, t_ff1), (gw_ff2, t_ff2)], core)
    gw_out = _matmul("grad_w_out", mixed, dy, "tn", 512, d, BF16).reshape(N_DEV, d // N_DEV, d)
    d_r, d_s, da_r, da_s, p_out = _out_bwd(dy, wf_out, proj, r_bf, s_bf, 2 * tm, min(512, d), riders=([gw_out], ["scatter"]))
    dretg, dsb, gw_ret, gw_sb = _branch_bwd(d_r, d_s, retg, sb, wf_ret, wt_sb)
    gw_ret = gw_ret.reshape(N_DEV, RET_V // N_DEV, d)
    dq_s, dk_s, dv_s, p_ff1, p_ff2 = _sb_bwd(qkv_sb, sb_weights, dsb, tq_sb, tk,
                                             riders=([s_ff1, s_ff2], ["chip_scatter"] * 2))
    dg_r, dret, d_gn = _gn_bwd(dretg, ret, proj, ret_gn_g, 2 * tm)
    dq_r, dk_r, dv_r, p_sb = _ret_bwd(qk_rot, v_bf, dret, log_gamma, tq, riders=([gw_sb], ["scatter"]))
    dproj, p_ret = _assemble_dproj(dq_r, dk_r, dv_r, dg_r, dq_s, dk_s, dv_s, da_r, da_s, cos_t, sin_t, idx_col, lg_lanes,
                                   tm, riders=([gw_ret], ["scatter"]))
    gw_in = _matmul("grad_w_in", dproj, h, "tn", 512, d, BF16).reshape(N_DEV, d_in // N_DEV, d)
    t_in = _exchange("pair_in", [gw_in], ["pair"])[0]
    s_in = _pair_sum("pair_sum_in", [(gw_in, t_in)], core)[0]
    grad_x, d_sh1, d_sc1, d_g1, p_in = _in_bwd(dproj, wt_in, x2, dhres, mod, pre_mix_g, tm,
                                               riders=([s_in], ["chip_scatter"]))
    loss_lanes = jnp.pad(loss_sum, ((0, 0), (0, LANES - 1)))
    small = jnp.concatenate([d_sh1, d_sc1, d_gt1, d_sh2, d_sc2, d_gt2, d_g1, d_gp1, d_g2, d_gp2, d_gn, loss_lanes], axis=1)
    small_all = _exchange("gather_small", [small], ["gather"])[0].reshape(N_DEV, small.shape[1])
    parts = [p_in, p_ret, p_sb, p_out, p_ff1, p_ff2]

    res = {}
    names = ["w_ret_branch", "w_sb_branch", "w_out", "w_ff1", "w_ff2"]
    ws = [w_ret_branch, w_sb_branch, w_out, w_ff1, w_ff2]
    ms = [m_w_ret_branch, m_w_sb_branch, m_w_out, m_w_ff1, m_w_ff2]
    vs = [v_w_ret_branch, v_w_sb_branch, v_w_out, v_w_ff1, v_w_ff2]
    sets = [(p, w[0], m[0], v[0]) for p, w, m, v in zip(parts[1:], ws, ms, vs)]
    for nm, outs4 in zip(names, _adam_reduce("adam_rest", sets, 2)):
        res[nm] = [o[None] for o in outs4]
    res["w_in"] = [jnp.swapaxes(o, 0, 1)[None]
                   for o in _adam_reduce("adam_w_in", [(parts[0], w_in_t, m_in_t, v_in_t)], 2)[0]]
    dmod_cols = lax.dynamic_slice(small_all, (0, me * n_ada), (N_DEV, n_ada))
    res["ada_w"] = [o[None] for o in _ada_bwd_adam(cs_all.reshape(N_DEV, d, 1), dmod_cols, ada_w[0], m_ada_w[0], v_ada_w[0])]
    vec_names = ["ada_b", "pre_mix_g", "post_mix_g", "pre_ffn_g", "post_ffn_g", "ret_gn_g"]
    vec_res, loss_lanes = _small_adam(small_all,
                                      [ada_b, pre_mix_g, post_mix_g, pre_ffn_g, post_ffn_g, ret_gn_g],
                                      [m_ada_b, m_pre_mix_g, m_post_mix_g, m_pre_ffn_g, m_post_ffn_g, m_ret_gn_g],
                                      [v_ada_b, v_pre_mix_g, v_post_mix_g, v_pre_ffn_g, v_post_ffn_g, v_ret_gn_g])
    res.update(zip(vec_names, vec_res))
    loss = (0.5 / d) * loss_lanes[0, 0]
    order = ["ada_w", "ada_b", "pre_mix_g", "post_mix_g", "pre_ffn_g", "post_ffn_g", "w_in", "ret_gn_g",
             "w_ret_branch", "w_sb_branch", "w_out", "w_ff1", "w_ff2"]
    outs = [loss, grad_x[None]]
    for k in range(4):
        outs += [res[nm][k] for nm in order]
    return tuple(outs)
```

```python
import functools

import numpy as np
import jax
import jax.numpy as jnp
from jax import lax
from jax.experimental import pallas as pl
from jax.experimental.pallas import tpu as pltpu

F32 = jnp.float32
BF16 = jnp.bfloat16
N_DEV = 8
AXES = ("x", "y", "c")

EPS = 1e-6
CHUNK = 64
CHUNK_SHIFT = 6
HEADS = 8
RET_DQK = 64
RET_DV = 128
SB_DH = 64
RET_QK = HEADS * RET_DQK
RET_V = HEADS * RET_DV
SB_W = HEADS * SB_DH
ROPE_BASE = 10000.0
LANES = 128

ADAM_LR = 0.001
ADAM_B1 = 0.9
ADAM_B2 = 0.999
ADAM_EPS = 1e-08
ADAM_WD = 0.01
ADAM_STEP = 10

VMEM_LIMIT = 56 * 1024 * 1024

_NN = (((1,), (0,)), ((), ()))
_NT = (((1,), (1,)), ((), ()))
_TN = (((0,), (0,)), ((), ()))


def _dot(a, b, dims=_NN):
    if a.dtype != BF16:
        a = a.astype(BF16)
    if b.dtype != BF16:
        b = b.astype(BF16)
    return lax.dot_general(a, b, dims, preferred_element_type=F32)


def _sigmoid(x):
    return 1.0 / (1.0 + jnp.exp(-x))


def _rms(x, d):
    r = lax.rsqrt(jnp.sum(x * x, axis=1, keepdims=True) * (1.0 / d) + EPS)
    return x * r, r


def _rms_bwd(dn, n, r, d):
    return r * (dn - n * (jnp.sum(dn * n, axis=1, keepdims=True) * (1.0 / d)))


def _colsum(v):
    return jnp.sum(v, axis=0, keepdims=True)


ROW_SPLIT = 2


def _zero_at_start(refs):
    @pl.when(pl.program_id(0) == 0)
    def _():
        for r in refs:
            r[...] = jnp.zeros_like(r)


def _pieces(tm):
    step = tm // ROW_SPLIT
    return [slice(k * step, (k + 1) * step) for k in range(ROW_SPLIT)]


KIND_SLOTS = {"gather": N_DEV, "scatter": N_DEV, "gather_chip": N_DEV, "forward": N_DEV, "pair": N_DEV // 2,
              "chip_scatter": N_DEV // 2}
SEMS_PER_ARRAY = N_DEV - 1


def _exchange_copies(ins, outs, send_sems, recv_sems, local_sems, kinds):
    x, y, c = (lax.axis_index(a) for a in AXES)
    me, chip, sibling = 4 * x + 2 * y + c, 2 * x + y, (x, y, 1 - c)
    mesh_id = pl.DeviceIdType.MESH
    other_chips = []
    for k in range(1, N_DEV // 2):
        px = 1 - x if k & 2 else x
        py = 1 - y if k & 1 else y
        other_chips.append((px, py))
    copies = []
    for i, kind in enumerate(kinds):
        def remote(src, dst, k, to, i=i):
            return pltpu.make_async_remote_copy(
                src_ref=src, dst_ref=dst, send_sem=send_sems.at[i * SEMS_PER_ARRAY + k],
                recv_sem=recv_sems.at[i * SEMS_PER_ARRAY + k], device_id=to, device_id_type=mesh_id)

        if kind in ("gather", "scatter"):
            pick = (lambda ref, d: ref.at[d]) if kind == "scatter" else (lambda ref, d: ref)
            copies.append(pltpu.make_async_copy(pick(ins[i], me), outs[i].at[me], local_sems.at[i]))
            for k in range(1, N_DEV):
                to = (1 - x if k & 4 else x, 1 - y if k & 2 else y, 1 - c if k & 1 else c)
                copies.append(remote(pick(ins[i], 4 * to[0] + 2 * to[1] + to[2]), outs[i].at[me], k - 1, to))
        elif kind == "gather_chip":
            copies.append(pltpu.make_async_copy(ins[i], outs[i].at[me], local_sems.at[i]))
            copies.append(remote(ins[i], outs[i].at[me], 0, sibling))
            for k, (px, py) in enumerate(other_chips):
                copies.append(remote(ins[i], outs[i].at[me], 1 + k, (px, py, c)))
        elif kind == "forward":
            for k, (px, py) in enumerate(other_chips):
                slot = 4 * px + 2 * py + c
                copies.append(remote(outs[i].at[slot], outs[i].at[slot], k, sibling))
        elif kind == "pair":
            for k in range(N_DEV // 2):
                copies.append(remote(ins[i].at[2 * k + 1 - c], outs[i].at[k], k, sibling))
        elif kind == "chip_scatter":
            copies.append(pltpu.make_async_copy(ins[i].at[chip], outs[i].at[chip], local_sems.at[i]))
            for k, (px, py) in enumerate(other_chips):
                copies.append(remote(ins[i].at[2 * px + py], outs[i].at[chip], k, (px, py, c)))
        else:
            raise ValueError(kind)
    return copies


def _exchange_shapes(arrays, kinds):
    shapes = []
    for a, kind in zip(arrays, kinds):
        tail = a.shape if kind in ("gather", "gather_chip") else a.shape[1:]
        shapes.append(jax.ShapeDtypeStruct((KIND_SLOTS[kind],) + tuple(tail), a.dtype))
    return shapes


def _exchange_sems(n):
    return [pltpu.SemaphoreType.DMA((n * SEMS_PER_ARRAY,)), pltpu.SemaphoreType.DMA((n * SEMS_PER_ARRAY,)),
            pltpu.SemaphoreType.DMA((n,))]


def _call(name, body, grid, ins, outs, scratch=(), riders=None, prefetch=None):
    any_spec = pl.BlockSpec(memory_space=pl.ANY)
    in_specs = [pl.BlockSpec(memory_space=im) if bs is None else pl.BlockSpec(bs, im) for _, bs, im in ins]
    out_specs = [pl.BlockSpec(bs, im) for _, _, bs, im in outs]
    out_shape = [jax.ShapeDtypeStruct(s, d) for s, d, _, _ in outs]
    operands = [a for a, _, _ in ins]
    scratch = list(scratch)
    aliases = {}
    n_pre = 0 if prefetch is None else 1
    kernel = functools.partial(body) if prefetch is None else (lambda _, *refs: body(*refs))
    if riders is not None:
        arrays, kinds = riders
        nr, n_in, n_out, n_scr = len(arrays), len(ins), len(outs), len(scratch)

        def kernel(*refs):
            refs = refs[n_pre:]
            own_in, ride_in = refs[:n_in], refs[n_in:n_in + nr]
            own_out = refs[n_in + nr:n_in + nr + n_out]
            ride_out = refs[n_in + nr + n_out:n_in + 2 * nr + n_out]
            own_scr = refs[n_in + 2 * nr + n_out:n_in + 2 * nr + n_out + n_scr]
            sems = refs[n_in + 2 * nr + n_out + n_scr:]
            ids = [pl.program_id(a) for a in range(len(grid))]
            first = functools.reduce(jnp.logical_and, [i == 0 for i in ids])
            last = functools.reduce(jnp.logical_and, [i == g - 1 for i, g in zip(ids, grid)])

            @pl.when(first)
            def _():
                for cp in _exchange_copies(ride_in, ride_out, *sems, kinds):
                    cp.start()

            body(*own_in, *own_out, *own_scr)

            @pl.when(last)
            def _():
                for cp in _exchange_copies(ride_in, ride_out, *sems, kinds):
                    cp.wait()

        in_specs += [any_spec] * nr
        out_specs += [any_spec] * nr
        out_shape += _exchange_shapes(arrays, kinds)
        operands += list(arrays)
        scratch += _exchange_sems(nr)
        aliases = {n_pre + n_in + r: n_out + r for r, kind in enumerate(kinds) if kind == "forward"}
    params = pltpu.CompilerParams(dimension_semantics=("arbitrary",) * len(grid), vmem_limit_bytes=VMEM_LIMIT)
    if prefetch is None:
        return pl.pallas_call(kernel, name=name, grid=grid, in_specs=in_specs, out_specs=out_specs,
                              out_shape=out_shape, scratch_shapes=scratch, input_output_aliases=aliases,
                              compiler_params=params)(*operands)
    grid_spec = pltpu.PrefetchScalarGridSpec(num_scalar_prefetch=1, grid=grid, in_specs=in_specs,
                                             out_specs=out_specs, scratch_shapes=scratch)
    return pl.pallas_call(kernel, name=name, grid_spec=grid_spec, out_shape=out_shape,
                          input_output_aliases=aliases, compiler_params=params)(prefetch, *operands)


def _exchange(name, arrays, kinds):
    n = len(arrays)

    def body(*refs):
        copies = _exchange_copies(refs[:n], refs[n:2 * n], *refs[2 * n:], kinds)
        for cp in copies:
            cp.start()
        for cp in copies:
            cp.wait()

    any_spec = pl.BlockSpec(memory_space=pl.ANY)
    return pl.pallas_call(
        functools.partial(body),
        name=name,
        in_specs=[any_spec] * n,
        out_specs=[any_spec] * n,
        out_shape=_exchange_shapes(arrays, kinds),
        scratch_shapes=_exchange_sems(n),
        input_output_aliases={i: i for i, kind in enumerate(kinds) if kind == "forward"},
    )(*arrays)


def _pair_sum(name, pairs, my_core):
    n = len(pairs)

    def body(*refs):
        for k in range(n):
            a_ref, b_ref, o_ref = refs[2 * k], refs[2 * k + 1], refs[2 * n + k]
            o_ref[...] = (a_ref[...].astype(F32) + b_ref[...].astype(F32)).astype(o_ref.dtype)

    ins, outs = [], []
    for mine, theirs in pairs:
        _, rws, cls = mine.shape
        ins += [(mine, (None, rws, cls), lambda k, core: (2 * k + core[0], 0, 0)),
                (theirs, (None, rws, cls), lambda k, core: (k, 0, 0))]
        outs.append(((N_DEV // 2, rws, cls), mine.dtype, (None, rws, cls), lambda k, core: (k, 0, 0)))
    return _call(name, body, (N_DEV // 2,), ins, outs, prefetch=my_core)


def _matmul(name, a, b, kind, tm, tn, out_dtype, blocked_out=False, riders=None):
    if kind == "tn":
        kdim, m = a.shape
    else:
        m, kdim = a.shape
    n = b.shape[0] if kind == "nt" else b.shape[1]
    tm, tn = min(tm, m), min(tn, n)
    dims = {"nn": _NN, "nt": _NT, "tn": _TN}[kind]

    def body(a_ref, b_ref, o_ref):
        o_ref[...] = _dot(a_ref[...], b_ref[...], dims).astype(o_ref.dtype)

    a_spec = (a, (kdim, tm), lambda j, i: (0, i)) if kind == "tn" else (a, (tm, kdim), lambda j, i: (i, 0))
    b_spec = (b, (tn, kdim), lambda j, i: (j, 0)) if kind == "nt" else (b, (kdim, tn), lambda j, i: (0, j))
    if blocked_out:
        out = ((n // tn, m, tn), out_dtype, (None, tm, tn), lambda j, i: (j, i, 0))
    else:
        out = ((m, n), out_dtype, (tm, tn), lambda j, i: (i, j))
    res = _call(name, body, (n // tn, m // tm), [a_spec, b_spec], [out], riders=riders)
    return res[0] if riders is None else res


def _ada_fwd(cs_all, ada_w, ada_b_cols):
    def body(c_ref, w_ref, b_ref, o_ref):
        o_ref[...] = lax.dot_general(c_ref[...], w_ref[...], _NN, preferred_element_type=F32,
                                     precision=lax.Precision.HIGHEST) + b_ref[...]

    r, d = cs_all.shape
    nc = ada_w.shape[1]
    return _call("ada_fwd", body, (1,),
                 [(cs_all, (r, d), lambda i: (0, 0)), (ada_w, (d, nc), lambda i: (0, 0)),
                  (ada_b_cols, (1, nc), lambda i: (0, 0))],
                 [((r, nc), F32, (r, nc), lambda i: (0, 0))])[0]


def _silu_rows(c_all):
    def body(c_ref, o_ref):
        v = c_ref[...]
        o_ref[...] = v * _sigmoid(v)

    return _call("silu_c", body, (1,), [(c_all, c_all.shape, lambda i: (0, 0))],
                 [(c_all.shape, F32, c_all.shape, lambda i: (0, 0))])[0]


def _pre_norm(x, g, mod, tm, riders=None):
    s, d = x.shape

    def body(x_ref, g_ref, mod_ref, h_ref):
        n, _ = _rms(x_ref[...], d)
        sh, sc = mod_ref[:, 0:d], mod_ref[:, d:2 * d]
        h_ref[...] = (n * g_ref[...] * (1.0 + sc) + sh).astype(BF16)

    return _call("pre_norm", body, (s // tm,),
                 [(x, (tm, d), lambda i: (i, 0)), (g, (1, d), lambda i: (0, 0)),
                  (mod, (1, 6 * d), lambda i: (0, 0))],
                 [((s, d), BF16, (tm, d), lambda i: (i, 0))], riders=riders)


LOG2E = 1.4426950408889634
LN2 = 0.6931471805599453


def _decay_scale(lg_ref, idx, g, sign):
    return jnp.exp((sign * idx) * lg_ref[:, g * LANES:(g + 1) * LANES])


def _prep(proj, pos_col, idx_col, inv_freq, lg_lanes, tm):
    s = proj.shape[0]
    sb_off = (2 * RET_QK + 2 * RET_V) // (3 * SB_W)
    n_q = RET_QK // LANES

    def body(qk_ref, v_ref, sb_ref, pos_ref, idx_ref, f_ref, lg_ref, qk_out, v_out, sb_out, cos_out, sin_out):
        ang = pos_ref[...] * f_ref[...]
        lane = lax.broadcasted_iota(jnp.int32, (1, LANES), 1)
        first = jnp.bitwise_and(lane, RET_DQK - 1) < (RET_DQK // 2)
        cos = jnp.cos(ang)
        sin = jnp.where(first, -1.0, 1.0) * jnp.sin(ang)
        cos_out[...] = cos
        sin_out[...] = sin
        idx = idx_ref[...]
        for g in range(2 * n_q):
            v = qk_ref[:, g * LANES:(g + 1) * LANES].astype(F32)
            sw = jnp.where(first, pltpu.roll(v, LANES - RET_DQK // 2, 1), pltpu.roll(v, RET_DQK // 2, 1))
            r = v * cos + sw * sin
            if g < n_q:
                r = r * _decay_scale(lg_ref, idx, g, 1.0)
            else:
                r = r * (_decay_scale(lg_ref, idx, g - n_q, -1.0) * (RET_DQK ** -0.5))
            qk_out[:, g * LANES:(g + 1) * LANES] = r.astype(BF16)
        v_out[...] = v_ref[...].astype(BF16)
        sb_out[:, 0:SB_W] = (sb_ref[:, 0:SB_W].astype(F32) * (SB_DH ** -0.5 * LOG2E)).astype(BF16)
        sb_out[:, SB_W:3 * SB_W] = sb_ref[:, SB_W:3 * SB_W].astype(BF16)

    return _call("prep", body, (s // tm,),
                 [(proj, (tm, 2 * RET_QK), lambda i: (i, 0)),
                  (proj, (tm, RET_V), lambda i: (i, 2 * RET_QK // RET_V)),
                  (proj, (tm, 3 * SB_W), lambda i: (i, sb_off)),
                  (pos_col, (tm, 1), lambda i: (i, 0)),
                  (idx_col, (tm, 1), lambda i: (i, 0)),
                  (inv_freq, (1, LANES), lambda i: (0, 0)),
                  (lg_lanes, (1, RET_QK), lambda i: (0, 0))],
                 [((s, 2 * RET_QK), BF16, (tm, 2 * RET_QK), lambda i: (i, 0)),
                  ((s, RET_V), BF16, (tm, RET_V), lambda i: (i, 0)),
                  ((s, 3 * SB_W), BF16, (tm, 3 * SB_W), lambda i: (i, 0)),
                  ((s, LANES), F32, (tm, LANES), lambda i: (i, 0)),
                  ((s, LANES), F32, (tm, LANES), lambda i: (i, 0))])


def _head_mask(hh):
    lane = lax.broadcasted_iota(jnp.int32, (1, LANES), 1)
    return (lane >= RET_DQK) if hh else (lane < RET_DQK)


def _masked(v, m):
    return jnp.where(m, v, jnp.zeros_like(v))


SB_GROUP = 4
SB_TQ = 256


def _stack_heads(v):
    return jnp.concatenate([_masked(v, _head_mask(0)), _masked(v, _head_mask(1))], axis=0)


def _side_by_side(v, t):
    return jnp.concatenate([v[:t], v[t:]], axis=1)


def _tile_pos(i, j, tq, tk):
    row = jnp.bitwise_and(lax.broadcasted_iota(jnp.int32, (2 * tq, tk), 0), tq - 1) + i * tq
    col = lax.broadcasted_iota(jnp.int32, (2 * tq, tk), 1) + j * tk
    return row, col


def _n_groups(i, tq, tk, grp):
    return ((i + 1) * (tq // tk) + grp - 1) // grp


def _n_full(i, tq, tk, grp):
    return (i * (tq // tk)) // grp


def _key_rows(j, tk):
    return pl.ds(pl.multiple_of(j * tk, tk), tk)


def _ret_weight(lg_rows, i, j, tq, tk):
    row, col = _tile_pos(i, j, tq, tk)
    same = jnp.right_shift(col, CHUNK_SHIFT) == jnp.right_shift(row, CHUNK_SHIFT)
    later = jnp.where(same, jnp.exp((2.0 * lg_rows) * (col - row).astype(F32)), 0.0)
    return jnp.where(col <= row, 1.0, later)


def _lg_rows(lg_ref, hp, tq):
    first = lax.broadcasted_iota(jnp.int32, (2 * tq, 1), 0) < tq
    return jnp.where(first, lg_ref[2 * hp], lg_ref[2 * hp + 1])


def _check_tiles(s, tq, tk, grp):
    assert tq % tk == 0 and tq & (tq - 1) == 0 and tk & (tk - 1) == 0
    assert s % tq == 0 and (s // tk) % grp == 0 and s // tk <= LANES


def _pair_mask():
    r = lax.broadcasted_iota(jnp.int32, (LANES, 2 * RET_DV), 0) >= RET_DQK
    c = lax.broadcasted_iota(jnp.int32, (LANES, 2 * RET_DV), 1) >= RET_DV
    return (r == c).astype(F32)


def _ret_block(lg_ref, hp, i, t, qb, kb):
    w = _ret_weight(_lg_rows(lg_ref, hp, t), i, i, t, t)
    return _dot(_stack_heads(qb), kb, _NT), w


RET_PAIRS = 2


def _lanes(ref, p, width):
    return ref[:, p * width:(p + 1) * width]


def _ret_fwd(qk_rot, v_bf, proj, gn_g, log_gamma, t, riders=None):
    s = qk_rot.shape[0]
    n_pair = HEADS // 2
    pw = 2 * RET_DV
    wq, wv = RET_PAIRS * LANES, RET_PAIRS * pw
    gate_off = (2 * RET_QK + RET_V) // wv
    assert s % t == 0 and t % CHUNK == 0 and t & (t - 1) == 0 and n_pair % RET_PAIRS == 0

    def body(lg_ref, q_ref, k_ref, v_ref, g_ref, w_ref, ret_ref, rg_ref, state_ref):
        hg, i = pl.program_id(0), pl.program_id(1)

        @pl.when(i == 0)
        def _():
            state_ref[...] = jnp.zeros_like(state_ref)

        pairs = range(RET_PAIRS)
        qbs = [_lanes(q_ref, p, LANES) for p in pairs]
        kbs = [_lanes(k_ref, p, LANES) for p in pairs]
        vbs = [_lanes(v_ref, p, pw) for p in pairs]
        zws = [_ret_block(lg_ref, hg * RET_PAIRS + p, i, t, qbs[p], kbs[p]) for p in pairs]
        ps = [(z * w).astype(BF16) for z, w in zws]
        outs = [jnp.concatenate([_dot(ps[p][:t], vbs[p][:, 0:RET_DV]), _dot(ps[p][t:], vbs[p][:, RET_DV:pw])], axis=1)
                + _dot(qbs[p], state_ref[p]) for p in pairs]
        for p in pairs:
            state_ref[p] += _pair_mask() * _dot(kbs[p], vbs[p], _TN)
        for p in pairs:
            for hh in range(2):
                cols = slice(p * pw + hh * RET_DV, p * pw + (hh + 1) * RET_DV)
                o = outs[p][:, hh * RET_DV:(hh + 1) * RET_DV]
                ret_ref[:, cols] = o
                mu = jnp.sum(o, axis=1, keepdims=True) * (1.0 / RET_DV)
                xc = o - mu
                var = jnp.sum(xc * xc, axis=1, keepdims=True) * (1.0 / RET_DV)
                nrm = xc * lax.rsqrt(var + EPS) * w_ref[:, cols]
                g = g_ref[:, cols].astype(F32)
                rg_ref[:, cols] = (g * _sigmoid(g) * nrm).astype(BF16)

    blk = lambda hg, i: (i, hg)
    return _call("ret_fwd", body, (n_pair // RET_PAIRS, s // t),
                 [(log_gamma, None, pltpu.SMEM),
                  (qk_rot, (t, wq), blk),
                  (qk_rot, (t, wq), lambda hg, i: (i, n_pair // RET_PAIRS + hg)),
                  (v_bf, (t, wv), blk),
                  (proj, (t, wv), lambda hg, i: (i, gate_off + hg)),
                  (gn_g, (1, wv), lambda hg, i: (0, hg))],
                 [((s, RET_V), F32, (t, wv), blk), ((s, RET_V), BF16, (t, wv), blk)],
                 scratch=[pltpu.VMEM((RET_PAIRS, LANES, pw), F32)], riders=riders)


def _tri(tk, strict_upper):
    r = lax.broadcasted_iota(jnp.int32, (tk, tk), 0)
    cc = lax.broadcasted_iota(jnp.int32, (tk, tk), 1)
    return ((r > cc) if strict_upper else (r < cc)).astype(BF16)


def _diagonal_step(i, tq, tk, make, carry):
    if (tq // tk) % SB_GROUP == 0:
        return make(SB_GROUP)(0, carry)
    assert 2 * (tq // tk) == SB_GROUP
    half = lax.rem(i, 2) == 0
    return lax.cond(half, lambda cr: make(SB_GROUP // 2)(0, cr), lambda cr: make(SB_GROUP)(0, cr), carry)


def _sb_valid(i, j, tq, tk):
    row, col = _tile_pos(i, j, tq, tk)
    return col < row


def _sb_fwd(qkv, tq, tk, riders=None):
    s = qkv.shape[0]
    n_pair = HEADS // 2
    _check_tiles(s, tq, tk, SB_GROUP)

    def body(q_ref, k_ref, v_ref, o_ref, a_ref):
        i = pl.program_id(1)
        upper = _tri(tk, True)
        qs = _stack_heads(q_ref[...])
        n_full, n_groups = _n_full(i, tq, tk, SB_GROUP), _n_groups(i, tq, tk, SB_GROUP)

        def make_step(near_diagonal, last, n_sub=SB_GROUP):
            def step(n, carry):
                c, o = carry
                g = last - 1 - n
                js = [g * SB_GROUP + sub for sub in range(n_sub)]
                zs = [_dot(qs, k_ref[_key_rows(j, tk), :], _NT) for j in js]
                log1ps = [jnp.log2(1.0 + jnp.exp2(-jnp.abs(z))) for z in zs]
                log_1ms = [-jnp.maximum(z, 0.0) - t for z, t in zip(zs, log1ps)]
                log_bs = [jnp.minimum(z, 0.0) - t for z, t in zip(zs, log1ps)]
                if near_diagonal:
                    valids = [_sb_valid(i, j, tq, tk) for j in js]
                    log_1ms = [jnp.where(v, l, 0.0) for v, l in zip(valids, log_1ms)]
                sticks = [_dot(l, upper) for l in log_1ms]
                sums = [jnp.sum(l, axis=1, keepdims=True) for l in log_1ms]
                cs = [None] * n_sub
                for sub in reversed(range(n_sub)):
                    cs[sub] = c
                    c = c + sums[sub]
                for sub, j in enumerate(js):
                    a = jnp.exp2(log_bs[sub] + sticks[sub] + cs[sub])
                    if near_diagonal:
                        a = jnp.where(valids[sub], a, 0.0)
                    a = a.astype(BF16)
                    a_ref[j] = a
                    o = o + _dot(_side_by_side(a, tq), _stack_heads(v_ref[_key_rows(j, tk), :]))
                return c, o
            return step

        carry = (jnp.zeros((2 * tq, 1), F32), jnp.zeros((tq, LANES), F32))
        carry = _diagonal_step(i, tq, tk, lambda n_sub: make_step(True, n_groups, n_sub), carry)
        _, acc = lax.fori_loop(0, n_full, make_step(False, n_full), carry)
        o_ref[...] = acc.astype(BF16)

    n_kb = s // tk
    return _call("sb_fwd", body, (n_pair, s // tq),
                 [(qkv, (tq, LANES), lambda hp, i: (i, hp)),
                  (qkv, (s, LANES), lambda hp, i: (0, n_pair + hp)),
                  (qkv, (s, LANES), lambda hp, i: (0, 2 * n_pair + hp))],
                 [((s, SB_W), BF16, (tq, LANES), lambda hp, i: (i, hp)),
                  ((n_pair, s // tq, n_kb, 2 * tq, tk), BF16, (None, None, n_kb, 2 * tq, tk),
                   lambda hp, i: (hp, i, 0, 0, 0))], riders=riders)


def _merge_out(retg, sb, w_ret, w_sb_t, w_out, proj, x, mod, gp1, g2, tm):
    s, d = x.shape
    gw = min(512, d)
    n_g = d // gw
    ar_off = (2 * RET_QK + 2 * RET_V + 3 * SB_W) // gw

    def body(rg_ref, sb_ref, wr_ref, ws_ref, wo_ref, *refs):
        gate_refs, (x_ref, mod_ref, gp_ref, g2_ref, mix_ref, r_ref, s_ref, y_ref, hres_ref, h2_ref) = refs[:2 * n_g], refs[2 * n_g:]
        for rows in _pieces(tm):
            rr = _dot(rg_ref[rows, :], wr_ref[...])
            ss = _dot(sb_ref[rows, :], ws_ref[...], _NT)
            a_r = jnp.concatenate([g[rows, :] for g in gate_refs[:n_g]], axis=1).astype(F32)
            a_s = jnp.concatenate([g[rows, :] for g in gate_refs[n_g:]], axis=1).astype(F32)
            mixed = (_sigmoid(a_r) * rr + _sigmoid(a_s) * ss).astype(BF16)
            mix_ref[rows, :] = mixed
            r_ref[rows, :] = rr.astype(BF16)
            s_ref[rows, :] = ss.astype(BF16)
            y = _dot(mixed, wo_ref[...])
            y_ref[rows, :] = y
            ny, _ = _rms(y, d)
            hres = x_ref[rows, :] + mod_ref[:, 2 * d:3 * d] * (ny * gp_ref[...])
            hres_ref[rows, :] = hres
            n2, _ = _rms(hres, d)
            h2_ref[rows, :] = (n2 * g2_ref[...] * (1.0 + mod_ref[:, 4 * d:5 * d]) + mod_ref[:, 3 * d:4 * d]).astype(BF16)

    row = lambda i: (i, 0)
    fix = lambda i: (0, 0)
    tile_bf = ((s, d), BF16, (tm, d), row)
    tile_f = ((s, d), F32, (tm, d), row)
    return _call("merge_out", body, (s // tm,),
                 [(retg, (tm, RET_V), row), (sb, (tm, SB_W), row), (w_ret, (RET_V, d), fix), (w_sb_t, (d, SB_W), fix),
                  (w_out, (d, d), fix)]
                 + [(proj, (tm, gw), functools.partial(lambda i, k: (i, ar_off + k), k=k)) for k in range(2 * n_g)]
                 + [(x, (tm, d), row), (mod, (1, 6 * d), fix), (gp1, (1, d), fix), (g2, (1, d), fix)],
                 [tile_bf, tile_bf, tile_bf, tile_f, tile_f, tile_bf])


def _ff1(h2, w_ff1_t, tm, tn):
    s, f = h2.shape[0], w_ff1_t.shape[0]
    tm = min(tm, s)

    def body(a_ref, w_ref, u_ref, act_ref):
        u = _dot(a_ref[...], w_ref[...], _NT)
        r = jnp.maximum(u, 0.0)
        u_ref[...] = u.astype(BF16)
        act_ref[...] = (r * r).astype(BF16)

    d = h2.shape[1]
    return _call("ff1", body, (f // tn, s // tm),
                 [(h2, (tm, d), lambda j, i: (i, 0)), (w_ff1_t, (tn, d), lambda j, i: (j, 0))],
                 [((s, f), BF16, (tm, tn), lambda j, i: (i, j))] * 2)


def _ff2_loss(act, w_ff2, hres, target, mod, gp2, tm):
    s, d = hres.shape
    f = act.shape[1]

    def body(a_ref, w_ref, h_ref, t_ref, mod_ref, gp_ref, dout_ref, df_ref, loss_ref, dgt_ref, dgp_ref):
        _zero_at_start([loss_ref, dgt_ref, dgp_ref])
        gt, gp = mod_ref[:, 5 * d:6 * d], gp_ref[...]
        for rows in _pieces(tm):
            ff = _dot(a_ref[rows, :], w_ref[...])
            nf, rf = _rms(ff, d)
            out = h_ref[rows, :] + gt * (nf * gp)
            err = out - t_ref[rows, :]
            sq = jnp.sum(err * err, axis=1, keepdims=True)
            loss_ref[...] += jnp.sum(sq, axis=0, keepdims=True)
            dout = err * (1.0 / d)
            dout_ref[rows, :] = dout
            dgt_ref[...] += _colsum(dout * (nf * gp))
            dgp_ref[...] += _colsum(dout * gt * nf)
            df_ref[rows, :] = _rms_bwd(dout * gt * gp, nf, rf, d).astype(BF16)

    row = lambda i: (i, 0)
    fix = lambda i: (0, 0)
    return _call("ff2_loss", body, (s // tm,),
                 [(act, (tm, f), row), (w_ff2, (f, d), fix), (hres, (tm, d), row), (target, (tm, d), row),
                  (mod, (1, 6 * d), fix), (gp2, (1, d), fix)],
                 [((s, d), F32, (tm, d), row), ((s, d), BF16, (tm, d), row), ((1, 1), F32, (1, 1), fix),
                  ((1, d), F32, (1, d), fix), ((1, d), F32, (1, d), fix)])


def _ffn_bwd(df, w_ff2, u, act, h2, tn):
    s, d = df.shape
    f = w_ff2.shape[0]

    def body(df_ref, w_ref, u_ref, act_ref, h2_ref, du_ref, gw2_ref, gw1_ref):
        dfb = df_ref[...]
        du = (_dot(dfb, w_ref[...], _NT) * (2.0 * jnp.maximum(u_ref[...].astype(F32), 0.0))).astype(BF16)
        du_ref[...] = du
        gw2_ref[...] = _dot(act_ref[...], dfb, _TN).astype(BF16)
        gw1_ref[...] = _dot(h2_ref[...], du, _TN).astype(BF16)

    fix = lambda j: (0, 0)
    col = lambda j: (0, j)
    return _call("ffn_bwd", body, (f // tn,),
                 [(df, (s, d), fix), (w_ff2, (tn, d), lambda j: (j, 0)), (u, (s, tn), col), (act, (s, tn), col),
                  (h2, (s, d), fix)],
                 [((s, f), BF16, (s, tn), col), ((f, d), BF16, (tn, d), lambda j: (j, 0)),
                  ((f // tn, d, tn), BF16, (None, d, tn), lambda j: (j, 0, 0))])


def _ff1_bwd(du, w_ff1_t, hres, dout, y, mod, g2, gp1, tm, riders=None):
    s, d = hres.shape
    f = du.shape[1]

    def body(a_ref, w_ref, h_ref, do_ref, y_ref, mod_ref, g2_ref, gp_ref,
             dh_ref, dy_ref, dsh_ref, dsc_ref, dg2_ref, dgt_ref, dgp_ref):
        _zero_at_start([dsh_ref, dsc_ref, dg2_ref, dgt_ref, dgp_ref])
        g2, sc2 = g2_ref[...], mod_ref[:, 4 * d:5 * d]
        gt, gp = mod_ref[:, 2 * d:3 * d], gp_ref[...]
        for rows in _pieces(tm):
            dh2 = _dot(a_ref[rows, :], w_ref[...])
            n2, r2 = _rms(h_ref[rows, :], d)
            dsh_ref[...] += _colsum(dh2)
            dsc_ref[...] += _colsum(dh2 * n2 * g2)
            dg2_ref[...] += _colsum(dh2 * n2 * (1.0 + sc2))
            dhres = do_ref[rows, :] + _rms_bwd(dh2 * g2 * (1.0 + sc2), n2, r2, d)
            dh_ref[rows, :] = dhres
            ny, ry = _rms(y_ref[rows, :], d)
            dgt_ref[...] += _colsum(dhres * (ny * gp))
            dgp_ref[...] += _colsum(dhres * gt * ny)
            dy_ref[rows, :] = _rms_bwd(dhres * gt * gp, ny, ry, d).astype(BF16)

    row = lambda i: (i, 0)
    fix = lambda i: (0, 0)
    vec = ((1, d), F32, (1, d), fix)
    return _call("ff1_bwd", body, (s // tm,),
                 [(du, (tm, f), row), (w_ff1_t, (f, d), fix), (hres, (tm, d), row), (dout, (tm, d), row),
                  (y, (tm, d), row), (mod, (1, 6 * d), fix), (g2, (1, d), fix), (gp1, (1, d), fix)],
                 [((s, d), F32, (tm, d), row), ((s, d), BF16, (tm, d), row), vec, vec, vec, vec, vec], riders=riders)


def _out_bwd(dy, w_out, proj, r_bf, s_bf, tm, tn, riders=None):
    s, d = dy.shape
    ar_off = (2 * RET_QK + 2 * RET_V + 3 * SB_W) // tn
    as_off = ar_off + d // tn

    def body(a_ref, w_ref, ar_ref, as_ref, r_ref, s_ref, dr_ref, ds_ref, dar_ref, das_ref):
        dm = _dot(a_ref[...], w_ref[...], _NT)
        sr, ss = _sigmoid(ar_ref[...].astype(F32)), _sigmoid(as_ref[...].astype(F32))
        dr_ref[...] = (dm * sr).astype(BF16)
        ds_ref[...] = (dm * ss).astype(BF16)
        dar_ref[...] = (dm * r_ref[...].astype(F32) * sr * (1.0 - sr)).astype(BF16)
        das_ref[...] = (dm * s_ref[...].astype(F32) * ss * (1.0 - ss)).astype(BF16)

    tile = (tm, tn)
    here = lambda j, i: (i, j)
    return _call("out_bwd", body, (d // tn, s // tm),
                 [(dy, (tm, d), lambda j, i: (i, 0)), (w_out, (tn, d), lambda j, i: (j, 0)),
                  (proj, tile, lambda j, i: (i, ar_off + j)), (proj, tile, lambda j, i: (i, as_off + j)),
                  (r_bf, tile, here), (s_bf, tile, here)],
                 [((s, d), BF16, tile, here)] * 4, riders=riders)


def _branch_bwd(d_r, d_s, retg, sb, w_ret, w_sb_t, ret, proj, gn_g):
    s, d = d_r.shape
    n_step = 4
    part_v, part_s, part_d = RET_V // n_step, SB_W // n_step, d // n_step
    per_dev = d // N_DEV
    n_blk = part_d // per_dev
    gate_off = (2 * RET_QK + RET_V) // part_v

    def body(dr_ref, ds_ref, rg_ref, sb_ref, wr_ref, ws_ref, r_ref, g_ref, w_ref,
             dsb_ref, gwr_ref, gws_ref, dg_ref, dret_ref, dw_ref):
        i = pl.program_id(0)
        dr, ds = dr_ref[...], ds_ref[...]
        dsb_ref[...] = _dot(ds, ws_ref[...]).astype(BF16)
        gwr_ref[...] = _dot(rg_ref[...], dr, _TN).astype(BF16)
        cols = pl.ds(pl.multiple_of(i * part_d, part_d), part_d)
        gws = _dot(sb_ref[...], ds_ref[:, cols], _TN).astype(BF16)
        for k in range(n_blk):
            gws_ref[k] = gws[:, k * per_dev:(k + 1) * per_dev]
        dretg = _dot(dr, wr_ref[...], _NT)
        for h in range(part_v // RET_DV):
            cols = slice(h * RET_DV, (h + 1) * RET_DV)
            o, g, w, d_o = r_ref[:, cols], g_ref[:, cols].astype(F32), w_ref[:, cols], dretg[:, cols]
            mu = jnp.sum(o, axis=1, keepdims=True) * (1.0 / RET_DV)
            xc = o - mu
            rstd = lax.rsqrt(jnp.sum(xc * xc, axis=1, keepdims=True) * (1.0 / RET_DV) + EPS)
            n = xc * rstd
            sg = _sigmoid(g)
            silu = g * sg
            dg_ref[:, cols] = (d_o * n * w * (sg * (1.0 + g * (1.0 - sg)))).astype(BF16)
            dw_ref[:, cols] = _colsum(d_o * silu * n)
            dn = d_o * silu * w
            m1 = jnp.sum(dn, axis=1, keepdims=True) * (1.0 / RET_DV)
            m2 = jnp.sum(dn * n, axis=1, keepdims=True) * (1.0 / RET_DV)
            dret_ref[:, cols] = (rstd * (dn - m1 - n * m2)).astype(BF16)

    fix = lambda i: (0, 0)
    col = lambda i: (0, i)
    return _call("branch_bwd", body, (n_step,),
                 [(d_r, (s, d), fix), (d_s, (s, d), fix), (retg, (s, part_v), col), (sb, (s, SB_W), fix),
                  (w_ret, (part_v, d), lambda i: (i, 0)), (w_sb_t, (d, part_s), col),
                  (ret, (s, part_v), col), (proj, (s, part_v), lambda i: (0, gate_off + i)), (gn_g, (1, part_v), col)],
                 [((s, SB_W), BF16, (s, part_s), col), ((RET_V, d), BF16, (part_v, d), lambda i: (i, 0)),
                  ((N_DEV, SB_W, per_dev), BF16, (n_blk, SB_W, per_dev), lambda i: (i, 0, 0)),
                  ((s, RET_V), BF16, (s, part_v), col), ((s, RET_V), BF16, (s, part_v), col),
                  ((1, RET_V), F32, (1, part_v), col)])


def _ret_bwd(qk_rot, v_bf, dret, log_gamma, t, riders=None):
    s = qk_rot.shape[0]
    n_pair = HEADS // 2
    pw = 2 * RET_DV
    wq, wv = RET_PAIRS * LANES, RET_PAIRS * pw
    n_blk = s // t
    pairs = range(RET_PAIRS)

    def load(q_ref, k_ref, v_ref, do_ref):
        return ([_lanes(q_ref, p, LANES) for p in pairs], [_lanes(k_ref, p, LANES) for p in pairs],
                [_lanes(v_ref, p, pw) for p in pairs], [_lanes(do_ref, p, pw) for p in pairs])

    def d_scores(lg_ref, hp, i, qb, kb, vb, dob):
        z, w = _ret_block(lg_ref, hp, i, t, qb, kb)
        dp = jnp.concatenate([_dot(dob[:, 0:RET_DV], vb[:, 0:RET_DV], _NT),
                              _dot(dob[:, RET_DV:pw], vb[:, RET_DV:pw], _NT)], axis=0)
        return (z * w).astype(BF16), (dp * w).astype(BF16)

    def up_body(lg_ref, q_ref, k_ref, v_ref, do_ref, dq_ref, state_ref):
        hg, i = pl.program_id(0), pl.program_id(1)

        @pl.when(i == 0)
        def _():
            state_ref[...] = jnp.zeros_like(state_ref)

        qbs, kbs, vbs, dobs = load(q_ref, k_ref, v_ref, do_ref)
        dss = [d_scores(lg_ref, hg * RET_PAIRS + p, i, qbs[p], kbs[p], vbs[p], dobs[p])[1] for p in pairs]
        for p in pairs:
            dq_ref[:, p * LANES:(p + 1) * LANES] = (_dot(_side_by_side(dss[p], t), _stack_heads(kbs[p]))
                                                    + _dot(dobs[p], state_ref[p], _NT)).astype(BF16)
        for p in pairs:
            state_ref[p] += _pair_mask() * _dot(kbs[p], vbs[p], _TN)

    def down_body(lg_ref, q_ref, k_ref, v_ref, do_ref, dk_ref, dv_ref, state_ref):
        hg, i = pl.program_id(0), n_blk - 1 - pl.program_id(1)

        @pl.when(pl.program_id(1) == 0)
        def _():
            state_ref[...] = jnp.zeros_like(state_ref)

        qbs, kbs, vbs, dobs = load(q_ref, k_ref, v_ref, do_ref)
        both = [d_scores(lg_ref, hg * RET_PAIRS + p, i, qbs[p], kbs[p], vbs[p], dobs[p]) for p in pairs]
        for p in pairs:
            pp, ds = both[p]
            later = state_ref[p]
            dv_ref[:, p * pw:(p + 1) * pw] = (jnp.concatenate(
                [_dot(pp[:t], dobs[p][:, 0:RET_DV], _TN), _dot(pp[t:], dobs[p][:, RET_DV:pw], _TN)],
                axis=1) + _dot(kbs[p], later)).astype(BF16)
            dk_ref[:, p * LANES:(p + 1) * LANES] = (_dot(ds, _stack_heads(qbs[p]), _TN)
                                                    + _dot(vbs[p], later, _NT)).astype(BF16)
        for p in pairs:
            state_ref[p] += _pair_mask() * _dot(qbs[p], dobs[p], _TN)

    n_grp = n_pair // RET_PAIRS

    def ins(order):
        return [(log_gamma, None, pltpu.SMEM),
                (qk_rot, (t, wq), lambda hg, i: (order(i), hg)),
                (qk_rot, (t, wq), lambda hg, i: (order(i), n_grp + hg)),
                (v_bf, (t, wv), lambda hg, i: (order(i), hg)),
                (dret, (t, wv), lambda hg, i: (order(i), hg))]

    up = lambda i: i
    down = lambda i: n_blk - 1 - i
    scratch = [pltpu.VMEM((RET_PAIRS, LANES, pw), F32)]
    dq = _call("ret_bwd_q", up_body, (n_grp, n_blk), ins(up),
               [((s, RET_QK), BF16, (t, wq), lambda hg, i: (i, hg))], scratch=scratch)[0]
    dk, dv, *rest = _call("ret_bwd_kv", down_body, (n_grp, n_blk), ins(down),
                          [((s, RET_QK), BF16, (t, wq), lambda hg, i: (down(i), hg)),
                           ((s, RET_V), BF16, (t, wv), lambda hg, i: (down(i), hg))],
                          scratch=scratch, riders=riders)
    return [dq, dk, dv] + rest


def _sb_bwd(qkv, weights, do, tq, tk, riders=None):
    s = qkv.shape[0]
    n_pair = HEADS // 2
    _check_tiles(s, tq, tk, SB_GROUP)

    def body(q_ref, k_ref, v_ref, a_ref, do_ref, dq_ref, dk_ref, dv_ref):
        i = pl.program_id(1)

        @pl.when(i == 0)
        def _():
            dk_ref[...] = jnp.zeros_like(dk_ref)
            dv_ref[...] = jnp.zeros_like(dv_ref)

        lower = _tri(tk, False)
        qs = _stack_heads(q_ref[...])
        dos = _stack_heads(do_ref[...].astype(BF16))

        def make_step(near_diagonal, n_sub=SB_GROUP):
            def step(g, carry):
                c_e, dq = carry
                js = [g * SB_GROUP + sub for sub in range(n_sub)]
                rows = [_key_rows(j, tk) for j in js]
                zs = [_dot(qs, k_ref[rw, :], _NT) for rw in rows]
                das = [_dot(dos, v_ref[rw, :], _NT) for rw in rows]
                avals = [a_ref[j] for j in js]
                for a, rw in zip(avals, rows):
                    dv_ref[rw, :] += _dot(a, dos, _TN)
                es = [a.astype(F32) * da for a, da in zip(avals, das)]
                prefixes = [_dot(e, lower) for e in es]
                betas = [1.0 / (1.0 + jnp.exp2(-z)) for z in zs]
                for sub in range(n_sub):
                    dz = es[sub] - (es[sub] + prefixes[sub] + c_e) * betas[sub]
                    if near_diagonal:
                        dz = jnp.where(_sb_valid(i, js[sub], tq, tk), dz, 0.0)
                    dz = dz.astype(BF16)
                    dk_ref[rows[sub], :] += _dot(dz, qs, _TN)
                    dq = dq + _dot(_side_by_side(dz, tq), _stack_heads(k_ref[rows[sub], :]))
                    c_e = c_e + jnp.sum(es[sub], axis=1, keepdims=True)
                return c_e, dq
            return step

        n_full = _n_full(i, tq, tk, SB_GROUP)
        carry = (jnp.zeros((2 * tq, 1), F32), jnp.zeros((tq, LANES), F32))
        carry = lax.fori_loop(0, n_full, make_step(False), carry)
        _, dq = _diagonal_step(i, tq, tk, lambda n_sub: (lambda n, cr: make_step(True, n_sub)(n_full, cr)), carry)
        dq_ref[...] = dq

    blk = lambda hp, i: (i, hp)
    n_kb = s // tk
    return _call("sb_bwd", body, (n_pair, s // tq),
                 [(qkv, (tq, LANES), blk),
                  (qkv, (s, LANES), lambda hp, i: (0, n_pair + hp)),
                  (qkv, (s, LANES), lambda hp, i: (0, 2 * n_pair + hp)),
                  (weights, (None, None, n_kb, 2 * tq, tk), lambda hp, i: (hp, i, 0, 0, 0)),
                  (do, (tq, LANES), blk)],
                 [((s, SB_W), F32, (tq, LANES), blk),
                  ((s, SB_W), F32, (s, LANES), lambda hp, i: (0, hp)),
                  ((s, SB_W), F32, (s, LANES), lambda hp, i: (0, hp))], riders=riders)


def _assemble_dproj(dq_r, dk_r, dv_r, dg_r, dq_s, dk_s, dv_s, da_r, da_s, cos, sin, idx_col, lg_lanes, tm, riders=None):
    s, d = da_r.shape
    width = 2 * RET_QK + 2 * RET_V + 3 * SB_W + 2 * d

    def body(dq_ref, dk_ref, dv_ref, dg_ref, dqs_ref, dks_ref, dvs_ref, dar_ref, das_ref, cos_ref, sin_ref,
             idx_ref, lg_ref, o_ref):
        lane = lax.broadcasted_iota(jnp.int32, (1, LANES), 1)
        first = jnp.bitwise_and(lane, RET_DQK - 1) < (RET_DQK // 2)
        cos, sin = cos_ref[...], sin_ref[...]
        idx = idx_ref[...]
        for src, base, sign, scale in ((dq_ref, 0, 1.0, 1.0), (dk_ref, RET_QK, -1.0, RET_DQK ** -0.5)):
            for g in range(RET_QK // LANES):
                v = src[:, g * LANES:(g + 1) * LANES].astype(F32) * (_decay_scale(lg_ref, idx, g, sign) * scale)
                sw = jnp.where(first, pltpu.roll(v, LANES - RET_DQK // 2, 1), pltpu.roll(v, RET_DQK // 2, 1))
                o_ref[:, base + g * LANES:base + (g + 1) * LANES] = (v * cos - sw * sin).astype(BF16)
        off = 2 * RET_QK
        o_ref[:, off:off + RET_V] = dv_ref[...].astype(BF16)
        off += RET_V
        o_ref[:, off:off + RET_V] = dg_ref[...]
        off += RET_V
        o_ref[:, off:off + SB_W] = (dqs_ref[...] * (SB_DH ** -0.5)).astype(BF16)
        off += SB_W
        o_ref[:, off:off + SB_W] = (dks_ref[...] * LN2).astype(BF16)
        off += SB_W
        o_ref[:, off:off + SB_W] = dvs_ref[...].astype(BF16)
        off += SB_W
        o_ref[:, off:off + d] = dar_ref[...]
        off += d
        o_ref[:, off:off + d] = das_ref[...]

    row = lambda i: (i, 0)
    ins = [(a, (tm, a.shape[1]), row) for a in (dq_r, dk_r, dv_r, dg_r, dq_s, dk_s, dv_s, da_r, da_s, cos, sin, idx_col)]
    ins.append((lg_lanes, (1, RET_QK), lambda i: (0, 0)))
    return _call("assemble_dproj", body, (s // tm,), ins, [((s, width), BF16, (tm, width), row)], riders=riders)


def _in_bwd(dproj, w_in_t, x, dhres, mod, g1, tm, riders=None):
    s, d = x.shape
    width = dproj.shape[1]

    def body(a_ref, w_ref, x_ref, dh_ref, mod_ref, g_ref, dx_ref, dsh_ref, dsc_ref, dg_ref):
        _zero_at_start([dsh_ref, dsc_ref, dg_ref])
        g1, sc1 = g_ref[...], mod_ref[:, d:2 * d]
        for rows in _pieces(tm):
            dh = _dot(a_ref[rows, :], w_ref[...])
            n1, r1 = _rms(x_ref[rows, :], d)
            dsh_ref[...] += _colsum(dh)
            dsc_ref[...] += _colsum(dh * n1 * g1)
            dg_ref[...] += _colsum(dh * n1 * (1.0 + sc1))
            dx_ref[rows, :] = dh_ref[rows, :] + _rms_bwd(dh * g1 * (1.0 + sc1), n1, r1, d)

    row = lambda i: (i, 0)
    fix = lambda i: (0, 0)
    vec = ((1, d), F32, (1, d), fix)
    return _call("in_bwd", body, (s // tm,),
                 [(dproj, (tm, width), row), (w_in_t, (width, d), fix), (x, (tm, d), row), (dhres, (tm, d), row),
                  (mod, (1, 6 * d), fix), (g1, (1, d), fix)],
                 [((s, d), F32, (tm, d), row), vec, vec, vec], riders=riders)


def _adamw(w, g, m, v):
    m = ADAM_B1 * m + (1.0 - ADAM_B1) * g
    v = ADAM_B2 * v + (1.0 - ADAM_B2) * (g * g)
    m_hat = m / (1.0 - ADAM_B1 ** ADAM_STEP)
    v_hat = v / (1.0 - ADAM_B2 ** ADAM_STEP)
    delta = -ADAM_LR * (m_hat / (jnp.sqrt(v_hat) + ADAM_EPS) + ADAM_WD * w)
    return delta, m, v


def _adam_reduce(name, sets, steps):
    n = len(sets)

    def body(*refs):
        for k in range(n):
            p_ref, w_ref, m_ref, v_ref = refs[4 * k:4 * k + 4]
            outs = refs[4 * n + 4 * k:4 * n + 4 * k + 4]
            g = p_ref[0].astype(F32)
            for j in range(1, p_ref.shape[0]):
                g = g + p_ref[j].astype(F32)
            for o_ref, val in zip(outs, (g,) + _adamw(w_ref[...], g, m_ref[...], v_ref[...])):
                o_ref[...] = val

    ins, outs = [], []
    row = lambda i: (i, 0)
    for parts, w, m, v in sets:
        rws, cls = w.shape
        tr = rws // steps
        assert tr * steps == rws and tr % 16 == 0
        ins += [(parts, (parts.shape[0], tr, cls), lambda i: (0, i, 0)), (w, (tr, cls), row), (m, (tr, cls), row),
                (v, (tr, cls), row)]
        outs += [((rws, cls), F32, (tr, cls), row)] * 4
    res = _call(name, body, (steps,), ins, outs)
    return [res[4 * k:4 * k + 4] for k in range(n)]


def _ada_bwd_adam(cs_t, dmod_cols, w, m, v):
    d, nc = w.shape

    def body(c_ref, dm_ref, w_ref, m_ref, v_ref, g_out, d_out, m_out, v_out):
        g = c_ref[0] * dm_ref[0:1, :]
        for r in range(1, N_DEV):
            g = g + c_ref[r] * dm_ref[r:r + 1, :]
        delta, mn, vn = _adamw(w_ref[...], g, m_ref[...], v_ref[...])
        g_out[...] = g
        d_out[...] = delta
        m_out[...] = mn
        v_out[...] = vn

    fix = lambda i: (0, 0)
    blk = (d, nc)
    return _call("ada_bwd_adam", body, (1,),
                 [(cs_t, (N_DEV, d, 1), lambda i: (0, 0, 0)), (dmod_cols, (N_DEV, nc), fix), (w, blk, fix), (m, blk, fix), (v, blk, fix)],
                 [((d, nc), F32, blk, fix)] * 4)


def _small_adam(parts, ws, ms, vs):
    n = len(ws)
    widths = [w.shape[1] for w in ws]
    total = parts.shape[1]
    assert sum(widths) + LANES == total

    def body(p_ref, *refs):
        w_refs, m_refs, v_refs = refs[:n], refs[n:2 * n], refs[2 * n:3 * n]
        outs = refs[3 * n:]
        g = p_ref[0:1, :]
        for k in range(1, N_DEV):
            g = g + p_ref[k:k + 1, :]
        off = 0
        for i, width in enumerate(widths):
            gi = g[:, off:off + width]
            delta, mn, vn = _adamw(w_refs[i][...], gi, m_refs[i][...], v_refs[i][...])
            for o_ref, val in zip(outs[4 * i:4 * i + 4], (gi, delta, mn, vn)):
                o_ref[...] = val
            off += width
        outs[4 * n][...] = g[:, off:off + LANES]

    fix = lambda i: (0, 0)
    vec = lambda a: (a, (1, a.shape[1]), fix)
    out_specs = [((1, width), F32, (1, width), fix) for width in widths for _ in range(4)]
    out_specs.append(((1, LANES), F32, (1, LANES), fix))
    res = _call("small_adam", body, (1,),
                [(parts, (N_DEV, total), fix)] + [vec(a) for a in list(ws) + list(ms) + list(vs)], out_specs)
    return [res[4 * i:4 * i + 4] for i in range(n)], res[4 * n]


def kernel(x, c, positions, ada_w, ada_b, pre_mix_g, post_mix_g, pre_ffn_g, post_ffn_g, w_in, ret_gn_g, w_ret_branch, w_sb_branch, w_out, w_ff1, w_ff2, loss_target, m_ada_w, m_ada_b, m_pre_mix_g, m_post_mix_g, m_pre_ffn_g, m_post_ffn_g, m_w_in, m_ret_gn_g, m_w_ret_branch, m_w_sb_branch, m_w_out, m_w_ff1, m_w_ff2, v_ada_w, v_ada_b, v_pre_mix_g, v_post_mix_g, v_pre_ffn_g, v_post_ffn_g, v_w_in, v_ret_gn_g, v_w_ret_branch, v_w_sb_branch, v_w_out, v_w_ff1, v_w_ff2):
    _, s, d = x.shape
    d_ff = w_ff1.shape[2] * N_DEV
    d_in = w_in.shape[2] * N_DEV
    me = 4 * lax.axis_index("x") + 2 * lax.axis_index("y") + lax.axis_index("c")
    x2, tgt = x[0], loss_target[0]

    core = lax.axis_index("c").astype(jnp.int32).reshape(1)
    bf = lambda w: w[0].astype(BF16)

    w_in_t, m_in_t, v_in_t = (jnp.swapaxes(a[0], 0, 1) for a in (w_in, m_w_in, v_w_in))

    c_all, g_in = _exchange("gather_in", [c, w_in_t.astype(BF16)], ["gather", "gather_chip"])
    c_all = c_all.reshape(N_DEV, d)

    n_ada = ada_w.shape[2]
    cs_all = _silu_rows(c_all)
    ada_b_cols = lax.dynamic_slice(ada_b, (0, me * n_ada), (1, n_ada))
    mod_cols = _ada_fwd(cs_all, ada_w[0], ada_b_cols)
    mod_all = _exchange("gather_mod", [mod_cols], ["gather"])[0]
    mod = lax.dynamic_index_in_dim(mod_all, me, axis=1, keepdims=False).reshape(1, 6 * d)

    tm = min(256, s)
    h, g_in = _pre_norm(x2, pre_mix_g, mod, 2 * tm, riders=([g_in], ["forward"]))
    wt_in = g_in.reshape(d_in, d)
    bf_t = lambda w: jnp.swapaxes(w[0], 0, 1).astype(BF16)
    small_w = [bf(w_ret_branch), bf_t(w_sb_branch), bf(w_out)]
    proj, *small_w = _matmul("in_proj", h, wt_in, "nt", s, 512, BF16, riders=(small_w, ["gather_chip"] * 3))
    pos_col = positions.reshape(s, 1).astype(F32)
    freqs = ROPE_BASE ** (-jnp.arange(0, RET_DQK, 2, dtype=F32) / RET_DQK)
    inv_freq = jnp.tile(freqs, LANES // (RET_DQK // 2)).reshape(1, LANES)
    log_gamma_np = np.log1p(-(2.0 ** (-5.0 - np.arange(HEADS))))
    log_gamma = jnp.asarray(log_gamma_np, F32)
    lg_lanes = jnp.asarray(np.repeat(log_gamma_np, RET_DQK).reshape(1, RET_QK), F32)
    idx_col = (jnp.arange(s, dtype=F32) - (s // 2)).reshape(s, 1)
    qk_rot, v_bf, qkv_sb, cos_t, sin_t = _prep(proj, pos_col, idx_col, inv_freq, lg_lanes, 2 * tm)
    tq, tk = min(256, s), min(128, s)
    tq_sb = min(SB_TQ, s)
    sb, sb_weights, *big_w = _sb_fwd(qkv_sb, tq_sb, tk, riders=([bf(w_ff2), bf_t(w_ff1)], ["gather_chip"] * 2))
    ret, retg, g_ret, g_sb, g_out, g_ff2, g_ff1 = _ret_fwd(qk_rot, v_bf, proj, ret_gn_g, log_gamma, tq,
                                                           riders=(small_w + big_w, ["forward"] * 5))
    wf_ret = g_ret.reshape(RET_V, d)
    wt_sb = g_sb.reshape(d, SB_W)
    wf_out = g_out.reshape(d, d)
    wt_ff1 = g_ff1.reshape(d_ff, d)
    wf_ff2 = g_ff2.reshape(d_ff, d)
    mixed, r_bf, s_bf, y, hres, h2 = _merge_out(retg, sb, wf_ret, wt_sb, wf_out, proj, x2, mod, post_mix_g, pre_ffn_g, tm)
    u, act = _ff1(h2, wt_ff1, s, 512)
    dout, df, loss_sum, d_gt2, d_gp2 = _ff2_loss(act, wf_ff2, hres, tgt, mod, post_ffn_g, tm)

    du, gw_ff2, gw_ff1 = _ffn_bwd(df, wf_ff2, u, act, h2, d_ff // N_DEV)
    gw_ff2 = gw_ff2.reshape(N_DEV, d_ff // N_DEV, d)
    dhres, dy, d_sh2, d_sc2, d_g2, d_gt1, d_gp1, t_ff1, t_ff2 = _ff1_bwd(
        du, wt_ff1, hres, dout, y, mod, pre_ffn_g, post_mix_g, tm, riders=([gw_ff1, gw_ff2], ["pair"] * 2))
    s_ff1, s_ff2 = _pair_sum("pair_sum_ff", [(gw_ff1, t_ff1), (gw_ff2, t_ff2)], core)
    gw_out = _matmul("grad_w_out", mixed, dy, "tn", 512, d, BF16).reshape(N_DEV, d // N_DEV, d)
    d_r, d_s, da_r, da_s, p_out = _out_bwd(dy, wf_out, proj, r_bf, s_bf, 2 * tm, min(512, d), riders=([gw_out], ["scatter"]))
    dsb, gw_ret, gw_sb, dg_r, dret, d_gn = _branch_bwd(d_r, d_s, retg, sb, wf_ret, wt_sb, ret, proj, ret_gn_g)
    gw_ret = gw_ret.reshape(N_DEV, RET_V // N_DEV, d)
    dq_s, dk_s, dv_s, p_ff1, p_ff2 = _sb_bwd(qkv_sb, sb_weights, dsb, tq_sb, tk,
                                             riders=([s_ff1, s_ff2], ["chip_scatter"] * 2))
    dq_r, dk_r, dv_r, p_sb = _ret_bwd(qk_rot, v_bf, dret, log_gamma, tq, riders=([gw_sb], ["scatter"]))
    dproj, p_ret = _assemble_dproj(dq_r, dk_r, dv_r, dg_r, dq_s, dk_s, dv_s, da_r, da_s, cos_t, sin_t, idx_col, lg_lanes,
                                   tm, riders=([gw_ret], ["scatter"]))
    gw_in = _matmul("grad_w_in", dproj, h, "tn", 512, d, BF16).reshape(N_DEV, d_in // N_DEV, d)
    t_in = _exchange("pair_in", [gw_in], ["pair"])[0]
    s_in = _pair_sum("pair_sum_in", [(gw_in, t_in)], core)[0]
    grad_x, d_sh1, d_sc1, d_g1, p_in = _in_bwd(dproj, wt_in, x2, dhres, mod, pre_mix_g, tm,
                                               riders=([s_in], ["chip_scatter"]))
    loss_lanes = jnp.pad(loss_sum, ((0, 0), (0, LANES - 1)))
    small = jnp.concatenate([d_sh1, d_sc1, d_gt1, d_sh2, d_sc2, d_gt2, d_g1, d_gp1, d_g2, d_gp2, d_gn, loss_lanes], axis=1)
    small_all = _exchange("gather_small", [small], ["gather"])[0].reshape(N_DEV, small.shape[1])
    parts = [p_in, p_ret, p_sb, p_out, p_ff1, p_ff2]

    res = {}
    names = ["w_ret_branch", "w_sb_branch", "w_out", "w_ff1", "w_ff2"]
    ws = [w_ret_branch, w_sb_branch, w_out, w_ff1, w_ff2]
    ms = [m_w_ret_branch, m_w_sb_branch, m_w_out, m_w_ff1, m_w_ff2]
    vs = [v_w_ret_branch, v_w_sb_branch, v_w_out, v_w_ff1, v_w_ff2]
    sets = [(p, w[0], m[0], v[0]) for p, w, m, v in zip(parts[1:], ws, ms, vs)]
    for nm, outs4 in zip(names, _adam_reduce("adam_rest", sets, 2)):
        res[nm] = [o[None] for o in outs4]
    res["w_in"] = [jnp.swapaxes(o, 0, 1)[None]
                   for o in _adam_reduce("adam_w_in", [(parts[0], w_in_t, m_in_t, v_in_t)], 2)[0]]
    dmod_cols = lax.dynamic_slice(small_all, (0, me * n_ada), (N_DEV, n_ada))
    res["ada_w"] = [o[None] for o in _ada_bwd_adam(cs_all.reshape(N_DEV, d, 1), dmod_cols, ada_w[0], m_ada_w[0], v_ada_w[0])]
    vec_names = ["ada_b", "pre_mix_g", "post_mix_g", "pre_ffn_g", "post_ffn_g", "ret_gn_g"]
    vec_res, loss_lanes = _small_adam(small_all,
                                      [ada_b, pre_mix_g, post_mix_g, pre_ffn_g, post_ffn_g, ret_gn_g],
                                      [m_ada_b, m_pre_mix_g, m_post_mix_g, m_pre_ffn_g, m_post_ffn_g, m_ret_gn_g],
                                      [v_ada_b, v_pre_mix_g, v_post_mix_g, v_pre_ffn_g, v_post_ffn_g, v_ret_gn_g])
    res.update(zip(vec_names, vec_res))
    loss = (0.5 / d) * loss_lanes[0, 0]
    order = ["ada_w", "ada_b", "pre_mix_g", "post_mix_g", "pre_ffn_g", "post_ffn_g", "w_in", "ret_gn_g",
             "w_ret_branch", "w_sb_branch", "w_out", "w_ff1", "w_ff2"]
    outs = [loss, grad_x[None]]
    for k in range(4):
        outs += [res[nm][k] for nm in order]
    return tuple(outs)
```

```python
import functools

import numpy as np
import jax
import jax.numpy as jnp
from jax import lax
from jax.experimental import pallas as pl
from jax.experimental.pallas import tpu as pltpu

F32 = jnp.float32
BF16 = jnp.bfloat16
N_DEV = 8
AXES = ("x", "y", "c")

EPS = 1e-6
CHUNK = 64
CHUNK_SHIFT = 6
HEADS = 8
RET_DQK = 64
RET_DV = 128
SB_DH = 64
RET_QK = HEADS * RET_DQK
RET_V = HEADS * RET_DV
SB_W = HEADS * SB_DH
ROPE_BASE = 10000.0
LANES = 128

ADAM_LR = 0.001
ADAM_B1 = 0.9
ADAM_B2 = 0.999
ADAM_EPS = 1e-08
ADAM_WD = 0.01
ADAM_STEP = 10

VMEM_LIMIT = 56 * 1024 * 1024

_NN = (((1,), (0,)), ((), ()))
_NT = (((1,), (1,)), ((), ()))
_TN = (((0,), (0,)), ((), ()))


def _dot(a, b, dims=_NN):
    if a.dtype != BF16:
        a = a.astype(BF16)
    if b.dtype != BF16:
        b = b.astype(BF16)
    return lax.dot_general(a, b, dims, preferred_element_type=F32)


def _sigmoid(x):
    return 1.0 / (1.0 + jnp.exp(-x))


def _rms(x, d):
    r = lax.rsqrt(jnp.sum(x * x, axis=1, keepdims=True) * (1.0 / d) + EPS)
    return x * r, r


def _rms_bwd(dn, n, r, d):
    return r * (dn - n * (jnp.sum(dn * n, axis=1, keepdims=True) * (1.0 / d)))


def _colsum(v):
    return jnp.sum(v, axis=0, keepdims=True)


ROW_SPLIT = 2


def _zero_at_start(refs):
    @pl.when(pl.program_id(0) == 0)
    def _():
        for r in refs:
            r[...] = jnp.zeros_like(r)


def _pieces(tm):
    step = tm // ROW_SPLIT
    return [slice(k * step, (k + 1) * step) for k in range(ROW_SPLIT)]


KIND_SLOTS = {"gather": N_DEV, "scatter": N_DEV, "gather_chip": N_DEV, "forward": N_DEV, "pair": N_DEV // 2,
              "chip_scatter": N_DEV // 2}
SEMS_PER_ARRAY = N_DEV - 1


def _exchange_copies(ins, outs, send_sems, recv_sems, local_sems, kinds):
    x, y, c = (lax.axis_index(a) for a in AXES)
    me, chip, sibling = 4 * x + 2 * y + c, 2 * x + y, (x, y, 1 - c)
    mesh_id = pl.DeviceIdType.MESH
    other_chips = []
    for k in range(1, N_DEV // 2):
        px = 1 - x if k & 2 else x
        py = 1 - y if k & 1 else y
        other_chips.append((px, py))
    copies = []
    for i, kind in enumerate(kinds):
        def remote(src, dst, k, to, i=i):
            return pltpu.make_async_remote_copy(
                src_ref=src, dst_ref=dst, send_sem=send_sems.at[i * SEMS_PER_ARRAY + k],
                recv_sem=recv_sems.at[i * SEMS_PER_ARRAY + k], device_id=to, device_id_type=mesh_id)

        if kind in ("gather", "scatter"):
            pick = (lambda ref, d: ref.at[d]) if kind == "scatter" else (lambda ref, d: ref)
            copies.append(pltpu.make_async_copy(pick(ins[i], me), outs[i].at[me], local_sems.at[i]))
            for k in range(1, N_DEV):
                to = (1 - x if k & 4 else x, 1 - y if k & 2 else y, 1 - c if k & 1 else c)
                copies.append(remote(pick(ins[i], 4 * to[0] + 2 * to[1] + to[2]), outs[i].at[me], k - 1, to))
        elif kind == "gather_chip":
            copies.append(pltpu.make_async_copy(ins[i], outs[i].at[me], local_sems.at[i]))
            copies.append(remote(ins[i], outs[i].at[me], 0, sibling))
            for k, (px, py) in enumerate(other_chips):
                copies.append(remote(ins[i], outs[i].at[me], 1 + k, (px, py, c)))
        elif kind == "forward":
            for k, (px, py) in enumerate(other_chips):
                slot = 4 * px + 2 * py + c
                copies.append(remote(outs[i].at[slot], outs[i].at[slot], k, sibling))
        elif kind == "pair":
            for k in range(N_DEV // 2):
                copies.append(remote(ins[i].at[2 * k + 1 - c], outs[i].at[k], k, sibling))
        elif kind == "chip_scatter":
            copies.append(pltpu.make_async_copy(ins[i].at[chip], outs[i].at[chip], local_sems.at[i]))
            for k, (px, py) in enumerate(other_chips):
                copies.append(remote(ins[i].at[2 * px + py], outs[i].at[chip], k, (px, py, c)))
        else:
            raise ValueError(kind)
    return copies


def _exchange_shapes(arrays, kinds):
    shapes = []
    for a, kind in zip(arrays, kinds):
        tail = a.shape if kind in ("gather", "gather_chip") else a.shape[1:]
        shapes.append(jax.ShapeDtypeStruct((KIND_SLOTS[kind],) + tuple(tail), a.dtype))
    return shapes


def _exchange_sems(n):
    return [pltpu.SemaphoreType.DMA((n * SEMS_PER_ARRAY,)), pltpu.SemaphoreType.DMA((n * SEMS_PER_ARRAY,)),
            pltpu.SemaphoreType.DMA((n,))]


def _call(name, body, grid, ins, outs, scratch=(), riders=None, prefetch=None):
    any_spec = pl.BlockSpec(memory_space=pl.ANY)
    in_specs = [pl.BlockSpec(memory_space=im) if bs is None else pl.BlockSpec(bs, im) for _, bs, im in ins]
    out_specs = [pl.BlockSpec(bs, im) for _, _, bs, im in outs]
    out_shape = [jax.ShapeDtypeStruct(s, d) for s, d, _, _ in outs]
    operands = [a for a, _, _ in ins]
    scratch = list(scratch)
    aliases = {}
    n_pre = 0 if prefetch is None else 1
    kernel = functools.partial(body) if prefetch is None else (lambda _, *refs: body(*refs))
    if riders is not None:
        arrays, kinds = riders
        nr, n_in, n_out, n_scr = len(arrays), len(ins), len(outs), len(scratch)

        def kernel(*refs):
            refs = refs[n_pre:]
            own_in, ride_in = refs[:n_in], refs[n_in:n_in + nr]
            own_out = refs[n_in + nr:n_in + nr + n_out]
            ride_out = refs[n_in + nr + n_out:n_in + 2 * nr + n_out]
            own_scr = refs[n_in + 2 * nr + n_out:n_in + 2 * nr + n_out + n_scr]
            sems = refs[n_in + 2 * nr + n_out + n_scr:]
            ids = [pl.program_id(a) for a in range(len(grid))]
            first = functools.reduce(jnp.logical_and, [i == 0 for i in ids])
            last = functools.reduce(jnp.logical_and, [i == g - 1 for i, g in zip(ids, grid)])

            @pl.when(first)
            def _():
                for cp in _exchange_copies(ride_in, ride_out, *sems, kinds):
                    cp.start()

            body(*own_in, *own_out, *own_scr)

            @pl.when(last)
            def _():
                for cp in _exchange_copies(ride_in, ride_out, *sems, kinds):
                    cp.wait()

        in_specs += [any_spec] * nr
        out_specs += [any_spec] * nr
        out_shape += _exchange_shapes(arrays, kinds)
        operands += list(arrays)
        scratch += _exchange_sems(nr)
        aliases = {n_pre + n_in + r: n_out + r for r, kind in enumerate(kinds) if kind == "forward"}
    params = pltpu.CompilerParams(dimension_semantics=("arbitrary",) * len(grid), vmem_limit_bytes=VMEM_LIMIT)
    if prefetch is None:
        return pl.pallas_call(kernel, name=name, grid=grid, in_specs=in_specs, out_specs=out_specs,
                              out_shape=out_shape, scratch_shapes=scratch, input_output_aliases=aliases,
                              compiler_params=params)(*operands)
    grid_spec = pltpu.PrefetchScalarGridSpec(num_scalar_prefetch=1, grid=grid, in_specs=in_specs,
                                             out_specs=out_specs, scratch_shapes=scratch)
    return pl.pallas_call(kernel, name=name, grid_spec=grid_spec, out_shape=out_shape,
                          input_output_aliases=aliases, compiler_params=params)(prefetch, *operands)


def _exchange(name, arrays, kinds):
    n = len(arrays)

    def body(*refs):
        copies = _exchange_copies(refs[:n], refs[n:2 * n], *refs[2 * n:], kinds)
        for cp in copies:
            cp.start()
        for cp in copies:
            cp.wait()

    any_spec = pl.BlockSpec(memory_space=pl.ANY)
    return pl.pallas_call(
        functools.partial(body),
        name=name,
        in_specs=[any_spec] * n,
        out_specs=[any_spec] * n,
        out_shape=_exchange_shapes(arrays, kinds),
        scratch_shapes=_exchange_sems(n),
        input_output_aliases={i: i for i, kind in enumerate(kinds) if kind == "forward"},
    )(*arrays)


def _pair_sum(name, pairs, my_core):
    n = len(pairs)

    def body(*refs):
        for k in range(n):
            a_ref, b_ref, o_ref = refs[2 * k], refs[2 * k + 1], refs[2 * n + k]
            o_ref[...] = (a_ref[...].astype(F32) + b_ref[...].astype(F32)).astype(o_ref.dtype)

    ins, outs = [], []
    for mine, theirs in pairs:
        _, rws, cls = mine.shape
        ins += [(mine, (None, rws, cls), lambda k, core: (2 * k + core[0], 0, 0)),
                (theirs, (None, rws, cls), lambda k, core: (k, 0, 0))]
        outs.append(((N_DEV // 2, rws, cls), mine.dtype, (None, rws, cls), lambda k, core: (k, 0, 0)))
    return _call(name, body, (N_DEV // 2,), ins, outs, prefetch=my_core)


def _matmul(name, a, b, kind, tm, tn, out_dtype, blocked_out=False, riders=None):
    if kind == "tn":
        kdim, m = a.shape
    else:
        m, kdim = a.shape
    n = b.shape[0] if kind == "nt" else b.shape[1]
    tm, tn = min(tm, m), min(tn, n)
    dims = {"nn": _NN, "nt": _NT, "tn": _TN}[kind]

    def body(a_ref, b_ref, o_ref):
        o_ref[...] = _dot(a_ref[...], b_ref[...], dims).astype(o_ref.dtype)

    a_spec = (a, (kdim, tm), lambda j, i: (0, i)) if kind == "tn" else (a, (tm, kdim), lambda j, i: (i, 0))
    b_spec = (b, (tn, kdim), lambda j, i: (j, 0)) if kind == "nt" else (b, (kdim, tn), lambda j, i: (0, j))
    if blocked_out:
        out = ((n // tn, m, tn), out_dtype, (None, tm, tn), lambda j, i: (j, i, 0))
    else:
        out = ((m, n), out_dtype, (tm, tn), lambda j, i: (i, j))
    res = _call(name, body, (n // tn, m // tm), [a_spec, b_spec], [out], riders=riders)
    return res[0] if riders is None else res


def _ada_fwd(cs_all, ada_w, ada_b_cols):
    def body(c_ref, w_ref, b_ref, o_ref):
        o_ref[...] = lax.dot_general(c_ref[...], w_ref[...], _NN, preferred_element_type=F32,
                                     precision=lax.Precision.HIGHEST) + b_ref[...]

    r, d = cs_all.shape
    nc = ada_w.shape[1]
    return _call("ada_fwd", body, (1,),
                 [(cs_all, (r, d), lambda i: (0, 0)), (ada_w, (d, nc), lambda i: (0, 0)),
                  (ada_b_cols, (1, nc), lambda i: (0, 0))],
                 [((r, nc), F32, (r, nc), lambda i: (0, 0))])[0]


def _silu_rows(c_all):
    def body(c_ref, o_ref):
        v = c_ref[...]
        o_ref[...] = v * _sigmoid(v)

    return _call("silu_c", body, (1,), [(c_all, c_all.shape, lambda i: (0, 0))],
                 [(c_all.shape, F32, c_all.shape, lambda i: (0, 0))])[0]


def _pre_norm(x, g, mod, tm, riders=None):
    s, d = x.shape

    def body(x_ref, g_ref, mod_ref, h_ref):
        n, _ = _rms(x_ref[...], d)
        sh, sc = mod_ref[:, 0:d], mod_ref[:, d:2 * d]
        h_ref[...] = (n * g_ref[...] * (1.0 + sc) + sh).astype(BF16)

    return _call("pre_norm", body, (s // tm,),
                 [(x, (tm, d), lambda i: (i, 0)), (g, (1, d), lambda i: (0, 0)),
                  (mod, (1, 6 * d), lambda i: (0, 0))],
                 [((s, d), BF16, (tm, d), lambda i: (i, 0))], riders=riders)


LOG2E = 1.4426950408889634
LN2 = 0.6931471805599453


def _decay_scale(lg_ref, idx, g, sign):
    return jnp.exp((sign * idx) * lg_ref[:, g * LANES:(g + 1) * LANES])


def _prep(proj, pos_col, idx_col, inv_freq, lg_lanes, tm):
    s = proj.shape[0]
    sb_off = (2 * RET_QK + 2 * RET_V) // (3 * SB_W)
    n_q = RET_QK // LANES

    def body(qk_ref, v_ref, sb_ref, pos_ref, idx_ref, f_ref, lg_ref, qk_out, v_out, sb_out, cos_out, sin_out):
        ang = pos_ref[...] * f_ref[...]
        lane = lax.broadcasted_iota(jnp.int32, (1, LANES), 1)
        first = jnp.bitwise_and(lane, RET_DQK - 1) < (RET_DQK // 2)
        cos = jnp.cos(ang)
        sin = jnp.where(first, -1.0, 1.0) * jnp.sin(ang)
        cos_out[...] = cos
        sin_out[...] = sin
        idx = idx_ref[...]
        for g in range(2 * n_q):
            v = qk_ref[:, g * LANES:(g + 1) * LANES].astype(F32)
            sw = jnp.where(first, pltpu.roll(v, LANES - RET_DQK // 2, 1), pltpu.roll(v, RET_DQK // 2, 1))
            r = v * cos + sw * sin
            if g < n_q:
                r = r * _decay_scale(lg_ref, idx, g, 1.0)
            else:
                r = r * (_decay_scale(lg_ref, idx, g - n_q, -1.0) * (RET_DQK ** -0.5))
            qk_out[:, g * LANES:(g + 1) * LANES] = r.astype(BF16)
        v_out[...] = v_ref[...].astype(BF16)
        sb_out[:, 0:SB_W] = (sb_ref[:, 0:SB_W].astype(F32) * (SB_DH ** -0.5 * LOG2E)).astype(BF16)
        sb_out[:, SB_W:3 * SB_W] = sb_ref[:, SB_W:3 * SB_W].astype(BF16)

    return _call("prep", body, (s // tm,),
                 [(proj, (tm, 2 * RET_QK), lambda i: (i, 0)),
                  (proj, (tm, RET_V), lambda i: (i, 2 * RET_QK // RET_V)),
                  (proj, (tm, 3 * SB_W), lambda i: (i, sb_off)),
                  (pos_col, (tm, 1), lambda i: (i, 0)),
                  (idx_col, (tm, 1), lambda i: (i, 0)),
                  (inv_freq, (1, LANES), lambda i: (0, 0)),
                  (lg_lanes, (1, RET_QK), lambda i: (0, 0))],
                 [((s, 2 * RET_QK), BF16, (tm, 2 * RET_QK), lambda i: (i, 0)),
                  ((s, RET_V), BF16, (tm, RET_V), lambda i: (i, 0)),
                  ((s, 3 * SB_W), BF16, (tm, 3 * SB_W), lambda i: (i, 0)),
                  ((s, LANES), F32, (tm, LANES), lambda i: (i, 0)),
                  ((s, LANES), F32, (tm, LANES), lambda i: (i, 0))])


def _head_mask(hh):
    lane = lax.broadcasted_iota(jnp.int32, (1, LANES), 1)
    return (lane >= RET_DQK) if hh else (lane < RET_DQK)


def _masked(v, m):
    return jnp.where(m, v, jnp.zeros_like(v))


SB_GROUP = 4
SB_TQ = 256


def _stack_heads(v):
    return jnp.concatenate([_masked(v, _head_mask(0)), _masked(v, _head_mask(1))], axis=0)


def _side_by_side(v, t):
    return jnp.concatenate([v[:t], v[t:]], axis=1)


def _tile_pos(i, j, tq, tk):
    row = jnp.bitwise_and(lax.broadcasted_iota(jnp.int32, (2 * tq, tk), 0), tq - 1) + i * tq
    col = lax.broadcasted_iota(jnp.int32, (2 * tq, tk), 1) + j * tk
    return row, col


def _n_groups(i, tq, tk, grp):
    return ((i + 1) * (tq // tk) + grp - 1) // grp


def _n_full(i, tq, tk, grp):
    return (i * (tq // tk)) // grp


def _key_rows(j, tk):
    return pl.ds(pl.multiple_of(j * tk, tk), tk)


def _ret_weight(lg_rows, i, j, tq, tk):
    row, col = _tile_pos(i, j, tq, tk)
    same = jnp.right_shift(col, CHUNK_SHIFT) == jnp.right_shift(row, CHUNK_SHIFT)
    later = jnp.where(same, jnp.exp((2.0 * lg_rows) * (col - row).astype(F32)), 0.0)
    return jnp.where(col <= row, 1.0, later)


def _lg_rows(lg_ref, hp, tq):
    first = lax.broadcasted_iota(jnp.int32, (2 * tq, 1), 0) < tq
    return jnp.where(first, lg_ref[2 * hp], lg_ref[2 * hp + 1])


def _check_tiles(s, tq, tk, grp):
    assert tq % tk == 0 and tq & (tq - 1) == 0 and tk & (tk - 1) == 0
    assert s % tq == 0 and (s // tk) % grp == 0 and s // tk <= LANES


def _pair_mask():
    r = lax.broadcasted_iota(jnp.int32, (LANES, 2 * RET_DV), 0) >= RET_DQK
    c = lax.broadcasted_iota(jnp.int32, (LANES, 2 * RET_DV), 1) >= RET_DV
    return (r == c).astype(F32)


def _ret_block(lg_ref, hp, i, t, qb, kb):
    w = _ret_weight(_lg_rows(lg_ref, hp, t), i, i, t, t)
    return _dot(_stack_heads(qb), kb, _NT), w


RET_PAIRS = 2


def _lanes(ref, p, width):
    return ref[:, p * width:(p + 1) * width]


def _ret_fwd(qk_rot, v_bf, proj, gn_g, log_gamma, t, riders=None):
    s = qk_rot.shape[0]
    n_pair = HEADS // 2
    pw = 2 * RET_DV
    wq, wv = RET_PAIRS * LANES, RET_PAIRS * pw
    gate_off = (2 * RET_QK + RET_V) // wv
    assert s % t == 0 and t % CHUNK == 0 and t & (t - 1) == 0 and n_pair % RET_PAIRS == 0

    def body(lg_ref, q_ref, k_ref, v_ref, g_ref, w_ref, ret_ref, rg_ref, state_ref):
        hg, i = pl.program_id(0), pl.program_id(1)

        @pl.when(i == 0)
        def _():
            state_ref[...] = jnp.zeros_like(state_ref)

        pairs = range(RET_PAIRS)
        qbs = [_lanes(q_ref, p, LANES) for p in pairs]
        kbs = [_lanes(k_ref, p, LANES) for p in pairs]
        vbs = [_lanes(v_ref, p, pw) for p in pairs]
        zws = [_ret_block(lg_ref, hg * RET_PAIRS + p, i, t, qbs[p], kbs[p]) for p in pairs]
        ps = [(z * w).astype(BF16) for z, w in zws]
        outs = [jnp.concatenate([_dot(ps[p][:t], vbs[p][:, 0:RET_DV]), _dot(ps[p][t:], vbs[p][:, RET_DV:pw])], axis=1)
                + _dot(qbs[p], state_ref[p]) for p in pairs]
        for p in pairs:
            state_ref[p] += _pair_mask() * _dot(kbs[p], vbs[p], _TN)
        for p in pairs:
            for hh in range(2):
                cols = slice(p * pw + hh * RET_DV, p * pw + (hh + 1) * RET_DV)
                o = outs[p][:, hh * RET_DV:(hh + 1) * RET_DV]
                ret_ref[:, cols] = o
                mu = jnp.sum(o, axis=1, keepdims=True) * (1.0 / RET_DV)
                xc = o - mu
                var = jnp.sum(xc * xc, axis=1, keepdims=True) * (1.0 / RET_DV)
                nrm = xc * lax.rsqrt(var + EPS) * w_ref[:, cols]
                g = g_ref[:, cols].astype(F32)
                rg_ref[:, cols] = (g * _sigmoid(g) * nrm).astype(BF16)

    blk = lambda hg, i: (i, hg)
    return _call("ret_fwd", body, (n_pair // RET_PAIRS, s // t),
                 [(log_gamma, None, pltpu.SMEM),
                  (qk_rot, (t, wq), blk),
                  (qk_rot, (t, wq), lambda hg, i: (i, n_pair // RET_PAIRS + hg)),
                  (v_bf, (t, wv), blk),
                  (proj, (t, wv), lambda hg, i: (i, gate_off + hg)),
                  (gn_g, (1, wv), lambda hg, i: (0, hg))],
                 [((s, RET_V), F32, (t, wv), blk), ((s, RET_V), BF16, (t, wv), blk)],
                 scratch=[pltpu.VMEM((RET_PAIRS, LANES, pw), F32)], riders=riders)


def _tri(tk, strict_upper):
    r = lax.broadcasted_iota(jnp.int32, (tk, tk), 0)
    cc = lax.broadcasted_iota(jnp.int32, (tk, tk), 1)
    return ((r > cc) if strict_upper else (r < cc)).astype(BF16)


def _diagonal_step(i, tq, tk, make, carry):
    if (tq // tk) % SB_GROUP == 0:
        return make(SB_GROUP)(0, carry)
    assert 2 * (tq // tk) == SB_GROUP
    half = lax.rem(i, 2) == 0
    return lax.cond(half, lambda cr: make(SB_GROUP // 2)(0, cr), lambda cr: make(SB_GROUP)(0, cr), carry)


def _sb_valid(i, j, tq, tk):
    row, col = _tile_pos(i, j, tq, tk)
    return col < row


def _sb_fwd(qkv, tq, tk, riders=None):
    s = qkv.shape[0]
    n_pair = HEADS // 2
    _check_tiles(s, tq, tk, SB_GROUP)

    def body(q_ref, k_ref, v_ref, o_ref, a_ref):
        i = pl.program_id(1)
        upper = _tri(tk, True)
        qs = _stack_heads(q_ref[...])
        n_full, n_groups = _n_full(i, tq, tk, SB_GROUP), _n_groups(i, tq, tk, SB_GROUP)

        def make_step(near_diagonal, last, n_sub=SB_GROUP):
            def step(n, carry):
                c, o = carry
                g = last - 1 - n
                js = [g * SB_GROUP + sub for sub in range(n_sub)]
                zs = [_dot(qs, k_ref[_key_rows(j, tk), :], _NT) for j in js]
                log1ps = [jnp.log2(1.0 + jnp.exp2(-jnp.abs(z))) for z in zs]
                log_1ms = [-jnp.maximum(z, 0.0) - t for z, t in zip(zs, log1ps)]
                log_bs = [jnp.minimum(z, 0.0) - t for z, t in zip(zs, log1ps)]
                if near_diagonal:
                    valids = [_sb_valid(i, j, tq, tk) for j in js]
                    log_1ms = [jnp.where(v, l, 0.0) for v, l in zip(valids, log_1ms)]
                sticks = [_dot(l, upper) for l in log_1ms]
                sums = [jnp.sum(l, axis=1, keepdims=True) for l in log_1ms]
                cs = [None] * n_sub
                for sub in reversed(range(n_sub)):
                    cs[sub] = c
                    c = c + sums[sub]
                for sub, j in enumerate(js):
                    a = jnp.exp2(log_bs[sub] + sticks[sub] + cs[sub])
                    if near_diagonal:
                        a = jnp.where(valids[sub], a, 0.0)
                    a = a.astype(BF16)
                    a_ref[j] = a
                    o = o + _dot(_side_by_side(a, tq), _stack_heads(v_ref[_key_rows(j, tk), :]))
                return c, o
            return step

        carry = (jnp.zeros((2 * tq, 1), F32), jnp.zeros((tq, LANES), F32))
        carry = _diagonal_step(i, tq, tk, lambda n_sub: make_step(True, n_groups, n_sub), carry)
        _, acc = lax.fori_loop(0, n_full, make_step(False, n_full), carry)
        o_ref[...] = acc.astype(BF16)

    n_kb = s // tk
    return _call("sb_fwd", body, (n_pair, s // tq),
                 [(qkv, (tq, LANES), lambda hp, i: (i, hp)),
                  (qkv, (s, LANES), lambda hp, i: (0, n_pair + hp)),
                  (qkv, (s, LANES), lambda hp, i: (0, 2 * n_pair + hp))],
                 [((s, SB_W), BF16, (tq, LANES), lambda hp, i: (i, hp)),
                  ((n_pair, s // tq, n_kb, 2 * tq, tk), BF16, (None, None, n_kb, 2 * tq, tk),
                   lambda hp, i: (hp, i, 0, 0, 0))], riders=riders)


def _merge_out(retg, sb, w_ret, w_sb_t, w_out, proj, x, mod, gp1, g2, tm):
    s, d = x.shape
    gw = min(512, d)
    n_g = d // gw
    ar_off = (2 * RET_QK + 2 * RET_V + 3 * SB_W) // gw

    def body(rg_ref, sb_ref, wr_ref, ws_ref, wo_ref, *refs):
        gate_refs, (x_ref, mod_ref, gp_ref, g2_ref, mix_ref, r_ref, s_ref, y_ref, hres_ref, h2_ref) = refs[:2 * n_g], refs[2 * n_g:]
        for rows in _pieces(tm):
            rr = _dot(rg_ref[rows, :], wr_ref[...])
            ss = _dot(sb_ref[rows, :], ws_ref[...], _NT)
            a_r = jnp.concatenate([g[rows, :] for g in gate_refs[:n_g]], axis=1).astype(F32)
            a_s = jnp.concatenate([g[rows, :] for g in gate_refs[n_g:]], axis=1).astype(F32)
            mixed = (_sigmoid(a_r) * rr + _sigmoid(a_s) * ss).astype(BF16)
            mix_ref[rows, :] = mixed
            r_ref[rows, :] = rr.astype(BF16)
            s_ref[rows, :] = ss.astype(BF16)
            y = _dot(mixed, wo_ref[...])
            y_ref[rows, :] = y
            ny, _ = _rms(y, d)
            hres = x_ref[rows, :] + mod_ref[:, 2 * d:3 * d] * (ny * gp_ref[...])
            hres_ref[rows, :] = hres
            n2, _ = _rms(hres, d)
            h2_ref[rows, :] = (n2 * g2_ref[...] * (1.0 + mod_ref[:, 4 * d:5 * d]) + mod_ref[:, 3 * d:4 * d]).astype(BF16)

    row = lambda i: (i, 0)
    fix = lambda i: (0, 0)
    tile_bf = ((s, d), BF16, (tm, d), row)
    tile_f = ((s, d), F32, (tm, d), row)
    return _call("merge_out", body, (s // tm,),
                 [(retg, (tm, RET_V), row), (sb, (tm, SB_W), row), (w_ret, (RET_V, d), fix), (w_sb_t, (d, SB_W), fix),
                  (w_out, (d, d), fix)]
                 + [(proj, (tm, gw), functools.partial(lambda i, k: (i, ar_off + k), k=k)) for k in range(2 * n_g)]
                 + [(x, (tm, d), row), (mod, (1, 6 * d), fix), (gp1, (1, d), fix), (g2, (1, d), fix)],
                 [tile_bf, tile_bf, tile_bf, tile_f, tile_f, tile_bf])


def _ff1(h2, w_ff1_t, tm, tn):
    s, f = h2.shape[0], w_ff1_t.shape[0]
    tm = min(tm, s)

    def body(a_ref, w_ref, u_ref, act_ref):
        u = _dot(a_ref[...], w_ref[...], _NT)
        r = jnp.maximum(u, 0.0)
        u_ref[...] = u.astype(BF16)
        act_ref[...] = (r * r).astype(BF16)

    d = h2.shape[1]
    return _call("ff1", body, (f // tn, s // tm),
                 [(h2, (tm, d), lambda j, i: (i, 0)), (w_ff1_t, (tn, d), lambda j, i: (j, 0))],
                 [((s, f), BF16, (tm, tn), lambda j, i: (i, j))] * 2)


def _ff2_loss(act, w_ff2, hres, target, mod, gp2, tm):
    s, d = hres.shape
    f = act.shape[1]

    def body(a_ref, w_ref, h_ref, t_ref, mod_ref, gp_ref, dout_ref, df_ref, loss_ref, dgt_ref, dgp_ref):
        _zero_at_start([loss_ref, dgt_ref, dgp_ref])
        gt, gp = mod_ref[:, 5 * d:6 * d], gp_ref[...]
        for rows in _pieces(tm):
            ff = _dot(a_ref[rows, :], w_ref[...])
            nf, rf = _rms(ff, d)
            out = h_ref[rows, :] + gt * (nf * gp)
            err = out - t_ref[rows, :]
            sq = jnp.sum(err * err, axis=1, keepdims=True)
            loss_ref[...] += jnp.sum(sq, axis=0, keepdims=True)
            dout = err * (1.0 / d)
            dout_ref[rows, :] = dout
            dgt_ref[...] += _colsum(dout * (nf * gp))
            dgp_ref[...] += _colsum(dout * gt * nf)
            df_ref[rows, :] = _rms_bwd(dout * gt * gp, nf, rf, d).astype(BF16)

    row = lambda i: (i, 0)
    fix = lambda i: (0, 0)
    return _call("ff2_loss", body, (s // tm,),
                 [(act, (tm, f), row), (w_ff2, (f, d), fix), (hres, (tm, d), row), (target, (tm, d), row),
                  (mod, (1, 6 * d), fix), (gp2, (1, d), fix)],
                 [((s, d), F32, (tm, d), row), ((s, d), BF16, (tm, d), row), ((1, 1), F32, (1, 1), fix),
                  ((1, d), F32, (1, d), fix), ((1, d), F32, (1, d), fix)])


def _ffn_bwd(df, w_ff2, u, act, h2, tn):
    s, d = df.shape
    f = w_ff2.shape[0]

    def body(df_ref, w_ref, u_ref, act_ref, h2_ref, du_ref, gw2_ref, gw1_ref):
        dfb = df_ref[...]
        du = (_dot(dfb, w_ref[...], _NT) * (2.0 * jnp.maximum(u_ref[...].astype(F32), 0.0))).astype(BF16)
        du_ref[...] = du
        gw2_ref[...] = _dot(act_ref[...], dfb, _TN).astype(BF16)
        gw1_ref[...] = _dot(h2_ref[...], du, _TN).astype(BF16)

    fix = lambda j: (0, 0)
    col = lambda j: (0, j)
    return _call("ffn_bwd", body, (f // tn,),
                 [(df, (s, d), fix), (w_ff2, (tn, d), lambda j: (j, 0)), (u, (s, tn), col), (act, (s, tn), col),
                  (h2, (s, d), fix)],
                 [((s, f), BF16, (s, tn), col), ((f, d), BF16, (tn, d), lambda j: (j, 0)),
                  ((f // tn, d, tn), BF16, (None, d, tn), lambda j: (j, 0, 0))])


def _ff1_bwd(du, w_ff1_t, hres, dout, y, mod, g2, gp1, tm, riders=None):
    s, d = hres.shape
    f = du.shape[1]

    def body(a_ref, w_ref, h_ref, do_ref, y_ref, mod_ref, g2_ref, gp_ref,
             dh_ref, dy_ref, dsh_ref, dsc_ref, dg2_ref, dgt_ref, dgp_ref):
        _zero_at_start([dsh_ref, dsc_ref, dg2_ref, dgt_ref, dgp_ref])
        g2, sc2 = g2_ref[...], mod_ref[:, 4 * d:5 * d]
        gt, gp = mod_ref[:, 2 * d:3 * d], gp_ref[...]
        for rows in _pieces(tm):
            dh2 = _dot(a_ref[rows, :], w_ref[...])
            n2, r2 = _rms(h_ref[rows, :], d)
            dsh_ref[...] += _colsum(dh2)
            dsc_ref[...] += _colsum(dh2 * n2 * g2)
            dg2_ref[...] += _colsum(dh2 * n2 * (1.0 + sc2))
            dhres = do_ref[rows, :] + _rms_bwd(dh2 * g2 * (1.0 + sc2), n2, r2, d)
            dh_ref[rows, :] = dhres
            ny, ry = _rms(y_ref[rows, :], d)
            dgt_ref[...] += _colsum(dhres * (ny * gp))
            dgp_ref[...] += _colsum(dhres * gt * ny)
            dy_ref[rows, :] = _rms_bwd(dhres * gt * gp, ny, ry, d).astype(BF16)

    row = lambda i: (i, 0)
    fix = lambda i: (0, 0)
    vec = ((1, d), F32, (1, d), fix)
    return _call("ff1_bwd", body, (s // tm,),
                 [(du, (tm, f), row), (w_ff1_t, (f, d), fix), (hres, (tm, d), row), (dout, (tm, d), row),
                  (y, (tm, d), row), (mod, (1, 6 * d), fix), (g2, (1, d), fix), (gp1, (1, d), fix)],
                 [((s, d), F32, (tm, d), row), ((s, d), BF16, (tm, d), row), vec, vec, vec, vec, vec], riders=riders)


def _out_bwd(dy, w_out, proj, r_bf, s_bf, tm, tn, riders=None):
    s, d = dy.shape
    ar_off = (2 * RET_QK + 2 * RET_V + 3 * SB_W) // tn
    as_off = ar_off + d // tn

    def body(a_ref, w_ref, ar_ref, as_ref, r_ref, s_ref, dr_ref, ds_ref, dar_ref, das_ref):
        dm = _dot(a_ref[...], w_ref[...], _NT)
        sr, ss = _sigmoid(ar_ref[...].astype(F32)), _sigmoid(as_ref[...].astype(F32))
        dr_ref[...] = (dm * sr).astype(BF16)
        ds_ref[...] = (dm * ss).astype(BF16)
        dar_ref[...] = (dm * r_ref[...].astype(F32) * sr * (1.0 - sr)).astype(BF16)
        das_ref[...] = (dm * s_ref[...].astype(F32) * ss * (1.0 - ss)).astype(BF16)

    tile = (tm, tn)
    here = lambda j, i: (i, j)
    return _call("out_bwd", body, (d // tn, s // tm),
                 [(dy, (tm, d), lambda j, i: (i, 0)), (w_out, (tn, d), lambda j, i: (j, 0)),
                  (proj, tile, lambda j, i: (i, ar_off + j)), (proj, tile, lambda j, i: (i, as_off + j)),
                  (r_bf, tile, here), (s_bf, tile, here)],
                 [((s, d), BF16, tile, here)] * 4, riders=riders)


def _branch_bwd(d_r, d_s, retg, sb, w_ret, w_sb_t, ret, proj, gn_g):
    s, d = d_r.shape
    n_step = 4
    part_v, part_s, part_d = RET_V // n_step, SB_W // n_step, d // n_step
    per_dev = d // N_DEV
    n_blk = part_d // per_dev
    gate_off = (2 * RET_QK + RET_V) // part_v

    def body(dr_ref, ds_ref, rg_ref, sb_ref, wr_ref, ws_ref, r_ref, g_ref, w_ref,
             dsb_ref, gwr_ref, gws_ref, dg_ref, dret_ref, dw_ref):
        i = pl.program_id(0)
        dr, ds = dr_ref[...], ds_ref[...]
        dsb_ref[...] = _dot(ds, ws_ref[...]).astype(BF16)
        gwr_ref[...] = _dot(rg_ref[...], dr, _TN).astype(BF16)
        cols = pl.ds(pl.multiple_of(i * part_d, part_d), part_d)
        gws = _dot(sb_ref[...], ds_ref[:, cols], _TN).astype(BF16)
        for k in range(n_blk):
            gws_ref[k] = gws[:, k * per_dev:(k + 1) * per_dev]
        dretg = _dot(dr, wr_ref[...], _NT)
        for h in range(part_v // RET_DV):
            cols = slice(h * RET_DV, (h + 1) * RET_DV)
            o, g, w, d_o = r_ref[:, cols], g_ref[:, cols].astype(F32), w_ref[:, cols], dretg[:, cols]
            mu = jnp.sum(o, axis=1, keepdims=True) * (1.0 / RET_DV)
            xc = o - mu
            rstd = lax.rsqrt(jnp.sum(xc * xc, axis=1, keepdims=True) * (1.0 / RET_DV) + EPS)
            n = xc * rstd
            sg = _sigmoid(g)
            silu = g * sg
            dg_ref[:, cols] = (d_o * n * w * (sg * (1.0 + g * (1.0 - sg)))).astype(BF16)
            dw_ref[:, cols] = _colsum(d_o * silu * n)
            dn = d_o * silu * w
            m1 = jnp.sum(dn, axis=1, keepdims=True) * (1.0 / RET_DV)
            m2 = jnp.sum(dn * n, axis=1, keepdims=True) * (1.0 / RET_DV)
            dret_ref[:, cols] = (rstd * (dn - m1 - n * m2)).astype(BF16)

    fix = lambda i: (0, 0)
    col = lambda i: (0, i)
    return _call("branch_bwd", body, (n_step,),
                 [(d_r, (s, d), fix), (d_s, (s, d), fix), (retg, (s, part_v), col), (sb, (s, SB_W), fix),
                  (w_ret, (part_v, d), lambda i: (i, 0)), (w_sb_t, (d, part_s), col),
                  (ret, (s, part_v), col), (proj, (s, part_v), lambda i: (0, gate_off + i)), (gn_g, (1, part_v), col)],
                 [((s, SB_W), BF16, (s, part_s), col), ((RET_V, d), BF16, (part_v, d), lambda i: (i, 0)),
                  ((N_DEV, SB_W, per_dev), BF16, (n_blk, SB_W, per_dev), lambda i: (i, 0, 0)),
                  ((s, RET_V), BF16, (s, part_v), col), ((s, RET_V), BF16, (s, part_v), col),
                  ((1, RET_V), F32, (1, part_v), col)])


def _ret_bwd(qk_rot, v_bf, dret, log_gamma, t, riders=None):
    s = qk_rot.shape[0]
    n_pair = HEADS // 2
    pw = 2 * RET_DV
    wq, wv = RET_PAIRS * LANES, RET_PAIRS * pw
    n_blk = s // t
    pairs = range(RET_PAIRS)

    def load(q_ref, k_ref, v_ref, do_ref):
        return ([_lanes(q_ref, p, LANES) for p in pairs], [_lanes(k_ref, p, LANES) for p in pairs],
                [_lanes(v_ref, p, pw) for p in pairs], [_lanes(do_ref, p, pw) for p in pairs])

    def d_scores(lg_ref, hp, i, qb, kb, vb, dob):
        z, w = _ret_block(lg_ref, hp, i, t, qb, kb)
        dp = jnp.concatenate([_dot(dob[:, 0:RET_DV], vb[:, 0:RET_DV], _NT),
                              _dot(dob[:, RET_DV:pw], vb[:, RET_DV:pw], _NT)], axis=0)
        return (z * w).astype(BF16), (dp * w).astype(BF16)

    def up_body(lg_ref, q_ref, k_ref, v_ref, do_ref, dq_ref, state_ref):
        hg, i = pl.program_id(0), pl.program_id(1)

        @pl.when(i == 0)
        def _():
            state_ref[...] = jnp.zeros_like(state_ref)

        qbs, kbs, vbs, dobs = load(q_ref, k_ref, v_ref, do_ref)
        dss = [d_scores(lg_ref, hg * RET_PAIRS + p, i, qbs[p], kbs[p], vbs[p], dobs[p])[1] for p in pairs]
        for p in pairs:
            dq_ref[:, p * LANES:(p + 1) * LANES] = (_dot(_side_by_side(dss[p], t), _stack_heads(kbs[p]))
                                                    + _dot(dobs[p], state_ref[p], _NT)).astype(BF16)
        for p in pairs:
            state_ref[p] += _pair_mask() * _dot(kbs[p], vbs[p], _TN)

    def down_body(lg_ref, q_ref, k_ref, v_ref, do_ref, dk_ref, dv_ref, state_ref):
        hg, i = pl.program_id(0), n_blk - 1 - pl.program_id(1)

        @pl.when(pl.program_id(1) == 0)
        def _():
            state_ref[...] = jnp.zeros_like(state_ref)

        qbs, kbs, vbs, dobs = load(q_ref, k_ref, v_ref, do_ref)
        both = [d_scores(lg_ref, hg * RET_PAIRS + p, i, qbs[p], kbs[p], vbs[p], dobs[p]) for p in pairs]
        for p in pairs:
            pp, ds = both[p]
            later = state_ref[p]
            dv_ref[:, p * pw:(p + 1) * pw] = (jnp.concatenate(
                [_dot(pp[:t], dobs[p][:, 0:RET_DV], _TN), _dot(pp[t:], dobs[p][:, RET_DV:pw], _TN)],
                axis=1) + _dot(kbs[p], later)).astype(BF16)
            dk_ref[:, p * LANES:(p + 1) * LANES] = (_dot(ds, _stack_heads(qbs[p]), _TN)
                                                    + _dot(vbs[p], later, _NT)).astype(BF16)
        for p in pairs:
            state_ref[p] += _pair_mask() * _dot(qbs[p], dobs[p], _TN)

    n_grp = n_pair // RET_PAIRS

    def ins(order):
        return [(log_gamma, None, pltpu.SMEM),
                (qk_rot, (t, wq), lambda hg, i: (order(i), hg)),
                (qk_rot, (t, wq), lambda hg, i: (order(i), n_grp + hg)),
                (v_bf, (t, wv), lambda hg, i: (order(i), hg)),
                (dret, (t, wv), lambda hg, i: (order(i), hg))]

    up = lambda i: i
    down = lambda i: n_blk - 1 - i
    scratch = [pltpu.VMEM((RET_PAIRS, LANES, pw), F32)]
    dq = _call("ret_bwd_q", up_body, (n_grp, n_blk), ins(up),
               [((s, RET_QK), BF16, (t, wq), lambda hg, i: (i, hg))], scratch=scratch)[0]
    dk, dv, *rest = _call("ret_bwd_kv", down_body, (n_grp, n_blk), ins(down),
                          [((s, RET_QK), BF16, (t, wq), lambda hg, i: (down(i), hg)),
                           ((s, RET_V), BF16, (t, wv), lambda hg, i: (down(i), hg))],
                          scratch=scratch, riders=riders)
    return [dq, dk, dv] + rest


def _sb_bwd(qkv, weights, do, tq, tk, riders=None):
    s = qkv.shape[0]
    n_pair = HEADS // 2
    _check_tiles(s, tq, tk, SB_GROUP)

    def body(q_ref, k_ref, v_ref, a_ref, do_ref, dq_ref, dk_ref, dv_ref):
        i = pl.program_id(1)

        @pl.when(i == 0)
        def _():
            dk_ref[...] = jnp.zeros_like(dk_ref)
            dv_ref[...] = jnp.zeros_like(dv_ref)

        lower = _tri(tk, False)
        qs = _stack_heads(q_ref[...])
        dos = _stack_heads(do_ref[...].astype(BF16))

        def make_step(near_diagonal, n_sub=SB_GROUP):
            def step(g, carry):
                c_e, dq = carry
                js = [g * SB_GROUP + sub for sub in range(n_sub)]
                rows = [_key_rows(j, tk) for j in js]
                zs = [_dot(qs, k_ref[rw, :], _NT) for rw in rows]
                das = [_dot(dos, v_ref[rw, :], _NT) for rw in rows]
                avals = [a_ref[j] for j in js]
                for a, rw in zip(avals, rows):
                    dv_ref[rw, :] += _dot(a, dos, _TN)
                es = [a.astype(F32) * da for a, da in zip(avals, das)]
                prefixes = [_dot(e, lower) for e in es]
                betas = [1.0 / (1.0 + jnp.exp2(-z)) for z in zs]
                for sub in range(n_sub):
                    dz = es[sub] - (es[sub] + prefixes[sub] + c_e) * betas[sub]
                    if near_diagonal:
                        dz = jnp.where(_sb_valid(i, js[sub], tq, tk), dz, 0.0)
                    dz = dz.astype(BF16)
                    dk_ref[rows[sub], :] += _dot(dz, qs, _TN)
                    dq = dq + _dot(_side_by_side(dz, tq), _stack_heads(k_ref[rows[sub], :]))
                    c_e = c_e + jnp.sum(es[sub], axis=1, keepdims=True)
                return c_e, dq
            return step

        n_full = _n_full(i, tq, tk, SB_GROUP)
        carry = (jnp.zeros((2 * tq, 1), F32), jnp.zeros((tq, LANES), F32))
        carry = lax.fori_loop(0, n_full, make_step(False), carry)
        _, dq = _diagonal_step(i, tq, tk, lambda n_sub: (lambda n, cr: make_step(True, n_sub)(n_full, cr)), carry)
        dq_ref[...] = dq

    blk = lambda hp, i: (i, hp)
    n_kb = s // tk
    return _call("sb_bwd", body, (n_pair, s // tq),
                 [(qkv, (tq, LANES), blk),
                  (qkv, (s, LANES), lambda hp, i: (0, n_pair + hp)),
                  (qkv, (s, LANES), lambda hp, i: (0, 2 * n_pair + hp)),
                  (weights, (None, None, n_kb, 2 * tq, tk), lambda hp, i: (hp, i, 0, 0, 0)),
                  (do, (tq, LANES), blk)],
                 [((s, SB_W), F32, (tq, LANES), blk),
                  ((s, SB_W), F32, (s, LANES), lambda hp, i: (0, hp)),
                  ((s, SB_W), F32, (s, LANES), lambda hp, i: (0, hp))], riders=riders)


def _assemble_dproj(dq_r, dk_r, dv_r, dg_r, dq_s, dk_s, dv_s, da_r, da_s, cos, sin, idx_col, lg_lanes, tm, riders=None):
    s, d = da_r.shape
    width = 2 * RET_QK + 2 * RET_V + 3 * SB_W + 2 * d

    def body(dq_ref, dk_ref, dv_ref, dg_ref, dqs_ref, dks_ref, dvs_ref, dar_ref, das_ref, cos_ref, sin_ref,
             idx_ref, lg_ref, o_ref):
        lane = lax.broadcasted_iota(jnp.int32, (1, LANES), 1)
        first = jnp.bitwise_and(lane, RET_DQK - 1) < (RET_DQK // 2)
        cos, sin = cos_ref[...], sin_ref[...]
        idx = idx_ref[...]
        for src, base, sign, scale in ((dq_ref, 0, 1.0, 1.0), (dk_ref, RET_QK, -1.0, RET_DQK ** -0.5)):
            for g in range(RET_QK // LANES):
                v = src[:, g * LANES:(g + 1) * LANES].astype(F32) * (_decay_scale(lg_ref, idx, g, sign) * scale)
                sw = jnp.where(first, pltpu.roll(v, LANES - RET_DQK // 2, 1), pltpu.roll(v, RET_DQK // 2, 1))
                o_ref[:, base + g * LANES:base + (g + 1) * LANES] = (v * cos - sw * sin).astype(BF16)
        off = 2 * RET_QK
        o_ref[:, off:off + RET_V] = dv_ref[...].astype(BF16)
        off += RET_V
        o_ref[:, off:off + RET_V] = dg_ref[...]
        off += RET_V
        o_ref[:, off:off + SB_W] = (dqs_ref[...] * (SB_DH ** -0.5)).astype(BF16)
        off += SB_W
        o_ref[:, off:off + SB_W] = (dks_ref[...] * LN2).astype(BF16)
        off += SB_W
        o_ref[:, off:off + SB_W] = dvs_ref[...].astype(BF16)
        off += SB_W
        o_ref[:, off:off + d] = dar_ref[...]
        off += d
        o_ref[:, off:off + d] = das_ref[...]

    row = lambda i: (i, 0)
    ins = [(a, (tm, a.shape[1]), row) for a in (dq_r, dk_r, dv_r, dg_r, dq_s, dk_s, dv_s, da_r, da_s, cos, sin, idx_col)]
    ins.append((lg_lanes, (1, RET_QK), lambda i: (0, 0)))
    return _call("assemble_dproj", body, (s // tm,), ins, [((s, width), BF16, (tm, width), row)], riders=riders)


def _in_bwd(dproj, w_in_t, x, dhres, mod, g1, tm, riders=None):
    s, d = x.shape
    width = dproj.shape[1]

    def body(a_ref, w_ref, x_ref, dh_ref, mod_ref, g_ref, dx_ref, dsh_ref, dsc_ref, dg_ref):
        _zero_at_start([dsh_ref, dsc_ref, dg_ref])
        g1, sc1 = g_ref[...], mod_ref[:, d:2 * d]
        for rows in _pieces(tm):
            dh = _dot(a_ref[rows, :], w_ref[...])
            n1, r1 = _rms(x_ref[rows, :], d)
            dsh_ref[...] += _colsum(dh)
            dsc_ref[...] += _colsum(dh * n1 * g1)
            dg_ref[...] += _colsum(dh * n1 * (1.0 + sc1))
            dx_ref[rows, :] = dh_ref[rows, :] + _rms_bwd(dh * g1 * (1.0 + sc1), n1, r1, d)

    row = lambda i: (i, 0)
    fix = lambda i: (0, 0)
    vec = ((1, d), F32, (1, d), fix)
    return _call("in_bwd", body, (s // tm,),
                 [(dproj, (tm, width), row), (w_in_t, (width, d), fix), (x, (tm, d), row), (dhres, (tm, d), row),
                  (mod, (1, 6 * d), fix), (g1, (1, d), fix)],
                 [((s, d), F32, (tm, d), row), vec, vec, vec], riders=riders)


def _adamw(w, g, m, v):
    m = ADAM_B1 * m + (1.0 - ADAM_B1) * g
    v = ADAM_B2 * v + (1.0 - ADAM_B2) * (g * g)
    m_hat = m / (1.0 - ADAM_B1 ** ADAM_STEP)
    v_hat = v / (1.0 - ADAM_B2 ** ADAM_STEP)
    delta = -ADAM_LR * (m_hat / (jnp.sqrt(v_hat) + ADAM_EPS) + ADAM_WD * w)
    return delta, m, v


def _adam_reduce(name, sets, steps):
    n = len(sets)

    def body(*refs):
        for k in range(n):
            p_ref, w_ref, m_ref, v_ref = refs[4 * k:4 * k + 4]
            outs = refs[4 * n + 4 * k:4 * n + 4 * k + 4]
            g = p_ref[0].astype(F32)
            for j in range(1, p_ref.shape[0]):
                g = g + p_ref[j].astype(F32)
            for o_ref, val in zip(outs, (g,) + _adamw(w_ref[...], g, m_ref[...], v_ref[...])):
                o_ref[...] = val

    ins, outs = [], []
    row = lambda i: (i, 0)
    for parts, w, m, v in sets:
        rws, cls = w.shape
        tr = rws // steps
        assert tr * steps == rws and tr % 16 == 0
        ins += [(parts, (parts.shape[0], tr, cls), lambda i: (0, i, 0)), (w, (tr, cls), row), (m, (tr, cls), row),
                (v, (tr, cls), row)]
        outs += [((rws, cls), F32, (tr, cls), row)] * 4
    res = _call(name, body, (steps,), ins, outs)
    return [res[4 * k:4 * k + 4] for k in range(n)]


def _ada_bwd_adam(cs_t, dmod_cols, w, m, v):
    d, nc = w.shape

    def body(c_ref, dm_ref, w_ref, m_ref, v_ref, g_out, d_out, m_out, v_out):
        g = c_ref[0] * dm_ref[0:1, :]
        for r in range(1, N_DEV):
            g = g + c_ref[r] * dm_ref[r:r + 1, :]
        delta, mn, vn = _adamw(w_ref[...], g, m_ref[...], v_ref[...])
        g_out[...] = g
        d_out[...] = delta
        m_out[...] = mn
        v_out[...] = vn

    fix = lambda i: (0, 0)
    blk = (d, nc)
    return _call("ada_bwd_adam", body, (1,),
                 [(cs_t, (N_DEV, d, 1), lambda i: (0, 0, 0)), (dmod_cols, (N_DEV, nc), fix), (w, blk, fix), (m, blk, fix), (v, blk, fix)],
                 [((d, nc), F32, blk, fix)] * 4)


def _small_adam(parts, ws, ms, vs):
    n = len(ws)
    widths = [w.shape[1] for w in ws]
    total = parts.shape[1]
    assert sum(widths) + LANES == total

    def body(p_ref, *refs):
        w_refs, m_refs, v_refs = refs[:n], refs[n:2 * n], refs[2 * n:3 * n]
        outs = refs[3 * n:]
        g = p_ref[0:1, :]
        for k in range(1, N_DEV):
            g = g + p_ref[k:k + 1, :]
        off = 0
        for i, width in enumerate(widths):
            gi = g[:, off:off + width]
            delta, mn, vn = _adamw(w_refs[i][...], gi, m_refs[i][...], v_refs[i][...])
            for o_ref, val in zip(outs[4 * i:4 * i + 4], (gi, delta, mn, vn)):
                o_ref[...] = val
            off += width
        outs[4 * n][...] = g[:, off:off + LANES]

    fix = lambda i: (0, 0)
    vec = lambda a: (a, (1, a.shape[1]), fix)
    out_specs = [((1, width), F32, (1, width), fix) for width in widths for _ in range(4)]
    out_specs.append(((1, LANES), F32, (1, LANES), fix))
    res = _call("small_adam", body, (1,),
                [(parts, (N_DEV, total), fix)] + [vec(a) for a in list(ws) + list(ms) + list(vs)], out_specs)
    return [res[4 * i:4 * i + 4] for i in range(n)], res[4 * n]


def kernel(x, c, positions, ada_w, ada_b, pre_mix_g, post_mix_g, pre_ffn_g, post_ffn_g, w_in, ret_gn_g, w_ret_branch, w_sb_branch, w_out, w_ff1, w_ff2, loss_target, m_ada_w, m_ada_b, m_pre_mix_g, m_post_mix_g, m_pre_ffn_g, m_post_ffn_g, m_w_in, m_ret_gn_g, m_w_ret_branch, m_w_sb_branch, m_w_out, m_w_ff1, m_w_ff2, v_ada_w, v_ada_b, v_pre_mix_g, v_post_mix_g, v_pre_ffn_g, v_post_ffn_g, v_w_in, v_ret_gn_g, v_w_ret_branch, v_w_sb_branch, v_w_out, v_w_ff1, v_w_ff2):
    _, s, d = x.shape
    d_ff = w_ff1.shape[2] * N_DEV
    d_in = w_in.shape[2] * N_DEV
    me = 4 * lax.axis_index("x") + 2 * lax.axis_index("y") + lax.axis_index("c")
    x2, tgt = x[0], loss_target[0]

    core = lax.axis_index("c").astype(jnp.int32).reshape(1)
    bf = lambda w: w[0].astype(BF16)

    w_in_t, m_in_t, v_in_t = (jnp.swapaxes(a[0], 0, 1) for a in (w_in, m_w_in, v_w_in))

    c_all, g_in = _exchange("gather_in", [c, w_in_t.astype(BF16)], ["gather", "gather_chip"])
    c_all = c_all.reshape(N_DEV, d)

    n_ada = ada_w.shape[2]
    cs_all = _silu_rows(c_all)
    ada_b_cols = lax.dynamic_slice(ada_b, (0, me * n_ada), (1, n_ada))
    mod_cols = _ada_fwd(cs_all, ada_w[0], ada_b_cols)
    mod_all = _exchange("gather_mod", [mod_cols], ["gather"])[0]
    mod = lax.dynamic_index_in_dim(mod_all, me, axis=1, keepdims=False).reshape(1, 6 * d)

    tm = min(256, s)
    h, g_in = _pre_norm(x2, pre_mix_g, mod, 2 * tm, riders=([g_in], ["forward"]))
    wt_in = g_in.reshape(d_in, d)
    bf_t = lambda w: jnp.swapaxes(w[0], 0, 1).astype(BF16)
    small_w = [bf(w_ret_branch), bf_t(w_sb_branch), bf(w_out)]
    proj, *small_w = _matmul("in_proj", h, wt_in, "nt", s, 512, BF16, riders=(small_w, ["gather_chip"] * 3))
    pos_col = positions.reshape(s, 1).astype(F32)
    freqs = ROPE_BASE ** (-jnp.arange(0, RET_DQK, 2, dtype=F32) / RET_DQK)
    inv_freq = jnp.tile(freqs, LANES // (RET_DQK // 2)).reshape(1, LANES)
    log_gamma_np = np.log1p(-(2.0 ** (-5.0 - np.arange(HEADS))))
    log_gamma = jnp.asarray(log_gamma_np, F32)
    lg_lanes = jnp.asarray(np.repeat(log_gamma_np, RET_DQK).reshape(1, RET_QK), F32)
    idx_col = (jnp.arange(s, dtype=F32) - (s // 2)).reshape(s, 1)
    qk_rot, v_bf, qkv_sb, cos_t, sin_t = _prep(proj, pos_col, idx_col, inv_freq, lg_lanes, 2 * tm)
    tq, tk = min(256, s), min(128, s)
    tq_sb = min(SB_TQ, s)
    sb, sb_weights, *big_w = _sb_fwd(qkv_sb, tq_sb, tk, riders=([bf(w_ff2), bf_t(w_ff1)], ["gather_chip"] * 2))
    ret, retg, g_ret, g_sb, g_out, g_ff2, g_ff1 = _ret_fwd(qk_rot, v_bf, proj, ret_gn_g, log_gamma, tq,
                                                           riders=(small_w + big_w, ["forward"] * 5))
    wf_ret = g_ret.reshape(RET_V, d)
    wt_sb = g_sb.reshape(d, SB_W)
    wf_out = g_out.reshape(d, d)
    wt_ff1 = g_ff1.reshape(d_ff, d)
    wf_ff2 = g_ff2.reshape(d_ff, d)
    mixed, r_bf, s_bf, y, hres, h2 = _merge_out(retg, sb, wf_ret, wt_sb, wf_out, proj, x2, mod, post_mix_g, pre_ffn_g, tm)
    u, act = _ff1(h2, wt_ff1, s, 512)
    dout, df, loss_sum, d_gt2, d_gp2 = _ff2_loss(act, wf_ff2, hres, tgt, mod, post_ffn_g, tm)

    du, gw_ff2, gw_ff1 = _ffn_bwd(df, wf_ff2, u, act, h2, d_ff // N_DEV)
    gw_ff2 = gw_ff2.reshape(N_DEV, d_ff // N_DEV, d)
    dhres, dy, d_sh2, d_sc2, d_g2, d_gt1, d_gp1, t_ff1, t_ff2 = _ff1_bwd(
        du, wt_ff1, hres, dout, y, mod, pre_ffn_g, post_mix_g, tm, riders=([gw_ff1, gw_ff2], ["pair"] * 2))
    s_ff1, s_ff2 = _pair_sum("pair_sum_ff", [(gw_ff1, t_ff1), (gw_ff2, t_ff2)], core)
    gw_out = _matmul("grad_w_out", mixed, dy, "tn", 512, d, BF16).reshape(N_DEV, d // N_DEV, d)
    d_r, d_s, da_r, da_s = _out_bwd(dy, wf_out, proj, r_bf, s_bf, 2 * tm, min(512, d))
    dsb, gw_ret, gw_sb, dg_r, dret, d_gn = _branch_bwd(d_r, d_s, retg, sb, wf_ret, wt_sb, ret, proj, ret_gn_g)
    gw_ret = gw_ret.reshape(N_DEV, RET_V // N_DEV, d)
    dq_s, dk_s, dv_s, p_ff1, p_ff2, p_out = _sb_bwd(qkv_sb, sb_weights, dsb, tq_sb, tk,
                                                    riders=([s_ff1, s_ff2, gw_out], ["chip_scatter"] * 2 + ["scatter"]))
    dq_r, dk_r, dv_r, p_sb = _ret_bwd(qk_rot, v_bf, dret, log_gamma, tq, riders=([gw_sb], ["scatter"]))
    dproj = _assemble_dproj(dq_r, dk_r, dv_r, dg_r, dq_s, dk_s, dv_s, da_r, da_s, cos_t, sin_t, idx_col, lg_lanes, tm)[0]
    gw_in, p_ret = _matmul("grad_w_in", dproj, h, "tn", 512, d, BF16, riders=([gw_ret], ["scatter"]))
    gw_in = gw_in.reshape(N_DEV, d_in // N_DEV, d)
    t_in = _exchange("pair_in", [gw_in], ["pair"])[0]
    s_in = _pair_sum("pair_sum_in", [(gw_in, t_in)], core)[0]
    grad_x, d_sh1, d_sc1, d_g1, p_in = _in_bwd(dproj, wt_in, x2, dhres, mod, pre_mix_g, tm,
                                               riders=([s_in], ["chip_scatter"]))
    loss_lanes = jnp.pad(loss_sum, ((0, 0), (0, LANES - 1)))
    small = jnp.concatenate([d_sh1, d_sc1, d_gt1, d_sh2, d_sc2, d_gt2, d_g1, d_gp1, d_g2, d_gp2, d_gn, loss_lanes], axis=1)
    small_all = _exchange("gather_small", [small], ["gather"])[0].reshape(N_DEV, small.shape[1])
    parts = [p_in, p_ret, p_sb, p_out, p_ff1, p_ff2]

    res = {}
    names = ["w_ret_branch", "w_sb_branch", "w_out", "w_ff1", "w_ff2"]
    ws = [w_ret_branch, w_sb_branch, w_out, w_ff1, w_ff2]
    ms = [m_w_ret_branch, m_w_sb_branch, m_w_out, m_w_ff1, m_w_ff2]
    vs = [v_w_ret_branch, v_w_sb_branch, v_w_out, v_w_ff1, v_w_ff2]
    sets = [(p, w[0], m[0], v[0]) for p, w, m, v in zip(parts[1:], ws, ms, vs)]
    for nm, outs4 in zip(names, _adam_reduce("adam_rest", sets, 2)):
        res[nm] = [o[None] for o in outs4]
    res["w_in"] = [jnp.swapaxes(o, 0, 1)[None]
                   for o in _adam_reduce("adam_w_in", [(parts[0], w_in_t, m_in_t, v_in_t)], 2)[0]]
    dmod_cols = lax.dynamic_slice(small_all, (0, me * n_ada), (N_DEV, n_ada))
    res["ada_w"] = [o[None] for o in _ada_bwd_adam(cs_all.reshape(N_DEV, d, 1), dmod_cols, ada_w[0], m_ada_w[0], v_ada_w[0])]
    vec_names = ["ada_b", "pre_mix_g", "post_mix_g", "pre_ffn_g", "post_ffn_g", "ret_gn_g"]
    vec_res, loss_lanes = _small_adam(small_all,
                                      [ada_b, pre_mix_g, post_mix_g, pre_ffn_g, post_ffn_g, ret_gn_g],
                                      [m_ada_b, m_pre_mix_g, m_post_mix_g, m_pre_ffn_g, m_post_ffn_g, m_ret_gn_g],
                                      [v_ada_b, v_pre_mix_g, v_post_mix_g, v_pre_ffn_g, v_post_ffn_g, v_ret_gn_g])
    res.update(zip(vec_names, vec_res))
    loss = (0.5 / d) * loss_lanes[0, 0]
    order = ["ada_w", "ada_b", "pre_mix_g", "post_mix_g", "pre_ffn_g", "post_ffn_g", "w_in", "ret_gn_g",
             "w_ret_branch", "w_sb_branch", "w_out", "w_ff1", "w_ff2"]
    outs = [loss, grad_x[None]]
    for k in range(4):
        outs += [res[nm][k] for nm in order]
    return tuple(outs)
```

```python
import functools

import numpy as np
import jax
import jax.numpy as jnp
from jax import lax
from jax.experimental import pallas as pl
from jax.experimental.pallas import tpu as pltpu

F32 = jnp.float32
BF16 = jnp.bfloat16
N_DEV = 8
AXES = ("x", "y", "c")

EPS = 1e-6
CHUNK = 64
CHUNK_SHIFT = 6
HEADS = 8
RET_DQK = 64
RET_DV = 128
SB_DH = 64
RET_QK = HEADS * RET_DQK
RET_V = HEADS * RET_DV
SB_W = HEADS * SB_DH
ROPE_BASE = 10000.0
LANES = 128

ADAM_LR = 0.001
ADAM_B1 = 0.9
ADAM_B2 = 0.999
ADAM_EPS = 1e-08
ADAM_WD = 0.01
ADAM_STEP = 10

VMEM_LIMIT = 56 * 1024 * 1024

_NN = (((1,), (0,)), ((), ()))
_NT = (((1,), (1,)), ((), ()))
_TN = (((0,), (0,)), ((), ()))


def _dot(a, b, dims=_NN):
    if a.dtype != BF16:
        a = a.astype(BF16)
    if b.dtype != BF16:
        b = b.astype(BF16)
    return lax.dot_general(a, b, dims, preferred_element_type=F32)


def _sigmoid(x):
    return 1.0 / (1.0 + jnp.exp(-x))


def _rms(x, d):
    r = lax.rsqrt(jnp.sum(x * x, axis=1, keepdims=True) * (1.0 / d) + EPS)
    return x * r, r


def _rms_bwd(dn, n, r, d):
    return r * (dn - n * (jnp.sum(dn * n, axis=1, keepdims=True) * (1.0 / d)))


def _colsum(v):
    return jnp.sum(v, axis=0, keepdims=True)


ROW_SPLIT = 2


def _zero_at_start(refs):
    @pl.when(pl.program_id(0) == 0)
    def _():
        for r in refs:
            r[...] = jnp.zeros_like(r)


def _pieces(tm):
    step = tm // ROW_SPLIT
    return [slice(k * step, (k + 1) * step) for k in range(ROW_SPLIT)]


KIND_SLOTS = {"gather": N_DEV, "scatter": N_DEV, "gather_chip": N_DEV, "forward": N_DEV, "pair": N_DEV // 2,
              "chip_scatter": N_DEV // 2}
SEMS_PER_ARRAY = N_DEV - 1


def _exchange_copies(ins, outs, send_sems, recv_sems, local_sems, kinds):
    x, y, c = (lax.axis_index(a) for a in AXES)
    me, chip, sibling = 4 * x + 2 * y + c, 2 * x + y, (x, y, 1 - c)
    mesh_id = pl.DeviceIdType.MESH
    other_chips = []
    for k in range(1, N_DEV // 2):
        px = 1 - x if k & 2 else x
        py = 1 - y if k & 1 else y
        other_chips.append((px, py))
    copies = []
    for i, kind in enumerate(kinds):
        def remote(src, dst, k, to, i=i):
            return pltpu.make_async_remote_copy(
                src_ref=src, dst_ref=dst, send_sem=send_sems.at[i * SEMS_PER_ARRAY + k],
                recv_sem=recv_sems.at[i * SEMS_PER_ARRAY + k], device_id=to, device_id_type=mesh_id)

        if kind in ("gather", "scatter"):
            pick = (lambda ref, d: ref.at[d]) if kind == "scatter" else (lambda ref, d: ref)
            copies.append(pltpu.make_async_copy(pick(ins[i], me), outs[i].at[me], local_sems.at[i]))
            for k in range(1, N_DEV):
                to = (1 - x if k & 4 else x, 1 - y if k & 2 else y, 1 - c if k & 1 else c)
                copies.append(remote(pick(ins[i], 4 * to[0] + 2 * to[1] + to[2]), outs[i].at[me], k - 1, to))
        elif kind == "gather_chip":
            copies.append(pltpu.make_async_copy(ins[i], outs[i].at[me], local_sems.at[i]))
            copies.append(remote(ins[i], outs[i].at[me], 0, sibling))
            for k, (px, py) in enumerate(other_chips):
                copies.append(remote(ins[i], outs[i].at[me], 1 + k, (px, py, c)))
        elif kind == "forward":
            for k, (px, py) in enumerate(other_chips):
                slot = 4 * px + 2 * py + c
                copies.append(remote(outs[i].at[slot], outs[i].at[slot], k, sibling))
        elif kind == "pair":
            for k in range(N_DEV // 2):
                copies.append(remote(ins[i].at[2 * k + 1 - c], outs[i].at[k], k, sibling))
        elif kind == "chip_scatter":
            copies.append(pltpu.make_async_copy(ins[i].at[chip], outs[i].at[chip], local_sems.at[i]))
            for k, (px, py) in enumerate(other_chips):
                copies.append(remote(ins[i].at[2 * px + py], outs[i].at[chip], k, (px, py, c)))
        else:
            raise ValueError(kind)
    return copies


def _exchange_shapes(arrays, kinds):
    shapes = []
    for a, kind in zip(arrays, kinds):
        tail = a.shape if kind in ("gather", "gather_chip") else a.shape[1:]
        shapes.append(jax.ShapeDtypeStruct((KIND_SLOTS[kind],) + tuple(tail), a.dtype))
    return shapes


def _exchange_sems(n):
    return [pltpu.SemaphoreType.DMA((n * SEMS_PER_ARRAY,)), pltpu.SemaphoreType.DMA((n * SEMS_PER_ARRAY,)),
            pltpu.SemaphoreType.DMA((n,))]


def _call(name, body, grid, ins, outs, scratch=(), riders=None, prefetch=None):
    any_spec = pl.BlockSpec(memory_space=pl.ANY)
    in_specs = [pl.BlockSpec(memory_space=im) if bs is None else pl.BlockSpec(bs, im) for _, bs, im in ins]
    out_specs = [pl.BlockSpec(bs, im) for _, _, bs, im in outs]
    out_shape = [jax.ShapeDtypeStruct(s, d) for s, d, _, _ in outs]
    operands = [a for a, _, _ in ins]
    scratch = list(scratch)
    aliases = {}
    n_pre = 0 if prefetch is None else 1
    kernel = functools.partial(body) if prefetch is None else (lambda _, *refs: body(*refs))
    if riders is not None:
        arrays, kinds = riders
        nr, n_in, n_out, n_scr = len(arrays), len(ins), len(outs), len(scratch)

        def kernel(*refs):
            refs = refs[n_pre:]
            own_in, ride_in = refs[:n_in], refs[n_in:n_in + nr]
            own_out = refs[n_in + nr:n_in + nr + n_out]
            ride_out = refs[n_in + nr + n_out:n_in + 2 * nr + n_out]
            own_scr = refs[n_in + 2 * nr + n_out:n_in + 2 * nr + n_out + n_scr]
            sems = refs[n_in + 2 * nr + n_out + n_scr:]
            ids = [pl.program_id(a) for a in range(len(grid))]
            first = functools.reduce(jnp.logical_and, [i == 0 for i in ids])
            last = functools.reduce(jnp.logical_and, [i == g - 1 for i, g in zip(ids, grid)])

            @pl.when(first)
            def _():
                for cp in _exchange_copies(ride_in, ride_out, *sems, kinds):
                    cp.start()

            body(*own_in, *own_out, *own_scr)

            @pl.when(last)
            def _():
                for cp in _exchange_copies(ride_in, ride_out, *sems, kinds):
                    cp.wait()

        in_specs += [any_spec] * nr
        out_specs += [any_spec] * nr
        out_shape += _exchange_shapes(arrays, kinds)
        operands += list(arrays)
        scratch += _exchange_sems(nr)
        aliases = {n_pre + n_in + r: n_out + r for r, kind in enumerate(kinds) if kind == "forward"}
    params = pltpu.CompilerParams(dimension_semantics=("arbitrary",) * len(grid), vmem_limit_bytes=VMEM_LIMIT)
    if prefetch is None:
        return pl.pallas_call(kernel, name=name, grid=grid, in_specs=in_specs, out_specs=out_specs,
                              out_shape=out_shape, scratch_shapes=scratch, input_output_aliases=aliases,
                              compiler_params=params)(*operands)
    grid_spec = pltpu.PrefetchScalarGridSpec(num_scalar_prefetch=1, grid=grid, in_specs=in_specs,
                                             out_specs=out_specs, scratch_shapes=scratch)
    return pl.pallas_call(kernel, name=name, grid_spec=grid_spec, out_shape=out_shape,
                          input_output_aliases=aliases, compiler_params=params)(prefetch, *operands)


def _exchange(name, arrays, kinds):
    n = len(arrays)

    def body(*refs):
        copies = _exchange_copies(refs[:n], refs[n:2 * n], *refs[2 * n:], kinds)
        for cp in copies:
            cp.start()
        for cp in copies:
            cp.wait()

    any_spec = pl.BlockSpec(memory_space=pl.ANY)
    return pl.pallas_call(
        functools.partial(body),
        name=name,
        in_specs=[any_spec] * n,
        out_specs=[any_spec] * n,
        out_shape=_exchange_shapes(arrays, kinds),
        scratch_shapes=_exchange_sems(n),
        input_output_aliases={i: i for i, kind in enumerate(kinds) if kind == "forward"},
    )(*arrays)


def _pair_sum(name, pairs, my_core):
    n = len(pairs)

    def body(*refs):
        for k in range(n):
            a_ref, b_ref, o_ref = refs[2 * k], refs[2 * k + 1], refs[2 * n + k]
            o_ref[...] = (a_ref[...].astype(F32) + b_ref[...].astype(F32)).astype(o_ref.dtype)

    ins, outs = [], []
    for mine, theirs in pairs:
        _, rws, cls = mine.shape
        ins += [(mine, (None, rws, cls), lambda k, core: (2 * k + core[0], 0, 0)),
                (theirs, (None, rws, cls), lambda k, core: (k, 0, 0))]
        outs.append(((N_DEV // 2, rws, cls), mine.dtype, (None, rws, cls), lambda k, core: (k, 0, 0)))
    return _call(name, body, (N_DEV // 2,), ins, outs, prefetch=my_core)


def _matmul(name, a, b, kind, tm, tn, out_dtype, blocked_out=False, riders=None):
    if kind == "tn":
        kdim, m = a.shape
    else:
        m, kdim = a.shape
    n = b.shape[0] if kind == "nt" else b.shape[1]
    tm, tn = min(tm, m), min(tn, n)
    dims = {"nn": _NN, "nt": _NT, "tn": _TN}[kind]

    def body(a_ref, b_ref, o_ref):
        o_ref[...] = _dot(a_ref[...], b_ref[...], dims).astype(o_ref.dtype)

    a_spec = (a, (kdim, tm), lambda j, i: (0, i)) if kind == "tn" else (a, (tm, kdim), lambda j, i: (i, 0))
    b_spec = (b, (tn, kdim), lambda j, i: (j, 0)) if kind == "nt" else (b, (kdim, tn), lambda j, i: (0, j))
    if blocked_out:
        out = ((n // tn, m, tn), out_dtype, (None, tm, tn), lambda j, i: (j, i, 0))
    else:
        out = ((m, n), out_dtype, (tm, tn), lambda j, i: (i, j))
    res = _call(name, body, (n // tn, m // tm), [a_spec, b_spec], [out], riders=riders)
    return res[0] if riders is None else res


def _ada_fwd(c_all, ada_w, ada_b_cols):
    def body(c_ref, w_ref, b_ref, cs_ref, o_ref):
        v = c_ref[...]
        cs = v * _sigmoid(v)
        cs_ref[...] = cs
        o_ref[...] = lax.dot_general(cs, w_ref[...], _NN, preferred_element_type=F32,
                                     precision=lax.Precision.HIGHEST) + b_ref[...]

    r, d = c_all.shape
    nc = ada_w.shape[1]
    fix = lambda i: (0, 0)
    return _call("ada_fwd", body, (1,),
                 [(c_all, (r, d), fix), (ada_w, (d, nc), fix), (ada_b_cols, (1, nc), fix)],
                 [((r, d), F32, (r, d), fix), ((r, nc), F32, (r, nc), fix)])


def _pre_norm(x, g, mod, tm, riders=None):
    s, d = x.shape

    def body(x_ref, g_ref, mod_ref, h_ref):
        n, _ = _rms(x_ref[...], d)
        sh, sc = mod_ref[:, 0:d], mod_ref[:, d:2 * d]
        h_ref[...] = (n * g_ref[...] * (1.0 + sc) + sh).astype(BF16)

    return _call("pre_norm", body, (s // tm,),
                 [(x, (tm, d), lambda i: (i, 0)), (g, (1, d), lambda i: (0, 0)),
                  (mod, (1, 6 * d), lambda i: (0, 0))],
                 [((s, d), BF16, (tm, d), lambda i: (i, 0))], riders=riders)


LOG2E = 1.4426950408889634
LN2 = 0.6931471805599453


def _decay_scale(lg_ref, idx, g, sign):
    return jnp.exp((sign * idx) * lg_ref[:, g * LANES:(g + 1) * LANES])


def _prep(proj, pos_col, idx_col, inv_freq, lg_lanes, tm):
    s = proj.shape[0]
    sb_off = (2 * RET_QK + 2 * RET_V) // (3 * SB_W)
    n_q = RET_QK // LANES

    def body(qk_ref, v_ref, sb_ref, pos_ref, idx_ref, f_ref, lg_ref, qk_out, v_out, sb_out, cos_out, sin_out):
        ang = pos_ref[...] * f_ref[...]
        lane = lax.broadcasted_iota(jnp.int32, (1, LANES), 1)
        first = jnp.bitwise_and(lane, RET_DQK - 1) < (RET_DQK // 2)
        cos = jnp.cos(ang)
        sin = jnp.where(first, -1.0, 1.0) * jnp.sin(ang)
        cos_out[...] = cos
        sin_out[...] = sin
        idx = idx_ref[...]
        for g in range(2 * n_q):
            v = qk_ref[:, g * LANES:(g + 1) * LANES].astype(F32)
            sw = jnp.where(first, pltpu.roll(v, LANES - RET_DQK // 2, 1), pltpu.roll(v, RET_DQK // 2, 1))
            r = v * cos + sw * sin
            if g < n_q:
                r = r * _decay_scale(lg_ref, idx, g, 1.0)
            else:
                r = r * (_decay_scale(lg_ref, idx, g - n_q, -1.0) * (RET_DQK ** -0.5))
            qk_out[:, g * LANES:(g + 1) * LANES] = r.astype(BF16)
        v_out[...] = v_ref[...].astype(BF16)
        sb_out[:, 0:SB_W] = (sb_ref[:, 0:SB_W].astype(F32) * (SB_DH ** -0.5 * LOG2E)).astype(BF16)
        sb_out[:, SB_W:3 * SB_W] = sb_ref[:, SB_W:3 * SB_W].astype(BF16)

    return _call("prep", body, (s // tm,),
                 [(proj, (tm, 2 * RET_QK), lambda i: (i, 0)),
                  (proj, (tm, RET_V), lambda i: (i, 2 * RET_QK // RET_V)),
                  (proj, (tm, 3 * SB_W), lambda i: (i, sb_off)),
                  (pos_col, (tm, 1), lambda i: (i, 0)),
                  (idx_col, (tm, 1), lambda i: (i, 0)),
                  (inv_freq, (1, LANES), lambda i: (0, 0)),
                  (lg_lanes, (1, RET_QK), lambda i: (0, 0))],
                 [((s, 2 * RET_QK), BF16, (tm, 2 * RET_QK), lambda i: (i, 0)),
                  ((s, RET_V), BF16, (tm, RET_V), lambda i: (i, 0)),
                  ((s, 3 * SB_W), BF16, (tm, 3 * SB_W), lambda i: (i, 0)),
                  ((s, LANES), F32, (tm, LANES), lambda i: (i, 0)),
                  ((s, LANES), F32, (tm, LANES), lambda i: (i, 0))])


def _head_mask(hh):
    lane = lax.broadcasted_iota(jnp.int32, (1, LANES), 1)
    return (lane >= RET_DQK) if hh else (lane < RET_DQK)


def _masked(v, m):
    return jnp.where(m, v, jnp.zeros_like(v))


SB_GROUP = 4
SB_TQ = 256


def _stack_heads(v):
    return jnp.concatenate([_masked(v, _head_mask(0)), _masked(v, _head_mask(1))], axis=0)


def _side_by_side(v, t):
    return jnp.concatenate([v[:t], v[t:]], axis=1)


def _tile_pos(i, j, tq, tk):
    row = jnp.bitwise_and(lax.broadcasted_iota(jnp.int32, (2 * tq, tk), 0), tq - 1) + i * tq
    col = lax.broadcasted_iota(jnp.int32, (2 * tq, tk), 1) + j * tk
    return row, col


def _n_groups(i, tq, tk, grp):
    return ((i + 1) * (tq // tk) + grp - 1) // grp


def _n_full(i, tq, tk, grp):
    return (i * (tq // tk)) // grp


def _key_rows(j, tk):
    return pl.ds(pl.multiple_of(j * tk, tk), tk)


def _ret_weight(lg_rows, i, j, tq, tk):
    row, col = _tile_pos(i, j, tq, tk)
    same = jnp.right_shift(col, CHUNK_SHIFT) == jnp.right_shift(row, CHUNK_SHIFT)
    later = jnp.where(same, jnp.exp((2.0 * lg_rows) * (col - row).astype(F32)), 0.0)
    return jnp.where(col <= row, 1.0, later)


def _lg_rows(lg_ref, hp, tq):
    first = lax.broadcasted_iota(jnp.int32, (2 * tq, 1), 0) < tq
    return jnp.where(first, lg_ref[2 * hp], lg_ref[2 * hp + 1])


def _check_tiles(s, tq, tk, grp):
    assert tq % tk == 0 and tq & (tq - 1) == 0 and tk & (tk - 1) == 0
    assert s % tq == 0 and (s // tk) % grp == 0 and s // tk <= LANES


def _pair_mask():
    r = lax.broadcasted_iota(jnp.int32, (LANES, 2 * RET_DV), 0) >= RET_DQK
    c = lax.broadcasted_iota(jnp.int32, (LANES, 2 * RET_DV), 1) >= RET_DV
    return (r == c).astype(F32)


def _ret_block(lg_ref, hp, i, t, qb, kb):
    w = _ret_weight(_lg_rows(lg_ref, hp, t), i, i, t, t)
    return _dot(_stack_heads(qb), kb, _NT), w


RET_PAIRS = 2


def _lanes(ref, p, width):
    return ref[:, p * width:(p + 1) * width]


def _ret_fwd(qk_rot, v_bf, proj, gn_g, log_gamma, t, riders=None):
    s = qk_rot.shape[0]
    n_pair = HEADS // 2
    pw = 2 * RET_DV
    wq, wv = RET_PAIRS * LANES, RET_PAIRS * pw
    gate_off = (2 * RET_QK + RET_V) // wv
    assert s % t == 0 and t % CHUNK == 0 and t & (t - 1) == 0 and n_pair % RET_PAIRS == 0

    def body(lg_ref, q_ref, k_ref, v_ref, g_ref, w_ref, ret_ref, rg_ref, state_ref):
        hg, i = pl.program_id(0), pl.program_id(1)

        @pl.when(i == 0)
        def _():
            state_ref[...] = jnp.zeros_like(state_ref)

        pairs = range(RET_PAIRS)
        qbs = [_lanes(q_ref, p, LANES) for p in pairs]
        kbs = [_lanes(k_ref, p, LANES) for p in pairs]
        vbs = [_lanes(v_ref, p, pw) for p in pairs]
        zws = [_ret_block(lg_ref, hg * RET_PAIRS + p, i, t, qbs[p], kbs[p]) for p in pairs]
        ps = [(z * w).astype(BF16) for z, w in zws]
        outs = [jnp.concatenate([_dot(ps[p][:t], vbs[p][:, 0:RET_DV]), _dot(ps[p][t:], vbs[p][:, RET_DV:pw])], axis=1)
                + _dot(qbs[p], state_ref[p]) for p in pairs]
        for p in pairs:
            state_ref[p] += _pair_mask() * _dot(kbs[p], vbs[p], _TN)
        for p in pairs:
            for hh in range(2):
                cols = slice(p * pw + hh * RET_DV, p * pw + (hh + 1) * RET_DV)
                o = outs[p][:, hh * RET_DV:(hh + 1) * RET_DV]
                ret_ref[:, cols] = o
                mu = jnp.sum(o, axis=1, keepdims=True) * (1.0 / RET_DV)
                xc = o - mu
                var = jnp.sum(xc * xc, axis=1, keepdims=True) * (1.0 / RET_DV)
                nrm = xc * lax.rsqrt(var + EPS) * w_ref[:, cols]
                g = g_ref[:, cols].astype(F32)
                rg_ref[:, cols] = (g * _sigmoid(g) * nrm).astype(BF16)

    blk = lambda hg, i: (i, hg)
    return _call("ret_fwd", body, (n_pair // RET_PAIRS, s // t),
                 [(log_gamma, None, pltpu.SMEM),
                  (qk_rot, (t, wq), blk),
                  (qk_rot, (t, wq), lambda hg, i: (i, n_pair // RET_PAIRS + hg)),
                  (v_bf, (t, wv), blk),
                  (proj, (t, wv), lambda hg, i: (i, gate_off + hg)),
                  (gn_g, (1, wv), lambda hg, i: (0, hg))],
                 [((s, RET_V), F32, (t, wv), blk), ((s, RET_V), BF16, (t, wv), blk)],
                 scratch=[pltpu.VMEM((RET_PAIRS, LANES, pw), F32)], riders=riders)


def _tri(tk, strict_upper):
    r = lax.broadcasted_iota(jnp.int32, (tk, tk), 0)
    cc = lax.broadcasted_iota(jnp.int32, (tk, tk), 1)
    return ((r > cc) if strict_upper else (r < cc)).astype(BF16)


def _diagonal_step(i, tq, tk, make, carry):
    if (tq // tk) % SB_GROUP == 0:
        return make(SB_GROUP)(0, carry)
    assert 2 * (tq // tk) == SB_GROUP
    half = lax.rem(i, 2) == 0
    return lax.cond(half, lambda cr: make(SB_GROUP // 2)(0, cr), lambda cr: make(SB_GROUP)(0, cr), carry)


def _sb_valid(i, j, tq, tk):
    row, col = _tile_pos(i, j, tq, tk)
    return col < row


def _sb_fwd(qkv, tq, tk, riders=None):
    s = qkv.shape[0]
    n_pair = HEADS // 2
    _check_tiles(s, tq, tk, SB_GROUP)

    def body(q_ref, k_ref, v_ref, o_ref, a_ref):
        i = pl.program_id(1)
        upper = _tri(tk, True)
        qs = _stack_heads(q_ref[...])
        n_full, n_groups = _n_full(i, tq, tk, SB_GROUP), _n_groups(i, tq, tk, SB_GROUP)

        def make_step(near_diagonal, last, n_sub=SB_GROUP):
            def step(n, carry):
                c, o = carry
                g = last - 1 - n
                js = [g * SB_GROUP + sub for sub in range(n_sub)]
                zs = [_dot(qs, k_ref[_key_rows(j, tk), :], _NT) for j in js]
                log1ps = [jnp.log2(1.0 + jnp.exp2(-jnp.abs(z))) for z in zs]
                log_1ms = [-jnp.maximum(z, 0.0) - t for z, t in zip(zs, log1ps)]
                log_bs = [jnp.minimum(z, 0.0) - t for z, t in zip(zs, log1ps)]
                if near_diagonal:
                    valids = [_sb_valid(i, j, tq, tk) for j in js]
                    log_1ms = [jnp.where(v, l, 0.0) for v, l in zip(valids, log_1ms)]
                sticks = [_dot(l, upper) for l in log_1ms]
                sums = [jnp.sum(l, axis=1, keepdims=True) for l in log_1ms]
                cs = [None] * n_sub
                for sub in reversed(range(n_sub)):
                    cs[sub] = c
                    c = c + sums[sub]
                for sub, j in enumerate(js):
                    a = jnp.exp2(log_bs[sub] + sticks[sub] + cs[sub])
                    if near_diagonal:
                        a = jnp.where(valids[sub], a, 0.0)
                    a = a.astype(BF16)
                    a_ref[j] = a
                    o = o + _dot(_side_by_side(a, tq), _stack_heads(v_ref[_key_rows(j, tk), :]))
                return c, o
            return step

        carry = (jnp.zeros((2 * tq, 1), F32), jnp.zeros((tq, LANES), F32))
        carry = _diagonal_step(i, tq, tk, lambda n_sub: make_step(True, n_groups, n_sub), carry)
        _, acc = lax.fori_loop(0, n_full, make_step(False, n_full), carry)
        o_ref[...] = acc.astype(BF16)

    n_kb = s // tk
    return _call("sb_fwd", body, (n_pair, s // tq),
                 [(qkv, (tq, LANES), lambda hp, i: (i, hp)),
                  (qkv, (s, LANES), lambda hp, i: (0, n_pair + hp)),
                  (qkv, (s, LANES), lambda hp, i: (0, 2 * n_pair + hp))],
                 [((s, SB_W), BF16, (tq, LANES), lambda hp, i: (i, hp)),
                  ((n_pair, s // tq, n_kb, 2 * tq, tk), BF16, (None, None, n_kb, 2 * tq, tk),
                   lambda hp, i: (hp, i, 0, 0, 0))], riders=riders)


def _merge_out(retg, sb, w_ret, w_sb_t, w_out, proj, x, mod, gp1, g2, tm):
    s, d = x.shape
    gw = min(512, d)
    n_g = d // gw
    ar_off = (2 * RET_QK + 2 * RET_V + 3 * SB_W) // gw

    def body(rg_ref, sb_ref, wr_ref, ws_ref, wo_ref, *refs):
        gate_refs, (x_ref, mod_ref, gp_ref, g2_ref, mix_ref, r_ref, s_ref, y_ref, hres_ref, h2_ref) = refs[:2 * n_g], refs[2 * n_g:]
        for rows in _pieces(tm):
            rr = _dot(rg_ref[rows, :], wr_ref[...])
            ss = _dot(sb_ref[rows, :], ws_ref[...], _NT)
            a_r = jnp.concatenate([g[rows, :] for g in gate_refs[:n_g]], axis=1).astype(F32)
            a_s = jnp.concatenate([g[rows, :] for g in gate_refs[n_g:]], axis=1).astype(F32)
            mixed = (_sigmoid(a_r) * rr + _sigmoid(a_s) * ss).astype(BF16)
            mix_ref[rows, :] = mixed
            r_ref[rows, :] = rr.astype(BF16)
            s_ref[rows, :] = ss.astype(BF16)
            y = _dot(mixed, wo_ref[...])
            y_ref[rows, :] = y
            ny, _ = _rms(y, d)
            hres = x_ref[rows, :] + mod_ref[:, 2 * d:3 * d] * (ny * gp_ref[...])
            hres_ref[rows, :] = hres
            n2, _ = _rms(hres, d)
            h2_ref[rows, :] = (n2 * g2_ref[...] * (1.0 + mod_ref[:, 4 * d:5 * d]) + mod_ref[:, 3 * d:4 * d]).astype(BF16)

    row = lambda i: (i, 0)
    fix = lambda i: (0, 0)
    tile_bf = ((s, d), BF16, (tm, d), row)
    tile_f = ((s, d), F32, (tm, d), row)
    return _call("merge_out", body, (s // tm,),
                 [(retg, (tm, RET_V), row), (sb, (tm, SB_W), row), (w_ret, (RET_V, d), fix), (w_sb_t, (d, SB_W), fix),
                  (w_out, (d, d), fix)]
                 + [(proj, (tm, gw), functools.partial(lambda i, k: (i, ar_off + k), k=k)) for k in range(2 * n_g)]
                 + [(x, (tm, d), row), (mod, (1, 6 * d), fix), (gp1, (1, d), fix), (g2, (1, d), fix)],
                 [tile_bf, tile_bf, tile_bf, tile_f, tile_f, tile_bf])


def _ff1(h2, w_ff1_t, tm, tn):
    s, f = h2.shape[0], w_ff1_t.shape[0]
    tm = min(tm, s)

    def body(a_ref, w_ref, u_ref, act_ref):
        u = _dot(a_ref[...], w_ref[...], _NT)
        r = jnp.maximum(u, 0.0)
        u_ref[...] = u.astype(BF16)
        act_ref[...] = (r * r).astype(BF16)

    d = h2.shape[1]
    return _call("ff1", body, (f // tn, s // tm),
                 [(h2, (tm, d), lambda j, i: (i, 0)), (w_ff1_t, (tn, d), lambda j, i: (j, 0))],
                 [((s, f), BF16, (tm, tn), lambda j, i: (i, j))] * 2)


def _ff2_loss(act, w_ff2, hres, target, mod, gp2, tm):
    s, d = hres.shape
    f = act.shape[1]

    def body(a_ref, w_ref, h_ref, t_ref, mod_ref, gp_ref, dout_ref, df_ref, loss_ref, dgt_ref, dgp_ref):
        _zero_at_start([loss_ref, dgt_ref, dgp_ref])
        gt, gp = mod_ref[:, 5 * d:6 * d], gp_ref[...]
        for rows in _pieces(tm):
            ff = _dot(a_ref[rows, :], w_ref[...])
            nf, rf = _rms(ff, d)
            out = h_ref[rows, :] + gt * (nf * gp)
            err = out - t_ref[rows, :]
            sq = jnp.sum(err * err, axis=1, keepdims=True)
            loss_ref[...] += jnp.sum(sq, axis=0, keepdims=True)
            dout = err * (1.0 / d)
            dout_ref[rows, :] = dout
            dgt_ref[...] += _colsum(dout * (nf * gp))
            dgp_ref[...] += _colsum(dout * gt * nf)
            df_ref[rows, :] = _rms_bwd(dout * gt * gp, nf, rf, d).astype(BF16)

    row = lambda i: (i, 0)
    fix = lambda i: (0, 0)
    return _call("ff2_loss", body, (s // tm,),
                 [(act, (tm, f), row), (w_ff2, (f, d), fix), (hres, (tm, d), row), (target, (tm, d), row),
                  (mod, (1, 6 * d), fix), (gp2, (1, d), fix)],
                 [((s, d), F32, (tm, d), row), ((s, d), BF16, (tm, d), row), ((1, 1), F32, (1, 1), fix),
                  ((1, d), F32, (1, d), fix), ((1, d), F32, (1, d), fix)])


def _ffn_bwd(df, w_ff2, u, act, h2, tn):
    s, d = df.shape
    f = w_ff2.shape[0]

    def body(df_ref, w_ref, u_ref, act_ref, h2_ref, du_ref, gw2_ref, gw1_ref):
        dfb = df_ref[...]
        du = (_dot(dfb, w_ref[...], _NT) * (2.0 * jnp.maximum(u_ref[...].astype(F32), 0.0))).astype(BF16)
        du_ref[...] = du
        gw2_ref[...] = _dot(act_ref[...], dfb, _TN).astype(BF16)
        gw1_ref[...] = _dot(h2_ref[...], du, _TN).astype(BF16)

    fix = lambda j: (0, 0)
    col = lambda j: (0, j)
    return _call("ffn_bwd", body, (f // tn,),
                 [(df, (s, d), fix), (w_ff2, (tn, d), lambda j: (j, 0)), (u, (s, tn), col), (act, (s, tn), col),
                  (h2, (s, d), fix)],
                 [((s, f), BF16, (s, tn), col), ((f, d), BF16, (tn, d), lambda j: (j, 0)),
                  ((f // tn, d, tn), BF16, (None, d, tn), lambda j: (j, 0, 0))])


def _ff1_bwd(du, w_ff1_t, hres, dout, y, mod, g2, gp1, tm, riders=None):
    s, d = hres.shape
    f = du.shape[1]

    def body(a_ref, w_ref, h_ref, do_ref, y_ref, mod_ref, g2_ref, gp_ref,
             dh_ref, dy_ref, dsh_ref, dsc_ref, dg2_ref, dgt_ref, dgp_ref):
        _zero_at_start([dsh_ref, dsc_ref, dg2_ref, dgt_ref, dgp_ref])
        g2, sc2 = g2_ref[...], mod_ref[:, 4 * d:5 * d]
        gt, gp = mod_ref[:, 2 * d:3 * d], gp_ref[...]
        for rows in _pieces(tm):
            dh2 = _dot(a_ref[rows, :], w_ref[...])
            n2, r2 = _rms(h_ref[rows, :], d)
            dsh_ref[...] += _colsum(dh2)
            dsc_ref[...] += _colsum(dh2 * n2 * g2)
            dg2_ref[...] += _colsum(dh2 * n2 * (1.0 + sc2))
            dhres = do_ref[rows, :] + _rms_bwd(dh2 * g2 * (1.0 + sc2), n2, r2, d)
            dh_ref[rows, :] = dhres
            ny, ry = _rms(y_ref[rows, :], d)
            dgt_ref[...] += _colsum(dhres * (ny * gp))
            dgp_ref[...] += _colsum(dhres * gt * ny)
            dy_ref[rows, :] = _rms_bwd(dhres * gt * gp, ny, ry, d).astype(BF16)

    row = lambda i: (i, 0)
    fix = lambda i: (0, 0)
    vec = ((1, d), F32, (1, d), fix)
    return _call("ff1_bwd", body, (s // tm,),
                 [(du, (tm, f), row), (w_ff1_t, (f, d), fix), (hres, (tm, d), row), (dout, (tm, d), row),
                  (y, (tm, d), row), (mod, (1, 6 * d), fix), (g2, (1, d), fix), (gp1, (1, d), fix)],
                 [((s, d), F32, (tm, d), row), ((s, d), BF16, (tm, d), row), vec, vec, vec, vec, vec], riders=riders)


def _out_bwd(dy, w_out, proj, r_bf, s_bf, tm, tn, riders=None):
    s, d = dy.shape
    ar_off = (2 * RET_QK + 2 * RET_V + 3 * SB_W) // tn
    as_off = ar_off + d // tn

    def body(a_ref, w_ref, ar_ref, as_ref, r_ref, s_ref, dr_ref, ds_ref, dar_ref, das_ref):
        dm = _dot(a_ref[...], w_ref[...], _NT)
        sr, ss = _sigmoid(ar_ref[...].astype(F32)), _sigmoid(as_ref[...].astype(F32))
        dr_ref[...] = (dm * sr).astype(BF16)
        ds_ref[...] = (dm * ss).astype(BF16)
        dar_ref[...] = (dm * r_ref[...].astype(F32) * sr * (1.0 - sr)).astype(BF16)
        das_ref[...] = (dm * s_ref[...].astype(F32) * ss * (1.0 - ss)).astype(BF16)

    tile = (tm, tn)
    here = lambda j, i: (i, j)
    return _call("out_bwd", body, (d // tn, s // tm),
                 [(dy, (tm, d), lambda j, i: (i, 0)), (w_out, (tn, d), lambda j, i: (j, 0)),
                  (proj, tile, lambda j, i: (i, ar_off + j)), (proj, tile, lambda j, i: (i, as_off + j)),
                  (r_bf, tile, here), (s_bf, tile, here)],
                 [((s, d), BF16, tile, here)] * 4, riders=riders)


def _branch_bwd(d_r, d_s, retg, sb, w_ret, w_sb_t, ret, proj, gn_g, riders=None):
    s, d = d_r.shape
    n_step = 4
    part_v, part_s, part_d = RET_V // n_step, SB_W // n_step, d // n_step
    per_dev = d // N_DEV
    n_blk = part_d // per_dev
    gate_off = (2 * RET_QK + RET_V) // part_v

    def body(dr_ref, ds_ref, rg_ref, sb_ref, wr_ref, ws_ref, r_ref, g_ref, w_ref,
             dsb_ref, gwr_ref, gws_ref, dg_ref, dret_ref, dw_ref):
        i = pl.program_id(0)
        dr, ds = dr_ref[...], ds_ref[...]
        dsb_ref[...] = _dot(ds, ws_ref[...]).astype(BF16)
        gwr_ref[...] = _dot(rg_ref[...], dr, _TN).astype(BF16)
        cols = pl.ds(pl.multiple_of(i * part_d, part_d), part_d)
        gws = _dot(sb_ref[...], ds_ref[:, cols], _TN).astype(BF16)
        for k in range(n_blk):
            gws_ref[k] = gws[:, k * per_dev:(k + 1) * per_dev]
        dretg = _dot(dr, wr_ref[...], _NT)
        for h in range(part_v // RET_DV):
            cols = slice(h * RET_DV, (h + 1) * RET_DV)
            o, g, w, d_o = r_ref[:, cols], g_ref[:, cols].astype(F32), w_ref[:, cols], dretg[:, cols]
            mu = jnp.sum(o, axis=1, keepdims=True) * (1.0 / RET_DV)
            xc = o - mu
            rstd = lax.rsqrt(jnp.sum(xc * xc, axis=1, keepdims=True) * (1.0 / RET_DV) + EPS)
            n = xc * rstd
            sg = _sigmoid(g)
            silu = g * sg
            dg_ref[:, cols] = (d_o * n * w * (sg * (1.0 + g * (1.0 - sg)))).astype(BF16)
            dw_ref[:, cols] = _colsum(d_o * silu * n)
            dn = d_o * silu * w
            m1 = jnp.sum(dn, axis=1, keepdims=True) * (1.0 / RET_DV)
            m2 = jnp.sum(dn * n, axis=1, keepdims=True) * (1.0 / RET_DV)
            dret_ref[:, cols] = (rstd * (dn - m1 - n * m2)).astype(BF16)

    fix = lambda i: (0, 0)
    col = lambda i: (0, i)
    return _call("branch_bwd", body, (n_step,),
                 [(d_r, (s, d), fix), (d_s, (s, d), fix), (retg, (s, part_v), col), (sb, (s, SB_W), fix),
                  (w_ret, (part_v, d), lambda i: (i, 0)), (w_sb_t, (d, part_s), col),
                  (ret, (s, part_v), col), (proj, (s, part_v), lambda i: (0, gate_off + i)), (gn_g, (1, part_v), col)],
                 [((s, SB_W), BF16, (s, part_s), col), ((RET_V, d), BF16, (part_v, d), lambda i: (i, 0)),
                  ((N_DEV, SB_W, per_dev), BF16, (n_blk, SB_W, per_dev), lambda i: (i, 0, 0)),
                  ((s, RET_V), BF16, (s, part_v), col), ((s, RET_V), BF16, (s, part_v), col),
                  ((1, RET_V), F32, (1, part_v), col)], riders=riders)


def _ret_bwd(qk_rot, v_bf, dret, log_gamma, t, riders=None):
    s = qk_rot.shape[0]
    n_pair = HEADS // 2
    pw = 2 * RET_DV
    wq, wv = RET_PAIRS * LANES, RET_PAIRS * pw
    n_blk = s // t
    pairs = range(RET_PAIRS)

    def load(q_ref, k_ref, v_ref, do_ref):
        return ([_lanes(q_ref, p, LANES) for p in pairs], [_lanes(k_ref, p, LANES) for p in pairs],
                [_lanes(v_ref, p, pw) for p in pairs], [_lanes(do_ref, p, pw) for p in pairs])

    def d_scores(lg_ref, hp, i, qb, kb, vb, dob):
        z, w = _ret_block(lg_ref, hp, i, t, qb, kb)
        dp = jnp.concatenate([_dot(dob[:, 0:RET_DV], vb[:, 0:RET_DV], _NT),
                              _dot(dob[:, RET_DV:pw], vb[:, RET_DV:pw], _NT)], axis=0)
        return (z * w).astype(BF16), (dp * w).astype(BF16)

    def up_body(lg_ref, q_ref, k_ref, v_ref, do_ref, dq_ref, state_ref):
        hg, i = pl.program_id(0), pl.program_id(1)

        @pl.when(i == 0)
        def _():
            state_ref[...] = jnp.zeros_like(state_ref)

        qbs, kbs, vbs, dobs = load(q_ref, k_ref, v_ref, do_ref)
        dss = [d_scores(lg_ref, hg * RET_PAIRS + p, i, qbs[p], kbs[p], vbs[p], dobs[p])[1] for p in pairs]
        for p in pairs:
            dq_ref[:, p * LANES:(p + 1) * LANES] = (_dot(_side_by_side(dss[p], t), _stack_heads(kbs[p]))
                                                    + _dot(dobs[p], state_ref[p], _NT)).astype(BF16)
        for p in pairs:
            state_ref[p] += _pair_mask() * _dot(kbs[p], vbs[p], _TN)

    def down_body(lg_ref, q_ref, k_ref, v_ref, do_ref, dk_ref, dv_ref, state_ref):
        hg, i = pl.program_id(0), n_blk - 1 - pl.program_id(1)

        @pl.when(pl.program_id(1) == 0)
        def _():
            state_ref[...] = jnp.zeros_like(state_ref)

        qbs, kbs, vbs, dobs = load(q_ref, k_ref, v_ref, do_ref)
        both = [d_scores(lg_ref, hg * RET_PAIRS + p, i, qbs[p], kbs[p], vbs[p], dobs[p]) for p in pairs]
        for p in pairs:
            pp, ds = both[p]
            later = state_ref[p]
            dv_ref[:, p * pw:(p + 1) * pw] = (jnp.concatenate(
                [_dot(pp[:t], dobs[p][:, 0:RET_DV], _TN), _dot(pp[t:], dobs[p][:, RET_DV:pw], _TN)],
                axis=1) + _dot(kbs[p], later)).astype(BF16)
            dk_ref[:, p * LANES:(p + 1) * LANES] = (_dot(ds, _stack_heads(qbs[p]), _TN)
                                                    + _dot(vbs[p], later, _NT)).astype(BF16)
        for p in pairs:
            state_ref[p] += _pair_mask() * _dot(qbs[p], dobs[p], _TN)

    n_grp = n_pair // RET_PAIRS

    def ins(order):
        return [(log_gamma, None, pltpu.SMEM),
                (qk_rot, (t, wq), lambda hg, i: (order(i), hg)),
                (qk_rot, (t, wq), lambda hg, i: (order(i), n_grp + hg)),
                (v_bf, (t, wv), lambda hg, i: (order(i), hg)),
                (dret, (t, wv), lambda hg, i: (order(i), hg))]

    up = lambda i: i
    down = lambda i: n_blk - 1 - i
    scratch = [pltpu.VMEM((RET_PAIRS, LANES, pw), F32)]
    dq = _call("ret_bwd_q", up_body, (n_grp, n_blk), ins(up),
               [((s, RET_QK), BF16, (t, wq), lambda hg, i: (i, hg))], scratch=scratch)[0]
    dk, dv, *rest = _call("ret_bwd_kv", down_body, (n_grp, n_blk), ins(down),
                          [((s, RET_QK), BF16, (t, wq), lambda hg, i: (down(i), hg)),
                           ((s, RET_V), BF16, (t, wv), lambda hg, i: (down(i), hg))],
                          scratch=scratch, riders=riders)
    return [dq, dk, dv] + rest


def _sb_bwd(qkv, weights, do, tq, tk, riders=None):
    s = qkv.shape[0]
    n_pair = HEADS // 2
    _check_tiles(s, tq, tk, SB_GROUP)

    def body(q_ref, k_ref, v_ref, a_ref, do_ref, dq_ref, dk_ref, dv_ref):
        i = pl.program_id(1)

        @pl.when(i == 0)
        def _():
            dk_ref[...] = jnp.zeros_like(dk_ref)
            dv_ref[...] = jnp.zeros_like(dv_ref)

        lower = _tri(tk, False)
        qs = _stack_heads(q_ref[...])
        dos = _stack_heads(do_ref[...].astype(BF16))

        def make_step(near_diagonal, n_sub=SB_GROUP):
            def step(g, carry):
                c_e, dq = carry
                js = [g * SB_GROUP + sub for sub in range(n_sub)]
                rows = [_key_rows(j, tk) for j in js]
                zs = [_dot(qs, k_ref[rw, :], _NT) for rw in rows]
                das = [_dot(dos, v_ref[rw, :], _NT) for rw in rows]
                avals = [a_ref[j] for j in js]
                for a, rw in zip(avals, rows):
                    dv_ref[rw, :] += _dot(a, dos, _TN)
                es = [a.astype(F32) * da for a, da in zip(avals, das)]
                prefixes = [_dot(e, lower) for e in es]
                betas = [1.0 / (1.0 + jnp.exp2(-z)) for z in zs]
                for sub in range(n_sub):
                    dz = es[sub] - (es[sub] + prefixes[sub] + c_e) * betas[sub]
                    if near_diagonal:
                        dz = jnp.where(_sb_valid(i, js[sub], tq, tk), dz, 0.0)
                    dz = dz.astype(BF16)
                    dk_ref[rows[sub], :] += _dot(dz, qs, _TN)
                    dq = dq + _dot(_side_by_side(dz, tq), _stack_heads(k_ref[rows[sub], :]))
                    c_e = c_e + jnp.sum(es[sub], axis=1, keepdims=True)
                return c_e, dq
            return step

        n_full = _n_full(i, tq, tk, SB_GROUP)
        carry = (jnp.zeros((2 * tq, 1), F32), jnp.zeros((tq, LANES), F32))
        carry = lax.fori_loop(0, n_full, make_step(False), carry)
        _, dq = _diagonal_step(i, tq, tk, lambda n_sub: (lambda n, cr: make_step(True, n_sub)(n_full, cr)), carry)
        dq_ref[...] = dq

    blk = lambda hp, i: (i, hp)
    n_kb = s // tk
    return _call("sb_bwd", body, (n_pair, s // tq),
                 [(qkv, (tq, LANES), blk),
                  (qkv, (s, LANES), lambda hp, i: (0, n_pair + hp)),
                  (qkv, (s, LANES), lambda hp, i: (0, 2 * n_pair + hp)),
                  (weights, (None, None, n_kb, 2 * tq, tk), lambda hp, i: (hp, i, 0, 0, 0)),
                  (do, (tq, LANES), blk)],
                 [((s, SB_W), F32, (tq, LANES), blk),
                  ((s, SB_W), F32, (s, LANES), lambda hp, i: (0, hp)),
                  ((s, SB_W), F32, (s, LANES), lambda hp, i: (0, hp))], riders=riders)


def _assemble_dproj(dq_r, dk_r, dv_r, dg_r, dq_s, dk_s, dv_s, da_r, da_s, cos, sin, idx_col, lg_lanes, tm, riders=None):
    s, d = da_r.shape
    width = 2 * RET_QK + 2 * RET_V + 3 * SB_W + 2 * d

    def body(dq_ref, dk_ref, dv_ref, dg_ref, dqs_ref, dks_ref, dvs_ref, dar_ref, das_ref, cos_ref, sin_ref,
             idx_ref, lg_ref, o_ref):
        lane = lax.broadcasted_iota(jnp.int32, (1, LANES), 1)
        first = jnp.bitwise_and(lane, RET_DQK - 1) < (RET_DQK // 2)
        cos, sin = cos_ref[...], sin_ref[...]
        idx = idx_ref[...]
        for src, base, sign, scale in ((dq_ref, 0, 1.0, 1.0), (dk_ref, RET_QK, -1.0, RET_DQK ** -0.5)):
            for g in range(RET_QK // LANES):
                v = src[:, g * LANES:(g + 1) * LANES].astype(F32) * (_decay_scale(lg_ref, idx, g, sign) * scale)
                sw = jnp.where(first, pltpu.roll(v, LANES - RET_DQK // 2, 1), pltpu.roll(v, RET_DQK // 2, 1))
                o_ref[:, base + g * LANES:base + (g + 1) * LANES] = (v * cos - sw * sin).astype(BF16)
        off = 2 * RET_QK
        o_ref[:, off:off + RET_V] = dv_ref[...].astype(BF16)
        off += RET_V
        o_ref[:, off:off + RET_V] = dg_ref[...]
        off += RET_V
        o_ref[:, off:off + SB_W] = (dqs_ref[...] * (SB_DH ** -0.5)).astype(BF16)
        off += SB_W
        o_ref[:, off:off + SB_W] = (dks_ref[...] * LN2).astype(BF16)
        off += SB_W
        o_ref[:, off:off + SB_W] = dvs_ref[...].astype(BF16)
        off += SB_W
        o_ref[:, off:off + d] = dar_ref[...]
        off += d
        o_ref[:, off:off + d] = das_ref[...]

    row = lambda i: (i, 0)
    ins = [(a, (tm, a.shape[1]), row) for a in (dq_r, dk_r, dv_r, dg_r, dq_s, dk_s, dv_s, da_r, da_s, cos, sin, idx_col)]
    ins.append((lg_lanes, (1, RET_QK), lambda i: (0, 0)))
    return _call("assemble_dproj", body, (s // tm,), ins, [((s, width), BF16, (tm, width), row)], riders=riders)


def _in_bwd(dproj, w_in_t, x, dhres, mod, g1, tm, riders=None):
    s, d = x.shape
    width = dproj.shape[1]

    def body(a_ref, w_ref, x_ref, dh_ref, mod_ref, g_ref, dx_ref, dsh_ref, dsc_ref, dg_ref):
        _zero_at_start([dsh_ref, dsc_ref, dg_ref])
        g1, sc1 = g_ref[...], mod_ref[:, d:2 * d]
        for rows in _pieces(tm):
            dh = _dot(a_ref[rows, :], w_ref[...])
            n1, r1 = _rms(x_ref[rows, :], d)
            dsh_ref[...] += _colsum(dh)
            dsc_ref[...] += _colsum(dh * n1 * g1)
            dg_ref[...] += _colsum(dh * n1 * (1.0 + sc1))
            dx_ref[rows, :] = dh_ref[rows, :] + _rms_bwd(dh * g1 * (1.0 + sc1), n1, r1, d)

    row = lambda i: (i, 0)
    fix = lambda i: (0, 0)
    vec = ((1, d), F32, (1, d), fix)
    return _call("in_bwd", body, (s // tm,),
                 [(dproj, (tm, width), row), (w_in_t, (width, d), fix), (x, (tm, d), row), (dhres, (tm, d), row),
                  (mod, (1, 6 * d), fix), (g1, (1, d), fix)],
                 [((s, d), F32, (tm, d), row), vec, vec, vec], riders=riders)


def _adamw(w, g, m, v):
    m = ADAM_B1 * m + (1.0 - ADAM_B1) * g
    v = ADAM_B2 * v + (1.0 - ADAM_B2) * (g * g)
    m_hat = m / (1.0 - ADAM_B1 ** ADAM_STEP)
    v_hat = v / (1.0 - ADAM_B2 ** ADAM_STEP)
    delta = -ADAM_LR * (m_hat / (jnp.sqrt(v_hat) + ADAM_EPS) + ADAM_WD * w)
    return delta, m, v


def _adam_reduce(name, sets, steps):
    n = len(sets)

    def body(*refs):
        for k in range(n):
            p_ref, w_ref, m_ref, v_ref = refs[4 * k:4 * k + 4]
            outs = refs[4 * n + 4 * k:4 * n + 4 * k + 4]
            g = p_ref[0].astype(F32)
            for j in range(1, p_ref.shape[0]):
                g = g + p_ref[j].astype(F32)
            for o_ref, val in zip(outs, (g,) + _adamw(w_ref[...], g, m_ref[...], v_ref[...])):
                o_ref[...] = val

    ins, outs = [], []
    row = lambda i: (i, 0)
    for parts, w, m, v in sets:
        rws, cls = w.shape
        tr = rws // steps
        assert tr * steps == rws and tr % 16 == 0
        ins += [(parts, (parts.shape[0], tr, cls), lambda i: (0, i, 0)), (w, (tr, cls), row), (m, (tr, cls), row),
                (v, (tr, cls), row)]
        outs += [((rws, cls), F32, (tr, cls), row)] * 4
    res = _call(name, body, (steps,), ins, outs)
    return [res[4 * k:4 * k + 4] for k in range(n)]


def _ada_bwd_adam(cs_t, dmod_cols, w, m, v):
    d, nc = w.shape

    def body(c_ref, dm_ref, w_ref, m_ref, v_ref, g_out, d_out, m_out, v_out):
        g = c_ref[0] * dm_ref[0:1, :]
        for r in range(1, N_DEV):
            g = g + c_ref[r] * dm_ref[r:r + 1, :]
        delta, mn, vn = _adamw(w_ref[...], g, m_ref[...], v_ref[...])
        g_out[...] = g
        d_out[...] = delta
        m_out[...] = mn
        v_out[...] = vn

    fix = lambda i: (0, 0)
    blk = (d, nc)
    return _call("ada_bwd_adam", body, (1,),
                 [(cs_t, (N_DEV, d, 1), lambda i: (0, 0, 0)), (dmod_cols, (N_DEV, nc), fix), (w, blk, fix), (m, blk, fix), (v, blk, fix)],
                 [((d, nc), F32, blk, fix)] * 4)


def _small_adam(parts, ws, ms, vs):
    n = len(ws)
    widths = [w.shape[1] for w in ws]
    total = parts.shape[1]
    assert sum(widths) + LANES == total

    def body(p_ref, *refs):
        w_refs, m_refs, v_refs = refs[:n], refs[n:2 * n], refs[2 * n:3 * n]
        outs = refs[3 * n:]
        g = p_ref[0:1, :]
        for k in range(1, N_DEV):
            g = g + p_ref[k:k + 1, :]
        off = 0
        for i, width in enumerate(widths):
            gi = g[:, off:off + width]
            delta, mn, vn = _adamw(w_refs[i][...], gi, m_refs[i][...], v_refs[i][...])
            for o_ref, val in zip(outs[4 * i:4 * i + 4], (gi, delta, mn, vn)):
                o_ref[...] = val
            off += width
        outs[4 * n][...] = g[:, off:off + LANES]

    fix = lambda i: (0, 0)
    vec = lambda a: (a, (1, a.shape[1]), fix)
    out_specs = [((1, width), F32, (1, width), fix) for width in widths for _ in range(4)]
    out_specs.append(((1, LANES), F32, (1, LANES), fix))
    res = _call("small_adam", body, (1,),
                [(parts, (N_DEV, total), fix)] + [vec(a) for a in list(ws) + list(ms) + list(vs)], out_specs)
    return [res[4 * i:4 * i + 4] for i in range(n)], res[4 * n]


def kernel(x, c, positions, ada_w, ada_b, pre_mix_g, post_mix_g, pre_ffn_g, post_ffn_g, w_in, ret_gn_g, w_ret_branch, w_sb_branch, w_out, w_ff1, w_ff2, loss_target, m_ada_w, m_ada_b, m_pre_mix_g, m_post_mix_g, m_pre_ffn_g, m_post_ffn_g, m_w_in, m_ret_gn_g, m_w_ret_branch, m_w_sb_branch, m_w_out, m_w_ff1, m_w_ff2, v_ada_w, v_ada_b, v_pre_mix_g, v_post_mix_g, v_pre_ffn_g, v_post_ffn_g, v_w_in, v_ret_gn_g, v_w_ret_branch, v_w_sb_branch, v_w_out, v_w_ff1, v_w_ff2):
    _, s, d = x.shape
    d_ff = w_ff1.shape[2] * N_DEV
    d_in = w_in.shape[2] * N_DEV
    me = 4 * lax.axis_index("x") + 2 * lax.axis_index("y") + lax.axis_index("c")
    x2, tgt = x[0], loss_target[0]

    core = lax.axis_index("c").astype(jnp.int32).reshape(1)
    bf = lambda w: w[0].astype(BF16)

    w_in_t, m_in_t, v_in_t = (jnp.swapaxes(a[0], 0, 1) for a in (w_in, m_w_in, v_w_in))

    c_all, g_in = _exchange("gather_in", [c, w_in_t.astype(BF16)], ["gather", "gather_chip"])
    c_all = c_all.reshape(N_DEV, d)

    n_ada = ada_w.shape[2]
    ada_b_cols = lax.dynamic_slice(ada_b, (0, me * n_ada), (1, n_ada))
    cs_all, mod_cols = _ada_fwd(c_all, ada_w[0], ada_b_cols)
    mod_all = _exchange("gather_mod", [mod_cols], ["gather"])[0]
    mod = lax.dynamic_index_in_dim(mod_all, me, axis=1, keepdims=False).reshape(1, 6 * d)

    tm = min(256, s)
    h, g_in = _pre_norm(x2, pre_mix_g, mod, 2 * tm, riders=([g_in], ["forward"]))
    wt_in = g_in.reshape(d_in, d)
    bf_t = lambda w: jnp.swapaxes(w[0], 0, 1).astype(BF16)
    small_w = [bf(w_ret_branch), bf_t(w_sb_branch), bf(w_out)]
    proj, *small_w = _matmul("in_proj", h, wt_in, "nt", s, 512, BF16, riders=(small_w, ["gather_chip"] * 3))
    pos_col = positions.reshape(s, 1).astype(F32)
    freqs = ROPE_BASE ** (-jnp.arange(0, RET_DQK, 2, dtype=F32) / RET_DQK)
    inv_freq = jnp.tile(freqs, LANES // (RET_DQK // 2)).reshape(1, LANES)
    log_gamma_np = np.log1p(-(2.0 ** (-5.0 - np.arange(HEADS))))
    log_gamma = jnp.asarray(log_gamma_np, F32)
    lg_lanes = jnp.asarray(np.repeat(log_gamma_np, RET_DQK).reshape(1, RET_QK), F32)
    idx_col = (jnp.arange(s, dtype=F32) - (s // 2)).reshape(s, 1)
    qk_rot, v_bf, qkv_sb, cos_t, sin_t = _prep(proj, pos_col, idx_col, inv_freq, lg_lanes, 2 * tm)
    tq, tk = min(256, s), min(128, s)
    tq_sb = min(SB_TQ, s)
    sb, sb_weights, *big_w = _sb_fwd(qkv_sb, tq_sb, tk, riders=([bf(w_ff2), bf_t(w_ff1)], ["gather_chip"] * 2))
    ret, retg, g_ret, g_sb, g_out, g_ff2, g_ff1 = _ret_fwd(qk_rot, v_bf, proj, ret_gn_g, log_gamma, tq,
                                                           riders=(small_w + big_w, ["forward"] * 5))
    wf_ret = g_ret.reshape(RET_V, d)
    wt_sb = g_sb.reshape(d, SB_W)
    wf_out = g_out.reshape(d, d)
    wt_ff1 = g_ff1.reshape(d_ff, d)
    wf_ff2 = g_ff2.reshape(d_ff, d)
    mixed, r_bf, s_bf, y, hres, h2 = _merge_out(retg, sb, wf_ret, wt_sb, wf_out, proj, x2, mod, post_mix_g, pre_ffn_g, tm)
    u, act = _ff1(h2, wt_ff1, s, 512)
    dout, df, loss_sum, d_gt2, d_gp2 = _ff2_loss(act, wf_ff2, hres, tgt, mod, post_ffn_g, tm)

    du, gw_ff2, gw_ff1 = _ffn_bwd(df, wf_ff2, u, act, h2, d_ff // N_DEV)
    gw_ff2 = gw_ff2.reshape(N_DEV, d_ff // N_DEV, d)
    dhres, dy, d_sh2, d_sc2, d_g2, d_gt1, d_gp1, t_ff1, t_ff2 = _ff1_bwd(
        du, wt_ff1, hres, dout, y, mod, pre_ffn_g, post_mix_g, tm, riders=([gw_ff1, gw_ff2], ["pair"] * 2))
    s_ff1, s_ff2 = _pair_sum("pair_sum_ff", [(gw_ff1, t_ff1), (gw_ff2, t_ff2)], core)
    gw_out = _matmul("grad_w_out", mixed, dy, "tn", 512, d, BF16).reshape(N_DEV, d // N_DEV, d)
    d_r, d_s, da_r, da_s = _out_bwd(dy, wf_out, proj, r_bf, s_bf, 2 * tm, min(512, d))
    dsb, gw_ret, gw_sb, dg_r, dret, d_gn, p_out = _branch_bwd(d_r, d_s, retg, sb, wf_ret, wt_sb, ret, proj, ret_gn_g,
                                                              riders=([gw_out], ["scatter"]))
    gw_ret = gw_ret.reshape(N_DEV, RET_V // N_DEV, d)
    dq_s, dk_s, dv_s, p_ff1, p_ff2 = _sb_bwd(qkv_sb, sb_weights, dsb, tq_sb, tk,
                                             riders=([s_ff1, s_ff2], ["chip_scatter"] * 2))
    dq_r, dk_r, dv_r, p_sb = _ret_bwd(qk_rot, v_bf, dret, log_gamma, tq, riders=([gw_sb], ["scatter"]))
    dproj = _assemble_dproj(dq_r, dk_r, dv_r, dg_r, dq_s, dk_s, dv_s, da_r, da_s, cos_t, sin_t, idx_col, lg_lanes, tm)[0]
    gw_in, p_ret = _matmul("grad_w_in", dproj, h, "tn", 512, d, BF16, riders=([gw_ret], ["scatter"]))
    gw_in = gw_in.reshape(N_DEV, d_in // N_DEV, d)
    t_in = _exchange("pair_in", [gw_in], ["pair"])[0]
    s_in = _pair_sum("pair_sum_in", [(gw_in, t_in)], core)[0]
    grad_x, d_sh1, d_sc1, d_g1, p_in = _in_bwd(dproj, wt_in, x2, dhres, mod, pre_mix_g, tm,
                                               riders=([s_in], ["chip_scatter"]))
    loss_lanes = jnp.pad(loss_sum, ((0, 0), (0, LANES - 1)))
    small = jnp.concatenate([d_sh1, d_sc1, d_gt1, d_sh2, d_sc2, d_gt2, d_g1, d_gp1, d_g2, d_gp2, d_gn, loss_lanes], axis=1)
    small_all = _exchange("gather_small", [small], ["gather"])[0].reshape(N_DEV, small.shape[1])
    parts = [p_in, p_ret, p_sb, p_out, p_ff1, p_ff2]

    res = {}
    names = ["w_ret_branch", "w_sb_branch", "w_out", "w_ff1", "w_ff2"]
    ws = [w_ret_branch, w_sb_branch, w_out, w_ff1, w_ff2]
    ms = [m_w_ret_branch, m_w_sb_branch, m_w_out, m_w_ff1, m_w_ff2]
    vs = [v_w_ret_branch, v_w_sb_branch, v_w_out, v_w_ff1, v_w_ff2]
    sets = [(p, w[0], m[0], v[0]) for p, w, m, v in zip(parts[1:], ws, ms, vs)]
    for nm, outs4 in zip(names, _adam_reduce("adam_rest", sets, 2)):
        res[nm] = [o[None] for o in outs4]
    res["w_in"] = [jnp.swapaxes(o, 0, 1)[None]
                   for o in _adam_reduce("adam_w_in", [(parts[0], w_in_t, m_in_t, v_in_t)], 2)[0]]
    dmod_cols = lax.dynamic_slice(small_all, (0, me * n_ada), (N_DEV, n_ada))
    res["ada_w"] = [o[None] for o in _ada_bwd_adam(cs_all.reshape(N_DEV, d, 1), dmod_cols, ada_w[0], m_ada_w[0], v_ada_w[0])]
    vec_names = ["ada_b", "pre_mix_g", "post_mix_g", "pre_ffn_g", "post_ffn_g", "ret_gn_g"]
    vec_res, loss_lanes = _small_adam(small_all,
                                      [ada_b, pre_mix_g, post_mix_g, pre_ffn_g, post_ffn_g, ret_gn_g],
                                      [m_ada_b, m_pre_mix_g, m_post_mix_g, m_pre_ffn_g, m_post_ffn_g, m_ret_gn_g],
                                      [v_ada_b, v_pre_mix_g, v_post_mix_g, v_pre_ffn_g, v_post_ffn_g, v_ret_gn_g])
    res.update(zip(vec_names, vec_res))
    loss = (0.5 / d) * loss_lanes[0, 0]
    order = ["ada_w", "ada_b", "pre_mix_g", "post_mix_g", "pre_ffn_g", "post_ffn_g", "w_in", "ret_gn_g",
             "w_ret_branch", "w_sb_branch", "w_out", "w_ff1", "w_ff2"]
    outs = [loss, grad_x[None]]
    for k in range(4):
        outs += [res[nm][k] for nm in order]
    return tuple(outs)
```

```python
import functools

import numpy as np
import jax
import jax.numpy as jnp
from jax import lax
from jax.experimental import pallas as pl
from jax.experimental.pallas import tpu as pltpu

F32 = jnp.float32
BF16 = jnp.bfloat16
N_DEV = 8
AXES = ("x", "y", "c")

EPS = 1e-6
CHUNK = 64
CHUNK_SHIFT = 6
HEADS = 8
RET_DQK = 64
RET_DV = 128
SB_DH = 64
RET_QK = HEADS * RET_DQK
RET_V = HEADS * RET_DV
SB_W = HEADS * SB_DH
ROPE_BASE = 10000.0
LANES = 128

ADAM_LR = 0.001
ADAM_B1 = 0.9
ADAM_B2 = 0.999
ADAM_EPS = 1e-08
ADAM_WD = 0.01
ADAM_STEP = 10

VMEM_LIMIT = 56 * 1024 * 1024

_NN = (((1,), (0,)), ((), ()))
_NT = (((1,), (1,)), ((), ()))
_TN = (((0,), (0,)), ((), ()))


def _dot(a, b, dims=_NN):
    if a.dtype != BF16:
        a = a.astype(BF16)
    if b.dtype != BF16:
        b = b.astype(BF16)
    return lax.dot_general(a, b, dims, preferred_element_type=F32)


def _sigmoid(x):
    return 1.0 / (1.0 + jnp.exp(-x))


def _rms(x, d):
    r = lax.rsqrt(jnp.sum(x * x, axis=1, keepdims=True) * (1.0 / d) + EPS)
    return x * r, r


def _rms_bwd(dn, n, r, d):
    return r * (dn - n * (jnp.sum(dn * n, axis=1, keepdims=True) * (1.0 / d)))


def _colsum(v):
    return jnp.sum(v, axis=0, keepdims=True)


ROW_SPLIT = 2


def _zero_at_start(refs):
    @pl.when(pl.program_id(0) == 0)
    def _():
        for r in refs:
            r[...] = jnp.zeros_like(r)


def _pieces(tm):
    step = tm // ROW_SPLIT
    return [slice(k * step, (k + 1) * step) for k in range(ROW_SPLIT)]


KIND_SLOTS = {"gather": N_DEV, "scatter": N_DEV, "gather_chip": N_DEV, "forward": N_DEV, "pair": N_DEV // 2,
              "chip_scatter": N_DEV // 2}
SEMS_PER_ARRAY = N_DEV - 1


def _exchange_copies(ins, outs, send_sems, recv_sems, local_sems, kinds):
    x, y, c = (lax.axis_index(a) for a in AXES)
    me, chip, sibling = 4 * x + 2 * y + c, 2 * x + y, (x, y, 1 - c)
    mesh_id = pl.DeviceIdType.MESH
    other_chips = []
    for k in range(1, N_DEV // 2):
        px = 1 - x if k & 2 else x
        py = 1 - y if k & 1 else y
        other_chips.append((px, py))
    copies = []
    for i, kind in enumerate(kinds):
        def remote(src, dst, k, to, i=i):
            return pltpu.make_async_remote_copy(
                src_ref=src, dst_ref=dst, send_sem=send_sems.at[i * SEMS_PER_ARRAY + k],
                recv_sem=recv_sems.at[i * SEMS_PER_ARRAY + k], device_id=to, device_id_type=mesh_id)

        if kind in ("gather", "scatter"):
            pick = (lambda ref, d: ref.at[d]) if kind == "scatter" else (lambda ref, d: ref)
            copies.append(pltpu.make_async_copy(pick(ins[i], me), outs[i].at[me], local_sems.at[i]))
            for k in range(1, N_DEV):
                to = (1 - x if k & 4 else x, 1 - y if k & 2 else y, 1 - c if k & 1 else c)
                copies.append(remote(pick(ins[i], 4 * to[0] + 2 * to[1] + to[2]), outs[i].at[me], k - 1, to))
        elif kind == "gather_chip":
            copies.append(pltpu.make_async_copy(ins[i], outs[i].at[me], local_sems.at[i]))
            copies.append(remote(ins[i], outs[i].at[me], 0, sibling))
            for k, (px, py) in enumerate(other_chips):
                copies.append(remote(ins[i], outs[i].at[me], 1 + k, (px, py, c)))
        elif kind == "forward":
            for k, (px, py) in enumerate(other_chips):
                slot = 4 * px + 2 * py + c
                copies.append(remote(outs[i].at[slot], outs[i].at[slot], k, sibling))
        elif kind == "pair":
            for k in range(N_DEV // 2):
                copies.append(remote(ins[i].at[2 * k + 1 - c], outs[i].at[k], k, sibling))
        elif kind == "chip_scatter":
            copies.append(pltpu.make_async_copy(ins[i].at[chip], outs[i].at[chip], local_sems.at[i]))
            for k, (px, py) in enumerate(other_chips):
                copies.append(remote(ins[i].at[2 * px + py], outs[i].at[chip], k, (px, py, c)))
        else:
            raise ValueError(kind)
    return copies


def _exchange_shapes(arrays, kinds):
    shapes = []
    for a, kind in zip(arrays, kinds):
        tail = a.shape if kind in ("gather", "gather_chip") else a.shape[1:]
        shapes.append(jax.ShapeDtypeStruct((KIND_SLOTS[kind],) + tuple(tail), a.dtype))
    return shapes


def _exchange_sems(n):
    return [pltpu.SemaphoreType.DMA((n * SEMS_PER_ARRAY,)), pltpu.SemaphoreType.DMA((n * SEMS_PER_ARRAY,)),
            pltpu.SemaphoreType.DMA((n,))]


def _call(name, body, grid, ins, outs, scratch=(), riders=None, prefetch=None):
    any_spec = pl.BlockSpec(memory_space=pl.ANY)
    in_specs = [pl.BlockSpec(memory_space=im) if bs is None else pl.BlockSpec(bs, im) for _, bs, im in ins]
    out_specs = [pl.BlockSpec(bs, im) for _, _, bs, im in outs]
    out_shape = [jax.ShapeDtypeStruct(s, d) for s, d, _, _ in outs]
    operands = [a for a, _, _ in ins]
    scratch = list(scratch)
    aliases = {}
    n_pre = 0 if prefetch is None else 1
    kernel = functools.partial(body) if prefetch is None else (lambda _, *refs: body(*refs))
    if riders is not None:
        arrays, kinds = riders
        nr, n_in, n_out, n_scr = len(arrays), len(ins), len(outs), len(scratch)

        def kernel(*refs):
            refs = refs[n_pre:]
            own_in, ride_in = refs[:n_in], refs[n_in:n_in + nr]
            own_out = refs[n_in + nr:n_in + nr + n_out]
            ride_out = refs[n_in + nr + n_out:n_in + 2 * nr + n_out]
            own_scr = refs[n_in + 2 * nr + n_out:n_in + 2 * nr + n_out + n_scr]
            sems = refs[n_in + 2 * nr + n_out + n_scr:]
            ids = [pl.program_id(a) for a in range(len(grid))]
            first = functools.reduce(jnp.logical_and, [i == 0 for i in ids])
            last = functools.reduce(jnp.logical_and, [i == g - 1 for i, g in zip(ids, grid)])

            @pl.when(first)
            def _():
                for cp in _exchange_copies(ride_in, ride_out, *sems, kinds):
                    cp.start()

            body(*own_in, *own_out, *own_scr)

            @pl.when(last)
            def _():
                for cp in _exchange_copies(ride_in, ride_out, *sems, kinds):
                    cp.wait()

        in_specs += [any_spec] * nr
        out_specs += [any_spec] * nr
        out_shape += _exchange_shapes(arrays, kinds)
        operands += list(arrays)
        scratch += _exchange_sems(nr)
        aliases = {n_pre + n_in + r: n_out + r for r, kind in enumerate(kinds) if kind == "forward"}
    params = pltpu.CompilerParams(dimension_semantics=("arbitrary",) * len(grid), vmem_limit_bytes=VMEM_LIMIT)
    if prefetch is None:
        return pl.pallas_call(kernel, name=name, grid=grid, in_specs=in_specs, out_specs=out_specs,
                              out_shape=out_shape, scratch_shapes=scratch, input_output_aliases=aliases,
                              compiler_params=params)(*operands)
    grid_spec = pltpu.PrefetchScalarGridSpec(num_scalar_prefetch=1, grid=grid, in_specs=in_specs,
                                             out_specs=out_specs, scratch_shapes=scratch)
    return pl.pallas_call(kernel, name=name, grid_spec=grid_spec, out_shape=out_shape,
                          input_output_aliases=aliases, compiler_params=params)(prefetch, *operands)


def _exchange(name, arrays, kinds):
    n = len(arrays)

    def body(*refs):
        copies = _exchange_copies(refs[:n], refs[n:2 * n], *refs[2 * n:], kinds)
        for cp in copies:
            cp.start()
        for cp in copies:
            cp.wait()

    any_spec = pl.BlockSpec(memory_space=pl.ANY)
    return pl.pallas_call(
        functools.partial(body),
        name=name,
        in_specs=[any_spec] * n,
        out_specs=[any_spec] * n,
        out_shape=_exchange_shapes(arrays, kinds),
        scratch_shapes=_exchange_sems(n),
        input_output_aliases={i: i for i, kind in enumerate(kinds) if kind == "forward"},
    )(*arrays)


def _pair_sum(name, pairs, my_core):
    n = len(pairs)

    def body(*refs):
        for k in range(n):
            a_ref, b_ref, o_ref = refs[2 * k], refs[2 * k + 1], refs[2 * n + k]
            o_ref[...] = (a_ref[...].astype(F32) + b_ref[...].astype(F32)).astype(o_ref.dtype)

    ins, outs = [], []
    for mine, theirs in pairs:
        _, rws, cls = mine.shape
        ins += [(mine, (None, rws, cls), lambda k, core: (2 * k + core[0], 0, 0)),
                (theirs, (None, rws, cls), lambda k, core: (k, 0, 0))]
        outs.append(((N_DEV // 2, rws, cls), mine.dtype, (None, rws, cls), lambda k, core: (k, 0, 0)))
    return _call(name, body, (N_DEV // 2,), ins, outs, prefetch=my_core)


def _matmul(name, a, b, kind, tm, tn, out_dtype, blocked_out=False, riders=None):
    if kind == "tn":
        kdim, m = a.shape
    else:
        m, kdim = a.shape
    n = b.shape[0] if kind == "nt" else b.shape[1]
    tm, tn = min(tm, m), min(tn, n)
    dims = {"nn": _NN, "nt": _NT, "tn": _TN}[kind]

    def body(a_ref, b_ref, o_ref):
        o_ref[...] = _dot(a_ref[...], b_ref[...], dims).astype(o_ref.dtype)

    a_spec = (a, (kdim, tm), lambda j, i: (0, i)) if kind == "tn" else (a, (tm, kdim), lambda j, i: (i, 0))
    b_spec = (b, (tn, kdim), lambda j, i: (j, 0)) if kind == "nt" else (b, (kdim, tn), lambda j, i: (0, j))
    if blocked_out:
        out = ((n // tn, m, tn), out_dtype, (None, tm, tn), lambda j, i: (j, i, 0))
    else:
        out = ((m, n), out_dtype, (tm, tn), lambda j, i: (i, j))
    res = _call(name, body, (n // tn, m // tm), [a_spec, b_spec], [out], riders=riders)
    return res[0] if riders is None else res


def _ada_fwd(c_all, ada_w, ada_b_cols):
    def body(c_ref, w_ref, b_ref, cs_ref, o_ref):
        v = c_ref[...]
        cs = v * _sigmoid(v)
        cs_ref[...] = cs
        o_ref[...] = lax.dot_general(cs, w_ref[...], _NN, preferred_element_type=F32,
                                     precision=lax.Precision.HIGHEST) + b_ref[...]

    r, d = c_all.shape
    nc = ada_w.shape[1]
    fix = lambda i: (0, 0)
    return _call("ada_fwd", body, (1,),
                 [(c_all, (r, d), fix), (ada_w, (d, nc), fix), (ada_b_cols, (1, nc), fix)],
                 [((r, d), F32, (r, d), fix), ((r, nc), F32, (r, nc), fix)])


def _pre_norm(x, g, mod, tm, riders=None):
    s, d = x.shape

    def body(x_ref, g_ref, mod_ref, h_ref):
        n, _ = _rms(x_ref[...], d)
        sh, sc = mod_ref[:, 0:d], mod_ref[:, d:2 * d]
        h_ref[...] = (n * g_ref[...] * (1.0 + sc) + sh).astype(BF16)

    return _call("pre_norm", body, (s // tm,),
                 [(x, (tm, d), lambda i: (i, 0)), (g, (1, d), lambda i: (0, 0)),
                  (mod, (1, 6 * d), lambda i: (0, 0))],
                 [((s, d), BF16, (tm, d), lambda i: (i, 0))], riders=riders)


LOG2E = 1.4426950408889634
LN2 = 0.6931471805599453


def _decay_scale(lg_ref, idx, g, sign):
    return jnp.exp((sign * idx) * lg_ref[:, g * LANES:(g + 1) * LANES])


def _prep(proj, pos_col, idx_col, inv_freq, lg_lanes, tm):
    s = proj.shape[0]
    sb_off = (2 * RET_QK + 2 * RET_V) // (3 * SB_W)
    n_q = RET_QK // LANES

    def body(qk_ref, v_ref, sb_ref, pos_ref, idx_ref, f_ref, lg_ref, qk_out, v_out, sb_out, cos_out, sin_out):
        ang = pos_ref[...] * f_ref[...]
        lane = lax.broadcasted_iota(jnp.int32, (1, LANES), 1)
        first = jnp.bitwise_and(lane, RET_DQK - 1) < (RET_DQK // 2)
        cos = jnp.cos(ang)
        sin = jnp.where(first, -1.0, 1.0) * jnp.sin(ang)
        cos_out[...] = cos
        sin_out[...] = sin
        idx = idx_ref[...]
        for g in range(2 * n_q):
            v = qk_ref[:, g * LANES:(g + 1) * LANES].astype(F32)
            sw = jnp.where(first, pltpu.roll(v, LANES - RET_DQK // 2, 1), pltpu.roll(v, RET_DQK // 2, 1))
            r = v * cos + sw * sin
            if g < n_q:
                r = r * _decay_scale(lg_ref, idx, g, 1.0)
            else:
                r = r * (_decay_scale(lg_ref, idx, g - n_q, -1.0) * (RET_DQK ** -0.5))
            qk_out[:, g * LANES:(g + 1) * LANES] = r.astype(BF16)
        v_out[...] = v_ref[...].astype(BF16)
        sb_out[:, 0:SB_W] = (sb_ref[:, 0:SB_W].astype(F32) * (SB_DH ** -0.5 * LOG2E)).astype(BF16)
        sb_out[:, SB_W:3 * SB_W] = sb_ref[:, SB_W:3 * SB_W].astype(BF16)

    return _call("prep", body, (s // tm,),
                 [(proj, (tm, 2 * RET_QK), lambda i: (i, 0)),
                  (proj, (tm, RET_V), lambda i: (i, 2 * RET_QK // RET_V)),
                  (proj, (tm, 3 * SB_W), lambda i: (i, sb_off)),
                  (pos_col, (tm, 1), lambda i: (i, 0)),
                  (idx_col, (tm, 1), lambda i: (i, 0)),
                  (inv_freq, (1, LANES), lambda i: (0, 0)),
                  (lg_lanes, (1, RET_QK), lambda i: (0, 0))],
                 [((s, 2 * RET_QK), BF16, (tm, 2 * RET_QK), lambda i: (i, 0)),
                  ((s, RET_V), BF16, (tm, RET_V), lambda i: (i, 0)),
                  ((s, 3 * SB_W), BF16, (tm, 3 * SB_W), lambda i: (i, 0)),
                  ((s, LANES), F32, (tm, LANES), lambda i: (i, 0)),
                  ((s, LANES), F32, (tm, LANES), lambda i: (i, 0))])


def _head_mask(hh):
    lane = lax.broadcasted_iota(jnp.int32, (1, LANES), 1)
    return (lane >= RET_DQK) if hh else (lane < RET_DQK)


def _masked(v, m):
    return jnp.where(m, v, jnp.zeros_like(v))


SB_GROUP = 4
SB_TQ = 256


def _stack_heads(v):
    return jnp.concatenate([_masked(v, _head_mask(0)), _masked(v, _head_mask(1))], axis=0)


def _side_by_side(v, t):
    return jnp.concatenate([v[:t], v[t:]], axis=1)


def _tile_pos(i, j, tq, tk):
    row = jnp.bitwise_and(lax.broadcasted_iota(jnp.int32, (2 * tq, tk), 0), tq - 1) + i * tq
    col = lax.broadcasted_iota(jnp.int32, (2 * tq, tk), 1) + j * tk
    return row, col


def _n_groups(i, tq, tk, grp):
    return ((i + 1) * (tq // tk) + grp - 1) // grp


def _n_full(i, tq, tk, grp):
    return (i * (tq // tk)) // grp


def _key_rows(j, tk):
    return pl.ds(pl.multiple_of(j * tk, tk), tk)


def _ret_weight(lg_rows, i, j, tq, tk):
    row, col = _tile_pos(i, j, tq, tk)
    same = jnp.right_shift(col, CHUNK_SHIFT) == jnp.right_shift(row, CHUNK_SHIFT)
    later = jnp.where(same, jnp.exp((2.0 * lg_rows) * (col - row).astype(F32)), 0.0)
    return jnp.where(col <= row, 1.0, later)


def _lg_rows(lg_ref, hp, tq):
    first = lax.broadcasted_iota(jnp.int32, (2 * tq, 1), 0) < tq
    return jnp.where(first, lg_ref[2 * hp], lg_ref[2 * hp + 1])


def _check_tiles(s, tq, tk, grp):
    assert tq % tk == 0 and tq & (tq - 1) == 0 and tk & (tk - 1) == 0
    assert s % tq == 0 and (s // tk) % grp == 0 and s // tk <= LANES


def _pair_mask():
    r = lax.broadcasted_iota(jnp.int32, (LANES, 2 * RET_DV), 0) >= RET_DQK
    c = lax.broadcasted_iota(jnp.int32, (LANES, 2 * RET_DV), 1) >= RET_DV
    return (r == c).astype(F32)


def _ret_block(lg_ref, hp, i, t, qb, kb):
    w = _ret_weight(_lg_rows(lg_ref, hp, t), i, i, t, t)
    return _dot(_stack_heads(qb), kb, _NT), w


RET_PAIRS = 4


def _lanes(ref, p, width):
    return ref[:, p * width:(p + 1) * width]


def _ret_fwd(qk_rot, v_bf, proj, gn_g, log_gamma, t, riders=None):
    s = qk_rot.shape[0]
    n_pair = HEADS // 2
    pw = 2 * RET_DV
    wq, wv = RET_PAIRS * LANES, RET_PAIRS * pw
    gate_off = (2 * RET_QK + RET_V) // wv
    assert s % t == 0 and t % CHUNK == 0 and t & (t - 1) == 0 and n_pair % RET_PAIRS == 0

    def body(lg_ref, q_ref, k_ref, v_ref, g_ref, w_ref, ret_ref, rg_ref, state_ref):
        hg, i = pl.program_id(0), pl.program_id(1)

        @pl.when(i == 0)
        def _():
            state_ref[...] = jnp.zeros_like(state_ref)

        pairs = range(RET_PAIRS)
        qbs = [_lanes(q_ref, p, LANES) for p in pairs]
        kbs = [_lanes(k_ref, p, LANES) for p in pairs]
        vbs = [_lanes(v_ref, p, pw) for p in pairs]
        zws = [_ret_block(lg_ref, hg * RET_PAIRS + p, i, t, qbs[p], kbs[p]) for p in pairs]
        ps = [(z * w).astype(BF16) for z, w in zws]
        outs = [jnp.concatenate([_dot(ps[p][:t], vbs[p][:, 0:RET_DV]), _dot(ps[p][t:], vbs[p][:, RET_DV:pw])], axis=1)
                + _dot(qbs[p], state_ref[p]) for p in pairs]
        for p in pairs:
            state_ref[p] += _pair_mask() * _dot(kbs[p], vbs[p], _TN)
        for p in pairs:
            for hh in range(2):
                cols = slice(p * pw + hh * RET_DV, p * pw + (hh + 1) * RET_DV)
                o = outs[p][:, hh * RET_DV:(hh + 1) * RET_DV]
                ret_ref[:, cols] = o
                mu = jnp.sum(o, axis=1, keepdims=True) * (1.0 / RET_DV)
                xc = o - mu
                var = jnp.sum(xc * xc, axis=1, keepdims=True) * (1.0 / RET_DV)
                nrm = xc * lax.rsqrt(var + EPS) * w_ref[:, cols]
                g = g_ref[:, cols].astype(F32)
                rg_ref[:, cols] = (g * _sigmoid(g) * nrm).astype(BF16)

    blk = lambda hg, i: (i, hg)
    return _call("ret_fwd", body, (n_pair // RET_PAIRS, s // t),
                 [(log_gamma, None, pltpu.SMEM),
                  (qk_rot, (t, wq), blk),
                  (qk_rot, (t, wq), lambda hg, i: (i, n_pair // RET_PAIRS + hg)),
                  (v_bf, (t, wv), blk),
                  (proj, (t, wv), lambda hg, i: (i, gate_off + hg)),
                  (gn_g, (1, wv), lambda hg, i: (0, hg))],
                 [((s, RET_V), F32, (t, wv), blk), ((s, RET_V), BF16, (t, wv), blk)],
                 scratch=[pltpu.VMEM((RET_PAIRS, LANES, pw), F32)], riders=riders)


def _tri(tk, strict_upper):
    r = lax.broadcasted_iota(jnp.int32, (tk, tk), 0)
    cc = lax.broadcasted_iota(jnp.int32, (tk, tk), 1)
    return ((r > cc) if strict_upper else (r < cc)).astype(BF16)


def _diagonal_step(i, tq, tk, make, carry):
    if (tq // tk) % SB_GROUP == 0:
        return make(SB_GROUP)(0, carry)
    assert 2 * (tq // tk) == SB_GROUP
    half = lax.rem(i, 2) == 0
    return lax.cond(half, lambda cr: make(SB_GROUP // 2)(0, cr), lambda cr: make(SB_GROUP)(0, cr), carry)


def _sb_valid(i, j, tq, tk):
    row, col = _tile_pos(i, j, tq, tk)
    return col < row


def _sb_fwd(qkv, tq, tk, riders=None):
    s = qkv.shape[0]
    n_pair = HEADS // 2
    _check_tiles(s, tq, tk, SB_GROUP)

    def body(q_ref, k_ref, v_ref, o_ref, a_ref):
        i = pl.program_id(1)
        upper = _tri(tk, True)
        qs = _stack_heads(q_ref[...])
        n_full, n_groups = _n_full(i, tq, tk, SB_GROUP), _n_groups(i, tq, tk, SB_GROUP)

        def make_step(near_diagonal, last, n_sub=SB_GROUP):
            def step(n, carry):
                c, o = carry
                g = last - 1 - n
                js = [g * SB_GROUP + sub for sub in range(n_sub)]
                zs = [_dot(qs, k_ref[_key_rows(j, tk), :], _NT) for j in js]
                log1ps = [jnp.log2(1.0 + jnp.exp2(-jnp.abs(z))) for z in zs]
                log_1ms = [-jnp.maximum(z, 0.0) - t for z, t in zip(zs, log1ps)]
                log_bs = [jnp.minimum(z, 0.0) - t for z, t in zip(zs, log1ps)]
                if near_diagonal:
                    valids = [_sb_valid(i, j, tq, tk) for j in js]
                    log_1ms = [jnp.where(v, l, 0.0) for v, l in zip(valids, log_1ms)]
                sticks = [_dot(l, upper) for l in log_1ms]
                sums = [jnp.sum(l, axis=1, keepdims=True) for l in log_1ms]
                cs = [None] * n_sub
                for sub in reversed(range(n_sub)):
                    cs[sub] = c
                    c = c + sums[sub]
                for sub, j in enumerate(js):
                    a = jnp.exp2(log_bs[sub] + sticks[sub] + cs[sub])
                    if near_diagonal:
                        a = jnp.where(valids[sub], a, 0.0)
                    a = a.astype(BF16)
                    a_ref[j] = a
                    o = o + _dot(_side_by_side(a, tq), _stack_heads(v_ref[_key_rows(j, tk), :]))
                return c, o
            return step

        carry = (jnp.zeros((2 * tq, 1), F32), jnp.zeros((tq, LANES), F32))
        carry = _diagonal_step(i, tq, tk, lambda n_sub: make_step(True, n_groups, n_sub), carry)
        _, acc = lax.fori_loop(0, n_full, make_step(False, n_full), carry)
        o_ref[...] = acc.astype(BF16)

    n_kb = s // tk
    return _call("sb_fwd", body, (n_pair, s // tq),
                 [(qkv, (tq, LANES), lambda hp, i: (i, hp)),
                  (qkv, (s, LANES), lambda hp, i: (0, n_pair + hp)),
                  (qkv, (s, LANES), lambda hp, i: (0, 2 * n_pair + hp))],
                 [((s, SB_W), BF16, (tq, LANES), lambda hp, i: (i, hp)),
                  ((n_pair, s // tq, n_kb, 2 * tq, tk), BF16, (None, None, n_kb, 2 * tq, tk),
                   lambda hp, i: (hp, i, 0, 0, 0))], riders=riders)


def _merge_out(retg, sb, w_ret, w_sb_t, w_out, proj, x, mod, gp1, g2, tm):
    s, d = x.shape
    gw = min(512, d)
    n_g = d // gw
    ar_off = (2 * RET_QK + 2 * RET_V + 3 * SB_W) // gw

    def body(rg_ref, sb_ref, wr_ref, ws_ref, wo_ref, *refs):
        gate_refs, (x_ref, mod_ref, gp_ref, g2_ref, mix_ref, r_ref, s_ref, y_ref, hres_ref, h2_ref) = refs[:2 * n_g], refs[2 * n_g:]
        for rows in _pieces(tm):
            rr = _dot(rg_ref[rows, :], wr_ref[...])
            ss = _dot(sb_ref[rows, :], ws_ref[...], _NT)
            a_r = jnp.concatenate([g[rows, :] for g in gate_refs[:n_g]], axis=1).astype(F32)
            a_s = jnp.concatenate([g[rows, :] for g in gate_refs[n_g:]], axis=1).astype(F32)
            mixed = (_sigmoid(a_r) * rr + _sigmoid(a_s) * ss).astype(BF16)
            mix_ref[rows, :] = mixed
            r_ref[rows, :] = rr.astype(BF16)
            s_ref[rows, :] = ss.astype(BF16)
            y = _dot(mixed, wo_ref[...])
            y_ref[rows, :] = y
            ny, _ = _rms(y, d)
            hres = x_ref[rows, :] + mod_ref[:, 2 * d:3 * d] * (ny * gp_ref[...])
            hres_ref[rows, :] = hres
            n2, _ = _rms(hres, d)
            h2_ref[rows, :] = (n2 * g2_ref[...] * (1.0 + mod_ref[:, 4 * d:5 * d]) + mod_ref[:, 3 * d:4 * d]).astype(BF16)

    row = lambda i: (i, 0)
    fix = lambda i: (0, 0)
    tile_bf = ((s, d), BF16, (tm, d), row)
    tile_f = ((s, d), F32, (tm, d), row)
    return _call("merge_out", body, (s // tm,),
                 [(retg, (tm, RET_V), row), (sb, (tm, SB_W), row), (w_ret, (RET_V, d), fix), (w_sb_t, (d, SB_W), fix),
                  (w_out, (d, d), fix)]
                 + [(proj, (tm, gw), functools.partial(lambda i, k: (i, ar_off + k), k=k)) for k in range(2 * n_g)]
                 + [(x, (tm, d), row), (mod, (1, 6 * d), fix), (gp1, (1, d), fix), (g2, (1, d), fix)],
                 [tile_bf, tile_bf, tile_bf, tile_f, tile_f, tile_bf])


def _ff1(h2, w_ff1_t, tm, tn):
    s, f = h2.shape[0], w_ff1_t.shape[0]
    tm = min(tm, s)

    def body(a_ref, w_ref, u_ref, act_ref):
        u = _dot(a_ref[...], w_ref[...], _NT)
        r = jnp.maximum(u, 0.0)
        u_ref[...] = u.astype(BF16)
        act_ref[...] = (r * r).astype(BF16)

    d = h2.shape[1]
    return _call("ff1", body, (f // tn, s // tm),
                 [(h2, (tm, d), lambda j, i: (i, 0)), (w_ff1_t, (tn, d), lambda j, i: (j, 0))],
                 [((s, f), BF16, (tm, tn), lambda j, i: (i, j))] * 2)


def _ff2_loss(act, w_ff2, hres, target, mod, gp2, tm):
    s, d = hres.shape
    f = act.shape[1]

    def body(a_ref, w_ref, h_ref, t_ref, mod_ref, gp_ref, dout_ref, df_ref, loss_ref, dgt_ref, dgp_ref):
        _zero_at_start([loss_ref, dgt_ref, dgp_ref])
        gt, gp = mod_ref[:, 5 * d:6 * d], gp_ref[...]
        for rows in _pieces(tm):
            ff = _dot(a_ref[rows, :], w_ref[...])
            nf, rf = _rms(ff, d)
            out = h_ref[rows, :] + gt * (nf * gp)
            err = out - t_ref[rows, :]
            sq = jnp.sum(err * err, axis=1, keepdims=True)
            loss_ref[...] += jnp.sum(sq, axis=0, keepdims=True)
            dout = err * (1.0 / d)
            dout_ref[rows, :] = dout
            dgt_ref[...] += _colsum(dout * (nf * gp))
            dgp_ref[...] += _colsum(dout * gt * nf)
            df_ref[rows, :] = _rms_bwd(dout * gt * gp, nf, rf, d).astype(BF16)

    row = lambda i: (i, 0)
    fix = lambda i: (0, 0)
    return _call("ff2_loss", body, (s // tm,),
                 [(act, (tm, f), row), (w_ff2, (f, d), fix), (hres, (tm, d), row), (target, (tm, d), row),
                  (mod, (1, 6 * d), fix), (gp2, (1, d), fix)],
                 [((s, d), F32, (tm, d), row), ((s, d), BF16, (tm, d), row), ((1, 1), F32, (1, 1), fix),
                  ((1, d), F32, (1, d), fix), ((1, d), F32, (1, d), fix)])


def _ffn_bwd(df, w_ff2, u, act, h2, tn):
    s, d = df.shape
    f = w_ff2.shape[0]

    def body(df_ref, w_ref, u_ref, act_ref, h2_ref, du_ref, gw2_ref, gw1_ref):
        dfb = df_ref[...]
        du = (_dot(dfb, w_ref[...], _NT) * (2.0 * jnp.maximum(u_ref[...].astype(F32), 0.0))).astype(BF16)
        du_ref[...] = du
        gw2_ref[...] = _dot(act_ref[...], dfb, _TN).astype(BF16)
        gw1_ref[...] = _dot(h2_ref[...], du, _TN).astype(BF16)

    fix = lambda j: (0, 0)
    col = lambda j: (0, j)
    return _call("ffn_bwd", body, (f // tn,),
                 [(df, (s, d), fix), (w_ff2, (tn, d), lambda j: (j, 0)), (u, (s, tn), col), (act, (s, tn), col),
                  (h2, (s, d), fix)],
                 [((s, f), BF16, (s, tn), col), ((f, d), BF16, (tn, d), lambda j: (j, 0)),
                  ((f // tn, d, tn), BF16, (None, d, tn), lambda j: (j, 0, 0))])


def _ff1_bwd(du, w_ff1_t, hres, dout, y, mixed, mod, g2, gp1, tm, riders=None):
    s, d = hres.shape
    f = du.shape[1]

    def body(a_ref, w_ref, h_ref, do_ref, y_ref, mix_ref, mod_ref, g2_ref, gp_ref,
             dh_ref, dy_ref, dsh_ref, dsc_ref, dg2_ref, dgt_ref, dgp_ref, gwo_ref, acc_ref):
        _zero_at_start([dsh_ref, dsc_ref, dg2_ref, dgt_ref, dgp_ref, acc_ref])
        g2, sc2 = g2_ref[...], mod_ref[:, 4 * d:5 * d]
        gt, gp = mod_ref[:, 2 * d:3 * d], gp_ref[...]
        for rows in _pieces(tm):
            dh2 = _dot(a_ref[rows, :], w_ref[...])
            n2, r2 = _rms(h_ref[rows, :], d)
            dsh_ref[...] += _colsum(dh2)
            dsc_ref[...] += _colsum(dh2 * n2 * g2)
            dg2_ref[...] += _colsum(dh2 * n2 * (1.0 + sc2))
            dhres = do_ref[rows, :] + _rms_bwd(dh2 * g2 * (1.0 + sc2), n2, r2, d)
            dh_ref[rows, :] = dhres
            ny, ry = _rms(y_ref[rows, :], d)
            dgt_ref[...] += _colsum(dhres * (ny * gp))
            dgp_ref[...] += _colsum(dhres * gt * ny)
            dy_ref[rows, :] = _rms_bwd(dhres * gt * gp, ny, ry, d).astype(BF16)
        acc_ref[...] += _dot(mix_ref[...], dy_ref[...], _TN)

        @pl.when(pl.program_id(0) == s // tm - 1)
        def _():
            gwo_ref[...] = acc_ref[...].astype(BF16)

    row = lambda i: (i, 0)
    fix = lambda i: (0, 0)
    vec = ((1, d), F32, (1, d), fix)
    return _call("ff1_bwd", body, (s // tm,),
                 [(du, (tm, f), row), (w_ff1_t, (f, d), fix), (hres, (tm, d), row), (dout, (tm, d), row),
                  (y, (tm, d), row), (mixed, (tm, d), row), (mod, (1, 6 * d), fix), (g2, (1, d), fix), (gp1, (1, d), fix)],
                 [((s, d), F32, (tm, d), row), ((s, d), BF16, (tm, d), row), vec, vec, vec, vec, vec,
                  ((d, d), BF16, (d, d), fix)], scratch=[pltpu.VMEM((d, d), F32)], riders=riders)


def _out_bwd(dy, w_out, proj, r_bf, s_bf, tm, tn, riders=None):
    s, d = dy.shape
    ar_off = (2 * RET_QK + 2 * RET_V + 3 * SB_W) // tn
    as_off = ar_off + d // tn

    def body(a_ref, w_ref, ar_ref, as_ref, r_ref, s_ref, dr_ref, ds_ref, dar_ref, das_ref):
        dm = _dot(a_ref[...], w_ref[...], _NT)
        sr, ss = _sigmoid(ar_ref[...].astype(F32)), _sigmoid(as_ref[...].astype(F32))
        dr_ref[...] = (dm * sr).astype(BF16)
        ds_ref[...] = (dm * ss).astype(BF16)
        dar_ref[...] = (dm * r_ref[...].astype(F32) * sr * (1.0 - sr)).astype(BF16)
        das_ref[...] = (dm * s_ref[...].astype(F32) * ss * (1.0 - ss)).astype(BF16)

    tile = (tm, tn)
    here = lambda j, i: (i, j)
    return _call("out_bwd", body, (d // tn, s // tm),
                 [(dy, (tm, d), lambda j, i: (i, 0)), (w_out, (tn, d), lambda j, i: (j, 0)),
                  (proj, tile, lambda j, i: (i, ar_off + j)), (proj, tile, lambda j, i: (i, as_off + j)),
                  (r_bf, tile, here), (s_bf, tile, here)],
                 [((s, d), BF16, tile, here)] * 4, riders=riders)


def _branch_bwd(d_r, d_s, retg, sb, w_ret, w_sb_t, ret, proj, gn_g, riders=None):
    s, d = d_r.shape
    n_step = 4
    part_v, part_s, part_d = RET_V // n_step, SB_W // n_step, d // n_step
    per_dev = d // N_DEV
    n_blk = part_d // per_dev
    gate_off = (2 * RET_QK + RET_V) // part_v

    def body(dr_ref, ds_ref, rg_ref, sb_ref, wr_ref, ws_ref, r_ref, g_ref, w_ref,
             dsb_ref, gwr_ref, gws_ref, dg_ref, dret_ref, dw_ref):
        i = pl.program_id(0)
        dr, ds = dr_ref[...], ds_ref[...]
        dsb_ref[...] = _dot(ds, ws_ref[...]).astype(BF16)
        gwr_ref[...] = _dot(rg_ref[...], dr, _TN).astype(BF16)
        cols = pl.ds(pl.multiple_of(i * part_d, part_d), part_d)
        gws = _dot(sb_ref[...], ds_ref[:, cols], _TN).astype(BF16)
        for k in range(n_blk):
            gws_ref[k] = gws[:, k * per_dev:(k + 1) * per_dev]
        dretg = _dot(dr, wr_ref[...], _NT)
        for h in range(part_v // RET_DV):
            cols = slice(h * RET_DV, (h + 1) * RET_DV)
            o, g, w, d_o = r_ref[:, cols], g_ref[:, cols].astype(F32), w_ref[:, cols], dretg[:, cols]
            mu = jnp.sum(o, axis=1, keepdims=True) * (1.0 / RET_DV)
            xc = o - mu
            rstd = lax.rsqrt(jnp.sum(xc * xc, axis=1, keepdims=True) * (1.0 / RET_DV) + EPS)
            n = xc * rstd
            sg = _sigmoid(g)
            silu = g * sg
            dg_ref[:, cols] = (d_o * n * w * (sg * (1.0 + g * (1.0 - sg)))).astype(BF16)
            dw_ref[:, cols] = _colsum(d_o * silu * n)
            dn = d_o * silu * w
            m1 = jnp.sum(dn, axis=1, keepdims=True) * (1.0 / RET_DV)
            m2 = jnp.sum(dn * n, axis=1, keepdims=True) * (1.0 / RET_DV)
            dret_ref[:, cols] = (rstd * (dn - m1 - n * m2)).astype(BF16)

    fix = lambda i: (0, 0)
    col = lambda i: (0, i)
    return _call("branch_bwd", body, (n_step,),
                 [(d_r, (s, d), fix), (d_s, (s, d), fix), (retg, (s, part_v), col), (sb, (s, SB_W), fix),
                  (w_ret, (part_v, d), lambda i: (i, 0)), (w_sb_t, (d, part_s), col),
                  (ret, (s, part_v), col), (proj, (s, part_v), lambda i: (0, gate_off + i)), (gn_g, (1, part_v), col)],
                 [((s, SB_W), BF16, (s, part_s), col), ((RET_V, d), BF16, (part_v, d), lambda i: (i, 0)),
                  ((N_DEV, SB_W, per_dev), BF16, (n_blk, SB_W, per_dev), lambda i: (i, 0, 0)),
                  ((s, RET_V), BF16, (s, part_v), col), ((s, RET_V), BF16, (s, part_v), col),
                  ((1, RET_V), F32, (1, part_v), col)], riders=riders)


def _ret_bwd(qk_rot, v_bf, dret, log_gamma, t, riders=None):
    s = qk_rot.shape[0]
    n_pair = HEADS // 2
    pw = 2 * RET_DV
    wq, wv = RET_PAIRS * LANES, RET_PAIRS * pw
    n_blk = s // t
    pairs = range(RET_PAIRS)

    def load(q_ref, k_ref, v_ref, do_ref):
        return ([_lanes(q_ref, p, LANES) for p in pairs], [_lanes(k_ref, p, LANES) for p in pairs],
                [_lanes(v_ref, p, pw) for p in pairs], [_lanes(do_ref, p, pw) for p in pairs])

    def d_scores(lg_ref, hp, i, qb, kb, vb, dob):
        z, w = _ret_block(lg_ref, hp, i, t, qb, kb)
        dp = jnp.concatenate([_dot(dob[:, 0:RET_DV], vb[:, 0:RET_DV], _NT),
                              _dot(dob[:, RET_DV:pw], vb[:, RET_DV:pw], _NT)], axis=0)
        return (z * w).astype(BF16), (dp * w).astype(BF16)

    def up_body(lg_ref, q_ref, k_ref, v_ref, do_ref, dq_ref, state_ref):
        hg, i = pl.program_id(0), pl.program_id(1)

        @pl.when(i == 0)
        def _():
            state_ref[...] = jnp.zeros_like(state_ref)

        qbs, kbs, vbs, dobs = load(q_ref, k_ref, v_ref, do_ref)
        dss = [d_scores(lg_ref, hg * RET_PAIRS + p, i, qbs[p], kbs[p], vbs[p], dobs[p])[1] for p in pairs]
        for p in pairs:
            dq_ref[:, p * LANES:(p + 1) * LANES] = (_dot(_side_by_side(dss[p], t), _stack_heads(kbs[p]))
                                                    + _dot(dobs[p], state_ref[p], _NT)).astype(BF16)
        for p in pairs:
            state_ref[p] += _pair_mask() * _dot(kbs[p], vbs[p], _TN)

    def down_body(lg_ref, q_ref, k_ref, v_ref, do_ref, dk_ref, dv_ref, state_ref):
        hg, i = pl.program_id(0), n_blk - 1 - pl.program_id(1)

        @pl.when(pl.program_id(1) == 0)
        def _():
            state_ref[...] = jnp.zeros_like(state_ref)

        qbs, kbs, vbs, dobs = load(q_ref, k_ref, v_ref, do_ref)
        both = [d_scores(lg_ref, hg * RET_PAIRS + p, i, qbs[p], kbs[p], vbs[p], dobs[p]) for p in pairs]
        for p in pairs:
            pp, ds = both[p]
            later = state_ref[p]
            dv_ref[:, p * pw:(p + 1) * pw] = (jnp.concatenate(
                [_dot(pp[:t], dobs[p][:, 0:RET_DV], _TN), _dot(pp[t:], dobs[p][:, RET_DV:pw], _TN)],
                axis=1) + _dot(kbs[p], later)).astype(BF16)
            dk_ref[:, p * LANES:(p + 1) * LANES] = (_dot(ds, _stack_heads(qbs[p]), _TN)
                                                    + _dot(vbs[p], later, _NT)).astype(BF16)
        for p in pairs:
            state_ref[p] += _pair_mask() * _dot(qbs[p], dobs[p], _TN)

    n_grp = n_pair // RET_PAIRS

    def ins(order):
        return [(log_gamma, None, pltpu.SMEM),
                (qk_rot, (t, wq), lambda hg, i: (order(i), hg)),
                (qk_rot, (t, wq), lambda hg, i: (order(i), n_grp + hg)),
                (v_bf, (t, wv), lambda hg, i: (order(i), hg)),
                (dret, (t, wv), lambda hg, i: (order(i), hg))]

    up = lambda i: i
    down = lambda i: n_blk - 1 - i
    scratch = [pltpu.VMEM((RET_PAIRS, LANES, pw), F32)]
    dq = _call("ret_bwd_q", up_body, (n_grp, n_blk), ins(up),
               [((s, RET_QK), BF16, (t, wq), lambda hg, i: (i, hg))], scratch=scratch)[0]
    dk, dv, *rest = _call("ret_bwd_kv", down_body, (n_grp, n_blk), ins(down),
                          [((s, RET_QK), BF16, (t, wq), lambda hg, i: (down(i), hg)),
                           ((s, RET_V), BF16, (t, wv), lambda hg, i: (down(i), hg))],
                          scratch=scratch, riders=riders)
    return [dq, dk, dv] + rest


def _sb_bwd(qkv, weights, do, tq, tk, riders=None):
    s = qkv.shape[0]
    n_pair = HEADS // 2
    _check_tiles(s, tq, tk, SB_GROUP)

    def body(q_ref, k_ref, v_ref, a_ref, do_ref, dq_ref, dk_ref, dv_ref):
        i = pl.program_id(1)

        @pl.when(i == 0)
        def _():
            dk_ref[...] = jnp.zeros_like(dk_ref)
            dv_ref[...] = jnp.zeros_like(dv_ref)

        lower = _tri(tk, False)
        qs = _stack_heads(q_ref[...])
        dos = _stack_heads(do_ref[...].astype(BF16))

        def make_step(near_diagonal, n_sub=SB_GROUP):
            def step(g, carry):
                c_e, dq = carry
                js = [g * SB_GROUP + sub for sub in range(n_sub)]
                rows = [_key_rows(j, tk) for j in js]
                zs = [_dot(qs, k_ref[rw, :], _NT) for rw in rows]
                das = [_dot(dos, v_ref[rw, :], _NT) for rw in rows]
                avals = [a_ref[j] for j in js]
                for a, rw in zip(avals, rows):
                    dv_ref[rw, :] += _dot(a, dos, _TN)
                es = [a.astype(F32) * da for a, da in zip(avals, das)]
                prefixes = [_dot(e, lower) for e in es]
                betas = [1.0 / (1.0 + jnp.exp2(-z)) for z in zs]
                for sub in range(n_sub):
                    dz = es[sub] - (es[sub] + prefixes[sub] + c_e) * betas[sub]
                    if near_diagonal:
                        dz = jnp.where(_sb_valid(i, js[sub], tq, tk), dz, 0.0)
                    dz = dz.astype(BF16)
                    dk_ref[rows[sub], :] += _dot(dz, qs, _TN)
                    dq = dq + _dot(_side_by_side(dz, tq), _stack_heads(k_ref[rows[sub], :]))
                    c_e = c_e + jnp.sum(es[sub], axis=1, keepdims=True)
                return c_e, dq
            return step

        n_full = _n_full(i, tq, tk, SB_GROUP)
        carry = (jnp.zeros((2 * tq, 1), F32), jnp.zeros((tq, LANES), F32))
        carry = lax.fori_loop(0, n_full, make_step(False), carry)
        _, dq = _diagonal_step(i, tq, tk, lambda n_sub: (lambda n, cr: make_step(True, n_sub)(n_full, cr)), carry)
        dq_ref[...] = dq

    blk = lambda hp, i: (i, hp)
    n_kb = s // tk
    return _call("sb_bwd", body, (n_pair, s // tq),
                 [(qkv, (tq, LANES), blk),
                  (qkv, (s, LANES), lambda hp, i: (0, n_pair + hp)),
                  (qkv, (s, LANES), lambda hp, i: (0, 2 * n_pair + hp)),
                  (weights, (None, None, n_kb, 2 * tq, tk), lambda hp, i: (hp, i, 0, 0, 0)),
                  (do, (tq, LANES), blk)],
                 [((s, SB_W), F32, (tq, LANES), blk),
                  ((s, SB_W), F32, (s, LANES), lambda hp, i: (0, hp)),
                  ((s, SB_W), F32, (s, LANES), lambda hp, i: (0, hp))], riders=riders)


def _assemble_dproj(dq_r, dk_r, dv_r, dg_r, dq_s, dk_s, dv_s, da_r, da_s, cos, sin, idx_col, lg_lanes, tm, riders=None):
    s, d = da_r.shape
    width = 2 * RET_QK + 2 * RET_V + 3 * SB_W + 2 * d

    def body(dq_ref, dk_ref, dv_ref, dg_ref, dqs_ref, dks_ref, dvs_ref, dar_ref, das_ref, cos_ref, sin_ref,
             idx_ref, lg_ref, o_ref):
        lane = lax.broadcasted_iota(jnp.int32, (1, LANES), 1)
        first = jnp.bitwise_and(lane, RET_DQK - 1) < (RET_DQK // 2)
        cos, sin = cos_ref[...], sin_ref[...]
        idx = idx_ref[...]
        for src, base, sign, scale in ((dq_ref, 0, 1.0, 1.0), (dk_ref, RET_QK, -1.0, RET_DQK ** -0.5)):
            for g in range(RET_QK // LANES):
                v = src[:, g * LANES:(g + 1) * LANES].astype(F32) * (_decay_scale(lg_ref, idx, g, sign) * scale)
                sw = jnp.where(first, pltpu.roll(v, LANES - RET_DQK // 2, 1), pltpu.roll(v, RET_DQK // 2, 1))
                o_ref[:, base + g * LANES:base + (g + 1) * LANES] = (v * cos - sw * sin).astype(BF16)
        off = 2 * RET_QK
        o_ref[:, off:off + RET_V] = dv_ref[...].astype(BF16)
        off += RET_V
        o_ref[:, off:off + RET_V] = dg_ref[...]
        off += RET_V
        o_ref[:, off:off + SB_W] = (dqs_ref[...] * (SB_DH ** -0.5)).astype(BF16)
        off += SB_W
        o_ref[:, off:off + SB_W] = (dks_ref[...] * LN2).astype(BF16)
        off += SB_W
        o_ref[:, off:off + SB_W] = dvs_ref[...].astype(BF16)
        off += SB_W
        o_ref[:, off:off + d] = dar_ref[...]
        off += d
        o_ref[:, off:off + d] = das_ref[...]

    row = lambda i: (i, 0)
    ins = [(a, (tm, a.shape[1]), row) for a in (dq_r, dk_r, dv_r, dg_r, dq_s, dk_s, dv_s, da_r, da_s, cos, sin, idx_col)]
    ins.append((lg_lanes, (1, RET_QK), lambda i: (0, 0)))
    return _call("assemble_dproj", body, (s // tm,), ins, [((s, width), BF16, (tm, width), row)], riders=riders)


def _in_bwd(dproj, w_in_t, x, dhres, mod, g1, tm, riders=None):
    s, d = x.shape
    width = dproj.shape[1]

    def body(a_ref, w_ref, x_ref, dh_ref, mod_ref, g_ref, dx_ref, dsh_ref, dsc_ref, dg_ref):
        _zero_at_start([dsh_ref, dsc_ref, dg_ref])
        g1, sc1 = g_ref[...], mod_ref[:, d:2 * d]
        for rows in _pieces(tm):
            dh = _dot(a_ref[rows, :], w_ref[...])
            n1, r1 = _rms(x_ref[rows, :], d)
            dsh_ref[...] += _colsum(dh)
            dsc_ref[...] += _colsum(dh * n1 * g1)
            dg_ref[...] += _colsum(dh * n1 * (1.0 + sc1))
            dx_ref[rows, :] = dh_ref[rows, :] + _rms_bwd(dh * g1 * (1.0 + sc1), n1, r1, d)

    row = lambda i: (i, 0)
    fix = lambda i: (0, 0)
    vec = ((1, d), F32, (1, d), fix)
    return _call("in_bwd", body, (s // tm,),
                 [(dproj, (tm, width), row), (w_in_t, (width, d), fix), (x, (tm, d), row), (dhres, (tm, d), row),
                  (mod, (1, 6 * d), fix), (g1, (1, d), fix)],
                 [((s, d), F32, (tm, d), row), vec, vec, vec], riders=riders)


def _adamw(w, g, m, v):
    m = ADAM_B1 * m + (1.0 - ADAM_B1) * g
    v = ADAM_B2 * v + (1.0 - ADAM_B2) * (g * g)
    m_hat = m / (1.0 - ADAM_B1 ** ADAM_STEP)
    v_hat = v / (1.0 - ADAM_B2 ** ADAM_STEP)
    delta = -ADAM_LR * (m_hat / (jnp.sqrt(v_hat) + ADAM_EPS) + ADAM_WD * w)
    return delta, m, v


def _adam_reduce(name, sets, steps):
    n = len(sets)

    def body(*refs):
        for k in range(n):
            p_ref, w_ref, m_ref, v_ref = refs[4 * k:4 * k + 4]
            outs = refs[4 * n + 4 * k:4 * n + 4 * k + 4]
            g = p_ref[0].astype(F32)
            for j in range(1, p_ref.shape[0]):
                g = g + p_ref[j].astype(F32)
            for o_ref, val in zip(outs, (g,) + _adamw(w_ref[...], g, m_ref[...], v_ref[...])):
                o_ref[...] = val

    ins, outs = [], []
    row = lambda i: (i, 0)
    for parts, w, m, v in sets:
        rws, cls = w.shape
        tr = rws // steps
        assert tr * steps == rws and tr % 16 == 0
        ins += [(parts, (parts.shape[0], tr, cls), lambda i: (0, i, 0)), (w, (tr, cls), row), (m, (tr, cls), row),
                (v, (tr, cls), row)]
        outs += [((rws, cls), F32, (tr, cls), row)] * 4
    res = _call(name, body, (steps,), ins, outs)
    return [res[4 * k:4 * k + 4] for k in range(n)]


def _ada_bwd_adam(cs_t, dmod_cols, w, m, v):
    d, nc = w.shape

    def body(c_ref, dm_ref, w_ref, m_ref, v_ref, g_out, d_out, m_out, v_out):
        g = c_ref[0] * dm_ref[0:1, :]
        for r in range(1, N_DEV):
            g = g + c_ref[r] * dm_ref[r:r + 1, :]
        delta, mn, vn = _adamw(w_ref[...], g, m_ref[...], v_ref[...])
        g_out[...] = g
        d_out[...] = delta
        m_out[...] = mn
        v_out[...] = vn

    fix = lambda i: (0, 0)
    blk = (d, nc)
    return _call("ada_bwd_adam", body, (1,),
                 [(cs_t, (N_DEV, d, 1), lambda i: (0, 0, 0)), (dmod_cols, (N_DEV, nc), fix), (w, blk, fix), (m, blk, fix), (v, blk, fix)],
                 [((d, nc), F32, blk, fix)] * 4)


def _small_adam(parts, ws, ms, vs):
    n = len(ws)
    widths = [w.shape[1] for w in ws]
    total = parts.shape[1]
    assert sum(widths) + LANES == total

    def body(p_ref, *refs):
        w_refs, m_refs, v_refs = refs[:n], refs[n:2 * n], refs[2 * n:3 * n]
        outs = refs[3 * n:]
        g = p_ref[0:1, :]
        for k in range(1, N_DEV):
            g = g + p_ref[k:k + 1, :]
        off = 0
        for i, width in enumerate(widths):
            gi = g[:, off:off + width]
            delta, mn, vn = _adamw(w_refs[i][...], gi, m_refs[i][...], v_refs[i][...])
            for o_ref, val in zip(outs[4 * i:4 * i + 4], (gi, delta, mn, vn)):
                o_ref[...] = val
            off += width
        outs[4 * n][...] = g[:, off:off + LANES]

    fix = lambda i: (0, 0)
    vec = lambda a: (a, (1, a.shape[1]), fix)
    out_specs = [((1, width), F32, (1, width), fix) for width in widths for _ in range(4)]
    out_specs.append(((1, LANES), F32, (1, LANES), fix))
    res = _call("small_adam", body, (1,),
                [(parts, (N_DEV, total), fix)] + [vec(a) for a in list(ws) + list(ms) + list(vs)], out_specs)
    return [res[4 * i:4 * i + 4] for i in range(n)], res[4 * n]


def kernel(x, c, positions, ada_w, ada_b, pre_mix_g, post_mix_g, pre_ffn_g, post_ffn_g, w_in, ret_gn_g, w_ret_branch, w_sb_branch, w_out, w_ff1, w_ff2, loss_target, m_ada_w, m_ada_b, m_pre_mix_g, m_post_mix_g, m_pre_ffn_g, m_post_ffn_g, m_w_in, m_ret_gn_g, m_w_ret_branch, m_w_sb_branch, m_w_out, m_w_ff1, m_w_ff2, v_ada_w, v_ada_b, v_pre_mix_g, v_post_mix_g, v_pre_ffn_g, v_post_ffn_g, v_w_in, v_ret_gn_g, v_w_ret_branch, v_w_sb_branch, v_w_out, v_w_ff1, v_w_ff2):
    _, s, d = x.shape
    d_ff = w_ff1.shape[2] * N_DEV
    d_in = w_in.shape[2] * N_DEV
    me = 4 * lax.axis_index("x") + 2 * lax.axis_index("y") + lax.axis_index("c")
    x2, tgt = x[0], loss_target[0]

    core = lax.axis_index("c").astype(jnp.int32).reshape(1)
    bf = lambda w: w[0].astype(BF16)

    w_in_t, m_in_t, v_in_t = (jnp.swapaxes(a[0], 0, 1) for a in (w_in, m_w_in, v_w_in))

    c_all, g_in = _exchange("gather_in", [c, w_in_t.astype(BF16)], ["gather", "gather_chip"])
    c_all = c_all.reshape(N_DEV, d)

    n_ada = ada_w.shape[2]
    ada_b_cols = lax.dynamic_slice(ada_b, (0, me * n_ada), (1, n_ada))
    cs_all, mod_cols = _ada_fwd(c_all, ada_w[0], ada_b_cols)
    mod_all = _exchange("gather_mod", [mod_cols], ["gather"])[0]
    mod = lax.dynamic_index_in_dim(mod_all, me, axis=1, keepdims=False).reshape(1, 6 * d)

    tm = min(256, s)
    h, g_in = _pre_norm(x2, pre_mix_g, mod, 2 * tm, riders=([g_in], ["forward"]))
    wt_in = g_in.reshape(d_in, d)
    bf_t = lambda w: jnp.swapaxes(w[0], 0, 1).astype(BF16)
    small_w = [bf(w_ret_branch), bf_t(w_sb_branch), bf(w_out)]
    proj, *small_w = _matmul("in_proj", h, wt_in, "nt", s, 512, BF16, riders=(small_w, ["gather_chip"] * 3))
    pos_col = positions.reshape(s, 1).astype(F32)
    freqs = ROPE_BASE ** (-jnp.arange(0, RET_DQK, 2, dtype=F32) / RET_DQK)
    inv_freq = jnp.tile(freqs, LANES // (RET_DQK // 2)).reshape(1, LANES)
    log_gamma_np = np.log1p(-(2.0 ** (-5.0 - np.arange(HEADS))))
    log_gamma = jnp.asarray(log_gamma_np, F32)
    lg_lanes = jnp.asarray(np.repeat(log_gamma_np, RET_DQK).reshape(1, RET_QK), F32)
    idx_col = (jnp.arange(s, dtype=F32) - (s // 2)).reshape(s, 1)
    qk_rot, v_bf, qkv_sb, cos_t, sin_t = _prep(proj, pos_col, idx_col, inv_freq, lg_lanes, 2 * tm)
    tq, tk = min(256, s), min(128, s)
    tq_sb = min(SB_TQ, s)
    sb, sb_weights, *big_w = _sb_fwd(qkv_sb, tq_sb, tk, riders=([bf(w_ff2), bf_t(w_ff1)], ["gather_chip"] * 2))
    ret, retg, g_ret, g_sb, g_out, g_ff2, g_ff1 = _ret_fwd(qk_rot, v_bf, proj, ret_gn_g, log_gamma, tq,
                                                           riders=(small_w + big_w, ["forward"] * 5))
    wf_ret = g_ret.reshape(RET_V, d)
    wt_sb = g_sb.reshape(d, SB_W)
    wf_out = g_out.reshape(d, d)
    wt_ff1 = g_ff1.reshape(d_ff, d)
    wf_ff2 = g_ff2.reshape(d_ff, d)
    mixed, r_bf, s_bf, y, hres, h2 = _merge_out(retg, sb, wf_ret, wt_sb, wf_out, proj, x2, mod, post_mix_g, pre_ffn_g, tm)
    u, act = _ff1(h2, wt_ff1, s, 512)
    dout, df, loss_sum, d_gt2, d_gp2 = _ff2_loss(act, wf_ff2, hres, tgt, mod, post_ffn_g, tm)

    du, gw_ff2, gw_ff1 = _ffn_bwd(df, wf_ff2, u, act, h2, d_ff // N_DEV)
    gw_ff2 = gw_ff2.reshape(N_DEV, d_ff // N_DEV, d)
    dhres, dy, d_sh2, d_sc2, d_g2, d_gt1, d_gp1, gw_out, t_ff1, t_ff2 = _ff1_bwd(
        du, wt_ff1, hres, dout, y, mixed, mod, pre_ffn_g, post_mix_g, tm, riders=([gw_ff1, gw_ff2], ["pair"] * 2))
    gw_out = gw_out.reshape(N_DEV, d // N_DEV, d)
    s_ff1, s_ff2 = _pair_sum("pair_sum_ff", [(gw_ff1, t_ff1), (gw_ff2, t_ff2)], core)
    d_r, d_s, da_r, da_s = _out_bwd(dy, wf_out, proj, r_bf, s_bf, 2 * tm, min(512, d))
    dsb, gw_ret, gw_sb, dg_r, dret, d_gn, p_out = _branch_bwd(d_r, d_s, retg, sb, wf_ret, wt_sb, ret, proj, ret_gn_g,
                                                              riders=([gw_out], ["scatter"]))
    gw_ret = gw_ret.reshape(N_DEV, RET_V // N_DEV, d)
    dq_s, dk_s, dv_s, p_ff1, p_ff2 = _sb_bwd(qkv_sb, sb_weights, dsb, tq_sb, tk,
                                             riders=([s_ff1, s_ff2], ["chip_scatter"] * 2))
    dq_r, dk_r, dv_r, p_sb = _ret_bwd(qk_rot, v_bf, dret, log_gamma, tq, riders=([gw_sb], ["scatter"]))
    dproj = _assemble_dproj(dq_r, dk_r, dv_r, dg_r, dq_s, dk_s, dv_s, da_r, da_s, cos_t, sin_t, idx_col, lg_lanes, tm)[0]
    gw_in, p_ret = _matmul("grad_w_in", dproj, h, "tn", 512, d, BF16, riders=([gw_ret], ["scatter"]))
    gw_in = gw_in.reshape(N_DEV, d_in // N_DEV, d)
    t_in = _exchange("pair_in", [gw_in], ["pair"])[0]
    s_in = _pair_sum("pair_sum_in", [(gw_in, t_in)], core)[0]
    grad_x, d_sh1, d_sc1, d_g1, p_in = _in_bwd(dproj, wt_in, x2, dhres, mod, pre_mix_g, tm,
                                               riders=([s_in], ["chip_scatter"]))
    loss_lanes = jnp.pad(loss_sum, ((0, 0), (0, LANES - 1)))
    small = jnp.concatenate([d_sh1, d_sc1, d_gt1, d_sh2, d_sc2, d_gt2, d_g1, d_gp1, d_g2, d_gp2, d_gn, loss_lanes], axis=1)
    small_all = _exchange("gather_small", [small], ["gather"])[0].reshape(N_DEV, small.shape[1])
    parts = [p_in, p_ret, p_sb, p_out, p_ff1, p_ff2]

    res = {}
    names = ["w_ret_branch", "w_sb_branch", "w_out", "w_ff1", "w_ff2"]
    ws = [w_ret_branch, w_sb_branch, w_out, w_ff1, w_ff2]
    ms = [m_w_ret_branch, m_w_sb_branch, m_w_out, m_w_ff1, m_w_ff2]
    vs = [v_w_ret_branch, v_w_sb_branch, v_w_out, v_w_ff1, v_w_ff2]
    sets = [(p, w[0], m[0], v[0]) for p, w, m, v in zip(parts[1:], ws, ms, vs)]
    for nm, outs4 in zip(names, _adam_reduce("adam_rest", sets, 2)):
        res[nm] = [o[None] for o in outs4]
    res["w_in"] = [jnp.swapaxes(o, 0, 1)[None]
                   for o in _adam_reduce("adam_w_in", [(parts[0], w_in_t, m_in_t, v_in_t)], 2)[0]]
    dmod_cols = lax.dynamic_slice(small_all, (0, me * n_ada), (N_DEV, n_ada))
    res["ada_w"] = [o[None] for o in _ada_bwd_adam(cs_all.reshape(N_DEV, d, 1), dmod_cols, ada_w[0], m_ada_w[0], v_ada_w[0])]
    vec_names = ["ada_b", "pre_mix_g", "post_mix_g", "pre_ffn_g", "post_ffn_g", "ret_gn_g"]
    vec_res, loss_lanes = _small_adam(small_all,
                                      [ada_b, pre_mix_g, post_mix_g, pre_ffn_g, post_ffn_g, ret_gn_g],
                                      [m_ada_b, m_pre_mix_g, m_post_mix_g, m_pre_ffn_g, m_post_ffn_g, m_ret_gn_g],
                                      [v_ada_b, v_pre_mix_g, v_post_mix_g, v_pre_ffn_g, v_post_ffn_g, v_ret_gn_g])
    res.update(zip(vec_names, vec_res))
    loss = (0.5 / d) * loss_lanes[0, 0]
    order = ["ada_w", "ada_b", "pre_mix_g", "post_mix_g", "pre_ffn_g", "post_ffn_g", "w_in", "ret_gn_g",
             "w_ret_branch", "w_sb_branch", "w_out", "w_ff1", "w_ff2"]
    outs = [loss, grad_x[None]]
    for k in range(4):
        outs += [res[nm][k] for nm in order]
    return tuple(outs)
```

```python
import functools

import numpy as np
import jax
import jax.numpy as jnp
from jax import lax
from jax.experimental import pallas as pl
from jax.experimental.pallas import tpu as pltpu

F32 = jnp.float32
BF16 = jnp.bfloat16
N_DEV = 8
AXES = ("x", "y", "c")

EPS = 1e-6
CHUNK = 64
CHUNK_SHIFT = 6
HEADS = 8
RET_DQK = 64
RET_DV = 128
SB_DH = 64
RET_QK = HEADS * RET_DQK
RET_V = HEADS * RET_DV
SB_W = HEADS * SB_DH
ROPE_BASE = 10000.0
LANES = 128

ADAM_LR = 0.001
ADAM_B1 = 0.9
ADAM_B2 = 0.999
ADAM_EPS = 1e-08
ADAM_WD = 0.01
ADAM_STEP = 10

VMEM_LIMIT = 56 * 1024 * 1024

_NN = (((1,), (0,)), ((), ()))
_NT = (((1,), (1,)), ((), ()))
_TN = (((0,), (0,)), ((), ()))


def _dot(a, b, dims=_NN):
    if a.dtype != BF16:
        a = a.astype(BF16)
    if b.dtype != BF16:
        b = b.astype(BF16)
    return lax.dot_general(a, b, dims, preferred_element_type=F32)


def _sigmoid(x):
    return 1.0 / (1.0 + jnp.exp(-x))


def _rms(x, d):
    r = lax.rsqrt(jnp.sum(x * x, axis=1, keepdims=True) * (1.0 / d) + EPS)
    return x * r, r


def _rms_bwd(dn, n, r, d):
    return r * (dn - n * (jnp.sum(dn * n, axis=1, keepdims=True) * (1.0 / d)))


def _colsum(v):
    return jnp.sum(v, axis=0, keepdims=True)


ROW_SPLIT = 2


def _zero_at_start(refs):
    @pl.when(pl.program_id(0) == 0)
    def _():
        for r in refs:
            r[...] = jnp.zeros_like(r)


def _pieces(tm):
    step = tm // ROW_SPLIT
    return [slice(k * step, (k + 1) * step) for k in range(ROW_SPLIT)]


KIND_SLOTS = {"gather": N_DEV, "scatter": N_DEV, "gather_chip": N_DEV, "forward": N_DEV, "pair": N_DEV // 2,
              "chip_scatter": N_DEV // 2}
SEMS_PER_ARRAY = N_DEV - 1


def _exchange_copies(ins, outs, send_sems, recv_sems, local_sems, kinds):
    x, y, c = (lax.axis_index(a) for a in AXES)
    me, chip, sibling = 4 * x + 2 * y + c, 2 * x + y, (x, y, 1 - c)
    mesh_id = pl.DeviceIdType.MESH
    other_chips = []
    for k in range(1, N_DEV // 2):
        px = 1 - x if k & 2 else x
        py = 1 - y if k & 1 else y
        other_chips.append((px, py))
    copies = []
    for i, kind in enumerate(kinds):
        def remote(src, dst, k, to, i=i):
            return pltpu.make_async_remote_copy(
                src_ref=src, dst_ref=dst, send_sem=send_sems.at[i * SEMS_PER_ARRAY + k],
                recv_sem=recv_sems.at[i * SEMS_PER_ARRAY + k], device_id=to, device_id_type=mesh_id)

        if kind in ("gather", "scatter"):
            pick = (lambda ref, d: ref.at[d]) if kind == "scatter" else (lambda ref, d: ref)
            copies.append(pltpu.make_async_copy(pick(ins[i], me), outs[i].at[me], local_sems.at[i]))
            for k in range(1, N_DEV):
                to = (1 - x if k & 4 else x, 1 - y if k & 2 else y, 1 - c if k & 1 else c)
                copies.append(remote(pick(ins[i], 4 * to[0] + 2 * to[1] + to[2]), outs[i].at[me], k - 1, to))
        elif kind == "gather_chip":
            copies.append(pltpu.make_async_copy(ins[i], outs[i].at[me], local_sems.at[i]))
            copies.append(remote(ins[i], outs[i].at[me], 0, sibling))
            for k, (px, py) in enumerate(other_chips):
                copies.append(remote(ins[i], outs[i].at[me], 1 + k, (px, py, c)))
        elif kind == "forward":
            for k, (px, py) in enumerate(other_chips):
                slot = 4 * px + 2 * py + c
                copies.append(remote(outs[i].at[slot], outs[i].at[slot], k, sibling))
        elif kind == "pair":
            for k in range(N_DEV // 2):
                copies.append(remote(ins[i].at[2 * k + 1 - c], outs[i].at[k], k, sibling))
        elif kind == "chip_scatter":
            copies.append(pltpu.make_async_copy(ins[i].at[chip], outs[i].at[chip], local_sems.at[i]))
            for k, (px, py) in enumerate(other_chips):
                copies.append(remote(ins[i].at[2 * px + py], outs[i].at[chip], k, (px, py, c)))
        else:
            raise ValueError(kind)
    return copies


def _exchange_shapes(arrays, kinds):
    shapes = []
    for a, kind in zip(arrays, kinds):
        tail = a.shape if kind in ("gather", "gather_chip") else a.shape[1:]
        shapes.append(jax.ShapeDtypeStruct((KIND_SLOTS[kind],) + tuple(tail), a.dtype))
    return shapes


def _exchange_sems(n):
    return [pltpu.SemaphoreType.DMA((n * SEMS_PER_ARRAY,)), pltpu.SemaphoreType.DMA((n * SEMS_PER_ARRAY,)),
            pltpu.SemaphoreType.DMA((n,))]


def _call(name, body, grid, ins, outs, scratch=(), riders=None, prefetch=None):
    any_spec = pl.BlockSpec(memory_space=pl.ANY)
    in_specs = [pl.BlockSpec(memory_space=im) if bs is None else pl.BlockSpec(bs, im) for _, bs, im in ins]
    out_specs = [pl.BlockSpec(bs, im) for _, _, bs, im in outs]
    out_shape = [jax.ShapeDtypeStruct(s, d) for s, d, _, _ in outs]
    operands = [a for a, _, _ in ins]
    scratch = list(scratch)
    aliases = {}
    n_pre = 0 if prefetch is None else 1
    kernel = functools.partial(body) if prefetch is None else (lambda _, *refs: body(*refs))
    if riders is not None:
        arrays, kinds = riders
        nr, n_in, n_out, n_scr = len(arrays), len(ins), len(outs), len(scratch)

        def kernel(*refs):
            refs = refs[n_pre:]
            own_in, ride_in = refs[:n_in], refs[n_in:n_in + nr]
            own_out = refs[n_in + nr:n_in + nr + n_out]
            ride_out = refs[n_in + nr + n_out:n_in + 2 * nr + n_out]
            own_scr = refs[n_in + 2 * nr + n_out:n_in + 2 * nr + n_out + n_scr]
            sems = refs[n_in + 2 * nr + n_out + n_scr:]
            ids = [pl.program_id(a) for a in range(len(grid))]
            first = functools.reduce(jnp.logical_and, [i == 0 for i in ids])
            last = functools.reduce(jnp.logical_and, [i == g - 1 for i, g in zip(ids, grid)])

            @pl.when(first)
            def _():
                for cp in _exchange_copies(ride_in, ride_out, *sems, kinds):
                    cp.start()

            body(*own_in, *own_out, *own_scr)

            @pl.when(last)
            def _():
                for cp in _exchange_copies(ride_in, ride_out, *sems, kinds):
                    cp.wait()

        in_specs += [any_spec] * nr
        out_specs += [any_spec] * nr
        out_shape += _exchange_shapes(arrays, kinds)
        operands += list(arrays)
        scratch += _exchange_sems(nr)
        aliases = {n_pre + n_in + r: n_out + r for r, kind in enumerate(kinds) if kind == "forward"}
    params = pltpu.CompilerParams(dimension_semantics=("arbitrary",) * len(grid), vmem_limit_bytes=VMEM_LIMIT)
    if prefetch is None:
        return pl.pallas_call(kernel, name=name, grid=grid, in_specs=in_specs, out_specs=out_specs,
                              out_shape=out_shape, scratch_shapes=scratch, input_output_aliases=aliases,
                              compiler_params=params)(*operands)
    grid_spec = pltpu.PrefetchScalarGridSpec(num_scalar_prefetch=1, grid=grid, in_specs=in_specs,
                                             out_specs=out_specs, scratch_shapes=scratch)
    return pl.pallas_call(kernel, name=name, grid_spec=grid_spec, out_shape=out_shape,
                          input_output_aliases=aliases, compiler_params=params)(prefetch, *operands)


def _exchange(name, arrays, kinds):
    n = len(arrays)

    def body(*refs):
        copies = _exchange_copies(refs[:n], refs[n:2 * n], *refs[2 * n:], kinds)
        for cp in copies:
            cp.start()
        for cp in copies:
            cp.wait()

    any_spec = pl.BlockSpec(memory_space=pl.ANY)
    return pl.pallas_call(
        functools.partial(body),
        name=name,
        in_specs=[any_spec] * n,
        out_specs=[any_spec] * n,
        out_shape=_exchange_shapes(arrays, kinds),
        scratch_shapes=_exchange_sems(n),
        input_output_aliases={i: i for i, kind in enumerate(kinds) if kind == "forward"},
    )(*arrays)


def _pair_sum(name, pairs, my_core):
    n = len(pairs)

    def body(*refs):
        for k in range(n):
            a_ref, b_ref, o_ref = refs[2 * k], refs[2 * k + 1], refs[2 * n + k]
            o_ref[...] = (a_ref[...].astype(F32) + b_ref[...].astype(F32)).astype(o_ref.dtype)

    ins, outs = [], []
    for mine, theirs in pairs:
        _, rws, cls = mine.shape
        ins += [(mine, (None, rws, cls), lambda k, core: (2 * k + core[0], 0, 0)),
                (theirs, (None, rws, cls), lambda k, core: (k, 0, 0))]
        outs.append(((N_DEV // 2, rws, cls), mine.dtype, (None, rws, cls), lambda k, core: (k, 0, 0)))
    return _call(name, body, (N_DEV // 2,), ins, outs, prefetch=my_core)


def _matmul(name, a, b, kind, tm, tn, out_dtype, blocked_out=False, riders=None):
    if kind == "tn":
        kdim, m = a.shape
    else:
        m, kdim = a.shape
    n = b.shape[0] if kind == "nt" else b.shape[1]
    tm, tn = min(tm, m), min(tn, n)
    dims = {"nn": _NN, "nt": _NT, "tn": _TN}[kind]

    def body(a_ref, b_ref, o_ref):
        o_ref[...] = _dot(a_ref[...], b_ref[...], dims).astype(o_ref.dtype)

    a_spec = (a, (kdim, tm), lambda j, i: (0, i)) if kind == "tn" else (a, (tm, kdim), lambda j, i: (i, 0))
    b_spec = (b, (tn, kdim), lambda j, i: (j, 0)) if kind == "nt" else (b, (kdim, tn), lambda j, i: (0, j))
    if blocked_out:
        out = ((n // tn, m, tn), out_dtype, (None, tm, tn), lambda j, i: (j, i, 0))
    else:
        out = ((m, n), out_dtype, (tm, tn), lambda j, i: (i, j))
    res = _call(name, body, (n // tn, m // tm), [a_spec, b_spec], [out], riders=riders)
    return res[0] if riders is None else res


def _ada_fwd(c_all, ada_w, ada_b_cols):
    def body(c_ref, w_ref, b_ref, cs_ref, o_ref):
        v = c_ref[...]
        cs = v * _sigmoid(v)
        cs_ref[...] = cs
        o_ref[...] = lax.dot_general(cs, w_ref[...], _NN, preferred_element_type=F32,
                                     precision=lax.Precision.HIGHEST) + b_ref[...]

    r, d = c_all.shape
    nc = ada_w.shape[1]
    fix = lambda i: (0, 0)
    return _call("ada_fwd", body, (1,),
                 [(c_all, (r, d), fix), (ada_w, (d, nc), fix), (ada_b_cols, (1, nc), fix)],
                 [((r, d), F32, (r, d), fix), ((r, nc), F32, (r, nc), fix)])


def _pre_norm(x, g, mod, tm, riders=None):
    s, d = x.shape

    def body(x_ref, g_ref, mod_ref, h_ref):
        n, _ = _rms(x_ref[...], d)
        sh, sc = mod_ref[:, 0:d], mod_ref[:, d:2 * d]
        h_ref[...] = (n * g_ref[...] * (1.0 + sc) + sh).astype(BF16)

    return _call("pre_norm", body, (s // tm,),
                 [(x, (tm, d), lambda i: (i, 0)), (g, (1, d), lambda i: (0, 0)),
                  (mod, (1, 6 * d), lambda i: (0, 0))],
                 [((s, d), BF16, (tm, d), lambda i: (i, 0))], riders=riders)


LOG2E = 1.4426950408889634
LN2 = 0.6931471805599453


def _decay_scale(lg_ref, idx, g, sign):
    return jnp.exp((sign * idx) * lg_ref[:, g * LANES:(g + 1) * LANES])


def _prep(proj, pos_col, idx_col, inv_freq, lg_lanes, tm):
    s = proj.shape[0]
    sb_off = (2 * RET_QK + 2 * RET_V) // SB_W
    n_q = RET_QK // LANES

    def body(qk_ref, qs_ref, pos_ref, idx_ref, f_ref, lg_ref, qk_out, qs_out, cos_out, sin_out):
        ang = pos_ref[...] * f_ref[...]
        lane = lax.broadcasted_iota(jnp.int32, (1, LANES), 1)
        first = jnp.bitwise_and(lane, RET_DQK - 1) < (RET_DQK // 2)
        cos = jnp.cos(ang)
        sin = jnp.where(first, -1.0, 1.0) * jnp.sin(ang)
        cos_out[...] = cos
        sin_out[...] = sin
        idx = idx_ref[...]
        for g in range(2 * n_q):
            v = qk_ref[:, g * LANES:(g + 1) * LANES].astype(F32)
            sw = jnp.where(first, pltpu.roll(v, LANES - RET_DQK // 2, 1), pltpu.roll(v, RET_DQK // 2, 1))
            r = v * cos + sw * sin
            if g < n_q:
                r = r * _decay_scale(lg_ref, idx, g, 1.0)
            else:
                r = r * (_decay_scale(lg_ref, idx, g - n_q, -1.0) * (RET_DQK ** -0.5))
            qk_out[:, g * LANES:(g + 1) * LANES] = r.astype(BF16)
        qs_out[...] = (qs_ref[...].astype(F32) * (SB_DH ** -0.5 * LOG2E)).astype(BF16)

    row = lambda i: (i, 0)
    return _call("prep", body, (s // tm,),
                 [(proj, (tm, 2 * RET_QK), row),
                  (proj, (tm, SB_W), lambda i: (i, sb_off)),
                  (pos_col, (tm, 1), row),
                  (idx_col, (tm, 1), row),
                  (inv_freq, (1, LANES), lambda i: (0, 0)),
                  (lg_lanes, (1, RET_QK), lambda i: (0, 0))],
                 [((s, 2 * RET_QK), BF16, (tm, 2 * RET_QK), row),
                  ((s, SB_W), BF16, (tm, SB_W), row),
                  ((s, LANES), F32, (tm, LANES), row),
                  ((s, LANES), F32, (tm, LANES), row)])


def _head_mask(hh):
    lane = lax.broadcasted_iota(jnp.int32, (1, LANES), 1)
    return (lane >= RET_DQK) if hh else (lane < RET_DQK)


def _masked(v, m):
    return jnp.where(m, v, jnp.zeros_like(v))


SB_GROUP = 4
SB_TQ = 256


def _stack_heads(v):
    return jnp.concatenate([_masked(v, _head_mask(0)), _masked(v, _head_mask(1))], axis=0)


def _side_by_side(v, t):
    return jnp.concatenate([v[:t], v[t:]], axis=1)


def _tile_pos(i, j, tq, tk):
    row = jnp.bitwise_and(lax.broadcasted_iota(jnp.int32, (2 * tq, tk), 0), tq - 1) + i * tq
    col = lax.broadcasted_iota(jnp.int32, (2 * tq, tk), 1) + j * tk
    return row, col


def _n_groups(i, tq, tk, grp):
    return ((i + 1) * (tq // tk) + grp - 1) // grp


def _n_full(i, tq, tk, grp):
    return (i * (tq // tk)) // grp


def _key_rows(j, tk):
    return pl.ds(pl.multiple_of(j * tk, tk), tk)


def _ret_weight(lg_rows, i, j, tq, tk):
    row, col = _tile_pos(i, j, tq, tk)
    same = jnp.right_shift(col, CHUNK_SHIFT) == jnp.right_shift(row, CHUNK_SHIFT)
    later = jnp.where(same, jnp.exp((2.0 * lg_rows) * (col - row).astype(F32)), 0.0)
    return jnp.where(col <= row, 1.0, later)


def _lg_rows(lg_ref, hp, tq):
    first = lax.broadcasted_iota(jnp.int32, (2 * tq, 1), 0) < tq
    return jnp.where(first, lg_ref[2 * hp], lg_ref[2 * hp + 1])


def _check_tiles(s, tq, tk, grp):
    assert tq % tk == 0 and tq & (tq - 1) == 0 and tk & (tk - 1) == 0
    assert s % tq == 0 and (s // tk) % grp == 0 and s // tk <= LANES


def _pair_mask():
    r = lax.broadcasted_iota(jnp.int32, (LANES, 2 * RET_DV), 0) >= RET_DQK
    c = lax.broadcasted_iota(jnp.int32, (LANES, 2 * RET_DV), 1) >= RET_DV
    return (r == c).astype(F32)


def _ret_block(lg_ref, hp, i, t, qb, kb):
    w = _ret_weight(_lg_rows(lg_ref, hp, t), i, i, t, t)
    return _dot(_stack_heads(qb), kb, _NT), w


RET_PAIRS = 4


def _lanes(ref, p, width):
    return ref[:, p * width:(p + 1) * width]


def _ret_fwd(qk_rot, proj, gn_g, log_gamma, t, riders=None):
    s = qk_rot.shape[0]
    n_pair = HEADS // 2
    pw = 2 * RET_DV
    wq, wv = RET_PAIRS * LANES, RET_PAIRS * pw
    v_off, gate_off = 2 * RET_QK // wv, (2 * RET_QK + RET_V) // wv
    assert s % t == 0 and t % CHUNK == 0 and t & (t - 1) == 0 and n_pair % RET_PAIRS == 0

    def body(lg_ref, q_ref, k_ref, v_ref, g_ref, w_ref, ret_ref, rg_ref, state_ref):
        hg, i = pl.program_id(0), pl.program_id(1)

        @pl.when(i == 0)
        def _():
            state_ref[...] = jnp.zeros_like(state_ref)

        pairs = range(RET_PAIRS)
        qbs = [_lanes(q_ref, p, LANES) for p in pairs]
        kbs = [_lanes(k_ref, p, LANES) for p in pairs]
        vbs = [_lanes(v_ref, p, pw) for p in pairs]
        zws = [_ret_block(lg_ref, hg * RET_PAIRS + p, i, t, qbs[p], kbs[p]) for p in pairs]
        ps = [(z * w).astype(BF16) for z, w in zws]
        outs = [jnp.concatenate([_dot(ps[p][:t], vbs[p][:, 0:RET_DV]), _dot(ps[p][t:], vbs[p][:, RET_DV:pw])], axis=1)
                + _dot(qbs[p], state_ref[p]) for p in pairs]
        for p in pairs:
            state_ref[p] += _pair_mask() * _dot(kbs[p], vbs[p], _TN)
        for p in pairs:
            for hh in range(2):
                cols = slice(p * pw + hh * RET_DV, p * pw + (hh + 1) * RET_DV)
                o = outs[p][:, hh * RET_DV:(hh + 1) * RET_DV]
                ret_ref[:, cols] = o
                mu = jnp.sum(o, axis=1, keepdims=True) * (1.0 / RET_DV)
                xc = o - mu
                var = jnp.sum(xc * xc, axis=1, keepdims=True) * (1.0 / RET_DV)
                nrm = xc * lax.rsqrt(var + EPS) * w_ref[:, cols]
                g = g_ref[:, cols].astype(F32)
                rg_ref[:, cols] = (g * _sigmoid(g) * nrm).astype(BF16)

    blk = lambda hg, i: (i, hg)
    return _call("ret_fwd", body, (n_pair // RET_PAIRS, s // t),
                 [(log_gamma, None, pltpu.SMEM),
                  (qk_rot, (t, wq), blk),
                  (qk_rot, (t, wq), lambda hg, i: (i, n_pair // RET_PAIRS + hg)),
                  (proj, (t, wv), lambda hg, i: (i, v_off + hg)),
                  (proj, (t, wv), lambda hg, i: (i, gate_off + hg)),
                  (gn_g, (1, wv), lambda hg, i: (0, hg))],
                 [((s, RET_V), F32, (t, wv), blk), ((s, RET_V), BF16, (t, wv), blk)],
                 scratch=[pltpu.VMEM((RET_PAIRS, LANES, pw), F32)], riders=riders)


def _tri(tk, strict_upper):
    r = lax.broadcasted_iota(jnp.int32, (tk, tk), 0)
    cc = lax.broadcasted_iota(jnp.int32, (tk, tk), 1)
    return ((r > cc) if strict_upper else (r < cc)).astype(BF16)


def _diagonal_step(i, tq, tk, make, carry):
    if (tq // tk) % SB_GROUP == 0:
        return make(SB_GROUP)(0, carry)
    assert 2 * (tq // tk) == SB_GROUP
    half = lax.rem(i, 2) == 0
    return lax.cond(half, lambda cr: make(SB_GROUP // 2)(0, cr), lambda cr: make(SB_GROUP)(0, cr), carry)


def _sb_valid(i, j, tq, tk):
    row, col = _tile_pos(i, j, tq, tk)
    return col < row


def _sb_fwd(q_sb, proj, tq, tk, riders=None):
    s = q_sb.shape[0]
    k_off = (2 * RET_QK + 2 * RET_V + SB_W) // LANES
    n_pair = HEADS // 2
    _check_tiles(s, tq, tk, SB_GROUP)

    def body(q_ref, k_ref, v_ref, o_ref, a_ref):
        i = pl.program_id(1)
        upper = _tri(tk, True)
        qs = _stack_heads(q_ref[...])
        n_full, n_groups = _n_full(i, tq, tk, SB_GROUP), _n_groups(i, tq, tk, SB_GROUP)

        def make_step(near_diagonal, last, n_sub=SB_GROUP):
            def step(n, carry):
                c, o = carry
                g = last - 1 - n
                js = [g * SB_GROUP + sub for sub in range(n_sub)]
                zs = [_dot(qs, k_ref[_key_rows(j, tk), :], _NT) for j in js]
                log1ps = [jnp.log2(1.0 + jnp.exp2(-jnp.abs(z))) for z in zs]
                log_1ms = [-jnp.maximum(z, 0.0) - t for z, t in zip(zs, log1ps)]
                log_bs = [jnp.minimum(z, 0.0) - t for z, t in zip(zs, log1ps)]
                if near_diagonal:
                    valids = [_sb_valid(i, j, tq, tk) for j in js]
                    log_1ms = [jnp.where(v, l, 0.0) for v, l in zip(valids, log_1ms)]
                sticks = [_dot(l, upper) for l in log_1ms]
                sums = [jnp.sum(l, axis=1, keepdims=True) for l in log_1ms]
                cs = [None] * n_sub
                for sub in reversed(range(n_sub)):
                    cs[sub] = c
                    c = c + sums[sub]
                for sub, j in enumerate(js):
                    a = jnp.exp2(log_bs[sub] + sticks[sub] + cs[sub])
                    if near_diagonal:
                        a = jnp.where(valids[sub], a, 0.0)
                    a = a.astype(BF16)
                    a_ref[j] = a
                    o = o + _dot(_side_by_side(a, tq), _stack_heads(v_ref[_key_rows(j, tk), :]))
                return c, o
            return step

        carry = (jnp.zeros((2 * tq, 1), F32), jnp.zeros((tq, LANES), F32))
        carry = _diagonal_step(i, tq, tk, lambda n_sub: make_step(True, n_groups, n_sub), carry)
        _, acc = lax.fori_loop(0, n_full, make_step(False, n_full), carry)
        o_ref[...] = acc.astype(BF16)

    n_kb = s // tk
    return _call("sb_fwd", body, (n_pair, s // tq),
                 [(q_sb, (tq, LANES), lambda hp, i: (i, hp)),
                  (proj, (s, LANES), lambda hp, i: (0, k_off + hp)),
                  (proj, (s, LANES), lambda hp, i: (0, k_off + n_pair + hp))],
                 [((s, SB_W), BF16, (tq, LANES), lambda hp, i: (i, hp)),
                  ((n_pair, s // tq, n_kb, 2 * tq, tk), BF16, (None, None, n_kb, 2 * tq, tk),
                   lambda hp, i: (hp, i, 0, 0, 0))], riders=riders)


def _merge_out(retg, sb, w_ret, w_sb_t, w_out, proj, x, mod, gp1, g2, tm):
    s, d = x.shape
    gw = min(512, d)
    n_g = d // gw
    ar_off = (2 * RET_QK + 2 * RET_V + 3 * SB_W) // gw

    def body(rg_ref, sb_ref, wr_ref, ws_ref, wo_ref, *refs):
        gate_refs, (x_ref, mod_ref, gp_ref, g2_ref, mix_ref, r_ref, s_ref, y_ref, hres_ref, h2_ref) = refs[:2 * n_g], refs[2 * n_g:]
        for rows in _pieces(tm):
            rr = _dot(rg_ref[rows, :], wr_ref[...])
            ss = _dot(sb_ref[rows, :], ws_ref[...], _NT)
            a_r = jnp.concatenate([g[rows, :] for g in gate_refs[:n_g]], axis=1).astype(F32)
            a_s = jnp.concatenate([g[rows, :] for g in gate_refs[n_g:]], axis=1).astype(F32)
            mixed = (_sigmoid(a_r) * rr + _sigmoid(a_s) * ss).astype(BF16)
            mix_ref[rows, :] = mixed
            r_ref[rows, :] = rr.astype(BF16)
            s_ref[rows, :] = ss.astype(BF16)
            y = _dot(mixed, wo_ref[...])
            y_ref[rows, :] = y
            ny, _ = _rms(y, d)
            hres = x_ref[rows, :] + mod_ref[:, 2 * d:3 * d] * (ny * gp_ref[...])
            hres_ref[rows, :] = hres
            n2, _ = _rms(hres, d)
            h2_ref[rows, :] = (n2 * g2_ref[...] * (1.0 + mod_ref[:, 4 * d:5 * d]) + mod_ref[:, 3 * d:4 * d]).astype(BF16)

    row = lambda i: (i, 0)
    fix = lambda i: (0, 0)
    tile_bf = ((s, d), BF16, (tm, d), row)
    tile_f = ((s, d), F32, (tm, d), row)
    return _call("merge_out", body, (s // tm,),
                 [(retg, (tm, RET_V), row), (sb, (tm, SB_W), row), (w_ret, (RET_V, d), fix), (w_sb_t, (d, SB_W), fix),
                  (w_out, (d, d), fix)]
                 + [(proj, (tm, gw), functools.partial(lambda i, k: (i, ar_off + k), k=k)) for k in range(2 * n_g)]
                 + [(x, (tm, d), row), (mod, (1, 6 * d), fix), (gp1, (1, d), fix), (g2, (1, d), fix)],
                 [tile_bf, tile_bf, tile_bf, tile_f, tile_f, tile_bf])


def _ff1(h2, w_ff1_t, tm, tn):
    s, f = h2.shape[0], w_ff1_t.shape[0]
    tm = min(tm, s)

    def body(a_ref, w_ref, u_ref, act_ref):
        u = _dot(a_ref[...], w_ref[...], _NT)
        r = jnp.maximum(u, 0.0)
        u_ref[...] = u.astype(BF16)
        act_ref[...] = (r * r).astype(BF16)

    d = h2.shape[1]
    return _call("ff1", body, (f // tn, s // tm),
                 [(h2, (tm, d), lambda j, i: (i, 0)), (w_ff1_t, (tn, d), lambda j, i: (j, 0))],
                 [((s, f), BF16, (tm, tn), lambda j, i: (i, j))] * 2)


def _ff2_loss(act, w_ff2, hres, target, mod, gp2, tm):
    s, d = hres.shape
    f = act.shape[1]

    def body(a_ref, w_ref, h_ref, t_ref, mod_ref, gp_ref, dout_ref, df_ref, loss_ref, dgt_ref, dgp_ref):
        _zero_at_start([loss_ref, dgt_ref, dgp_ref])
        gt, gp = mod_ref[:, 5 * d:6 * d], gp_ref[...]
        for rows in _pieces(tm):
            ff = _dot(a_ref[rows, :], w_ref[...])
            nf, rf = _rms(ff, d)
            out = h_ref[rows, :] + gt * (nf * gp)
            err = out - t_ref[rows, :]
            sq = jnp.sum(err * err, axis=1, keepdims=True)
            loss_ref[...] += jnp.sum(sq, axis=0, keepdims=True)
            dout = err * (1.0 / d)
            dout_ref[rows, :] = dout
            dgt_ref[...] += _colsum(dout * (nf * gp))
            dgp_ref[...] += _colsum(dout * gt * nf)
            df_ref[rows, :] = _rms_bwd(dout * gt * gp, nf, rf, d).astype(BF16)

    row = lambda i: (i, 0)
    fix = lambda i: (0, 0)
    return _call("ff2_loss", body, (s // tm,),
                 [(act, (tm, f), row), (w_ff2, (f, d), fix), (hres, (tm, d), row), (target, (tm, d), row),
                  (mod, (1, 6 * d), fix), (gp2, (1, d), fix)],
                 [((s, d), F32, (tm, d), row), ((s, d), BF16, (tm, d), row), ((1, 1), F32, (1, 1), fix),
                  ((1, d), F32, (1, d), fix), ((1, d), F32, (1, d), fix)])


def _ffn_bwd(df, w_ff2, u, act, h2, tn):
    s, d = df.shape
    f = w_ff2.shape[0]

    def body(df_ref, w_ref, u_ref, act_ref, h2_ref, du_ref, gw2_ref, gw1_ref):
        dfb = df_ref[...]
        du = (_dot(dfb, w_ref[...], _NT) * (2.0 * jnp.maximum(u_ref[...].astype(F32), 0.0))).astype(BF16)
        du_ref[...] = du
        gw2_ref[...] = _dot(act_ref[...], dfb, _TN).astype(BF16)
        gw1_ref[...] = _dot(h2_ref[...], du, _TN).astype(BF16)

    fix = lambda j: (0, 0)
    col = lambda j: (0, j)
    return _call("ffn_bwd", body, (f // tn,),
                 [(df, (s, d), fix), (w_ff2, (tn, d), lambda j: (j, 0)), (u, (s, tn), col), (act, (s, tn), col),
                  (h2, (s, d), fix)],
                 [((s, f), BF16, (s, tn), col), ((f, d), BF16, (tn, d), lambda j: (j, 0)),
                  ((f // tn, d, tn), BF16, (None, d, tn), lambda j: (j, 0, 0))])


def _ff1_bwd(du, w_ff1_t, hres, dout, y, mixed, mod, g2, gp1, tm, riders=None):
    s, d = hres.shape
    f = du.shape[1]

    def body(a_ref, w_ref, h_ref, do_ref, y_ref, mix_ref, mod_ref, g2_ref, gp_ref,
             dh_ref, dy_ref, dsh_ref, dsc_ref, dg2_ref, dgt_ref, dgp_ref, gwo_ref, acc_ref):
        _zero_at_start([dsh_ref, dsc_ref, dg2_ref, dgt_ref, dgp_ref, acc_ref])
        g2, sc2 = g2_ref[...], mod_ref[:, 4 * d:5 * d]
        gt, gp = mod_ref[:, 2 * d:3 * d], gp_ref[...]
        for rows in _pieces(tm):
            dh2 = _dot(a_ref[rows, :], w_ref[...])
            n2, r2 = _rms(h_ref[rows, :], d)
            dsh_ref[...] += _colsum(dh2)
            dsc_ref[...] += _colsum(dh2 * n2 * g2)
            dg2_ref[...] += _colsum(dh2 * n2 * (1.0 + sc2))
            dhres = do_ref[rows, :] + _rms_bwd(dh2 * g2 * (1.0 + sc2), n2, r2, d)
            dh_ref[rows, :] = dhres
            ny, ry = _rms(y_ref[rows, :], d)
            dgt_ref[...] += _colsum(dhres * (ny * gp))
            dgp_ref[...] += _colsum(dhres * gt * ny)
            dy_ref[rows, :] = _rms_bwd(dhres * gt * gp, ny, ry, d).astype(BF16)
        acc_ref[...] += _dot(mix_ref[...], dy_ref[...], _TN)

        @pl.when(pl.program_id(0) == s // tm - 1)
        def _():
            gwo_ref[...] = acc_ref[...].astype(BF16)

    row = lambda i: (i, 0)
    fix = lambda i: (0, 0)
    vec = ((1, d), F32, (1, d), fix)
    return _call("ff1_bwd", body, (s // tm,),
                 [(du, (tm, f), row), (w_ff1_t, (f, d), fix), (hres, (tm, d), row), (dout, (tm, d), row),
                  (y, (tm, d), row), (mixed, (tm, d), row), (mod, (1, 6 * d), fix), (g2, (1, d), fix), (gp1, (1, d), fix)],
                 [((s, d), F32, (tm, d), row), ((s, d), BF16, (tm, d), row), vec, vec, vec, vec, vec,
                  ((d, d), BF16, (d, d), fix)], scratch=[pltpu.VMEM((d, d), F32)], riders=riders)


def _out_bwd(dy, w_out, proj, r_bf, s_bf, tm, tn, riders=None):
    s, d = dy.shape
    ar_off = (2 * RET_QK + 2 * RET_V + 3 * SB_W) // tn
    as_off = ar_off + d // tn

    def body(a_ref, w_ref, ar_ref, as_ref, r_ref, s_ref, dr_ref, ds_ref, dar_ref, das_ref):
        dm = _dot(a_ref[...], w_ref[...], _NT)
        sr, ss = _sigmoid(ar_ref[...].astype(F32)), _sigmoid(as_ref[...].astype(F32))
        dr_ref[...] = (dm * sr).astype(BF16)
        ds_ref[...] = (dm * ss).astype(BF16)
        dar_ref[...] = (dm * r_ref[...].astype(F32) * sr * (1.0 - sr)).astype(BF16)
        das_ref[...] = (dm * s_ref[...].astype(F32) * ss * (1.0 - ss)).astype(BF16)

    tile = (tm, tn)
    here = lambda j, i: (i, j)
    return _call("out_bwd", body, (d // tn, s // tm),
                 [(dy, (tm, d), lambda j, i: (i, 0)), (w_out, (tn, d), lambda j, i: (j, 0)),
                  (proj, tile, lambda j, i: (i, ar_off + j)), (proj, tile, lambda j, i: (i, as_off + j)),
                  (r_bf, tile, here), (s_bf, tile, here)],
                 [((s, d), BF16, tile, here)] * 4, riders=riders)


def _branch_bwd(d_r, d_s, retg, sb, w_ret, w_sb_t, ret, proj, gn_g, riders=None):
    s, d = d_r.shape
    n_step = 4
    part_v, part_s, part_d = RET_V // n_step, SB_W // n_step, d // n_step
    per_dev = d // N_DEV
    n_blk = part_d // per_dev
    gate_off = (2 * RET_QK + RET_V) // part_v

    def body(dr_ref, ds_ref, rg_ref, sb_ref, wr_ref, ws_ref, r_ref, g_ref, w_ref,
             dsb_ref, gwr_ref, gws_ref, dg_ref, dret_ref, dw_ref):
        i = pl.program_id(0)
        dr, ds = dr_ref[...], ds_ref[...]
        dsb_ref[...] = _dot(ds, ws_ref[...]).astype(BF16)
        gwr_ref[...] = _dot(rg_ref[...], dr, _TN).astype(BF16)
        cols = pl.ds(pl.multiple_of(i * part_d, part_d), part_d)
        gws = _dot(sb_ref[...], ds_ref[:, cols], _TN).astype(BF16)
        for k in range(n_blk):
            gws_ref[k] = gws[:, k * per_dev:(k + 1) * per_dev]
        dretg = _dot(dr, wr_ref[...], _NT)
        for h in range(part_v // RET_DV):
            cols = slice(h * RET_DV, (h + 1) * RET_DV)
            o, g, w, d_o = r_ref[:, cols], g_ref[:, cols].astype(F32), w_ref[:, cols], dretg[:, cols]
            mu = jnp.sum(o, axis=1, keepdims=True) * (1.0 / RET_DV)
            xc = o - mu
            rstd = lax.rsqrt(jnp.sum(xc * xc, axis=1, keepdims=True) * (1.0 / RET_DV) + EPS)
            n = xc * rstd
            sg = _sigmoid(g)
            silu = g * sg
            dg_ref[:, cols] = (d_o * n * w * (sg * (1.0 + g * (1.0 - sg)))).astype(BF16)
            dw_ref[:, cols] = _colsum(d_o * silu * n)
            dn = d_o * silu * w
            m1 = jnp.sum(dn, axis=1, keepdims=True) * (1.0 / RET_DV)
            m2 = jnp.sum(dn * n, axis=1, keepdims=True) * (1.0 / RET_DV)
            dret_ref[:, cols] = (rstd * (dn - m1 - n * m2)).astype(BF16)

    fix = lambda i: (0, 0)
    col = lambda i: (0, i)
    return _call("branch_bwd", body, (n_step,),
                 [(d_r, (s, d), fix), (d_s, (s, d), fix), (retg, (s, part_v), col), (sb, (s, SB_W), fix),
                  (w_ret, (part_v, d), lambda i: (i, 0)), (w_sb_t, (d, part_s), col),
                  (ret, (s, part_v), col), (proj, (s, part_v), lambda i: (0, gate_off + i)), (gn_g, (1, part_v), col)],
                 [((s, SB_W), BF16, (s, part_s), col), ((RET_V, d), BF16, (part_v, d), lambda i: (i, 0)),
                  ((N_DEV, SB_W, per_dev), BF16, (n_blk, SB_W, per_dev), lambda i: (i, 0, 0)),
                  ((s, RET_V), BF16, (s, part_v), col), ((s, RET_V), BF16, (s, part_v), col),
                  ((1, RET_V), F32, (1, part_v), col)], riders=riders)


def _ret_bwd(qk_rot, proj, dret, log_gamma, t, riders=None):
    s = qk_rot.shape[0]
    n_pair = HEADS // 2
    pw = 2 * RET_DV
    wq, wv = RET_PAIRS * LANES, RET_PAIRS * pw
    n_blk = s // t
    pairs = range(RET_PAIRS)

    def load(q_ref, k_ref, v_ref, do_ref):
        return ([_lanes(q_ref, p, LANES) for p in pairs], [_lanes(k_ref, p, LANES) for p in pairs],
                [_lanes(v_ref, p, pw) for p in pairs], [_lanes(do_ref, p, pw) for p in pairs])

    def d_scores(lg_ref, hp, i, qb, kb, vb, dob):
        z, w = _ret_block(lg_ref, hp, i, t, qb, kb)
        dp = jnp.concatenate([_dot(dob[:, 0:RET_DV], vb[:, 0:RET_DV], _NT),
                              _dot(dob[:, RET_DV:pw], vb[:, RET_DV:pw], _NT)], axis=0)
        return (z * w).astype(BF16), (dp * w).astype(BF16)

    def up_body(lg_ref, q_ref, k_ref, v_ref, do_ref, dq_ref, state_ref):
        hg, i = pl.program_id(0), pl.program_id(1)

        @pl.when(i == 0)
        def _():
            state_ref[...] = jnp.zeros_like(state_ref)

        qbs, kbs, vbs, dobs = load(q_ref, k_ref, v_ref, do_ref)
        dss = [d_scores(lg_ref, hg * RET_PAIRS + p, i, qbs[p], kbs[p], vbs[p], dobs[p])[1] for p in pairs]
        for p in pairs:
            dq_ref[:, p * LANES:(p + 1) * LANES] = (_dot(_side_by_side(dss[p], t), _stack_heads(kbs[p]))
                                                    + _dot(dobs[p], state_ref[p], _NT)).astype(BF16)
        for p in pairs:
            state_ref[p] += _pair_mask() * _dot(kbs[p], vbs[p], _TN)

    def down_body(lg_ref, q_ref, k_ref, v_ref, do_ref, dk_ref, dv_ref, state_ref):
        hg, i = pl.program_id(0), n_blk - 1 - pl.program_id(1)

        @pl.when(pl.program_id(1) == 0)
        def _():
            state_ref[...] = jnp.zeros_like(state_ref)

        qbs, kbs, vbs, dobs = load(q_ref, k_ref, v_ref, do_ref)
        both = [d_scores(lg_ref, hg * RET_PAIRS + p, i, qbs[p], kbs[p], vbs[p], dobs[p]) for p in pairs]
        for p in pairs:
            pp, ds = both[p]
            later = state_ref[p]
            dv_ref[:, p * pw:(p + 1) * pw] = (jnp.concatenate(
                [_dot(pp[:t], dobs[p][:, 0:RET_DV], _TN), _dot(pp[t:], dobs[p][:, RET_DV:pw], _TN)],
                axis=1) + _dot(kbs[p], later)).astype(BF16)
            dk_ref[:, p * LANES:(p + 1) * LANES] = (_dot(ds, _stack_heads(qbs[p]), _TN)
                                                    + _dot(vbs[p], later, _NT)).astype(BF16)
        for p in pairs:
            state_ref[p] += _pair_mask() * _dot(qbs[p], dobs[p], _TN)

    n_grp = n_pair // RET_PAIRS

    def ins(order):
        return [(log_gamma, None, pltpu.SMEM),
                (qk_rot, (t, wq), lambda hg, i: (order(i), hg)),
                (qk_rot, (t, wq), lambda hg, i: (order(i), n_grp + hg)),
                (proj, (t, wv), lambda hg, i: (order(i), 2 * RET_QK // wv + hg)),
                (dret, (t, wv), lambda hg, i: (order(i), hg))]

    up = lambda i: i
    down = lambda i: n_blk - 1 - i
    scratch = [pltpu.VMEM((RET_PAIRS, LANES, pw), F32)]
    dq = _call("ret_bwd_q", up_body, (n_grp, n_blk), ins(up),
               [((s, RET_QK), BF16, (t, wq), lambda hg, i: (i, hg))], scratch=scratch)[0]
    dk, dv, *rest = _call("ret_bwd_kv", down_body, (n_grp, n_blk), ins(down),
                          [((s, RET_QK), BF16, (t, wq), lambda hg, i: (down(i), hg)),
                           ((s, RET_V), BF16, (t, wv), lambda hg, i: (down(i), hg))],
                          scratch=scratch, riders=riders)
    return [dq, dk, dv] + rest


def _sb_bwd(q_sb, proj, weights, do, tq, tk, riders=None):
    s = q_sb.shape[0]
    k_off = (2 * RET_QK + 2 * RET_V + SB_W) // LANES
    n_pair = HEADS // 2
    _check_tiles(s, tq, tk, SB_GROUP)

    def body(q_ref, k_ref, v_ref, a_ref, do_ref, dq_ref, dk_ref, dv_ref):
        i = pl.program_id(1)

        @pl.when(i == 0)
        def _():
            dk_ref[...] = jnp.zeros_like(dk_ref)
            dv_ref[...] = jnp.zeros_like(dv_ref)

        lower = _tri(tk, False)
        qs = _stack_heads(q_ref[...])
        dos = _stack_heads(do_ref[...].astype(BF16))

        def make_step(near_diagonal, n_sub=SB_GROUP):
            def step(g, carry):
                c_e, dq = carry
                js = [g * SB_GROUP + sub for sub in range(n_sub)]
                rows = [_key_rows(j, tk) for j in js]
                zs = [_dot(qs, k_ref[rw, :], _NT) for rw in rows]
                das = [_dot(dos, v_ref[rw, :], _NT) for rw in rows]
                avals = [a_ref[j] for j in js]
                for a, rw in zip(avals, rows):
                    dv_ref[rw, :] += _dot(a, dos, _TN)
                es = [a.astype(F32) * da for a, da in zip(avals, das)]
                prefixes = [_dot(e, lower) for e in es]
                betas = [1.0 / (1.0 + jnp.exp2(-z)) for z in zs]
                for sub in range(n_sub):
                    dz = es[sub] - (es[sub] + prefixes[sub] + c_e) * betas[sub]
                    if near_diagonal:
                        dz = jnp.where(_sb_valid(i, js[sub], tq, tk), dz, 0.0)
                    dz = dz.astype(BF16)
                    dk_ref[rows[sub], :] += _dot(dz, qs, _TN)
                    dq = dq + _dot(_side_by_side(dz, tq), _stack_heads(k_ref[rows[sub], :]))
                    c_e = c_e + jnp.sum(es[sub], axis=1, keepdims=True)
                return c_e, dq
            return step

        n_full = _n_full(i, tq, tk, SB_GROUP)
        carry = (jnp.zeros((2 * tq, 1), F32), jnp.zeros((tq, LANES), F32))
        carry = lax.fori_loop(0, n_full, make_step(False), carry)
        _, dq = _diagonal_step(i, tq, tk, lambda n_sub: (lambda n, cr: make_step(True, n_sub)(n_full, cr)), carry)
        dq_ref[...] = dq

    blk = lambda hp, i: (i, hp)
    n_kb = s // tk
    return _call("sb_bwd", body, (n_pair, s // tq),
                 [(q_sb, (tq, LANES), blk),
                  (proj, (s, LANES), lambda hp, i: (0, k_off + hp)),
                  (proj, (s, LANES), lambda hp, i: (0, k_off + n_pair + hp)),
                  (weights, (None, None, n_kb, 2 * tq, tk), lambda hp, i: (hp, i, 0, 0, 0)),
                  (do, (tq, LANES), blk)],
                 [((s, SB_W), F32, (tq, LANES), blk),
                  ((s, SB_W), F32, (s, LANES), lambda hp, i: (0, hp)),
                  ((s, SB_W), F32, (s, LANES), lambda hp, i: (0, hp))], riders=riders)


def _assemble_dproj(dq_r, dk_r, dv_r, dg_r, dq_s, dk_s, dv_s, da_r, da_s, cos, sin, idx_col, lg_lanes, tm, riders=None):
    s, d = da_r.shape
    width = 2 * RET_QK + 2 * RET_V + 3 * SB_W + 2 * d

    def body(dq_ref, dk_ref, dv_ref, dg_ref, dqs_ref, dks_ref, dvs_ref, dar_ref, das_ref, cos_ref, sin_ref,
             idx_ref, lg_ref, o_ref):
        lane = lax.broadcasted_iota(jnp.int32, (1, LANES), 1)
        first = jnp.bitwise_and(lane, RET_DQK - 1) < (RET_DQK // 2)
        cos, sin = cos_ref[...], sin_ref[...]
        idx = idx_ref[...]
        for src, base, sign, scale in ((dq_ref, 0, 1.0, 1.0), (dk_ref, RET_QK, -1.0, RET_DQK ** -0.5)):
            for g in range(RET_QK // LANES):
                v = src[:, g * LANES:(g + 1) * LANES].astype(F32) * (_decay_scale(lg_ref, idx, g, sign) * scale)
                sw = jnp.where(first, pltpu.roll(v, LANES - RET_DQK // 2, 1), pltpu.roll(v, RET_DQK // 2, 1))
                o_ref[:, base + g * LANES:base + (g + 1) * LANES] = (v * cos - sw * sin).astype(BF16)
        off = 2 * RET_QK
        o_ref[:, off:off + RET_V] = dv_ref[...].astype(BF16)
        off += RET_V
        o_ref[:, off:off + RET_V] = dg_ref[...]
        off += RET_V
        o_ref[:, off:off + SB_W] = (dqs_ref[...] * (SB_DH ** -0.5)).astype(BF16)
        off += SB_W
        o_ref[:, off:off + SB_W] = (dks_ref[...] * LN2).astype(BF16)
        off += SB_W
        o_ref[:, off:off + SB_W] = dvs_ref[...].astype(BF16)
        off += SB_W
        o_ref[:, off:off + d] = dar_ref[...]
        off += d
        o_ref[:, off:off + d] = das_ref[...]

    row = lambda i: (i, 0)
    ins = [(a, (tm, a.shape[1]), row) for a in (dq_r, dk_r, dv_r, dg_r, dq_s, dk_s, dv_s, da_r, da_s, cos, sin, idx_col)]
    ins.append((lg_lanes, (1, RET_QK), lambda i: (0, 0)))
    return _call("assemble_dproj", body, (s // tm,), ins, [((s, width), BF16, (tm, width), row)], riders=riders)


def _in_bwd(dproj, w_in_t, x, dhres, mod, g1, tm, riders=None):
    s, d = x.shape
    width = dproj.shape[1]

    def body(a_ref, w_ref, x_ref, dh_ref, mod_ref, g_ref, dx_ref, dsh_ref, dsc_ref, dg_ref):
        _zero_at_start([dsh_ref, dsc_ref, dg_ref])
        g1, sc1 = g_ref[...], mod_ref[:, d:2 * d]
        for rows in _pieces(tm):
            dh = _dot(a_ref[rows, :], w_ref[...])
            n1, r1 = _rms(x_ref[rows, :], d)
            dsh_ref[...] += _colsum(dh)
            dsc_ref[...] += _colsum(dh * n1 * g1)
            dg_ref[...] += _colsum(dh * n1 * (1.0 + sc1))
            dx_ref[rows, :] = dh_ref[rows, :] + _rms_bwd(dh * g1 * (1.0 + sc1), n1, r1, d)

    row = lambda i: (i, 0)
    fix = lambda i: (0, 0)
    vec = ((1, d), F32, (1, d), fix)
    return _call("in_bwd", body, (s // tm,),
                 [(dproj, (tm, width), row), (w_in_t, (width, d), fix), (x, (tm, d), row), (dhres, (tm, d), row),
                  (mod, (1, 6 * d), fix), (g1, (1, d), fix)],
                 [((s, d), F32, (tm, d), row), vec, vec, vec], riders=riders)


def _adamw(w, g, m, v):
    m = ADAM_B1 * m + (1.0 - ADAM_B1) * g
    v = ADAM_B2 * v + (1.0 - ADAM_B2) * (g * g)
    m_hat = m / (1.0 - ADAM_B1 ** ADAM_STEP)
    v_hat = v / (1.0 - ADAM_B2 ** ADAM_STEP)
    delta = -ADAM_LR * (m_hat / (jnp.sqrt(v_hat) + ADAM_EPS) + ADAM_WD * w)
    return delta, m, v


def _adam_reduce(name, sets, steps):
    n = len(sets)

    def body(*refs):
        for k in range(n):
            p_ref, w_ref, m_ref, v_ref = refs[4 * k:4 * k + 4]
            outs = refs[4 * n + 4 * k:4 * n + 4 * k + 4]
            g = p_ref[0].astype(F32)
            for j in range(1, p_ref.shape[0]):
                g = g + p_ref[j].astype(F32)
            for o_ref, val in zip(outs, (g,) + _adamw(w_ref[...], g, m_ref[...], v_ref[...])):
                o_ref[...] = val

    ins, outs = [], []
    row = lambda i: (i, 0)
    for parts, w, m, v in sets:
        rws, cls = w.shape
        tr = rws // steps
        assert tr * steps == rws and tr % 16 == 0
        ins += [(parts, (parts.shape[0], tr, cls), lambda i: (0, i, 0)), (w, (tr, cls), row), (m, (tr, cls), row),
                (v, (tr, cls), row)]
        outs += [((rws, cls), F32, (tr, cls), row)] * 4
    res = _call(name, body, (steps,), ins, outs)
    return [res[4 * k:4 * k + 4] for k in range(n)]


def _ada_bwd_adam(cs_t, dmod_cols, w, m, v):
    d, nc = w.shape

    def body(c_ref, dm_ref, w_ref, m_ref, v_ref, g_out, d_out, m_out, v_out):
        g = c_ref[0] * dm_ref[0:1, :]
        for r in range(1, N_DEV):
            g = g + c_ref[r] * dm_ref[r:r + 1, :]
        delta, mn, vn = _adamw(w_ref[...], g, m_ref[...], v_ref[...])
        g_out[...] = g
        d_out[...] = delta
        m_out[...] = mn
        v_out[...] = vn

    fix = lambda i: (0, 0)
    blk = (d, nc)
    return _call("ada_bwd_adam", body, (1,),
                 [(cs_t, (N_DEV, d, 1), lambda i: (0, 0, 0)), (dmod_cols, (N_DEV, nc), fix), (w, blk, fix), (m, blk, fix), (v, blk, fix)],
                 [((d, nc), F32, blk, fix)] * 4)


def _small_adam(parts, ws, ms, vs):
    n = len(ws)
    widths = [w.shape[1] for w in ws]
    total = parts.shape[1]
    assert sum(widths) + LANES == total

    def body(p_ref, *refs):
        w_refs, m_refs, v_refs = refs[:n], refs[n:2 * n], refs[2 * n:3 * n]
        outs = refs[3 * n:]
        g = p_ref[0:1, :]
        for k in range(1, N_DEV):
            g = g + p_ref[k:k + 1, :]
        off = 0
        for i, width in enumerate(widths):
            gi = g[:, off:off + width]
            delta, mn, vn = _adamw(w_refs[i][...], gi, m_refs[i][...], v_refs[i][...])
            for o_ref, val in zip(outs[4 * i:4 * i + 4], (gi, delta, mn, vn)):
                o_ref[...] = val
            off += width
        outs[4 * n][...] = g[:, off:off + LANES]

    fix = lambda i: (0, 0)
    vec = lambda a: (a, (1, a.shape[1]), fix)
    out_specs = [((1, width), F32, (1, width), fix) for width in widths for _ in range(4)]
    out_specs.append(((1, LANES), F32, (1, LANES), fix))
    res = _call("small_adam", body, (1,),
                [(parts, (N_DEV, total), fix)] + [vec(a) for a in list(ws) + list(ms) + list(vs)], out_specs)
    return [res[4 * i:4 * i + 4] for i in range(n)], res[4 * n]


def kernel(x, c, positions, ada_w, ada_b, pre_mix_g, post_mix_g, pre_ffn_g, post_ffn_g, w_in, ret_gn_g, w_ret_branch, w_sb_branch, w_out, w_ff1, w_ff2, loss_target, m_ada_w, m_ada_b, m_pre_mix_g, m_post_mix_g, m_pre_ffn_g, m_post_ffn_g, m_w_in, m_ret_gn_g, m_w_ret_branch, m_w_sb_branch, m_w_out, m_w_ff1, m_w_ff2, v_ada_w, v_ada_b, v_pre_mix_g, v_post_mix_g, v_pre_ffn_g, v_post_ffn_g, v_w_in, v_ret_gn_g, v_w_ret_branch, v_w_sb_branch, v_w_out, v_w_ff1, v_w_ff2):
    _, s, d = x.shape
    d_ff = w_ff1.shape[2] * N_DEV
    d_in = w_in.shape[2] * N_DEV
    me = 4 * lax.axis_index("x") + 2 * lax.axis_index("y") + lax.axis_index("c")
    x2, tgt = x[0], loss_target[0]

    core = lax.axis_index("c").astype(jnp.int32).reshape(1)
    bf = lambda w: w[0].astype(BF16)

    w_in_t, m_in_t, v_in_t = (jnp.swapaxes(a[0], 0, 1) for a in (w_in, m_w_in, v_w_in))

    c_all, g_in = _exchange("gather_in", [c, w_in_t.astype(BF16)], ["gather", "gather_chip"])
    c_all = c_all.reshape(N_DEV, d)

    n_ada = ada_w.shape[2]
    ada_b_cols = lax.dynamic_slice(ada_b, (0, me * n_ada), (1, n_ada))
    cs_all, mod_cols = _ada_fwd(c_all, ada_w[0], ada_b_cols)
    mod_all = _exchange("gather_mod", [mod_cols], ["gather"])[0]
    mod = lax.dynamic_index_in_dim(mod_all, me, axis=1, keepdims=False).reshape(1, 6 * d)

    tm = min(256, s)
    h, g_in = _pre_norm(x2, pre_mix_g, mod, 2 * tm, riders=([g_in], ["forward"]))
    wt_in = g_in.reshape(d_in, d)
    bf_t = lambda w: jnp.swapaxes(w[0], 0, 1).astype(BF16)
    small_w = [bf(w_ret_branch), bf_t(w_sb_branch), bf(w_out)]
    proj, *small_w = _matmul("in_proj", h, wt_in, "nt", s, 512, BF16, riders=(small_w, ["gather_chip"] * 3))
    pos_col = positions.reshape(s, 1).astype(F32)
    freqs = ROPE_BASE ** (-jnp.arange(0, RET_DQK, 2, dtype=F32) / RET_DQK)
    inv_freq = jnp.tile(freqs, LANES // (RET_DQK // 2)).reshape(1, LANES)
    log_gamma_np = np.log1p(-(2.0 ** (-5.0 - np.arange(HEADS))))
    log_gamma = jnp.asarray(log_gamma_np, F32)
    lg_lanes = jnp.asarray(np.repeat(log_gamma_np, RET_DQK).reshape(1, RET_QK), F32)
    idx_col = (jnp.arange(s, dtype=F32) - (s // 2)).reshape(s, 1)
    qk_rot, q_sb, cos_t, sin_t = _prep(proj, pos_col, idx_col, inv_freq, lg_lanes, 2 * tm)
    tq, tk = min(256, s), min(128, s)
    tq_sb = min(SB_TQ, s)
    sb, sb_weights, *big_w = _sb_fwd(q_sb, proj, tq_sb, tk, riders=([bf(w_ff2), bf_t(w_ff1)], ["gather_chip"] * 2))
    ret, retg, g_ret, g_sb, g_out, g_ff2, g_ff1 = _ret_fwd(qk_rot, proj, ret_gn_g, log_gamma, tq,
                                                           riders=(small_w + big_w, ["forward"] * 5))
    wf_ret = g_ret.reshape(RET_V, d)
    wt_sb = g_sb.reshape(d, SB_W)
    wf_out = g_out.reshape(d, d)
    wt_ff1 = g_ff1.reshape(d_ff, d)
    wf_ff2 = g_ff2.reshape(d_ff, d)
    mixed, r_bf, s_bf, y, hres, h2 = _merge_out(retg, sb, wf_ret, wt_sb, wf_out, proj, x2, mod, post_mix_g, pre_ffn_g, tm)
    u, act = _ff1(h2, wt_ff1, s, 512)
    dout, df, loss_sum, d_gt2, d_gp2 = _ff2_loss(act, wf_ff2, hres, tgt, mod, post_ffn_g, tm)

    du, gw_ff2, gw_ff1 = _ffn_bwd(df, wf_ff2, u, act, h2, d_ff // N_DEV)
    gw_ff2 = gw_ff2.reshape(N_DEV, d_ff // N_DEV, d)
    dhres, dy, d_sh2, d_sc2, d_g2, d_gt1, d_gp1, gw_out, t_ff1, t_ff2 = _ff1_bwd(
        du, wt_ff1, hres, dout, y, mixed, mod, pre_ffn_g, post_mix_g, tm, riders=([gw_ff1, gw_ff2], ["pair"] * 2))
    gw_out = gw_out.reshape(N_DEV, d // N_DEV, d)
    s_ff1, s_ff2 = _pair_sum("pair_sum_ff", [(gw_ff1, t_ff1), (gw_ff2, t_ff2)], core)
    d_r, d_s, da_r, da_s = _out_bwd(dy, wf_out, proj, r_bf, s_bf, 2 * tm, min(512, d))
    dsb, gw_ret, gw_sb, dg_r, dret, d_gn, p_out = _branch_bwd(d_r, d_s, retg, sb, wf_ret, wt_sb, ret, proj, ret_gn_g,
                                                              riders=([gw_out], ["scatter"]))
    gw_ret = gw_ret.reshape(N_DEV, RET_V // N_DEV, d)
    dq_s, dk_s, dv_s, p_ff1, p_ff2 = _sb_bwd(q_sb, proj, sb_weights, dsb, tq_sb, tk,
                                             riders=([s_ff1, s_ff2], ["chip_scatter"] * 2))
    dq_r, dk_r, dv_r, p_sb = _ret_bwd(qk_rot, proj, dret, log_gamma, tq, riders=([gw_sb], ["scatter"]))
    dproj = _assemble_dproj(dq_r, dk_r, dv_r, dg_r, dq_s, dk_s, dv_s, da_r, da_s, cos_t, sin_t, idx_col, lg_lanes, tm)[0]
    gw_in, p_ret = _matmul("grad_w_in", dproj, h, "tn", 512, d, BF16, riders=([gw_ret], ["scatter"]))
    gw_in = gw_in.reshape(N_DEV, d_in // N_DEV, d)
    t_in = _exchange("pair_in", [gw_in], ["pair"])[0]
    s_in = _pair_sum("pair_sum_in", [(gw_in, t_in)], core)[0]
    grad_x, d_sh1, d_sc1, d_g1, p_in = _in_bwd(dproj, wt_in, x2, dhres, mod, pre_mix_g, tm,
                                               riders=([s_in], ["chip_scatter"]))
    loss_lanes = jnp.pad(loss_sum, ((0, 0), (0, LANES - 1)))
    small = jnp.concatenate([d_sh1, d_sc1, d_gt1, d_sh2, d_sc2, d_gt2, d_g1, d_gp1, d_g2, d_gp2, d_gn, loss_lanes], axis=1)
    small_all = _exchange("gather_small", [small], ["gather"])[0].reshape(N_DEV, small.shape[1])
    parts = [p_in, p_ret, p_sb, p_out, p_ff1, p_ff2]

    res = {}
    names = ["w_ret_branch", "w_sb_branch", "w_out", "w_ff1", "w_ff2"]
    ws = [w_ret_branch, w_sb_branch, w_out, w_ff1, w_ff2]
    ms = [m_w_ret_branch, m_w_sb_branch, m_w_out, m_w_ff1, m_w_ff2]
    vs = [v_w_ret_branch, v_w_sb_branch, v_w_out, v_w_ff1, v_w_ff2]
    sets = [(parts[0], w_in_t, m_in_t, v_in_t)] + [(p, w[0], m[0], v[0]) for p, w, m, v in zip(parts[1:], ws, ms, vs)]
    updated = _adam_reduce("adam_big", sets, 4)
    res["w_in"] = [jnp.swapaxes(o, 0, 1)[None] for o in updated[0]]
    for nm, outs4 in zip(names, updated[1:]):
        res[nm] = [o[None] for o in outs4]
    dmod_cols = lax.dynamic_slice(small_all, (0, me * n_ada), (N_DEV, n_ada))
    res["ada_w"] = [o[None] for o in _ada_bwd_adam(cs_all.reshape(N_DEV, d, 1), dmod_cols, ada_w[0], m_ada_w[0], v_ada_w[0])]
    vec_names = ["ada_b", "pre_mix_g", "post_mix_g", "pre_ffn_g", "post_ffn_g", "ret_gn_g"]
    vec_res, loss_lanes = _small_adam(small_all,
                                      [ada_b, pre_mix_g, post_mix_g, pre_ffn_g, post_ffn_g, ret_gn_g],
                                      [m_ada_b, m_pre_mix_g, m_post_mix_g, m_pre_ffn_g, m_post_ffn_g, m_ret_gn_g],
                                      [v_ada_b, v_pre_mix_g, v_post_mix_g, v_pre_ffn_g, v_post_ffn_g, v_ret_gn_g])
    res.update(zip(vec_names, vec_res))
    loss = (0.5 / d) * loss_lanes[0, 0]
    order = ["ada_w", "ada_b", "pre_mix_g", "post_mix_g", "pre_ffn_g", "post_ffn_g", "w_in", "ret_gn_g",
             "w_ret_branch", "w_sb_branch", "w_out", "w_ff1", "w_ff2"]
    outs = [loss, grad_x[None]]
    for k in range(4):
        outs += [res[nm][k] for nm in order]
    return tuple(outs)
```

```python
import functools

import numpy as np
import jax
import jax.numpy as jnp
from jax import lax
from jax.experimental import pallas as pl
from jax.experimental.pallas import tpu as pltpu

F32 = jnp.float32
BF16 = jnp.bfloat16
N_DEV = 8
AXES = ("x", "y", "c")

EPS = 1e-6
CHUNK = 64
CHUNK_SHIFT = 6
HEADS = 8
RET_DQK = 64
RET_DV = 128
SB_DH = 64
RET_QK = HEADS * RET_DQK
RET_V = HEADS * RET_DV
SB_W = HEADS * SB_DH
ROPE_BASE = 10000.0
LANES = 128

ADAM_LR = 0.001
ADAM_B1 = 0.9
ADAM_B2 = 0.999
ADAM_EPS = 1e-08
ADAM_WD = 0.01
ADAM_STEP = 10

VMEM_LIMIT = 56 * 1024 * 1024

_NN = (((1,), (0,)), ((), ()))
_NT = (((1,), (1,)), ((), ()))
_TN = (((0,), (0,)), ((), ()))


def _dot(a, b, dims=_NN):
    if a.dtype != BF16:
        a = a.astype(BF16)
    if b.dtype != BF16:
        b = b.astype(BF16)
    return lax.dot_general(a, b, dims, preferred_element_type=F32)


def _sigmoid(x):
    return 1.0 / (1.0 + jnp.exp(-x))


def _rms(x, d):
    r = lax.rsqrt(jnp.sum(x * x, axis=1, keepdims=True) * (1.0 / d) + EPS)
    return x * r, r


def _rms_bwd(dn, n, r, d):
    return r * (dn - n * (jnp.sum(dn * n, axis=1, keepdims=True) * (1.0 / d)))


def _colsum(v):
    return jnp.sum(v, axis=0, keepdims=True)


ROW_SPLIT = 2


def _zero_at_start(refs):
    @pl.when(pl.program_id(0) == 0)
    def _():
        for r in refs:
            r[...] = jnp.zeros_like(r)


def _pieces(tm):
    step = tm // ROW_SPLIT
    return [slice(k * step, (k + 1) * step) for k in range(ROW_SPLIT)]


KIND_SLOTS = {"gather": N_DEV, "scatter": N_DEV, "gather_chip": N_DEV, "forward": N_DEV, "pair": N_DEV // 2,
              "chip_scatter": N_DEV // 2}
SEMS_PER_ARRAY = N_DEV - 1


def _exchange_copies(ins, outs, send_sems, recv_sems, local_sems, kinds):
    x, y, c = (lax.axis_index(a) for a in AXES)
    me, chip, sibling = 4 * x + 2 * y + c, 2 * x + y, (x, y, 1 - c)
    mesh_id = pl.DeviceIdType.MESH
    other_chips = []
    for k in range(1, N_DEV // 2):
        px = 1 - x if k & 2 else x
        py = 1 - y if k & 1 else y
        other_chips.append((px, py))
    copies = []
    for i, kind in enumerate(kinds):
        def remote(src, dst, k, to, i=i):
            return pltpu.make_async_remote_copy(
                src_ref=src, dst_ref=dst, send_sem=send_sems.at[i * SEMS_PER_ARRAY + k],
                recv_sem=recv_sems.at[i * SEMS_PER_ARRAY + k], device_id=to, device_id_type=mesh_id)

        if kind in ("gather", "scatter"):
            pick = (lambda ref, d: ref.at[d]) if kind == "scatter" else (lambda ref, d: ref)
            copies.append(pltpu.make_async_copy(pick(ins[i], me), outs[i].at[me], local_sems.at[i]))
            for k in range(1, N_DEV):
                to = (1 - x if k & 4 else x, 1 - y if k & 2 else y, 1 - c if k & 1 else c)
                copies.append(remote(pick(ins[i], 4 * to[0] + 2 * to[1] + to[2]), outs[i].at[me], k - 1, to))
        elif kind == "gather_chip":
            copies.append(pltpu.make_async_copy(ins[i], outs[i].at[me], local_sems.at[i]))
            copies.append(remote(ins[i], outs[i].at[me], 0, sibling))
            for k, (px, py) in enumerate(other_chips):
                copies.append(remote(ins[i], outs[i].at[me], 1 + k, (px, py, c)))
        elif kind == "forward":
            for k, (px, py) in enumerate(other_chips):
                slot = 4 * px + 2 * py + c
                copies.append(remote(outs[i].at[slot], outs[i].at[slot], k, sibling))
        elif kind == "pair":
            for k in range(N_DEV // 2):
                copies.append(remote(ins[i].at[2 * k + 1 - c], outs[i].at[k], k, sibling))
        elif kind == "chip_scatter":
            copies.append(pltpu.make_async_copy(ins[i].at[chip], outs[i].at[chip], local_sems.at[i]))
            for k, (px, py) in enumerate(other_chips):
                copies.append(remote(ins[i].at[2 * px + py], outs[i].at[chip], k, (px, py, c)))
        else:
            raise ValueError(kind)
    return copies


def _exchange_shapes(arrays, kinds):
    shapes = []
    for a, kind in zip(arrays, kinds):
        tail = a.shape if kind in ("gather", "gather_chip") else a.shape[1:]
        shapes.append(jax.ShapeDtypeStruct((KIND_SLOTS[kind],) + tuple(tail), a.dtype))
    return shapes


def _exchange_sems(n):
    return [pltpu.SemaphoreType.DMA((n * SEMS_PER_ARRAY,)), pltpu.SemaphoreType.DMA((n * SEMS_PER_ARRAY,)),
            pltpu.SemaphoreType.DMA((n,))]


def _call(name, body, grid, ins, outs, scratch=(), riders=None, prefetch=None):
    any_spec = pl.BlockSpec(memory_space=pl.ANY)
    in_specs = [pl.BlockSpec(memory_space=im) if bs is None else pl.BlockSpec(bs, im) for _, bs, im in ins]
    out_specs = [pl.BlockSpec(bs, im) for _, _, bs, im in outs]
    out_shape = [jax.ShapeDtypeStruct(s, d) for s, d, _, _ in outs]
    operands = [a for a, _, _ in ins]
    scratch = list(scratch)
    aliases = {}
    n_pre = 0 if prefetch is None else 1
    kernel = functools.partial(body) if prefetch is None else (lambda _, *refs: body(*refs))
    if riders is not None:
        arrays, kinds = riders
        nr, n_in, n_out, n_scr = len(arrays), len(ins), len(outs), len(scratch)

        def kernel(*refs):
            refs = refs[n_pre:]
            own_in, ride_in = refs[:n_in], refs[n_in:n_in + nr]
            own_out = refs[n_in + nr:n_in + nr + n_out]
            ride_out = refs[n_in + nr + n_out:n_in + 2 * nr + n_out]
            own_scr = refs[n_in + 2 * nr + n_out:n_in + 2 * nr + n_out + n_scr]
            sems = refs[n_in + 2 * nr + n_out + n_scr:]
            ids = [pl.program_id(a) for a in range(len(grid))]
            first = functools.reduce(jnp.logical_and, [i == 0 for i in ids])
            last = functools.reduce(jnp.logical_and, [i == g - 1 for i, g in zip(ids, grid)])

            @pl.when(first)
            def _():
                for cp in _exchange_copies(ride_in, ride_out, *sems, kinds):
                    cp.start()

            body(*own_in, *own_out, *own_scr)

            @pl.when(last)
            def _():
                for cp in _exchange_copies(ride_in, ride_out, *sems, kinds):
                    cp.wait()

        in_specs += [any_spec] * nr
        out_specs += [any_spec] * nr
        out_shape += _exchange_shapes(arrays, kinds)
        operands += list(arrays)
        scratch += _exchange_sems(nr)
        aliases = {n_pre + n_in + r: n_out + r for r, kind in enumerate(kinds) if kind == "forward"}
    params = pltpu.CompilerParams(dimension_semantics=("arbitrary",) * len(grid), vmem_limit_bytes=VMEM_LIMIT)
    if prefetch is None:
        return pl.pallas_call(kernel, name=name, grid=grid, in_specs=in_specs, out_specs=out_specs,
                              out_shape=out_shape, scratch_shapes=scratch, input_output_aliases=aliases,
                              compiler_params=params)(*operands)
    grid_spec = pltpu.PrefetchScalarGridSpec(num_scalar_prefetch=1, grid=grid, in_specs=in_specs,
                                             out_specs=out_specs, scratch_shapes=scratch)
    return pl.pallas_call(kernel, name=name, grid_spec=grid_spec, out_shape=out_shape,
                          input_output_aliases=aliases, compiler_params=params)(prefetch, *operands)


def _exchange(name, arrays, kinds):
    n = len(arrays)

    def body(*refs):
        copies = _exchange_copies(refs[:n], refs[n:2 * n], *refs[2 * n:], kinds)
        for cp in copies:
            cp.start()
        for cp in copies:
            cp.wait()

    any_spec = pl.BlockSpec(memory_space=pl.ANY)
    return pl.pallas_call(
        functools.partial(body),
        name=name,
        in_specs=[any_spec] * n,
        out_specs=[any_spec] * n,
        out_shape=_exchange_shapes(arrays, kinds),
        scratch_shapes=_exchange_sems(n),
        input_output_aliases={i: i for i, kind in enumerate(kinds) if kind == "forward"},
    )(*arrays)


def _pair_sum(name, pairs, my_core):
    n = len(pairs)

    def body(*refs):
        for k in range(n):
            a_ref, b_ref, o_ref = refs[2 * k], refs[2 * k + 1], refs[2 * n + k]
            o_ref[...] = (a_ref[...].astype(F32) + b_ref[...].astype(F32)).astype(o_ref.dtype)

    ins, outs = [], []
    for mine, theirs in pairs:
        _, rws, cls = mine.shape
        ins += [(mine, (None, rws, cls), lambda k, core: (2 * k + core[0], 0, 0)),
                (theirs, (None, rws, cls), lambda k, core: (k, 0, 0))]
        outs.append(((N_DEV // 2, rws, cls), mine.dtype, (None, rws, cls), lambda k, core: (k, 0, 0)))
    return _call(name, body, (N_DEV // 2,), ins, outs, prefetch=my_core)


def _matmul(name, a, b, kind, tm, tn, out_dtype, blocked_out=False, riders=None):
    if kind == "tn":
        kdim, m = a.shape
    else:
        m, kdim = a.shape
    n = b.shape[0] if kind == "nt" else b.shape[1]
    tm, tn = min(tm, m), min(tn, n)
    dims = {"nn": _NN, "nt": _NT, "tn": _TN}[kind]

    def body(a_ref, b_ref, o_ref):
        o_ref[...] = _dot(a_ref[...], b_ref[...], dims).astype(o_ref.dtype)

    a_spec = (a, (kdim, tm), lambda j, i: (0, i)) if kind == "tn" else (a, (tm, kdim), lambda j, i: (i, 0))
    b_spec = (b, (tn, kdim), lambda j, i: (j, 0)) if kind == "nt" else (b, (kdim, tn), lambda j, i: (0, j))
    if blocked_out:
        out = ((n // tn, m, tn), out_dtype, (None, tm, tn), lambda j, i: (j, i, 0))
    else:
        out = ((m, n), out_dtype, (tm, tn), lambda j, i: (i, j))
    res = _call(name, body, (n // tn, m // tm), [a_spec, b_spec], [out], riders=riders)
    return res[0] if riders is None else res


def _ada_fwd(c_all, ada_w, ada_b_cols):
    def body(c_ref, w_ref, b_ref, cs_ref, o_ref):
        v = c_ref[...]
        cs = v * _sigmoid(v)
        cs_ref[...] = cs
        o_ref[...] = lax.dot_general(cs, w_ref[...], _NN, preferred_element_type=F32,
                                     precision=lax.Precision.HIGHEST) + b_ref[...]

    r, d = c_all.shape
    nc = ada_w.shape[1]
    fix = lambda i: (0, 0)
    return _call("ada_fwd", body, (1,),
                 [(c_all, (r, d), fix), (ada_w, (d, nc), fix), (ada_b_cols, (1, nc), fix)],
                 [((r, d), F32, (r, d), fix), ((r, nc), F32, (r, nc), fix)])


def _pre_norm(x, g, mod, tm, riders=None):
    s, d = x.shape

    def body(x_ref, g_ref, mod_ref, h_ref):
        n, _ = _rms(x_ref[...], d)
        sh, sc = mod_ref[:, 0:d], mod_ref[:, d:2 * d]
        h_ref[...] = (n * g_ref[...] * (1.0 + sc) + sh).astype(BF16)

    return _call("pre_norm", body, (s // tm,),
                 [(x, (tm, d), lambda i: (i, 0)), (g, (1, d), lambda i: (0, 0)),
                  (mod, (1, 6 * d), lambda i: (0, 0))],
                 [((s, d), BF16, (tm, d), lambda i: (i, 0))], riders=riders)


LOG2E = 1.4426950408889634
LN2 = 0.6931471805599453


def _decay_scale(lg_ref, idx, g, sign):
    return jnp.exp((sign * idx) * lg_ref[:, g * LANES:(g + 1) * LANES])


def _prep(proj, pos_col, idx_col, inv_freq, lg_lanes, tm):
    s = proj.shape[0]
    sb_off = (2 * RET_QK + 2 * RET_V) // SB_W
    n_q = RET_QK // LANES

    def body(qk_ref, qs_ref, pos_ref, idx_ref, f_ref, lg_ref, qk_out, qs_out, cos_out, sin_out):
        ang = pos_ref[...] * f_ref[...]
        lane = lax.broadcasted_iota(jnp.int32, (1, LANES), 1)
        first = jnp.bitwise_and(lane, RET_DQK - 1) < (RET_DQK // 2)
        cos = jnp.cos(ang)
        sin = jnp.where(first, -1.0, 1.0) * jnp.sin(ang)
        cos_out[...] = cos
        sin_out[...] = sin
        idx = idx_ref[...]
        for g in range(2 * n_q):
            v = qk_ref[:, g * LANES:(g + 1) * LANES].astype(F32)
            sw = jnp.where(first, pltpu.roll(v, LANES - RET_DQK // 2, 1), pltpu.roll(v, RET_DQK // 2, 1))
            r = v * cos + sw * sin
            if g < n_q:
                r = r * _decay_scale(lg_ref, idx, g, 1.0)
            else:
                r = r * (_decay_scale(lg_ref, idx, g - n_q, -1.0) * (RET_DQK ** -0.5))
            qk_out[:, g * LANES:(g + 1) * LANES] = r.astype(BF16)
        qs_out[...] = (qs_ref[...].astype(F32) * (SB_DH ** -0.5 * LOG2E)).astype(BF16)

    row = lambda i: (i, 0)
    return _call("prep", body, (s // tm,),
                 [(proj, (tm, 2 * RET_QK), row),
                  (proj, (tm, SB_W), lambda i: (i, sb_off)),
                  (pos_col, (tm, 1), row),
                  (idx_col, (tm, 1), row),
                  (inv_freq, (1, LANES), lambda i: (0, 0)),
                  (lg_lanes, (1, RET_QK), lambda i: (0, 0))],
                 [((s, 2 * RET_QK), BF16, (tm, 2 * RET_QK), row),
                  ((s, SB_W), BF16, (tm, SB_W), row),
                  ((s, LANES), F32, (tm, LANES), row),
                  ((s, LANES), F32, (tm, LANES), row)])


def _head_mask(hh):
    lane = lax.broadcasted_iota(jnp.int32, (1, LANES), 1)
    return (lane >= RET_DQK) if hh else (lane < RET_DQK)


def _masked(v, m):
    return jnp.where(m, v, jnp.zeros_like(v))


SB_GROUP = 4
SB_TQ = 256


def _stack_heads(v):
    return jnp.concatenate([_masked(v, _head_mask(0)), _masked(v, _head_mask(1))], axis=0)


def _side_by_side(v, t):
    return jnp.concatenate([v[:t], v[t:]], axis=1)


def _tile_pos(i, j, tq, tk):
    row = jnp.bitwise_and(lax.broadcasted_iota(jnp.int32, (2 * tq, tk), 0), tq - 1) + i * tq
    col = lax.broadcasted_iota(jnp.int32, (2 * tq, tk), 1) + j * tk
    return row, col


def _n_groups(i, tq, tk, grp):
    return ((i + 1) * (tq // tk) + grp - 1) // grp


def _n_full(i, tq, tk, grp):
    return (i * (tq // tk)) // grp


def _key_rows(j, tk):
    return pl.ds(pl.multiple_of(j * tk, tk), tk)


def _ret_weight(lg_rows, i, j, tq, tk):
    row, col = _tile_pos(i, j, tq, tk)
    same = jnp.right_shift(col, CHUNK_SHIFT) == jnp.right_shift(row, CHUNK_SHIFT)
    later = jnp.where(same, jnp.exp((2.0 * lg_rows) * (col - row).astype(F32)), 0.0)
    return jnp.where(col <= row, 1.0, later)


def _lg_rows(lg_ref, hp, tq):
    first = lax.broadcasted_iota(jnp.int32, (2 * tq, 1), 0) < tq
    return jnp.where(first, lg_ref[2 * hp], lg_ref[2 * hp + 1])


def _check_tiles(s, tq, tk, grp):
    assert tq % tk == 0 and tq & (tq - 1) == 0 and tk & (tk - 1) == 0
    assert s % tq == 0 and (s // tk) % grp == 0 and s // tk <= LANES


def _pair_mask():
    r = lax.broadcasted_iota(jnp.int32, (LANES, 2 * RET_DV), 0) >= RET_DQK
    c = lax.broadcasted_iota(jnp.int32, (LANES, 2 * RET_DV), 1) >= RET_DV
    return (r == c).astype(F32)


def _ret_block(lg_ref, hp, i, t, qb, kb):
    w = _ret_weight(_lg_rows(lg_ref, hp, t), i, i, t, t)
    return _dot(_stack_heads(qb), kb, _NT), w


RET_PAIRS = 4


def _lanes(ref, p, width):
    return ref[:, p * width:(p + 1) * width]


def _ret_fwd(qk_rot, proj, gn_g, log_gamma, t, riders=None):
    s = qk_rot.shape[0]
    n_pair = HEADS // 2
    pw = 2 * RET_DV
    wq, wv = RET_PAIRS * LANES, RET_PAIRS * pw
    v_off, gate_off = 2 * RET_QK // wv, (2 * RET_QK + RET_V) // wv
    assert s % t == 0 and t % CHUNK == 0 and t & (t - 1) == 0 and n_pair % RET_PAIRS == 0

    def body(lg_ref, q_ref, k_ref, v_ref, g_ref, w_ref, ret_ref, rg_ref, state_ref):
        hg, i = pl.program_id(0), pl.program_id(1)

        @pl.when(i == 0)
        def _():
            state_ref[...] = jnp.zeros_like(state_ref)

        pairs = range(RET_PAIRS)
        qbs = [_lanes(q_ref, p, LANES) for p in pairs]
        kbs = [_lanes(k_ref, p, LANES) for p in pairs]
        vbs = [_lanes(v_ref, p, pw) for p in pairs]
        zws = [_ret_block(lg_ref, hg * RET_PAIRS + p, i, t, qbs[p], kbs[p]) for p in pairs]
        ps = [(z * w).astype(BF16) for z, w in zws]
        outs = [jnp.concatenate([_dot(ps[p][:t], vbs[p][:, 0:RET_DV]), _dot(ps[p][t:], vbs[p][:, RET_DV:pw])], axis=1)
                + _dot(qbs[p], state_ref[p]) for p in pairs]
        for p in pairs:
            state_ref[p] += _pair_mask() * _dot(kbs[p], vbs[p], _TN)
        for p in pairs:
            for hh in range(2):
                cols = slice(p * pw + hh * RET_DV, p * pw + (hh + 1) * RET_DV)
                o = outs[p][:, hh * RET_DV:(hh + 1) * RET_DV]
                ret_ref[:, cols] = o
                mu = jnp.sum(o, axis=1, keepdims=True) * (1.0 / RET_DV)
                xc = o - mu
                var = jnp.sum(xc * xc, axis=1, keepdims=True) * (1.0 / RET_DV)
                nrm = xc * lax.rsqrt(var + EPS) * w_ref[:, cols]
                g = g_ref[:, cols].astype(F32)
                rg_ref[:, cols] = (g * _sigmoid(g) * nrm).astype(BF16)

    blk = lambda hg, i: (i, hg)
    return _call("ret_fwd", body, (n_pair // RET_PAIRS, s // t),
                 [(log_gamma, None, pltpu.SMEM),
                  (qk_rot, (t, wq), blk),
                  (qk_rot, (t, wq), lambda hg, i: (i, n_pair // RET_PAIRS + hg)),
                  (proj, (t, wv), lambda hg, i: (i, v_off + hg)),
                  (proj, (t, wv), lambda hg, i: (i, gate_off + hg)),
                  (gn_g, (1, wv), lambda hg, i: (0, hg))],
                 [((s, RET_V), F32, (t, wv), blk), ((s, RET_V), BF16, (t, wv), blk)],
                 scratch=[pltpu.VMEM((RET_PAIRS, LANES, pw), F32)], riders=riders)


def _tri(tk, strict_upper):
    r = lax.broadcasted_iota(jnp.int32, (tk, tk), 0)
    cc = lax.broadcasted_iota(jnp.int32, (tk, tk), 1)
    return ((r > cc) if strict_upper else (r < cc)).astype(BF16)


def _diagonal_step(i, tq, tk, make, carry):
    if (tq // tk) % SB_GROUP == 0:
        return make(SB_GROUP)(0, carry)
    assert 2 * (tq // tk) == SB_GROUP
    half = lax.rem(i, 2) == 0
    return lax.cond(half, lambda cr: make(SB_GROUP // 2)(0, cr), lambda cr: make(SB_GROUP)(0, cr), carry)


def _sb_valid(i, j, tq, tk):
    row, col = _tile_pos(i, j, tq, tk)
    return col < row


def _sb_fwd(q_sb, proj, tq, tk, riders=None):
    s = q_sb.shape[0]
    k_off = (2 * RET_QK + 2 * RET_V + SB_W) // LANES
    n_pair = HEADS // 2
    _check_tiles(s, tq, tk, SB_GROUP)

    def body(q_ref, k_ref, v_ref, o_ref, a_ref):
        i = pl.program_id(1)
        upper = _tri(tk, True)
        qs = _stack_heads(q_ref[...])
        n_full, n_groups = _n_full(i, tq, tk, SB_GROUP), _n_groups(i, tq, tk, SB_GROUP)

        def make_step(near_diagonal, last, n_sub=SB_GROUP):
            def step(n, carry):
                c, o = carry
                g = last - 1 - n
                js = [g * SB_GROUP + sub for sub in range(n_sub)]
                zs = [_dot(qs, k_ref[_key_rows(j, tk), :], _NT) for j in js]
                log1ps = [jnp.log2(1.0 + jnp.exp2(-jnp.abs(z))) for z in zs]
                log_1ms = [-jnp.maximum(z, 0.0) - t for z, t in zip(zs, log1ps)]
                log_bs = [jnp.minimum(z, 0.0) - t for z, t in zip(zs, log1ps)]
                if near_diagonal:
                    valids = [_sb_valid(i, j, tq, tk) for j in js]
                    log_1ms = [jnp.where(v, l, 0.0) for v, l in zip(valids, log_1ms)]
                sticks = [_dot(l, upper) for l in log_1ms]
                sums = [jnp.sum(l, axis=1, keepdims=True) for l in log_1ms]
                cs = [None] * n_sub
                for sub in reversed(range(n_sub)):
                    cs[sub] = c
                    c = c + sums[sub]
                for sub, j in enumerate(js):
                    a = jnp.exp2(log_bs[sub] + sticks[sub] + cs[sub])
                    if near_diagonal:
                        a = jnp.where(valids[sub], a, 0.0)
                    a = a.astype(BF16)
                    a_ref[j] = a
                    o = o + _dot(_side_by_side(a, tq), _stack_heads(v_ref[_key_rows(j, tk), :]))
                return c, o
            return step

        carry = (jnp.zeros((2 * tq, 1), F32), jnp.zeros((tq, LANES), F32))
        carry = _diagonal_step(i, tq, tk, lambda n_sub: make_step(True, n_groups, n_sub), carry)
        _, acc = lax.fori_loop(0, n_full, make_step(False, n_full), carry)
        o_ref[...] = acc.astype(BF16)

    n_kb = s // tk
    return _call("sb_fwd", body, (n_pair, s // tq),
                 [(q_sb, (tq, LANES), lambda hp, i: (i, hp)),
                  (proj, (s, LANES), lambda hp, i: (0, k_off + hp)),
                  (proj, (s, LANES), lambda hp, i: (0, k_off + n_pair + hp))],
                 [((s, SB_W), BF16, (tq, LANES), lambda hp, i: (i, hp)),
                  ((n_pair, s // tq, n_kb, 2 * tq, tk), BF16, (None, None, n_kb, 2 * tq, tk),
                   lambda hp, i: (hp, i, 0, 0, 0))], riders=riders)


def _merge_out(retg, sb, w_ret, w_sb_t, w_out, proj, x, mod, gp1, g2, tm):
    s, d = x.shape
    gw = min(512, d)
    n_g = d // gw
    ar_off = (2 * RET_QK + 2 * RET_V + 3 * SB_W) // gw

    def body(rg_ref, sb_ref, wr_ref, ws_ref, wo_ref, *refs):
        gate_refs, (x_ref, mod_ref, gp_ref, g2_ref, mix_ref, r_ref, s_ref, y_ref, hres_ref, h2_ref) = refs[:2 * n_g], refs[2 * n_g:]
        for rows in _pieces(tm):
            rr = _dot(rg_ref[rows, :], wr_ref[...])
            ss = _dot(sb_ref[rows, :], ws_ref[...], _NT)
            a_r = jnp.concatenate([g[rows, :] for g in gate_refs[:n_g]], axis=1).astype(F32)
            a_s = jnp.concatenate([g[rows, :] for g in gate_refs[n_g:]], axis=1).astype(F32)
            mixed = (_sigmoid(a_r) * rr + _sigmoid(a_s) * ss).astype(BF16)
            mix_ref[rows, :] = mixed
            r_ref[rows, :] = rr.astype(BF16)
            s_ref[rows, :] = ss.astype(BF16)
            y = _dot(mixed, wo_ref[...])
            y_ref[rows, :] = y
            ny, _ = _rms(y, d)
            hres = x_ref[rows, :] + mod_ref[:, 2 * d:3 * d] * (ny * gp_ref[...])
            hres_ref[rows, :] = hres
            n2, _ = _rms(hres, d)
            h2_ref[rows, :] = (n2 * g2_ref[...] * (1.0 + mod_ref[:, 4 * d:5 * d]) + mod_ref[:, 3 * d:4 * d]).astype(BF16)

    row = lambda i: (i, 0)
    fix = lambda i: (0, 0)
    tile_bf = ((s, d), BF16, (tm, d), row)
    tile_f = ((s, d), F32, (tm, d), row)
    return _call("merge_out", body, (s // tm,),
                 [(retg, (tm, RET_V), row), (sb, (tm, SB_W), row), (w_ret, (RET_V, d), fix), (w_sb_t, (d, SB_W), fix),
                  (w_out, (d, d), fix)]
                 + [(proj, (tm, gw), functools.partial(lambda i, k: (i, ar_off + k), k=k)) for k in range(2 * n_g)]
                 + [(x, (tm, d), row), (mod, (1, 6 * d), fix), (gp1, (1, d), fix), (g2, (1, d), fix)],
                 [tile_bf, tile_bf, tile_bf, tile_f, tile_f, tile_bf])


def _ff1(h2, w_ff1_t, tm, tn):
    s, f = h2.shape[0], w_ff1_t.shape[0]
    tm = min(tm, s)

    def body(a_ref, w_ref, u_ref, act_ref):
        u = _dot(a_ref[...], w_ref[...], _NT)
        r = jnp.maximum(u, 0.0)
        u_ref[...] = u.astype(BF16)
        act_ref[...] = (r * r).astype(BF16)

    d = h2.shape[1]
    return _call("ff1", body, (f // tn, s // tm),
                 [(h2, (tm, d), lambda j, i: (i, 0)), (w_ff1_t, (tn, d), lambda j, i: (j, 0))],
                 [((s, f), BF16, (tm, tn), lambda j, i: (i, j))] * 2)


def _ff2_loss(act, w_ff2, hres, target, mod, gp2, tm):
    s, d = hres.shape
    f = act.shape[1]

    def body(a_ref, w_ref, h_ref, t_ref, mod_ref, gp_ref, dout_ref, df_ref, loss_ref, dgt_ref, dgp_ref):
        _zero_at_start([loss_ref, dgt_ref, dgp_ref])
        gt, gp = mod_ref[:, 5 * d:6 * d], gp_ref[...]
        for rows in _pieces(tm):
            ff = _dot(a_ref[rows, :], w_ref[...])
            nf, rf = _rms(ff, d)
            out = h_ref[rows, :] + gt * (nf * gp)
            err = out - t_ref[rows, :]
            sq = jnp.sum(err * err, axis=1, keepdims=True)
            loss_ref[...] += jnp.sum(sq, axis=0, keepdims=True)
            dout = err * (1.0 / d)
            dout_ref[rows, :] = dout
            dgt_ref[...] += _colsum(dout * (nf * gp))
            dgp_ref[...] += _colsum(dout * gt * nf)
            df_ref[rows, :] = _rms_bwd(dout * gt * gp, nf, rf, d).astype(BF16)

    row = lambda i: (i, 0)
    fix = lambda i: (0, 0)
    return _call("ff2_loss", body, (s // tm,),
                 [(act, (tm, f), row), (w_ff2, (f, d), fix), (hres, (tm, d), row), (target, (tm, d), row),
                  (mod, (1, 6 * d), fix), (gp2, (1, d), fix)],
                 [((s, d), F32, (tm, d), row), ((s, d), BF16, (tm, d), row), ((1, 1), F32, (1, 1), fix),
                  ((1, d), F32, (1, d), fix), ((1, d), F32, (1, d), fix)])


def _ffn_bwd(df, w_ff2, u, act, h2, tn):
    s, d = df.shape
    f = w_ff2.shape[0]

    def body(df_ref, w_ref, u_ref, act_ref, h2_ref, du_ref, gw2_ref, gw1_ref):
        dfb = df_ref[...]
        du = (_dot(dfb, w_ref[...], _NT) * (2.0 * jnp.maximum(u_ref[...].astype(F32), 0.0))).astype(BF16)
        du_ref[...] = du
        gw2_ref[...] = _dot(act_ref[...], dfb, _TN).astype(BF16)
        gw1_ref[...] = _dot(h2_ref[...], du, _TN).astype(BF16)

    fix = lambda j: (0, 0)
    col = lambda j: (0, j)
    return _call("ffn_bwd", body, (f // tn,),
                 [(df, (s, d), fix), (w_ff2, (tn, d), lambda j: (j, 0)), (u, (s, tn), col), (act, (s, tn), col),
                  (h2, (s, d), fix)],
                 [((s, f), BF16, (s, tn), col), ((f, d), BF16, (tn, d), lambda j: (j, 0)),
                  ((f // tn, d, tn), BF16, (None, d, tn), lambda j: (j, 0, 0))])


def _ff1_bwd(du, w_ff1_t, hres, dout, y, mixed, mod, g2, gp1, tm, riders=None):
    s, d = hres.shape
    f = du.shape[1]

    def body(a_ref, w_ref, h_ref, do_ref, y_ref, mix_ref, mod_ref, g2_ref, gp_ref,
             dh_ref, dy_ref, dsh_ref, dsc_ref, dg2_ref, dgt_ref, dgp_ref, gwo_ref, acc_ref):
        _zero_at_start([dsh_ref, dsc_ref, dg2_ref, dgt_ref, dgp_ref, acc_ref])
        g2, sc2 = g2_ref[...], mod_ref[:, 4 * d:5 * d]
        gt, gp = mod_ref[:, 2 * d:3 * d], gp_ref[...]
        for rows in _pieces(tm):
            dh2 = _dot(a_ref[rows, :], w_ref[...])
            n2, r2 = _rms(h_ref[rows, :], d)
            dsh_ref[...] += _colsum(dh2)
            dsc_ref[...] += _colsum(dh2 * n2 * g2)
            dg2_ref[...] += _colsum(dh2 * n2 * (1.0 + sc2))
            dhres = do_ref[rows, :] + _rms_bwd(dh2 * g2 * (1.0 + sc2), n2, r2, d)
            dh_ref[rows, :] = dhres
            ny, ry = _rms(y_ref[rows, :], d)
            dgt_ref[...] += _colsum(dhres * (ny * gp))
            dgp_ref[...] += _colsum(dhres * gt * ny)
            dy_ref[rows, :] = _rms_bwd(dhres * gt * gp, ny, ry, d).astype(BF16)
        acc_ref[...] += _dot(mix_ref[...], dy_ref[...], _TN)

        @pl.when(pl.program_id(0) == s // tm - 1)
        def _():
            gwo_ref[...] = acc_ref[...].astype(BF16)

    row = lambda i: (i, 0)
    fix = lambda i: (0, 0)
    vec = ((1, d), F32, (1, d), fix)
    return _call("ff1_bwd", body, (s // tm,),
                 [(du, (tm, f), row), (w_ff1_t, (f, d), fix), (hres, (tm, d), row), (dout, (tm, d), row),
                  (y, (tm, d), row), (mixed, (tm, d), row), (mod, (1, 6 * d), fix), (g2, (1, d), fix), (gp1, (1, d), fix)],
                 [((s, d), F32, (tm, d), row), ((s, d), BF16, (tm, d), row), vec, vec, vec, vec, vec,
                  ((d, d), BF16, (d, d), fix)], scratch=[pltpu.VMEM((d, d), F32)], riders=riders)


def _out_bwd(dy, w_out, proj, r_bf, s_bf, tm, tn, riders=None):
    s, d = dy.shape
    ar_off = (2 * RET_QK + 2 * RET_V + 3 * SB_W) // tn
    as_off = ar_off + d // tn

    def body(a_ref, w_ref, ar_ref, as_ref, r_ref, s_ref, dr_ref, ds_ref, dar_ref, das_ref):
        dm = _dot(a_ref[...], w_ref[...], _NT)
        sr, ss = _sigmoid(ar_ref[...].astype(F32)), _sigmoid(as_ref[...].astype(F32))
        dr_ref[...] = (dm * sr).astype(BF16)
        ds_ref[...] = (dm * ss).astype(BF16)
        dar_ref[...] = (dm * r_ref[...].astype(F32) * sr * (1.0 - sr)).astype(BF16)
        das_ref[...] = (dm * s_ref[...].astype(F32) * ss * (1.0 - ss)).astype(BF16)

    tile = (tm, tn)
    here = lambda j, i: (i, j)
    return _call("out_bwd", body, (d // tn, s // tm),
                 [(dy, (tm, d), lambda j, i: (i, 0)), (w_out, (tn, d), lambda j, i: (j, 0)),
                  (proj, tile, lambda j, i: (i, ar_off + j)), (proj, tile, lambda j, i: (i, as_off + j)),
                  (r_bf, tile, here), (s_bf, tile, here)],
                 [((s, d), BF16, tile, here)] * 4, riders=riders)


def _branch_bwd(d_r, d_s, retg, sb, w_ret, w_sb_t, ret, proj, gn_g, riders=None):
    s, d = d_r.shape
    n_step = 4
    part_v, part_s, part_d = RET_V // n_step, SB_W // n_step, d // n_step
    per_dev = d // N_DEV
    n_blk = part_d // per_dev
    gate_off = (2 * RET_QK + RET_V) // part_v

    def body(dr_ref, ds_ref, rg_ref, sb_ref, wr_ref, ws_ref, r_ref, g_ref, w_ref,
             dsb_ref, gwr_ref, gws_ref, dg_ref, dret_ref, dw_ref):
        i = pl.program_id(0)
        dr, ds = dr_ref[...], ds_ref[...]
        dsb_ref[...] = _dot(ds, ws_ref[...]).astype(BF16)
        gwr_ref[...] = _dot(rg_ref[...], dr, _TN).astype(BF16)
        cols = pl.ds(pl.multiple_of(i * part_d, part_d), part_d)
        gws = _dot(sb_ref[...], ds_ref[:, cols], _TN).astype(BF16)
        for k in range(n_blk):
            gws_ref[k] = gws[:, k * per_dev:(k + 1) * per_dev]
        dretg = _dot(dr, wr_ref[...], _NT)
        for h in range(part_v // RET_DV):
            cols = slice(h * RET_DV, (h + 1) * RET_DV)
            o, g, w, d_o = r_ref[:, cols], g_ref[:, cols].astype(F32), w_ref[:, cols], dretg[:, cols]
            mu = jnp.sum(o, axis=1, keepdims=True) * (1.0 / RET_DV)
            xc = o - mu
            rstd = lax.rsqrt(jnp.sum(xc * xc, axis=1, keepdims=True) * (1.0 / RET_DV) + EPS)
            n = xc * rstd
            sg = _sigmoid(g)
            silu = g * sg
            dg_ref[:, cols] = (d_o * n * w * (sg * (1.0 + g * (1.0 - sg)))).astype(BF16)
            dw_ref[:, cols] = _colsum(d_o * silu * n)
            dn = d_o * silu * w
            m1 = jnp.sum(dn, axis=1, keepdims=True) * (1.0 / RET_DV)
            m2 = jnp.sum(dn * n, axis=1, keepdims=True) * (1.0 / RET_DV)
            dret_ref[:, cols] = (rstd * (dn - m1 - n * m2)).astype(BF16)

    fix = lambda i: (0, 0)
    col = lambda i: (0, i)
    return _call("branch_bwd", body, (n_step,),
                 [(d_r, (s, d), fix), (d_s, (s, d), fix), (retg, (s, part_v), col), (sb, (s, SB_W), fix),
                  (w_ret, (part_v, d), lambda i: (i, 0)), (w_sb_t, (d, part_s), col),
                  (ret, (s, part_v), col), (proj, (s, part_v), lambda i: (0, gate_off + i)), (gn_g, (1, part_v), col)],
                 [((s, SB_W), BF16, (s, part_s), col), ((RET_V, d), BF16, (part_v, d), lambda i: (i, 0)),
                  ((N_DEV, SB_W, per_dev), BF16, (n_blk, SB_W, per_dev), lambda i: (i, 0, 0)),
                  ((s, RET_V), BF16, (s, part_v), col), ((s, RET_V), BF16, (s, part_v), col),
                  ((1, RET_V), F32, (1, part_v), col)], riders=riders)


def _ret_bwd(qk_rot, proj, dret, log_gamma, t, riders=None):
    s = qk_rot.shape[0]
    n_pair = HEADS // 2
    pw = 2 * RET_DV
    wq, wv = RET_PAIRS * LANES, RET_PAIRS * pw
    n_blk = s // t
    pairs = range(RET_PAIRS)

    def load(q_ref, k_ref, v_ref, do_ref):
        return ([_lanes(q_ref, p, LANES) for p in pairs], [_lanes(k_ref, p, LANES) for p in pairs],
                [_lanes(v_ref, p, pw) for p in pairs], [_lanes(do_ref, p, pw) for p in pairs])

    def d_scores(lg_ref, hp, i, qb, kb, vb, dob):
        z, w = _ret_block(lg_ref, hp, i, t, qb, kb)
        dp = jnp.concatenate([_dot(dob[:, 0:RET_DV], vb[:, 0:RET_DV], _NT),
                              _dot(dob[:, RET_DV:pw], vb[:, RET_DV:pw], _NT)], axis=0)
        return (z * w).astype(BF16), (dp * w).astype(BF16)

    def up_body(lg_ref, q_ref, k_ref, v_ref, do_ref, dq_ref, state_ref):
        hg, i = pl.program_id(0), pl.program_id(1)

        @pl.when(i == 0)
        def _():
            state_ref[...] = jnp.zeros_like(state_ref)

        qbs, kbs, vbs, dobs = load(q_ref, k_ref, v_ref, do_ref)
        dss = [d_scores(lg_ref, hg * RET_PAIRS + p, i, qbs[p], kbs[p], vbs[p], dobs[p])[1] for p in pairs]
        for p in pairs:
            dq_ref[:, p * LANES:(p + 1) * LANES] = (_dot(_side_by_side(dss[p], t), _stack_heads(kbs[p]))
                                                    + _dot(dobs[p], state_ref[p], _NT)).astype(BF16)
        for p in pairs:
            state_ref[p] += _pair_mask() * _dot(kbs[p], vbs[p], _TN)

    def down_body(lg_ref, q_ref, k_ref, v_ref, do_ref, dk_ref, dv_ref, state_ref):
        hg, i = pl.program_id(0), n_blk - 1 - pl.program_id(1)

        @pl.when(pl.program_id(1) == 0)
        def _():
            state_ref[...] = jnp.zeros_like(state_ref)

        qbs, kbs, vbs, dobs = load(q_ref, k_ref, v_ref, do_ref)
        both = [d_scores(lg_ref, hg * RET_PAIRS + p, i, qbs[p], kbs[p], vbs[p], dobs[p]) for p in pairs]
        for p in pairs:
            pp, ds = both[p]
            later = state_ref[p]
            dv_ref[:, p * pw:(p + 1) * pw] = (jnp.concatenate(
                [_dot(pp[:t], dobs[p][:, 0:RET_DV], _TN), _dot(pp[t:], dobs[p][:, RET_DV:pw], _TN)],
                axis=1) + _dot(kbs[p], later)).astype(BF16)
            dk_ref[:, p * LANES:(p + 1) * LANES] = (_dot(ds, _stack_heads(qbs[p]), _TN)
                                                    + _dot(vbs[p], later, _NT)).astype(BF16)
        for p in pairs:
            state_ref[p] += _pair_mask() * _dot(qbs[p], dobs[p], _TN)

    n_grp = n_pair // RET_PAIRS

    def ins(order):
        return [(log_gamma, None, pltpu.SMEM),
                (qk_rot, (t, wq), lambda hg, i: (order(i), hg)),
                (qk_rot, (t, wq), lambda hg, i: (order(i), n_grp + hg)),
                (proj, (t, wv), lambda hg, i: (order(i), 2 * RET_QK // wv + hg)),
                (dret, (t, wv), lambda hg, i: (order(i), hg))]

    up = lambda i: i
    down = lambda i: n_blk - 1 - i
    scratch = [pltpu.VMEM((RET_PAIRS, LANES, pw), F32)]
    dq = _call("ret_bwd_q", up_body, (n_grp, n_blk), ins(up),
               [((s, RET_QK), BF16, (t, wq), lambda hg, i: (i, hg))], scratch=scratch)[0]
    dk, dv, *rest = _call("ret_bwd_kv", down_body, (n_grp, n_blk), ins(down),
                          [((s, RET_QK), BF16, (t, wq), lambda hg, i: (down(i), hg)),
                           ((s, RET_V), BF16, (t, wv), lambda hg, i: (down(i), hg))],
                          scratch=scratch, riders=riders)
    return [dq, dk, dv] + rest


def _sb_bwd(q_sb, proj, weights, do, tq, tk, riders=None):
    s = q_sb.shape[0]
    k_off = (2 * RET_QK + 2 * RET_V + SB_W) // LANES
    n_pair = HEADS // 2
    _check_tiles(s, tq, tk, SB_GROUP)

    def body(q_ref, k_ref, v_ref, a_ref, do_ref, dq_ref, dk_ref, dv_ref):
        i = pl.program_id(1)

        @pl.when(i == 0)
        def _():
            dk_ref[...] = jnp.zeros_like(dk_ref)
            dv_ref[...] = jnp.zeros_like(dv_ref)

        lower = _tri(tk, False)
        qs = _stack_heads(q_ref[...])
        dos = _stack_heads(do_ref[...].astype(BF16))

        def make_step(near_diagonal, n_sub=SB_GROUP):
            def step(g, carry):
                c_e, dq = carry
                js = [g * SB_GROUP + sub for sub in range(n_sub)]
                rows = [_key_rows(j, tk) for j in js]
                zs = [_dot(qs, k_ref[rw, :], _NT) for rw in rows]
                das = [_dot(dos, v_ref[rw, :], _NT) for rw in rows]
                avals = [a_ref[j] for j in js]
                for a, rw in zip(avals, rows):
                    dv_ref[rw, :] += _dot(a, dos, _TN)
                es = [a.astype(F32) * da for a, da in zip(avals, das)]
                prefixes = [_dot(e, lower) for e in es]
                betas = [1.0 / (1.0 + jnp.exp2(-z)) for z in zs]
                for sub in range(n_sub):
                    dz = es[sub] - (es[sub] + prefixes[sub] + c_e) * betas[sub]
                    if near_diagonal:
                        dz = jnp.where(_sb_valid(i, js[sub], tq, tk), dz, 0.0)
                    dz = dz.astype(BF16)
                    dk_ref[rows[sub], :] += _dot(dz, qs, _TN)
                    dq = dq + _dot(_side_by_side(dz, tq), _stack_heads(k_ref[rows[sub], :]))
                    c_e = c_e + jnp.sum(es[sub], axis=1, keepdims=True)
                return c_e, dq
            return step

        n_full = _n_full(i, tq, tk, SB_GROUP)
        carry = (jnp.zeros((2 * tq, 1), F32), jnp.zeros((tq, LANES), F32))
        carry = lax.fori_loop(0, n_full, make_step(False), carry)
        _, dq = _diagonal_step(i, tq, tk, lambda n_sub: (lambda n, cr: make_step(True, n_sub)(n_full, cr)), carry)
        dq_ref[...] = dq

    blk = lambda hp, i: (i, hp)
    n_kb = s // tk
    return _call("sb_bwd", body, (n_pair, s // tq),
                 [(q_sb, (tq, LANES), blk),
                  (proj, (s, LANES), lambda hp, i: (0, k_off + hp)),
                  (proj, (s, LANES), lambda hp, i: (0, k_off + n_pair + hp)),
                  (weights, (None, None, n_kb, 2 * tq, tk), lambda hp, i: (hp, i, 0, 0, 0)),
                  (do, (tq, LANES), blk)],
                 [((s, SB_W), F32, (tq, LANES), blk),
                  ((s, SB_W), F32, (s, LANES), lambda hp, i: (0, hp)),
                  ((s, SB_W), F32, (s, LANES), lambda hp, i: (0, hp))], riders=riders)


def _assemble_dproj(dq_r, dk_r, dv_r, dg_r, dq_s, dk_s, dv_s, da_r, da_s, cos, sin, idx_col, lg_lanes, tm, riders=None):
    s, d = da_r.shape
    width = 2 * RET_QK + 2 * RET_V + 3 * SB_W + 2 * d

    def body(dq_ref, dk_ref, dv_ref, dg_ref, dqs_ref, dks_ref, dvs_ref, dar_ref, das_ref, cos_ref, sin_ref,
             idx_ref, lg_ref, o_ref):
        lane = lax.broadcasted_iota(jnp.int32, (1, LANES), 1)
        first = jnp.bitwise_and(lane, RET_DQK - 1) < (RET_DQK // 2)
        cos, sin = cos_ref[...], sin_ref[...]
        idx = idx_ref[...]
        for src, base, sign, scale in ((dq_ref, 0, 1.0, 1.0), (dk_ref, RET_QK, -1.0, RET_DQK ** -0.5)):
            for g in range(RET_QK // LANES):
                v = src[:, g * LANES:(g + 1) * LANES].astype(F32) * (_decay_scale(lg_ref, idx, g, sign) * scale)
                sw = jnp.where(first, pltpu.roll(v, LANES - RET_DQK // 2, 1), pltpu.roll(v, RET_DQK // 2, 1))
                o_ref[:, base + g * LANES:base + (g + 1) * LANES] = (v * cos - sw * sin).astype(BF16)
        off = 2 * RET_QK
        o_ref[:, off:off + RET_V] = dv_ref[...].astype(BF16)
        off += RET_V
        o_ref[:, off:off + RET_V] = dg_ref[...]
        off += RET_V
        o_ref[:, off:off + SB_W] = (dqs_ref[...] * (SB_DH ** -0.5)).astype(BF16)
        off += SB_W
        o_ref[:, off:off + SB_W] = (dks_ref[...] * LN2).astype(BF16)
        off += SB_W
        o_ref[:, off:off + SB_W] = dvs_ref[...].astype(BF16)
        off += SB_W
        o_ref[:, off:off + d] = dar_ref[...]
        off += d
        o_ref[:, off:off + d] = das_ref[...]

    row = lambda i: (i, 0)
    ins = [(a, (tm, a.shape[1]), row) for a in (dq_r, dk_r, dv_r, dg_r, dq_s, dk_s, dv_s, da_r, da_s, cos, sin, idx_col)]
    ins.append((lg_lanes, (1, RET_QK), lambda i: (0, 0)))
    return _call("assemble_dproj", body, (s // tm,), ins, [((s, width), BF16, (tm, width), row)], riders=riders)


def _in_bwd(dproj, w_in_t, x, dhres, mod, g1, tm, riders=None):
    s, d = x.shape
    width = dproj.shape[1]

    def body(a_ref, w_ref, x_ref, dh_ref, mod_ref, g_ref, dx_ref, dsh_ref, dsc_ref, dg_ref):
        _zero_at_start([dsh_ref, dsc_ref, dg_ref])
        g1, sc1 = g_ref[...], mod_ref[:, d:2 * d]
        for rows in _pieces(tm):
            dh = _dot(a_ref[rows, :], w_ref[...])
            n1, r1 = _rms(x_ref[rows, :], d)
            dsh_ref[...] += _colsum(dh)
            dsc_ref[...] += _colsum(dh * n1 * g1)
            dg_ref[...] += _colsum(dh * n1 * (1.0 + sc1))
            dx_ref[rows, :] = dh_ref[rows, :] + _rms_bwd(dh * g1 * (1.0 + sc1), n1, r1, d)

    row = lambda i: (i, 0)
    fix = lambda i: (0, 0)
    vec = ((1, d), F32, (1, d), fix)
    return _call("in_bwd", body, (s // tm,),
                 [(dproj, (tm, width), row), (w_in_t, (width, d), fix), (x, (tm, d), row), (dhres, (tm, d), row),
                  (mod, (1, 6 * d), fix), (g1, (1, d), fix)],
                 [((s, d), F32, (tm, d), row), vec, vec, vec], riders=riders)


def _adamw(w, g, m, v):
    m = ADAM_B1 * m + (1.0 - ADAM_B1) * g
    v = ADAM_B2 * v + (1.0 - ADAM_B2) * (g * g)
    m_hat = m / (1.0 - ADAM_B1 ** ADAM_STEP)
    v_hat = v / (1.0 - ADAM_B2 ** ADAM_STEP)
    delta = -ADAM_LR * (m_hat / (jnp.sqrt(v_hat) + ADAM_EPS) + ADAM_WD * w)
    return delta, m, v


def _adam_reduce(name, sets, steps):
    n = len(sets)

    def body(*refs):
        for k in range(n):
            p_ref, w_ref, m_ref, v_ref = refs[4 * k:4 * k + 4]
            outs = refs[4 * n + 4 * k:4 * n + 4 * k + 4]
            g = p_ref[0].astype(F32)
            for j in range(1, p_ref.shape[0]):
                g = g + p_ref[j].astype(F32)
            for o_ref, val in zip(outs, (g,) + _adamw(w_ref[...], g, m_ref[...], v_ref[...])):
                o_ref[...] = val

    ins, outs = [], []
    row = lambda i: (i, 0)
    for parts, w, m, v in sets:
        rws, cls = w.shape
        tr = rws // steps
        assert tr * steps == rws and tr % 16 == 0
        ins += [(parts, (parts.shape[0], tr, cls), lambda i: (0, i, 0)), (w, (tr, cls), row), (m, (tr, cls), row),
                (v, (tr, cls), row)]
        outs += [((rws, cls), F32, (tr, cls), row)] * 4
    res = _call(name, body, (steps,), ins, outs)
    return [res[4 * k:4 * k + 4] for k in range(n)]


def _ada_bwd_adam(cs_t, dmod_cols, w, m, v, tr):
    d, nc = w.shape

    def body(c_ref, dm_ref, w_ref, m_ref, v_ref, g_out, d_out, m_out, v_out):
        g = c_ref[0] * dm_ref[0:1, :]
        for r in range(1, N_DEV):
            g = g + c_ref[r] * dm_ref[r:r + 1, :]
        delta, mn, vn = _adamw(w_ref[...], g, m_ref[...], v_ref[...])
        g_out[...] = g
        d_out[...] = delta
        m_out[...] = mn
        v_out[...] = vn

    row = lambda i: (i, 0)
    blk = (tr, nc)
    return _call("ada_bwd_adam", body, (d // tr,),
                 [(cs_t, (N_DEV, tr, 1), lambda i: (0, i, 0)), (dmod_cols, (N_DEV, nc), lambda i: (0, 0)),
                  (w, blk, row), (m, blk, row), (v, blk, row)],
                 [((d, nc), F32, blk, row)] * 4)


def _small_adam(parts, ws, ms, vs):
    n = len(ws)
    widths = [w.shape[1] for w in ws]
    total = parts.shape[1]
    assert sum(widths) + LANES == total

    def body(p_ref, *refs):
        w_refs, m_refs, v_refs = refs[:n], refs[n:2 * n], refs[2 * n:3 * n]
        outs = refs[3 * n:]
        g = p_ref[0:1, :]
        for k in range(1, N_DEV):
            g = g + p_ref[k:k + 1, :]
        off = 0
        for i, width in enumerate(widths):
            gi = g[:, off:off + width]
            delta, mn, vn = _adamw(w_refs[i][...], gi, m_refs[i][...], v_refs[i][...])
            for o_ref, val in zip(outs[4 * i:4 * i + 4], (gi, delta, mn, vn)):
                o_ref[...] = val
            off += width
        outs[4 * n][...] = g[:, off:off + LANES]

    fix = lambda i: (0, 0)
    vec = lambda a: (a, (1, a.shape[1]), fix)
    out_specs = [((1, width), F32, (1, width), fix) for width in widths for _ in range(4)]
    out_specs.append(((1, LANES), F32, (1, LANES), fix))
    res = _call("small_adam", body, (1,),
                [(parts, (N_DEV, total), fix)] + [vec(a) for a in list(ws) + list(ms) + list(vs)], out_specs)
    return [res[4 * i:4 * i + 4] for i in range(n)], res[4 * n]


def kernel(x, c, positions, ada_w, ada_b, pre_mix_g, post_mix_g, pre_ffn_g, post_ffn_g, w_in, ret_gn_g, w_ret_branch, w_sb_branch, w_out, w_ff1, w_ff2, loss_target, m_ada_w, m_ada_b, m_pre_mix_g, m_post_mix_g, m_pre_ffn_g, m_post_ffn_g, m_w_in, m_ret_gn_g, m_w_ret_branch, m_w_sb_branch, m_w_out, m_w_ff1, m_w_ff2, v_ada_w, v_ada_b, v_pre_mix_g, v_post_mix_g, v_pre_ffn_g, v_post_ffn_g, v_w_in, v_ret_gn_g, v_w_ret_branch, v_w_sb_branch, v_w_out, v_w_ff1, v_w_ff2):
    _, s, d = x.shape
    d_ff = w_ff1.shape[2] * N_DEV
    d_in = w_in.shape[2] * N_DEV
    me = 4 * lax.axis_index("x") + 2 * lax.axis_index("y") + lax.axis_index("c")
    x2, tgt = x[0], loss_target[0]

    core = lax.axis_index("c").astype(jnp.int32).reshape(1)
    bf = lambda w: w[0].astype(BF16)

    w_in_t, m_in_t, v_in_t = (jnp.swapaxes(a[0], 0, 1) for a in (w_in, m_w_in, v_w_in))

    c_all, g_in = _exchange("gather_in", [c, w_in_t.astype(BF16)], ["gather", "gather_chip"])
    c_all = c_all.reshape(N_DEV, d)

    n_ada = ada_w.shape[2]
    ada_b_cols = lax.dynamic_slice(ada_b, (0, me * n_ada), (1, n_ada))
    cs_all, mod_cols = _ada_fwd(c_all, ada_w[0], ada_b_cols)
    mod_all, g_in = _exchange("gather_mod", [mod_cols, g_in], ["gather", "forward"])
    mod = lax.dynamic_index_in_dim(mod_all, me, axis=1, keepdims=False).reshape(1, 6 * d)

    tm = min(256, s)
    h = _pre_norm(x2, pre_mix_g, mod, 2 * tm)[0]
    wt_in = g_in.reshape(d_in, d)
    bf_t = lambda w: jnp.swapaxes(w[0], 0, 1).astype(BF16)
    small_w = [bf(w_ret_branch), bf_t(w_sb_branch), bf(w_out)]
    proj, *small_w = _matmul("in_proj", h, wt_in, "nt", s, 512, BF16, riders=(small_w, ["gather_chip"] * 3))
    pos_col = positions.reshape(s, 1).astype(F32)
    freqs = ROPE_BASE ** (-jnp.arange(0, RET_DQK, 2, dtype=F32) / RET_DQK)
    inv_freq = jnp.tile(freqs, LANES // (RET_DQK // 2)).reshape(1, LANES)
    log_gamma_np = np.log1p(-(2.0 ** (-5.0 - np.arange(HEADS))))
    log_gamma = jnp.asarray(log_gamma_np, F32)
    lg_lanes = jnp.asarray(np.repeat(log_gamma_np, RET_DQK).reshape(1, RET_QK), F32)
    idx_col = (jnp.arange(s, dtype=F32) - (s // 2)).reshape(s, 1)
    qk_rot, q_sb, cos_t, sin_t = _prep(proj, pos_col, idx_col, inv_freq, lg_lanes, 2 * tm)
    tq, tk = min(256, s), min(128, s)
    tq_sb = min(SB_TQ, s)
    sb, sb_weights, *big_w = _sb_fwd(q_sb, proj, tq_sb, tk, riders=([bf(w_ff2), bf_t(w_ff1)], ["gather_chip"] * 2))
    ret, retg, g_ret, g_sb, g_out, g_ff2, g_ff1 = _ret_fwd(qk_rot, proj, ret_gn_g, log_gamma, tq,
                                                           riders=(small_w + big_w, ["forward"] * 5))
    wf_ret = g_ret.reshape(RET_V, d)
    wt_sb = g_sb.reshape(d, SB_W)
    wf_out = g_out.reshape(d, d)
    wt_ff1 = g_ff1.reshape(d_ff, d)
    wf_ff2 = g_ff2.reshape(d_ff, d)
    mixed, r_bf, s_bf, y, hres, h2 = _merge_out(retg, sb, wf_ret, wt_sb, wf_out, proj, x2, mod, post_mix_g, pre_ffn_g, tm)
    u, act = _ff1(h2, wt_ff1, s, 512)
    dout, df, loss_sum, d_gt2, d_gp2 = _ff2_loss(act, wf_ff2, hres, tgt, mod, post_ffn_g, tm)

    du, gw_ff2, gw_ff1 = _ffn_bwd(df, wf_ff2, u, act, h2, d_ff // N_DEV)
    gw_ff2 = gw_ff2.reshape(N_DEV, d_ff // N_DEV, d)
    dhres, dy, d_sh2, d_sc2, d_g2, d_gt1, d_gp1, gw_out, t_ff1, t_ff2 = _ff1_bwd(
        du, wt_ff1, hres, dout, y, mixed, mod, pre_ffn_g, post_mix_g, tm, riders=([gw_ff1, gw_ff2], ["pair"] * 2))
    gw_out = gw_out.reshape(N_DEV, d // N_DEV, d)
    s_ff1, s_ff2 = _pair_sum("pair_sum_ff", [(gw_ff1, t_ff1), (gw_ff2, t_ff2)], core)
    d_r, d_s, da_r, da_s = _out_bwd(dy, wf_out, proj, r_bf, s_bf, 2 * tm, min(512, d))
    dsb, gw_ret, gw_sb, dg_r, dret, d_gn, p_out = _branch_bwd(d_r, d_s, retg, sb, wf_ret, wt_sb, ret, proj, ret_gn_g,
                                                              riders=([gw_out], ["scatter"]))
    gw_ret = gw_ret.reshape(N_DEV, RET_V // N_DEV, d)
    dq_s, dk_s, dv_s, p_ff1, p_ff2 = _sb_bwd(q_sb, proj, sb_weights, dsb, tq_sb, tk,
                                             riders=([s_ff1, s_ff2], ["chip_scatter"] * 2))
    dq_r, dk_r, dv_r, p_sb = _ret_bwd(qk_rot, proj, dret, log_gamma, tq, riders=([gw_sb], ["scatter"]))
    dproj = _assemble_dproj(dq_r, dk_r, dv_r, dg_r, dq_s, dk_s, dv_s, da_r, da_s, cos_t, sin_t, idx_col, lg_lanes, tm)[0]
    gw_in, p_ret = _matmul("grad_w_in", dproj, h, "tn", 512, d, BF16, riders=([gw_ret], ["scatter"]))
    gw_in = gw_in.reshape(N_DEV, d_in // N_DEV, d)
    t_in = _exchange("pair_in", [gw_in], ["pair"])[0]
    s_in = _pair_sum("pair_sum_in", [(gw_in, t_in)], core)[0]
    grad_x, d_sh1, d_sc1, d_g1, p_in = _in_bwd(dproj, wt_in, x2, dhres, mod, pre_mix_g, tm,
                                               riders=([s_in], ["chip_scatter"]))
    loss_lanes = jnp.pad(loss_sum, ((0, 0), (0, LANES - 1)))
    small = jnp.concatenate([d_sh1, d_sc1, d_gt1, d_sh2, d_sc2, d_gt2, d_g1, d_gp1, d_g2, d_gp2, d_gn, loss_lanes], axis=1)
    small_all = _exchange("gather_small", [small], ["gather"])[0].reshape(N_DEV, small.shape[1])
    parts = [p_in, p_ret, p_sb, p_out, p_ff1, p_ff2]

    res = {}
    names = ["w_ret_branch", "w_sb_branch", "w_out", "w_ff1", "w_ff2"]
    ws = [w_ret_branch, w_sb_branch, w_out, w_ff1, w_ff2]
    ms = [m_w_ret_branch, m_w_sb_branch, m_w_out, m_w_ff1, m_w_ff2]
    vs = [v_w_ret_branch, v_w_sb_branch, v_w_out, v_w_ff1, v_w_ff2]
    sets = [(parts[0], w_in_t, m_in_t, v_in_t)] + [(p, w[0], m[0], v[0]) for p, w, m, v in zip(parts[1:], ws, ms, vs)]
    updated = _adam_reduce("adam_big", sets, 4)
    res["w_in"] = [jnp.swapaxes(o, 0, 1)[None] for o in updated[0]]
    for nm, outs4 in zip(names, updated[1:]):
        res[nm] = [o[None] for o in outs4]
    dmod_cols = lax.dynamic_slice(small_all, (0, me * n_ada), (N_DEV, n_ada))
    res["ada_w"] = [o[None] for o in _ada_bwd_adam(cs_all.reshape(N_DEV, d, 1), dmod_cols, ada_w[0], m_ada_w[0], v_ada_w[0], tm)]
    vec_names = ["ada_b", "pre_mix_g", "post_mix_g", "pre_ffn_g", "post_ffn_g", "ret_gn_g"]
    vec_res, loss_lanes = _small_adam(small_all,
                                      [ada_b, pre_mix_g, post_mix_g, pre_ffn_g, post_ffn_g, ret_gn_g],
                                      [m_ada_b, m_pre_mix_g, m_post_mix_g, m_pre_ffn_g, m_post_ffn_g, m_ret_gn_g],
                                      [v_ada_b, v_pre_mix_g, v_post_mix_g, v_pre_ffn_g, v_post_ffn_g, v_ret_gn_g])
    res.update(zip(vec_names, vec_res))
    loss = (0.5 / d) * loss_lanes[0, 0]
    order = ["ada_w", "ada_b", "pre_mix_g", "post_mix_g", "pre_ffn_g", "post_ffn_g", "w_in", "ret_gn_g",
             "w_ret_branch", "w_sb_branch", "w_out", "w_ff1", "w_ff2"]
    outs = [loss, grad_x[None]]
    for k in range(4):
        outs += [res[nm][k] for nm in order]
    return tuple(outs)
```

```python
import functools

import numpy as np
import jax
import jax.numpy as jnp
from jax import lax
from jax.experimental import pallas as pl
from jax.experimental.pallas import tpu as pltpu

F32 = jnp.float32
BF16 = jnp.bfloat16
N_DEV = 8
AXES = ("x", "y", "c")

EPS = 1e-6
CHUNK = 64
CHUNK_SHIFT = 6
HEADS = 8
RET_DQK = 64
RET_DV = 128
SB_DH = 64
RET_QK = HEADS * RET_DQK
RET_V = HEADS * RET_DV
SB_W = HEADS * SB_DH
ROPE_BASE = 10000.0
LANES = 128

ADAM_LR = 0.001
ADAM_B1 = 0.9
ADAM_B2 = 0.999
ADAM_EPS = 1e-08
ADAM_WD = 0.01
ADAM_STEP = 10

VMEM_LIMIT = 56 * 1024 * 1024

_NN = (((1,), (0,)), ((), ()))
_NT = (((1,), (1,)), ((), ()))
_TN = (((0,), (0,)), ((), ()))


def _dot(a, b, dims=_NN):
    if a.dtype != BF16:
        a = a.astype(BF16)
    if b.dtype != BF16:
        b = b.astype(BF16)
    return lax.dot_general(a, b, dims, preferred_element_type=F32)


def _sigmoid(x):
    return 1.0 / (1.0 + jnp.exp(-x))


def _rms(x, d):
    r = lax.rsqrt(jnp.sum(x * x, axis=1, keepdims=True) * (1.0 / d) + EPS)
    return x * r, r


def _rms_bwd(dn, n, r, d):
    return r * (dn - n * (jnp.sum(dn * n, axis=1, keepdims=True) * (1.0 / d)))


def _colsum(v):
    return jnp.sum(v, axis=0, keepdims=True)


ROW_SPLIT = 2


def _zero_at_start(refs):
    @pl.when(pl.program_id(0) == 0)
    def _():
        for r in refs:
            r[...] = jnp.zeros_like(r)


def _pieces(tm):
    step = tm // ROW_SPLIT
    return [slice(k * step, (k + 1) * step) for k in range(ROW_SPLIT)]


KIND_SLOTS = {"gather": N_DEV, "scatter": N_DEV, "gather_chip": N_DEV, "forward": N_DEV, "pair": N_DEV // 2,
              "chip_scatter": N_DEV // 2}
SEMS_PER_ARRAY = N_DEV - 1


def _exchange_copies(ins, outs, send_sems, recv_sems, local_sems, kinds):
    x, y, c = (lax.axis_index(a) for a in AXES)
    me, chip, sibling = 4 * x + 2 * y + c, 2 * x + y, (x, y, 1 - c)
    mesh_id = pl.DeviceIdType.MESH
    other_chips = []
    for k in range(1, N_DEV // 2):
        px = 1 - x if k & 2 else x
        py = 1 - y if k & 1 else y
        other_chips.append((px, py))
    copies = []
    for i, kind in enumerate(kinds):
        def remote(src, dst, k, to, i=i):
            return pltpu.make_async_remote_copy(
                src_ref=src, dst_ref=dst, send_sem=send_sems.at[i * SEMS_PER_ARRAY + k],
                recv_sem=recv_sems.at[i * SEMS_PER_ARRAY + k], device_id=to, device_id_type=mesh_id)

        if kind in ("gather", "scatter"):
            pick = (lambda ref, d: ref.at[d]) if kind == "scatter" else (lambda ref, d: ref)
            copies.append(pltpu.make_async_copy(pick(ins[i], me), outs[i].at[me], local_sems.at[i]))
            for k in range(1, N_DEV):
                to = (1 - x if k & 4 else x, 1 - y if k & 2 else y, 1 - c if k & 1 else c)
                copies.append(remote(pick(ins[i], 4 * to[0] + 2 * to[1] + to[2]), outs[i].at[me], k - 1, to))
        elif kind == "gather_chip":
            copies.append(pltpu.make_async_copy(ins[i], outs[i].at[me], local_sems.at[i]))
            copies.append(remote(ins[i], outs[i].at[me], 0, sibling))
            for k, (px, py) in enumerate(other_chips):
                copies.append(remote(ins[i], outs[i].at[me], 1 + k, (px, py, c)))
        elif kind == "forward":
            for k, (px, py) in enumerate(other_chips):
                slot = 4 * px + 2 * py + c
                copies.append(remote(outs[i].at[slot], outs[i].at[slot], k, sibling))
        elif kind == "pair":
            for k in range(N_DEV // 2):
                copies.append(remote(ins[i].at[2 * k + 1 - c], outs[i].at[k], k, sibling))
        elif kind == "chip_scatter":
            copies.append(pltpu.make_async_copy(ins[i].at[chip], outs[i].at[chip], local_sems.at[i]))
            for k, (px, py) in enumerate(other_chips):
                copies.append(remote(ins[i].at[2 * px + py], outs[i].at[chip], k, (px, py, c)))
        else:
            raise ValueError(kind)
    return copies


def _exchange_shapes(arrays, kinds):
    shapes = []
    for a, kind in zip(arrays, kinds):
        tail = a.shape if kind in ("gather", "gather_chip") else a.shape[1:]
        shapes.append(jax.ShapeDtypeStruct((KIND_SLOTS[kind],) + tuple(tail), a.dtype))
    return shapes


def _exchange_sems(n):
    return [pltpu.SemaphoreType.DMA((n * SEMS_PER_ARRAY,)), pltpu.SemaphoreType.DMA((n * SEMS_PER_ARRAY,)),
            pltpu.SemaphoreType.DMA((n,))]


def _call(name, body, grid, ins, outs, scratch=(), riders=None, prefetch=None):
    any_spec = pl.BlockSpec(memory_space=pl.ANY)
    in_specs = [pl.BlockSpec(memory_space=im) if bs is None else pl.BlockSpec(bs, im) for _, bs, im in ins]
    out_specs = [pl.BlockSpec(bs, im) for _, _, bs, im in outs]
    out_shape = [jax.ShapeDtypeStruct(s, d) for s, d, _, _ in outs]
    operands = [a for a, _, _ in ins]
    scratch = list(scratch)
    aliases = {}
    n_pre = 0 if prefetch is None else 1
    kernel = functools.partial(body) if prefetch is None else (lambda _, *refs: body(*refs))
    if riders is not None:
        arrays, kinds = riders
        nr, n_in, n_out, n_scr = len(arrays), len(ins), len(outs), len(scratch)

        def kernel(*refs):
            refs = refs[n_pre:]
            own_in, ride_in = refs[:n_in], refs[n_in:n_in + nr]
            own_out = refs[n_in + nr:n_in + nr + n_out]
            ride_out = refs[n_in + nr + n_out:n_in + 2 * nr + n_out]
            own_scr = refs[n_in + 2 * nr + n_out:n_in + 2 * nr + n_out + n_scr]
            sems = refs[n_in + 2 * nr + n_out + n_scr:]
            ids = [pl.program_id(a) for a in range(len(grid))]
            first = functools.reduce(jnp.logical_and, [i == 0 for i in ids])
            last = functools.reduce(jnp.logical_and, [i == g - 1 for i, g in zip(ids, grid)])

            @pl.when(first)
            def _():
                for cp in _exchange_copies(ride_in, ride_out, *sems, kinds):
                    cp.start()

            body(*own_in, *own_out, *own_scr)

            @pl.when(last)
            def _():
                for cp in _exchange_copies(ride_in, ride_out, *sems, kinds):
                    cp.wait()

        in_specs += [any_spec] * nr
        out_specs += [any_spec] * nr
        out_shape += _exchange_shapes(arrays, kinds)
        operands += list(arrays)
        scratch += _exchange_sems(nr)
        aliases = {n_pre + n_in + r: n_out + r for r, kind in enumerate(kinds) if kind == "forward"}
    params = pltpu.CompilerParams(dimension_semantics=("arbitrary",) * len(grid), vmem_limit_bytes=VMEM_LIMIT)
    if prefetch is None:
        return pl.pallas_call(kernel, name=name, grid=grid, in_specs=in_specs, out_specs=out_specs,
                              out_shape=out_shape, scratch_shapes=scratch, input_output_aliases=aliases,
                              compiler_params=params)(*operands)
    grid_spec = pltpu.PrefetchScalarGridSpec(num_scalar_prefetch=1, grid=grid, in_specs=in_specs,
                                             out_specs=out_specs, scratch_shapes=scratch)
    return pl.pallas_call(kernel, name=name, grid_spec=grid_spec, out_shape=out_shape,
                          input_output_aliases=aliases, compiler_params=params)(prefetch, *operands)


def _exchange(name, arrays, kinds):
    n = len(arrays)

    def body(*refs):
        copies = _exchange_copies(refs[:n], refs[n:2 * n], *refs[2 * n:], kinds)
        for cp in copies:
            cp.start()
        for cp in copies:
            cp.wait()

    any_spec = pl.BlockSpec(memory_space=pl.ANY)
    return pl.pallas_call(
        functools.partial(body),
        name=name,
        in_specs=[any_spec] * n,
        out_specs=[any_spec] * n,
        out_shape=_exchange_shapes(arrays, kinds),
        scratch_shapes=_exchange_sems(n),
        input_output_aliases={i: i for i, kind in enumerate(kinds) if kind == "forward"},
    )(*arrays)


def _pair_sum(name, pairs, my_core):
    n = len(pairs)

    def body(*refs):
        for k in range(n):
            a_ref, b_ref, o_ref = refs[2 * k], refs[2 * k + 1], refs[2 * n + k]
            o_ref[...] = (a_ref[...].astype(F32) + b_ref[...].astype(F32)).astype(o_ref.dtype)

    ins, outs = [], []
    for mine, theirs in pairs:
        _, rws, cls = mine.shape
        ins += [(mine, (None, rws, cls), lambda k, core: (2 * k + core[0], 0, 0)),
                (theirs, (None, rws, cls), lambda k, core: (k, 0, 0))]
        outs.append(((N_DEV // 2, rws, cls), mine.dtype, (None, rws, cls), lambda k, core: (k, 0, 0)))
    return _call(name, body, (N_DEV // 2,), ins, outs, prefetch=my_core)


def _matmul(name, a, b, kind, tm, tn, out_dtype, blocked_out=False, riders=None):
    if kind == "tn":
        kdim, m = a.shape
    else:
        m, kdim = a.shape
    n = b.shape[0] if kind == "nt" else b.shape[1]
    tm, tn = min(tm, m), min(tn, n)
    dims = {"nn": _NN, "nt": _NT, "tn": _TN}[kind]

    def body(a_ref, b_ref, o_ref):
        o_ref[...] = _dot(a_ref[...], b_ref[...], dims).astype(o_ref.dtype)

    a_spec = (a, (kdim, tm), lambda j, i: (0, i)) if kind == "tn" else (a, (tm, kdim), lambda j, i: (i, 0))
    b_spec = (b, (tn, kdim), lambda j, i: (j, 0)) if kind == "nt" else (b, (kdim, tn), lambda j, i: (0, j))
    if blocked_out:
        out = ((n // tn, m, tn), out_dtype, (None, tm, tn), lambda j, i: (j, i, 0))
    else:
        out = ((m, n), out_dtype, (tm, tn), lambda j, i: (i, j))
    res = _call(name, body, (n // tn, m // tm), [a_spec, b_spec], [out], riders=riders)
    return res[0] if riders is None else res


def _ada_fwd(c_all, ada_w, ada_b_cols):
    def body(c_ref, w_ref, b_ref, cs_ref, o_ref):
        v = c_ref[...]
        cs = v * _sigmoid(v)
        cs_ref[...] = cs
        o_ref[...] = lax.dot_general(cs, w_ref[...], _NN, preferred_element_type=F32,
                                     precision=lax.Precision.HIGHEST) + b_ref[...]

    r, d = c_all.shape
    nc = ada_w.shape[1]
    fix = lambda i: (0, 0)
    return _call("ada_fwd", body, (1,),
                 [(c_all, (r, d), fix), (ada_w, (d, nc), fix), (ada_b_cols, (1, nc), fix)],
                 [((r, d), F32, (r, d), fix), ((r, nc), F32, (r, nc), fix)])


def _pre_norm(x, g, mod, tm, riders=None):
    s, d = x.shape

    def body(x_ref, g_ref, mod_ref, h_ref):
        n, _ = _rms(x_ref[...], d)
        sh, sc = mod_ref[:, 0:d], mod_ref[:, d:2 * d]
        h_ref[...] = (n * g_ref[...] * (1.0 + sc) + sh).astype(BF16)

    return _call("pre_norm", body, (s // tm,),
                 [(x, (tm, d), lambda i: (i, 0)), (g, (1, d), lambda i: (0, 0)),
                  (mod, (1, 6 * d), lambda i: (0, 0))],
                 [((s, d), BF16, (tm, d), lambda i: (i, 0))], riders=riders)


LOG2E = 1.4426950408889634
LN2 = 0.6931471805599453


def _decay_scale(lg_ref, idx, g, sign):
    return jnp.exp((sign * idx) * lg_ref[:, g * LANES:(g + 1) * LANES])


def _prep(proj, pos_col, idx_col, inv_freq, lg_lanes, tm):
    s = proj.shape[0]
    sb_off = (2 * RET_QK + 2 * RET_V) // SB_W
    n_q = RET_QK // LANES

    def body(qk_ref, qs_ref, pos_ref, idx_ref, f_ref, lg_ref, qk_out, qs_out, cos_out, sin_out):
        ang = pos_ref[...] * f_ref[...]
        lane = lax.broadcasted_iota(jnp.int32, (1, LANES), 1)
        first = jnp.bitwise_and(lane, RET_DQK - 1) < (RET_DQK // 2)
        cos = jnp.cos(ang)
        sin = jnp.where(first, -1.0, 1.0) * jnp.sin(ang)
        cos_out[...] = cos
        sin_out[...] = sin
        idx = idx_ref[...]
        for g in range(2 * n_q):
            v = qk_ref[:, g * LANES:(g + 1) * LANES].astype(F32)
            sw = jnp.where(first, pltpu.roll(v, LANES - RET_DQK // 2, 1), pltpu.roll(v, RET_DQK // 2, 1))
            r = v * cos + sw * sin
            if g < n_q:
                r = r * _decay_scale(lg_ref, idx, g, 1.0)
            else:
                r = r * (_decay_scale(lg_ref, idx, g - n_q, -1.0) * (RET_DQK ** -0.5))
            qk_out[:, g * LANES:(g + 1) * LANES] = r.astype(BF16)
        qs_out[...] = (qs_ref[...].astype(F32) * (SB_DH ** -0.5 * LOG2E)).astype(BF16)

    row = lambda i: (i, 0)
    return _call("prep", body, (s // tm,),
                 [(proj, (tm, 2 * RET_QK), row),
                  (proj, (tm, SB_W), lambda i: (i, sb_off)),
                  (pos_col, (tm, 1), row),
                  (idx_col, (tm, 1), row),
                  (inv_freq, (1, LANES), lambda i: (0, 0)),
                  (lg_lanes, (1, RET_QK), lambda i: (0, 0))],
                 [((s, 2 * RET_QK), BF16, (tm, 2 * RET_QK), row),
                  ((s, SB_W), BF16, (tm, SB_W), row),
                  ((s, LANES), F32, (tm, LANES), row),
                  ((s, LANES), F32, (tm, LANES), row)])


def _head_mask(hh):
    lane = lax.broadcasted_iota(jnp.int32, (1, LANES), 1)
    return (lane >= RET_DQK) if hh else (lane < RET_DQK)


def _masked(v, m):
    return jnp.where(m, v, jnp.zeros_like(v))


SB_GROUP = 4
SB_TQ = 256


def _stack_heads(v):
    return jnp.concatenate([_masked(v, _head_mask(0)), _masked(v, _head_mask(1))], axis=0)


def _side_by_side(v, t):
    return jnp.concatenate([v[:t], v[t:]], axis=1)


def _tile_pos(i, j, tq, tk):
    row = jnp.bitwise_and(lax.broadcasted_iota(jnp.int32, (2 * tq, tk), 0), tq - 1) + i * tq
    col = lax.broadcasted_iota(jnp.int32, (2 * tq, tk), 1) + j * tk
    return row, col


def _n_groups(i, tq, tk, grp):
    return ((i + 1) * (tq // tk) + grp - 1) // grp


def _n_full(i, tq, tk, grp):
    return (i * (tq // tk)) // grp


def _key_rows(j, tk):
    return pl.ds(pl.multiple_of(j * tk, tk), tk)


def _ret_weight(lg_rows, i, j, tq, tk):
    row, col = _tile_pos(i, j, tq, tk)
    same = jnp.right_shift(col, CHUNK_SHIFT) == jnp.right_shift(row, CHUNK_SHIFT)
    later = jnp.where(same, jnp.exp((2.0 * lg_rows) * (col - row).astype(F32)), 0.0)
    return jnp.where(col <= row, 1.0, later)


def _lg_rows(lg_ref, hp, tq):
    first = lax.broadcasted_iota(jnp.int32, (2 * tq, 1), 0) < tq
    return jnp.where(first, lg_ref[2 * hp], lg_ref[2 * hp + 1])


def _check_tiles(s, tq, tk, grp):
    assert tq % tk == 0 and tq & (tq - 1) == 0 and tk & (tk - 1) == 0
    assert s % tq == 0 and (s // tk) % grp == 0 and s // tk <= LANES


def _pair_mask():
    r = lax.broadcasted_iota(jnp.int32, (LANES, 2 * RET_DV), 0) >= RET_DQK
    c = lax.broadcasted_iota(jnp.int32, (LANES, 2 * RET_DV), 1) >= RET_DV
    return (r == c).astype(F32)


def _ret_block(lg_ref, hp, i, t, qb, kb):
    w = _ret_weight(_lg_rows(lg_ref, hp, t), i, i, t, t)
    return _dot(_stack_heads(qb), kb, _NT), w


RET_PAIRS = 4


def _lanes(ref, p, width):
    return ref[:, p * width:(p + 1) * width]


def _ret_fwd(qk_rot, proj, gn_g, log_gamma, t, riders=None):
    s = qk_rot.shape[0]
    n_pair = HEADS // 2
    pw = 2 * RET_DV
    wq, wv = RET_PAIRS * LANES, RET_PAIRS * pw
    v_off, gate_off = 2 * RET_QK // wv, (2 * RET_QK + RET_V) // wv
    assert s % t == 0 and t % CHUNK == 0 and t & (t - 1) == 0 and n_pair % RET_PAIRS == 0

    def body(lg_ref, q_ref, k_ref, v_ref, g_ref, w_ref, ret_ref, rg_ref, state_ref):
        hg, i = pl.program_id(0), pl.program_id(1)

        @pl.when(i == 0)
        def _():
            state_ref[...] = jnp.zeros_like(state_ref)

        pairs = range(RET_PAIRS)
        qbs = [_lanes(q_ref, p, LANES) for p in pairs]
        kbs = [_lanes(k_ref, p, LANES) for p in pairs]
        vbs = [_lanes(v_ref, p, pw) for p in pairs]
        zws = [_ret_block(lg_ref, hg * RET_PAIRS + p, i, t, qbs[p], kbs[p]) for p in pairs]
        ps = [(z * w).astype(BF16) for z, w in zws]
        outs = [jnp.concatenate([_dot(ps[p][:t], vbs[p][:, 0:RET_DV]), _dot(ps[p][t:], vbs[p][:, RET_DV:pw])], axis=1)
                + _dot(qbs[p], state_ref[p]) for p in pairs]
        for p in pairs:
            state_ref[p] += _pair_mask() * _dot(kbs[p], vbs[p], _TN)
        for p in pairs:
            for hh in range(2):
                cols = slice(p * pw + hh * RET_DV, p * pw + (hh + 1) * RET_DV)
                o = outs[p][:, hh * RET_DV:(hh + 1) * RET_DV]
                ret_ref[:, cols] = o
                mu = jnp.sum(o, axis=1, keepdims=True) * (1.0 / RET_DV)
                xc = o - mu
                var = jnp.sum(xc * xc, axis=1, keepdims=True) * (1.0 / RET_DV)
                nrm = xc * lax.rsqrt(var + EPS) * w_ref[:, cols]
                g = g_ref[:, cols].astype(F32)
                rg_ref[:, cols] = (g * _sigmoid(g) * nrm).astype(BF16)

    blk = lambda hg, i: (i, hg)
    return _call("ret_fwd", body, (n_pair // RET_PAIRS, s // t),
                 [(log_gamma, None, pltpu.SMEM),
                  (qk_rot, (t, wq), blk),
                  (qk_rot, (t, wq), lambda hg, i: (i, n_pair // RET_PAIRS + hg)),
                  (proj, (t, wv), lambda hg, i: (i, v_off + hg)),
                  (proj, (t, wv), lambda hg, i: (i, gate_off + hg)),
                  (gn_g, (1, wv), lambda hg, i: (0, hg))],
                 [((s, RET_V), F32, (t, wv), blk), ((s, RET_V), BF16, (t, wv), blk)],
                 scratch=[pltpu.VMEM((RET_PAIRS, LANES, pw), F32)], riders=riders)


def _tri(tk, strict_upper):
    r = lax.broadcasted_iota(jnp.int32, (tk, tk), 0)
    cc = lax.broadcasted_iota(jnp.int32, (tk, tk), 1)
    return ((r > cc) if strict_upper else (r < cc)).astype(BF16)


def _diagonal_step(i, tq, tk, make, carry):
    if (tq // tk) % SB_GROUP == 0:
        return make(SB_GROUP)(0, carry)
    assert 2 * (tq // tk) == SB_GROUP
    half = lax.rem(i, 2) == 0
    return lax.cond(half, lambda cr: make(SB_GROUP // 2)(0, cr), lambda cr: make(SB_GROUP)(0, cr), carry)


def _sb_valid(i, j, tq, tk):
    row, col = _tile_pos(i, j, tq, tk)
    return col < row


def _sb_fwd(q_sb, proj, tq, tk, riders=None):
    s = q_sb.shape[0]
    k_off = (2 * RET_QK + 2 * RET_V + SB_W) // LANES
    n_pair = HEADS // 2
    _check_tiles(s, tq, tk, SB_GROUP)

    def body(q_ref, k_ref, v_ref, o_ref, a_ref):
        i = pl.program_id(1)
        upper = _tri(tk, True)
        qs = _stack_heads(q_ref[...])
        n_full, n_groups = _n_full(i, tq, tk, SB_GROUP), _n_groups(i, tq, tk, SB_GROUP)

        def make_step(near_diagonal, last, n_sub=SB_GROUP):
            def step(n, carry):
                c, o = carry
                g = last - 1 - n
                js = [g * SB_GROUP + sub for sub in range(n_sub)]
                zs = [_dot(qs, k_ref[_key_rows(j, tk), :], _NT) for j in js]
                log1ps = [jnp.log2(1.0 + jnp.exp2(-jnp.abs(z))) for z in zs]
                log_1ms = [-jnp.maximum(z, 0.0) - t for z, t in zip(zs, log1ps)]
                log_bs = [jnp.minimum(z, 0.0) - t for z, t in zip(zs, log1ps)]
                if near_diagonal:
                    valids = [_sb_valid(i, j, tq, tk) for j in js]
                    log_1ms = [jnp.where(v, l, 0.0) for v, l in zip(valids, log_1ms)]
                sticks = [_dot(l, upper) for l in log_1ms]
                sums = [jnp.sum(l, axis=1, keepdims=True) for l in log_1ms]
                cs = [None] * n_sub
                for sub in reversed(range(n_sub)):
                    cs[sub] = c
                    c = c + sums[sub]
                for sub, j in enumerate(js):
                    a = jnp.exp2(log_bs[sub] + sticks[sub] + cs[sub])
                    if near_diagonal:
                        a = jnp.where(valids[sub], a, 0.0)
                    a = a.astype(BF16)
                    a_ref[j] = a
                    o = o + _dot(_side_by_side(a, tq), _stack_heads(v_ref[_key_rows(j, tk), :]))
                return c, o
            return step

        carry = (jnp.zeros((2 * tq, 1), F32), jnp.zeros((tq, LANES), F32))
        carry = _diagonal_step(i, tq, tk, lambda n_sub: make_step(True, n_groups, n_sub), carry)
        _, acc = lax.fori_loop(0, n_full, make_step(False, n_full), carry)
        o_ref[...] = acc.astype(BF16)

    n_kb = s // tk
    return _call("sb_fwd", body, (n_pair, s // tq),
                 [(q_sb, (tq, LANES), lambda hp, i: (i, hp)),
                  (proj, (s, LANES), lambda hp, i: (0, k_off + hp)),
                  (proj, (s, LANES), lambda hp, i: (0, k_off + n_pair + hp))],
                 [((s, SB_W), BF16, (tq, LANES), lambda hp, i: (i, hp)),
                  ((n_pair, s // tq, n_kb, 2 * tq, tk), BF16, (None, None, n_kb, 2 * tq, tk),
                   lambda hp, i: (hp, i, 0, 0, 0))], riders=riders)


def _merge_out(retg, sb, w_ret, w_sb_t, w_out, proj, x, mod, gp1, g2, tm):
    s, d = x.shape
    gw = min(512, d)
    n_g = d // gw
    ar_off = (2 * RET_QK + 2 * RET_V + 3 * SB_W) // gw

    def body(rg_ref, sb_ref, wr_ref, ws_ref, wo_ref, *refs):
        gate_refs, (x_ref, mod_ref, gp_ref, g2_ref, mix_ref, r_ref, s_ref, y_ref, hres_ref, h2_ref) = refs[:2 * n_g], refs[2 * n_g:]
        for rows in _pieces(tm):
            rr = _dot(rg_ref[rows, :], wr_ref[...])
            ss = _dot(sb_ref[rows, :], ws_ref[...], _NT)
            a_r = jnp.concatenate([g[rows, :] for g in gate_refs[:n_g]], axis=1).astype(F32)
            a_s = jnp.concatenate([g[rows, :] for g in gate_refs[n_g:]], axis=1).astype(F32)
            mixed = (_sigmoid(a_r) * rr + _sigmoid(a_s) * ss).astype(BF16)
            mix_ref[rows, :] = mixed
            r_ref[rows, :] = rr.astype(BF16)
            s_ref[rows, :] = ss.astype(BF16)
            y = _dot(mixed, wo_ref[...])
            y_ref[rows, :] = y
            ny, _ = _rms(y, d)
            hres = x_ref[rows, :] + mod_ref[:, 2 * d:3 * d] * (ny * gp_ref[...])
            hres_ref[rows, :] = hres
            n2, _ = _rms(hres, d)
            h2_ref[rows, :] = (n2 * g2_ref[...] * (1.0 + mod_ref[:, 4 * d:5 * d]) + mod_ref[:, 3 * d:4 * d]).astype(BF16)

    row = lambda i: (i, 0)
    fix = lambda i: (0, 0)
    tile_bf = ((s, d), BF16, (tm, d), row)
    tile_f = ((s, d), F32, (tm, d), row)
    return _call("merge_out", body, (s // tm,),
                 [(retg, (tm, RET_V), row), (sb, (tm, SB_W), row), (w_ret, (RET_V, d), fix), (w_sb_t, (d, SB_W), fix),
                  (w_out, (d, d), fix)]
                 + [(proj, (tm, gw), functools.partial(lambda i, k: (i, ar_off + k), k=k)) for k in range(2 * n_g)]
                 + [(x, (tm, d), row), (mod, (1, 6 * d), fix), (gp1, (1, d), fix), (g2, (1, d), fix)],
                 [tile_bf, tile_bf, tile_bf, tile_f, tile_f, tile_bf])


def _ff1(h2, w_ff1_t, tm, tn):
    s, f = h2.shape[0], w_ff1_t.shape[0]
    tm = min(tm, s)

    def body(a_ref, w_ref, u_ref, act_ref):
        u = _dot(a_ref[...], w_ref[...], _NT)
        r = jnp.maximum(u, 0.0)
        u_ref[...] = u.astype(BF16)
        act_ref[...] = (r * r).astype(BF16)

    d = h2.shape[1]
    return _call("ff1", body, (f // tn, s // tm),
                 [(h2, (tm, d), lambda j, i: (i, 0)), (w_ff1_t, (tn, d), lambda j, i: (j, 0))],
                 [((s, f), BF16, (tm, tn), lambda j, i: (i, j))] * 2)


def _ff2_loss(act, w_ff2, hres, target, mod, gp2, tm):
    s, d = hres.shape
    f = act.shape[1]

    def body(a_ref, w_ref, h_ref, t_ref, mod_ref, gp_ref, dout_ref, df_ref, loss_ref, dgt_ref, dgp_ref):
        _zero_at_start([loss_ref, dgt_ref, dgp_ref])
        gt, gp = mod_ref[:, 5 * d:6 * d], gp_ref[...]
        for rows in _pieces(tm):
            ff = _dot(a_ref[rows, :], w_ref[...])
            nf, rf = _rms(ff, d)
            out = h_ref[rows, :] + gt * (nf * gp)
            err = out - t_ref[rows, :]
            sq = jnp.sum(err * err, axis=1, keepdims=True)
            loss_ref[...] += jnp.sum(sq, axis=0, keepdims=True)
            dout = err * (1.0 / d)
            dout_ref[rows, :] = dout
            dgt_ref[...] += _colsum(dout * (nf * gp))
            dgp_ref[...] += _colsum(dout * gt * nf)
            df_ref[rows, :] = _rms_bwd(dout * gt * gp, nf, rf, d).astype(BF16)

    row = lambda i: (i, 0)
    fix = lambda i: (0, 0)
    return _call("ff2_loss", body, (s // tm,),
                 [(act, (tm, f), row), (w_ff2, (f, d), fix), (hres, (tm, d), row), (target, (tm, d), row),
                  (mod, (1, 6 * d), fix), (gp2, (1, d), fix)],
                 [((s, d), F32, (tm, d), row), ((s, d), BF16, (tm, d), row), ((1, 1), F32, (1, 1), fix),
                  ((1, d), F32, (1, d), fix), ((1, d), F32, (1, d), fix)])


def _ffn_bwd(df, w_ff2, u, act, h2, tn):
    s, d = df.shape
    f = w_ff2.shape[0]

    def body(df_ref, w_ref, u_ref, act_ref, h2_ref, du_ref, gw2_ref, gw1_ref):
        dfb = df_ref[...]
        du = (_dot(dfb, w_ref[...], _NT) * (2.0 * jnp.maximum(u_ref[...].astype(F32), 0.0))).astype(BF16)
        du_ref[...] = du
        gw2_ref[...] = _dot(act_ref[...], dfb, _TN).astype(BF16)
        gw1_ref[...] = _dot(h2_ref[...], du, _TN).astype(BF16)

    fix = lambda j: (0, 0)
    col = lambda j: (0, j)
    return _call("ffn_bwd", body, (f // tn,),
                 [(df, (s, d), fix), (w_ff2, (tn, d), lambda j: (j, 0)), (u, (s, tn), col), (act, (s, tn), col),
                  (h2, (s, d), fix)],
                 [((s, f), BF16, (s, tn), col), ((f, d), BF16, (tn, d), lambda j: (j, 0)),
                  ((f // tn, d, tn), BF16, (None, d, tn), lambda j: (j, 0, 0))])


def _ff1_bwd(du, w_ff1_t, hres, dout, y, mixed, mod, g2, gp1, tm, riders=None):
    s, d = hres.shape
    f = du.shape[1]

    def body(a_ref, w_ref, h_ref, do_ref, y_ref, mix_ref, mod_ref, g2_ref, gp_ref,
             dh_ref, dy_ref, dsh_ref, dsc_ref, dg2_ref, dgt_ref, dgp_ref, gwo_ref, acc_ref):
        _zero_at_start([dsh_ref, dsc_ref, dg2_ref, dgt_ref, dgp_ref, acc_ref])
        g2, sc2 = g2_ref[...], mod_ref[:, 4 * d:5 * d]
        gt, gp = mod_ref[:, 2 * d:3 * d], gp_ref[...]
        for rows in _pieces(tm):
            dh2 = _dot(a_ref[rows, :], w_ref[...])
            n2, r2 = _rms(h_ref[rows, :], d)
            dsh_ref[...] += _colsum(dh2)
            dsc_ref[...] += _colsum(dh2 * n2 * g2)
            dg2_ref[...] += _colsum(dh2 * n2 * (1.0 + sc2))
            dhres = do_ref[rows, :] + _rms_bwd(dh2 * g2 * (1.0 + sc2), n2, r2, d)
            dh_ref[rows, :] = dhres
            ny, ry = _rms(y_ref[rows, :], d)
            dgt_ref[...] += _colsum(dhres * (ny * gp))
            dgp_ref[...] += _colsum(dhres * gt * ny)
            dy_ref[rows, :] = _rms_bwd(dhres * gt * gp, ny, ry, d).astype(BF16)
        acc_ref[...] += _dot(mix_ref[...], dy_ref[...], _TN)

        @pl.when(pl.program_id(0) == s // tm - 1)
        def _():
            gwo_ref[...] = acc_ref[...].astype(BF16)

    row = lambda i: (i, 0)
    fix = lambda i: (0, 0)
    vec = ((1, d), F32, (1, d), fix)
    return _call("ff1_bwd", body, (s // tm,),
                 [(du, (tm, f), row), (w_ff1_t, (f, d), fix), (hres, (tm, d), row), (dout, (tm, d), row),
                  (y, (tm, d), row), (mixed, (tm, d), row), (mod, (1, 6 * d), fix), (g2, (1, d), fix), (gp1, (1, d), fix)],
                 [((s, d), F32, (tm, d), row), ((s, d), BF16, (tm, d), row), vec, vec, vec, vec, vec,
                  ((d, d), BF16, (d, d), fix)], scratch=[pltpu.VMEM((d, d), F32)], riders=riders)


def _out_bwd(dy, w_out, proj, r_bf, s_bf, tm, tn, riders=None):
    s, d = dy.shape
    ar_off = (2 * RET_QK + 2 * RET_V + 3 * SB_W) // tn
    as_off = ar_off + d // tn

    def body(a_ref, w_ref, ar_ref, as_ref, r_ref, s_ref, dr_ref, ds_ref, dar_ref, das_ref):
        dm = _dot(a_ref[...], w_ref[...], _NT)
        sr, ss = _sigmoid(ar_ref[...].astype(F32)), _sigmoid(as_ref[...].astype(F32))
        dr_ref[...] = (dm * sr).astype(BF16)
        ds_ref[...] = (dm * ss).astype(BF16)
        dar_ref[...] = (dm * r_ref[...].astype(F32) * sr * (1.0 - sr)).astype(BF16)
        das_ref[...] = (dm * s_ref[...].astype(F32) * ss * (1.0 - ss)).astype(BF16)

    tile = (tm, tn)
    here = lambda j, i: (i, j)
    return _call("out_bwd", body, (d // tn, s // tm),
                 [(dy, (tm, d), lambda j, i: (i, 0)), (w_out, (tn, d), lambda j, i: (j, 0)),
                  (proj, tile, lambda j, i: (i, ar_off + j)), (proj, tile, lambda j, i: (i, as_off + j)),
                  (r_bf, tile, here), (s_bf, tile, here)],
                 [((s, d), BF16, tile, here)] * 4, riders=riders)


def _branch_bwd(d_r, d_s, retg, sb, w_ret, w_sb_t, ret, proj, gn_g, riders=None):
    s, d = d_r.shape
    n_step = 4
    part_v, part_s, part_d = RET_V // n_step, SB_W // n_step, d // n_step
    per_dev = d // N_DEV
    n_blk = part_d // per_dev
    gate_off = (2 * RET_QK + RET_V) // part_v

    def body(dr_ref, ds_ref, rg_ref, sb_ref, wr_ref, ws_ref, r_ref, g_ref, w_ref,
             dsb_ref, gwr_ref, gws_ref, dg_ref, dret_ref, dw_ref):
        i = pl.program_id(0)
        dr, ds = dr_ref[...], ds_ref[...]
        dsb_ref[...] = _dot(ds, ws_ref[...]).astype(BF16)
        gwr_ref[...] = _dot(rg_ref[...], dr, _TN).astype(BF16)
        cols = pl.ds(pl.multiple_of(i * part_d, part_d), part_d)
        gws = _dot(sb_ref[...], ds_ref[:, cols], _TN).astype(BF16)
        for k in range(n_blk):
            gws_ref[k] = gws[:, k * per_dev:(k + 1) * per_dev]
        dretg = _dot(dr, wr_ref[...], _NT)
        for h in range(part_v // RET_DV):
            cols = slice(h * RET_DV, (h + 1) * RET_DV)
            o, g, w, d_o = r_ref[:, cols], g_ref[:, cols].astype(F32), w_ref[:, cols], dretg[:, cols]
            mu = jnp.sum(o, axis=1, keepdims=True) * (1.0 / RET_DV)
            xc = o - mu
            rstd = lax.rsqrt(jnp.sum(xc * xc, axis=1, keepdims=True) * (1.0 / RET_DV) + EPS)
            n = xc * rstd
            sg = _sigmoid(g)
            silu = g * sg
            dg_ref[:, cols] = (d_o * n * w * (sg * (1.0 + g * (1.0 - sg)))).astype(BF16)
            dw_ref[:, cols] = _colsum(d_o * silu * n)
            dn = d_o * silu * w
            m1 = jnp.sum(dn, axis=1, keepdims=True) * (1.0 / RET_DV)
            m2 = jnp.sum(dn * n, axis=1, keepdims=True) * (1.0 / RET_DV)
            dret_ref[:, cols] = (rstd * (dn - m1 - n * m2)).astype(BF16)

    fix = lambda i: (0, 0)
    col = lambda i: (0, i)
    return _call("branch_bwd", body, (n_step,),
                 [(d_r, (s, d), fix), (d_s, (s, d), fix), (retg, (s, part_v), col), (sb, (s, SB_W), fix),
                  (w_ret, (part_v, d), lambda i: (i, 0)), (w_sb_t, (d, part_s), col),
                  (ret, (s, part_v), col), (proj, (s, part_v), lambda i: (0, gate_off + i)), (gn_g, (1, part_v), col)],
                 [((s, SB_W), BF16, (s, part_s), col), ((RET_V, d), BF16, (part_v, d), lambda i: (i, 0)),
                  ((N_DEV, SB_W, per_dev), BF16, (n_blk, SB_W, per_dev), lambda i: (i, 0, 0)),
                  ((s, RET_V), BF16, (s, part_v), col), ((s, RET_V), BF16, (s, part_v), col),
                  ((1, RET_V), F32, (1, part_v), col)], riders=riders)


def _ret_bwd(qk_rot, proj, dret, log_gamma, t, riders=None):
    s = qk_rot.shape[0]
    n_pair = HEADS // 2
    pw = 2 * RET_DV
    wq, wv = RET_PAIRS * LANES, RET_PAIRS * pw
    n_blk = s // t
    pairs = range(RET_PAIRS)

    def load(q_ref, k_ref, v_ref, do_ref):
        return ([_lanes(q_ref, p, LANES) for p in pairs], [_lanes(k_ref, p, LANES) for p in pairs],
                [_lanes(v_ref, p, pw) for p in pairs], [_lanes(do_ref, p, pw) for p in pairs])

    def d_scores(lg_ref, hp, i, qb, kb, vb, dob):
        z, w = _ret_block(lg_ref, hp, i, t, qb, kb)
        dp = jnp.concatenate([_dot(dob[:, 0:RET_DV], vb[:, 0:RET_DV], _NT),
                              _dot(dob[:, RET_DV:pw], vb[:, RET_DV:pw], _NT)], axis=0)
        return (z * w).astype(BF16), (dp * w).astype(BF16)

    def up_body(lg_ref, q_ref, k_ref, v_ref, do_ref, dq_ref, state_ref):
        hg, i = pl.program_id(0), pl.program_id(1)

        @pl.when(i == 0)
        def _():
            state_ref[...] = jnp.zeros_like(state_ref)

        qbs, kbs, vbs, dobs = load(q_ref, k_ref, v_ref, do_ref)
        dss = [d_scores(lg_ref, hg * RET_PAIRS + p, i, qbs[p], kbs[p], vbs[p], dobs[p])[1] for p in pairs]
        for p in pairs:
            dq_ref[:, p * LANES:(p + 1) * LANES] = (_dot(_side_by_side(dss[p], t), _stack_heads(kbs[p]))
                                                    + _dot(dobs[p], state_ref[p], _NT)).astype(BF16)
        for p in pairs:
            state_ref[p] += _pair_mask() * _dot(kbs[p], vbs[p], _TN)

    def down_body(lg_ref, q_ref, k_ref, v_ref, do_ref, dk_ref, dv_ref, state_ref):
        hg, i = pl.program_id(0), n_blk - 1 - pl.program_id(1)

        @pl.when(pl.program_id(1) == 0)
        def _():
            state_ref[...] = jnp.zeros_like(state_ref)

        qbs, kbs, vbs, dobs = load(q_ref, k_ref, v_ref, do_ref)
        both = [d_scores(lg_ref, hg * RET_PAIRS + p, i, qbs[p], kbs[p], vbs[p], dobs[p]) for p in pairs]
        for p in pairs:
            pp, ds = both[p]
            later = state_ref[p]
            dv_ref[:, p * pw:(p + 1) * pw] = (jnp.concatenate(
                [_dot(pp[:t], dobs[p][:, 0:RET_DV], _TN), _dot(pp[t:], dobs[p][:, RET_DV:pw], _TN)],
                axis=1) + _dot(kbs[p], later)).astype(BF16)
            dk_ref[:, p * LANES:(p + 1) * LANES] = (_dot(ds, _stack_heads(qbs[p]), _TN)
                                                    + _dot(vbs[p], later, _NT)).astype(BF16)
        for p in pairs:
            state_ref[p] += _pair_mask() * _dot(qbs[p], dobs[p], _TN)

    n_grp = n_pair // RET_PAIRS

    def ins(order):
        return [(log_gamma, None, pltpu.SMEM),
                (qk_rot, (t, wq), lambda hg, i: (order(i), hg)),
                (qk_rot, (t, wq), lambda hg, i: (order(i), n_grp + hg)),
                (proj, (t, wv), lambda hg, i: (order(i), 2 * RET_QK // wv + hg)),
                (dret, (t, wv), lambda hg, i: (order(i), hg))]

    up = lambda i: i
    down = lambda i: n_blk - 1 - i
    scratch = [pltpu.VMEM((RET_PAIRS, LANES, pw), F32)]
    dq = _call("ret_bwd_q", up_body, (n_grp, n_blk), ins(up),
               [((s, RET_QK), BF16, (t, wq), lambda hg, i: (i, hg))], scratch=scratch)[0]
    dk, dv, *rest = _call("ret_bwd_kv", down_body, (n_grp, n_blk), ins(down),
                          [((s, RET_QK), BF16, (t, wq), lambda hg, i: (down(i), hg)),
                           ((s, RET_V), BF16, (t, wv), lambda hg, i: (down(i), hg))],
                          scratch=scratch, riders=riders)
    return [dq, dk, dv] + rest


def _sb_bwd(q_sb, proj, weights, do, tq, tk, riders=None):
    s = q_sb.shape[0]
    k_off = (2 * RET_QK + 2 * RET_V + SB_W) // LANES
    n_pair = HEADS // 2
    _check_tiles(s, tq, tk, SB_GROUP)

    def body(q_ref, k_ref, v_ref, a_ref, do_ref, dq_ref, dk_ref, dv_ref):
        i = pl.program_id(1)

        @pl.when(i == 0)
        def _():
            dk_ref[...] = jnp.zeros_like(dk_ref)
            dv_ref[...] = jnp.zeros_like(dv_ref)

        lower = _tri(tk, False)
        qs = _stack_heads(q_ref[...])
        dos = _stack_heads(do_ref[...].astype(BF16))

        def make_step(near_diagonal, n_sub=SB_GROUP):
            def step(g, carry):
                c_e, dq = carry
                js = [g * SB_GROUP + sub for sub in range(n_sub)]
                rows = [_key_rows(j, tk) for j in js]
                zs = [_dot(qs, k_ref[rw, :], _NT) for rw in rows]
                das = [_dot(dos, v_ref[rw, :], _NT) for rw in rows]
                avals = [a_ref[j] for j in js]
                for a, rw in zip(avals, rows):
                    dv_ref[rw, :] += _dot(a, dos, _TN)
                es = [a.astype(F32) * da for a, da in zip(avals, das)]
                prefixes = [_dot(e, lower) for e in es]
                betas = [1.0 / (1.0 + jnp.exp2(-z)) for z in zs]
                for sub in range(n_sub):
                    dz = es[sub] - (es[sub] + prefixes[sub] + c_e) * betas[sub]
                    if near_diagonal:
                        dz = jnp.where(_sb_valid(i, js[sub], tq, tk), dz, 0.0)
                    dz = dz.astype(BF16)
                    dk_ref[rows[sub], :] += _dot(dz, qs, _TN)
                    dq = dq + _dot(_side_by_side(dz, tq), _stack_heads(k_ref[rows[sub], :]))
                    c_e = c_e + jnp.sum(es[sub], axis=1, keepdims=True)
                return c_e, dq
            return step

        n_full = _n_full(i, tq, tk, SB_GROUP)
        carry = (jnp.zeros((2 * tq, 1), F32), jnp.zeros((tq, LANES), F32))
        carry = lax.fori_loop(0, n_full, make_step(False), carry)
        _, dq = _diagonal_step(i, tq, tk, lambda n_sub: (lambda n, cr: make_step(True, n_sub)(n_full, cr)), carry)
        dq_ref[...] = dq

    blk = lambda hp, i: (i, hp)
    n_kb = s // tk
    return _call("sb_bwd", body, (n_pair, s // tq),
                 [(q_sb, (tq, LANES), blk),
                  (proj, (s, LANES), lambda hp, i: (0, k_off + hp)),
                  (proj, (s, LANES), lambda hp, i: (0, k_off + n_pair + hp)),
                  (weights, (None, None, n_kb, 2 * tq, tk), lambda hp, i: (hp, i, 0, 0, 0)),
                  (do, (tq, LANES), blk)],
                 [((s, SB_W), F32, (tq, LANES), blk),
                  ((s, SB_W), F32, (s, LANES), lambda hp, i: (0, hp)),
                  ((s, SB_W), F32, (s, LANES), lambda hp, i: (0, hp))], riders=riders)


def _assemble_dproj(dq_r, dk_r, dv_r, dg_r, dq_s, dk_s, dv_s, da_r, da_s, cos, sin, idx_col, lg_lanes, tm, riders=None):
    s, d = da_r.shape
    width = 2 * RET_QK + 2 * RET_V + 3 * SB_W + 2 * d

    def body(dq_ref, dk_ref, dv_ref, dg_ref, dqs_ref, dks_ref, dvs_ref, dar_ref, das_ref, cos_ref, sin_ref,
             idx_ref, lg_ref, o_ref):
        lane = lax.broadcasted_iota(jnp.int32, (1, LANES), 1)
        first = jnp.bitwise_and(lane, RET_DQK - 1) < (RET_DQK // 2)
        cos, sin = cos_ref[...], sin_ref[...]
        idx = idx_ref[...]
        for src, base, sign, scale in ((dq_ref, 0, 1.0, 1.0), (dk_ref, RET_QK, -1.0, RET_DQK ** -0.5)):
            for g in range(RET_QK // LANES):
                v = src[:, g * LANES:(g + 1) * LANES].astype(F32) * (_decay_scale(lg_ref, idx, g, sign) * scale)
                sw = jnp.where(first, pltpu.roll(v, LANES - RET_DQK // 2, 1), pltpu.roll(v, RET_DQK // 2, 1))
                o_ref[:, base + g * LANES:base + (g + 1) * LANES] = (v * cos - sw * sin).astype(BF16)
        off = 2 * RET_QK
        o_ref[:, off:off + RET_V] = dv_ref[...].astype(BF16)
        off += RET_V
        o_ref[:, off:off + RET_V] = dg_ref[...]
        off += RET_V
        o_ref[:, off:off + SB_W] = (dqs_ref[...] * (SB_DH ** -0.5)).astype(BF16)
        off += SB_W
        o_ref[:, off:off + SB_W] = (dks_ref[...] * LN2).astype(BF16)
        off += SB_W
        o_ref[:, off:off + SB_W] = dvs_ref[...].astype(BF16)
        off += SB_W
        o_ref[:, off:off + d] = dar_ref[...]
        off += d
        o_ref[:, off:off + d] = das_ref[...]

    row = lambda i: (i, 0)
    ins = [(a, (tm, a.shape[1]), row) for a in (dq_r, dk_r, dv_r, dg_r, dq_s, dk_s, dv_s, da_r, da_s, cos, sin, idx_col)]
    ins.append((lg_lanes, (1, RET_QK), lambda i: (0, 0)))
    return _call("assemble_dproj", body, (s // tm,), ins, [((s, width), BF16, (tm, width), row)], riders=riders)


def _in_bwd(dproj, w_in_t, x, dhres, mod, g1, tm, riders=None):
    s, d = x.shape
    width = dproj.shape[1]

    def body(a_ref, w_ref, x_ref, dh_ref, mod_ref, g_ref, dx_ref, dsh_ref, dsc_ref, dg_ref):
        _zero_at_start([dsh_ref, dsc_ref, dg_ref])
        g1, sc1 = g_ref[...], mod_ref[:, d:2 * d]
        for rows in _pieces(tm):
            dh = _dot(a_ref[rows, :], w_ref[...])
            n1, r1 = _rms(x_ref[rows, :], d)
            dsh_ref[...] += _colsum(dh)
            dsc_ref[...] += _colsum(dh * n1 * g1)
            dg_ref[...] += _colsum(dh * n1 * (1.0 + sc1))
            dx_ref[rows, :] = dh_ref[rows, :] + _rms_bwd(dh * g1 * (1.0 + sc1), n1, r1, d)

    row = lambda i: (i, 0)
    fix = lambda i: (0, 0)
    vec = ((1, d), F32, (1, d), fix)
    return _call("in_bwd", body, (s // tm,),
                 [(dproj, (tm, width), row), (w_in_t, (width, d), fix), (x, (tm, d), row), (dhres, (tm, d), row),
                  (mod, (1, 6 * d), fix), (g1, (1, d), fix)],
                 [((s, d), F32, (tm, d), row), vec, vec, vec], riders=riders)


def _adamw(w, g, m, v):
    m = ADAM_B1 * m + (1.0 - ADAM_B1) * g
    v = ADAM_B2 * v + (1.0 - ADAM_B2) * (g * g)
    m_hat = m / (1.0 - ADAM_B1 ** ADAM_STEP)
    v_hat = v / (1.0 - ADAM_B2 ** ADAM_STEP)
    delta = -ADAM_LR * (m_hat / (jnp.sqrt(v_hat) + ADAM_EPS) + ADAM_WD * w)
    return delta, m, v


def _adam_reduce(name, sets, steps):
    n = len(sets)

    def body(*refs):
        for k in range(n):
            p_ref, w_ref, m_ref, v_ref = refs[4 * k:4 * k + 4]
            outs = refs[4 * n + 4 * k:4 * n + 4 * k + 4]
            g = p_ref[0].astype(F32)
            for j in range(1, p_ref.shape[0]):
                g = g + p_ref[j].astype(F32)
            for o_ref, val in zip(outs, (g,) + _adamw(w_ref[...], g, m_ref[...], v_ref[...])):
                o_ref[...] = val

    ins, outs = [], []
    row = lambda i: (i, 0)
    for parts, w, m, v in sets:
        rws, cls = w.shape
        tr = rws // steps
        assert tr * steps == rws and tr % 16 == 0
        ins += [(parts, (parts.shape[0], tr, cls), lambda i: (0, i, 0)), (w, (tr, cls), row), (m, (tr, cls), row),
                (v, (tr, cls), row)]
        outs += [((rws, cls), F32, (tr, cls), row)] * 4
    res = _call(name, body, (steps,), ins, outs)
    return [res[4 * k:4 * k + 4] for k in range(n)]


def _ada_bwd_adam(cs_t, dmod_cols, w, m, v, tr):
    d, nc = w.shape

    def body(c_ref, dm_ref, w_ref, m_ref, v_ref, g_out, d_out, m_out, v_out):
        g = c_ref[0] * dm_ref[0:1, :]
        for r in range(1, N_DEV):
            g = g + c_ref[r] * dm_ref[r:r + 1, :]
        delta, mn, vn = _adamw(w_ref[...], g, m_ref[...], v_ref[...])
        g_out[...] = g
        d_out[...] = delta
        m_out[...] = mn
        v_out[...] = vn

    row = lambda i: (i, 0)
    blk = (tr, nc)
    return _call("ada_bwd_adam", body, (d // tr,),
                 [(cs_t, (N_DEV, tr, 1), lambda i: (0, i, 0)), (dmod_cols, (N_DEV, nc), lambda i: (0, 0)),
                  (w, blk, row), (m, blk, row), (v, blk, row)],
                 [((d, nc), F32, blk, row)] * 4)


def _small_adam(parts, ws, ms, vs):
    n = len(ws)
    widths = [w.shape[1] for w in ws]
    total = parts.shape[1]
    assert sum(widths) + LANES == total

    def body(p_ref, *refs):
        w_refs, m_refs, v_refs = refs[:n], refs[n:2 * n], refs[2 * n:3 * n]
        outs = refs[3 * n:]
        g = p_ref[0:1, :]
        for k in range(1, N_DEV):
            g = g + p_ref[k:k + 1, :]
        off = 0
        for i, width in enumerate(widths):
            gi = g[:, off:off + width]
            delta, mn, vn = _adamw(w_refs[i][...], gi, m_refs[i][...], v_refs[i][...])
            for o_ref, val in zip(outs[4 * i:4 * i + 4], (gi, delta, mn, vn)):
                o_ref[...] = val
            off += width
        outs[4 * n][...] = g[:, off:off + LANES]

    fix = lambda i: (0, 0)
    vec = lambda a: (a, (1, a.shape[1]), fix)
    out_specs = [((1, width), F32, (1, width), fix) for width in widths for _ in range(4)]
    out_specs.append(((1, LANES), F32, (1, LANES), fix))
    res = _call("small_adam", body, (1,),
                [(parts, (N_DEV, total), fix)] + [vec(a) for a in list(ws) + list(ms) + list(vs)], out_specs)
    return [res[4 * i:4 * i + 4] for i in range(n)], res[4 * n]


def kernel(x, c, positions, ada_w, ada_b, pre_mix_g, post_mix_g, pre_ffn_g, post_ffn_g, w_in, ret_gn_g, w_ret_branch, w_sb_branch, w_out, w_ff1, w_ff2, loss_target, m_ada_w, m_ada_b, m_pre_mix_g, m_post_mix_g, m_pre_ffn_g, m_post_ffn_g, m_w_in, m_ret_gn_g, m_w_ret_branch, m_w_sb_branch, m_w_out, m_w_ff1, m_w_ff2, v_ada_w, v_ada_b, v_pre_mix_g, v_post_mix_g, v_pre_ffn_g, v_post_ffn_g, v_w_in, v_ret_gn_g, v_w_ret_branch, v_w_sb_branch, v_w_out, v_w_ff1, v_w_ff2):
    _, s, d = x.shape
    d_ff = w_ff1.shape[2] * N_DEV
    d_in = w_in.shape[2] * N_DEV
    me = 4 * lax.axis_index("x") + 2 * lax.axis_index("y") + lax.axis_index("c")
    x2, tgt = x[0], loss_target[0]

    core = lax.axis_index("c").astype(jnp.int32).reshape(1)
    bf = lambda w: w[0].astype(BF16)

    w_in_t, m_in_t, v_in_t = (jnp.swapaxes(a[0], 0, 1) for a in (w_in, m_w_in, v_w_in))

    c_all, g_in = _exchange("gather_in", [c, w_in_t.astype(BF16)], ["gather", "gather_chip"])
    c_all = c_all.reshape(N_DEV, d)

    n_ada = ada_w.shape[2]
    ada_b_cols = lax.dynamic_slice(ada_b, (0, me * n_ada), (1, n_ada))
    cs_all, mod_cols = _ada_fwd(c_all, ada_w[0], ada_b_cols)
    mod_all, g_in = _exchange("gather_mod", [mod_cols, g_in], ["gather", "forward"])
    mod = lax.dynamic_index_in_dim(mod_all, me, axis=1, keepdims=False).reshape(1, 6 * d)

    tm = min(256, s)
    h = _pre_norm(x2, pre_mix_g, mod, 2 * tm)[0]
    wt_in = g_in.reshape(d_in, d)
    bf_t = lambda w: jnp.swapaxes(w[0], 0, 1).astype(BF16)
    small_w = [bf(w_ret_branch), bf_t(w_sb_branch), bf(w_out)]
    proj, *small_w = _matmul("in_proj", h, wt_in, "nt", s, 512, BF16, riders=(small_w, ["gather_chip"] * 3))
    pos_col = positions.reshape(s, 1).astype(F32)
    freqs = ROPE_BASE ** (-jnp.arange(0, RET_DQK, 2, dtype=F32) / RET_DQK)
    inv_freq = jnp.tile(freqs, LANES // (RET_DQK // 2)).reshape(1, LANES)
    log_gamma_np = np.log1p(-(2.0 ** (-5.0 - np.arange(HEADS))))
    log_gamma = jnp.asarray(log_gamma_np, F32)
    lg_lanes = jnp.asarray(np.repeat(log_gamma_np, RET_DQK).reshape(1, RET_QK), F32)
    idx_col = (jnp.arange(s, dtype=F32) - (s // 2)).reshape(s, 1)
    qk_rot, q_sb, cos_t, sin_t = _prep(proj, pos_col, idx_col, inv_freq, lg_lanes, 2 * tm)
    tq, tk = min(256, s), min(128, s)
    tq_sb = min(SB_TQ, s)
    sb, sb_weights, *big_w = _sb_fwd(q_sb, proj, tq_sb, tk, riders=([bf(w_ff2), bf_t(w_ff1)], ["gather_chip"] * 2))
    ret, retg, g_ret, g_sb, g_out, g_ff2, g_ff1 = _ret_fwd(qk_rot, proj, ret_gn_g, log_gamma, tq,
                                                           riders=(small_w + big_w, ["forward"] * 5))
    wf_ret = g_ret.reshape(RET_V, d)
    wt_sb = g_sb.reshape(d, SB_W)
    wf_out = g_out.reshape(d, d)
    wt_ff1 = g_ff1.reshape(d_ff, d)
    wf_ff2 = g_ff2.reshape(d_ff, d)
    mixed, r_bf, s_bf, y, hres, h2 = _merge_out(retg, sb, wf_ret, wt_sb, wf_out, proj, x2, mod, post_mix_g, pre_ffn_g, tm)
    u, act = _ff1(h2, wt_ff1, s, 512)
    dout, df, loss_sum, d_gt2, d_gp2 = _ff2_loss(act, wf_ff2, hres, tgt, mod, post_ffn_g, tm)

    du, gw_ff2, gw_ff1 = _ffn_bwd(df, wf_ff2, u, act, h2, d_ff // N_DEV)
    gw_ff2 = gw_ff2.reshape(N_DEV, d_ff // N_DEV, d)
    dhres, dy, d_sh2, d_sc2, d_g2, d_gt1, d_gp1, gw_out = _ff1_bwd(
        du, wt_ff1, hres, dout, y, mixed, mod, pre_ffn_g, post_mix_g, tm)
    gw_out = gw_out.reshape(N_DEV, d // N_DEV, d)
    d_r, d_s, da_r, da_s, t_ff1, t_ff2 = _out_bwd(dy, wf_out, proj, r_bf, s_bf, 2 * tm, min(512, d),
                                                  riders=([gw_ff1, gw_ff2], ["pair"] * 2))
    s_ff1, s_ff2 = _pair_sum("pair_sum_ff", [(gw_ff1, t_ff1), (gw_ff2, t_ff2)], core)
    dsb, gw_ret, gw_sb, dg_r, dret, d_gn, p_out = _branch_bwd(d_r, d_s, retg, sb, wf_ret, wt_sb, ret, proj, ret_gn_g,
                                                              riders=([gw_out], ["scatter"]))
    gw_ret = gw_ret.reshape(N_DEV, RET_V // N_DEV, d)
    dq_s, dk_s, dv_s, p_ff1, p_ff2 = _sb_bwd(q_sb, proj, sb_weights, dsb, tq_sb, tk,
                                             riders=([s_ff1, s_ff2], ["chip_scatter"] * 2))
    dq_r, dk_r, dv_r, p_sb = _ret_bwd(qk_rot, proj, dret, log_gamma, tq, riders=([gw_sb], ["scatter"]))
    dproj = _assemble_dproj(dq_r, dk_r, dv_r, dg_r, dq_s, dk_s, dv_s, da_r, da_s, cos_t, sin_t, idx_col, lg_lanes, 2 * tm)[0]
    gw_in, p_ret = _matmul("grad_w_in", dproj, h, "tn", 512, d, BF16, riders=([gw_ret], ["scatter"]))
    gw_in = gw_in.reshape(N_DEV, d_in // N_DEV, d)
    t_in = _exchange("pair_in", [gw_in], ["pair"])[0]
    s_in = _pair_sum("pair_sum_in", [(gw_in, t_in)], core)[0]
    grad_x, d_sh1, d_sc1, d_g1, p_in = _in_bwd(dproj, wt_in, x2, dhres, mod, pre_mix_g, tm,
                                               riders=([s_in], ["chip_scatter"]))
    loss_lanes = jnp.pad(loss_sum, ((0, 0), (0, LANES - 1)))
    small = jnp.concatenate([d_sh1, d_sc1, d_gt1, d_sh2, d_sc2, d_gt2, d_g1, d_gp1, d_g2, d_gp2, d_gn, loss_lanes], axis=1)
    small_all = _exchange("gather_small", [small], ["gather"])[0].reshape(N_DEV, small.shape[1])
    parts = [p_in, p_ret, p_sb, p_out, p_ff1, p_ff2]

    res = {}
    names = ["w_ret_branch", "w_sb_branch", "w_out", "w_ff1", "w_ff2"]
    ws = [w_ret_branch, w_sb_branch, w_out, w_ff1, w_ff2]
    ms = [m_w_ret_branch, m_w_sb_branch, m_w_out, m_w_ff1, m_w_ff2]
    vs = [v_w_ret_branch, v_w_sb_branch, v_w_out, v_w_ff1, v_w_ff2]
    sets = [(parts[0], w_in_t, m_in_t, v_in_t)] + [(p, w[0], m[0], v[0]) for p, w, m, v in zip(parts[1:], ws, ms, vs)]
    updated = _adam_reduce("adam_big", sets, 4)
    res["w_in"] = [jnp.swapaxes(o, 0, 1)[None] for o in updated[0]]
    for nm, outs4 in zip(names, updated[1:]):
        res[nm] = [o[None] for o in outs4]
    dmod_cols = lax.dynamic_slice(small_all, (0, me * n_ada), (N_DEV, n_ada))
    res["ada_w"] = [o[None] for o in _ada_bwd_adam(cs_all.reshape(N_DEV, d, 1), dmod_cols, ada_w[0], m_ada_w[0], v_ada_w[0], tm)]
    vec_names = ["ada_b", "pre_mix_g", "post_mix_g", "pre_ffn_g", "post_ffn_g", "ret_gn_g"]
    vec_res, loss_lanes = _small_adam(small_all,
                                      [ada_b, pre_mix_g, post_mix_g, pre_ffn_g, post_ffn_g, ret_gn_g],
                                      [m_ada_b, m_pre_mix_g, m_post_mix_g, m_pre_ffn_g, m_post_ffn_g, m_ret_gn_g],
                                      [v_ada_b, v_pre_mix_g, v_post_mix_g, v_pre_ffn_g, v_post_ffn_g, v_ret_gn_g])
    res.update(zip(vec_names, vec_res))
    loss = (0.5 / d) * loss_lanes[0, 0]
    order = ["ada_w", "ada_b", "pre_mix_g", "post_mix_g", "pre_ffn_g", "post_ffn_g", "w_in", "ret_gn_g",
             "w_ret_branch", "w_sb_branch", "w_out", "w_ff1", "w_ff2"]
    outs = [loss, grad_x[None]]
    for k in range(4):
        outs += [res[nm][k] for nm in order]
    return tuple(outs)
```

```python
import functools

import numpy as np
import jax
import jax.numpy as jnp
from jax import lax
from jax.experimental import pallas as pl
from jax.experimental.pallas import tpu as pltpu

F32 = jnp.float32
BF16 = jnp.bfloat16
N_DEV = 8
AXES = ("x", "y", "c")

EPS = 1e-6
CHUNK = 64
CHUNK_SHIFT = 6
HEADS = 8
RET_DQK = 64
RET_DV = 128
SB_DH = 64
RET_QK = HEADS * RET_DQK
RET_V = HEADS * RET_DV
SB_W = HEADS * SB_DH
ROPE_BASE = 10000.0
LANES = 128

ADAM_LR = 0.001
ADAM_B1 = 0.9
ADAM_B2 = 0.999
ADAM_EPS = 1e-08
ADAM_WD = 0.01
ADAM_STEP = 10

VMEM_LIMIT = 56 * 1024 * 1024

_NN = (((1,), (0,)), ((), ()))
_NT = (((1,), (1,)), ((), ()))
_TN = (((0,), (0,)), ((), ()))


def _dot(a, b, dims=_NN):
    if a.dtype != BF16:
        a = a.astype(BF16)
    if b.dtype != BF16:
        b = b.astype(BF16)
    return lax.dot_general(a, b, dims, preferred_element_type=F32)


def _sigmoid(x):
    return 1.0 / (1.0 + jnp.exp(-x))


def _rms(x, d):
    r = lax.rsqrt(jnp.sum(x * x, axis=1, keepdims=True) * (1.0 / d) + EPS)
    return x * r, r


def _rms_bwd(dn, n, r, d):
    return r * (dn - n * (jnp.sum(dn * n, axis=1, keepdims=True) * (1.0 / d)))


def _colsum(v):
    return jnp.sum(v, axis=0, keepdims=True)


ROW_SPLIT = 2


def _zero_at_start(refs):
    @pl.when(pl.program_id(0) == 0)
    def _():
        for r in refs:
            r[...] = jnp.zeros_like(r)


def _pieces(tm):
    step = tm // ROW_SPLIT
    return [slice(k * step, (k + 1) * step) for k in range(ROW_SPLIT)]


KIND_SLOTS = {"gather": N_DEV, "scatter": N_DEV, "gather_chip": N_DEV, "forward": N_DEV, "pair": N_DEV // 2,
              "chip_scatter": N_DEV // 2}
SEMS_PER_ARRAY = N_DEV - 1


def _exchange_copies(ins, outs, send_sems, recv_sems, local_sems, kinds):
    x, y, c = (lax.axis_index(a) for a in AXES)
    me, chip, sibling = 4 * x + 2 * y + c, 2 * x + y, (x, y, 1 - c)
    mesh_id = pl.DeviceIdType.MESH
    other_chips = []
    for k in range(1, N_DEV // 2):
        px = 1 - x if k & 2 else x
        py = 1 - y if k & 1 else y
        other_chips.append((px, py))
    copies = []
    for i, kind in enumerate(kinds):
        def remote(src, dst, k, to, i=i):
            return pltpu.make_async_remote_copy(
                src_ref=src, dst_ref=dst, send_sem=send_sems.at[i * SEMS_PER_ARRAY + k],
                recv_sem=recv_sems.at[i * SEMS_PER_ARRAY + k], device_id=to, device_id_type=mesh_id)

        if kind in ("gather", "scatter"):
            pick = (lambda ref, d: ref.at[d]) if kind == "scatter" else (lambda ref, d: ref)
            copies.append(pltpu.make_async_copy(pick(ins[i], me), outs[i].at[me], local_sems.at[i]))
            for k in range(1, N_DEV):
                to = (1 - x if k & 4 else x, 1 - y if k & 2 else y, 1 - c if k & 1 else c)
                copies.append(remote(pick(ins[i], 4 * to[0] + 2 * to[1] + to[2]), outs[i].at[me], k - 1, to))
        elif kind == "gather_chip":
            copies.append(pltpu.make_async_copy(ins[i], outs[i].at[me], local_sems.at[i]))
            copies.append(remote(ins[i], outs[i].at[me], 0, sibling))
            for k, (px, py) in enumerate(other_chips):
                copies.append(remote(ins[i], outs[i].at[me], 1 + k, (px, py, c)))
        elif kind == "forward":
            for k, (px, py) in enumerate(other_chips):
                slot = 4 * px + 2 * py + c
                copies.append(remote(outs[i].at[slot], outs[i].at[slot], k, sibling))
        elif kind == "pair":
            for k in range(N_DEV // 2):
                copies.append(remote(ins[i].at[2 * k + 1 - c], outs[i].at[k], k, sibling))
        elif kind == "chip_scatter":
            copies.append(pltpu.make_async_copy(ins[i].at[chip], outs[i].at[chip], local_sems.at[i]))
            for k, (px, py) in enumerate(other_chips):
                copies.append(remote(ins[i].at[2 * px + py], outs[i].at[chip], k, (px, py, c)))
        else:
            raise ValueError(kind)
    return copies


def _exchange_shapes(arrays, kinds):
    shapes = []
    for a, kind in zip(arrays, kinds):
        tail = a.shape if kind in ("gather", "gather_chip") else a.shape[1:]
        shapes.append(jax.ShapeDtypeStruct((KIND_SLOTS[kind],) + tuple(tail), a.dtype))
    return shapes


def _exchange_sems(n):
    return [pltpu.SemaphoreType.DMA((n * SEMS_PER_ARRAY,)), pltpu.SemaphoreType.DMA((n * SEMS_PER_ARRAY,)),
            pltpu.SemaphoreType.DMA((n,))]


def _call(name, body, grid, ins, outs, scratch=(), riders=None, prefetch=None):
    any_spec = pl.BlockSpec(memory_space=pl.ANY)
    in_specs = [pl.BlockSpec(memory_space=im) if bs is None else pl.BlockSpec(bs, im) for _, bs, im in ins]
    out_specs = [pl.BlockSpec(bs, im) for _, _, bs, im in outs]
    out_shape = [jax.ShapeDtypeStruct(s, d) for s, d, _, _ in outs]
    operands = [a for a, _, _ in ins]
    scratch = list(scratch)
    aliases = {}
    n_pre = 0 if prefetch is None else 1
    kernel = functools.partial(body) if prefetch is None else (lambda _, *refs: body(*refs))
    if riders is not None:
        arrays, kinds = riders
        nr, n_in, n_out, n_scr = len(arrays), len(ins), len(outs), len(scratch)

        def kernel(*refs):
            refs = refs[n_pre:]
            own_in, ride_in = refs[:n_in], refs[n_in:n_in + nr]
            own_out = refs[n_in + nr:n_in + nr + n_out]
            ride_out = refs[n_in + nr + n_out:n_in + 2 * nr + n_out]
            own_scr = refs[n_in + 2 * nr + n_out:n_in + 2 * nr + n_out + n_scr]
            sems = refs[n_in + 2 * nr + n_out + n_scr:]
            ids = [pl.program_id(a) for a in range(len(grid))]
            first = functools.reduce(jnp.logical_and, [i == 0 for i in ids])
            last = functools.reduce(jnp.logical_and, [i == g - 1 for i, g in zip(ids, grid)])

            @pl.when(first)
            def _():
                for cp in _exchange_copies(ride_in, ride_out, *sems, kinds):
                    cp.start()

            body(*own_in, *own_out, *own_scr)

            @pl.when(last)
            def _():
                for cp in _exchange_copies(ride_in, ride_out, *sems, kinds):
                    cp.wait()

        in_specs += [any_spec] * nr
        out_specs += [any_spec] * nr
        out_shape += _exchange_shapes(arrays, kinds)
        operands += list(arrays)
        scratch += _exchange_sems(nr)
        aliases = {n_pre + n_in + r: n_out + r for r, kind in enumerate(kinds) if kind == "forward"}
    params = pltpu.CompilerParams(dimension_semantics=("arbitrary",) * len(grid), vmem_limit_bytes=VMEM_LIMIT)
    if prefetch is None:
        return pl.pallas_call(kernel, name=name, grid=grid, in_specs=in_specs, out_specs=out_specs,
                              out_shape=out_shape, scratch_shapes=scratch, input_output_aliases=aliases,
                              compiler_params=params)(*operands)
    grid_spec = pltpu.PrefetchScalarGridSpec(num_scalar_prefetch=1, grid=grid, in_specs=in_specs,
                                             out_specs=out_specs, scratch_shapes=scratch)
    return pl.pallas_call(kernel, name=name, grid_spec=grid_spec, out_shape=out_shape,
                          input_output_aliases=aliases, compiler_params=params)(prefetch, *operands)


def _exchange(name, arrays, kinds):
    n = len(arrays)

    def body(*refs):
        copies = _exchange_copies(refs[:n], refs[n:2 * n], *refs[2 * n:], kinds)
        for cp in copies:
            cp.start()
        for cp in copies:
            cp.wait()

    any_spec = pl.BlockSpec(memory_space=pl.ANY)
    return pl.pallas_call(
        functools.partial(body),
        name=name,
        in_specs=[any_spec] * n,
        out_specs=[any_spec] * n,
        out_shape=_exchange_shapes(arrays, kinds),
        scratch_shapes=_exchange_sems(n),
        input_output_aliases={i: i for i, kind in enumerate(kinds) if kind == "forward"},
    )(*arrays)


def _pair_sum(name, pairs, my_core):
    n = len(pairs)

    def body(*refs):
        for k in range(n):
            a_ref, b_ref, o_ref = refs[2 * k], refs[2 * k + 1], refs[2 * n + k]
            o_ref[...] = (a_ref[...].astype(F32) + b_ref[...].astype(F32)).astype(o_ref.dtype)

    ins, outs = [], []
    for mine, theirs in pairs:
        _, rws, cls = mine.shape
        ins += [(mine, (None, rws, cls), lambda k, core: (2 * k + core[0], 0, 0)),
                (theirs, (None, rws, cls), lambda k, core: (k, 0, 0))]
        outs.append(((N_DEV // 2, rws, cls), mine.dtype, (None, rws, cls), lambda k, core: (k, 0, 0)))
    return _call(name, body, (N_DEV // 2,), ins, outs, prefetch=my_core)


def _matmul(name, a, b, kind, tm, tn, out_dtype, blocked_out=False, riders=None):
    if kind == "tn":
        kdim, m = a.shape
    else:
        m, kdim = a.shape
    n = b.shape[0] if kind == "nt" else b.shape[1]
    tm, tn = min(tm, m), min(tn, n)
    dims = {"nn": _NN, "nt": _NT, "tn": _TN}[kind]

    def body(a_ref, b_ref, o_ref):
        o_ref[...] = _dot(a_ref[...], b_ref[...], dims).astype(o_ref.dtype)

    a_spec = (a, (kdim, tm), lambda j, i: (0, i)) if kind == "tn" else (a, (tm, kdim), lambda j, i: (i, 0))
    b_spec = (b, (tn, kdim), lambda j, i: (j, 0)) if kind == "nt" else (b, (kdim, tn), lambda j, i: (0, j))
    if blocked_out:
        out = ((n // tn, m, tn), out_dtype, (None, tm, tn), lambda j, i: (j, i, 0))
    else:
        out = ((m, n), out_dtype, (tm, tn), lambda j, i: (i, j))
    res = _call(name, body, (n // tn, m // tm), [a_spec, b_spec], [out], riders=riders)
    return res[0] if riders is None else res


def _ada_fwd(c_all, ada_w, ada_b_cols):
    def body(c_ref, w_ref, b_ref, cs_ref, o_ref):
        v = c_ref[...]
        cs = v * _sigmoid(v)
        cs_ref[...] = cs
        o_ref[...] = lax.dot_general(cs, w_ref[...], _NN, preferred_element_type=F32,
                                     precision=lax.Precision.HIGHEST) + b_ref[...]

    r, d = c_all.shape
    nc = ada_w.shape[1]
    fix = lambda i: (0, 0)
    return _call("ada_fwd", body, (1,),
                 [(c_all, (r, d), fix), (ada_w, (d, nc), fix), (ada_b_cols, (1, nc), fix)],
                 [((r, d), F32, (r, d), fix), ((r, nc), F32, (r, nc), fix)])


def _pre_norm(x, g, mod, tm, riders=None):
    s, d = x.shape

    def body(x_ref, g_ref, mod_ref, h_ref):
        n, _ = _rms(x_ref[...], d)
        sh, sc = mod_ref[:, 0:d], mod_ref[:, d:2 * d]
        h_ref[...] = (n * g_ref[...] * (1.0 + sc) + sh).astype(BF16)

    return _call("pre_norm", body, (s // tm,),
                 [(x, (tm, d), lambda i: (i, 0)), (g, (1, d), lambda i: (0, 0)),
                  (mod, (1, 6 * d), lambda i: (0, 0))],
                 [((s, d), BF16, (tm, d), lambda i: (i, 0))], riders=riders)


LOG2E = 1.4426950408889634
LN2 = 0.6931471805599453


def _decay_scale(lg_ref, idx, g, sign):
    return jnp.exp((sign * idx) * lg_ref[:, g * LANES:(g + 1) * LANES])


def _prep(proj, pos_col, idx_col, inv_freq, lg_lanes, tm):
    s = proj.shape[0]
    sb_off = (2 * RET_QK + 2 * RET_V) // SB_W
    n_q = RET_QK // LANES

    def body(qk_ref, qs_ref, pos_ref, idx_ref, f_ref, lg_ref, qk_out, qs_out, cos_out, sin_out):
        ang = pos_ref[...] * f_ref[...]
        lane = lax.broadcasted_iota(jnp.int32, (1, LANES), 1)
        first = jnp.bitwise_and(lane, RET_DQK - 1) < (RET_DQK // 2)
        cos = jnp.cos(ang)
        sin = jnp.where(first, -1.0, 1.0) * jnp.sin(ang)
        cos_out[...] = cos
        sin_out[...] = sin
        idx = idx_ref[...]
        for g in range(2 * n_q):
            v = qk_ref[:, g * LANES:(g + 1) * LANES].astype(F32)
            sw = jnp.where(first, pltpu.roll(v, LANES - RET_DQK // 2, 1), pltpu.roll(v, RET_DQK // 2, 1))
            r = v * cos + sw * sin
            if g < n_q:
                r = r * _decay_scale(lg_ref, idx, g, 1.0)
            else:
                r = r * (_decay_scale(lg_ref, idx, g - n_q, -1.0) * (RET_DQK ** -0.5))
            qk_out[:, g * LANES:(g + 1) * LANES] = r.astype(BF16)
        qs_out[...] = (qs_ref[...].astype(F32) * (SB_DH ** -0.5 * LOG2E)).astype(BF16)

    row = lambda i: (i, 0)
    return _call("prep", body, (s // tm,),
                 [(proj, (tm, 2 * RET_QK), row),
                  (proj, (tm, SB_W), lambda i: (i, sb_off)),
                  (pos_col, (tm, 1), row),
                  (idx_col, (tm, 1), row),
                  (inv_freq, (1, LANES), lambda i: (0, 0)),
                  (lg_lanes, (1, RET_QK), lambda i: (0, 0))],
                 [((s, 2 * RET_QK), BF16, (tm, 2 * RET_QK), row),
                  ((s, SB_W), BF16, (tm, SB_W), row),
                  ((s, LANES), F32, (tm, LANES), row),
                  ((s, LANES), F32, (tm, LANES), row)])


def _head_mask(hh):
    lane = lax.broadcasted_iota(jnp.int32, (1, LANES), 1)
    return (lane >= RET_DQK) if hh else (lane < RET_DQK)


def _masked(v, m):
    return jnp.where(m, v, jnp.zeros_like(v))


SB_GROUP = 4
SB_TQ = 256


def _stack_heads(v):
    return jnp.concatenate([_masked(v, _head_mask(0)), _masked(v, _head_mask(1))], axis=0)


def _side_by_side(v, t):
    return jnp.concatenate([v[:t], v[t:]], axis=1)


def _tile_pos(i, j, tq, tk):
    row = jnp.bitwise_and(lax.broadcasted_iota(jnp.int32, (2 * tq, tk), 0), tq - 1) + i * tq
    col = lax.broadcasted_iota(jnp.int32, (2 * tq, tk), 1) + j * tk
    return row, col


def _n_groups(i, tq, tk, grp):
    return ((i + 1) * (tq // tk) + grp - 1) // grp


def _n_full(i, tq, tk, grp):
    return (i * (tq // tk)) // grp


def _key_rows(j, tk):
    return pl.ds(pl.multiple_of(j * tk, tk), tk)


def _ret_weight(lg_rows, i, j, tq, tk):
    row, col = _tile_pos(i, j, tq, tk)
    same = jnp.right_shift(col, CHUNK_SHIFT) == jnp.right_shift(row, CHUNK_SHIFT)
    later = jnp.where(same, jnp.exp((2.0 * lg_rows) * (col - row).astype(F32)), 0.0)
    return jnp.where(col <= row, 1.0, later)


def _lg_rows(lg_ref, hp, tq):
    first = lax.broadcasted_iota(jnp.int32, (2 * tq, 1), 0) < tq
    return jnp.where(first, lg_ref[2 * hp], lg_ref[2 * hp + 1])


def _check_tiles(s, tq, tk, grp):
    assert tq % tk == 0 and tq & (tq - 1) == 0 and tk & (tk - 1) == 0
    assert s % tq == 0 and (s // tk) % grp == 0 and s // tk <= LANES


def _pair_mask():
    r = lax.broadcasted_iota(jnp.int32, (LANES, 2 * RET_DV), 0) >= RET_DQK
    c = lax.broadcasted_iota(jnp.int32, (LANES, 2 * RET_DV), 1) >= RET_DV
    return (r == c).astype(F32)


def _ret_block(lg_ref, hp, i, t, qb, kb):
    w = _ret_weight(_lg_rows(lg_ref, hp, t), i, i, t, t)
    return _dot(_stack_heads(qb), kb, _NT), w


RET_PAIRS = 4


def _lanes(ref, p, width):
    return ref[:, p * width:(p + 1) * width]


def _ret_fwd(qk_rot, proj, gn_g, log_gamma, t, riders=None):
    s = qk_rot.shape[0]
    n_pair = HEADS // 2
    pw = 2 * RET_DV
    wq, wv = RET_PAIRS * LANES, RET_PAIRS * pw
    v_off, gate_off = 2 * RET_QK // wv, (2 * RET_QK + RET_V) // wv
    assert s % t == 0 and t % CHUNK == 0 and t & (t - 1) == 0 and n_pair % RET_PAIRS == 0

    def body(lg_ref, q_ref, k_ref, v_ref, g_ref, w_ref, ret_ref, rg_ref, state_ref):
        hg, i = pl.program_id(0), pl.program_id(1)

        @pl.when(i == 0)
        def _():
            state_ref[...] = jnp.zeros_like(state_ref)

        pairs = range(RET_PAIRS)
        qbs = [_lanes(q_ref, p, LANES) for p in pairs]
        kbs = [_lanes(k_ref, p, LANES) for p in pairs]
        vbs = [_lanes(v_ref, p, pw) for p in pairs]
        zws = [_ret_block(lg_ref, hg * RET_PAIRS + p, i, t, qbs[p], kbs[p]) for p in pairs]
        ps = [(z * w).astype(BF16) for z, w in zws]
        outs = [jnp.concatenate([_dot(ps[p][:t], vbs[p][:, 0:RET_DV]), _dot(ps[p][t:], vbs[p][:, RET_DV:pw])], axis=1)
                + _dot(qbs[p], state_ref[p]) for p in pairs]
        for p in pairs:
            state_ref[p] += _pair_mask() * _dot(kbs[p], vbs[p], _TN)
        for p in pairs:
            for hh in range(2):
                cols = slice(p * pw + hh * RET_DV, p * pw + (hh + 1) * RET_DV)
                o = outs[p][:, hh * RET_DV:(hh + 1) * RET_DV]
                ret_ref[:, cols] = o
                mu = jnp.sum(o, axis=1, keepdims=True) * (1.0 / RET_DV)
                xc = o - mu
                var = jnp.sum(xc * xc, axis=1, keepdims=True) * (1.0 / RET_DV)
                nrm = xc * lax.rsqrt(var + EPS) * w_ref[:, cols]
                g = g_ref[:, cols].astype(F32)
                rg_ref[:, cols] = (g * _sigmoid(g) * nrm).astype(BF16)

    blk = lambda hg, i: (i, hg)
    return _call("ret_fwd", body, (n_pair // RET_PAIRS, s // t),
                 [(log_gamma, None, pltpu.SMEM),
                  (qk_rot, (t, wq), blk),
                  (qk_rot, (t, wq), lambda hg, i: (i, n_pair // RET_PAIRS + hg)),
                  (proj, (t, wv), lambda hg, i: (i, v_off + hg)),
                  (proj, (t, wv), lambda hg, i: (i, gate_off + hg)),
                  (gn_g, (1, wv), lambda hg, i: (0, hg))],
                 [((s, RET_V), F32, (t, wv), blk), ((s, RET_V), BF16, (t, wv), blk)],
                 scratch=[pltpu.VMEM((RET_PAIRS, LANES, pw), F32)], riders=riders)


def _tri(tk, strict_upper):
    r = lax.broadcasted_iota(jnp.int32, (tk, tk), 0)
    cc = lax.broadcasted_iota(jnp.int32, (tk, tk), 1)
    return ((r > cc) if strict_upper else (r < cc)).astype(BF16)


def _diagonal_step(i, tq, tk, make, carry):
    if (tq // tk) % SB_GROUP == 0:
        return make(SB_GROUP)(0, carry)
    assert 2 * (tq // tk) == SB_GROUP
    half = lax.rem(i, 2) == 0
    return lax.cond(half, lambda cr: make(SB_GROUP // 2)(0, cr), lambda cr: make(SB_GROUP)(0, cr), carry)


def _sb_valid(i, j, tq, tk):
    row, col = _tile_pos(i, j, tq, tk)
    return col < row


def _sb_fwd(q_sb, proj, tq, tk, riders=None):
    s = q_sb.shape[0]
    k_off = (2 * RET_QK + 2 * RET_V + SB_W) // LANES
    n_pair = HEADS // 2
    _check_tiles(s, tq, tk, SB_GROUP)

    def body(q_ref, k_ref, v_ref, o_ref, a_ref):
        i = pl.program_id(1)
        upper = _tri(tk, True)
        qs = _stack_heads(q_ref[...])
        n_full, n_groups = _n_full(i, tq, tk, SB_GROUP), _n_groups(i, tq, tk, SB_GROUP)

        def make_step(near_diagonal, last, n_sub=SB_GROUP):
            def step(n, carry):
                c, o = carry
                g = last - 1 - n
                js = [g * SB_GROUP + sub for sub in range(n_sub)]
                zs = [_dot(qs, k_ref[_key_rows(j, tk), :], _NT) for j in js]
                log1ps = [jnp.log2(1.0 + jnp.exp2(-jnp.abs(z))) for z in zs]
                log_1ms = [-jnp.maximum(z, 0.0) - t for z, t in zip(zs, log1ps)]
                log_bs = [jnp.minimum(z, 0.0) - t for z, t in zip(zs, log1ps)]
                if near_diagonal:
                    valids = [_sb_valid(i, j, tq, tk) for j in js]
                    log_1ms = [jnp.where(v, l, 0.0) for v, l in zip(valids, log_1ms)]
                sticks = [_dot(l, upper) for l in log_1ms]
                sums = [jnp.sum(l, axis=1, keepdims=True) for l in log_1ms]
                cs = [None] * n_sub
                for sub in reversed(range(n_sub)):
                    cs[sub] = c
                    c = c + sums[sub]
                for sub, j in enumerate(js):
                    a = jnp.exp2(log_bs[sub] + sticks[sub] + cs[sub])
                    if near_diagonal:
                        a = jnp.where(valids[sub], a, 0.0)
                    a = a.astype(BF16)
                    a_ref[j] = a
                    o = o + _dot(_side_by_side(a, tq), _stack_heads(v_ref[_key_rows(j, tk), :]))
                return c, o
            return step

        carry = (jnp.zeros((2 * tq, 1), F32), jnp.zeros((tq, LANES), F32))
        carry = _diagonal_step(i, tq, tk, lambda n_sub: make_step(True, n_groups, n_sub), carry)
        _, acc = lax.fori_loop(0, n_full, make_step(False, n_full), carry)
        o_ref[...] = acc.astype(BF16)

    n_kb = s // tk
    return _call("sb_fwd", body, (n_pair, s // tq),
                 [(q_sb, (tq, LANES), lambda hp, i: (i, hp)),
                  (proj, (s, LANES), lambda hp, i: (0, k_off + hp)),
                  (proj, (s, LANES), lambda hp, i: (0, k_off + n_pair + hp))],
                 [((s, SB_W), BF16, (tq, LANES), lambda hp, i: (i, hp)),
                  ((n_pair, s // tq, n_kb, 2 * tq, tk), BF16, (None, None, n_kb, 2 * tq, tk),
                   lambda hp, i: (hp, i, 0, 0, 0))], riders=riders)


def _merge_out(retg, sb, w_ret, w_sb_t, w_out, proj, x, mod, gp1, g2, tm):
    s, d = x.shape
    gw = min(512, d)
    n_g = d // gw
    ar_off = (2 * RET_QK + 2 * RET_V + 3 * SB_W) // gw

    def body(rg_ref, sb_ref, wr_ref, ws_ref, wo_ref, *refs):
        gate_refs, (x_ref, mod_ref, gp_ref, g2_ref, mix_ref, r_ref, s_ref, y_ref, hres_ref, h2_ref) = refs[:2 * n_g], refs[2 * n_g:]
        for rows in _pieces(tm):
            rr = _dot(rg_ref[rows, :], wr_ref[...])
            ss = _dot(sb_ref[rows, :], ws_ref[...], _NT)
            a_r = jnp.concatenate([g[rows, :] for g in gate_refs[:n_g]], axis=1).astype(F32)
            a_s = jnp.concatenate([g[rows, :] for g in gate_refs[n_g:]], axis=1).astype(F32)
            mixed = (_sigmoid(a_r) * rr + _sigmoid(a_s) * ss).astype(BF16)
            mix_ref[rows, :] = mixed
            r_ref[rows, :] = rr.astype(BF16)
            s_ref[rows, :] = ss.astype(BF16)
            y = _dot(mixed, wo_ref[...])
            y_ref[rows, :] = y
            ny, _ = _rms(y, d)
            hres = x_ref[rows, :] + mod_ref[:, 2 * d:3 * d] * (ny * gp_ref[...])
            hres_ref[rows, :] = hres
            n2, _ = _rms(hres, d)
            h2_ref[rows, :] = (n2 * g2_ref[...] * (1.0 + mod_ref[:, 4 * d:5 * d]) + mod_ref[:, 3 * d:4 * d]).astype(BF16)

    row = lambda i: (i, 0)
    fix = lambda i: (0, 0)
    tile_bf = ((s, d), BF16, (tm, d), row)
    tile_f = ((s, d), F32, (tm, d), row)
    return _call("merge_out", body, (s // tm,),
                 [(retg, (tm, RET_V), row), (sb, (tm, SB_W), row), (w_ret, (RET_V, d), fix), (w_sb_t, (d, SB_W), fix),
                  (w_out, (d, d), fix)]
                 + [(proj, (tm, gw), functools.partial(lambda i, k: (i, ar_off + k), k=k)) for k in range(2 * n_g)]
                 + [(x, (tm, d), row), (mod, (1, 6 * d), fix), (gp1, (1, d), fix), (g2, (1, d), fix)],
                 [tile_bf, tile_bf, tile_bf, tile_f, tile_f, tile_bf])


def _ff1(h2, w_ff1_t, tm, tn):
    s, f = h2.shape[0], w_ff1_t.shape[0]
    tm = min(tm, s)

    def body(a_ref, w_ref, u_ref, act_ref):
        u = _dot(a_ref[...], w_ref[...], _NT)
        r = jnp.maximum(u, 0.0)
        u_ref[...] = u.astype(BF16)
        act_ref[...] = (r * r).astype(BF16)

    d = h2.shape[1]
    return _call("ff1", body, (f // tn, s // tm),
                 [(h2, (tm, d), lambda j, i: (i, 0)), (w_ff1_t, (tn, d), lambda j, i: (j, 0))],
                 [((s, f), BF16, (tm, tn), lambda j, i: (i, j))] * 2)


def _ff2_loss(act, w_ff2, hres, target, mod, gp2, tm):
    s, d = hres.shape
    f = act.shape[1]

    def body(a_ref, w_ref, h_ref, t_ref, mod_ref, gp_ref, dout_ref, df_ref, loss_ref, dgt_ref, dgp_ref):
        _zero_at_start([loss_ref, dgt_ref, dgp_ref])
        gt, gp = mod_ref[:, 5 * d:6 * d], gp_ref[...]
        for rows in _pieces(tm):
            ff = _dot(a_ref[rows, :], w_ref[...])
            nf, rf = _rms(ff, d)
            out = h_ref[rows, :] + gt * (nf * gp)
            err = out - t_ref[rows, :]
            sq = jnp.sum(err * err, axis=1, keepdims=True)
            loss_ref[...] += jnp.sum(sq, axis=0, keepdims=True)
            dout = err * (1.0 / d)
            dout_ref[rows, :] = dout
            dgt_ref[...] += _colsum(dout * (nf * gp))
            dgp_ref[...] += _colsum(dout * gt * nf)
            df_ref[rows, :] = _rms_bwd(dout * gt * gp, nf, rf, d).astype(BF16)

    row = lambda i: (i, 0)
    fix = lambda i: (0, 0)
    return _call("ff2_loss", body, (s // tm,),
                 [(act, (tm, f), row), (w_ff2, (f, d), fix), (hres, (tm, d), row), (target, (tm, d), row),
                  (mod, (1, 6 * d), fix), (gp2, (1, d), fix)],
                 [((s, d), F32, (tm, d), row), ((s, d), BF16, (tm, d), row), ((1, 1), F32, (1, 1), fix),
                  ((1, d), F32, (1, d), fix), ((1, d), F32, (1, d), fix)])


def _ffn_bwd(df, w_ff2, u, act, h2, tn):
    s, d = df.shape
    f = w_ff2.shape[0]

    def body(df_ref, w_ref, u_ref, act_ref, h2_ref, du_ref, gw2_ref, gw1_ref):
        dfb = df_ref[...]
        du = (_dot(dfb, w_ref[...], _NT) * (2.0 * jnp.maximum(u_ref[...].astype(F32), 0.0))).astype(BF16)
        du_ref[...] = du
        gw2_ref[...] = _dot(act_ref[...], dfb, _TN).astype(BF16)
        gw1_ref[...] = _dot(h2_ref[...], du, _TN).astype(BF16)

    fix = lambda j: (0, 0)
    col = lambda j: (0, j)
    return _call("ffn_bwd", body, (f // tn,),
                 [(df, (s, d), fix), (w_ff2, (tn, d), lambda j: (j, 0)), (u, (s, tn), col), (act, (s, tn), col),
                  (h2, (s, d), fix)],
                 [((s, f), BF16, (s, tn), col), ((f, d), BF16, (tn, d), lambda j: (j, 0)),
                  ((f // tn, d, tn), BF16, (None, d, tn), lambda j: (j, 0, 0))])


def _ff1_bwd(du, w_ff1_t, hres, dout, y, mixed, mod, g2, gp1, tm, riders=None):
    s, d = hres.shape
    f = du.shape[1]

    def body(a_ref, w_ref, h_ref, do_ref, y_ref, mix_ref, mod_ref, g2_ref, gp_ref,
             dh_ref, dy_ref, dsh_ref, dsc_ref, dg2_ref, dgt_ref, dgp_ref, gwo_ref, acc_ref):
        _zero_at_start([dsh_ref, dsc_ref, dg2_ref, dgt_ref, dgp_ref, acc_ref])
        g2, sc2 = g2_ref[...], mod_ref[:, 4 * d:5 * d]
        gt, gp = mod_ref[:, 2 * d:3 * d], gp_ref[...]
        for rows in _pieces(tm):
            dh2 = _dot(a_ref[rows, :], w_ref[...])
            n2, r2 = _rms(h_ref[rows, :], d)
            dsh_ref[...] += _colsum(dh2)
            dsc_ref[...] += _colsum(dh2 * n2 * g2)
            dg2_ref[...] += _colsum(dh2 * n2 * (1.0 + sc2))
            dhres = do_ref[rows, :] + _rms_bwd(dh2 * g2 * (1.0 + sc2), n2, r2, d)
            dh_ref[rows, :] = dhres
            ny, ry = _rms(y_ref[rows, :], d)
            dgt_ref[...] += _colsum(dhres * (ny * gp))
            dgp_ref[...] += _colsum(dhres * gt * ny)
            dy_ref[rows, :] = _rms_bwd(dhres * gt * gp, ny, ry, d).astype(BF16)
        acc_ref[...] += _dot(mix_ref[...], dy_ref[...], _TN)

        @pl.when(pl.program_id(0) == s // tm - 1)
        def _():
            gwo_ref[...] = acc_ref[...].astype(BF16)

    row = lambda i: (i, 0)
    fix = lambda i: (0, 0)
    vec = ((1, d), F32, (1, d), fix)
    return _call("ff1_bwd", body, (s // tm,),
                 [(du, (tm, f), row), (w_ff1_t, (f, d), fix), (hres, (tm, d), row), (dout, (tm, d), row),
                  (y, (tm, d), row), (mixed, (tm, d), row), (mod, (1, 6 * d), fix), (g2, (1, d), fix), (gp1, (1, d), fix)],
                 [((s, d), F32, (tm, d), row), ((s, d), BF16, (tm, d), row), vec, vec, vec, vec, vec,
                  ((d, d), BF16, (d, d), fix)], scratch=[pltpu.VMEM((d, d), F32)], riders=riders)


def _out_bwd(dy, w_out, proj, r_bf, s_bf, tm, tn, riders=None):
    s, d = dy.shape
    ar_off = (2 * RET_QK + 2 * RET_V + 3 * SB_W) // tn
    as_off = ar_off + d // tn

    def body(a_ref, w_ref, ar_ref, as_ref, r_ref, s_ref, dr_ref, ds_ref, dar_ref, das_ref):
        dm = _dot(a_ref[...], w_ref[...], _NT)
        sr, ss = _sigmoid(ar_ref[...].astype(F32)), _sigmoid(as_ref[...].astype(F32))
        dr_ref[...] = (dm * sr).astype(BF16)
        ds_ref[...] = (dm * ss).astype(BF16)
        dar_ref[...] = (dm * r_ref[...].astype(F32) * sr * (1.0 - sr)).astype(BF16)
        das_ref[...] = (dm * s_ref[...].astype(F32) * ss * (1.0 - ss)).astype(BF16)

    tile = (tm, tn)
    here = lambda j, i: (i, j)
    return _call("out_bwd", body, (d // tn, s // tm),
                 [(dy, (tm, d), lambda j, i: (i, 0)), (w_out, (tn, d), lambda j, i: (j, 0)),
                  (proj, tile, lambda j, i: (i, ar_off + j)), (proj, tile, lambda j, i: (i, as_off + j)),
                  (r_bf, tile, here), (s_bf, tile, here)],
                 [((s, d), BF16, tile, here)] * 4, riders=riders)


def _branch_bwd(d_r, d_s, retg, sb, w_ret, w_sb_t, ret, proj, gn_g, riders=None):
    s, d = d_r.shape
    n_step = 4
    part_v, part_s, part_d = RET_V // n_step, SB_W // n_step, d // n_step
    per_dev = d // N_DEV
    n_blk = part_d // per_dev
    gate_off = (2 * RET_QK + RET_V) // part_v

    def body(dr_ref, ds_ref, rg_ref, sb_ref, wr_ref, ws_ref, r_ref, g_ref, w_ref,
             dsb_ref, gwr_ref, gws_ref, dg_ref, dret_ref, dw_ref):
        i = pl.program_id(0)
        dr, ds = dr_ref[...], ds_ref[...]
        dsb_ref[...] = _dot(ds, ws_ref[...]).astype(BF16)
        gwr_ref[...] = _dot(rg_ref[...], dr, _TN).astype(BF16)
        cols = pl.ds(pl.multiple_of(i * part_d, part_d), part_d)
        gws = _dot(sb_ref[...], ds_ref[:, cols], _TN).astype(BF16)
        for k in range(n_blk):
            gws_ref[k] = gws[:, k * per_dev:(k + 1) * per_dev]
        dretg = _dot(dr, wr_ref[...], _NT)
        for h in range(part_v // RET_DV):
            cols = slice(h * RET_DV, (h + 1) * RET_DV)
            o, g, w, d_o = r_ref[:, cols], g_ref[:, cols].astype(F32), w_ref[:, cols], dretg[:, cols]
            mu = jnp.sum(o, axis=1, keepdims=True) * (1.0 / RET_DV)
            xc = o - mu
            rstd = lax.rsqrt(jnp.sum(xc * xc, axis=1, keepdims=True) * (1.0 / RET_DV) + EPS)
            n = xc * rstd
            sg = _sigmoid(g)
            silu = g * sg
            dg_ref[:, cols] = (d_o * n * w * (sg * (1.0 + g * (1.0 - sg)))).astype(BF16)
            dw_ref[:, cols] = _colsum(d_o * silu * n)
            dn = d_o * silu * w
            m1 = jnp.sum(dn, axis=1, keepdims=True) * (1.0 / RET_DV)
            m2 = jnp.sum(dn * n, axis=1, keepdims=True) * (1.0 / RET_DV)
            dret_ref[:, cols] = (rstd * (dn - m1 - n * m2)).astype(BF16)

    fix = lambda i: (0, 0)
    col = lambda i: (0, i)
    return _call("branch_bwd", body, (n_step,),
                 [(d_r, (s, d), fix), (d_s, (s, d), fix), (retg, (s, part_v), col), (sb, (s, SB_W), fix),
                  (w_ret, (part_v, d), lambda i: (i, 0)), (w_sb_t, (d, part_s), col),
                  (ret, (s, part_v), col), (proj, (s, part_v), lambda i: (0, gate_off + i)), (gn_g, (1, part_v), col)],
                 [((s, SB_W), BF16, (s, part_s), col), ((RET_V, d), BF16, (part_v, d), lambda i: (i, 0)),
                  ((N_DEV, SB_W, per_dev), BF16, (n_blk, SB_W, per_dev), lambda i: (i, 0, 0)),
                  ((s, RET_V), BF16, (s, part_v), col), ((s, RET_V), BF16, (s, part_v), col),
                  ((1, RET_V), F32, (1, part_v), col)], riders=riders)


def _ret_bwd(qk_rot, proj, dret, log_gamma, t, riders=None):
    s = qk_rot.shape[0]
    n_pair = HEADS // 2
    pw = 2 * RET_DV
    wq, wv = RET_PAIRS * LANES, RET_PAIRS * pw
    n_blk = s // t
    pairs = range(RET_PAIRS)

    def load(q_ref, k_ref, v_ref, do_ref):
        return ([_lanes(q_ref, p, LANES) for p in pairs], [_lanes(k_ref, p, LANES) for p in pairs],
                [_lanes(v_ref, p, pw) for p in pairs], [_lanes(do_ref, p, pw) for p in pairs])

    def d_scores(lg_ref, hp, i, qb, kb, vb, dob):
        z, w = _ret_block(lg_ref, hp, i, t, qb, kb)
        dp = jnp.concatenate([_dot(dob[:, 0:RET_DV], vb[:, 0:RET_DV], _NT),
                              _dot(dob[:, RET_DV:pw], vb[:, RET_DV:pw], _NT)], axis=0)
        return (z * w).astype(BF16), (dp * w).astype(BF16)

    def up_body(lg_ref, q_ref, k_ref, v_ref, do_ref, dq_ref, state_ref):
        hg, i = pl.program_id(0), pl.program_id(1)

        @pl.when(i == 0)
        def _():
            state_ref[...] = jnp.zeros_like(state_ref)

        qbs, kbs, vbs, dobs = load(q_ref, k_ref, v_ref, do_ref)
        dss = [d_scores(lg_ref, hg * RET_PAIRS + p, i, qbs[p], kbs[p], vbs[p], dobs[p])[1] for p in pairs]
        for p in pairs:
            dq_ref[:, p * LANES:(p + 1) * LANES] = (_dot(_side_by_side(dss[p], t), _stack_heads(kbs[p]))
                                                    + _dot(dobs[p], state_ref[p], _NT)).astype(BF16)
        for p in pairs:
            state_ref[p] += _pair_mask() * _dot(kbs[p], vbs[p], _TN)

    def down_body(lg_ref, q_ref, k_ref, v_ref, do_ref, dk_ref, dv_ref, state_ref):
        hg, i = pl.program_id(0), n_blk - 1 - pl.program_id(1)

        @pl.when(pl.program_id(1) == 0)
        def _():
            state_ref[...] = jnp.zeros_like(state_ref)

        qbs, kbs, vbs, dobs = load(q_ref, k_ref, v_ref, do_ref)
        both = [d_scores(lg_ref, hg * RET_PAIRS + p, i, qbs[p], kbs[p], vbs[p], dobs[p]) for p in pairs]
        for p in pairs:
            pp, ds = both[p]
            later = state_ref[p]
            dv_ref[:, p * pw:(p + 1) * pw] = (jnp.concatenate(
                [_dot(pp[:t], dobs[p][:, 0:RET_DV], _TN), _dot(pp[t:], dobs[p][:, RET_DV:pw], _TN)],
                axis=1) + _dot(kbs[p], later)).astype(BF16)
            dk_ref[:, p * LANES:(p + 1) * LANES] = (_dot(ds, _stack_heads(qbs[p]), _TN)
                                                    + _dot(vbs[p], later, _NT)).astype(BF16)
        for p in pairs:
            state_ref[p] += _pair_mask() * _dot(qbs[p], dobs[p], _TN)

    n_grp = n_pair // RET_PAIRS

    def ins(order):
        return [(log_gamma, None, pltpu.SMEM),
                (qk_rot, (t, wq), lambda hg, i: (order(i), hg)),
                (qk_rot, (t, wq), lambda hg, i: (order(i), n_grp + hg)),
                (proj, (t, wv), lambda hg, i: (order(i), 2 * RET_QK // wv + hg)),
                (dret, (t, wv), lambda hg, i: (order(i), hg))]

    up = lambda i: i
    down = lambda i: n_blk - 1 - i
    scratch = [pltpu.VMEM((RET_PAIRS, LANES, pw), F32)]
    dq = _call("ret_bwd_q", up_body, (n_grp, n_blk), ins(up),
               [((s, RET_QK), BF16, (t, wq), lambda hg, i: (i, hg))], scratch=scratch)[0]
    dk, dv, *rest = _call("ret_bwd_kv", down_body, (n_grp, n_blk), ins(down),
                          [((s, RET_QK), BF16, (t, wq), lambda hg, i: (down(i), hg)),
                           ((s, RET_V), BF16, (t, wv), lambda hg, i: (down(i), hg))],
                          scratch=scratch, riders=riders)
    return [dq, dk, dv] + rest


def _sb_bwd(q_sb, proj, weights, do, tq, tk, riders=None):
    s = q_sb.shape[0]
    k_off = (2 * RET_QK + 2 * RET_V + SB_W) // LANES
    n_pair = HEADS // 2
    _check_tiles(s, tq, tk, SB_GROUP)

    def body(q_ref, k_ref, v_ref, a_ref, do_ref, dq_ref, dk_ref, dv_ref):
        i = pl.program_id(1)

        @pl.when(i == 0)
        def _():
            dk_ref[...] = jnp.zeros_like(dk_ref)
            dv_ref[...] = jnp.zeros_like(dv_ref)

        lower = _tri(tk, False)
        qs = _stack_heads(q_ref[...])
        dos = _stack_heads(do_ref[...].astype(BF16))

        def make_step(near_diagonal, n_sub=SB_GROUP):
            def step(g, carry):
                c_e, dq = carry
                js = [g * SB_GROUP + sub for sub in range(n_sub)]
                rows = [_key_rows(j, tk) for j in js]
                zs = [_dot(qs, k_ref[rw, :], _NT) for rw in rows]
                das = [_dot(dos, v_ref[rw, :], _NT) for rw in rows]
                avals = [a_ref[j] for j in js]
                for a, rw in zip(avals, rows):
                    dv_ref[rw, :] += _dot(a, dos, _TN)
                es = [a.astype(F32) * da for a, da in zip(avals, das)]
                prefixes = [_dot(e, lower) for e in es]
                betas = [1.0 / (1.0 + jnp.exp2(-z)) for z in zs]
                for sub in range(n_sub):
                    dz = es[sub] - (es[sub] + prefixes[sub] + c_e) * betas[sub]
                    if near_diagonal:
                        dz = jnp.where(_sb_valid(i, js[sub], tq, tk), dz, 0.0)
                    dz = dz.astype(BF16)
                    dk_ref[rows[sub], :] += _dot(dz, qs, _TN)
                    dq = dq + _dot(_side_by_side(dz, tq), _stack_heads(k_ref[rows[sub], :]))
                    c_e = c_e + jnp.sum(es[sub], axis=1, keepdims=True)
                return c_e, dq
            return step

        n_full = _n_full(i, tq, tk, SB_GROUP)
        carry = (jnp.zeros((2 * tq, 1), F32), jnp.zeros((tq, LANES), F32))
        carry = lax.fori_loop(0, n_full, make_step(False), carry)
        _, dq = _diagonal_step(i, tq, tk, lambda n_sub: (lambda n, cr: make_step(True, n_sub)(n_full, cr)), carry)
        dq_ref[...] = dq

    blk = lambda hp, i: (i, hp)
    n_kb = s // tk
    return _call("sb_bwd", body, (n_pair, s // tq),
                 [(q_sb, (tq, LANES), blk),
                  (proj, (s, LANES), lambda hp, i: (0, k_off + hp)),
                  (proj, (s, LANES), lambda hp, i: (0, k_off + n_pair + hp)),
                  (weights, (None, None, n_kb, 2 * tq, tk), lambda hp, i: (hp, i, 0, 0, 0)),
                  (do, (tq, LANES), blk)],
                 [((s, SB_W), F32, (tq, LANES), blk),
                  ((s, SB_W), F32, (s, LANES), lambda hp, i: (0, hp)),
                  ((s, SB_W), F32, (s, LANES), lambda hp, i: (0, hp))], riders=riders)


def _assemble_dproj(dq_r, dk_r, dv_r, dg_r, dq_s, dk_s, dv_s, da_r, da_s, cos, sin, idx_col, lg_lanes, tm, riders=None):
    s, d = da_r.shape
    width = 2 * RET_QK + 2 * RET_V + 3 * SB_W + 2 * d

    def body(dq_ref, dk_ref, dv_ref, dg_ref, dqs_ref, dks_ref, dvs_ref, dar_ref, das_ref, cos_ref, sin_ref,
             idx_ref, lg_ref, o_ref):
        lane = lax.broadcasted_iota(jnp.int32, (1, LANES), 1)
        first = jnp.bitwise_and(lane, RET_DQK - 1) < (RET_DQK // 2)
        cos, sin = cos_ref[...], sin_ref[...]
        idx = idx_ref[...]
        for src, base, sign, scale in ((dq_ref, 0, 1.0, 1.0), (dk_ref, RET_QK, -1.0, RET_DQK ** -0.5)):
            for g in range(RET_QK // LANES):
                v = src[:, g * LANES:(g + 1) * LANES].astype(F32) * (_decay_scale(lg_ref, idx, g, sign) * scale)
                sw = jnp.where(first, pltpu.roll(v, LANES - RET_DQK // 2, 1), pltpu.roll(v, RET_DQK // 2, 1))
                o_ref[:, base + g * LANES:base + (g + 1) * LANES] = (v * cos - sw * sin).astype(BF16)
        off = 2 * RET_QK
        o_ref[:, off:off + RET_V] = dv_ref[...].astype(BF16)
        off += RET_V
        o_ref[:, off:off + RET_V] = dg_ref[...]
        off += RET_V
        o_ref[:, off:off + SB_W] = (dqs_ref[...] * (SB_DH ** -0.5)).astype(BF16)
        off += SB_W
        o_ref[:, off:off + SB_W] = (dks_ref[...] * LN2).astype(BF16)
        off += SB_W
        o_ref[:, off:off + SB_W] = dvs_ref[...].astype(BF16)
        off += SB_W
        o_ref[:, off:off + d] = dar_ref[...]
        off += d
        o_ref[:, off:off + d] = das_ref[...]

    row = lambda i: (i, 0)
    ins = [(a, (tm, a.shape[1]), row) for a in (dq_r, dk_r, dv_r, dg_r, dq_s, dk_s, dv_s, da_r, da_s, cos, sin, idx_col)]
    ins.append((lg_lanes, (1, RET_QK), lambda i: (0, 0)))
    return _call("assemble_dproj", body, (s // tm,), ins, [((s, width), BF16, (tm, width), row)], riders=riders)


def _in_bwd(dproj, w_in_t, x, dhres, mod, g1, tm, riders=None):
    s, d = x.shape
    width = dproj.shape[1]

    def body(a_ref, w_hbm, x_ref, dh_ref, mod_ref, g_ref, dx_ref, dsh_ref, dsc_ref, dg_ref, w_ref, w_sem):
        _zero_at_start([dsh_ref, dsc_ref, dg_ref])

        @pl.when(pl.program_id(0) == 0)
        def _():
            fetch = pltpu.make_async_copy(w_hbm, w_ref, w_sem)
            fetch.start()
            fetch.wait()

        g1, sc1 = g_ref[...], mod_ref[:, d:2 * d]
        for rows in _pieces(tm):
            dh = _dot(a_ref[rows, :], w_ref[...])
            n1, r1 = _rms(x_ref[rows, :], d)
            dsh_ref[...] += _colsum(dh)
            dsc_ref[...] += _colsum(dh * n1 * g1)
            dg_ref[...] += _colsum(dh * n1 * (1.0 + sc1))
            dx_ref[rows, :] = dh_ref[rows, :] + _rms_bwd(dh * g1 * (1.0 + sc1), n1, r1, d)

    row = lambda i: (i, 0)
    fix = lambda i: (0, 0)
    vec = ((1, d), F32, (1, d), fix)
    return _call("in_bwd", body, (s // tm,),
                 [(dproj, (tm, width), row), (w_in_t, None, pl.ANY), (x, (tm, d), row), (dhres, (tm, d), row),
                  (mod, (1, 6 * d), fix), (g1, (1, d), fix)],
                 [((s, d), F32, (tm, d), row), vec, vec, vec],
                 scratch=[pltpu.VMEM((width, d), w_in_t.dtype), pltpu.SemaphoreType.DMA], riders=riders)


def _adamw(w, g, m, v):
    m = ADAM_B1 * m + (1.0 - ADAM_B1) * g
    v = ADAM_B2 * v + (1.0 - ADAM_B2) * (g * g)
    m_hat = m / (1.0 - ADAM_B1 ** ADAM_STEP)
    v_hat = v / (1.0 - ADAM_B2 ** ADAM_STEP)
    delta = -ADAM_LR * (m_hat / (jnp.sqrt(v_hat) + ADAM_EPS) + ADAM_WD * w)
    return delta, m, v


def _adam_reduce(name, sets, steps):
    n = len(sets)

    def body(*refs):
        for k in range(n):
            p_ref, w_ref, m_ref, v_ref = refs[4 * k:4 * k + 4]
            outs = refs[4 * n + 4 * k:4 * n + 4 * k + 4]
            g = p_ref[0].astype(F32)
            for j in range(1, p_ref.shape[0]):
                g = g + p_ref[j].astype(F32)
            for o_ref, val in zip(outs, (g,) + _adamw(w_ref[...], g, m_ref[...], v_ref[...])):
                o_ref[...] = val

    ins, outs = [], []
    row = lambda i: (i, 0)
    for parts, w, m, v in sets:
        rws, cls = w.shape
        tr = rws // steps
        assert tr * steps == rws and tr % 16 == 0
        ins += [(parts, (parts.shape[0], tr, cls), lambda i: (0, i, 0)), (w, (tr, cls), row), (m, (tr, cls), row),
                (v, (tr, cls), row)]
        outs += [((rws, cls), F32, (tr, cls), row)] * 4
    res = _call(name, body, (steps,), ins, outs)
    return [res[4 * k:4 * k + 4] for k in range(n)]


def _ada_bwd_adam(cs_t, dmod_cols, w, m, v, tr):
    d, nc = w.shape

    def body(c_ref, dm_ref, w_ref, m_ref, v_ref, g_out, d_out, m_out, v_out):
        g = c_ref[0] * dm_ref[0:1, :]
        for r in range(1, N_DEV):
            g = g + c_ref[r] * dm_ref[r:r + 1, :]
        delta, mn, vn = _adamw(w_ref[...], g, m_ref[...], v_ref[...])
        g_out[...] = g
        d_out[...] = delta
        m_out[...] = mn
        v_out[...] = vn

    row = lambda i: (i, 0)
    blk = (tr, nc)
    return _call("ada_bwd_adam", body, (d // tr,),
                 [(cs_t, (N_DEV, tr, 1), lambda i: (0, i, 0)), (dmod_cols, (N_DEV, nc), lambda i: (0, 0)),
                  (w, blk, row), (m, blk, row), (v, blk, row)],
                 [((d, nc), F32, blk, row)] * 4)


def _small_adam(parts, ws, ms, vs):
    n = len(ws)
    widths = [w.shape[1] for w in ws]
    total = parts.shape[1]
    assert sum(widths) + LANES == total

    def body(p_ref, *refs):
        w_refs, m_refs, v_refs = refs[:n], refs[n:2 * n], refs[2 * n:3 * n]
        outs = refs[3 * n:]
        g = p_ref[0:1, :]
        for k in range(1, N_DEV):
            g = g + p_ref[k:k + 1, :]
        off = 0
        for i, width in enumerate(widths):
            gi = g[:, off:off + width]
            delta, mn, vn = _adamw(w_refs[i][...], gi, m_refs[i][...], v_refs[i][...])
            for o_ref, val in zip(outs[4 * i:4 * i + 4], (gi, delta, mn, vn)):
                o_ref[...] = val
            off += width
        outs[4 * n][...] = g[:, off:off + LANES]

    fix = lambda i: (0, 0)
    vec = lambda a: (a, (1, a.shape[1]), fix)
    out_specs = [((1, width), F32, (1, width), fix) for width in widths for _ in range(4)]
    out_specs.append(((1, LANES), F32, (1, LANES), fix))
    res = _call("small_adam", body, (1,),
                [(parts, (N_DEV, total), fix)] + [vec(a) for a in list(ws) + list(ms) + list(vs)], out_specs)
    return [res[4 * i:4 * i + 4] for i in range(n)], res[4 * n]


def kernel(x, c, positions, ada_w, ada_b, pre_mix_g, post_mix_g, pre_ffn_g, post_ffn_g, w_in, ret_gn_g, w_ret_branch, w_sb_branch, w_out, w_ff1, w_ff2, loss_target, m_ada_w, m_ada_b, m_pre_mix_g, m_post_mix_g, m_pre_ffn_g, m_post_ffn_g, m_w_in, m_ret_gn_g, m_w_ret_branch, m_w_sb_branch, m_w_out, m_w_ff1, m_w_ff2, v_ada_w, v_ada_b, v_pre_mix_g, v_post_mix_g, v_pre_ffn_g, v_post_ffn_g, v_w_in, v_ret_gn_g, v_w_ret_branch, v_w_sb_branch, v_w_out, v_w_ff1, v_w_ff2):
    _, s, d = x.shape
    d_ff = w_ff1.shape[2] * N_DEV
    d_in = w_in.shape[2] * N_DEV
    me = 4 * lax.axis_index("x") + 2 * lax.axis_index("y") + lax.axis_index("c")
    x2, tgt = x[0], loss_target[0]

    core = lax.axis_index("c").astype(jnp.int32).reshape(1)
    bf = lambda w: w[0].astype(BF16)

    w_in_t, m_in_t, v_in_t = (jnp.swapaxes(a[0], 0, 1) for a in (w_in, m_w_in, v_w_in))

    c_all, g_in = _exchange("gather_in", [c, w_in_t.astype(BF16)], ["gather", "gather_chip"])
    c_all = c_all.reshape(N_DEV, d)

    n_ada = ada_w.shape[2]
    ada_b_cols = lax.dynamic_slice(ada_b, (0, me * n_ada), (1, n_ada))
    cs_all, mod_cols = _ada_fwd(c_all, ada_w[0], ada_b_cols)
    mod_all, g_in = _exchange("gather_mod", [mod_cols, g_in], ["gather", "forward"])
    mod = lax.dynamic_index_in_dim(mod_all, me, axis=1, keepdims=False).reshape(1, 6 * d)

    tm = min(256, s)
    h = _pre_norm(x2, pre_mix_g, mod, 2 * tm)[0]
    wt_in = g_in.reshape(d_in, d)
    bf_t = lambda w: jnp.swapaxes(w[0], 0, 1).astype(BF16)
    small_w = [bf(w_ret_branch), bf_t(w_sb_branch), bf(w_out)]
    proj, *small_w = _matmul("in_proj", h, wt_in, "nt", s, 512, BF16, riders=(small_w, ["gather_chip"] * 3))
    pos_col = positions.reshape(s, 1).astype(F32)
    freqs = ROPE_BASE ** (-jnp.arange(0, RET_DQK, 2, dtype=F32) / RET_DQK)
    inv_freq = jnp.tile(freqs, LANES // (RET_DQK // 2)).reshape(1, LANES)
    log_gamma_np = np.log1p(-(2.0 ** (-5.0 - np.arange(HEADS))))
    log_gamma = jnp.asarray(log_gamma_np, F32)
    lg_lanes = jnp.asarray(np.repeat(log_gamma_np, RET_DQK).reshape(1, RET_QK), F32)
    idx_col = (jnp.arange(s, dtype=F32) - (s // 2)).reshape(s, 1)
    qk_rot, q_sb, cos_t, sin_t = _prep(proj, pos_col, idx_col, inv_freq, lg_lanes, 2 * tm)
    tq, tk = min(256, s), min(128, s)
    tq_sb = min(SB_TQ, s)
    sb, sb_weights, *big_w = _sb_fwd(q_sb, proj, tq_sb, tk, riders=([bf(w_ff2), bf_t(w_ff1)], ["gather_chip"] * 2))
    ret, retg, g_ret, g_sb, g_out, g_ff2, g_ff1 = _ret_fwd(qk_rot, proj, ret_gn_g, log_gamma, tq,
                                                           riders=(small_w + big_w, ["forward"] * 5))
    wf_ret = g_ret.reshape(RET_V, d)
    wt_sb = g_sb.reshape(d, SB_W)
    wf_out = g_out.reshape(d, d)
    wt_ff1 = g_ff1.reshape(d_ff, d)
    wf_ff2 = g_ff2.reshape(d_ff, d)
    mixed, r_bf, s_bf, y, hres, h2 = _merge_out(retg, sb, wf_ret, wt_sb, wf_out, proj, x2, mod, post_mix_g, pre_ffn_g, tm)
    u, act = _ff1(h2, wt_ff1, s, 512)
    dout, df, loss_sum, d_gt2, d_gp2 = _ff2_loss(act, wf_ff2, hres, tgt, mod, post_ffn_g, tm)

    du, gw_ff2, gw_ff1 = _ffn_bwd(df, wf_ff2, u, act, h2, d_ff // N_DEV)
    gw_ff2 = gw_ff2.reshape(N_DEV, d_ff // N_DEV, d)
    dhres, dy, d_sh2, d_sc2, d_g2, d_gt1, d_gp1, gw_out, t_ff1, t_ff2 = _ff1_bwd(
        du, wt_ff1, hres, dout, y, mixed, mod, pre_ffn_g, post_mix_g, tm, riders=([gw_ff1, gw_ff2], ["pair"] * 2))
    gw_out = gw_out.reshape(N_DEV, d // N_DEV, d)
    s_ff1, s_ff2 = _pair_sum("pair_sum_ff", [(gw_ff1, t_ff1), (gw_ff2, t_ff2)], core)
    d_r, d_s, da_r, da_s = _out_bwd(dy, wf_out, proj, r_bf, s_bf, 2 * tm, min(512, d))
    dsb, gw_ret, gw_sb, dg_r, dret, d_gn, p_out = _branch_bwd(d_r, d_s, retg, sb, wf_ret, wt_sb, ret, proj, ret_gn_g,
                                                              riders=([gw_out], ["scatter"]))
    gw_ret = gw_ret.reshape(N_DEV, RET_V // N_DEV, d)
    dq_s, dk_s, dv_s, p_ff1, p_ff2 = _sb_bwd(q_sb, proj, sb_weights, dsb, tq_sb, tk,
                                             riders=([s_ff1, s_ff2], ["chip_scatter"] * 2))
    dq_r, dk_r, dv_r, p_sb = _ret_bwd(qk_rot, proj, dret, log_gamma, tq, riders=([gw_sb], ["scatter"]))
    dproj = _assemble_dproj(dq_r, dk_r, dv_r, dg_r, dq_s, dk_s, dv_s, da_r, da_s, cos_t, sin_t, idx_col, lg_lanes, tm)[0]
    gw_in, p_ret = _matmul("grad_w_in", dproj, h, "tn", 512, d, BF16, riders=([gw_ret], ["scatter"]))
    gw_in = gw_in.reshape(N_DEV, d_in // N_DEV, d)
    t_in = _exchange("pair_in", [gw_in], ["pair"])[0]
    s_in = _pair_sum("pair_sum_in", [(gw_in, t_in)], core)[0]
    grad_x, d_sh1, d_sc1, d_g1, p_in = _in_bwd(dproj, wt_in, x2, dhres, mod, pre_mix_g, tm,
                                               riders=([s_in], ["chip_scatter"]))
    loss_lanes = jnp.pad(loss_sum, ((0, 0), (0, LANES - 1)))
    small = jnp.concatenate([d_sh1, d_sc1, d_gt1, d_sh2, d_sc2, d_gt2, d_g1, d_gp1, d_g2, d_gp2, d_gn, loss_lanes], axis=1)
    small_all = _exchange("gather_small", [small], ["gather"])[0].reshape(N_DEV, small.shape[1])
    parts = [p_in, p_ret, p_sb, p_out, p_ff1, p_ff2]

    res = {}
    names = ["w_ret_branch", "w_sb_branch", "w_out", "w_ff1", "w_ff2"]
    ws = [w_ret_branch, w_sb_branch, w_out, w_ff1, w_ff2]
    ms = [m_w_ret_branch, m_w_sb_branch, m_w_out, m_w_ff1, m_w_ff2]
    vs = [v_w_ret_branch, v_w_sb_branch, v_w_out, v_w_ff1, v_w_ff2]
    sets = [(parts[0], w_in_t, m_in_t, v_in_t)] + [(p, w[0], m[0], v[0]) for p, w, m, v in zip(parts[1:], ws, ms, vs)]
    updated = _adam_reduce("adam_big", sets, 4)
    res["w_in"] = [jnp.swapaxes(o, 0, 1)[None] for o in updated[0]]
    for nm, outs4 in zip(names, updated[1:]):
        res[nm] = [o[None] for o in outs4]
    dmod_cols = lax.dynamic_slice(small_all, (0, me * n_ada), (N_DEV, n_ada))
    res["ada_w"] = [o[None] for o in _ada_bwd_adam(cs_all.reshape(N_DEV, d, 1), dmod_cols, ada_w[0], m_ada_w[0], v_ada_w[0], tm)]
    vec_names = ["ada_b", "pre_mix_g", "post_mix_g", "pre_ffn_g", "post_ffn_g", "ret_gn_g"]
    vec_res, loss_lanes = _small_adam(small_all,
                                      [ada_b, pre_mix_g, post_mix_g, pre_ffn_g, post_ffn_g, ret_gn_g],
                                      [m_ada_b, m_pre_mix_g, m_post_mix_g, m_pre_ffn_g, m_post_ffn_g, m_ret_gn_g],
                                      [v_ada_b, v_pre_mix_g, v_post_mix_g, v_pre_ffn_g, v_post_ffn_g, v_ret_gn_g])
    res.update(zip(vec_names, vec_res))
    loss = (0.5 / d) * loss_lanes[0, 0]
    order = ["ada_w", "ada_b", "pre_mix_g", "post_mix_g", "pre_ffn_g", "post_ffn_g", "w_in", "ret_gn_g",
             "w_ret_branch", "w_sb_branch", "w_out", "w_ff1", "w_ff2"]
    outs = [loss, grad_x[None]]
    for k in range(4):
        outs += [res[nm][k] for nm in order]
    return tuple(outs)
```

```python
import functools

import numpy as np
import jax
import jax.numpy as jnp
from jax import lax
from jax.experimental import pallas as pl
from jax.experimental.pallas import tpu as pltpu

F32 = jnp.float32
BF16 = jnp.bfloat16
N_DEV = 8
AXES = ("x", "y", "c")

EPS = 1e-6
CHUNK = 64
CHUNK_SHIFT = 6
HEADS = 8
RET_DQK = 64
RET_DV = 128
SB_DH = 64
RET_QK = HEADS * RET_DQK
RET_V = HEADS * RET_DV
SB_W = HEADS * SB_DH
ROPE_BASE = 10000.0
LANES = 128

ADAM_LR = 0.001
ADAM_B1 = 0.9
ADAM_B2 = 0.999
ADAM_EPS = 1e-08
ADAM_WD = 0.01
ADAM_STEP = 10

VMEM_LIMIT = 56 * 1024 * 1024

_NN = (((1,), (0,)), ((), ()))
_NT = (((1,), (1,)), ((), ()))
_TN = (((0,), (0,)), ((), ()))


def _dot(a, b, dims=_NN):
    if a.dtype != BF16:
        a = a.astype(BF16)
    if b.dtype != BF16:
        b = b.astype(BF16)
    return lax.dot_general(a, b, dims, preferred_element_type=F32)


def _sigmoid(x):
    return 1.0 / (1.0 + jnp.exp(-x))


def _rms(x, d):
    r = lax.rsqrt(jnp.sum(x * x, axis=1, keepdims=True) * (1.0 / d) + EPS)
    return x * r, r


def _rms_bwd(dn, n, r, d):
    return r * (dn - n * (jnp.sum(dn * n, axis=1, keepdims=True) * (1.0 / d)))


def _colsum(v):
    return jnp.sum(v, axis=0, keepdims=True)


ROW_SPLIT = 2


def _zero_at_start(refs):
    @pl.when(pl.program_id(0) == 0)
    def _():
        for r in refs:
            r[...] = jnp.zeros_like(r)


def _pieces(tm):
    step = tm // ROW_SPLIT
    return [slice(k * step, (k + 1) * step) for k in range(ROW_SPLIT)]


KIND_SLOTS = {"gather": N_DEV, "scatter": N_DEV, "gather_chip": N_DEV, "forward": N_DEV, "pair": N_DEV // 2,
              "chip_scatter": N_DEV // 2}
SEMS_PER_ARRAY = N_DEV - 1


def _exchange_copies(ins, outs, send_sems, recv_sems, local_sems, kinds):
    x, y, c = (lax.axis_index(a) for a in AXES)
    me, chip, sibling = 4 * x + 2 * y + c, 2 * x + y, (x, y, 1 - c)
    mesh_id = pl.DeviceIdType.MESH
    other_chips = []
    for k in range(1, N_DEV // 2):
        px = 1 - x if k & 2 else x
        py = 1 - y if k & 1 else y
        other_chips.append((px, py))
    copies = []
    for i, kind in enumerate(kinds):
        def remote(src, dst, k, to, i=i):
            return pltpu.make_async_remote_copy(
                src_ref=src, dst_ref=dst, send_sem=send_sems.at[i * SEMS_PER_ARRAY + k],
                recv_sem=recv_sems.at[i * SEMS_PER_ARRAY + k], device_id=to, device_id_type=mesh_id)

        if kind in ("gather", "scatter"):
            pick = (lambda ref, d: ref.at[d]) if kind == "scatter" else (lambda ref, d: ref)
            copies.append(pltpu.make_async_copy(pick(ins[i], me), outs[i].at[me], local_sems.at[i]))
            for k in range(1, N_DEV):
                to = (1 - x if k & 4 else x, 1 - y if k & 2 else y, 1 - c if k & 1 else c)
                copies.append(remote(pick(ins[i], 4 * to[0] + 2 * to[1] + to[2]), outs[i].at[me], k - 1, to))
        elif kind == "gather_chip":
            copies.append(pltpu.make_async_copy(ins[i], outs[i].at[me], local_sems.at[i]))
            copies.append(remote(ins[i], outs[i].at[me], 0, sibling))
            for k, (px, py) in enumerate(other_chips):
                copies.append(remote(ins[i], outs[i].at[me], 1 + k, (px, py, c)))
        elif kind == "forward":
            for k, (px, py) in enumerate(other_chips):
                slot = 4 * px + 2 * py + c
                copies.append(remote(outs[i].at[slot], outs[i].at[slot], k, sibling))
        elif kind == "pair":
            for k in range(N_DEV // 2):
                copies.append(remote(ins[i].at[2 * k + 1 - c], outs[i].at[k], k, sibling))
        elif kind == "chip_scatter":
            copies.append(pltpu.make_async_copy(ins[i].at[chip], outs[i].at[chip], local_sems.at[i]))
            for k, (px, py) in enumerate(other_chips):
                copies.append(remote(ins[i].at[2 * px + py], outs[i].at[chip], k, (px, py, c)))
        else:
            raise ValueError(kind)
    return copies


def _exchange_shapes(arrays, kinds):
    shapes = []
    for a, kind in zip(arrays, kinds):
        tail = a.shape if kind in ("gather", "gather_chip") else a.shape[1:]
        shapes.append(jax.ShapeDtypeStruct((KIND_SLOTS[kind],) + tuple(tail), a.dtype))
    return shapes


def _exchange_sems(n):
    return [pltpu.SemaphoreType.DMA((n * SEMS_PER_ARRAY,)), pltpu.SemaphoreType.DMA((n * SEMS_PER_ARRAY,)),
            pltpu.SemaphoreType.DMA((n,))]


def _call(name, body, grid, ins, outs, scratch=(), riders=None, prefetch=None):
    any_spec = pl.BlockSpec(memory_space=pl.ANY)
    in_specs = [pl.BlockSpec(memory_space=im) if bs is None else pl.BlockSpec(bs, im) for _, bs, im in ins]
    out_specs = [pl.BlockSpec(bs, im) for _, _, bs, im in outs]
    out_shape = [jax.ShapeDtypeStruct(s, d) for s, d, _, _ in outs]
    operands = [a for a, _, _ in ins]
    scratch = list(scratch)
    aliases = {}
    n_pre = 0 if prefetch is None else 1
    kernel = functools.partial(body) if prefetch is None else (lambda _, *refs: body(*refs))
    if riders is not None:
        arrays, kinds = riders
        nr, n_in, n_out, n_scr = len(arrays), len(ins), len(outs), len(scratch)

        def kernel(*refs):
            refs = refs[n_pre:]
            own_in, ride_in = refs[:n_in], refs[n_in:n_in + nr]
            own_out = refs[n_in + nr:n_in + nr + n_out]
            ride_out = refs[n_in + nr + n_out:n_in + 2 * nr + n_out]
            own_scr = refs[n_in + 2 * nr + n_out:n_in + 2 * nr + n_out + n_scr]
            sems = refs[n_in + 2 * nr + n_out + n_scr:]
            ids = [pl.program_id(a) for a in range(len(grid))]
            first = functools.reduce(jnp.logical_and, [i == 0 for i in ids])
            last = functools.reduce(jnp.logical_and, [i == g - 1 for i, g in zip(ids, grid)])

            @pl.when(first)
            def _():
                for cp in _exchange_copies(ride_in, ride_out, *sems, kinds):
                    cp.start()

            body(*own_in, *own_out, *own_scr)

            @pl.when(last)
            def _():
                for cp in _exchange_copies(ride_in, ride_out, *sems, kinds):
                    cp.wait()

        in_specs += [any_spec] * nr
        out_specs += [any_spec] * nr
        out_shape += _exchange_shapes(arrays, kinds)
        operands += list(arrays)
        scratch += _exchange_sems(nr)
        aliases = {n_pre + n_in + r: n_out + r for r, kind in enumerate(kinds) if kind == "forward"}
    params = pltpu.CompilerParams(dimension_semantics=("arbitrary",) * len(grid), vmem_limit_bytes=VMEM_LIMIT)
    if prefetch is None:
        return pl.pallas_call(kernel, name=name, grid=grid, in_specs=in_specs, out_specs=out_specs,
                              out_shape=out_shape, scratch_shapes=scratch, input_output_aliases=aliases,
                              compiler_params=params)(*operands)
    grid_spec = pltpu.PrefetchScalarGridSpec(num_scalar_prefetch=1, grid=grid, in_specs=in_specs,
                                             out_specs=out_specs, scratch_shapes=scratch)
    return pl.pallas_call(kernel, name=name, grid_spec=grid_spec, out_shape=out_shape,
                          input_output_aliases=aliases, compiler_params=params)(prefetch, *operands)


def _exchange(name, arrays, kinds):
    n = len(arrays)

    def body(*refs):
        copies = _exchange_copies(refs[:n], refs[n:2 * n], *refs[2 * n:], kinds)
        for cp in copies:
            cp.start()
        for cp in copies:
            cp.wait()

    any_spec = pl.BlockSpec(memory_space=pl.ANY)
    return pl.pallas_call(
        functools.partial(body),
        name=name,
        in_specs=[any_spec] * n,
        out_specs=[any_spec] * n,
        out_shape=_exchange_shapes(arrays, kinds),
        scratch_shapes=_exchange_sems(n),
        input_output_aliases={i: i for i, kind in enumerate(kinds) if kind == "forward"},
    )(*arrays)


def _pair_sum(name, pairs, my_core):
    n = len(pairs)

    def body(*refs):
        for k in range(n):
            a_ref, b_ref, o_ref = refs[2 * k], refs[2 * k + 1], refs[2 * n + k]
            o_ref[...] = (a_ref[...].astype(F32) + b_ref[...].astype(F32)).astype(o_ref.dtype)

    ins, outs = [], []
    for mine, theirs in pairs:
        _, rws, cls = mine.shape
        ins += [(mine, (None, rws, cls), lambda k, core: (2 * k + core[0], 0, 0)),
                (theirs, (None, rws, cls), lambda k, core: (k, 0, 0))]
        outs.append(((N_DEV // 2, rws, cls), mine.dtype, (None, rws, cls), lambda k, core: (k, 0, 0)))
    return _call(name, body, (N_DEV // 2,), ins, outs, prefetch=my_core)


def _matmul(name, a, b, kind, tm, tn, out_dtype, blocked_out=False, riders=None):
    if kind == "tn":
        kdim, m = a.shape
    else:
        m, kdim = a.shape
    n = b.shape[0] if kind == "nt" else b.shape[1]
    tm, tn = min(tm, m), min(tn, n)
    dims = {"nn": _NN, "nt": _NT, "tn": _TN}[kind]

    def body(a_ref, b_ref, o_ref):
        o_ref[...] = _dot(a_ref[...], b_ref[...], dims).astype(o_ref.dtype)

    a_spec = (a, (kdim, tm), lambda j, i: (0, i)) if kind == "tn" else (a, (tm, kdim), lambda j, i: (i, 0))
    b_spec = (b, (tn, kdim), lambda j, i: (j, 0)) if kind == "nt" else (b, (kdim, tn), lambda j, i: (0, j))
    if blocked_out:
        out = ((n // tn, m, tn), out_dtype, (None, tm, tn), lambda j, i: (j, i, 0))
    else:
        out = ((m, n), out_dtype, (tm, tn), lambda j, i: (i, j))
    res = _call(name, body, (n // tn, m // tm), [a_spec, b_spec], [out], riders=riders)
    return res[0] if riders is None else res


def _ada_fwd(c_all, ada_w, ada_b_cols):
    def body(c_ref, w_ref, b_ref, cs_ref, o_ref):
        v = c_ref[...]
        cs = v * _sigmoid(v)
        cs_ref[...] = cs
        o_ref[...] = lax.dot_general(cs, w_ref[...], _NN, preferred_element_type=F32,
                                     precision=lax.Precision.HIGHEST) + b_ref[...]

    r, d = c_all.shape
    nc = ada_w.shape[1]
    fix = lambda i: (0, 0)
    return _call("ada_fwd", body, (1,),
                 [(c_all, (r, d), fix), (ada_w, (d, nc), fix), (ada_b_cols, (1, nc), fix)],
                 [((r, d), F32, (r, d), fix), ((r, nc), F32, (r, nc), fix)])


def _pre_norm(x, g, mod, tm, riders=None):
    s, d = x.shape

    def body(x_ref, g_ref, mod_ref, h_ref):
        n, _ = _rms(x_ref[...], d)
        sh, sc = mod_ref[:, 0:d], mod_ref[:, d:2 * d]
        h_ref[...] = (n * g_ref[...] * (1.0 + sc) + sh).astype(BF16)

    return _call("pre_norm", body, (s // tm,),
                 [(x, (tm, d), lambda i: (i, 0)), (g, (1, d), lambda i: (0, 0)),
                  (mod, (1, 6 * d), lambda i: (0, 0))],
                 [((s, d), BF16, (tm, d), lambda i: (i, 0))], riders=riders)


LOG2E = 1.4426950408889634
LN2 = 0.6931471805599453


def _decay_scale(lg_ref, idx, g, sign):
    return jnp.exp((sign * idx) * lg_ref[:, g * LANES:(g + 1) * LANES])


def _prep(proj, pos_col, idx_col, inv_freq, lg_lanes, tm):
    s = proj.shape[0]
    sb_off = (2 * RET_QK + 2 * RET_V) // SB_W
    n_q = RET_QK // LANES

    def body(qk_ref, qs_ref, pos_ref, idx_ref, f_ref, lg_ref, qk_out, qs_out, cos_out, sin_out):
        ang = pos_ref[...] * f_ref[...]
        lane = lax.broadcasted_iota(jnp.int32, (1, LANES), 1)
        first = jnp.bitwise_and(lane, RET_DQK - 1) < (RET_DQK // 2)
        cos = jnp.cos(ang)
        sin = jnp.where(first, -1.0, 1.0) * jnp.sin(ang)
        cos_out[...] = cos
        sin_out[...] = sin
        idx = idx_ref[...]
        for g in range(2 * n_q):
            v = qk_ref[:, g * LANES:(g + 1) * LANES].astype(F32)
            sw = jnp.where(first, pltpu.roll(v, LANES - RET_DQK // 2, 1), pltpu.roll(v, RET_DQK // 2, 1))
            r = v * cos + sw * sin
            if g < n_q:
                r = r * _decay_scale(lg_ref, idx, g, 1.0)
            else:
                r = r * (_decay_scale(lg_ref, idx, g - n_q, -1.0) * (RET_DQK ** -0.5))
            qk_out[:, g * LANES:(g + 1) * LANES] = r.astype(BF16)
        qs_out[...] = (qs_ref[...].astype(F32) * (SB_DH ** -0.5 * LOG2E)).astype(BF16)

    row = lambda i: (i, 0)
    return _call("prep", body, (s // tm,),
                 [(proj, (tm, 2 * RET_QK), row),
                  (proj, (tm, SB_W), lambda i: (i, sb_off)),
                  (pos_col, (tm, 1), row),
                  (idx_col, (tm, 1), row),
                  (inv_freq, (1, LANES), lambda i: (0, 0)),
                  (lg_lanes, (1, RET_QK), lambda i: (0, 0))],
                 [((s, 2 * RET_QK), BF16, (tm, 2 * RET_QK), row),
                  ((s, SB_W), BF16, (tm, SB_W), row),
                  ((s, LANES), F32, (tm, LANES), row),
                  ((s, LANES), F32, (tm, LANES), row)])


def _head_mask(hh):
    lane = lax.broadcasted_iota(jnp.int32, (1, LANES), 1)
    return (lane >= RET_DQK) if hh else (lane < RET_DQK)


def _masked(v, m):
    return jnp.where(m, v, jnp.zeros_like(v))


SB_GROUP = 4
SB_TQ = 256


def _stack_heads(v):
    return jnp.concatenate([_masked(v, _head_mask(0)), _masked(v, _head_mask(1))], axis=0)


def _side_by_side(v, t):
    return jnp.concatenate([v[:t], v[t:]], axis=1)


def _tile_pos(i, j, tq, tk):
    row = jnp.bitwise_and(lax.broadcasted_iota(jnp.int32, (2 * tq, tk), 0), tq - 1) + i * tq
    col = lax.broadcasted_iota(jnp.int32, (2 * tq, tk), 1) + j * tk
    return row, col


def _n_groups(i, tq, tk, grp):
    return ((i + 1) * (tq // tk) + grp - 1) // grp


def _n_full(i, tq, tk, grp):
    return (i * (tq // tk)) // grp


def _key_rows(j, tk):
    return pl.ds(pl.multiple_of(j * tk, tk), tk)


def _ret_weight(lg_rows, i, j, tq, tk):
    row, col = _tile_pos(i, j, tq, tk)
    same = jnp.right_shift(col, CHUNK_SHIFT) == jnp.right_shift(row, CHUNK_SHIFT)
    later = jnp.where(same, jnp.exp((2.0 * lg_rows) * (col - row).astype(F32)), 0.0)
    return jnp.where(col <= row, 1.0, later)


def _lg_rows(lg_ref, hp, tq):
    first = lax.broadcasted_iota(jnp.int32, (2 * tq, 1), 0) < tq
    return jnp.where(first, lg_ref[2 * hp], lg_ref[2 * hp + 1])


def _check_tiles(s, tq, tk, grp):
    assert tq % tk == 0 and tq & (tq - 1) == 0 and tk & (tk - 1) == 0
    assert s % tq == 0 and (s // tk) % grp == 0 and s // tk <= LANES


def _pair_mask():
    r = lax.broadcasted_iota(jnp.int32, (LANES, 2 * RET_DV), 0) >= RET_DQK
    c = lax.broadcasted_iota(jnp.int32, (LANES, 2 * RET_DV), 1) >= RET_DV
    return (r == c).astype(F32)


def _ret_block(lg_ref, hp, i, t, qb, kb):
    w = _ret_weight(_lg_rows(lg_ref, hp, t), i, i, t, t)
    return _dot(_stack_heads(qb), kb, _NT), w


RET_PAIRS = 4


def _lanes(ref, p, width):
    return ref[:, p * width:(p + 1) * width]


def _ret_fwd(qk_rot, proj, gn_g, log_gamma, t, riders=None):
    s = qk_rot.shape[0]
    n_pair = HEADS // 2
    pw = 2 * RET_DV
    wq, wv = RET_PAIRS * LANES, RET_PAIRS * pw
    v_off, gate_off = 2 * RET_QK // wv, (2 * RET_QK + RET_V) // wv
    assert s % t == 0 and t % CHUNK == 0 and t & (t - 1) == 0 and n_pair % RET_PAIRS == 0

    def body(lg_ref, q_ref, k_ref, v_ref, g_ref, w_ref, ret_ref, rg_ref, state_ref):
        hg, i = pl.program_id(0), pl.program_id(1)

        @pl.when(i == 0)
        def _():
            state_ref[...] = jnp.zeros_like(state_ref)

        pairs = range(RET_PAIRS)
        qbs = [_lanes(q_ref, p, LANES) for p in pairs]
        kbs = [_lanes(k_ref, p, LANES) for p in pairs]
        vbs = [_lanes(v_ref, p, pw) for p in pairs]
        zws = [_ret_block(lg_ref, hg * RET_PAIRS + p, i, t, qbs[p], kbs[p]) for p in pairs]
        ps = [(z * w).astype(BF16) for z, w in zws]
        outs = [jnp.concatenate([_dot(ps[p][:t], vbs[p][:, 0:RET_DV]), _dot(ps[p][t:], vbs[p][:, RET_DV:pw])], axis=1)
                + _dot(qbs[p], state_ref[p]) for p in pairs]
        for p in pairs:
            state_ref[p] += _pair_mask() * _dot(kbs[p], vbs[p], _TN)
        for p in pairs:
            for hh in range(2):
                cols = slice(p * pw + hh * RET_DV, p * pw + (hh + 1) * RET_DV)
                o = outs[p][:, hh * RET_DV:(hh + 1) * RET_DV]
                ret_ref[:, cols] = o
                mu = jnp.sum(o, axis=1, keepdims=True) * (1.0 / RET_DV)
                xc = o - mu
                var = jnp.sum(xc * xc, axis=1, keepdims=True) * (1.0 / RET_DV)
                nrm = xc * lax.rsqrt(var + EPS) * w_ref[:, cols]
                g = g_ref[:, cols].astype(F32)
                rg_ref[:, cols] = (g * _sigmoid(g) * nrm).astype(BF16)

    blk = lambda hg, i: (i, hg)
    return _call("ret_fwd", body, (n_pair // RET_PAIRS, s // t),
                 [(log_gamma, None, pltpu.SMEM),
                  (qk_rot, (t, wq), blk),
                  (qk_rot, (t, wq), lambda hg, i: (i, n_pair // RET_PAIRS + hg)),
                  (proj, (t, wv), lambda hg, i: (i, v_off + hg)),
                  (proj, (t, wv), lambda hg, i: (i, gate_off + hg)),
                  (gn_g, (1, wv), lambda hg, i: (0, hg))],
                 [((s, RET_V), F32, (t, wv), blk), ((s, RET_V), BF16, (t, wv), blk)],
                 scratch=[pltpu.VMEM((RET_PAIRS, LANES, pw), F32)], riders=riders)


def _tri(tk, strict_upper):
    r = lax.broadcasted_iota(jnp.int32, (tk, tk), 0)
    cc = lax.broadcasted_iota(jnp.int32, (tk, tk), 1)
    return ((r > cc) if strict_upper else (r < cc)).astype(BF16)


def _diagonal_step(i, tq, tk, make, carry):
    if (tq // tk) % SB_GROUP == 0:
        return make(SB_GROUP)(0, carry)
    assert 2 * (tq // tk) == SB_GROUP
    half = lax.rem(i, 2) == 0
    return lax.cond(half, lambda cr: make(SB_GROUP // 2)(0, cr), lambda cr: make(SB_GROUP)(0, cr), carry)


def _sb_valid(i, j, tq, tk):
    row, col = _tile_pos(i, j, tq, tk)
    return col < row


def _sb_fwd(q_sb, proj, tq, tk, riders=None):
    s = q_sb.shape[0]
    k_off = (2 * RET_QK + 2 * RET_V + SB_W) // LANES
    n_pair = HEADS // 2
    _check_tiles(s, tq, tk, SB_GROUP)

    def body(q_ref, k_ref, v_ref, o_ref, a_ref):
        i = pl.program_id(1)
        upper = _tri(tk, True)
        qs = _stack_heads(q_ref[...])
        n_full, n_groups = _n_full(i, tq, tk, SB_GROUP), _n_groups(i, tq, tk, SB_GROUP)

        def make_step(near_diagonal, last, n_sub=SB_GROUP):
            def step(n, carry):
                c, o = carry
                g = last - 1 - n
                js = [g * SB_GROUP + sub for sub in range(n_sub)]
                zs = [_dot(qs, k_ref[_key_rows(j, tk), :], _NT) for j in js]
                log1ps = [jnp.log2(1.0 + jnp.exp2(-jnp.abs(z))) for z in zs]
                log_1ms = [-jnp.maximum(z, 0.0) - t for z, t in zip(zs, log1ps)]
                log_bs = [jnp.minimum(z, 0.0) - t for z, t in zip(zs, log1ps)]
                if near_diagonal:
                    valids = [_sb_valid(i, j, tq, tk) for j in js]
                    log_1ms = [jnp.where(v, l, 0.0) for v, l in zip(valids, log_1ms)]
                sticks = [_dot(l, upper) for l in log_1ms]
                sums = [jnp.sum(l, axis=1, keepdims=True) for l in log_1ms]
                cs = [None] * n_sub
                for sub in reversed(range(n_sub)):
                    cs[sub] = c
                    c = c + sums[sub]
                for sub, j in enumerate(js):
                    a = jnp.exp2(log_bs[sub] + sticks[sub] + cs[sub])
                    if near_diagonal:
                        a = jnp.where(valids[sub], a, 0.0)
                    a = a.astype(BF16)
                    a_ref[j] = a
                    o = o + _dot(_side_by_side(a, tq), _stack_heads(v_ref[_key_rows(j, tk), :]))
                return c, o
            return step

        carry = (jnp.zeros((2 * tq, 1), F32), jnp.zeros((tq, LANES), F32))
        carry = _diagonal_step(i, tq, tk, lambda n_sub: make_step(True, n_groups, n_sub), carry)
        _, acc = lax.fori_loop(0, n_full, make_step(False, n_full), carry)
        o_ref[...] = acc.astype(BF16)

    n_kb = s // tk
    return _call("sb_fwd", body, (n_pair, s // tq),
                 [(q_sb, (tq, LANES), lambda hp, i: (i, hp)),
                  (proj, (s, LANES), lambda hp, i: (0, k_off + hp)),
                  (proj, (s, LANES), lambda hp, i: (0, k_off + n_pair + hp))],
                 [((s, SB_W), BF16, (tq, LANES), lambda hp, i: (i, hp)),
                  ((n_pair, s // tq, n_kb, 2 * tq, tk), BF16, (None, None, n_kb, 2 * tq, tk),
                   lambda hp, i: (hp, i, 0, 0, 0))], riders=riders)


def _merge_out(retg, sb, w_ret, w_sb_t, w_out, proj, x, mod, gp1, g2, tm):
    s, d = x.shape
    gw = min(512, d)
    n_g = d // gw
    ar_off = (2 * RET_QK + 2 * RET_V + 3 * SB_W) // gw

    def body(rg_ref, sb_ref, wr_ref, ws_ref, wo_ref, *refs):
        gate_refs, (x_ref, mod_ref, gp_ref, g2_ref, mix_ref, r_ref, s_ref, y_ref, hres_ref, h2_ref) = refs[:2 * n_g], refs[2 * n_g:]
        for rows in _pieces(tm):
            rr = _dot(rg_ref[rows, :], wr_ref[...])
            ss = _dot(sb_ref[rows, :], ws_ref[...], _NT)
            a_r = jnp.concatenate([g[rows, :] for g in gate_refs[:n_g]], axis=1).astype(F32)
            a_s = jnp.concatenate([g[rows, :] for g in gate_refs[n_g:]], axis=1).astype(F32)
            mixed = (_sigmoid(a_r) * rr + _sigmoid(a_s) * ss).astype(BF16)
            mix_ref[rows, :] = mixed
            r_ref[rows, :] = rr.astype(BF16)
            s_ref[rows, :] = ss.astype(BF16)
            y = _dot(mixed, wo_ref[...])
            y_ref[rows, :] = y
            ny, _ = _rms(y, d)
            hres = x_ref[rows, :] + mod_ref[:, 2 * d:3 * d] * (ny * gp_ref[...])
            hres_ref[rows, :] = hres
            n2, _ = _rms(hres, d)
            h2_ref[rows, :] = (n2 * g2_ref[...] * (1.0 + mod_ref[:, 4 * d:5 * d]) + mod_ref[:, 3 * d:4 * d]).astype(BF16)

    row = lambda i: (i, 0)
    fix = lambda i: (0, 0)
    tile_bf = ((s, d), BF16, (tm, d), row)
    tile_f = ((s, d), F32, (tm, d), row)
    return _call("merge_out", body, (s // tm,),
                 [(retg, (tm, RET_V), row), (sb, (tm, SB_W), row), (w_ret, (RET_V, d), fix), (w_sb_t, (d, SB_W), fix),
                  (w_out, (d, d), fix)]
                 + [(proj, (tm, gw), functools.partial(lambda i, k: (i, ar_off + k), k=k)) for k in range(2 * n_g)]
                 + [(x, (tm, d), row), (mod, (1, 6 * d), fix), (gp1, (1, d), fix), (g2, (1, d), fix)],
                 [tile_bf, tile_bf, tile_bf, tile_f, tile_f, tile_bf])


def _ff1(h2, w_ff1_t, tm, tn):
    s, f = h2.shape[0], w_ff1_t.shape[0]
    tm = min(tm, s)

    def body(a_ref, w_ref, u_ref, act_ref):
        u = _dot(a_ref[...], w_ref[...], _NT)
        r = jnp.maximum(u, 0.0)
        u_ref[...] = u.astype(BF16)
        act_ref[...] = (r * r).astype(BF16)

    d = h2.shape[1]
    return _call("ff1", body, (f // tn, s // tm),
                 [(h2, (tm, d), lambda j, i: (i, 0)), (w_ff1_t, (tn, d), lambda j, i: (j, 0))],
                 [((s, f), BF16, (tm, tn), lambda j, i: (i, j))] * 2)


def _ff2_loss(act, w_ff2, hres, target, mod, gp2, tm):
    s, d = hres.shape
    f = act.shape[1]

    def body(a_ref, w_ref, h_ref, t_ref, mod_ref, gp_ref, dout_ref, df_ref, loss_ref, dgt_ref, dgp_ref):
        _zero_at_start([loss_ref, dgt_ref, dgp_ref])
        gt, gp = mod_ref[:, 5 * d:6 * d], gp_ref[...]
        for rows in _pieces(tm):
            ff = _dot(a_ref[rows, :], w_ref[...])
            nf, rf = _rms(ff, d)
            out = h_ref[rows, :] + gt * (nf * gp)
            err = out - t_ref[rows, :]
            sq = jnp.sum(err * err, axis=1, keepdims=True)
            loss_ref[...] += jnp.sum(sq, axis=0, keepdims=True)
            dout = err * (1.0 / d)
            dout_ref[rows, :] = dout
            dgt_ref[...] += _colsum(dout * (nf * gp))
            dgp_ref[...] += _colsum(dout * gt * nf)
            df_ref[rows, :] = _rms_bwd(dout * gt * gp, nf, rf, d).astype(BF16)

    row = lambda i: (i, 0)
    fix = lambda i: (0, 0)
    return _call("ff2_loss", body, (s // tm,),
                 [(act, (tm, f), row), (w_ff2, (f, d), fix), (hres, (tm, d), row), (target, (tm, d), row),
                  (mod, (1, 6 * d), fix), (gp2, (1, d), fix)],
                 [((s, d), F32, (tm, d), row), ((s, d), BF16, (tm, d), row), ((1, 1), F32, (1, 1), fix),
                  ((1, d), F32, (1, d), fix), ((1, d), F32, (1, d), fix)])


def _ffn_bwd(df, w_ff2, u, act, h2, tn):
    s, d = df.shape
    f = w_ff2.shape[0]

    def body(df_ref, w_ref, u_ref, act_ref, h2_ref, du_ref, gw2_ref, gw1_ref):
        dfb = df_ref[...]
        du = (_dot(dfb, w_ref[...], _NT) * (2.0 * jnp.maximum(u_ref[...].astype(F32), 0.0))).astype(BF16)
        du_ref[...] = du
        gw2_ref[...] = _dot(act_ref[...], dfb, _TN).astype(BF16)
        gw1_ref[...] = _dot(h2_ref[...], du, _TN).astype(BF16)

    fix = lambda j: (0, 0)
    col = lambda j: (0, j)
    return _call("ffn_bwd", body, (f // tn,),
                 [(df, (s, d), fix), (w_ff2, (tn, d), lambda j: (j, 0)), (u, (s, tn), col), (act, (s, tn), col),
                  (h2, (s, d), fix)],
                 [((s, f), BF16, (s, tn), col), ((f, d), BF16, (tn, d), lambda j: (j, 0)),
                  ((f // tn, d, tn), BF16, (None, d, tn), lambda j: (j, 0, 0))])


def _ff1_bwd(du, w_ff1_t, hres, dout, y, mixed, mod, g2, gp1, tm, riders=None):
    s, d = hres.shape
    f = du.shape[1]

    def body(a_ref, w_ref, h_ref, do_ref, y_ref, mix_ref, mod_ref, g2_ref, gp_ref,
             dh_ref, dy_ref, dsh_ref, dsc_ref, dg2_ref, dgt_ref, dgp_ref, gwo_ref, acc_ref):
        _zero_at_start([dsh_ref, dsc_ref, dg2_ref, dgt_ref, dgp_ref, acc_ref])
        g2, sc2 = g2_ref[...], mod_ref[:, 4 * d:5 * d]
        gt, gp = mod_ref[:, 2 * d:3 * d], gp_ref[...]
        for rows in _pieces(tm):
            dh2 = _dot(a_ref[rows, :], w_ref[...])
            n2, r2 = _rms(h_ref[rows, :], d)
            dsh_ref[...] += _colsum(dh2)
            dsc_ref[...] += _colsum(dh2 * n2 * g2)
            dg2_ref[...] += _colsum(dh2 * n2 * (1.0 + sc2))
            dhres = do_ref[rows, :] + _rms_bwd(dh2 * g2 * (1.0 + sc2), n2, r2, d)
            dh_ref[rows, :] = dhres
            ny, ry = _rms(y_ref[rows, :], d)
            dgt_ref[...] += _colsum(dhres * (ny * gp))
            dgp_ref[...] += _colsum(dhres * gt * ny)
            dy_ref[rows, :] = _rms_bwd(dhres * gt * gp, ny, ry, d).astype(BF16)
        acc_ref[...] += _dot(mix_ref[...], dy_ref[...], _TN)

        @pl.when(pl.program_id(0) == s // tm - 1)
        def _():
            gwo_ref[...] = acc_ref[...].astype(BF16)

    row = lambda i: (i, 0)
    fix = lambda i: (0, 0)
    vec = ((1, d), F32, (1, d), fix)
    return _call("ff1_bwd", body, (s // tm,),
                 [(du, (tm, f), row), (w_ff1_t, (f, d), fix), (hres, (tm, d), row), (dout, (tm, d), row),
                  (y, (tm, d), row), (mixed, (tm, d), row), (mod, (1, 6 * d), fix), (g2, (1, d), fix), (gp1, (1, d), fix)],
                 [((s, d), F32, (tm, d), row), ((s, d), BF16, (tm, d), row), vec, vec, vec, vec, vec,
                  ((d, d), BF16, (d, d), fix)], scratch=[pltpu.VMEM((d, d), F32)], riders=riders)


def _out_bwd(dy, w_out, proj, r_bf, s_bf, tm, tn, riders=None):
    s, d = dy.shape
    ar_off = (2 * RET_QK + 2 * RET_V + 3 * SB_W) // tn
    as_off = ar_off + d // tn

    def body(a_ref, w_ref, ar_ref, as_ref, r_ref, s_ref, dr_ref, ds_ref, dar_ref, das_ref):
        dm = _dot(a_ref[...], w_ref[...], _NT)
        sr, ss = _sigmoid(ar_ref[...].astype(F32)), _sigmoid(as_ref[...].astype(F32))
        dr_ref[...] = (dm * sr).astype(BF16)
        ds_ref[...] = (dm * ss).astype(BF16)
        dar_ref[...] = (dm * r_ref[...].astype(F32) * sr * (1.0 - sr)).astype(BF16)
        das_ref[...] = (dm * s_ref[...].astype(F32) * ss * (1.0 - ss)).astype(BF16)

    tile = (tm, tn)
    here = lambda j, i: (i, j)
    return _call("out_bwd", body, (d // tn, s // tm),
                 [(dy, (tm, d), lambda j, i: (i, 0)), (w_out, (tn, d), lambda j, i: (j, 0)),
                  (proj, tile, lambda j, i: (i, ar_off + j)), (proj, tile, lambda j, i: (i, as_off + j)),
                  (r_bf, tile, here), (s_bf, tile, here)],
                 [((s, d), BF16, tile, here)] * 4, riders=riders)


def _branch_bwd(d_r, d_s, retg, sb, w_ret, w_sb_t, ret, proj, gn_g, riders=None):
    s, d = d_r.shape
    n_step = 4
    part_v, part_s, part_d = RET_V // n_step, SB_W // n_step, d // n_step
    per_dev = d // N_DEV
    n_blk = part_d // per_dev
    gate_off = (2 * RET_QK + RET_V) // part_v

    def body(dr_hbm, ds_hbm, rg_ref, sb_ref, wr_ref, ws_ref, r_ref, g_ref, w_ref,
             dsb_ref, gwr_ref, gws_ref, dg_ref, dret_ref, dw_ref, dr_ref, ds_ref, fetch_sems):
        i = pl.program_id(0)

        @pl.when(i == 0)
        def _():
            fetches = [pltpu.make_async_copy(dr_hbm, dr_ref, fetch_sems.at[0]),
                       pltpu.make_async_copy(ds_hbm, ds_ref, fetch_sems.at[1])]
            for cp in fetches:
                cp.start()
            for cp in fetches:
                cp.wait()

        dr, ds = dr_ref[...], ds_ref[...]
        dsb_ref[...] = _dot(ds, ws_ref[...]).astype(BF16)
        gwr_ref[...] = _dot(rg_ref[...], dr, _TN).astype(BF16)
        cols = pl.ds(pl.multiple_of(i * part_d, part_d), part_d)
        gws = _dot(sb_ref[...], ds_ref[:, cols], _TN).astype(BF16)
        for k in range(n_blk):
            gws_ref[k] = gws[:, k * per_dev:(k + 1) * per_dev]
        dretg = _dot(dr, wr_ref[...], _NT)
        for h in range(part_v // RET_DV):
            cols = slice(h * RET_DV, (h + 1) * RET_DV)
            o, g, w, d_o = r_ref[:, cols], g_ref[:, cols].astype(F32), w_ref[:, cols], dretg[:, cols]
            mu = jnp.sum(o, axis=1, keepdims=True) * (1.0 / RET_DV)
            xc = o - mu
            rstd = lax.rsqrt(jnp.sum(xc * xc, axis=1, keepdims=True) * (1.0 / RET_DV) + EPS)
            n = xc * rstd
            sg = _sigmoid(g)
            silu = g * sg
            dg_ref[:, cols] = (d_o * n * w * (sg * (1.0 + g * (1.0 - sg)))).astype(BF16)
            dw_ref[:, cols] = _colsum(d_o * silu * n)
            dn = d_o * silu * w
            m1 = jnp.sum(dn, axis=1, keepdims=True) * (1.0 / RET_DV)
            m2 = jnp.sum(dn * n, axis=1, keepdims=True) * (1.0 / RET_DV)
            dret_ref[:, cols] = (rstd * (dn - m1 - n * m2)).astype(BF16)

    fix = lambda i: (0, 0)
    col = lambda i: (0, i)
    return _call("branch_bwd", body, (n_step,),
                 [(d_r, None, pl.ANY), (d_s, None, pl.ANY), (retg, (s, part_v), col), (sb, (s, SB_W), fix),
                  (w_ret, (part_v, d), lambda i: (i, 0)), (w_sb_t, (d, part_s), col),
                  (ret, (s, part_v), col), (proj, (s, part_v), lambda i: (0, gate_off + i)), (gn_g, (1, part_v), col)],
                 [((s, SB_W), BF16, (s, part_s), col), ((RET_V, d), BF16, (part_v, d), lambda i: (i, 0)),
                  ((N_DEV, SB_W, per_dev), BF16, (n_blk, SB_W, per_dev), lambda i: (i, 0, 0)),
                  ((s, RET_V), BF16, (s, part_v), col), ((s, RET_V), BF16, (s, part_v), col),
                  ((1, RET_V), F32, (1, part_v), col)],
                 scratch=[pltpu.VMEM((s, d), d_r.dtype), pltpu.VMEM((s, d), d_s.dtype), pltpu.SemaphoreType.DMA((2,))],
                 riders=riders)


def _ret_bwd(qk_rot, proj, dret, log_gamma, t, riders=None):
    s = qk_rot.shape[0]
    n_pair = HEADS // 2
    pw = 2 * RET_DV
    wq, wv = RET_PAIRS * LANES, RET_PAIRS * pw
    n_blk = s // t
    pairs = range(RET_PAIRS)

    def load(q_ref, k_ref, v_ref, do_ref):
        return ([_lanes(q_ref, p, LANES) for p in pairs], [_lanes(k_ref, p, LANES) for p in pairs],
                [_lanes(v_ref, p, pw) for p in pairs], [_lanes(do_ref, p, pw) for p in pairs])

    def d_scores(lg_ref, hp, i, qb, kb, vb, dob):
        z, w = _ret_block(lg_ref, hp, i, t, qb, kb)
        dp = jnp.concatenate([_dot(dob[:, 0:RET_DV], vb[:, 0:RET_DV], _NT),
                              _dot(dob[:, RET_DV:pw], vb[:, RET_DV:pw], _NT)], axis=0)
        return (z * w).astype(BF16), (dp * w).astype(BF16)

    def up_body(lg_ref, q_ref, k_ref, v_ref, do_ref, dq_ref, state_ref):
        hg, i = pl.program_id(0), pl.program_id(1)

        @pl.when(i == 0)
        def _():
            state_ref[...] = jnp.zeros_like(state_ref)

        qbs, kbs, vbs, dobs = load(q_ref, k_ref, v_ref, do_ref)
        dss = [d_scores(lg_ref, hg * RET_PAIRS + p, i, qbs[p], kbs[p], vbs[p], dobs[p])[1] for p in pairs]
        for p in pairs:
            dq_ref[:, p * LANES:(p + 1) * LANES] = (_dot(_side_by_side(dss[p], t), _stack_heads(kbs[p]))
                                                    + _dot(dobs[p], state_ref[p], _NT)).astype(BF16)
        for p in pairs:
            state_ref[p] += _pair_mask() * _dot(kbs[p], vbs[p], _TN)

    def down_body(lg_ref, q_ref, k_ref, v_ref, do_ref, dk_ref, dv_ref, state_ref):
        hg, i = pl.program_id(0), n_blk - 1 - pl.program_id(1)

        @pl.when(pl.program_id(1) == 0)
        def _():
            state_ref[...] = jnp.zeros_like(state_ref)

        qbs, kbs, vbs, dobs = load(q_ref, k_ref, v_ref, do_ref)
        both = [d_scores(lg_ref, hg * RET_PAIRS + p, i, qbs[p], kbs[p], vbs[p], dobs[p]) for p in pairs]
        for p in pairs:
            pp, ds = both[p]
            later = state_ref[p]
            dv_ref[:, p * pw:(p + 1) * pw] = (jnp.concatenate(
                [_dot(pp[:t], dobs[p][:, 0:RET_DV], _TN), _dot(pp[t:], dobs[p][:, RET_DV:pw], _TN)],
                axis=1) + _dot(kbs[p], later)).astype(BF16)
            dk_ref[:, p * LANES:(p + 1) * LANES] = (_dot(ds, _stack_heads(qbs[p]), _TN)
                                                    + _dot(vbs[p], later, _NT)).astype(BF16)
        for p in pairs:
            state_ref[p] += _pair_mask() * _dot(qbs[p], dobs[p], _TN)

    n_grp = n_pair // RET_PAIRS

    def ins(order):
        return [(log_gamma, None, pltpu.SMEM),
                (qk_rot, (t, wq), lambda hg, i: (order(i), hg)),
                (qk_rot, (t, wq), lambda hg, i: (order(i), n_grp + hg)),
                (proj, (t, wv), lambda hg, i: (order(i), 2 * RET_QK // wv + hg)),
                (dret, (t, wv), lambda hg, i: (order(i), hg))]

    up = lambda i: i
    down = lambda i: n_blk - 1 - i
    scratch = [pltpu.VMEM((RET_PAIRS, LANES, pw), F32)]
    dq = _call("ret_bwd_q", up_body, (n_grp, n_blk), ins(up),
               [((s, RET_QK), BF16, (t, wq), lambda hg, i: (i, hg))], scratch=scratch)[0]
    dk, dv, *rest = _call("ret_bwd_kv", down_body, (n_grp, n_blk), ins(down),
                          [((s, RET_QK), BF16, (t, wq), lambda hg, i: (down(i), hg)),
                           ((s, RET_V), BF16, (t, wv), lambda hg, i: (down(i), hg))],
                          scratch=scratch, riders=riders)
    return [dq, dk, dv] + rest


def _sb_bwd(q_sb, proj, weights, do, tq, tk, riders=None):
    s = q_sb.shape[0]
    k_off = (2 * RET_QK + 2 * RET_V + SB_W) // LANES
    n_pair = HEADS // 2
    _check_tiles(s, tq, tk, SB_GROUP)

    def body(q_ref, k_ref, v_ref, a_ref, do_ref, dq_ref, dk_ref, dv_ref):
        i = pl.program_id(1)

        @pl.when(i == 0)
        def _():
            dk_ref[...] = jnp.zeros_like(dk_ref)
            dv_ref[...] = jnp.zeros_like(dv_ref)

        lower = _tri(tk, False)
        qs = _stack_heads(q_ref[...])
        dos = _stack_heads(do_ref[...].astype(BF16))

        def make_step(near_diagonal, n_sub=SB_GROUP):
            def step(g, carry):
                c_e, dq = carry
                js = [g * SB_GROUP + sub for sub in range(n_sub)]
                rows = [_key_rows(j, tk) for j in js]
                zs = [_dot(qs, k_ref[rw, :], _NT) for rw in rows]
                das = [_dot(dos, v_ref[rw, :], _NT) for rw in rows]
                avals = [a_ref[j] for j in js]
                for a, rw in zip(avals, rows):
                    dv_ref[rw, :] += _dot(a, dos, _TN)
                es = [a.astype(F32) * da for a, da in zip(avals, das)]
                prefixes = [_dot(e, lower) for e in es]
                betas = [1.0 / (1.0 + jnp.exp2(-z)) for z in zs]
                for sub in range(n_sub):
                    dz = es[sub] - (es[sub] + prefixes[sub] + c_e) * betas[sub]
                    if near_diagonal:
                        dz = jnp.where(_sb_valid(i, js[sub], tq, tk), dz, 0.0)
                    dz = dz.astype(BF16)
                    dk_ref[rows[sub], :] += _dot(dz, qs, _TN)
                    dq = dq + _dot(_side_by_side(dz, tq), _stack_heads(k_ref[rows[sub], :]))
                    c_e = c_e + jnp.sum(es[sub], axis=1, keepdims=True)
                return c_e, dq
            return step

        n_full = _n_full(i, tq, tk, SB_GROUP)
        carry = (jnp.zeros((2 * tq, 1), F32), jnp.zeros((tq, LANES), F32))
        carry = lax.fori_loop(0, n_full, make_step(False), carry)
        _, dq = _diagonal_step(i, tq, tk, lambda n_sub: (lambda n, cr: make_step(True, n_sub)(n_full, cr)), carry)
        dq_ref[...] = dq

    blk = lambda hp, i: (i, hp)
    n_kb = s // tk
    return _call("sb_bwd", body, (n_pair, s // tq),
                 [(q_sb, (tq, LANES), blk),
                  (proj, (s, LANES), lambda hp, i: (0, k_off + hp)),
                  (proj, (s, LANES), lambda hp, i: (0, k_off + n_pair + hp)),
                  (weights, (None, None, n_kb, 2 * tq, tk), lambda hp, i: (hp, i, 0, 0, 0)),
                  (do, (tq, LANES), blk)],
                 [((s, SB_W), F32, (tq, LANES), blk),
                  ((s, SB_W), F32, (s, LANES), lambda hp, i: (0, hp)),
                  ((s, SB_W), F32, (s, LANES), lambda hp, i: (0, hp))], riders=riders)


def _assemble_dproj(dq_r, dk_r, dv_r, dg_r, dq_s, dk_s, dv_s, da_r, da_s, cos, sin, idx_col, lg_lanes, tm, riders=None):
    s, d = da_r.shape
    width = 2 * RET_QK + 2 * RET_V + 3 * SB_W + 2 * d

    def body(dq_ref, dk_ref, dv_ref, dg_ref, dqs_ref, dks_ref, dvs_ref, dar_ref, das_ref, cos_ref, sin_ref,
             idx_ref, lg_ref, o_ref):
        lane = lax.broadcasted_iota(jnp.int32, (1, LANES), 1)
        first = jnp.bitwise_and(lane, RET_DQK - 1) < (RET_DQK // 2)
        cos, sin = cos_ref[...], sin_ref[...]
        idx = idx_ref[...]
        for src, base, sign, scale in ((dq_ref, 0, 1.0, 1.0), (dk_ref, RET_QK, -1.0, RET_DQK ** -0.5)):
            for g in range(RET_QK // LANES):
                v = src[:, g * LANES:(g + 1) * LANES].astype(F32) * (_decay_scale(lg_ref, idx, g, sign) * scale)
                sw = jnp.where(first, pltpu.roll(v, LANES - RET_DQK // 2, 1), pltpu.roll(v, RET_DQK // 2, 1))
                o_ref[:, base + g * LANES:base + (g + 1) * LANES] = (v * cos - sw * sin).astype(BF16)
        off = 2 * RET_QK
        o_ref[:, off:off + RET_V] = dv_ref[...].astype(BF16)
        off += RET_V
        o_ref[:, off:off + RET_V] = dg_ref[...]
        off += RET_V
        o_ref[:, off:off + SB_W] = (dqs_ref[...] * (SB_DH ** -0.5)).astype(BF16)
        off += SB_W
        o_ref[:, off:off + SB_W] = (dks_ref[...] * LN2).astype(BF16)
        off += SB_W
        o_ref[:, off:off + SB_W] = dvs_ref[...].astype(BF16)
        off += SB_W
        o_ref[:, off:off + d] = dar_ref[...]
        off += d
        o_ref[:, off:off + d] = das_ref[...]

    row = lambda i: (i, 0)
    ins = [(a, (tm, a.shape[1]), row) for a in (dq_r, dk_r, dv_r, dg_r, dq_s, dk_s, dv_s, da_r, da_s, cos, sin, idx_col)]
    ins.append((lg_lanes, (1, RET_QK), lambda i: (0, 0)))
    return _call("assemble_dproj", body, (s // tm,), ins, [((s, width), BF16, (tm, width), row)], riders=riders)


def _in_bwd(dproj, w_in_t, x, dhres, mod, g1, tm, riders=None):
    s, d = x.shape
    width = dproj.shape[1]

    def body(a_ref, w_hbm, x_ref, dh_ref, mod_ref, g_ref, dx_ref, dsh_ref, dsc_ref, dg_ref, w_ref, w_sem):
        _zero_at_start([dsh_ref, dsc_ref, dg_ref])

        @pl.when(pl.program_id(0) == 0)
        def _():
            fetch = pltpu.make_async_copy(w_hbm, w_ref, w_sem)
            fetch.start()
            fetch.wait()

        g1, sc1 = g_ref[...], mod_ref[:, d:2 * d]
        for rows in _pieces(tm):
            dh = _dot(a_ref[rows, :], w_ref[...])
            n1, r1 = _rms(x_ref[rows, :], d)
            dsh_ref[...] += _colsum(dh)
            dsc_ref[...] += _colsum(dh * n1 * g1)
            dg_ref[...] += _colsum(dh * n1 * (1.0 + sc1))
            dx_ref[rows, :] = dh_ref[rows, :] + _rms_bwd(dh * g1 * (1.0 + sc1), n1, r1, d)

    row = lambda i: (i, 0)
    fix = lambda i: (0, 0)
    vec = ((1, d), F32, (1, d), fix)
    return _call("in_bwd", body, (s // tm,),
                 [(dproj, (tm, width), row), (w_in_t, None, pl.ANY), (x, (tm, d), row), (dhres, (tm, d), row),
                  (mod, (1, 6 * d), fix), (g1, (1, d), fix)],
                 [((s, d), F32, (tm, d), row), vec, vec, vec],
                 scratch=[pltpu.VMEM((width, d), w_in_t.dtype), pltpu.SemaphoreType.DMA], riders=riders)


def _adamw(w, g, m, v):
    m = ADAM_B1 * m + (1.0 - ADAM_B1) * g
    v = ADAM_B2 * v + (1.0 - ADAM_B2) * (g * g)
    m_hat = m / (1.0 - ADAM_B1 ** ADAM_STEP)
    v_hat = v / (1.0 - ADAM_B2 ** ADAM_STEP)
    delta = -ADAM_LR * (m_hat / (jnp.sqrt(v_hat) + ADAM_EPS) + ADAM_WD * w)
    return delta, m, v


def _adam_reduce(name, sets, steps):
    n = len(sets)

    def body(*refs):
        for k in range(n):
            p_ref, w_ref, m_ref, v_ref = refs[4 * k:4 * k + 4]
            outs = refs[4 * n + 4 * k:4 * n + 4 * k + 4]
            g = p_ref[0].astype(F32)
            for j in range(1, p_ref.shape[0]):
                g = g + p_ref[j].astype(F32)
            for o_ref, val in zip(outs, (g,) + _adamw(w_ref[...], g, m_ref[...], v_ref[...])):
                o_ref[...] = val

    ins, outs = [], []
    row = lambda i: (i, 0)
    for parts, w, m, v in sets:
        rws, cls = w.shape
        tr = rws // steps
        assert tr * steps == rws and tr % 16 == 0
        ins += [(parts, (parts.shape[0], tr, cls), lambda i: (0, i, 0)), (w, (tr, cls), row), (m, (tr, cls), row),
                (v, (tr, cls), row)]
        outs += [((rws, cls), F32, (tr, cls), row)] * 4
    res = _call(name, body, (steps,), ins, outs)
    return [res[4 * k:4 * k + 4] for k in range(n)]


def _ada_bwd_adam(cs_t, dmod_cols, w, m, v, tr):
    d, nc = w.shape

    def body(c_ref, dm_ref, w_ref, m_ref, v_ref, g_out, d_out, m_out, v_out):
        g = c_ref[0] * dm_ref[0:1, :]
        for r in range(1, N_DEV):
            g = g + c_ref[r] * dm_ref[r:r + 1, :]
        delta, mn, vn = _adamw(w_ref[...], g, m_ref[...], v_ref[...])
        g_out[...] = g
        d_out[...] = delta
        m_out[...] = mn
        v_out[...] = vn

    row = lambda i: (i, 0)
    blk = (tr, nc)
    return _call("ada_bwd_adam", body, (d // tr,),
                 [(cs_t, (N_DEV, tr, 1), lambda i: (0, i, 0)), (dmod_cols, (N_DEV, nc), lambda i: (0, 0)),
                  (w, blk, row), (m, blk, row), (v, blk, row)],
                 [((d, nc), F32, blk, row)] * 4)


def _small_adam(parts, ws, ms, vs):
    n = len(ws)
    widths = [w.shape[1] for w in ws]
    total = parts.shape[1]
    assert sum(widths) + LANES == total

    def body(p_ref, *refs):
        w_refs, m_refs, v_refs = refs[:n], refs[n:2 * n], refs[2 * n:3 * n]
        outs = refs[3 * n:]
        g = p_ref[0:1, :]
        for k in range(1, N_DEV):
            g = g + p_ref[k:k + 1, :]
        off = 0
        for i, width in enumerate(widths):
            gi = g[:, off:off + width]
            delta, mn, vn = _adamw(w_refs[i][...], gi, m_refs[i][...], v_refs[i][...])
            for o_ref, val in zip(outs[4 * i:4 * i + 4], (gi, delta, mn, vn)):
                o_ref[...] = val
            off += width
        outs[4 * n][...] = g[:, off:off + LANES]

    fix = lambda i: (0, 0)
    vec = lambda a: (a, (1, a.shape[1]), fix)
    out_specs = [((1, width), F32, (1, width), fix) for width in widths for _ in range(4)]
    out_specs.append(((1, LANES), F32, (1, LANES), fix))
    res = _call("small_adam", body, (1,),
                [(parts, (N_DEV, total), fix)] + [vec(a) for a in list(ws) + list(ms) + list(vs)], out_specs)
    return [res[4 * i:4 * i + 4] for i in range(n)], res[4 * n]


def kernel(x, c, positions, ada_w, ada_b, pre_mix_g, post_mix_g, pre_ffn_g, post_ffn_g, w_in, ret_gn_g, w_ret_branch, w_sb_branch, w_out, w_ff1, w_ff2, loss_target, m_ada_w, m_ada_b, m_pre_mix_g, m_post_mix_g, m_pre_ffn_g, m_post_ffn_g, m_w_in, m_ret_gn_g, m_w_ret_branch, m_w_sb_branch, m_w_out, m_w_ff1, m_w_ff2, v_ada_w, v_ada_b, v_pre_mix_g, v_post_mix_g, v_pre_ffn_g, v_post_ffn_g, v_w_in, v_ret_gn_g, v_w_ret_branch, v_w_sb_branch, v_w_out, v_w_ff1, v_w_ff2):
    _, s, d = x.shape
    d_ff = w_ff1.shape[2] * N_DEV
    d_in = w_in.shape[2] * N_DEV
    me = 4 * lax.axis_index("x") + 2 * lax.axis_index("y") + lax.axis_index("c")
    x2, tgt = x[0], loss_target[0]

    core = lax.axis_index("c").astype(jnp.int32).reshape(1)
    bf = lambda w: w[0].astype(BF16)

    w_in_t, m_in_t, v_in_t = (jnp.swapaxes(a[0], 0, 1) for a in (w_in, m_w_in, v_w_in))

    c_all, g_in = _exchange("gather_in", [c, w_in_t.astype(BF16)], ["gather", "gather_chip"])
    c_all = c_all.reshape(N_DEV, d)

    n_ada = ada_w.shape[2]
    ada_b_cols = lax.dynamic_slice(ada_b, (0, me * n_ada), (1, n_ada))
    cs_all, mod_cols = _ada_fwd(c_all, ada_w[0], ada_b_cols)
    mod_all, g_in = _exchange("gather_mod", [mod_cols, g_in], ["gather", "forward"])
    mod = lax.dynamic_index_in_dim(mod_all, me, axis=1, keepdims=False).reshape(1, 6 * d)

    tm = min(256, s)
    h = _pre_norm(x2, pre_mix_g, mod, 2 * tm)[0]
    wt_in = g_in.reshape(d_in, d)
    bf_t = lambda w: jnp.swapaxes(w[0], 0, 1).astype(BF16)
    small_w = [bf(w_ret_branch), bf_t(w_sb_branch), bf(w_out)]
    proj, *small_w = _matmul("in_proj", h, wt_in, "nt", s, 512, BF16, riders=(small_w, ["gather_chip"] * 3))
    pos_col = positions.reshape(s, 1).astype(F32)
    freqs = ROPE_BASE ** (-jnp.arange(0, RET_DQK, 2, dtype=F32) / RET_DQK)
    inv_freq = jnp.tile(freqs, LANES // (RET_DQK // 2)).reshape(1, LANES)
    log_gamma_np = np.log1p(-(2.0 ** (-5.0 - np.arange(HEADS))))
    log_gamma = jnp.asarray(log_gamma_np, F32)
    lg_lanes = jnp.asarray(np.repeat(log_gamma_np, RET_DQK).reshape(1, RET_QK), F32)
    idx_col = (jnp.arange(s, dtype=F32) - (s // 2)).reshape(s, 1)
    qk_rot, q_sb, cos_t, sin_t = _prep(proj, pos_col, idx_col, inv_freq, lg_lanes, 2 * tm)
    tq, tk = min(256, s), min(128, s)
    tq_sb = min(SB_TQ, s)
    sb, sb_weights, *big_w = _sb_fwd(q_sb, proj, tq_sb, tk, riders=([bf(w_ff2), bf_t(w_ff1)], ["gather_chip"] * 2))
    ret, retg, g_ret, g_sb, g_out, g_ff2, g_ff1 = _ret_fwd(qk_rot, proj, ret_gn_g, log_gamma, tq,
                                                           riders=(small_w + big_w, ["forward"] * 5))
    wf_ret = g_ret.reshape(RET_V, d)
    wt_sb = g_sb.reshape(d, SB_W)
    wf_out = g_out.reshape(d, d)
    wt_ff1 = g_ff1.reshape(d_ff, d)
    wf_ff2 = g_ff2.reshape(d_ff, d)
    mixed, r_bf, s_bf, y, hres, h2 = _merge_out(retg, sb, wf_ret, wt_sb, wf_out, proj, x2, mod, post_mix_g, pre_ffn_g, tm)
    u, act = _ff1(h2, wt_ff1, s, 512)
    dout, df, loss_sum, d_gt2, d_gp2 = _ff2_loss(act, wf_ff2, hres, tgt, mod, post_ffn_g, tm)

    du, gw_ff2, gw_ff1 = _ffn_bwd(df, wf_ff2, u, act, h2, d_ff // N_DEV)
    gw_ff2 = gw_ff2.reshape(N_DEV, d_ff // N_DEV, d)
    dhres, dy, d_sh2, d_sc2, d_g2, d_gt1, d_gp1, gw_out, t_ff1, t_ff2 = _ff1_bwd(
        du, wt_ff1, hres, dout, y, mixed, mod, pre_ffn_g, post_mix_g, tm, riders=([gw_ff1, gw_ff2], ["pair"] * 2))
    gw_out = gw_out.reshape(N_DEV, d // N_DEV, d)
    s_ff1, s_ff2 = _pair_sum("pair_sum_ff", [(gw_ff1, t_ff1), (gw_ff2, t_ff2)], core)
    d_r, d_s, da_r, da_s = _out_bwd(dy, wf_out, proj, r_bf, s_bf, 2 * tm, min(512, d))
    dsb, gw_ret, gw_sb, dg_r, dret, d_gn, p_out = _branch_bwd(d_r, d_s, retg, sb, wf_ret, wt_sb, ret, proj, ret_gn_g,
                                                              riders=([gw_out], ["scatter"]))
    gw_ret = gw_ret.reshape(N_DEV, RET_V // N_DEV, d)
    dq_s, dk_s, dv_s, p_ff1, p_ff2 = _sb_bwd(q_sb, proj, sb_weights, dsb, tq_sb, tk,
                                             riders=([s_ff1, s_ff2], ["chip_scatter"] * 2))
    dq_r, dk_r, dv_r, p_sb = _ret_bwd(qk_rot, proj, dret, log_gamma, tq, riders=([gw_sb], ["scatter"]))
    dproj = _assemble_dproj(dq_r, dk_r, dv_r, dg_r, dq_s, dk_s, dv_s, da_r, da_s, cos_t, sin_t, idx_col, lg_lanes, tm)[0]
    gw_in, p_ret = _matmul("grad_w_in", dproj, h, "tn", 512, d, BF16, riders=([gw_ret], ["scatter"]))
    gw_in = gw_in.reshape(N_DEV, d_in // N_DEV, d)
    t_in = _exchange("pair_in", [gw_in], ["pair"])[0]
    s_in = _pair_sum("pair_sum_in", [(gw_in, t_in)], core)[0]
    grad_x, d_sh1, d_sc1, d_g1, p_in = _in_bwd(dproj, wt_in, x2, dhres, mod, pre_mix_g, tm,
                                               riders=([s_in], ["chip_scatter"]))
    loss_lanes = jnp.pad(loss_sum, ((0, 0), (0, LANES - 1)))
    small = jnp.concatenate([d_sh1, d_sc1, d_gt1, d_sh2, d_sc2, d_gt2, d_g1, d_gp1, d_g2, d_gp2, d_gn, loss_lanes], axis=1)
    small_all = _exchange("gather_small", [small], ["gather"])[0].reshape(N_DEV, small.shape[1])
    parts = [p_in, p_ret, p_sb, p_out, p_ff1, p_ff2]

    res = {}
    names = ["w_ret_branch", "w_sb_branch", "w_out", "w_ff1", "w_ff2"]
    ws = [w_ret_branch, w_sb_branch, w_out, w_ff1, w_ff2]
    ms = [m_w_ret_branch, m_w_sb_branch, m_w_out, m_w_ff1, m_w_ff2]
    vs = [v_w_ret_branch, v_w_sb_branch, v_w_out, v_w_ff1, v_w_ff2]
    sets = [(parts[0], w_in_t, m_in_t, v_in_t)] + [(p, w[0], m[0], v[0]) for p, w, m, v in zip(parts[1:], ws, ms, vs)]
    updated = _adam_reduce("adam_big", sets, 4)
    res["w_in"] = [jnp.swapaxes(o, 0, 1)[None] for o in updated[0]]
    for nm, outs4 in zip(names, updated[1:]):
        res[nm] = [o[None] for o in outs4]
    dmod_cols = lax.dynamic_slice(small_all, (0, me * n_ada), (N_DEV, n_ada))
    res["ada_w"] = [o[None] for o in _ada_bwd_adam(cs_all.reshape(N_DEV, d, 1), dmod_cols, ada_w[0], m_ada_w[0], v_ada_w[0], tm)]
    vec_names = ["ada_b", "pre_mix_g", "post_mix_g", "pre_ffn_g", "post_ffn_g", "ret_gn_g"]
    vec_res, loss_lanes = _small_adam(small_all,
                                      [ada_b, pre_mix_g, post_mix_g, pre_ffn_g, post_ffn_g, ret_gn_g],
                                      [m_ada_b, m_pre_mix_g, m_post_mix_g, m_pre_ffn_g, m_post_ffn_g, m_ret_gn_g],
                                      [v_ada_b, v_pre_mix_g, v_post_mix_g, v_pre_ffn_g, v_post_ffn_g, v_ret_gn_g])
    res.update(zip(vec_names, vec_res))
    loss = (0.5 / d) * loss_lanes[0, 0]
    order = ["ada_w", "ada_b", "pre_mix_g", "post_mix_g", "pre_ffn_g", "post_ffn_g", "w_in", "ret_gn_g",
             "w_ret_branch", "w_sb_branch", "w_out", "w_ff1", "w_ff2"]
    outs = [loss, grad_x[None]]
    for k in range(4):
        outs += [res[nm][k] for nm in order]
    return tuple(outs)
```

```python
import functools

import numpy as np
import jax
import jax.numpy as jnp
from jax import lax
from jax.experimental import pallas as pl
from jax.experimental.pallas import tpu as pltpu

F32 = jnp.float32
BF16 = jnp.bfloat16
N_DEV = 8
AXES = ("x", "y", "c")

EPS = 1e-6
CHUNK = 64
CHUNK_SHIFT = 6
HEADS = 8
RET_DQK = 64
RET_DV = 128
SB_DH = 64
RET_QK = HEADS * RET_DQK
RET_V = HEADS * RET_DV
SB_W = HEADS * SB_DH
ROPE_BASE = 10000.0
LANES = 128

ADAM_LR = 0.001
ADAM_B1 = 0.9
ADAM_B2 = 0.999
ADAM_EPS = 1e-08
ADAM_WD = 0.01
ADAM_STEP = 10

VMEM_LIMIT = 56 * 1024 * 1024

_NN = (((1,), (0,)), ((), ()))
_NT = (((1,), (1,)), ((), ()))
_TN = (((0,), (0,)), ((), ()))


def _dot(a, b, dims=_NN):
    if a.dtype != BF16:
        a = a.astype(BF16)
    if b.dtype != BF16:
        b = b.astype(BF16)
    return lax.dot_general(a, b, dims, preferred_element_type=F32)


def _sigmoid(x):
    return 1.0 / (1.0 + jnp.exp(-x))


def _rms(x, d):
    r = lax.rsqrt(jnp.sum(x * x, axis=1, keepdims=True) * (1.0 / d) + EPS)
    return x * r, r


def _rms_bwd(dn, n, r, d):
    return r * (dn - n * (jnp.sum(dn * n, axis=1, keepdims=True) * (1.0 / d)))


def _colsum(v):
    return jnp.sum(v, axis=0, keepdims=True)


ROW_SPLIT = 2


def _zero_at_start(refs):
    @pl.when(pl.program_id(0) == 0)
    def _():
        for r in refs:
            r[...] = jnp.zeros_like(r)


def _pieces(tm):
    step = tm // ROW_SPLIT
    return [slice(k * step, (k + 1) * step) for k in range(ROW_SPLIT)]


KIND_SLOTS = {"gather": N_DEV, "scatter": N_DEV, "gather_chip": N_DEV, "forward": N_DEV, "pair": N_DEV // 2,
              "chip_scatter": N_DEV // 2}
SEMS_PER_ARRAY = N_DEV - 1


def _exchange_copies(ins, outs, send_sems, recv_sems, local_sems, kinds):
    x, y, c = (lax.axis_index(a) for a in AXES)
    me, chip, sibling = 4 * x + 2 * y + c, 2 * x + y, (x, y, 1 - c)
    mesh_id = pl.DeviceIdType.MESH
    other_chips = []
    for k in range(1, N_DEV // 2):
        px = 1 - x if k & 2 else x
        py = 1 - y if k & 1 else y
        other_chips.append((px, py))
    copies = []
    for i, kind in enumerate(kinds):
        def remote(src, dst, k, to, i=i):
            return pltpu.make_async_remote_copy(
                src_ref=src, dst_ref=dst, send_sem=send_sems.at[i * SEMS_PER_ARRAY + k],
                recv_sem=recv_sems.at[i * SEMS_PER_ARRAY + k], device_id=to, device_id_type=mesh_id)

        if kind in ("gather", "scatter"):
            pick = (lambda ref, d: ref.at[d]) if kind == "scatter" else (lambda ref, d: ref)
            copies.append(pltpu.make_async_copy(pick(ins[i], me), outs[i].at[me], local_sems.at[i]))
            for k in range(1, N_DEV):
                to = (1 - x if k & 4 else x, 1 - y if k & 2 else y, 1 - c if k & 1 else c)
                copies.append(remote(pick(ins[i], 4 * to[0] + 2 * to[1] + to[2]), outs[i].at[me], k - 1, to))
        elif kind == "gather_chip":
            copies.append(pltpu.make_async_copy(ins[i], outs[i].at[me], local_sems.at[i]))
            copies.append(remote(ins[i], outs[i].at[me], 0, sibling))
            for k, (px, py) in enumerate(other_chips):
                copies.append(remote(ins[i], outs[i].at[me], 1 + k, (px, py, c)))
        elif kind == "forward":
            for k, (px, py) in enumerate(other_chips):
                slot = 4 * px + 2 * py + c
                copies.append(remote(outs[i].at[slot], outs[i].at[slot], k, sibling))
        elif kind == "pair":
            for k in range(N_DEV // 2):
                copies.append(remote(ins[i].at[2 * k + 1 - c], outs[i].at[k], k, sibling))
        elif kind == "chip_scatter":
            copies.append(pltpu.make_async_copy(ins[i].at[chip], outs[i].at[chip], local_sems.at[i]))
            for k, (px, py) in enumerate(other_chips):
                copies.append(remote(ins[i].at[2 * px + py], outs[i].at[chip], k, (px, py, c)))
        else:
            raise ValueError(kind)
    return copies


def _exchange_shapes(arrays, kinds):
    shapes = []
    for a, kind in zip(arrays, kinds):
        tail = a.shape if kind in ("gather", "gather_chip") else a.shape[1:]
        shapes.append(jax.ShapeDtypeStruct((KIND_SLOTS[kind],) + tuple(tail), a.dtype))
    return shapes


def _exchange_sems(n):
    return [pltpu.SemaphoreType.DMA((n * SEMS_PER_ARRAY,)), pltpu.SemaphoreType.DMA((n * SEMS_PER_ARRAY,)),
            pltpu.SemaphoreType.DMA((n,))]


def _call(name, body, grid, ins, outs, scratch=(), riders=None, prefetch=None):
    any_spec = pl.BlockSpec(memory_space=pl.ANY)
    in_specs = [pl.BlockSpec(memory_space=im) if bs is None else pl.BlockSpec(bs, im) for _, bs, im in ins]
    out_specs = [pl.BlockSpec(bs, im) for _, _, bs, im in outs]
    out_shape = [jax.ShapeDtypeStruct(s, d) for s, d, _, _ in outs]
    operands = [a for a, _, _ in ins]
    scratch = list(scratch)
    aliases = {}
    n_pre = 0 if prefetch is None else 1
    kernel = functools.partial(body) if prefetch is None else (lambda _, *refs: body(*refs))
    if riders is not None:
        arrays, kinds = riders
        nr, n_in, n_out, n_scr = len(arrays), len(ins), len(outs), len(scratch)

        def kernel(*refs):
            refs = refs[n_pre:]
            own_in, ride_in = refs[:n_in], refs[n_in:n_in + nr]
            own_out = refs[n_in + nr:n_in + nr + n_out]
            ride_out = refs[n_in + nr + n_out:n_in + 2 * nr + n_out]
            own_scr = refs[n_in + 2 * nr + n_out:n_in + 2 * nr + n_out + n_scr]
            sems = refs[n_in + 2 * nr + n_out + n_scr:]
            ids = [pl.program_id(a) for a in range(len(grid))]
            first = functools.reduce(jnp.logical_and, [i == 0 for i in ids])
            last = functools.reduce(jnp.logical_and, [i == g - 1 for i, g in zip(ids, grid)])

            @pl.when(first)
            def _():
                for cp in _exchange_copies(ride_in, ride_out, *sems, kinds):
                    cp.start()

            body(*own_in, *own_out, *own_scr)

            @pl.when(last)
            def _():
                for cp in _exchange_copies(ride_in, ride_out, *sems, kinds):
                    cp.wait()

        in_specs += [any_spec] * nr
        out_specs += [any_spec] * nr
        out_shape += _exchange_shapes(arrays, kinds)
        operands += list(arrays)
        scratch += _exchange_sems(nr)
        aliases = {n_pre + n_in + r: n_out + r for r, kind in enumerate(kinds) if kind == "forward"}
    params = pltpu.CompilerParams(dimension_semantics=("arbitrary",) * len(grid), vmem_limit_bytes=VMEM_LIMIT)
    if prefetch is None:
        return pl.pallas_call(kernel, name=name, grid=grid, in_specs=in_specs, out_specs=out_specs,
                              out_shape=out_shape, scratch_shapes=scratch, input_output_aliases=aliases,
                              compiler_params=params)(*operands)
    grid_spec = pltpu.PrefetchScalarGridSpec(num_scalar_prefetch=1, grid=grid, in_specs=in_specs,
                                             out_specs=out_specs, scratch_shapes=scratch)
    return pl.pallas_call(kernel, name=name, grid_spec=grid_spec, out_shape=out_shape,
                          input_output_aliases=aliases, compiler_params=params)(prefetch, *operands)


def _exchange(name, arrays, kinds):
    n = len(arrays)

    def body(*refs):
        copies = _exchange_copies(refs[:n], refs[n:2 * n], *refs[2 * n:], kinds)
        for cp in copies:
            cp.start()
        for cp in copies:
            cp.wait()

    any_spec = pl.BlockSpec(memory_space=pl.ANY)
    return pl.pallas_call(
        functools.partial(body),
        name=name,
        in_specs=[any_spec] * n,
        out_specs=[any_spec] * n,
        out_shape=_exchange_shapes(arrays, kinds),
        scratch_shapes=_exchange_sems(n),
        input_output_aliases={i: i for i, kind in enumerate(kinds) if kind == "forward"},
    )(*arrays)


def _pair_sum(name, pairs, my_core):
    n = len(pairs)

    def body(*refs):
        for k in range(n):
            a_ref, b_ref, o_ref = refs[2 * k], refs[2 * k + 1], refs[2 * n + k]
            o_ref[...] = (a_ref[...].astype(F32) + b_ref[...].astype(F32)).astype(o_ref.dtype)

    ins, outs = [], []
    for mine, theirs in pairs:
        _, rws, cls = mine.shape
        ins += [(mine, (None, rws, cls), lambda k, core: (2 * k + core[0], 0, 0)),
                (theirs, (None, rws, cls), lambda k, core: (k, 0, 0))]
        outs.append(((N_DEV // 2, rws, cls), mine.dtype, (None, rws, cls), lambda k, core: (k, 0, 0)))
    return _call(name, body, (N_DEV // 2,), ins, outs, prefetch=my_core)


def _matmul(name, a, b, kind, tm, tn, out_dtype, blocked_out=False, riders=None):
    if kind == "tn":
        kdim, m = a.shape
    else:
        m, kdim = a.shape
    n = b.shape[0] if kind == "nt" else b.shape[1]
    tm, tn = min(tm, m), min(tn, n)
    dims = {"nn": _NN, "nt": _NT, "tn": _TN}[kind]

    def body(a_ref, b_ref, o_ref):
        o_ref[...] = _dot(a_ref[...], b_ref[...], dims).astype(o_ref.dtype)

    a_spec = (a, (kdim, tm), lambda j, i: (0, i)) if kind == "tn" else (a, (tm, kdim), lambda j, i: (i, 0))
    b_spec = (b, (tn, kdim), lambda j, i: (j, 0)) if kind == "nt" else (b, (kdim, tn), lambda j, i: (0, j))
    if blocked_out:
        out = ((n // tn, m, tn), out_dtype, (None, tm, tn), lambda j, i: (j, i, 0))
    else:
        out = ((m, n), out_dtype, (tm, tn), lambda j, i: (i, j))
    res = _call(name, body, (n // tn, m // tm), [a_spec, b_spec], [out], riders=riders)
    return res[0] if riders is None else res


def _ada_fwd(c_all, ada_w, ada_b_cols):
    def body(c_ref, w_ref, b_ref, cs_ref, o_ref):
        v = c_ref[...]
        cs = v * _sigmoid(v)
        cs_ref[...] = cs
        o_ref[...] = lax.dot_general(cs, w_ref[...], _NN, preferred_element_type=F32,
                                     precision=lax.Precision.HIGHEST) + b_ref[...]

    r, d = c_all.shape
    nc = ada_w.shape[1]
    fix = lambda i: (0, 0)
    return _call("ada_fwd", body, (1,),
                 [(c_all, (r, d), fix), (ada_w, (d, nc), fix), (ada_b_cols, (1, nc), fix)],
                 [((r, d), F32, (r, d), fix), ((r, nc), F32, (r, nc), fix)])


def _pre_norm(x, g, mod, tm, riders=None):
    s, d = x.shape

    def body(x_ref, g_ref, mod_ref, h_ref):
        n, _ = _rms(x_ref[...], d)
        sh, sc = mod_ref[:, 0:d], mod_ref[:, d:2 * d]
        h_ref[...] = (n * g_ref[...] * (1.0 + sc) + sh).astype(BF16)

    return _call("pre_norm", body, (s // tm,),
                 [(x, (tm, d), lambda i: (i, 0)), (g, (1, d), lambda i: (0, 0)),
                  (mod, (1, 6 * d), lambda i: (0, 0))],
                 [((s, d), BF16, (tm, d), lambda i: (i, 0))], riders=riders)


LOG2E = 1.4426950408889634
LN2 = 0.6931471805599453


def _decay_scale(lg_ref, idx, g, sign):
    return jnp.exp((sign * idx) * lg_ref[:, g * LANES:(g + 1) * LANES])


def _prep(proj, pos_col, idx_col, inv_freq, lg_lanes, tm):
    s = proj.shape[0]
    sb_off = (2 * RET_QK + 2 * RET_V) // SB_W
    n_q = RET_QK // LANES

    def body(qk_ref, qs_ref, pos_ref, idx_ref, f_ref, lg_ref, qk_out, qs_out, cos_out, sin_out):
        ang = pos_ref[...] * f_ref[...]
        lane = lax.broadcasted_iota(jnp.int32, (1, LANES), 1)
        first = jnp.bitwise_and(lane, RET_DQK - 1) < (RET_DQK // 2)
        cos = jnp.cos(ang)
        sin = jnp.where(first, -1.0, 1.0) * jnp.sin(ang)
        cos_out[...] = cos
        sin_out[...] = sin
        idx = idx_ref[...]
        for g in range(2 * n_q):
            v = qk_ref[:, g * LANES:(g + 1) * LANES].astype(F32)
            sw = jnp.where(first, pltpu.roll(v, LANES - RET_DQK // 2, 1), pltpu.roll(v, RET_DQK // 2, 1))
            r = v * cos + sw * sin
            if g < n_q:
                r = r * _decay_scale(lg_ref, idx, g, 1.0)
            else:
                r = r * (_decay_scale(lg_ref, idx, g - n_q, -1.0) * (RET_DQK ** -0.5))
            qk_out[:, g * LANES:(g + 1) * LANES] = r.astype(BF16)
        qs_out[...] = (qs_ref[...].astype(F32) * (SB_DH ** -0.5 * LOG2E)).astype(BF16)

    row = lambda i: (i, 0)
    return _call("prep", body, (s // tm,),
                 [(proj, (tm, 2 * RET_QK), row),
                  (proj, (tm, SB_W), lambda i: (i, sb_off)),
                  (pos_col, (tm, 1), row),
                  (idx_col, (tm, 1), row),
                  (inv_freq, (1, LANES), lambda i: (0, 0)),
                  (lg_lanes, (1, RET_QK), lambda i: (0, 0))],
                 [((s, 2 * RET_QK), BF16, (tm, 2 * RET_QK), row),
                  ((s, SB_W), BF16, (tm, SB_W), row),
                  ((s, LANES), F32, (tm, LANES), row),
                  ((s, LANES), F32, (tm, LANES), row)])


def _head_mask(hh):
    lane = lax.broadcasted_iota(jnp.int32, (1, LANES), 1)
    return (lane >= RET_DQK) if hh else (lane < RET_DQK)


def _masked(v, m):
    return jnp.where(m, v, jnp.zeros_like(v))


SB_GROUP = 4
SB_TQ = 512


def _stack_heads(v):
    return jnp.concatenate([_masked(v, _head_mask(0)), _masked(v, _head_mask(1))], axis=0)


def _side_by_side(v, t):
    return jnp.concatenate([v[:t], v[t:]], axis=1)


def _tile_pos(i, j, tq, tk):
    row = jnp.bitwise_and(lax.broadcasted_iota(jnp.int32, (2 * tq, tk), 0), tq - 1) + i * tq
    col = lax.broadcasted_iota(jnp.int32, (2 * tq, tk), 1) + j * tk
    return row, col


def _n_groups(i, tq, tk, grp):
    return ((i + 1) * (tq // tk) + grp - 1) // grp


def _n_full(i, tq, tk, grp):
    return (i * (tq // tk)) // grp


def _key_rows(j, tk):
    return pl.ds(pl.multiple_of(j * tk, tk), tk)


def _ret_weight(lg_rows, i, j, tq, tk):
    row, col = _tile_pos(i, j, tq, tk)
    same = jnp.right_shift(col, CHUNK_SHIFT) == jnp.right_shift(row, CHUNK_SHIFT)
    later = jnp.where(same, jnp.exp((2.0 * lg_rows) * (col - row).astype(F32)), 0.0)
    return jnp.where(col <= row, 1.0, later)


def _lg_rows(lg_ref, hp, tq):
    first = lax.broadcasted_iota(jnp.int32, (2 * tq, 1), 0) < tq
    return jnp.where(first, lg_ref[2 * hp], lg_ref[2 * hp + 1])


def _check_tiles(s, tq, tk, grp):
    assert tq % tk == 0 and tq & (tq - 1) == 0 and tk & (tk - 1) == 0
    assert s % tq == 0 and (s // tk) % grp == 0 and s // tk <= LANES


def _pair_mask():
    r = lax.broadcasted_iota(jnp.int32, (LANES, 2 * RET_DV), 0) >= RET_DQK
    c = lax.broadcasted_iota(jnp.int32, (LANES, 2 * RET_DV), 1) >= RET_DV
    return (r == c).astype(F32)


def _ret_block(lg_ref, hp, i, t, qb, kb):
    w = _ret_weight(_lg_rows(lg_ref, hp, t), i, i, t, t)
    return _dot(_stack_heads(qb), kb, _NT), w


RET_PAIRS = 4


def _lanes(ref, p, width):
    return ref[:, p * width:(p + 1) * width]


def _ret_fwd(qk_rot, proj, gn_g, log_gamma, t, riders=None):
    s = qk_rot.shape[0]
    n_pair = HEADS // 2
    pw = 2 * RET_DV
    wq, wv = RET_PAIRS * LANES, RET_PAIRS * pw
    v_off, gate_off = 2 * RET_QK // wv, (2 * RET_QK + RET_V) // wv
    assert s % t == 0 and t % CHUNK == 0 and t & (t - 1) == 0 and n_pair % RET_PAIRS == 0

    def body(lg_ref, q_ref, k_ref, v_ref, g_ref, w_ref, ret_ref, rg_ref, state_ref):
        hg, i = pl.program_id(0), pl.program_id(1)

        @pl.when(i == 0)
        def _():
            state_ref[...] = jnp.zeros_like(state_ref)

        pairs = range(RET_PAIRS)
        qbs = [_lanes(q_ref, p, LANES) for p in pairs]
        kbs = [_lanes(k_ref, p, LANES) for p in pairs]
        vbs = [_lanes(v_ref, p, pw) for p in pairs]
        zws = [_ret_block(lg_ref, hg * RET_PAIRS + p, i, t, qbs[p], kbs[p]) for p in pairs]
        ps = [(z * w).astype(BF16) for z, w in zws]
        outs = [jnp.concatenate([_dot(ps[p][:t], vbs[p][:, 0:RET_DV]), _dot(ps[p][t:], vbs[p][:, RET_DV:pw])], axis=1)
                + _dot(qbs[p], state_ref[p]) for p in pairs]
        for p in pairs:
            state_ref[p] += _pair_mask() * _dot(kbs[p], vbs[p], _TN)
        for p in pairs:
            for hh in range(2):
                cols = slice(p * pw + hh * RET_DV, p * pw + (hh + 1) * RET_DV)
                o = outs[p][:, hh * RET_DV:(hh + 1) * RET_DV]
                ret_ref[:, cols] = o
                mu = jnp.sum(o, axis=1, keepdims=True) * (1.0 / RET_DV)
                xc = o - mu
                var = jnp.sum(xc * xc, axis=1, keepdims=True) * (1.0 / RET_DV)
                nrm = xc * lax.rsqrt(var + EPS) * w_ref[:, cols]
                g = g_ref[:, cols].astype(F32)
                rg_ref[:, cols] = (g * _sigmoid(g) * nrm).astype(BF16)

    blk = lambda hg, i: (i, hg)
    return _call("ret_fwd", body, (n_pair // RET_PAIRS, s // t),
                 [(log_gamma, None, pltpu.SMEM),
                  (qk_rot, (t, wq), blk),
                  (qk_rot, (t, wq), lambda hg, i: (i, n_pair // RET_PAIRS + hg)),
                  (proj, (t, wv), lambda hg, i: (i, v_off + hg)),
                  (proj, (t, wv), lambda hg, i: (i, gate_off + hg)),
                  (gn_g, (1, wv), lambda hg, i: (0, hg))],
                 [((s, RET_V), F32, (t, wv), blk), ((s, RET_V), BF16, (t, wv), blk)],
                 scratch=[pltpu.VMEM((RET_PAIRS, LANES, pw), F32)], riders=riders)


def _tri(tk, strict_upper):
    r = lax.broadcasted_iota(jnp.int32, (tk, tk), 0)
    cc = lax.broadcasted_iota(jnp.int32, (tk, tk), 1)
    return ((r > cc) if strict_upper else (r < cc)).astype(BF16)


def _diagonal_step(i, tq, tk, make, carry):
    if (tq // tk) % SB_GROUP == 0:
        return make(SB_GROUP)(0, carry)
    assert 2 * (tq // tk) == SB_GROUP
    half = lax.rem(i, 2) == 0
    return lax.cond(half, lambda cr: make(SB_GROUP // 2)(0, cr), lambda cr: make(SB_GROUP)(0, cr), carry)


def _sb_valid(i, j, tq, tk):
    row, col = _tile_pos(i, j, tq, tk)
    return col < row


def _sb_fwd(q_sb, proj, tq, tk, riders=None):
    s = q_sb.shape[0]
    k_off = (2 * RET_QK + 2 * RET_V + SB_W) // LANES
    n_pair = HEADS // 2
    _check_tiles(s, tq, tk, SB_GROUP)

    def body(q_ref, k_ref, v_ref, o_ref, a_ref):
        i = pl.program_id(1)
        upper = _tri(tk, True)
        qs = _stack_heads(q_ref[...])
        n_full, n_groups = _n_full(i, tq, tk, SB_GROUP), _n_groups(i, tq, tk, SB_GROUP)

        def make_step(near_diagonal, last, n_sub=SB_GROUP):
            def step(n, carry):
                c, o = carry
                g = last - 1 - n
                js = [g * SB_GROUP + sub for sub in range(n_sub)]
                zs = [_dot(qs, k_ref[_key_rows(j, tk), :], _NT) for j in js]
                log1ps = [jnp.log2(1.0 + jnp.exp2(-jnp.abs(z))) for z in zs]
                log_1ms = [-jnp.maximum(z, 0.0) - t for z, t in zip(zs, log1ps)]
                log_bs = [jnp.minimum(z, 0.0) - t for z, t in zip(zs, log1ps)]
                if near_diagonal:
                    valids = [_sb_valid(i, j, tq, tk) for j in js]
                    log_1ms = [jnp.where(v, l, 0.0) for v, l in zip(valids, log_1ms)]
                sticks = [_dot(l, upper) for l in log_1ms]
                sums = [jnp.sum(l, axis=1, keepdims=True) for l in log_1ms]
                cs = [None] * n_sub
                for sub in reversed(range(n_sub)):
                    cs[sub] = c
                    c = c + sums[sub]
                for sub, j in enumerate(js):
                    a = jnp.exp2(log_bs[sub] + sticks[sub] + cs[sub])
                    if near_diagonal:
                        a = jnp.where(valids[sub], a, 0.0)
                    a = a.astype(BF16)
                    a_ref[j] = a
                    o = o + _dot(_side_by_side(a, tq), _stack_heads(v_ref[_key_rows(j, tk), :]))
                return c, o
            return step

        carry = (jnp.zeros((2 * tq, 1), F32), jnp.zeros((tq, LANES), F32))
        carry = _diagonal_step(i, tq, tk, lambda n_sub: make_step(True, n_groups, n_sub), carry)
        _, acc = lax.fori_loop(0, n_full, make_step(False, n_full), carry)
        o_ref[...] = acc.astype(BF16)

    n_kb = s // tk
    return _call("sb_fwd", body, (n_pair, s // tq),
                 [(q_sb, (tq, LANES), lambda hp, i: (i, hp)),
                  (proj, (s, LANES), lambda hp, i: (0, k_off + hp)),
                  (proj, (s, LANES), lambda hp, i: (0, k_off + n_pair + hp))],
                 [((s, SB_W), BF16, (tq, LANES), lambda hp, i: (i, hp)),
                  ((n_pair, s // tq, n_kb, 2 * tq, tk), BF16, (None, None, n_kb, 2 * tq, tk),
                   lambda hp, i: (hp, i, 0, 0, 0))], riders=riders)


def _merge_out(retg, sb, w_ret, w_sb_t, w_out, proj, x, mod, gp1, g2, tm):
    s, d = x.shape
    gw = min(512, d)
    n_g = d // gw
    ar_off = (2 * RET_QK + 2 * RET_V + 3 * SB_W) // gw

    def body(rg_ref, sb_ref, wr_ref, ws_ref, wo_ref, *refs):
        gate_refs, (x_ref, mod_ref, gp_ref, g2_ref, mix_ref, r_ref, s_ref, y_ref, hres_ref, h2_ref) = refs[:2 * n_g], refs[2 * n_g:]
        for rows in _pieces(tm):
            rr = _dot(rg_ref[rows, :], wr_ref[...])
            ss = _dot(sb_ref[rows, :], ws_ref[...], _NT)
            a_r = jnp.concatenate([g[rows, :] for g in gate_refs[:n_g]], axis=1).astype(F32)
            a_s = jnp.concatenate([g[rows, :] for g in gate_refs[n_g:]], axis=1).astype(F32)
            mixed = (_sigmoid(a_r) * rr + _sigmoid(a_s) * ss).astype(BF16)
            mix_ref[rows, :] = mixed
            r_ref[rows, :] = rr.astype(BF16)
            s_ref[rows, :] = ss.astype(BF16)
            y = _dot(mixed, wo_ref[...])
            y_ref[rows, :] = y
            ny, _ = _rms(y, d)
            hres = x_ref[rows, :] + mod_ref[:, 2 * d:3 * d] * (ny * gp_ref[...])
            hres_ref[rows, :] = hres
            n2, _ = _rms(hres, d)
            h2_ref[rows, :] = (n2 * g2_ref[...] * (1.0 + mod_ref[:, 4 * d:5 * d]) + mod_ref[:, 3 * d:4 * d]).astype(BF16)

    row = lambda i: (i, 0)
    fix = lambda i: (0, 0)
    tile_bf = ((s, d), BF16, (tm, d), row)
    tile_f = ((s, d), F32, (tm, d), row)
    return _call("merge_out", body, (s // tm,),
                 [(retg, (tm, RET_V), row), (sb, (tm, SB_W), row), (w_ret, (RET_V, d), fix), (w_sb_t, (d, SB_W), fix),
                  (w_out, (d, d), fix)]
                 + [(proj, (tm, gw), functools.partial(lambda i, k: (i, ar_off + k), k=k)) for k in range(2 * n_g)]
                 + [(x, (tm, d), row), (mod, (1, 6 * d), fix), (gp1, (1, d), fix), (g2, (1, d), fix)],
                 [tile_bf, tile_bf, tile_bf, tile_f, tile_f, tile_bf])


def _ff1(h2, w_ff1_t, tm, tn):
    s, f = h2.shape[0], w_ff1_t.shape[0]
    tm = min(tm, s)

    def body(a_ref, w_ref, u_ref, act_ref):
        u = _dot(a_ref[...], w_ref[...], _NT)
        r = jnp.maximum(u, 0.0)
        u_ref[...] = u.astype(BF16)
        act_ref[...] = (r * r).astype(BF16)

    d = h2.shape[1]
    return _call("ff1", body, (f // tn, s // tm),
                 [(h2, (tm, d), lambda j, i: (i, 0)), (w_ff1_t, (tn, d), lambda j, i: (j, 0))],
                 [((s, f), BF16, (tm, tn), lambda j, i: (i, j))] * 2)


def _ff2_loss(act, w_ff2, hres, target, mod, gp2, tm):
    s, d = hres.shape
    f = act.shape[1]

    def body(a_ref, w_ref, h_ref, t_ref, mod_ref, gp_ref, dout_ref, df_ref, loss_ref, dgt_ref, dgp_ref):
        _zero_at_start([loss_ref, dgt_ref, dgp_ref])
        gt, gp = mod_ref[:, 5 * d:6 * d], gp_ref[...]
        for rows in _pieces(tm):
            ff = _dot(a_ref[rows, :], w_ref[...])
            nf, rf = _rms(ff, d)
            out = h_ref[rows, :] + gt * (nf * gp)
            err = out - t_ref[rows, :]
            sq = jnp.sum(err * err, axis=1, keepdims=True)
            loss_ref[...] += jnp.sum(sq, axis=0, keepdims=True)
            dout = err * (1.0 / d)
            dout_ref[rows, :] = dout
            dgt_ref[...] += _colsum(dout * (nf * gp))
            dgp_ref[...] += _colsum(dout * gt * nf)
            df_ref[rows, :] = _rms_bwd(dout * gt * gp, nf, rf, d).astype(BF16)

    row = lambda i: (i, 0)
    fix = lambda i: (0, 0)
    return _call("ff2_loss", body, (s // tm,),
                 [(act, (tm, f), row), (w_ff2, (f, d), fix), (hres, (tm, d), row), (target, (tm, d), row),
                  (mod, (1, 6 * d), fix), (gp2, (1, d), fix)],
                 [((s, d), F32, (tm, d), row), ((s, d), BF16, (tm, d), row), ((1, 1), F32, (1, 1), fix),
                  ((1, d), F32, (1, d), fix), ((1, d), F32, (1, d), fix)])


def _ffn_bwd(df, w_ff2, u, act, h2, tn):
    s, d = df.shape
    f = w_ff2.shape[0]

    def body(df_ref, w_ref, u_ref, act_ref, h2_ref, du_ref, gw2_ref, gw1_ref):
        dfb = df_ref[...]
        du = (_dot(dfb, w_ref[...], _NT) * (2.0 * jnp.maximum(u_ref[...].astype(F32), 0.0))).astype(BF16)
        du_ref[...] = du
        gw2_ref[...] = _dot(act_ref[...], dfb, _TN).astype(BF16)
        gw1_ref[...] = _dot(h2_ref[...], du, _TN).astype(BF16)

    fix = lambda j: (0, 0)
    col = lambda j: (0, j)
    return _call("ffn_bwd", body, (f // tn,),
                 [(df, (s, d), fix), (w_ff2, (tn, d), lambda j: (j, 0)), (u, (s, tn), col), (act, (s, tn), col),
                  (h2, (s, d), fix)],
                 [((s, f), BF16, (s, tn), col), ((f, d), BF16, (tn, d), lambda j: (j, 0)),
                  ((f // tn, d, tn), BF16, (None, d, tn), lambda j: (j, 0, 0))])


def _ff1_bwd(du, w_ff1_t, hres, dout, y, mixed, mod, g2, gp1, tm, riders=None):
    s, d = hres.shape
    f = du.shape[1]

    def body(a_ref, w_ref, h_ref, do_ref, y_ref, mix_ref, mod_ref, g2_ref, gp_ref,
             dh_ref, dy_ref, dsh_ref, dsc_ref, dg2_ref, dgt_ref, dgp_ref, gwo_ref, acc_ref):
        _zero_at_start([dsh_ref, dsc_ref, dg2_ref, dgt_ref, dgp_ref, acc_ref])
        g2, sc2 = g2_ref[...], mod_ref[:, 4 * d:5 * d]
        gt, gp = mod_ref[:, 2 * d:3 * d], gp_ref[...]
        for rows in _pieces(tm):
            dh2 = _dot(a_ref[rows, :], w_ref[...])
            n2, r2 = _rms(h_ref[rows, :], d)
            dsh_ref[...] += _colsum(dh2)
            dsc_ref[...] += _colsum(dh2 * n2 * g2)
            dg2_ref[...] += _colsum(dh2 * n2 * (1.0 + sc2))
            dhres = do_ref[rows, :] + _rms_bwd(dh2 * g2 * (1.0 + sc2), n2, r2, d)
            dh_ref[rows, :] = dhres
            ny, ry = _rms(y_ref[rows, :], d)
            dgt_ref[...] += _colsum(dhres * (ny * gp))
            dgp_ref[...] += _colsum(dhres * gt * ny)
            dy_ref[rows, :] = _rms_bwd(dhres * gt * gp, ny, ry, d).astype(BF16)
        acc_ref[...] += _dot(mix_ref[...], dy_ref[...], _TN)

        @pl.when(pl.program_id(0) == s // tm - 1)
        def _():
            gwo_ref[...] = acc_ref[...].astype(BF16)

    row = lambda i: (i, 0)
    fix = lambda i: (0, 0)
    vec = ((1, d), F32, (1, d), fix)
    return _call("ff1_bwd", body, (s // tm,),
                 [(du, (tm, f), row), (w_ff1_t, (f, d), fix), (hres, (tm, d), row), (dout, (tm, d), row),
                  (y, (tm, d), row), (mixed, (tm, d), row), (mod, (1, 6 * d), fix), (g2, (1, d), fix), (gp1, (1, d), fix)],
                 [((s, d), F32, (tm, d), row), ((s, d), BF16, (tm, d), row), vec, vec, vec, vec, vec,
                  ((d, d), BF16, (d, d), fix)], scratch=[pltpu.VMEM((d, d), F32)], riders=riders)


def _out_bwd(dy, w_out, proj, r_bf, s_bf, tm, tn, riders=None):
    s, d = dy.shape
    ar_off = (2 * RET_QK + 2 * RET_V + 3 * SB_W) // tn
    as_off = ar_off + d // tn

    def body(a_ref, w_ref, ar_ref, as_ref, r_ref, s_ref, dr_ref, ds_ref, dar_ref, das_ref):
        dm = _dot(a_ref[...], w_ref[...], _NT)
        sr, ss = _sigmoid(ar_ref[...].astype(F32)), _sigmoid(as_ref[...].astype(F32))
        dr_ref[...] = (dm * sr).astype(BF16)
        ds_ref[...] = (dm * ss).astype(BF16)
        dar_ref[...] = (dm * r_ref[...].astype(F32) * sr * (1.0 - sr)).astype(BF16)
        das_ref[...] = (dm * s_ref[...].astype(F32) * ss * (1.0 - ss)).astype(BF16)

    tile = (tm, tn)
    here = lambda j, i: (i, j)
    return _call("out_bwd", body, (d // tn, s // tm),
                 [(dy, (tm, d), lambda j, i: (i, 0)), (w_out, (tn, d), lambda j, i: (j, 0)),
                  (proj, tile, lambda j, i: (i, ar_off + j)), (proj, tile, lambda j, i: (i, as_off + j)),
                  (r_bf, tile, here), (s_bf, tile, here)],
                 [((s, d), BF16, tile, here)] * 4, riders=riders)


def _branch_bwd(d_r, d_s, retg, sb, w_ret, w_sb_t, ret, proj, gn_g, riders=None):
    s, d = d_r.shape
    n_step = 4
    part_v, part_s, part_d = RET_V // n_step, SB_W // n_step, d // n_step
    per_dev = d // N_DEV
    n_blk = part_d // per_dev
    gate_off = (2 * RET_QK + RET_V) // part_v

    def body(dr_ref, ds_ref, rg_ref, sb_ref, wr_ref, ws_ref, r_ref, g_ref, w_ref,
             dsb_ref, gwr_ref, gws_ref, dg_ref, dret_ref, dw_ref):
        i = pl.program_id(0)
        dr, ds = dr_ref[...], ds_ref[...]
        dsb_ref[...] = _dot(ds, ws_ref[...]).astype(BF16)
        gwr_ref[...] = _dot(rg_ref[...], dr, _TN).astype(BF16)
        cols = pl.ds(pl.multiple_of(i * part_d, part_d), part_d)
        gws = _dot(sb_ref[...], ds_ref[:, cols], _TN).astype(BF16)
        for k in range(n_blk):
            gws_ref[k] = gws[:, k * per_dev:(k + 1) * per_dev]
        dretg = _dot(dr, wr_ref[...], _NT)
        for h in range(part_v // RET_DV):
            cols = slice(h * RET_DV, (h + 1) * RET_DV)
            o, g, w, d_o = r_ref[:, cols], g_ref[:, cols].astype(F32), w_ref[:, cols], dretg[:, cols]
            mu = jnp.sum(o, axis=1, keepdims=True) * (1.0 / RET_DV)
            xc = o - mu
            rstd = lax.rsqrt(jnp.sum(xc * xc, axis=1, keepdims=True) * (1.0 / RET_DV) + EPS)
            n = xc * rstd
            sg = _sigmoid(g)
            silu = g * sg
            dg_ref[:, cols] = (d_o * n * w * (sg * (1.0 + g * (1.0 - sg)))).astype(BF16)
            dw_ref[:, cols] = _colsum(d_o * silu * n)
            dn = d_o * silu * w
            m1 = jnp.sum(dn, axis=1, keepdims=True) * (1.0 / RET_DV)
            m2 = jnp.sum(dn * n, axis=1, keepdims=True) * (1.0 / RET_DV)
            dret_ref[:, cols] = (rstd * (dn - m1 - n * m2)).astype(BF16)

    fix = lambda i: (0, 0)
    col = lambda i: (0, i)
    return _call("branch_bwd", body, (n_step,),
                 [(d_r, (s, d), fix), (d_s, (s, d), fix), (retg, (s, part_v), col), (sb, (s, SB_W), fix),
                  (w_ret, (part_v, d), lambda i: (i, 0)), (w_sb_t, (d, part_s), col),
                  (ret, (s, part_v), col), (proj, (s, part_v), lambda i: (0, gate_off + i)), (gn_g, (1, part_v), col)],
                 [((s, SB_W), BF16, (s, part_s), col), ((RET_V, d), BF16, (part_v, d), lambda i: (i, 0)),
                  ((N_DEV, SB_W, per_dev), BF16, (n_blk, SB_W, per_dev), lambda i: (i, 0, 0)),
                  ((s, RET_V), BF16, (s, part_v), col), ((s, RET_V), BF16, (s, part_v), col),
                  ((1, RET_V), F32, (1, part_v), col)], riders=riders)


def _ret_bwd(qk_rot, proj, dret, log_gamma, t, riders=None):
    s = qk_rot.shape[0]
    n_pair = HEADS // 2
    pw = 2 * RET_DV
    wq, wv = RET_PAIRS * LANES, RET_PAIRS * pw
    n_blk = s // t
    pairs = range(RET_PAIRS)

    def load(q_ref, k_ref, v_ref, do_ref):
        return ([_lanes(q_ref, p, LANES) for p in pairs], [_lanes(k_ref, p, LANES) for p in pairs],
                [_lanes(v_ref, p, pw) for p in pairs], [_lanes(do_ref, p, pw) for p in pairs])

    def d_scores(lg_ref, hp, i, qb, kb, vb, dob):
        z, w = _ret_block(lg_ref, hp, i, t, qb, kb)
        dp = jnp.concatenate([_dot(dob[:, 0:RET_DV], vb[:, 0:RET_DV], _NT),
                              _dot(dob[:, RET_DV:pw], vb[:, RET_DV:pw], _NT)], axis=0)
        return (z * w).astype(BF16), (dp * w).astype(BF16)

    def up_body(lg_ref, q_ref, k_ref, v_ref, do_ref, dq_ref, state_ref):
        hg, i = pl.program_id(0), pl.program_id(1)

        @pl.when(i == 0)
        def _():
            state_ref[...] = jnp.zeros_like(state_ref)

        qbs, kbs, vbs, dobs = load(q_ref, k_ref, v_ref, do_ref)
        dss = [d_scores(lg_ref, hg * RET_PAIRS + p, i, qbs[p], kbs[p], vbs[p], dobs[p])[1] for p in pairs]
        for p in pairs:
            dq_ref[:, p * LANES:(p + 1) * LANES] = (_dot(_side_by_side(dss[p], t), _stack_heads(kbs[p]))
                                                    + _dot(dobs[p], state_ref[p], _NT)).astype(BF16)
        for p in pairs:
            state_ref[p] += _pair_mask() * _dot(kbs[p], vbs[p], _TN)

    def down_body(lg_ref, q_ref, k_ref, v_ref, do_ref, dk_ref, dv_ref, state_ref):
        hg, i = pl.program_id(0), n_blk - 1 - pl.program_id(1)

        @pl.when(pl.program_id(1) == 0)
        def _():
            state_ref[...] = jnp.zeros_like(state_ref)

        qbs, kbs, vbs, dobs = load(q_ref, k_ref, v_ref, do_ref)
        both = [d_scores(lg_ref, hg * RET_PAIRS + p, i, qbs[p], kbs[p], vbs[p], dobs[p]) for p in pairs]
        for p in pairs:
            pp, ds = both[p]
            later = state_ref[p]
            dv_ref[:, p * pw:(p + 1) * pw] = (jnp.concatenate(
                [_dot(pp[:t], dobs[p][:, 0:RET_DV], _TN), _dot(pp[t:], dobs[p][:, RET_DV:pw], _TN)],
                axis=1) + _dot(kbs[p], later)).astype(BF16)
            dk_ref[:, p * LANES:(p + 1) * LANES] = (_dot(ds, _stack_heads(qbs[p]), _TN)
                                                    + _dot(vbs[p], later, _NT)).astype(BF16)
        for p in pairs:
            state_ref[p] += _pair_mask() * _dot(qbs[p], dobs[p], _TN)

    n_grp = n_pair // RET_PAIRS

    def ins(order):
        return [(log_gamma, None, pltpu.SMEM),
                (qk_rot, (t, wq), lambda hg, i: (order(i), hg)),
                (qk_rot, (t, wq), lambda hg, i: (order(i), n_grp + hg)),
                (proj, (t, wv), lambda hg, i: (order(i), 2 * RET_QK // wv + hg)),
                (dret, (t, wv), lambda hg, i: (order(i), hg))]

    up = lambda i: i
    down = lambda i: n_blk - 1 - i
    scratch = [pltpu.VMEM((RET_PAIRS, LANES, pw), F32)]
    dq = _call("ret_bwd_q", up_body, (n_grp, n_blk), ins(up),
               [((s, RET_QK), BF16, (t, wq), lambda hg, i: (i, hg))], scratch=scratch)[0]
    dk, dv, *rest = _call("ret_bwd_kv", down_body, (n_grp, n_blk), ins(down),
                          [((s, RET_QK), BF16, (t, wq), lambda hg, i: (down(i), hg)),
                           ((s, RET_V), BF16, (t, wv), lambda hg, i: (down(i), hg))],
                          scratch=scratch, riders=riders)
    return [dq, dk, dv] + rest


def _sb_bwd(q_sb, proj, weights, do, tq, tk, riders=None):
    s = q_sb.shape[0]
    k_off = (2 * RET_QK + 2 * RET_V + SB_W) // LANES
    n_pair = HEADS // 2
    _check_tiles(s, tq, tk, SB_GROUP)

    def body(q_ref, k_ref, v_ref, a_ref, do_ref, dq_ref, dk_ref, dv_ref):
        i = pl.program_id(1)

        @pl.when(i == 0)
        def _():
            dk_ref[...] = jnp.zeros_like(dk_ref)
            dv_ref[...] = jnp.zeros_like(dv_ref)

        lower = _tri(tk, False)
        qs = _stack_heads(q_ref[...])
        dos = _stack_heads(do_ref[...].astype(BF16))

        def make_step(near_diagonal, n_sub=SB_GROUP):
            def step(g, carry):
                c_e, dq = carry
                js = [g * SB_GROUP + sub for sub in range(n_sub)]
                rows = [_key_rows(j, tk) for j in js]
                zs = [_dot(qs, k_ref[rw, :], _NT) for rw in rows]
                das = [_dot(dos, v_ref[rw, :], _NT) for rw in rows]
                avals = [a_ref[j] for j in js]
                for a, rw in zip(avals, rows):
                    dv_ref[rw, :] += _dot(a, dos, _TN)
                es = [a.astype(F32) * da for a, da in zip(avals, das)]
                prefixes = [_dot(e, lower) for e in es]
                betas = [1.0 / (1.0 + jnp.exp2(-z)) for z in zs]
                for sub in range(n_sub):
                    dz = es[sub] - (es[sub] + prefixes[sub] + c_e) * betas[sub]
                    if near_diagonal:
                        dz = jnp.where(_sb_valid(i, js[sub], tq, tk), dz, 0.0)
                    dz = dz.astype(BF16)
                    dk_ref[rows[sub], :] += _dot(dz, qs, _TN)
                    dq = dq + _dot(_side_by_side(dz, tq), _stack_heads(k_ref[rows[sub], :]))
                    c_e = c_e + jnp.sum(es[sub], axis=1, keepdims=True)
                return c_e, dq
            return step

        n_full = _n_full(i, tq, tk, SB_GROUP)
        carry = (jnp.zeros((2 * tq, 1), F32), jnp.zeros((tq, LANES), F32))
        carry = lax.fori_loop(0, n_full, make_step(False), carry)
        _, dq = _diagonal_step(i, tq, tk, lambda n_sub: (lambda n, cr: make_step(True, n_sub)(n_full, cr)), carry)
        dq_ref[...] = dq

    blk = lambda hp, i: (i, hp)
    n_kb = s // tk
    return _call("sb_bwd", body, (n_pair, s // tq),
                 [(q_sb, (tq, LANES), blk),
                  (proj, (s, LANES), lambda hp, i: (0, k_off + hp)),
                  (proj, (s, LANES), lambda hp, i: (0, k_off + n_pair + hp)),
                  (weights, (None, None, n_kb, 2 * tq, tk), lambda hp, i: (hp, i, 0, 0, 0)),
                  (do, (tq, LANES), blk)],
                 [((s, SB_W), F32, (tq, LANES), blk),
                  ((s, SB_W), F32, (s, LANES), lambda hp, i: (0, hp)),
                  ((s, SB_W), F32, (s, LANES), lambda hp, i: (0, hp))], riders=riders)


def _assemble_dproj(dq_r, dk_r, dv_r, dg_r, dq_s, dk_s, dv_s, da_r, da_s, cos, sin, idx_col, lg_lanes, tm, riders=None):
    s, d = da_r.shape
    width = 2 * RET_QK + 2 * RET_V + 3 * SB_W + 2 * d

    def body(dq_ref, dk_ref, dv_ref, dg_ref, dqs_ref, dks_ref, dvs_ref, dar_ref, das_ref, cos_ref, sin_ref,
             idx_ref, lg_ref, o_ref):
        lane = lax.broadcasted_iota(jnp.int32, (1, LANES), 1)
        first = jnp.bitwise_and(lane, RET_DQK - 1) < (RET_DQK // 2)
        cos, sin = cos_ref[...], sin_ref[...]
        idx = idx_ref[...]
        for src, base, sign, scale in ((dq_ref, 0, 1.0, 1.0), (dk_ref, RET_QK, -1.0, RET_DQK ** -0.5)):
            for g in range(RET_QK // LANES):
                v = src[:, g * LANES:(g + 1) * LANES].astype(F32) * (_decay_scale(lg_ref, idx, g, sign) * scale)
                sw = jnp.where(first, pltpu.roll(v, LANES - RET_DQK // 2, 1), pltpu.roll(v, RET_DQK // 2, 1))
                o_ref[:, base + g * LANES:base + (g + 1) * LANES] = (v * cos - sw * sin).astype(BF16)
        off = 2 * RET_QK
        o_ref[:, off:off + RET_V] = dv_ref[...].astype(BF16)
        off += RET_V
        o_ref[:, off:off + RET_V] = dg_ref[...]
        off += RET_V
        o_ref[:, off:off + SB_W] = (dqs_ref[...] * (SB_DH ** -0.5)).astype(BF16)
        off += SB_W
        o_ref[:, off:off + SB_W] = (dks_ref[...] * LN2).astype(BF16)
        off += SB_W
        o_ref[:, off:off + SB_W] = dvs_ref[...].astype(BF16)
        off += SB_W
        o_ref[:, off:off + d] = dar_ref[...]
        off += d
        o_ref[:, off:off + d] = das_ref[...]

    row = lambda i: (i, 0)
    ins = [(a, (tm, a.shape[1]), row) for a in (dq_r, dk_r, dv_r, dg_r, dq_s, dk_s, dv_s, da_r, da_s, cos, sin, idx_col)]
    ins.append((lg_lanes, (1, RET_QK), lambda i: (0, 0)))
    return _call("assemble_dproj", body, (s // tm,), ins, [((s, width), BF16, (tm, width), row)], riders=riders)


def _in_bwd(dproj, w_in_t, x, dhres, mod, g1, tm, riders=None):
    s, d = x.shape
    width = dproj.shape[1]

    def body(a_ref, w_hbm, x_ref, dh_ref, mod_ref, g_ref, dx_ref, dsh_ref, dsc_ref, dg_ref, w_ref, w_sem):
        _zero_at_start([dsh_ref, dsc_ref, dg_ref])

        @pl.when(pl.program_id(0) == 0)
        def _():
            fetch = pltpu.make_async_copy(w_hbm, w_ref, w_sem)
            fetch.start()
            fetch.wait()

        g1, sc1 = g_ref[...], mod_ref[:, d:2 * d]
        for rows in _pieces(tm):
            dh = _dot(a_ref[rows, :], w_ref[...])
            n1, r1 = _rms(x_ref[rows, :], d)
            dsh_ref[...] += _colsum(dh)
            dsc_ref[...] += _colsum(dh * n1 * g1)
            dg_ref[...] += _colsum(dh * n1 * (1.0 + sc1))
            dx_ref[rows, :] = dh_ref[rows, :] + _rms_bwd(dh * g1 * (1.0 + sc1), n1, r1, d)

    row = lambda i: (i, 0)
    fix = lambda i: (0, 0)
    vec = ((1, d), F32, (1, d), fix)
    return _call("in_bwd", body, (s // tm,),
                 [(dproj, (tm, width), row), (w_in_t, None, pl.ANY), (x, (tm, d), row), (dhres, (tm, d), row),
                  (mod, (1, 6 * d), fix), (g1, (1, d), fix)],
                 [((s, d), F32, (tm, d), row), vec, vec, vec],
                 scratch=[pltpu.VMEM((width, d), w_in_t.dtype), pltpu.SemaphoreType.DMA], riders=riders)


def _adamw(w, g, m, v):
    m = ADAM_B1 * m + (1.0 - ADAM_B1) * g
    v = ADAM_B2 * v + (1.0 - ADAM_B2) * (g * g)
    m_hat = m / (1.0 - ADAM_B1 ** ADAM_STEP)
    v_hat = v / (1.0 - ADAM_B2 ** ADAM_STEP)
    delta = -ADAM_LR * (m_hat / (jnp.sqrt(v_hat) + ADAM_EPS) + ADAM_WD * w)
    return delta, m, v


def _adam_reduce(name, sets, steps):
    n = len(sets)

    def body(*refs):
        for k in range(n):
            p_ref, w_ref, m_ref, v_ref = refs[4 * k:4 * k + 4]
            outs = refs[4 * n + 4 * k:4 * n + 4 * k + 4]
            g = p_ref[0].astype(F32)
            for j in range(1, p_ref.shape[0]):
                g = g + p_ref[j].astype(F32)
            for o_ref, val in zip(outs, (g,) + _adamw(w_ref[...], g, m_ref[...], v_ref[...])):
                o_ref[...] = val

    ins, outs = [], []
    row = lambda i: (i, 0)
    for parts, w, m, v in sets:
        rws, cls = w.shape
        tr = rws // steps
        assert tr * steps == rws and tr % 16 == 0
        ins += [(parts, (parts.shape[0], tr, cls), lambda i: (0, i, 0)), (w, (tr, cls), row), (m, (tr, cls), row),
                (v, (tr, cls), row)]
        outs += [((rws, cls), F32, (tr, cls), row)] * 4
    res = _call(name, body, (steps,), ins, outs)
    return [res[4 * k:4 * k + 4] for k in range(n)]


def _ada_bwd_adam(cs_t, dmod_cols, w, m, v, tr):
    d, nc = w.shape

    def body(c_ref, dm_ref, w_ref, m_ref, v_ref, g_out, d_out, m_out, v_out):
        g = c_ref[0] * dm_ref[0:1, :]
        for r in range(1, N_DEV):
            g = g + c_ref[r] * dm_ref[r:r + 1, :]
        delta, mn, vn = _adamw(w_ref[...], g, m_ref[...], v_ref[...])
        g_out[...] = g
        d_out[...] = delta
        m_out[...] = mn
        v_out[...] = vn

    row = lambda i: (i, 0)
    blk = (tr, nc)
    return _call("ada_bwd_adam", body, (d // tr,),
                 [(cs_t, (N_DEV, tr, 1), lambda i: (0, i, 0)), (dmod_cols, (N_DEV, nc), lambda i: (0, 0)),
                  (w, blk, row), (m, blk, row), (v, blk, row)],
                 [((d, nc), F32, blk, row)] * 4)


def _small_adam(parts, ws, ms, vs):
    n = len(ws)
    widths = [w.shape[1] for w in ws]
    total = parts.shape[1]
    assert sum(widths) + LANES == total

    def body(p_ref, *refs):
        w_refs, m_refs, v_refs = refs[:n], refs[n:2 * n], refs[2 * n:3 * n]
        outs = refs[3 * n:]
        g = p_ref[0:1, :]
        for k in range(1, N_DEV):
            g = g + p_ref[k:k + 1, :]
        off = 0
        for i, width in enumerate(widths):
            gi = g[:, off:off + width]
            delta, mn, vn = _adamw(w_refs[i][...], gi, m_refs[i][...], v_refs[i][...])
            for o_ref, val in zip(outs[4 * i:4 * i + 4], (gi, delta, mn, vn)):
                o_ref[...] = val
            off += width
        outs[4 * n][...] = g[:, off:off + LANES]

    fix = lambda i: (0, 0)
    vec = lambda a: (a, (1, a.shape[1]), fix)
    out_specs = [((1, width), F32, (1, width), fix) for width in widths for _ in range(4)]
    out_specs.append(((1, LANES), F32, (1, LANES), fix))
    res = _call("small_adam", body, (1,),
                [(parts, (N_DEV, total), fix)] + [vec(a) for a in list(ws) + list(ms) + list(vs)], out_specs)
    return [res[4 * i:4 * i + 4] for i in range(n)], res[4 * n]


def kernel(x, c, positions, ada_w, ada_b, pre_mix_g, post_mix_g, pre_ffn_g, post_ffn_g, w_in, ret_gn_g, w_ret_branch, w_sb_branch, w_out, w_ff1, w_ff2, loss_target, m_ada_w, m_ada_b, m_pre_mix_g, m_post_mix_g, m_pre_ffn_g, m_post_ffn_g, m_w_in, m_ret_gn_g, m_w_ret_branch, m_w_sb_branch, m_w_out, m_w_ff1, m_w_ff2, v_ada_w, v_ada_b, v_pre_mix_g, v_post_mix_g, v_pre_ffn_g, v_post_ffn_g, v_w_in, v_ret_gn_g, v_w_ret_branch, v_w_sb_branch, v_w_out, v_w_ff1, v_w_ff2):
    _, s, d = x.shape
    d_ff = w_ff1.shape[2] * N_DEV
    d_in = w_in.shape[2] * N_DEV
    me = 4 * lax.axis_index("x") + 2 * lax.axis_index("y") + lax.axis_index("c")
    x2, tgt = x[0], loss_target[0]

    core = lax.axis_index("c").astype(jnp.int32).reshape(1)
    bf = lambda w: w[0].astype(BF16)

    w_in_t, m_in_t, v_in_t = (jnp.swapaxes(a[0], 0, 1) for a in (w_in, m_w_in, v_w_in))

    c_all, g_in = _exchange("gather_in", [c, w_in_t.astype(BF16)], ["gather", "gather_chip"])
    c_all = c_all.reshape(N_DEV, d)

    n_ada = ada_w.shape[2]
    ada_b_cols = lax.dynamic_slice(ada_b, (0, me * n_ada), (1, n_ada))
    cs_all, mod_cols = _ada_fwd(c_all, ada_w[0], ada_b_cols)
    mod_all, g_in = _exchange("gather_mod", [mod_cols, g_in], ["gather", "forward"])
    mod = lax.dynamic_index_in_dim(mod_all, me, axis=1, keepdims=False).reshape(1, 6 * d)

    tm = min(256, s)
    h = _pre_norm(x2, pre_mix_g, mod, 2 * tm)[0]
    wt_in = g_in.reshape(d_in, d)
    bf_t = lambda w: jnp.swapaxes(w[0], 0, 1).astype(BF16)
    small_w = [bf(w_ret_branch), bf_t(w_sb_branch), bf(w_out)]
    proj, *small_w = _matmul("in_proj", h, wt_in, "nt", s, 512, BF16, riders=(small_w, ["gather_chip"] * 3))
    pos_col = positions.reshape(s, 1).astype(F32)
    freqs = ROPE_BASE ** (-jnp.arange(0, RET_DQK, 2, dtype=F32) / RET_DQK)
    inv_freq = jnp.tile(freqs, LANES // (RET_DQK // 2)).reshape(1, LANES)
    log_gamma_np = np.log1p(-(2.0 ** (-5.0 - np.arange(HEADS))))
    log_gamma = jnp.asarray(log_gamma_np, F32)
    lg_lanes = jnp.asarray(np.repeat(log_gamma_np, RET_DQK).reshape(1, RET_QK), F32)
    idx_col = (jnp.arange(s, dtype=F32) - (s // 2)).reshape(s, 1)
    qk_rot, q_sb, cos_t, sin_t = _prep(proj, pos_col, idx_col, inv_freq, lg_lanes, 2 * tm)
    tq, tk = min(256, s), min(128, s)
    tq_sb = min(SB_TQ, s)
    sb, sb_weights, *big_w = _sb_fwd(q_sb, proj, tq_sb, tk, riders=([bf(w_ff2), bf_t(w_ff1)], ["gather_chip"] * 2))
    ret, retg, g_ret, g_sb, g_out, g_ff2, g_ff1 = _ret_fwd(qk_rot, proj, ret_gn_g, log_gamma, tq,
                                                           riders=(small_w + big_w, ["forward"] * 5))
    wf_ret = g_ret.reshape(RET_V, d)
    wt_sb = g_sb.reshape(d, SB_W)
    wf_out = g_out.reshape(d, d)
    wt_ff1 = g_ff1.reshape(d_ff, d)
    wf_ff2 = g_ff2.reshape(d_ff, d)
    mixed, r_bf, s_bf, y, hres, h2 = _merge_out(retg, sb, wf_ret, wt_sb, wf_out, proj, x2, mod, post_mix_g, pre_ffn_g, tm)
    u, act = _ff1(h2, wt_ff1, s, 512)
    dout, df, loss_sum, d_gt2, d_gp2 = _ff2_loss(act, wf_ff2, hres, tgt, mod, post_ffn_g, tm)

    du, gw_ff2, gw_ff1 = _ffn_bwd(df, wf_ff2, u, act, h2, d_ff // N_DEV)
    gw_ff2 = gw_ff2.reshape(N_DEV, d_ff // N_DEV, d)
    dhres, dy, d_sh2, d_sc2, d_g2, d_gt1, d_gp1, gw_out, t_ff1, t_ff2 = _ff1_bwd(
        du, wt_ff1, hres, dout, y, mixed, mod, pre_ffn_g, post_mix_g, tm, riders=([gw_ff1, gw_ff2], ["pair"] * 2))
    gw_out = gw_out.reshape(N_DEV, d // N_DEV, d)
    s_ff1, s_ff2 = _pair_sum("pair_sum_ff", [(gw_ff1, t_ff1), (gw_ff2, t_ff2)], core)
    d_r, d_s, da_r, da_s = _out_bwd(dy, wf_out, proj, r_bf, s_bf, 2 * tm, min(512, d))
    dsb, gw_ret, gw_sb, dg_r, dret, d_gn, p_out = _branch_bwd(d_r, d_s, retg, sb, wf_ret, wt_sb, ret, proj, ret_gn_g,
                                                              riders=([gw_out], ["scatter"]))
    gw_ret = gw_ret.reshape(N_DEV, RET_V // N_DEV, d)
    dq_s, dk_s, dv_s, p_ff1, p_ff2 = _sb_bwd(q_sb, proj, sb_weights, dsb, tq_sb, tk,
                                             riders=([s_ff1, s_ff2], ["chip_scatter"] * 2))
    dq_r, dk_r, dv_r, p_sb = _ret_bwd(qk_rot, proj, dret, log_gamma, tq, riders=([gw_sb], ["scatter"]))
    dproj = _assemble_dproj(dq_r, dk_r, dv_r, dg_r, dq_s, dk_s, dv_s, da_r, da_s, cos_t, sin_t, idx_col, lg_lanes, tm)[0]
    gw_in, p_ret = _matmul("grad_w_in", dproj, h, "tn", 512, d, BF16, riders=([gw_ret], ["scatter"]))
    gw_in = gw_in.reshape(N_DEV, d_in // N_DEV, d)
    t_in = _exchange("pair_in", [gw_in], ["pair"])[0]
    s_in = _pair_sum("pair_sum_in", [(gw_in, t_in)], core)[0]
    grad_x, d_sh1, d_sc1, d_g1, p_in = _in_bwd(dproj, wt_in, x2, dhres, mod, pre_mix_g, tm,
                                               riders=([s_in], ["chip_scatter"]))
    loss_lanes = jnp.pad(loss_sum, ((0, 0), (0, LANES - 1)))
    small = jnp.concatenate([d_sh1, d_sc1, d_gt1, d_sh2, d_sc2, d_gt2, d_g1, d_gp1, d_g2, d_gp2, d_gn, loss_lanes], axis=1)
    small_all = _exchange("gather_small", [small], ["gather"])[0].reshape(N_DEV, small.shape[1])
    parts = [p_in, p_ret, p_sb, p_out, p_ff1, p_ff2]

    res = {}
    names = ["w_ret_branch", "w_sb_branch", "w_out", "w_ff1", "w_ff2"]
    ws = [w_ret_branch, w_sb_branch, w_out, w_ff1, w_ff2]
    ms = [m_w_ret_branch, m_w_sb_branch, m_w_out, m_w_ff1, m_w_ff2]
    vs = [v_w_ret_branch, v_w_sb_branch, v_w_out, v_w_ff1, v_w_ff2]
    sets = [(parts[0], w_in_t, m_in_t, v_in_t)] + [(p, w[0], m[0], v[0]) for p, w, m, v in zip(parts[1:], ws, ms, vs)]
    updated = _adam_reduce("adam_big", sets, 4)
    res["w_in"] = [jnp.swapaxes(o, 0, 1)[None] for o in updated[0]]
    for nm, outs4 in zip(names, updated[1:]):
        res[nm] = [o[None] for o in outs4]
    dmod_cols = lax.dynamic_slice(small_all, (0, me * n_ada), (N_DEV, n_ada))
    res["ada_w"] = [o[None] for o in _ada_bwd_adam(cs_all.reshape(N_DEV, d, 1), dmod_cols, ada_w[0], m_ada_w[0], v_ada_w[0], tm)]
    vec_names = ["ada_b", "pre_mix_g", "post_mix_g", "pre_ffn_g", "post_ffn_g", "ret_gn_g"]
    vec_res, loss_lanes = _small_adam(small_all,
                                      [ada_b, pre_mix_g, post_mix_g, pre_ffn_g, post_ffn_g, ret_gn_g],
                                      [m_ada_b, m_pre_mix_g, m_post_mix_g, m_pre_ffn_g, m_post_ffn_g, m_ret_gn_g],
                                      [v_ada_b, v_pre_mix_g, v_post_mix_g, v_pre_ffn_g, v_post_ffn_g, v_ret_gn_g])
    res.update(zip(vec_names, vec_res))
    loss = (0.5 / d) * loss_lanes[0, 0]
    order = ["ada_w", "ada_b", "pre_mix_g", "post_mix_g", "pre_ffn_g", "post_ffn_g", "w_in", "ret_gn_g",
             "w_ret_branch", "w_sb_branch", "w_out", "w_ff1", "w_ff2"]
    outs = [loss, grad_x[None]]
    for k in range(4):
        outs += [res[nm][k] for nm in order]
    return tuple(outs)
```

```python
import functools

import numpy as np
import jax
import jax.numpy as jnp
from jax import lax
from jax.experimental import pallas as pl
from jax.experimental.pallas import tpu as pltpu

F32 = jnp.float32
BF16 = jnp.bfloat16
N_DEV = 8
AXES = ("x", "y", "c")

EPS = 1e-6
CHUNK = 64
CHUNK_SHIFT = 6
HEADS = 8
RET_DQK = 64
RET_DV = 128
SB_DH = 64
RET_QK = HEADS * RET_DQK
RET_V = HEADS * RET_DV
SB_W = HEADS * SB_DH
ROPE_BASE = 10000.0
LANES = 128

ADAM_LR = 0.001
ADAM_B1 = 0.9
ADAM_B2 = 0.999
ADAM_EPS = 1e-08
ADAM_WD = 0.01
ADAM_STEP = 10

VMEM_LIMIT = 56 * 1024 * 1024

_NN = (((1,), (0,)), ((), ()))
_NT = (((1,), (1,)), ((), ()))
_TN = (((0,), (0,)), ((), ()))


def _dot(a, b, dims=_NN):
    if a.dtype != BF16:
        a = a.astype(BF16)
    if b.dtype != BF16:
        b = b.astype(BF16)
    return lax.dot_general(a, b, dims, preferred_element_type=F32)


def _sigmoid(x):
    return 1.0 / (1.0 + jnp.exp(-x))


def _rms(x, d):
    r = lax.rsqrt(jnp.sum(x * x, axis=1, keepdims=True) * (1.0 / d) + EPS)
    return x * r, r


def _rms_bwd(dn, n, r, d):
    return r * (dn - n * (jnp.sum(dn * n, axis=1, keepdims=True) * (1.0 / d)))


def _colsum(v):
    return jnp.sum(v, axis=0, keepdims=True)


ROW_SPLIT = 2


def _zero_at_start(refs):
    @pl.when(pl.program_id(0) == 0)
    def _():
        for r in refs:
            r[...] = jnp.zeros_like(r)


def _pieces(tm):
    step = tm // ROW_SPLIT
    return [slice(k * step, (k + 1) * step) for k in range(ROW_SPLIT)]


KIND_SLOTS = {"gather": N_DEV, "scatter": N_DEV, "gather_chip": N_DEV, "forward": N_DEV, "pair": N_DEV // 2,
              "chip_scatter": N_DEV // 2}
SEMS_PER_ARRAY = N_DEV - 1


def _exchange_copies(ins, outs, send_sems, recv_sems, local_sems, kinds):
    x, y, c = (lax.axis_index(a) for a in AXES)
    me, chip, sibling = 4 * x + 2 * y + c, 2 * x + y, (x, y, 1 - c)
    mesh_id = pl.DeviceIdType.MESH
    other_chips = []
    for k in range(1, N_DEV // 2):
        px = 1 - x if k & 2 else x
        py = 1 - y if k & 1 else y
        other_chips.append((px, py))
    copies = []
    for i, kind in enumerate(kinds):
        def remote(src, dst, k, to, i=i):
            return pltpu.make_async_remote_copy(
                src_ref=src, dst_ref=dst, send_sem=send_sems.at[i * SEMS_PER_ARRAY + k],
                recv_sem=recv_sems.at[i * SEMS_PER_ARRAY + k], device_id=to, device_id_type=mesh_id)

        if kind in ("gather", "scatter"):
            pick = (lambda ref, d: ref.at[d]) if kind == "scatter" else (lambda ref, d: ref)
            copies.append(pltpu.make_async_copy(pick(ins[i], me), outs[i].at[me], local_sems.at[i]))
            for k in range(1, N_DEV):
                to = (1 - x if k & 4 else x, 1 - y if k & 2 else y, 1 - c if k & 1 else c)
                copies.append(remote(pick(ins[i], 4 * to[0] + 2 * to[1] + to[2]), outs[i].at[me], k - 1, to))
        elif kind == "gather_chip":
            copies.append(pltpu.make_async_copy(ins[i], outs[i].at[me], local_sems.at[i]))
            copies.append(remote(ins[i], outs[i].at[me], 0, sibling))
            for k, (px, py) in enumerate(other_chips):
                copies.append(remote(ins[i], outs[i].at[me], 1 + k, (px, py, c)))
        elif kind == "forward":
            for k, (px, py) in enumerate(other_chips):
                slot = 4 * px + 2 * py + c
                copies.append(remote(outs[i].at[slot], outs[i].at[slot], k, sibling))
        elif kind == "pair":
            for k in range(N_DEV // 2):
                copies.append(remote(ins[i].at[2 * k + 1 - c], outs[i].at[k], k, sibling))
        elif kind == "chip_scatter":
            copies.append(pltpu.make_async_copy(ins[i].at[chip], outs[i].at[chip], local_sems.at[i]))
            for k, (px, py) in enumerate(other_chips):
                copies.append(remote(ins[i].at[2 * px + py], outs[i].at[chip], k, (px, py, c)))
        else:
            raise ValueError(kind)
    return copies


def _exchange_shapes(arrays, kinds):
    shapes = []
    for a, kind in zip(arrays, kinds):
        tail = a.shape if kind in ("gather", "gather_chip") else a.shape[1:]
        shapes.append(jax.ShapeDtypeStruct((KIND_SLOTS[kind],) + tuple(tail), a.dtype))
    return shapes


def _exchange_sems(n):
    return [pltpu.SemaphoreType.DMA((n * SEMS_PER_ARRAY,)), pltpu.SemaphoreType.DMA((n * SEMS_PER_ARRAY,)),
            pltpu.SemaphoreType.DMA((n,))]


def _call(name, body, grid, ins, outs, scratch=(), riders=None, prefetch=None):
    any_spec = pl.BlockSpec(memory_space=pl.ANY)
    in_specs = [pl.BlockSpec(memory_space=im) if bs is None else pl.BlockSpec(bs, im) for _, bs, im in ins]
    out_specs = [pl.BlockSpec(bs, im) for _, _, bs, im in outs]
    out_shape = [jax.ShapeDtypeStruct(s, d) for s, d, _, _ in outs]
    operands = [a for a, _, _ in ins]
    scratch = list(scratch)
    aliases = {}
    n_pre = 0 if prefetch is None else 1
    kernel = functools.partial(body) if prefetch is None else (lambda _, *refs: body(*refs))
    if riders is not None:
        arrays, kinds = riders
        nr, n_in, n_out, n_scr = len(arrays), len(ins), len(outs), len(scratch)

        def kernel(*refs):
            refs = refs[n_pre:]
            own_in, ride_in = refs[:n_in], refs[n_in:n_in + nr]
            own_out = refs[n_in + nr:n_in + nr + n_out]
            ride_out = refs[n_in + nr + n_out:n_in + 2 * nr + n_out]
            own_scr = refs[n_in + 2 * nr + n_out:n_in + 2 * nr + n_out + n_scr]
            sems = refs[n_in + 2 * nr + n_out + n_scr:]
            ids = [pl.program_id(a) for a in range(len(grid))]
            first = functools.reduce(jnp.logical_and, [i == 0 for i in ids])
            last = functools.reduce(jnp.logical_and, [i == g - 1 for i, g in zip(ids, grid)])

            @pl.when(first)
            def _():
                for cp in _exchange_copies(ride_in, ride_out, *sems, kinds):
                    cp.start()

            body(*own_in, *own_out, *own_scr)

            @pl.when(last)
            def _():
                for cp in _exchange_copies(ride_in, ride_out, *sems, kinds):
                    cp.wait()

        in_specs += [any_spec] * nr
        out_specs += [any_spec] * nr
        out_shape += _exchange_shapes(arrays, kinds)
        operands += list(arrays)
        scratch += _exchange_sems(nr)
        aliases = {n_pre + n_in + r: n_out + r for r, kind in enumerate(kinds) if kind == "forward"}
    params = pltpu.CompilerParams(dimension_semantics=("arbitrary",) * len(grid), vmem_limit_bytes=VMEM_LIMIT)
    if prefetch is None:
        return pl.pallas_call(kernel, name=name, grid=grid, in_specs=in_specs, out_specs=out_specs,
                              out_shape=out_shape, scratch_shapes=scratch, input_output_aliases=aliases,
                              compiler_params=params)(*operands)
    grid_spec = pltpu.PrefetchScalarGridSpec(num_scalar_prefetch=1, grid=grid, in_specs=in_specs,
                                             out_specs=out_specs, scratch_shapes=scratch)
    return pl.pallas_call(kernel, name=name, grid_spec=grid_spec, out_shape=out_shape,
                          input_output_aliases=aliases, compiler_params=params)(prefetch, *operands)


def _exchange(name, arrays, kinds):
    n = len(arrays)

    def body(*refs):
        copies = _exchange_copies(refs[:n], refs[n:2 * n], *refs[2 * n:], kinds)
        for cp in copies:
            cp.start()
        for cp in copies:
            cp.wait()

    any_spec = pl.BlockSpec(memory_space=pl.ANY)
    return pl.pallas_call(
        functools.partial(body),
        name=name,
        in_specs=[any_spec] * n,
        out_specs=[any_spec] * n,
        out_shape=_exchange_shapes(arrays, kinds),
        scratch_shapes=_exchange_sems(n),
        input_output_aliases={i: i for i, kind in enumerate(kinds) if kind == "forward"},
    )(*arrays)


def _gather_first(c, w):
    def body(c_ref, w_ref, c_out, w_out, send_sems, recv_sems, local_sems):
        x, y, cc = (lax.axis_index(a) for a in AXES)
        me, sibling = 4 * x + 2 * y + cc, (x, y, 1 - cc)
        small = _exchange_copies([c_ref], [c_out], send_sems, recv_sems, local_sems, ["gather"])
        for cp in small:
            cp.start()

        def remote(src, dst, k, to):
            return pltpu.make_async_remote_copy(
                src_ref=src, dst_ref=dst, send_sem=send_sems.at[SEMS_PER_ARRAY + k],
                recv_sem=recv_sems.at[SEMS_PER_ARRAY + k], device_id=to, device_id_type=pl.DeviceIdType.MESH)

        chips = []
        for k in range(1, N_DEV // 2):
            chips.append((1 - x if k & 2 else x, 1 - y if k & 1 else y))
        own = pltpu.make_async_copy(w_ref, w_out.at[me], local_sems.at[1])
        own.start()
        first = [remote(w_ref, w_out.at[me], 0, sibling)]
        first += [remote(w_ref, w_out.at[me], 1 + k, (px, py, cc)) for k, (px, py) in enumerate(chips)]
        for cp in first:
            cp.start()
        passed = []
        for k, (px, py) in enumerate(chips):
            slot = 4 * px + 2 * py + cc
            first[1 + k].wait_recv()
            passed.append(remote(w_out.at[slot], w_out.at[slot], N_DEV // 2 + k, sibling))
            passed[k].start()
        first[0].wait_recv()
        for cp in passed:
            cp.wait_recv()
        for cp in first + passed:
            cp.wait_send()
        own.wait()
        for cp in small:
            cp.wait()

    any_spec = pl.BlockSpec(memory_space=pl.ANY)
    return pl.pallas_call(
        functools.partial(body),
        name="gather_first",
        in_specs=[any_spec] * 2,
        out_specs=[any_spec] * 2,
        out_shape=_exchange_shapes([c, w], ["gather", "gather"]),
        scratch_shapes=_exchange_sems(2),
    )(c, w)


def _pair_sum(name, pairs, my_core):
    n = len(pairs)

    def body(*refs):
        for k in range(n):
            a_ref, b_ref, o_ref = refs[2 * k], refs[2 * k + 1], refs[2 * n + k]
            o_ref[...] = (a_ref[...].astype(F32) + b_ref[...].astype(F32)).astype(o_ref.dtype)

    ins, outs = [], []
    for mine, theirs in pairs:
        _, rws, cls = mine.shape
        ins += [(mine, (None, rws, cls), lambda k, core: (2 * k + core[0], 0, 0)),
                (theirs, (None, rws, cls), lambda k, core: (k, 0, 0))]
        outs.append(((N_DEV // 2, rws, cls), mine.dtype, (None, rws, cls), lambda k, core: (k, 0, 0)))
    return _call(name, body, (N_DEV // 2,), ins, outs, prefetch=my_core)


def _matmul(name, a, b, kind, tm, tn, out_dtype, blocked_out=False, riders=None):
    if kind == "tn":
        kdim, m = a.shape
    else:
        m, kdim = a.shape
    n = b.shape[0] if kind == "nt" else b.shape[1]
    tm, tn = min(tm, m), min(tn, n)
    dims = {"nn": _NN, "nt": _NT, "tn": _TN}[kind]

    def body(a_ref, b_ref, o_ref):
        o_ref[...] = _dot(a_ref[...], b_ref[...], dims).astype(o_ref.dtype)

    a_spec = (a, (kdim, tm), lambda j, i: (0, i)) if kind == "tn" else (a, (tm, kdim), lambda j, i: (i, 0))
    b_spec = (b, (tn, kdim), lambda j, i: (j, 0)) if kind == "nt" else (b, (kdim, tn), lambda j, i: (0, j))
    if blocked_out:
        out = ((n // tn, m, tn), out_dtype, (None, tm, tn), lambda j, i: (j, i, 0))
    else:
        out = ((m, n), out_dtype, (tm, tn), lambda j, i: (i, j))
    res = _call(name, body, (n // tn, m // tm), [a_spec, b_spec], [out], riders=riders)
    return res[0] if riders is None else res


def _ada_fwd(c_all, ada_w, ada_b_cols):
    def body(c_ref, w_ref, b_ref, cs_ref, o_ref):
        v = c_ref[...]
        cs = v * _sigmoid(v)
        cs_ref[...] = cs
        o_ref[...] = lax.dot_general(cs, w_ref[...], _NN, preferred_element_type=F32,
                                     precision=lax.Precision.HIGHEST) + b_ref[...]

    r, d = c_all.shape
    nc = ada_w.shape[1]
    fix = lambda i: (0, 0)
    return _call("ada_fwd", body, (1,),
                 [(c_all, (r, d), fix), (ada_w, (d, nc), fix), (ada_b_cols, (1, nc), fix)],
                 [((r, d), F32, (r, d), fix), ((r, nc), F32, (r, nc), fix)])


def _pre_norm(x, g, mod, tm, riders=None):
    s, d = x.shape

    def body(x_ref, g_ref, mod_ref, h_ref):
        n, _ = _rms(x_ref[...], d)
        sh, sc = mod_ref[:, 0:d], mod_ref[:, d:2 * d]
        h_ref[...] = (n * g_ref[...] * (1.0 + sc) + sh).astype(BF16)

    return _call("pre_norm", body, (s // tm,),
                 [(x, (tm, d), lambda i: (i, 0)), (g, (1, d), lambda i: (0, 0)),
                  (mod, (1, 6 * d), lambda i: (0, 0))],
                 [((s, d), BF16, (tm, d), lambda i: (i, 0))], riders=riders)


LOG2E = 1.4426950408889634
LN2 = 0.6931471805599453


def _decay_scale(lg_ref, idx, g, sign):
    return jnp.exp((sign * idx) * lg_ref[:, g * LANES:(g + 1) * LANES])


def _prep(proj, pos_col, idx_col, inv_freq, lg_lanes, tm):
    s = proj.shape[0]
    sb_off = (2 * RET_QK + 2 * RET_V) // SB_W
    n_q = RET_QK // LANES

    def body(qk_ref, qs_ref, pos_ref, idx_ref, f_ref, lg_ref, qk_out, qs_out, cos_out, sin_out):
        ang = pos_ref[...] * f_ref[...]
        lane = lax.broadcasted_iota(jnp.int32, (1, LANES), 1)
        first = jnp.bitwise_and(lane, RET_DQK - 1) < (RET_DQK // 2)
        cos = jnp.cos(ang)
        sin = jnp.where(first, -1.0, 1.0) * jnp.sin(ang)
        cos_out[...] = cos
        sin_out[...] = sin
        idx = idx_ref[...]
        for g in range(2 * n_q):
            v = qk_ref[:, g * LANES:(g + 1) * LANES].astype(F32)
            sw = jnp.where(first, pltpu.roll(v, LANES - RET_DQK // 2, 1), pltpu.roll(v, RET_DQK // 2, 1))
            r = v * cos + sw * sin
            if g < n_q:
                r = r * _decay_scale(lg_ref, idx, g, 1.0)
            else:
                r = r * (_decay_scale(lg_ref, idx, g - n_q, -1.0) * (RET_DQK ** -0.5))
            qk_out[:, g * LANES:(g + 1) * LANES] = r.astype(BF16)
        qs_out[...] = (qs_ref[...].astype(F32) * (SB_DH ** -0.5 * LOG2E)).astype(BF16)

    row = lambda i: (i, 0)
    return _call("prep", body, (s // tm,),
                 [(proj, (tm, 2 * RET_QK), row),
                  (proj, (tm, SB_W), lambda i: (i, sb_off)),
                  (pos_col, (tm, 1), row),
                  (idx_col, (tm, 1), row),
                  (inv_freq, (1, LANES), lambda i: (0, 0)),
                  (lg_lanes, (1, RET_QK), lambda i: (0, 0))],
                 [((s, 2 * RET_QK), BF16, (tm, 2 * RET_QK), row),
                  ((s, SB_W), BF16, (tm, SB_W), row),
                  ((s, LANES), F32, (tm, LANES), row),
                  ((s, LANES), F32, (tm, LANES), row)])


def _head_mask(hh):
    lane = lax.broadcasted_iota(jnp.int32, (1, LANES), 1)
    return (lane >= RET_DQK) if hh else (lane < RET_DQK)


def _masked(v, m):
    return jnp.where(m, v, jnp.zeros_like(v))


SB_GROUP = 4
SB_TQ = 512


def _stack_heads(v):
    return jnp.concatenate([_masked(v, _head_mask(0)), _masked(v, _head_mask(1))], axis=0)


def _side_by_side(v, t):
    return jnp.concatenate([v[:t], v[t:]], axis=1)


def _tile_pos(i, j, tq, tk):
    row = jnp.bitwise_and(lax.broadcasted_iota(jnp.int32, (2 * tq, tk), 0), tq - 1) + i * tq
    col = lax.broadcasted_iota(jnp.int32, (2 * tq, tk), 1) + j * tk
    return row, col


def _n_groups(i, tq, tk, grp):
    return ((i + 1) * (tq // tk) + grp - 1) // grp


def _n_full(i, tq, tk, grp):
    return (i * (tq // tk)) // grp


def _key_rows(j, tk):
    return pl.ds(pl.multiple_of(j * tk, tk), tk)


def _ret_weight(lg_rows, i, j, tq, tk):
    row, col = _tile_pos(i, j, tq, tk)
    same = jnp.right_shift(col, CHUNK_SHIFT) == jnp.right_shift(row, CHUNK_SHIFT)
    later = jnp.where(same, jnp.exp((2.0 * lg_rows) * (col - row).astype(F32)), 0.0)
    return jnp.where(col <= row, 1.0, later)


def _lg_rows(lg_ref, hp, tq):
    first = lax.broadcasted_iota(jnp.int32, (2 * tq, 1), 0) < tq
    return jnp.where(first, lg_ref[2 * hp], lg_ref[2 * hp + 1])


def _check_tiles(s, tq, tk, grp):
    assert tq % tk == 0 and tq & (tq - 1) == 0 and tk & (tk - 1) == 0
    assert s % tq == 0 and (s // tk) % grp == 0 and s // tk <= LANES


def _pair_mask():
    r = lax.broadcasted_iota(jnp.int32, (LANES, 2 * RET_DV), 0) >= RET_DQK
    c = lax.broadcasted_iota(jnp.int32, (LANES, 2 * RET_DV), 1) >= RET_DV
    return (r == c).astype(F32)


def _ret_block(lg_ref, hp, i, t, qb, kb):
    w = _ret_weight(_lg_rows(lg_ref, hp, t), i, i, t, t)
    return _dot(_stack_heads(qb), kb, _NT), w


RET_PAIRS = 4


def _lanes(ref, p, width):
    return ref[:, p * width:(p + 1) * width]


def _ret_fwd(qk_rot, proj, gn_g, log_gamma, t, riders=None):
    s = qk_rot.shape[0]
    n_pair = HEADS // 2
    pw = 2 * RET_DV
    wq, wv = RET_PAIRS * LANES, RET_PAIRS * pw
    v_off, gate_off = 2 * RET_QK // wv, (2 * RET_QK + RET_V) // wv
    assert s % t == 0 and t % CHUNK == 0 and t & (t - 1) == 0 and n_pair % RET_PAIRS == 0

    def body(lg_ref, q_ref, k_ref, v_ref, g_ref, w_ref, ret_ref, rg_ref, state_ref):
        hg, i = pl.program_id(0), pl.program_id(1)

        @pl.when(i == 0)
        def _():
            state_ref[...] = jnp.zeros_like(state_ref)

        pairs = range(RET_PAIRS)
        qbs = [_lanes(q_ref, p, LANES) for p in pairs]
        kbs = [_lanes(k_ref, p, LANES) for p in pairs]
        vbs = [_lanes(v_ref, p, pw) for p in pairs]
        zws = [_ret_block(lg_ref, hg * RET_PAIRS + p, i, t, qbs[p], kbs[p]) for p in pairs]
        ps = [(z * w).astype(BF16) for z, w in zws]
        outs = [jnp.concatenate([_dot(ps[p][:t], vbs[p][:, 0:RET_DV]), _dot(ps[p][t:], vbs[p][:, RET_DV:pw])], axis=1)
                + _dot(qbs[p], state_ref[p]) for p in pairs]
        for p in pairs:
            state_ref[p] += _pair_mask() * _dot(kbs[p], vbs[p], _TN)
        for p in pairs:
            for hh in range(2):
                cols = slice(p * pw + hh * RET_DV, p * pw + (hh + 1) * RET_DV)
                o = outs[p][:, hh * RET_DV:(hh + 1) * RET_DV]
                ret_ref[:, cols] = o
                mu = jnp.sum(o, axis=1, keepdims=True) * (1.0 / RET_DV)
                xc = o - mu
                var = jnp.sum(xc * xc, axis=1, keepdims=True) * (1.0 / RET_DV)
                nrm = xc * lax.rsqrt(var + EPS) * w_ref[:, cols]
                g = g_ref[:, cols].astype(F32)
                rg_ref[:, cols] = (g * _sigmoid(g) * nrm).astype(BF16)

    blk = lambda hg, i: (i, hg)
    return _call("ret_fwd", body, (n_pair // RET_PAIRS, s // t),
                 [(log_gamma, None, pltpu.SMEM),
                  (qk_rot, (t, wq), blk),
                  (qk_rot, (t, wq), lambda hg, i: (i, n_pair // RET_PAIRS + hg)),
                  (proj, (t, wv), lambda hg, i: (i, v_off + hg)),
                  (proj, (t, wv), lambda hg, i: (i, gate_off + hg)),
                  (gn_g, (1, wv), lambda hg, i: (0, hg))],
                 [((s, RET_V), F32, (t, wv), blk), ((s, RET_V), BF16, (t, wv), blk)],
                 scratch=[pltpu.VMEM((RET_PAIRS, LANES, pw), F32)], riders=riders)


def _tri(tk, strict_upper):
    r = lax.broadcasted_iota(jnp.int32, (tk, tk), 0)
    cc = lax.broadcasted_iota(jnp.int32, (tk, tk), 1)
    return ((r > cc) if strict_upper else (r < cc)).astype(BF16)


def _diagonal_step(i, tq, tk, make, carry):
    if (tq // tk) % SB_GROUP == 0:
        return make(SB_GROUP)(0, carry)
    assert 2 * (tq // tk) == SB_GROUP
    half = lax.rem(i, 2) == 0
    return lax.cond(half, lambda cr: make(SB_GROUP // 2)(0, cr), lambda cr: make(SB_GROUP)(0, cr), carry)


def _sb_valid(i, j, tq, tk):
    row, col = _tile_pos(i, j, tq, tk)
    return col < row


def _sb_fwd(q_sb, proj, tq, tk, riders=None):
    s = q_sb.shape[0]
    k_off = (2 * RET_QK + 2 * RET_V + SB_W) // LANES
    n_pair = HEADS // 2
    _check_tiles(s, tq, tk, SB_GROUP)

    def body(q_ref, k_ref, v_ref, o_ref, a_ref):
        i = pl.program_id(1)
        upper = _tri(tk, True)
        qs = _stack_heads(q_ref[...])
        n_full, n_groups = _n_full(i, tq, tk, SB_GROUP), _n_groups(i, tq, tk, SB_GROUP)

        def make_step(near_diagonal, last, n_sub=SB_GROUP):
            def step(n, carry):
                c, o = carry
                g = last - 1 - n
                js = [g * SB_GROUP + sub for sub in range(n_sub)]
                zs = [_dot(qs, k_ref[_key_rows(j, tk), :], _NT) for j in js]
                log1ps = [jnp.log2(1.0 + jnp.exp2(-jnp.abs(z))) for z in zs]
                log_1ms = [-jnp.maximum(z, 0.0) - t for z, t in zip(zs, log1ps)]
                log_bs = [jnp.minimum(z, 0.0) - t for z, t in zip(zs, log1ps)]
                if near_diagonal:
                    valids = [_sb_valid(i, j, tq, tk) for j in js]
                    log_1ms = [jnp.where(v, l, 0.0) for v, l in zip(valids, log_1ms)]
                sticks = [_dot(l, upper) for l in log_1ms]
                sums = [jnp.sum(l, axis=1, keepdims=True) for l in log_1ms]
                cs = [None] * n_sub
                for sub in reversed(range(n_sub)):
                    cs[sub] = c
                    c = c + sums[sub]
                for sub, j in enumerate(js):
                    a = jnp.exp2(log_bs[sub] + sticks[sub] + cs[sub])
                    if near_diagonal:
                        a = jnp.where(valids[sub], a, 0.0)
                    a = a.astype(BF16)
                    a_ref[j] = a
                    o = o + _dot(_side_by_side(a, tq), _stack_heads(v_ref[_key_rows(j, tk), :]))
                return c, o
            return step

        carry = (jnp.zeros((2 * tq, 1), F32), jnp.zeros((tq, LANES), F32))
        carry = _diagonal_step(i, tq, tk, lambda n_sub: make_step(True, n_groups, n_sub), carry)
        _, acc = lax.fori_loop(0, n_full, make_step(False, n_full), carry)
        o_ref[...] = acc.astype(BF16)

    n_kb = s // tk
    return _call("sb_fwd", body, (n_pair, s // tq),
                 [(q_sb, (tq, LANES), lambda hp, i: (i, hp)),
                  (proj, (s, LANES), lambda hp, i: (0, k_off + hp)),
                  (proj, (s, LANES), lambda hp, i: (0, k_off + n_pair + hp))],
                 [((s, SB_W), BF16, (tq, LANES), lambda hp, i: (i, hp)),
                  ((n_pair, s // tq, n_kb, 2 * tq, tk), BF16, (None, None, n_kb, 2 * tq, tk),
                   lambda hp, i: (hp, i, 0, 0, 0))], riders=riders)


def _merge_out(retg, sb, w_ret, w_sb_t, w_out, proj, x, mod, gp1, g2, tm):
    s, d = x.shape
    gw = min(512, d)
    n_g = d // gw
    ar_off = (2 * RET_QK + 2 * RET_V + 3 * SB_W) // gw

    def body(rg_ref, sb_ref, wr_ref, ws_ref, wo_ref, *refs):
        gate_refs, (x_ref, mod_ref, gp_ref, g2_ref, mix_ref, r_ref, s_ref, y_ref, hres_ref, h2_ref) = refs[:2 * n_g], refs[2 * n_g:]
        for rows in _pieces(tm):
            rr = _dot(rg_ref[rows, :], wr_ref[...])
            ss = _dot(sb_ref[rows, :], ws_ref[...], _NT)
            a_r = jnp.concatenate([g[rows, :] for g in gate_refs[:n_g]], axis=1).astype(F32)
            a_s = jnp.concatenate([g[rows, :] for g in gate_refs[n_g:]], axis=1).astype(F32)
            mixed = (_sigmoid(a_r) * rr + _sigmoid(a_s) * ss).astype(BF16)
            mix_ref[rows, :] = mixed
            r_ref[rows, :] = rr.astype(BF16)
            s_ref[rows, :] = ss.astype(BF16)
            y = _dot(mixed, wo_ref[...])
            y_ref[rows, :] = y
            ny, _ = _rms(y, d)
            hres = x_ref[rows, :] + mod_ref[:, 2 * d:3 * d] * (ny * gp_ref[...])
            hres_ref[rows, :] = hres
            n2, _ = _rms(hres, d)
            h2_ref[rows, :] = (n2 * g2_ref[...] * (1.0 + mod_ref[:, 4 * d:5 * d]) + mod_ref[:, 3 * d:4 * d]).astype(BF16)

    row = lambda i: (i, 0)
    fix = lambda i: (0, 0)
    tile_bf = ((s, d), BF16, (tm, d), row)
    tile_f = ((s, d), F32, (tm, d), row)
    return _call("merge_out", body, (s // tm,),
                 [(retg, (tm, RET_V), row), (sb, (tm, SB_W), row), (w_ret, (RET_V, d), fix), (w_sb_t, (d, SB_W), fix),
                  (w_out, (d, d), fix)]
                 + [(proj, (tm, gw), functools.partial(lambda i, k: (i, ar_off + k), k=k)) for k in range(2 * n_g)]
                 + [(x, (tm, d), row), (mod, (1, 6 * d), fix), (gp1, (1, d), fix), (g2, (1, d), fix)],
                 [tile_bf, tile_bf, tile_bf, tile_f, tile_f, tile_bf])


def _ff1(h2, w_ff1_t, tm, tn):
    s, f = h2.shape[0], w_ff1_t.shape[0]
    tm = min(tm, s)

    def body(a_ref, w_ref, u_ref, act_ref):
        u = _dot(a_ref[...], w_ref[...], _NT)
        r = jnp.maximum(u, 0.0)
        u_ref[...] = u.astype(BF16)
        act_ref[...] = (r * r).astype(BF16)

    d = h2.shape[1]
    return _call("ff1", body, (f // tn, s // tm),
                 [(h2, (tm, d), lambda j, i: (i, 0)), (w_ff1_t, (tn, d), lambda j, i: (j, 0))],
                 [((s, f), BF16, (tm, tn), lambda j, i: (i, j))] * 2)


def _ff2_loss(act, w_ff2, hres, target, mod, gp2, tm):
    s, d = hres.shape
    f = act.shape[1]

    def body(a_ref, w_ref, h_ref, t_ref, mod_ref, gp_ref, dout_ref, df_ref, loss_ref, dgt_ref, dgp_ref):
        _zero_at_start([loss_ref, dgt_ref, dgp_ref])
        gt, gp = mod_ref[:, 5 * d:6 * d], gp_ref[...]
        for rows in _pieces(tm):
            ff = _dot(a_ref[rows, :], w_ref[...])
            nf, rf = _rms(ff, d)
            out = h_ref[rows, :] + gt * (nf * gp)
            err = out - t_ref[rows, :]
            sq = jnp.sum(err * err, axis=1, keepdims=True)
            loss_ref[...] += jnp.sum(sq, axis=0, keepdims=True)
            dout = err * (1.0 / d)
            dout_ref[rows, :] = dout
            dgt_ref[...] += _colsum(dout * (nf * gp))
            dgp_ref[...] += _colsum(dout * gt * nf)
            df_ref[rows, :] = _rms_bwd(dout * gt * gp, nf, rf, d).astype(BF16)

    row = lambda i: (i, 0)
    fix = lambda i: (0, 0)
    return _call("ff2_loss", body, (s // tm,),
                 [(act, (tm, f), row), (w_ff2, (f, d), fix), (hres, (tm, d), row), (target, (tm, d), row),
                  (mod, (1, 6 * d), fix), (gp2, (1, d), fix)],
                 [((s, d), F32, (tm, d), row), ((s, d), BF16, (tm, d), row), ((1, 1), F32, (1, 1), fix),
                  ((1, d), F32, (1, d), fix), ((1, d), F32, (1, d), fix)])


def _ffn_bwd(df, w_ff2, u, act, h2, tn):
    s, d = df.shape
    f = w_ff2.shape[0]

    def body(df_ref, w_ref, u_ref, act_ref, h2_ref, du_ref, gw2_ref, gw1_ref):
        dfb = df_ref[...]
        du = (_dot(dfb, w_ref[...], _NT) * (2.0 * jnp.maximum(u_ref[...].astype(F32), 0.0))).astype(BF16)
        du_ref[...] = du
        gw2_ref[...] = _dot(act_ref[...], dfb, _TN).astype(BF16)
        gw1_ref[...] = _dot(h2_ref[...], du, _TN).astype(BF16)

    fix = lambda j: (0, 0)
    col = lambda j: (0, j)
    return _call("ffn_bwd", body, (f // tn,),
                 [(df, (s, d), fix), (w_ff2, (tn, d), lambda j: (j, 0)), (u, (s, tn), col), (act, (s, tn), col),
                  (h2, (s, d), fix)],
                 [((s, f), BF16, (s, tn), col), ((f, d), BF16, (tn, d), lambda j: (j, 0)),
                  ((f // tn, d, tn), BF16, (None, d, tn), lambda j: (j, 0, 0))])


def _ff1_bwd(du, w_ff1_t, hres, dout, y, mixed, mod, g2, gp1, tm, riders=None):
    s, d = hres.shape
    f = du.shape[1]

    def body(a_ref, w_ref, h_ref, do_ref, y_ref, mix_ref, mod_ref, g2_ref, gp_ref,
             dh_ref, dy_ref, dsh_ref, dsc_ref, dg2_ref, dgt_ref, dgp_ref, gwo_ref, acc_ref):
        _zero_at_start([dsh_ref, dsc_ref, dg2_ref, dgt_ref, dgp_ref, acc_ref])
        g2, sc2 = g2_ref[...], mod_ref[:, 4 * d:5 * d]
        gt, gp = mod_ref[:, 2 * d:3 * d], gp_ref[...]
        for rows in _pieces(tm):
            dh2 = _dot(a_ref[rows, :], w_ref[...])
            n2, r2 = _rms(h_ref[rows, :], d)
            dsh_ref[...] += _colsum(dh2)
            dsc_ref[...] += _colsum(dh2 * n2 * g2)
            dg2_ref[...] += _colsum(dh2 * n2 * (1.0 + sc2))
            dhres = do_ref[rows, :] + _rms_bwd(dh2 * g2 * (1.0 + sc2), n2, r2, d)
            dh_ref[rows, :] = dhres
            ny, ry = _rms(y_ref[rows, :], d)
            dgt_ref[...] += _colsum(dhres * (ny * gp))
            dgp_ref[...] += _colsum(dhres * gt * ny)
            dy_ref[rows, :] = _rms_bwd(dhres * gt * gp, ny, ry, d).astype(BF16)
        acc_ref[...] += _dot(mix_ref[...], dy_ref[...], _TN)

        @pl.when(pl.program_id(0) == s // tm - 1)
        def _():
            gwo_ref[...] = acc_ref[...].astype(BF16)

    row = lambda i: (i, 0)
    fix = lambda i: (0, 0)
    vec = ((1, d), F32, (1, d), fix)
    return _call("ff1_bwd", body, (s // tm,),
                 [(du, (tm, f), row), (w_ff1_t, (f, d), fix), (hres, (tm, d), row), (dout, (tm, d), row),
                  (y, (tm, d), row), (mixed, (tm, d), row), (mod, (1, 6 * d), fix), (g2, (1, d), fix), (gp1, (1, d), fix)],
                 [((s, d), F32, (tm, d), row), ((s, d), BF16, (tm, d), row), vec, vec, vec, vec, vec,
                  ((d, d), BF16, (d, d), fix)], scratch=[pltpu.VMEM((d, d), F32)], riders=riders)


def _out_bwd(dy, w_out, proj, r_bf, s_bf, tm, tn, riders=None):
    s, d = dy.shape
    ar_off = (2 * RET_QK + 2 * RET_V + 3 * SB_W) // tn
    as_off = ar_off + d // tn

    def body(a_ref, w_ref, ar_ref, as_ref, r_ref, s_ref, dr_ref, ds_ref, dar_ref, das_ref):
        dm = _dot(a_ref[...], w_ref[...], _NT)
        sr, ss = _sigmoid(ar_ref[...].astype(F32)), _sigmoid(as_ref[...].astype(F32))
        dr_ref[...] = (dm * sr).astype(BF16)
        ds_ref[...] = (dm * ss).astype(BF16)
        dar_ref[...] = (dm * r_ref[...].astype(F32) * sr * (1.0 - sr)).astype(BF16)
        das_ref[...] = (dm * s_ref[...].astype(F32) * ss * (1.0 - ss)).astype(BF16)

    tile = (tm, tn)
    here = lambda j, i: (i, j)
    return _call("out_bwd", body, (d // tn, s // tm),
                 [(dy, (tm, d), lambda j, i: (i, 0)), (w_out, (tn, d), lambda j, i: (j, 0)),
                  (proj, tile, lambda j, i: (i, ar_off + j)), (proj, tile, lambda j, i: (i, as_off + j)),
                  (r_bf, tile, here), (s_bf, tile, here)],
                 [((s, d), BF16, tile, here)] * 4, riders=riders)


def _branch_bwd(d_r, d_s, retg, sb, w_ret, w_sb_t, ret, proj, gn_g, riders=None):
    s, d = d_r.shape
    n_step = 4
    part_v, part_s, part_d = RET_V // n_step, SB_W // n_step, d // n_step
    per_dev = d // N_DEV
    n_blk = part_d // per_dev
    gate_off = (2 * RET_QK + RET_V) // part_v

    def body(dr_ref, ds_ref, rg_ref, sb_ref, wr_ref, ws_ref, r_ref, g_ref, w_ref,
             dsb_ref, gwr_ref, gws_ref, dg_ref, dret_ref, dw_ref):
        i = pl.program_id(0)
        dr, ds = dr_ref[...], ds_ref[...]
        dsb_ref[...] = _dot(ds, ws_ref[...]).astype(BF16)
        gwr_ref[...] = _dot(rg_ref[...], dr, _TN).astype(BF16)
        cols = pl.ds(pl.multiple_of(i * part_d, part_d), part_d)
        gws = _dot(sb_ref[...], ds_ref[:, cols], _TN).astype(BF16)
        for k in range(n_blk):
            gws_ref[k] = gws[:, k * per_dev:(k + 1) * per_dev]
        dretg = _dot(dr, wr_ref[...], _NT)
        for h in range(part_v // RET_DV):
            cols = slice(h * RET_DV, (h + 1) * RET_DV)
            o, g, w, d_o = r_ref[:, cols], g_ref[:, cols].astype(F32), w_ref[:, cols], dretg[:, cols]
            mu = jnp.sum(o, axis=1, keepdims=True) * (1.0 / RET_DV)
            xc = o - mu
            rstd = lax.rsqrt(jnp.sum(xc * xc, axis=1, keepdims=True) * (1.0 / RET_DV) + EPS)
            n = xc * rstd
            sg = _sigmoid(g)
            silu = g * sg
            dg_ref[:, cols] = (d_o * n * w * (sg * (1.0 + g * (1.0 - sg)))).astype(BF16)
            dw_ref[:, cols] = _colsum(d_o * silu * n)
            dn = d_o * silu * w
            m1 = jnp.sum(dn, axis=1, keepdims=True) * (1.0 / RET_DV)
            m2 = jnp.sum(dn * n, axis=1, keepdims=True) * (1.0 / RET_DV)
            dret_ref[:, cols] = (rstd * (dn - m1 - n * m2)).astype(BF16)

    fix = lambda i: (0, 0)
    col = lambda i: (0, i)
    return _call("branch_bwd", body, (n_step,),
                 [(d_r, (s, d), fix), (d_s, (s, d), fix), (retg, (s, part_v), col), (sb, (s, SB_W), fix),
                  (w_ret, (part_v, d), lambda i: (i, 0)), (w_sb_t, (d, part_s), col),
                  (ret, (s, part_v), col), (proj, (s, part_v), lambda i: (0, gate_off + i)), (gn_g, (1, part_v), col)],
                 [((s, SB_W), BF16, (s, part_s), col), ((RET_V, d), BF16, (part_v, d), lambda i: (i, 0)),
                  ((N_DEV, SB_W, per_dev), BF16, (n_blk, SB_W, per_dev), lambda i: (i, 0, 0)),
                  ((s, RET_V), BF16, (s, part_v), col), ((s, RET_V), BF16, (s, part_v), col),
                  ((1, RET_V), F32, (1, part_v), col)], riders=riders)


def _ret_bwd(qk_rot, proj, dret, log_gamma, t, riders=None):
    s = qk_rot.shape[0]
    n_pair = HEADS // 2
    pw = 2 * RET_DV
    wq, wv = RET_PAIRS * LANES, RET_PAIRS * pw
    n_blk = s // t
    pairs = range(RET_PAIRS)

    def load(q_ref, k_ref, v_ref, do_ref):
        return ([_lanes(q_ref, p, LANES) for p in pairs], [_lanes(k_ref, p, LANES) for p in pairs],
                [_lanes(v_ref, p, pw) for p in pairs], [_lanes(do_ref, p, pw) for p in pairs])

    def d_scores(lg_ref, hp, i, qb, kb, vb, dob):
        z, w = _ret_block(lg_ref, hp, i, t, qb, kb)
        dp = jnp.concatenate([_dot(dob[:, 0:RET_DV], vb[:, 0:RET_DV], _NT),
                              _dot(dob[:, RET_DV:pw], vb[:, RET_DV:pw], _NT)], axis=0)
        return (z * w).astype(BF16), (dp * w).astype(BF16)

    def up_body(lg_ref, q_ref, k_ref, v_ref, do_ref, dq_ref, state_ref):
        hg, i = pl.program_id(0), pl.program_id(1)

        @pl.when(i == 0)
        def _():
            state_ref[...] = jnp.zeros_like(state_ref)

        qbs, kbs, vbs, dobs = load(q_ref, k_ref, v_ref, do_ref)
        dss = [d_scores(lg_ref, hg * RET_PAIRS + p, i, qbs[p], kbs[p], vbs[p], dobs[p])[1] for p in pairs]
        for p in pairs:
            dq_ref[:, p * LANES:(p + 1) * LANES] = (_dot(_side_by_side(dss[p], t), _stack_heads(kbs[p]))
                                                    + _dot(dobs[p], state_ref[p], _NT)).astype(BF16)
        for p in pairs:
            state_ref[p] += _pair_mask() * _dot(kbs[p], vbs[p], _TN)

    def down_body(lg_ref, q_ref, k_ref, v_ref, do_ref, dk_ref, dv_ref, state_ref):
        hg, i = pl.program_id(0), n_blk - 1 - pl.program_id(1)

        @pl.when(pl.program_id(1) == 0)
        def _():
            state_ref[...] = jnp.zeros_like(state_ref)

        qbs, kbs, vbs, dobs = load(q_ref, k_ref, v_ref, do_ref)
        both = [d_scores(lg_ref, hg * RET_PAIRS + p, i, qbs[p], kbs[p], vbs[p], dobs[p]) for p in pairs]
        for p in pairs:
            pp, ds = both[p]
            later = state_ref[p]
            dv_ref[:, p * pw:(p + 1) * pw] = (jnp.concatenate(
                [_dot(pp[:t], dobs[p][:, 0:RET_DV], _TN), _dot(pp[t:], dobs[p][:, RET_DV:pw], _TN)],
                axis=1) + _dot(kbs[p], later)).astype(BF16)
            dk_ref[:, p * LANES:(p + 1) * LANES] = (_dot(ds, _stack_heads(qbs[p]), _TN)
                                                    + _dot(vbs[p], later, _NT)).astype(BF16)
        for p in pairs:
            state_ref[p] += _pair_mask() * _dot(qbs[p], dobs[p], _TN)

    n_grp = n_pair // RET_PAIRS

    def ins(order):
        return [(log_gamma, None, pltpu.SMEM),
                (qk_rot, (t, wq), lambda hg, i: (order(i), hg)),
                (qk_rot, (t, wq), lambda hg, i: (order(i), n_grp + hg)),
                (proj, (t, wv), lambda hg, i: (order(i), 2 * RET_QK // wv + hg)),
                (dret, (t, wv), lambda hg, i: (order(i), hg))]

    up = lambda i: i
    down = lambda i: n_blk - 1 - i
    scratch = [pltpu.VMEM((RET_PAIRS, LANES, pw), F32)]
    dq = _call("ret_bwd_q", up_body, (n_grp, n_blk), ins(up),
               [((s, RET_QK), BF16, (t, wq), lambda hg, i: (i, hg))], scratch=scratch)[0]
    dk, dv, *rest = _call("ret_bwd_kv", down_body, (n_grp, n_blk), ins(down),
                          [((s, RET_QK), BF16, (t, wq), lambda hg, i: (down(i), hg)),
                           ((s, RET_V), BF16, (t, wv), lambda hg, i: (down(i), hg))],
                          scratch=scratch, riders=riders)
    return [dq, dk, dv] + rest


def _sb_bwd(q_sb, proj, weights, do, tq, tk, riders=None):
    s = q_sb.shape[0]
    k_off = (2 * RET_QK + 2 * RET_V + SB_W) // LANES
    n_pair = HEADS // 2
    _check_tiles(s, tq, tk, SB_GROUP)

    def body(q_ref, k_ref, v_ref, a_ref, do_ref, dq_ref, dk_ref, dv_ref):
        i = pl.program_id(1)

        @pl.when(i == 0)
        def _():
            dk_ref[...] = jnp.zeros_like(dk_ref)
            dv_ref[...] = jnp.zeros_like(dv_ref)

        lower = _tri(tk, False)
        qs = _stack_heads(q_ref[...])
        dos = _stack_heads(do_ref[...].astype(BF16))

        def make_step(near_diagonal, n_sub=SB_GROUP):
            def step(g, carry):
                c_e, dq = carry
                js = [g * SB_GROUP + sub for sub in range(n_sub)]
                rows = [_key_rows(j, tk) for j in js]
                zs = [_dot(qs, k_ref[rw, :], _NT) for rw in rows]
                das = [_dot(dos, v_ref[rw, :], _NT) for rw in rows]
                avals = [a_ref[j] for j in js]
                for a, rw in zip(avals, rows):
                    dv_ref[rw, :] += _dot(a, dos, _TN)
                es = [a.astype(F32) * da for a, da in zip(avals, das)]
                prefixes = [_dot(e, lower) for e in es]
                betas = [1.0 / (1.0 + jnp.exp2(-z)) for z in zs]
                for sub in range(n_sub):
                    dz = es[sub] - (es[sub] + prefixes[sub] + c_e) * betas[sub]
                    if near_diagonal:
                        dz = jnp.where(_sb_valid(i, js[sub], tq, tk), dz, 0.0)
                    dz = dz.astype(BF16)
                    dk_ref[rows[sub], :] += _dot(dz, qs, _TN)
                    dq = dq + _dot(_side_by_side(dz, tq), _stack_heads(k_ref[rows[sub], :]))
                    c_e = c_e + jnp.sum(es[sub], axis=1, keepdims=True)
                return c_e, dq
            return step

        n_full = _n_full(i, tq, tk, SB_GROUP)
        carry = (jnp.zeros((2 * tq, 1), F32), jnp.zeros((tq, LANES), F32))
        carry = lax.fori_loop(0, n_full, make_step(False), carry)
        _, dq = _diagonal_step(i, tq, tk, lambda n_sub: (lambda n, cr: make_step(True, n_sub)(n_full, cr)), carry)
        dq_ref[...] = dq

    blk = lambda hp, i: (i, hp)
    n_kb = s // tk
    return _call("sb_bwd", body, (n_pair, s // tq),
                 [(q_sb, (tq, LANES), blk),
                  (proj, (s, LANES), lambda hp, i: (0, k_off + hp)),
                  (proj, (s, LANES), lambda hp, i: (0, k_off + n_pair + hp)),
                  (weights, (None, None, n_kb, 2 * tq, tk), lambda hp, i: (hp, i, 0, 0, 0)),
                  (do, (tq, LANES), blk)],
                 [((s, SB_W), F32, (tq, LANES), blk),
                  ((s, SB_W), F32, (s, LANES), lambda hp, i: (0, hp)),
                  ((s, SB_W), F32, (s, LANES), lambda hp, i: (0, hp))], riders=riders)


def _assemble_dproj(dq_r, dk_r, dv_r, dg_r, dq_s, dk_s, dv_s, da_r, da_s, cos, sin, idx_col, lg_lanes, tm, riders=None):
    s, d = da_r.shape
    width = 2 * RET_QK + 2 * RET_V + 3 * SB_W + 2 * d

    def body(dq_ref, dk_ref, dv_ref, dg_ref, dqs_ref, dks_ref, dvs_ref, dar_ref, das_ref, cos_ref, sin_ref,
             idx_ref, lg_ref, o_ref):
        lane = lax.broadcasted_iota(jnp.int32, (1, LANES), 1)
        first = jnp.bitwise_and(lane, RET_DQK - 1) < (RET_DQK // 2)
        cos, sin = cos_ref[...], sin_ref[...]
        idx = idx_ref[...]
        for src, base, sign, scale in ((dq_ref, 0, 1.0, 1.0), (dk_ref, RET_QK, -1.0, RET_DQK ** -0.5)):
            for g in range(RET_QK // LANES):
                v = src[:, g * LANES:(g + 1) * LANES].astype(F32) * (_decay_scale(lg_ref, idx, g, sign) * scale)
                sw = jnp.where(first, pltpu.roll(v, LANES - RET_DQK // 2, 1), pltpu.roll(v, RET_DQK // 2, 1))
                o_ref[:, base + g * LANES:base + (g + 1) * LANES] = (v * cos - sw * sin).astype(BF16)
        off = 2 * RET_QK
        o_ref[:, off:off + RET_V] = dv_ref[...].astype(BF16)
        off += RET_V
        o_ref[:, off:off + RET_V] = dg_ref[...]
        off += RET_V
        o_ref[:, off:off + SB_W] = (dqs_ref[...] * (SB_DH ** -0.5)).astype(BF16)
        off += SB_W
        o_ref[:, off:off + SB_W] = (dks_ref[...] * LN2).astype(BF16)
        off += SB_W
        o_ref[:, off:off + SB_W] = dvs_ref[...].astype(BF16)
        off += SB_W
        o_ref[:, off:off + d] = dar_ref[...]
        off += d
        o_ref[:, off:off + d] = das_ref[...]

    row = lambda i: (i, 0)
    ins = [(a, (tm, a.shape[1]), row) for a in (dq_r, dk_r, dv_r, dg_r, dq_s, dk_s, dv_s, da_r, da_s, cos, sin, idx_col)]
    ins.append((lg_lanes, (1, RET_QK), lambda i: (0, 0)))
    return _call("assemble_dproj", body, (s // tm,), ins, [((s, width), BF16, (tm, width), row)], riders=riders)


def _in_bwd(dproj, w_in_t, x, dhres, mod, g1, tm, riders=None):
    s, d = x.shape
    width = dproj.shape[1]

    def body(a_ref, w_hbm, x_ref, dh_ref, mod_ref, g_ref, dx_ref, dsh_ref, dsc_ref, dg_ref, w_ref, w_sem):
        _zero_at_start([dsh_ref, dsc_ref, dg_ref])

        @pl.when(pl.program_id(0) == 0)
        def _():
            fetch = pltpu.make_async_copy(w_hbm, w_ref, w_sem)
            fetch.start()
            fetch.wait()

        g1, sc1 = g_ref[...], mod_ref[:, d:2 * d]
        for rows in _pieces(tm):
            dh = _dot(a_ref[rows, :], w_ref[...])
            n1, r1 = _rms(x_ref[rows, :], d)
            dsh_ref[...] += _colsum(dh)
            dsc_ref[...] += _colsum(dh * n1 * g1)
            dg_ref[...] += _colsum(dh * n1 * (1.0 + sc1))
            dx_ref[rows, :] = dh_ref[rows, :] + _rms_bwd(dh * g1 * (1.0 + sc1), n1, r1, d)

    row = lambda i: (i, 0)
    fix = lambda i: (0, 0)
    vec = ((1, d), F32, (1, d), fix)
    return _call("in_bwd", body, (s // tm,),
                 [(dproj, (tm, width), row), (w_in_t, None, pl.ANY), (x, (tm, d), row), (dhres, (tm, d), row),
                  (mod, (1, 6 * d), fix), (g1, (1, d), fix)],
                 [((s, d), F32, (tm, d), row), vec, vec, vec],
                 scratch=[pltpu.VMEM((width, d), w_in_t.dtype), pltpu.SemaphoreType.DMA], riders=riders)


def _adamw(w, g, m, v):
    m = ADAM_B1 * m + (1.0 - ADAM_B1) * g
    v = ADAM_B2 * v + (1.0 - ADAM_B2) * (g * g)
    m_hat = m / (1.0 - ADAM_B1 ** ADAM_STEP)
    v_hat = v / (1.0 - ADAM_B2 ** ADAM_STEP)
    delta = -ADAM_LR * (m_hat / (jnp.sqrt(v_hat) + ADAM_EPS) + ADAM_WD * w)
    return delta, m, v


def _adam_reduce(name, sets, steps):
    n = len(sets)

    def body(*refs):
        for k in range(n):
            p_ref, w_ref, m_ref, v_ref = refs[4 * k:4 * k + 4]
            outs = refs[4 * n + 4 * k:4 * n + 4 * k + 4]
            g = p_ref[0].astype(F32)
            for j in range(1, p_ref.shape[0]):
                g = g + p_ref[j].astype(F32)
            for o_ref, val in zip(outs, (g,) + _adamw(w_ref[...], g, m_ref[...], v_ref[...])):
                o_ref[...] = val

    ins, outs = [], []
    row = lambda i: (i, 0)
    for parts, w, m, v in sets:
        rws, cls = w.shape
        tr = rws // steps
        assert tr * steps == rws and tr % 16 == 0
        ins += [(parts, (parts.shape[0], tr, cls), lambda i: (0, i, 0)), (w, (tr, cls), row), (m, (tr, cls), row),
                (v, (tr, cls), row)]
        outs += [((rws, cls), F32, (tr, cls), row)] * 4
    res = _call(name, body, (steps,), ins, outs)
    return [res[4 * k:4 * k + 4] for k in range(n)]


def _ada_bwd_adam(cs_t, dmod_cols, w, m, v, tr):
    d, nc = w.shape

    def body(c_ref, dm_ref, w_ref, m_ref, v_ref, g_out, d_out, m_out, v_out):
        g = c_ref[0] * dm_ref[0:1, :]
        for r in range(1, N_DEV):
            g = g + c_ref[r] * dm_ref[r:r + 1, :]
        delta, mn, vn = _adamw(w_ref[...], g, m_ref[...], v_ref[...])
        g_out[...] = g
        d_out[...] = delta
        m_out[...] = mn
        v_out[...] = vn

    row = lambda i: (i, 0)
    blk = (tr, nc)
    return _call("ada_bwd_adam", body, (d // tr,),
                 [(cs_t, (N_DEV, tr, 1), lambda i: (0, i, 0)), (dmod_cols, (N_DEV, nc), lambda i: (0, 0)),
                  (w, blk, row), (m, blk, row), (v, blk, row)],
                 [((d, nc), F32, blk, row)] * 4)


def _small_adam(parts, ws, ms, vs):
    n = len(ws)
    widths = [w.shape[1] for w in ws]
    total = parts.shape[1]
    assert sum(widths) + LANES == total

    def body(p_ref, *refs):
        w_refs, m_refs, v_refs = refs[:n], refs[n:2 * n], refs[2 * n:3 * n]
        outs = refs[3 * n:]
        g = p_ref[0:1, :]
        for k in range(1, N_DEV):
            g = g + p_ref[k:k + 1, :]
        off = 0
        for i, width in enumerate(widths):
            gi = g[:, off:off + width]
            delta, mn, vn = _adamw(w_refs[i][...], gi, m_refs[i][...], v_refs[i][...])
            for o_ref, val in zip(outs[4 * i:4 * i + 4], (gi, delta, mn, vn)):
                o_ref[...] = val
            off += width
        outs[4 * n][...] = g[:, off:off + LANES]

    fix = lambda i: (0, 0)
    vec = lambda a: (a, (1, a.shape[1]), fix)
    out_specs = [((1, width), F32, (1, width), fix) for width in widths for _ in range(4)]
    out_specs.append(((1, LANES), F32, (1, LANES), fix))
    res = _call("small_adam", body, (1,),
                [(parts, (N_DEV, total), fix)] + [vec(a) for a in list(ws) + list(ms) + list(vs)], out_specs)
    return [res[4 * i:4 * i + 4] for i in range(n)], res[4 * n]


def kernel(x, c, positions, ada_w, ada_b, pre_mix_g, post_mix_g, pre_ffn_g, post_ffn_g, w_in, ret_gn_g, w_ret_branch, w_sb_branch, w_out, w_ff1, w_ff2, loss_target, m_ada_w, m_ada_b, m_pre_mix_g, m_post_mix_g, m_pre_ffn_g, m_post_ffn_g, m_w_in, m_ret_gn_g, m_w_ret_branch, m_w_sb_branch, m_w_out, m_w_ff1, m_w_ff2, v_ada_w, v_ada_b, v_pre_mix_g, v_post_mix_g, v_pre_ffn_g, v_post_ffn_g, v_w_in, v_ret_gn_g, v_w_ret_branch, v_w_sb_branch, v_w_out, v_w_ff1, v_w_ff2):
    _, s, d = x.shape
    d_ff = w_ff1.shape[2] * N_DEV
    d_in = w_in.shape[2] * N_DEV
    me = 4 * lax.axis_index("x") + 2 * lax.axis_index("y") + lax.axis_index("c")
    x2, tgt = x[0], loss_target[0]

    core = lax.axis_index("c").astype(jnp.int32).reshape(1)
    bf = lambda w: w[0].astype(BF16)

    w_in_t, m_in_t, v_in_t = (jnp.swapaxes(a[0], 0, 1) for a in (w_in, m_w_in, v_w_in))

    c_all, g_in = _gather_first(c, w_in_t.astype(BF16))
    c_all = c_all.reshape(N_DEV, d)

    n_ada = ada_w.shape[2]
    ada_b_cols = lax.dynamic_slice(ada_b, (0, me * n_ada), (1, n_ada))
    cs_all, mod_cols = _ada_fwd(c_all, ada_w[0], ada_b_cols)
    mod_all = _exchange("gather_mod", [mod_cols], ["gather"])[0]
    mod = lax.dynamic_index_in_dim(mod_all, me, axis=1, keepdims=False).reshape(1, 6 * d)

    tm = min(256, s)
    h = _pre_norm(x2, pre_mix_g, mod, 2 * tm)[0]
    wt_in = g_in.reshape(d_in, d)
    bf_t = lambda w: jnp.swapaxes(w[0], 0, 1).astype(BF16)
    small_w = [bf(w_ret_branch), bf_t(w_sb_branch), bf(w_out)]
    proj, *small_w = _matmul("in_proj", h, wt_in, "nt", s, 512, BF16, riders=(small_w, ["gather_chip"] * 3))
    pos_col = positions.reshape(s, 1).astype(F32)
    freqs = ROPE_BASE ** (-jnp.arange(0, RET_DQK, 2, dtype=F32) / RET_DQK)
    inv_freq = jnp.tile(freqs, LANES // (RET_DQK // 2)).reshape(1, LANES)
    log_gamma_np = np.log1p(-(2.0 ** (-5.0 - np.arange(HEADS))))
    log_gamma = jnp.asarray(log_gamma_np, F32)
    lg_lanes = jnp.asarray(np.repeat(log_gamma_np, RET_DQK).reshape(1, RET_QK), F32)
    idx_col = (jnp.arange(s, dtype=F32) - (s // 2)).reshape(s, 1)
    qk_rot, q_sb, cos_t, sin_t = _prep(proj, pos_col, idx_col, inv_freq, lg_lanes, 2 * tm)
    tq, tk = min(256, s), min(128, s)
    tq_sb = min(SB_TQ, s)
    sb, sb_weights, *big_w = _sb_fwd(q_sb, proj, tq_sb, tk, riders=([bf(w_ff2), bf_t(w_ff1)], ["gather_chip"] * 2))
    ret, retg, g_ret, g_sb, g_out, g_ff2, g_ff1 = _ret_fwd(qk_rot, proj, ret_gn_g, log_gamma, tq,
                                                           riders=(small_w + big_w, ["forward"] * 5))
    wf_ret = g_ret.reshape(RET_V, d)
    wt_sb = g_sb.reshape(d, SB_W)
    wf_out = g_out.reshape(d, d)
    wt_ff1 = g_ff1.reshape(d_ff, d)
    wf_ff2 = g_ff2.reshape(d_ff, d)
    mixed, r_bf, s_bf, y, hres, h2 = _merge_out(retg, sb, wf_ret, wt_sb, wf_out, proj, x2, mod, post_mix_g, pre_ffn_g, tm)
    u, act = _ff1(h2, wt_ff1, s, 512)
    dout, df, loss_sum, d_gt2, d_gp2 = _ff2_loss(act, wf_ff2, hres, tgt, mod, post_ffn_g, tm)

    du, gw_ff2, gw_ff1 = _ffn_bwd(df, wf_ff2, u, act, h2, d_ff // N_DEV)
    gw_ff2 = gw_ff2.reshape(N_DEV, d_ff // N_DEV, d)
    dhres, dy, d_sh2, d_sc2, d_g2, d_gt1, d_gp1, gw_out, t_ff1, t_ff2 = _ff1_bwd(
        du, wt_ff1, hres, dout, y, mixed, mod, pre_ffn_g, post_mix_g, tm, riders=([gw_ff1, gw_ff2], ["pair"] * 2))
    gw_out = gw_out.reshape(N_DEV, d // N_DEV, d)
    s_ff1, s_ff2 = _pair_sum("pair_sum_ff", [(gw_ff1, t_ff1), (gw_ff2, t_ff2)], core)
    d_r, d_s, da_r, da_s = _out_bwd(dy, wf_out, proj, r_bf, s_bf, 2 * tm, min(512, d))
    dsb, gw_ret, gw_sb, dg_r, dret, d_gn, p_out = _branch_bwd(d_r, d_s, retg, sb, wf_ret, wt_sb, ret, proj, ret_gn_g,
                                                              riders=([gw_out], ["scatter"]))
    gw_ret = gw_ret.reshape(N_DEV, RET_V // N_DEV, d)
    dq_s, dk_s, dv_s, p_ff1, p_ff2 = _sb_bwd(q_sb, proj, sb_weights, dsb, tq_sb, tk,
                                             riders=([s_ff1, s_ff2], ["chip_scatter"] * 2))
    dq_r, dk_r, dv_r, p_sb = _ret_bwd(qk_rot, proj, dret, log_gamma, tq, riders=([gw_sb], ["scatter"]))
    dproj = _assemble_dproj(dq_r, dk_r, dv_r, dg_r, dq_s, dk_s, dv_s, da_r, da_s, cos_t, sin_t, idx_col, lg_lanes, tm)[0]
    gw_in, p_ret = _matmul("grad_w_in", dproj, h, "tn", 512, d, BF16, riders=([gw_ret], ["scatter"]))
    gw_in = gw_in.reshape(N_DEV, d_in // N_DEV, d)
    t_in = _exchange("pair_in", [gw_in], ["pair"])[0]
    s_in = _pair_sum("pair_sum_in", [(gw_in, t_in)], core)[0]
    grad_x, d_sh1, d_sc1, d_g1, p_in = _in_bwd(dproj, wt_in, x2, dhres, mod, pre_mix_g, tm,
                                               riders=([s_in], ["chip_scatter"]))
    loss_lanes = jnp.pad(loss_sum, ((0, 0), (0, LANES - 1)))
    small = jnp.concatenate([d_sh1, d_sc1, d_gt1, d_sh2, d_sc2, d_gt2, d_g1, d_gp1, d_g2, d_gp2, d_gn, loss_lanes], axis=1)
    small_all = _exchange("gather_small", [small], ["gather"])[0].reshape(N_DEV, small.shape[1])
    parts = [p_in, p_ret, p_sb, p_out, p_ff1, p_ff2]

    res = {}
    names = ["w_ret_branch", "w_sb_branch", "w_out", "w_ff1", "w_ff2"]
    ws = [w_ret_branch, w_sb_branch, w_out, w_ff1, w_ff2]
    ms = [m_w_ret_branch, m_w_sb_branch, m_w_out, m_w_ff1, m_w_ff2]
    vs = [v_w_ret_branch, v_w_sb_branch, v_w_out, v_w_ff1, v_w_ff2]
    sets = [(parts[0], w_in_t, m_in_t, v_in_t)] + [(p, w[0], m[0], v[0]) for p, w, m, v in zip(parts[1:], ws, ms, vs)]
    updated = _adam_reduce("adam_big", sets, 4)
    res["w_in"] = [jnp.swapaxes(o, 0, 1)[None] for o in updated[0]]
    for nm, outs4 in zip(names, updated[1:]):
        res[nm] = [o[None] for o in outs4]
    dmod_cols = lax.dynamic_slice(small_all, (0, me * n_ada), (N_DEV, n_ada))
    res["ada_w"] = [o[None] for o in _ada_bwd_adam(cs_all.reshape(N_DEV, d, 1), dmod_cols, ada_w[0], m_ada_w[0], v_ada_w[0], tm)]
    vec_names = ["ada_b", "pre_mix_g", "post_mix_g", "pre_ffn_g", "post_ffn_g", "ret_gn_g"]
    vec_res, loss_lanes = _small_adam(small_all,
                                      [ada_b, pre_mix_g, post_mix_g, pre_ffn_g, post_ffn_g, ret_gn_g],
                                      [m_ada_b, m_pre_mix_g, m_post_mix_g, m_pre_ffn_g, m_post_ffn_g, m_ret_gn_g],
                                      [v_ada_b, v_pre_mix_g, v_post_mix_g, v_pre_ffn_g, v_post_ffn_g, v_ret_gn_g])
    res.update(zip(vec_names, vec_res))
    loss = (0.5 / d) * loss_lanes[0, 0]
    order = ["ada_w", "ada_b", "pre_mix_g", "post_mix_g", "pre_ffn_g", "post_ffn_g", "w_in", "ret_gn_g",
             "w_ret_branch", "w_sb_branch", "w_out", "w_ff1", "w_ff2"]
    outs = [loss, grad_x[None]]
    for k in range(4):
        outs += [res[nm][k] for nm in order]
    return tuple(outs)
```
